```python
import jax, jax.numpy as jnp
from jax import lax
import numpy as np


D_MODEL = 1024
BATCH = 8
SEQ = 8192
DEPTH = 2

N_MIXERS = 2
N_LAYERS_A = (DEPTH + 1) // 2
N_LAYERS_B = DEPTH // 2
MLA_HEADS = 8
QK_NOPE = 128
QK_ROPE = 64
V_DIM = 128
Q_LORA = 384
KV_LORA = 256
ROPE_THETA = 10000.0
SWA_HEADS = 16
SWA_KV_HEADS = 4
SWA_HEAD_DIM = 64
WINDOW = 128
D_FF = 4 * D_MODEL
BLOCK_Q = 128
EPS = 1e-6

kernel_name = 'hybrid_mla_swa_sink_alibi_adaln'


def rmsnorm(x, g):
    xf = x.astype(jnp.float32)
    y = xf * lax.rsqrt(jnp.mean(xf * xf, axis=-1, keepdims=True) + EPS)
    return (y * g.astype(jnp.float32)).astype(x.dtype)


def modulate(x, g, shift, scale):
    return rmsnorm(x, g) * (1.0 + scale[:, None, :]) + shift[:, None, :]


def rope(x, positions):
    half = QK_ROPE // 2
    inv_freq = ROPE_THETA ** (-jnp.arange(half, dtype=jnp.float32) / half)
    ang = positions.astype(jnp.float32)[..., None] * inv_freq
    shape = ang.shape[:2] + (1,) * (x.ndim - 3) + (half,)
    cos = jnp.cos(ang).reshape(shape)
    sin = jnp.sin(ang).reshape(shape)
    xf = x.astype(jnp.float32)
    x1, x2 = xf[..., :half], xf[..., half:]
    out = jnp.concatenate([x1 * cos - x2 * sin, x1 * sin + x2 * cos], axis=-1)
    return out.astype(x.dtype)


def alibi_slopes(n_heads):
    return jnp.asarray(2.0 ** (-8.0 * np.arange(1, n_heads + 1) / n_heads), dtype=jnp.float32)


def mla(h, positions, w_dq, g_q, w_uq, w_dkv, g_kv, w_ukv, w_o):
    B, S, _ = h.shape
    H = MLA_HEADS
    cq = rmsnorm(h @ w_dq, g_q)
    q = (cq @ w_uq).reshape(B, S, H, QK_NOPE + QK_ROPE)
    q_nope = q[..., :QK_NOPE]
    q_rope = rope(q[..., QK_NOPE:], positions)
    ckv_kr = h @ w_dkv
    ckv = rmsnorm(ckv_kr[..., :KV_LORA], g_kv)
    k_rope = rope(ckv_kr[..., KV_LORA:], positions)
    kv = (ckv @ w_ukv).reshape(B, S, H, QK_NOPE + V_DIM)
    k_nope, v = kv[..., :QK_NOPE], kv[..., QK_NOPE:]
    scale = (QK_NOPE + QK_ROPE) ** -0.5
    n_blk = S // BLOCK_Q
    qn_blk = q_nope.reshape(B, n_blk, BLOCK_Q, H, QK_NOPE).transpose(1, 0, 2, 3, 4)
    qr_blk = q_rope.reshape(B, n_blk, BLOCK_Q, H, QK_ROPE).transpose(1, 0, 2, 3, 4)
    key_idx = jnp.arange(S)

    def one_block(args):
        i, qn, qr = args
        s = (jnp.einsum('bqhd,bkhd->bhqk', qn, k_nope)
             + jnp.einsum('bqhr,bkr->bhqk', qr, k_rope)).astype(jnp.float32) * scale
        q_idx = i * BLOCK_Q + jnp.arange(BLOCK_Q)
        causal = key_idx[None, :] <= q_idx[:, None]
        s = jnp.where(causal[None, None], s, -jnp.inf)
        p = jax.nn.softmax(s, axis=-1).astype(v.dtype)
        return jnp.einsum('bhqk,bkhd->bqhd', p, v)

    o = lax.map(one_block, (jnp.arange(n_blk), qn_blk, qr_blk))
    o = o.transpose(1, 0, 2, 3, 4).reshape(B, S, H * V_DIM)
    return o @ w_o


def swa(h, w_qkv, b_qkv, sinks, w_o, b_o):
    B, S, _ = h.shape
    Hq, Hk, Dh, W = SWA_HEADS, SWA_KV_HEADS, SWA_HEAD_DIM, WINDOW
    G = Hq // Hk
    qkv = h @ w_qkv + b_qkv
    q = qkv[..., :Hq * Dh]
    k = qkv[..., Hq * Dh:(Hq + Hk) * Dh].reshape(B, S, Hk, Dh)
    v = qkv[..., (Hq + Hk) * Dh:].reshape(B, S, Hk, Dh)
    n_blk = S // W
    qb = q.reshape(B, n_blk, W, Hk, G, Dh)

    def band(t):
        tb = t.reshape(B, n_blk, W, Hk, Dh)
        prev = jnp.pad(tb[:, :-1], ((0, 0), (1, 0), (0, 0), (0, 0), (0, 0)))
        return jnp.concatenate([prev, tb], axis=2)

    kb, vb = band(k), band(v)
    s = jnp.einsum('bnqkgd,bnjkd->bnkgqj', qb, kb).astype(jnp.float32) * (Dh ** -0.5)
    dist = W + jnp.arange(W)[:, None] - jnp.arange(2 * W)[None, :]
    in_window = (dist >= 0) & (dist < W)
    real_key = (jnp.arange(n_blk)[:, None] > 0) | (jnp.arange(2 * W)[None, :] >= W)
    mask = in_window[None] & real_key[:, None, :]
    slopes = alibi_slopes(Hq).reshape(Hk, G)
    s = s - slopes[:, :, None, None] * dist.astype(jnp.float32)
    s = jnp.where(mask[None, :, None, None], s, -jnp.inf)
    sink = sinks.astype(jnp.float32).reshape(Hk, G)[:, :, None]
    m = jnp.maximum(s.max(axis=-1), sink)
    p = jnp.exp(s - m[..., None])
    denom = p.sum(axis=-1) + jnp.exp(sink - m)
    p = (p / denom[..., None]).astype(vb.dtype)
    o = jnp.einsum('bnkgqj,bnjkd->bnqkgd', p, vb).reshape(B, S, Hq * Dh)
    return o @ w_o + b_o


def _fwd_setup_inputs(seed: int = 0) -> dict:
    key = jax.random.key(seed)
    ks = jax.random.split(key, 24)
    f32 = jnp.float32

    def w(k, shape, fan_in, gain=1.0):
        return jax.random.normal(k, shape, f32) * (gain * fan_in ** -0.5)

    def g(k, shape):
        return 1.0 + 0.05 * jax.random.normal(k, shape, f32)

    A, Bn = N_LAYERS_A, N_LAYERS_B
    x = jax.random.normal(ks[0], (BATCH, SEQ, D_MODEL), f32)
    c = jax.random.normal(ks[1], (BATCH, D_MODEL), f32)
    positions = (jnp.arange(SEQ, dtype=jnp.int32)[None, :]
                 + jax.random.randint(ks[2], (BATCH, 1), 0, 1024, dtype=jnp.int32))
    return {
        'x': x,
        'c': c,
        'positions': positions,
        'w_ada': w(ks[3], (DEPTH, D_MODEL, 6 * D_MODEL), D_MODEL, 0.5),
        'b_ada': 0.02 * jax.random.normal(ks[4], (DEPTH, 6 * D_MODEL), f32),
        'g_mix': g(ks[5], (DEPTH, D_MODEL)),
        'g_mlp': g(ks[6], (DEPTH, D_MODEL)),
        'mla_w_dq': w(ks[7], (A, D_MODEL, Q_LORA), D_MODEL),
        'mla_g_q': g(ks[8], (A, Q_LORA)),
        'mla_w_uq': w(ks[9], (A, Q_LORA, MLA_HEADS * (QK_NOPE + QK_ROPE)), Q_LORA),
        'mla_w_dkv': w(ks[10], (A, D_MODEL, KV_LORA + QK_ROPE), D_MODEL),
        'mla_g_kv': g(ks[11], (A, KV_LORA)),
        'mla_w_ukv': w(ks[12], (A, KV_LORA, MLA_HEADS * (QK_NOPE + V_DIM)), KV_LORA),
        'mla_w_o': w(ks[13], (A, MLA_HEADS * V_DIM, D_MODEL), MLA_HEADS * V_DIM),
        'swa_w_qkv': w(ks[14], (Bn, D_MODEL, (SWA_HEADS + 2 * SWA_KV_HEADS) * SWA_HEAD_DIM), D_MODEL),
        'swa_b_qkv': 0.02 * jax.random.normal(ks[15], (Bn, (SWA_HEADS + 2 * SWA_KV_HEADS) * SWA_HEAD_DIM), f32),
        'swa_sinks': 0.5 * jax.random.normal(ks[16], (Bn, SWA_HEADS), f32),
        'swa_w_o': w(ks[17], (Bn, SWA_HEADS * SWA_HEAD_DIM, D_MODEL), SWA_HEADS * SWA_HEAD_DIM),
        'swa_b_o': 0.02 * jax.random.normal(ks[18], (Bn, D_MODEL), f32),
        'w_ff1': w(ks[19], (DEPTH, D_MODEL, D_FF), D_MODEL),
        'w_ff2': w(ks[20], (DEPTH, D_FF, D_MODEL), D_FF),
        'g_final': g(ks[21], (D_MODEL,)),
    }


def _fwd_reference(x, c, positions, w_ada, b_ada, g_mix, g_mlp,
              mla_w_dq, mla_g_q, mla_w_uq, mla_w_dkv, mla_g_kv, mla_w_ukv, mla_w_o,
              swa_w_qkv, swa_b_qkv, swa_sinks, swa_w_o, swa_b_o,
              w_ff1, w_ff2, g_final):
    cond = jax.nn.silu(c)
    for i in range(DEPTH):
        mod = cond @ w_ada[i] + b_ada[i]
        sh1, sc1, gt1, sh2, sc2, gt2 = jnp.split(mod, 6, axis=-1)
        h = modulate(x, g_mix[i], sh1, sc1)
        j = i // N_MIXERS
        if i % N_MIXERS == 0:
            y = mla(h, positions, mla_w_dq[j], mla_g_q[j], mla_w_uq[j], mla_w_dkv[j],
                    mla_g_kv[j], mla_w_ukv[j], mla_w_o[j])
        else:
            y = swa(h, swa_w_qkv[j], swa_b_qkv[j], swa_sinks[j], swa_w_o[j], swa_b_o[j])
        x = x + gt1[:, None, :] * y
        h = modulate(x, g_mlp[i], sh2, sc2)
        y = jnp.square(jax.nn.relu(h @ w_ff1[i])) @ w_ff2[i]
        x = x + gt2[:, None, :] * y
    return rmsnorm(x, g_final)


import jax as _jax
import jax.numpy as _jnp

TWIN_FORMAT = 'train_step'
FWD_PARAMS = ['x', 'c', 'positions', 'w_ada', 'b_ada', 'g_mix', 'g_mlp', 'mla_w_dq', 'mla_g_q', 'mla_w_uq', 'mla_w_dkv', 'mla_g_kv', 'mla_w_ukv', 'mla_w_o', 'swa_w_qkv', 'swa_b_qkv', 'swa_sinks', 'swa_w_o', 'swa_b_o', 'w_ff1', 'w_ff2', 'g_final']
TWIN_WEIGHTS = ['w_ada', 'b_ada', 'g_mix', 'g_mlp', 'mla_w_dq', 'mla_g_q', 'mla_w_uq', 'mla_w_dkv', 'mla_g_kv', 'mla_w_ukv', 'mla_w_o', 'swa_w_qkv', 'swa_b_qkv', 'swa_sinks', 'swa_w_o', 'swa_b_o', 'w_ff1', 'w_ff2', 'g_final']
TWIN_DIFF_INPUT = 'x'
TWIN_INPUTS = ['x', 'c', 'positions', 'w_ada', 'b_ada', 'g_mix', 'g_mlp', 'mla_w_dq', 'mla_g_q', 'mla_w_uq', 'mla_w_dkv', 'mla_g_kv', 'mla_w_ukv', 'mla_w_o', 'swa_w_qkv', 'swa_b_qkv', 'swa_sinks', 'swa_w_o', 'swa_b_o', 'w_ff1', 'w_ff2', 'g_final', 'loss_target', 'm_w_ada', 'm_b_ada', 'm_g_mix', 'm_g_mlp', 'm_mla_w_dq', 'm_mla_g_q', 'm_mla_w_uq', 'm_mla_w_dkv', 'm_mla_g_kv', 'm_mla_w_ukv', 'm_mla_w_o', 'm_swa_w_qkv', 'm_swa_b_qkv', 'm_swa_sinks', 'm_swa_w_o', 'm_swa_b_o', 'm_w_ff1', 'm_w_ff2', 'm_g_final', 'v_w_ada', 'v_b_ada', 'v_g_mix', 'v_g_mlp', 'v_mla_w_dq', 'v_mla_g_q', 'v_mla_w_uq', 'v_mla_w_dkv', 'v_mla_g_kv', 'v_mla_w_ukv', 'v_mla_w_o', 'v_swa_w_qkv', 'v_swa_b_qkv', 'v_swa_sinks', 'v_swa_w_o', 'v_swa_b_o', 'v_w_ff1', 'v_w_ff2', 'v_g_final']
TWIN_OUTPUTS = ['loss', 'grad_x', 'grad_w_ada', 'grad_b_ada', 'grad_g_mix', 'grad_g_mlp', 'grad_mla_w_dq', 'grad_mla_g_q', 'grad_mla_w_uq', 'grad_mla_w_dkv', 'grad_mla_g_kv', 'grad_mla_w_ukv', 'grad_mla_w_o', 'grad_swa_w_qkv', 'grad_swa_b_qkv', 'grad_swa_sinks', 'grad_swa_w_o', 'grad_swa_b_o', 'grad_w_ff1', 'grad_w_ff2', 'grad_g_final', 'delta_w_ada', 'delta_b_ada', 'delta_g_mix', 'delta_g_mlp', 'delta_mla_w_dq', 'delta_mla_g_q', 'delta_mla_w_uq', 'delta_mla_w_dkv', 'delta_mla_g_kv', 'delta_mla_w_ukv', 'delta_mla_w_o', 'delta_swa_w_qkv', 'delta_swa_b_qkv', 'delta_swa_sinks', 'delta_swa_w_o', 'delta_swa_b_o', 'delta_w_ff1', 'delta_w_ff2', 'delta_g_final', 'new_m_w_ada', 'new_m_b_ada', 'new_m_g_mix', 'new_m_g_mlp', 'new_m_mla_w_dq', 'new_m_mla_g_q', 'new_m_mla_w_uq', 'new_m_mla_w_dkv', 'new_m_mla_g_kv', 'new_m_mla_w_ukv', 'new_m_mla_w_o', 'new_m_swa_w_qkv', 'new_m_swa_b_qkv', 'new_m_swa_sinks', 'new_m_swa_w_o', 'new_m_swa_b_o', 'new_m_w_ff1', 'new_m_w_ff2', 'new_m_g_final', 'new_v_w_ada', 'new_v_b_ada', 'new_v_g_mix', 'new_v_g_mlp', 'new_v_mla_w_dq', 'new_v_mla_g_q', 'new_v_mla_w_uq', 'new_v_mla_w_dkv', 'new_v_mla_g_kv', 'new_v_mla_w_ukv', 'new_v_mla_w_o', 'new_v_swa_w_qkv', 'new_v_swa_b_qkv', 'new_v_swa_sinks', 'new_v_swa_w_o', 'new_v_swa_b_o', 'new_v_w_ff1', 'new_v_w_ff2', 'new_v_g_final']
TWIN_LEAF_KINDS = {'loss': 'loss', 'grad_x': 'grad_x', 'grad_w_ada': 'grad_w', 'grad_b_ada': 'grad_w', 'grad_g_mix': 'grad_w', 'grad_g_mlp': 'grad_w', 'grad_mla_w_dq': 'grad_w', 'grad_mla_g_q': 'grad_w', 'grad_mla_w_uq': 'grad_w', 'grad_mla_w_dkv': 'grad_w', 'grad_mla_g_kv': 'grad_w', 'grad_mla_w_ukv': 'grad_w', 'grad_mla_w_o': 'grad_w', 'grad_swa_w_qkv': 'grad_w', 'grad_swa_b_qkv': 'grad_w', 'grad_swa_sinks': 'grad_w', 'grad_swa_w_o': 'grad_w', 'grad_swa_b_o': 'grad_w', 'grad_w_ff1': 'grad_w', 'grad_w_ff2': 'grad_w', 'grad_g_final': 'grad_w', 'delta_w_ada': 'delta_w', 'delta_b_ada': 'delta_w', 'delta_g_mix': 'delta_w', 'delta_g_mlp': 'delta_w', 'delta_mla_w_dq': 'delta_w', 'delta_mla_g_q': 'delta_w', 'delta_mla_w_uq': 'delta_w', 'delta_mla_w_dkv': 'delta_w', 'delta_mla_g_kv': 'delta_w', 'delta_mla_w_ukv': 'delta_w', 'delta_mla_w_o': 'delta_w', 'delta_swa_w_qkv': 'delta_w', 'delta_swa_b_qkv': 'delta_w', 'delta_swa_sinks': 'delta_w', 'delta_swa_w_o': 'delta_w', 'delta_swa_b_o': 'delta_w', 'delta_w_ff1': 'delta_w', 'delta_w_ff2': 'delta_w', 'delta_g_final': 'delta_w', 'new_m_w_ada': 'new_m', 'new_m_b_ada': 'new_m', 'new_m_g_mix': 'new_m', 'new_m_g_mlp': 'new_m', 'new_m_mla_w_dq': 'new_m', 'new_m_mla_g_q': 'new_m', 'new_m_mla_w_uq': 'new_m', 'new_m_mla_w_dkv': 'new_m', 'new_m_mla_g_kv': 'new_m', 'new_m_mla_w_ukv': 'new_m', 'new_m_mla_w_o': 'new_m', 'new_m_swa_w_qkv': 'new_m', 'new_m_swa_b_qkv': 'new_m', 'new_m_swa_sinks': 'new_m', 'new_m_swa_w_o': 'new_m', 'new_m_swa_b_o': 'new_m', 'new_m_w_ff1': 'new_m', 'new_m_w_ff2': 'new_m', 'new_m_g_final': 'new_m', 'new_v_w_ada': 'new_v', 'new_v_b_ada': 'new_v', 'new_v_g_mix': 'new_v', 'new_v_g_mlp': 'new_v', 'new_v_mla_w_dq': 'new_v', 'new_v_mla_g_q': 'new_v', 'new_v_mla_w_uq': 'new_v', 'new_v_mla_w_dkv': 'new_v', 'new_v_mla_g_kv': 'new_v', 'new_v_mla_w_ukv': 'new_v', 'new_v_mla_w_o': 'new_v', 'new_v_swa_w_qkv': 'new_v', 'new_v_swa_b_qkv': 'new_v', 'new_v_swa_sinks': 'new_v', 'new_v_swa_w_o': 'new_v', 'new_v_swa_b_o': 'new_v', 'new_v_w_ff1': 'new_v', 'new_v_w_ff2': 'new_v', 'new_v_g_final': 'new_v'}


def _forward(args):
    return _fwd_reference(*[args[k] for k in FWD_PARAMS])


def _output_shape():
    def fwd():
        inp = _fwd_setup_inputs(0)
        return _fwd_reference(*[inp[k] for k in FWD_PARAMS])
    out = _jax.eval_shape(fwd)
    return out.shape, out.dtype

N_MICROBATCH = 1
ADAM_LR = 0.001
ADAM_B1 = 0.9
ADAM_B2 = 0.999
ADAM_EPS = 1e-08
ADAM_WD = 0.01
ADAM_STEP = 10
PER_EXAMPLE_BATCH_AXIS = {'x': 0, 'c': 0, 'positions': 0, 'loss_target': 0}
SHARED_INPUTS = []
_WEIGHT_DTYPES = {'w_ada': _jnp.float32, 'b_ada': _jnp.float32, 'g_mix': _jnp.float32, 'g_mlp': _jnp.float32, 'mla_w_dq': _jnp.float32, 'mla_g_q': _jnp.float32, 'mla_w_uq': _jnp.float32, 'mla_w_dkv': _jnp.float32, 'mla_g_kv': _jnp.float32, 'mla_w_ukv': _jnp.float32, 'mla_w_o': _jnp.float32, 'swa_w_qkv': _jnp.float32, 'swa_b_qkv': _jnp.float32, 'swa_sinks': _jnp.float32, 'swa_w_o': _jnp.float32, 'swa_b_o': _jnp.float32, 'w_ff1': _jnp.float32, 'w_ff2': _jnp.float32, 'g_final': _jnp.float32}
MOMENT_SCALE = {'w_ada': 2.212940e-01, 'b_ada': 4.846995e-01, 'g_mix': 3.452190e-02, 'g_mlp': 1.121826e-01, 'mla_w_dq': 2.314109e-02, 'mla_g_q': 2.235065e-02, 'mla_w_uq': 1.157019e-02, 'mla_w_dkv': 5.640254e-02, 'mla_g_kv': 6.724547e-02, 'mla_w_ukv': 2.326180e-02, 'mla_w_o': 3.469865e-02, 'swa_w_qkv': 4.481638e-02, 'swa_b_qkv': 1.304743e-01, 'swa_sinks': 3.829427e-02, 'swa_w_o': 5.051036e-02, 'swa_b_o': 1.516333e-01, 'w_ff1': 6.134633e-02, 'w_ff2': 1.562666e-01, 'g_final': 6.463272e+01}


def _to_microbatches(a, axis):
    t = _jnp.moveaxis(a, axis, 0)
    t = t.reshape((N_MICROBATCH, t.shape[0] // N_MICROBATCH) + t.shape[1:])
    return _jnp.moveaxis(t, 1, axis + 1)


def setup_inputs(seed: int = 0) -> dict:
    inp = _fwd_setup_inputs(seed)
    key = _jax.random.fold_in(_jax.random.key(seed), 7919)
    shape, _ = _output_shape()
    out = dict(inp)
    out["loss_target"] = _jax.random.normal(_jax.random.fold_in(key, 0), shape, _jnp.float32)
    for i, name in enumerate(TWIN_WEIGHTS):
        w = inp[name].astype(_jnp.float32)
        if MOMENT_SCALE is None:
            s = _jnp.sqrt(_jnp.mean(_jnp.square(w)) + 1e-30)
        else:
            s = MOMENT_SCALE[name]
        km, kv = _jax.random.split(_jax.random.fold_in(key, i + 1))
        out[name] = w
        out["m_" + name] = s * _jax.random.normal(km, w.shape, _jnp.float32)
        out["v_" + name] = (s * s) * _jax.random.uniform(kv, w.shape, _jnp.float32, 0.5, 1.5)
    if N_MICROBATCH > 1:
        for name, axis in PER_EXAMPLE_BATCH_AXIS.items():
            out[name] = _to_microbatches(out[name], axis)
    return {'x': out['x'], 'c': out['c'], 'positions': out['positions'], 'w_ada': out['w_ada'], 'b_ada': out['b_ada'], 'g_mix': out['g_mix'], 'g_mlp': out['g_mlp'], 'mla_w_dq': out['mla_w_dq'], 'mla_g_q': out['mla_g_q'], 'mla_w_uq': out['mla_w_uq'], 'mla_w_dkv': out['mla_w_dkv'], 'mla_g_kv': out['mla_g_kv'], 'mla_w_ukv': out['mla_w_ukv'], 'mla_w_o': out['mla_w_o'], 'swa_w_qkv': out['swa_w_qkv'], 'swa_b_qkv': out['swa_b_qkv'], 'swa_sinks': out['swa_sinks'], 'swa_w_o': out['swa_w_o'], 'swa_b_o': out['swa_b_o'], 'w_ff1': out['w_ff1'], 'w_ff2': out['w_ff2'], 'g_final': out['g_final'], 'loss_target': out['loss_target'], 'm_w_ada': out['m_w_ada'], 'm_b_ada': out['m_b_ada'], 'm_g_mix': out['m_g_mix'], 'm_g_mlp': out['m_g_mlp'], 'm_mla_w_dq': out['m_mla_w_dq'], 'm_mla_g_q': out['m_mla_g_q'], 'm_mla_w_uq': out['m_mla_w_uq'], 'm_mla_w_dkv': out['m_mla_w_dkv'], 'm_mla_g_kv': out['m_mla_g_kv'], 'm_mla_w_ukv': out['m_mla_w_ukv'], 'm_mla_w_o': out['m_mla_w_o'], 'm_swa_w_qkv': out['m_swa_w_qkv'], 'm_swa_b_qkv': out['m_swa_b_qkv'], 'm_swa_sinks': out['m_swa_sinks'], 'm_swa_w_o': out['m_swa_w_o'], 'm_swa_b_o': out['m_swa_b_o'], 'm_w_ff1': out['m_w_ff1'], 'm_w_ff2': out['m_w_ff2'], 'm_g_final': out['m_g_final'], 'v_w_ada': out['v_w_ada'], 'v_b_ada': out['v_b_ada'], 'v_g_mix': out['v_g_mix'], 'v_g_mlp': out['v_g_mlp'], 'v_mla_w_dq': out['v_mla_w_dq'], 'v_mla_g_q': out['v_mla_g_q'], 'v_mla_w_uq': out['v_mla_w_uq'], 'v_mla_w_dkv': out['v_mla_w_dkv'], 'v_mla_g_kv': out['v_mla_g_kv'], 'v_mla_w_ukv': out['v_mla_w_ukv'], 'v_mla_w_o': out['v_mla_w_o'], 'v_swa_w_qkv': out['v_swa_w_qkv'], 'v_swa_b_qkv': out['v_swa_b_qkv'], 'v_swa_sinks': out['v_swa_sinks'], 'v_swa_w_o': out['v_swa_w_o'], 'v_swa_b_o': out['v_swa_b_o'], 'v_w_ff1': out['v_w_ff1'], 'v_w_ff2': out['v_w_ff2'], 'v_g_final': out['v_g_final']}


def _loss(weights, diff, rest, loss_target):
    with _jax.named_scope("forward"):
        args = {**rest, TWIN_DIFF_INPUT: diff, **{k: w.astype(_WEIGHT_DTYPES[k]) for k, w in weights.items()}}
        y = _forward(args)
    with _jax.named_scope("loss_head"):
        err = _jnp.square(y.astype(_jnp.float32) - loss_target)
        return 0.5 * _jnp.sum(_jnp.mean(err, axis=-1)) if err.ndim else 0.5 * err


def _adamw(w, g, m, v):
    m = ADAM_B1 * m + (1.0 - ADAM_B1) * g
    v = ADAM_B2 * v + (1.0 - ADAM_B2) * _jnp.square(g)
    m_hat = m / (1.0 - ADAM_B1 ** ADAM_STEP)
    v_hat = v / (1.0 - ADAM_B2 ** ADAM_STEP)
    delta = -ADAM_LR * (m_hat / (_jnp.sqrt(v_hat) + ADAM_EPS) + ADAM_WD * w)
    return delta, m, v


def reference(x, c, positions, w_ada, b_ada, g_mix, g_mlp, mla_w_dq, mla_g_q, mla_w_uq, mla_w_dkv, mla_g_kv, mla_w_ukv, mla_w_o, swa_w_qkv, swa_b_qkv, swa_sinks, swa_w_o, swa_b_o, w_ff1, w_ff2, g_final, loss_target, m_w_ada, m_b_ada, m_g_mix, m_g_mlp, m_mla_w_dq, m_mla_g_q, m_mla_w_uq, m_mla_w_dkv, m_mla_g_kv, m_mla_w_ukv, m_mla_w_o, m_swa_w_qkv, m_swa_b_qkv, m_swa_sinks, m_swa_w_o, m_swa_b_o, m_w_ff1, m_w_ff2, m_g_final, v_w_ada, v_b_ada, v_g_mix, v_g_mlp, v_mla_w_dq, v_mla_g_q, v_mla_w_uq, v_mla_w_dkv, v_mla_g_kv, v_mla_w_ukv, v_mla_w_o, v_swa_w_qkv, v_swa_b_qkv, v_swa_sinks, v_swa_w_o, v_swa_b_o, v_w_ff1, v_w_ff2, v_g_final):
    given = dict(x=x, c=c, positions=positions, w_ada=w_ada, b_ada=b_ada, g_mix=g_mix, g_mlp=g_mlp, mla_w_dq=mla_w_dq, mla_g_q=mla_g_q, mla_w_uq=mla_w_uq, mla_w_dkv=mla_w_dkv, mla_g_kv=mla_g_kv, mla_w_ukv=mla_w_ukv, mla_w_o=mla_w_o, swa_w_qkv=swa_w_qkv, swa_b_qkv=swa_b_qkv, swa_sinks=swa_sinks, swa_w_o=swa_w_o, swa_b_o=swa_b_o, w_ff1=w_ff1, w_ff2=w_ff2, g_final=g_final, loss_target=loss_target, m_w_ada=m_w_ada, m_b_ada=m_b_ada, m_g_mix=m_g_mix, m_g_mlp=m_g_mlp, m_mla_w_dq=m_mla_w_dq, m_mla_g_q=m_mla_g_q, m_mla_w_uq=m_mla_w_uq, m_mla_w_dkv=m_mla_w_dkv, m_mla_g_kv=m_mla_g_kv, m_mla_w_ukv=m_mla_w_ukv, m_mla_w_o=m_mla_w_o, m_swa_w_qkv=m_swa_w_qkv, m_swa_b_qkv=m_swa_b_qkv, m_swa_sinks=m_swa_sinks, m_swa_w_o=m_swa_w_o, m_swa_b_o=m_swa_b_o, m_w_ff1=m_w_ff1, m_w_ff2=m_w_ff2, m_g_final=m_g_final, v_w_ada=v_w_ada, v_b_ada=v_b_ada, v_g_mix=v_g_mix, v_g_mlp=v_g_mlp, v_mla_w_dq=v_mla_w_dq, v_mla_g_q=v_mla_g_q, v_mla_w_uq=v_mla_w_uq, v_mla_w_dkv=v_mla_w_dkv, v_mla_g_kv=v_mla_g_kv, v_mla_w_ukv=v_mla_w_ukv, v_mla_w_o=v_mla_w_o, v_swa_w_qkv=v_swa_w_qkv, v_swa_b_qkv=v_swa_b_qkv, v_swa_sinks=v_swa_sinks, v_swa_w_o=v_swa_w_o, v_swa_b_o=v_swa_b_o, v_w_ff1=v_w_ff1, v_w_ff2=v_w_ff2, v_g_final=v_g_final)
    weights = {n: given[n] for n in TWIN_WEIGHTS}
    shared = {n: given[n] for n in SHARED_INPUTS}
    per_example = {n: given[n] for n in ['x', 'c', 'positions']}
    grad_fn = _jax.value_and_grad(_loss, argnums=(0, 1))

    def one_microbatch(ex, loss_target):
        ex = dict(ex)
        diff = ex.pop(TWIN_DIFF_INPUT)
        return grad_fn(weights, diff, {**shared, **ex}, loss_target)

    if N_MICROBATCH == 1:
        loss, (grad_w, grad_x) = one_microbatch(per_example, given["loss_target"])
    else:
        def body(carry, xs):
            loss_sum, grad_sum = carry
            l_k, (gw_k, gx_k) = one_microbatch(xs[0], xs[1])
            with _jax.named_scope("update"):
                return (loss_sum + l_k, _jax.tree.map(_jnp.add, grad_sum, gw_k)), gx_k

        init = (_jnp.zeros((), _jnp.float32), _jax.tree.map(_jnp.zeros_like, weights))
        (loss, grad_w), grad_x = _jax.lax.scan(body, init, (per_example, given["loss_target"]))
    with _jax.named_scope("update"):
        delta_w, new_m, new_v = {}, {}, {}
        for n in TWIN_WEIGHTS:
            delta_w[n], new_m[n], new_v[n] = _adamw(weights[n], grad_w[n], given["m_" + n], given["v_" + n])
    return (loss, grad_x, *[grad_w[n] for n in TWIN_WEIGHTS], *[delta_w[n] for n in TWIN_WEIGHTS],
            *[new_m[n] for n in TWIN_WEIGHTS], *[new_v[n] for n in TWIN_WEIGHTS])
```

```python
import functools
import math

import jax
import jax.numpy as jnp
import numpy as np
from jax import lax
from jax.experimental import pallas as pl
from jax.experimental.pallas import tpu as pltpu

D = 1024
DEPTH = 2
MLA_HEADS = 8
QK_NOPE = 128
QK_ROPE = 64
V_DIM = 128
Q_LORA = 384
KV_LORA = 256
ROPE_THETA = 10000.0
SWA_HEADS = 16
SWA_KV_HEADS = 4
SWA_HEAD_DIM = 64
SWA_GROUP = SWA_HEADS // SWA_KV_HEADS
WINDOW = 128
D_FF = 4 * D
EPS = 1e-6
ADAM_LR = 0.001
ADAM_B1 = 0.9
ADAM_B2 = 0.999
ADAM_EPS = 1e-08
ADAM_WD = 0.01
ADAM_STEP = 10

N_CHIPS = 4
N_DEV = 8
LANES = 128
QK_EXT = 256
MLA_SCALE = (QK_NOPE + QK_ROPE) ** -0.5
SWA_SCALE = SWA_HEAD_DIM ** -0.5
NEG = -1e30
MXU_DTYPE = jnp.bfloat16
VMEM_LIMIT = 56 * 1024 * 1024
PACK_COLS = 512

R_SH1, R_SC1, R_GT1, R_SH2, R_SC2, R_GT2, R_GMIX, R_GMLP = range(8)
R_BO = 6


def _tile(n, pref):
    if n <= pref:
        return n
    for t in range(pref, 7, -1):
        if n % t == 0 and t % 8 == 0:
            return t
    return n


def _dot(a, b):
    return jnp.dot(a, b, preferred_element_type=jnp.float32)


def _dot_nt(a, b):
    return lax.dot_general(a, b, (((1,), (1,)), ((), ())), preferred_element_type=jnp.float32)


def _dot_tn(a, b):
    return lax.dot_general(a, b, (((0,), (0,)), ((), ())), preferred_element_type=jnp.float32)


def _rms(x):
    r = lax.rsqrt(jnp.mean(x * x, axis=-1, keepdims=True) + EPS)
    return x * r, r


def _rms_bwd(dxhat, xhat, r):
    return r * (dxhat - xhat * jnp.mean(dxhat * xhat, axis=-1, keepdims=True))


def _rowsum(v):
    return jnp.sum(v, axis=0, keepdims=True)


def _params(*sem):
    return pltpu.CompilerParams(dimension_semantics=sem, vmem_limit_bytes=VMEM_LIMIT)


def _full(shape):
    nd = len(shape)
    return pl.BlockSpec(shape, lambda *_: (0,) * nd)


def _rows(tm, cols):
    return pl.BlockSpec((tm, cols), lambda i, *_: (i, 0))


def _modulate_bwd(dh, x, vec_ref, r_g, r_sc, r_sh, ps_ref, dres):
    xhat, r = _rms(x)
    g = vec_ref[r_g:r_g + 1, :]
    n = xhat * g
    ps_ref[r_sh:r_sh + 1, :] += _rowsum(dh)
    ps_ref[r_sc:r_sc + 1, :] += _rowsum(dh * n)
    dn = dh * (1.0 + vec_ref[r_sc:r_sc + 1, :])
    ps_ref[r_g:r_g + 1, :] += _rowsum(dn * xhat)
    return dres + _rms_bwd(dn * g, xhat, r)


def _mla_pre(x, vec, wcat, g_q, g_kv, wuq, wukv, cs):
    T = x.shape[0]
    tm = _tile(T, 512)
    H = MLA_HEADS

    def body(x_ref, vec_ref, wcat_ref, gq_ref, gkv_ref, wuq_ref, wukv_ref, cs_ref, h_ref, z_ref, q_ref, k_ref, v_ref):
        xhat, _ = _rms(x_ref[...])
        h = xhat * vec_ref[R_GMIX:R_GMIX + 1, :] * (1.0 + vec_ref[R_SC1:R_SC1 + 1, :]) + vec_ref[R_SH1:R_SH1 + 1, :]
        hb = h.astype(MXU_DTYPE)
        h_ref[...] = hb
        z = _dot(hb, wcat_ref[...])
        z_ref[...] = z
        cq = (_rms(z[:, :Q_LORA])[0] * gq_ref[...]).astype(MXU_DTYPE)
        ckv = (_rms(z[:, Q_LORA:Q_LORA + KV_LORA])[0] * gkv_ref[...]).astype(MXU_DTYPE)
        cs_t = cs_ref[...]
        t = z[:, Q_LORA + KV_LORA:] * cs_t
        k_rope = (t + pltpu.roll(t, QK_ROPE, axis=1)).astype(MXU_DTYPE)
        low = lax.broadcasted_iota(jnp.int32, (1, LANES), 1) < QK_ROPE
        for hd in range(H):
            qf = _dot(cq, wuq_ref[hd])
            tq = qf[:, QK_NOPE:] * cs_t
            tq = tq + pltpu.roll(tq, QK_ROPE, axis=1)
            q_ref[hd, :, :QK_NOPE] = qf[:, :QK_NOPE].astype(MXU_DTYPE)
            q_ref[hd, :, QK_NOPE:] = jnp.where(low, tq, 0.0).astype(MXU_DTYPE)
            kvf = _dot(ckv, wukv_ref[hd])
            k_ref[hd, :, :QK_NOPE] = kvf[:, :QK_NOPE].astype(MXU_DTYPE)
            k_ref[hd, :, QK_NOPE:] = k_rope
            v_ref[hd] = kvf[:, QK_NOPE:].astype(MXU_DTYPE)

    zc = wcat.shape[1]
    return pl.pallas_call(
        body, name="mla_pre", grid=(T // tm,),
        in_specs=[_rows(tm, D), _full((8, D)), _full(wcat.shape), _full(g_q.shape), _full(g_kv.shape),
                  _full(wuq.shape), _full(wukv.shape), _rows(tm, LANES)],
        out_specs=[_rows(tm, D), _rows(tm, zc),
                   pl.BlockSpec((H, tm, QK_EXT), lambda i: (0, i, 0)),
                   pl.BlockSpec((H, tm, QK_EXT), lambda i: (0, i, 0)),
                   pl.BlockSpec((H, tm, V_DIM), lambda i: (0, i, 0))],
        out_shape=[jax.ShapeDtypeStruct((T, D), MXU_DTYPE), jax.ShapeDtypeStruct((T, zc), jnp.float32),
                   jax.ShapeDtypeStruct((H, T, QK_EXT), MXU_DTYPE), jax.ShapeDtypeStruct((H, T, QK_EXT), MXU_DTYPE),
                   jax.ShapeDtypeStruct((H, T, V_DIM), MXU_DTYPE)],
        compiler_params=_params("parallel"),
    )(x, vec, wcat, g_q, g_kv, wuq, wukv, cs)


def _mla_attn_fwd(q, k, v):
    H, T, _ = q.shape
    tq = _tile(T, 512)
    nq = T // tq

    def body(q_ref, k_ref, v_ref, o_ref, lse_ref, m_sc, l_sc, acc_sc):
        qi, kj = pl.program_id(1), pl.program_id(2)

        @pl.when(kj == 0)
        def _():
            m_sc[...] = jnp.full_like(m_sc, NEG)
            l_sc[...] = jnp.zeros_like(l_sc)
            acc_sc[...] = jnp.zeros_like(acc_sc)

        def step(masked):
            s = _dot_nt(q_ref[0], k_ref[0]) * MLA_SCALE
            if masked:
                row = lax.broadcasted_iota(jnp.int32, (tq, tq), 0)
                col = lax.broadcasted_iota(jnp.int32, (tq, tq), 1)
                s = jnp.where(col <= row, s, NEG)
            m_prev = m_sc[...]
            m_new = jnp.maximum(m_prev, jnp.max(s, axis=1, keepdims=True))
            alpha = jnp.exp(m_prev - m_new)
            p = jnp.exp(s - m_new)
            l_sc[...] = alpha * l_sc[...] + jnp.sum(p, axis=1, keepdims=True)
            acc_sc[...] = alpha * acc_sc[...] + _dot(p.astype(MXU_DTYPE), v_ref[0])
            m_sc[...] = m_new

        @pl.when(kj < qi)
        def _():
            step(False)

        @pl.when(kj == qi)
        def _():
            step(True)
            l = l_sc[...]
            o_ref[...] = (acc_sc[...] / l).astype(o_ref.dtype)
            lse_ref[0] = jnp.broadcast_to(m_sc[...] + jnp.log(l), (tq, LANES))

    kv_idx = lambda h, i, j: (h, jnp.minimum(i, j), 0)
    return pl.pallas_call(
        body, name="mla_attn_fwd", grid=(H, nq, nq),
        in_specs=[pl.BlockSpec((1, tq, QK_EXT), lambda h, i, j: (h, i, 0)),
                  pl.BlockSpec((1, tq, QK_EXT), kv_idx),
                  pl.BlockSpec((1, tq, V_DIM), kv_idx)],
        out_specs=[pl.BlockSpec((tq, V_DIM), lambda h, i, j: (i, h)),
                   pl.BlockSpec((1, tq, LANES), lambda h, i, j: (h, i, 0))],
        out_shape=[jax.ShapeDtypeStruct((T, H * V_DIM), MXU_DTYPE), jax.ShapeDtypeStruct((H, T, LANES), jnp.float32)],
        scratch_shapes=[pltpu.VMEM((tq, 1), jnp.float32), pltpu.VMEM((tq, 1), jnp.float32),
                        pltpu.VMEM((tq, V_DIM), jnp.float32)],
        compiler_params=_params("parallel", "parallel", "arbitrary"),
    )(q, k, v)


def _post_attn(o, x, w_o, bias, vec):
    T = x.shape[0]
    tm = _tile(T, 512)

    def body(o_ref, x_ref, w_ref, b_ref, vec_ref, y_ref, xm_ref, h_ref):
        y = _dot(o_ref[...], w_ref[...]) + b_ref[...]
        y_ref[...] = y.astype(y_ref.dtype)
        xm = x_ref[...] + vec_ref[R_GT1:R_GT1 + 1, :] * y
        xm_ref[...] = xm
        xhat, _ = _rms(xm)
        h = xhat * vec_ref[R_GMLP:R_GMLP + 1, :] * (1.0 + vec_ref[R_SC2:R_SC2 + 1, :]) + vec_ref[R_SH2:R_SH2 + 1, :]
        h_ref[...] = h.astype(h_ref.dtype)

    return pl.pallas_call(
        body, name="post_attn", grid=(T // tm,),
        in_specs=[_rows(tm, D), _rows(tm, D), _full((D, D)), _full((1, D)), _full((8, D))],
        out_specs=[_rows(tm, D), _rows(tm, D), _rows(tm, D)],
        out_shape=[jax.ShapeDtypeStruct((T, D), MXU_DTYPE), jax.ShapeDtypeStruct((T, D), jnp.float32),
                   jax.ShapeDtypeStruct((T, D), MXU_DTYPE)],
        compiler_params=_params("parallel"),
    )(o, x, w_o, bias, vec)


def _mlp_fwd(h2, w1, w2, xm, vec):
    T = h2.shape[0]
    tm = _tile(T, 1024)
    tf = _tile(D_FF, 512)
    nf = D_FF // tf

    def body(h_ref, w1_ref, w2_ref, xm_ref, vec_ref, a_ref, y_ref, xo_ref, acc):
        f = pl.program_id(1)

        @pl.when(f == 0)
        def _():
            acc[...] = jnp.zeros_like(acc)

        u = jnp.maximum(_dot(h_ref[...], w1_ref[...]), 0.0)
        ab = (u * u).astype(MXU_DTYPE)
        a_ref[...] = ab
        acc[...] += _dot(ab, w2_ref[...])

        @pl.when(f == nf - 1)
        def _():
            y = acc[...]
            y_ref[...] = y.astype(y_ref.dtype)
            xo_ref[...] = xm_ref[...] + vec_ref[R_GT2:R_GT2 + 1, :] * y

    return pl.pallas_call(
        body, name="mlp_fwd", grid=(T // tm, nf),
        in_specs=[_rows(tm, D), pl.BlockSpec((D, tf), lambda i, f: (0, f)), pl.BlockSpec((tf, D), lambda i, f: (f, 0)),
                  _rows(tm, D), _full((8, D))],
        out_specs=[pl.BlockSpec((tm, tf), lambda i, f: (i, f)), _rows(tm, D), _rows(tm, D)],
        out_shape=[jax.ShapeDtypeStruct((T, D_FF), MXU_DTYPE), jax.ShapeDtypeStruct((T, D), MXU_DTYPE),
                   jax.ShapeDtypeStruct((T, D), jnp.float32)],
        scratch_shapes=[pltpu.VMEM((tm, D), jnp.float32)],
        compiler_params=_params("parallel", "arbitrary"),
    )(h2, w1, w2, xm, vec)


def _swa_pre(x, vec, w_qkv, b_qkv):
    T = x.shape[0]
    tm = _tile(T, 512)
    nq = SWA_HEADS * SWA_HEAD_DIM
    nk = SWA_KV_HEADS * SWA_HEAD_DIM

    def body(x_ref, vec_ref, w_ref, b_ref, h_ref, q_ref, k_ref, v_ref):
        xhat, _ = _rms(x_ref[...])
        h = xhat * vec_ref[R_GMIX:R_GMIX + 1, :] * (1.0 + vec_ref[R_SC1:R_SC1 + 1, :]) + vec_ref[R_SH1:R_SH1 + 1, :]
        hb = h.astype(MXU_DTYPE)
        h_ref[...] = hb
        qkv = _dot(hb, w_ref[...]) + b_ref[...]
        q_ref[...] = (qkv[:, :nq] * SWA_SCALE).astype(MXU_DTYPE)
        k_ref[...] = qkv[:, nq:nq + nk].astype(MXU_DTYPE)
        v_ref[...] = qkv[:, nq + nk:].astype(MXU_DTYPE)

    return pl.pallas_call(
        body, name="swa_pre", grid=(T // tm,),
        in_specs=[_rows(tm, D), _full((8, D)), _full(w_qkv.shape), _full(b_qkv.shape)],
        out_specs=[_rows(tm, D), _rows(tm, nq), _rows(tm, nk), _rows(tm, nk)],
        out_shape=[jax.ShapeDtypeStruct((T, D), MXU_DTYPE), jax.ShapeDtypeStruct((T, nq), MXU_DTYPE),
                   jax.ShapeDtypeStruct((T, nk), MXU_DTYPE), jax.ShapeDtypeStruct((T, nk), MXU_DTYPE)],
        compiler_params=_params("parallel"),
    )(x, vec, w_qkv, b_qkv)


def _swa_bias():
    W = WINDOW
    slopes = 2.0 ** (-8.0 * np.arange(1, SWA_HEADS + 1) / SWA_HEADS)
    dist = W + np.arange(W)[:, None] - np.arange(2 * W)[None, :]
    inside = (dist >= 0) & (dist < W)
    bias = np.where(inside[None], -slopes[:, None, None] * dist[None].astype(np.float64), NEG)
    return jnp.asarray(bias.reshape(SWA_KV_HEADS, SWA_GROUP * W, 2 * W), jnp.float32)


def _swa_probs(n, kh, q_ref, kp_ref, kc_ref, bias_ref, sink_ref):
    W, Dh, G = WINDOW, SWA_HEAD_DIM, SWA_GROUP
    qs = jnp.concatenate([q_ref[:, (kh * G + g) * Dh:(kh * G + g + 1) * Dh] for g in range(G)], axis=0)
    kb = jnp.concatenate([kp_ref[:, kh * Dh:(kh + 1) * Dh], kc_ref[:, kh * Dh:(kh + 1) * Dh]], axis=0)
    s = _dot_nt(qs, kb) + bias_ref[kh]
    col = lax.broadcasted_iota(jnp.int32, (1, 2 * W), 1)
    s = jnp.where((col >= W) | (n > 0), s, NEG)
    sink = jnp.concatenate(
        [jnp.broadcast_to(sink_ref[kh * G + g:kh * G + g + 1, :1], (W, 1)) for g in range(G)], axis=0)
    m = jnp.maximum(jnp.max(s, axis=1, keepdims=True), sink)
    p = jnp.exp(s - m)
    p_sink = jnp.exp(sink - m)
    inv = 1.0 / (jnp.sum(p, axis=1, keepdims=True) + p_sink)
    return qs, kb, p * inv, p_sink * inv


def _swa_attn_fwd(q, k, v, bias, sinks_b):
    T = q.shape[0]
    W, Dh, G, Hk = WINDOW, SWA_HEAD_DIM, SWA_GROUP, SWA_KV_HEADS
    nk = Hk * Dh

    def body(q_ref, kp_ref, kc_ref, vp_ref, vc_ref, bias_ref, sink_ref, o_ref):
        n = pl.program_id(0)
        outs = []
        for kh in range(Hk):
            _, _, pn, _ = _swa_probs(n, kh, q_ref, kp_ref, kc_ref, bias_ref, sink_ref)
            vb = jnp.concatenate([vp_ref[:, kh * Dh:(kh + 1) * Dh], vc_ref[:, kh * Dh:(kh + 1) * Dh]], axis=0)
            o = _dot(pn.astype(MXU_DTYPE), vb)
            outs += [o[g * W:(g + 1) * W] for g in range(G)]
        o_ref[...] = jnp.concatenate(outs, axis=1).astype(o_ref.dtype)

    prev = lambda n: (jnp.maximum(n - 1, 0), 0)
    cur = lambda n: (n, 0)
    return pl.pallas_call(
        body, name="swa_attn_fwd", grid=(T // W,),
        in_specs=[pl.BlockSpec((W, D), cur), pl.BlockSpec((W, nk), prev), pl.BlockSpec((W, nk), cur),
                  pl.BlockSpec((W, nk), prev), pl.BlockSpec((W, nk), cur), _full(bias.shape), _full(sinks_b.shape)],
        out_specs=pl.BlockSpec((W, D), cur),
        out_shape=jax.ShapeDtypeStruct((T, D), MXU_DTYPE),
        compiler_params=_params("parallel"),
    )(q, k, k, v, v, bias, sinks_b)


def _final_loss(x, tgt, g):
    T = x.shape[0]
    tm = _tile(T, 512)

    def body(x_ref, t_ref, g_ref, loss_ref, dx_ref, dg_ref):
        @pl.when(pl.program_id(0) == 0)
        def _():
            loss_ref[...] = jnp.zeros_like(loss_ref)
            dg_ref[...] = jnp.zeros_like(dg_ref)

        xhat, r = _rms(x_ref[...])
        gv = g_ref[...]
        e = xhat * gv - t_ref[...]
        loss_ref[...] += 0.5 * jnp.sum(jnp.mean(e * e, axis=-1, keepdims=True), axis=0, keepdims=True)
        dy = e * (1.0 / D)
        dg_ref[...] += _rowsum(dy * xhat)
        dx_ref[...] = _rms_bwd(dy * gv, xhat, r)

    return pl.pallas_call(
        body, name="final_loss", grid=(T // tm,),
        in_specs=[_rows(tm, D), _rows(tm, D), _full((1, D))],
        out_specs=[_full((8, LANES)), _rows(tm, D), _full((1, D))],
        out_shape=[jax.ShapeDtypeStruct((8, LANES), jnp.float32), jax.ShapeDtypeStruct((T, D), jnp.float32),
                   jax.ShapeDtypeStruct((1, D), jnp.float32)],
        compiler_params=_params("arbitrary"),
    )(x, tgt, g)


def _mlp_bwd(dxo, y2, a, w1, w2, xm, vec):
    T = dxo.shape[0]
    tm = _tile(T, 1024)
    tf = _tile(D_FF, 512)
    nf = D_FF // tf

    def body(dxo_ref, y_ref, a_ref, w1_ref, w2_ref, xm_ref, vec_ref, du_ref, dy_ref, dxm_ref, ps_ref, dyb, acc):
        i, f = pl.program_id(0), pl.program_id(1)

        @pl.when((i == 0) & (f == 0))
        def _():
            ps_ref[...] = jnp.zeros_like(ps_ref)

        @pl.when(f == 0)
        def _():
            dxo_t = dxo_ref[...]
            d = (dxo_t * vec_ref[R_GT2:R_GT2 + 1, :]).astype(MXU_DTYPE)
            dyb[...] = d
            dy_ref[...] = d
            acc[...] = jnp.zeros_like(acc)
            ps_ref[R_GT2:R_GT2 + 1, :] += _rowsum(dxo_t * y_ref[...].astype(jnp.float32))

        da = _dot_nt(dyb[...], w2_ref[...])
        dub = (da * (2.0 * jnp.sqrt(a_ref[...].astype(jnp.float32)))).astype(MXU_DTYPE)
        du_ref[...] = dub
        acc[...] += _dot_nt(dub, w1_ref[...])

        @pl.when(f == nf - 1)
        def _():
            dxm_ref[...] = _modulate_bwd(acc[...], xm_ref[...], vec_ref, R_GMLP, R_SC2, R_SH2, ps_ref, dxo_ref[...])

    return pl.pallas_call(
        body, name="mlp_bwd", grid=(T // tm, nf),
        in_specs=[_rows(tm, D), _rows(tm, D), pl.BlockSpec((tm, tf), lambda i, f: (i, f)),
                  pl.BlockSpec((D, tf), lambda i, f: (0, f)), pl.BlockSpec((tf, D), lambda i, f: (f, 0)),
                  _rows(tm, D), _full((8, D))],
        out_specs=[pl.BlockSpec((tm, tf), lambda i, f: (i, f)), _rows(tm, D), _rows(tm, D), _full((8, D))],
        out_shape=[jax.ShapeDtypeStruct((T, D_FF), MXU_DTYPE), jax.ShapeDtypeStruct((T, D), MXU_DTYPE),
                   jax.ShapeDtypeStruct((T, D), jnp.float32), jax.ShapeDtypeStruct((8, D), jnp.float32)],
        scratch_shapes=[pltpu.VMEM((tm, D), MXU_DTYPE), pltpu.VMEM((tm, D), jnp.float32)],
        compiler_params=_params("arbitrary", "arbitrary"),
    )(dxo, y2, a, w1, w2, xm, vec)


def _mm_tn(a, g, name):
    T, K = a.shape
    N = g.shape[1]
    bk, bn, bt = _tile(K, 1024), _tile(N, 1024), _tile(T, 1024)
    if N % bn or bn % LANES:
        bn = N

    def body(a_ref, g_ref, o_ref):
        @pl.when(pl.program_id(2) == 0)
        def _():
            o_ref[...] = jnp.zeros_like(o_ref)

        o_ref[...] += _dot_tn(a_ref[...], g_ref[...])

    return pl.pallas_call(
        body, name=name, grid=(K // bk, N // bn, T // bt),
        in_specs=[pl.BlockSpec((bt, bk), lambda k, n, t: (t, k)), pl.BlockSpec((bt, bn), lambda k, n, t: (t, n))],
        out_specs=pl.BlockSpec((bk, bn), lambda k, n, t: (k, n)),
        out_shape=jax.ShapeDtypeStruct((K, N), jnp.float32),
        compiler_params=_params("parallel", "parallel", "arbitrary"),
    )(a, g)


def _attn_out_bwd(dxm, y1, o, w_o, vec, with_delta):
    T = dxm.shape[0]
    tm = _tile(T, 512)
    H = MLA_HEADS

    def body(dxm_ref, y_ref, o_ref, w_ref, vec_ref, dy_ref, do_ref, ps_ref, *delta_ref):
        @pl.when(pl.program_id(0) == 0)
        def _():
            ps_ref[...] = jnp.zeros_like(ps_ref)

        dxm_t = dxm_ref[...]
        dy = dxm_t * vec_ref[R_GT1:R_GT1 + 1, :]
        ps_ref[R_GT1:R_GT1 + 1, :] += _rowsum(dxm_t * y_ref[...].astype(jnp.float32))
        ps_ref[R_BO:R_BO + 1, :] += _rowsum(dy)
        dyb = dy.astype(MXU_DTYPE)
        dy_ref[...] = dyb
        do = _dot_nt(dyb, w_ref[...])
        do_ref[...] = do.astype(do_ref.dtype)
        if with_delta:
            of = o_ref[...].astype(jnp.float32)
            for hd in range(H):
                sl = slice(hd * V_DIM, (hd + 1) * V_DIM)
                d = jnp.sum(do[:, sl] * of[:, sl], axis=1, keepdims=True)
                delta_ref[0][hd] = jnp.broadcast_to(d, (tm, LANES))

    out_specs = [_rows(tm, D), _rows(tm, D), _full((8, D))]
    out_shape = [jax.ShapeDtypeStruct((T, D), MXU_DTYPE), jax.ShapeDtypeStruct((T, D), MXU_DTYPE),
                 jax.ShapeDtypeStruct((8, D), jnp.float32)]
    if with_delta:
        out_specs.append(pl.BlockSpec((H, tm, LANES), lambda i: (0, i, 0)))
        out_shape.append(jax.ShapeDtypeStruct((H, T, LANES), jnp.float32))
    return pl.pallas_call(
        body, name="attn_out_bwd_mla" if with_delta else "attn_out_bwd_swa", grid=(T // tm,),
        in_specs=[_rows(tm, D), _rows(tm, D), _rows(tm, D), _full((D, D)), _full((8, D))],
        out_specs=out_specs, out_shape=out_shape,
        compiler_params=_params("arbitrary"),
    )(dxm, y1, o, w_o, vec)


def _mla_attn_bwd(q, k, v, do, lse, delta):
    H, T, _ = q.shape
    tq = _tile(T, 512)
    nq = T // tq

    def body(q_ref, k_ref, v_ref, do_ref, lse_ref, dl_ref, dq_ref, dk_ref, dv_ref, dk_acc, dv_acc):
        j, i = pl.program_id(1), pl.program_id(2)

        @pl.when((j == 0) & (i == 0))
        def _():
            dq_ref[...] = jnp.zeros_like(dq_ref)

        def step(masked):
            qb, kb, dob = q_ref[0], k_ref[0], do_ref[...]
            s = _dot_nt(qb, kb) * MLA_SCALE
            if masked:
                row = lax.broadcasted_iota(jnp.int32, (tq, tq), 0)
                col = lax.broadcasted_iota(jnp.int32, (tq, tq), 1)
                s = jnp.where(col <= row, s, NEG)
            p = jnp.exp(s - lse_ref[0][:, :1])
            dv_new = _dot_tn(p.astype(MXU_DTYPE), dob)
            dp = _dot_nt(dob, v_ref[0])
            dsb = (p * (dp - dl_ref[0][:, :1]) * MLA_SCALE).astype(MXU_DTYPE)
            rows = pl.ds(pl.multiple_of(i * tq, tq), tq)
            dq_ref[0, rows, :] += _dot(dsb, kb)
            dk_new = _dot_tn(dsb, qb)
            if masked:
                dk_acc[...] = dk_new
                dv_acc[...] = dv_new
            else:
                dk_acc[...] += dk_new
                dv_acc[...] += dv_new

        @pl.when(i == j)
        def _():
            step(True)

        @pl.when(i > j)
        def _():
            step(False)

        @pl.when(i == nq - 1)
        def _():
            dk_ref[0] = dk_acc[...].astype(dk_ref.dtype)
            dv_ref[0] = dv_acc[...].astype(dv_ref.dtype)

    q_idx = lambda h, j, i: (h, jnp.maximum(i, j), 0)
    kv_idx = lambda h, j, i: (h, j, 0)
    return pl.pallas_call(
        body, name="mla_attn_bwd", grid=(H, nq, nq),
        in_specs=[pl.BlockSpec((1, tq, QK_EXT), q_idx), pl.BlockSpec((1, tq, QK_EXT), kv_idx),
                  pl.BlockSpec((1, tq, V_DIM), kv_idx),
                  pl.BlockSpec((tq, V_DIM), lambda h, j, i: (jnp.maximum(i, j), h)),
                  pl.BlockSpec((1, tq, LANES), q_idx), pl.BlockSpec((1, tq, LANES), q_idx)],
        out_specs=[pl.BlockSpec((1, T, QK_EXT), lambda h, j, i: (h, 0, 0)),
                   pl.BlockSpec((1, tq, QK_EXT), kv_idx), pl.BlockSpec((1, tq, V_DIM), kv_idx)],
        out_shape=[jax.ShapeDtypeStruct((H, T, QK_EXT), jnp.float32), jax.ShapeDtypeStruct((H, T, QK_EXT), MXU_DTYPE),
                   jax.ShapeDtypeStruct((H, T, V_DIM), MXU_DTYPE)],
        scratch_shapes=[pltpu.VMEM((tq, QK_EXT), jnp.float32), pltpu.VMEM((tq, V_DIM), jnp.float32)],
        compiler_params=_params("parallel", "arbitrary", "arbitrary"),
    )(q, k, v, do, lse, delta)


def _mla_pre_bwd(x, dxm, vec, hb, z, dq, dk, dv, cs, wcat, g_q, g_kv, wuq, wukv):
    T = x.shape[0]
    tm = _tile(T, 256)
    H = MLA_HEADS
    zc = wcat.shape[1]

    def body(x_ref, dxm_ref, vec_ref, h_ref, z_ref, dq_ref, dk_ref, dv_ref, cs_ref, wcat_ref, gq_ref, gkv_ref,
             wuq_ref, wukv_ref, dx_ref, ps_ref, dgq_ref, dgkv_ref, dwcat_ref, dwuq_ref, dwukv_ref):
        @pl.when(pl.program_id(0) == 0)
        def _():
            for ref in (ps_ref, dgq_ref, dgkv_ref, dwcat_ref, dwuq_ref, dwukv_ref):
                ref[...] = jnp.zeros_like(ref)

        z = z_ref[...]
        cs_t = cs_ref[...]
        cqhat, rq = _rms(z[:, :Q_LORA])
        ckhat, rk = _rms(z[:, Q_LORA:Q_LORA + KV_LORA])
        gq, gkv = gq_ref[...], gkv_ref[...]
        cq = (cqhat * gq).astype(MXU_DTYPE)
        ckv = (ckhat * gkv).astype(MXU_DTYPE)
        dcq = jnp.zeros((tm, Q_LORA), jnp.float32)
        dckv = jnp.zeros((tm, KV_LORA), jnp.float32)
        dkr = jnp.zeros((tm, LANES), jnp.float32)
        for hd in range(H):
            dqh = dq_ref[hd]
            gqh = jnp.concatenate([dqh[:, :QK_NOPE], dqh[:, QK_NOPE:] * cs_t], axis=1).astype(MXU_DTYPE)
            dcq += _dot_nt(gqh, wuq_ref[hd])
            dwuq_ref[hd] += _dot_tn(cq, gqh)
            dkh = dk_ref[hd]
            gkvh = jnp.concatenate([dkh[:, :QK_NOPE], dv_ref[hd]], axis=1)
            dckv += _dot_nt(gkvh, wukv_ref[hd])
            dwukv_ref[hd] += _dot_tn(ckv, gkvh)
            dkr += dkh[:, QK_NOPE:].astype(jnp.float32)
        dgq_ref[...] += _rowsum(dcq * cqhat)
        dgkv_ref[...] += _rowsum(dckv * ckhat)
        dcq_pre = _rms_bwd(dcq * gq, cqhat, rq)
        dckv_pre = _rms_bwd(dckv * gkv, ckhat, rk)
        dkr2 = (dkr + pltpu.roll(dkr, QK_ROPE, axis=1)) * cs_t
        dz = jnp.concatenate([dcq_pre, dckv_pre, dkr2], axis=1).astype(MXU_DTYPE)
        dwcat_ref[...] += _dot_tn(h_ref[...], dz)
        dh = _dot_nt(dz, wcat_ref[...])
        dx_ref[...] = _modulate_bwd(dh, x_ref[...], vec_ref, R_GMIX, R_SC1, R_SH1, ps_ref, dxm_ref[...])

    hblk = lambda w: pl.BlockSpec((H, tm, w), lambda i: (0, i, 0))
    return pl.pallas_call(
        body, name="mla_pre_bwd", grid=(T // tm,),
        in_specs=[_rows(tm, D), _rows(tm, D), _full((8, D)), _rows(tm, D), _rows(tm, zc), hblk(QK_EXT), hblk(QK_EXT),
                  hblk(V_DIM), _rows(tm, LANES), _full(wcat.shape), _full(g_q.shape), _full(g_kv.shape),
                  _full(wuq.shape), _full(wukv.shape)],
        out_specs=[_rows(tm, D), _full((8, D)), _full(g_q.shape), _full(g_kv.shape), _full(wcat.shape),
                   _full(wuq.shape), _full(wukv.shape)],
        out_shape=[jax.ShapeDtypeStruct((T, D), jnp.float32), jax.ShapeDtypeStruct((8, D), jnp.float32),
                   jax.ShapeDtypeStruct(g_q.shape, jnp.float32), jax.ShapeDtypeStruct(g_kv.shape, jnp.float32),
                   jax.ShapeDtypeStruct(wcat.shape, jnp.float32), jax.ShapeDtypeStruct(wuq.shape, jnp.float32),
                   jax.ShapeDtypeStruct(wukv.shape, jnp.float32)],
        compiler_params=_params("arbitrary"),
    )(x, dxm, vec, hb, z, dq, dk, dv, cs, wcat, g_q, g_kv, wuq, wukv)


def _swa_attn_bwd(q, k, v, do, bias, sinks_b):
    T = q.shape[0]
    W, Dh, G, Hk = WINDOW, SWA_HEAD_DIM, SWA_GROUP, SWA_KV_HEADS
    nk = Hk * Dh

    def body(q_ref, kp_ref, kc_ref, vp_ref, vc_ref, do_ref, bias_ref, sink_ref, dq_ref, dk_ref, dv_ref, dsink_ref):
        n = pl.program_id(0)

        @pl.when(n == 0)
        def _():
            dk_ref[...] = jnp.zeros_like(dk_ref)
            dv_ref[...] = jnp.zeros_like(dv_ref)
            dsink_ref[...] = jnp.zeros_like(dsink_ref)

        dqs, dks, dvs = [], [], []
        for kh in range(Hk):
            qs, kb, pn, p_sink = _swa_probs(n, kh, q_ref, kp_ref, kc_ref, bias_ref, sink_ref)
            vb = jnp.concatenate([vp_ref[:, kh * Dh:(kh + 1) * Dh], vc_ref[:, kh * Dh:(kh + 1) * Dh]], axis=0)
            dos = jnp.concatenate([do_ref[:, (kh * G + g) * Dh:(kh * G + g + 1) * Dh] for g in range(G)], axis=0)
            dp = _dot_nt(dos, vb)
            delta = jnp.sum(pn * dp, axis=1, keepdims=True)
            dsb = (pn * (dp - delta)).astype(MXU_DTYPE)
            dsk = -p_sink * delta
            for g in range(G):
                h = kh * G + g
                dsink_ref[h:h + 1, :] += jnp.broadcast_to(jnp.sum(dsk[g * W:(g + 1) * W], axis=0, keepdims=True), (1, LANES))
            dq_st = _dot(dsb, kb) * SWA_SCALE
            dqs += [dq_st[g * W:(g + 1) * W] for g in range(G)]
            dks.append(_dot_tn(dsb, qs))
            dvs.append(_dot_tn(pn.astype(MXU_DTYPE), dos))
        dq_ref[...] = jnp.concatenate(dqs, axis=1)
        dkb = jnp.concatenate(dks, axis=1)
        dvb = jnp.concatenate(dvs, axis=1)
        cur_rows = pl.ds(pl.multiple_of(n * W, W), W)
        dk_ref[cur_rows, :] += dkb[W:]
        dv_ref[cur_rows, :] += dvb[W:]

        @pl.when(n > 0)
        def _():
            prev_rows = pl.ds(pl.multiple_of((n - 1) * W, W), W)
            dk_ref[prev_rows, :] += dkb[:W]
            dv_ref[prev_rows, :] += dvb[:W]

    prev = lambda n: (jnp.maximum(n - 1, 0), 0)
    cur = lambda n: (n, 0)
    return pl.pallas_call(
        body, name="swa_attn_bwd", grid=(T // W,),
        in_specs=[pl.BlockSpec((W, D), cur), pl.BlockSpec((W, nk), prev), pl.BlockSpec((W, nk), cur),
                  pl.BlockSpec((W, nk), prev), pl.BlockSpec((W, nk), cur), pl.BlockSpec((W, D), cur),
                  _full(bias.shape), _full(sinks_b.shape)],
        out_specs=[pl.BlockSpec((W, D), cur), _full((T, nk)), _full((T, nk)), _full(sinks_b.shape)],
        out_shape=[jax.ShapeDtypeStruct((T, D), jnp.float32), jax.ShapeDtypeStruct((T, nk), jnp.float32),
                   jax.ShapeDtypeStruct((T, nk), jnp.float32), jax.ShapeDtypeStruct(sinks_b.shape, jnp.float32)],
        compiler_params=_params("arbitrary"),
    )(q, k, k, v, v, do, bias, sinks_b)


def _swa_pre_bwd(x, dxm, vec, dq, dk, dv, w_qkv):
    T = x.shape[0]
    tm = _tile(T, 512)
    nq = SWA_HEADS * SWA_HEAD_DIM
    nk = SWA_KV_HEADS * SWA_HEAD_DIM
    nqkv = nq + 2 * nk

    def body(x_ref, dxm_ref, vec_ref, dq_ref, dk_ref, dv_ref, w_ref, dx_ref, dqkv_ref, ps_ref, db_ref):
        @pl.when(pl.program_id(0) == 0)
        def _():
            ps_ref[...] = jnp.zeros_like(ps_ref)
            db_ref[...] = jnp.zeros_like(db_ref)

        dqkv = jnp.concatenate([dq_ref[...], dk_ref[...], dv_ref[...]], axis=1)
        db_ref[...] += _rowsum(dqkv)
        dqkv_b = dqkv.astype(MXU_DTYPE)
        dqkv_ref[...] = dqkv_b
        dh = _dot_nt(dqkv_b, w_ref[...])
        dx_ref[...] = _modulate_bwd(dh, x_ref[...], vec_ref, R_GMIX, R_SC1, R_SH1, ps_ref, dxm_ref[...])

    return pl.pallas_call(
        body, name="swa_pre_bwd", grid=(T // tm,),
        in_specs=[_rows(tm, D), _rows(tm, D), _full((8, D)), _rows(tm, nq), _rows(tm, nk), _rows(tm, nk),
                  _full(w_qkv.shape)],
        out_specs=[_rows(tm, D), _rows(tm, nqkv), _full((8, D)), _full((1, nqkv))],
        out_shape=[jax.ShapeDtypeStruct((T, D), jnp.float32), jax.ShapeDtypeStruct((T, nqkv), MXU_DTYPE),
                   jax.ShapeDtypeStruct((8, D), jnp.float32), jax.ShapeDtypeStruct((1, nqkv), jnp.float32)],
        compiler_params=_params("arbitrary"),
    )(x, dxm, vec, dq, dk, dv, w_qkv)


def _rot_cols(w):
    half = QK_ROPE // 2
    return jnp.concatenate([-w[..., half:], w[..., :half]], axis=-1)


def _unrot_grad(d_rope, d_rot):
    half = QK_ROPE // 2
    return d_rope + jnp.concatenate([d_rot[..., half:], -d_rot[..., :half]], axis=-1)


def _rope_table(positions):
    half = QK_ROPE // 2
    inv_freq = ROPE_THETA ** (-jnp.arange(half, dtype=jnp.float32) / half)
    ang = positions.astype(jnp.float32)[:, None] * inv_freq
    cos, sin = jnp.cos(ang), jnp.sin(ang)
    return jnp.concatenate([cos, cos, sin, sin], axis=1)


def _sequence_step(x, tgt, positions, vecs, g_q, g_kv, sinks, g_final, wts):
    H = MLA_HEADS
    cs = _rope_table(positions)
    w_dkv = wts["mla_w_dkv"]
    wcat = jnp.concatenate([wts["mla_w_dq"], w_dkv, _rot_cols(w_dkv[:, KV_LORA:])], axis=1)
    uq = wts["mla_w_uq"].reshape(Q_LORA, H, QK_NOPE + QK_ROPE)
    wuq = jnp.concatenate([uq, _rot_cols(uq[..., QK_NOPE:])], axis=-1).transpose(1, 0, 2)
    wukv = wts["mla_w_ukv"].reshape(KV_LORA, H, QK_NOPE + V_DIM).transpose(1, 0, 2)
    zero_bias = jnp.zeros((1, D), jnp.float32)
    bias = _swa_bias()
    sinks_b = jnp.broadcast_to(sinks.reshape(SWA_HEADS, 1), (SWA_HEADS, LANES))

    h1a, z, q, k, v = _mla_pre(x, vecs[0], wcat, g_q, g_kv, wuq, wukv, cs)
    o_a, lse = _mla_attn_fwd(q, k, v)
    y1a, xm_a, h2a = _post_attn(o_a, x, wts["mla_w_o"], zero_bias, vecs[0])
    a_a, y2a, x1 = _mlp_fwd(h2a, wts["w_ff1"][0], wts["w_ff2"][0], xm_a, vecs[0])

    h1b, qs, ks, vs = _swa_pre(x1, vecs[1], wts["swa_w_qkv"], wts["swa_b_qkv"])
    o_b = _swa_attn_fwd(qs, ks, vs, bias, sinks_b)
    y1b, xm_b, h2b = _post_attn(o_b, x1, wts["swa_w_o"], wts["swa_b_o"], vecs[1])
    a_b, y2b, x2 = _mlp_fwd(h2b, wts["w_ff1"][1], wts["w_ff2"][1], xm_b, vecs[1])

    loss8, dx2, dg_final = _final_loss(x2, tgt, g_final.reshape(1, D))

    du_b, dy2b, dxm_b, ps_mlp_b = _mlp_bwd(dx2, y2b, a_b, wts["w_ff1"][1], wts["w_ff2"][1], xm_b, vecs[1])
    g_ff2_b = _mm_tn(a_b, dy2b, "dw_ff2")
    g_ff1_b = _mm_tn(h2b, du_b, "dw_ff1")
    dy1b, do_b, ps_out_b = _attn_out_bwd(dxm_b, y1b, o_b, wts["swa_w_o"], vecs[1], False)
    g_swa_o = _mm_tn(o_b, dy1b, "dw_o")
    dqs, dks, dvs, dsinks = _swa_attn_bwd(qs, ks, vs, do_b, bias, sinks_b)
    dx1, dqkv, ps_pre_b, g_swa_bqkv = _swa_pre_bwd(x1, dxm_b, vecs[1], dqs, dks, dvs, wts["swa_w_qkv"])
    g_swa_qkv = _mm_tn(h1b, dqkv, "dw_qkv")

    du_a, dy2a, dxm_a, ps_mlp_a = _mlp_bwd(dx1, y2a, a_a, wts["w_ff1"][0], wts["w_ff2"][0], xm_a, vecs[0])
    g_ff2_a = _mm_tn(a_a, dy2a, "dw_ff2")
    g_ff1_a = _mm_tn(h2a, du_a, "dw_ff1")
    dy1a, do_a, ps_out_a, delta = _attn_out_bwd(dxm_a, y1a, o_a, wts["mla_w_o"], vecs[0], True)
    g_mla_o = _mm_tn(o_a, dy1a, "dw_o")
    dq, dk, dv = _mla_attn_bwd(q, k, v, do_a, lse, delta)
    dx0, ps_pre_a, dg_q, dg_kv, dwcat, dwuq, dwukv = _mla_pre_bwd(
        x, dxm_a, vecs[0], h1a, z, dq, dk, dv, cs, wcat, g_q, g_kv, wuq, wukv)

    c0, c1, c2 = Q_LORA, Q_LORA + KV_LORA, Q_LORA + KV_LORA + QK_ROPE
    g_dq = dwcat[:, :c0]
    g_dkv = jnp.concatenate([dwcat[:, c0:c1], _unrot_grad(dwcat[:, c1:c2], dwcat[:, c2:])], axis=1)
    e0 = QK_NOPE + QK_ROPE
    g_uq = jnp.concatenate([dwuq[..., :QK_NOPE], _unrot_grad(dwuq[..., QK_NOPE:e0], dwuq[..., e0:])], axis=-1)
    g_uq = g_uq.transpose(1, 0, 2).reshape(Q_LORA, H * e0)
    g_ukv = dwukv.transpose(1, 0, 2).reshape(KV_LORA, H * (QK_NOPE + V_DIM))

    def dmod(ps_pre, ps_out, ps_mlp):
        return jnp.concatenate([ps_pre[R_SH1:R_SC1 + 1], ps_out[R_GT1:R_GT1 + 1], ps_mlp[R_SH2:R_GT2 + 1]], axis=0)

    grads = {
        "mla_w_dq": g_dq, "mla_w_uq": g_uq, "mla_w_dkv": g_dkv, "mla_w_ukv": g_ukv, "mla_w_o": g_mla_o,
        "swa_w_qkv": g_swa_qkv, "swa_b_qkv": g_swa_bqkv, "swa_w_o": g_swa_o, "swa_b_o": ps_out_b[R_BO:R_BO + 1],
        "w_ff1": jnp.stack([g_ff1_a, g_ff1_b]), "w_ff2": jnp.stack([g_ff2_a, g_ff2_b]),
    }
    small = {
        "dmod": jnp.stack([dmod(ps_pre_a, ps_out_a, ps_mlp_a), dmod(ps_pre_b, ps_out_b, ps_mlp_b)]).reshape(DEPTH, 6 * D),
        "g_mix": jnp.stack([ps_pre_a[R_GMIX], ps_pre_b[R_GMIX]]),
        "g_mlp": jnp.stack([ps_mlp_a[R_GMLP], ps_mlp_b[R_GMLP]]),
        "mla_g_q": dg_q, "mla_g_kv": dg_kv, "swa_sinks": dsinks[:, 0].reshape(1, SWA_HEADS),
        "g_final": dg_final.reshape(D), "loss": loss8[0, 0],
    }
    return dx0, grads, small


SHARDED = {
    "mla_w_dq": ((1, D // N_CHIPS, Q_LORA), 1),
    "mla_w_uq": ((1, Q_LORA, MLA_HEADS * (QK_NOPE + QK_ROPE) // N_CHIPS), 2),
    "mla_w_dkv": ((1, D // N_CHIPS, KV_LORA + QK_ROPE), 1),
    "mla_w_ukv": ((1, KV_LORA, MLA_HEADS * (QK_NOPE + V_DIM) // N_CHIPS), 2),
    "mla_w_o": ((1, MLA_HEADS * V_DIM // N_CHIPS, D), 1),
    "swa_w_qkv": ((1, D, (SWA_HEADS + 2 * SWA_KV_HEADS) * SWA_HEAD_DIM // N_CHIPS), 2),
    "swa_b_qkv": ((1, (SWA_HEADS + 2 * SWA_KV_HEADS) * SWA_HEAD_DIM // N_CHIPS), 1),
    "swa_w_o": ((1, SWA_HEADS * SWA_HEAD_DIM // N_CHIPS, D), 1),
    "swa_b_o": ((1, D // N_CHIPS), 1),
    "w_ff1": ((DEPTH, D, D_FF // N_CHIPS), 2),
    "w_ff2": ((DEPTH, D_FF // N_CHIPS, D), 1),
}
PACK_ELEMS = sum(math.prod(s) for s, _ in SHARDED.values())
PACK_ROWS = -(-PACK_ELEMS // (PACK_COLS * 1024)) * 1024
HALF_ROWS = PACK_ROWS // 2

SMALL = {"b_ada": (DEPTH, 6 * D), "g_mix": (DEPTH, D), "g_mlp": (DEPTH, D), "mla_g_q": (1, Q_LORA),
         "mla_g_kv": (1, KV_LORA), "swa_sinks": (1, SWA_HEADS), "g_final": (D,), "loss": ()}
SMALL_ROWS = 144


def _small_slots():
    slots, off = {}, 0
    for name, shape in SMALL.items():
        n = max(math.prod(shape), 1)
        slots[name] = (off, n)
        off += -(-n // LANES) * LANES
    assert off <= SMALL_ROWS * LANES
    return slots


def _pack(shards, dtype):
    flat = [shards[n].astype(dtype).reshape(-1) for n in SHARDED]
    flat.append(jnp.zeros((PACK_ROWS * PACK_COLS - PACK_ELEMS,), dtype))
    return jnp.concatenate(flat).reshape(PACK_ROWS, PACK_COLS)


def _unpack(packed):
    lead = packed.shape[:-2]
    flat = packed.reshape(lead + (PACK_ROWS * PACK_COLS,))
    out, off = {}, 0
    for name, (shape, _) in SHARDED.items():
        n = math.prod(shape)
        out[name] = flat[..., off:off + n].reshape(lead + shape)
        off += n
    return out


def _pack_small(vals):
    slots = _small_slots()
    buf = jnp.zeros((SMALL_ROWS * LANES,), jnp.float32)
    for name, (off, n) in slots.items():
        if name in vals:
            buf = lax.dynamic_update_slice(buf, vals[name].astype(jnp.float32).reshape(-1), (off,))
    return buf.reshape(SMALL_ROWS, LANES)


def _unpack_small(buf):
    flat = buf.reshape(-1)
    return {name: flat[off:off + n].reshape(SMALL[name]) for name, (off, n) in _small_slots().items()}


HBM = pl.BlockSpec(memory_space=pltpu.HBM)
MESH = pl.DeviceIdType.MESH


def _place():
    x, y, c = lax.axis_index("x"), lax.axis_index("y"), lax.axis_index("c")
    chips = [(1 - x, y), (x, 1 - y), (1 - x, 1 - y)]
    return x, y, c, chips


def _all_gather(block):
    m_per, n = block.shape

    def body(x_ref, out_ref, send_sems, recv_sems, local_sem):
        x, y, c, chips = _place()
        me, sibling = (x, y, c), (x, y, 1 - c)

        def rows(px, py, pc):
            return out_ref.at[pl.ds((4 * px + 2 * py + pc) * m_per, m_per), :]

        def copy(k, blk, to, src=None):
            return pltpu.make_async_remote_copy(
                src_ref=rows(*blk) if src is None else src, dst_ref=rows(*blk),
                send_sem=send_sems.at[k], recv_sem=recv_sems.at[k], device_id=to, device_id_type=MESH)

        mine = pltpu.make_async_copy(x_ref, rows(*me), local_sem)
        mine.start()
        first = [copy(0, me, sibling, src=x_ref)]
        first += [copy(1 + j, me, (*chip, c), src=x_ref) for j, chip in enumerate(chips)]
        for cp in first:
            cp.start()
        passed = [copy(4 + j, (*chip, c), sibling) for j, chip in enumerate(chips)]
        for j, chip in enumerate(chips):
            copy(1 + j, (*chip, c), me).wait_recv()
            passed[j].start()
        copy(0, sibling, me).wait_recv()
        for j, chip in enumerate(chips):
            copy(4 + j, (*chip, 1 - c), me).wait_recv()
        for cp in first + passed:
            cp.wait_send()
        mine.wait()

    out = pl.pallas_call(
        body, name="all_gather_small",
        out_shape=jax.ShapeDtypeStruct((N_DEV * m_per, n), block.dtype),
        in_specs=[pl.BlockSpec(memory_space=pltpu.VMEM)],
        out_specs=pl.BlockSpec(memory_space=pltpu.VMEM),
        scratch_shapes=[pltpu.SemaphoreType.DMA((7,)), pltpu.SemaphoreType.DMA((7,)), pltpu.SemaphoreType.DMA],
    )(block)
    return out.reshape(N_DEV, m_per, n)


def _weight_gather(packed):
    R, C = packed.shape
    RH = R // 2

    def body(w_ref, out_ref, send_sems, recv_sems, local_sem):
        x, y, c, chips = _place()
        sibling = (x, y, 1 - c)

        def slab(px, py, half):
            return out_ref.at[2 * px + py, pl.ds(half * RH, RH), :]

        def copy(k, src, dst, to):
            return pltpu.make_async_remote_copy(src_ref=src, dst_ref=dst, send_sem=send_sems.at[k],
                                                recv_sem=recv_sems.at[k], device_id=to, device_id_type=MESH)

        mine = pltpu.make_async_copy(w_ref, out_ref.at[2 * x + y], local_sem)
        mine.start()
        first = [copy(j, w_ref.at[pl.ds(c * RH, RH), :], slab(x, y, c), (*chip, c)) for j, chip in enumerate(chips)]
        for cp in first:
            cp.start()
        passed = [copy(3 + j, slab(*chip, c), slab(*chip, c), sibling) for j, chip in enumerate(chips)]
        for j, chip in enumerate(chips):
            copy(j, slab(*chip, c), slab(*chip, c), (*chip, c)).wait_recv()
            passed[j].start()
        for j, chip in enumerate(chips):
            copy(3 + j, slab(*chip, 1 - c), slab(*chip, 1 - c), sibling).wait_recv()
        for cp in first + passed:
            cp.wait_send()
        mine.wait()

    return pl.pallas_call(
        body, name="weight_gather",
        out_shape=jax.ShapeDtypeStruct((N_CHIPS, R, C), packed.dtype),
        in_specs=[HBM], out_specs=HBM,
        scratch_shapes=[pltpu.SemaphoreType.DMA((6,)), pltpu.SemaphoreType.DMA((6,)), pltpu.SemaphoreType.DMA],
    )(packed)


def _grad_pair_in(g):
    _, R, C = g.shape
    RH = R // 2

    def body(g_ref, own_ref, got_ref, send_sem, recv_sem, local_sem):
        x, y, c, _ = _place()
        keep = pltpu.make_async_copy(g_ref.at[:, pl.ds(c * RH, RH), :], own_ref, local_sem)
        keep.start()
        give = pltpu.make_async_remote_copy(
            src_ref=g_ref.at[:, pl.ds((1 - c) * RH, RH), :], dst_ref=got_ref, send_sem=send_sem, recv_sem=recv_sem,
            device_id=(x, y, 1 - c), device_id_type=MESH)
        give.start()
        give.wait()
        keep.wait()

    shape = jax.ShapeDtypeStruct((N_CHIPS, RH, C), g.dtype)
    return pl.pallas_call(
        body, name="grad_pair_in", out_shape=[shape, shape], in_specs=[HBM], out_specs=[HBM, HBM],
        scratch_shapes=[pltpu.SemaphoreType.DMA, pltpu.SemaphoreType.DMA, pltpu.SemaphoreType.DMA],
    )(g)


def _grad_chip_exchange(a):
    _, RH, C = a.shape

    def body(a_ref, mine_ref, got_ref, send_sems, recv_sems, local_sem):
        x, y, c, chips = _place()
        keep = pltpu.make_async_copy(a_ref.at[2 * x + y], mine_ref, local_sem)
        keep.start()
        sends = [pltpu.make_async_remote_copy(
            src_ref=a_ref.at[2 * cx + cy], dst_ref=got_ref.at[j], send_sem=send_sems.at[j], recv_sem=recv_sems.at[j],
            device_id=(cx, cy, c), device_id_type=MESH) for j, (cx, cy) in enumerate(chips)]
        for cp in sends:
            cp.start()
        for cp in sends:
            cp.wait_recv()
        for cp in sends:
            cp.wait_send()
        keep.wait()

    return pl.pallas_call(
        body, name="grad_chip_exchange",
        out_shape=[jax.ShapeDtypeStruct((RH, C), a.dtype), jax.ShapeDtypeStruct((N_CHIPS - 1, RH, C), a.dtype)],
        in_specs=[HBM], out_specs=[HBM, HBM],
        scratch_shapes=[pltpu.SemaphoreType.DMA((3,)), pltpu.SemaphoreType.DMA((3,)), pltpu.SemaphoreType.DMA],
    )(a)


def _grad_pair_out(h):
    RH, C = h.shape

    def body(h_ref, full_ref, send_sem, recv_sem, local_sem):
        x, y, c, _ = _place()
        keep = pltpu.make_async_copy(h_ref, full_ref.at[c], local_sem)
        keep.start()
        give = pltpu.make_async_remote_copy(src_ref=h_ref, dst_ref=full_ref.at[c], send_sem=send_sem, recv_sem=recv_sem,
                                            device_id=(x, y, 1 - c), device_id_type=MESH)
        give.start()
        give.wait()
        keep.wait()

    return pl.pallas_call(
        body, name="grad_pair_out", out_shape=jax.ShapeDtypeStruct((2, RH, C), h.dtype), in_specs=[HBM], out_specs=HBM,
        scratch_shapes=[pltpu.SemaphoreType.DMA, pltpu.SemaphoreType.DMA, pltpu.SemaphoreType.DMA],
    )(h)


def _sum_rows(parts, name):
    rows, cols = parts[0].shape
    tr = _tile(rows, 1024)

    def body(*refs):
        acc = refs[0][...]
        for r in refs[1:-1]:
            acc = acc + r[...]
        refs[-1][...] = acc

    return pl.pallas_call(
        body, name=name, grid=(rows // tr,), in_specs=[_rows(tr, cols)] * len(parts), out_specs=_rows(tr, cols),
        out_shape=jax.ShapeDtypeStruct((rows, cols), parts[0].dtype), compiler_params=_params("parallel"),
    )(*parts)


def _ada_part(c_all, w_ada):
    L, _, ncol = w_ada.shape
    tn = _tile(ncol, 512)

    def body(c_ref, w_ref, cond_ref, part_ref):
        cv = c_ref[...]
        cond = cv * jax.nn.sigmoid(cv)
        cond_ref[...] = cond
        part_ref[0] = jnp.dot(cond, w_ref[0], precision=lax.Precision.HIGHEST, preferred_element_type=jnp.float32)

    return pl.pallas_call(
        body, name="ada_part", grid=(L, ncol // tn),
        in_specs=[_full((N_DEV, D)), pl.BlockSpec((1, D, tn), lambda l, j: (l, 0, j))],
        out_specs=[_full((N_DEV, D)), pl.BlockSpec((1, N_DEV, tn), lambda l, j: (l, 0, j))],
        out_shape=[jax.ShapeDtypeStruct((N_DEV, D), jnp.float32), jax.ShapeDtypeStruct((L, N_DEV, ncol), jnp.float32)],
        compiler_params=_params("arbitrary", "arbitrary"),
    )(c_all, w_ada)


def _adamw_math(w, g, m, v):
    m = ADAM_B1 * m + (1.0 - ADAM_B1) * g
    v = ADAM_B2 * v + (1.0 - ADAM_B2) * jnp.square(g)
    m_hat = m / (1.0 - ADAM_B1 ** ADAM_STEP)
    v_hat = v / (1.0 - ADAM_B2 ** ADAM_STEP)
    delta = -ADAM_LR * (m_hat / (jnp.sqrt(v_hat) + ADAM_EPS) + ADAM_WD * w)
    return delta, m, v


def _adamw(w, g, m, v, name):
    shape = w.shape
    cols = shape[-1]
    rows = math.prod(shape[:-1])
    tr = _tile(rows, 512)
    two_d = lambda t: t.reshape(rows, cols)

    def body(w_ref, g_ref, m_ref, v_ref, d_ref, mo_ref, vo_ref):
        d_ref[...], mo_ref[...], vo_ref[...] = _adamw_math(w_ref[...], g_ref[...], m_ref[...], v_ref[...])

    out = jax.ShapeDtypeStruct((rows, cols), jnp.float32)
    outs = pl.pallas_call(
        body, name=name, grid=(rows // tr,), in_specs=[_rows(tr, cols)] * 4, out_specs=[_rows(tr, cols)] * 3,
        out_shape=[out, out, out], compiler_params=_params("parallel"),
    )(two_d(w), two_d(g), two_d(m), two_d(v))
    return [t.reshape(shape) for t in outs]


def _ada_grad_adamw(cond_t, dm, w, m, v):
    L, _, ncol = w.shape
    tn = _tile(ncol, 512)

    def body(ct_ref, dm_ref, w_ref, m_ref, v_ref, g_ref, d_ref, mo_ref, vo_ref):
        g = ct_ref[:, 0:1] * dm_ref[0, 0:1, :]
        for b in range(1, N_DEV):
            g = g + ct_ref[:, b:b + 1] * dm_ref[0, b:b + 1, :]
        g_ref[0] = g
        d_ref[0], mo_ref[0], vo_ref[0] = _adamw_math(w_ref[0], g, m_ref[0], v_ref[0])

    wblk = pl.BlockSpec((1, D, tn), lambda l, j: (l, 0, j))
    out = jax.ShapeDtypeStruct(w.shape, jnp.float32)
    return pl.pallas_call(
        body, name="ada_grad_adamw", grid=(L, ncol // tn),
        in_specs=[_full((D, N_DEV)), pl.BlockSpec((1, N_DEV, tn), lambda l, j: (l, 0, j)), wblk, wblk, wblk],
        out_specs=[wblk] * 4, out_shape=[out] * 4, compiler_params=_params("parallel", "parallel"),
    )(cond_t, dm, w, m, v)


def _small_adamw(gathered, w, m, v):
    def body(ga_ref, w_ref, m_ref, v_ref, g_ref, d_ref, mo_ref, vo_ref):
        g = ga_ref[0]
        for dev in range(1, N_DEV):
            g = g + ga_ref[dev]
        g_ref[...] = g
        d_ref[...], mo_ref[...], vo_ref[...] = _adamw_math(w_ref[...], g, m_ref[...], v_ref[...])

    out = jax.ShapeDtypeStruct((SMALL_ROWS, LANES), jnp.float32)
    return pl.pallas_call(
        body, name="small_adamw", out_shape=[out] * 4,
        in_specs=[pl.BlockSpec(memory_space=pltpu.VMEM)] * 4, out_specs=[pl.BlockSpec(memory_space=pltpu.VMEM)] * 4,
    )(gathered, w, m, v)


def _one_hot_pick(arr, index, axis):
    n = arr.shape[axis]
    shape = [1] * arr.ndim
    shape[axis] = n
    hot = (jnp.arange(n) == index).astype(arr.dtype).reshape(shape)
    return jnp.sum(arr * hot, axis=axis)


def kernel(x, c, positions, w_ada, b_ada, g_mix, g_mlp, mla_w_dq, mla_g_q, mla_w_uq, mla_w_dkv, mla_g_kv, mla_w_ukv, mla_w_o, swa_w_qkv, swa_b_qkv, swa_sinks, swa_w_o, swa_b_o, w_ff1, w_ff2, g_final, loss_target, m_w_ada, m_b_ada, m_g_mix, m_g_mlp, m_mla_w_dq, m_mla_g_q, m_mla_w_uq, m_mla_w_dkv, m_mla_g_kv, m_mla_w_ukv, m_mla_w_o, m_swa_w_qkv, m_swa_b_qkv, m_swa_sinks, m_swa_w_o, m_swa_b_o, m_w_ff1, m_w_ff2, m_g_final, v_w_ada, v_b_ada, v_g_mix, v_g_mlp, v_mla_w_dq, v_mla_g_q, v_mla_w_uq, v_mla_w_dkv, v_mla_g_kv, v_mla_w_ukv, v_mla_w_o, v_swa_w_qkv, v_swa_b_qkv, v_swa_sinks, v_swa_w_o, v_swa_b_o, v_w_ff1, v_w_ff2, v_g_final):
    W = dict(w_ada=w_ada, b_ada=b_ada, g_mix=g_mix, g_mlp=g_mlp, mla_w_dq=mla_w_dq, mla_g_q=mla_g_q, mla_w_uq=mla_w_uq,
             mla_w_dkv=mla_w_dkv, mla_g_kv=mla_g_kv, mla_w_ukv=mla_w_ukv, mla_w_o=mla_w_o, swa_w_qkv=swa_w_qkv,
             swa_b_qkv=swa_b_qkv, swa_sinks=swa_sinks, swa_w_o=swa_w_o, swa_b_o=swa_b_o, w_ff1=w_ff1, w_ff2=w_ff2,
             g_final=g_final)
    M = dict(w_ada=m_w_ada, b_ada=m_b_ada, g_mix=m_g_mix, g_mlp=m_g_mlp, mla_w_dq=m_mla_w_dq, mla_g_q=m_mla_g_q,
             mla_w_uq=m_mla_w_uq, mla_w_dkv=m_mla_w_dkv, mla_g_kv=m_mla_g_kv, mla_w_ukv=m_mla_w_ukv, mla_w_o=m_mla_w_o,
             swa_w_qkv=m_swa_w_qkv, swa_b_qkv=m_swa_b_qkv, swa_sinks=m_swa_sinks, swa_w_o=m_swa_w_o, swa_b_o=m_swa_b_o,
             w_ff1=m_w_ff1, w_ff2=m_w_ff2, g_final=m_g_final)
    V = dict(w_ada=v_w_ada, b_ada=v_b_ada, g_mix=v_g_mix, g_mlp=v_g_mlp, mla_w_dq=v_mla_w_dq, mla_g_q=v_mla_g_q,
             mla_w_uq=v_mla_w_uq, mla_w_dkv=v_mla_w_dkv, mla_g_kv=v_mla_g_kv, mla_w_ukv=v_mla_w_ukv, mla_w_o=v_mla_w_o,
             swa_w_qkv=v_swa_w_qkv, swa_b_qkv=v_swa_b_qkv, swa_sinks=v_swa_sinks, swa_w_o=v_swa_w_o, swa_b_o=v_swa_b_o,
             w_ff1=v_w_ff1, w_ff2=v_w_ff2, g_final=v_g_final)
    order = list(W)
    chip = 2 * lax.axis_index("x") + lax.axis_index("y")
    dev = 2 * chip + lax.axis_index("c")

    gathered = _unpack(_weight_gather(_pack({n: W[n] for n in SHARDED}, MXU_DTYPE)))
    wts = {}
    for name, (shape, axis) in SHARDED.items():
        full = jnp.concatenate([gathered[name][i] for i in range(N_CHIPS)], axis=axis)
        wts[name] = full if full.shape[0] == DEPTH and name in ("w_ff1", "w_ff2") else full[0]

    nbq, nbo = SHARDED["swa_b_qkv"][0][1], SHARDED["swa_b_o"][0][1]
    first = jnp.concatenate([c.reshape(-1), swa_b_qkv.reshape(-1), swa_b_o.reshape(-1),
                             jnp.zeros((16 * LANES - D - nbq - nbo,), jnp.float32)]).reshape(16, LANES)
    first_all = _all_gather(first).reshape(N_DEV, 16 * LANES)
    c_all = first_all[:, :D]
    north = first_all[0::2]
    wts["swa_b_qkv"] = north[:, D:D + nbq].reshape(1, N_CHIPS * nbq)
    wts["swa_b_o"] = north[:, D + nbq:D + nbq + nbo].reshape(1, N_CHIPS * nbo)
    cond_all, part = _ada_part(c_all, w_ada)
    ncol = w_ada.shape[2]
    part_all = _all_gather(part.reshape(-1, LANES)).reshape(N_DEV, DEPTH, N_DEV, ncol)
    mine = _one_hot_pick(part_all[0::2], dev, axis=2)
    mod = mine.transpose(1, 0, 2).reshape(DEPTH, N_CHIPS * ncol) + b_ada
    vecs = jnp.concatenate([mod.reshape(DEPTH, 6, D), g_mix[:, None, :], g_mlp[:, None, :]], axis=1)

    grad_x, grads, small = _sequence_step(x[0], loss_target[0], positions[0], vecs, mla_g_q, mla_g_kv, swa_sinks,
                                          g_final, wts)

    small["b_ada"] = small.pop("dmod")
    small_all = _all_gather(_pack_small(small))
    pk = lambda src: _pack_small({n: src[n] for n in SMALL if n != "loss"})
    g_small, d_small, m_small, v_small = [_unpack_small(t) for t in _small_adamw(small_all, pk(W), pk(M), pk(V))]
    off, n = _small_slots()["b_ada"]
    dmod_all = small_all.reshape(N_DEV, -1)[:, off:off + n].reshape(N_DEV, DEPTH, N_CHIPS, ncol)
    dm = _one_hot_pick(dmod_all, chip, axis=2).transpose(1, 0, 2)
    ada = _ada_grad_adamw(cond_all.T, dm, w_ada, m_w_ada, v_w_ada)

    shards = {}
    for name, (shape, axis) in SHARDED.items():
        g = grads[name].reshape((shape[0],) + grads[name].shape[-(len(shape) - 1):])
        shards[name] = jnp.stack(jnp.split(g, N_CHIPS, axis=axis))
    packed = jnp.stack([_pack({n: shards[n][i] for n in SHARDED}, jnp.float32) for i in range(N_CHIPS)])
    own, got = _grad_pair_in(packed)
    two_d = lambda t: t.reshape(-1, PACK_COLS)
    chip_sum = _sum_rows([two_d(own), two_d(got)], "grad_pair_sum").reshape(N_CHIPS, HALF_ROWS, PACK_COLS)
    mine_half, others = _grad_chip_exchange(chip_sum)
    half = _sum_rows([mine_half, others[0], others[1], others[2]], "grad_chip_sum")
    g_shard = _unpack(_grad_pair_out(half).reshape(PACK_ROWS, PACK_COLS))

    res = {}
    for name in order:
        if name == "w_ada":
            res[name] = ada
        elif name in SHARDED:
            res[name] = [g_shard[name]] + _adamw(W[name], g_shard[name], M[name], V[name], "adamw_" + name)
        else:
            res[name] = [t[name] for t in (g_small, d_small, m_small, v_small)]
    outs = [g_small["loss"], grad_x[None]]
    for k in range(4):
        outs += [res[name][k] for name in order]
    return tuple(outs)
```

```python
import functools
import math

import jax
import jax.numpy as jnp
import numpy as np
from jax import lax
from jax.experimental import pallas as pl
from jax.experimental.pallas import tpu as pltpu

D = 1024
DEPTH = 2
MLA_HEADS = 8
QK_NOPE = 128
QK_ROPE = 64
V_DIM = 128
Q_LORA = 384
KV_LORA = 256
ROPE_THETA = 10000.0
SWA_HEADS = 16
SWA_KV_HEADS = 4
SWA_HEAD_DIM = 64
SWA_GROUP = SWA_HEADS // SWA_KV_HEADS
WINDOW = 128
D_FF = 4 * D
EPS = 1e-6
ADAM_LR = 0.001
ADAM_B1 = 0.9
ADAM_B2 = 0.999
ADAM_EPS = 1e-08
ADAM_WD = 0.01
ADAM_STEP = 10

N_CHIPS = 4
N_DEV = 8
LANES = 128
QK_EXT = 256
MLA_SCALE = (QK_NOPE + QK_ROPE) ** -0.5
SWA_SCALE = SWA_HEAD_DIM ** -0.5
NEG = -1e30
MXU_DTYPE = jnp.bfloat16
VMEM_LIMIT = 56 * 1024 * 1024

R_SH1, R_SC1, R_GT1, R_SH2, R_SC2, R_GT2, R_GMIX, R_GMLP = range(8)
R_BO = 6


def _tile(n, pref):
    if n <= pref:
        return n
    for t in range(pref, 7, -1):
        if n % t == 0 and t % 8 == 0:
            return t
    return n


def _dot(a, b):
    return jnp.dot(a, b, preferred_element_type=jnp.float32)


def _dot_nt(a, b):
    return lax.dot_general(a, b, (((1,), (1,)), ((), ())), preferred_element_type=jnp.float32)


def _dot_tn(a, b):
    return lax.dot_general(a, b, (((0,), (0,)), ((), ())), preferred_element_type=jnp.float32)


def _rms(x):
    r = lax.rsqrt(jnp.mean(x * x, axis=-1, keepdims=True) + EPS)
    return x * r, r


def _rms_bwd(dxhat, xhat, r):
    return r * (dxhat - xhat * jnp.mean(dxhat * xhat, axis=-1, keepdims=True))


def _rowsum(v):
    return jnp.sum(v, axis=0, keepdims=True)


def _params(*sem):
    return pltpu.CompilerParams(dimension_semantics=sem, vmem_limit_bytes=VMEM_LIMIT)


def _full(shape):
    nd = len(shape)
    return pl.BlockSpec(shape, lambda *_: (0,) * nd)


def _rows(tm, cols):
    return pl.BlockSpec((tm, cols), lambda i, *_: (i, 0))


def _modulate_bwd(dh, x, vec_ref, r_g, r_sc, r_sh, ps_ref, dres):
    xhat, r = _rms(x)
    g = vec_ref[r_g:r_g + 1, :]
    n = xhat * g
    ps_ref[r_sh:r_sh + 1, :] += _rowsum(dh)
    ps_ref[r_sc:r_sc + 1, :] += _rowsum(dh * n)
    dn = dh * (1.0 + vec_ref[r_sc:r_sc + 1, :])
    ps_ref[r_g:r_g + 1, :] += _rowsum(dn * xhat)
    return dres + _rms_bwd(dn * g, xhat, r)


def _mla_pre(x, vec, wcat, g_q, g_kv, wuq, wukv, cs):
    T = x.shape[0]
    tm = _tile(T, 512)
    H = MLA_HEADS

    def body(x_ref, vec_ref, wcat_ref, gq_ref, gkv_ref, wuq_ref, wukv_ref, cs_ref, h_ref, z_ref, q_ref, k_ref, v_ref):
        xhat, _ = _rms(x_ref[...])
        h = xhat * vec_ref[R_GMIX:R_GMIX + 1, :] * (1.0 + vec_ref[R_SC1:R_SC1 + 1, :]) + vec_ref[R_SH1:R_SH1 + 1, :]
        hb = h.astype(MXU_DTYPE)
        h_ref[...] = hb
        z = _dot(hb, wcat_ref[...])
        z_ref[...] = z
        cq = (_rms(z[:, :Q_LORA])[0] * gq_ref[...]).astype(MXU_DTYPE)
        ckv = (_rms(z[:, Q_LORA:Q_LORA + KV_LORA])[0] * gkv_ref[...]).astype(MXU_DTYPE)
        cs_t = cs_ref[...]
        t = z[:, Q_LORA + KV_LORA:] * cs_t
        k_rope = (t + pltpu.roll(t, QK_ROPE, axis=1)).astype(MXU_DTYPE)
        low = lax.broadcasted_iota(jnp.int32, (1, LANES), 1) < QK_ROPE
        for hd in range(H):
            qf = _dot(cq, wuq_ref[hd])
            tq = qf[:, QK_NOPE:] * cs_t
            tq = tq + pltpu.roll(tq, QK_ROPE, axis=1)
            q_ref[hd, :, :QK_NOPE] = qf[:, :QK_NOPE].astype(MXU_DTYPE)
            q_ref[hd, :, QK_NOPE:] = jnp.where(low, tq, 0.0).astype(MXU_DTYPE)
            kvf = _dot(ckv, wukv_ref[hd])
            k_ref[hd, :, :QK_NOPE] = kvf[:, :QK_NOPE].astype(MXU_DTYPE)
            k_ref[hd, :, QK_NOPE:] = k_rope
            v_ref[hd] = kvf[:, QK_NOPE:].astype(MXU_DTYPE)

    zc = wcat.shape[1]
    return pl.pallas_call(
        body, name="mla_pre", grid=(T // tm,),
        in_specs=[_rows(tm, D), _full((8, D)), _full(wcat.shape), _full(g_q.shape), _full(g_kv.shape),
                  _full(wuq.shape), _full(wukv.shape), _rows(tm, LANES)],
        out_specs=[_rows(tm, D), _rows(tm, zc),
                   pl.BlockSpec((H, tm, QK_EXT), lambda i: (0, i, 0)),
                   pl.BlockSpec((H, tm, QK_EXT), lambda i: (0, i, 0)),
                   pl.BlockSpec((H, tm, V_DIM), lambda i: (0, i, 0))],
        out_shape=[jax.ShapeDtypeStruct((T, D), MXU_DTYPE), jax.ShapeDtypeStruct((T, zc), jnp.float32),
                   jax.ShapeDtypeStruct((H, T, QK_EXT), MXU_DTYPE), jax.ShapeDtypeStruct((H, T, QK_EXT), MXU_DTYPE),
                   jax.ShapeDtypeStruct((H, T, V_DIM), MXU_DTYPE)],
        compiler_params=_params("parallel"),
    )(x, vec, wcat, g_q, g_kv, wuq, wukv, cs)


def _mla_attn_fwd(q, k, v):
    H, T, _ = q.shape
    tq = _tile(T, 512)
    nq = T // tq

    def body(q_ref, k_ref, v_ref, o_ref, lse_ref, m_sc, l_sc, acc_sc):
        qi, kj = pl.program_id(1), pl.program_id(2)

        @pl.when(kj == 0)
        def _():
            m_sc[...] = jnp.full_like(m_sc, NEG)
            l_sc[...] = jnp.zeros_like(l_sc)
            acc_sc[...] = jnp.zeros_like(acc_sc)

        def step(masked):
            s = _dot_nt(q_ref[0], k_ref[0]) * MLA_SCALE
            if masked:
                row = lax.broadcasted_iota(jnp.int32, (tq, tq), 0)
                col = lax.broadcasted_iota(jnp.int32, (tq, tq), 1)
                s = jnp.where(col <= row, s, NEG)
            m_prev = m_sc[...]
            m_new = jnp.maximum(m_prev, jnp.max(s, axis=1, keepdims=True))
            alpha = jnp.exp(m_prev - m_new)
            p = jnp.exp(s - m_new)
            l_sc[...] = alpha * l_sc[...] + jnp.sum(p, axis=1, keepdims=True)
            acc_sc[...] = alpha * acc_sc[...] + _dot(p.astype(MXU_DTYPE), v_ref[0])
            m_sc[...] = m_new

        @pl.when(kj < qi)
        def _():
            step(False)

        @pl.when(kj == qi)
        def _():
            step(True)
            l = l_sc[...]
            o_ref[...] = (acc_sc[...] / l).astype(o_ref.dtype)
            lse_ref[0] = jnp.broadcast_to(m_sc[...] + jnp.log(l), (tq, LANES))

    kv_idx = lambda h, i, j: (h, jnp.minimum(i, j), 0)
    return pl.pallas_call(
        body, name="mla_attn_fwd", grid=(H, nq, nq),
        in_specs=[pl.BlockSpec((1, tq, QK_EXT), lambda h, i, j: (h, i, 0)),
                  pl.BlockSpec((1, tq, QK_EXT), kv_idx),
                  pl.BlockSpec((1, tq, V_DIM), kv_idx)],
        out_specs=[pl.BlockSpec((tq, V_DIM), lambda h, i, j: (i, h)),
                   pl.BlockSpec((1, tq, LANES), lambda h, i, j: (h, i, 0))],
        out_shape=[jax.ShapeDtypeStruct((T, H * V_DIM), MXU_DTYPE), jax.ShapeDtypeStruct((H, T, LANES), jnp.float32)],
        scratch_shapes=[pltpu.VMEM((tq, 1), jnp.float32), pltpu.VMEM((tq, 1), jnp.float32),
                        pltpu.VMEM((tq, V_DIM), jnp.float32)],
        compiler_params=_params("parallel", "parallel", "arbitrary"),
    )(q, k, v)


def _post_attn(o, x, w_o, bias, vec):
    T = x.shape[0]
    tm = _tile(T, 512)

    def body(o_ref, x_ref, w_ref, b_ref, vec_ref, y_ref, xm_ref, h_ref):
        y = _dot(o_ref[...], w_ref[...]) + b_ref[...]
        y_ref[...] = y.astype(y_ref.dtype)
        xm = x_ref[...] + vec_ref[R_GT1:R_GT1 + 1, :] * y
        xm_ref[...] = xm
        xhat, _ = _rms(xm)
        h = xhat * vec_ref[R_GMLP:R_GMLP + 1, :] * (1.0 + vec_ref[R_SC2:R_SC2 + 1, :]) + vec_ref[R_SH2:R_SH2 + 1, :]
        h_ref[...] = h.astype(h_ref.dtype)

    return pl.pallas_call(
        body, name="post_attn", grid=(T // tm,),
        in_specs=[_rows(tm, D), _rows(tm, D), _full((D, D)), _full((1, D)), _full((8, D))],
        out_specs=[_rows(tm, D), _rows(tm, D), _rows(tm, D)],
        out_shape=[jax.ShapeDtypeStruct((T, D), MXU_DTYPE), jax.ShapeDtypeStruct((T, D), jnp.float32),
                   jax.ShapeDtypeStruct((T, D), MXU_DTYPE)],
        compiler_params=_params("parallel"),
    )(o, x, w_o, bias, vec)


def _ff_specs(tf, layer):
    per = D_FF // N_CHIPS // tf
    w1 = pl.BlockSpec((None, None, D, tf), lambda i, f: (f // per, layer, 0, f % per))
    w2 = pl.BlockSpec((None, None, tf, D), lambda i, f: (f // per, layer, f % per, 0))
    return w1, w2


def _mlp_fwd(h2, w1, w2, layer, xm, vec):
    T = h2.shape[0]
    tm = _tile(T, 1024)
    tf = _tile(D_FF // N_CHIPS, 512)
    nf = D_FF // tf
    w1_spec, w2_spec = _ff_specs(tf, layer)

    def body(h_ref, w1_ref, w2_ref, xm_ref, vec_ref, a_ref, y_ref, xo_ref, acc):
        f = pl.program_id(1)

        @pl.when(f == 0)
        def _():
            acc[...] = jnp.zeros_like(acc)

        u = jnp.maximum(_dot(h_ref[...], w1_ref[...]), 0.0)
        ab = (u * u).astype(MXU_DTYPE)
        a_ref[...] = ab
        acc[...] += _dot(ab, w2_ref[...])

        @pl.when(f == nf - 1)
        def _():
            y = acc[...]
            y_ref[...] = y.astype(y_ref.dtype)
            xo_ref[...] = xm_ref[...] + vec_ref[R_GT2:R_GT2 + 1, :] * y

    return pl.pallas_call(
        body, name="mlp_fwd", grid=(T // tm, nf),
        in_specs=[_rows(tm, D), w1_spec, w2_spec, _rows(tm, D), _full((8, D))],
        out_specs=[pl.BlockSpec((tm, tf), lambda i, f: (i, f)), _rows(tm, D), _rows(tm, D)],
        out_shape=[jax.ShapeDtypeStruct((T, D_FF), MXU_DTYPE), jax.ShapeDtypeStruct((T, D), MXU_DTYPE),
                   jax.ShapeDtypeStruct((T, D), jnp.float32)],
        scratch_shapes=[pltpu.VMEM((tm, D), jnp.float32)],
        compiler_params=_params("parallel", "arbitrary"),
    )(h2, w1, w2, xm, vec)


def _swa_pre(x, vec, w_qkv, b_qkv):
    T = x.shape[0]
    tm = _tile(T, 512)
    nq = SWA_HEADS * SWA_HEAD_DIM
    nk = SWA_KV_HEADS * SWA_HEAD_DIM

    def body(x_ref, vec_ref, w_ref, b_ref, h_ref, q_ref, k_ref, v_ref):
        xhat, _ = _rms(x_ref[...])
        h = xhat * vec_ref[R_GMIX:R_GMIX + 1, :] * (1.0 + vec_ref[R_SC1:R_SC1 + 1, :]) + vec_ref[R_SH1:R_SH1 + 1, :]
        hb = h.astype(MXU_DTYPE)
        h_ref[...] = hb
        qkv = _dot(hb, w_ref[...]) + b_ref[...]
        q_ref[...] = (qkv[:, :nq] * SWA_SCALE).astype(MXU_DTYPE)
        k_ref[...] = qkv[:, nq:nq + nk].astype(MXU_DTYPE)
        v_ref[...] = qkv[:, nq + nk:].astype(MXU_DTYPE)

    return pl.pallas_call(
        body, name="swa_pre", grid=(T // tm,),
        in_specs=[_rows(tm, D), _full((8, D)), _full(w_qkv.shape), _full(b_qkv.shape)],
        out_specs=[_rows(tm, D), _rows(tm, nq), _rows(tm, nk), _rows(tm, nk)],
        out_shape=[jax.ShapeDtypeStruct((T, D), MXU_DTYPE), jax.ShapeDtypeStruct((T, nq), MXU_DTYPE),
                   jax.ShapeDtypeStruct((T, nk), MXU_DTYPE), jax.ShapeDtypeStruct((T, nk), MXU_DTYPE)],
        compiler_params=_params("parallel"),
    )(x, vec, w_qkv, b_qkv)


def _swa_bias():
    W = WINDOW
    slopes = 2.0 ** (-8.0 * np.arange(1, SWA_HEADS + 1) / SWA_HEADS)
    dist = W + np.arange(W)[:, None] - np.arange(2 * W)[None, :]
    inside = (dist >= 0) & (dist < W)
    bias = np.where(inside[None], -slopes[:, None, None] * dist[None].astype(np.float64), NEG)
    return jnp.asarray(bias.reshape(SWA_KV_HEADS, SWA_GROUP * W, 2 * W), jnp.float32)


def _swa_probs(n, kh, q_ref, kp_ref, kc_ref, bias_ref, sink_ref):
    W, Dh, G = WINDOW, SWA_HEAD_DIM, SWA_GROUP
    qs = jnp.concatenate([q_ref[:, (kh * G + g) * Dh:(kh * G + g + 1) * Dh] for g in range(G)], axis=0)
    kb = jnp.concatenate([kp_ref[:, kh * Dh:(kh + 1) * Dh], kc_ref[:, kh * Dh:(kh + 1) * Dh]], axis=0)
    s = _dot_nt(qs, kb) + bias_ref[kh]
    col = lax.broadcasted_iota(jnp.int32, (1, 2 * W), 1)
    s = jnp.where((col >= W) | (n > 0), s, NEG)
    sink = jnp.concatenate(
        [jnp.broadcast_to(sink_ref[kh * G + g:kh * G + g + 1, :1], (W, 1)) for g in range(G)], axis=0)
    m = jnp.maximum(jnp.max(s, axis=1, keepdims=True), sink)
    p = jnp.exp(s - m)
    p_sink = jnp.exp(sink - m)
    inv = 1.0 / (jnp.sum(p, axis=1, keepdims=True) + p_sink)
    return qs, kb, p * inv, p_sink * inv


def _swa_attn_fwd(q, k, v, bias, sinks_b):
    T = q.shape[0]
    W, Dh, G, Hk = WINDOW, SWA_HEAD_DIM, SWA_GROUP, SWA_KV_HEADS
    nk = Hk * Dh

    def body(q_ref, kp_ref, kc_ref, vp_ref, vc_ref, bias_ref, sink_ref, o_ref):
        n = pl.program_id(0)
        outs = []
        for kh in range(Hk):
            _, _, pn, _ = _swa_probs(n, kh, q_ref, kp_ref, kc_ref, bias_ref, sink_ref)
            vb = jnp.concatenate([vp_ref[:, kh * Dh:(kh + 1) * Dh], vc_ref[:, kh * Dh:(kh + 1) * Dh]], axis=0)
            o = _dot(pn.astype(MXU_DTYPE), vb)
            outs += [o[g * W:(g + 1) * W] for g in range(G)]
        o_ref[...] = jnp.concatenate(outs, axis=1).astype(o_ref.dtype)

    prev = lambda n: (jnp.maximum(n - 1, 0), 0)
    cur = lambda n: (n, 0)
    return pl.pallas_call(
        body, name="swa_attn_fwd", grid=(T // W,),
        in_specs=[pl.BlockSpec((W, D), cur), pl.BlockSpec((W, nk), prev), pl.BlockSpec((W, nk), cur),
                  pl.BlockSpec((W, nk), prev), pl.BlockSpec((W, nk), cur), _full(bias.shape), _full(sinks_b.shape)],
        out_specs=pl.BlockSpec((W, D), cur),
        out_shape=jax.ShapeDtypeStruct((T, D), MXU_DTYPE),
        compiler_params=_params("parallel"),
    )(q, k, k, v, v, bias, sinks_b)


def _final_loss(x, tgt, g):
    T = x.shape[0]
    tm = _tile(T, 512)

    def body(x_ref, t_ref, g_ref, loss_ref, dx_ref, dg_ref):
        @pl.when(pl.program_id(0) == 0)
        def _():
            loss_ref[...] = jnp.zeros_like(loss_ref)
            dg_ref[...] = jnp.zeros_like(dg_ref)

        xhat, r = _rms(x_ref[...])
        gv = g_ref[...]
        e = xhat * gv - t_ref[...]
        loss_ref[...] += 0.5 * jnp.sum(jnp.mean(e * e, axis=-1, keepdims=True), axis=0, keepdims=True)
        dy = e * (1.0 / D)
        dg_ref[...] += _rowsum(dy * xhat)
        dx_ref[...] = _rms_bwd(dy * gv, xhat, r)

    return pl.pallas_call(
        body, name="final_loss", grid=(T // tm,),
        in_specs=[_rows(tm, D), _rows(tm, D), _full((1, D))],
        out_specs=[_full((8, LANES)), _rows(tm, D), _full((1, D))],
        out_shape=[jax.ShapeDtypeStruct((8, LANES), jnp.float32), jax.ShapeDtypeStruct((T, D), jnp.float32),
                   jax.ShapeDtypeStruct((1, D), jnp.float32)],
        compiler_params=_params("arbitrary"),
    )(x, tgt, g)


def _mlp_bwd(dxo, y2, a, w1, w2, layer, xm, vec):
    T = dxo.shape[0]
    tm = _tile(T, 1024)
    tf = _tile(D_FF // N_CHIPS, 512)
    nf = D_FF // tf
    w1_spec, w2_spec = _ff_specs(tf, layer)

    def body(dxo_ref, y_ref, a_ref, w1_ref, w2_ref, xm_ref, vec_ref, du_ref, dy_ref, dxm_ref, ps_ref, dyb, acc):
        i, f = pl.program_id(0), pl.program_id(1)

        @pl.when((i == 0) & (f == 0))
        def _():
            ps_ref[...] = jnp.zeros_like(ps_ref)

        @pl.when(f == 0)
        def _():
            dxo_t = dxo_ref[...]
            d = (dxo_t * vec_ref[R_GT2:R_GT2 + 1, :]).astype(MXU_DTYPE)
            dyb[...] = d
            dy_ref[...] = d
            acc[...] = jnp.zeros_like(acc)
            ps_ref[R_GT2:R_GT2 + 1, :] += _rowsum(dxo_t * y_ref[...].astype(jnp.float32))

        da = _dot_nt(dyb[...], w2_ref[...])
        dub = (da * (2.0 * jnp.sqrt(a_ref[...].astype(jnp.float32)))).astype(MXU_DTYPE)
        du_ref[...] = dub
        acc[...] += _dot_nt(dub, w1_ref[...])

        @pl.when(f == nf - 1)
        def _():
            dxm_ref[...] = _modulate_bwd(acc[...], xm_ref[...], vec_ref, R_GMLP, R_SC2, R_SH2, ps_ref, dxo_ref[...])

    return pl.pallas_call(
        body, name="mlp_bwd", grid=(T // tm, nf),
        in_specs=[_rows(tm, D), _rows(tm, D), pl.BlockSpec((tm, tf), lambda i, f: (i, f)), w1_spec, w2_spec,
                  _rows(tm, D), _full((8, D))],
        out_specs=[pl.BlockSpec((tm, tf), lambda i, f: (i, f)), _rows(tm, D), _rows(tm, D), _full((8, D))],
        out_shape=[jax.ShapeDtypeStruct((T, D_FF), MXU_DTYPE), jax.ShapeDtypeStruct((T, D), MXU_DTYPE),
                   jax.ShapeDtypeStruct((T, D), jnp.float32), jax.ShapeDtypeStruct((8, D), jnp.float32)],
        scratch_shapes=[pltpu.VMEM((tm, D), MXU_DTYPE), pltpu.VMEM((tm, D), jnp.float32)],
        compiler_params=_params("arbitrary", "arbitrary"),
    )(dxo, y2, a, w1, w2, xm, vec)


def _mm_tn(a, g, name, split=None, layers=1, layer=0, into=None):
    T, K = a.shape
    N = g.shape[1]
    kq = K // N_CHIPS if split == "rows" else K
    nq = N // N_CHIPS if split == "cols" else N
    bk, bn, bt = _tile(kq, 1024), _tile(nq, 1024), _tile(T, 1024)
    if nq % bn or bn % LANES:
        bn = nq
    kper, nper = kq // bk, nq // bn

    def body(*refs):
        a_ref, g_ref, o_ref = refs[0], refs[1], refs[-1]

        @pl.when(pl.program_id(2) == 0)
        def _():
            o_ref[...] = jnp.zeros_like(o_ref)

        o_ref[...] += _dot_tn(a_ref[...], g_ref[...])

    in_specs = [pl.BlockSpec((bt, bk), lambda k, n, t: (t, k)), pl.BlockSpec((bt, bn), lambda k, n, t: (t, n))]
    args = [a, g]
    aliases = {}
    if split is None:
        out_spec = pl.BlockSpec((bk, bn), lambda k, n, t: (k, n))
        out_shape = jax.ShapeDtypeStruct((K, N), jnp.float32)
    else:
        if split == "cols":
            idx = lambda k, n, t: (n // nper, layer, k, n % nper)
        else:
            idx = lambda k, n, t: (k // kper, layer, k % kper, n)
        out_spec = pl.BlockSpec((None, None, bk, bn), idx)
        out_shape = jax.ShapeDtypeStruct((N_CHIPS, layers, kq, nq), jnp.float32)
        if into is not None:
            in_specs.append(pl.BlockSpec(memory_space=pl.ANY))
            args.append(into)
            aliases = {2: 0}
    return pl.pallas_call(
        body, name=name, grid=(K // bk, N // bn, T // bt), in_specs=in_specs, out_specs=out_spec, out_shape=out_shape,
        input_output_aliases=aliases, compiler_params=_params("parallel", "parallel", "arbitrary"),
    )(*args)


def _attn_out_bwd(dxm, y1, o, w_o, vec, with_delta):
    T = dxm.shape[0]
    tm = _tile(T, 512)
    H = MLA_HEADS

    def body(dxm_ref, y_ref, o_ref, w_ref, vec_ref, dy_ref, do_ref, ps_ref, *delta_ref):
        @pl.when(pl.program_id(0) == 0)
        def _():
            ps_ref[...] = jnp.zeros_like(ps_ref)

        dxm_t = dxm_ref[...]
        dy = dxm_t * vec_ref[R_GT1:R_GT1 + 1, :]
        ps_ref[R_GT1:R_GT1 + 1, :] += _rowsum(dxm_t * y_ref[...].astype(jnp.float32))
        ps_ref[R_BO:R_BO + 1, :] += _rowsum(dy)
        dyb = dy.astype(MXU_DTYPE)
        dy_ref[...] = dyb
        do = _dot_nt(dyb, w_ref[...])
        do_ref[...] = do.astype(do_ref.dtype)
        if with_delta:
            of = o_ref[...].astype(jnp.float32)
            for hd in range(H):
                sl = slice(hd * V_DIM, (hd + 1) * V_DIM)
                d = jnp.sum(do[:, sl] * of[:, sl], axis=1, keepdims=True)
                delta_ref[0][hd] = jnp.broadcast_to(d, (tm, LANES))

    out_specs = [_rows(tm, D), _rows(tm, D), _full((8, D))]
    out_shape = [jax.ShapeDtypeStruct((T, D), MXU_DTYPE), jax.ShapeDtypeStruct((T, D), MXU_DTYPE),
                 jax.ShapeDtypeStruct((8, D), jnp.float32)]
    if with_delta:
        out_specs.append(pl.BlockSpec((H, tm, LANES), lambda i: (0, i, 0)))
        out_shape.append(jax.ShapeDtypeStruct((H, T, LANES), jnp.float32))
    return pl.pallas_call(
        body, name="attn_out_bwd_mla" if with_delta else "attn_out_bwd_swa", grid=(T // tm,),
        in_specs=[_rows(tm, D), _rows(tm, D), _rows(tm, D), _full((D, D)), _full((8, D))],
        out_specs=out_specs, out_shape=out_shape,
        compiler_params=_params("arbitrary"),
    )(dxm, y1, o, w_o, vec)


def _mla_attn_bwd(q, k, v, do, lse, delta):
    H, T, _ = q.shape
    tq = _tile(T, 512)
    nq = T // tq

    def body(q_ref, k_ref, v_ref, do_ref, lse_ref, dl_ref, dq_ref, dk_ref, dv_ref, dk_acc, dv_acc):
        j, i = pl.program_id(1), pl.program_id(2)

        @pl.when((j == 0) & (i == 0))
        def _():
            dq_ref[...] = jnp.zeros_like(dq_ref)

        def step(masked):
            qb, kb, dob = q_ref[0], k_ref[0], do_ref[...]
            s = _dot_nt(qb, kb) * MLA_SCALE
            if masked:
                row = lax.broadcasted_iota(jnp.int32, (tq, tq), 0)
                col = lax.broadcasted_iota(jnp.int32, (tq, tq), 1)
                s = jnp.where(col <= row, s, NEG)
            p = jnp.exp(s - lse_ref[0][:, :1])
            dv_new = _dot_tn(p.astype(MXU_DTYPE), dob)
            dp = _dot_nt(dob, v_ref[0])
            dsb = (p * (dp - dl_ref[0][:, :1]) * MLA_SCALE).astype(MXU_DTYPE)
            rows = pl.ds(pl.multiple_of(i * tq, tq), tq)
            dq_ref[0, rows, :] += _dot(dsb, kb)
            dk_new = _dot_tn(dsb, qb)
            if masked:
                dk_acc[...] = dk_new
                dv_acc[...] = dv_new
            else:
                dk_acc[...] += dk_new
                dv_acc[...] += dv_new

        @pl.when(i == j)
        def _():
            step(True)

        @pl.when(i > j)
        def _():
            step(False)

        @pl.when(i == nq - 1)
        def _():
            dk_ref[0] = dk_acc[...].astype(dk_ref.dtype)
            dv_ref[0] = dv_acc[...].astype(dv_ref.dtype)

    q_idx = lambda h, j, i: (h, jnp.maximum(i, j), 0)
    kv_idx = lambda h, j, i: (h, j, 0)
    return pl.pallas_call(
        body, name="mla_attn_bwd", grid=(H, nq, nq),
        in_specs=[pl.BlockSpec((1, tq, QK_EXT), q_idx), pl.BlockSpec((1, tq, QK_EXT), kv_idx),
                  pl.BlockSpec((1, tq, V_DIM), kv_idx),
                  pl.BlockSpec((tq, V_DIM), lambda h, j, i: (jnp.maximum(i, j), h)),
                  pl.BlockSpec((1, tq, LANES), q_idx), pl.BlockSpec((1, tq, LANES), q_idx)],
        out_specs=[pl.BlockSpec((1, T, QK_EXT), lambda h, j, i: (h, 0, 0)),
                   pl.BlockSpec((1, tq, QK_EXT), kv_idx), pl.BlockSpec((1, tq, V_DIM), kv_idx)],
        out_shape=[jax.ShapeDtypeStruct((H, T, QK_EXT), jnp.float32), jax.ShapeDtypeStruct((H, T, QK_EXT), MXU_DTYPE),
                   jax.ShapeDtypeStruct((H, T, V_DIM), MXU_DTYPE)],
        scratch_shapes=[pltpu.VMEM((tq, QK_EXT), jnp.float32), pltpu.VMEM((tq, V_DIM), jnp.float32)],
        compiler_params=_params("parallel", "arbitrary", "arbitrary"),
    )(q, k, v, do, lse, delta)


def _mla_pre_bwd(x, dxm, vec, hb, z, dq, dk, dv, cs, wcat, g_q, g_kv, wuq, wukv):
    T = x.shape[0]
    tm = _tile(T, 256)
    H = MLA_HEADS
    zc = wcat.shape[1]

    def body(x_ref, dxm_ref, vec_ref, h_ref, z_ref, dq_ref, dk_ref, dv_ref, cs_ref, wcat_ref, gq_ref, gkv_ref,
             wuq_ref, wukv_ref, dx_ref, ps_ref, dgq_ref, dgkv_ref, dwcat_ref, dwuq_ref, dwukv_ref):
        @pl.when(pl.program_id(0) == 0)
        def _():
            for ref in (ps_ref, dgq_ref, dgkv_ref, dwcat_ref, dwuq_ref, dwukv_ref):
                ref[...] = jnp.zeros_like(ref)

        z = z_ref[...]
        cs_t = cs_ref[...]
        cqhat, rq = _rms(z[:, :Q_LORA])
        ckhat, rk = _rms(z[:, Q_LORA:Q_LORA + KV_LORA])
        gq, gkv = gq_ref[...], gkv_ref[...]
        cq = (cqhat * gq).astype(MXU_DTYPE)
        ckv = (ckhat * gkv).astype(MXU_DTYPE)
        dcq = jnp.zeros((tm, Q_LORA), jnp.float32)
        dckv = jnp.zeros((tm, KV_LORA), jnp.float32)
        dkr = jnp.zeros((tm, LANES), jnp.float32)
        for hd in range(H):
            dqh = dq_ref[hd]
            gqh = jnp.concatenate([dqh[:, :QK_NOPE], dqh[:, QK_NOPE:] * cs_t], axis=1).astype(MXU_DTYPE)
            dcq += _dot_nt(gqh, wuq_ref[hd])
            dwuq_ref[hd] += _dot_tn(cq, gqh)
            dkh = dk_ref[hd]
            gkvh = jnp.concatenate([dkh[:, :QK_NOPE], dv_ref[hd]], axis=1)
            dckv += _dot_nt(gkvh, wukv_ref[hd])
            dwukv_ref[hd] += _dot_tn(ckv, gkvh)
            dkr += dkh[:, QK_NOPE:].astype(jnp.float32)
        dgq_ref[...] += _rowsum(dcq * cqhat)
        dgkv_ref[...] += _rowsum(dckv * ckhat)
        dcq_pre = _rms_bwd(dcq * gq, cqhat, rq)
        dckv_pre = _rms_bwd(dckv * gkv, ckhat, rk)
        dkr2 = (dkr + pltpu.roll(dkr, QK_ROPE, axis=1)) * cs_t
        dz = jnp.concatenate([dcq_pre, dckv_pre, dkr2], axis=1).astype(MXU_DTYPE)
        dwcat_ref[...] += _dot_tn(h_ref[...], dz)
        dh = _dot_nt(dz, wcat_ref[...])
        dx_ref[...] = _modulate_bwd(dh, x_ref[...], vec_ref, R_GMIX, R_SC1, R_SH1, ps_ref, dxm_ref[...])

    hblk = lambda w: pl.BlockSpec((H, tm, w), lambda i: (0, i, 0))
    return pl.pallas_call(
        body, name="mla_pre_bwd", grid=(T // tm,),
        in_specs=[_rows(tm, D), _rows(tm, D), _full((8, D)), _rows(tm, D), _rows(tm, zc), hblk(QK_EXT), hblk(QK_EXT),
                  hblk(V_DIM), _rows(tm, LANES), _full(wcat.shape), _full(g_q.shape), _full(g_kv.shape),
                  _full(wuq.shape), _full(wukv.shape)],
        out_specs=[_rows(tm, D), _full((8, D)), _full(g_q.shape), _full(g_kv.shape), _full(wcat.shape),
                   _full(wuq.shape), _full(wukv.shape)],
        out_shape=[jax.ShapeDtypeStruct((T, D), jnp.float32), jax.ShapeDtypeStruct((8, D), jnp.float32),
                   jax.ShapeDtypeStruct(g_q.shape, jnp.float32), jax.ShapeDtypeStruct(g_kv.shape, jnp.float32),
                   jax.ShapeDtypeStruct(wcat.shape, jnp.float32), jax.ShapeDtypeStruct(wuq.shape, jnp.float32),
                   jax.ShapeDtypeStruct(wukv.shape, jnp.float32)],
        compiler_params=_params("arbitrary"),
    )(x, dxm, vec, hb, z, dq, dk, dv, cs, wcat, g_q, g_kv, wuq, wukv)


def _swa_attn_bwd(q, k, v, do, bias, sinks_b):
    T = q.shape[0]
    W, Dh, G, Hk = WINDOW, SWA_HEAD_DIM, SWA_GROUP, SWA_KV_HEADS
    nk = Hk * Dh

    def body(q_ref, kp_ref, kc_ref, vp_ref, vc_ref, do_ref, bias_ref, sink_ref, dq_ref, dk_ref, dv_ref, dsink_ref):
        n = pl.program_id(0)

        @pl.when(n == 0)
        def _():
            dk_ref[...] = jnp.zeros_like(dk_ref)
            dv_ref[...] = jnp.zeros_like(dv_ref)
            dsink_ref[...] = jnp.zeros_like(dsink_ref)

        dqs, dks, dvs = [], [], []
        for kh in range(Hk):
            qs, kb, pn, p_sink = _swa_probs(n, kh, q_ref, kp_ref, kc_ref, bias_ref, sink_ref)
            vb = jnp.concatenate([vp_ref[:, kh * Dh:(kh + 1) * Dh], vc_ref[:, kh * Dh:(kh + 1) * Dh]], axis=0)
            dos = jnp.concatenate([do_ref[:, (kh * G + g) * Dh:(kh * G + g + 1) * Dh] for g in range(G)], axis=0)
            dp = _dot_nt(dos, vb)
            delta = jnp.sum(pn * dp, axis=1, keepdims=True)
            dsb = (pn * (dp - delta)).astype(MXU_DTYPE)
            dsk = -p_sink * delta
            for g in range(G):
                h = kh * G + g
                dsink_ref[h:h + 1, :] += jnp.broadcast_to(jnp.sum(dsk[g * W:(g + 1) * W], axis=0, keepdims=True), (1, LANES))
            dq_st = _dot(dsb, kb) * SWA_SCALE
            dqs += [dq_st[g * W:(g + 1) * W] for g in range(G)]
            dks.append(_dot_tn(dsb, qs))
            dvs.append(_dot_tn(pn.astype(MXU_DTYPE), dos))
        dq_ref[...] = jnp.concatenate(dqs, axis=1)
        dkb = jnp.concatenate(dks, axis=1)
        dvb = jnp.concatenate(dvs, axis=1)
        cur_rows = pl.ds(pl.multiple_of(n * W, W), W)
        dk_ref[cur_rows, :] += dkb[W:]
        dv_ref[cur_rows, :] += dvb[W:]

        @pl.when(n > 0)
        def _():
            prev_rows = pl.ds(pl.multiple_of((n - 1) * W, W), W)
            dk_ref[prev_rows, :] += dkb[:W]
            dv_ref[prev_rows, :] += dvb[:W]

    prev = lambda n: (jnp.maximum(n - 1, 0), 0)
    cur = lambda n: (n, 0)
    return pl.pallas_call(
        body, name="swa_attn_bwd", grid=(T // W,),
        in_specs=[pl.BlockSpec((W, D), cur), pl.BlockSpec((W, nk), prev), pl.BlockSpec((W, nk), cur),
                  pl.BlockSpec((W, nk), prev), pl.BlockSpec((W, nk), cur), pl.BlockSpec((W, D), cur),
                  _full(bias.shape), _full(sinks_b.shape)],
        out_specs=[pl.BlockSpec((W, D), cur), _full((T, nk)), _full((T, nk)), _full(sinks_b.shape)],
        out_shape=[jax.ShapeDtypeStruct((T, D), jnp.float32), jax.ShapeDtypeStruct((T, nk), jnp.float32),
                   jax.ShapeDtypeStruct((T, nk), jnp.float32), jax.ShapeDtypeStruct(sinks_b.shape, jnp.float32)],
        compiler_params=_params("arbitrary"),
    )(q, k, k, v, v, do, bias, sinks_b)


def _swa_pre_bwd(x, dxm, vec, dq, dk, dv, w_qkv):
    T = x.shape[0]
    tm = _tile(T, 512)
    nq = SWA_HEADS * SWA_HEAD_DIM
    nk = SWA_KV_HEADS * SWA_HEAD_DIM
    nqkv = nq + 2 * nk

    def body(x_ref, dxm_ref, vec_ref, dq_ref, dk_ref, dv_ref, w_ref, dx_ref, dqkv_ref, ps_ref, db_ref):
        @pl.when(pl.program_id(0) == 0)
        def _():
            ps_ref[...] = jnp.zeros_like(ps_ref)
            db_ref[...] = jnp.zeros_like(db_ref)

        dqkv = jnp.concatenate([dq_ref[...], dk_ref[...], dv_ref[...]], axis=1)
        db_ref[...] += _rowsum(dqkv)
        dqkv_b = dqkv.astype(MXU_DTYPE)
        dqkv_ref[...] = dqkv_b
        dh = _dot_nt(dqkv_b, w_ref[...])
        dx_ref[...] = _modulate_bwd(dh, x_ref[...], vec_ref, R_GMIX, R_SC1, R_SH1, ps_ref, dxm_ref[...])

    return pl.pallas_call(
        body, name="swa_pre_bwd", grid=(T // tm,),
        in_specs=[_rows(tm, D), _rows(tm, D), _full((8, D)), _rows(tm, nq), _rows(tm, nk), _rows(tm, nk),
                  _full(w_qkv.shape)],
        out_specs=[_rows(tm, D), _rows(tm, nqkv), _full((8, D)), _full((1, nqkv))],
        out_shape=[jax.ShapeDtypeStruct((T, D), jnp.float32), jax.ShapeDtypeStruct((T, nqkv), MXU_DTYPE),
                   jax.ShapeDtypeStruct((8, D), jnp.float32), jax.ShapeDtypeStruct((1, nqkv), jnp.float32)],
        compiler_params=_params("arbitrary"),
    )(x, dxm, vec, dq, dk, dv, w_qkv)


def _rot_cols(w):
    half = QK_ROPE // 2
    return jnp.concatenate([-w[..., half:], w[..., :half]], axis=-1)


def _unrot_grad(d_rope, d_rot):
    half = QK_ROPE // 2
    return d_rope + jnp.concatenate([d_rot[..., half:], -d_rot[..., :half]], axis=-1)


def _rope_table(positions):
    half = QK_ROPE // 2
    inv_freq = ROPE_THETA ** (-jnp.arange(half, dtype=jnp.float32) / half)
    ang = positions.astype(jnp.float32)[:, None] * inv_freq
    cos, sin = jnp.cos(ang), jnp.sin(ang)
    return jnp.concatenate([cos, cos, sin, sin], axis=1)


def _sequence_step(x, tgt, positions, vecs, g_q, g_kv, sinks, g_final, wts):
    H = MLA_HEADS
    cs = _rope_table(positions)
    w_dkv = wts["mla_w_dkv"]
    wcat = jnp.concatenate([wts["mla_w_dq"], w_dkv, _rot_cols(w_dkv[:, KV_LORA:])], axis=1)
    uq = wts["mla_w_uq"].reshape(Q_LORA, H, QK_NOPE + QK_ROPE)
    wuq = jnp.concatenate([uq, _rot_cols(uq[..., QK_NOPE:])], axis=-1).transpose(1, 0, 2)
    wukv = wts["mla_w_ukv"].reshape(KV_LORA, H, QK_NOPE + V_DIM).transpose(1, 0, 2)
    zero_bias = jnp.zeros((1, D), jnp.float32)
    bias = _swa_bias()
    sinks_b = jnp.broadcast_to(sinks.reshape(SWA_HEADS, 1), (SWA_HEADS, LANES))

    h1a, z, q, k, v = _mla_pre(x, vecs[0], wcat, g_q, g_kv, wuq, wukv, cs)
    o_a, lse = _mla_attn_fwd(q, k, v)
    y1a, xm_a, h2a = _post_attn(o_a, x, wts["mla_w_o"], zero_bias, vecs[0])
    a_a, y2a, x1 = _mlp_fwd(h2a, wts["w_ff1"], wts["w_ff2"], 0, xm_a, vecs[0])

    h1b, qs, ks, vs = _swa_pre(x1, vecs[1], wts["swa_w_qkv"], wts["swa_b_qkv"])
    o_b = _swa_attn_fwd(qs, ks, vs, bias, sinks_b)
    y1b, xm_b, h2b = _post_attn(o_b, x1, wts["swa_w_o"], wts["swa_b_o"], vecs[1])
    a_b, y2b, x2 = _mlp_fwd(h2b, wts["w_ff1"], wts["w_ff2"], 1, xm_b, vecs[1])

    loss8, dx2, dg_final = _final_loss(x2, tgt, g_final.reshape(1, D))

    du_b, dy2b, dxm_b, ps_mlp_b = _mlp_bwd(dx2, y2b, a_b, wts["w_ff1"], wts["w_ff2"], 1, xm_b, vecs[1])
    g_ff2 = _mm_tn(a_b, dy2b, "dw_ff2_l1", "rows", DEPTH, 1)
    g_ff1 = _mm_tn(h2b, du_b, "dw_ff1_l1", "cols", DEPTH, 1)
    dy1b, do_b, ps_out_b = _attn_out_bwd(dxm_b, y1b, o_b, wts["swa_w_o"], vecs[1], False)
    g_swa_o = _mm_tn(o_b, dy1b, "dw_o_swa")
    dqs, dks, dvs, dsinks = _swa_attn_bwd(qs, ks, vs, do_b, bias, sinks_b)
    dx1, dqkv, ps_pre_b, g_swa_bqkv = _swa_pre_bwd(x1, dxm_b, vecs[1], dqs, dks, dvs, wts["swa_w_qkv"])
    g_swa_qkv = _mm_tn(h1b, dqkv, "dw_qkv", "cols")

    du_a, dy2a, dxm_a, ps_mlp_a = _mlp_bwd(dx1, y2a, a_a, wts["w_ff1"], wts["w_ff2"], 0, xm_a, vecs[0])
    g_ff2 = _mm_tn(a_a, dy2a, "dw_ff2_l0", "rows", DEPTH, 0, g_ff2)
    g_ff1 = _mm_tn(h2a, du_a, "dw_ff1_l0", "cols", DEPTH, 0, g_ff1)
    dy1a, do_a, ps_out_a, delta = _attn_out_bwd(dxm_a, y1a, o_a, wts["mla_w_o"], vecs[0], True)
    g_mla_o = _mm_tn(o_a, dy1a, "dw_o_mla")
    dq, dk, dv = _mla_attn_bwd(q, k, v, do_a, lse, delta)
    dx0, ps_pre_a, dg_q, dg_kv, dwcat, dwuq, dwukv = _mla_pre_bwd(
        x, dxm_a, vecs[0], h1a, z, dq, dk, dv, cs, wcat, g_q, g_kv, wuq, wukv)

    c0, c1, c2 = Q_LORA, Q_LORA + KV_LORA, Q_LORA + KV_LORA + QK_ROPE
    g_dq = dwcat[:, :c0]
    g_dkv = jnp.concatenate([dwcat[:, c0:c1], _unrot_grad(dwcat[:, c1:c2], dwcat[:, c2:])], axis=1)
    e0 = QK_NOPE + QK_ROPE
    g_uq = jnp.concatenate([dwuq[..., :QK_NOPE], _unrot_grad(dwuq[..., QK_NOPE:e0], dwuq[..., e0:])], axis=-1)
    per = H // N_CHIPS
    g_uq = g_uq.reshape(N_CHIPS, per, Q_LORA, e0).transpose(0, 2, 1, 3).reshape(N_CHIPS, Q_LORA, per * e0)
    g_ukv = dwukv.reshape(N_CHIPS, per, KV_LORA, QK_NOPE + V_DIM).transpose(0, 2, 1, 3)
    g_ukv = g_ukv.reshape(N_CHIPS, KV_LORA, per * (QK_NOPE + V_DIM))

    def dmod(ps_pre, ps_out, ps_mlp):
        return jnp.concatenate([ps_pre[R_SH1:R_SC1 + 1], ps_out[R_GT1:R_GT1 + 1], ps_mlp[R_SH2:R_GT2 + 1]], axis=0)

    rows4 = lambda g: g.reshape(N_CHIPS, g.shape[0] // N_CHIPS, g.shape[1])
    grads = {
        "mla_w_dq": rows4(g_dq), "mla_w_uq": g_uq, "mla_w_dkv": rows4(g_dkv), "mla_w_ukv": g_ukv,
        "mla_w_o": rows4(g_mla_o), "swa_w_qkv": g_swa_qkv.reshape(N_CHIPS, D, -1), "swa_w_o": rows4(g_swa_o),
        "w_ff1": g_ff1.reshape(N_CHIPS, DEPTH * D, -1), "w_ff2": g_ff2.reshape(N_CHIPS, -1, D),
    }
    small = {
        "dmod": jnp.stack([dmod(ps_pre_a, ps_out_a, ps_mlp_a), dmod(ps_pre_b, ps_out_b, ps_mlp_b)]).reshape(DEPTH, 6 * D),
        "g_mix": jnp.stack([ps_pre_a[R_GMIX], ps_pre_b[R_GMIX]]),
        "g_mlp": jnp.stack([ps_mlp_a[R_GMLP], ps_mlp_b[R_GMLP]]),
        "mla_g_q": dg_q, "mla_g_kv": dg_kv, "swa_sinks": dsinks[:, 0].reshape(1, SWA_HEADS),
        "swa_b_qkv": g_swa_bqkv, "swa_b_o": ps_out_b[R_BO:R_BO + 1],
        "g_final": dg_final.reshape(D), "loss": loss8[0, 0],
    }
    return dx0, grads, small


SHARDED = {
    "mla_w_dq": (1, D // N_CHIPS, Q_LORA),
    "mla_w_uq": (1, Q_LORA, MLA_HEADS * (QK_NOPE + QK_ROPE) // N_CHIPS),
    "mla_w_dkv": (1, D // N_CHIPS, KV_LORA + QK_ROPE),
    "mla_w_ukv": (1, KV_LORA, MLA_HEADS * (QK_NOPE + V_DIM) // N_CHIPS),
    "mla_w_o": (1, MLA_HEADS * V_DIM // N_CHIPS, D),
    "swa_w_qkv": (1, D, (SWA_HEADS + 2 * SWA_KV_HEADS) * SWA_HEAD_DIM // N_CHIPS),
    "swa_w_o": (1, SWA_HEADS * SWA_HEAD_DIM // N_CHIPS, D),
    "w_ff1": (DEPTH, D, D_FF // N_CHIPS),
    "w_ff2": (DEPTH, D_FF // N_CHIPS, D),
}
COL_SPLIT = ("mla_w_uq", "mla_w_ukv", "swa_w_qkv")
BIASES = {"swa_b_qkv": (SWA_HEADS + 2 * SWA_KV_HEADS) * SWA_HEAD_DIM, "swa_b_o": D}


def _view2d(name):
    shape = SHARDED[name]
    return math.prod(shape[:-1]), shape[-1]


SMALL = {"b_ada": (DEPTH, 6 * D), "g_mix": (DEPTH, D), "g_mlp": (DEPTH, D), "mla_g_q": (1, Q_LORA),
         "mla_g_kv": (1, KV_LORA), "swa_sinks": (1, SWA_HEADS), "g_final": (D,), "loss": (),
         "swa_b_qkv": (1, BIASES["swa_b_qkv"]), "swa_b_o": (1, BIASES["swa_b_o"])}
SMALL_ROWS = 168
DMA_ROWS = 256


def _small_slots():
    slots, off = {}, 0
    for name, shape in SMALL.items():
        n = max(math.prod(shape), 1)
        slots[name] = (off, n)
        off += -(-n // LANES) * LANES
    assert off <= SMALL_ROWS * LANES
    return slots


def _pack_small(vals):
    parts, end = [], 0
    for name, (off, n) in _small_slots().items():
        pad = -(-n // LANES) * LANES - n
        v = vals[name].astype(jnp.float32).reshape(-1) if name in vals else jnp.zeros((n,), jnp.float32)
        parts += [v, jnp.zeros((pad,), jnp.float32)]
        end = off + n + pad
    parts.append(jnp.zeros((SMALL_ROWS * LANES - end,), jnp.float32))
    return jnp.concatenate(parts).reshape(SMALL_ROWS, LANES)


def _unpack_small(buf):
    flat = buf.reshape(-1)
    return {name: flat[off:off + n].reshape(SMALL[name]) for name, (off, n) in _small_slots().items()}


def _pieces(rows):
    return [(off, min(DMA_ROWS, rows - off)) for off in range(0, rows, DMA_ROWS)]


HBM = pl.BlockSpec(memory_space=pltpu.HBM)
MESH = pl.DeviceIdType.MESH


def _place():
    x, y, c = lax.axis_index("x"), lax.axis_index("y"), lax.axis_index("c")
    chips = [(1 - x, y), (x, 1 - y), (1 - x, 1 - y)]
    return x, y, c, chips


def _all_gather(block):
    m_per, n = block.shape

    def body(x_ref, out_ref, send_sems, recv_sems, local_sem):
        x, y, c, chips = _place()
        me, sibling = (x, y, c), (x, y, 1 - c)

        def rows(px, py, pc):
            return out_ref.at[pl.ds((4 * px + 2 * py + pc) * m_per, m_per), :]

        def copy(k, blk, to, src=None):
            return pltpu.make_async_remote_copy(
                src_ref=rows(*blk) if src is None else src, dst_ref=rows(*blk),
                send_sem=send_sems.at[k], recv_sem=recv_sems.at[k], device_id=to, device_id_type=MESH)

        mine = pltpu.make_async_copy(x_ref, rows(*me), local_sem)
        mine.start()
        first = [copy(0, me, sibling, src=x_ref)]
        first += [copy(1 + j, me, (*chip, c), src=x_ref) for j, chip in enumerate(chips)]
        for cp in first:
            cp.start()
        passed = [copy(4 + j, (*chip, c), sibling) for j, chip in enumerate(chips)]
        for j, chip in enumerate(chips):
            copy(1 + j, (*chip, c), me).wait_recv()
            passed[j].start()
        copy(0, sibling, me).wait_recv()
        for j, chip in enumerate(chips):
            copy(4 + j, (*chip, 1 - c), me).wait_recv()
        for cp in first + passed:
            cp.wait_send()
        mine.wait()

    out = pl.pallas_call(
        body, name="all_gather_small",
        out_shape=jax.ShapeDtypeStruct((N_DEV * m_per, n), block.dtype),
        in_specs=[pl.BlockSpec(memory_space=pltpu.VMEM)],
        out_specs=pl.BlockSpec(memory_space=pltpu.VMEM),
        scratch_shapes=[pltpu.SemaphoreType.DMA((7,)), pltpu.SemaphoreType.DMA((7,)), pltpu.SemaphoreType.DMA],
    )(block)
    return out.reshape(N_DEV, m_per, n)


def _weight_gather(shards):
    nt = len(shards)

    def body(*refs):
        w_refs, out_refs = refs[:nt], refs[nt:2 * nt]
        send_sems, recv_sems, local_sems = refs[2 * nt:]
        x, y, c, chips = _place()
        sibling = (x, y, 1 - c)

        def slab(t, px, py, half):
            rh = shards[t].shape[0] // 2
            return out_refs[t].at[2 * px + py, pl.ds(half * rh, rh), :]

        def copy(t, k, src, dst, to):
            return pltpu.make_async_remote_copy(src_ref=src, dst_ref=dst, send_sem=send_sems.at[6 * t + k],
                                                recv_sem=recv_sems.at[6 * t + k], device_id=to, device_id_type=MESH)

        mine = [pltpu.make_async_copy(w_refs[t], out_refs[t].at[2 * x + y], local_sems.at[t]) for t in range(nt)]
        for cp in mine:
            cp.start()
        first = []
        for t in range(nt):
            rh = shards[t].shape[0] // 2
            first += [copy(t, j, w_refs[t].at[pl.ds(c * rh, rh), :], slab(t, x, y, c), (*chip, c))
                      for j, chip in enumerate(chips)]
        for cp in first:
            cp.start()
        passed = []
        for t in range(nt):
            for j, chip in enumerate(chips):
                copy(t, j, slab(t, *chip, c), slab(t, *chip, c), (*chip, c)).wait_recv()
                fw = copy(t, 3 + j, slab(t, *chip, c), slab(t, *chip, c), sibling)
                fw.start()
                passed.append(fw)
        for t in range(nt):
            for j, chip in enumerate(chips):
                copy(t, 3 + j, slab(t, *chip, 1 - c), slab(t, *chip, 1 - c), sibling).wait_recv()
        for cp in first + passed:
            cp.wait_send()
        for cp in mine:
            cp.wait()

    return pl.pallas_call(
        body, name="weight_gather",
        out_shape=[jax.ShapeDtypeStruct((N_CHIPS,) + s.shape, s.dtype) for s in shards],
        in_specs=[HBM] * nt, out_specs=[HBM] * nt,
        scratch_shapes=[pltpu.SemaphoreType.DMA((6 * nt,)), pltpu.SemaphoreType.DMA((6 * nt,)),
                        pltpu.SemaphoreType.DMA((nt,))],
    )(*shards)


def _grad_pair_in(grads):
    nt = len(grads)

    def body(*refs):
        g_refs, got_refs = refs[:nt], refs[nt:2 * nt]
        send_sems, recv_sems = refs[2 * nt:]
        x, y, c, _ = _place()
        sibling = (x, y, 1 - c)

        def copy(t, src, dst):
            return pltpu.make_async_remote_copy(src_ref=src, dst_ref=dst, send_sem=send_sems.at[t],
                                                recv_sem=recv_sems.at[t], device_id=sibling, device_id_type=MESH)

        for t in range(nt):
            rh = grads[t].shape[1] // 2
            for p in range(N_CHIPS):
                for off, n in _pieces(rh):
                    copy(t, g_refs[t].at[p, pl.ds((1 - c) * rh + off, n), :], got_refs[t].at[p, pl.ds(off, n), :]).start()
        for t in range(nt):
            rh = grads[t].shape[1] // 2
            copy(t, g_refs[t].at[:, pl.ds((1 - c) * rh, rh), :], got_refs[t]).wait()

    return pl.pallas_call(
        body, name="grad_pair_in",
        out_shape=[jax.ShapeDtypeStruct((N_CHIPS, g.shape[1] // 2, g.shape[2]), g.dtype) for g in grads],
        in_specs=[HBM] * nt, out_specs=[HBM] * nt,
        scratch_shapes=[pltpu.SemaphoreType.DMA((nt,)), pltpu.SemaphoreType.DMA((nt,))],
    )(*grads)


def _pair_sum(g, got, core, name):
    _, rows, cols = g.shape
    rh = rows // 2
    tr = _tile(rh, 512)
    nb = rh // tr

    def body(c_ref, g_ref, got_ref, s32_ref, s16_ref):
        s = g_ref[...] + got_ref[...]
        s32_ref[...] = s
        s16_ref[...] = s.astype(s16_ref.dtype)

    blk = pl.BlockSpec((None, tr, cols), lambda p, i, c_ref: (p, i, 0))
    return pl.pallas_call(
        body, name=name,
        grid_spec=pltpu.PrefetchScalarGridSpec(
            num_scalar_prefetch=1, grid=(N_CHIPS, nb),
            in_specs=[pl.BlockSpec((None, tr, cols), lambda p, i, c_ref: (p, c_ref[0] * nb + i, 0)), blk],
            out_specs=[blk, blk]),
        out_shape=[jax.ShapeDtypeStruct((N_CHIPS, rh, cols), jnp.float32),
                   jax.ShapeDtypeStruct((N_CHIPS, rh, cols), jnp.bfloat16)],
        compiler_params=_params("parallel", "parallel"),
    )(core, g, got)


def _grad_chip_exchange(parts):
    nt = len(parts)

    def body(*refs):
        a_refs, got_refs = refs[:nt], refs[nt:2 * nt]
        send_sems, recv_sems = refs[2 * nt:]
        x, y, c, chips = _place()
        sends = [pltpu.make_async_remote_copy(
            src_ref=a_refs[t].at[2 * cx + cy], dst_ref=got_refs[t].at[j], send_sem=send_sems.at[3 * t + j],
            recv_sem=recv_sems.at[3 * t + j], device_id=(cx, cy, c), device_id_type=MESH)
            for t in range(nt) for j, (cx, cy) in enumerate(chips)]
        for cp in sends:
            cp.start()
        for cp in sends:
            cp.wait_recv()
        for cp in sends:
            cp.wait_send()

    return pl.pallas_call(
        body, name="grad_chip_exchange",
        out_shape=[jax.ShapeDtypeStruct((N_CHIPS - 1,) + a.shape[1:], a.dtype) for a in parts],
        in_specs=[HBM] * nt, out_specs=[HBM] * nt,
        scratch_shapes=[pltpu.SemaphoreType.DMA((3 * nt,)), pltpu.SemaphoreType.DMA((3 * nt,))],
    )(*parts)


def _chip_sum(s32, got, chip, name):
    _, rh, cols = s32.shape
    tr = _tile(rh, 512)

    def body(p_ref, s_ref, got_ref, o_ref):
        acc = s_ref[...]
        for j in range(N_CHIPS - 1):
            acc = acc + got_ref[j].astype(jnp.float32)
        o_ref[...] = acc

    return pl.pallas_call(
        body, name=name,
        grid_spec=pltpu.PrefetchScalarGridSpec(
            num_scalar_prefetch=1, grid=(rh // tr,),
            in_specs=[pl.BlockSpec((None, tr, cols), lambda i, p_ref: (p_ref[0], i, 0)),
                      pl.BlockSpec((N_CHIPS - 1, tr, cols), lambda i, p_ref: (0, i, 0))],
            out_specs=pl.BlockSpec((tr, cols), lambda i, p_ref: (i, 0))),
        out_shape=jax.ShapeDtypeStruct((rh, cols), jnp.float32),
        compiler_params=_params("parallel"),
    )(chip, s32, got)


def _grad_pair_out(halves):
    nt = len(halves)

    def body(*refs):
        h_refs, full_refs = refs[:nt], refs[nt:2 * nt]
        send_sems, recv_sems, local_sems = refs[2 * nt:]
        x, y, c, _ = _place()
        sibling = (x, y, 1 - c)

        def copy(t, src, dst):
            return pltpu.make_async_remote_copy(src_ref=src, dst_ref=dst, send_sem=send_sems.at[t],
                                                recv_sem=recv_sems.at[t], device_id=sibling, device_id_type=MESH)

        keeps = [pltpu.make_async_copy(h_refs[t], full_refs[t].at[c], local_sems.at[t]) for t in range(nt)]
        for cp in keeps:
            cp.start()
        for t in range(nt):
            for off, n in _pieces(halves[t].shape[0]):
                copy(t, h_refs[t].at[pl.ds(off, n), :], full_refs[t].at[c, pl.ds(off, n), :]).start()
        for t in range(nt):
            copy(t, h_refs[t], full_refs[t].at[c]).wait()
        for cp in keeps:
            cp.wait()

    return pl.pallas_call(
        body, name="grad_pair_out",
        out_shape=[jax.ShapeDtypeStruct((2,) + h.shape, h.dtype) for h in halves],
        in_specs=[HBM] * nt, out_specs=[HBM] * nt,
        scratch_shapes=[pltpu.SemaphoreType.DMA((nt,)), pltpu.SemaphoreType.DMA((nt,)), pltpu.SemaphoreType.DMA((nt,))],
    )(*halves)


def _ada_part(c_all, w_ada):
    L, _, ncol = w_ada.shape
    tn = _tile(ncol, 512)

    def body(c_ref, w_ref, cond_ref, part_ref):
        cv = c_ref[...]
        cond = cv * jax.nn.sigmoid(cv)
        cond_ref[...] = cond
        part_ref[0] = jnp.dot(cond, w_ref[0], precision=lax.Precision.HIGHEST, preferred_element_type=jnp.float32)

    return pl.pallas_call(
        body, name="ada_part", grid=(L, ncol // tn),
        in_specs=[_full((N_DEV, D)), pl.BlockSpec((1, D, tn), lambda l, j: (l, 0, j))],
        out_specs=[_full((N_DEV, D)), pl.BlockSpec((1, N_DEV, tn), lambda l, j: (l, 0, j))],
        out_shape=[jax.ShapeDtypeStruct((N_DEV, D), jnp.float32), jax.ShapeDtypeStruct((L, N_DEV, ncol), jnp.float32)],
        compiler_params=_params("arbitrary", "arbitrary"),
    )(c_all, w_ada)


def _adamw_math(w, g, m, v):
    m = ADAM_B1 * m + (1.0 - ADAM_B1) * g
    v = ADAM_B2 * v + (1.0 - ADAM_B2) * jnp.square(g)
    m_hat = m / (1.0 - ADAM_B1 ** ADAM_STEP)
    v_hat = v / (1.0 - ADAM_B2 ** ADAM_STEP)
    delta = -ADAM_LR * (m_hat / (jnp.sqrt(v_hat) + ADAM_EPS) + ADAM_WD * w)
    return delta, m, v


def _adamw(w, g, m, v, name):
    shape = w.shape
    cols = shape[-1]
    rows = math.prod(shape[:-1])
    tr = _tile(rows, 512)
    two_d = lambda t: t.reshape(rows, cols)

    def body(w_ref, g_ref, m_ref, v_ref, d_ref, mo_ref, vo_ref):
        d_ref[...], mo_ref[...], vo_ref[...] = _adamw_math(w_ref[...], g_ref[...], m_ref[...], v_ref[...])

    out = jax.ShapeDtypeStruct((rows, cols), jnp.float32)
    outs = pl.pallas_call(
        body, name=name, grid=(rows // tr,), in_specs=[_rows(tr, cols)] * 4, out_specs=[_rows(tr, cols)] * 3,
        out_shape=[out, out, out], compiler_params=_params("parallel"),
    )(two_d(w), two_d(g), two_d(m), two_d(v))
    return [t.reshape(shape) for t in outs]


def _ada_grad_adamw(cond_t, dm, w, m, v):
    L, _, ncol = w.shape
    tn = _tile(ncol, 512)

    def body(ct_ref, dm_ref, w_ref, m_ref, v_ref, g_ref, d_ref, mo_ref, vo_ref):
        g = ct_ref[:, 0:1] * dm_ref[0, 0:1, :]
        for b in range(1, N_DEV):
            g = g + ct_ref[:, b:b + 1] * dm_ref[0, b:b + 1, :]
        g_ref[0] = g
        d_ref[0], mo_ref[0], vo_ref[0] = _adamw_math(w_ref[0], g, m_ref[0], v_ref[0])

    wblk = pl.BlockSpec((1, D, tn), lambda l, j: (l, 0, j))
    out = jax.ShapeDtypeStruct(w.shape, jnp.float32)
    return pl.pallas_call(
        body, name="ada_grad_adamw", grid=(L, ncol // tn),
        in_specs=[_full((D, N_DEV)), pl.BlockSpec((1, N_DEV, tn), lambda l, j: (l, 0, j)), wblk, wblk, wblk],
        out_specs=[wblk] * 4, out_shape=[out] * 4, compiler_params=_params("parallel", "parallel"),
    )(cond_t, dm, w, m, v)


def _small_adamw(gathered, w, m, v):
    def body(ga_ref, w_ref, m_ref, v_ref, g_ref, d_ref, mo_ref, vo_ref):
        g = ga_ref[0]
        for dev in range(1, N_DEV):
            g = g + ga_ref[dev]
        g_ref[...] = g
        d_ref[...], mo_ref[...], vo_ref[...] = _adamw_math(w_ref[...], g, m_ref[...], v_ref[...])

    out = jax.ShapeDtypeStruct((SMALL_ROWS, LANES), jnp.float32)
    return pl.pallas_call(
        body, name="small_adamw", out_shape=[out] * 4,
        in_specs=[pl.BlockSpec(memory_space=pltpu.VMEM)] * 4, out_specs=[pl.BlockSpec(memory_space=pltpu.VMEM)] * 4,
    )(gathered, w, m, v)


def _one_hot_pick(arr, index, axis):
    n = arr.shape[axis]
    shape = [1] * arr.ndim
    shape[axis] = n
    hot = (jnp.arange(n) == index).astype(arr.dtype).reshape(shape)
    return jnp.sum(arr * hot, axis=axis)


def kernel(x, c, positions, w_ada, b_ada, g_mix, g_mlp, mla_w_dq, mla_g_q, mla_w_uq, mla_w_dkv, mla_g_kv, mla_w_ukv, mla_w_o, swa_w_qkv, swa_b_qkv, swa_sinks, swa_w_o, swa_b_o, w_ff1, w_ff2, g_final, loss_target, m_w_ada, m_b_ada, m_g_mix, m_g_mlp, m_mla_w_dq, m_mla_g_q, m_mla_w_uq, m_mla_w_dkv, m_mla_g_kv, m_mla_w_ukv, m_mla_w_o, m_swa_w_qkv, m_swa_b_qkv, m_swa_sinks, m_swa_w_o, m_swa_b_o, m_w_ff1, m_w_ff2, m_g_final, v_w_ada, v_b_ada, v_g_mix, v_g_mlp, v_mla_w_dq, v_mla_g_q, v_mla_w_uq, v_mla_w_dkv, v_mla_g_kv, v_mla_w_ukv, v_mla_w_o, v_swa_w_qkv, v_swa_b_qkv, v_swa_sinks, v_swa_w_o, v_swa_b_o, v_w_ff1, v_w_ff2, v_g_final):
    W = dict(w_ada=w_ada, b_ada=b_ada, g_mix=g_mix, g_mlp=g_mlp, mla_w_dq=mla_w_dq, mla_g_q=mla_g_q, mla_w_uq=mla_w_uq,
             mla_w_dkv=mla_w_dkv, mla_g_kv=mla_g_kv, mla_w_ukv=mla_w_ukv, mla_w_o=mla_w_o, swa_w_qkv=swa_w_qkv,
             swa_b_qkv=swa_b_qkv, swa_sinks=swa_sinks, swa_w_o=swa_w_o, swa_b_o=swa_b_o, w_ff1=w_ff1, w_ff2=w_ff2,
             g_final=g_final)
    M = dict(w_ada=m_w_ada, b_ada=m_b_ada, g_mix=m_g_mix, g_mlp=m_g_mlp, mla_w_dq=m_mla_w_dq, mla_g_q=m_mla_g_q,
             mla_w_uq=m_mla_w_uq, mla_w_dkv=m_mla_w_dkv, mla_g_kv=m_mla_g_kv, mla_w_ukv=m_mla_w_ukv, mla_w_o=m_mla_w_o,
             swa_w_qkv=m_swa_w_qkv, swa_b_qkv=m_swa_b_qkv, swa_sinks=m_swa_sinks, swa_w_o=m_swa_w_o, swa_b_o=m_swa_b_o,
             w_ff1=m_w_ff1, w_ff2=m_w_ff2, g_final=m_g_final)
    V = dict(w_ada=v_w_ada, b_ada=v_b_ada, g_mix=v_g_mix, g_mlp=v_g_mlp, mla_w_dq=v_mla_w_dq, mla_g_q=v_mla_g_q,
             mla_w_uq=v_mla_w_uq, mla_w_dkv=v_mla_w_dkv, mla_g_kv=v_mla_g_kv, mla_w_ukv=v_mla_w_ukv, mla_w_o=v_mla_w_o,
             swa_w_qkv=v_swa_w_qkv, swa_b_qkv=v_swa_b_qkv, swa_sinks=v_swa_sinks, swa_w_o=v_swa_w_o, swa_b_o=v_swa_b_o,
             w_ff1=v_w_ff1, w_ff2=v_w_ff2, g_final=v_g_final)
    order = list(W)
    names = list(SHARDED)
    core = lax.axis_index("c")
    chip = 2 * lax.axis_index("x") + lax.axis_index("y")
    dev = 2 * chip + core
    core_arr = core.astype(jnp.int32).reshape(1)
    chip_arr = chip.astype(jnp.int32).reshape(1)

    gathered = dict(zip(names, _weight_gather([W[n].astype(MXU_DTYPE).reshape(_view2d(n)) for n in names])))
    wts = {}
    for n in names:
        g = gathered[n]
        if n in ("w_ff1", "w_ff2"):
            wts[n] = g.reshape((N_CHIPS,) + SHARDED[n])
        elif n in COL_SPLIT:
            wts[n] = g.transpose(1, 0, 2).reshape(g.shape[1], N_CHIPS * g.shape[2])
        else:
            wts[n] = g.reshape(N_CHIPS * g.shape[1], g.shape[2])

    nbq, nbo = BIASES["swa_b_qkv"] // N_CHIPS, BIASES["swa_b_o"] // N_CHIPS
    first = jnp.concatenate([c.reshape(-1), swa_b_qkv.reshape(-1), swa_b_o.reshape(-1),
                             jnp.zeros((16 * LANES - D - nbq - nbo,), jnp.float32)]).reshape(16, LANES)
    first_all = _all_gather(first).reshape(N_DEV, 16 * LANES)
    c_all = first_all[:, :D]
    south = first_all[0::2]
    wts["swa_b_qkv"] = south[:, D:D + nbq].reshape(1, N_CHIPS * nbq)
    wts["swa_b_o"] = south[:, D + nbq:D + nbq + nbo].reshape(1, N_CHIPS * nbo)
    cond_all, part = _ada_part(c_all, w_ada)
    ncol = w_ada.shape[2]
    part_all = _all_gather(part.reshape(-1, LANES)).reshape(N_DEV, DEPTH, N_DEV, ncol)
    mine = _one_hot_pick(part_all[0::2], dev, axis=2)
    mod = mine.transpose(1, 0, 2).reshape(DEPTH, N_CHIPS * ncol) + b_ada
    vecs = jnp.concatenate([mod.reshape(DEPTH, 6, D), g_mix[:, None, :], g_mlp[:, None, :]], axis=1)

    grad_x, grads, small = _sequence_step(x[0], loss_target[0], positions[0], vecs, mla_g_q, mla_g_kv, swa_sinks,
                                          g_final, wts)

    small["b_ada"] = small.pop("dmod")
    small_all = _all_gather(_pack_small(small))
    pk = lambda src: _pack_small({n: src[n] for n in SMALL if n != "loss" and n not in BIASES})
    g_small, d_small, m_small, v_small = [_unpack_small(t) for t in _small_adamw(small_all, pk(W), pk(M), pk(V))]
    off, n = _small_slots()["b_ada"]
    dmod_all = small_all.reshape(N_DEV, -1)[:, off:off + n].reshape(N_DEV, DEPTH, N_CHIPS, ncol)
    dm = _one_hot_pick(dmod_all, chip, axis=2).transpose(1, 0, 2)
    ada = _ada_grad_adamw(cond_all.T, dm, w_ada, m_w_ada, v_w_ada)

    gl = [grads[n] for n in names]
    got = _grad_pair_in(gl)
    sums = [_pair_sum(g, s, core_arr, "pair_sum_" + n) for n, g, s in zip(names, gl, got)]
    others = _grad_chip_exchange([s16 for _, s16 in sums])
    halves = [_chip_sum(s32, o, chip_arr, "chip_sum_" + n) for n, (s32, _), o in zip(names, sums, others)]
    g_shard = {n: f.reshape(SHARDED[n]) for n, f in zip(names, _grad_pair_out(halves))}
    for n, width in BIASES.items():
        g_shard[n] = _one_hot_pick(g_small[n].reshape(N_CHIPS, width // N_CHIPS), chip, axis=0).reshape(1, -1)

    res = {}
    for name in order:
        if name == "w_ada":
            res[name] = ada
        elif name in g_shard:
            res[name] = [g_shard[name]] + _adamw(W[name], g_shard[name], M[name], V[name], "adamw_" + name)
        else:
            res[name] = [t[name] for t in (g_small, d_small, m_small, v_small)]
    outs = [g_small["loss"], grad_x[None]]
    for k in range(4):
        outs += [res[name][k] for name in order]
    return tuple(outs)
```

```python
import functools
import math

import jax
import jax.numpy as jnp
import numpy as np
from jax import lax
from jax.experimental import pallas as pl
from jax.experimental.pallas import tpu as pltpu

D = 1024
DEPTH = 2
MLA_HEADS = 8
QK_NOPE = 128
QK_ROPE = 64
V_DIM = 128
Q_LORA = 384
KV_LORA = 256
ROPE_THETA = 10000.0
SWA_HEADS = 16
SWA_KV_HEADS = 4
SWA_HEAD_DIM = 64
SWA_GROUP = SWA_HEADS // SWA_KV_HEADS
WINDOW = 128
D_FF = 4 * D
EPS = 1e-6
ADAM_LR = 0.001
ADAM_B1 = 0.9
ADAM_B2 = 0.999
ADAM_EPS = 1e-08
ADAM_WD = 0.01
ADAM_STEP = 10

N_CHIPS = 4
N_DEV = 8
LANES = 128
QK_EXT = 256
MLA_SCALE = (QK_NOPE + QK_ROPE) ** -0.5
LOG2E = math.log2(math.e)
LN2 = math.log(2.0)
MLA_QSCALE = MLA_SCALE * LOG2E
ATTN_BLOCK = 1024
ATTN_SUB = 512
SWA_SCALE = SWA_HEAD_DIM ** -0.5
NEG = -1e30
MXU_DTYPE = jnp.bfloat16
VMEM_LIMIT = 56 * 1024 * 1024

R_SH1, R_SC1, R_GT1, R_SH2, R_SC2, R_GT2, R_GMIX, R_GMLP = range(8)
R_BO = 6


def _tile(n, pref):
    if n <= pref:
        return n
    for t in range(pref, 7, -1):
        if n % t == 0 and t % 8 == 0:
            return t
    return n


def _dot(a, b):
    return jnp.dot(a, b, preferred_element_type=jnp.float32)


def _dot_nt(a, b):
    return lax.dot_general(a, b, (((1,), (1,)), ((), ())), preferred_element_type=jnp.float32)


def _dot_tn(a, b):
    return lax.dot_general(a, b, (((0,), (0,)), ((), ())), preferred_element_type=jnp.float32)


def _rms(x):
    r = lax.rsqrt(jnp.mean(x * x, axis=-1, keepdims=True) + EPS)
    return x * r, r


def _rms_bwd(dxhat, xhat, r):
    return r * (dxhat - xhat * jnp.mean(dxhat * xhat, axis=-1, keepdims=True))


def _rowsum(v):
    return jnp.sum(v, axis=0, keepdims=True)


def _params(*sem):
    return pltpu.CompilerParams(dimension_semantics=sem, vmem_limit_bytes=VMEM_LIMIT)


def _full(shape):
    nd = len(shape)
    return pl.BlockSpec(shape, lambda *_: (0,) * nd)


def _rows(tm, cols):
    return pl.BlockSpec((tm, cols), lambda i, *_: (i, 0))


def _modulate_bwd(dh, x, vec_ref, r_g, r_sc, r_sh, ps_ref, dres):
    xhat, r = _rms(x)
    g = vec_ref[r_g:r_g + 1, :]
    n = xhat * g
    ps_ref[r_sh:r_sh + 1, :] += _rowsum(dh)
    ps_ref[r_sc:r_sc + 1, :] += _rowsum(dh * n)
    dn = dh * (1.0 + vec_ref[r_sc:r_sc + 1, :])
    ps_ref[r_g:r_g + 1, :] += _rowsum(dn * xhat)
    return dres + _rms_bwd(dn * g, xhat, r)


def _mla_pre(x, vec, wcat, g_q, g_kv, wuq, wukv, cs):
    T = x.shape[0]
    tm = _tile(T, 512)
    H = MLA_HEADS

    def body(x_ref, vec_ref, wcat_ref, gq_ref, gkv_ref, wuq_ref, wukv_ref, cs_ref, h_ref, z_ref, q_ref, k_ref, v_ref):
        xhat, _ = _rms(x_ref[...])
        h = xhat * vec_ref[R_GMIX:R_GMIX + 1, :] * (1.0 + vec_ref[R_SC1:R_SC1 + 1, :]) + vec_ref[R_SH1:R_SH1 + 1, :]
        hb = h.astype(MXU_DTYPE)
        h_ref[...] = hb
        z = _dot(hb, wcat_ref[...])
        z_ref[...] = z
        cq = (_rms(z[:, :Q_LORA])[0] * gq_ref[...]).astype(MXU_DTYPE)
        ckv = (_rms(z[:, Q_LORA:Q_LORA + KV_LORA])[0] * gkv_ref[...]).astype(MXU_DTYPE)
        cs_t = cs_ref[...]
        t = z[:, Q_LORA + KV_LORA:] * cs_t
        k_rope = (t + pltpu.roll(t, QK_ROPE, axis=1)).astype(MXU_DTYPE)
        low = lax.broadcasted_iota(jnp.int32, (1, LANES), 1) < QK_ROPE
        for hd in range(H):
            qf = _dot(cq, wuq_ref[hd])
            tq = qf[:, QK_NOPE:] * cs_t
            tq = tq + pltpu.roll(tq, QK_ROPE, axis=1)
            q_ref[hd, :, :QK_NOPE] = (qf[:, :QK_NOPE] * MLA_QSCALE).astype(MXU_DTYPE)
            q_ref[hd, :, QK_NOPE:] = jnp.where(low, tq * MLA_QSCALE, 0.0).astype(MXU_DTYPE)
            kvf = _dot(ckv, wukv_ref[hd])
            k_ref[hd, :, :QK_NOPE] = kvf[:, :QK_NOPE].astype(MXU_DTYPE)
            k_ref[hd, :, QK_NOPE:] = k_rope
            v_ref[hd] = kvf[:, QK_NOPE:].astype(MXU_DTYPE)

    zc = wcat.shape[1]
    return pl.pallas_call(
        body, name="mla_pre", grid=(T // tm,),
        in_specs=[_rows(tm, D), _full((8, D)), _full(wcat.shape), _full(g_q.shape), _full(g_kv.shape),
                  _full(wuq.shape), _full(wukv.shape), _rows(tm, LANES)],
        out_specs=[_rows(tm, D), _rows(tm, zc),
                   pl.BlockSpec((H, tm, QK_EXT), lambda i: (0, i, 0)),
                   pl.BlockSpec((H, tm, QK_EXT), lambda i: (0, i, 0)),
                   pl.BlockSpec((H, tm, V_DIM), lambda i: (0, i, 0))],
        out_shape=[jax.ShapeDtypeStruct((T, D), MXU_DTYPE), jax.ShapeDtypeStruct((T, zc), jnp.float32),
                   jax.ShapeDtypeStruct((H, T, QK_EXT), MXU_DTYPE), jax.ShapeDtypeStruct((H, T, QK_EXT), MXU_DTYPE),
                   jax.ShapeDtypeStruct((H, T, V_DIM), MXU_DTYPE)],
        compiler_params=_params("parallel"),
    )(x, vec, wcat, g_q, g_kv, wuq, wukv, cs)


def _mla_attn_fwd(q, k, v):
    H, T, _ = q.shape
    tb = _tile(T, ATTN_BLOCK)
    sub = min(ATTN_SUB, tb)
    ns, nb = tb // sub, T // tb

    def body(q_ref, k_ref, v_ref, o_ref, lse_ref, m_sc, l_sc, acc_sc):
        qi, kj = pl.program_id(1), pl.program_id(2)

        @pl.when(kj == 0)
        def _():
            m_sc[...] = jnp.full_like(m_sc, NEG)
            l_sc[...] = jnp.zeros_like(l_sc)
            acc_sc[...] = jnp.zeros_like(acc_sc)

        def update(r, kk, masked):
            rows, keys = pl.ds(r * sub, sub), pl.ds(kk * sub, sub)
            s = _dot_nt(q_ref[0, rows, :], k_ref[0, keys, :])
            if masked:
                row = lax.broadcasted_iota(jnp.int32, (sub, sub), 0)
                col = lax.broadcasted_iota(jnp.int32, (sub, sub), 1)
                s = jnp.where(col <= row, s, NEG)
            m_prev = m_sc[rows, :]
            m_new = jnp.maximum(m_prev, jnp.max(s, axis=1, keepdims=True))
            alpha = jnp.exp2(m_prev - m_new)
            p = jnp.exp2(s - jnp.tile(m_new, (1, sub // LANES)))
            l_sc[rows, :] = alpha * l_sc[rows, :] + jnp.sum(p, axis=1, keepdims=True)
            acc_sc[rows, :] = alpha * acc_sc[rows, :] + _dot(p.astype(MXU_DTYPE), v_ref[0, keys, :])
            m_sc[rows, :] = m_new

        @pl.when(kj < qi)
        def _():
            for kk in range(ns):
                for r in range(ns):
                    update(r, kk, False)

        @pl.when(kj == qi)
        def _():
            for kk in range(ns):
                for r in range(kk, ns):
                    update(r, kk, r == kk)
            l = l_sc[...]
            o_ref[...] = (acc_sc[...] / l).astype(o_ref.dtype)
            lse = m_sc[...] + jnp.log2(l)
            pick = (lax.broadcasted_iota(jnp.int32, (8, LANES), 1) == 0).astype(jnp.float32)
            row = lax.dot_general(pick, lse, (((1,), (1,)), ((), ())), precision=lax.Precision.HIGHEST,
                                  preferred_element_type=jnp.float32)
            lse_ref[0] = row[0:1, :]

    kv_idx = lambda h, i, j: (h, jnp.minimum(i, j), 0)
    return pl.pallas_call(
        body, name="mla_attn_fwd", grid=(H, nb, nb),
        in_specs=[pl.BlockSpec((1, tb, QK_EXT), lambda h, i, j: (h, i, 0)),
                  pl.BlockSpec((1, tb, QK_EXT), kv_idx),
                  pl.BlockSpec((1, tb, V_DIM), kv_idx)],
        out_specs=[pl.BlockSpec((tb, V_DIM), lambda h, i, j: (i, h)),
                   pl.BlockSpec((1, 1, tb), lambda h, i, j: (h, 0, i))],
        out_shape=[jax.ShapeDtypeStruct((T, H * V_DIM), MXU_DTYPE), jax.ShapeDtypeStruct((H, 1, T), jnp.float32)],
        scratch_shapes=[pltpu.VMEM((tb, LANES), jnp.float32), pltpu.VMEM((tb, LANES), jnp.float32),
                        pltpu.VMEM((tb, V_DIM), jnp.float32)],
        compiler_params=_params("parallel", "parallel", "arbitrary"),
    )(q, k, v)


def _post_attn(o, x, w_o, bias, vec):
    T = x.shape[0]
    tm = _tile(T, 512)

    def body(o_ref, x_ref, w_ref, b_ref, vec_ref, y_ref, xm_ref, h_ref):
        y = _dot(o_ref[...], w_ref[...]) + b_ref[...]
        y_ref[...] = y.astype(y_ref.dtype)
        xm = x_ref[...] + vec_ref[R_GT1:R_GT1 + 1, :] * y
        xm_ref[...] = xm
        xhat, _ = _rms(xm)
        h = xhat * vec_ref[R_GMLP:R_GMLP + 1, :] * (1.0 + vec_ref[R_SC2:R_SC2 + 1, :]) + vec_ref[R_SH2:R_SH2 + 1, :]
        h_ref[...] = h.astype(h_ref.dtype)

    return pl.pallas_call(
        body, name="post_attn", grid=(T // tm,),
        in_specs=[_rows(tm, D), _rows(tm, D), _full((D, D)), _full((1, D)), _full((8, D))],
        out_specs=[_rows(tm, D), _rows(tm, D), _rows(tm, D)],
        out_shape=[jax.ShapeDtypeStruct((T, D), MXU_DTYPE), jax.ShapeDtypeStruct((T, D), jnp.float32),
                   jax.ShapeDtypeStruct((T, D), MXU_DTYPE)],
        compiler_params=_params("parallel"),
    )(o, x, w_o, bias, vec)


def _ff_specs(tf, layer):
    per = D_FF // N_CHIPS // tf
    w1 = pl.BlockSpec((None, None, D, tf), lambda i, f: (f // per, layer, 0, f % per))
    w2 = pl.BlockSpec((None, None, tf, D), lambda i, f: (f // per, layer, f % per, 0))
    return w1, w2


def _mlp_fwd(h2, w1, w2, layer, xm, vec):
    T = h2.shape[0]
    tm = _tile(T, 1024)
    tf = _tile(D_FF // N_CHIPS, 512)
    nf = D_FF // tf
    w1_spec, w2_spec = _ff_specs(tf, layer)

    def body(h_ref, w1_ref, w2_ref, xm_ref, vec_ref, a_ref, y_ref, xo_ref, acc):
        f = pl.program_id(1)

        @pl.when(f == 0)
        def _():
            acc[...] = jnp.zeros_like(acc)

        u = jnp.maximum(_dot(h_ref[...], w1_ref[...]), 0.0)
        ab = (u * u).astype(MXU_DTYPE)
        a_ref[...] = ab
        acc[...] += _dot(ab, w2_ref[...])

        @pl.when(f == nf - 1)
        def _():
            y = acc[...]
            y_ref[...] = y.astype(y_ref.dtype)
            xo_ref[...] = xm_ref[...] + vec_ref[R_GT2:R_GT2 + 1, :] * y

    return pl.pallas_call(
        body, name="mlp_fwd", grid=(T // tm, nf),
        in_specs=[_rows(tm, D), w1_spec, w2_spec, _rows(tm, D), _full((8, D))],
        out_specs=[pl.BlockSpec((tm, tf), lambda i, f: (i, f)), _rows(tm, D), _rows(tm, D)],
        out_shape=[jax.ShapeDtypeStruct((T, D_FF), MXU_DTYPE), jax.ShapeDtypeStruct((T, D), MXU_DTYPE),
                   jax.ShapeDtypeStruct((T, D), jnp.float32)],
        scratch_shapes=[pltpu.VMEM((tm, D), jnp.float32)],
        compiler_params=_params("parallel", "arbitrary"),
    )(h2, w1, w2, xm, vec)


def _swa_pre(x, vec, w_qkv, b_qkv):
    T = x.shape[0]
    tm = _tile(T, 512)
    nq = SWA_HEADS * SWA_HEAD_DIM
    nk = SWA_KV_HEADS * SWA_HEAD_DIM

    def body(x_ref, vec_ref, w_ref, b_ref, h_ref, q_ref, k_ref, v_ref):
        xhat, _ = _rms(x_ref[...])
        h = xhat * vec_ref[R_GMIX:R_GMIX + 1, :] * (1.0 + vec_ref[R_SC1:R_SC1 + 1, :]) + vec_ref[R_SH1:R_SH1 + 1, :]
        hb = h.astype(MXU_DTYPE)
        h_ref[...] = hb
        qkv = _dot(hb, w_ref[...]) + b_ref[...]
        q_ref[...] = (qkv[:, :nq] * SWA_SCALE).astype(MXU_DTYPE)
        k_ref[...] = qkv[:, nq:nq + nk].astype(MXU_DTYPE)
        v_ref[...] = qkv[:, nq + nk:].astype(MXU_DTYPE)

    return pl.pallas_call(
        body, name="swa_pre", grid=(T // tm,),
        in_specs=[_rows(tm, D), _full((8, D)), _full(w_qkv.shape), _full(b_qkv.shape)],
        out_specs=[_rows(tm, D), _rows(tm, nq), _rows(tm, nk), _rows(tm, nk)],
        out_shape=[jax.ShapeDtypeStruct((T, D), MXU_DTYPE), jax.ShapeDtypeStruct((T, nq), MXU_DTYPE),
                   jax.ShapeDtypeStruct((T, nk), MXU_DTYPE), jax.ShapeDtypeStruct((T, nk), MXU_DTYPE)],
        compiler_params=_params("parallel"),
    )(x, vec, w_qkv, b_qkv)


def _swa_bias():
    W = WINDOW
    slopes = 2.0 ** (-8.0 * np.arange(1, SWA_HEADS + 1) / SWA_HEADS)
    dist = W + np.arange(W)[:, None] - np.arange(2 * W)[None, :]
    inside = (dist >= 0) & (dist < W)
    bias = np.where(inside[None], -slopes[:, None, None] * dist[None].astype(np.float64), NEG)
    return jnp.asarray(bias.reshape(SWA_KV_HEADS, SWA_GROUP * W, 2 * W), jnp.float32)


def _swa_probs(n, kh, q_ref, kp_ref, kc_ref, bias_ref, sink_ref):
    W, Dh, G = WINDOW, SWA_HEAD_DIM, SWA_GROUP
    qs = jnp.concatenate([q_ref[:, (kh * G + g) * Dh:(kh * G + g + 1) * Dh] for g in range(G)], axis=0)
    kb = jnp.concatenate([kp_ref[:, kh * Dh:(kh + 1) * Dh], kc_ref[:, kh * Dh:(kh + 1) * Dh]], axis=0)
    s = _dot_nt(qs, kb) + bias_ref[kh]
    col = lax.broadcasted_iota(jnp.int32, (1, 2 * W), 1)
    s = jnp.where((col >= W) | (n > 0), s, NEG)
    sink = jnp.concatenate(
        [jnp.broadcast_to(sink_ref[kh * G + g:kh * G + g + 1, :1], (W, 1)) for g in range(G)], axis=0)
    m = jnp.maximum(jnp.max(s, axis=1, keepdims=True), sink)
    p = jnp.exp(s - m)
    p_sink = jnp.exp(sink - m)
    inv = 1.0 / (jnp.sum(p, axis=1, keepdims=True) + p_sink)
    return qs, kb, p * inv, p_sink * inv


def _swa_attn_fwd(q, k, v, bias, sinks_b):
    T = q.shape[0]
    W, Dh, G, Hk = WINDOW, SWA_HEAD_DIM, SWA_GROUP, SWA_KV_HEADS
    nk = Hk * Dh

    def body(q_ref, kp_ref, kc_ref, vp_ref, vc_ref, bias_ref, sink_ref, o_ref):
        n = pl.program_id(0)
        outs = []
        for kh in range(Hk):
            _, _, pn, _ = _swa_probs(n, kh, q_ref, kp_ref, kc_ref, bias_ref, sink_ref)
            vb = jnp.concatenate([vp_ref[:, kh * Dh:(kh + 1) * Dh], vc_ref[:, kh * Dh:(kh + 1) * Dh]], axis=0)
            o = _dot(pn.astype(MXU_DTYPE), vb)
            outs += [o[g * W:(g + 1) * W] for g in range(G)]
        o_ref[...] = jnp.concatenate(outs, axis=1).astype(o_ref.dtype)

    prev = lambda n: (jnp.maximum(n - 1, 0), 0)
    cur = lambda n: (n, 0)
    return pl.pallas_call(
        body, name="swa_attn_fwd", grid=(T // W,),
        in_specs=[pl.BlockSpec((W, D), cur), pl.BlockSpec((W, nk), prev), pl.BlockSpec((W, nk), cur),
                  pl.BlockSpec((W, nk), prev), pl.BlockSpec((W, nk), cur), _full(bias.shape), _full(sinks_b.shape)],
        out_specs=pl.BlockSpec((W, D), cur),
        out_shape=jax.ShapeDtypeStruct((T, D), MXU_DTYPE),
        compiler_params=_params("parallel"),
    )(q, k, k, v, v, bias, sinks_b)


def _final_loss(x, tgt, g):
    T = x.shape[0]
    tm = _tile(T, 512)

    def body(x_ref, t_ref, g_ref, loss_ref, dx_ref, dg_ref):
        @pl.when(pl.program_id(0) == 0)
        def _():
            loss_ref[...] = jnp.zeros_like(loss_ref)
            dg_ref[...] = jnp.zeros_like(dg_ref)

        xhat, r = _rms(x_ref[...])
        gv = g_ref[...]
        e = xhat * gv - t_ref[...]
        loss_ref[...] += 0.5 * jnp.sum(jnp.mean(e * e, axis=-1, keepdims=True), axis=0, keepdims=True)
        dy = e * (1.0 / D)
        dg_ref[...] += _rowsum(dy * xhat)
        dx_ref[...] = _rms_bwd(dy * gv, xhat, r)

    return pl.pallas_call(
        body, name="final_loss", grid=(T // tm,),
        in_specs=[_rows(tm, D), _rows(tm, D), _full((1, D))],
        out_specs=[_full((8, LANES)), _rows(tm, D), _full((1, D))],
        out_shape=[jax.ShapeDtypeStruct((8, LANES), jnp.float32), jax.ShapeDtypeStruct((T, D), jnp.float32),
                   jax.ShapeDtypeStruct((1, D), jnp.float32)],
        compiler_params=_params("arbitrary"),
    )(x, tgt, g)


def _mlp_bwd(dxo, y2, a, w1, w2, layer, xm, vec):
    T = dxo.shape[0]
    tm = _tile(T, 1024)
    tf = _tile(D_FF // N_CHIPS, 512)
    nf = D_FF // tf
    w1_spec, w2_spec = _ff_specs(tf, layer)

    def body(dxo_ref, y_ref, a_ref, w1_ref, w2_ref, xm_ref, vec_ref, du_ref, dy_ref, dxm_ref, ps_ref, dyb, acc):
        i, f = pl.program_id(0), pl.program_id(1)

        @pl.when((i == 0) & (f == 0))
        def _():
            ps_ref[...] = jnp.zeros_like(ps_ref)

        @pl.when(f == 0)
        def _():
            dxo_t = dxo_ref[...]
            d = (dxo_t * vec_ref[R_GT2:R_GT2 + 1, :]).astype(MXU_DTYPE)
            dyb[...] = d
            dy_ref[...] = d
            acc[...] = jnp.zeros_like(acc)
            ps_ref[R_GT2:R_GT2 + 1, :] += _rowsum(dxo_t * y_ref[...].astype(jnp.float32))

        da = _dot_nt(dyb[...], w2_ref[...])
        dub = (da * (2.0 * jnp.sqrt(a_ref[...].astype(jnp.float32)))).astype(MXU_DTYPE)
        du_ref[...] = dub
        acc[...] += _dot_nt(dub, w1_ref[...])

        @pl.when(f == nf - 1)
        def _():
            dxm_ref[...] = _modulate_bwd(acc[...], xm_ref[...], vec_ref, R_GMLP, R_SC2, R_SH2, ps_ref, dxo_ref[...])

    return pl.pallas_call(
        body, name="mlp_bwd", grid=(T // tm, nf),
        in_specs=[_rows(tm, D), _rows(tm, D), pl.BlockSpec((tm, tf), lambda i, f: (i, f)), w1_spec, w2_spec,
                  _rows(tm, D), _full((8, D))],
        out_specs=[pl.BlockSpec((tm, tf), lambda i, f: (i, f)), _rows(tm, D), _rows(tm, D), _full((8, D))],
        out_shape=[jax.ShapeDtypeStruct((T, D_FF), MXU_DTYPE), jax.ShapeDtypeStruct((T, D), MXU_DTYPE),
                   jax.ShapeDtypeStruct((T, D), jnp.float32), jax.ShapeDtypeStruct((8, D), jnp.float32)],
        scratch_shapes=[pltpu.VMEM((tm, D), MXU_DTYPE), pltpu.VMEM((tm, D), jnp.float32)],
        compiler_params=_params("arbitrary", "arbitrary"),
    )(dxo, y2, a, w1, w2, xm, vec)


def _mm_tn(a, g, name, split=None, layers=1, layer=0, into=None):
    T, K = a.shape
    N = g.shape[1]
    kq = K // N_CHIPS if split == "rows" else K
    nq = N // N_CHIPS if split == "cols" else N
    bk, bn, bt = _tile(kq, 1024), _tile(nq, 1024), _tile(T, 1024)
    if nq % bn or bn % LANES:
        bn = nq
    kper, nper = kq // bk, nq // bn

    def body(*refs):
        a_ref, g_ref, o_ref = refs[0], refs[1], refs[-1]

        @pl.when(pl.program_id(2) == 0)
        def _():
            o_ref[...] = jnp.zeros_like(o_ref)

        o_ref[...] += _dot_tn(a_ref[...], g_ref[...])

    in_specs = [pl.BlockSpec((bt, bk), lambda k, n, t: (t, k)), pl.BlockSpec((bt, bn), lambda k, n, t: (t, n))]
    args = [a, g]
    aliases = {}
    if split is None:
        out_spec = pl.BlockSpec((bk, bn), lambda k, n, t: (k, n))
        out_shape = jax.ShapeDtypeStruct((K, N), jnp.float32)
    else:
        if split == "cols":
            idx = lambda k, n, t: (n // nper, layer, k, n % nper)
        else:
            idx = lambda k, n, t: (k // kper, layer, k % kper, n)
        out_spec = pl.BlockSpec((None, None, bk, bn), idx)
        out_shape = jax.ShapeDtypeStruct((N_CHIPS, layers, kq, nq), jnp.float32)
        if into is not None:
            in_specs.append(pl.BlockSpec(memory_space=pl.ANY))
            args.append(into)
            aliases = {2: 0}
    return pl.pallas_call(
        body, name=name, grid=(K // bk, N // bn, T // bt), in_specs=in_specs, out_specs=out_spec, out_shape=out_shape,
        input_output_aliases=aliases, compiler_params=_params("parallel", "parallel", "arbitrary"),
    )(*args)


def _attn_out_bwd(dxm, y1, o, w_o, vec, with_delta):
    T = dxm.shape[0]
    tm = _tile(T, 512)
    H = MLA_HEADS

    def body(dxm_ref, y_ref, o_ref, w_ref, vec_ref, dy_ref, do_ref, ps_ref, *delta_ref):
        @pl.when(pl.program_id(0) == 0)
        def _():
            ps_ref[...] = jnp.zeros_like(ps_ref)

        dxm_t = dxm_ref[...]
        dy = dxm_t * vec_ref[R_GT1:R_GT1 + 1, :]
        ps_ref[R_GT1:R_GT1 + 1, :] += _rowsum(dxm_t * y_ref[...].astype(jnp.float32))
        ps_ref[R_BO:R_BO + 1, :] += _rowsum(dy)
        dyb = dy.astype(MXU_DTYPE)
        dy_ref[...] = dyb
        do = _dot_nt(dyb, w_ref[...])
        do_ref[...] = do.astype(do_ref.dtype)
        if with_delta:
            of = o_ref[...].astype(jnp.float32)
            ones = jnp.ones((8, V_DIM), jnp.float32)
            for hd in range(H):
                sl = slice(hd * V_DIM, (hd + 1) * V_DIM)
                d = lax.dot_general(ones, do[:, sl] * of[:, sl], (((1,), (1,)), ((), ())),
                                    precision=lax.Precision.HIGHEST, preferred_element_type=jnp.float32)
                delta_ref[0][hd] = d[0:1, :]

    out_specs = [_rows(tm, D), _rows(tm, D), _full((8, D))]
    out_shape = [jax.ShapeDtypeStruct((T, D), MXU_DTYPE), jax.ShapeDtypeStruct((T, D), MXU_DTYPE),
                 jax.ShapeDtypeStruct((8, D), jnp.float32)]
    if with_delta:
        out_specs.append(pl.BlockSpec((H, 1, tm), lambda i: (0, 0, i)))
        out_shape.append(jax.ShapeDtypeStruct((H, 1, T), jnp.float32))
    return pl.pallas_call(
        body, name="attn_out_bwd_mla" if with_delta else "attn_out_bwd_swa", grid=(T // tm,),
        in_specs=[_rows(tm, D), _rows(tm, D), _rows(tm, D), _full((D, D)), _full((8, D))],
        out_specs=out_specs, out_shape=out_shape,
        compiler_params=_params("arbitrary"),
    )(dxm, y1, o, w_o, vec)


def _mla_attn_bwd(q, k, v, do, lse, delta):
    H, T, _ = q.shape
    tb = _tile(T, ATTN_BLOCK)
    sub = min(ATTN_SUB, tb)
    ns, nb = tb // sub, T // tb

    def body(q_ref, k_ref, v_ref, do_ref, lse_ref, dl_ref, dq_ref, dk_ref, dv_ref, dk_acc, dv_acc):
        j, i = pl.program_id(1), pl.program_id(2)

        @pl.when((j == 0) & (i == 0))
        def _():
            dq_ref[...] = jnp.zeros_like(dq_ref)

        def update(kk, r, masked):
            keys, rows = pl.ds(kk * sub, sub), pl.ds(r * sub, sub)
            kb, qb, dob = k_ref[0, keys, :], q_ref[0, rows, :], do_ref[rows, :]
            st = _dot_nt(kb, qb)
            if masked:
                row = lax.broadcasted_iota(jnp.int32, (sub, sub), 0)
                col = lax.broadcasted_iota(jnp.int32, (sub, sub), 1)
                st = jnp.where(row <= col, st, NEG)
            pt = jnp.exp2(st - lse_ref[0, :, rows])
            dv_acc[keys, :] += _dot(pt.astype(MXU_DTYPE), dob)
            dpt = _dot_nt(v_ref[0, keys, :], dob)
            dst = (pt * (dpt - dl_ref[0, :, rows])).astype(MXU_DTYPE)
            dk_acc[keys, :] += _dot(dst, qb)
            q_rows = pl.ds(pl.multiple_of(i * tb + r * sub, sub), sub)
            dq_ref[0, q_rows, :] += _dot_tn(dst, kb)

        @pl.when(i == j)
        def _():
            dk_acc[...] = jnp.zeros_like(dk_acc)
            dv_acc[...] = jnp.zeros_like(dv_acc)
            for r in range(ns):
                for kk in range(r + 1):
                    update(kk, r, kk == r)

        @pl.when(i > j)
        def _():
            for r in range(ns):
                for kk in range(ns):
                    update(kk, r, False)

        @pl.when(i == nb - 1)
        def _():
            dk_ref[0] = (dk_acc[...] * LN2).astype(dk_ref.dtype)
            dv_ref[0] = dv_acc[...].astype(dv_ref.dtype)

    q_idx = lambda h, j, i: (h, jnp.maximum(i, j), 0)
    kv_idx = lambda h, j, i: (h, j, 0)
    stat_idx = lambda h, j, i: (h, 0, jnp.maximum(i, j))
    return pl.pallas_call(
        body, name="mla_attn_bwd", grid=(H, nb, nb),
        in_specs=[pl.BlockSpec((1, tb, QK_EXT), q_idx), pl.BlockSpec((1, tb, QK_EXT), kv_idx),
                  pl.BlockSpec((1, tb, V_DIM), kv_idx),
                  pl.BlockSpec((tb, V_DIM), lambda h, j, i: (jnp.maximum(i, j), h)),
                  pl.BlockSpec((1, 1, tb), stat_idx), pl.BlockSpec((1, 1, tb), stat_idx)],
        out_specs=[pl.BlockSpec((1, T, QK_EXT), lambda h, j, i: (h, 0, 0)),
                   pl.BlockSpec((1, tb, QK_EXT), kv_idx), pl.BlockSpec((1, tb, V_DIM), kv_idx)],
        out_shape=[jax.ShapeDtypeStruct((H, T, QK_EXT), jnp.float32), jax.ShapeDtypeStruct((H, T, QK_EXT), MXU_DTYPE),
                   jax.ShapeDtypeStruct((H, T, V_DIM), MXU_DTYPE)],
        scratch_shapes=[pltpu.VMEM((tb, QK_EXT), jnp.float32), pltpu.VMEM((tb, V_DIM), jnp.float32)],
        compiler_params=_params("parallel", "arbitrary", "arbitrary"),
    )(q, k, v, do, lse, delta)


def _mla_pre_bwd(x, dxm, vec, hb, z, dq, dk, dv, cs, wcat, g_q, g_kv, wuq, wukv):
    T = x.shape[0]
    tm = _tile(T, 256)
    H = MLA_HEADS
    zc = wcat.shape[1]

    def body(x_ref, dxm_ref, vec_ref, h_ref, z_ref, dq_ref, dk_ref, dv_ref, cs_ref, wcat_ref, gq_ref, gkv_ref,
             wuq_ref, wukv_ref, dx_ref, ps_ref, dgq_ref, dgkv_ref, dwcat_ref, dwuq_ref, dwukv_ref):
        @pl.when(pl.program_id(0) == 0)
        def _():
            for ref in (ps_ref, dgq_ref, dgkv_ref, dwcat_ref, dwuq_ref, dwukv_ref):
                ref[...] = jnp.zeros_like(ref)

        z = z_ref[...]
        cs_t = cs_ref[...]
        cqhat, rq = _rms(z[:, :Q_LORA])
        ckhat, rk = _rms(z[:, Q_LORA:Q_LORA + KV_LORA])
        gq, gkv = gq_ref[...], gkv_ref[...]
        cq = (cqhat * gq).astype(MXU_DTYPE)
        ckv = (ckhat * gkv).astype(MXU_DTYPE)
        dcq = jnp.zeros((tm, Q_LORA), jnp.float32)
        dckv = jnp.zeros((tm, KV_LORA), jnp.float32)
        dkr = jnp.zeros((tm, LANES), jnp.float32)
        for hd in range(H):
            dqh = dq_ref[hd] * MLA_SCALE
            gqh = jnp.concatenate([dqh[:, :QK_NOPE], dqh[:, QK_NOPE:] * cs_t], axis=1).astype(MXU_DTYPE)
            dcq += _dot_nt(gqh, wuq_ref[hd])
            dwuq_ref[hd] += _dot_tn(cq, gqh)
            dkh = dk_ref[hd]
            gkvh = jnp.concatenate([dkh[:, :QK_NOPE], dv_ref[hd]], axis=1)
            dckv += _dot_nt(gkvh, wukv_ref[hd])
            dwukv_ref[hd] += _dot_tn(ckv, gkvh)
            dkr += dkh[:, QK_NOPE:].astype(jnp.float32)
        dgq_ref[...] += _rowsum(dcq * cqhat)
        dgkv_ref[...] += _rowsum(dckv * ckhat)
        dcq_pre = _rms_bwd(dcq * gq, cqhat, rq)
        dckv_pre = _rms_bwd(dckv * gkv, ckhat, rk)
        dkr2 = (dkr + pltpu.roll(dkr, QK_ROPE, axis=1)) * cs_t
        dz = jnp.concatenate([dcq_pre, dckv_pre, dkr2], axis=1).astype(MXU_DTYPE)
        dwcat_ref[...] += _dot_tn(h_ref[...], dz)
        dh = _dot_nt(dz, wcat_ref[...])
        dx_ref[...] = _modulate_bwd(dh, x_ref[...], vec_ref, R_GMIX, R_SC1, R_SH1, ps_ref, dxm_ref[...])

    hblk = lambda w: pl.BlockSpec((H, tm, w), lambda i: (0, i, 0))
    return pl.pallas_call(
        body, name="mla_pre_bwd", grid=(T // tm,),
        in_specs=[_rows(tm, D), _rows(tm, D), _full((8, D)), _rows(tm, D), _rows(tm, zc), hblk(QK_EXT), hblk(QK_EXT),
                  hblk(V_DIM), _rows(tm, LANES), _full(wcat.shape), _full(g_q.shape), _full(g_kv.shape),
                  _full(wuq.shape), _full(wukv.shape)],
        out_specs=[_rows(tm, D), _full((8, D)), _full(g_q.shape), _full(g_kv.shape), _full(wcat.shape),
                   _full(wuq.shape), _full(wukv.shape)],
        out_shape=[jax.ShapeDtypeStruct((T, D), jnp.float32), jax.ShapeDtypeStruct((8, D), jnp.float32),
                   jax.ShapeDtypeStruct(g_q.shape, jnp.float32), jax.ShapeDtypeStruct(g_kv.shape, jnp.float32),
                   jax.ShapeDtypeStruct(wcat.shape, jnp.float32), jax.ShapeDtypeStruct(wuq.shape, jnp.float32),
                   jax.ShapeDtypeStruct(wukv.shape, jnp.float32)],
        compiler_params=_params("arbitrary"),
    )(x, dxm, vec, hb, z, dq, dk, dv, cs, wcat, g_q, g_kv, wuq, wukv)


def _swa_attn_bwd(q, k, v, do, bias, sinks_b):
    T = q.shape[0]
    W, Dh, G, Hk = WINDOW, SWA_HEAD_DIM, SWA_GROUP, SWA_KV_HEADS
    nk = Hk * Dh

    def body(q_ref, kp_ref, kc_ref, vp_ref, vc_ref, do_ref, bias_ref, sink_ref, dq_ref, dk_ref, dv_ref, dsink_ref):
        n = pl.program_id(0)

        @pl.when(n == 0)
        def _():
            dk_ref[...] = jnp.zeros_like(dk_ref)
            dv_ref[...] = jnp.zeros_like(dv_ref)
            dsink_ref[...] = jnp.zeros_like(dsink_ref)

        dqs, dks, dvs = [], [], []
        for kh in range(Hk):
            qs, kb, pn, p_sink = _swa_probs(n, kh, q_ref, kp_ref, kc_ref, bias_ref, sink_ref)
            vb = jnp.concatenate([vp_ref[:, kh * Dh:(kh + 1) * Dh], vc_ref[:, kh * Dh:(kh + 1) * Dh]], axis=0)
            dos = jnp.concatenate([do_ref[:, (kh * G + g) * Dh:(kh * G + g + 1) * Dh] for g in range(G)], axis=0)
            dp = _dot_nt(dos, vb)
            delta = jnp.sum(pn * dp, axis=1, keepdims=True)
            dsb = (pn * (dp - delta)).astype(MXU_DTYPE)
            dsk = -p_sink * delta
            for g in range(G):
                h = kh * G + g
                dsink_ref[h:h + 1, :] += jnp.broadcast_to(jnp.sum(dsk[g * W:(g + 1) * W], axis=0, keepdims=True), (1, LANES))
            dq_st = _dot(dsb, kb) * SWA_SCALE
            dqs += [dq_st[g * W:(g + 1) * W] for g in range(G)]
            dks.append(_dot_tn(dsb, qs))
            dvs.append(_dot_tn(pn.astype(MXU_DTYPE), dos))
        dq_ref[...] = jnp.concatenate(dqs, axis=1)
        dkb = jnp.concatenate(dks, axis=1)
        dvb = jnp.concatenate(dvs, axis=1)
        cur_rows = pl.ds(pl.multiple_of(n * W, W), W)
        dk_ref[cur_rows, :] += dkb[W:]
        dv_ref[cur_rows, :] += dvb[W:]

        @pl.when(n > 0)
        def _():
            prev_rows = pl.ds(pl.multiple_of((n - 1) * W, W), W)
            dk_ref[prev_rows, :] += dkb[:W]
            dv_ref[prev_rows, :] += dvb[:W]

    prev = lambda n: (jnp.maximum(n - 1, 0), 0)
    cur = lambda n: (n, 0)
    return pl.pallas_call(
        body, name="swa_attn_bwd", grid=(T // W,),
        in_specs=[pl.BlockSpec((W, D), cur), pl.BlockSpec((W, nk), prev), pl.BlockSpec((W, nk), cur),
                  pl.BlockSpec((W, nk), prev), pl.BlockSpec((W, nk), cur), pl.BlockSpec((W, D), cur),
                  _full(bias.shape), _full(sinks_b.shape)],
        out_specs=[pl.BlockSpec((W, D), cur), _full((T, nk)), _full((T, nk)), _full(sinks_b.shape)],
        out_shape=[jax.ShapeDtypeStruct((T, D), jnp.float32), jax.ShapeDtypeStruct((T, nk), jnp.float32),
                   jax.ShapeDtypeStruct((T, nk), jnp.float32), jax.ShapeDtypeStruct(sinks_b.shape, jnp.float32)],
        compiler_params=_params("arbitrary"),
    )(q, k, k, v, v, do, bias, sinks_b)


def _swa_pre_bwd(x, dxm, vec, dq, dk, dv, w_qkv):
    T = x.shape[0]
    tm = _tile(T, 512)
    nq = SWA_HEADS * SWA_HEAD_DIM
    nk = SWA_KV_HEADS * SWA_HEAD_DIM
    nqkv = nq + 2 * nk

    def body(x_ref, dxm_ref, vec_ref, dq_ref, dk_ref, dv_ref, w_ref, dx_ref, dqkv_ref, ps_ref, db_ref):
        @pl.when(pl.program_id(0) == 0)
        def _():
            ps_ref[...] = jnp.zeros_like(ps_ref)
            db_ref[...] = jnp.zeros_like(db_ref)

        dqkv = jnp.concatenate([dq_ref[...], dk_ref[...], dv_ref[...]], axis=1)
        db_ref[...] += _rowsum(dqkv)
        dqkv_b = dqkv.astype(MXU_DTYPE)
        dqkv_ref[...] = dqkv_b
        dh = _dot_nt(dqkv_b, w_ref[...])
        dx_ref[...] = _modulate_bwd(dh, x_ref[...], vec_ref, R_GMIX, R_SC1, R_SH1, ps_ref, dxm_ref[...])

    return pl.pallas_call(
        body, name="swa_pre_bwd", grid=(T // tm,),
        in_specs=[_rows(tm, D), _rows(tm, D), _full((8, D)), _rows(tm, nq), _rows(tm, nk), _rows(tm, nk),
                  _full(w_qkv.shape)],
        out_specs=[_rows(tm, D), _rows(tm, nqkv), _full((8, D)), _full((1, nqkv))],
        out_shape=[jax.ShapeDtypeStruct((T, D), jnp.float32), jax.ShapeDtypeStruct((T, nqkv), MXU_DTYPE),
                   jax.ShapeDtypeStruct((8, D), jnp.float32), jax.ShapeDtypeStruct((1, nqkv), jnp.float32)],
        compiler_params=_params("arbitrary"),
    )(x, dxm, vec, dq, dk, dv, w_qkv)


def _rot_cols(w):
    half = QK_ROPE // 2
    return jnp.concatenate([-w[..., half:], w[..., :half]], axis=-1)


def _unrot_grad(d_rope, d_rot):
    half = QK_ROPE // 2
    return d_rope + jnp.concatenate([d_rot[..., half:], -d_rot[..., :half]], axis=-1)


def _rope_table(positions):
    half = QK_ROPE // 2
    inv_freq = ROPE_THETA ** (-jnp.arange(half, dtype=jnp.float32) / half)
    ang = positions.astype(jnp.float32)[:, None] * inv_freq
    cos, sin = jnp.cos(ang), jnp.sin(ang)
    return jnp.concatenate([cos, cos, sin, sin], axis=1)


def _sequence_step(x, tgt, positions, vecs, g_q, g_kv, sinks, g_final, wts):
    H = MLA_HEADS
    cs = _rope_table(positions)
    w_dkv = wts["mla_w_dkv"]
    wcat = jnp.concatenate([wts["mla_w_dq"], w_dkv, _rot_cols(w_dkv[:, KV_LORA:])], axis=1)
    uq = wts["mla_w_uq"].reshape(Q_LORA, H, QK_NOPE + QK_ROPE)
    wuq = jnp.concatenate([uq, _rot_cols(uq[..., QK_NOPE:])], axis=-1).transpose(1, 0, 2)
    wukv = wts["mla_w_ukv"].reshape(KV_LORA, H, QK_NOPE + V_DIM).transpose(1, 0, 2)
    zero_bias = jnp.zeros((1, D), jnp.float32)
    bias = _swa_bias()
    sinks_b = jnp.broadcast_to(sinks.reshape(SWA_HEADS, 1), (SWA_HEADS, LANES))

    h1a, z, q, k, v = _mla_pre(x, vecs[0], wcat, g_q, g_kv, wuq, wukv, cs)
    o_a, lse = _mla_attn_fwd(q, k, v)
    y1a, xm_a, h2a = _post_attn(o_a, x, wts["mla_w_o"], zero_bias, vecs[0])
    a_a, y2a, x1 = _mlp_fwd(h2a, wts["w_ff1"], wts["w_ff2"], 0, xm_a, vecs[0])

    h1b, qs, ks, vs = _swa_pre(x1, vecs[1], wts["swa_w_qkv"], wts["swa_b_qkv"])
    o_b = _swa_attn_fwd(qs, ks, vs, bias, sinks_b)
    y1b, xm_b, h2b = _post_attn(o_b, x1, wts["swa_w_o"], wts["swa_b_o"], vecs[1])
    a_b, y2b, x2 = _mlp_fwd(h2b, wts["w_ff1"], wts["w_ff2"], 1, xm_b, vecs[1])

    loss8, dx2, dg_final = _final_loss(x2, tgt, g_final.reshape(1, D))

    du_b, dy2b, dxm_b, ps_mlp_b = _mlp_bwd(dx2, y2b, a_b, wts["w_ff1"], wts["w_ff2"], 1, xm_b, vecs[1])
    g_ff2 = _mm_tn(a_b, dy2b, "dw_ff2_l1", "rows", DEPTH, 1)
    g_ff1 = _mm_tn(h2b, du_b, "dw_ff1_l1", "cols", DEPTH, 1)
    dy1b, do_b, ps_out_b = _attn_out_bwd(dxm_b, y1b, o_b, wts["swa_w_o"], vecs[1], False)
    g_swa_o = _mm_tn(o_b, dy1b, "dw_o_swa")
    dqs, dks, dvs, dsinks = _swa_attn_bwd(qs, ks, vs, do_b, bias, sinks_b)
    dx1, dqkv, ps_pre_b, g_swa_bqkv = _swa_pre_bwd(x1, dxm_b, vecs[1], dqs, dks, dvs, wts["swa_w_qkv"])
    g_swa_qkv = _mm_tn(h1b, dqkv, "dw_qkv", "cols")

    du_a, dy2a, dxm_a, ps_mlp_a = _mlp_bwd(dx1, y2a, a_a, wts["w_ff1"], wts["w_ff2"], 0, xm_a, vecs[0])
    g_ff2 = _mm_tn(a_a, dy2a, "dw_ff2_l0", "rows", DEPTH, 0, g_ff2)
    g_ff1 = _mm_tn(h2a, du_a, "dw_ff1_l0", "cols", DEPTH, 0, g_ff1)
    dy1a, do_a, ps_out_a, delta = _attn_out_bwd(dxm_a, y1a, o_a, wts["mla_w_o"], vecs[0], True)
    g_mla_o = _mm_tn(o_a, dy1a, "dw_o_mla")
    dq, dk, dv = _mla_attn_bwd(q, k, v, do_a, lse, delta)
    dx0, ps_pre_a, dg_q, dg_kv, dwcat, dwuq, dwukv = _mla_pre_bwd(
        x, dxm_a, vecs[0], h1a, z, dq, dk, dv, cs, wcat, g_q, g_kv, wuq, wukv)

    c0, c1, c2 = Q_LORA, Q_LORA + KV_LORA, Q_LORA + KV_LORA + QK_ROPE
    g_dq = dwcat[:, :c0]
    g_dkv = jnp.concatenate([dwcat[:, c0:c1], _unrot_grad(dwcat[:, c1:c2], dwcat[:, c2:])], axis=1)
    e0 = QK_NOPE + QK_ROPE
    g_uq = jnp.concatenate([dwuq[..., :QK_NOPE], _unrot_grad(dwuq[..., QK_NOPE:e0], dwuq[..., e0:])], axis=-1)
    per = H // N_CHIPS
    g_uq = g_uq.reshape(N_CHIPS, per, Q_LORA, e0).transpose(0, 2, 1, 3).reshape(N_CHIPS, Q_LORA, per * e0)
    g_ukv = dwukv.reshape(N_CHIPS, per, KV_LORA, QK_NOPE + V_DIM).transpose(0, 2, 1, 3)
    g_ukv = g_ukv.reshape(N_CHIPS, KV_LORA, per * (QK_NOPE + V_DIM))

    def dmod(ps_pre, ps_out, ps_mlp):
        return jnp.concatenate([ps_pre[R_SH1:R_SC1 + 1], ps_out[R_GT1:R_GT1 + 1], ps_mlp[R_SH2:R_GT2 + 1]], axis=0)

    rows4 = lambda g: g.reshape(N_CHIPS, g.shape[0] // N_CHIPS, g.shape[1])
    grads = {
        "mla_w_dq": rows4(g_dq), "mla_w_uq": g_uq, "mla_w_dkv": rows4(g_dkv), "mla_w_ukv": g_ukv,
        "mla_w_o": rows4(g_mla_o), "swa_w_qkv": g_swa_qkv.reshape(N_CHIPS, D, -1), "swa_w_o": rows4(g_swa_o),
        "w_ff1": g_ff1.reshape(N_CHIPS, DEPTH * D, -1), "w_ff2": g_ff2.reshape(N_CHIPS, -1, D),
    }
    small = {
        "dmod": jnp.stack([dmod(ps_pre_a, ps_out_a, ps_mlp_a), dmod(ps_pre_b, ps_out_b, ps_mlp_b)]).reshape(DEPTH, 6 * D),
        "g_mix": jnp.stack([ps_pre_a[R_GMIX], ps_pre_b[R_GMIX]]),
        "g_mlp": jnp.stack([ps_mlp_a[R_GMLP], ps_mlp_b[R_GMLP]]),
        "mla_g_q": dg_q, "mla_g_kv": dg_kv, "swa_sinks": dsinks[:, 0].reshape(1, SWA_HEADS),
        "swa_b_qkv": g_swa_bqkv, "swa_b_o": ps_out_b[R_BO:R_BO + 1],
        "g_final": dg_final.reshape(D), "loss": loss8[0, 0],
    }
    return dx0, grads, small


SHARDED = {
    "mla_w_dq": (1, D // N_CHIPS, Q_LORA),
    "mla_w_uq": (1, Q_LORA, MLA_HEADS * (QK_NOPE + QK_ROPE) // N_CHIPS),
    "mla_w_dkv": (1, D // N_CHIPS, KV_LORA + QK_ROPE),
    "mla_w_ukv": (1, KV_LORA, MLA_HEADS * (QK_NOPE + V_DIM) // N_CHIPS),
    "mla_w_o": (1, MLA_HEADS * V_DIM // N_CHIPS, D),
    "swa_w_qkv": (1, D, (SWA_HEADS + 2 * SWA_KV_HEADS) * SWA_HEAD_DIM // N_CHIPS),
    "swa_w_o": (1, SWA_HEADS * SWA_HEAD_DIM // N_CHIPS, D),
    "w_ff1": (DEPTH, D, D_FF // N_CHIPS),
    "w_ff2": (DEPTH, D_FF // N_CHIPS, D),
}
COL_SPLIT = ("mla_w_uq", "mla_w_ukv", "swa_w_qkv")
BIASES = {"swa_b_qkv": (SWA_HEADS + 2 * SWA_KV_HEADS) * SWA_HEAD_DIM, "swa_b_o": D}


def _view2d(name):
    shape = SHARDED[name]
    return math.prod(shape[:-1]), shape[-1]


SMALL = {"b_ada": (DEPTH, 6 * D), "g_mix": (DEPTH, D), "g_mlp": (DEPTH, D), "mla_g_q": (1, Q_LORA),
         "mla_g_kv": (1, KV_LORA), "swa_sinks": (1, SWA_HEADS), "g_final": (D,), "loss": (),
         "swa_b_qkv": (1, BIASES["swa_b_qkv"]), "swa_b_o": (1, BIASES["swa_b_o"])}
SMALL_ROWS = 168
DMA_ROWS = 256


def _small_slots():
    slots, off = {}, 0
    for name, shape in SMALL.items():
        n = max(math.prod(shape), 1)
        slots[name] = (off, n)
        off += -(-n // LANES) * LANES
    assert off <= SMALL_ROWS * LANES
    return slots


def _pack_small(vals):
    parts, end = [], 0
    for name, (off, n) in _small_slots().items():
        pad = -(-n // LANES) * LANES - n
        v = vals[name].astype(jnp.float32).reshape(-1) if name in vals else jnp.zeros((n,), jnp.float32)
        parts += [v, jnp.zeros((pad,), jnp.float32)]
        end = off + n + pad
    parts.append(jnp.zeros((SMALL_ROWS * LANES - end,), jnp.float32))
    return jnp.concatenate(parts).reshape(SMALL_ROWS, LANES)


def _unpack_small(buf):
    flat = buf.reshape(-1)
    return {name: flat[off:off + n].reshape(SMALL[name]) for name, (off, n) in _small_slots().items()}


def _pieces(rows):
    return [(off, min(DMA_ROWS, rows - off)) for off in range(0, rows, DMA_ROWS)]


HBM = pl.BlockSpec(memory_space=pltpu.HBM)
MESH = pl.DeviceIdType.MESH


def _place():
    x, y, c = lax.axis_index("x"), lax.axis_index("y"), lax.axis_index("c")
    chips = [(1 - x, y), (x, 1 - y), (1 - x, 1 - y)]
    return x, y, c, chips


def _all_gather(block):
    m_per, n = block.shape

    def body(x_ref, out_ref, send_sems, recv_sems, local_sem):
        x, y, c, chips = _place()
        me, sibling = (x, y, c), (x, y, 1 - c)

        def rows(px, py, pc):
            return out_ref.at[pl.ds((4 * px + 2 * py + pc) * m_per, m_per), :]

        def copy(k, blk, to, src=None):
            return pltpu.make_async_remote_copy(
                src_ref=rows(*blk) if src is None else src, dst_ref=rows(*blk),
                send_sem=send_sems.at[k], recv_sem=recv_sems.at[k], device_id=to, device_id_type=MESH)

        mine = pltpu.make_async_copy(x_ref, rows(*me), local_sem)
        mine.start()
        first = [copy(0, me, sibling, src=x_ref)]
        first += [copy(1 + j, me, (*chip, c), src=x_ref) for j, chip in enumerate(chips)]
        for cp in first:
            cp.start()
        passed = [copy(4 + j, (*chip, c), sibling) for j, chip in enumerate(chips)]
        for j, chip in enumerate(chips):
            copy(1 + j, (*chip, c), me).wait_recv()
            passed[j].start()
        copy(0, sibling, me).wait_recv()
        for j, chip in enumerate(chips):
            copy(4 + j, (*chip, 1 - c), me).wait_recv()
        for cp in first + passed:
            cp.wait_send()
        mine.wait()

    out = pl.pallas_call(
        body, name="all_gather_small",
        out_shape=jax.ShapeDtypeStruct((N_DEV * m_per, n), block.dtype),
        in_specs=[pl.BlockSpec(memory_space=pltpu.VMEM)],
        out_specs=pl.BlockSpec(memory_space=pltpu.VMEM),
        scratch_shapes=[pltpu.SemaphoreType.DMA((7,)), pltpu.SemaphoreType.DMA((7,)), pltpu.SemaphoreType.DMA],
    )(block)
    return out.reshape(N_DEV, m_per, n)


def _weight_gather(shards):
    nt = len(shards)

    def body(*refs):
        w_refs, out_refs = refs[:nt], refs[nt:2 * nt]
        send_sems, recv_sems, local_sems = refs[2 * nt:]
        x, y, c, chips = _place()
        sibling = (x, y, 1 - c)

        def slab(t, px, py, half):
            rh = shards[t].shape[0] // 2
            return out_refs[t].at[2 * px + py, pl.ds(half * rh, rh), :]

        def copy(t, k, src, dst, to):
            return pltpu.make_async_remote_copy(src_ref=src, dst_ref=dst, send_sem=send_sems.at[6 * t + k],
                                                recv_sem=recv_sems.at[6 * t + k], device_id=to, device_id_type=MESH)

        mine = [pltpu.make_async_copy(w_refs[t], out_refs[t].at[2 * x + y], local_sems.at[t]) for t in range(nt)]
        for t in range(nt):
            for off, n in _pieces(shards[t].shape[0]):
                pltpu.make_async_copy(w_refs[t].at[pl.ds(off, n), :], out_refs[t].at[2 * x + y, pl.ds(off, n), :],
                                      local_sems.at[t]).start()
        first = []
        for t in range(nt):
            rh = shards[t].shape[0] // 2
            first += [copy(t, j, w_refs[t].at[pl.ds(c * rh, rh), :], slab(t, x, y, c), (*chip, c))
                      for j, chip in enumerate(chips)]
        for cp in first:
            cp.start()
        passed = []
        for t in range(nt):
            for j, chip in enumerate(chips):
                copy(t, j, slab(t, *chip, c), slab(t, *chip, c), (*chip, c)).wait_recv()
                rh = shards[t].shape[0] // 2
                for off, n in _pieces(rh):
                    piece = out_refs[t].at[2 * chip[0] + chip[1], pl.ds(c * rh + off, n), :]
                    copy(t, 3 + j, piece, piece, sibling).start()
                passed.append(copy(t, 3 + j, slab(t, *chip, c), slab(t, *chip, c), sibling))
        for t in range(nt):
            for j, chip in enumerate(chips):
                copy(t, 3 + j, slab(t, *chip, 1 - c), slab(t, *chip, 1 - c), sibling).wait_recv()
        for cp in first + passed:
            cp.wait_send()
        for cp in mine:
            cp.wait()

    return pl.pallas_call(
        body, name="weight_gather",
        out_shape=[jax.ShapeDtypeStruct((N_CHIPS,) + s.shape, s.dtype) for s in shards],
        in_specs=[HBM] * nt, out_specs=[HBM] * nt,
        scratch_shapes=[pltpu.SemaphoreType.DMA((6 * nt,)), pltpu.SemaphoreType.DMA((6 * nt,)),
                        pltpu.SemaphoreType.DMA((nt,))],
    )(*shards)


def _grad_pair_in(grads):
    nt = len(grads)

    def body(*refs):
        g_refs, got_refs = refs[:nt], refs[nt:2 * nt]
        send_sems, recv_sems = refs[2 * nt:]
        x, y, c, _ = _place()
        sibling = (x, y, 1 - c)

        def copy(t, src, dst):
            return pltpu.make_async_remote_copy(src_ref=src, dst_ref=dst, send_sem=send_sems.at[t],
                                                recv_sem=recv_sems.at[t], device_id=sibling, device_id_type=MESH)

        for t in range(nt):
            rh = grads[t].shape[1] // 2
            for p in range(N_CHIPS):
                for off, n in _pieces(rh):
                    copy(t, g_refs[t].at[p, pl.ds((1 - c) * rh + off, n), :], got_refs[t].at[p, pl.ds(off, n), :]).start()
        for t in range(nt):
            rh = grads[t].shape[1] // 2
            copy(t, g_refs[t].at[:, pl.ds((1 - c) * rh, rh), :], got_refs[t]).wait()

    return pl.pallas_call(
        body, name="grad_pair_in",
        out_shape=[jax.ShapeDtypeStruct((N_CHIPS, g.shape[1] // 2, g.shape[2]), g.dtype) for g in grads],
        in_specs=[HBM] * nt, out_specs=[HBM] * nt,
        scratch_shapes=[pltpu.SemaphoreType.DMA((nt,)), pltpu.SemaphoreType.DMA((nt,))],
    )(*grads)


def _pair_sum(g, got, core, name):
    _, rows, cols = g.shape
    rh = rows // 2
    tr = _tile(rh, 512)
    nb = rh // tr

    def body(c_ref, g_ref, got_ref, s32_ref, s16_ref):
        s = g_ref[...] + got_ref[...]
        s32_ref[...] = s
        s16_ref[...] = s.astype(s16_ref.dtype)

    blk = pl.BlockSpec((None, tr, cols), lambda p, i, c_ref: (p, i, 0))
    return pl.pallas_call(
        body, name=name,
        grid_spec=pltpu.PrefetchScalarGridSpec(
            num_scalar_prefetch=1, grid=(N_CHIPS, nb),
            in_specs=[pl.BlockSpec((None, tr, cols), lambda p, i, c_ref: (p, c_ref[0] * nb + i, 0)), blk],
            out_specs=[blk, blk]),
        out_shape=[jax.ShapeDtypeStruct((N_CHIPS, rh, cols), jnp.float32),
                   jax.ShapeDtypeStruct((N_CHIPS, rh, cols), jnp.bfloat16)],
        compiler_params=_params("parallel", "parallel"),
    )(core, g, got)


def _grad_chip_exchange(parts):
    nt = len(parts)

    def body(*refs):
        a_refs, got_refs = refs[:nt], refs[nt:2 * nt]
        send_sems, recv_sems = refs[2 * nt:]
        x, y, c, chips = _place()
        sends = [pltpu.make_async_remote_copy(
            src_ref=a_refs[t].at[2 * cx + cy], dst_ref=got_refs[t].at[j], send_sem=send_sems.at[3 * t + j],
            recv_sem=recv_sems.at[3 * t + j], device_id=(cx, cy, c), device_id_type=MESH)
            for t in range(nt) for j, (cx, cy) in enumerate(chips)]
        for cp in sends:
            cp.start()
        for cp in sends:
            cp.wait_recv()
        for cp in sends:
            cp.wait_send()

    return pl.pallas_call(
        body, name="grad_chip_exchange",
        out_shape=[jax.ShapeDtypeStruct((N_CHIPS - 1,) + a.shape[1:], a.dtype) for a in parts],
        in_specs=[HBM] * nt, out_specs=[HBM] * nt,
        scratch_shapes=[pltpu.SemaphoreType.DMA((3 * nt,)), pltpu.SemaphoreType.DMA((3 * nt,))],
    )(*parts)


def _chip_sum(s32, got, chip, name):
    _, rh, cols = s32.shape
    tr = _tile(rh, 512)

    def body(p_ref, s_ref, got_ref, o_ref):
        acc = s_ref[...]
        for j in range(N_CHIPS - 1):
            acc = acc + got_ref[j].astype(jnp.float32)
        o_ref[...] = acc

    return pl.pallas_call(
        body, name=name,
        grid_spec=pltpu.PrefetchScalarGridSpec(
            num_scalar_prefetch=1, grid=(rh // tr,),
            in_specs=[pl.BlockSpec((None, tr, cols), lambda i, p_ref: (p_ref[0], i, 0)),
                      pl.BlockSpec((N_CHIPS - 1, tr, cols), lambda i, p_ref: (0, i, 0))],
            out_specs=pl.BlockSpec((tr, cols), lambda i, p_ref: (i, 0))),
        out_shape=jax.ShapeDtypeStruct((rh, cols), jnp.float32),
        compiler_params=_params("parallel"),
    )(chip, s32, got)


def _grad_pair_out(halves):
    nt = len(halves)

    def body(*refs):
        h_refs, full_refs = refs[:nt], refs[nt:2 * nt]
        send_sems, recv_sems, local_sems = refs[2 * nt:]
        x, y, c, _ = _place()
        sibling = (x, y, 1 - c)

        def copy(t, src, dst):
            return pltpu.make_async_remote_copy(src_ref=src, dst_ref=dst, send_sem=send_sems.at[t],
                                                recv_sem=recv_sems.at[t], device_id=sibling, device_id_type=MESH)

        keeps = [pltpu.make_async_copy(h_refs[t], full_refs[t].at[c], local_sems.at[t]) for t in range(nt)]
        for t in range(nt):
            for off, n in _pieces(halves[t].shape[0]):
                pltpu.make_async_copy(h_refs[t].at[pl.ds(off, n), :], full_refs[t].at[c, pl.ds(off, n), :],
                                      local_sems.at[t]).start()
                copy(t, h_refs[t].at[pl.ds(off, n), :], full_refs[t].at[c, pl.ds(off, n), :]).start()
        for t in range(nt):
            copy(t, h_refs[t], full_refs[t].at[c]).wait()
        for cp in keeps:
            cp.wait()

    return pl.pallas_call(
        body, name="grad_pair_out",
        out_shape=[jax.ShapeDtypeStruct((2,) + h.shape, h.dtype) for h in halves],
        in_specs=[HBM] * nt, out_specs=[HBM] * nt,
        scratch_shapes=[pltpu.SemaphoreType.DMA((nt,)), pltpu.SemaphoreType.DMA((nt,)), pltpu.SemaphoreType.DMA((nt,))],
    )(*halves)


def _ada_part(c_all, w_ada):
    L, _, ncol = w_ada.shape
    tn = _tile(ncol, 512)

    def body(c_ref, w_ref, cond_ref, part_ref):
        cv = c_ref[...]
        cond = cv * jax.nn.sigmoid(cv)
        cond_ref[...] = cond
        part_ref[0] = jnp.dot(cond, w_ref[0], precision=lax.Precision.HIGHEST, preferred_element_type=jnp.float32)

    return pl.pallas_call(
        body, name="ada_part", grid=(L, ncol // tn),
        in_specs=[_full((N_DEV, D)), pl.BlockSpec((1, D, tn), lambda l, j: (l, 0, j))],
        out_specs=[_full((N_DEV, D)), pl.BlockSpec((1, N_DEV, tn), lambda l, j: (l, 0, j))],
        out_shape=[jax.ShapeDtypeStruct((N_DEV, D), jnp.float32), jax.ShapeDtypeStruct((L, N_DEV, ncol), jnp.float32)],
        compiler_params=_params("arbitrary", "arbitrary"),
    )(c_all, w_ada)


def _adamw_math(w, g, m, v):
    m = ADAM_B1 * m + (1.0 - ADAM_B1) * g
    v = ADAM_B2 * v + (1.0 - ADAM_B2) * jnp.square(g)
    m_hat = m / (1.0 - ADAM_B1 ** ADAM_STEP)
    v_hat = v / (1.0 - ADAM_B2 ** ADAM_STEP)
    delta = -ADAM_LR * (m_hat / (jnp.sqrt(v_hat) + ADAM_EPS) + ADAM_WD * w)
    return delta, m, v


def _adamw(w, g, m, v, name):
    shape = w.shape
    cols = shape[-1]
    rows = math.prod(shape[:-1])
    tr = _tile(rows, 512)
    two_d = lambda t: t.reshape(rows, cols)

    def body(w_ref, g_ref, m_ref, v_ref, d_ref, mo_ref, vo_ref):
        d_ref[...], mo_ref[...], vo_ref[...] = _adamw_math(w_ref[...], g_ref[...], m_ref[...], v_ref[...])

    out = jax.ShapeDtypeStruct((rows, cols), jnp.float32)
    outs = pl.pallas_call(
        body, name=name, grid=(rows // tr,), in_specs=[_rows(tr, cols)] * 4, out_specs=[_rows(tr, cols)] * 3,
        out_shape=[out, out, out], compiler_params=_params("parallel"),
    )(two_d(w), two_d(g), two_d(m), two_d(v))
    return [t.reshape(shape) for t in outs]


def _ada_grad_adamw(cond_t, dm, w, m, v):
    L, _, ncol = w.shape
    tn = _tile(ncol, 512)

    def body(ct_ref, dm_ref, w_ref, m_ref, v_ref, g_ref, d_ref, mo_ref, vo_ref):
        g = ct_ref[:, 0:1] * dm_ref[0, 0:1, :]
        for b in range(1, N_DEV):
            g = g + ct_ref[:, b:b + 1] * dm_ref[0, b:b + 1, :]
        g_ref[0] = g
        d_ref[0], mo_ref[0], vo_ref[0] = _adamw_math(w_ref[0], g, m_ref[0], v_ref[0])

    wblk = pl.BlockSpec((1, D, tn), lambda l, j: (l, 0, j))
    out = jax.ShapeDtypeStruct(w.shape, jnp.float32)
    return pl.pallas_call(
        body, name="ada_grad_adamw", grid=(L, ncol // tn),
        in_specs=[_full((D, N_DEV)), pl.BlockSpec((1, N_DEV, tn), lambda l, j: (l, 0, j)), wblk, wblk, wblk],
        out_specs=[wblk] * 4, out_shape=[out] * 4, compiler_params=_params("parallel", "parallel"),
    )(cond_t, dm, w, m, v)


def _small_adamw(gathered, w, m, v):
    def body(ga_ref, w_ref, m_ref, v_ref, g_ref, d_ref, mo_ref, vo_ref):
        g = ga_ref[0]
        for dev in range(1, N_DEV):
            g = g + ga_ref[dev]
        g_ref[...] = g
        d_ref[...], mo_ref[...], vo_ref[...] = _adamw_math(w_ref[...], g, m_ref[...], v_ref[...])

    out = jax.ShapeDtypeStruct((SMALL_ROWS, LANES), jnp.float32)
    return pl.pallas_call(
        body, name="small_adamw", out_shape=[out] * 4,
        in_specs=[pl.BlockSpec(memory_space=pltpu.VMEM)] * 4, out_specs=[pl.BlockSpec(memory_space=pltpu.VMEM)] * 4,
    )(gathered, w, m, v)


def _one_hot_pick(arr, index, axis):
    n = arr.shape[axis]
    shape = [1] * arr.ndim
    shape[axis] = n
    hot = (jnp.arange(n) == index).astype(arr.dtype).reshape(shape)
    return jnp.sum(arr * hot, axis=axis)


def kernel(x, c, positions, w_ada, b_ada, g_mix, g_mlp, mla_w_dq, mla_g_q, mla_w_uq, mla_w_dkv, mla_g_kv, mla_w_ukv, mla_w_o, swa_w_qkv, swa_b_qkv, swa_sinks, swa_w_o, swa_b_o, w_ff1, w_ff2, g_final, loss_target, m_w_ada, m_b_ada, m_g_mix, m_g_mlp, m_mla_w_dq, m_mla_g_q, m_mla_w_uq, m_mla_w_dkv, m_mla_g_kv, m_mla_w_ukv, m_mla_w_o, m_swa_w_qkv, m_swa_b_qkv, m_swa_sinks, m_swa_w_o, m_swa_b_o, m_w_ff1, m_w_ff2, m_g_final, v_w_ada, v_b_ada, v_g_mix, v_g_mlp, v_mla_w_dq, v_mla_g_q, v_mla_w_uq, v_mla_w_dkv, v_mla_g_kv, v_mla_w_ukv, v_mla_w_o, v_swa_w_qkv, v_swa_b_qkv, v_swa_sinks, v_swa_w_o, v_swa_b_o, v_w_ff1, v_w_ff2, v_g_final):
    W = dict(w_ada=w_ada, b_ada=b_ada, g_mix=g_mix, g_mlp=g_mlp, mla_w_dq=mla_w_dq, mla_g_q=mla_g_q, mla_w_uq=mla_w_uq,
             mla_w_dkv=mla_w_dkv, mla_g_kv=mla_g_kv, mla_w_ukv=mla_w_ukv, mla_w_o=mla_w_o, swa_w_qkv=swa_w_qkv,
             swa_b_qkv=swa_b_qkv, swa_sinks=swa_sinks, swa_w_o=swa_w_o, swa_b_o=swa_b_o, w_ff1=w_ff1, w_ff2=w_ff2,
             g_final=g_final)
    M = dict(w_ada=m_w_ada, b_ada=m_b_ada, g_mix=m_g_mix, g_mlp=m_g_mlp, mla_w_dq=m_mla_w_dq, mla_g_q=m_mla_g_q,
             mla_w_uq=m_mla_w_uq, mla_w_dkv=m_mla_w_dkv, mla_g_kv=m_mla_g_kv, mla_w_ukv=m_mla_w_ukv, mla_w_o=m_mla_w_o,
             swa_w_qkv=m_swa_w_qkv, swa_b_qkv=m_swa_b_qkv, swa_sinks=m_swa_sinks, swa_w_o=m_swa_w_o, swa_b_o=m_swa_b_o,
             w_ff1=m_w_ff1, w_ff2=m_w_ff2, g_final=m_g_final)
    V = dict(w_ada=v_w_ada, b_ada=v_b_ada, g_mix=v_g_mix, g_mlp=v_g_mlp, mla_w_dq=v_mla_w_dq, mla_g_q=v_mla_g_q,
             mla_w_uq=v_mla_w_uq, mla_w_dkv=v_mla_w_dkv, mla_g_kv=v_mla_g_kv, mla_w_ukv=v_mla_w_ukv, mla_w_o=v_mla_w_o,
             swa_w_qkv=v_swa_w_qkv, swa_b_qkv=v_swa_b_qkv, swa_sinks=v_swa_sinks, swa_w_o=v_swa_w_o, swa_b_o=v_swa_b_o,
             w_ff1=v_w_ff1, w_ff2=v_w_ff2, g_final=v_g_final)
    order = list(W)
    names = list(SHARDED)
    core = lax.axis_index("c")
    chip = 2 * lax.axis_index("x") + lax.axis_index("y")
    dev = 2 * chip + core
    core_arr = core.astype(jnp.int32).reshape(1)
    chip_arr = chip.astype(jnp.int32).reshape(1)

    gathered = dict(zip(names, _weight_gather([W[n].astype(MXU_DTYPE).reshape(_view2d(n)) for n in names])))
    wts = {}
    for n in names:
        g = gathered[n]
        if n in ("w_ff1", "w_ff2"):
            wts[n] = g.reshape((N_CHIPS,) + SHARDED[n])
        elif n in COL_SPLIT:
            wts[n] = g.transpose(1, 0, 2).reshape(g.shape[1], N_CHIPS * g.shape[2])
        else:
            wts[n] = g.reshape(N_CHIPS * g.shape[1], g.shape[2])

    nbq, nbo = BIASES["swa_b_qkv"] // N_CHIPS, BIASES["swa_b_o"] // N_CHIPS
    first = jnp.concatenate([c.reshape(-1), swa_b_qkv.reshape(-1), swa_b_o.reshape(-1),
                             jnp.zeros((16 * LANES - D - nbq - nbo,), jnp.float32)]).reshape(16, LANES)
    first_all = _all_gather(first).reshape(N_DEV, 16 * LANES)
    c_all = first_all[:, :D]
    south = first_all[0::2]
    wts["swa_b_qkv"] = south[:, D:D + nbq].reshape(1, N_CHIPS * nbq)
    wts["swa_b_o"] = south[:, D + nbq:D + nbq + nbo].reshape(1, N_CHIPS * nbo)
    cond_all, part = _ada_part(c_all, w_ada)
    ncol = w_ada.shape[2]
    part_all = _all_gather(part.reshape(-1, LANES)).reshape(N_DEV, DEPTH, N_DEV, ncol)
    mine = _one_hot_pick(part_all[0::2], dev, axis=2)
    mod = mine.transpose(1, 0, 2).reshape(DEPTH, N_CHIPS * ncol) + b_ada
    vecs = jnp.concatenate([mod.reshape(DEPTH, 6, D), g_mix[:, None, :], g_mlp[:, None, :]], axis=1)

    grad_x, grads, small = _sequence_step(x[0], loss_target[0], positions[0], vecs, mla_g_q, mla_g_kv, swa_sinks,
                                          g_final, wts)

    small["b_ada"] = small.pop("dmod")
    small_all = _all_gather(_pack_small(small))
    pk = lambda src: _pack_small({n: src[n] for n in SMALL if n != "loss" and n not in BIASES})
    g_small, d_small, m_small, v_small = [_unpack_small(t) for t in _small_adamw(small_all, pk(W), pk(M), pk(V))]
    off, n = _small_slots()["b_ada"]
    dmod_all = small_all.reshape(N_DEV, -1)[:, off:off + n].reshape(N_DEV, DEPTH, N_CHIPS, ncol)
    dm = _one_hot_pick(dmod_all, chip, axis=2).transpose(1, 0, 2)
    ada = _ada_grad_adamw(cond_all.T, dm, w_ada, m_w_ada, v_w_ada)

    gl = [grads[n] for n in names]
    got = _grad_pair_in(gl)
    sums = [_pair_sum(g, s, core_arr, "pair_sum_" + n) for n, g, s in zip(names, gl, got)]
    others = _grad_chip_exchange([s16 for _, s16 in sums])
    halves = [_chip_sum(s32, o, chip_arr, "chip_sum_" + n) for n, (s32, _), o in zip(names, sums, others)]
    g_shard = {n: f.reshape(SHARDED[n]) for n, f in zip(names, _grad_pair_out(halves))}
    for n, width in BIASES.items():
        g_shard[n] = _one_hot_pick(g_small[n].reshape(N_CHIPS, width // N_CHIPS), chip, axis=0).reshape(1, -1)

    res = {}
    for name in order:
        if name == "w_ada":
            res[name] = ada
        elif name in g_shard:
            res[name] = [g_shard[name]] + _adamw(W[name], g_shard[name], M[name], V[name], "adamw_" + name)
        else:
            res[name] = [t[name] for t in (g_small, d_small, m_small, v_small)]
    outs = [g_small["loss"], grad_x[None]]
    for k in range(4):
        outs += [res[name][k] for name in order]
    return tuple(outs)
```

```python
import functools
import math

import jax
import jax.numpy as jnp
import numpy as np
from jax import lax
from jax.experimental import pallas as pl
from jax.experimental.pallas import tpu as pltpu

D = 1024
DEPTH = 2
MLA_HEADS = 8
QK_NOPE = 128
QK_ROPE = 64
V_DIM = 128
Q_LORA = 384
KV_LORA = 256
ROPE_THETA = 10000.0
SWA_HEADS = 16
SWA_KV_HEADS = 4
SWA_HEAD_DIM = 64
SWA_GROUP = SWA_HEADS // SWA_KV_HEADS
WINDOW = 128
D_FF = 4 * D
EPS = 1e-6
ADAM_LR = 0.001
ADAM_B1 = 0.9
ADAM_B2 = 0.999
ADAM_EPS = 1e-08
ADAM_WD = 0.01
ADAM_STEP = 10

N_CHIPS = 4
N_DEV = 8
LANES = 128
QK_EXT = 256
MLA_SCALE = (QK_NOPE + QK_ROPE) ** -0.5
LOG2E = math.log2(math.e)
LN2 = math.log(2.0)
MLA_QSCALE = MLA_SCALE * LOG2E
ATTN_BLOCK = 1024
ATTN_SUB = 512
SWA_SCALE = SWA_HEAD_DIM ** -0.5
NEG = -1e30
MXU_DTYPE = jnp.bfloat16
VMEM_LIMIT = 56 * 1024 * 1024

R_SH1, R_SC1, R_GT1, R_SH2, R_SC2, R_GT2, R_GMIX, R_GMLP = range(8)
R_BO = 6


def _tile(n, pref):
    if n <= pref:
        return n
    for t in range(pref, 7, -1):
        if n % t == 0 and t % 8 == 0:
            return t
    return n


def _dot(a, b):
    return jnp.dot(a, b, preferred_element_type=jnp.float32)


def _dot_nt(a, b):
    return lax.dot_general(a, b, (((1,), (1,)), ((), ())), preferred_element_type=jnp.float32)


def _dot_tn(a, b):
    return lax.dot_general(a, b, (((0,), (0,)), ((), ())), preferred_element_type=jnp.float32)


def _rms(x):
    r = lax.rsqrt(jnp.mean(x * x, axis=-1, keepdims=True) + EPS)
    return x * r, r


def _rms_bwd(dxhat, xhat, r):
    return r * (dxhat - xhat * jnp.mean(dxhat * xhat, axis=-1, keepdims=True))


def _rowsum(v):
    return jnp.sum(v, axis=0, keepdims=True)


def _params(*sem):
    return pltpu.CompilerParams(dimension_semantics=sem, vmem_limit_bytes=VMEM_LIMIT)


def _full(shape):
    nd = len(shape)
    return pl.BlockSpec(shape, lambda *_: (0,) * nd)


def _rows(tm, cols):
    return pl.BlockSpec((tm, cols), lambda i, *_: (i, 0))


def _modulate_bwd(dh, x, vec_ref, r_g, r_sc, r_sh, ps_ref, dres):
    xhat, r = _rms(x)
    g = vec_ref[r_g:r_g + 1, :]
    n = xhat * g
    ps_ref[r_sh:r_sh + 1, :] += _rowsum(dh)
    ps_ref[r_sc:r_sc + 1, :] += _rowsum(dh * n)
    dn = dh * (1.0 + vec_ref[r_sc:r_sc + 1, :])
    ps_ref[r_g:r_g + 1, :] += _rowsum(dn * xhat)
    return dres + _rms_bwd(dn * g, xhat, r)


def _mla_pre(x, vec, wcat, g_q, g_kv, wuq, wukv, cs):
    T = x.shape[0]
    tm = _tile(T, 512)
    H = MLA_HEADS

    def body(x_ref, vec_ref, wcat_ref, gq_ref, gkv_ref, wuq_ref, wukv_ref, cs_ref, h_ref, z_ref, q_ref, k_ref, v_ref):
        xhat, _ = _rms(x_ref[...])
        h = xhat * vec_ref[R_GMIX:R_GMIX + 1, :] * (1.0 + vec_ref[R_SC1:R_SC1 + 1, :]) + vec_ref[R_SH1:R_SH1 + 1, :]
        hb = h.astype(MXU_DTYPE)
        h_ref[...] = hb
        z = _dot(hb, wcat_ref[...])
        z_ref[...] = z
        cq = (_rms(z[:, :Q_LORA])[0] * gq_ref[...]).astype(MXU_DTYPE)
        ckv = (_rms(z[:, Q_LORA:Q_LORA + KV_LORA])[0] * gkv_ref[...]).astype(MXU_DTYPE)
        cs_t = cs_ref[...]
        t = z[:, Q_LORA + KV_LORA:] * cs_t
        k_rope = (t + pltpu.roll(t, QK_ROPE, axis=1)).astype(MXU_DTYPE)
        low = lax.broadcasted_iota(jnp.int32, (1, LANES), 1) < QK_ROPE
        for hd in range(H):
            qf = _dot(cq, wuq_ref[hd])
            tq = qf[:, QK_NOPE:] * cs_t
            tq = tq + pltpu.roll(tq, QK_ROPE, axis=1)
            q_ref[hd, :, :QK_NOPE] = (qf[:, :QK_NOPE] * MLA_QSCALE).astype(MXU_DTYPE)
            q_ref[hd, :, QK_NOPE:] = jnp.where(low, tq * MLA_QSCALE, 0.0).astype(MXU_DTYPE)
            kvf = _dot(ckv, wukv_ref[hd])
            k_ref[hd, :, :QK_NOPE] = kvf[:, :QK_NOPE].astype(MXU_DTYPE)
            k_ref[hd, :, QK_NOPE:] = k_rope
            v_ref[hd] = kvf[:, QK_NOPE:].astype(MXU_DTYPE)

    zc = wcat.shape[1]
    return pl.pallas_call(
        body, name="mla_pre", grid=(T // tm,),
        in_specs=[_rows(tm, D), _full((8, D)), _full(wcat.shape), _full(g_q.shape), _full(g_kv.shape),
                  _full(wuq.shape), _full(wukv.shape), _rows(tm, LANES)],
        out_specs=[_rows(tm, D), _rows(tm, zc),
                   pl.BlockSpec((H, tm, QK_EXT), lambda i: (0, i, 0)),
                   pl.BlockSpec((H, tm, QK_EXT), lambda i: (0, i, 0)),
                   pl.BlockSpec((H, tm, V_DIM), lambda i: (0, i, 0))],
        out_shape=[jax.ShapeDtypeStruct((T, D), MXU_DTYPE), jax.ShapeDtypeStruct((T, zc), jnp.float32),
                   jax.ShapeDtypeStruct((H, T, QK_EXT), MXU_DTYPE), jax.ShapeDtypeStruct((H, T, QK_EXT), MXU_DTYPE),
                   jax.ShapeDtypeStruct((H, T, V_DIM), MXU_DTYPE)],
        compiler_params=_params("parallel"),
    )(x, vec, wcat, g_q, g_kv, wuq, wukv, cs)


def _mla_attn_fwd(q, k, v):
    H, T, _ = q.shape
    tb = _tile(T, ATTN_BLOCK)
    sub = min(ATTN_SUB, tb)
    ns, nb = tb // sub, T // tb

    def body(q_ref, k_ref, v_ref, o_ref, lse_ref, m_sc, l_sc, acc_sc):
        qi, kj = pl.program_id(1), pl.program_id(2)

        @pl.when(kj == 0)
        def _():
            m_sc[...] = jnp.full_like(m_sc, NEG)
            l_sc[...] = jnp.zeros_like(l_sc)
            acc_sc[...] = jnp.zeros_like(acc_sc)

        def update(r, kk, masked):
            rows, keys = pl.ds(r * sub, sub), pl.ds(kk * sub, sub)
            s = _dot_nt(q_ref[0, rows, :], k_ref[0, keys, :])
            if masked:
                row = lax.broadcasted_iota(jnp.int32, (sub, sub), 0)
                col = lax.broadcasted_iota(jnp.int32, (sub, sub), 1)
                s = jnp.where(col <= row, s, NEG)
            m_prev = m_sc[rows, :]
            m_new = jnp.maximum(m_prev, jnp.max(s, axis=1, keepdims=True))
            alpha = jnp.exp2(m_prev - m_new)
            p = jnp.exp2(s - jnp.tile(m_new, (1, sub // LANES)))
            l_sc[rows, :] = alpha * l_sc[rows, :] + jnp.sum(p, axis=1, keepdims=True)
            acc_sc[rows, :] = alpha * acc_sc[rows, :] + _dot(p.astype(MXU_DTYPE), v_ref[0, keys, :])
            m_sc[rows, :] = m_new

        @pl.when(kj < qi)
        def _():
            for kk in range(ns):
                for r in range(ns):
                    update(r, kk, False)

        @pl.when(kj == qi)
        def _():
            for kk in range(ns):
                for r in range(kk, ns):
                    update(r, kk, r == kk)
            l = l_sc[...]
            o_ref[...] = (acc_sc[...] / l).astype(o_ref.dtype)
            lse = m_sc[...] + jnp.log2(l)
            pick = (lax.broadcasted_iota(jnp.int32, (8, LANES), 1) == 0).astype(jnp.float32)
            row = lax.dot_general(pick, lse, (((1,), (1,)), ((), ())), precision=lax.Precision.HIGHEST,
                                  preferred_element_type=jnp.float32)
            lse_ref[0] = row[0:1, :]

    kv_idx = lambda h, i, j: (h, jnp.minimum(i, j), 0)
    return pl.pallas_call(
        body, name="mla_attn_fwd", grid=(H, nb, nb),
        in_specs=[pl.BlockSpec((1, tb, QK_EXT), lambda h, i, j: (h, i, 0)),
                  pl.BlockSpec((1, tb, QK_EXT), kv_idx),
                  pl.BlockSpec((1, tb, V_DIM), kv_idx)],
        out_specs=[pl.BlockSpec((tb, V_DIM), lambda h, i, j: (i, h)),
                   pl.BlockSpec((1, 1, tb), lambda h, i, j: (h, 0, i))],
        out_shape=[jax.ShapeDtypeStruct((T, H * V_DIM), MXU_DTYPE), jax.ShapeDtypeStruct((H, 1, T), jnp.float32)],
        scratch_shapes=[pltpu.VMEM((tb, LANES), jnp.float32), pltpu.VMEM((tb, LANES), jnp.float32),
                        pltpu.VMEM((tb, V_DIM), jnp.float32)],
        compiler_params=_params("parallel", "parallel", "arbitrary"),
    )(q, k, v)


def _post_attn(o, x, w_o, bias, vec):
    T = x.shape[0]
    tm = _tile(T, 512)

    def body(o_ref, x_ref, w_ref, b_ref, vec_ref, y_ref, xm_ref, h_ref):
        y = _dot(o_ref[...], w_ref[...]) + b_ref[...]
        y_ref[...] = y.astype(y_ref.dtype)
        xm = x_ref[...] + vec_ref[R_GT1:R_GT1 + 1, :] * y
        xm_ref[...] = xm
        xhat, _ = _rms(xm)
        h = xhat * vec_ref[R_GMLP:R_GMLP + 1, :] * (1.0 + vec_ref[R_SC2:R_SC2 + 1, :]) + vec_ref[R_SH2:R_SH2 + 1, :]
        h_ref[...] = h.astype(h_ref.dtype)

    return pl.pallas_call(
        body, name="post_attn", grid=(T // tm,),
        in_specs=[_rows(tm, D), _rows(tm, D), _full((D, D)), _full((1, D)), _full((8, D))],
        out_specs=[_rows(tm, D), _rows(tm, D), _rows(tm, D)],
        out_shape=[jax.ShapeDtypeStruct((T, D), MXU_DTYPE), jax.ShapeDtypeStruct((T, D), jnp.float32),
                   jax.ShapeDtypeStruct((T, D), MXU_DTYPE)],
        compiler_params=_params("parallel"),
    )(o, x, w_o, bias, vec)


def _ff_specs(tf, layer):
    per = D_FF // N_CHIPS // tf
    w1 = pl.BlockSpec((None, None, D, tf), lambda i, f: (f // per, layer, 0, f % per))
    w2 = pl.BlockSpec((None, None, tf, D), lambda i, f: (f // per, layer, f % per, 0))
    return w1, w2


def _mlp_fwd(h2, w1, w2, layer, xm, vec):
    T = h2.shape[0]
    tm = _tile(T, 1024)
    tf = _tile(D_FF // N_CHIPS, 512)
    nf = D_FF // tf
    w1_spec, w2_spec = _ff_specs(tf, layer)

    def body(h_ref, w1_ref, w2_ref, xm_ref, vec_ref, a_ref, y_ref, xo_ref, acc):
        f = pl.program_id(1)

        @pl.when(f == 0)
        def _():
            acc[...] = jnp.zeros_like(acc)

        u = jnp.maximum(_dot(h_ref[...], w1_ref[...]), 0.0)
        ab = (u * u).astype(MXU_DTYPE)
        a_ref[...] = ab
        acc[...] += _dot(ab, w2_ref[...])

        @pl.when(f == nf - 1)
        def _():
            y = acc[...]
            y_ref[...] = y.astype(y_ref.dtype)
            xo_ref[...] = xm_ref[...] + vec_ref[R_GT2:R_GT2 + 1, :] * y

    return pl.pallas_call(
        body, name="mlp_fwd", grid=(T // tm, nf),
        in_specs=[_rows(tm, D), w1_spec, w2_spec, _rows(tm, D), _full((8, D))],
        out_specs=[pl.BlockSpec((tm, tf), lambda i, f: (i, f)), _rows(tm, D), _rows(tm, D)],
        out_shape=[jax.ShapeDtypeStruct((T, D_FF), MXU_DTYPE), jax.ShapeDtypeStruct((T, D), MXU_DTYPE),
                   jax.ShapeDtypeStruct((T, D), jnp.float32)],
        scratch_shapes=[pltpu.VMEM((tm, D), jnp.float32)],
        compiler_params=_params("parallel", "arbitrary"),
    )(h2, w1, w2, xm, vec)


def _swa_pre(x, vec, w_qkv, b_qkv):
    T = x.shape[0]
    tm = _tile(T, 512)
    nq = SWA_HEADS * SWA_HEAD_DIM
    nk = SWA_KV_HEADS * SWA_HEAD_DIM

    def body(x_ref, vec_ref, w_ref, b_ref, h_ref, q_ref, k_ref, v_ref):
        xhat, _ = _rms(x_ref[...])
        h = xhat * vec_ref[R_GMIX:R_GMIX + 1, :] * (1.0 + vec_ref[R_SC1:R_SC1 + 1, :]) + vec_ref[R_SH1:R_SH1 + 1, :]
        hb = h.astype(MXU_DTYPE)
        h_ref[...] = hb
        qkv = _dot(hb, w_ref[...]) + b_ref[...]
        q_ref[...] = (qkv[:, :nq] * SWA_SCALE).astype(MXU_DTYPE)
        k_ref[...] = qkv[:, nq:nq + nk].astype(MXU_DTYPE)
        v_ref[...] = qkv[:, nq + nk:].astype(MXU_DTYPE)

    return pl.pallas_call(
        body, name="swa_pre", grid=(T // tm,),
        in_specs=[_rows(tm, D), _full((8, D)), _full(w_qkv.shape), _full(b_qkv.shape)],
        out_specs=[_rows(tm, D), _rows(tm, nq), _rows(tm, nk), _rows(tm, nk)],
        out_shape=[jax.ShapeDtypeStruct((T, D), MXU_DTYPE), jax.ShapeDtypeStruct((T, nq), MXU_DTYPE),
                   jax.ShapeDtypeStruct((T, nk), MXU_DTYPE), jax.ShapeDtypeStruct((T, nk), MXU_DTYPE)],
        compiler_params=_params("parallel"),
    )(x, vec, w_qkv, b_qkv)


def _swa_bias():
    W = WINDOW
    slopes = 2.0 ** (-8.0 * np.arange(1, SWA_HEADS + 1) / SWA_HEADS)
    dist = W + np.arange(W)[None, :] - np.arange(2 * W)[:, None]
    inside = (dist >= 0) & (dist < W)
    bias = np.where(inside[None], -slopes[:, None, None] * dist[None].astype(np.float64), NEG)
    bias = bias.reshape(SWA_KV_HEADS, SWA_GROUP, 2 * W, W).transpose(0, 2, 1, 3)
    return jnp.asarray(bias.reshape(SWA_KV_HEADS, 2 * W, SWA_GROUP * W), jnp.float32)


def _swa_probs(n, kh, qt_ref, kp_ref, kc_ref, bias_ref, sink_ref):
    W, Dh, G = WINDOW, SWA_HEAD_DIM, SWA_GROUP
    qt = jnp.concatenate([qt_ref[(kh * G + g) * Dh:(kh * G + g + 1) * Dh, :] for g in range(G)], axis=1)
    kb = jnp.concatenate([kp_ref[:, kh * Dh:(kh + 1) * Dh], kc_ref[:, kh * Dh:(kh + 1) * Dh]], axis=0)
    s = _dot(kb, qt) + bias_ref[kh]
    key = lax.broadcasted_iota(jnp.int32, (2 * W, 1), 0)
    s = jnp.where((key >= W) | (n > 0), s, NEG)
    sink = sink_ref[kh]
    m = jnp.maximum(jnp.max(s, axis=0, keepdims=True), sink)
    p = jnp.exp(s - m)
    p_sink = jnp.exp(sink - m)
    inv = 1.0 / (jnp.sum(p, axis=0, keepdims=True) + p_sink)
    return qt, kb, p * inv, p_sink * inv


def _swa_attn_fwd(qt, k, v, bias, sink_rows):
    T = qt.shape[1]
    W, Dh, G, Hk = WINDOW, SWA_HEAD_DIM, SWA_GROUP, SWA_KV_HEADS
    nk = Hk * Dh

    def body(qt_ref, kp_ref, kc_ref, vp_ref, vc_ref, bias_ref, sink_ref, ot_ref):
        n = pl.program_id(0)
        for kh in range(Hk):
            _, _, pn, _ = _swa_probs(n, kh, qt_ref, kp_ref, kc_ref, bias_ref, sink_ref)
            vb = jnp.concatenate([vp_ref[:, kh * Dh:(kh + 1) * Dh], vc_ref[:, kh * Dh:(kh + 1) * Dh]], axis=0)
            ot = _dot_tn(vb, pn.astype(MXU_DTYPE))
            for g in range(G):
                ot_ref[(kh * G + g) * Dh:(kh * G + g + 1) * Dh, :] = ot[:, g * W:(g + 1) * W].astype(ot_ref.dtype)

    prev = lambda n: (jnp.maximum(n - 1, 0), 0)
    cur = lambda n: (n, 0)
    col = lambda n: (0, n)
    return pl.pallas_call(
        body, name="swa_attn_fwd", grid=(T // W,),
        in_specs=[pl.BlockSpec((D, W), col), pl.BlockSpec((W, nk), prev), pl.BlockSpec((W, nk), cur),
                  pl.BlockSpec((W, nk), prev), pl.BlockSpec((W, nk), cur), _full(bias.shape), _full(sink_rows.shape)],
        out_specs=pl.BlockSpec((D, W), col),
        out_shape=jax.ShapeDtypeStruct((D, T), MXU_DTYPE),
        compiler_params=_params("parallel"),
    )(qt, k, k, v, v, bias, sink_rows)


def _final_loss(x, tgt, g):
    T = x.shape[0]
    tm = _tile(T, 512)

    def body(x_ref, t_ref, g_ref, loss_ref, dx_ref, dg_ref):
        @pl.when(pl.program_id(0) == 0)
        def _():
            loss_ref[...] = jnp.zeros_like(loss_ref)
            dg_ref[...] = jnp.zeros_like(dg_ref)

        xhat, r = _rms(x_ref[...])
        gv = g_ref[...]
        e = xhat * gv - t_ref[...]
        loss_ref[...] += 0.5 * jnp.sum(jnp.mean(e * e, axis=-1, keepdims=True), axis=0, keepdims=True)
        dy = e * (1.0 / D)
        dg_ref[...] += _rowsum(dy * xhat)
        dx_ref[...] = _rms_bwd(dy * gv, xhat, r)

    return pl.pallas_call(
        body, name="final_loss", grid=(T // tm,),
        in_specs=[_rows(tm, D), _rows(tm, D), _full((1, D))],
        out_specs=[_full((8, LANES)), _rows(tm, D), _full((1, D))],
        out_shape=[jax.ShapeDtypeStruct((8, LANES), jnp.float32), jax.ShapeDtypeStruct((T, D), jnp.float32),
                   jax.ShapeDtypeStruct((1, D), jnp.float32)],
        compiler_params=_params("arbitrary"),
    )(x, tgt, g)


def _mlp_bwd(dxo, y2, a, w1, w2, layer, xm, vec):
    T = dxo.shape[0]
    tm = _tile(T, 1024)
    tf = _tile(D_FF // N_CHIPS, 512)
    nf = D_FF // tf
    w1_spec, w2_spec = _ff_specs(tf, layer)

    def body(dxo_ref, y_ref, a_ref, w1_ref, w2_ref, xm_ref, vec_ref, du_ref, dy_ref, dxm_ref, ps_ref, dyb, acc):
        i, f = pl.program_id(0), pl.program_id(1)

        @pl.when((i == 0) & (f == 0))
        def _():
            ps_ref[...] = jnp.zeros_like(ps_ref)

        @pl.when(f == 0)
        def _():
            dxo_t = dxo_ref[...]
            d = (dxo_t * vec_ref[R_GT2:R_GT2 + 1, :]).astype(MXU_DTYPE)
            dyb[...] = d
            dy_ref[...] = d
            acc[...] = jnp.zeros_like(acc)
            ps_ref[R_GT2:R_GT2 + 1, :] += _rowsum(dxo_t * y_ref[...].astype(jnp.float32))

        da = _dot_nt(dyb[...], w2_ref[...])
        dub = (da * (2.0 * jnp.sqrt(a_ref[...].astype(jnp.float32)))).astype(MXU_DTYPE)
        du_ref[...] = dub
        acc[...] += _dot_nt(dub, w1_ref[...])

        @pl.when(f == nf - 1)
        def _():
            dxm_ref[...] = _modulate_bwd(acc[...], xm_ref[...], vec_ref, R_GMLP, R_SC2, R_SH2, ps_ref, dxo_ref[...])

    return pl.pallas_call(
        body, name="mlp_bwd", grid=(T // tm, nf),
        in_specs=[_rows(tm, D), _rows(tm, D), pl.BlockSpec((tm, tf), lambda i, f: (i, f)), w1_spec, w2_spec,
                  _rows(tm, D), _full((8, D))],
        out_specs=[pl.BlockSpec((tm, tf), lambda i, f: (i, f)), _rows(tm, D), _rows(tm, D), _full((8, D))],
        out_shape=[jax.ShapeDtypeStruct((T, D_FF), MXU_DTYPE), jax.ShapeDtypeStruct((T, D), MXU_DTYPE),
                   jax.ShapeDtypeStruct((T, D), jnp.float32), jax.ShapeDtypeStruct((8, D), jnp.float32)],
        scratch_shapes=[pltpu.VMEM((tm, D), MXU_DTYPE), pltpu.VMEM((tm, D), jnp.float32)],
        compiler_params=_params("arbitrary", "arbitrary"),
    )(dxo, y2, a, w1, w2, xm, vec)


def _mm_tn(a, g, name, split=None, layers=1, layer=0, into=None):
    T, K = a.shape
    N = g.shape[1]
    kq = K // N_CHIPS if split == "rows" else K
    nq = N // N_CHIPS if split == "cols" else N
    bk, bn, bt = _tile(kq, 1024), _tile(nq, 1024), _tile(T, 1024)
    if nq % bn or bn % LANES:
        bn = nq
    kper, nper = kq // bk, nq // bn

    def body(*refs):
        a_ref, g_ref, o_ref = refs[0], refs[1], refs[-1]

        @pl.when(pl.program_id(2) == 0)
        def _():
            o_ref[...] = jnp.zeros_like(o_ref)

        o_ref[...] += _dot_tn(a_ref[...], g_ref[...])

    in_specs = [pl.BlockSpec((bt, bk), lambda k, n, t: (t, k)), pl.BlockSpec((bt, bn), lambda k, n, t: (t, n))]
    args = [a, g]
    aliases = {}
    if split is None:
        out_spec = pl.BlockSpec((bk, bn), lambda k, n, t: (k, n))
        out_shape = jax.ShapeDtypeStruct((K, N), jnp.float32)
    else:
        if split == "cols":
            idx = lambda k, n, t: (n // nper, layer, k, n % nper)
        else:
            idx = lambda k, n, t: (k // kper, layer, k % kper, n)
        out_spec = pl.BlockSpec((None, None, bk, bn), idx)
        out_shape = jax.ShapeDtypeStruct((N_CHIPS, layers, kq, nq), jnp.float32)
        if into is not None:
            in_specs.append(pl.BlockSpec(memory_space=pl.ANY))
            args.append(into)
            aliases = {2: 0}
    return pl.pallas_call(
        body, name=name, grid=(K // bk, N // bn, T // bt), in_specs=in_specs, out_specs=out_spec, out_shape=out_shape,
        input_output_aliases=aliases, compiler_params=_params("parallel", "parallel", "arbitrary"),
    )(*args)


def _attn_out_bwd(dxm, y1, o, w_o, vec, with_delta):
    T = dxm.shape[0]
    tm = _tile(T, 512)
    H = MLA_HEADS

    def body(dxm_ref, y_ref, o_ref, w_ref, vec_ref, dy_ref, do_ref, ps_ref, *delta_ref):
        @pl.when(pl.program_id(0) == 0)
        def _():
            ps_ref[...] = jnp.zeros_like(ps_ref)

        dxm_t = dxm_ref[...]
        dy = dxm_t * vec_ref[R_GT1:R_GT1 + 1, :]
        ps_ref[R_GT1:R_GT1 + 1, :] += _rowsum(dxm_t * y_ref[...].astype(jnp.float32))
        ps_ref[R_BO:R_BO + 1, :] += _rowsum(dy)
        dyb = dy.astype(MXU_DTYPE)
        dy_ref[...] = dyb
        do = _dot_nt(dyb, w_ref[...])
        do_ref[...] = do.astype(do_ref.dtype)
        if with_delta:
            of = o_ref[...].astype(jnp.float32)
            ones = jnp.ones((8, V_DIM), jnp.float32)
            for hd in range(H):
                sl = slice(hd * V_DIM, (hd + 1) * V_DIM)
                d = lax.dot_general(ones, do[:, sl] * of[:, sl], (((1,), (1,)), ((), ())),
                                    precision=lax.Precision.HIGHEST, preferred_element_type=jnp.float32)
                delta_ref[0][hd] = d[0:1, :]

    out_specs = [_rows(tm, D), _rows(tm, D), _full((8, D))]
    out_shape = [jax.ShapeDtypeStruct((T, D), MXU_DTYPE), jax.ShapeDtypeStruct((T, D), MXU_DTYPE),
                 jax.ShapeDtypeStruct((8, D), jnp.float32)]
    if with_delta:
        out_specs.append(pl.BlockSpec((H, 1, tm), lambda i: (0, 0, i)))
        out_shape.append(jax.ShapeDtypeStruct((H, 1, T), jnp.float32))
    return pl.pallas_call(
        body, name="attn_out_bwd_mla" if with_delta else "attn_out_bwd_swa", grid=(T // tm,),
        in_specs=[_rows(tm, D), _rows(tm, D), _rows(tm, D), _full((D, D)), _full((8, D))],
        out_specs=out_specs, out_shape=out_shape,
        compiler_params=_params("arbitrary"),
    )(dxm, y1, o, w_o, vec)


def _mla_attn_bwd(q, k, v, do, lse, delta):
    H, T, _ = q.shape
    tb = _tile(T, ATTN_BLOCK)
    sub = min(ATTN_SUB, tb)
    ns, nb = tb // sub, T // tb

    def body(q_ref, k_ref, v_ref, do_ref, lse_ref, dl_ref, dq_ref, dk_ref, dv_ref, dk_acc, dv_acc):
        j, i = pl.program_id(1), pl.program_id(2)

        @pl.when((j == 0) & (i == 0))
        def _():
            dq_ref[...] = jnp.zeros_like(dq_ref)

        def update(kk, r, masked):
            keys, rows = pl.ds(kk * sub, sub), pl.ds(r * sub, sub)
            kb, qb, dob = k_ref[0, keys, :], q_ref[0, rows, :], do_ref[rows, :]
            st = _dot_nt(kb, qb)
            if masked:
                row = lax.broadcasted_iota(jnp.int32, (sub, sub), 0)
                col = lax.broadcasted_iota(jnp.int32, (sub, sub), 1)
                st = jnp.where(row <= col, st, NEG)
            pt = jnp.exp2(st - lse_ref[0, :, rows])
            dv_acc[keys, :] += _dot(pt.astype(MXU_DTYPE), dob)
            dpt = _dot_nt(v_ref[0, keys, :], dob)
            dst = (pt * (dpt - dl_ref[0, :, rows])).astype(MXU_DTYPE)
            dk_acc[keys, :] += _dot(dst, qb)
            q_rows = pl.ds(pl.multiple_of(i * tb + r * sub, sub), sub)
            dq_ref[0, q_rows, :] += _dot_tn(dst, kb)

        @pl.when(i == j)
        def _():
            dk_acc[...] = jnp.zeros_like(dk_acc)
            dv_acc[...] = jnp.zeros_like(dv_acc)
            for r in range(ns):
                for kk in range(r + 1):
                    update(kk, r, kk == r)

        @pl.when(i > j)
        def _():
            for r in range(ns):
                for kk in range(ns):
                    update(kk, r, False)

        @pl.when(i == nb - 1)
        def _():
            dk_ref[0] = (dk_acc[...] * LN2).astype(dk_ref.dtype)
            dv_ref[0] = dv_acc[...].astype(dv_ref.dtype)

    q_idx = lambda h, j, i: (h, jnp.maximum(i, j), 0)
    kv_idx = lambda h, j, i: (h, j, 0)
    stat_idx = lambda h, j, i: (h, 0, jnp.maximum(i, j))
    return pl.pallas_call(
        body, name="mla_attn_bwd", grid=(H, nb, nb),
        in_specs=[pl.BlockSpec((1, tb, QK_EXT), q_idx), pl.BlockSpec((1, tb, QK_EXT), kv_idx),
                  pl.BlockSpec((1, tb, V_DIM), kv_idx),
                  pl.BlockSpec((tb, V_DIM), lambda h, j, i: (jnp.maximum(i, j), h)),
                  pl.BlockSpec((1, 1, tb), stat_idx), pl.BlockSpec((1, 1, tb), stat_idx)],
        out_specs=[pl.BlockSpec((1, T, QK_EXT), lambda h, j, i: (h, 0, 0)),
                   pl.BlockSpec((1, tb, QK_EXT), kv_idx), pl.BlockSpec((1, tb, V_DIM), kv_idx)],
        out_shape=[jax.ShapeDtypeStruct((H, T, QK_EXT), jnp.float32), jax.ShapeDtypeStruct((H, T, QK_EXT), MXU_DTYPE),
                   jax.ShapeDtypeStruct((H, T, V_DIM), MXU_DTYPE)],
        scratch_shapes=[pltpu.VMEM((tb, QK_EXT), jnp.float32), pltpu.VMEM((tb, V_DIM), jnp.float32)],
        compiler_params=_params("parallel", "arbitrary", "arbitrary"),
    )(q, k, v, do, lse, delta)


def _mla_pre_bwd(x, dxm, vec, hb, z, dq, dk, dv, cs, wcat, g_q, g_kv, wuq, wukv):
    T = x.shape[0]
    tm = _tile(T, 256)
    H = MLA_HEADS
    zc = wcat.shape[1]

    def body(x_ref, dxm_ref, vec_ref, h_ref, z_ref, dq_ref, dk_ref, dv_ref, cs_ref, wcat_ref, gq_ref, gkv_ref,
             wuq_ref, wukv_ref, dx_ref, ps_ref, dgq_ref, dgkv_ref, dwcat_ref, dwuq_ref, dwukv_ref):
        @pl.when(pl.program_id(0) == 0)
        def _():
            for ref in (ps_ref, dgq_ref, dgkv_ref, dwcat_ref, dwuq_ref, dwukv_ref):
                ref[...] = jnp.zeros_like(ref)

        z = z_ref[...]
        cs_t = cs_ref[...]
        cqhat, rq = _rms(z[:, :Q_LORA])
        ckhat, rk = _rms(z[:, Q_LORA:Q_LORA + KV_LORA])
        gq, gkv = gq_ref[...], gkv_ref[...]
        cq = (cqhat * gq).astype(MXU_DTYPE)
        ckv = (ckhat * gkv).astype(MXU_DTYPE)
        dcq = jnp.zeros((tm, Q_LORA), jnp.float32)
        dckv = jnp.zeros((tm, KV_LORA), jnp.float32)
        dkr = jnp.zeros((tm, LANES), jnp.float32)
        for hd in range(H):
            dqh = dq_ref[hd] * MLA_SCALE
            gqh = jnp.concatenate([dqh[:, :QK_NOPE], dqh[:, QK_NOPE:] * cs_t], axis=1).astype(MXU_DTYPE)
            dcq += _dot_nt(gqh, wuq_ref[hd])
            dwuq_ref[hd] += _dot_tn(cq, gqh)
            dkh = dk_ref[hd]
            gkvh = jnp.concatenate([dkh[:, :QK_NOPE], dv_ref[hd]], axis=1)
            dckv += _dot_nt(gkvh, wukv_ref[hd])
            dwukv_ref[hd] += _dot_tn(ckv, gkvh)
            dkr += dkh[:, QK_NOPE:].astype(jnp.float32)
        dgq_ref[...] += _rowsum(dcq * cqhat)
        dgkv_ref[...] += _rowsum(dckv * ckhat)
        dcq_pre = _rms_bwd(dcq * gq, cqhat, rq)
        dckv_pre = _rms_bwd(dckv * gkv, ckhat, rk)
        dkr2 = (dkr + pltpu.roll(dkr, QK_ROPE, axis=1)) * cs_t
        dz = jnp.concatenate([dcq_pre, dckv_pre, dkr2], axis=1).astype(MXU_DTYPE)
        dwcat_ref[...] += _dot_tn(h_ref[...], dz)
        dh = _dot_nt(dz, wcat_ref[...])
        dx_ref[...] = _modulate_bwd(dh, x_ref[...], vec_ref, R_GMIX, R_SC1, R_SH1, ps_ref, dxm_ref[...])

    hblk = lambda w: pl.BlockSpec((H, tm, w), lambda i: (0, i, 0))
    return pl.pallas_call(
        body, name="mla_pre_bwd", grid=(T // tm,),
        in_specs=[_rows(tm, D), _rows(tm, D), _full((8, D)), _rows(tm, D), _rows(tm, zc), hblk(QK_EXT), hblk(QK_EXT),
                  hblk(V_DIM), _rows(tm, LANES), _full(wcat.shape), _full(g_q.shape), _full(g_kv.shape),
                  _full(wuq.shape), _full(wukv.shape)],
        out_specs=[_rows(tm, D), _full((8, D)), _full(g_q.shape), _full(g_kv.shape), _full(wcat.shape),
                   _full(wuq.shape), _full(wukv.shape)],
        out_shape=[jax.ShapeDtypeStruct((T, D), jnp.float32), jax.ShapeDtypeStruct((8, D), jnp.float32),
                   jax.ShapeDtypeStruct(g_q.shape, jnp.float32), jax.ShapeDtypeStruct(g_kv.shape, jnp.float32),
                   jax.ShapeDtypeStruct(wcat.shape, jnp.float32), jax.ShapeDtypeStruct(wuq.shape, jnp.float32),
                   jax.ShapeDtypeStruct(wukv.shape, jnp.float32)],
        compiler_params=_params("arbitrary"),
    )(x, dxm, vec, hb, z, dq, dk, dv, cs, wcat, g_q, g_kv, wuq, wukv)


def _swa_attn_bwd(qt, k, v, dot_, bias, sink_rows):
    T = qt.shape[1]
    W, Dh, G, Hk = WINDOW, SWA_HEAD_DIM, SWA_GROUP, SWA_KV_HEADS
    nk = Hk * Dh

    def body(qt_ref, kp_ref, kc_ref, vp_ref, vc_ref, dot_ref, bias_ref, sink_ref, dqt_ref, dk_ref, dv_ref, dsink_ref):
        n = pl.program_id(0)

        @pl.when(n == 0)
        def _():
            dk_ref[...] = jnp.zeros_like(dk_ref)
            dv_ref[...] = jnp.zeros_like(dv_ref)
            dsink_ref[...] = jnp.zeros_like(dsink_ref)

        dks, dvs = [], []
        for kh in range(Hk):
            qt, kb, pn, p_sink = _swa_probs(n, kh, qt_ref, kp_ref, kc_ref, bias_ref, sink_ref)
            vb = jnp.concatenate([vp_ref[:, kh * Dh:(kh + 1) * Dh], vc_ref[:, kh * Dh:(kh + 1) * Dh]], axis=0)
            dot_h = jnp.concatenate([dot_ref[(kh * G + g) * Dh:(kh * G + g + 1) * Dh, :] for g in range(G)], axis=1)
            dp = _dot(vb, dot_h)
            delta = jnp.sum(pn * dp, axis=0, keepdims=True)
            dsb = (pn * (dp - delta)).astype(MXU_DTYPE)
            dsink_ref[kh] += -p_sink * delta
            dqt = _dot_tn(kb, dsb) * SWA_SCALE
            for g in range(G):
                dqt_ref[(kh * G + g) * Dh:(kh * G + g + 1) * Dh, :] = dqt[:, g * W:(g + 1) * W]
            dks.append(_dot_nt(dsb, qt))
            dvs.append(_dot_nt(pn.astype(MXU_DTYPE), dot_h))
        dkb = jnp.concatenate(dks, axis=1)
        dvb = jnp.concatenate(dvs, axis=1)
        cur_rows = pl.ds(pl.multiple_of(n * W, W), W)
        dk_ref[cur_rows, :] += dkb[W:]
        dv_ref[cur_rows, :] += dvb[W:]

        @pl.when(n > 0)
        def _():
            prev_rows = pl.ds(pl.multiple_of((n - 1) * W, W), W)
            dk_ref[prev_rows, :] += dkb[:W]
            dv_ref[prev_rows, :] += dvb[:W]

    prev = lambda n: (jnp.maximum(n - 1, 0), 0)
    cur = lambda n: (n, 0)
    col = lambda n: (0, n)
    return pl.pallas_call(
        body, name="swa_attn_bwd", grid=(T // W,),
        in_specs=[pl.BlockSpec((D, W), col), pl.BlockSpec((W, nk), prev), pl.BlockSpec((W, nk), cur),
                  pl.BlockSpec((W, nk), prev), pl.BlockSpec((W, nk), cur), pl.BlockSpec((D, W), col),
                  _full(bias.shape), _full(sink_rows.shape)],
        out_specs=[pl.BlockSpec((D, W), col), _full((T, nk)), _full((T, nk)), _full(sink_rows.shape)],
        out_shape=[jax.ShapeDtypeStruct((D, T), jnp.float32), jax.ShapeDtypeStruct((T, nk), jnp.float32),
                   jax.ShapeDtypeStruct((T, nk), jnp.float32), jax.ShapeDtypeStruct(sink_rows.shape, jnp.float32)],
        compiler_params=_params("arbitrary"),
    )(qt, k, k, v, v, dot_, bias, sink_rows)


def _swa_pre_bwd(x, dxm, vec, dq, dk, dv, w_qkv):
    T = x.shape[0]
    tm = _tile(T, 512)
    nq = SWA_HEADS * SWA_HEAD_DIM
    nk = SWA_KV_HEADS * SWA_HEAD_DIM
    nqkv = nq + 2 * nk

    def body(x_ref, dxm_ref, vec_ref, dq_ref, dk_ref, dv_ref, w_ref, dx_ref, dqkv_ref, ps_ref, db_ref):
        @pl.when(pl.program_id(0) == 0)
        def _():
            ps_ref[...] = jnp.zeros_like(ps_ref)
            db_ref[...] = jnp.zeros_like(db_ref)

        dqkv = jnp.concatenate([dq_ref[...], dk_ref[...], dv_ref[...]], axis=1)
        db_ref[...] += _rowsum(dqkv)
        dqkv_b = dqkv.astype(MXU_DTYPE)
        dqkv_ref[...] = dqkv_b
        dh = _dot_nt(dqkv_b, w_ref[...])
        dx_ref[...] = _modulate_bwd(dh, x_ref[...], vec_ref, R_GMIX, R_SC1, R_SH1, ps_ref, dxm_ref[...])

    return pl.pallas_call(
        body, name="swa_pre_bwd", grid=(T // tm,),
        in_specs=[_rows(tm, D), _rows(tm, D), _full((8, D)), _rows(tm, nq), _rows(tm, nk), _rows(tm, nk),
                  _full(w_qkv.shape)],
        out_specs=[_rows(tm, D), _rows(tm, nqkv), _full((8, D)), _full((1, nqkv))],
        out_shape=[jax.ShapeDtypeStruct((T, D), jnp.float32), jax.ShapeDtypeStruct((T, nqkv), MXU_DTYPE),
                   jax.ShapeDtypeStruct((8, D), jnp.float32), jax.ShapeDtypeStruct((1, nqkv), jnp.float32)],
        compiler_params=_params("arbitrary"),
    )(x, dxm, vec, dq, dk, dv, w_qkv)


def _rot_cols(w):
    half = QK_ROPE // 2
    return jnp.concatenate([-w[..., half:], w[..., :half]], axis=-1)


def _unrot_grad(d_rope, d_rot):
    half = QK_ROPE // 2
    return d_rope + jnp.concatenate([d_rot[..., half:], -d_rot[..., :half]], axis=-1)


def _rope_table(positions):
    half = QK_ROPE // 2
    inv_freq = ROPE_THETA ** (-jnp.arange(half, dtype=jnp.float32) / half)
    ang = positions.astype(jnp.float32)[:, None] * inv_freq
    cos, sin = jnp.cos(ang), jnp.sin(ang)
    return jnp.concatenate([cos, cos, sin, sin], axis=1)


def _sequence_step(x, tgt, positions, vecs, g_q, g_kv, sinks, g_final, wts):
    H = MLA_HEADS
    cs = _rope_table(positions)
    w_dkv = wts["mla_w_dkv"]
    wcat = jnp.concatenate([wts["mla_w_dq"], w_dkv, _rot_cols(w_dkv[:, KV_LORA:])], axis=1)
    uq = wts["mla_w_uq"].reshape(Q_LORA, H, QK_NOPE + QK_ROPE)
    wuq = jnp.concatenate([uq, _rot_cols(uq[..., QK_NOPE:])], axis=-1).transpose(1, 0, 2)
    wukv = wts["mla_w_ukv"].reshape(KV_LORA, H, QK_NOPE + V_DIM).transpose(1, 0, 2)
    zero_bias = jnp.zeros((1, D), jnp.float32)
    bias = _swa_bias()
    sink_rows = jnp.broadcast_to(sinks.reshape(SWA_KV_HEADS, 1, SWA_GROUP, 1),
                                 (SWA_KV_HEADS, 1, SWA_GROUP, WINDOW)).reshape(SWA_KV_HEADS, 1, SWA_GROUP * WINDOW)

    h1a, z, q, k, v = _mla_pre(x, vecs[0], wcat, g_q, g_kv, wuq, wukv, cs)
    o_a, lse = _mla_attn_fwd(q, k, v)
    y1a, xm_a, h2a = _post_attn(o_a, x, wts["mla_w_o"], zero_bias, vecs[0])
    a_a, y2a, x1 = _mlp_fwd(h2a, wts["w_ff1"], wts["w_ff2"], 0, xm_a, vecs[0])

    h1b, qs, ks, vs = _swa_pre(x1, vecs[1], wts["swa_w_qkv"], wts["swa_b_qkv"])
    qs_t = qs.T
    o_b = _swa_attn_fwd(qs_t, ks, vs, bias, sink_rows).T
    y1b, xm_b, h2b = _post_attn(o_b, x1, wts["swa_w_o"], wts["swa_b_o"], vecs[1])
    a_b, y2b, x2 = _mlp_fwd(h2b, wts["w_ff1"], wts["w_ff2"], 1, xm_b, vecs[1])

    loss8, dx2, dg_final = _final_loss(x2, tgt, g_final.reshape(1, D))

    du_b, dy2b, dxm_b, ps_mlp_b = _mlp_bwd(dx2, y2b, a_b, wts["w_ff1"], wts["w_ff2"], 1, xm_b, vecs[1])
    g_ff2 = _mm_tn(a_b, dy2b, "dw_ff2_l1", "rows", DEPTH, 1)
    g_ff1 = _mm_tn(h2b, du_b, "dw_ff1_l1", "cols", DEPTH, 1)
    dy1b, do_b, ps_out_b = _attn_out_bwd(dxm_b, y1b, o_b, wts["swa_w_o"], vecs[1], False)
    g_swa_o = _mm_tn(o_b, dy1b, "dw_o_swa")
    dqs_t, dks, dvs, dsinks = _swa_attn_bwd(qs_t, ks, vs, do_b.T, bias, sink_rows)
    dqs = dqs_t.T
    dx1, dqkv, ps_pre_b, g_swa_bqkv = _swa_pre_bwd(x1, dxm_b, vecs[1], dqs, dks, dvs, wts["swa_w_qkv"])
    g_swa_qkv = _mm_tn(h1b, dqkv, "dw_qkv", "cols")

    du_a, dy2a, dxm_a, ps_mlp_a = _mlp_bwd(dx1, y2a, a_a, wts["w_ff1"], wts["w_ff2"], 0, xm_a, vecs[0])
    g_ff2 = _mm_tn(a_a, dy2a, "dw_ff2_l0", "rows", DEPTH, 0, g_ff2)
    g_ff1 = _mm_tn(h2a, du_a, "dw_ff1_l0", "cols", DEPTH, 0, g_ff1)
    dy1a, do_a, ps_out_a, delta = _attn_out_bwd(dxm_a, y1a, o_a, wts["mla_w_o"], vecs[0], True)
    g_mla_o = _mm_tn(o_a, dy1a, "dw_o_mla")
    dq, dk, dv = _mla_attn_bwd(q, k, v, do_a, lse, delta)
    dx0, ps_pre_a, dg_q, dg_kv, dwcat, dwuq, dwukv = _mla_pre_bwd(
        x, dxm_a, vecs[0], h1a, z, dq, dk, dv, cs, wcat, g_q, g_kv, wuq, wukv)

    c0, c1, c2 = Q_LORA, Q_LORA + KV_LORA, Q_LORA + KV_LORA + QK_ROPE
    g_dq = dwcat[:, :c0]
    g_dkv = jnp.concatenate([dwcat[:, c0:c1], _unrot_grad(dwcat[:, c1:c2], dwcat[:, c2:])], axis=1)
    e0 = QK_NOPE + QK_ROPE
    g_uq = jnp.concatenate([dwuq[..., :QK_NOPE], _unrot_grad(dwuq[..., QK_NOPE:e0], dwuq[..., e0:])], axis=-1)
    per = H // N_CHIPS
    g_uq = g_uq.reshape(N_CHIPS, per, Q_LORA, e0).transpose(0, 2, 1, 3).reshape(N_CHIPS, Q_LORA, per * e0)
    g_ukv = dwukv.reshape(N_CHIPS, per, KV_LORA, QK_NOPE + V_DIM).transpose(0, 2, 1, 3)
    g_ukv = g_ukv.reshape(N_CHIPS, KV_LORA, per * (QK_NOPE + V_DIM))

    def dmod(ps_pre, ps_out, ps_mlp):
        return jnp.concatenate([ps_pre[R_SH1:R_SC1 + 1], ps_out[R_GT1:R_GT1 + 1], ps_mlp[R_SH2:R_GT2 + 1]], axis=0)

    rows4 = lambda g: g.reshape(N_CHIPS, g.shape[0] // N_CHIPS, g.shape[1])
    grads = {
        "mla_w_dq": rows4(g_dq), "mla_w_uq": g_uq, "mla_w_dkv": rows4(g_dkv), "mla_w_ukv": g_ukv,
        "mla_w_o": rows4(g_mla_o), "swa_w_qkv": g_swa_qkv.reshape(N_CHIPS, D, -1), "swa_w_o": rows4(g_swa_o),
        "w_ff1": g_ff1.reshape(N_CHIPS, DEPTH * D, -1), "w_ff2": g_ff2.reshape(N_CHIPS, -1, D),
    }
    small = {
        "dmod": jnp.stack([dmod(ps_pre_a, ps_out_a, ps_mlp_a), dmod(ps_pre_b, ps_out_b, ps_mlp_b)]).reshape(DEPTH, 6 * D),
        "g_mix": jnp.stack([ps_pre_a[R_GMIX], ps_pre_b[R_GMIX]]),
        "g_mlp": jnp.stack([ps_mlp_a[R_GMLP], ps_mlp_b[R_GMLP]]),
        "mla_g_q": dg_q, "mla_g_kv": dg_kv, "swa_sinks": jnp.sum(dsinks.reshape(SWA_HEADS, WINDOW), axis=1).reshape(1, SWA_HEADS),
        "swa_b_qkv": g_swa_bqkv, "swa_b_o": ps_out_b[R_BO:R_BO + 1],
        "g_final": dg_final.reshape(D), "loss": loss8[0, 0],
    }
    return dx0, grads, small


SHARDED = {
    "mla_w_dq": (1, D // N_CHIPS, Q_LORA),
    "mla_w_uq": (1, Q_LORA, MLA_HEADS * (QK_NOPE + QK_ROPE) // N_CHIPS),
    "mla_w_dkv": (1, D // N_CHIPS, KV_LORA + QK_ROPE),
    "mla_w_ukv": (1, KV_LORA, MLA_HEADS * (QK_NOPE + V_DIM) // N_CHIPS),
    "mla_w_o": (1, MLA_HEADS * V_DIM // N_CHIPS, D),
    "swa_w_qkv": (1, D, (SWA_HEADS + 2 * SWA_KV_HEADS) * SWA_HEAD_DIM // N_CHIPS),
    "swa_w_o": (1, SWA_HEADS * SWA_HEAD_DIM // N_CHIPS, D),
    "w_ff1": (DEPTH, D, D_FF // N_CHIPS),
    "w_ff2": (DEPTH, D_FF // N_CHIPS, D),
}
COL_SPLIT = ("mla_w_uq", "mla_w_ukv", "swa_w_qkv")
BIASES = {"swa_b_qkv": (SWA_HEADS + 2 * SWA_KV_HEADS) * SWA_HEAD_DIM, "swa_b_o": D}


def _view2d(name):
    shape = SHARDED[name]
    return math.prod(shape[:-1]), shape[-1]


SMALL = {"b_ada": (DEPTH, 6 * D), "g_mix": (DEPTH, D), "g_mlp": (DEPTH, D), "mla_g_q": (1, Q_LORA),
         "mla_g_kv": (1, KV_LORA), "swa_sinks": (1, SWA_HEADS), "g_final": (D,), "loss": (),
         "swa_b_qkv": (1, BIASES["swa_b_qkv"]), "swa_b_o": (1, BIASES["swa_b_o"])}
SMALL_ROWS = 168
DMA_ROWS = 256


def _small_slots():
    slots, off = {}, 0
    for name, shape in SMALL.items():
        n = max(math.prod(shape), 1)
        slots[name] = (off, n)
        off += -(-n // LANES) * LANES
    assert off <= SMALL_ROWS * LANES
    return slots


def _pack_small(vals):
    parts, end = [], 0
    for name, (off, n) in _small_slots().items():
        pad = -(-n // LANES) * LANES - n
        v = vals[name].astype(jnp.float32).reshape(-1) if name in vals else jnp.zeros((n,), jnp.float32)
        parts += [v, jnp.zeros((pad,), jnp.float32)]
        end = off + n + pad
    parts.append(jnp.zeros((SMALL_ROWS * LANES - end,), jnp.float32))
    return jnp.concatenate(parts).reshape(SMALL_ROWS, LANES)


def _unpack_small(buf):
    flat = buf.reshape(-1)
    return {name: flat[off:off + n].reshape(SMALL[name]) for name, (off, n) in _small_slots().items()}


def _pieces(rows):
    return [(off, min(DMA_ROWS, rows - off)) for off in range(0, rows, DMA_ROWS)]


HBM = pl.BlockSpec(memory_space=pltpu.HBM)
MESH = pl.DeviceIdType.MESH


def _place():
    x, y, c = lax.axis_index("x"), lax.axis_index("y"), lax.axis_index("c")
    chips = [(1 - x, y), (x, 1 - y), (1 - x, 1 - y)]
    return x, y, c, chips


def _all_gather(block):
    m_per, n = block.shape

    def body(x_ref, out_ref, send_sems, recv_sems, local_sem):
        x, y, c, chips = _place()
        me, sibling = (x, y, c), (x, y, 1 - c)

        def rows(px, py, pc):
            return out_ref.at[pl.ds((4 * px + 2 * py + pc) * m_per, m_per), :]

        def copy(k, blk, to, src=None):
            return pltpu.make_async_remote_copy(
                src_ref=rows(*blk) if src is None else src, dst_ref=rows(*blk),
                send_sem=send_sems.at[k], recv_sem=recv_sems.at[k], device_id=to, device_id_type=MESH)

        mine = pltpu.make_async_copy(x_ref, rows(*me), local_sem)
        mine.start()
        first = [copy(0, me, sibling, src=x_ref)]
        first += [copy(1 + j, me, (*chip, c), src=x_ref) for j, chip in enumerate(chips)]
        for cp in first:
            cp.start()
        passed = [copy(4 + j, (*chip, c), sibling) for j, chip in enumerate(chips)]
        for j, chip in enumerate(chips):
            copy(1 + j, (*chip, c), me).wait_recv()
            passed[j].start()
        copy(0, sibling, me).wait_recv()
        for j, chip in enumerate(chips):
            copy(4 + j, (*chip, 1 - c), me).wait_recv()
        for cp in first + passed:
            cp.wait_send()
        mine.wait()

    out = pl.pallas_call(
        body, name="all_gather_small",
        out_shape=jax.ShapeDtypeStruct((N_DEV * m_per, n), block.dtype),
        in_specs=[pl.BlockSpec(memory_space=pltpu.VMEM)],
        out_specs=pl.BlockSpec(memory_space=pltpu.VMEM),
        scratch_shapes=[pltpu.SemaphoreType.DMA((7,)), pltpu.SemaphoreType.DMA((7,)), pltpu.SemaphoreType.DMA],
    )(block)
    return out.reshape(N_DEV, m_per, n)


def _weight_gather(shards):
    nt = len(shards)

    def body(*refs):
        w_refs, out_refs = refs[:nt], refs[nt:2 * nt]
        send_sems, recv_sems = refs[2 * nt:]
        x, y, c, chips = _place()
        sibling = (x, y, 1 - c)

        def slab(t, px, py, half):
            rh = shards[t].shape[0] // 2
            return out_refs[t].at[2 * px + py, pl.ds(half * rh, rh), :]

        def copy(t, k, src, dst, to):
            return pltpu.make_async_remote_copy(src_ref=src, dst_ref=dst, send_sem=send_sems.at[6 * t + k],
                                                recv_sem=recv_sems.at[6 * t + k], device_id=to, device_id_type=MESH)

        first = []
        for t in range(nt):
            rh = shards[t].shape[0] // 2
            first += [copy(t, j, w_refs[t].at[pl.ds(c * rh, rh), :], slab(t, x, y, c), (*chip, c))
                      for j, chip in enumerate(chips)]
        for cp in first:
            cp.start()
        passed = []
        for t in range(nt):
            for j, chip in enumerate(chips):
                copy(t, j, slab(t, *chip, c), slab(t, *chip, c), (*chip, c)).wait_recv()
                rh = shards[t].shape[0] // 2
                for off, n in _pieces(rh):
                    piece = out_refs[t].at[2 * chip[0] + chip[1], pl.ds(c * rh + off, n), :]
                    copy(t, 3 + j, piece, piece, sibling).start()
                passed.append(copy(t, 3 + j, slab(t, *chip, c), slab(t, *chip, c), sibling))
        for t in range(nt):
            for j, chip in enumerate(chips):
                copy(t, 3 + j, slab(t, *chip, 1 - c), slab(t, *chip, 1 - c), sibling).wait_recv()
        for cp in first + passed:
            cp.wait_send()

    return pl.pallas_call(
        body, name="weight_gather",
        out_shape=[jax.ShapeDtypeStruct((N_CHIPS,) + s.shape, s.dtype) for s in shards],
        in_specs=[HBM] * nt, out_specs=[HBM] * nt,
        scratch_shapes=[pltpu.SemaphoreType.DMA((6 * nt,)), pltpu.SemaphoreType.DMA((6 * nt,))],
    )(*shards)


def _grad_pair_in(grads):
    nt = len(grads)

    def body(*refs):
        g_refs, got_refs = refs[:nt], refs[nt:2 * nt]
        send_sems, recv_sems = refs[2 * nt:]
        x, y, c, _ = _place()
        sibling = (x, y, 1 - c)

        def copy(t, src, dst):
            return pltpu.make_async_remote_copy(src_ref=src, dst_ref=dst, send_sem=send_sems.at[t],
                                                recv_sem=recv_sems.at[t], device_id=sibling, device_id_type=MESH)

        for t in range(nt):
            rh = grads[t].shape[1] // 2
            for p in range(N_CHIPS):
                for off, n in _pieces(rh):
                    copy(t, g_refs[t].at[p, pl.ds((1 - c) * rh + off, n), :], got_refs[t].at[p, pl.ds(off, n), :]).start()
        for t in range(nt):
            rh = grads[t].shape[1] // 2
            copy(t, g_refs[t].at[:, pl.ds((1 - c) * rh, rh), :], got_refs[t]).wait()

    return pl.pallas_call(
        body, name="grad_pair_in",
        out_shape=[jax.ShapeDtypeStruct((N_CHIPS, g.shape[1] // 2, g.shape[2]), g.dtype) for g in grads],
        in_specs=[HBM] * nt, out_specs=[HBM] * nt,
        scratch_shapes=[pltpu.SemaphoreType.DMA((nt,)), pltpu.SemaphoreType.DMA((nt,))],
    )(*grads)


def _pair_sum(g, got, core, name):
    _, rows, cols = g.shape
    rh = rows // 2
    tr = _tile(rh, 512)
    nb = rh // tr

    def body(c_ref, g_ref, got_ref, s32_ref, s16_ref):
        s = g_ref[...] + got_ref[...]
        s32_ref[...] = s
        s16_ref[...] = s.astype(s16_ref.dtype)

    blk = pl.BlockSpec((None, tr, cols), lambda p, i, c_ref: (p, i, 0))
    return pl.pallas_call(
        body, name=name,
        grid_spec=pltpu.PrefetchScalarGridSpec(
            num_scalar_prefetch=1, grid=(N_CHIPS, nb),
            in_specs=[pl.BlockSpec((None, tr, cols), lambda p, i, c_ref: (p, c_ref[0] * nb + i, 0)), blk],
            out_specs=[blk, blk]),
        out_shape=[jax.ShapeDtypeStruct((N_CHIPS, rh, cols), jnp.float32),
                   jax.ShapeDtypeStruct((N_CHIPS, rh, cols), jnp.bfloat16)],
        compiler_params=_params("parallel", "parallel"),
    )(core, g, got)


def _grad_chip_exchange(parts):
    nt = len(parts)

    def body(*refs):
        a_refs, got_refs = refs[:nt], refs[nt:2 * nt]
        send_sems, recv_sems = refs[2 * nt:]
        x, y, c, chips = _place()
        sends = [pltpu.make_async_remote_copy(
            src_ref=a_refs[t].at[2 * cx + cy], dst_ref=got_refs[t].at[j], send_sem=send_sems.at[3 * t + j],
            recv_sem=recv_sems.at[3 * t + j], device_id=(cx, cy, c), device_id_type=MESH)
            for t in range(nt) for j, (cx, cy) in enumerate(chips)]
        for cp in sends:
            cp.start()
        for cp in sends:
            cp.wait_recv()
        for cp in sends:
            cp.wait_send()

    return pl.pallas_call(
        body, name="grad_chip_exchange",
        out_shape=[jax.ShapeDtypeStruct((N_CHIPS - 1,) + a.shape[1:], a.dtype) for a in parts],
        in_specs=[HBM] * nt, out_specs=[HBM] * nt,
        scratch_shapes=[pltpu.SemaphoreType.DMA((3 * nt,)), pltpu.SemaphoreType.DMA((3 * nt,))],
    )(*parts)


def _chip_sum(s32, got, chip, name):
    _, rh, cols = s32.shape
    tr = _tile(rh, 512)

    def body(p_ref, s_ref, got_ref, o_ref):
        acc = s_ref[...]
        for j in range(N_CHIPS - 1):
            acc = acc + got_ref[j].astype(jnp.float32)
        o_ref[...] = acc

    return pl.pallas_call(
        body, name=name,
        grid_spec=pltpu.PrefetchScalarGridSpec(
            num_scalar_prefetch=1, grid=(rh // tr,),
            in_specs=[pl.BlockSpec((None, tr, cols), lambda i, p_ref: (p_ref[0], i, 0)),
                      pl.BlockSpec((N_CHIPS - 1, tr, cols), lambda i, p_ref: (0, i, 0))],
            out_specs=pl.BlockSpec((tr, cols), lambda i, p_ref: (i, 0))),
        out_shape=jax.ShapeDtypeStruct((rh, cols), jnp.float32),
        compiler_params=_params("parallel"),
    )(chip, s32, got)


def _grad_pair_out(halves):
    nt = len(halves)

    def body(*refs):
        h_refs, got_refs = refs[:nt], refs[nt:2 * nt]
        send_sems, recv_sems = refs[2 * nt:]
        x, y, c, _ = _place()
        sibling = (x, y, 1 - c)

        def copy(t, src, dst):
            return pltpu.make_async_remote_copy(src_ref=src, dst_ref=dst, send_sem=send_sems.at[t],
                                                recv_sem=recv_sems.at[t], device_id=sibling, device_id_type=MESH)

        for t in range(nt):
            for off, n in _pieces(halves[t].shape[0]):
                copy(t, h_refs[t].at[pl.ds(off, n), :], got_refs[t].at[pl.ds(off, n), :]).start()
        for t in range(nt):
            copy(t, h_refs[t], got_refs[t]).wait()

    return pl.pallas_call(
        body, name="grad_pair_out",
        out_shape=[jax.ShapeDtypeStruct(h.shape, h.dtype) for h in halves],
        in_specs=[HBM] * nt, out_specs=[HBM] * nt,
        scratch_shapes=[pltpu.SemaphoreType.DMA((nt,)), pltpu.SemaphoreType.DMA((nt,))],
    )(*halves)


def _ada_part(c_all, w_ada):
    L, _, ncol = w_ada.shape
    tn = _tile(ncol, 512)

    def body(c_ref, w_ref, cond_ref, part_ref):
        cv = c_ref[...]
        cond = cv * jax.nn.sigmoid(cv)
        cond_ref[...] = cond
        part_ref[0] = jnp.dot(cond, w_ref[0], precision=lax.Precision.HIGHEST, preferred_element_type=jnp.float32)

    return pl.pallas_call(
        body, name="ada_part", grid=(L, ncol // tn),
        in_specs=[_full((N_DEV, D)), pl.BlockSpec((1, D, tn), lambda l, j: (l, 0, j))],
        out_specs=[_full((N_DEV, D)), pl.BlockSpec((1, N_DEV, tn), lambda l, j: (l, 0, j))],
        out_shape=[jax.ShapeDtypeStruct((N_DEV, D), jnp.float32), jax.ShapeDtypeStruct((L, N_DEV, ncol), jnp.float32)],
        compiler_params=_params("arbitrary", "arbitrary"),
    )(c_all, w_ada)


def _adamw_math(w, g, m, v):
    m = ADAM_B1 * m + (1.0 - ADAM_B1) * g
    v = ADAM_B2 * v + (1.0 - ADAM_B2) * jnp.square(g)
    m_hat = m / (1.0 - ADAM_B1 ** ADAM_STEP)
    v_hat = v / (1.0 - ADAM_B2 ** ADAM_STEP)
    delta = -ADAM_LR * (m_hat / (jnp.sqrt(v_hat) + ADAM_EPS) + ADAM_WD * w)
    return delta, m, v


def _adamw(w, g, m, v, name):
    shape = w.shape
    cols = shape[-1]
    rows = math.prod(shape[:-1])
    tr = _tile(rows, 512)
    two_d = lambda t: t.reshape(rows, cols)

    def body(w_ref, g_ref, m_ref, v_ref, d_ref, mo_ref, vo_ref):
        d_ref[...], mo_ref[...], vo_ref[...] = _adamw_math(w_ref[...], g_ref[...], m_ref[...], v_ref[...])

    out = jax.ShapeDtypeStruct((rows, cols), jnp.float32)
    outs = pl.pallas_call(
        body, name=name, grid=(rows // tr,), in_specs=[_rows(tr, cols)] * 4, out_specs=[_rows(tr, cols)] * 3,
        out_shape=[out, out, out], compiler_params=_params("parallel"),
    )(two_d(w), two_d(g), two_d(m), two_d(v))
    return [t.reshape(shape) for t in outs]


def _adamw_halves(w, mine, got, m, v, core, name):
    shape = w.shape
    cols = shape[-1]
    rows = math.prod(shape[:-1])
    rh = rows // 2
    tr = _tile(rh, 512)
    nbh = rh // tr
    two_d = lambda t: t.reshape(rows, cols)

    def body(c_ref, w_ref, a_ref, b_ref, m_ref, v_ref, g_ref, d_ref, mo_ref, vo_ref):
        g = jnp.where(pl.program_id(0) // nbh == c_ref[0], a_ref[...], b_ref[...])
        g_ref[...] = g
        d_ref[...], mo_ref[...], vo_ref[...] = _adamw_math(w_ref[...], g, m_ref[...], v_ref[...])

    row = pl.BlockSpec((tr, cols), lambda i, c_ref: (i, 0))

    def half(keep):
        return pl.BlockSpec((tr, cols), lambda i, c_ref: (jnp.where((i // nbh == c_ref[0]) == keep, i % nbh, 0), 0))

    out = jax.ShapeDtypeStruct((rows, cols), jnp.float32)
    outs = pl.pallas_call(
        body, name=name,
        grid_spec=pltpu.PrefetchScalarGridSpec(
            num_scalar_prefetch=1, grid=(rows // tr,),
            in_specs=[row, half(True), half(False), row, row], out_specs=[row] * 4),
        out_shape=[out] * 4, compiler_params=_params("arbitrary"),
    )(core, two_d(w), mine, got, two_d(m), two_d(v))
    return [t.reshape(shape) for t in outs]


def _ada_grad_adamw(cond_t, dm, w, m, v):
    L, _, ncol = w.shape
    tn = _tile(ncol, 512)

    def body(ct_ref, dm_ref, w_ref, m_ref, v_ref, g_ref, d_ref, mo_ref, vo_ref):
        g = ct_ref[:, 0:1] * dm_ref[0, 0:1, :]
        for b in range(1, N_DEV):
            g = g + ct_ref[:, b:b + 1] * dm_ref[0, b:b + 1, :]
        g_ref[0] = g
        d_ref[0], mo_ref[0], vo_ref[0] = _adamw_math(w_ref[0], g, m_ref[0], v_ref[0])

    wblk = pl.BlockSpec((1, D, tn), lambda l, j: (l, 0, j))
    out = jax.ShapeDtypeStruct(w.shape, jnp.float32)
    return pl.pallas_call(
        body, name="ada_grad_adamw", grid=(L, ncol // tn),
        in_specs=[_full((D, N_DEV)), pl.BlockSpec((1, N_DEV, tn), lambda l, j: (l, 0, j)), wblk, wblk, wblk],
        out_specs=[wblk] * 4, out_shape=[out] * 4, compiler_params=_params("parallel", "parallel"),
    )(cond_t, dm, w, m, v)


def _small_adamw(gathered, w, m, v):
    def body(ga_ref, w_ref, m_ref, v_ref, g_ref, d_ref, mo_ref, vo_ref):
        g = ga_ref[0]
        for dev in range(1, N_DEV):
            g = g + ga_ref[dev]
        g_ref[...] = g
        d_ref[...], mo_ref[...], vo_ref[...] = _adamw_math(w_ref[...], g, m_ref[...], v_ref[...])

    out = jax.ShapeDtypeStruct((SMALL_ROWS, LANES), jnp.float32)
    return pl.pallas_call(
        body, name="small_adamw", out_shape=[out] * 4,
        in_specs=[pl.BlockSpec(memory_space=pltpu.VMEM)] * 4, out_specs=[pl.BlockSpec(memory_space=pltpu.VMEM)] * 4,
    )(gathered, w, m, v)


def _one_hot_pick(arr, index, axis):
    n = arr.shape[axis]
    shape = [1] * arr.ndim
    shape[axis] = n
    hot = (jnp.arange(n) == index).astype(arr.dtype).reshape(shape)
    return jnp.sum(arr * hot, axis=axis)


def kernel(x, c, positions, w_ada, b_ada, g_mix, g_mlp, mla_w_dq, mla_g_q, mla_w_uq, mla_w_dkv, mla_g_kv, mla_w_ukv, mla_w_o, swa_w_qkv, swa_b_qkv, swa_sinks, swa_w_o, swa_b_o, w_ff1, w_ff2, g_final, loss_target, m_w_ada, m_b_ada, m_g_mix, m_g_mlp, m_mla_w_dq, m_mla_g_q, m_mla_w_uq, m_mla_w_dkv, m_mla_g_kv, m_mla_w_ukv, m_mla_w_o, m_swa_w_qkv, m_swa_b_qkv, m_swa_sinks, m_swa_w_o, m_swa_b_o, m_w_ff1, m_w_ff2, m_g_final, v_w_ada, v_b_ada, v_g_mix, v_g_mlp, v_mla_w_dq, v_mla_g_q, v_mla_w_uq, v_mla_w_dkv, v_mla_g_kv, v_mla_w_ukv, v_mla_w_o, v_swa_w_qkv, v_swa_b_qkv, v_swa_sinks, v_swa_w_o, v_swa_b_o, v_w_ff1, v_w_ff2, v_g_final):
    W = dict(w_ada=w_ada, b_ada=b_ada, g_mix=g_mix, g_mlp=g_mlp, mla_w_dq=mla_w_dq, mla_g_q=mla_g_q, mla_w_uq=mla_w_uq,
             mla_w_dkv=mla_w_dkv, mla_g_kv=mla_g_kv, mla_w_ukv=mla_w_ukv, mla_w_o=mla_w_o, swa_w_qkv=swa_w_qkv,
             swa_b_qkv=swa_b_qkv, swa_sinks=swa_sinks, swa_w_o=swa_w_o, swa_b_o=swa_b_o, w_ff1=w_ff1, w_ff2=w_ff2,
             g_final=g_final)
    M = dict(w_ada=m_w_ada, b_ada=m_b_ada, g_mix=m_g_mix, g_mlp=m_g_mlp, mla_w_dq=m_mla_w_dq, mla_g_q=m_mla_g_q,
             mla_w_uq=m_mla_w_uq, mla_w_dkv=m_mla_w_dkv, mla_g_kv=m_mla_g_kv, mla_w_ukv=m_mla_w_ukv, mla_w_o=m_mla_w_o,
             swa_w_qkv=m_swa_w_qkv, swa_b_qkv=m_swa_b_qkv, swa_sinks=m_swa_sinks, swa_w_o=m_swa_w_o, swa_b_o=m_swa_b_o,
             w_ff1=m_w_ff1, w_ff2=m_w_ff2, g_final=m_g_final)
    V = dict(w_ada=v_w_ada, b_ada=v_b_ada, g_mix=v_g_mix, g_mlp=v_g_mlp, mla_w_dq=v_mla_w_dq, mla_g_q=v_mla_g_q,
             mla_w_uq=v_mla_w_uq, mla_w_dkv=v_mla_w_dkv, mla_g_kv=v_mla_g_kv, mla_w_ukv=v_mla_w_ukv, mla_w_o=v_mla_w_o,
             swa_w_qkv=v_swa_w_qkv, swa_b_qkv=v_swa_b_qkv, swa_sinks=v_swa_sinks, swa_w_o=v_swa_w_o, swa_b_o=v_swa_b_o,
             w_ff1=v_w_ff1, w_ff2=v_w_ff2, g_final=v_g_final)
    order = list(W)
    names = list(SHARDED)
    core = lax.axis_index("c")
    chip = 2 * lax.axis_index("x") + lax.axis_index("y")
    dev = 2 * chip + core
    core_arr = core.astype(jnp.int32).reshape(1)
    chip_arr = chip.astype(jnp.int32).reshape(1)

    local = [W[n].astype(MXU_DTYPE).reshape(_view2d(n)) for n in names]
    gathered = dict(zip(names, _weight_gather(local)))
    wts = {}
    for n, own in zip(names, local):
        g = lax.dynamic_update_slice(gathered[n], own[None], (chip, 0, 0))
        if n in ("w_ff1", "w_ff2"):
            wts[n] = g.reshape((N_CHIPS,) + SHARDED[n])
        elif n in COL_SPLIT:
            wts[n] = g.transpose(1, 0, 2).reshape(g.shape[1], N_CHIPS * g.shape[2])
        else:
            wts[n] = g.reshape(N_CHIPS * g.shape[1], g.shape[2])

    nbq, nbo = BIASES["swa_b_qkv"] // N_CHIPS, BIASES["swa_b_o"] // N_CHIPS
    first = jnp.concatenate([c.reshape(-1), swa_b_qkv.reshape(-1), swa_b_o.reshape(-1),
                             jnp.zeros((16 * LANES - D - nbq - nbo,), jnp.float32)]).reshape(16, LANES)
    first_all = _all_gather(first).reshape(N_DEV, 16 * LANES)
    c_all = first_all[:, :D]
    south = first_all[0::2]
    wts["swa_b_qkv"] = south[:, D:D + nbq].reshape(1, N_CHIPS * nbq)
    wts["swa_b_o"] = south[:, D + nbq:D + nbq + nbo].reshape(1, N_CHIPS * nbo)
    cond_all, part = _ada_part(c_all, w_ada)
    ncol = w_ada.shape[2]
    part_all = _all_gather(part.reshape(-1, LANES)).reshape(N_DEV, DEPTH, N_DEV, ncol)
    mine = _one_hot_pick(part_all[0::2], dev, axis=2)
    mod = mine.transpose(1, 0, 2).reshape(DEPTH, N_CHIPS * ncol) + b_ada
    vecs = jnp.concatenate([mod.reshape(DEPTH, 6, D), g_mix[:, None, :], g_mlp[:, None, :]], axis=1)

    grad_x, grads, small = _sequence_step(x[0], loss_target[0], positions[0], vecs, mla_g_q, mla_g_kv, swa_sinks,
                                          g_final, wts)

    small["b_ada"] = small.pop("dmod")
    small_all = _all_gather(_pack_small(small))
    pk = lambda src: _pack_small({n: src[n] for n in SMALL if n != "loss" and n not in BIASES})
    g_small, d_small, m_small, v_small = [_unpack_small(t) for t in _small_adamw(small_all, pk(W), pk(M), pk(V))]
    off, n = _small_slots()["b_ada"]
    dmod_all = small_all.reshape(N_DEV, -1)[:, off:off + n].reshape(N_DEV, DEPTH, N_CHIPS, ncol)
    dm = _one_hot_pick(dmod_all, chip, axis=2).transpose(1, 0, 2)
    ada = _ada_grad_adamw(cond_all.T, dm, w_ada, m_w_ada, v_w_ada)

    gl = [grads[n] for n in names]
    got = _grad_pair_in(gl)
    sums = [_pair_sum(g, s, core_arr, "pair_sum_" + n) for n, g, s in zip(names, gl, got)]
    others = _grad_chip_exchange([s16 for _, s16 in sums])
    halves = [_chip_sum(s32, o, chip_arr, "chip_sum_" + n) for n, (s32, _), o in zip(names, sums, others)]
    sibling_halves = _grad_pair_out(halves)

    res = {"w_ada": ada}
    for n, mine_h, got_h in zip(names, halves, sibling_halves):
        res[n] = _adamw_halves(W[n], mine_h, got_h, M[n], V[n], core_arr, "adamw_" + n)
    for n, width in BIASES.items():
        g = _one_hot_pick(g_small[n].reshape(N_CHIPS, width // N_CHIPS), chip, axis=0).reshape(1, -1)
        res[n] = [g] + _adamw(W[n], g, M[n], V[n], "adamw_" + n)
    for name in order:
        if name not in res:
            res[name] = [t[name] for t in (g_small, d_small, m_small, v_small)]
    outs = [g_small["loss"], grad_x[None]]
    for k in range(4):
        outs += [res[name][k] for name in order]
    return tuple(outs)
```

```python
import functools
import math

import jax
import jax.numpy as jnp
import numpy as np
from jax import lax
from jax.experimental import pallas as pl
from jax.experimental.pallas import tpu as pltpu

D = 1024
DEPTH = 2
MLA_HEADS = 8
QK_NOPE = 128
QK_ROPE = 64
V_DIM = 128
Q_LORA = 384
KV_LORA = 256
ROPE_THETA = 10000.0
SWA_HEADS = 16
SWA_KV_HEADS = 4
SWA_HEAD_DIM = 64
SWA_GROUP = SWA_HEADS // SWA_KV_HEADS
WINDOW = 128
D_FF = 4 * D
EPS = 1e-6
ADAM_LR = 0.001
ADAM_B1 = 0.9
ADAM_B2 = 0.999
ADAM_EPS = 1e-08
ADAM_WD = 0.01
ADAM_STEP = 10

N_CHIPS = 4
N_DEV = 8
LANES = 128
QK_EXT = 256
MLA_SCALE = (QK_NOPE + QK_ROPE) ** -0.5
LOG2E = math.log2(math.e)
LN2 = math.log(2.0)
MLA_QSCALE = MLA_SCALE * LOG2E
ATTN_BLOCK = 1024
ATTN_SUB = 512
SWA_SCALE = SWA_HEAD_DIM ** -0.5
NEG = -1e30
MXU_DTYPE = jnp.bfloat16
VMEM_LIMIT = 56 * 1024 * 1024

R_SH1, R_SC1, R_GT1, R_SH2, R_SC2, R_GT2, R_GMIX, R_GMLP = range(8)
R_BO = 6


def _tile(n, pref):
    if n <= pref:
        return n
    for t in range(pref, 7, -1):
        if n % t == 0 and t % 8 == 0:
            return t
    return n


def _dot(a, b):
    return jnp.dot(a, b, preferred_element_type=jnp.float32)


def _dot_nt(a, b):
    return lax.dot_general(a, b, (((1,), (1,)), ((), ())), preferred_element_type=jnp.float32)


def _dot_tn(a, b):
    return lax.dot_general(a, b, (((0,), (0,)), ((), ())), preferred_element_type=jnp.float32)


def _rms(x):
    r = lax.rsqrt(jnp.mean(x * x, axis=-1, keepdims=True) + EPS)
    return x * r, r


def _rms_bwd(dxhat, xhat, r):
    return r * (dxhat - xhat * jnp.mean(dxhat * xhat, axis=-1, keepdims=True))


def _rowsum(v):
    return jnp.sum(v, axis=0, keepdims=True)


def _params(*sem):
    return pltpu.CompilerParams(dimension_semantics=sem, vmem_limit_bytes=VMEM_LIMIT)


def _full(shape):
    nd = len(shape)
    return pl.BlockSpec(shape, lambda *_: (0,) * nd)


def _rows(tm, cols):
    return pl.BlockSpec((tm, cols), lambda i, *_: (i, 0))


def _modulate_bwd(dh, x, vec_ref, r_g, r_sc, r_sh, ps_ref, dres):
    xhat, r = _rms(x)
    g = vec_ref[r_g:r_g + 1, :]
    n = xhat * g
    ps_ref[r_sh:r_sh + 1, :] += _rowsum(dh)
    ps_ref[r_sc:r_sc + 1, :] += _rowsum(dh * n)
    dn = dh * (1.0 + vec_ref[r_sc:r_sc + 1, :])
    ps_ref[r_g:r_g + 1, :] += _rowsum(dn * xhat)
    return dres + _rms_bwd(dn * g, xhat, r)


def _mla_pre(x, vec, wcat, g_q, g_kv, wuq, wukv, cs):
    T = x.shape[0]
    tm = _tile(T, 512)
    H = MLA_HEADS

    def body(x_ref, vec_ref, wcat_ref, gq_ref, gkv_ref, wuq_ref, wukv_ref, cs_ref, h_ref, z_ref, q_ref, k_ref, v_ref):
        xhat, _ = _rms(x_ref[...])
        h = xhat * vec_ref[R_GMIX:R_GMIX + 1, :] * (1.0 + vec_ref[R_SC1:R_SC1 + 1, :]) + vec_ref[R_SH1:R_SH1 + 1, :]
        hb = h.astype(MXU_DTYPE)
        h_ref[...] = hb
        z = _dot(hb, wcat_ref[...])
        z_ref[...] = z
        cq = (_rms(z[:, :Q_LORA])[0] * gq_ref[...]).astype(MXU_DTYPE)
        ckv = (_rms(z[:, Q_LORA:Q_LORA + KV_LORA])[0] * gkv_ref[...]).astype(MXU_DTYPE)
        cs_t = cs_ref[...]
        t = z[:, Q_LORA + KV_LORA:] * cs_t
        k_rope = (t + pltpu.roll(t, QK_ROPE, axis=1)).astype(MXU_DTYPE)
        low = lax.broadcasted_iota(jnp.int32, (1, LANES), 1) < QK_ROPE
        for hd in range(H):
            qf = _dot(cq, wuq_ref[hd])
            tq = qf[:, QK_NOPE:] * cs_t
            tq = tq + pltpu.roll(tq, QK_ROPE, axis=1)
            q_ref[hd, :, :QK_NOPE] = (qf[:, :QK_NOPE] * MLA_QSCALE).astype(MXU_DTYPE)
            q_ref[hd, :, QK_NOPE:] = jnp.where(low, tq * MLA_QSCALE, 0.0).astype(MXU_DTYPE)
            kvf = _dot(ckv, wukv_ref[hd])
            k_ref[hd, :, :QK_NOPE] = kvf[:, :QK_NOPE].astype(MXU_DTYPE)
            k_ref[hd, :, QK_NOPE:] = k_rope
            v_ref[hd] = kvf[:, QK_NOPE:].astype(MXU_DTYPE)

    zc = wcat.shape[1]
    return pl.pallas_call(
        body, name="mla_pre", grid=(T // tm,),
        in_specs=[_rows(tm, D), _full((8, D)), _full(wcat.shape), _full(g_q.shape), _full(g_kv.shape),
                  _full(wuq.shape), _full(wukv.shape), _rows(tm, LANES)],
        out_specs=[_rows(tm, D), _rows(tm, zc),
                   pl.BlockSpec((H, tm, QK_EXT), lambda i: (0, i, 0)),
                   pl.BlockSpec((H, tm, QK_EXT), lambda i: (0, i, 0)),
                   pl.BlockSpec((H, tm, V_DIM), lambda i: (0, i, 0))],
        out_shape=[jax.ShapeDtypeStruct((T, D), MXU_DTYPE), jax.ShapeDtypeStruct((T, zc), jnp.float32),
                   jax.ShapeDtypeStruct((H, T, QK_EXT), MXU_DTYPE), jax.ShapeDtypeStruct((H, T, QK_EXT), MXU_DTYPE),
                   jax.ShapeDtypeStruct((H, T, V_DIM), MXU_DTYPE)],
        compiler_params=_params("parallel"),
    )(x, vec, wcat, g_q, g_kv, wuq, wukv, cs)


def _mla_attn_fwd(q, k, v):
    H, T, _ = q.shape
    tb = _tile(T, ATTN_BLOCK)
    sub = min(ATTN_SUB, tb)
    ns, nb = tb // sub, T // tb

    def body(q_ref, k_ref, v_ref, o_ref, lse_ref, m_sc, l_sc, acc_sc):
        qi, kj = pl.program_id(1), pl.program_id(2)

        @pl.when(kj == 0)
        def _():
            m_sc[...] = jnp.full_like(m_sc, NEG)
            l_sc[...] = jnp.zeros_like(l_sc)
            acc_sc[...] = jnp.zeros_like(acc_sc)

        def update(r, kk, masked):
            rows, keys = pl.ds(r * sub, sub), pl.ds(kk * sub, sub)
            s = _dot_nt(q_ref[0, rows, :], k_ref[0, keys, :])
            if masked:
                row = lax.broadcasted_iota(jnp.int32, (sub, sub), 0)
                col = lax.broadcasted_iota(jnp.int32, (sub, sub), 1)
                s = jnp.where(col <= row, s, NEG)
            m_prev = m_sc[rows, :]
            m_new = jnp.maximum(m_prev, jnp.max(s, axis=1, keepdims=True))
            alpha = jnp.exp2(m_prev - m_new)
            p = jnp.exp2(s - jnp.tile(m_new, (1, sub // LANES)))
            l_sc[rows, :] = alpha * l_sc[rows, :] + jnp.sum(p, axis=1, keepdims=True)
            acc_sc[rows, :] = alpha * acc_sc[rows, :] + _dot(p.astype(MXU_DTYPE), v_ref[0, keys, :])
            m_sc[rows, :] = m_new

        @pl.when(kj < qi)
        def _():
            for kk in range(ns):
                for r in range(ns):
                    update(r, kk, False)

        @pl.when(kj == qi)
        def _():
            for kk in range(ns):
                for r in range(kk, ns):
                    update(r, kk, r == kk)
            l = l_sc[...]
            o_ref[...] = (acc_sc[...] / l).astype(o_ref.dtype)
            lse = m_sc[...] + jnp.log2(l)
            pick = (lax.broadcasted_iota(jnp.int32, (8, LANES), 1) == 0).astype(jnp.float32)
            row = lax.dot_general(pick, lse, (((1,), (1,)), ((), ())), precision=lax.Precision.HIGHEST,
                                  preferred_element_type=jnp.float32)
            lse_ref[0] = row[0:1, :]

    kv_idx = lambda h, i, j: (h, jnp.minimum(i, j), 0)
    return pl.pallas_call(
        body, name="mla_attn_fwd", grid=(H, nb, nb),
        in_specs=[pl.BlockSpec((1, tb, QK_EXT), lambda h, i, j: (h, i, 0)),
                  pl.BlockSpec((1, tb, QK_EXT), kv_idx),
                  pl.BlockSpec((1, tb, V_DIM), kv_idx)],
        out_specs=[pl.BlockSpec((tb, V_DIM), lambda h, i, j: (i, h)),
                   pl.BlockSpec((1, 1, tb), lambda h, i, j: (h, 0, i))],
        out_shape=[jax.ShapeDtypeStruct((T, H * V_DIM), MXU_DTYPE), jax.ShapeDtypeStruct((H, 1, T), jnp.float32)],
        scratch_shapes=[pltpu.VMEM((tb, LANES), jnp.float32), pltpu.VMEM((tb, LANES), jnp.float32),
                        pltpu.VMEM((tb, V_DIM), jnp.float32)],
        compiler_params=_params("parallel", "parallel", "arbitrary"),
    )(q, k, v)


def _post_attn(o, x, w_o, bias, vec):
    T = x.shape[0]
    tm = _tile(T, 512)

    def body(o_ref, x_ref, w_ref, b_ref, vec_ref, y_ref, xm_ref, h_ref):
        y = _dot(o_ref[...], w_ref[...]) + b_ref[...]
        y_ref[...] = y.astype(y_ref.dtype)
        xm = x_ref[...] + vec_ref[R_GT1:R_GT1 + 1, :] * y
        xm_ref[...] = xm
        xhat, _ = _rms(xm)
        h = xhat * vec_ref[R_GMLP:R_GMLP + 1, :] * (1.0 + vec_ref[R_SC2:R_SC2 + 1, :]) + vec_ref[R_SH2:R_SH2 + 1, :]
        h_ref[...] = h.astype(h_ref.dtype)

    return pl.pallas_call(
        body, name="post_attn", grid=(T // tm,),
        in_specs=[_rows(tm, D), _rows(tm, D), _full((D, D)), _full((1, D)), _full((8, D))],
        out_specs=[_rows(tm, D), _rows(tm, D), _rows(tm, D)],
        out_shape=[jax.ShapeDtypeStruct((T, D), MXU_DTYPE), jax.ShapeDtypeStruct((T, D), jnp.float32),
                   jax.ShapeDtypeStruct((T, D), MXU_DTYPE)],
        compiler_params=_params("parallel"),
    )(o, x, w_o, bias, vec)


def _ff_specs(tf):
    per = D_FF // N_CHIPS // tf
    w1 = pl.BlockSpec((None, D, tf), lambda i, f: (f // per, 0, f % per))
    w2 = pl.BlockSpec((None, tf, D), lambda i, f: (f // per, f % per, 0))
    return w1, w2


def _mlp_fwd(h2, w1, w2, xm, vec):
    T = h2.shape[0]
    tm = _tile(T, 1024)
    tf = _tile(D_FF // N_CHIPS, 512)
    nf = D_FF // tf
    w1_spec, w2_spec = _ff_specs(tf)

    def body(h_ref, w1_ref, w2_ref, xm_ref, vec_ref, a_ref, y_ref, xo_ref, acc):
        f = pl.program_id(1)

        @pl.when(f == 0)
        def _():
            acc[...] = jnp.zeros_like(acc)

        u = jnp.maximum(_dot(h_ref[...], w1_ref[...]), 0.0)
        ab = (u * u).astype(MXU_DTYPE)
        a_ref[...] = ab
        acc[...] += _dot(ab, w2_ref[...])

        @pl.when(f == nf - 1)
        def _():
            y = acc[...]
            y_ref[...] = y.astype(y_ref.dtype)
            xo_ref[...] = xm_ref[...] + vec_ref[R_GT2:R_GT2 + 1, :] * y

    return pl.pallas_call(
        body, name="mlp_fwd", grid=(T // tm, nf),
        in_specs=[_rows(tm, D), w1_spec, w2_spec, _rows(tm, D), _full((8, D))],
        out_specs=[pl.BlockSpec((tm, tf), lambda i, f: (i, f)), _rows(tm, D), _rows(tm, D)],
        out_shape=[jax.ShapeDtypeStruct((T, D_FF), MXU_DTYPE), jax.ShapeDtypeStruct((T, D), MXU_DTYPE),
                   jax.ShapeDtypeStruct((T, D), jnp.float32)],
        scratch_shapes=[pltpu.VMEM((tm, D), jnp.float32)],
        compiler_params=_params("parallel", "arbitrary"),
    )(h2, w1, w2, xm, vec)


def _swa_pre(x, vec, w_qkv, b_qkv):
    T = x.shape[0]
    tm = _tile(T, 512)
    nq = SWA_HEADS * SWA_HEAD_DIM
    nk = SWA_KV_HEADS * SWA_HEAD_DIM

    def body(x_ref, vec_ref, w_ref, b_ref, h_ref, q_ref, k_ref, v_ref):
        xhat, _ = _rms(x_ref[...])
        h = xhat * vec_ref[R_GMIX:R_GMIX + 1, :] * (1.0 + vec_ref[R_SC1:R_SC1 + 1, :]) + vec_ref[R_SH1:R_SH1 + 1, :]
        hb = h.astype(MXU_DTYPE)
        h_ref[...] = hb
        qkv = _dot(hb, w_ref[...]) + b_ref[...]
        q_ref[...] = (qkv[:, :nq] * SWA_SCALE).astype(MXU_DTYPE)
        k_ref[...] = qkv[:, nq:nq + nk].astype(MXU_DTYPE)
        v_ref[...] = qkv[:, nq + nk:].astype(MXU_DTYPE)

    return pl.pallas_call(
        body, name="swa_pre", grid=(T // tm,),
        in_specs=[_rows(tm, D), _full((8, D)), _full(w_qkv.shape), _full(b_qkv.shape)],
        out_specs=[_rows(tm, D), _rows(tm, nq), _rows(tm, nk), _rows(tm, nk)],
        out_shape=[jax.ShapeDtypeStruct((T, D), MXU_DTYPE), jax.ShapeDtypeStruct((T, nq), MXU_DTYPE),
                   jax.ShapeDtypeStruct((T, nk), MXU_DTYPE), jax.ShapeDtypeStruct((T, nk), MXU_DTYPE)],
        compiler_params=_params("parallel"),
    )(x, vec, w_qkv, b_qkv)


def _swa_bias():
    W = WINDOW
    slopes = 2.0 ** (-8.0 * np.arange(1, SWA_HEADS + 1) / SWA_HEADS)
    dist = W + np.arange(W)[None, :] - np.arange(2 * W)[:, None]
    inside = (dist >= 0) & (dist < W)
    bias = np.where(inside[None], -slopes[:, None, None] * dist[None].astype(np.float64), NEG)
    bias = bias.reshape(SWA_KV_HEADS, SWA_GROUP, 2 * W, W).transpose(0, 2, 1, 3)
    return jnp.asarray(bias.reshape(SWA_KV_HEADS, 2 * W, SWA_GROUP * W), jnp.float32)


def _swa_probs(n, kh, qt_ref, kp_ref, kc_ref, bias_ref, sink_ref):
    W, Dh, G = WINDOW, SWA_HEAD_DIM, SWA_GROUP
    qt = jnp.concatenate([qt_ref[(kh * G + g) * Dh:(kh * G + g + 1) * Dh, :] for g in range(G)], axis=1)
    kb = jnp.concatenate([kp_ref[:, kh * Dh:(kh + 1) * Dh], kc_ref[:, kh * Dh:(kh + 1) * Dh]], axis=0)
    s = _dot(kb, qt) + bias_ref[kh]
    key = lax.broadcasted_iota(jnp.int32, (2 * W, 1), 0)
    s = jnp.where((key >= W) | (n > 0), s, NEG)
    sink = sink_ref[kh]
    m = jnp.maximum(jnp.max(s, axis=0, keepdims=True), sink)
    p = jnp.exp(s - m)
    p_sink = jnp.exp(sink - m)
    inv = 1.0 / (jnp.sum(p, axis=0, keepdims=True) + p_sink)
    return qt, kb, p * inv, p_sink * inv


def _swa_attn_fwd(qt, k, v, bias, sink_rows):
    T = qt.shape[1]
    W, Dh, G, Hk = WINDOW, SWA_HEAD_DIM, SWA_GROUP, SWA_KV_HEADS
    nk = Hk * Dh

    def body(qt_ref, kp_ref, kc_ref, vp_ref, vc_ref, bias_ref, sink_ref, ot_ref):
        n = pl.program_id(0)
        for kh in range(Hk):
            _, _, pn, _ = _swa_probs(n, kh, qt_ref, kp_ref, kc_ref, bias_ref, sink_ref)
            vb = jnp.concatenate([vp_ref[:, kh * Dh:(kh + 1) * Dh], vc_ref[:, kh * Dh:(kh + 1) * Dh]], axis=0)
            ot = _dot_tn(vb, pn.astype(MXU_DTYPE))
            for g in range(G):
                ot_ref[(kh * G + g) * Dh:(kh * G + g + 1) * Dh, :] = ot[:, g * W:(g + 1) * W].astype(ot_ref.dtype)

    prev = lambda n: (jnp.maximum(n - 1, 0), 0)
    cur = lambda n: (n, 0)
    col = lambda n: (0, n)
    return pl.pallas_call(
        body, name="swa_attn_fwd", grid=(T // W,),
        in_specs=[pl.BlockSpec((D, W), col), pl.BlockSpec((W, nk), prev), pl.BlockSpec((W, nk), cur),
                  pl.BlockSpec((W, nk), prev), pl.BlockSpec((W, nk), cur), _full(bias.shape), _full(sink_rows.shape)],
        out_specs=pl.BlockSpec((D, W), col),
        out_shape=jax.ShapeDtypeStruct((D, T), MXU_DTYPE),
        compiler_params=_params("parallel"),
    )(qt, k, k, v, v, bias, sink_rows)


def _final_loss(x, tgt, g):
    T = x.shape[0]
    tm = _tile(T, 512)

    def body(x_ref, t_ref, g_ref, loss_ref, dx_ref, dg_ref):
        @pl.when(pl.program_id(0) == 0)
        def _():
            loss_ref[...] = jnp.zeros_like(loss_ref)
            dg_ref[...] = jnp.zeros_like(dg_ref)

        xhat, r = _rms(x_ref[...])
        gv = g_ref[...]
        e = xhat * gv - t_ref[...]
        loss_ref[...] += 0.5 * jnp.sum(jnp.mean(e * e, axis=-1, keepdims=True), axis=0, keepdims=True)
        dy = e * (1.0 / D)
        dg_ref[...] += _rowsum(dy * xhat)
        dx_ref[...] = _rms_bwd(dy * gv, xhat, r)

    return pl.pallas_call(
        body, name="final_loss", grid=(T // tm,),
        in_specs=[_rows(tm, D), _rows(tm, D), _full((1, D))],
        out_specs=[_full((8, LANES)), _rows(tm, D), _full((1, D))],
        out_shape=[jax.ShapeDtypeStruct((8, LANES), jnp.float32), jax.ShapeDtypeStruct((T, D), jnp.float32),
                   jax.ShapeDtypeStruct((1, D), jnp.float32)],
        compiler_params=_params("arbitrary"),
    )(x, tgt, g)


def _mlp_bwd(dxo, y2, a, w1, w2, xm, vec):
    T = dxo.shape[0]
    tm = _tile(T, 1024)
    tf = _tile(D_FF // N_CHIPS, 512)
    nf = D_FF // tf
    w1_spec, w2_spec = _ff_specs(tf)

    def body(dxo_ref, y_ref, a_ref, w1_ref, w2_ref, xm_ref, vec_ref, du_ref, dy_ref, dxm_ref, ps_ref, dyb, acc):
        i, f = pl.program_id(0), pl.program_id(1)

        @pl.when((i == 0) & (f == 0))
        def _():
            ps_ref[...] = jnp.zeros_like(ps_ref)

        @pl.when(f == 0)
        def _():
            dxo_t = dxo_ref[...]
            d = (dxo_t * vec_ref[R_GT2:R_GT2 + 1, :]).astype(MXU_DTYPE)
            dyb[...] = d
            dy_ref[...] = d
            acc[...] = jnp.zeros_like(acc)
            ps_ref[R_GT2:R_GT2 + 1, :] += _rowsum(dxo_t * y_ref[...].astype(jnp.float32))

        da = _dot_nt(dyb[...], w2_ref[...])
        dub = (da * (2.0 * jnp.sqrt(a_ref[...].astype(jnp.float32)))).astype(MXU_DTYPE)
        du_ref[...] = dub
        acc[...] += _dot_nt(dub, w1_ref[...])

        @pl.when(f == nf - 1)
        def _():
            dxm_ref[...] = _modulate_bwd(acc[...], xm_ref[...], vec_ref, R_GMLP, R_SC2, R_SH2, ps_ref, dxo_ref[...])

    return pl.pallas_call(
        body, name="mlp_bwd", grid=(T // tm, nf),
        in_specs=[_rows(tm, D), _rows(tm, D), pl.BlockSpec((tm, tf), lambda i, f: (i, f)), w1_spec, w2_spec,
                  _rows(tm, D), _full((8, D))],
        out_specs=[pl.BlockSpec((tm, tf), lambda i, f: (i, f)), _rows(tm, D), _rows(tm, D), _full((8, D))],
        out_shape=[jax.ShapeDtypeStruct((T, D_FF), MXU_DTYPE), jax.ShapeDtypeStruct((T, D), MXU_DTYPE),
                   jax.ShapeDtypeStruct((T, D), jnp.float32), jax.ShapeDtypeStruct((8, D), jnp.float32)],
        scratch_shapes=[pltpu.VMEM((tm, D), MXU_DTYPE), pltpu.VMEM((tm, D), jnp.float32)],
        compiler_params=_params("arbitrary", "arbitrary"),
    )(dxo, y2, a, w1, w2, xm, vec)


def _mm_tn(a, g, name, split=None, layers=1, layer=0, into=None):
    T, K = a.shape
    N = g.shape[1]
    kq = K // N_CHIPS if split == "rows" else K
    nq = N // N_CHIPS if split == "cols" else N
    bk, bn, bt = _tile(kq, 1024), _tile(nq, 1024), _tile(T, 1024)
    if nq % bn or bn % LANES:
        bn = nq
    kper, nper = kq // bk, nq // bn

    def body(*refs):
        a_ref, g_ref, o_ref = refs[0], refs[1], refs[-1]

        @pl.when(pl.program_id(2) == 0)
        def _():
            o_ref[...] = jnp.zeros_like(o_ref)

        o_ref[...] += _dot_tn(a_ref[...], g_ref[...])

    in_specs = [pl.BlockSpec((bt, bk), lambda k, n, t: (t, k)), pl.BlockSpec((bt, bn), lambda k, n, t: (t, n))]
    args = [a, g]
    aliases = {}
    if split is None:
        out_spec = pl.BlockSpec((bk, bn), lambda k, n, t: (k, n))
        out_shape = jax.ShapeDtypeStruct((K, N), jnp.float32)
    else:
        if split == "cols":
            idx = lambda k, n, t: (n // nper, layer, k, n % nper)
        else:
            idx = lambda k, n, t: (k // kper, layer, k % kper, n)
        out_spec = pl.BlockSpec((None, None, bk, bn), idx)
        out_shape = jax.ShapeDtypeStruct((N_CHIPS, layers, kq, nq), jnp.float32)
        if into is not None:
            in_specs.append(pl.BlockSpec(memory_space=pl.ANY))
            args.append(into)
            aliases = {2: 0}
    return pl.pallas_call(
        body, name=name, grid=(K // bk, N // bn, T // bt), in_specs=in_specs, out_specs=out_spec, out_shape=out_shape,
        input_output_aliases=aliases, compiler_params=_params("parallel", "parallel", "arbitrary"),
    )(*args)


def _attn_out_bwd(dxm, y1, o, w_o, vec, with_delta):
    T = dxm.shape[0]
    tm = _tile(T, 512)
    H = MLA_HEADS

    def body(dxm_ref, y_ref, o_ref, w_ref, vec_ref, dy_ref, do_ref, ps_ref, *delta_ref):
        @pl.when(pl.program_id(0) == 0)
        def _():
            ps_ref[...] = jnp.zeros_like(ps_ref)

        dxm_t = dxm_ref[...]
        dy = dxm_t * vec_ref[R_GT1:R_GT1 + 1, :]
        ps_ref[R_GT1:R_GT1 + 1, :] += _rowsum(dxm_t * y_ref[...].astype(jnp.float32))
        ps_ref[R_BO:R_BO + 1, :] += _rowsum(dy)
        dyb = dy.astype(MXU_DTYPE)
        dy_ref[...] = dyb
        do = _dot_nt(dyb, w_ref[...])
        do_ref[...] = do.astype(do_ref.dtype)
        if with_delta:
            of = o_ref[...].astype(jnp.float32)
            ones = jnp.ones((8, V_DIM), jnp.float32)
            for hd in range(H):
                sl = slice(hd * V_DIM, (hd + 1) * V_DIM)
                d = lax.dot_general(ones, do[:, sl] * of[:, sl], (((1,), (1,)), ((), ())),
                                    precision=lax.Precision.HIGHEST, preferred_element_type=jnp.float32)
                delta_ref[0][hd] = d[0:1, :]

    out_specs = [_rows(tm, D), _rows(tm, D), _full((8, D))]
    out_shape = [jax.ShapeDtypeStruct((T, D), MXU_DTYPE), jax.ShapeDtypeStruct((T, D), MXU_DTYPE),
                 jax.ShapeDtypeStruct((8, D), jnp.float32)]
    if with_delta:
        out_specs.append(pl.BlockSpec((H, 1, tm), lambda i: (0, 0, i)))
        out_shape.append(jax.ShapeDtypeStruct((H, 1, T), jnp.float32))
    return pl.pallas_call(
        body, name="attn_out_bwd_mla" if with_delta else "attn_out_bwd_swa", grid=(T // tm,),
        in_specs=[_rows(tm, D), _rows(tm, D), _rows(tm, D), _full((D, D)), _full((8, D))],
        out_specs=out_specs, out_shape=out_shape,
        compiler_params=_params("arbitrary"),
    )(dxm, y1, o, w_o, vec)


def _mla_attn_bwd(q, k, v, do, lse, delta):
    H, T, _ = q.shape
    tb = _tile(T, ATTN_BLOCK)
    sub = min(ATTN_SUB, tb)
    ns, nb = tb // sub, T // tb

    def body(q_ref, k_ref, v_ref, do_ref, lse_ref, dl_ref, dq_ref, dk_ref, dv_ref, dk_acc, dv_acc):
        j, i = pl.program_id(1), pl.program_id(2)

        @pl.when((j == 0) & (i == 0))
        def _():
            dq_ref[...] = jnp.zeros_like(dq_ref)

        def update(kk, r, masked):
            keys, rows = pl.ds(kk * sub, sub), pl.ds(r * sub, sub)
            kb, qb, dob = k_ref[0, keys, :], q_ref[0, rows, :], do_ref[rows, :]
            st = _dot_nt(kb, qb)
            if masked:
                row = lax.broadcasted_iota(jnp.int32, (sub, sub), 0)
                col = lax.broadcasted_iota(jnp.int32, (sub, sub), 1)
                st = jnp.where(row <= col, st, NEG)
            pt = jnp.exp2(st - lse_ref[0, :, rows])
            dv_acc[keys, :] += _dot(pt.astype(MXU_DTYPE), dob)
            dpt = _dot_nt(v_ref[0, keys, :], dob)
            dst = (pt * (dpt - dl_ref[0, :, rows])).astype(MXU_DTYPE)
            dk_acc[keys, :] += _dot(dst, qb)
            q_rows = pl.ds(pl.multiple_of(i * tb + r * sub, sub), sub)
            dq_ref[0, q_rows, :] += _dot_tn(dst, kb)

        @pl.when(i == j)
        def _():
            dk_acc[...] = jnp.zeros_like(dk_acc)
            dv_acc[...] = jnp.zeros_like(dv_acc)
            for r in range(ns):
                for kk in range(r + 1):
                    update(kk, r, kk == r)

        @pl.when(i > j)
        def _():
            for r in range(ns):
                for kk in range(ns):
                    update(kk, r, False)

        @pl.when(i == nb - 1)
        def _():
            dk_ref[0] = (dk_acc[...] * LN2).astype(dk_ref.dtype)
            dv_ref[0] = dv_acc[...].astype(dv_ref.dtype)

    q_idx = lambda h, j, i: (h, jnp.maximum(i, j), 0)
    kv_idx = lambda h, j, i: (h, j, 0)
    stat_idx = lambda h, j, i: (h, 0, jnp.maximum(i, j))
    return pl.pallas_call(
        body, name="mla_attn_bwd", grid=(H, nb, nb),
        in_specs=[pl.BlockSpec((1, tb, QK_EXT), q_idx), pl.BlockSpec((1, tb, QK_EXT), kv_idx),
                  pl.BlockSpec((1, tb, V_DIM), kv_idx),
                  pl.BlockSpec((tb, V_DIM), lambda h, j, i: (jnp.maximum(i, j), h)),
                  pl.BlockSpec((1, 1, tb), stat_idx), pl.BlockSpec((1, 1, tb), stat_idx)],
        out_specs=[pl.BlockSpec((1, T, QK_EXT), lambda h, j, i: (h, 0, 0)),
                   pl.BlockSpec((1, tb, QK_EXT), kv_idx), pl.BlockSpec((1, tb, V_DIM), kv_idx)],
        out_shape=[jax.ShapeDtypeStruct((H, T, QK_EXT), jnp.float32), jax.ShapeDtypeStruct((H, T, QK_EXT), MXU_DTYPE),
                   jax.ShapeDtypeStruct((H, T, V_DIM), MXU_DTYPE)],
        scratch_shapes=[pltpu.VMEM((tb, QK_EXT), jnp.float32), pltpu.VMEM((tb, V_DIM), jnp.float32)],
        compiler_params=_params("parallel", "arbitrary", "arbitrary"),
    )(q, k, v, do, lse, delta)


def _mla_pre_bwd(x, dxm, vec, hb, z, dq, dk, dv, cs, wcat, g_q, g_kv, wuq, wukv):
    T = x.shape[0]
    tm = _tile(T, 256)
    H = MLA_HEADS
    zc = wcat.shape[1]

    def body(x_ref, dxm_ref, vec_ref, h_ref, z_ref, dq_ref, dk_ref, dv_ref, cs_ref, wcat_ref, gq_ref, gkv_ref,
             wuq_ref, wukv_ref, dx_ref, ps_ref, dgq_ref, dgkv_ref, dwcat_ref, dwuq_ref, dwukv_ref):
        @pl.when(pl.program_id(0) == 0)
        def _():
            for ref in (ps_ref, dgq_ref, dgkv_ref, dwcat_ref, dwuq_ref, dwukv_ref):
                ref[...] = jnp.zeros_like(ref)

        z = z_ref[...]
        cs_t = cs_ref[...]
        cqhat, rq = _rms(z[:, :Q_LORA])
        ckhat, rk = _rms(z[:, Q_LORA:Q_LORA + KV_LORA])
        gq, gkv = gq_ref[...], gkv_ref[...]
        cq = (cqhat * gq).astype(MXU_DTYPE)
        ckv = (ckhat * gkv).astype(MXU_DTYPE)
        dcq = jnp.zeros((tm, Q_LORA), jnp.float32)
        dckv = jnp.zeros((tm, KV_LORA), jnp.float32)
        dkr = jnp.zeros((tm, LANES), jnp.float32)
        for hd in range(H):
            dqh = dq_ref[hd] * MLA_SCALE
            gqh = jnp.concatenate([dqh[:, :QK_NOPE], dqh[:, QK_NOPE:] * cs_t], axis=1).astype(MXU_DTYPE)
            dcq += _dot_nt(gqh, wuq_ref[hd])
            dwuq_ref[hd] += _dot_tn(cq, gqh)
            dkh = dk_ref[hd]
            gkvh = jnp.concatenate([dkh[:, :QK_NOPE], dv_ref[hd]], axis=1)
            dckv += _dot_nt(gkvh, wukv_ref[hd])
            dwukv_ref[hd] += _dot_tn(ckv, gkvh)
            dkr += dkh[:, QK_NOPE:].astype(jnp.float32)
        dgq_ref[...] += _rowsum(dcq * cqhat)
        dgkv_ref[...] += _rowsum(dckv * ckhat)
        dcq_pre = _rms_bwd(dcq * gq, cqhat, rq)
        dckv_pre = _rms_bwd(dckv * gkv, ckhat, rk)
        dkr2 = (dkr + pltpu.roll(dkr, QK_ROPE, axis=1)) * cs_t
        dz = jnp.concatenate([dcq_pre, dckv_pre, dkr2], axis=1).astype(MXU_DTYPE)
        dwcat_ref[...] += _dot_tn(h_ref[...], dz)
        dh = _dot_nt(dz, wcat_ref[...])
        dx_ref[...] = _modulate_bwd(dh, x_ref[...], vec_ref, R_GMIX, R_SC1, R_SH1, ps_ref, dxm_ref[...])

    hblk = lambda w: pl.BlockSpec((H, tm, w), lambda i: (0, i, 0))
    return pl.pallas_call(
        body, name="mla_pre_bwd", grid=(T // tm,),
        in_specs=[_rows(tm, D), _rows(tm, D), _full((8, D)), _rows(tm, D), _rows(tm, zc), hblk(QK_EXT), hblk(QK_EXT),
                  hblk(V_DIM), _rows(tm, LANES), _full(wcat.shape), _full(g_q.shape), _full(g_kv.shape),
                  _full(wuq.shape), _full(wukv.shape)],
        out_specs=[_rows(tm, D), _full((8, D)), _full(g_q.shape), _full(g_kv.shape), _full(wcat.shape),
                   _full(wuq.shape), _full(wukv.shape)],
        out_shape=[jax.ShapeDtypeStruct((T, D), jnp.float32), jax.ShapeDtypeStruct((8, D), jnp.float32),
                   jax.ShapeDtypeStruct(g_q.shape, jnp.float32), jax.ShapeDtypeStruct(g_kv.shape, jnp.float32),
                   jax.ShapeDtypeStruct(wcat.shape, jnp.float32), jax.ShapeDtypeStruct(wuq.shape, jnp.float32),
                   jax.ShapeDtypeStruct(wukv.shape, jnp.float32)],
        compiler_params=_params("arbitrary"),
    )(x, dxm, vec, hb, z, dq, dk, dv, cs, wcat, g_q, g_kv, wuq, wukv)


def _swa_attn_bwd(qt, k, v, dot_, bias, sink_rows):
    T = qt.shape[1]
    W, Dh, G, Hk = WINDOW, SWA_HEAD_DIM, SWA_GROUP, SWA_KV_HEADS
    nk = Hk * Dh

    def body(qt_ref, kp_ref, kc_ref, vp_ref, vc_ref, dot_ref, bias_ref, sink_ref, dqt_ref, dk_ref, dv_ref, dsink_ref):
        n = pl.program_id(0)

        @pl.when(n == 0)
        def _():
            dk_ref[...] = jnp.zeros_like(dk_ref)
            dv_ref[...] = jnp.zeros_like(dv_ref)
            dsink_ref[...] = jnp.zeros_like(dsink_ref)

        dks, dvs = [], []
        for kh in range(Hk):
            qt, kb, pn, p_sink = _swa_probs(n, kh, qt_ref, kp_ref, kc_ref, bias_ref, sink_ref)
            vb = jnp.concatenate([vp_ref[:, kh * Dh:(kh + 1) * Dh], vc_ref[:, kh * Dh:(kh + 1) * Dh]], axis=0)
            dot_h = jnp.concatenate([dot_ref[(kh * G + g) * Dh:(kh * G + g + 1) * Dh, :] for g in range(G)], axis=1)
            dp = _dot(vb, dot_h)
            delta = jnp.sum(pn * dp, axis=0, keepdims=True)
            dsb = (pn * (dp - delta)).astype(MXU_DTYPE)
            dsink_ref[kh] += -p_sink * delta
            dqt = _dot_tn(kb, dsb) * SWA_SCALE
            for g in range(G):
                dqt_ref[(kh * G + g) * Dh:(kh * G + g + 1) * Dh, :] = dqt[:, g * W:(g + 1) * W]
            dks.append(_dot_nt(dsb, qt))
            dvs.append(_dot_nt(pn.astype(MXU_DTYPE), dot_h))
        dkb = jnp.concatenate(dks, axis=1)
        dvb = jnp.concatenate(dvs, axis=1)
        cur_rows = pl.ds(pl.multiple_of(n * W, W), W)
        dk_ref[cur_rows, :] += dkb[W:]
        dv_ref[cur_rows, :] += dvb[W:]

        @pl.when(n > 0)
        def _():
            prev_rows = pl.ds(pl.multiple_of((n - 1) * W, W), W)
            dk_ref[prev_rows, :] += dkb[:W]
            dv_ref[prev_rows, :] += dvb[:W]

    prev = lambda n: (jnp.maximum(n - 1, 0), 0)
    cur = lambda n: (n, 0)
    col = lambda n: (0, n)
    return pl.pallas_call(
        body, name="swa_attn_bwd", grid=(T // W,),
        in_specs=[pl.BlockSpec((D, W), col), pl.BlockSpec((W, nk), prev), pl.BlockSpec((W, nk), cur),
                  pl.BlockSpec((W, nk), prev), pl.BlockSpec((W, nk), cur), pl.BlockSpec((D, W), col),
                  _full(bias.shape), _full(sink_rows.shape)],
        out_specs=[pl.BlockSpec((D, W), col), _full((T, nk)), _full((T, nk)), _full(sink_rows.shape)],
        out_shape=[jax.ShapeDtypeStruct((D, T), jnp.float32), jax.ShapeDtypeStruct((T, nk), jnp.float32),
                   jax.ShapeDtypeStruct((T, nk), jnp.float32), jax.ShapeDtypeStruct(sink_rows.shape, jnp.float32)],
        compiler_params=_params("arbitrary"),
    )(qt, k, k, v, v, dot_, bias, sink_rows)


def _swa_pre_bwd(x, dxm, vec, dq, dk, dv, w_qkv):
    T = x.shape[0]
    tm = _tile(T, 512)
    nq = SWA_HEADS * SWA_HEAD_DIM
    nk = SWA_KV_HEADS * SWA_HEAD_DIM
    nqkv = nq + 2 * nk

    def body(x_ref, dxm_ref, vec_ref, dq_ref, dk_ref, dv_ref, w_ref, dx_ref, dqkv_ref, ps_ref, db_ref):
        @pl.when(pl.program_id(0) == 0)
        def _():
            ps_ref[...] = jnp.zeros_like(ps_ref)
            db_ref[...] = jnp.zeros_like(db_ref)

        dqkv = jnp.concatenate([dq_ref[...], dk_ref[...], dv_ref[...]], axis=1)
        db_ref[...] += _rowsum(dqkv)
        dqkv_b = dqkv.astype(MXU_DTYPE)
        dqkv_ref[...] = dqkv_b
        dh = _dot_nt(dqkv_b, w_ref[...])
        dx_ref[...] = _modulate_bwd(dh, x_ref[...], vec_ref, R_GMIX, R_SC1, R_SH1, ps_ref, dxm_ref[...])

    return pl.pallas_call(
        body, name="swa_pre_bwd", grid=(T // tm,),
        in_specs=[_rows(tm, D), _rows(tm, D), _full((8, D)), _rows(tm, nq), _rows(tm, nk), _rows(tm, nk),
                  _full(w_qkv.shape)],
        out_specs=[_rows(tm, D), _rows(tm, nqkv), _full((8, D)), _full((1, nqkv))],
        out_shape=[jax.ShapeDtypeStruct((T, D), jnp.float32), jax.ShapeDtypeStruct((T, nqkv), MXU_DTYPE),
                   jax.ShapeDtypeStruct((8, D), jnp.float32), jax.ShapeDtypeStruct((1, nqkv), jnp.float32)],
        compiler_params=_params("arbitrary"),
    )(x, dxm, vec, dq, dk, dv, w_qkv)


def _rot_cols(w):
    half = QK_ROPE // 2
    return jnp.concatenate([-w[..., half:], w[..., :half]], axis=-1)


def _unrot_grad(d_rope, d_rot):
    half = QK_ROPE // 2
    return d_rope + jnp.concatenate([d_rot[..., half:], -d_rot[..., :half]], axis=-1)


def _rope_table(positions):
    half = QK_ROPE // 2
    inv_freq = ROPE_THETA ** (-jnp.arange(half, dtype=jnp.float32) / half)
    ang = positions.astype(jnp.float32)[:, None] * inv_freq
    cos, sin = jnp.cos(ang), jnp.sin(ang)
    return jnp.concatenate([cos, cos, sin, sin], axis=1)


def _sequence_step(x, tgt, positions, vecs, g_q, g_kv, sinks, g_final, wts, late_weights):
    H = MLA_HEADS
    cs = _rope_table(positions)
    w_dkv = wts["mla_w_dkv"]
    wcat = jnp.concatenate([wts["mla_w_dq"], w_dkv, _rot_cols(w_dkv[:, KV_LORA:])], axis=1)
    uq = wts["mla_w_uq"].reshape(Q_LORA, H, QK_NOPE + QK_ROPE)
    wuq = jnp.concatenate([uq, _rot_cols(uq[..., QK_NOPE:])], axis=-1).transpose(1, 0, 2)
    wukv = wts["mla_w_ukv"].reshape(KV_LORA, H, QK_NOPE + V_DIM).transpose(1, 0, 2)
    zero_bias = jnp.zeros((1, D), jnp.float32)
    bias = _swa_bias()
    sink_rows = jnp.broadcast_to(sinks.reshape(SWA_KV_HEADS, 1, SWA_GROUP, 1),
                                 (SWA_KV_HEADS, 1, SWA_GROUP, WINDOW)).reshape(SWA_KV_HEADS, 1, SWA_GROUP * WINDOW)

    h1a, z, q, k, v = _mla_pre(x, vecs[0], wcat, g_q, g_kv, wuq, wukv, cs)
    o_a, lse = _mla_attn_fwd(q, k, v)
    y1a, xm_a, h2a = _post_attn(o_a, x, wts["mla_w_o"], zero_bias, vecs[0])
    wts = {**wts, **late_weights(h2a)}
    a_a, y2a, x1 = _mlp_fwd(h2a, wts["w_ff1"][0], wts["w_ff2"][0], xm_a, vecs[0])

    h1b, qs, ks, vs = _swa_pre(x1, vecs[1], wts["swa_w_qkv"], wts["swa_b_qkv"])
    qs_t = qs.T
    o_b = _swa_attn_fwd(qs_t, ks, vs, bias, sink_rows).T
    y1b, xm_b, h2b = _post_attn(o_b, x1, wts["swa_w_o"], wts["swa_b_o"], vecs[1])
    a_b, y2b, x2 = _mlp_fwd(h2b, wts["w_ff1"][1], wts["w_ff2"][1], xm_b, vecs[1])

    loss8, dx2, dg_final = _final_loss(x2, tgt, g_final.reshape(1, D))

    du_b, dy2b, dxm_b, ps_mlp_b = _mlp_bwd(dx2, y2b, a_b, wts["w_ff1"][1], wts["w_ff2"][1], xm_b, vecs[1])
    g_ff2 = _mm_tn(a_b, dy2b, "dw_ff2_l1", "rows", DEPTH, 1)
    g_ff1 = _mm_tn(h2b, du_b, "dw_ff1_l1", "cols", DEPTH, 1)
    dy1b, do_b, ps_out_b = _attn_out_bwd(dxm_b, y1b, o_b, wts["swa_w_o"], vecs[1], False)
    g_swa_o = _mm_tn(o_b, dy1b, "dw_o_swa")
    dqs_t, dks, dvs, dsinks = _swa_attn_bwd(qs_t, ks, vs, do_b.T, bias, sink_rows)
    dqs = dqs_t.T
    dx1, dqkv, ps_pre_b, g_swa_bqkv = _swa_pre_bwd(x1, dxm_b, vecs[1], dqs, dks, dvs, wts["swa_w_qkv"])
    g_swa_qkv = _mm_tn(h1b, dqkv, "dw_qkv", "cols")

    du_a, dy2a, dxm_a, ps_mlp_a = _mlp_bwd(dx1, y2a, a_a, wts["w_ff1"][0], wts["w_ff2"][0], xm_a, vecs[0])
    g_ff2 = _mm_tn(a_a, dy2a, "dw_ff2_l0", "rows", DEPTH, 0, g_ff2)
    g_ff1 = _mm_tn(h2a, du_a, "dw_ff1_l0", "cols", DEPTH, 0, g_ff1)
    dy1a, do_a, ps_out_a, delta = _attn_out_bwd(dxm_a, y1a, o_a, wts["mla_w_o"], vecs[0], True)
    g_mla_o = _mm_tn(o_a, dy1a, "dw_o_mla")
    dq, dk, dv = _mla_attn_bwd(q, k, v, do_a, lse, delta)
    dx0, ps_pre_a, dg_q, dg_kv, dwcat, dwuq, dwukv = _mla_pre_bwd(
        x, dxm_a, vecs[0], h1a, z, dq, dk, dv, cs, wcat, g_q, g_kv, wuq, wukv)

    c0, c1, c2 = Q_LORA, Q_LORA + KV_LORA, Q_LORA + KV_LORA + QK_ROPE
    g_dq = dwcat[:, :c0]
    g_dkv = jnp.concatenate([dwcat[:, c0:c1], _unrot_grad(dwcat[:, c1:c2], dwcat[:, c2:])], axis=1)
    e0 = QK_NOPE + QK_ROPE
    g_uq = jnp.concatenate([dwuq[..., :QK_NOPE], _unrot_grad(dwuq[..., QK_NOPE:e0], dwuq[..., e0:])], axis=-1)
    per = H // N_CHIPS
    g_uq = g_uq.reshape(N_CHIPS, per, Q_LORA, e0).transpose(0, 2, 1, 3).reshape(N_CHIPS, Q_LORA, per * e0)
    g_ukv = dwukv.reshape(N_CHIPS, per, KV_LORA, QK_NOPE + V_DIM).transpose(0, 2, 1, 3)
    g_ukv = g_ukv.reshape(N_CHIPS, KV_LORA, per * (QK_NOPE + V_DIM))

    def dmod(ps_pre, ps_out, ps_mlp):
        return jnp.concatenate([ps_pre[R_SH1:R_SC1 + 1], ps_out[R_GT1:R_GT1 + 1], ps_mlp[R_SH2:R_GT2 + 1]], axis=0)

    rows4 = lambda g: g.reshape(N_CHIPS, g.shape[0] // N_CHIPS, g.shape[1])
    grads = {
        "mla_w_dq": rows4(g_dq), "mla_w_uq": g_uq, "mla_w_dkv": rows4(g_dkv), "mla_w_ukv": g_ukv,
        "mla_w_o": rows4(g_mla_o), "swa_w_qkv": g_swa_qkv.reshape(N_CHIPS, D, -1), "swa_w_o": rows4(g_swa_o),
        "w_ff1": g_ff1.reshape(N_CHIPS, DEPTH * D, -1), "w_ff2": g_ff2.reshape(N_CHIPS, -1, D),
    }
    small = {
        "dmod": jnp.stack([dmod(ps_pre_a, ps_out_a, ps_mlp_a), dmod(ps_pre_b, ps_out_b, ps_mlp_b)]).reshape(DEPTH, 6 * D),
        "g_mix": jnp.stack([ps_pre_a[R_GMIX], ps_pre_b[R_GMIX]]),
        "g_mlp": jnp.stack([ps_mlp_a[R_GMLP], ps_mlp_b[R_GMLP]]),
        "mla_g_q": dg_q, "mla_g_kv": dg_kv, "swa_sinks": jnp.sum(dsinks.reshape(SWA_HEADS, WINDOW), axis=1).reshape(1, SWA_HEADS),
        "swa_b_qkv": g_swa_bqkv, "swa_b_o": ps_out_b[R_BO:R_BO + 1],
        "g_final": dg_final.reshape(D), "loss": loss8[0, 0],
    }
    return dx0, grads, small


SHARDED = {
    "mla_w_dq": (1, D // N_CHIPS, Q_LORA),
    "mla_w_uq": (1, Q_LORA, MLA_HEADS * (QK_NOPE + QK_ROPE) // N_CHIPS),
    "mla_w_dkv": (1, D // N_CHIPS, KV_LORA + QK_ROPE),
    "mla_w_ukv": (1, KV_LORA, MLA_HEADS * (QK_NOPE + V_DIM) // N_CHIPS),
    "mla_w_o": (1, MLA_HEADS * V_DIM // N_CHIPS, D),
    "swa_w_qkv": (1, D, (SWA_HEADS + 2 * SWA_KV_HEADS) * SWA_HEAD_DIM // N_CHIPS),
    "swa_w_o": (1, SWA_HEADS * SWA_HEAD_DIM // N_CHIPS, D),
    "w_ff1": (DEPTH, D, D_FF // N_CHIPS),
    "w_ff2": (DEPTH, D_FF // N_CHIPS, D),
}
COL_SPLIT = ("mla_w_uq", "mla_w_ukv", "swa_w_qkv")
BIASES = {"swa_b_qkv": (SWA_HEADS + 2 * SWA_KV_HEADS) * SWA_HEAD_DIM, "swa_b_o": D}


def _view2d(name):
    shape = SHARDED[name]
    return math.prod(shape[:-1]), shape[-1]


SMALL = {"b_ada": (DEPTH, 6 * D), "g_mix": (DEPTH, D), "g_mlp": (DEPTH, D), "mla_g_q": (1, Q_LORA),
         "mla_g_kv": (1, KV_LORA), "swa_sinks": (1, SWA_HEADS), "g_final": (D,), "loss": (),
         "swa_b_qkv": (1, BIASES["swa_b_qkv"]), "swa_b_o": (1, BIASES["swa_b_o"])}
SMALL_ROWS = 168
DMA_ROWS = 256


def _small_slots():
    slots, off = {}, 0
    for name, shape in SMALL.items():
        n = max(math.prod(shape), 1)
        slots[name] = (off, n)
        off += -(-n // LANES) * LANES
    assert off <= SMALL_ROWS * LANES
    return slots


def _pack_small(vals):
    parts, end = [], 0
    for name, (off, n) in _small_slots().items():
        pad = -(-n // LANES) * LANES - n
        v = vals[name].astype(jnp.float32).reshape(-1) if name in vals else jnp.zeros((n,), jnp.float32)
        parts += [v, jnp.zeros((pad,), jnp.float32)]
        end = off + n + pad
    parts.append(jnp.zeros((SMALL_ROWS * LANES - end,), jnp.float32))
    return jnp.concatenate(parts).reshape(SMALL_ROWS, LANES)


def _unpack_small(buf):
    flat = buf.reshape(-1)
    return {name: flat[off:off + n].reshape(SMALL[name]) for name, (off, n) in _small_slots().items()}


def _pieces(rows):
    return [(off, min(DMA_ROWS, rows - off)) for off in range(0, rows, DMA_ROWS)]


HBM = pl.BlockSpec(memory_space=pltpu.HBM)
MESH = pl.DeviceIdType.MESH


def _place():
    x, y, c = lax.axis_index("x"), lax.axis_index("y"), lax.axis_index("c")
    chips = [(1 - x, y), (x, 1 - y), (1 - x, 1 - y)]
    return x, y, c, chips


def _all_gather(block):
    m_per, n = block.shape

    def body(x_ref, out_ref, send_sems, recv_sems, local_sem):
        x, y, c, chips = _place()
        me, sibling = (x, y, c), (x, y, 1 - c)

        def rows(px, py, pc):
            return out_ref.at[pl.ds((4 * px + 2 * py + pc) * m_per, m_per), :]

        def copy(k, blk, to, src=None):
            return pltpu.make_async_remote_copy(
                src_ref=rows(*blk) if src is None else src, dst_ref=rows(*blk),
                send_sem=send_sems.at[k], recv_sem=recv_sems.at[k], device_id=to, device_id_type=MESH)

        mine = pltpu.make_async_copy(x_ref, rows(*me), local_sem)
        mine.start()
        first = [copy(0, me, sibling, src=x_ref)]
        first += [copy(1 + j, me, (*chip, c), src=x_ref) for j, chip in enumerate(chips)]
        for cp in first:
            cp.start()
        passed = [copy(4 + j, (*chip, c), sibling) for j, chip in enumerate(chips)]
        for j, chip in enumerate(chips):
            copy(1 + j, (*chip, c), me).wait_recv()
            passed[j].start()
        copy(0, sibling, me).wait_recv()
        for j, chip in enumerate(chips):
            copy(4 + j, (*chip, 1 - c), me).wait_recv()
        for cp in first + passed:
            cp.wait_send()
        mine.wait()

    out = pl.pallas_call(
        body, name="all_gather_small",
        out_shape=jax.ShapeDtypeStruct((N_DEV * m_per, n), block.dtype),
        in_specs=[pl.BlockSpec(memory_space=pltpu.VMEM)],
        out_specs=pl.BlockSpec(memory_space=pltpu.VMEM),
        scratch_shapes=[pltpu.SemaphoreType.DMA((7,)), pltpu.SemaphoreType.DMA((7,)), pltpu.SemaphoreType.DMA],
    )(block)
    return out.reshape(N_DEV, m_per, n)


def _weight_gather(shards):
    nt = len(shards)

    def body(*refs):
        w_refs, out_refs = refs[:nt], refs[nt:2 * nt]
        send_sems, recv_sems = refs[2 * nt:]
        x, y, c, chips = _place()
        sibling = (x, y, 1 - c)

        def slab(t, px, py, half):
            rh = shards[t].shape[0] // 2
            return out_refs[t].at[2 * px + py, pl.ds(half * rh, rh), :]

        def copy(t, k, src, dst, to):
            return pltpu.make_async_remote_copy(src_ref=src, dst_ref=dst, send_sem=send_sems.at[6 * t + k],
                                                recv_sem=recv_sems.at[6 * t + k], device_id=to, device_id_type=MESH)

        first = []
        for t in range(nt):
            rh = shards[t].shape[0] // 2
            first += [copy(t, j, w_refs[t].at[pl.ds(c * rh, rh), :], slab(t, x, y, c), (*chip, c))
                      for j, chip in enumerate(chips)]
        for cp in first:
            cp.start()
        passed = []
        for t in range(nt):
            for j, chip in enumerate(chips):
                copy(t, j, slab(t, *chip, c), slab(t, *chip, c), (*chip, c)).wait_recv()
                rh = shards[t].shape[0] // 2
                for off, n in _pieces(rh):
                    piece = out_refs[t].at[2 * chip[0] + chip[1], pl.ds(c * rh + off, n), :]
                    copy(t, 3 + j, piece, piece, sibling).start()
                passed.append(copy(t, 3 + j, slab(t, *chip, c), slab(t, *chip, c), sibling))
        for t in range(nt):
            for j, chip in enumerate(chips):
                copy(t, 3 + j, slab(t, *chip, 1 - c), slab(t, *chip, 1 - c), sibling).wait_recv()
        for cp in first + passed:
            cp.wait_send()

    return pl.pallas_call(
        body, name="weight_gather",
        out_shape=[jax.ShapeDtypeStruct((N_CHIPS,) + s.shape, s.dtype) for s in shards],
        in_specs=[HBM] * nt, out_specs=[HBM] * nt,
        scratch_shapes=[pltpu.SemaphoreType.DMA((6 * nt,)), pltpu.SemaphoreType.DMA((6 * nt,))],
    )(*shards)


SEM = pl.BlockSpec(memory_space=pltpu.SEMAPHORE)
ANY = pl.BlockSpec(memory_space=pl.ANY)
SPLIT_COPY = pltpu.SideEffectType.DATAFLOW_SIDE_EFFECTING


def _late_copies(w_refs, land_refs, send_sems, recv_sems):
    x, y, c, chips = _place()
    return [pltpu.make_async_remote_copy(
        src_ref=w_refs[t], dst_ref=land_refs[t].at[2 * x + y], send_sem=send_sems.at[3 * t + j],
        recv_sem=recv_sems.at[3 * t + j], device_id=(cx, cy, c), device_id_type=MESH)
        for t in range(len(w_refs)) for j, (cx, cy) in enumerate(chips)], chips


def _late_gather_start(shards, after):
    nt, na = len(shards), len(after)

    def body(*refs):
        w_refs, land_refs = refs[:nt], refs[nt:2 * nt]
        send_sems, recv_sems, token = refs[2 * nt + na], refs[2 * nt + na + 1], refs[-1]
        copies, _ = _late_copies(w_refs, land_refs, send_sems, recv_sems)
        for cp in copies:
            cp.start()
        token[...] = jnp.zeros_like(token)

    hbm = lambda a: pltpu.with_memory_space_constraint(a, pltpu.HBM)
    lands = [lax.empty((N_CHIPS,) + s.shape, s.dtype) for s in shards]
    outs = pl.pallas_call(
        body, name="late_gather_start",
        out_shape=(pltpu.SemaphoreType.DMA((3 * nt,)), pltpu.SemaphoreType.DMA((3 * nt,)),
                   *[pltpu.HBM(s.shape, s.dtype) for s in shards], *[pltpu.HBM(l.shape, l.dtype) for l in lands],
                   jax.ShapeDtypeStruct((8, LANES), jnp.float32)),
        in_specs=[HBM] * (2 * nt) + [ANY] * na,
        out_specs=(SEM, SEM, *([HBM] * (2 * nt)), pl.BlockSpec(memory_space=pltpu.VMEM)),
        input_output_aliases={i: 2 + i for i in range(2 * nt)},
        compiler_params=pltpu.CompilerParams(has_side_effects=SPLIT_COPY),
    )(*[hbm(s) for s in shards], *[hbm(l) for l in lands], *after)
    return outs[0], outs[1], list(outs[2:2 + nt]), list(outs[2 + nt:2 + 2 * nt]), outs[-1]


def _late_gather_wait(send_sems, recv_sems, shards, lands, after):
    nt = len(shards)

    def body(*refs):
        w_refs, land_refs = refs[:nt], refs[nt:2 * nt]
        s_sems, r_sems = refs[2 * nt], refs[2 * nt + 1]
        x, y, c, chips = _place()
        for t in range(nt):
            for j, (cx, cy) in enumerate(chips):
                cp = pltpu.make_async_remote_copy(
                    src_ref=w_refs[t], dst_ref=land_refs[t].at[2 * cx + cy], send_sem=s_sems.at[3 * t + j],
                    recv_sem=r_sems.at[3 * t + j], device_id=(cx, cy, c), device_id_type=MESH)
                cp.wait_send()
                cp.wait_recv()

    outs = pl.pallas_call(
        body, name="late_gather_wait",
        out_shape=(*[pltpu.HBM(s.shape, s.dtype) for s in shards], *[pltpu.HBM(l.shape, l.dtype) for l in lands]),
        in_specs=[HBM] * (2 * nt) + [SEM, SEM, ANY], out_specs=tuple([HBM] * (2 * nt)),
        input_output_aliases={i: i for i in range(2 * nt)},
        compiler_params=pltpu.CompilerParams(has_side_effects=SPLIT_COPY),
    )(*shards, *lands, send_sems, recv_sems, after)
    return list(outs[nt:])


def _grad_pair_in(grads):
    nt = len(grads)

    def body(*refs):
        g_refs, got_refs = refs[:nt], refs[nt:2 * nt]
        send_sems, recv_sems = refs[2 * nt:]
        x, y, c, _ = _place()
        sibling = (x, y, 1 - c)

        def copy(t, src, dst):
            return pltpu.make_async_remote_copy(src_ref=src, dst_ref=dst, send_sem=send_sems.at[t],
                                                recv_sem=recv_sems.at[t], device_id=sibling, device_id_type=MESH)

        for t in range(nt):
            rh = grads[t].shape[1] // 2
            for p in range(N_CHIPS):
                for off, n in _pieces(rh):
                    copy(t, g_refs[t].at[p, pl.ds((1 - c) * rh + off, n), :], got_refs[t].at[p, pl.ds(off, n), :]).start()
        for t in range(nt):
            rh = grads[t].shape[1] // 2
            copy(t, g_refs[t].at[:, pl.ds((1 - c) * rh, rh), :], got_refs[t]).wait()

    return pl.pallas_call(
        body, name="grad_pair_in",
        out_shape=[jax.ShapeDtypeStruct((N_CHIPS, g.shape[1] // 2, g.shape[2]), g.dtype) for g in grads],
        in_specs=[HBM] * nt, out_specs=[HBM] * nt,
        scratch_shapes=[pltpu.SemaphoreType.DMA((nt,)), pltpu.SemaphoreType.DMA((nt,))],
    )(*grads)


def _pair_sum(g, got, core, name):
    _, rows, cols = g.shape
    rh = rows // 2
    tr = _tile(rh, 512)
    nb = rh // tr

    def body(c_ref, g_ref, got_ref, s32_ref, s16_ref):
        s = g_ref[...] + got_ref[...]
        s32_ref[...] = s
        s16_ref[...] = s.astype(s16_ref.dtype)

    blk = pl.BlockSpec((None, tr, cols), lambda p, i, c_ref: (p, i, 0))
    return pl.pallas_call(
        body, name=name,
        grid_spec=pltpu.PrefetchScalarGridSpec(
            num_scalar_prefetch=1, grid=(N_CHIPS, nb),
            in_specs=[pl.BlockSpec((None, tr, cols), lambda p, i, c_ref: (p, c_ref[0] * nb + i, 0)), blk],
            out_specs=[blk, blk]),
        out_shape=[jax.ShapeDtypeStruct((N_CHIPS, rh, cols), jnp.float32),
                   jax.ShapeDtypeStruct((N_CHIPS, rh, cols), jnp.bfloat16)],
        compiler_params=_params("parallel", "parallel"),
    )(core, g, got)


def _grad_chip_exchange(parts):
    nt = len(parts)

    def body(*refs):
        a_refs, got_refs = refs[:nt], refs[nt:2 * nt]
        send_sems, recv_sems = refs[2 * nt:]
        x, y, c, chips = _place()
        sends = [pltpu.make_async_remote_copy(
            src_ref=a_refs[t].at[2 * cx + cy], dst_ref=got_refs[t].at[j], send_sem=send_sems.at[3 * t + j],
            recv_sem=recv_sems.at[3 * t + j], device_id=(cx, cy, c), device_id_type=MESH)
            for t in range(nt) for j, (cx, cy) in enumerate(chips)]
        for cp in sends:
            cp.start()
        for cp in sends:
            cp.wait_recv()
        for cp in sends:
            cp.wait_send()

    return pl.pallas_call(
        body, name="grad_chip_exchange",
        out_shape=[jax.ShapeDtypeStruct((N_CHIPS - 1,) + a.shape[1:], a.dtype) for a in parts],
        in_specs=[HBM] * nt, out_specs=[HBM] * nt,
        scratch_shapes=[pltpu.SemaphoreType.DMA((3 * nt,)), pltpu.SemaphoreType.DMA((3 * nt,))],
    )(*parts)


def _chip_sum(s32, got, chip, name):
    _, rh, cols = s32.shape
    tr = _tile(rh, 512)

    def body(p_ref, s_ref, got_ref, o_ref):
        acc = s_ref[...]
        for j in range(N_CHIPS - 1):
            acc = acc + got_ref[j].astype(jnp.float32)
        o_ref[...] = acc

    return pl.pallas_call(
        body, name=name,
        grid_spec=pltpu.PrefetchScalarGridSpec(
            num_scalar_prefetch=1, grid=(rh // tr,),
            in_specs=[pl.BlockSpec((None, tr, cols), lambda i, p_ref: (p_ref[0], i, 0)),
                      pl.BlockSpec((N_CHIPS - 1, tr, cols), lambda i, p_ref: (0, i, 0))],
            out_specs=pl.BlockSpec((tr, cols), lambda i, p_ref: (i, 0))),
        out_shape=jax.ShapeDtypeStruct((rh, cols), jnp.float32),
        compiler_params=_params("parallel"),
    )(chip, s32, got)


def _grad_pair_out(halves):
    nt = len(halves)

    def body(*refs):
        h_refs, got_refs = refs[:nt], refs[nt:2 * nt]
        send_sems, recv_sems = refs[2 * nt:]
        x, y, c, _ = _place()
        sibling = (x, y, 1 - c)

        def copy(t, src, dst):
            return pltpu.make_async_remote_copy(src_ref=src, dst_ref=dst, send_sem=send_sems.at[t],
                                                recv_sem=recv_sems.at[t], device_id=sibling, device_id_type=MESH)

        for t in range(nt):
            for off, n in _pieces(halves[t].shape[0]):
                copy(t, h_refs[t].at[pl.ds(off, n), :], got_refs[t].at[pl.ds(off, n), :]).start()
        for t in range(nt):
            copy(t, h_refs[t], got_refs[t]).wait()

    return pl.pallas_call(
        body, name="grad_pair_out",
        out_shape=[jax.ShapeDtypeStruct(h.shape, h.dtype) for h in halves],
        in_specs=[HBM] * nt, out_specs=[HBM] * nt,
        scratch_shapes=[pltpu.SemaphoreType.DMA((nt,)), pltpu.SemaphoreType.DMA((nt,))],
    )(*halves)


def _ada_part(c_all, w_ada):
    L, _, ncol = w_ada.shape
    tn = _tile(ncol, 512)

    def body(c_ref, w_ref, cond_ref, part_ref):
        cv = c_ref[...]
        cond = cv * jax.nn.sigmoid(cv)
        cond_ref[...] = cond
        part_ref[0] = jnp.dot(cond, w_ref[0], precision=lax.Precision.HIGHEST, preferred_element_type=jnp.float32)

    return pl.pallas_call(
        body, name="ada_part", grid=(L, ncol // tn),
        in_specs=[_full((N_DEV, D)), pl.BlockSpec((1, D, tn), lambda l, j: (l, 0, j))],
        out_specs=[_full((N_DEV, D)), pl.BlockSpec((1, N_DEV, tn), lambda l, j: (l, 0, j))],
        out_shape=[jax.ShapeDtypeStruct((N_DEV, D), jnp.float32), jax.ShapeDtypeStruct((L, N_DEV, ncol), jnp.float32)],
        compiler_params=_params("arbitrary", "arbitrary"),
    )(c_all, w_ada)


def _adamw_math(w, g, m, v):
    m = ADAM_B1 * m + (1.0 - ADAM_B1) * g
    v = ADAM_B2 * v + (1.0 - ADAM_B2) * jnp.square(g)
    m_hat = m / (1.0 - ADAM_B1 ** ADAM_STEP)
    v_hat = v / (1.0 - ADAM_B2 ** ADAM_STEP)
    delta = -ADAM_LR * (m_hat / (jnp.sqrt(v_hat) + ADAM_EPS) + ADAM_WD * w)
    return delta, m, v


def _adamw(w, g, m, v, name):
    shape = w.shape
    cols = shape[-1]
    rows = math.prod(shape[:-1])
    tr = _tile(rows, 512)
    two_d = lambda t: t.reshape(rows, cols)

    def body(w_ref, g_ref, m_ref, v_ref, d_ref, mo_ref, vo_ref):
        d_ref[...], mo_ref[...], vo_ref[...] = _adamw_math(w_ref[...], g_ref[...], m_ref[...], v_ref[...])

    out = jax.ShapeDtypeStruct((rows, cols), jnp.float32)
    outs = pl.pallas_call(
        body, name=name, grid=(rows // tr,), in_specs=[_rows(tr, cols)] * 4, out_specs=[_rows(tr, cols)] * 3,
        out_shape=[out, out, out], compiler_params=_params("parallel"),
    )(two_d(w), two_d(g), two_d(m), two_d(v))
    return [t.reshape(shape) for t in outs]


def _adamw_halves(w, mine, got, m, v, core, name):
    shape = w.shape
    cols = shape[-1]
    rows = math.prod(shape[:-1])
    rh = rows // 2
    tr = _tile(rh, 512)
    nbh = rh // tr
    two_d = lambda t: t.reshape(rows, cols)

    def body(c_ref, w_ref, a_ref, b_ref, m_ref, v_ref, g_ref, d_ref, mo_ref, vo_ref):
        g = jnp.where(pl.program_id(0) // nbh == c_ref[0], a_ref[...], b_ref[...])
        g_ref[...] = g
        d_ref[...], mo_ref[...], vo_ref[...] = _adamw_math(w_ref[...], g, m_ref[...], v_ref[...])

    row = pl.BlockSpec((tr, cols), lambda i, c_ref: (i, 0))

    def half(keep):
        return pl.BlockSpec((tr, cols), lambda i, c_ref: (jnp.where((i // nbh == c_ref[0]) == keep, i % nbh, 0), 0))

    out = jax.ShapeDtypeStruct((rows, cols), jnp.float32)
    outs = pl.pallas_call(
        body, name=name,
        grid_spec=pltpu.PrefetchScalarGridSpec(
            num_scalar_prefetch=1, grid=(rows // tr,),
            in_specs=[row, half(True), half(False), row, row], out_specs=[row] * 4),
        out_shape=[out] * 4, compiler_params=_params("arbitrary"),
    )(core, two_d(w), mine, got, two_d(m), two_d(v))
    return [t.reshape(shape) for t in outs]


def _ada_grad_adamw(cond_t, dm, w, m, v):
    L, _, ncol = w.shape
    tn = _tile(ncol, 512)

    def body(ct_ref, dm_ref, w_ref, m_ref, v_ref, g_ref, d_ref, mo_ref, vo_ref):
        g = ct_ref[:, 0:1] * dm_ref[0, 0:1, :]
        for b in range(1, N_DEV):
            g = g + ct_ref[:, b:b + 1] * dm_ref[0, b:b + 1, :]
        g_ref[0] = g
        d_ref[0], mo_ref[0], vo_ref[0] = _adamw_math(w_ref[0], g, m_ref[0], v_ref[0])

    wblk = pl.BlockSpec((1, D, tn), lambda l, j: (l, 0, j))
    out = jax.ShapeDtypeStruct(w.shape, jnp.float32)
    return pl.pallas_call(
        body, name="ada_grad_adamw", grid=(L, ncol // tn),
        in_specs=[_full((D, N_DEV)), pl.BlockSpec((1, N_DEV, tn), lambda l, j: (l, 0, j)), wblk, wblk, wblk],
        out_specs=[wblk] * 4, out_shape=[out] * 4, compiler_params=_params("parallel", "parallel"),
    )(cond_t, dm, w, m, v)


def _small_adamw(gathered, w, m, v):
    def body(ga_ref, w_ref, m_ref, v_ref, g_ref, d_ref, mo_ref, vo_ref):
        g = ga_ref[0]
        for dev in range(1, N_DEV):
            g = g + ga_ref[dev]
        g_ref[...] = g
        d_ref[...], mo_ref[...], vo_ref[...] = _adamw_math(w_ref[...], g, m_ref[...], v_ref[...])

    out = jax.ShapeDtypeStruct((SMALL_ROWS, LANES), jnp.float32)
    return pl.pallas_call(
        body, name="small_adamw", out_shape=[out] * 4,
        in_specs=[pl.BlockSpec(memory_space=pltpu.VMEM)] * 4, out_specs=[pl.BlockSpec(memory_space=pltpu.VMEM)] * 4,
    )(gathered, w, m, v)


def _one_hot_pick(arr, index, axis):
    n = arr.shape[axis]
    shape = [1] * arr.ndim
    shape[axis] = n
    hot = (jnp.arange(n) == index).astype(arr.dtype).reshape(shape)
    return jnp.sum(arr * hot, axis=axis)


def kernel(x, c, positions, w_ada, b_ada, g_mix, g_mlp, mla_w_dq, mla_g_q, mla_w_uq, mla_w_dkv, mla_g_kv, mla_w_ukv, mla_w_o, swa_w_qkv, swa_b_qkv, swa_sinks, swa_w_o, swa_b_o, w_ff1, w_ff2, g_final, loss_target, m_w_ada, m_b_ada, m_g_mix, m_g_mlp, m_mla_w_dq, m_mla_g_q, m_mla_w_uq, m_mla_w_dkv, m_mla_g_kv, m_mla_w_ukv, m_mla_w_o, m_swa_w_qkv, m_swa_b_qkv, m_swa_sinks, m_swa_w_o, m_swa_b_o, m_w_ff1, m_w_ff2, m_g_final, v_w_ada, v_b_ada, v_g_mix, v_g_mlp, v_mla_w_dq, v_mla_g_q, v_mla_w_uq, v_mla_w_dkv, v_mla_g_kv, v_mla_w_ukv, v_mla_w_o, v_swa_w_qkv, v_swa_b_qkv, v_swa_sinks, v_swa_w_o, v_swa_b_o, v_w_ff1, v_w_ff2, v_g_final):
    W = dict(w_ada=w_ada, b_ada=b_ada, g_mix=g_mix, g_mlp=g_mlp, mla_w_dq=mla_w_dq, mla_g_q=mla_g_q, mla_w_uq=mla_w_uq,
             mla_w_dkv=mla_w_dkv, mla_g_kv=mla_g_kv, mla_w_ukv=mla_w_ukv, mla_w_o=mla_w_o, swa_w_qkv=swa_w_qkv,
             swa_b_qkv=swa_b_qkv, swa_sinks=swa_sinks, swa_w_o=swa_w_o, swa_b_o=swa_b_o, w_ff1=w_ff1, w_ff2=w_ff2,
             g_final=g_final)
    M = dict(w_ada=m_w_ada, b_ada=m_b_ada, g_mix=m_g_mix, g_mlp=m_g_mlp, mla_w_dq=m_mla_w_dq, mla_g_q=m_mla_g_q,
             mla_w_uq=m_mla_w_uq, mla_w_dkv=m_mla_w_dkv, mla_g_kv=m_mla_g_kv, mla_w_ukv=m_mla_w_ukv, mla_w_o=m_mla_w_o,
             swa_w_qkv=m_swa_w_qkv, swa_b_qkv=m_swa_b_qkv, swa_sinks=m_swa_sinks, swa_w_o=m_swa_w_o, swa_b_o=m_swa_b_o,
             w_ff1=m_w_ff1, w_ff2=m_w_ff2, g_final=m_g_final)
    V = dict(w_ada=v_w_ada, b_ada=v_b_ada, g_mix=v_g_mix, g_mlp=v_g_mlp, mla_w_dq=v_mla_w_dq, mla_g_q=v_mla_g_q,
             mla_w_uq=v_mla_w_uq, mla_w_dkv=v_mla_w_dkv, mla_g_kv=v_mla_g_kv, mla_w_ukv=v_mla_w_ukv, mla_w_o=v_mla_w_o,
             swa_w_qkv=v_swa_w_qkv, swa_b_qkv=v_swa_b_qkv, swa_sinks=v_swa_sinks, swa_w_o=v_swa_w_o, swa_b_o=v_swa_b_o,
             w_ff1=v_w_ff1, w_ff2=v_w_ff2, g_final=v_g_final)
    order = list(W)
    names = list(SHARDED)
    core = lax.axis_index("c")
    chip = 2 * lax.axis_index("x") + lax.axis_index("y")
    dev = 2 * chip + core
    core_arr = core.astype(jnp.int32).reshape(1)
    chip_arr = chip.astype(jnp.int32).reshape(1)

    def whole(n, g, own):
        g = lax.dynamic_update_slice(g, own[None], (chip, 0, 0))
        if n in ("w_ff1", "w_ff2"):
            return g
        if n in COL_SPLIT:
            return g.transpose(1, 0, 2).reshape(g.shape[1], N_CHIPS * g.shape[2])
        return g.reshape(N_CHIPS * g.shape[1], g.shape[2])

    early = [n for n in names if n.startswith("mla_")]
    local = {n: W[n].astype(MXU_DTYPE).reshape(_view2d(n)) for n in early}
    wts = {n: whole(n, g, local[n]) for n, g in zip(early, _weight_gather([local[n] for n in early]))}

    nbq, nbo = BIASES["swa_b_qkv"] // N_CHIPS, BIASES["swa_b_o"] // N_CHIPS
    first = jnp.concatenate([c.reshape(-1), swa_b_qkv.reshape(-1), swa_b_o.reshape(-1),
                             jnp.zeros((16 * LANES - D - nbq - nbo,), jnp.float32)]).reshape(16, LANES)
    first_all = _all_gather(first).reshape(N_DEV, 16 * LANES)
    c_all = first_all[:, :D]
    south = first_all[0::2]
    wts["swa_b_qkv"] = south[:, D:D + nbq].reshape(1, N_CHIPS * nbq)
    wts["swa_b_o"] = south[:, D + nbq:D + nbq + nbo].reshape(1, N_CHIPS * nbo)
    cond_all, part = _ada_part(c_all, w_ada)
    ncol = w_ada.shape[2]
    part_all = _all_gather(part.reshape(-1, LANES)).reshape(N_DEV, DEPTH, N_DEV, ncol)
    mine = _one_hot_pick(part_all[0::2], dev, axis=2)
    mod = mine.transpose(1, 0, 2).reshape(DEPTH, N_CHIPS * ncol) + b_ada
    vecs = jnp.concatenate([mod.reshape(DEPTH, 6, D), g_mix[:, None, :], g_mlp[:, None, :]], axis=1)

    late = [("w_ff1", 0), ("w_ff2", 0), ("swa_w_qkv", None), ("swa_w_o", None), ("w_ff1", 1), ("w_ff2", 1)]
    late_local = [(W[n][0] if l is None else W[n][l]).astype(MXU_DTYPE) for n, l in late]
    send_sems, recv_sems, passed, lands, token = _late_gather_start(late_local, [vecs] + [wts[n] for n in early])

    def late_weights(after):
        got = _late_gather_wait(send_sems, recv_sems, passed, lands, after)
        out = {"w_ff1": [None] * DEPTH, "w_ff2": [None] * DEPTH}
        for (n, l), g, own in zip(late, got, late_local):
            if l is None:
                out[n] = whole(n, g, own)
            else:
                out[n][l] = whole(n, g, own)
        return out

    grad_x, grads, small = _sequence_step(x[0], loss_target[0], positions[0], vecs, mla_g_q + token[0, 0], mla_g_kv,
                                          swa_sinks, g_final, wts, late_weights)

    small["b_ada"] = small.pop("dmod")
    small_all = _all_gather(_pack_small(small))
    pk = lambda src: _pack_small({n: src[n] for n in SMALL if n != "loss" and n not in BIASES})
    g_small, d_small, m_small, v_small = [_unpack_small(t) for t in _small_adamw(small_all, pk(W), pk(M), pk(V))]
    off, n = _small_slots()["b_ada"]
    dmod_all = small_all.reshape(N_DEV, -1)[:, off:off + n].reshape(N_DEV, DEPTH, N_CHIPS, ncol)
    dm = _one_hot_pick(dmod_all, chip, axis=2).transpose(1, 0, 2)
    ada = _ada_grad_adamw(cond_all.T, dm, w_ada, m_w_ada, v_w_ada)

    gl = [grads[n] for n in names]
    got = _grad_pair_in(gl)
    sums = [_pair_sum(g, s, core_arr, "pair_sum_" + n) for n, g, s in zip(names, gl, got)]
    others = _grad_chip_exchange([s16 for _, s16 in sums])
    halves = [_chip_sum(s32, o, chip_arr, "chip_sum_" + n) for n, (s32, _), o in zip(names, sums, others)]
    sibling_halves = _grad_pair_out(halves)

    res = {"w_ada": ada}
    for n, mine_h, got_h in zip(names, halves, sibling_halves):
        res[n] = _adamw_halves(W[n], mine_h, got_h, M[n], V[n], core_arr, "adamw_" + n)
    for n, width in BIASES.items():
        g = _one_hot_pick(g_small[n].reshape(N_CHIPS, width // N_CHIPS), chip, axis=0).reshape(1, -1)
        res[n] = [g] + _adamw(W[n], g, M[n], V[n], "adamw_" + n)
    for name in order:
        if name not in res:
            res[name] = [t[name] for t in (g_small, d_small, m_small, v_small)]
    outs = [g_small["loss"], grad_x[None]]
    for k in range(4):
        outs += [res[name][k] for name in order]
    return tuple(outs)
```

```python
import functools
import math

import jax
import jax.numpy as jnp
import numpy as np
from jax import lax
from jax.experimental import pallas as pl
from jax.experimental.pallas import tpu as pltpu

D = 1024
DEPTH = 2
MLA_HEADS = 8
QK_NOPE = 128
QK_ROPE = 64
V_DIM = 128
Q_LORA = 384
KV_LORA = 256
ROPE_THETA = 10000.0
SWA_HEADS = 16
SWA_KV_HEADS = 4
SWA_HEAD_DIM = 64
SWA_GROUP = SWA_HEADS // SWA_KV_HEADS
WINDOW = 128
D_FF = 4 * D
EPS = 1e-6
ADAM_LR = 0.001
ADAM_B1 = 0.9
ADAM_B2 = 0.999
ADAM_EPS = 1e-08
ADAM_WD = 0.01
ADAM_STEP = 10

N_CHIPS = 4
N_DEV = 8
LANES = 128
QK_EXT = 256
MLA_SCALE = (QK_NOPE + QK_ROPE) ** -0.5
LOG2E = math.log2(math.e)
LN2 = math.log(2.0)
MLA_QSCALE = MLA_SCALE * LOG2E
ATTN_BLOCK = 1024
ATTN_SUB = 512
SWA_SCALE = SWA_HEAD_DIM ** -0.5
NEG = -1e30
MXU_DTYPE = jnp.bfloat16
VMEM_LIMIT = 56 * 1024 * 1024

R_SH1, R_SC1, R_GT1, R_SH2, R_SC2, R_GT2, R_GMIX, R_GMLP = range(8)
R_BO = 6


def _tile(n, pref):
    if n <= pref:
        return n
    for t in range(pref, 7, -1):
        if n % t == 0 and t % 8 == 0:
            return t
    return n


def _dot(a, b):
    return jnp.dot(a, b, preferred_element_type=jnp.float32)


def _dot_nt(a, b):
    return lax.dot_general(a, b, (((1,), (1,)), ((), ())), preferred_element_type=jnp.float32)


def _dot_tn(a, b):
    return lax.dot_general(a, b, (((0,), (0,)), ((), ())), preferred_element_type=jnp.float32)


def _rms(x):
    r = lax.rsqrt(jnp.mean(x * x, axis=-1, keepdims=True) + EPS)
    return x * r, r


def _rms_bwd(dxhat, xhat, r):
    return r * (dxhat - xhat * jnp.mean(dxhat * xhat, axis=-1, keepdims=True))


def _rowsum(v):
    return jnp.sum(v, axis=0, keepdims=True)


def _params(*sem):
    return pltpu.CompilerParams(dimension_semantics=sem, vmem_limit_bytes=VMEM_LIMIT)


def _full(shape):
    nd = len(shape)
    return pl.BlockSpec(shape, lambda *_: (0,) * nd)


def _rows(tm, cols):
    return pl.BlockSpec((tm, cols), lambda i, *_: (i, 0))


def _modulate_bwd(dh, x, vec_ref, r_g, r_sc, r_sh, ps_ref, dres):
    xhat, r = _rms(x)
    g = vec_ref[r_g:r_g + 1, :]
    n = xhat * g
    ps_ref[r_sh:r_sh + 1, :] += _rowsum(dh)
    ps_ref[r_sc:r_sc + 1, :] += _rowsum(dh * n)
    dn = dh * (1.0 + vec_ref[r_sc:r_sc + 1, :])
    ps_ref[r_g:r_g + 1, :] += _rowsum(dn * xhat)
    return dres + _rms_bwd(dn * g, xhat, r)


def _mla_pre(x, vec, wcat, g_q, g_kv, wuq, wukv, cs):
    T = x.shape[0]
    tm = _tile(T, 512)
    H = MLA_HEADS

    def body(x_ref, vec_ref, wcat_ref, gq_ref, gkv_ref, wuq_ref, wukv_ref, cs_ref, h_ref, z_ref, q_ref, k_ref, v_ref):
        xhat, _ = _rms(x_ref[...])
        h = xhat * vec_ref[R_GMIX:R_GMIX + 1, :] * (1.0 + vec_ref[R_SC1:R_SC1 + 1, :]) + vec_ref[R_SH1:R_SH1 + 1, :]
        hb = h.astype(MXU_DTYPE)
        h_ref[...] = hb
        z = _dot(hb, wcat_ref[...])
        z_ref[...] = z
        cq = (_rms(z[:, :Q_LORA])[0] * gq_ref[...]).astype(MXU_DTYPE)
        ckv = (_rms(z[:, Q_LORA:Q_LORA + KV_LORA])[0] * gkv_ref[...]).astype(MXU_DTYPE)
        cs_t = cs_ref[...]
        t = z[:, Q_LORA + KV_LORA:] * cs_t
        k_rope = (t + pltpu.roll(t, QK_ROPE, axis=1)).astype(MXU_DTYPE)
        low = lax.broadcasted_iota(jnp.int32, (1, LANES), 1) < QK_ROPE
        for hd in range(H):
            qf = _dot(cq, wuq_ref[hd])
            tq = qf[:, QK_NOPE:] * cs_t
            tq = tq + pltpu.roll(tq, QK_ROPE, axis=1)
            q_ref[hd, :, :QK_NOPE] = (qf[:, :QK_NOPE] * MLA_QSCALE).astype(MXU_DTYPE)
            q_ref[hd, :, QK_NOPE:] = jnp.where(low, tq * MLA_QSCALE, 0.0).astype(MXU_DTYPE)
            kvf = _dot(ckv, wukv_ref[hd])
            k_ref[hd, :, :QK_NOPE] = kvf[:, :QK_NOPE].astype(MXU_DTYPE)
            k_ref[hd, :, QK_NOPE:] = k_rope
            v_ref[hd] = kvf[:, QK_NOPE:].astype(MXU_DTYPE)

    zc = wcat.shape[1]
    return pl.pallas_call(
        body, name="mla_pre", grid=(T // tm,),
        in_specs=[_rows(tm, D), _full((8, D)), _full(wcat.shape), _full(g_q.shape), _full(g_kv.shape),
                  _full(wuq.shape), _full(wukv.shape), _rows(tm, LANES)],
        out_specs=[_rows(tm, D), _rows(tm, zc),
                   pl.BlockSpec((H, tm, QK_EXT), lambda i: (0, i, 0)),
                   pl.BlockSpec((H, tm, QK_EXT), lambda i: (0, i, 0)),
                   pl.BlockSpec((H, tm, V_DIM), lambda i: (0, i, 0))],
        out_shape=[jax.ShapeDtypeStruct((T, D), MXU_DTYPE), jax.ShapeDtypeStruct((T, zc), jnp.float32),
                   jax.ShapeDtypeStruct((H, T, QK_EXT), MXU_DTYPE), jax.ShapeDtypeStruct((H, T, QK_EXT), MXU_DTYPE),
                   jax.ShapeDtypeStruct((H, T, V_DIM), MXU_DTYPE)],
        compiler_params=_params("parallel"),
    )(x, vec, wcat, g_q, g_kv, wuq, wukv, cs)


def _mla_attn_fwd(q, k, v):
    H, T, _ = q.shape
    tb = _tile(T, ATTN_BLOCK)
    sub = min(ATTN_SUB, tb)
    ns, nb = tb // sub, T // tb

    def body(q_ref, k_ref, v_ref, o_ref, lse_ref, m_sc, l_sc, acc_sc):
        qi, kj = pl.program_id(1), pl.program_id(2)

        @pl.when(kj == 0)
        def _():
            m_sc[...] = jnp.full_like(m_sc, NEG)
            l_sc[...] = jnp.zeros_like(l_sc)
            acc_sc[...] = jnp.zeros_like(acc_sc)

        def update(r, kk, masked):
            rows, keys = pl.ds(r * sub, sub), pl.ds(kk * sub, sub)
            s = _dot_nt(q_ref[0, rows, :], k_ref[0, keys, :])
            if masked:
                row = lax.broadcasted_iota(jnp.int32, (sub, sub), 0)
                col = lax.broadcasted_iota(jnp.int32, (sub, sub), 1)
                s = jnp.where(col <= row, s, NEG)
            m_prev = m_sc[rows, :]
            m_new = jnp.maximum(m_prev, jnp.max(s, axis=1, keepdims=True))
            alpha = jnp.exp2(m_prev - m_new)
            p = jnp.exp2(s - jnp.tile(m_new, (1, sub // LANES)))
            l_sc[rows, :] = alpha * l_sc[rows, :] + jnp.sum(p, axis=1, keepdims=True)
            acc_sc[rows, :] = alpha * acc_sc[rows, :] + _dot(p.astype(MXU_DTYPE), v_ref[0, keys, :])
            m_sc[rows, :] = m_new

        @pl.when(kj < qi)
        def _():
            for kk in range(ns):
                for r in range(ns):
                    update(r, kk, False)

        @pl.when(kj == qi)
        def _():
            for kk in range(ns):
                for r in range(kk, ns):
                    update(r, kk, r == kk)
            l = l_sc[...]
            o_ref[...] = (acc_sc[...] / l).astype(o_ref.dtype)
            lse = m_sc[...] + jnp.log2(l)
            pick = (lax.broadcasted_iota(jnp.int32, (8, LANES), 1) == 0).astype(jnp.float32)
            row = lax.dot_general(pick, lse, (((1,), (1,)), ((), ())), precision=lax.Precision.HIGHEST,
                                  preferred_element_type=jnp.float32)
            lse_ref[0] = row[0:1, :]

    kv_idx = lambda h, i, j: (h, jnp.minimum(i, j), 0)
    return pl.pallas_call(
        body, name="mla_attn_fwd", grid=(H, nb, nb),
        in_specs=[pl.BlockSpec((1, tb, QK_EXT), lambda h, i, j: (h, i, 0)),
                  pl.BlockSpec((1, tb, QK_EXT), kv_idx),
                  pl.BlockSpec((1, tb, V_DIM), kv_idx)],
        out_specs=[pl.BlockSpec((tb, V_DIM), lambda h, i, j: (i, h)),
                   pl.BlockSpec((1, 1, tb), lambda h, i, j: (h, 0, i))],
        out_shape=[jax.ShapeDtypeStruct((T, H * V_DIM), MXU_DTYPE), jax.ShapeDtypeStruct((H, 1, T), jnp.float32)],
        scratch_shapes=[pltpu.VMEM((tb, LANES), jnp.float32), pltpu.VMEM((tb, LANES), jnp.float32),
                        pltpu.VMEM((tb, V_DIM), jnp.float32)],
        compiler_params=_params("parallel", "parallel", "arbitrary"),
    )(q, k, v)


def _post_attn(o, x, w_o, bias, vec):
    T = x.shape[0]
    tm = _tile(T, 512)

    def body(o_ref, x_ref, w_ref, b_ref, vec_ref, y_ref, xm_ref, h_ref):
        y = _dot(o_ref[...], w_ref[...]) + b_ref[...]
        y_ref[...] = y.astype(y_ref.dtype)
        xm = x_ref[...] + vec_ref[R_GT1:R_GT1 + 1, :] * y
        xm_ref[...] = xm
        xhat, _ = _rms(xm)
        h = xhat * vec_ref[R_GMLP:R_GMLP + 1, :] * (1.0 + vec_ref[R_SC2:R_SC2 + 1, :]) + vec_ref[R_SH2:R_SH2 + 1, :]
        h_ref[...] = h.astype(h_ref.dtype)

    return pl.pallas_call(
        body, name="post_attn", grid=(T // tm,),
        in_specs=[_rows(tm, D), _rows(tm, D), _full((D, D)), _full((1, D)), _full((8, D))],
        out_specs=[_rows(tm, D), _rows(tm, D), _rows(tm, D)],
        out_shape=[jax.ShapeDtypeStruct((T, D), MXU_DTYPE), jax.ShapeDtypeStruct((T, D), jnp.float32),
                   jax.ShapeDtypeStruct((T, D), MXU_DTYPE)],
        compiler_params=_params("parallel"),
    )(o, x, w_o, bias, vec)


def _ff_specs(tf):
    per = D_FF // N_CHIPS // tf
    w1 = pl.BlockSpec((None, D, tf), lambda i, f: (f // per, 0, f % per))
    w2 = pl.BlockSpec((None, tf, D), lambda i, f: (f // per, f % per, 0))
    return w1, w2


def _mlp_fwd(h2, w1, w2, xm, vec):
    T = h2.shape[0]
    tm = _tile(T, 1024)
    tf = _tile(D_FF // N_CHIPS, 512)
    nf = D_FF // tf
    w1_spec, w2_spec = _ff_specs(tf)

    def body(h_ref, w1_ref, w2_ref, xm_ref, vec_ref, a_ref, y_ref, xo_ref, acc):
        f = pl.program_id(1)

        @pl.when(f == 0)
        def _():
            acc[...] = jnp.zeros_like(acc)

        u = jnp.maximum(_dot(h_ref[...], w1_ref[...]), 0.0)
        ab = (u * u).astype(MXU_DTYPE)
        a_ref[...] = ab
        acc[...] += _dot(ab, w2_ref[...])

        @pl.when(f == nf - 1)
        def _():
            y = acc[...]
            y_ref[...] = y.astype(y_ref.dtype)
            xo_ref[...] = xm_ref[...] + vec_ref[R_GT2:R_GT2 + 1, :] * y

    return pl.pallas_call(
        body, name="mlp_fwd", grid=(T // tm, nf),
        in_specs=[_rows(tm, D), w1_spec, w2_spec, _rows(tm, D), _full((8, D))],
        out_specs=[pl.BlockSpec((tm, tf), lambda i, f: (i, f)), _rows(tm, D), _rows(tm, D)],
        out_shape=[jax.ShapeDtypeStruct((T, D_FF), MXU_DTYPE), jax.ShapeDtypeStruct((T, D), MXU_DTYPE),
                   jax.ShapeDtypeStruct((T, D), jnp.float32)],
        scratch_shapes=[pltpu.VMEM((tm, D), jnp.float32)],
        compiler_params=_params("parallel", "arbitrary"),
    )(h2, w1, w2, xm, vec)


def _swa_pre(x, vec, w_qkv, b_qkv):
    T = x.shape[0]
    tm = _tile(T, 512)
    nq = SWA_HEADS * SWA_HEAD_DIM
    nk = SWA_KV_HEADS * SWA_HEAD_DIM

    def body(x_ref, vec_ref, w_ref, b_ref, h_ref, q_ref, k_ref, v_ref):
        xhat, _ = _rms(x_ref[...])
        h = xhat * vec_ref[R_GMIX:R_GMIX + 1, :] * (1.0 + vec_ref[R_SC1:R_SC1 + 1, :]) + vec_ref[R_SH1:R_SH1 + 1, :]
        hb = h.astype(MXU_DTYPE)
        h_ref[...] = hb
        qkv = _dot(hb, w_ref[...]) + b_ref[...]
        q_ref[...] = (qkv[:, :nq] * SWA_SCALE).astype(MXU_DTYPE)
        k_ref[...] = qkv[:, nq:nq + nk].astype(MXU_DTYPE)
        v_ref[...] = qkv[:, nq + nk:].astype(MXU_DTYPE)

    return pl.pallas_call(
        body, name="swa_pre", grid=(T // tm,),
        in_specs=[_rows(tm, D), _full((8, D)), _full(w_qkv.shape), _full(b_qkv.shape)],
        out_specs=[_rows(tm, D), _rows(tm, nq), _rows(tm, nk), _rows(tm, nk)],
        out_shape=[jax.ShapeDtypeStruct((T, D), MXU_DTYPE), jax.ShapeDtypeStruct((T, nq), MXU_DTYPE),
                   jax.ShapeDtypeStruct((T, nk), MXU_DTYPE), jax.ShapeDtypeStruct((T, nk), MXU_DTYPE)],
        compiler_params=_params("parallel"),
    )(x, vec, w_qkv, b_qkv)


def _swa_bias():
    W = WINDOW
    slopes = 2.0 ** (-8.0 * np.arange(1, SWA_HEADS + 1) / SWA_HEADS)
    dist = W + np.arange(W)[None, :] - np.arange(2 * W)[:, None]
    inside = (dist >= 0) & (dist < W)
    bias = np.where(inside[None], -slopes[:, None, None] * dist[None].astype(np.float64), NEG)
    bias = bias.reshape(SWA_KV_HEADS, SWA_GROUP, 2 * W, W).transpose(0, 2, 1, 3)
    return jnp.asarray(bias.reshape(SWA_KV_HEADS, 2 * W, SWA_GROUP * W), jnp.float32)


def _swa_probs(n, kh, qt_ref, kp_ref, kc_ref, bias_ref, sink_ref):
    W, Dh, G = WINDOW, SWA_HEAD_DIM, SWA_GROUP
    qt = jnp.concatenate([qt_ref[(kh * G + g) * Dh:(kh * G + g + 1) * Dh, :] for g in range(G)], axis=1)
    kb = jnp.concatenate([kp_ref[:, kh * Dh:(kh + 1) * Dh], kc_ref[:, kh * Dh:(kh + 1) * Dh]], axis=0)
    s = _dot(kb, qt) + bias_ref[kh]
    key = lax.broadcasted_iota(jnp.int32, (2 * W, 1), 0)
    s = jnp.where((key >= W) | (n > 0), s, NEG)
    sink = sink_ref[kh]
    m = jnp.maximum(jnp.max(s, axis=0, keepdims=True), sink)
    p = jnp.exp(s - m)
    p_sink = jnp.exp(sink - m)
    inv = 1.0 / (jnp.sum(p, axis=0, keepdims=True) + p_sink)
    return qt, kb, p * inv, p_sink * inv


def _swa_attn_fwd(qt, k, v, bias, sink_rows):
    T = qt.shape[1]
    W, Dh, G, Hk = WINDOW, SWA_HEAD_DIM, SWA_GROUP, SWA_KV_HEADS
    nk = Hk * Dh

    def body(qt_ref, kp_ref, kc_ref, vp_ref, vc_ref, bias_ref, sink_ref, ot_ref):
        n = pl.program_id(0)
        for kh in range(Hk):
            _, _, pn, _ = _swa_probs(n, kh, qt_ref, kp_ref, kc_ref, bias_ref, sink_ref)
            vb = jnp.concatenate([vp_ref[:, kh * Dh:(kh + 1) * Dh], vc_ref[:, kh * Dh:(kh + 1) * Dh]], axis=0)
            ot = _dot_tn(vb, pn.astype(MXU_DTYPE))
            for g in range(G):
                ot_ref[(kh * G + g) * Dh:(kh * G + g + 1) * Dh, :] = ot[:, g * W:(g + 1) * W].astype(ot_ref.dtype)

    prev = lambda n: (jnp.maximum(n - 1, 0), 0)
    cur = lambda n: (n, 0)
    col = lambda n: (0, n)
    return pl.pallas_call(
        body, name="swa_attn_fwd", grid=(T // W,),
        in_specs=[pl.BlockSpec((D, W), col), pl.BlockSpec((W, nk), prev), pl.BlockSpec((W, nk), cur),
                  pl.BlockSpec((W, nk), prev), pl.BlockSpec((W, nk), cur), _full(bias.shape), _full(sink_rows.shape)],
        out_specs=pl.BlockSpec((D, W), col),
        out_shape=jax.ShapeDtypeStruct((D, T), MXU_DTYPE),
        compiler_params=_params("parallel"),
    )(qt, k, k, v, v, bias, sink_rows)


def _final_loss(x, tgt, g):
    T = x.shape[0]
    tm = _tile(T, 512)

    def body(x_ref, t_ref, g_ref, loss_ref, dx_ref, dg_ref):
        @pl.when(pl.program_id(0) == 0)
        def _():
            loss_ref[...] = jnp.zeros_like(loss_ref)
            dg_ref[...] = jnp.zeros_like(dg_ref)

        xhat, r = _rms(x_ref[...])
        gv = g_ref[...]
        e = xhat * gv - t_ref[...]
        loss_ref[...] += 0.5 * jnp.sum(jnp.mean(e * e, axis=-1, keepdims=True), axis=0, keepdims=True)
        dy = e * (1.0 / D)
        dg_ref[...] += _rowsum(dy * xhat)
        dx_ref[...] = _rms_bwd(dy * gv, xhat, r)

    return pl.pallas_call(
        body, name="final_loss", grid=(T // tm,),
        in_specs=[_rows(tm, D), _rows(tm, D), _full((1, D))],
        out_specs=[_full((8, LANES)), _rows(tm, D), _full((1, D))],
        out_shape=[jax.ShapeDtypeStruct((8, LANES), jnp.float32), jax.ShapeDtypeStruct((T, D), jnp.float32),
                   jax.ShapeDtypeStruct((1, D), jnp.float32)],
        compiler_params=_params("arbitrary"),
    )(x, tgt, g)


def _mlp_bwd(dxo, y2, a, w1, w2, xm, vec):
    T = dxo.shape[0]
    tm = _tile(T, 1024)
    tf = _tile(D_FF // N_CHIPS, 512)
    nf = D_FF // tf
    w1_spec, w2_spec = _ff_specs(tf)

    def body(dxo_ref, y_ref, a_ref, w1_ref, w2_ref, xm_ref, vec_ref, du_ref, dy_ref, dxm_ref, ps_ref, dyb, acc):
        i, f = pl.program_id(0), pl.program_id(1)

        @pl.when((i == 0) & (f == 0))
        def _():
            ps_ref[...] = jnp.zeros_like(ps_ref)

        @pl.when(f == 0)
        def _():
            dxo_t = dxo_ref[...]
            d = (dxo_t * vec_ref[R_GT2:R_GT2 + 1, :]).astype(MXU_DTYPE)
            dyb[...] = d
            dy_ref[...] = d
            acc[...] = jnp.zeros_like(acc)
            ps_ref[R_GT2:R_GT2 + 1, :] += _rowsum(dxo_t * y_ref[...].astype(jnp.float32))

        da = _dot_nt(dyb[...], w2_ref[...])
        dub = (da * (2.0 * jnp.sqrt(a_ref[...].astype(jnp.float32)))).astype(MXU_DTYPE)
        du_ref[...] = dub
        acc[...] += _dot_nt(dub, w1_ref[...])

        @pl.when(f == nf - 1)
        def _():
            dxm_ref[...] = _modulate_bwd(acc[...], xm_ref[...], vec_ref, R_GMLP, R_SC2, R_SH2, ps_ref, dxo_ref[...])

    return pl.pallas_call(
        body, name="mlp_bwd", grid=(T // tm, nf),
        in_specs=[_rows(tm, D), _rows(tm, D), pl.BlockSpec((tm, tf), lambda i, f: (i, f)), w1_spec, w2_spec,
                  _rows(tm, D), _full((8, D))],
        out_specs=[pl.BlockSpec((tm, tf), lambda i, f: (i, f)), _rows(tm, D), _rows(tm, D), _full((8, D))],
        out_shape=[jax.ShapeDtypeStruct((T, D_FF), MXU_DTYPE), jax.ShapeDtypeStruct((T, D), MXU_DTYPE),
                   jax.ShapeDtypeStruct((T, D), jnp.float32), jax.ShapeDtypeStruct((8, D), jnp.float32)],
        scratch_shapes=[pltpu.VMEM((tm, D), MXU_DTYPE), pltpu.VMEM((tm, D), jnp.float32)],
        compiler_params=_params("arbitrary", "arbitrary"),
    )(dxo, y2, a, w1, w2, xm, vec)


def _mm_tn(a, g, name, split=None, layers=1, layer=0, into=None):
    T, K = a.shape
    N = g.shape[1]
    kq = K // N_CHIPS if split == "rows" else K
    nq = N // N_CHIPS if split == "cols" else N
    bk, bn, bt = _tile(kq, 1024), _tile(nq, 1024), _tile(T, 1024)
    if nq % bn or bn % LANES:
        bn = nq
    kper, nper = kq // bk, nq // bn

    def body(*refs):
        a_ref, g_ref, o_ref = refs[0], refs[1], refs[-1]

        @pl.when(pl.program_id(2) == 0)
        def _():
            o_ref[...] = jnp.zeros_like(o_ref)

        o_ref[...] += _dot_tn(a_ref[...], g_ref[...])

    in_specs = [pl.BlockSpec((bt, bk), lambda k, n, t: (t, k)), pl.BlockSpec((bt, bn), lambda k, n, t: (t, n))]
    args = [a, g]
    aliases = {}
    if split is None:
        out_spec = pl.BlockSpec((bk, bn), lambda k, n, t: (k, n))
        out_shape = jax.ShapeDtypeStruct((K, N), jnp.float32)
    else:
        if split == "cols":
            idx = lambda k, n, t: (n // nper, layer, k, n % nper)
        else:
            idx = lambda k, n, t: (k // kper, layer, k % kper, n)
        out_spec = pl.BlockSpec((None, None, bk, bn), idx)
        out_shape = jax.ShapeDtypeStruct((N_CHIPS, layers, kq, nq), jnp.float32)
        if into is not None:
            in_specs.append(pl.BlockSpec(memory_space=pl.ANY))
            args.append(into)
            aliases = {2: 0}
    return pl.pallas_call(
        body, name=name, grid=(K // bk, N // bn, T // bt), in_specs=in_specs, out_specs=out_spec, out_shape=out_shape,
        input_output_aliases=aliases, compiler_params=_params("parallel", "parallel", "arbitrary"),
    )(*args)


def _attn_out_bwd(dxm, y1, o, w_o, vec, with_delta):
    T = dxm.shape[0]
    tm = _tile(T, 512)
    H = MLA_HEADS

    def body(dxm_ref, y_ref, o_ref, w_ref, vec_ref, dy_ref, do_ref, ps_ref, *delta_ref):
        @pl.when(pl.program_id(0) == 0)
        def _():
            ps_ref[...] = jnp.zeros_like(ps_ref)

        dxm_t = dxm_ref[...]
        dy = dxm_t * vec_ref[R_GT1:R_GT1 + 1, :]
        ps_ref[R_GT1:R_GT1 + 1, :] += _rowsum(dxm_t * y_ref[...].astype(jnp.float32))
        ps_ref[R_BO:R_BO + 1, :] += _rowsum(dy)
        dyb = dy.astype(MXU_DTYPE)
        dy_ref[...] = dyb
        do = _dot_nt(dyb, w_ref[...])
        do_ref[...] = do.astype(do_ref.dtype)
        if with_delta:
            of = o_ref[...].astype(jnp.float32)
            ones = jnp.ones((8, V_DIM), jnp.float32)
            for hd in range(H):
                sl = slice(hd * V_DIM, (hd + 1) * V_DIM)
                d = lax.dot_general(ones, do[:, sl] * of[:, sl], (((1,), (1,)), ((), ())),
                                    precision=lax.Precision.HIGHEST, preferred_element_type=jnp.float32)
                delta_ref[0][hd] = d[0:1, :]

    out_specs = [_rows(tm, D), _rows(tm, D), _full((8, D))]
    out_shape = [jax.ShapeDtypeStruct((T, D), MXU_DTYPE), jax.ShapeDtypeStruct((T, D), MXU_DTYPE),
                 jax.ShapeDtypeStruct((8, D), jnp.float32)]
    if with_delta:
        out_specs.append(pl.BlockSpec((H, 1, tm), lambda i: (0, 0, i)))
        out_shape.append(jax.ShapeDtypeStruct((H, 1, T), jnp.float32))
    return pl.pallas_call(
        body, name="attn_out_bwd_mla" if with_delta else "attn_out_bwd_swa", grid=(T // tm,),
        in_specs=[_rows(tm, D), _rows(tm, D), _rows(tm, D), _full((D, D)), _full((8, D))],
        out_specs=out_specs, out_shape=out_shape,
        compiler_params=_params("arbitrary"),
    )(dxm, y1, o, w_o, vec)


def _mla_attn_bwd(q, k, v, do, lse, delta):
    H, T, _ = q.shape
    tb = _tile(T, ATTN_BLOCK)
    sub = min(ATTN_SUB, tb)
    ns, nb = tb // sub, T // tb

    def body(q_ref, k_ref, v_ref, do_ref, lse_ref, dl_ref, dq_ref, dk_ref, dv_ref, dk_acc, dv_acc):
        j, i = pl.program_id(1), pl.program_id(2)

        @pl.when((j == 0) & (i == 0))
        def _():
            dq_ref[...] = jnp.zeros_like(dq_ref)

        def update(kk, r, masked):
            keys, rows = pl.ds(kk * sub, sub), pl.ds(r * sub, sub)
            kb, qb, dob = k_ref[0, keys, :], q_ref[0, rows, :], do_ref[rows, :]
            st = _dot_nt(kb, qb)
            if masked:
                row = lax.broadcasted_iota(jnp.int32, (sub, sub), 0)
                col = lax.broadcasted_iota(jnp.int32, (sub, sub), 1)
                st = jnp.where(row <= col, st, NEG)
            pt = jnp.exp2(st - lse_ref[0, :, rows])
            dv_acc[keys, :] += _dot(pt.astype(MXU_DTYPE), dob)
            dpt = _dot_nt(v_ref[0, keys, :], dob)
            dst = (pt * (dpt - dl_ref[0, :, rows])).astype(MXU_DTYPE)
            dk_acc[keys, :] += _dot(dst, qb)
            q_rows = pl.ds(pl.multiple_of(i * tb + r * sub, sub), sub)
            dq_ref[0, q_rows, :] += _dot_tn(dst, kb)

        @pl.when(i == j)
        def _():
            dk_acc[...] = jnp.zeros_like(dk_acc)
            dv_acc[...] = jnp.zeros_like(dv_acc)
            for r in range(ns):
                for kk in range(r + 1):
                    update(kk, r, kk == r)

        @pl.when(i > j)
        def _():
            for r in range(ns):
                for kk in range(ns):
                    update(kk, r, False)

        @pl.when(i == nb - 1)
        def _():
            dk_ref[0] = (dk_acc[...] * LN2).astype(dk_ref.dtype)
            dv_ref[0] = dv_acc[...].astype(dv_ref.dtype)

    q_idx = lambda h, j, i: (h, jnp.maximum(i, j), 0)
    kv_idx = lambda h, j, i: (h, j, 0)
    stat_idx = lambda h, j, i: (h, 0, jnp.maximum(i, j))
    return pl.pallas_call(
        body, name="mla_attn_bwd", grid=(H, nb, nb),
        in_specs=[pl.BlockSpec((1, tb, QK_EXT), q_idx), pl.BlockSpec((1, tb, QK_EXT), kv_idx),
                  pl.BlockSpec((1, tb, V_DIM), kv_idx),
                  pl.BlockSpec((tb, V_DIM), lambda h, j, i: (jnp.maximum(i, j), h)),
                  pl.BlockSpec((1, 1, tb), stat_idx), pl.BlockSpec((1, 1, tb), stat_idx)],
        out_specs=[pl.BlockSpec((1, T, QK_EXT), lambda h, j, i: (h, 0, 0)),
                   pl.BlockSpec((1, tb, QK_EXT), kv_idx), pl.BlockSpec((1, tb, V_DIM), kv_idx)],
        out_shape=[jax.ShapeDtypeStruct((H, T, QK_EXT), jnp.float32), jax.ShapeDtypeStruct((H, T, QK_EXT), MXU_DTYPE),
                   jax.ShapeDtypeStruct((H, T, V_DIM), MXU_DTYPE)],
        scratch_shapes=[pltpu.VMEM((tb, QK_EXT), jnp.float32), pltpu.VMEM((tb, V_DIM), jnp.float32)],
        compiler_params=_params("parallel", "arbitrary", "arbitrary"),
    )(q, k, v, do, lse, delta)


def _mla_pre_bwd(x, dxm, vec, hb, z, dq, dk, dv, cs, wcat, g_q, g_kv, wuq, wukv):
    T = x.shape[0]
    tm = _tile(T, 256)
    H = MLA_HEADS
    zc = wcat.shape[1]

    def body(x_ref, dxm_ref, vec_ref, h_ref, z_ref, dq_ref, dk_ref, dv_ref, cs_ref, wcat_ref, gq_ref, gkv_ref,
             wuq_ref, wukv_ref, dx_ref, ps_ref, dgq_ref, dgkv_ref, dwcat_ref, dwuq_ref, dwukv_ref):
        @pl.when(pl.program_id(0) == 0)
        def _():
            for ref in (ps_ref, dgq_ref, dgkv_ref, dwcat_ref, dwuq_ref, dwukv_ref):
                ref[...] = jnp.zeros_like(ref)

        z = z_ref[...]
        cs_t = cs_ref[...]
        cqhat, rq = _rms(z[:, :Q_LORA])
        ckhat, rk = _rms(z[:, Q_LORA:Q_LORA + KV_LORA])
        gq, gkv = gq_ref[...], gkv_ref[...]
        cq = (cqhat * gq).astype(MXU_DTYPE)
        ckv = (ckhat * gkv).astype(MXU_DTYPE)
        dcq = jnp.zeros((tm, Q_LORA), jnp.float32)
        dckv = jnp.zeros((tm, KV_LORA), jnp.float32)
        dkr = jnp.zeros((tm, LANES), jnp.float32)
        for hd in range(H):
            dqh = dq_ref[hd] * MLA_SCALE
            gqh = jnp.concatenate([dqh[:, :QK_NOPE], dqh[:, QK_NOPE:] * cs_t], axis=1).astype(MXU_DTYPE)
            dcq += _dot_nt(gqh, wuq_ref[hd])
            dwuq_ref[hd] += _dot_tn(cq, gqh)
            dkh = dk_ref[hd]
            gkvh = jnp.concatenate([dkh[:, :QK_NOPE], dv_ref[hd]], axis=1)
            dckv += _dot_nt(gkvh, wukv_ref[hd])
            dwukv_ref[hd] += _dot_tn(ckv, gkvh)
            dkr += dkh[:, QK_NOPE:].astype(jnp.float32)
        dgq_ref[...] += _rowsum(dcq * cqhat)
        dgkv_ref[...] += _rowsum(dckv * ckhat)
        dcq_pre = _rms_bwd(dcq * gq, cqhat, rq)
        dckv_pre = _rms_bwd(dckv * gkv, ckhat, rk)
        dkr2 = (dkr + pltpu.roll(dkr, QK_ROPE, axis=1)) * cs_t
        dz = jnp.concatenate([dcq_pre, dckv_pre, dkr2], axis=1).astype(MXU_DTYPE)
        dwcat_ref[...] += _dot_tn(h_ref[...], dz)
        dh = _dot_nt(dz, wcat_ref[...])
        dx_ref[...] = _modulate_bwd(dh, x_ref[...], vec_ref, R_GMIX, R_SC1, R_SH1, ps_ref, dxm_ref[...])

    hblk = lambda w: pl.BlockSpec((H, tm, w), lambda i: (0, i, 0))
    return pl.pallas_call(
        body, name="mla_pre_bwd", grid=(T // tm,),
        in_specs=[_rows(tm, D), _rows(tm, D), _full((8, D)), _rows(tm, D), _rows(tm, zc), hblk(QK_EXT), hblk(QK_EXT),
                  hblk(V_DIM), _rows(tm, LANES), _full(wcat.shape), _full(g_q.shape), _full(g_kv.shape),
                  _full(wuq.shape), _full(wukv.shape)],
        out_specs=[_rows(tm, D), _full((8, D)), _full(g_q.shape), _full(g_kv.shape), _full(wcat.shape),
                   _full(wuq.shape), _full(wukv.shape)],
        out_shape=[jax.ShapeDtypeStruct((T, D), jnp.float32), jax.ShapeDtypeStruct((8, D), jnp.float32),
                   jax.ShapeDtypeStruct(g_q.shape, jnp.float32), jax.ShapeDtypeStruct(g_kv.shape, jnp.float32),
                   jax.ShapeDtypeStruct(wcat.shape, jnp.float32), jax.ShapeDtypeStruct(wuq.shape, jnp.float32),
                   jax.ShapeDtypeStruct(wukv.shape, jnp.float32)],
        compiler_params=_params("arbitrary"),
    )(x, dxm, vec, hb, z, dq, dk, dv, cs, wcat, g_q, g_kv, wuq, wukv)


def _swa_attn_bwd(qt, k, v, dot_, bias, sink_rows):
    T = qt.shape[1]
    W, Dh, G, Hk = WINDOW, SWA_HEAD_DIM, SWA_GROUP, SWA_KV_HEADS
    nk = Hk * Dh

    def body(qt_ref, kp_ref, kc_ref, vp_ref, vc_ref, dot_ref, bias_ref, sink_ref, dqt_ref, dk_ref, dv_ref, dsink_ref):
        n = pl.program_id(0)

        @pl.when(n == 0)
        def _():
            dk_ref[...] = jnp.zeros_like(dk_ref)
            dv_ref[...] = jnp.zeros_like(dv_ref)
            dsink_ref[...] = jnp.zeros_like(dsink_ref)

        dks, dvs = [], []
        for kh in range(Hk):
            qt, kb, pn, p_sink = _swa_probs(n, kh, qt_ref, kp_ref, kc_ref, bias_ref, sink_ref)
            vb = jnp.concatenate([vp_ref[:, kh * Dh:(kh + 1) * Dh], vc_ref[:, kh * Dh:(kh + 1) * Dh]], axis=0)
            dot_h = jnp.concatenate([dot_ref[(kh * G + g) * Dh:(kh * G + g + 1) * Dh, :] for g in range(G)], axis=1)
            dp = _dot(vb, dot_h)
            delta = jnp.sum(pn * dp, axis=0, keepdims=True)
            dsb = (pn * (dp - delta)).astype(MXU_DTYPE)
            dsink_ref[kh] += -p_sink * delta
            dqt = _dot_tn(kb, dsb) * SWA_SCALE
            for g in range(G):
                dqt_ref[(kh * G + g) * Dh:(kh * G + g + 1) * Dh, :] = dqt[:, g * W:(g + 1) * W]
            dks.append(_dot_nt(dsb, qt))
            dvs.append(_dot_nt(pn.astype(MXU_DTYPE), dot_h))
        dkb = jnp.concatenate(dks, axis=1)
        dvb = jnp.concatenate(dvs, axis=1)
        cur_rows = pl.ds(pl.multiple_of(n * W, W), W)
        dk_ref[cur_rows, :] += dkb[W:]
        dv_ref[cur_rows, :] += dvb[W:]

        @pl.when(n > 0)
        def _():
            prev_rows = pl.ds(pl.multiple_of((n - 1) * W, W), W)
            dk_ref[prev_rows, :] += dkb[:W]
            dv_ref[prev_rows, :] += dvb[:W]

    prev = lambda n: (jnp.maximum(n - 1, 0), 0)
    cur = lambda n: (n, 0)
    col = lambda n: (0, n)
    return pl.pallas_call(
        body, name="swa_attn_bwd", grid=(T // W,),
        in_specs=[pl.BlockSpec((D, W), col), pl.BlockSpec((W, nk), prev), pl.BlockSpec((W, nk), cur),
                  pl.BlockSpec((W, nk), prev), pl.BlockSpec((W, nk), cur), pl.BlockSpec((D, W), col),
                  _full(bias.shape), _full(sink_rows.shape)],
        out_specs=[pl.BlockSpec((D, W), col), _full((T, nk)), _full((T, nk)), _full(sink_rows.shape)],
        out_shape=[jax.ShapeDtypeStruct((D, T), jnp.float32), jax.ShapeDtypeStruct((T, nk), jnp.float32),
                   jax.ShapeDtypeStruct((T, nk), jnp.float32), jax.ShapeDtypeStruct(sink_rows.shape, jnp.float32)],
        compiler_params=_params("arbitrary"),
    )(qt, k, k, v, v, dot_, bias, sink_rows)


def _swa_pre_bwd(x, dxm, vec, dq, dk, dv, w_qkv):
    T = x.shape[0]
    tm = _tile(T, 512)
    nq = SWA_HEADS * SWA_HEAD_DIM
    nk = SWA_KV_HEADS * SWA_HEAD_DIM
    nqkv = nq + 2 * nk

    def body(x_ref, dxm_ref, vec_ref, dq_ref, dk_ref, dv_ref, w_ref, dx_ref, dqkv_ref, ps_ref, db_ref):
        @pl.when(pl.program_id(0) == 0)
        def _():
            ps_ref[...] = jnp.zeros_like(ps_ref)
            db_ref[...] = jnp.zeros_like(db_ref)

        dqkv = jnp.concatenate([dq_ref[...], dk_ref[...], dv_ref[...]], axis=1)
        db_ref[...] += _rowsum(dqkv)
        dqkv_b = dqkv.astype(MXU_DTYPE)
        dqkv_ref[...] = dqkv_b
        dh = _dot_nt(dqkv_b, w_ref[...])
        dx_ref[...] = _modulate_bwd(dh, x_ref[...], vec_ref, R_GMIX, R_SC1, R_SH1, ps_ref, dxm_ref[...])

    return pl.pallas_call(
        body, name="swa_pre_bwd", grid=(T // tm,),
        in_specs=[_rows(tm, D), _rows(tm, D), _full((8, D)), _rows(tm, nq), _rows(tm, nk), _rows(tm, nk),
                  _full(w_qkv.shape)],
        out_specs=[_rows(tm, D), _rows(tm, nqkv), _full((8, D)), _full((1, nqkv))],
        out_shape=[jax.ShapeDtypeStruct((T, D), jnp.float32), jax.ShapeDtypeStruct((T, nqkv), MXU_DTYPE),
                   jax.ShapeDtypeStruct((8, D), jnp.float32), jax.ShapeDtypeStruct((1, nqkv), jnp.float32)],
        compiler_params=_params("arbitrary"),
    )(x, dxm, vec, dq, dk, dv, w_qkv)


def _rot_cols(w):
    half = QK_ROPE // 2
    return jnp.concatenate([-w[..., half:], w[..., :half]], axis=-1)


def _unrot_grad(d_rope, d_rot):
    half = QK_ROPE // 2
    return d_rope + jnp.concatenate([d_rot[..., half:], -d_rot[..., :half]], axis=-1)


def _rope_table(positions):
    half = QK_ROPE // 2
    inv_freq = ROPE_THETA ** (-jnp.arange(half, dtype=jnp.float32) / half)
    ang = positions.astype(jnp.float32)[:, None] * inv_freq
    cos, sin = jnp.cos(ang), jnp.sin(ang)
    return jnp.concatenate([cos, cos, sin, sin], axis=1)


def _sequence_step(x, tgt, positions, vecs, g_q, g_kv, sinks, g_final, wts, late_weights, on_late_grads):
    H = MLA_HEADS
    cs = _rope_table(positions)
    w_dkv = wts["mla_w_dkv"]
    wcat = jnp.concatenate([wts["mla_w_dq"], w_dkv, _rot_cols(w_dkv[:, KV_LORA:])], axis=1)
    uq = wts["mla_w_uq"].reshape(Q_LORA, H, QK_NOPE + QK_ROPE)
    wuq = jnp.concatenate([uq, _rot_cols(uq[..., QK_NOPE:])], axis=-1).transpose(1, 0, 2)
    wukv = wts["mla_w_ukv"].reshape(KV_LORA, H, QK_NOPE + V_DIM).transpose(1, 0, 2)
    zero_bias = jnp.zeros((1, D), jnp.float32)
    bias = _swa_bias()
    sink_rows = jnp.broadcast_to(sinks.reshape(SWA_KV_HEADS, 1, SWA_GROUP, 1),
                                 (SWA_KV_HEADS, 1, SWA_GROUP, WINDOW)).reshape(SWA_KV_HEADS, 1, SWA_GROUP * WINDOW)

    h1a, z, q, k, v = _mla_pre(x, vecs[0], wcat, g_q, g_kv, wuq, wukv, cs)
    o_a, lse = _mla_attn_fwd(q, k, v)
    y1a, xm_a, h2a = _post_attn(o_a, x, wts["mla_w_o"], zero_bias, vecs[0])
    wts = {**wts, **late_weights(h2a)}
    a_a, y2a, x1 = _mlp_fwd(h2a, wts["w_ff1"][0], wts["w_ff2"][0], xm_a, vecs[0])

    h1b, qs, ks, vs = _swa_pre(x1, vecs[1], wts["swa_w_qkv"], wts["swa_b_qkv"])
    qs_t = qs.T
    o_b = _swa_attn_fwd(qs_t, ks, vs, bias, sink_rows).T
    y1b, xm_b, h2b = _post_attn(o_b, x1, wts["swa_w_o"], wts["swa_b_o"], vecs[1])
    a_b, y2b, x2 = _mlp_fwd(h2b, wts["w_ff1"][1], wts["w_ff2"][1], xm_b, vecs[1])

    loss8, dx2, dg_final = _final_loss(x2, tgt, g_final.reshape(1, D))

    du_b, dy2b, dxm_b, ps_mlp_b = _mlp_bwd(dx2, y2b, a_b, wts["w_ff1"][1], wts["w_ff2"][1], xm_b, vecs[1])
    g_ff2 = _mm_tn(a_b, dy2b, "dw_ff2_l1", "rows", DEPTH, 1)
    g_ff1 = _mm_tn(h2b, du_b, "dw_ff1_l1", "cols", DEPTH, 1)
    dy1b, do_b, ps_out_b = _attn_out_bwd(dxm_b, y1b, o_b, wts["swa_w_o"], vecs[1], False)
    g_swa_o = _mm_tn(o_b, dy1b, "dw_o_swa")
    dqs_t, dks, dvs, dsinks = _swa_attn_bwd(qs_t, ks, vs, do_b.T, bias, sink_rows)
    dqs = dqs_t.T
    dx1, dqkv, ps_pre_b, g_swa_bqkv = _swa_pre_bwd(x1, dxm_b, vecs[1], dqs, dks, dvs, wts["swa_w_qkv"])
    g_swa_qkv = _mm_tn(h1b, dqkv, "dw_qkv", "cols")

    du_a, dy2a, dxm_a, ps_mlp_a = _mlp_bwd(dx1, y2a, a_a, wts["w_ff1"][0], wts["w_ff2"][0], xm_a, vecs[0])
    g_ff2 = _mm_tn(a_a, dy2a, "dw_ff2_l0", "rows", DEPTH, 0, g_ff2)
    g_ff1 = _mm_tn(h2a, du_a, "dw_ff1_l0", "cols", DEPTH, 0, g_ff1)
    rows4 = lambda g: g.reshape(N_CHIPS, g.shape[0] // N_CHIPS, g.shape[1])
    token = on_late_grads({
        "swa_w_qkv": g_swa_qkv.reshape(N_CHIPS, D, -1), "swa_w_o": rows4(g_swa_o),
        "w_ff1": g_ff1.reshape(N_CHIPS, DEPTH * D, -1), "w_ff2": g_ff2.reshape(N_CHIPS, -1, D)})
    dy1a, do_a, ps_out_a, delta = _attn_out_bwd(dxm_a, y1a, o_a, wts["mla_w_o"], vecs[0] + token[0, 0], True)
    g_mla_o = _mm_tn(o_a, dy1a, "dw_o_mla")
    dq, dk, dv = _mla_attn_bwd(q, k, v, do_a, lse, delta)
    dx0, ps_pre_a, dg_q, dg_kv, dwcat, dwuq, dwukv = _mla_pre_bwd(
        x, dxm_a, vecs[0], h1a, z, dq, dk, dv, cs, wcat, g_q, g_kv, wuq, wukv)

    c0, c1, c2 = Q_LORA, Q_LORA + KV_LORA, Q_LORA + KV_LORA + QK_ROPE
    g_dq = dwcat[:, :c0]
    g_dkv = jnp.concatenate([dwcat[:, c0:c1], _unrot_grad(dwcat[:, c1:c2], dwcat[:, c2:])], axis=1)
    e0 = QK_NOPE + QK_ROPE
    g_uq = jnp.concatenate([dwuq[..., :QK_NOPE], _unrot_grad(dwuq[..., QK_NOPE:e0], dwuq[..., e0:])], axis=-1)
    per = H // N_CHIPS
    g_uq = g_uq.reshape(N_CHIPS, per, Q_LORA, e0).transpose(0, 2, 1, 3).reshape(N_CHIPS, Q_LORA, per * e0)
    g_ukv = dwukv.reshape(N_CHIPS, per, KV_LORA, QK_NOPE + V_DIM).transpose(0, 2, 1, 3)
    g_ukv = g_ukv.reshape(N_CHIPS, KV_LORA, per * (QK_NOPE + V_DIM))

    def dmod(ps_pre, ps_out, ps_mlp):
        return jnp.concatenate([ps_pre[R_SH1:R_SC1 + 1], ps_out[R_GT1:R_GT1 + 1], ps_mlp[R_SH2:R_GT2 + 1]], axis=0)

    grads = {"mla_w_dq": rows4(g_dq), "mla_w_uq": g_uq, "mla_w_dkv": rows4(g_dkv), "mla_w_ukv": g_ukv,
             "mla_w_o": rows4(g_mla_o)}
    small = {
        "dmod": jnp.stack([dmod(ps_pre_a, ps_out_a, ps_mlp_a), dmod(ps_pre_b, ps_out_b, ps_mlp_b)]).reshape(DEPTH, 6 * D),
        "g_mix": jnp.stack([ps_pre_a[R_GMIX], ps_pre_b[R_GMIX]]),
        "g_mlp": jnp.stack([ps_mlp_a[R_GMLP], ps_mlp_b[R_GMLP]]),
        "mla_g_q": dg_q, "mla_g_kv": dg_kv, "swa_sinks": jnp.sum(dsinks.reshape(SWA_HEADS, WINDOW), axis=1).reshape(1, SWA_HEADS),
        "swa_b_qkv": g_swa_bqkv, "swa_b_o": ps_out_b[R_BO:R_BO + 1],
        "g_final": dg_final.reshape(D), "loss": loss8[0, 0],
    }
    return dx0, grads, small


SHARDED = {
    "mla_w_dq": (1, D // N_CHIPS, Q_LORA),
    "mla_w_uq": (1, Q_LORA, MLA_HEADS * (QK_NOPE + QK_ROPE) // N_CHIPS),
    "mla_w_dkv": (1, D // N_CHIPS, KV_LORA + QK_ROPE),
    "mla_w_ukv": (1, KV_LORA, MLA_HEADS * (QK_NOPE + V_DIM) // N_CHIPS),
    "mla_w_o": (1, MLA_HEADS * V_DIM // N_CHIPS, D),
    "swa_w_qkv": (1, D, (SWA_HEADS + 2 * SWA_KV_HEADS) * SWA_HEAD_DIM // N_CHIPS),
    "swa_w_o": (1, SWA_HEADS * SWA_HEAD_DIM // N_CHIPS, D),
    "w_ff1": (DEPTH, D, D_FF // N_CHIPS),
    "w_ff2": (DEPTH, D_FF // N_CHIPS, D),
}
COL_SPLIT = ("mla_w_uq", "mla_w_ukv", "swa_w_qkv")
BIASES = {"swa_b_qkv": (SWA_HEADS + 2 * SWA_KV_HEADS) * SWA_HEAD_DIM, "swa_b_o": D}


def _view2d(name):
    shape = SHARDED[name]
    return math.prod(shape[:-1]), shape[-1]


SMALL = {"b_ada": (DEPTH, 6 * D), "g_mix": (DEPTH, D), "g_mlp": (DEPTH, D), "mla_g_q": (1, Q_LORA),
         "mla_g_kv": (1, KV_LORA), "swa_sinks": (1, SWA_HEADS), "g_final": (D,), "loss": (),
         "swa_b_qkv": (1, BIASES["swa_b_qkv"]), "swa_b_o": (1, BIASES["swa_b_o"])}
SMALL_ROWS = 168
DMA_ROWS = 256


def _small_slots():
    slots, off = {}, 0
    for name, shape in SMALL.items():
        n = max(math.prod(shape), 1)
        slots[name] = (off, n)
        off += -(-n // LANES) * LANES
    assert off <= SMALL_ROWS * LANES
    return slots


def _pack_small(vals):
    parts, end = [], 0
    for name, (off, n) in _small_slots().items():
        pad = -(-n // LANES) * LANES - n
        v = vals[name].astype(jnp.float32).reshape(-1) if name in vals else jnp.zeros((n,), jnp.float32)
        parts += [v, jnp.zeros((pad,), jnp.float32)]
        end = off + n + pad
    parts.append(jnp.zeros((SMALL_ROWS * LANES - end,), jnp.float32))
    return jnp.concatenate(parts).reshape(SMALL_ROWS, LANES)


def _unpack_small(buf):
    flat = buf.reshape(-1)
    return {name: flat[off:off + n].reshape(SMALL[name]) for name, (off, n) in _small_slots().items()}


def _pieces(rows):
    return [(off, min(DMA_ROWS, rows - off)) for off in range(0, rows, DMA_ROWS)]


HBM = pl.BlockSpec(memory_space=pltpu.HBM)
MESH = pl.DeviceIdType.MESH


def _place():
    x, y, c = lax.axis_index("x"), lax.axis_index("y"), lax.axis_index("c")
    chips = [(1 - x, y), (x, 1 - y), (1 - x, 1 - y)]
    return x, y, c, chips


def _all_gather(block):
    m_per, n = block.shape

    def body(x_ref, out_ref, send_sems, recv_sems, local_sem):
        x, y, c, chips = _place()
        me, sibling = (x, y, c), (x, y, 1 - c)

        def rows(px, py, pc):
            return out_ref.at[pl.ds((4 * px + 2 * py + pc) * m_per, m_per), :]

        def copy(k, blk, to, src=None):
            return pltpu.make_async_remote_copy(
                src_ref=rows(*blk) if src is None else src, dst_ref=rows(*blk),
                send_sem=send_sems.at[k], recv_sem=recv_sems.at[k], device_id=to, device_id_type=MESH)

        mine = pltpu.make_async_copy(x_ref, rows(*me), local_sem)
        mine.start()
        first = [copy(0, me, sibling, src=x_ref)]
        first += [copy(1 + j, me, (*chip, c), src=x_ref) for j, chip in enumerate(chips)]
        for cp in first:
            cp.start()
        passed = [copy(4 + j, (*chip, c), sibling) for j, chip in enumerate(chips)]
        for j, chip in enumerate(chips):
            copy(1 + j, (*chip, c), me).wait_recv()
            passed[j].start()
        copy(0, sibling, me).wait_recv()
        for j, chip in enumerate(chips):
            copy(4 + j, (*chip, 1 - c), me).wait_recv()
        for cp in first + passed:
            cp.wait_send()
        mine.wait()

    out = pl.pallas_call(
        body, name="all_gather_small",
        out_shape=jax.ShapeDtypeStruct((N_DEV * m_per, n), block.dtype),
        in_specs=[pl.BlockSpec(memory_space=pltpu.VMEM)],
        out_specs=pl.BlockSpec(memory_space=pltpu.VMEM),
        scratch_shapes=[pltpu.SemaphoreType.DMA((7,)), pltpu.SemaphoreType.DMA((7,)), pltpu.SemaphoreType.DMA],
    )(block)
    return out.reshape(N_DEV, m_per, n)


def _weight_gather(shards):
    nt = len(shards)

    def body(*refs):
        w_refs, out_refs = refs[:nt], refs[nt:2 * nt]
        send_sems, recv_sems = refs[2 * nt:]
        x, y, c, chips = _place()
        sibling = (x, y, 1 - c)

        def slab(t, px, py, half):
            rh = shards[t].shape[0] // 2
            return out_refs[t].at[2 * px + py, pl.ds(half * rh, rh), :]

        def copy(t, k, src, dst, to):
            return pltpu.make_async_remote_copy(src_ref=src, dst_ref=dst, send_sem=send_sems.at[6 * t + k],
                                                recv_sem=recv_sems.at[6 * t + k], device_id=to, device_id_type=MESH)

        first = []
        for t in range(nt):
            rh = shards[t].shape[0] // 2
            first += [copy(t, j, w_refs[t].at[pl.ds(c * rh, rh), :], slab(t, x, y, c), (*chip, c))
                      for j, chip in enumerate(chips)]
        for cp in first:
            cp.start()
        passed = []
        for t in range(nt):
            for j, chip in enumerate(chips):
                copy(t, j, slab(t, *chip, c), slab(t, *chip, c), (*chip, c)).wait_recv()
                rh = shards[t].shape[0] // 2
                for off, n in _pieces(rh):
                    piece = out_refs[t].at[2 * chip[0] + chip[1], pl.ds(c * rh + off, n), :]
                    copy(t, 3 + j, piece, piece, sibling).start()
                passed.append(copy(t, 3 + j, slab(t, *chip, c), slab(t, *chip, c), sibling))
        for t in range(nt):
            for j, chip in enumerate(chips):
                copy(t, 3 + j, slab(t, *chip, 1 - c), slab(t, *chip, 1 - c), sibling).wait_recv()
        for cp in first + passed:
            cp.wait_send()

    return pl.pallas_call(
        body, name="weight_gather",
        out_shape=[jax.ShapeDtypeStruct((N_CHIPS,) + s.shape, s.dtype) for s in shards],
        in_specs=[HBM] * nt, out_specs=[HBM] * nt,
        scratch_shapes=[pltpu.SemaphoreType.DMA((6 * nt,)), pltpu.SemaphoreType.DMA((6 * nt,))],
    )(*shards)


SEM = pl.BlockSpec(memory_space=pltpu.SEMAPHORE)
ANY = pl.BlockSpec(memory_space=pl.ANY)
SPLIT_COPY = pltpu.SideEffectType.DATAFLOW_SIDE_EFFECTING


def _late_copies(w_refs, land_refs, send_sems, recv_sems):
    x, y, c, chips = _place()
    return [pltpu.make_async_remote_copy(
        src_ref=w_refs[t], dst_ref=land_refs[t].at[2 * x + y], send_sem=send_sems.at[3 * t + j],
        recv_sem=recv_sems.at[3 * t + j], device_id=(cx, cy, c), device_id_type=MESH)
        for t in range(len(w_refs)) for j, (cx, cy) in enumerate(chips)], chips


def _late_gather_start(shards, after):
    nt, na = len(shards), len(after)

    def body(*refs):
        w_refs, land_refs = refs[:nt], refs[nt:2 * nt]
        send_sems, recv_sems, token = refs[2 * nt + na], refs[2 * nt + na + 1], refs[-1]
        copies, _ = _late_copies(w_refs, land_refs, send_sems, recv_sems)
        for cp in copies:
            cp.start()
        token[...] = jnp.zeros_like(token)

    hbm = lambda a: pltpu.with_memory_space_constraint(a, pltpu.HBM)
    lands = [lax.empty((N_CHIPS,) + s.shape, s.dtype) for s in shards]
    outs = pl.pallas_call(
        body, name="late_gather_start",
        out_shape=(pltpu.SemaphoreType.DMA((3 * nt,)), pltpu.SemaphoreType.DMA((3 * nt,)),
                   *[pltpu.HBM(s.shape, s.dtype) for s in shards], *[pltpu.HBM(l.shape, l.dtype) for l in lands],
                   jax.ShapeDtypeStruct((8, LANES), jnp.float32)),
        in_specs=[HBM] * (2 * nt) + [ANY] * na,
        out_specs=(SEM, SEM, *([HBM] * (2 * nt)), pl.BlockSpec(memory_space=pltpu.VMEM)),
        input_output_aliases={i: 2 + i for i in range(2 * nt)},
        compiler_params=pltpu.CompilerParams(has_side_effects=SPLIT_COPY),
    )(*[hbm(s) for s in shards], *[hbm(l) for l in lands], *after)
    return outs[0], outs[1], list(outs[2:2 + nt]), list(outs[2 + nt:2 + 2 * nt]), outs[-1]


def _late_gather_wait(send_sems, recv_sems, shards, lands, after):
    nt = len(shards)

    def body(*refs):
        w_refs, land_refs = refs[:nt], refs[nt:2 * nt]
        s_sems, r_sems = refs[2 * nt], refs[2 * nt + 1]
        x, y, c, chips = _place()
        for t in range(nt):
            for j, (cx, cy) in enumerate(chips):
                cp = pltpu.make_async_remote_copy(
                    src_ref=w_refs[t], dst_ref=land_refs[t].at[2 * cx + cy], send_sem=s_sems.at[3 * t + j],
                    recv_sem=r_sems.at[3 * t + j], device_id=(cx, cy, c), device_id_type=MESH)
                cp.wait_send()
                cp.wait_recv()

    outs = pl.pallas_call(
        body, name="late_gather_wait",
        out_shape=(*[pltpu.HBM(s.shape, s.dtype) for s in shards], *[pltpu.HBM(l.shape, l.dtype) for l in lands]),
        in_specs=[HBM] * (2 * nt) + [SEM, SEM, ANY], out_specs=tuple([HBM] * (2 * nt)),
        input_output_aliases={i: i for i in range(2 * nt)},
        compiler_params=pltpu.CompilerParams(has_side_effects=SPLIT_COPY),
    )(*shards, *lands, send_sems, recv_sems, after)
    return list(outs[nt:])


def _grad_pair_in(grads):
    nt = len(grads)

    def body(*refs):
        g_refs, got_refs = refs[:nt], refs[nt:2 * nt]
        send_sems, recv_sems = refs[2 * nt:]
        x, y, c, _ = _place()
        sibling = (x, y, 1 - c)

        def copy(t, src, dst):
            return pltpu.make_async_remote_copy(src_ref=src, dst_ref=dst, send_sem=send_sems.at[t],
                                                recv_sem=recv_sems.at[t], device_id=sibling, device_id_type=MESH)

        for t in range(nt):
            rh = grads[t].shape[1] // 2
            for p in range(N_CHIPS):
                for off, n in _pieces(rh):
                    copy(t, g_refs[t].at[p, pl.ds((1 - c) * rh + off, n), :], got_refs[t].at[p, pl.ds(off, n), :]).start()
        for t in range(nt):
            rh = grads[t].shape[1] // 2
            copy(t, g_refs[t].at[:, pl.ds((1 - c) * rh, rh), :], got_refs[t]).wait()

    return pl.pallas_call(
        body, name="grad_pair_in",
        out_shape=[jax.ShapeDtypeStruct((N_CHIPS, g.shape[1] // 2, g.shape[2]), g.dtype) for g in grads],
        in_specs=[HBM] * nt, out_specs=[HBM] * nt,
        scratch_shapes=[pltpu.SemaphoreType.DMA((nt,)), pltpu.SemaphoreType.DMA((nt,))],
    )(*grads)


def _pair_sum(g, got, core, name):
    _, rows, cols = g.shape
    rh = rows // 2
    tr = _tile(rh, 512)
    nb = rh // tr

    def body(c_ref, g_ref, got_ref, s32_ref, s16_ref):
        s = g_ref[...] + got_ref[...]
        s32_ref[...] = s
        s16_ref[...] = s.astype(s16_ref.dtype)

    blk = pl.BlockSpec((None, tr, cols), lambda p, i, c_ref: (p, i, 0))
    return pl.pallas_call(
        body, name=name,
        grid_spec=pltpu.PrefetchScalarGridSpec(
            num_scalar_prefetch=1, grid=(N_CHIPS, nb),
            in_specs=[pl.BlockSpec((None, tr, cols), lambda p, i, c_ref: (p, c_ref[0] * nb + i, 0)), blk],
            out_specs=[blk, blk]),
        out_shape=[jax.ShapeDtypeStruct((N_CHIPS, rh, cols), jnp.float32),
                   jax.ShapeDtypeStruct((N_CHIPS, rh, cols), jnp.bfloat16)],
        compiler_params=_params("parallel", "parallel"),
    )(core, g, got)


def _grad_chip_exchange(parts):
    nt = len(parts)

    def body(*refs):
        a_refs, got_refs = refs[:nt], refs[nt:2 * nt]
        send_sems, recv_sems = refs[2 * nt:]
        x, y, c, chips = _place()
        sends = [pltpu.make_async_remote_copy(
            src_ref=a_refs[t].at[2 * cx + cy], dst_ref=got_refs[t].at[j], send_sem=send_sems.at[3 * t + j],
            recv_sem=recv_sems.at[3 * t + j], device_id=(cx, cy, c), device_id_type=MESH)
            for t in range(nt) for j, (cx, cy) in enumerate(chips)]
        for cp in sends:
            cp.start()
        for cp in sends:
            cp.wait_recv()
        for cp in sends:
            cp.wait_send()

    return pl.pallas_call(
        body, name="grad_chip_exchange",
        out_shape=[jax.ShapeDtypeStruct((N_CHIPS - 1,) + a.shape[1:], a.dtype) for a in parts],
        in_specs=[HBM] * nt, out_specs=[HBM] * nt,
        scratch_shapes=[pltpu.SemaphoreType.DMA((3 * nt,)), pltpu.SemaphoreType.DMA((3 * nt,))],
    )(*parts)


def _exchange_start(parts):
    nt = len(parts)

    def body(*refs):
        a_refs, land_refs = refs[:nt], refs[nt:2 * nt]
        send_sems, recv_sems, token = refs[2 * nt], refs[2 * nt + 1], refs[-1]
        x, y, c, chips = _place()
        for t in range(nt):
            for j, (cx, cy) in enumerate(chips):
                pltpu.make_async_remote_copy(
                    src_ref=a_refs[t].at[2 * cx + cy], dst_ref=land_refs[t].at[j], send_sem=send_sems.at[3 * t + j],
                    recv_sem=recv_sems.at[3 * t + j], device_id=(cx, cy, c), device_id_type=MESH).start()
        token[...] = jnp.zeros_like(token)

    hbm = lambda a: pltpu.with_memory_space_constraint(a, pltpu.HBM)
    lands = [lax.empty((N_CHIPS - 1,) + a.shape[1:], a.dtype) for a in parts]
    outs = pl.pallas_call(
        body, name="grad_exchange_start",
        out_shape=(pltpu.SemaphoreType.DMA((3 * nt,)), pltpu.SemaphoreType.DMA((3 * nt,)),
                   *[pltpu.HBM(a.shape, a.dtype) for a in parts], *[pltpu.HBM(l.shape, l.dtype) for l in lands],
                   jax.ShapeDtypeStruct((8, LANES), jnp.float32)),
        in_specs=[HBM] * (2 * nt),
        out_specs=(SEM, SEM, *([HBM] * (2 * nt)), pl.BlockSpec(memory_space=pltpu.VMEM)),
        input_output_aliases={i: 2 + i for i in range(2 * nt)},
        compiler_params=pltpu.CompilerParams(has_side_effects=SPLIT_COPY),
    )(*[hbm(a) for a in parts], *[hbm(l) for l in lands])
    return outs[0], outs[1], list(outs[2:2 + nt]), list(outs[2 + nt:2 + 2 * nt]), outs[-1]


def _exchange_wait(send_sems, recv_sems, parts, lands, after):
    nt = len(parts)

    def body(*refs):
        a_refs, land_refs = refs[:nt], refs[nt:2 * nt]
        s_sems, r_sems = refs[2 * nt], refs[2 * nt + 1]
        x, y, c, chips = _place()
        for t in range(nt):
            for j, (cx, cy) in enumerate(chips):
                cp = pltpu.make_async_remote_copy(
                    src_ref=a_refs[t].at[2 * cx + cy], dst_ref=land_refs[t].at[j], send_sem=s_sems.at[3 * t + j],
                    recv_sem=r_sems.at[3 * t + j], device_id=(cx, cy, c), device_id_type=MESH)
                cp.wait_send()
                cp.wait_recv()

    outs = pl.pallas_call(
        body, name="grad_exchange_wait",
        out_shape=(*[pltpu.HBM(a.shape, a.dtype) for a in parts], *[pltpu.HBM(l.shape, l.dtype) for l in lands]),
        in_specs=[HBM] * (2 * nt) + [SEM, SEM, ANY], out_specs=tuple([HBM] * (2 * nt)),
        input_output_aliases={i: i for i in range(2 * nt)},
        compiler_params=pltpu.CompilerParams(has_side_effects=SPLIT_COPY),
    )(*parts, *lands, send_sems, recv_sems, after)
    return list(outs[nt:])


def _chip_sum(s32, got, chip, name):
    _, rh, cols = s32.shape
    tr = _tile(rh, 512)

    def body(p_ref, s_ref, got_ref, o_ref):
        acc = s_ref[...]
        for j in range(N_CHIPS - 1):
            acc = acc + got_ref[j].astype(jnp.float32)
        o_ref[...] = acc

    return pl.pallas_call(
        body, name=name,
        grid_spec=pltpu.PrefetchScalarGridSpec(
            num_scalar_prefetch=1, grid=(rh // tr,),
            in_specs=[pl.BlockSpec((None, tr, cols), lambda i, p_ref: (p_ref[0], i, 0)),
                      pl.BlockSpec((N_CHIPS - 1, tr, cols), lambda i, p_ref: (0, i, 0))],
            out_specs=pl.BlockSpec((tr, cols), lambda i, p_ref: (i, 0))),
        out_shape=jax.ShapeDtypeStruct((rh, cols), jnp.float32),
        compiler_params=_params("parallel"),
    )(chip, s32, got)


def _grad_pair_out(halves):
    nt = len(halves)

    def body(*refs):
        h_refs, got_refs = refs[:nt], refs[nt:2 * nt]
        send_sems, recv_sems = refs[2 * nt:]
        x, y, c, _ = _place()
        sibling = (x, y, 1 - c)

        def copy(t, src, dst):
            return pltpu.make_async_remote_copy(src_ref=src, dst_ref=dst, send_sem=send_sems.at[t],
                                                recv_sem=recv_sems.at[t], device_id=sibling, device_id_type=MESH)

        for t in range(nt):
            for off, n in _pieces(halves[t].shape[0]):
                copy(t, h_refs[t].at[pl.ds(off, n), :], got_refs[t].at[pl.ds(off, n), :]).start()
        for t in range(nt):
            copy(t, h_refs[t], got_refs[t]).wait()

    return pl.pallas_call(
        body, name="grad_pair_out",
        out_shape=[jax.ShapeDtypeStruct(h.shape, h.dtype) for h in halves],
        in_specs=[HBM] * nt, out_specs=[HBM] * nt,
        scratch_shapes=[pltpu.SemaphoreType.DMA((nt,)), pltpu.SemaphoreType.DMA((nt,))],
    )(*halves)


def _ada_part(c_all, w_ada):
    L, _, ncol = w_ada.shape
    tn = _tile(ncol, 512)

    def body(c_ref, w_ref, cond_ref, part_ref):
        cv = c_ref[...]
        cond = cv * jax.nn.sigmoid(cv)
        cond_ref[...] = cond
        part_ref[0] = jnp.dot(cond, w_ref[0], precision=lax.Precision.HIGHEST, preferred_element_type=jnp.float32)

    return pl.pallas_call(
        body, name="ada_part", grid=(L, ncol // tn),
        in_specs=[_full((N_DEV, D)), pl.BlockSpec((1, D, tn), lambda l, j: (l, 0, j))],
        out_specs=[_full((N_DEV, D)), pl.BlockSpec((1, N_DEV, tn), lambda l, j: (l, 0, j))],
        out_shape=[jax.ShapeDtypeStruct((N_DEV, D), jnp.float32), jax.ShapeDtypeStruct((L, N_DEV, ncol), jnp.float32)],
        compiler_params=_params("arbitrary", "arbitrary"),
    )(c_all, w_ada)


def _adamw_math(w, g, m, v):
    m = ADAM_B1 * m + (1.0 - ADAM_B1) * g
    v = ADAM_B2 * v + (1.0 - ADAM_B2) * jnp.square(g)
    m_hat = m / (1.0 - ADAM_B1 ** ADAM_STEP)
    v_hat = v / (1.0 - ADAM_B2 ** ADAM_STEP)
    delta = -ADAM_LR * (m_hat / (jnp.sqrt(v_hat) + ADAM_EPS) + ADAM_WD * w)
    return delta, m, v


def _adamw(w, g, m, v, name):
    shape = w.shape
    cols = shape[-1]
    rows = math.prod(shape[:-1])
    tr = _tile(rows, 512)
    two_d = lambda t: t.reshape(rows, cols)

    def body(w_ref, g_ref, m_ref, v_ref, d_ref, mo_ref, vo_ref):
        d_ref[...], mo_ref[...], vo_ref[...] = _adamw_math(w_ref[...], g_ref[...], m_ref[...], v_ref[...])

    out = jax.ShapeDtypeStruct((rows, cols), jnp.float32)
    outs = pl.pallas_call(
        body, name=name, grid=(rows // tr,), in_specs=[_rows(tr, cols)] * 4, out_specs=[_rows(tr, cols)] * 3,
        out_shape=[out, out, out], compiler_params=_params("parallel"),
    )(two_d(w), two_d(g), two_d(m), two_d(v))
    return [t.reshape(shape) for t in outs]


def _adamw_halves(w, mine, got, m, v, core, name):
    shape = w.shape
    cols = shape[-1]
    rows = math.prod(shape[:-1])
    rh = rows // 2
    tr = _tile(rh, 512)
    nbh = rh // tr
    two_d = lambda t: t.reshape(rows, cols)

    def body(c_ref, w_ref, a_ref, b_ref, m_ref, v_ref, g_ref, d_ref, mo_ref, vo_ref):
        g = jnp.where(pl.program_id(0) // nbh == c_ref[0], a_ref[...], b_ref[...])
        g_ref[...] = g
        d_ref[...], mo_ref[...], vo_ref[...] = _adamw_math(w_ref[...], g, m_ref[...], v_ref[...])

    row = pl.BlockSpec((tr, cols), lambda i, c_ref: (i, 0))

    def half(keep):
        return pl.BlockSpec((tr, cols), lambda i, c_ref: (jnp.where((i // nbh == c_ref[0]) == keep, i % nbh, 0), 0))

    out = jax.ShapeDtypeStruct((rows, cols), jnp.float32)
    outs = pl.pallas_call(
        body, name=name,
        grid_spec=pltpu.PrefetchScalarGridSpec(
            num_scalar_prefetch=1, grid=(rows // tr,),
            in_specs=[row, half(True), half(False), row, row], out_specs=[row] * 4),
        out_shape=[out] * 4, compiler_params=_params("arbitrary"),
    )(core, two_d(w), mine, got, two_d(m), two_d(v))
    return [t.reshape(shape) for t in outs]


def _ada_grad_adamw(cond_t, dm, w, m, v):
    L, _, ncol = w.shape
    tn = _tile(ncol, 512)

    def body(ct_ref, dm_ref, w_ref, m_ref, v_ref, g_ref, d_ref, mo_ref, vo_ref):
        g = ct_ref[:, 0:1] * dm_ref[0, 0:1, :]
        for b in range(1, N_DEV):
            g = g + ct_ref[:, b:b + 1] * dm_ref[0, b:b + 1, :]
        g_ref[0] = g
        d_ref[0], mo_ref[0], vo_ref[0] = _adamw_math(w_ref[0], g, m_ref[0], v_ref[0])

    wblk = pl.BlockSpec((1, D, tn), lambda l, j: (l, 0, j))
    out = jax.ShapeDtypeStruct(w.shape, jnp.float32)
    return pl.pallas_call(
        body, name="ada_grad_adamw", grid=(L, ncol // tn),
        in_specs=[_full((D, N_DEV)), pl.BlockSpec((1, N_DEV, tn), lambda l, j: (l, 0, j)), wblk, wblk, wblk],
        out_specs=[wblk] * 4, out_shape=[out] * 4, compiler_params=_params("parallel", "parallel"),
    )(cond_t, dm, w, m, v)


def _small_adamw(gathered, w, m, v):
    def body(ga_ref, w_ref, m_ref, v_ref, g_ref, d_ref, mo_ref, vo_ref):
        g = ga_ref[0]
        for dev in range(1, N_DEV):
            g = g + ga_ref[dev]
        g_ref[...] = g
        d_ref[...], mo_ref[...], vo_ref[...] = _adamw_math(w_ref[...], g, m_ref[...], v_ref[...])

    out = jax.ShapeDtypeStruct((SMALL_ROWS, LANES), jnp.float32)
    return pl.pallas_call(
        body, name="small_adamw", out_shape=[out] * 4,
        in_specs=[pl.BlockSpec(memory_space=pltpu.VMEM)] * 4, out_specs=[pl.BlockSpec(memory_space=pltpu.VMEM)] * 4,
    )(gathered, w, m, v)


def _one_hot_pick(arr, index, axis):
    n = arr.shape[axis]
    shape = [1] * arr.ndim
    shape[axis] = n
    hot = (jnp.arange(n) == index).astype(arr.dtype).reshape(shape)
    return jnp.sum(arr * hot, axis=axis)


def kernel(x, c, positions, w_ada, b_ada, g_mix, g_mlp, mla_w_dq, mla_g_q, mla_w_uq, mla_w_dkv, mla_g_kv, mla_w_ukv, mla_w_o, swa_w_qkv, swa_b_qkv, swa_sinks, swa_w_o, swa_b_o, w_ff1, w_ff2, g_final, loss_target, m_w_ada, m_b_ada, m_g_mix, m_g_mlp, m_mla_w_dq, m_mla_g_q, m_mla_w_uq, m_mla_w_dkv, m_mla_g_kv, m_mla_w_ukv, m_mla_w_o, m_swa_w_qkv, m_swa_b_qkv, m_swa_sinks, m_swa_w_o, m_swa_b_o, m_w_ff1, m_w_ff2, m_g_final, v_w_ada, v_b_ada, v_g_mix, v_g_mlp, v_mla_w_dq, v_mla_g_q, v_mla_w_uq, v_mla_w_dkv, v_mla_g_kv, v_mla_w_ukv, v_mla_w_o, v_swa_w_qkv, v_swa_b_qkv, v_swa_sinks, v_swa_w_o, v_swa_b_o, v_w_ff1, v_w_ff2, v_g_final):
    W = dict(w_ada=w_ada, b_ada=b_ada, g_mix=g_mix, g_mlp=g_mlp, mla_w_dq=mla_w_dq, mla_g_q=mla_g_q, mla_w_uq=mla_w_uq,
             mla_w_dkv=mla_w_dkv, mla_g_kv=mla_g_kv, mla_w_ukv=mla_w_ukv, mla_w_o=mla_w_o, swa_w_qkv=swa_w_qkv,
             swa_b_qkv=swa_b_qkv, swa_sinks=swa_sinks, swa_w_o=swa_w_o, swa_b_o=swa_b_o, w_ff1=w_ff1, w_ff2=w_ff2,
             g_final=g_final)
    M = dict(w_ada=m_w_ada, b_ada=m_b_ada, g_mix=m_g_mix, g_mlp=m_g_mlp, mla_w_dq=m_mla_w_dq, mla_g_q=m_mla_g_q,
             mla_w_uq=m_mla_w_uq, mla_w_dkv=m_mla_w_dkv, mla_g_kv=m_mla_g_kv, mla_w_ukv=m_mla_w_ukv, mla_w_o=m_mla_w_o,
             swa_w_qkv=m_swa_w_qkv, swa_b_qkv=m_swa_b_qkv, swa_sinks=m_swa_sinks, swa_w_o=m_swa_w_o, swa_b_o=m_swa_b_o,
             w_ff1=m_w_ff1, w_ff2=m_w_ff2, g_final=m_g_final)
    V = dict(w_ada=v_w_ada, b_ada=v_b_ada, g_mix=v_g_mix, g_mlp=v_g_mlp, mla_w_dq=v_mla_w_dq, mla_g_q=v_mla_g_q,
             mla_w_uq=v_mla_w_uq, mla_w_dkv=v_mla_w_dkv, mla_g_kv=v_mla_g_kv, mla_w_ukv=v_mla_w_ukv, mla_w_o=v_mla_w_o,
             swa_w_qkv=v_swa_w_qkv, swa_b_qkv=v_swa_b_qkv, swa_sinks=v_swa_sinks, swa_w_o=v_swa_w_o, swa_b_o=v_swa_b_o,
             w_ff1=v_w_ff1, w_ff2=v_w_ff2, g_final=v_g_final)
    order = list(W)
    names = list(SHARDED)
    core = lax.axis_index("c")
    chip = 2 * lax.axis_index("x") + lax.axis_index("y")
    dev = 2 * chip + core
    core_arr = core.astype(jnp.int32).reshape(1)
    chip_arr = chip.astype(jnp.int32).reshape(1)

    def whole(n, g, own):
        g = lax.dynamic_update_slice(g, own[None], (chip, 0, 0))
        if n in ("w_ff1", "w_ff2"):
            return g
        if n in COL_SPLIT:
            return g.transpose(1, 0, 2).reshape(g.shape[1], N_CHIPS * g.shape[2])
        return g.reshape(N_CHIPS * g.shape[1], g.shape[2])

    early = [n for n in names if n.startswith("mla_")]
    local = {n: W[n].astype(MXU_DTYPE).reshape(_view2d(n)) for n in early}
    wts = {n: whole(n, g, local[n]) for n, g in zip(early, _weight_gather([local[n] for n in early]))}

    nbq, nbo = BIASES["swa_b_qkv"] // N_CHIPS, BIASES["swa_b_o"] // N_CHIPS
    first = jnp.concatenate([c.reshape(-1), swa_b_qkv.reshape(-1), swa_b_o.reshape(-1),
                             jnp.zeros((16 * LANES - D - nbq - nbo,), jnp.float32)]).reshape(16, LANES)
    first_all = _all_gather(first).reshape(N_DEV, 16 * LANES)
    c_all = first_all[:, :D]
    south = first_all[0::2]
    wts["swa_b_qkv"] = south[:, D:D + nbq].reshape(1, N_CHIPS * nbq)
    wts["swa_b_o"] = south[:, D + nbq:D + nbq + nbo].reshape(1, N_CHIPS * nbo)
    cond_all, part = _ada_part(c_all, w_ada)
    ncol = w_ada.shape[2]
    part_all = _all_gather(part.reshape(-1, LANES)).reshape(N_DEV, DEPTH, N_DEV, ncol)
    mine = _one_hot_pick(part_all[0::2], dev, axis=2)
    mod = mine.transpose(1, 0, 2).reshape(DEPTH, N_CHIPS * ncol) + b_ada
    vecs = jnp.concatenate([mod.reshape(DEPTH, 6, D), g_mix[:, None, :], g_mlp[:, None, :]], axis=1)

    late = [("w_ff1", 0), ("w_ff2", 0), ("swa_w_qkv", None), ("swa_w_o", None), ("w_ff1", 1), ("w_ff2", 1)]
    late_local = [(W[n][0] if l is None else W[n][l]).astype(MXU_DTYPE) for n, l in late]
    send_sems, recv_sems, passed, lands, token = _late_gather_start(late_local, [vecs] + [wts[n] for n in early])

    def late_weights(after):
        got = _late_gather_wait(send_sems, recv_sems, passed, lands, after)
        out = {"w_ff1": [None] * DEPTH, "w_ff2": [None] * DEPTH}
        for (n, l), g, own in zip(late, got, late_local):
            if l is None:
                out[n] = whole(n, g, own)
            else:
                out[n][l] = whole(n, g, own)
        return out

    late_names = [n for n in names if n not in early]
    reduce_state = {}

    def on_late_grads(late_grads):
        gl = [late_grads[n] for n in late_names]
        got = _grad_pair_in(gl)
        sums = [_pair_sum(g, s, core_arr, "pair_sum_" + n) for n, g, s in zip(late_names, gl, got)]
        s_sems, r_sems, parts, zones, tok = _exchange_start([s16 for _, s16 in sums])
        reduce_state.update(sums=sums, split=(s_sems, r_sems, parts, zones))
        return tok

    grad_x, grads, small = _sequence_step(
        x[0], loss_target[0], positions[0], vecs, mla_g_q + token[0, 0], mla_g_kv, swa_sinks, g_final, wts,
        late_weights, on_late_grads)

    small["b_ada"] = small.pop("dmod")
    small_all = _all_gather(_pack_small(small))
    pk = lambda src: _pack_small({n: src[n] for n in SMALL if n != "loss" and n not in BIASES})
    g_small, d_small, m_small, v_small = [_unpack_small(t) for t in _small_adamw(small_all, pk(W), pk(M), pk(V))]
    off, n = _small_slots()["b_ada"]
    dmod_all = small_all.reshape(N_DEV, -1)[:, off:off + n].reshape(N_DEV, DEPTH, N_CHIPS, ncol)
    dm = _one_hot_pick(dmod_all, chip, axis=2).transpose(1, 0, 2)
    ada = _ada_grad_adamw(cond_all.T, dm, w_ada, m_w_ada, v_w_ada)

    gl = [grads[n] for n in early]
    got = _grad_pair_in(gl)
    sums = [_pair_sum(g, s, core_arr, "pair_sum_" + n) for n, g, s in zip(early, gl, got)]
    others = _grad_chip_exchange([s16 for _, s16 in sums])
    late_others = _exchange_wait(*reduce_state["split"], grad_x)
    sums, others = list(sums) + list(reduce_state["sums"]), list(others) + list(late_others)
    halves = [_chip_sum(s32, o, chip_arr, "chip_sum_" + n) for n, (s32, _), o in zip(names, sums, others)]
    sibling_halves = _grad_pair_out(halves)

    res = {"w_ada": ada}
    for n, mine_h, got_h in zip(names, halves, sibling_halves):
        res[n] = _adamw_halves(W[n], mine_h, got_h, M[n], V[n], core_arr, "adamw_" + n)
    for n, width in BIASES.items():
        g = _one_hot_pick(g_small[n].reshape(N_CHIPS, width // N_CHIPS), chip, axis=0).reshape(1, -1)
        res[n] = [g] + _adamw(W[n], g, M[n], V[n], "adamw_" + n)
    for name in order:
        if name not in res:
            res[name] = [t[name] for t in (g_small, d_small, m_small, v_small)]
    outs = [g_small["loss"], grad_x[None]]
    for k in range(4):
        outs += [res[name][k] for name in order]
    return tuple(outs)
```

```python
import functools
import math

import jax
import jax.numpy as jnp
import numpy as np
from jax import lax
from jax.experimental import pallas as pl
from jax.experimental.pallas import tpu as pltpu

D = 1024
DEPTH = 2
MLA_HEADS = 8
QK_NOPE = 128
QK_ROPE = 64
V_DIM = 128
Q_LORA = 384
KV_LORA = 256
ROPE_THETA = 10000.0
SWA_HEADS = 16
SWA_KV_HEADS = 4
SWA_HEAD_DIM = 64
SWA_GROUP = SWA_HEADS // SWA_KV_HEADS
WINDOW = 128
D_FF = 4 * D
EPS = 1e-6
ADAM_LR = 0.001
ADAM_B1 = 0.9
ADAM_B2 = 0.999
ADAM_EPS = 1e-08
ADAM_WD = 0.01
ADAM_STEP = 10

N_CHIPS = 4
N_DEV = 8
LANES = 128
QK_EXT = 256
MLA_SCALE = (QK_NOPE + QK_ROPE) ** -0.5
LOG2E = math.log2(math.e)
LN2 = math.log(2.0)
MLA_QSCALE = MLA_SCALE * LOG2E
ATTN_BLOCK = 1024
ATTN_SUB = 512
MLP_ROWS = 512
MLP_COLS = 1024
SWA_SCALE = SWA_HEAD_DIM ** -0.5
NEG = -1e30
MXU_DTYPE = jnp.bfloat16
VMEM_LIMIT = 56 * 1024 * 1024

R_SH1, R_SC1, R_GT1, R_SH2, R_SC2, R_GT2, R_GMIX, R_GMLP = range(8)
R_BO = 6


def _tile(n, pref):
    if n <= pref:
        return n
    for t in range(pref, 7, -1):
        if n % t == 0 and t % 8 == 0:
            return t
    return n


def _dot(a, b):
    return jnp.dot(a, b, preferred_element_type=jnp.float32)


def _dot_nt(a, b):
    return lax.dot_general(a, b, (((1,), (1,)), ((), ())), preferred_element_type=jnp.float32)


def _dot_tn(a, b):
    return lax.dot_general(a, b, (((0,), (0,)), ((), ())), preferred_element_type=jnp.float32)


def _rms(x):
    r = lax.rsqrt(jnp.mean(x * x, axis=-1, keepdims=True) + EPS)
    return x * r, r


def _rms_bwd(dxhat, xhat, r):
    return r * (dxhat - xhat * jnp.mean(dxhat * xhat, axis=-1, keepdims=True))


def _rowsum(v):
    return jnp.sum(v, axis=0, keepdims=True)


def _params(*sem):
    return pltpu.CompilerParams(dimension_semantics=sem, vmem_limit_bytes=VMEM_LIMIT)


def _full(shape):
    nd = len(shape)
    return pl.BlockSpec(shape, lambda *_: (0,) * nd)


def _rows(tm, cols):
    return pl.BlockSpec((tm, cols), lambda i, *_: (i, 0))


def _modulate_bwd(dh, x, vec_ref, r_g, r_sc, r_sh, ps_ref, dres):
    xhat, r = _rms(x)
    g = vec_ref[r_g:r_g + 1, :]
    n = xhat * g
    ps_ref[r_sh:r_sh + 1, :] += _rowsum(dh)
    ps_ref[r_sc:r_sc + 1, :] += _rowsum(dh * n)
    dn = dh * (1.0 + vec_ref[r_sc:r_sc + 1, :])
    ps_ref[r_g:r_g + 1, :] += _rowsum(dn * xhat)
    return dres + _rms_bwd(dn * g, xhat, r)


def _mla_pre(x, vec, wcat, g_q, g_kv, wuq, wukv, cs):
    T = x.shape[0]
    tm = _tile(T, 512)
    H = MLA_HEADS

    def body(x_ref, vec_ref, wcat_ref, gq_ref, gkv_ref, wuq_ref, wukv_ref, cs_ref, h_ref, z_ref, q_ref, k_ref, v_ref):
        xhat, _ = _rms(x_ref[...])
        h = xhat * vec_ref[R_GMIX:R_GMIX + 1, :] * (1.0 + vec_ref[R_SC1:R_SC1 + 1, :]) + vec_ref[R_SH1:R_SH1 + 1, :]
        hb = h.astype(MXU_DTYPE)
        h_ref[...] = hb
        z = _dot(hb, wcat_ref[...])
        z_ref[...] = z
        cq = (_rms(z[:, :Q_LORA])[0] * gq_ref[...]).astype(MXU_DTYPE)
        ckv = (_rms(z[:, Q_LORA:Q_LORA + KV_LORA])[0] * gkv_ref[...]).astype(MXU_DTYPE)
        cs_t = cs_ref[...]
        t = z[:, Q_LORA + KV_LORA:] * cs_t
        k_rope = (t + pltpu.roll(t, QK_ROPE, axis=1)).astype(MXU_DTYPE)
        low = lax.broadcasted_iota(jnp.int32, (1, LANES), 1) < QK_ROPE
        for hd in range(H):
            qf = _dot(cq, wuq_ref[hd])
            tq = qf[:, QK_NOPE:] * cs_t
            tq = tq + pltpu.roll(tq, QK_ROPE, axis=1)
            q_ref[hd, :, :QK_NOPE] = (qf[:, :QK_NOPE] * MLA_QSCALE).astype(MXU_DTYPE)
            q_ref[hd, :, QK_NOPE:] = jnp.where(low, tq * MLA_QSCALE, 0.0).astype(MXU_DTYPE)
            kvf = _dot(ckv, wukv_ref[hd])
            k_ref[hd, :, :QK_NOPE] = kvf[:, :QK_NOPE].astype(MXU_DTYPE)
            k_ref[hd, :, QK_NOPE:] = k_rope
            v_ref[hd] = kvf[:, QK_NOPE:].astype(MXU_DTYPE)

    zc = wcat.shape[1]
    return pl.pallas_call(
        body, name="mla_pre", grid=(T // tm,),
        in_specs=[_rows(tm, D), _full((8, D)), _full(wcat.shape), _full(g_q.shape), _full(g_kv.shape),
                  _full(wuq.shape), _full(wukv.shape), _rows(tm, LANES)],
        out_specs=[_rows(tm, D), _rows(tm, zc),
                   pl.BlockSpec((H, tm, QK_EXT), lambda i: (0, i, 0)),
                   pl.BlockSpec((H, tm, QK_EXT), lambda i: (0, i, 0)),
                   pl.BlockSpec((H, tm, V_DIM), lambda i: (0, i, 0))],
        out_shape=[jax.ShapeDtypeStruct((T, D), MXU_DTYPE), jax.ShapeDtypeStruct((T, zc), jnp.float32),
                   jax.ShapeDtypeStruct((H, T, QK_EXT), MXU_DTYPE), jax.ShapeDtypeStruct((H, T, QK_EXT), MXU_DTYPE),
                   jax.ShapeDtypeStruct((H, T, V_DIM), MXU_DTYPE)],
        compiler_params=_params("parallel"),
    )(x, vec, wcat, g_q, g_kv, wuq, wukv, cs)


def _mla_attn_fwd(q, k, v):
    H, T, _ = q.shape
    tb = _tile(T, ATTN_BLOCK)
    sub = min(ATTN_SUB, tb)
    ns, nb = tb // sub, T // tb

    def body(q_ref, k_ref, v_ref, o_ref, lse_ref, m_sc, l_sc, acc_sc):
        qi, kj = pl.program_id(1), pl.program_id(2)

        @pl.when(kj == 0)
        def _():
            m_sc[...] = jnp.full_like(m_sc, NEG)
            l_sc[...] = jnp.zeros_like(l_sc)
            acc_sc[...] = jnp.zeros_like(acc_sc)

        def update(r, kk, masked):
            rows, keys = pl.ds(r * sub, sub), pl.ds(kk * sub, sub)
            s = _dot_nt(q_ref[0, rows, :], k_ref[0, keys, :])
            if masked:
                row = lax.broadcasted_iota(jnp.int32, (sub, sub), 0)
                col = lax.broadcasted_iota(jnp.int32, (sub, sub), 1)
                s = jnp.where(col <= row, s, NEG)
            m_prev = m_sc[rows, :]
            m_new = jnp.maximum(m_prev, jnp.max(s, axis=1, keepdims=True))
            alpha = jnp.exp2(m_prev - m_new)
            p = jnp.exp2(s - jnp.tile(m_new, (1, sub // LANES)))
            l_sc[rows, :] = alpha * l_sc[rows, :] + jnp.sum(p, axis=1, keepdims=True)
            acc_sc[rows, :] = alpha * acc_sc[rows, :] + _dot(p.astype(MXU_DTYPE), v_ref[0, keys, :])
            m_sc[rows, :] = m_new

        @pl.when(kj < qi)
        def _():
            for kk in range(ns):
                for r in range(ns):
                    update(r, kk, False)

        @pl.when(kj == qi)
        def _():
            for kk in range(ns):
                for r in range(kk, ns):
                    update(r, kk, r == kk)
            l = l_sc[...]
            o_ref[...] = (acc_sc[...] / l).astype(o_ref.dtype)
            lse = m_sc[...] + jnp.log2(l)
            pick = (lax.broadcasted_iota(jnp.int32, (8, LANES), 1) == 0).astype(jnp.float32)
            row = lax.dot_general(pick, lse, (((1,), (1,)), ((), ())), precision=lax.Precision.HIGHEST,
                                  preferred_element_type=jnp.float32)
            lse_ref[0] = row[0:1, :]

    kv_idx = lambda h, i, j: (h, jnp.minimum(i, j), 0)
    return pl.pallas_call(
        body, name="mla_attn_fwd", grid=(H, nb, nb),
        in_specs=[pl.BlockSpec((1, tb, QK_EXT), lambda h, i, j: (h, i, 0)),
                  pl.BlockSpec((1, tb, QK_EXT), kv_idx),
                  pl.BlockSpec((1, tb, V_DIM), kv_idx)],
        out_specs=[pl.BlockSpec((tb, V_DIM), lambda h, i, j: (i, h)),
                   pl.BlockSpec((1, 1, tb), lambda h, i, j: (h, 0, i))],
        out_shape=[jax.ShapeDtypeStruct((T, H * V_DIM), MXU_DTYPE), jax.ShapeDtypeStruct((H, 1, T), jnp.float32)],
        scratch_shapes=[pltpu.VMEM((tb, LANES), jnp.float32), pltpu.VMEM((tb, LANES), jnp.float32),
                        pltpu.VMEM((tb, V_DIM), jnp.float32)],
        compiler_params=_params("parallel", "parallel", "arbitrary"),
    )(q, k, v)


def _post_attn(o, x, w_o, bias, vec):
    T = x.shape[0]
    tm = _tile(T, 512)

    def body(o_ref, x_ref, w_ref, b_ref, vec_ref, y_ref, xm_ref, h_ref):
        y = _dot(o_ref[...], w_ref[...]) + b_ref[...]
        y_ref[...] = y.astype(y_ref.dtype)
        xm = x_ref[...] + vec_ref[R_GT1:R_GT1 + 1, :] * y
        xm_ref[...] = xm
        xhat, _ = _rms(xm)
        h = xhat * vec_ref[R_GMLP:R_GMLP + 1, :] * (1.0 + vec_ref[R_SC2:R_SC2 + 1, :]) + vec_ref[R_SH2:R_SH2 + 1, :]
        h_ref[...] = h.astype(h_ref.dtype)

    return pl.pallas_call(
        body, name="post_attn", grid=(T // tm,),
        in_specs=[_rows(tm, D), _rows(tm, D), _full((D, D)), _full((1, D)), _full((8, D))],
        out_specs=[_rows(tm, D), _rows(tm, D), _rows(tm, D)],
        out_shape=[jax.ShapeDtypeStruct((T, D), MXU_DTYPE), jax.ShapeDtypeStruct((T, D), jnp.float32),
                   jax.ShapeDtypeStruct((T, D), MXU_DTYPE)],
        compiler_params=_params("parallel"),
    )(o, x, w_o, bias, vec)


def _ff_specs(tf):
    per = D_FF // N_CHIPS // tf
    w1 = pl.BlockSpec((None, D, tf), lambda i, f: (f // per, 0, f % per))
    w2 = pl.BlockSpec((None, tf, D), lambda i, f: (f // per, f % per, 0))
    return w1, w2


def _mlp_fwd(h2, w1, w2, xm, vec):
    T = h2.shape[0]
    tm = _tile(T, MLP_ROWS)
    tf = _tile(D_FF // N_CHIPS, MLP_COLS)
    nf = D_FF // tf
    w1_spec, w2_spec = _ff_specs(tf)

    def body(h_ref, w1_ref, w2_ref, xm_ref, vec_ref, a_ref, y_ref, xo_ref, acc):
        f = pl.program_id(1)

        @pl.when(f == 0)
        def _():
            acc[...] = jnp.zeros_like(acc)

        u = jnp.maximum(_dot(h_ref[...], w1_ref[...]), 0.0)
        ab = (u * u).astype(MXU_DTYPE)
        a_ref[...] = ab
        acc[...] += _dot(ab, w2_ref[...])

        @pl.when(f == nf - 1)
        def _():
            y = acc[...]
            y_ref[...] = y.astype(y_ref.dtype)
            xo_ref[...] = xm_ref[...] + vec_ref[R_GT2:R_GT2 + 1, :] * y

    return pl.pallas_call(
        body, name="mlp_fwd", grid=(T // tm, nf),
        in_specs=[_rows(tm, D), w1_spec, w2_spec, _rows(tm, D), _full((8, D))],
        out_specs=[pl.BlockSpec((tm, tf), lambda i, f: (i, f)), _rows(tm, D), _rows(tm, D)],
        out_shape=[jax.ShapeDtypeStruct((T, D_FF), MXU_DTYPE), jax.ShapeDtypeStruct((T, D), MXU_DTYPE),
                   jax.ShapeDtypeStruct((T, D), jnp.float32)],
        scratch_shapes=[pltpu.VMEM((tm, D), jnp.float32)],
        compiler_params=_params("parallel", "arbitrary"),
    )(h2, w1, w2, xm, vec)


def _swa_pre(x, vec, w_qkv, b_qkv):
    T = x.shape[0]
    tm = _tile(T, 512)
    nq = SWA_HEADS * SWA_HEAD_DIM
    nk = SWA_KV_HEADS * SWA_HEAD_DIM
    wq_t, w_kv = w_qkv[:, :nq].T, w_qkv[:, nq:]
    bq_col, b_kv = b_qkv[:, :nq].reshape(nq, 1), b_qkv[:, nq:]

    def body(x_ref, vec_ref, wq_ref, wkv_ref, bq_ref, bkv_ref, h_ref, qt_ref, k_ref, v_ref):
        xhat, _ = _rms(x_ref[...])
        h = xhat * vec_ref[R_GMIX:R_GMIX + 1, :] * (1.0 + vec_ref[R_SC1:R_SC1 + 1, :]) + vec_ref[R_SH1:R_SH1 + 1, :]
        hb = h.astype(MXU_DTYPE)
        h_ref[...] = hb
        qt_ref[...] = ((_dot_nt(wq_ref[...], hb) + bq_ref[...]) * SWA_SCALE).astype(MXU_DTYPE)
        kv = _dot(hb, wkv_ref[...]) + bkv_ref[...]
        k_ref[...] = kv[:, :nk].astype(MXU_DTYPE)
        v_ref[...] = kv[:, nk:].astype(MXU_DTYPE)

    return pl.pallas_call(
        body, name="swa_pre", grid=(T // tm,),
        in_specs=[_rows(tm, D), _full((8, D)), _full(wq_t.shape), _full(w_kv.shape), _full(bq_col.shape),
                  _full(b_kv.shape)],
        out_specs=[_rows(tm, D), pl.BlockSpec((nq, tm), lambda i: (0, i)), _rows(tm, nk), _rows(tm, nk)],
        out_shape=[jax.ShapeDtypeStruct((T, D), MXU_DTYPE), jax.ShapeDtypeStruct((nq, T), MXU_DTYPE),
                   jax.ShapeDtypeStruct((T, nk), MXU_DTYPE), jax.ShapeDtypeStruct((T, nk), MXU_DTYPE)],
        compiler_params=_params("parallel"),
    )(x, vec, wq_t, w_kv, bq_col, b_kv)


def _swa_bias():
    W = WINDOW
    slopes = 2.0 ** (-8.0 * np.arange(1, SWA_HEADS + 1) / SWA_HEADS)
    dist = W + np.arange(W)[None, :] - np.arange(2 * W)[:, None]
    inside = (dist >= 0) & (dist < W)
    bias = np.where(inside[None], -slopes[:, None, None] * dist[None].astype(np.float64), NEG)
    bias = bias.reshape(SWA_KV_HEADS, SWA_GROUP, 2 * W, W).transpose(0, 2, 1, 3)
    return jnp.asarray(bias.reshape(SWA_KV_HEADS, 2 * W, SWA_GROUP * W), jnp.float32)


def _swa_probs(n, kh, qt_ref, kp_ref, kc_ref, bias_ref, sink_ref):
    W, Dh, G = WINDOW, SWA_HEAD_DIM, SWA_GROUP
    qt = jnp.concatenate([qt_ref[(kh * G + g) * Dh:(kh * G + g + 1) * Dh, :] for g in range(G)], axis=1)
    kb = jnp.concatenate([kp_ref[:, kh * Dh:(kh + 1) * Dh], kc_ref[:, kh * Dh:(kh + 1) * Dh]], axis=0)
    s = _dot(kb, qt) + bias_ref[kh]
    key = lax.broadcasted_iota(jnp.int32, (2 * W, 1), 0)
    s = jnp.where((key >= W) | (n > 0), s, NEG)
    sink = sink_ref[kh]
    m = jnp.maximum(jnp.max(s, axis=0, keepdims=True), sink)
    p = jnp.exp(s - m)
    p_sink = jnp.exp(sink - m)
    inv = 1.0 / (jnp.sum(p, axis=0, keepdims=True) + p_sink)
    return qt, kb, p * inv, p_sink * inv


def _swa_attn_fwd(qt, k, v, bias, sink_rows):
    T = qt.shape[1]
    W, Dh, G, Hk = WINDOW, SWA_HEAD_DIM, SWA_GROUP, SWA_KV_HEADS
    nk = Hk * Dh

    def body(qt_ref, kp_ref, kc_ref, vp_ref, vc_ref, bias_ref, sink_ref, ot_ref):
        n = pl.program_id(0)
        for kh in range(Hk):
            _, _, pn, _ = _swa_probs(n, kh, qt_ref, kp_ref, kc_ref, bias_ref, sink_ref)
            vb = jnp.concatenate([vp_ref[:, kh * Dh:(kh + 1) * Dh], vc_ref[:, kh * Dh:(kh + 1) * Dh]], axis=0)
            ot = _dot_tn(vb, pn.astype(MXU_DTYPE))
            for g in range(G):
                ot_ref[(kh * G + g) * Dh:(kh * G + g + 1) * Dh, :] = ot[:, g * W:(g + 1) * W].astype(ot_ref.dtype)

    prev = lambda n: (jnp.maximum(n - 1, 0), 0)
    cur = lambda n: (n, 0)
    col = lambda n: (0, n)
    return pl.pallas_call(
        body, name="swa_attn_fwd", grid=(T // W,),
        in_specs=[pl.BlockSpec((D, W), col), pl.BlockSpec((W, nk), prev), pl.BlockSpec((W, nk), cur),
                  pl.BlockSpec((W, nk), prev), pl.BlockSpec((W, nk), cur), _full(bias.shape), _full(sink_rows.shape)],
        out_specs=pl.BlockSpec((D, W), col),
        out_shape=jax.ShapeDtypeStruct((D, T), MXU_DTYPE),
        compiler_params=_params("parallel"),
    )(qt, k, k, v, v, bias, sink_rows)


def _final_loss(x, tgt, g):
    T = x.shape[0]
    tm = _tile(T, 512)

    def body(x_ref, t_ref, g_ref, loss_ref, dx_ref, dg_ref):
        @pl.when(pl.program_id(0) == 0)
        def _():
            loss_ref[...] = jnp.zeros_like(loss_ref)
            dg_ref[...] = jnp.zeros_like(dg_ref)

        xhat, r = _rms(x_ref[...])
        gv = g_ref[...]
        e = xhat * gv - t_ref[...]
        loss_ref[...] += 0.5 * jnp.sum(jnp.mean(e * e, axis=-1, keepdims=True), axis=0, keepdims=True)
        dy = e * (1.0 / D)
        dg_ref[...] += _rowsum(dy * xhat)
        dx_ref[...] = _rms_bwd(dy * gv, xhat, r)

    return pl.pallas_call(
        body, name="final_loss", grid=(T // tm,),
        in_specs=[_rows(tm, D), _rows(tm, D), _full((1, D))],
        out_specs=[_full((8, LANES)), _rows(tm, D), _full((1, D))],
        out_shape=[jax.ShapeDtypeStruct((8, LANES), jnp.float32), jax.ShapeDtypeStruct((T, D), jnp.float32),
                   jax.ShapeDtypeStruct((1, D), jnp.float32)],
        compiler_params=_params("arbitrary"),
    )(x, tgt, g)


def _mlp_bwd(dxo, y2, a, w1, w2, xm, vec):
    T = dxo.shape[0]
    tm = _tile(T, MLP_ROWS)
    tf = _tile(D_FF // N_CHIPS, MLP_COLS)
    nf = D_FF // tf
    w1_spec, w2_spec = _ff_specs(tf)

    def body(dxo_ref, y_ref, a_ref, w1_ref, w2_ref, xm_ref, vec_ref, du_ref, dy_ref, dxm_ref, ps_ref, dyb, acc):
        i, f = pl.program_id(0), pl.program_id(1)

        @pl.when((i == 0) & (f == 0))
        def _():
            ps_ref[...] = jnp.zeros_like(ps_ref)

        @pl.when(f == 0)
        def _():
            dxo_t = dxo_ref[...]
            d = (dxo_t * vec_ref[R_GT2:R_GT2 + 1, :]).astype(MXU_DTYPE)
            dyb[...] = d
            dy_ref[...] = d
            acc[...] = jnp.zeros_like(acc)
            ps_ref[R_GT2:R_GT2 + 1, :] += _rowsum(dxo_t * y_ref[...].astype(jnp.float32))

        da = _dot_nt(dyb[...], w2_ref[...])
        dub = (da * (2.0 * jnp.sqrt(a_ref[...].astype(jnp.float32)))).astype(MXU_DTYPE)
        du_ref[...] = dub
        acc[...] += _dot_nt(dub, w1_ref[...])

        @pl.when(f == nf - 1)
        def _():
            dxm_ref[...] = _modulate_bwd(acc[...], xm_ref[...], vec_ref, R_GMLP, R_SC2, R_SH2, ps_ref, dxo_ref[...])

    return pl.pallas_call(
        body, name="mlp_bwd", grid=(T // tm, nf),
        in_specs=[_rows(tm, D), _rows(tm, D), pl.BlockSpec((tm, tf), lambda i, f: (i, f)), w1_spec, w2_spec,
                  _rows(tm, D), _full((8, D))],
        out_specs=[pl.BlockSpec((tm, tf), lambda i, f: (i, f)), _rows(tm, D), _rows(tm, D), _full((8, D))],
        out_shape=[jax.ShapeDtypeStruct((T, D_FF), MXU_DTYPE), jax.ShapeDtypeStruct((T, D), MXU_DTYPE),
                   jax.ShapeDtypeStruct((T, D), jnp.float32), jax.ShapeDtypeStruct((8, D), jnp.float32)],
        scratch_shapes=[pltpu.VMEM((tm, D), MXU_DTYPE), pltpu.VMEM((tm, D), jnp.float32)],
        compiler_params=_params("arbitrary", "arbitrary"),
    )(dxo, y2, a, w1, w2, xm, vec)


def _mm_tn(a, g, name, split=None, layers=1, layer=0, into=None):
    T, K = a.shape
    N = g.shape[1]
    kq = K // N_CHIPS if split == "rows" else K
    nq = N // N_CHIPS if split == "cols" else N
    bk, bn, bt = _tile(kq, 1024), _tile(nq, 1024), _tile(T, 1024)
    if nq % bn or bn % LANES:
        bn = nq
    kper, nper = kq // bk, nq // bn

    def body(*refs):
        a_ref, g_ref, o_ref = refs[0], refs[1], refs[-1]

        @pl.when(pl.program_id(2) == 0)
        def _():
            o_ref[...] = jnp.zeros_like(o_ref)

        o_ref[...] += _dot_tn(a_ref[...], g_ref[...])

    in_specs = [pl.BlockSpec((bt, bk), lambda k, n, t: (t, k)), pl.BlockSpec((bt, bn), lambda k, n, t: (t, n))]
    args = [a, g]
    aliases = {}
    if split is None:
        out_spec = pl.BlockSpec((bk, bn), lambda k, n, t: (k, n))
        out_shape = jax.ShapeDtypeStruct((K, N), jnp.float32)
    else:
        if split == "cols":
            idx = lambda k, n, t: (n // nper, layer, k, n % nper)
        else:
            idx = lambda k, n, t: (k // kper, layer, k % kper, n)
        out_spec = pl.BlockSpec((None, None, bk, bn), idx)
        out_shape = jax.ShapeDtypeStruct((N_CHIPS, layers, kq, nq), jnp.float32)
        if into is not None:
            in_specs.append(pl.BlockSpec(memory_space=pl.ANY))
            args.append(into)
            aliases = {2: 0}
    return pl.pallas_call(
        body, name=name, grid=(K // bk, N // bn, T // bt), in_specs=in_specs, out_specs=out_spec, out_shape=out_shape,
        input_output_aliases=aliases, compiler_params=_params("parallel", "parallel", "arbitrary"),
    )(*args)


def _attn_out_bwd(dxm, y1, o, w_o, vec, with_delta):
    T = dxm.shape[0]
    tm = _tile(T, 512)
    H = MLA_HEADS

    def body(dxm_ref, y_ref, o_ref, w_ref, vec_ref, dy_ref, do_ref, ps_ref, *delta_ref):
        @pl.when(pl.program_id(0) == 0)
        def _():
            ps_ref[...] = jnp.zeros_like(ps_ref)

        dxm_t = dxm_ref[...]
        dy = dxm_t * vec_ref[R_GT1:R_GT1 + 1, :]
        ps_ref[R_GT1:R_GT1 + 1, :] += _rowsum(dxm_t * y_ref[...].astype(jnp.float32))
        ps_ref[R_BO:R_BO + 1, :] += _rowsum(dy)
        dyb = dy.astype(MXU_DTYPE)
        dy_ref[...] = dyb
        if not with_delta:
            do_ref[...] = _dot_nt(w_ref[...], dyb).astype(do_ref.dtype)
        else:
            do = _dot_nt(dyb, w_ref[...])
            do_ref[...] = do.astype(do_ref.dtype)
            of = o_ref[...].astype(jnp.float32)
            ones = jnp.ones((8, V_DIM), jnp.float32)
            for hd in range(H):
                sl = slice(hd * V_DIM, (hd + 1) * V_DIM)
                d = lax.dot_general(ones, do[:, sl] * of[:, sl], (((1,), (1,)), ((), ())),
                                    precision=lax.Precision.HIGHEST, preferred_element_type=jnp.float32)
                delta_ref[0][hd] = d[0:1, :]

    out_specs = [_rows(tm, D), _rows(tm, D), _full((8, D))]
    out_shape = [jax.ShapeDtypeStruct((T, D), MXU_DTYPE), jax.ShapeDtypeStruct((T, D), MXU_DTYPE),
                 jax.ShapeDtypeStruct((8, D), jnp.float32)]
    if not with_delta:
        out_specs[1] = pl.BlockSpec((D, tm), lambda i: (0, i))
        out_shape[1] = jax.ShapeDtypeStruct((D, T), MXU_DTYPE)
    if with_delta:
        out_specs.append(pl.BlockSpec((H, 1, tm), lambda i: (0, 0, i)))
        out_shape.append(jax.ShapeDtypeStruct((H, 1, T), jnp.float32))
    return pl.pallas_call(
        body, name="attn_out_bwd_mla" if with_delta else "attn_out_bwd_swa", grid=(T // tm,),
        in_specs=[_rows(tm, D), _rows(tm, D), _rows(tm, D), _full((D, D)), _full((8, D))],
        out_specs=out_specs, out_shape=out_shape,
        compiler_params=_params("arbitrary"),
    )(dxm, y1, o, w_o, vec)


def _mla_attn_bwd(q, k, v, do, lse, delta):
    H, T, _ = q.shape
    tb = _tile(T, ATTN_BLOCK)
    sub = min(ATTN_SUB, tb)
    ns, nb = tb // sub, T // tb

    def body(q_ref, k_ref, v_ref, do_ref, lse_ref, dl_ref, dq_ref, dk_ref, dv_ref, dk_acc, dv_acc):
        j, i = pl.program_id(1), pl.program_id(2)

        @pl.when((j == 0) & (i == 0))
        def _():
            dq_ref[...] = jnp.zeros_like(dq_ref)

        def update(kk, r, masked):
            keys, rows = pl.ds(kk * sub, sub), pl.ds(r * sub, sub)
            kb, qb, dob = k_ref[0, keys, :], q_ref[0, rows, :], do_ref[rows, :]
            st = _dot_nt(kb, qb)
            if masked:
                row = lax.broadcasted_iota(jnp.int32, (sub, sub), 0)
                col = lax.broadcasted_iota(jnp.int32, (sub, sub), 1)
                st = jnp.where(row <= col, st, NEG)
            pt = jnp.exp2(st - lse_ref[0, :, rows])
            dv_acc[keys, :] += _dot(pt.astype(MXU_DTYPE), dob)
            dpt = _dot_nt(v_ref[0, keys, :], dob)
            dst = (pt * (dpt - dl_ref[0, :, rows])).astype(MXU_DTYPE)
            dk_acc[keys, :] += _dot(dst, qb)
            q_rows = pl.ds(pl.multiple_of(i * tb + r * sub, sub), sub)
            dq_ref[0, q_rows, :] += _dot_tn(dst, kb)

        @pl.when(i == j)
        def _():
            dk_acc[...] = jnp.zeros_like(dk_acc)
            dv_acc[...] = jnp.zeros_like(dv_acc)
            for r in range(ns):
                for kk in range(r + 1):
                    update(kk, r, kk == r)

        @pl.when(i > j)
        def _():
            for r in range(ns):
                for kk in range(ns):
                    update(kk, r, False)

        @pl.when(i == nb - 1)
        def _():
            dk_ref[0] = (dk_acc[...] * LN2).astype(dk_ref.dtype)
            dv_ref[0] = dv_acc[...].astype(dv_ref.dtype)

    q_idx = lambda h, j, i: (h, jnp.maximum(i, j), 0)
    kv_idx = lambda h, j, i: (h, j, 0)
    stat_idx = lambda h, j, i: (h, 0, jnp.maximum(i, j))
    return pl.pallas_call(
        body, name="mla_attn_bwd", grid=(H, nb, nb),
        in_specs=[pl.BlockSpec((1, tb, QK_EXT), q_idx), pl.BlockSpec((1, tb, QK_EXT), kv_idx),
                  pl.BlockSpec((1, tb, V_DIM), kv_idx),
                  pl.BlockSpec((tb, V_DIM), lambda h, j, i: (jnp.maximum(i, j), h)),
                  pl.BlockSpec((1, 1, tb), stat_idx), pl.BlockSpec((1, 1, tb), stat_idx)],
        out_specs=[pl.BlockSpec((1, T, QK_EXT), lambda h, j, i: (h, 0, 0)),
                   pl.BlockSpec((1, tb, QK_EXT), kv_idx), pl.BlockSpec((1, tb, V_DIM), kv_idx)],
        out_shape=[jax.ShapeDtypeStruct((H, T, QK_EXT), jnp.float32), jax.ShapeDtypeStruct((H, T, QK_EXT), MXU_DTYPE),
                   jax.ShapeDtypeStruct((H, T, V_DIM), MXU_DTYPE)],
        scratch_shapes=[pltpu.VMEM((tb, QK_EXT), jnp.float32), pltpu.VMEM((tb, V_DIM), jnp.float32)],
        compiler_params=_params("parallel", "arbitrary", "arbitrary"),
    )(q, k, v, do, lse, delta)


def _mla_pre_bwd(x, dxm, vec, hb, z, dq, dk, dv, cs, wcat, g_q, g_kv, wuq, wukv):
    T = x.shape[0]
    tm = _tile(T, 256)
    H = MLA_HEADS
    zc = wcat.shape[1]

    def body(x_ref, dxm_ref, vec_ref, h_ref, z_ref, dq_ref, dk_ref, dv_ref, cs_ref, wcat_ref, gq_ref, gkv_ref,
             wuq_ref, wukv_ref, dx_ref, ps_ref, dgq_ref, dgkv_ref, dwcat_ref, dwuq_ref, dwukv_ref):
        @pl.when(pl.program_id(0) == 0)
        def _():
            for ref in (ps_ref, dgq_ref, dgkv_ref, dwcat_ref, dwuq_ref, dwukv_ref):
                ref[...] = jnp.zeros_like(ref)

        z = z_ref[...]
        cs_t = cs_ref[...]
        cqhat, rq = _rms(z[:, :Q_LORA])
        ckhat, rk = _rms(z[:, Q_LORA:Q_LORA + KV_LORA])
        gq, gkv = gq_ref[...], gkv_ref[...]
        cq = (cqhat * gq).astype(MXU_DTYPE)
        ckv = (ckhat * gkv).astype(MXU_DTYPE)
        dcq = jnp.zeros((tm, Q_LORA), jnp.float32)
        dckv = jnp.zeros((tm, KV_LORA), jnp.float32)
        dkr = jnp.zeros((tm, LANES), jnp.float32)
        for hd in range(H):
            dqh = dq_ref[hd] * MLA_SCALE
            gqh = jnp.concatenate([dqh[:, :QK_NOPE], dqh[:, QK_NOPE:] * cs_t], axis=1).astype(MXU_DTYPE)
            dcq += _dot_nt(gqh, wuq_ref[hd])
            dwuq_ref[hd] += _dot_tn(cq, gqh)
            dkh = dk_ref[hd]
            gkvh = jnp.concatenate([dkh[:, :QK_NOPE], dv_ref[hd]], axis=1)
            dckv += _dot_nt(gkvh, wukv_ref[hd])
            dwukv_ref[hd] += _dot_tn(ckv, gkvh)
            dkr += dkh[:, QK_NOPE:].astype(jnp.float32)
        dgq_ref[...] += _rowsum(dcq * cqhat)
        dgkv_ref[...] += _rowsum(dckv * ckhat)
        dcq_pre = _rms_bwd(dcq * gq, cqhat, rq)
        dckv_pre = _rms_bwd(dckv * gkv, ckhat, rk)
        dkr2 = (dkr + pltpu.roll(dkr, QK_ROPE, axis=1)) * cs_t
        dz = jnp.concatenate([dcq_pre, dckv_pre, dkr2], axis=1).astype(MXU_DTYPE)
        dwcat_ref[...] += _dot_tn(h_ref[...], dz)
        dh = _dot_nt(dz, wcat_ref[...])
        dx_ref[...] = _modulate_bwd(dh, x_ref[...], vec_ref, R_GMIX, R_SC1, R_SH1, ps_ref, dxm_ref[...])

    hblk = lambda w: pl.BlockSpec((H, tm, w), lambda i: (0, i, 0))
    return pl.pallas_call(
        body, name="mla_pre_bwd", grid=(T // tm,),
        in_specs=[_rows(tm, D), _rows(tm, D), _full((8, D)), _rows(tm, D), _rows(tm, zc), hblk(QK_EXT), hblk(QK_EXT),
                  hblk(V_DIM), _rows(tm, LANES), _full(wcat.shape), _full(g_q.shape), _full(g_kv.shape),
                  _full(wuq.shape), _full(wukv.shape)],
        out_specs=[_rows(tm, D), _full((8, D)), _full(g_q.shape), _full(g_kv.shape), _full(wcat.shape),
                   _full(wuq.shape), _full(wukv.shape)],
        out_shape=[jax.ShapeDtypeStruct((T, D), jnp.float32), jax.ShapeDtypeStruct((8, D), jnp.float32),
                   jax.ShapeDtypeStruct(g_q.shape, jnp.float32), jax.ShapeDtypeStruct(g_kv.shape, jnp.float32),
                   jax.ShapeDtypeStruct(wcat.shape, jnp.float32), jax.ShapeDtypeStruct(wuq.shape, jnp.float32),
                   jax.ShapeDtypeStruct(wukv.shape, jnp.float32)],
        compiler_params=_params("arbitrary"),
    )(x, dxm, vec, hb, z, dq, dk, dv, cs, wcat, g_q, g_kv, wuq, wukv)


def _swa_attn_bwd(qt, k, v, dot_, bias, sink_rows):
    T = qt.shape[1]
    W, Dh, G, Hk = WINDOW, SWA_HEAD_DIM, SWA_GROUP, SWA_KV_HEADS
    nk = Hk * Dh

    def body(qt_ref, kp_ref, kc_ref, vp_ref, vc_ref, dot_ref, bias_ref, sink_ref, dqt_ref, dk_ref, dv_ref, dsink_ref):
        n = pl.program_id(0)

        @pl.when(n == 0)
        def _():
            dk_ref[...] = jnp.zeros_like(dk_ref)
            dv_ref[...] = jnp.zeros_like(dv_ref)
            dsink_ref[...] = jnp.zeros_like(dsink_ref)

        dks, dvs = [], []
        for kh in range(Hk):
            qt, kb, pn, p_sink = _swa_probs(n, kh, qt_ref, kp_ref, kc_ref, bias_ref, sink_ref)
            vb = jnp.concatenate([vp_ref[:, kh * Dh:(kh + 1) * Dh], vc_ref[:, kh * Dh:(kh + 1) * Dh]], axis=0)
            dot_h = jnp.concatenate([dot_ref[(kh * G + g) * Dh:(kh * G + g + 1) * Dh, :] for g in range(G)], axis=1)
            dp = _dot(vb, dot_h)
            delta = jnp.sum(pn * dp, axis=0, keepdims=True)
            dsb = (pn * (dp - delta)).astype(MXU_DTYPE)
            dsink_ref[kh] += -p_sink * delta
            dqt = _dot_tn(kb, dsb) * SWA_SCALE
            for g in range(G):
                dqt_ref[(kh * G + g) * Dh:(kh * G + g + 1) * Dh, :] = dqt[:, g * W:(g + 1) * W]
            dks.append(_dot_nt(dsb, qt))
            dvs.append(_dot_nt(pn.astype(MXU_DTYPE), dot_h))
        dkb = jnp.concatenate(dks, axis=1)
        dvb = jnp.concatenate(dvs, axis=1)
        cur_rows = pl.ds(pl.multiple_of(n * W, W), W)
        dk_ref[cur_rows, :] += dkb[W:]
        dv_ref[cur_rows, :] += dvb[W:]

        @pl.when(n > 0)
        def _():
            prev_rows = pl.ds(pl.multiple_of((n - 1) * W, W), W)
            dk_ref[prev_rows, :] += dkb[:W]
            dv_ref[prev_rows, :] += dvb[:W]

    prev = lambda n: (jnp.maximum(n - 1, 0), 0)
    cur = lambda n: (n, 0)
    col = lambda n: (0, n)
    return pl.pallas_call(
        body, name="swa_attn_bwd", grid=(T // W,),
        in_specs=[pl.BlockSpec((D, W), col), pl.BlockSpec((W, nk), prev), pl.BlockSpec((W, nk), cur),
                  pl.BlockSpec((W, nk), prev), pl.BlockSpec((W, nk), cur), pl.BlockSpec((D, W), col),
                  _full(bias.shape), _full(sink_rows.shape)],
        out_specs=[pl.BlockSpec((D, W), col), _full((T, nk)), _full((T, nk)), _full(sink_rows.shape)],
        out_shape=[jax.ShapeDtypeStruct((D, T), jnp.float32), jax.ShapeDtypeStruct((T, nk), jnp.float32),
                   jax.ShapeDtypeStruct((T, nk), jnp.float32), jax.ShapeDtypeStruct(sink_rows.shape, jnp.float32)],
        compiler_params=_params("arbitrary"),
    )(qt, k, k, v, v, dot_, bias, sink_rows)


def _swa_pre_bwd(x, dxm, vec, dq, dk, dv, w_qkv):
    T = x.shape[0]
    tm = _tile(T, 512)
    nq = SWA_HEADS * SWA_HEAD_DIM
    nk = SWA_KV_HEADS * SWA_HEAD_DIM
    nqkv = nq + 2 * nk

    def body(x_ref, dxm_ref, vec_ref, dq_ref, dk_ref, dv_ref, w_ref, dx_ref, dqkv_ref, ps_ref, db_ref):
        @pl.when(pl.program_id(0) == 0)
        def _():
            ps_ref[...] = jnp.zeros_like(ps_ref)
            db_ref[...] = jnp.zeros_like(db_ref)

        dqkv = jnp.concatenate([dq_ref[...], dk_ref[...], dv_ref[...]], axis=1)
        db_ref[...] += _rowsum(dqkv)
        dqkv_b = dqkv.astype(MXU_DTYPE)
        dqkv_ref[...] = dqkv_b
        dh = _dot_nt(dqkv_b, w_ref[...])
        dx_ref[...] = _modulate_bwd(dh, x_ref[...], vec_ref, R_GMIX, R_SC1, R_SH1, ps_ref, dxm_ref[...])

    return pl.pallas_call(
        body, name="swa_pre_bwd", grid=(T // tm,),
        in_specs=[_rows(tm, D), _rows(tm, D), _full((8, D)), _rows(tm, nq), _rows(tm, nk), _rows(tm, nk),
                  _full(w_qkv.shape)],
        out_specs=[_rows(tm, D), _rows(tm, nqkv), _full((8, D)), _full((1, nqkv))],
        out_shape=[jax.ShapeDtypeStruct((T, D), jnp.float32), jax.ShapeDtypeStruct((T, nqkv), MXU_DTYPE),
                   jax.ShapeDtypeStruct((8, D), jnp.float32), jax.ShapeDtypeStruct((1, nqkv), jnp.float32)],
        compiler_params=_params("arbitrary"),
    )(x, dxm, vec, dq, dk, dv, w_qkv)


def _rot_cols(w):
    half = QK_ROPE // 2
    return jnp.concatenate([-w[..., half:], w[..., :half]], axis=-1)


def _unrot_grad(d_rope, d_rot):
    half = QK_ROPE // 2
    return d_rope + jnp.concatenate([d_rot[..., half:], -d_rot[..., :half]], axis=-1)


def _rope_table(positions):
    half = QK_ROPE // 2
    inv_freq = ROPE_THETA ** (-jnp.arange(half, dtype=jnp.float32) / half)
    ang = positions.astype(jnp.float32)[:, None] * inv_freq
    cos, sin = jnp.cos(ang), jnp.sin(ang)
    return jnp.concatenate([cos, cos, sin, sin], axis=1)


def _sequence_step(x, tgt, positions, vecs, g_q, g_kv, sinks, g_final, wts, late_weights, on_late_grads):
    H = MLA_HEADS
    cs = _rope_table(positions)
    w_dkv = wts["mla_w_dkv"]
    wcat = jnp.concatenate([wts["mla_w_dq"], w_dkv, _rot_cols(w_dkv[:, KV_LORA:])], axis=1)
    uq = wts["mla_w_uq"].reshape(Q_LORA, H, QK_NOPE + QK_ROPE)
    wuq = jnp.concatenate([uq, _rot_cols(uq[..., QK_NOPE:])], axis=-1).transpose(1, 0, 2)
    wukv = wts["mla_w_ukv"].reshape(KV_LORA, H, QK_NOPE + V_DIM).transpose(1, 0, 2)
    zero_bias = jnp.zeros((1, D), jnp.float32)
    bias = _swa_bias()
    sink_rows = jnp.broadcast_to(sinks.reshape(SWA_KV_HEADS, 1, SWA_GROUP, 1),
                                 (SWA_KV_HEADS, 1, SWA_GROUP, WINDOW)).reshape(SWA_KV_HEADS, 1, SWA_GROUP * WINDOW)

    h1a, z, q, k, v = _mla_pre(x, vecs[0], wcat, g_q, g_kv, wuq, wukv, cs)
    o_a, lse = _mla_attn_fwd(q, k, v)
    y1a, xm_a, h2a = _post_attn(o_a, x, wts["mla_w_o"], zero_bias, vecs[0])
    wts = {**wts, **late_weights(h2a)}
    a_a, y2a, x1 = _mlp_fwd(h2a, wts["w_ff1"][0], wts["w_ff2"][0], xm_a, vecs[0])

    h1b, qs_t, ks, vs = _swa_pre(x1, vecs[1], wts["swa_w_qkv"], wts["swa_b_qkv"])
    o_b = _swa_attn_fwd(qs_t, ks, vs, bias, sink_rows).T
    y1b, xm_b, h2b = _post_attn(o_b, x1, wts["swa_w_o"], wts["swa_b_o"], vecs[1])
    a_b, y2b, x2 = _mlp_fwd(h2b, wts["w_ff1"][1], wts["w_ff2"][1], xm_b, vecs[1])

    loss8, dx2, dg_final = _final_loss(x2, tgt, g_final.reshape(1, D))

    du_b, dy2b, dxm_b, ps_mlp_b = _mlp_bwd(dx2, y2b, a_b, wts["w_ff1"][1], wts["w_ff2"][1], xm_b, vecs[1])
    g_ff2 = _mm_tn(a_b, dy2b, "dw_ff2_l1", "rows", DEPTH, 1)
    g_ff1 = _mm_tn(h2b, du_b, "dw_ff1_l1", "cols", DEPTH, 1)
    dy1b, do_bt, ps_out_b = _attn_out_bwd(dxm_b, y1b, o_b, wts["swa_w_o"], vecs[1], False)
    g_swa_o = _mm_tn(o_b, dy1b, "dw_o_swa")
    dqs_t, dks, dvs, dsinks = _swa_attn_bwd(qs_t, ks, vs, do_bt, bias, sink_rows)
    dqs = dqs_t.T
    dx1, dqkv, ps_pre_b, g_swa_bqkv = _swa_pre_bwd(x1, dxm_b, vecs[1], dqs, dks, dvs, wts["swa_w_qkv"])
    g_swa_qkv = _mm_tn(h1b, dqkv, "dw_qkv", "cols")

    du_a, dy2a, dxm_a, ps_mlp_a = _mlp_bwd(dx1, y2a, a_a, wts["w_ff1"][0], wts["w_ff2"][0], xm_a, vecs[0])
    g_ff2 = _mm_tn(a_a, dy2a, "dw_ff2_l0", "rows", DEPTH, 0, g_ff2)
    g_ff1 = _mm_tn(h2a, du_a, "dw_ff1_l0", "cols", DEPTH, 0, g_ff1)
    rows4 = lambda g: g.reshape(N_CHIPS, g.shape[0] // N_CHIPS, g.shape[1])
    token = on_late_grads({
        "swa_w_qkv": g_swa_qkv.reshape(N_CHIPS, D, -1), "swa_w_o": rows4(g_swa_o),
        "w_ff1": g_ff1.reshape(N_CHIPS, DEPTH * D, -1), "w_ff2": g_ff2.reshape(N_CHIPS, -1, D)})
    dy1a, do_a, ps_out_a, delta = _attn_out_bwd(dxm_a, y1a, o_a, wts["mla_w_o"], vecs[0] + token[0, 0], True)
    g_mla_o = _mm_tn(o_a, dy1a, "dw_o_mla")
    dq, dk, dv = _mla_attn_bwd(q, k, v, do_a, lse, delta)
    dx0, ps_pre_a, dg_q, dg_kv, dwcat, dwuq, dwukv = _mla_pre_bwd(
        x, dxm_a, vecs[0], h1a, z, dq, dk, dv, cs, wcat, g_q, g_kv, wuq, wukv)

    c0, c1, c2 = Q_LORA, Q_LORA + KV_LORA, Q_LORA + KV_LORA + QK_ROPE
    g_dq = dwcat[:, :c0]
    g_dkv = jnp.concatenate([dwcat[:, c0:c1], _unrot_grad(dwcat[:, c1:c2], dwcat[:, c2:])], axis=1)
    e0 = QK_NOPE + QK_ROPE
    g_uq = jnp.concatenate([dwuq[..., :QK_NOPE], _unrot_grad(dwuq[..., QK_NOPE:e0], dwuq[..., e0:])], axis=-1)
    per = H // N_CHIPS
    g_uq = g_uq.reshape(N_CHIPS, per, Q_LORA, e0).transpose(0, 2, 1, 3).reshape(N_CHIPS, Q_LORA, per * e0)
    g_ukv = dwukv.reshape(N_CHIPS, per, KV_LORA, QK_NOPE + V_DIM).transpose(0, 2, 1, 3)
    g_ukv = g_ukv.reshape(N_CHIPS, KV_LORA, per * (QK_NOPE + V_DIM))

    def dmod(ps_pre, ps_out, ps_mlp):
        return jnp.concatenate([ps_pre[R_SH1:R_SC1 + 1], ps_out[R_GT1:R_GT1 + 1], ps_mlp[R_SH2:R_GT2 + 1]], axis=0)

    grads = {"mla_w_dq": rows4(g_dq), "mla_w_uq": g_uq, "mla_w_dkv": rows4(g_dkv), "mla_w_ukv": g_ukv,
             "mla_w_o": rows4(g_mla_o)}
    small = {
        "dmod": jnp.stack([dmod(ps_pre_a, ps_out_a, ps_mlp_a), dmod(ps_pre_b, ps_out_b, ps_mlp_b)]).reshape(DEPTH, 6 * D),
        "g_mix": jnp.stack([ps_pre_a[R_GMIX], ps_pre_b[R_GMIX]]),
        "g_mlp": jnp.stack([ps_mlp_a[R_GMLP], ps_mlp_b[R_GMLP]]),
        "mla_g_q": dg_q, "mla_g_kv": dg_kv, "swa_sinks": jnp.sum(dsinks.reshape(SWA_HEADS, WINDOW), axis=1).reshape(1, SWA_HEADS),
        "swa_b_qkv": g_swa_bqkv, "swa_b_o": ps_out_b[R_BO:R_BO + 1],
        "g_final": dg_final.reshape(D), "loss": loss8[0, 0],
    }
    return dx0, grads, small


SHARDED = {
    "mla_w_dq": (1, D // N_CHIPS, Q_LORA),
    "mla_w_uq": (1, Q_LORA, MLA_HEADS * (QK_NOPE + QK_ROPE) // N_CHIPS),
    "mla_w_dkv": (1, D // N_CHIPS, KV_LORA + QK_ROPE),
    "mla_w_ukv": (1, KV_LORA, MLA_HEADS * (QK_NOPE + V_DIM) // N_CHIPS),
    "mla_w_o": (1, MLA_HEADS * V_DIM // N_CHIPS, D),
    "swa_w_qkv": (1, D, (SWA_HEADS + 2 * SWA_KV_HEADS) * SWA_HEAD_DIM // N_CHIPS),
    "swa_w_o": (1, SWA_HEADS * SWA_HEAD_DIM // N_CHIPS, D),
    "w_ff1": (DEPTH, D, D_FF // N_CHIPS),
    "w_ff2": (DEPTH, D_FF // N_CHIPS, D),
}
COL_SPLIT = ("mla_w_uq", "mla_w_ukv", "swa_w_qkv")
BIASES = {"swa_b_qkv": (SWA_HEADS + 2 * SWA_KV_HEADS) * SWA_HEAD_DIM, "swa_b_o": D}


def _view2d(name):
    shape = SHARDED[name]
    return math.prod(shape[:-1]), shape[-1]


SMALL = {"b_ada": (DEPTH, 6 * D), "g_mix": (DEPTH, D), "g_mlp": (DEPTH, D), "mla_g_q": (1, Q_LORA),
         "mla_g_kv": (1, KV_LORA), "swa_sinks": (1, SWA_HEADS), "g_final": (D,), "loss": (),
         "swa_b_qkv": (1, BIASES["swa_b_qkv"]), "swa_b_o": (1, BIASES["swa_b_o"])}
SMALL_ROWS = 168
DMA_ROWS = 256


def _small_slots():
    slots, off = {}, 0
    for name, shape in SMALL.items():
        n = max(math.prod(shape), 1)
        slots[name] = (off, n)
        off += -(-n // LANES) * LANES
    assert off <= SMALL_ROWS * LANES
    return slots


def _pack_small(vals):
    parts, end = [], 0
    for name, (off, n) in _small_slots().items():
        pad = -(-n // LANES) * LANES - n
        v = vals[name].astype(jnp.float32).reshape(-1) if name in vals else jnp.zeros((n,), jnp.float32)
        parts += [v, jnp.zeros((pad,), jnp.float32)]
        end = off + n + pad
    parts.append(jnp.zeros((SMALL_ROWS * LANES - end,), jnp.float32))
    return jnp.concatenate(parts).reshape(SMALL_ROWS, LANES)


def _unpack_small(buf):
    flat = buf.reshape(-1)
    return {name: flat[off:off + n].reshape(SMALL[name]) for name, (off, n) in _small_slots().items()}


def _pieces(rows):
    return [(off, min(DMA_ROWS, rows - off)) for off in range(0, rows, DMA_ROWS)]


HBM = pl.BlockSpec(memory_space=pltpu.HBM)
MESH = pl.DeviceIdType.MESH


def _place():
    x, y, c = lax.axis_index("x"), lax.axis_index("y"), lax.axis_index("c")
    chips = [(1 - x, y), (x, 1 - y), (1 - x, 1 - y)]
    return x, y, c, chips


def _all_gather(block):
    m_per, n = block.shape

    def body(x_ref, out_ref, send_sems, recv_sems, local_sem):
        x, y, c, chips = _place()
        me, sibling = (x, y, c), (x, y, 1 - c)

        def rows(px, py, pc):
            return out_ref.at[pl.ds((4 * px + 2 * py + pc) * m_per, m_per), :]

        def copy(k, blk, to, src=None):
            return pltpu.make_async_remote_copy(
                src_ref=rows(*blk) if src is None else src, dst_ref=rows(*blk),
                send_sem=send_sems.at[k], recv_sem=recv_sems.at[k], device_id=to, device_id_type=MESH)

        mine = pltpu.make_async_copy(x_ref, rows(*me), local_sem)
        mine.start()
        first = [copy(0, me, sibling, src=x_ref)]
        first += [copy(1 + j, me, (*chip, c), src=x_ref) for j, chip in enumerate(chips)]
        for cp in first:
            cp.start()
        passed = [copy(4 + j, (*chip, c), sibling) for j, chip in enumerate(chips)]
        for j, chip in enumerate(chips):
            copy(1 + j, (*chip, c), me).wait_recv()
            passed[j].start()
        copy(0, sibling, me).wait_recv()
        for j, chip in enumerate(chips):
            copy(4 + j, (*chip, 1 - c), me).wait_recv()
        for cp in first + passed:
            cp.wait_send()
        mine.wait()

    out = pl.pallas_call(
        body, name="all_gather_small",
        out_shape=jax.ShapeDtypeStruct((N_DEV * m_per, n), block.dtype),
        in_specs=[pl.BlockSpec(memory_space=pltpu.VMEM)],
        out_specs=pl.BlockSpec(memory_space=pltpu.VMEM),
        scratch_shapes=[pltpu.SemaphoreType.DMA((7,)), pltpu.SemaphoreType.DMA((7,)), pltpu.SemaphoreType.DMA],
    )(block)
    return out.reshape(N_DEV, m_per, n)


def _weight_gather(shards):
    nt = len(shards)

    def body(*refs):
        w_refs, out_refs = refs[:nt], refs[nt:2 * nt]
        send_sems, recv_sems = refs[2 * nt:]
        x, y, c, chips = _place()
        sibling = (x, y, 1 - c)

        def slab(t, px, py, half):
            rh = shards[t].shape[0] // 2
            return out_refs[t].at[2 * px + py, pl.ds(half * rh, rh), :]

        def copy(t, k, src, dst, to):
            return pltpu.make_async_remote_copy(src_ref=src, dst_ref=dst, send_sem=send_sems.at[6 * t + k],
                                                recv_sem=recv_sems.at[6 * t + k], device_id=to, device_id_type=MESH)

        first = []
        for t in range(nt):
            rh = shards[t].shape[0] // 2
            first += [copy(t, j, w_refs[t].at[pl.ds(c * rh, rh), :], slab(t, x, y, c), (*chip, c))
                      for j, chip in enumerate(chips)]
        for cp in first:
            cp.start()
        passed = []
        for t in range(nt):
            for j, chip in enumerate(chips):
                copy(t, j, slab(t, *chip, c), slab(t, *chip, c), (*chip, c)).wait_recv()
                rh = shards[t].shape[0] // 2
                for off, n in _pieces(rh):
                    piece = out_refs[t].at[2 * chip[0] + chip[1], pl.ds(c * rh + off, n), :]
                    copy(t, 3 + j, piece, piece, sibling).start()
                passed.append(copy(t, 3 + j, slab(t, *chip, c), slab(t, *chip, c), sibling))
        for t in range(nt):
            for j, chip in enumerate(chips):
                copy(t, 3 + j, slab(t, *chip, 1 - c), slab(t, *chip, 1 - c), sibling).wait_recv()
        for cp in first + passed:
            cp.wait_send()

    return pl.pallas_call(
        body, name="weight_gather",
        out_shape=[jax.ShapeDtypeStruct((N_CHIPS,) + s.shape, s.dtype) for s in shards],
        in_specs=[HBM] * nt, out_specs=[HBM] * nt,
        scratch_shapes=[pltpu.SemaphoreType.DMA((6 * nt,)), pltpu.SemaphoreType.DMA((6 * nt,))],
    )(*shards)


SEM = pl.BlockSpec(memory_space=pltpu.SEMAPHORE)
ANY = pl.BlockSpec(memory_space=pl.ANY)
SPLIT_COPY = pltpu.SideEffectType.DATAFLOW_SIDE_EFFECTING


def _late_copies(w_refs, land_refs, send_sems, recv_sems):
    x, y, c, chips = _place()
    return [pltpu.make_async_remote_copy(
        src_ref=w_refs[t], dst_ref=land_refs[t].at[2 * x + y], send_sem=send_sems.at[3 * t + j],
        recv_sem=recv_sems.at[3 * t + j], device_id=(cx, cy, c), device_id_type=MESH)
        for t in range(len(w_refs)) for j, (cx, cy) in enumerate(chips)], chips


def _late_gather_start(shards, after):
    nt, na = len(shards), len(after)

    def body(*refs):
        w_refs, land_refs = refs[:nt], refs[nt:2 * nt]
        send_sems, recv_sems, token = refs[2 * nt + na], refs[2 * nt + na + 1], refs[-1]
        copies, _ = _late_copies(w_refs, land_refs, send_sems, recv_sems)
        for cp in copies:
            cp.start()
        token[...] = jnp.zeros_like(token)

    hbm = lambda a: pltpu.with_memory_space_constraint(a, pltpu.HBM)
    lands = [lax.empty((N_CHIPS,) + s.shape, s.dtype) for s in shards]
    outs = pl.pallas_call(
        body, name="late_gather_start",
        out_shape=(pltpu.SemaphoreType.DMA((3 * nt,)), pltpu.SemaphoreType.DMA((3 * nt,)),
                   *[pltpu.HBM(s.shape, s.dtype) for s in shards], *[pltpu.HBM(l.shape, l.dtype) for l in lands],
                   jax.ShapeDtypeStruct((8, LANES), jnp.float32)),
        in_specs=[HBM] * (2 * nt) + [ANY] * na,
        out_specs=(SEM, SEM, *([HBM] * (2 * nt)), pl.BlockSpec(memory_space=pltpu.VMEM)),
        input_output_aliases={i: 2 + i for i in range(2 * nt)},
        compiler_params=pltpu.CompilerParams(has_side_effects=SPLIT_COPY),
    )(*[hbm(s) for s in shards], *[hbm(l) for l in lands], *after)
    return outs[0], outs[1], list(outs[2:2 + nt]), list(outs[2 + nt:2 + 2 * nt]), outs[-1]


def _late_gather_wait(send_sems, recv_sems, shards, lands, after):
    nt = len(shards)

    def body(*refs):
        w_refs, land_refs = refs[:nt], refs[nt:2 * nt]
        s_sems, r_sems = refs[2 * nt], refs[2 * nt + 1]
        x, y, c, chips = _place()
        for t in range(nt):
            for j, (cx, cy) in enumerate(chips):
                cp = pltpu.make_async_remote_copy(
                    src_ref=w_refs[t], dst_ref=land_refs[t].at[2 * cx + cy], send_sem=s_sems.at[3 * t + j],
                    recv_sem=r_sems.at[3 * t + j], device_id=(cx, cy, c), device_id_type=MESH)
                cp.wait_send()
                cp.wait_recv()

    outs = pl.pallas_call(
        body, name="late_gather_wait",
        out_shape=(*[pltpu.HBM(s.shape, s.dtype) for s in shards], *[pltpu.HBM(l.shape, l.dtype) for l in lands]),
        in_specs=[HBM] * (2 * nt) + [SEM, SEM, ANY], out_specs=tuple([HBM] * (2 * nt)),
        input_output_aliases={i: i for i in range(2 * nt)},
        compiler_params=pltpu.CompilerParams(has_side_effects=SPLIT_COPY),
    )(*shards, *lands, send_sems, recv_sems, after)
    return list(outs[nt:])


def _grad_pair_in(grads):
    nt = len(grads)

    def body(*refs):
        g_refs, got_refs = refs[:nt], refs[nt:2 * nt]
        send_sems, recv_sems = refs[2 * nt:]
        x, y, c, _ = _place()
        sibling = (x, y, 1 - c)

        def copy(t, src, dst):
            return pltpu.make_async_remote_copy(src_ref=src, dst_ref=dst, send_sem=send_sems.at[t],
                                                recv_sem=recv_sems.at[t], device_id=sibling, device_id_type=MESH)

        for t in range(nt):
            rh = grads[t].shape[1] // 2
            for p in range(N_CHIPS):
                for off, n in _pieces(rh):
                    copy(t, g_refs[t].at[p, pl.ds((1 - c) * rh + off, n), :], got_refs[t].at[p, pl.ds(off, n), :]).start()
        for t in range(nt):
            rh = grads[t].shape[1] // 2
            copy(t, g_refs[t].at[:, pl.ds((1 - c) * rh, rh), :], got_refs[t]).wait()

    return pl.pallas_call(
        body, name="grad_pair_in",
        out_shape=[jax.ShapeDtypeStruct((N_CHIPS, g.shape[1] // 2, g.shape[2]), g.dtype) for g in grads],
        in_specs=[HBM] * nt, out_specs=[HBM] * nt,
        scratch_shapes=[pltpu.SemaphoreType.DMA((nt,)), pltpu.SemaphoreType.DMA((nt,))],
    )(*grads)


def _pair_sum(g, got, core, name):
    _, rows, cols = g.shape
    rh = rows // 2
    tr = _tile(rh, 512)
    nb = rh // tr

    def body(c_ref, g_ref, got_ref, s32_ref, s16_ref):
        s = g_ref[...] + got_ref[...]
        s32_ref[...] = s
        s16_ref[...] = s.astype(s16_ref.dtype)

    blk = pl.BlockSpec((None, tr, cols), lambda p, i, c_ref: (p, i, 0))
    return pl.pallas_call(
        body, name=name,
        grid_spec=pltpu.PrefetchScalarGridSpec(
            num_scalar_prefetch=1, grid=(N_CHIPS, nb),
            in_specs=[pl.BlockSpec((None, tr, cols), lambda p, i, c_ref: (p, c_ref[0] * nb + i, 0)), blk],
            out_specs=[blk, blk]),
        out_shape=[jax.ShapeDtypeStruct((N_CHIPS, rh, cols), jnp.float32),
                   jax.ShapeDtypeStruct((N_CHIPS, rh, cols), jnp.bfloat16)],
        compiler_params=_params("parallel", "parallel"),
    )(core, g, got)


def _grad_chip_exchange(parts):
    nt = len(parts)

    def body(*refs):
        a_refs, got_refs = refs[:nt], refs[nt:2 * nt]
        send_sems, recv_sems = refs[2 * nt:]
        x, y, c, chips = _place()
        sends = [pltpu.make_async_remote_copy(
            src_ref=a_refs[t].at[2 * cx + cy], dst_ref=got_refs[t].at[j], send_sem=send_sems.at[3 * t + j],
            recv_sem=recv_sems.at[3 * t + j], device_id=(cx, cy, c), device_id_type=MESH)
            for t in range(nt) for j, (cx, cy) in enumerate(chips)]
        for cp in sends:
            cp.start()
        for cp in sends:
            cp.wait_recv()
        for cp in sends:
            cp.wait_send()

    return pl.pallas_call(
        body, name="grad_chip_exchange",
        out_shape=[jax.ShapeDtypeStruct((N_CHIPS - 1,) + a.shape[1:], a.dtype) for a in parts],
        in_specs=[HBM] * nt, out_specs=[HBM] * nt,
        scratch_shapes=[pltpu.SemaphoreType.DMA((3 * nt,)), pltpu.SemaphoreType.DMA((3 * nt,))],
    )(*parts)


def _exchange_start(parts):
    nt = len(parts)

    def body(*refs):
        a_refs, land_refs = refs[:nt], refs[nt:2 * nt]
        send_sems, recv_sems, token = refs[2 * nt], refs[2 * nt + 1], refs[-1]
        x, y, c, chips = _place()
        for t in range(nt):
            for j, (cx, cy) in enumerate(chips):
                pltpu.make_async_remote_copy(
                    src_ref=a_refs[t].at[2 * cx + cy], dst_ref=land_refs[t].at[j], send_sem=send_sems.at[3 * t + j],
                    recv_sem=recv_sems.at[3 * t + j], device_id=(cx, cy, c), device_id_type=MESH).start()
        token[...] = jnp.zeros_like(token)

    hbm = lambda a: pltpu.with_memory_space_constraint(a, pltpu.HBM)
    lands = [lax.empty((N_CHIPS - 1,) + a.shape[1:], a.dtype) for a in parts]
    outs = pl.pallas_call(
        body, name="grad_exchange_start",
        out_shape=(pltpu.SemaphoreType.DMA((3 * nt,)), pltpu.SemaphoreType.DMA((3 * nt,)),
                   *[pltpu.HBM(a.shape, a.dtype) for a in parts], *[pltpu.HBM(l.shape, l.dtype) for l in lands],
                   jax.ShapeDtypeStruct((8, LANES), jnp.float32)),
        in_specs=[HBM] * (2 * nt),
        out_specs=(SEM, SEM, *([HBM] * (2 * nt)), pl.BlockSpec(memory_space=pltpu.VMEM)),
        input_output_aliases={i: 2 + i for i in range(2 * nt)},
        compiler_params=pltpu.CompilerParams(has_side_effects=SPLIT_COPY),
    )(*[hbm(a) for a in parts], *[hbm(l) for l in lands])
    return outs[0], outs[1], list(outs[2:2 + nt]), list(outs[2 + nt:2 + 2 * nt]), outs[-1]


def _exchange_wait(send_sems, recv_sems, parts, lands, after):
    nt = len(parts)

    def body(*refs):
        a_refs, land_refs = refs[:nt], refs[nt:2 * nt]
        s_sems, r_sems = refs[2 * nt], refs[2 * nt + 1]
        x, y, c, chips = _place()
        for t in range(nt):
            for j, (cx, cy) in enumerate(chips):
                cp = pltpu.make_async_remote_copy(
                    src_ref=a_refs[t].at[2 * cx + cy], dst_ref=land_refs[t].at[j], send_sem=s_sems.at[3 * t + j],
                    recv_sem=r_sems.at[3 * t + j], device_id=(cx, cy, c), device_id_type=MESH)
                cp.wait_send()
                cp.wait_recv()

    outs = pl.pallas_call(
        body, name="grad_exchange_wait",
        out_shape=(*[pltpu.HBM(a.shape, a.dtype) for a in parts], *[pltpu.HBM(l.shape, l.dtype) for l in lands]),
        in_specs=[HBM] * (2 * nt) + [SEM, SEM, ANY], out_specs=tuple([HBM] * (2 * nt)),
        input_output_aliases={i: i for i in range(2 * nt)},
        compiler_params=pltpu.CompilerParams(has_side_effects=SPLIT_COPY),
    )(*parts, *lands, send_sems, recv_sems, after)
    return list(outs[nt:])


def _chip_sum(s32, got, chip, name):
    _, rh, cols = s32.shape
    tr = _tile(rh, 512)

    def body(p_ref, s_ref, got_ref, o_ref):
        acc = s_ref[...]
        for j in range(N_CHIPS - 1):
            acc = acc + got_ref[j].astype(jnp.float32)
        o_ref[...] = acc

    return pl.pallas_call(
        body, name=name,
        grid_spec=pltpu.PrefetchScalarGridSpec(
            num_scalar_prefetch=1, grid=(rh // tr,),
            in_specs=[pl.BlockSpec((None, tr, cols), lambda i, p_ref: (p_ref[0], i, 0)),
                      pl.BlockSpec((N_CHIPS - 1, tr, cols), lambda i, p_ref: (0, i, 0))],
            out_specs=pl.BlockSpec((tr, cols), lambda i, p_ref: (i, 0))),
        out_shape=jax.ShapeDtypeStruct((rh, cols), jnp.float32),
        compiler_params=_params("parallel"),
    )(chip, s32, got)


def _grad_pair_out(halves):
    nt = len(halves)

    def body(*refs):
        h_refs, got_refs = refs[:nt], refs[nt:2 * nt]
        send_sems, recv_sems = refs[2 * nt:]
        x, y, c, _ = _place()
        sibling = (x, y, 1 - c)

        def copy(t, src, dst):
            return pltpu.make_async_remote_copy(src_ref=src, dst_ref=dst, send_sem=send_sems.at[t],
                                                recv_sem=recv_sems.at[t], device_id=sibling, device_id_type=MESH)

        for t in range(nt):
            for off, n in _pieces(halves[t].shape[0]):
                copy(t, h_refs[t].at[pl.ds(off, n), :], got_refs[t].at[pl.ds(off, n), :]).start()
        for t in range(nt):
            copy(t, h_refs[t], got_refs[t]).wait()

    return pl.pallas_call(
        body, name="grad_pair_out",
        out_shape=[jax.ShapeDtypeStruct(h.shape, h.dtype) for h in halves],
        in_specs=[HBM] * nt, out_specs=[HBM] * nt,
        scratch_shapes=[pltpu.SemaphoreType.DMA((nt,)), pltpu.SemaphoreType.DMA((nt,))],
    )(*halves)


def _ada_part(c_all, w_ada):
    L, _, ncol = w_ada.shape
    tn = _tile(ncol, 512)

    def body(c_ref, w_ref, cond_ref, part_ref):
        cv = c_ref[...]
        cond = cv * jax.nn.sigmoid(cv)
        cond_ref[...] = cond
        part_ref[0] = jnp.dot(cond, w_ref[0], precision=lax.Precision.HIGHEST, preferred_element_type=jnp.float32)

    return pl.pallas_call(
        body, name="ada_part", grid=(L, ncol // tn),
        in_specs=[_full((N_DEV, D)), pl.BlockSpec((1, D, tn), lambda l, j: (l, 0, j))],
        out_specs=[_full((N_DEV, D)), pl.BlockSpec((1, N_DEV, tn), lambda l, j: (l, 0, j))],
        out_shape=[jax.ShapeDtypeStruct((N_DEV, D), jnp.float32), jax.ShapeDtypeStruct((L, N_DEV, ncol), jnp.float32)],
        compiler_params=_params("arbitrary", "arbitrary"),
    )(c_all, w_ada)


def _adamw_math(w, g, m, v):
    m = ADAM_B1 * m + (1.0 - ADAM_B1) * g
    v = ADAM_B2 * v + (1.0 - ADAM_B2) * jnp.square(g)
    m_hat = m / (1.0 - ADAM_B1 ** ADAM_STEP)
    v_hat = v / (1.0 - ADAM_B2 ** ADAM_STEP)
    delta = -ADAM_LR * (m_hat / (jnp.sqrt(v_hat) + ADAM_EPS) + ADAM_WD * w)
    return delta, m, v


def _adamw(w, g, m, v, name):
    shape = w.shape
    cols = shape[-1]
    rows = math.prod(shape[:-1])
    tr = _tile(rows, 512)
    two_d = lambda t: t.reshape(rows, cols)

    def body(w_ref, g_ref, m_ref, v_ref, d_ref, mo_ref, vo_ref):
        d_ref[...], mo_ref[...], vo_ref[...] = _adamw_math(w_ref[...], g_ref[...], m_ref[...], v_ref[...])

    out = jax.ShapeDtypeStruct((rows, cols), jnp.float32)
    outs = pl.pallas_call(
        body, name=name, grid=(rows // tr,), in_specs=[_rows(tr, cols)] * 4, out_specs=[_rows(tr, cols)] * 3,
        out_shape=[out, out, out], compiler_params=_params("parallel"),
    )(two_d(w), two_d(g), two_d(m), two_d(v))
    return [t.reshape(shape) for t in outs]


def _adamw_halves(w, mine, got, m, v, core, name):
    shape = w.shape
    cols = shape[-1]
    rows = math.prod(shape[:-1])
    rh = rows // 2
    tr = _tile(rh, 512)
    nbh = rh // tr
    two_d = lambda t: t.reshape(rows, cols)

    def body(c_ref, w_ref, a_ref, b_ref, m_ref, v_ref, g_ref, d_ref, mo_ref, vo_ref):
        g = jnp.where(pl.program_id(0) // nbh == c_ref[0], a_ref[...], b_ref[...])
        g_ref[...] = g
        d_ref[...], mo_ref[...], vo_ref[...] = _adamw_math(w_ref[...], g, m_ref[...], v_ref[...])

    row = pl.BlockSpec((tr, cols), lambda i, c_ref: (i, 0))

    def half(keep):
        return pl.BlockSpec((tr, cols), lambda i, c_ref: (jnp.where((i // nbh == c_ref[0]) == keep, i % nbh, 0), 0))

    out = jax.ShapeDtypeStruct((rows, cols), jnp.float32)
    outs = pl.pallas_call(
        body, name=name,
        grid_spec=pltpu.PrefetchScalarGridSpec(
            num_scalar_prefetch=1, grid=(rows // tr,),
            in_specs=[row, half(True), half(False), row, row], out_specs=[row] * 4),
        out_shape=[out] * 4, compiler_params=_params("arbitrary"),
    )(core, two_d(w), mine, got, two_d(m), two_d(v))
    return [t.reshape(shape) for t in outs]


def _ada_grad_adamw(cond_t, dm, w, m, v):
    L, _, ncol = w.shape
    tn = _tile(ncol, 512)

    def body(ct_ref, dm_ref, w_ref, m_ref, v_ref, g_ref, d_ref, mo_ref, vo_ref):
        g = ct_ref[:, 0:1] * dm_ref[0, 0:1, :]
        for b in range(1, N_DEV):
            g = g + ct_ref[:, b:b + 1] * dm_ref[0, b:b + 1, :]
        g_ref[0] = g
        d_ref[0], mo_ref[0], vo_ref[0] = _adamw_math(w_ref[0], g, m_ref[0], v_ref[0])

    wblk = pl.BlockSpec((1, D, tn), lambda l, j: (l, 0, j))
    out = jax.ShapeDtypeStruct(w.shape, jnp.float32)
    return pl.pallas_call(
        body, name="ada_grad_adamw", grid=(L, ncol // tn),
        in_specs=[_full((D, N_DEV)), pl.BlockSpec((1, N_DEV, tn), lambda l, j: (l, 0, j)), wblk, wblk, wblk],
        out_specs=[wblk] * 4, out_shape=[out] * 4, compiler_params=_params("parallel", "parallel"),
    )(cond_t, dm, w, m, v)


def _small_adamw(gathered, w, m, v):
    def body(ga_ref, w_ref, m_ref, v_ref, g_ref, d_ref, mo_ref, vo_ref):
        g = ga_ref[0]
        for dev in range(1, N_DEV):
            g = g + ga_ref[dev]
        g_ref[...] = g
        d_ref[...], mo_ref[...], vo_ref[...] = _adamw_math(w_ref[...], g, m_ref[...], v_ref[...])

    out = jax.ShapeDtypeStruct((SMALL_ROWS, LANES), jnp.float32)
    return pl.pallas_call(
        body, name="small_adamw", out_shape=[out] * 4,
        in_specs=[pl.BlockSpec(memory_space=pltpu.VMEM)] * 4, out_specs=[pl.BlockSpec(memory_space=pltpu.VMEM)] * 4,
    )(gathered, w, m, v)


def _one_hot_pick(arr, index, axis):
    n = arr.shape[axis]
    shape = [1] * arr.ndim
    shape[axis] = n
    hot = (jnp.arange(n) == index).astype(arr.dtype).reshape(shape)
    return jnp.sum(arr * hot, axis=axis)


def kernel(x, c, positions, w_ada, b_ada, g_mix, g_mlp, mla_w_dq, mla_g_q, mla_w_uq, mla_w_dkv, mla_g_kv, mla_w_ukv, mla_w_o, swa_w_qkv, swa_b_qkv, swa_sinks, swa_w_o, swa_b_o, w_ff1, w_ff2, g_final, loss_target, m_w_ada, m_b_ada, m_g_mix, m_g_mlp, m_mla_w_dq, m_mla_g_q, m_mla_w_uq, m_mla_w_dkv, m_mla_g_kv, m_mla_w_ukv, m_mla_w_o, m_swa_w_qkv, m_swa_b_qkv, m_swa_sinks, m_swa_w_o, m_swa_b_o, m_w_ff1, m_w_ff2, m_g_final, v_w_ada, v_b_ada, v_g_mix, v_g_mlp, v_mla_w_dq, v_mla_g_q, v_mla_w_uq, v_mla_w_dkv, v_mla_g_kv, v_mla_w_ukv, v_mla_w_o, v_swa_w_qkv, v_swa_b_qkv, v_swa_sinks, v_swa_w_o, v_swa_b_o, v_w_ff1, v_w_ff2, v_g_final):
    W = dict(w_ada=w_ada, b_ada=b_ada, g_mix=g_mix, g_mlp=g_mlp, mla_w_dq=mla_w_dq, mla_g_q=mla_g_q, mla_w_uq=mla_w_uq,
             mla_w_dkv=mla_w_dkv, mla_g_kv=mla_g_kv, mla_w_ukv=mla_w_ukv, mla_w_o=mla_w_o, swa_w_qkv=swa_w_qkv,
             swa_b_qkv=swa_b_qkv, swa_sinks=swa_sinks, swa_w_o=swa_w_o, swa_b_o=swa_b_o, w_ff1=w_ff1, w_ff2=w_ff2,
             g_final=g_final)
    M = dict(w_ada=m_w_ada, b_ada=m_b_ada, g_mix=m_g_mix, g_mlp=m_g_mlp, mla_w_dq=m_mla_w_dq, mla_g_q=m_mla_g_q,
             mla_w_uq=m_mla_w_uq, mla_w_dkv=m_mla_w_dkv, mla_g_kv=m_mla_g_kv, mla_w_ukv=m_mla_w_ukv, mla_w_o=m_mla_w_o,
             swa_w_qkv=m_swa_w_qkv, swa_b_qkv=m_swa_b_qkv, swa_sinks=m_swa_sinks, swa_w_o=m_swa_w_o, swa_b_o=m_swa_b_o,
             w_ff1=m_w_ff1, w_ff2=m_w_ff2, g_final=m_g_final)
    V = dict(w_ada=v_w_ada, b_ada=v_b_ada, g_mix=v_g_mix, g_mlp=v_g_mlp, mla_w_dq=v_mla_w_dq, mla_g_q=v_mla_g_q,
             mla_w_uq=v_mla_w_uq, mla_w_dkv=v_mla_w_dkv, mla_g_kv=v_mla_g_kv, mla_w_ukv=v_mla_w_ukv, mla_w_o=v_mla_w_o,
             swa_w_qkv=v_swa_w_qkv, swa_b_qkv=v_swa_b_qkv, swa_sinks=v_swa_sinks, swa_w_o=v_swa_w_o, swa_b_o=v_swa_b_o,
             w_ff1=v_w_ff1, w_ff2=v_w_ff2, g_final=v_g_final)
    order = list(W)
    names = list(SHARDED)
    core = lax.axis_index("c")
    chip = 2 * lax.axis_index("x") + lax.axis_index("y")
    dev = 2 * chip + core
    core_arr = core.astype(jnp.int32).reshape(1)
    chip_arr = chip.astype(jnp.int32).reshape(1)

    def whole(n, g, own):
        g = lax.dynamic_update_slice(g, own[None], (chip, 0, 0))
        if n in ("w_ff1", "w_ff2"):
            return g
        if n in COL_SPLIT:
            return g.transpose(1, 0, 2).reshape(g.shape[1], N_CHIPS * g.shape[2])
        return g.reshape(N_CHIPS * g.shape[1], g.shape[2])

    early = [n for n in names if n.startswith("mla_")]
    local = {n: W[n].astype(MXU_DTYPE).reshape(_view2d(n)) for n in early}
    wts = {n: whole(n, g, local[n]) for n, g in zip(early, _weight_gather([local[n] for n in early]))}

    nbq, nbo = BIASES["swa_b_qkv"] // N_CHIPS, BIASES["swa_b_o"] // N_CHIPS
    first = jnp.concatenate([c.reshape(-1), swa_b_qkv.reshape(-1), swa_b_o.reshape(-1),
                             jnp.zeros((16 * LANES - D - nbq - nbo,), jnp.float32)]).reshape(16, LANES)
    first_all = _all_gather(first).reshape(N_DEV, 16 * LANES)
    c_all = first_all[:, :D]
    south = first_all[0::2]
    wts["swa_b_qkv"] = south[:, D:D + nbq].reshape(1, N_CHIPS * nbq)
    wts["swa_b_o"] = south[:, D + nbq:D + nbq + nbo].reshape(1, N_CHIPS * nbo)
    cond_all, part = _ada_part(c_all, w_ada)
    ncol = w_ada.shape[2]
    part_all = _all_gather(part.reshape(-1, LANES)).reshape(N_DEV, DEPTH, N_DEV, ncol)
    mine = _one_hot_pick(part_all[0::2], dev, axis=2)
    mod = mine.transpose(1, 0, 2).reshape(DEPTH, N_CHIPS * ncol) + b_ada
    vecs = jnp.concatenate([mod.reshape(DEPTH, 6, D), g_mix[:, None, :], g_mlp[:, None, :]], axis=1)

    late = [("w_ff1", 0), ("w_ff2", 0), ("swa_w_qkv", None), ("swa_w_o", None), ("w_ff1", 1), ("w_ff2", 1)]
    late_local = [(W[n][0] if l is None else W[n][l]).astype(MXU_DTYPE) for n, l in late]
    send_sems, recv_sems, passed, lands, token = _late_gather_start(late_local, [vecs] + [wts[n] for n in early])

    def late_weights(after):
        got = _late_gather_wait(send_sems, recv_sems, passed, lands, after)
        out = {"w_ff1": [None] * DEPTH, "w_ff2": [None] * DEPTH}
        for (n, l), g, own in zip(late, got, late_local):
            if l is None:
                out[n] = whole(n, g, own)
            else:
                out[n][l] = whole(n, g, own)
        return out

    late_names = [n for n in names if n not in early]
    reduce_state = {}

    def on_late_grads(late_grads):
        gl = [late_grads[n] for n in late_names]
        got = _grad_pair_in(gl)
        sums = [_pair_sum(g, s, core_arr, "pair_sum_" + n) for n, g, s in zip(late_names, gl, got)]
        s_sems, r_sems, parts, zones, tok = _exchange_start([s16 for _, s16 in sums])
        reduce_state.update(sums=sums, split=(s_sems, r_sems, parts, zones))
        return tok

    grad_x, grads, small = _sequence_step(
        x[0], loss_target[0], positions[0], vecs, mla_g_q + token[0, 0], mla_g_kv, swa_sinks, g_final, wts,
        late_weights, on_late_grads)

    small["b_ada"] = small.pop("dmod")
    small_all = _all_gather(_pack_small(small))
    pk = lambda src: _pack_small({n: src[n] for n in SMALL if n != "loss" and n not in BIASES})
    g_small, d_small, m_small, v_small = [_unpack_small(t) for t in _small_adamw(small_all, pk(W), pk(M), pk(V))]
    off, n = _small_slots()["b_ada"]
    dmod_all = small_all.reshape(N_DEV, -1)[:, off:off + n].reshape(N_DEV, DEPTH, N_CHIPS, ncol)
    dm = _one_hot_pick(dmod_all, chip, axis=2).transpose(1, 0, 2)
    ada = _ada_grad_adamw(cond_all.T, dm, w_ada, m_w_ada, v_w_ada)

    gl = [grads[n] for n in early]
    got = _grad_pair_in(gl)
    sums = [_pair_sum(g, s, core_arr, "pair_sum_" + n) for n, g, s in zip(early, gl, got)]
    others = _grad_chip_exchange([s16 for _, s16 in sums])
    late_others = _exchange_wait(*reduce_state["split"], grad_x)
    sums, others = list(sums) + list(reduce_state["sums"]), list(others) + list(late_others)
    halves = [_chip_sum(s32, o, chip_arr, "chip_sum_" + n) for n, (s32, _), o in zip(names, sums, others)]
    sibling_halves = _grad_pair_out(halves)

    res = {"w_ada": ada}
    for n, mine_h, got_h in zip(names, halves, sibling_halves):
        res[n] = _adamw_halves(W[n], mine_h, got_h, M[n], V[n], core_arr, "adamw_" + n)
    for n, width in BIASES.items():
        g = _one_hot_pick(g_small[n].reshape(N_CHIPS, width // N_CHIPS), chip, axis=0).reshape(1, -1)
        res[n] = [g] + _adamw(W[n], g, M[n], V[n], "adamw_" + n)
    for name in order:
        if name not in res:
            res[name] = [t[name] for t in (g_small, d_small, m_small, v_small)]
    outs = [g_small["loss"], grad_x[None]]
    for k in range(4):
        outs += [res[name][k] for name in order]
    return tuple(outs)
```

```python
import functools
import math

import jax
import jax.numpy as jnp
import numpy as np
from jax import lax
from jax.experimental import pallas as pl
from jax.experimental.pallas import tpu as pltpu

D = 1024
DEPTH = 2
MLA_HEADS = 8
QK_NOPE = 128
QK_ROPE = 64
V_DIM = 128
Q_LORA = 384
KV_LORA = 256
ROPE_THETA = 10000.0
SWA_HEADS = 16
SWA_KV_HEADS = 4
SWA_HEAD_DIM = 64
SWA_GROUP = SWA_HEADS // SWA_KV_HEADS
WINDOW = 128
D_FF = 4 * D
EPS = 1e-6
ADAM_LR = 0.001
ADAM_B1 = 0.9
ADAM_B2 = 0.999
ADAM_EPS = 1e-08
ADAM_WD = 0.01
ADAM_STEP = 10

N_CHIPS = 4
N_DEV = 8
LANES = 128
QK_EXT = 256
MLA_SCALE = (QK_NOPE + QK_ROPE) ** -0.5
LOG2E = math.log2(math.e)
LN2 = math.log(2.0)
MLA_QSCALE = MLA_SCALE * LOG2E
ATTN_BLOCK = 1024
ATTN_SUB = 512
MLP_FWD_TILE = (1024, 512)
MLP_BWD_TILE = (512, 1024)
DW_TOKENS = 2048
SWA_SCALE = SWA_HEAD_DIM ** -0.5
NEG = -1e30
MXU_DTYPE = jnp.bfloat16
VMEM_LIMIT = 56 * 1024 * 1024

R_SH1, R_SC1, R_GT1, R_SH2, R_SC2, R_GT2, R_GMIX, R_GMLP = range(8)
R_BO = 6


def _tile(n, pref):
    if n <= pref:
        return n
    for t in range(pref, 7, -1):
        if n % t == 0 and t % 8 == 0:
            return t
    return n


def _dot(a, b):
    return jnp.dot(a, b, preferred_element_type=jnp.float32)


def _dot_nt(a, b):
    return lax.dot_general(a, b, (((1,), (1,)), ((), ())), preferred_element_type=jnp.float32)


def _dot_tn(a, b):
    return lax.dot_general(a, b, (((0,), (0,)), ((), ())), preferred_element_type=jnp.float32)


def _rms(x):
    r = lax.rsqrt(jnp.mean(x * x, axis=-1, keepdims=True) + EPS)
    return x * r, r


def _rms_bwd(dxhat, xhat, r):
    return r * (dxhat - xhat * jnp.mean(dxhat * xhat, axis=-1, keepdims=True))


def _rowsum(v):
    return jnp.sum(v, axis=0, keepdims=True)


def _params(*sem):
    return pltpu.CompilerParams(dimension_semantics=sem, vmem_limit_bytes=VMEM_LIMIT)


def _full(shape):
    nd = len(shape)
    return pl.BlockSpec(shape, lambda *_: (0,) * nd)


def _rows(tm, cols):
    return pl.BlockSpec((tm, cols), lambda i, *_: (i, 0))


def _modulate_bwd(dh, x, vec_ref, r_g, r_sc, r_sh, ps_ref, dres):
    xhat, r = _rms(x)
    g = vec_ref[r_g:r_g + 1, :]
    n = xhat * g
    ps_ref[r_sh:r_sh + 1, :] += _rowsum(dh)
    ps_ref[r_sc:r_sc + 1, :] += _rowsum(dh * n)
    dn = dh * (1.0 + vec_ref[r_sc:r_sc + 1, :])
    ps_ref[r_g:r_g + 1, :] += _rowsum(dn * xhat)
    return dres + _rms_bwd(dn * g, xhat, r)


def _mla_pre(x, vec, wcat, g_q, g_kv, wuq, wukv, cs):
    T = x.shape[0]
    tm = _tile(T, 512)
    H = MLA_HEADS

    def body(x_ref, vec_ref, wcat_ref, gq_ref, gkv_ref, wuq_ref, wukv_ref, cs_ref, h_ref, z_ref, q_ref, k_ref, v_ref):
        xhat, _ = _rms(x_ref[...])
        h = xhat * vec_ref[R_GMIX:R_GMIX + 1, :] * (1.0 + vec_ref[R_SC1:R_SC1 + 1, :]) + vec_ref[R_SH1:R_SH1 + 1, :]
        hb = h.astype(MXU_DTYPE)
        h_ref[...] = hb
        z = _dot(hb, wcat_ref[...])
        z_ref[...] = z
        cq = (_rms(z[:, :Q_LORA])[0] * gq_ref[...]).astype(MXU_DTYPE)
        ckv = (_rms(z[:, Q_LORA:Q_LORA + KV_LORA])[0] * gkv_ref[...]).astype(MXU_DTYPE)
        cs_t = cs_ref[...]
        t = z[:, Q_LORA + KV_LORA:] * cs_t
        k_rope = (t + pltpu.roll(t, QK_ROPE, axis=1)).astype(MXU_DTYPE)
        low = lax.broadcasted_iota(jnp.int32, (1, LANES), 1) < QK_ROPE
        for hd in range(H):
            qf = _dot(cq, wuq_ref[hd])
            tq = qf[:, QK_NOPE:] * cs_t
            tq = tq + pltpu.roll(tq, QK_ROPE, axis=1)
            q_ref[hd, :, :QK_NOPE] = (qf[:, :QK_NOPE] * MLA_QSCALE).astype(MXU_DTYPE)
            q_ref[hd, :, QK_NOPE:] = jnp.where(low, tq * MLA_QSCALE, 0.0).astype(MXU_DTYPE)
            kvf = _dot(ckv, wukv_ref[hd])
            k_ref[hd, :, :QK_NOPE] = kvf[:, :QK_NOPE].astype(MXU_DTYPE)
            k_ref[hd, :, QK_NOPE:] = k_rope
            v_ref[hd] = kvf[:, QK_NOPE:].astype(MXU_DTYPE)

    zc = wcat.shape[1]
    return pl.pallas_call(
        body, name="mla_pre", grid=(T // tm,),
        in_specs=[_rows(tm, D), _full((8, D)), _full(wcat.shape), _full(g_q.shape), _full(g_kv.shape),
                  _full(wuq.shape), _full(wukv.shape), _rows(tm, LANES)],
        out_specs=[_rows(tm, D), _rows(tm, zc),
                   pl.BlockSpec((H, tm, QK_EXT), lambda i: (0, i, 0)),
                   pl.BlockSpec((H, tm, QK_EXT), lambda i: (0, i, 0)),
                   pl.BlockSpec((H, tm, V_DIM), lambda i: (0, i, 0))],
        out_shape=[jax.ShapeDtypeStruct((T, D), MXU_DTYPE), jax.ShapeDtypeStruct((T, zc), jnp.float32),
                   jax.ShapeDtypeStruct((H, T, QK_EXT), MXU_DTYPE), jax.ShapeDtypeStruct((H, T, QK_EXT), MXU_DTYPE),
                   jax.ShapeDtypeStruct((H, T, V_DIM), MXU_DTYPE)],
        compiler_params=_params("parallel"),
    )(x, vec, wcat, g_q, g_kv, wuq, wukv, cs)


def _mla_attn_fwd(q, k, v):
    H, T, _ = q.shape
    tb = _tile(T, ATTN_BLOCK)
    sub = min(ATTN_SUB, tb)
    ns, nb = tb // sub, T // tb

    def body(q_ref, k_ref, v_ref, o_ref, lse_ref, m_sc, l_sc, acc_sc):
        qi, kj = pl.program_id(1), pl.program_id(2)

        @pl.when(kj == 0)
        def _():
            m_sc[...] = jnp.full_like(m_sc, NEG)
            l_sc[...] = jnp.zeros_like(l_sc)
            acc_sc[...] = jnp.zeros_like(acc_sc)

        def update(r, kk, masked):
            rows, keys = pl.ds(r * sub, sub), pl.ds(kk * sub, sub)
            s = _dot_nt(q_ref[0, rows, :], k_ref[0, keys, :])
            if masked:
                row = lax.broadcasted_iota(jnp.int32, (sub, sub), 0)
                col = lax.broadcasted_iota(jnp.int32, (sub, sub), 1)
                s = jnp.where(col <= row, s, NEG)
            m_prev = m_sc[rows, :]
            m_new = jnp.maximum(m_prev, jnp.max(s, axis=1, keepdims=True))
            alpha = jnp.exp2(m_prev - m_new)
            p = jnp.exp2(s - jnp.tile(m_new, (1, sub // LANES)))
            l_sc[rows, :] = alpha * l_sc[rows, :] + jnp.sum(p, axis=1, keepdims=True)
            acc_sc[rows, :] = alpha * acc_sc[rows, :] + _dot(p.astype(MXU_DTYPE), v_ref[0, keys, :])
            m_sc[rows, :] = m_new

        @pl.when(kj < qi)
        def _():
            for kk in range(ns):
                for r in range(ns):
                    update(r, kk, False)

        @pl.when(kj == qi)
        def _():
            for kk in range(ns):
                for r in range(kk, ns):
                    update(r, kk, r == kk)
            l = l_sc[...]
            o_ref[...] = (acc_sc[...] / l).astype(o_ref.dtype)
            lse = m_sc[...] + jnp.log2(l)
            pick = (lax.broadcasted_iota(jnp.int32, (8, LANES), 1) == 0).astype(jnp.float32)
            row = lax.dot_general(pick, lse, (((1,), (1,)), ((), ())), precision=lax.Precision.HIGHEST,
                                  preferred_element_type=jnp.float32)
            lse_ref[0] = row[0:1, :]

    kv_idx = lambda h, i, j: (h, jnp.minimum(i, j), 0)
    return pl.pallas_call(
        body, name="mla_attn_fwd", grid=(H, nb, nb),
        in_specs=[pl.BlockSpec((1, tb, QK_EXT), lambda h, i, j: (h, i, 0)),
                  pl.BlockSpec((1, tb, QK_EXT), kv_idx),
                  pl.BlockSpec((1, tb, V_DIM), kv_idx)],
        out_specs=[pl.BlockSpec((tb, V_DIM), lambda h, i, j: (i, h)),
                   pl.BlockSpec((1, 1, tb), lambda h, i, j: (h, 0, i))],
        out_shape=[jax.ShapeDtypeStruct((T, H * V_DIM), MXU_DTYPE), jax.ShapeDtypeStruct((H, 1, T), jnp.float32)],
        scratch_shapes=[pltpu.VMEM((tb, LANES), jnp.float32), pltpu.VMEM((tb, LANES), jnp.float32),
                        pltpu.VMEM((tb, V_DIM), jnp.float32)],
        compiler_params=_params("parallel", "parallel", "arbitrary"),
    )(q, k, v)


def _post_attn(o, x, w_o, bias, vec, o_transposed=False):
    T = x.shape[0]
    tm = _tile(T, 512)
    o_spec = pl.BlockSpec((D, tm), lambda i: (0, i)) if o_transposed else _rows(tm, D)

    def body(o_ref, x_ref, w_ref, b_ref, vec_ref, y_ref, xm_ref, h_ref):
        y = (_dot_tn if o_transposed else _dot)(o_ref[...], w_ref[...]) + b_ref[...]
        y_ref[...] = y.astype(y_ref.dtype)
        xm = x_ref[...] + vec_ref[R_GT1:R_GT1 + 1, :] * y
        xm_ref[...] = xm
        xhat, _ = _rms(xm)
        h = xhat * vec_ref[R_GMLP:R_GMLP + 1, :] * (1.0 + vec_ref[R_SC2:R_SC2 + 1, :]) + vec_ref[R_SH2:R_SH2 + 1, :]
        h_ref[...] = h.astype(h_ref.dtype)

    return pl.pallas_call(
        body, name="post_attn", grid=(T // tm,),
        in_specs=[o_spec, _rows(tm, D), _full((D, D)), _full((1, D)), _full((8, D))],
        out_specs=[_rows(tm, D), _rows(tm, D), _rows(tm, D)],
        out_shape=[jax.ShapeDtypeStruct((T, D), MXU_DTYPE), jax.ShapeDtypeStruct((T, D), jnp.float32),
                   jax.ShapeDtypeStruct((T, D), MXU_DTYPE)],
        compiler_params=_params("parallel"),
    )(o, x, w_o, bias, vec)


def _ff_specs(tf):
    per = D_FF // N_CHIPS // tf
    w1 = pl.BlockSpec((None, D, tf), lambda i, f: (f // per, 0, f % per))
    w2 = pl.BlockSpec((None, tf, D), lambda i, f: (f // per, f % per, 0))
    return w1, w2


def _mlp_fwd(h2, w1, w2, xm, vec):
    T = h2.shape[0]
    tm = _tile(T, MLP_FWD_TILE[0])
    tf = _tile(D_FF // N_CHIPS, MLP_FWD_TILE[1])
    nf = D_FF // tf
    w1_spec, w2_spec = _ff_specs(tf)

    def body(h_ref, w1_ref, w2_ref, xm_ref, vec_ref, a_ref, y_ref, xo_ref, acc):
        f = pl.program_id(1)

        @pl.when(f == 0)
        def _():
            acc[...] = jnp.zeros_like(acc)

        u = jnp.maximum(_dot(h_ref[...], w1_ref[...]), 0.0)
        ab = (u * u).astype(MXU_DTYPE)
        a_ref[...] = ab
        acc[...] += _dot(ab, w2_ref[...])

        @pl.when(f == nf - 1)
        def _():
            y = acc[...]
            y_ref[...] = y.astype(y_ref.dtype)
            xo_ref[...] = xm_ref[...] + vec_ref[R_GT2:R_GT2 + 1, :] * y

    return pl.pallas_call(
        body, name="mlp_fwd", grid=(T // tm, nf),
        in_specs=[_rows(tm, D), w1_spec, w2_spec, _rows(tm, D), _full((8, D))],
        out_specs=[pl.BlockSpec((tm, tf), lambda i, f: (i, f)), _rows(tm, D), _rows(tm, D)],
        out_shape=[jax.ShapeDtypeStruct((T, D_FF), MXU_DTYPE), jax.ShapeDtypeStruct((T, D), MXU_DTYPE),
                   jax.ShapeDtypeStruct((T, D), jnp.float32)],
        scratch_shapes=[pltpu.VMEM((tm, D), jnp.float32)],
        compiler_params=_params("parallel", "arbitrary"),
    )(h2, w1, w2, xm, vec)


def _swa_pre(x, vec, w_qkv, b_qkv):
    T = x.shape[0]
    tm = _tile(T, 512)
    nq = SWA_HEADS * SWA_HEAD_DIM
    nk = SWA_KV_HEADS * SWA_HEAD_DIM
    wq_t, w_kv = w_qkv[:, :nq].T, w_qkv[:, nq:]
    bq_col, b_kv = b_qkv[:, :nq].reshape(nq, 1), b_qkv[:, nq:]

    def body(x_ref, vec_ref, wq_ref, wkv_ref, bq_ref, bkv_ref, h_ref, qt_ref, k_ref, v_ref):
        xhat, _ = _rms(x_ref[...])
        h = xhat * vec_ref[R_GMIX:R_GMIX + 1, :] * (1.0 + vec_ref[R_SC1:R_SC1 + 1, :]) + vec_ref[R_SH1:R_SH1 + 1, :]
        hb = h.astype(MXU_DTYPE)
        h_ref[...] = hb
        qt_ref[...] = ((_dot_nt(wq_ref[...], hb) + bq_ref[...]) * SWA_SCALE).astype(MXU_DTYPE)
        kv = _dot(hb, wkv_ref[...]) + bkv_ref[...]
        k_ref[...] = kv[:, :nk].astype(MXU_DTYPE)
        v_ref[...] = kv[:, nk:].astype(MXU_DTYPE)

    return pl.pallas_call(
        body, name="swa_pre", grid=(T // tm,),
        in_specs=[_rows(tm, D), _full((8, D)), _full(wq_t.shape), _full(w_kv.shape), _full(bq_col.shape),
                  _full(b_kv.shape)],
        out_specs=[_rows(tm, D), pl.BlockSpec((nq, tm), lambda i: (0, i)), _rows(tm, nk), _rows(tm, nk)],
        out_shape=[jax.ShapeDtypeStruct((T, D), MXU_DTYPE), jax.ShapeDtypeStruct((nq, T), MXU_DTYPE),
                   jax.ShapeDtypeStruct((T, nk), MXU_DTYPE), jax.ShapeDtypeStruct((T, nk), MXU_DTYPE)],
        compiler_params=_params("parallel"),
    )(x, vec, wq_t, w_kv, bq_col, b_kv)


def _swa_bias():
    W = WINDOW
    slopes = 2.0 ** (-8.0 * np.arange(1, SWA_HEADS + 1) / SWA_HEADS)
    dist = W + np.arange(W)[None, :] - np.arange(2 * W)[:, None]
    inside = (dist >= 0) & (dist < W)
    bias = np.where(inside[None], -slopes[:, None, None] * dist[None].astype(np.float64), NEG)
    bias = bias.reshape(SWA_KV_HEADS, SWA_GROUP, 2 * W, W).transpose(0, 2, 1, 3)
    return jnp.asarray(bias.reshape(SWA_KV_HEADS, 2 * W, SWA_GROUP * W), jnp.float32)


def _swa_probs(n, kh, qt_ref, kp_ref, kc_ref, bias_ref, sink_ref):
    W, Dh, G = WINDOW, SWA_HEAD_DIM, SWA_GROUP
    qt = jnp.concatenate([qt_ref[(kh * G + g) * Dh:(kh * G + g + 1) * Dh, :] for g in range(G)], axis=1)
    kb = jnp.concatenate([kp_ref[:, kh * Dh:(kh + 1) * Dh], kc_ref[:, kh * Dh:(kh + 1) * Dh]], axis=0)
    s = _dot(kb, qt) + bias_ref[kh]
    key = lax.broadcasted_iota(jnp.int32, (2 * W, 1), 0)
    s = jnp.where((key >= W) | (n > 0), s, NEG)
    sink = sink_ref[kh]
    m = jnp.maximum(jnp.max(s, axis=0, keepdims=True), sink)
    p = jnp.exp(s - m)
    p_sink = jnp.exp(sink - m)
    inv = 1.0 / (jnp.sum(p, axis=0, keepdims=True) + p_sink)
    return qt, kb, p * inv, p_sink * inv


def _swa_attn_fwd(qt, k, v, bias, sink_rows):
    T = qt.shape[1]
    W, Dh, G, Hk = WINDOW, SWA_HEAD_DIM, SWA_GROUP, SWA_KV_HEADS
    nk = Hk * Dh

    def body(qt_ref, kp_ref, kc_ref, vp_ref, vc_ref, bias_ref, sink_ref, ot_ref):
        n = pl.program_id(0)
        for kh in range(Hk):
            _, _, pn, _ = _swa_probs(n, kh, qt_ref, kp_ref, kc_ref, bias_ref, sink_ref)
            vb = jnp.concatenate([vp_ref[:, kh * Dh:(kh + 1) * Dh], vc_ref[:, kh * Dh:(kh + 1) * Dh]], axis=0)
            ot = _dot_tn(vb, pn.astype(MXU_DTYPE))
            for g in range(G):
                ot_ref[(kh * G + g) * Dh:(kh * G + g + 1) * Dh, :] = ot[:, g * W:(g + 1) * W].astype(ot_ref.dtype)

    prev = lambda n: (jnp.maximum(n - 1, 0), 0)
    cur = lambda n: (n, 0)
    col = lambda n: (0, n)
    return pl.pallas_call(
        body, name="swa_attn_fwd", grid=(T // W,),
        in_specs=[pl.BlockSpec((D, W), col), pl.BlockSpec((W, nk), prev), pl.BlockSpec((W, nk), cur),
                  pl.BlockSpec((W, nk), prev), pl.BlockSpec((W, nk), cur), _full(bias.shape), _full(sink_rows.shape)],
        out_specs=pl.BlockSpec((D, W), col),
        out_shape=jax.ShapeDtypeStruct((D, T), MXU_DTYPE),
        compiler_params=_params("parallel"),
    )(qt, k, k, v, v, bias, sink_rows)


def _final_loss(x, tgt, g):
    T = x.shape[0]
    tm = _tile(T, 512)

    def body(x_ref, t_ref, g_ref, loss_ref, dx_ref, dg_ref):
        @pl.when(pl.program_id(0) == 0)
        def _():
            loss_ref[...] = jnp.zeros_like(loss_ref)
            dg_ref[...] = jnp.zeros_like(dg_ref)

        xhat, r = _rms(x_ref[...])
        gv = g_ref[...]
        e = xhat * gv - t_ref[...]
        loss_ref[...] += 0.5 * jnp.sum(jnp.mean(e * e, axis=-1, keepdims=True), axis=0, keepdims=True)
        dy = e * (1.0 / D)
        dg_ref[...] += _rowsum(dy * xhat)
        dx_ref[...] = _rms_bwd(dy * gv, xhat, r)

    return pl.pallas_call(
        body, name="final_loss", grid=(T // tm,),
        in_specs=[_rows(tm, D), _rows(tm, D), _full((1, D))],
        out_specs=[_full((8, LANES)), _rows(tm, D), _full((1, D))],
        out_shape=[jax.ShapeDtypeStruct((8, LANES), jnp.float32), jax.ShapeDtypeStruct((T, D), jnp.float32),
                   jax.ShapeDtypeStruct((1, D), jnp.float32)],
        compiler_params=_params("arbitrary"),
    )(x, tgt, g)


def _mlp_bwd(dxo, y2, a, w1, w2, xm, vec):
    T = dxo.shape[0]
    tm = _tile(T, MLP_BWD_TILE[0])
    tf = _tile(D_FF // N_CHIPS, MLP_BWD_TILE[1])
    nf = D_FF // tf
    w1_spec, w2_spec = _ff_specs(tf)

    def body(dxo_ref, y_ref, a_ref, w1_ref, w2_ref, xm_ref, vec_ref, du_ref, dy_ref, dxm_ref, ps_ref, dyb, acc):
        i, f = pl.program_id(0), pl.program_id(1)

        @pl.when((i == 0) & (f == 0))
        def _():
            ps_ref[...] = jnp.zeros_like(ps_ref)

        @pl.when(f == 0)
        def _():
            dxo_t = dxo_ref[...]
            d = (dxo_t * vec_ref[R_GT2:R_GT2 + 1, :]).astype(MXU_DTYPE)
            dyb[...] = d
            dy_ref[...] = d
            acc[...] = jnp.zeros_like(acc)
            ps_ref[R_GT2:R_GT2 + 1, :] += _rowsum(dxo_t * y_ref[...].astype(jnp.float32))

        da = _dot_nt(dyb[...], w2_ref[...])
        dub = (da * (2.0 * jnp.sqrt(a_ref[...].astype(jnp.float32)))).astype(MXU_DTYPE)
        du_ref[...] = dub
        acc[...] += _dot_nt(dub, w1_ref[...])

        @pl.when(f == nf - 1)
        def _():
            dxm_ref[...] = _modulate_bwd(acc[...], xm_ref[...], vec_ref, R_GMLP, R_SC2, R_SH2, ps_ref, dxo_ref[...])

    return pl.pallas_call(
        body, name="mlp_bwd", grid=(T // tm, nf),
        in_specs=[_rows(tm, D), _rows(tm, D), pl.BlockSpec((tm, tf), lambda i, f: (i, f)), w1_spec, w2_spec,
                  _rows(tm, D), _full((8, D))],
        out_specs=[pl.BlockSpec((tm, tf), lambda i, f: (i, f)), _rows(tm, D), _rows(tm, D), _full((8, D))],
        out_shape=[jax.ShapeDtypeStruct((T, D_FF), MXU_DTYPE), jax.ShapeDtypeStruct((T, D), MXU_DTYPE),
                   jax.ShapeDtypeStruct((T, D), jnp.float32), jax.ShapeDtypeStruct((8, D), jnp.float32)],
        scratch_shapes=[pltpu.VMEM((tm, D), MXU_DTYPE), pltpu.VMEM((tm, D), jnp.float32)],
        compiler_params=_params("arbitrary", "arbitrary"),
    )(dxo, y2, a, w1, w2, xm, vec)


def _mm_tn(a, g, name, split=None, layers=1, layer=0, into=None, a_transposed=False):
    K, T = a.shape if a_transposed else a.shape[::-1]
    N = g.shape[1]
    kq = K // N_CHIPS if split == "rows" else K
    nq = N // N_CHIPS if split == "cols" else N
    bk, bn, bt = _tile(kq, 1024), _tile(nq, 1024), _tile(T, DW_TOKENS)
    if nq % bn or bn % LANES:
        bn = nq
    kper, nper = kq // bk, nq // bn

    def body(*refs):
        a_ref, g_ref, o_ref = refs[0], refs[1], refs[-1]

        @pl.when(pl.program_id(2) == 0)
        def _():
            o_ref[...] = jnp.zeros_like(o_ref)

        o_ref[...] += (_dot if a_transposed else _dot_tn)(a_ref[...], g_ref[...])

    a_spec = pl.BlockSpec((bk, bt), lambda k, n, t: (k, t)) if a_transposed else pl.BlockSpec((bt, bk), lambda k, n, t: (t, k))
    in_specs = [a_spec, pl.BlockSpec((bt, bn), lambda k, n, t: (t, n))]
    args = [a, g]
    aliases = {}
    if split is None:
        out_spec = pl.BlockSpec((bk, bn), lambda k, n, t: (k, n))
        out_shape = jax.ShapeDtypeStruct((K, N), jnp.float32)
    else:
        if split == "cols":
            idx = lambda k, n, t: (n // nper, layer, k, n % nper)
        else:
            idx = lambda k, n, t: (k // kper, layer, k % kper, n)
        out_spec = pl.BlockSpec((None, None, bk, bn), idx)
        out_shape = jax.ShapeDtypeStruct((N_CHIPS, layers, kq, nq), jnp.float32)
        if into is not None:
            in_specs.append(pl.BlockSpec(memory_space=pl.ANY))
            args.append(into)
            aliases = {2: 0}
    return pl.pallas_call(
        body, name=name, grid=(K // bk, N // bn, T // bt), in_specs=in_specs, out_specs=out_spec, out_shape=out_shape,
        input_output_aliases=aliases, compiler_params=_params("parallel", "parallel", "arbitrary"),
    )(*args)


def _attn_out_bwd(dxm, y1, o, w_o, vec, with_delta):
    T = dxm.shape[0]
    tm = _tile(T, 512)
    H = MLA_HEADS

    def body(dxm_ref, y_ref, w_ref, vec_ref, *refs):
        o_ref = refs[0] if with_delta else None
        dy_ref, do_ref, ps_ref, *delta_ref = refs[1:] if with_delta else refs

        @pl.when(pl.program_id(0) == 0)
        def _():
            ps_ref[...] = jnp.zeros_like(ps_ref)

        dxm_t = dxm_ref[...]
        dy = dxm_t * vec_ref[R_GT1:R_GT1 + 1, :]
        ps_ref[R_GT1:R_GT1 + 1, :] += _rowsum(dxm_t * y_ref[...].astype(jnp.float32))
        ps_ref[R_BO:R_BO + 1, :] += _rowsum(dy)
        dyb = dy.astype(MXU_DTYPE)
        dy_ref[...] = dyb
        if not with_delta:
            do_ref[...] = _dot_nt(w_ref[...], dyb).astype(do_ref.dtype)
        else:
            do = _dot_nt(dyb, w_ref[...])
            do_ref[...] = do.astype(do_ref.dtype)
            of = o_ref[...].astype(jnp.float32)
            ones = jnp.ones((8, V_DIM), jnp.float32)
            for hd in range(H):
                sl = slice(hd * V_DIM, (hd + 1) * V_DIM)
                d = lax.dot_general(ones, do[:, sl] * of[:, sl], (((1,), (1,)), ((), ())),
                                    precision=lax.Precision.HIGHEST, preferred_element_type=jnp.float32)
                delta_ref[0][hd] = d[0:1, :]

    out_specs = [_rows(tm, D), _rows(tm, D), _full((8, D))]
    out_shape = [jax.ShapeDtypeStruct((T, D), MXU_DTYPE), jax.ShapeDtypeStruct((T, D), MXU_DTYPE),
                 jax.ShapeDtypeStruct((8, D), jnp.float32)]
    if not with_delta:
        out_specs[1] = pl.BlockSpec((D, tm), lambda i: (0, i))
        out_shape[1] = jax.ShapeDtypeStruct((D, T), MXU_DTYPE)
    if with_delta:
        out_specs.append(pl.BlockSpec((H, 1, tm), lambda i: (0, 0, i)))
        out_shape.append(jax.ShapeDtypeStruct((H, 1, T), jnp.float32))
    return pl.pallas_call(
        body, name="attn_out_bwd_mla" if with_delta else "attn_out_bwd_swa", grid=(T // tm,),
        in_specs=[_rows(tm, D), _rows(tm, D), _full((D, D)), _full((8, D))] + ([_rows(tm, D)] if with_delta else []),
        out_specs=out_specs, out_shape=out_shape,
        compiler_params=_params("arbitrary"),
    )(dxm, y1, w_o, vec, *([o] if with_delta else []))


def _mla_attn_bwd(q, k, v, do, lse, delta):
    H, T, _ = q.shape
    tb = _tile(T, ATTN_BLOCK)
    sub = min(ATTN_SUB, tb)
    ns, nb = tb // sub, T // tb

    def body(q_ref, k_ref, v_ref, do_ref, lse_ref, dl_ref, dq_ref, dk_ref, dv_ref, dk_acc, dv_acc):
        j, i = pl.program_id(1), pl.program_id(2)

        @pl.when((j == 0) & (i == 0))
        def _():
            dq_ref[...] = jnp.zeros_like(dq_ref)

        def update(kk, r, masked):
            keys, rows = pl.ds(kk * sub, sub), pl.ds(r * sub, sub)
            kb, qb, dob = k_ref[0, keys, :], q_ref[0, rows, :], do_ref[rows, :]
            st = _dot_nt(kb, qb)
            if masked:
                row = lax.broadcasted_iota(jnp.int32, (sub, sub), 0)
                col = lax.broadcasted_iota(jnp.int32, (sub, sub), 1)
                st = jnp.where(row <= col, st, NEG)
            pt = jnp.exp2(st - lse_ref[0, :, rows])
            dv_acc[keys, :] += _dot(pt.astype(MXU_DTYPE), dob)
            dpt = _dot_nt(v_ref[0, keys, :], dob)
            dst = (pt * (dpt - dl_ref[0, :, rows])).astype(MXU_DTYPE)
            dk_acc[keys, :] += _dot(dst, qb)
            q_rows = pl.ds(pl.multiple_of(i * tb + r * sub, sub), sub)
            dq_ref[0, q_rows, :] += _dot_tn(dst, kb)

        @pl.when(i == j)
        def _():
            dk_acc[...] = jnp.zeros_like(dk_acc)
            dv_acc[...] = jnp.zeros_like(dv_acc)
            for r in range(ns):
                for kk in range(r + 1):
                    update(kk, r, kk == r)

        @pl.when(i > j)
        def _():
            for r in range(ns):
                for kk in range(ns):
                    update(kk, r, False)

        @pl.when(i == nb - 1)
        def _():
            dk_ref[0] = (dk_acc[...] * LN2).astype(dk_ref.dtype)
            dv_ref[0] = dv_acc[...].astype(dv_ref.dtype)

    q_idx = lambda h, j, i: (h, jnp.maximum(i, j), 0)
    kv_idx = lambda h, j, i: (h, j, 0)
    stat_idx = lambda h, j, i: (h, 0, jnp.maximum(i, j))
    return pl.pallas_call(
        body, name="mla_attn_bwd", grid=(H, nb, nb),
        in_specs=[pl.BlockSpec((1, tb, QK_EXT), q_idx), pl.BlockSpec((1, tb, QK_EXT), kv_idx),
                  pl.BlockSpec((1, tb, V_DIM), kv_idx),
                  pl.BlockSpec((tb, V_DIM), lambda h, j, i: (jnp.maximum(i, j), h)),
                  pl.BlockSpec((1, 1, tb), stat_idx), pl.BlockSpec((1, 1, tb), stat_idx)],
        out_specs=[pl.BlockSpec((1, T, QK_EXT), lambda h, j, i: (h, 0, 0)),
                   pl.BlockSpec((1, tb, QK_EXT), kv_idx), pl.BlockSpec((1, tb, V_DIM), kv_idx)],
        out_shape=[jax.ShapeDtypeStruct((H, T, QK_EXT), jnp.float32), jax.ShapeDtypeStruct((H, T, QK_EXT), MXU_DTYPE),
                   jax.ShapeDtypeStruct((H, T, V_DIM), MXU_DTYPE)],
        scratch_shapes=[pltpu.VMEM((tb, QK_EXT), jnp.float32), pltpu.VMEM((tb, V_DIM), jnp.float32)],
        compiler_params=_params("parallel", "arbitrary", "arbitrary"),
    )(q, k, v, do, lse, delta)


def _mla_pre_bwd(x, dxm, vec, hb, z, dq, dk, dv, cs, wcat, g_q, g_kv, wuq, wukv):
    T = x.shape[0]
    tm = _tile(T, 256)
    H = MLA_HEADS
    zc = wcat.shape[1]

    def body(x_ref, dxm_ref, vec_ref, h_ref, z_ref, dq_ref, dk_ref, dv_ref, cs_ref, wcat_ref, gq_ref, gkv_ref,
             wuq_ref, wukv_ref, dx_ref, ps_ref, dgq_ref, dgkv_ref, dwcat_ref, dwuq_ref, dwukv_ref):
        @pl.when(pl.program_id(0) == 0)
        def _():
            for ref in (ps_ref, dgq_ref, dgkv_ref, dwcat_ref, dwuq_ref, dwukv_ref):
                ref[...] = jnp.zeros_like(ref)

        z = z_ref[...]
        cs_t = cs_ref[...]
        cqhat, rq = _rms(z[:, :Q_LORA])
        ckhat, rk = _rms(z[:, Q_LORA:Q_LORA + KV_LORA])
        gq, gkv = gq_ref[...], gkv_ref[...]
        cq = (cqhat * gq).astype(MXU_DTYPE)
        ckv = (ckhat * gkv).astype(MXU_DTYPE)
        dcq = jnp.zeros((tm, Q_LORA), jnp.float32)
        dckv = jnp.zeros((tm, KV_LORA), jnp.float32)
        dkr = jnp.zeros((tm, LANES), jnp.float32)
        for hd in range(H):
            dqh = dq_ref[hd] * MLA_SCALE
            gqh = jnp.concatenate([dqh[:, :QK_NOPE], dqh[:, QK_NOPE:] * cs_t], axis=1).astype(MXU_DTYPE)
            dcq += _dot_nt(gqh, wuq_ref[hd])
            dwuq_ref[hd] += _dot_tn(cq, gqh)
            dkh = dk_ref[hd]
            gkvh = jnp.concatenate([dkh[:, :QK_NOPE], dv_ref[hd]], axis=1)
            dckv += _dot_nt(gkvh, wukv_ref[hd])
            dwukv_ref[hd] += _dot_tn(ckv, gkvh)
            dkr += dkh[:, QK_NOPE:].astype(jnp.float32)
        dgq_ref[...] += _rowsum(dcq * cqhat)
        dgkv_ref[...] += _rowsum(dckv * ckhat)
        dcq_pre = _rms_bwd(dcq * gq, cqhat, rq)
        dckv_pre = _rms_bwd(dckv * gkv, ckhat, rk)
        dkr2 = (dkr + pltpu.roll(dkr, QK_ROPE, axis=1)) * cs_t
        dz = jnp.concatenate([dcq_pre, dckv_pre, dkr2], axis=1).astype(MXU_DTYPE)
        dwcat_ref[...] += _dot_tn(h_ref[...], dz)
        dh = _dot_nt(dz, wcat_ref[...])
        dx_ref[...] = _modulate_bwd(dh, x_ref[...], vec_ref, R_GMIX, R_SC1, R_SH1, ps_ref, dxm_ref[...])

    hblk = lambda w: pl.BlockSpec((H, tm, w), lambda i: (0, i, 0))
    return pl.pallas_call(
        body, name="mla_pre_bwd", grid=(T // tm,),
        in_specs=[_rows(tm, D), _rows(tm, D), _full((8, D)), _rows(tm, D), _rows(tm, zc), hblk(QK_EXT), hblk(QK_EXT),
                  hblk(V_DIM), _rows(tm, LANES), _full(wcat.shape), _full(g_q.shape), _full(g_kv.shape),
                  _full(wuq.shape), _full(wukv.shape)],
        out_specs=[_rows(tm, D), _full((8, D)), _full(g_q.shape), _full(g_kv.shape), _full(wcat.shape),
                   _full(wuq.shape), _full(wukv.shape)],
        out_shape=[jax.ShapeDtypeStruct((T, D), jnp.float32), jax.ShapeDtypeStruct((8, D), jnp.float32),
                   jax.ShapeDtypeStruct(g_q.shape, jnp.float32), jax.ShapeDtypeStruct(g_kv.shape, jnp.float32),
                   jax.ShapeDtypeStruct(wcat.shape, jnp.float32), jax.ShapeDtypeStruct(wuq.shape, jnp.float32),
                   jax.ShapeDtypeStruct(wukv.shape, jnp.float32)],
        compiler_params=_params("arbitrary"),
    )(x, dxm, vec, hb, z, dq, dk, dv, cs, wcat, g_q, g_kv, wuq, wukv)


def _swa_attn_bwd(qt, k, v, dot_, bias, sink_rows):
    T = qt.shape[1]
    W, Dh, G, Hk = WINDOW, SWA_HEAD_DIM, SWA_GROUP, SWA_KV_HEADS
    nk = Hk * Dh

    def body(qt_ref, kp_ref, kc_ref, vp_ref, vc_ref, dot_ref, bias_ref, sink_ref, dqt_ref, dk_ref, dv_ref, dsink_ref):
        n = pl.program_id(0)

        @pl.when(n == 0)
        def _():
            dk_ref[...] = jnp.zeros_like(dk_ref)
            dv_ref[...] = jnp.zeros_like(dv_ref)
            dsink_ref[...] = jnp.zeros_like(dsink_ref)

        dks, dvs = [], []
        for kh in range(Hk):
            qt, kb, pn, p_sink = _swa_probs(n, kh, qt_ref, kp_ref, kc_ref, bias_ref, sink_ref)
            vb = jnp.concatenate([vp_ref[:, kh * Dh:(kh + 1) * Dh], vc_ref[:, kh * Dh:(kh + 1) * Dh]], axis=0)
            dot_h = jnp.concatenate([dot_ref[(kh * G + g) * Dh:(kh * G + g + 1) * Dh, :] for g in range(G)], axis=1)
            dp = _dot(vb, dot_h)
            delta = jnp.sum(pn * dp, axis=0, keepdims=True)
            dsb = (pn * (dp - delta)).astype(MXU_DTYPE)
            dsink_ref[kh] += -p_sink * delta
            dqt = _dot_tn(kb, dsb) * SWA_SCALE
            for g in range(G):
                dqt_ref[(kh * G + g) * Dh:(kh * G + g + 1) * Dh, :] = dqt[:, g * W:(g + 1) * W]
            dks.append(_dot_nt(dsb, qt))
            dvs.append(_dot_nt(pn.astype(MXU_DTYPE), dot_h))
        dkb = jnp.concatenate(dks, axis=1)
        dvb = jnp.concatenate(dvs, axis=1)
        cur_rows = pl.ds(pl.multiple_of(n * W, W), W)
        dk_ref[cur_rows, :] += dkb[W:]
        dv_ref[cur_rows, :] += dvb[W:]

        @pl.when(n > 0)
        def _():
            prev_rows = pl.ds(pl.multiple_of((n - 1) * W, W), W)
            dk_ref[prev_rows, :] += dkb[:W]
            dv_ref[prev_rows, :] += dvb[:W]

    prev = lambda n: (jnp.maximum(n - 1, 0), 0)
    cur = lambda n: (n, 0)
    col = lambda n: (0, n)
    return pl.pallas_call(
        body, name="swa_attn_bwd", grid=(T // W,),
        in_specs=[pl.BlockSpec((D, W), col), pl.BlockSpec((W, nk), prev), pl.BlockSpec((W, nk), cur),
                  pl.BlockSpec((W, nk), prev), pl.BlockSpec((W, nk), cur), pl.BlockSpec((D, W), col),
                  _full(bias.shape), _full(sink_rows.shape)],
        out_specs=[pl.BlockSpec((D, W), col), _full((T, nk)), _full((T, nk)), _full(sink_rows.shape)],
        out_shape=[jax.ShapeDtypeStruct((D, T), jnp.float32), jax.ShapeDtypeStruct((T, nk), jnp.float32),
                   jax.ShapeDtypeStruct((T, nk), jnp.float32), jax.ShapeDtypeStruct(sink_rows.shape, jnp.float32)],
        compiler_params=_params("arbitrary"),
    )(qt, k, k, v, v, dot_, bias, sink_rows)


def _swa_pre_bwd(x, dxm, vec, dq, dk, dv, w_qkv):
    T = x.shape[0]
    tm = _tile(T, 512)
    nq = SWA_HEADS * SWA_HEAD_DIM
    nk = SWA_KV_HEADS * SWA_HEAD_DIM
    nqkv = nq + 2 * nk

    def body(x_ref, dxm_ref, vec_ref, dq_ref, dk_ref, dv_ref, w_ref, dx_ref, dqkv_ref, ps_ref, db_ref):
        @pl.when(pl.program_id(0) == 0)
        def _():
            ps_ref[...] = jnp.zeros_like(ps_ref)
            db_ref[...] = jnp.zeros_like(db_ref)

        dqkv = jnp.concatenate([dq_ref[...], dk_ref[...], dv_ref[...]], axis=1)
        db_ref[...] += _rowsum(dqkv)
        dqkv_b = dqkv.astype(MXU_DTYPE)
        dqkv_ref[...] = dqkv_b
        dh = _dot_nt(dqkv_b, w_ref[...])
        dx_ref[...] = _modulate_bwd(dh, x_ref[...], vec_ref, R_GMIX, R_SC1, R_SH1, ps_ref, dxm_ref[...])

    return pl.pallas_call(
        body, name="swa_pre_bwd", grid=(T // tm,),
        in_specs=[_rows(tm, D), _rows(tm, D), _full((8, D)), _rows(tm, nq), _rows(tm, nk), _rows(tm, nk),
                  _full(w_qkv.shape)],
        out_specs=[_rows(tm, D), _rows(tm, nqkv), _full((8, D)), _full((1, nqkv))],
        out_shape=[jax.ShapeDtypeStruct((T, D), jnp.float32), jax.ShapeDtypeStruct((T, nqkv), MXU_DTYPE),
                   jax.ShapeDtypeStruct((8, D), jnp.float32), jax.ShapeDtypeStruct((1, nqkv), jnp.float32)],
        compiler_params=_params("arbitrary"),
    )(x, dxm, vec, dq, dk, dv, w_qkv)


def _rot_cols(w):
    half = QK_ROPE // 2
    return jnp.concatenate([-w[..., half:], w[..., :half]], axis=-1)


def _unrot_grad(d_rope, d_rot):
    half = QK_ROPE // 2
    return d_rope + jnp.concatenate([d_rot[..., half:], -d_rot[..., :half]], axis=-1)


def _rope_table(positions):
    half = QK_ROPE // 2
    inv_freq = ROPE_THETA ** (-jnp.arange(half, dtype=jnp.float32) / half)
    ang = positions.astype(jnp.float32)[:, None] * inv_freq
    cos, sin = jnp.cos(ang), jnp.sin(ang)
    return jnp.concatenate([cos, cos, sin, sin], axis=1)


def _sequence_step(x, tgt, positions, vecs, g_q, g_kv, sinks, g_final, wts, late_weights, on_late_grads, on_late_landed):
    H = MLA_HEADS
    cs = _rope_table(positions)
    w_dkv = wts["mla_w_dkv"]
    wcat = jnp.concatenate([wts["mla_w_dq"], w_dkv, _rot_cols(w_dkv[:, KV_LORA:])], axis=1)
    uq = wts["mla_w_uq"].reshape(Q_LORA, H, QK_NOPE + QK_ROPE)
    wuq = jnp.concatenate([uq, _rot_cols(uq[..., QK_NOPE:])], axis=-1).transpose(1, 0, 2)
    wukv = wts["mla_w_ukv"].reshape(KV_LORA, H, QK_NOPE + V_DIM).transpose(1, 0, 2)
    zero_bias = jnp.zeros((1, D), jnp.float32)
    bias = _swa_bias()
    sink_rows = jnp.broadcast_to(sinks.reshape(SWA_KV_HEADS, 1, SWA_GROUP, 1),
                                 (SWA_KV_HEADS, 1, SWA_GROUP, WINDOW)).reshape(SWA_KV_HEADS, 1, SWA_GROUP * WINDOW)

    h1a, z, q, k, v = _mla_pre(x, vecs[0], wcat, g_q, g_kv, wuq, wukv, cs)
    o_a, lse = _mla_attn_fwd(q, k, v)
    y1a, xm_a, h2a = _post_attn(o_a, x, wts["mla_w_o"], zero_bias, vecs[0])
    wts = {**wts, **late_weights(h2a)}
    a_a, y2a, x1 = _mlp_fwd(h2a, wts["w_ff1"][0], wts["w_ff2"][0], xm_a, vecs[0])

    h1b, qs_t, ks, vs = _swa_pre(x1, vecs[1], wts["swa_w_qkv"], wts["swa_b_qkv"])
    o_bt = _swa_attn_fwd(qs_t, ks, vs, bias, sink_rows)
    y1b, xm_b, h2b = _post_attn(o_bt, x1, wts["swa_w_o"], wts["swa_b_o"], vecs[1], o_transposed=True)
    a_b, y2b, x2 = _mlp_fwd(h2b, wts["w_ff1"][1], wts["w_ff2"][1], xm_b, vecs[1])

    loss8, dx2, dg_final = _final_loss(x2, tgt, g_final.reshape(1, D))

    du_b, dy2b, dxm_b, ps_mlp_b = _mlp_bwd(dx2, y2b, a_b, wts["w_ff1"][1], wts["w_ff2"][1], xm_b, vecs[1])
    g_ff2 = _mm_tn(a_b, dy2b, "dw_ff2_l1", "rows", DEPTH, 1)
    g_ff1 = _mm_tn(h2b, du_b, "dw_ff1_l1", "cols", DEPTH, 1)
    dy1b, do_bt, ps_out_b = _attn_out_bwd(dxm_b, y1b, None, wts["swa_w_o"], vecs[1], False)
    g_swa_o = _mm_tn(o_bt, dy1b, "dw_o_swa", a_transposed=True)
    dqs_t, dks, dvs, dsinks = _swa_attn_bwd(qs_t, ks, vs, do_bt, bias, sink_rows)
    dqs = dqs_t.T
    dx1, dqkv, ps_pre_b, g_swa_bqkv = _swa_pre_bwd(x1, dxm_b, vecs[1], dqs, dks, dvs, wts["swa_w_qkv"])
    g_swa_qkv = _mm_tn(h1b, dqkv, "dw_qkv", "cols")

    du_a, dy2a, dxm_a, ps_mlp_a = _mlp_bwd(dx1, y2a, a_a, wts["w_ff1"][0], wts["w_ff2"][0], xm_a, vecs[0])
    g_ff2 = _mm_tn(a_a, dy2a, "dw_ff2_l0", "rows", DEPTH, 0, g_ff2)
    g_ff1 = _mm_tn(h2a, du_a, "dw_ff1_l0", "cols", DEPTH, 0, g_ff1)
    rows4 = lambda g: g.reshape(N_CHIPS, g.shape[0] // N_CHIPS, g.shape[1])
    token = on_late_grads({
        "swa_w_qkv": g_swa_qkv.reshape(N_CHIPS, D, -1), "swa_w_o": rows4(g_swa_o),
        "w_ff1": g_ff1.reshape(N_CHIPS, DEPTH * D, -1), "w_ff2": g_ff2.reshape(N_CHIPS, -1, D)})
    dy1a, do_a, ps_out_a, delta = _attn_out_bwd(dxm_a, y1a, o_a, wts["mla_w_o"], vecs[0] + token[0, 0], True)
    g_mla_o = _mm_tn(o_a, dy1a, "dw_o_mla")
    token = on_late_landed(g_mla_o)
    dq, dk, dv = _mla_attn_bwd(q, k, v, do_a, lse, delta + token[0, 0])
    dx0, ps_pre_a, dg_q, dg_kv, dwcat, dwuq, dwukv = _mla_pre_bwd(
        x, dxm_a, vecs[0], h1a, z, dq, dk, dv, cs, wcat, g_q, g_kv, wuq, wukv)

    c0, c1, c2 = Q_LORA, Q_LORA + KV_LORA, Q_LORA + KV_LORA + QK_ROPE
    g_dq = dwcat[:, :c0]
    g_dkv = jnp.concatenate([dwcat[:, c0:c1], _unrot_grad(dwcat[:, c1:c2], dwcat[:, c2:])], axis=1)
    e0 = QK_NOPE + QK_ROPE
    g_uq = jnp.concatenate([dwuq[..., :QK_NOPE], _unrot_grad(dwuq[..., QK_NOPE:e0], dwuq[..., e0:])], axis=-1)
    per = H // N_CHIPS
    g_uq = g_uq.reshape(N_CHIPS, per, Q_LORA, e0).transpose(0, 2, 1, 3).reshape(N_CHIPS, Q_LORA, per * e0)
    g_ukv = dwukv.reshape(N_CHIPS, per, KV_LORA, QK_NOPE + V_DIM).transpose(0, 2, 1, 3)
    g_ukv = g_ukv.reshape(N_CHIPS, KV_LORA, per * (QK_NOPE + V_DIM))

    def dmod(ps_pre, ps_out, ps_mlp):
        return jnp.concatenate([ps_pre[R_SH1:R_SC1 + 1], ps_out[R_GT1:R_GT1 + 1], ps_mlp[R_SH2:R_GT2 + 1]], axis=0)

    grads = {"mla_w_dq": rows4(g_dq), "mla_w_uq": g_uq, "mla_w_dkv": rows4(g_dkv), "mla_w_ukv": g_ukv,
             "mla_w_o": rows4(g_mla_o)}
    small = {
        "dmod": jnp.stack([dmod(ps_pre_a, ps_out_a, ps_mlp_a), dmod(ps_pre_b, ps_out_b, ps_mlp_b)]).reshape(DEPTH, 6 * D),
        "g_mix": jnp.stack([ps_pre_a[R_GMIX], ps_pre_b[R_GMIX]]),
        "g_mlp": jnp.stack([ps_mlp_a[R_GMLP], ps_mlp_b[R_GMLP]]),
        "mla_g_q": dg_q, "mla_g_kv": dg_kv, "swa_sinks": jnp.sum(dsinks.reshape(SWA_HEADS, WINDOW), axis=1).reshape(1, SWA_HEADS),
        "swa_b_qkv": g_swa_bqkv, "swa_b_o": ps_out_b[R_BO:R_BO + 1],
        "g_final": dg_final.reshape(D), "loss": loss8[0, 0],
    }
    return dx0, grads, small


SHARDED = {
    "mla_w_dq": (1, D // N_CHIPS, Q_LORA),
    "mla_w_uq": (1, Q_LORA, MLA_HEADS * (QK_NOPE + QK_ROPE) // N_CHIPS),
    "mla_w_dkv": (1, D // N_CHIPS, KV_LORA + QK_ROPE),
    "mla_w_ukv": (1, KV_LORA, MLA_HEADS * (QK_NOPE + V_DIM) // N_CHIPS),
    "mla_w_o": (1, MLA_HEADS * V_DIM // N_CHIPS, D),
    "swa_w_qkv": (1, D, (SWA_HEADS + 2 * SWA_KV_HEADS) * SWA_HEAD_DIM // N_CHIPS),
    "swa_w_o": (1, SWA_HEADS * SWA_HEAD_DIM // N_CHIPS, D),
    "w_ff1": (DEPTH, D, D_FF // N_CHIPS),
    "w_ff2": (DEPTH, D_FF // N_CHIPS, D),
}
COL_SPLIT = ("mla_w_uq", "mla_w_ukv", "swa_w_qkv")
BIASES = {"swa_b_qkv": (SWA_HEADS + 2 * SWA_KV_HEADS) * SWA_HEAD_DIM, "swa_b_o": D}


def _view2d(name):
    shape = SHARDED[name]
    return math.prod(shape[:-1]), shape[-1]


SMALL = {"b_ada": (DEPTH, 6 * D), "g_mix": (DEPTH, D), "g_mlp": (DEPTH, D), "mla_g_q": (1, Q_LORA),
         "mla_g_kv": (1, KV_LORA), "swa_sinks": (1, SWA_HEADS), "g_final": (D,), "loss": (),
         "swa_b_qkv": (1, BIASES["swa_b_qkv"]), "swa_b_o": (1, BIASES["swa_b_o"])}
SMALL_ROWS = 168
DMA_ROWS = 256


def _small_slots():
    slots, off = {}, 0
    for name, shape in SMALL.items():
        n = max(math.prod(shape), 1)
        slots[name] = (off, n)
        off += -(-n // LANES) * LANES
    assert off <= SMALL_ROWS * LANES
    return slots


def _pack_small(vals):
    parts, end = [], 0
    for name, (off, n) in _small_slots().items():
        pad = -(-n // LANES) * LANES - n
        v = vals[name].astype(jnp.float32).reshape(-1) if name in vals else jnp.zeros((n,), jnp.float32)
        parts += [v, jnp.zeros((pad,), jnp.float32)]
        end = off + n + pad
    parts.append(jnp.zeros((SMALL_ROWS * LANES - end,), jnp.float32))
    return jnp.concatenate(parts).reshape(SMALL_ROWS, LANES)


def _unpack_small(buf):
    flat = buf.reshape(-1)
    return {name: flat[off:off + n].reshape(SMALL[name]) for name, (off, n) in _small_slots().items()}


def _pieces(rows):
    return [(off, min(DMA_ROWS, rows - off)) for off in range(0, rows, DMA_ROWS)]


HBM = pl.BlockSpec(memory_space=pltpu.HBM)
MESH = pl.DeviceIdType.MESH


def _place():
    x, y, c = lax.axis_index("x"), lax.axis_index("y"), lax.axis_index("c")
    chips = [(1 - x, y), (x, 1 - y), (1 - x, 1 - y)]
    return x, y, c, chips


def _all_gather(block):
    m_per, n = block.shape

    def body(x_ref, out_ref, send_sems, recv_sems, local_sem):
        x, y, c, chips = _place()
        me, sibling = (x, y, c), (x, y, 1 - c)

        def rows(px, py, pc):
            return out_ref.at[pl.ds((4 * px + 2 * py + pc) * m_per, m_per), :]

        def copy(k, blk, to, src=None):
            return pltpu.make_async_remote_copy(
                src_ref=rows(*blk) if src is None else src, dst_ref=rows(*blk),
                send_sem=send_sems.at[k], recv_sem=recv_sems.at[k], device_id=to, device_id_type=MESH)

        mine = pltpu.make_async_copy(x_ref, rows(*me), local_sem)
        mine.start()
        first = [copy(0, me, sibling, src=x_ref)]
        first += [copy(1 + j, me, (*chip, c), src=x_ref) for j, chip in enumerate(chips)]
        for cp in first:
            cp.start()
        passed = [copy(4 + j, (*chip, c), sibling) for j, chip in enumerate(chips)]
        for j, chip in enumerate(chips):
            copy(1 + j, (*chip, c), me).wait_recv()
            passed[j].start()
        copy(0, sibling, me).wait_recv()
        for j, chip in enumerate(chips):
            copy(4 + j, (*chip, 1 - c), me).wait_recv()
        for cp in first + passed:
            cp.wait_send()
        mine.wait()

    out = pl.pallas_call(
        body, name="all_gather_small",
        out_shape=jax.ShapeDtypeStruct((N_DEV * m_per, n), block.dtype),
        in_specs=[pl.BlockSpec(memory_space=pltpu.VMEM)],
        out_specs=pl.BlockSpec(memory_space=pltpu.VMEM),
        scratch_shapes=[pltpu.SemaphoreType.DMA((7,)), pltpu.SemaphoreType.DMA((7,)), pltpu.SemaphoreType.DMA],
    )(block)
    return out.reshape(N_DEV, m_per, n)


def _weight_gather(shards):
    nt = len(shards)

    def body(*refs):
        w_refs, out_refs = refs[:nt], refs[nt:2 * nt]
        send_sems, recv_sems = refs[2 * nt:]
        x, y, c, chips = _place()
        sibling = (x, y, 1 - c)

        def slab(t, px, py, half):
            rh = shards[t].shape[0] // 2
            return out_refs[t].at[2 * px + py, pl.ds(half * rh, rh), :]

        def copy(t, k, src, dst, to):
            return pltpu.make_async_remote_copy(src_ref=src, dst_ref=dst, send_sem=send_sems.at[6 * t + k],
                                                recv_sem=recv_sems.at[6 * t + k], device_id=to, device_id_type=MESH)

        first = []
        for t in range(nt):
            rh = shards[t].shape[0] // 2
            first += [copy(t, j, w_refs[t].at[pl.ds(c * rh, rh), :], slab(t, x, y, c), (*chip, c))
                      for j, chip in enumerate(chips)]
        for cp in first:
            cp.start()
        passed = []
        for t in range(nt):
            for j, chip in enumerate(chips):
                copy(t, j, slab(t, *chip, c), slab(t, *chip, c), (*chip, c)).wait_recv()
                rh = shards[t].shape[0] // 2
                for off, n in _pieces(rh):
                    piece = out_refs[t].at[2 * chip[0] + chip[1], pl.ds(c * rh + off, n), :]
                    copy(t, 3 + j, piece, piece, sibling).start()
                passed.append(copy(t, 3 + j, slab(t, *chip, c), slab(t, *chip, c), sibling))
        for t in range(nt):
            for j, chip in enumerate(chips):
                copy(t, 3 + j, slab(t, *chip, 1 - c), slab(t, *chip, 1 - c), sibling).wait_recv()
        for cp in first + passed:
            cp.wait_send()

    return pl.pallas_call(
        body, name="weight_gather",
        out_shape=[jax.ShapeDtypeStruct((N_CHIPS,) + s.shape, s.dtype) for s in shards],
        in_specs=[HBM] * nt, out_specs=[HBM] * nt,
        scratch_shapes=[pltpu.SemaphoreType.DMA((6 * nt,)), pltpu.SemaphoreType.DMA((6 * nt,))],
    )(*shards)


SEM = pl.BlockSpec(memory_space=pltpu.SEMAPHORE)
ANY = pl.BlockSpec(memory_space=pl.ANY)
SPLIT_COPY = pltpu.SideEffectType.DATAFLOW_SIDE_EFFECTING


def _late_copies(w_refs, land_refs, send_sems, recv_sems):
    x, y, c, chips = _place()
    return [pltpu.make_async_remote_copy(
        src_ref=w_refs[t], dst_ref=land_refs[t].at[2 * x + y], send_sem=send_sems.at[3 * t + j],
        recv_sem=recv_sems.at[3 * t + j], device_id=(cx, cy, c), device_id_type=MESH)
        for t in range(len(w_refs)) for j, (cx, cy) in enumerate(chips)], chips


def _late_gather_start(shards, after):
    nt, na = len(shards), len(after)

    def body(*refs):
        w_refs, land_refs = refs[:nt], refs[nt:2 * nt]
        send_sems, recv_sems, token = refs[2 * nt + na], refs[2 * nt + na + 1], refs[-1]
        copies, _ = _late_copies(w_refs, land_refs, send_sems, recv_sems)
        for cp in copies:
            cp.start()
        token[...] = jnp.zeros_like(token)

    hbm = lambda a: pltpu.with_memory_space_constraint(a, pltpu.HBM)
    lands = [lax.empty((N_CHIPS,) + s.shape, s.dtype) for s in shards]
    outs = pl.pallas_call(
        body, name="late_gather_start",
        out_shape=(pltpu.SemaphoreType.DMA((3 * nt,)), pltpu.SemaphoreType.DMA((3 * nt,)),
                   *[pltpu.HBM(s.shape, s.dtype) for s in shards], *[pltpu.HBM(l.shape, l.dtype) for l in lands],
                   jax.ShapeDtypeStruct((8, LANES), jnp.float32)),
        in_specs=[HBM] * (2 * nt) + [ANY] * na,
        out_specs=(SEM, SEM, *([HBM] * (2 * nt)), pl.BlockSpec(memory_space=pltpu.VMEM)),
        input_output_aliases={i: 2 + i for i in range(2 * nt)},
        compiler_params=pltpu.CompilerParams(has_side_effects=SPLIT_COPY),
    )(*[hbm(s) for s in shards], *[hbm(l) for l in lands], *after)
    return outs[0], outs[1], list(outs[2:2 + nt]), list(outs[2 + nt:2 + 2 * nt]), outs[-1]


def _late_gather_wait(send_sems, recv_sems, shards, lands, after):
    nt = len(shards)

    def body(*refs):
        w_refs, land_refs = refs[:nt], refs[nt:2 * nt]
        s_sems, r_sems = refs[2 * nt], refs[2 * nt + 1]
        x, y, c, chips = _place()
        for t in range(nt):
            for j, (cx, cy) in enumerate(chips):
                cp = pltpu.make_async_remote_copy(
                    src_ref=w_refs[t], dst_ref=land_refs[t].at[2 * cx + cy], send_sem=s_sems.at[3 * t + j],
                    recv_sem=r_sems.at[3 * t + j], device_id=(cx, cy, c), device_id_type=MESH)
                cp.wait_send()
                cp.wait_recv()

    outs = pl.pallas_call(
        body, name="late_gather_wait",
        out_shape=(*[pltpu.HBM(s.shape, s.dtype) for s in shards], *[pltpu.HBM(l.shape, l.dtype) for l in lands]),
        in_specs=[HBM] * (2 * nt) + [SEM, SEM, ANY], out_specs=tuple([HBM] * (2 * nt)),
        input_output_aliases={i: i for i in range(2 * nt)},
        compiler_params=pltpu.CompilerParams(has_side_effects=SPLIT_COPY),
    )(*shards, *lands, send_sems, recv_sems, after)
    return list(outs[nt:])


def _grad_pair_in(grads):
    nt = len(grads)

    def body(*refs):
        g_refs, got_refs = refs[:nt], refs[nt:2 * nt]
        send_sems, recv_sems = refs[2 * nt:]
        x, y, c, _ = _place()
        sibling = (x, y, 1 - c)

        def copy(t, src, dst):
            return pltpu.make_async_remote_copy(src_ref=src, dst_ref=dst, send_sem=send_sems.at[t],
                                                recv_sem=recv_sems.at[t], device_id=sibling, device_id_type=MESH)

        for t in range(nt):
            rh = grads[t].shape[1] // 2
            for p in range(N_CHIPS):
                for off, n in _pieces(rh):
                    copy(t, g_refs[t].at[p, pl.ds((1 - c) * rh + off, n), :], got_refs[t].at[p, pl.ds(off, n), :]).start()
        for t in range(nt):
            rh = grads[t].shape[1] // 2
            copy(t, g_refs[t].at[:, pl.ds((1 - c) * rh, rh), :], got_refs[t]).wait()

    return pl.pallas_call(
        body, name="grad_pair_in",
        out_shape=[jax.ShapeDtypeStruct((N_CHIPS, g.shape[1] // 2, g.shape[2]), g.dtype) for g in grads],
        in_specs=[HBM] * nt, out_specs=[HBM] * nt,
        scratch_shapes=[pltpu.SemaphoreType.DMA((nt,)), pltpu.SemaphoreType.DMA((nt,))],
    )(*grads)


def _pair_in_start(grads):
    nt = len(grads)

    def body(*refs):
        g_refs, land_refs = refs[:nt], refs[nt:2 * nt]
        send_sems, recv_sems, token = refs[2 * nt], refs[2 * nt + 1], refs[-1]
        x, y, c, _ = _place()
        for t in range(nt):
            rh = grads[t].shape[1] // 2
            for p in range(N_CHIPS):
                for off, n in _pieces(rh):
                    pltpu.make_async_remote_copy(
                        src_ref=g_refs[t].at[p, pl.ds((1 - c) * rh + off, n), :], dst_ref=land_refs[t].at[p, pl.ds(off, n), :],
                        send_sem=send_sems.at[t], recv_sem=recv_sems.at[t], device_id=(x, y, 1 - c),
                        device_id_type=MESH).start()
        token[...] = jnp.zeros_like(token)

    hbm = lambda a: pltpu.with_memory_space_constraint(a, pltpu.HBM)
    lands = [lax.empty((N_CHIPS, g.shape[1] // 2, g.shape[2]), g.dtype) for g in grads]
    outs = pl.pallas_call(
        body, name="grad_pair_in_start",
        out_shape=(pltpu.SemaphoreType.DMA((nt,)), pltpu.SemaphoreType.DMA((nt,)),
                   *[pltpu.HBM(g.shape, g.dtype) for g in grads], *[pltpu.HBM(l.shape, l.dtype) for l in lands],
                   jax.ShapeDtypeStruct((8, LANES), jnp.float32)),
        in_specs=[HBM] * (2 * nt),
        out_specs=(SEM, SEM, *([HBM] * (2 * nt)), pl.BlockSpec(memory_space=pltpu.VMEM)),
        input_output_aliases={i: 2 + i for i in range(2 * nt)},
        compiler_params=pltpu.CompilerParams(has_side_effects=SPLIT_COPY),
    )(*[hbm(g) for g in grads], *[hbm(l) for l in lands])
    return outs[0], outs[1], list(outs[2:2 + nt]), list(outs[2 + nt:2 + 2 * nt]), outs[-1]


def _pair_in_wait(send_sems, recv_sems, grads, lands, after):
    nt = len(grads)

    def body(*refs):
        g_refs, land_refs = refs[:nt], refs[nt:2 * nt]
        s_sems, r_sems = refs[2 * nt], refs[2 * nt + 1]
        x, y, c, _ = _place()
        for t in range(nt):
            rh = grads[t].shape[1] // 2
            cp = pltpu.make_async_remote_copy(
                src_ref=g_refs[t].at[:, pl.ds((1 - c) * rh, rh), :], dst_ref=land_refs[t], send_sem=s_sems.at[t],
                recv_sem=r_sems.at[t], device_id=(x, y, 1 - c), device_id_type=MESH)
            cp.wait_send()
            cp.wait_recv()

    outs = pl.pallas_call(
        body, name="grad_pair_in_wait",
        out_shape=(*[pltpu.HBM(g.shape, g.dtype) for g in grads], *[pltpu.HBM(l.shape, l.dtype) for l in lands]),
        in_specs=[HBM] * (2 * nt) + [SEM, SEM, ANY], out_specs=tuple([HBM] * (2 * nt)),
        input_output_aliases={i: i for i in range(2 * nt)},
        compiler_params=pltpu.CompilerParams(has_side_effects=SPLIT_COPY),
    )(*grads, *lands, send_sems, recv_sems, after)
    return list(outs[:nt]), list(outs[nt:])


def _pair_sum(g, got, core, name):
    _, rows, cols = g.shape
    rh = rows // 2
    tr = _tile(rh, 512)
    nb = rh // tr

    def body(c_ref, g_ref, got_ref, s32_ref, s16_ref):
        s = g_ref[...] + got_ref[...]
        s32_ref[...] = s
        s16_ref[...] = s.astype(s16_ref.dtype)

    blk = pl.BlockSpec((None, tr, cols), lambda p, i, c_ref: (p, i, 0))
    return pl.pallas_call(
        body, name=name,
        grid_spec=pltpu.PrefetchScalarGridSpec(
            num_scalar_prefetch=1, grid=(N_CHIPS, nb),
            in_specs=[pl.BlockSpec((None, tr, cols), lambda p, i, c_ref: (p, c_ref[0] * nb + i, 0)), blk],
            out_specs=[blk, blk]),
        out_shape=[jax.ShapeDtypeStruct((N_CHIPS, rh, cols), jnp.float32),
                   jax.ShapeDtypeStruct((N_CHIPS, rh, cols), jnp.bfloat16)],
        compiler_params=_params("parallel", "parallel"),
    )(core, g, got)


def _grad_chip_exchange(parts):
    nt = len(parts)

    def body(*refs):
        a_refs, got_refs = refs[:nt], refs[nt:2 * nt]
        send_sems, recv_sems = refs[2 * nt:]
        x, y, c, chips = _place()
        sends = [pltpu.make_async_remote_copy(
            src_ref=a_refs[t].at[2 * cx + cy], dst_ref=got_refs[t].at[j], send_sem=send_sems.at[3 * t + j],
            recv_sem=recv_sems.at[3 * t + j], device_id=(cx, cy, c), device_id_type=MESH)
            for t in range(nt) for j, (cx, cy) in enumerate(chips)]
        for cp in sends:
            cp.start()
        for cp in sends:
            cp.wait_recv()
        for cp in sends:
            cp.wait_send()

    return pl.pallas_call(
        body, name="grad_chip_exchange",
        out_shape=[jax.ShapeDtypeStruct((N_CHIPS - 1,) + a.shape[1:], a.dtype) for a in parts],
        in_specs=[HBM] * nt, out_specs=[HBM] * nt,
        scratch_shapes=[pltpu.SemaphoreType.DMA((3 * nt,)), pltpu.SemaphoreType.DMA((3 * nt,))],
    )(*parts)


def _exchange_start(parts):
    nt = len(parts)

    def body(*refs):
        a_refs, land_refs = refs[:nt], refs[nt:2 * nt]
        send_sems, recv_sems, token = refs[2 * nt], refs[2 * nt + 1], refs[-1]
        x, y, c, chips = _place()
        for t in range(nt):
            for j, (cx, cy) in enumerate(chips):
                pltpu.make_async_remote_copy(
                    src_ref=a_refs[t].at[2 * cx + cy], dst_ref=land_refs[t].at[j], send_sem=send_sems.at[3 * t + j],
                    recv_sem=recv_sems.at[3 * t + j], device_id=(cx, cy, c), device_id_type=MESH).start()
        token[...] = jnp.zeros_like(token)

    hbm = lambda a: pltpu.with_memory_space_constraint(a, pltpu.HBM)
    lands = [lax.empty((N_CHIPS - 1,) + a.shape[1:], a.dtype) for a in parts]
    outs = pl.pallas_call(
        body, name="grad_exchange_start",
        out_shape=(pltpu.SemaphoreType.DMA((3 * nt,)), pltpu.SemaphoreType.DMA((3 * nt,)),
                   *[pltpu.HBM(a.shape, a.dtype) for a in parts], *[pltpu.HBM(l.shape, l.dtype) for l in lands],
                   jax.ShapeDtypeStruct((8, LANES), jnp.float32)),
        in_specs=[HBM] * (2 * nt),
        out_specs=(SEM, SEM, *([HBM] * (2 * nt)), pl.BlockSpec(memory_space=pltpu.VMEM)),
        input_output_aliases={i: 2 + i for i in range(2 * nt)},
        compiler_params=pltpu.CompilerParams(has_side_effects=SPLIT_COPY),
    )(*[hbm(a) for a in parts], *[hbm(l) for l in lands])
    return outs[0], outs[1], list(outs[2:2 + nt]), list(outs[2 + nt:2 + 2 * nt]), outs[-1]


def _exchange_wait(send_sems, recv_sems, parts, lands, after):
    nt = len(parts)

    def body(*refs):
        a_refs, land_refs = refs[:nt], refs[nt:2 * nt]
        s_sems, r_sems = refs[2 * nt], refs[2 * nt + 1]
        x, y, c, chips = _place()
        for t in range(nt):
            for j, (cx, cy) in enumerate(chips):
                cp = pltpu.make_async_remote_copy(
                    src_ref=a_refs[t].at[2 * cx + cy], dst_ref=land_refs[t].at[j], send_sem=s_sems.at[3 * t + j],
                    recv_sem=r_sems.at[3 * t + j], device_id=(cx, cy, c), device_id_type=MESH)
                cp.wait_send()
                cp.wait_recv()

    outs = pl.pallas_call(
        body, name="grad_exchange_wait",
        out_shape=(*[pltpu.HBM(a.shape, a.dtype) for a in parts], *[pltpu.HBM(l.shape, l.dtype) for l in lands]),
        in_specs=[HBM] * (2 * nt) + [SEM, SEM, ANY], out_specs=tuple([HBM] * (2 * nt)),
        input_output_aliases={i: i for i in range(2 * nt)},
        compiler_params=pltpu.CompilerParams(has_side_effects=SPLIT_COPY),
    )(*parts, *lands, send_sems, recv_sems, after)
    return list(outs[nt:])


def _chip_sum(s32, got, chip, name):
    _, rh, cols = s32.shape
    tr = _tile(rh, 512)

    def body(p_ref, s_ref, got_ref, o_ref):
        acc = s_ref[...]
        for j in range(N_CHIPS - 1):
            acc = acc + got_ref[j].astype(jnp.float32)
        o_ref[...] = acc

    return pl.pallas_call(
        body, name=name,
        grid_spec=pltpu.PrefetchScalarGridSpec(
            num_scalar_prefetch=1, grid=(rh // tr,),
            in_specs=[pl.BlockSpec((None, tr, cols), lambda i, p_ref: (p_ref[0], i, 0)),
                      pl.BlockSpec((N_CHIPS - 1, tr, cols), lambda i, p_ref: (0, i, 0))],
            out_specs=pl.BlockSpec((tr, cols), lambda i, p_ref: (i, 0))),
        out_shape=jax.ShapeDtypeStruct((rh, cols), jnp.float32),
        compiler_params=_params("parallel"),
    )(chip, s32, got)


def _grad_pair_out(halves):
    nt = len(halves)

    def body(*refs):
        h_refs, got_refs = refs[:nt], refs[nt:2 * nt]
        send_sems, recv_sems = refs[2 * nt:]
        x, y, c, _ = _place()
        sibling = (x, y, 1 - c)

        def copy(t, src, dst):
            return pltpu.make_async_remote_copy(src_ref=src, dst_ref=dst, send_sem=send_sems.at[t],
                                                recv_sem=recv_sems.at[t], device_id=sibling, device_id_type=MESH)

        for t in range(nt):
            for off, n in _pieces(halves[t].shape[0]):
                copy(t, h_refs[t].at[pl.ds(off, n), :], got_refs[t].at[pl.ds(off, n), :]).start()
        for t in range(nt):
            copy(t, h_refs[t], got_refs[t]).wait()

    return pl.pallas_call(
        body, name="grad_pair_out",
        out_shape=[jax.ShapeDtypeStruct(h.shape, h.dtype) for h in halves],
        in_specs=[HBM] * nt, out_specs=[HBM] * nt,
        scratch_shapes=[pltpu.SemaphoreType.DMA((nt,)), pltpu.SemaphoreType.DMA((nt,))],
    )(*halves)


def _ada_part(c_all, w_ada):
    L, _, ncol = w_ada.shape
    tn = _tile(ncol, 512)

    def body(c_ref, w_ref, cond_ref, part_ref):
        cv = c_ref[...]
        cond = cv * jax.nn.sigmoid(cv)
        cond_ref[...] = cond
        part_ref[0] = jnp.dot(cond, w_ref[0], precision=lax.Precision.HIGHEST, preferred_element_type=jnp.float32)

    return pl.pallas_call(
        body, name="ada_part", grid=(L, ncol // tn),
        in_specs=[_full((N_DEV, D)), pl.BlockSpec((1, D, tn), lambda l, j: (l, 0, j))],
        out_specs=[_full((N_DEV, D)), pl.BlockSpec((1, N_DEV, tn), lambda l, j: (l, 0, j))],
        out_shape=[jax.ShapeDtypeStruct((N_DEV, D), jnp.float32), jax.ShapeDtypeStruct((L, N_DEV, ncol), jnp.float32)],
        compiler_params=_params("arbitrary", "arbitrary"),
    )(c_all, w_ada)


def _adamw_math(w, g, m, v):
    m = ADAM_B1 * m + (1.0 - ADAM_B1) * g
    v = ADAM_B2 * v + (1.0 - ADAM_B2) * jnp.square(g)
    m_hat = m / (1.0 - ADAM_B1 ** ADAM_STEP)
    v_hat = v / (1.0 - ADAM_B2 ** ADAM_STEP)
    delta = -ADAM_LR * (m_hat / (jnp.sqrt(v_hat) + ADAM_EPS) + ADAM_WD * w)
    return delta, m, v


def _adamw(w, g, m, v, name):
    shape = w.shape
    cols = shape[-1]
    rows = math.prod(shape[:-1])
    tr = _tile(rows, 512)
    two_d = lambda t: t.reshape(rows, cols)

    def body(w_ref, g_ref, m_ref, v_ref, d_ref, mo_ref, vo_ref):
        d_ref[...], mo_ref[...], vo_ref[...] = _adamw_math(w_ref[...], g_ref[...], m_ref[...], v_ref[...])

    out = jax.ShapeDtypeStruct((rows, cols), jnp.float32)
    outs = pl.pallas_call(
        body, name=name, grid=(rows // tr,), in_specs=[_rows(tr, cols)] * 4, out_specs=[_rows(tr, cols)] * 3,
        out_shape=[out, out, out], compiler_params=_params("parallel"),
    )(two_d(w), two_d(g), two_d(m), two_d(v))
    return [t.reshape(shape) for t in outs]


def _adamw_halves(w, mine, got, m, v, core, name):
    shape = w.shape
    cols = shape[-1]
    rows = math.prod(shape[:-1])
    rh = rows // 2
    tr = _tile(rh, 512)
    nbh = rh // tr
    two_d = lambda t: t.reshape(rows, cols)

    def body(c_ref, w_ref, a_ref, b_ref, m_ref, v_ref, g_ref, d_ref, mo_ref, vo_ref):
        g = jnp.where(pl.program_id(0) // nbh == c_ref[0], a_ref[...], b_ref[...])
        g_ref[...] = g
        d_ref[...], mo_ref[...], vo_ref[...] = _adamw_math(w_ref[...], g, m_ref[...], v_ref[...])

    row = pl.BlockSpec((tr, cols), lambda i, c_ref: (i, 0))

    def half(keep):
        return pl.BlockSpec((tr, cols), lambda i, c_ref: (jnp.where((i // nbh == c_ref[0]) == keep, i % nbh, 0), 0))

    out = jax.ShapeDtypeStruct((rows, cols), jnp.float32)
    outs = pl.pallas_call(
        body, name=name,
        grid_spec=pltpu.PrefetchScalarGridSpec(
            num_scalar_prefetch=1, grid=(rows // tr,),
            in_specs=[row, half(True), half(False), row, row], out_specs=[row] * 4),
        out_shape=[out] * 4, compiler_params=_params("arbitrary"),
    )(core, two_d(w), mine, got, two_d(m), two_d(v))
    return [t.reshape(shape) for t in outs]


def _ada_grad_adamw(cond_t, dm, w, m, v):
    L, _, ncol = w.shape
    tn = _tile(ncol, 512)

    def body(ct_ref, dm_ref, w_ref, m_ref, v_ref, g_ref, d_ref, mo_ref, vo_ref):
        g = ct_ref[:, 0:1] * dm_ref[0, 0:1, :]
        for b in range(1, N_DEV):
            g = g + ct_ref[:, b:b + 1] * dm_ref[0, b:b + 1, :]
        g_ref[0] = g
        d_ref[0], mo_ref[0], vo_ref[0] = _adamw_math(w_ref[0], g, m_ref[0], v_ref[0])

    wblk = pl.BlockSpec((1, D, tn), lambda l, j: (l, 0, j))
    out = jax.ShapeDtypeStruct(w.shape, jnp.float32)
    return pl.pallas_call(
        body, name="ada_grad_adamw", grid=(L, ncol // tn),
        in_specs=[_full((D, N_DEV)), pl.BlockSpec((1, N_DEV, tn), lambda l, j: (l, 0, j)), wblk, wblk, wblk],
        out_specs=[wblk] * 4, out_shape=[out] * 4, compiler_params=_params("parallel", "parallel"),
    )(cond_t, dm, w, m, v)


def _small_adamw(gathered, w, m, v):
    def body(ga_ref, w_ref, m_ref, v_ref, g_ref, d_ref, mo_ref, vo_ref):
        g = ga_ref[0]
        for dev in range(1, N_DEV):
            g = g + ga_ref[dev]
        g_ref[...] = g
        d_ref[...], mo_ref[...], vo_ref[...] = _adamw_math(w_ref[...], g, m_ref[...], v_ref[...])

    out = jax.ShapeDtypeStruct((SMALL_ROWS, LANES), jnp.float32)
    return pl.pallas_call(
        body, name="small_adamw", out_shape=[out] * 4,
        in_specs=[pl.BlockSpec(memory_space=pltpu.VMEM)] * 4, out_specs=[pl.BlockSpec(memory_space=pltpu.VMEM)] * 4,
    )(gathered, w, m, v)


def _one_hot_pick(arr, index, axis):
    n = arr.shape[axis]
    shape = [1] * arr.ndim
    shape[axis] = n
    hot = (jnp.arange(n) == index).astype(arr.dtype).reshape(shape)
    return jnp.sum(arr * hot, axis=axis)


def kernel(x, c, positions, w_ada, b_ada, g_mix, g_mlp, mla_w_dq, mla_g_q, mla_w_uq, mla_w_dkv, mla_g_kv, mla_w_ukv, mla_w_o, swa_w_qkv, swa_b_qkv, swa_sinks, swa_w_o, swa_b_o, w_ff1, w_ff2, g_final, loss_target, m_w_ada, m_b_ada, m_g_mix, m_g_mlp, m_mla_w_dq, m_mla_g_q, m_mla_w_uq, m_mla_w_dkv, m_mla_g_kv, m_mla_w_ukv, m_mla_w_o, m_swa_w_qkv, m_swa_b_qkv, m_swa_sinks, m_swa_w_o, m_swa_b_o, m_w_ff1, m_w_ff2, m_g_final, v_w_ada, v_b_ada, v_g_mix, v_g_mlp, v_mla_w_dq, v_mla_g_q, v_mla_w_uq, v_mla_w_dkv, v_mla_g_kv, v_mla_w_ukv, v_mla_w_o, v_swa_w_qkv, v_swa_b_qkv, v_swa_sinks, v_swa_w_o, v_swa_b_o, v_w_ff1, v_w_ff2, v_g_final):
    W = dict(w_ada=w_ada, b_ada=b_ada, g_mix=g_mix, g_mlp=g_mlp, mla_w_dq=mla_w_dq, mla_g_q=mla_g_q, mla_w_uq=mla_w_uq,
             mla_w_dkv=mla_w_dkv, mla_g_kv=mla_g_kv, mla_w_ukv=mla_w_ukv, mla_w_o=mla_w_o, swa_w_qkv=swa_w_qkv,
             swa_b_qkv=swa_b_qkv, swa_sinks=swa_sinks, swa_w_o=swa_w_o, swa_b_o=swa_b_o, w_ff1=w_ff1, w_ff2=w_ff2,
             g_final=g_final)
    M = dict(w_ada=m_w_ada, b_ada=m_b_ada, g_mix=m_g_mix, g_mlp=m_g_mlp, mla_w_dq=m_mla_w_dq, mla_g_q=m_mla_g_q,
             mla_w_uq=m_mla_w_uq, mla_w_dkv=m_mla_w_dkv, mla_g_kv=m_mla_g_kv, mla_w_ukv=m_mla_w_ukv, mla_w_o=m_mla_w_o,
             swa_w_qkv=m_swa_w_qkv, swa_b_qkv=m_swa_b_qkv, swa_sinks=m_swa_sinks, swa_w_o=m_swa_w_o, swa_b_o=m_swa_b_o,
             w_ff1=m_w_ff1, w_ff2=m_w_ff2, g_final=m_g_final)
    V = dict(w_ada=v_w_ada, b_ada=v_b_ada, g_mix=v_g_mix, g_mlp=v_g_mlp, mla_w_dq=v_mla_w_dq, mla_g_q=v_mla_g_q,
             mla_w_uq=v_mla_w_uq, mla_w_dkv=v_mla_w_dkv, mla_g_kv=v_mla_g_kv, mla_w_ukv=v_mla_w_ukv, mla_w_o=v_mla_w_o,
             swa_w_qkv=v_swa_w_qkv, swa_b_qkv=v_swa_b_qkv, swa_sinks=v_swa_sinks, swa_w_o=v_swa_w_o, swa_b_o=v_swa_b_o,
             w_ff1=v_w_ff1, w_ff2=v_w_ff2, g_final=v_g_final)
    order = list(W)
    names = list(SHARDED)
    core = lax.axis_index("c")
    chip = 2 * lax.axis_index("x") + lax.axis_index("y")
    dev = 2 * chip + core
    core_arr = core.astype(jnp.int32).reshape(1)
    chip_arr = chip.astype(jnp.int32).reshape(1)

    def whole(n, g, own):
        g = lax.dynamic_update_slice(g, own[None], (chip, 0, 0))
        if n in ("w_ff1", "w_ff2"):
            return g
        if n in COL_SPLIT:
            return g.transpose(1, 0, 2).reshape(g.shape[1], N_CHIPS * g.shape[2])
        return g.reshape(N_CHIPS * g.shape[1], g.shape[2])

    early = [n for n in names if n.startswith("mla_")]
    local = {n: W[n].astype(MXU_DTYPE).reshape(_view2d(n)) for n in early}
    wts = {n: whole(n, g, local[n]) for n, g in zip(early, _weight_gather([local[n] for n in early]))}

    nbq, nbo = BIASES["swa_b_qkv"] // N_CHIPS, BIASES["swa_b_o"] // N_CHIPS
    first = jnp.concatenate([c.reshape(-1), swa_b_qkv.reshape(-1), swa_b_o.reshape(-1),
                             jnp.zeros((16 * LANES - D - nbq - nbo,), jnp.float32)]).reshape(16, LANES)
    first_all = _all_gather(first).reshape(N_DEV, 16 * LANES)
    c_all = first_all[:, :D]
    south = first_all[0::2]
    wts["swa_b_qkv"] = south[:, D:D + nbq].reshape(1, N_CHIPS * nbq)
    wts["swa_b_o"] = south[:, D + nbq:D + nbq + nbo].reshape(1, N_CHIPS * nbo)
    cond_all, part = _ada_part(c_all, w_ada)
    ncol = w_ada.shape[2]
    part_all = _all_gather(part.reshape(-1, LANES)).reshape(N_DEV, DEPTH, N_DEV, ncol)
    mine = _one_hot_pick(part_all[0::2], dev, axis=2)
    mod = mine.transpose(1, 0, 2).reshape(DEPTH, N_CHIPS * ncol) + b_ada
    vecs = jnp.concatenate([mod.reshape(DEPTH, 6, D), g_mix[:, None, :], g_mlp[:, None, :]], axis=1)

    late = [("w_ff1", 0), ("w_ff2", 0), ("swa_w_qkv", None), ("swa_w_o", None), ("w_ff1", 1), ("w_ff2", 1)]
    late_local = [(W[n][0] if l is None else W[n][l]).astype(MXU_DTYPE) for n, l in late]
    send_sems, recv_sems, passed, lands, token = _late_gather_start(late_local, [vecs] + [wts[n] for n in early])

    def late_weights(after):
        got = _late_gather_wait(send_sems, recv_sems, passed, lands, after)
        out = {"w_ff1": [None] * DEPTH, "w_ff2": [None] * DEPTH}
        for (n, l), g, own in zip(late, got, late_local):
            if l is None:
                out[n] = whole(n, g, own)
            else:
                out[n][l] = whole(n, g, own)
        return out

    late_names = [n for n in names if n not in early]
    reduce_state = {}

    def on_late_grads(late_grads):
        s_sems, r_sems, passed_g, zones, tok = _pair_in_start([late_grads[n] for n in late_names])
        reduce_state.update(pair=(s_sems, r_sems, passed_g, zones))
        return tok

    def on_late_landed(after):
        gl, got = _pair_in_wait(*reduce_state["pair"], after)
        sums = [_pair_sum(g, s, core_arr, "pair_sum_" + n) for n, g, s in zip(late_names, gl, got)]
        s_sems, r_sems, parts, zones, tok = _exchange_start([s16 for _, s16 in sums])
        reduce_state.update(sums=sums, split=(s_sems, r_sems, parts, zones))
        return tok

    grad_x, grads, small = _sequence_step(
        x[0], loss_target[0], positions[0], vecs, mla_g_q + token[0, 0], mla_g_kv, swa_sinks, g_final, wts,
        late_weights, on_late_grads, on_late_landed)

    small["b_ada"] = small.pop("dmod")
    small_all = _all_gather(_pack_small(small))
    pk = lambda src: _pack_small({n: src[n] for n in SMALL if n != "loss" and n not in BIASES})
    g_small, d_small, m_small, v_small = [_unpack_small(t) for t in _small_adamw(small_all, pk(W), pk(M), pk(V))]
    off, n = _small_slots()["b_ada"]
    dmod_all = small_all.reshape(N_DEV, -1)[:, off:off + n].reshape(N_DEV, DEPTH, N_CHIPS, ncol)
    dm = _one_hot_pick(dmod_all, chip, axis=2).transpose(1, 0, 2)
    ada = _ada_grad_adamw(cond_all.T, dm, w_ada, m_w_ada, v_w_ada)

    gl = [grads[n] for n in early]
    got = _grad_pair_in(gl)
    sums = [_pair_sum(g, s, core_arr, "pair_sum_" + n) for n, g, s in zip(early, gl, got)]
    others = _grad_chip_exchange([s16 for _, s16 in sums])
    late_others = _exchange_wait(*reduce_state["split"], grad_x)
    sums, others = list(sums) + list(reduce_state["sums"]), list(others) + list(late_others)
    halves = [_chip_sum(s32, o, chip_arr, "chip_sum_" + n) for n, (s32, _), o in zip(names, sums, others)]
    sibling_halves = _grad_pair_out(halves)

    res = {"w_ada": ada}
    for n, mine_h, got_h in zip(names, halves, sibling_halves):
        res[n] = _adamw_halves(W[n], mine_h, got_h, M[n], V[n], core_arr, "adamw_" + n)
    for n, width in BIASES.items():
        g = _one_hot_pick(g_small[n].reshape(N_CHIPS, width // N_CHIPS), chip, axis=0).reshape(1, -1)
        res[n] = [g] + _adamw(W[n], g, M[n], V[n], "adamw_" + n)
    for name in order:
        if name not in res:
            res[name] = [t[name] for t in (g_small, d_small, m_small, v_small)]
    outs = [g_small["loss"], grad_x[None]]
    for k in range(4):
        outs += [res[name][k] for name in order]
    return tuple(outs)
```

```python
import functools
import math

import jax
import jax.numpy as jnp
import numpy as np
from jax import lax
from jax.experimental import pallas as pl
from jax.experimental.pallas import tpu as pltpu

D = 1024
DEPTH = 2
MLA_HEADS = 8
QK_NOPE = 128
QK_ROPE = 64
V_DIM = 128
Q_LORA = 384
KV_LORA = 256
ROPE_THETA = 10000.0
SWA_HEADS = 16
SWA_KV_HEADS = 4
SWA_HEAD_DIM = 64
SWA_GROUP = SWA_HEADS // SWA_KV_HEADS
WINDOW = 128
D_FF = 4 * D
EPS = 1e-6
ADAM_LR = 0.001
ADAM_B1 = 0.9
ADAM_B2 = 0.999
ADAM_EPS = 1e-08
ADAM_WD = 0.01
ADAM_STEP = 10

N_CHIPS = 4
N_DEV = 8
LANES = 128
QK_EXT = 256
MLA_SCALE = (QK_NOPE + QK_ROPE) ** -0.5
LOG2E = math.log2(math.e)
LN2 = math.log(2.0)
MLA_QSCALE = MLA_SCALE * LOG2E
ATTN_BLOCK = 1024
ATTN_SUB = 512
MLP_FWD_TILE = (1024, 512)
MLP_BWD_TILE = (512, 1024)
DW_TOKENS = 2048
SWA_SCALE = SWA_HEAD_DIM ** -0.5
NEG = -1e30
MXU_DTYPE = jnp.bfloat16
VMEM_LIMIT = 56 * 1024 * 1024

R_SH1, R_SC1, R_GT1, R_SH2, R_SC2, R_GT2, R_GMIX, R_GMLP = range(8)
R_BO = 6


def _tile(n, pref):
    if n <= pref:
        return n
    for t in range(pref, 7, -1):
        if n % t == 0 and t % 8 == 0:
            return t
    return n


def _dot(a, b):
    return jnp.dot(a, b, preferred_element_type=jnp.float32)


def _dot_nt(a, b):
    return lax.dot_general(a, b, (((1,), (1,)), ((), ())), preferred_element_type=jnp.float32)


def _dot_tn(a, b):
    return lax.dot_general(a, b, (((0,), (0,)), ((), ())), preferred_element_type=jnp.float32)


def _rms(x):
    r = lax.rsqrt(jnp.mean(x * x, axis=-1, keepdims=True) + EPS)
    return x * r, r


def _rms_bwd(dxhat, xhat, r):
    return r * (dxhat - xhat * jnp.mean(dxhat * xhat, axis=-1, keepdims=True))


def _rowsum(v):
    return jnp.sum(v, axis=0, keepdims=True)


def _params(*sem):
    return pltpu.CompilerParams(dimension_semantics=sem, vmem_limit_bytes=VMEM_LIMIT)


def _full(shape):
    nd = len(shape)
    return pl.BlockSpec(shape, lambda *_: (0,) * nd)


def _rows(tm, cols):
    return pl.BlockSpec((tm, cols), lambda i, *_: (i, 0))


def _modulate_bwd(dh, x, vec_ref, r_g, r_sc, r_sh, ps_ref, dres):
    xhat, r = _rms(x)
    g = vec_ref[r_g:r_g + 1, :]
    n = xhat * g
    ps_ref[r_sh:r_sh + 1, :] += _rowsum(dh)
    ps_ref[r_sc:r_sc + 1, :] += _rowsum(dh * n)
    dn = dh * (1.0 + vec_ref[r_sc:r_sc + 1, :])
    ps_ref[r_g:r_g + 1, :] += _rowsum(dn * xhat)
    return dres + _rms_bwd(dn * g, xhat, r)


def _mla_pre(x, vec, wcat, g_q, g_kv, wuq, wukv, cs):
    T = x.shape[0]
    tm = _tile(T, 512)
    H = MLA_HEADS

    def body(x_ref, vec_ref, wcat_ref, gq_ref, gkv_ref, wuq_ref, wukv_ref, cs_ref, h_ref, z_ref, q_ref, k_ref, v_ref):
        xhat, _ = _rms(x_ref[...])
        h = xhat * vec_ref[R_GMIX:R_GMIX + 1, :] * (1.0 + vec_ref[R_SC1:R_SC1 + 1, :]) + vec_ref[R_SH1:R_SH1 + 1, :]
        hb = h.astype(MXU_DTYPE)
        h_ref[...] = hb
        z = _dot(hb, wcat_ref[...])
        z_ref[...] = z
        cq = (_rms(z[:, :Q_LORA])[0] * gq_ref[...]).astype(MXU_DTYPE)
        ckv = (_rms(z[:, Q_LORA:Q_LORA + KV_LORA])[0] * gkv_ref[...]).astype(MXU_DTYPE)
        cs_t = cs_ref[...]
        t = z[:, Q_LORA + KV_LORA:] * cs_t
        k_rope = (t + pltpu.roll(t, QK_ROPE, axis=1)).astype(MXU_DTYPE)
        low = lax.broadcasted_iota(jnp.int32, (1, LANES), 1) < QK_ROPE
        for hd in range(H):
            qf = _dot(cq, wuq_ref[hd])
            tq = qf[:, QK_NOPE:] * cs_t
            tq = tq + pltpu.roll(tq, QK_ROPE, axis=1)
            q_ref[hd, :, :QK_NOPE] = (qf[:, :QK_NOPE] * MLA_QSCALE).astype(MXU_DTYPE)
            q_ref[hd, :, QK_NOPE:] = jnp.where(low, tq * MLA_QSCALE, 0.0).astype(MXU_DTYPE)
            kvf = _dot(ckv, wukv_ref[hd])
            k_ref[hd, :, :QK_NOPE] = kvf[:, :QK_NOPE].astype(MXU_DTYPE)
            k_ref[hd, :, QK_NOPE:] = k_rope
            v_ref[hd] = kvf[:, QK_NOPE:].astype(MXU_DTYPE)

    zc = wcat.shape[1]
    return pl.pallas_call(
        body, name="mla_pre", grid=(T // tm,),
        in_specs=[_rows(tm, D), _full((8, D)), _full(wcat.shape), _full(g_q.shape), _full(g_kv.shape),
                  _full(wuq.shape), _full(wukv.shape), _rows(tm, LANES)],
        out_specs=[_rows(tm, D), _rows(tm, zc),
                   pl.BlockSpec((H, tm, QK_EXT), lambda i: (0, i, 0)),
                   pl.BlockSpec((H, tm, QK_EXT), lambda i: (0, i, 0)),
                   pl.BlockSpec((H, tm, V_DIM), lambda i: (0, i, 0))],
        out_shape=[jax.ShapeDtypeStruct((T, D), MXU_DTYPE), jax.ShapeDtypeStruct((T, zc), jnp.float32),
                   jax.ShapeDtypeStruct((H, T, QK_EXT), MXU_DTYPE), jax.ShapeDtypeStruct((H, T, QK_EXT), MXU_DTYPE),
                   jax.ShapeDtypeStruct((H, T, V_DIM), MXU_DTYPE)],
        compiler_params=_params("parallel"),
    )(x, vec, wcat, g_q, g_kv, wuq, wukv, cs)


def _mla_attn_fwd(q, k, v):
    H, T, _ = q.shape
    tb = _tile(T, ATTN_BLOCK)
    sub = min(ATTN_SUB, tb)
    ns, nb = tb // sub, T // tb

    def body(q_ref, k_ref, v_ref, o_ref, lse_ref, m_sc, l_sc, acc_sc):
        qi, kj = pl.program_id(1), pl.program_id(2)

        @pl.when(kj == 0)
        def _():
            m_sc[...] = jnp.full_like(m_sc, NEG)
            l_sc[...] = jnp.zeros_like(l_sc)
            acc_sc[...] = jnp.zeros_like(acc_sc)

        def update(r, kk, masked):
            rows, keys = pl.ds(r * sub, sub), pl.ds(kk * sub, sub)
            s = _dot_nt(q_ref[0, rows, :], k_ref[0, keys, :])
            if masked:
                row = lax.broadcasted_iota(jnp.int32, (sub, sub), 0)
                col = lax.broadcasted_iota(jnp.int32, (sub, sub), 1)
                s = jnp.where(col <= row, s, NEG)
            m_prev = m_sc[rows, :]
            m_new = jnp.maximum(m_prev, jnp.max(s, axis=1, keepdims=True))
            alpha = jnp.exp2(m_prev - m_new)
            p = jnp.exp2(s - jnp.tile(m_new, (1, sub // LANES)))
            l_sc[rows, :] = alpha * l_sc[rows, :] + jnp.sum(p, axis=1, keepdims=True)
            acc_sc[rows, :] = alpha * acc_sc[rows, :] + _dot(p.astype(MXU_DTYPE), v_ref[0, keys, :])
            m_sc[rows, :] = m_new

        @pl.when(kj < qi)
        def _():
            for kk in range(ns):
                for r in range(ns):
                    update(r, kk, False)

        @pl.when(kj == qi)
        def _():
            for kk in range(ns):
                for r in range(kk, ns):
                    update(r, kk, r == kk)
            l = l_sc[...]
            o_ref[...] = (acc_sc[...] / l).astype(o_ref.dtype)
            lse = m_sc[...] + jnp.log2(l)
            pick = (lax.broadcasted_iota(jnp.int32, (8, LANES), 1) == 0).astype(jnp.float32)
            row = lax.dot_general(pick, lse, (((1,), (1,)), ((), ())), precision=lax.Precision.HIGHEST,
                                  preferred_element_type=jnp.float32)
            lse_ref[0] = row[0:1, :]

    kv_idx = lambda h, i, j: (h, jnp.minimum(i, j), 0)
    return pl.pallas_call(
        body, name="mla_attn_fwd", grid=(H, nb, nb),
        in_specs=[pl.BlockSpec((1, tb, QK_EXT), lambda h, i, j: (h, i, 0)),
                  pl.BlockSpec((1, tb, QK_EXT), kv_idx),
                  pl.BlockSpec((1, tb, V_DIM), kv_idx)],
        out_specs=[pl.BlockSpec((tb, V_DIM), lambda h, i, j: (i, h)),
                   pl.BlockSpec((1, 1, tb), lambda h, i, j: (h, 0, i))],
        out_shape=[jax.ShapeDtypeStruct((T, H * V_DIM), MXU_DTYPE), jax.ShapeDtypeStruct((H, 1, T), jnp.float32)],
        scratch_shapes=[pltpu.VMEM((tb, LANES), jnp.float32), pltpu.VMEM((tb, LANES), jnp.float32),
                        pltpu.VMEM((tb, V_DIM), jnp.float32)],
        compiler_params=_params("parallel", "parallel", "arbitrary"),
    )(q, k, v)


def _post_attn(o, x, w_o, bias, vec, o_transposed=False):
    T = x.shape[0]
    tm = _tile(T, 512)
    o_spec = pl.BlockSpec((D, tm), lambda i: (0, i)) if o_transposed else _rows(tm, D)

    def body(o_ref, x_ref, w_ref, b_ref, vec_ref, y_ref, xm_ref, h_ref):
        y = (_dot_tn if o_transposed else _dot)(o_ref[...], w_ref[...]) + b_ref[...]
        y_ref[...] = y.astype(y_ref.dtype)
        xm = x_ref[...] + vec_ref[R_GT1:R_GT1 + 1, :] * y
        xm_ref[...] = xm
        xhat, _ = _rms(xm)
        h = xhat * vec_ref[R_GMLP:R_GMLP + 1, :] * (1.0 + vec_ref[R_SC2:R_SC2 + 1, :]) + vec_ref[R_SH2:R_SH2 + 1, :]
        h_ref[...] = h.astype(h_ref.dtype)

    return pl.pallas_call(
        body, name="post_attn", grid=(T // tm,),
        in_specs=[o_spec, _rows(tm, D), _full((D, D)), _full((1, D)), _full((8, D))],
        out_specs=[_rows(tm, D), _rows(tm, D), _rows(tm, D)],
        out_shape=[jax.ShapeDtypeStruct((T, D), MXU_DTYPE), jax.ShapeDtypeStruct((T, D), jnp.float32),
                   jax.ShapeDtypeStruct((T, D), MXU_DTYPE)],
        compiler_params=_params("parallel"),
    )(o, x, w_o, bias, vec)


def _ff_specs(tf):
    per = D_FF // N_CHIPS // tf
    w1 = pl.BlockSpec((None, D, tf), lambda i, f: (f // per, 0, f % per))
    w2 = pl.BlockSpec((None, tf, D), lambda i, f: (f // per, f % per, 0))
    return w1, w2


def _mlp_fwd(h2, w1, w2, xm, vec):
    T = h2.shape[0]
    tm = _tile(T, MLP_FWD_TILE[0])
    tf = _tile(D_FF // N_CHIPS, MLP_FWD_TILE[1])
    nf = D_FF // tf
    w1_spec, w2_spec = _ff_specs(tf)

    def body(h_ref, w1_ref, w2_ref, xm_ref, vec_ref, a_ref, r_ref, y_ref, xo_ref, acc):
        f = pl.program_id(1)

        @pl.when(f == 0)
        def _():
            acc[...] = jnp.zeros_like(acc)

        u = jnp.maximum(_dot(h_ref[...], w1_ref[...]), 0.0)
        ab = (u * u).astype(MXU_DTYPE)
        a_ref[...] = ab
        r_ref[...] = (2.0 * u).astype(MXU_DTYPE)
        acc[...] += _dot(ab, w2_ref[...])

        @pl.when(f == nf - 1)
        def _():
            y = acc[...]
            y_ref[...] = y.astype(y_ref.dtype)
            xo_ref[...] = xm_ref[...] + vec_ref[R_GT2:R_GT2 + 1, :] * y

    return pl.pallas_call(
        body, name="mlp_fwd", grid=(T // tm, nf),
        in_specs=[_rows(tm, D), w1_spec, w2_spec, _rows(tm, D), _full((8, D))],
        out_specs=[pl.BlockSpec((tm, tf), lambda i, f: (i, f)), pl.BlockSpec((tm, tf), lambda i, f: (i, f)),
                   _rows(tm, D), _rows(tm, D)],
        out_shape=[jax.ShapeDtypeStruct((T, D_FF), MXU_DTYPE), jax.ShapeDtypeStruct((T, D_FF), MXU_DTYPE),
                   jax.ShapeDtypeStruct((T, D), MXU_DTYPE), jax.ShapeDtypeStruct((T, D), jnp.float32)],
        scratch_shapes=[pltpu.VMEM((tm, D), jnp.float32)],
        compiler_params=_params("parallel", "arbitrary"),
    )(h2, w1, w2, xm, vec)


def _swa_pre(x, vec, w_qkv, b_qkv):
    T = x.shape[0]
    tm = _tile(T, 512)
    nq = SWA_HEADS * SWA_HEAD_DIM
    nk = SWA_KV_HEADS * SWA_HEAD_DIM
    wq_t, w_kv = w_qkv[:, :nq].T, w_qkv[:, nq:]
    bq_col, b_kv = b_qkv[:, :nq].reshape(nq, 1), b_qkv[:, nq:]

    def body(x_ref, vec_ref, wq_ref, wkv_ref, bq_ref, bkv_ref, h_ref, qt_ref, k_ref, v_ref):
        xhat, _ = _rms(x_ref[...])
        h = xhat * vec_ref[R_GMIX:R_GMIX + 1, :] * (1.0 + vec_ref[R_SC1:R_SC1 + 1, :]) + vec_ref[R_SH1:R_SH1 + 1, :]
        hb = h.astype(MXU_DTYPE)
        h_ref[...] = hb
        qt_ref[...] = ((_dot_nt(wq_ref[...], hb) + bq_ref[...]) * SWA_SCALE).astype(MXU_DTYPE)
        kv = _dot(hb, wkv_ref[...]) + bkv_ref[...]
        k_ref[...] = kv[:, :nk].astype(MXU_DTYPE)
        v_ref[...] = kv[:, nk:].astype(MXU_DTYPE)

    return pl.pallas_call(
        body, name="swa_pre", grid=(T // tm,),
        in_specs=[_rows(tm, D), _full((8, D)), _full(wq_t.shape), _full(w_kv.shape), _full(bq_col.shape),
                  _full(b_kv.shape)],
        out_specs=[_rows(tm, D), pl.BlockSpec((nq, tm), lambda i: (0, i)), _rows(tm, nk), _rows(tm, nk)],
        out_shape=[jax.ShapeDtypeStruct((T, D), MXU_DTYPE), jax.ShapeDtypeStruct((nq, T), MXU_DTYPE),
                   jax.ShapeDtypeStruct((T, nk), MXU_DTYPE), jax.ShapeDtypeStruct((T, nk), MXU_DTYPE)],
        compiler_params=_params("parallel"),
    )(x, vec, wq_t, w_kv, bq_col, b_kv)


def _swa_bias():
    W = WINDOW
    slopes = 2.0 ** (-8.0 * np.arange(1, SWA_HEADS + 1) / SWA_HEADS)
    dist = W + np.arange(W)[None, :] - np.arange(2 * W)[:, None]
    inside = (dist >= 0) & (dist < W)
    bias = np.where(inside[None], -slopes[:, None, None] * dist[None].astype(np.float64), NEG)
    bias = bias.reshape(SWA_KV_HEADS, SWA_GROUP, 2 * W, W).transpose(0, 2, 1, 3)
    return jnp.asarray(bias.reshape(SWA_KV_HEADS, 2 * W, SWA_GROUP * W), jnp.float32)


SWA_STEP_BLOCKS = 4


def _swa_blocks(T):
    nb = T // WINDOW
    return next(b for b in (SWA_STEP_BLOCKS, 2, 1) if nb % b == 0)


def _swa_views(b, qt_ref, kp_ref, kc_ref):
    W = WINDOW
    prev = kp_ref if b == 0 else kc_ref.at[pl.ds((b - 1) * W, W), :]
    return qt_ref.at[:, pl.ds(b * W, W)], prev, kc_ref.at[pl.ds(b * W, W), :]


def _swa_probs(has_prev, kh, qt_ref, kp_ref, kc_ref, bias_ref, sink_ref):
    W, Dh, G = WINDOW, SWA_HEAD_DIM, SWA_GROUP
    qt = jnp.concatenate([qt_ref[(kh * G + g) * Dh:(kh * G + g + 1) * Dh, :] for g in range(G)], axis=1)
    kb = jnp.concatenate([kp_ref[:, kh * Dh:(kh + 1) * Dh], kc_ref[:, kh * Dh:(kh + 1) * Dh]], axis=0)
    s = _dot(kb, qt) + bias_ref[kh]
    if has_prev is not True:
        key = lax.broadcasted_iota(jnp.int32, (2 * W, 1), 0)
        s = jnp.where((key >= W) | has_prev, s, NEG)
    sink = sink_ref[kh]
    m = jnp.maximum(jnp.max(s, axis=0, keepdims=True), sink)
    p = jnp.exp(s - m)
    p_sink = jnp.exp(sink - m)
    inv = 1.0 / (jnp.sum(p, axis=0, keepdims=True) + p_sink)
    return qt, kb, p * inv, p_sink * inv


def _swa_attn_fwd(qt, k, v, bias, sink_rows):
    T = qt.shape[1]
    W, Dh, G, Hk = WINDOW, SWA_HEAD_DIM, SWA_GROUP, SWA_KV_HEADS
    nk = Hk * Dh

    nb = _swa_blocks(T)

    def body(qt_ref, kp_ref, kc_ref, vp_ref, vc_ref, bias_ref, sink_ref, ot_ref):
        n = pl.program_id(0)
        for b in range(nb):
            q_b, kp_b, kc_b = _swa_views(b, qt_ref, kp_ref, kc_ref)
            _, vp_b, vc_b = _swa_views(b, qt_ref, vp_ref, vc_ref)
            for kh in range(Hk):
                _, _, pn, _ = _swa_probs(True if b else n > 0, kh, q_b, kp_b, kc_b, bias_ref, sink_ref)
                vb = jnp.concatenate([vp_b[:, kh * Dh:(kh + 1) * Dh], vc_b[:, kh * Dh:(kh + 1) * Dh]], axis=0)
                ot = _dot_tn(vb, pn.astype(MXU_DTYPE))
                for g in range(G):
                    rows = pl.ds((kh * G + g) * Dh, Dh)
                    ot_ref[rows, pl.ds(b * W, W)] = ot[:, g * W:(g + 1) * W].astype(ot_ref.dtype)

    prev = lambda n: (jnp.maximum(n * nb - 1, 0), 0)
    cur = lambda n: (n, 0)
    col = lambda n: (0, n)
    return pl.pallas_call(
        body, name="swa_attn_fwd", grid=(T // (nb * W),),
        in_specs=[pl.BlockSpec((D, nb * W), col), pl.BlockSpec((W, nk), prev), pl.BlockSpec((nb * W, nk), cur),
                  pl.BlockSpec((W, nk), prev), pl.BlockSpec((nb * W, nk), cur), _full(bias.shape),
                  _full(sink_rows.shape)],
        out_specs=pl.BlockSpec((D, nb * W), col),
        out_shape=jax.ShapeDtypeStruct((D, T), MXU_DTYPE),
        compiler_params=_params("parallel"),
    )(qt, k, k, v, v, bias, sink_rows)


def _final_loss(x, tgt, g):
    T = x.shape[0]
    tm = _tile(T, 512)

    def body(x_ref, t_ref, g_ref, loss_ref, dx_ref, dg_ref):
        @pl.when(pl.program_id(0) == 0)
        def _():
            loss_ref[...] = jnp.zeros_like(loss_ref)
            dg_ref[...] = jnp.zeros_like(dg_ref)

        xhat, r = _rms(x_ref[...])
        gv = g_ref[...]
        e = xhat * gv - t_ref[...]
        loss_ref[...] += 0.5 * jnp.sum(jnp.mean(e * e, axis=-1, keepdims=True), axis=0, keepdims=True)
        dy = e * (1.0 / D)
        dg_ref[...] += _rowsum(dy * xhat)
        dx_ref[...] = _rms_bwd(dy * gv, xhat, r)

    return pl.pallas_call(
        body, name="final_loss", grid=(T // tm,),
        in_specs=[_rows(tm, D), _rows(tm, D), _full((1, D))],
        out_specs=[_full((8, LANES)), _rows(tm, D), _full((1, D))],
        out_shape=[jax.ShapeDtypeStruct((8, LANES), jnp.float32), jax.ShapeDtypeStruct((T, D), jnp.float32),
                   jax.ShapeDtypeStruct((1, D), jnp.float32)],
        compiler_params=_params("arbitrary"),
    )(x, tgt, g)


def _mlp_bwd(dxo, y2, a, w1, w2, xm, vec):
    T = dxo.shape[0]
    tm = _tile(T, MLP_BWD_TILE[0])
    tf = _tile(D_FF // N_CHIPS, MLP_BWD_TILE[1])
    nf = D_FF // tf
    w1_spec, w2_spec = _ff_specs(tf)

    def body(dxo_ref, y_ref, a_ref, w1_ref, w2_ref, xm_ref, vec_ref, du_ref, dy_ref, dxm_ref, ps_ref, dyb, acc):
        i, f = pl.program_id(0), pl.program_id(1)

        @pl.when((i == 0) & (f == 0))
        def _():
            ps_ref[...] = jnp.zeros_like(ps_ref)

        @pl.when(f == 0)
        def _():
            dxo_t = dxo_ref[...]
            d = (dxo_t * vec_ref[R_GT2:R_GT2 + 1, :]).astype(MXU_DTYPE)
            dyb[...] = d
            dy_ref[...] = d
            acc[...] = jnp.zeros_like(acc)
            ps_ref[R_GT2:R_GT2 + 1, :] += _rowsum(dxo_t * y_ref[...].astype(jnp.float32))

        da = _dot_nt(dyb[...], w2_ref[...])
        dub = (da * a_ref[...].astype(jnp.float32)).astype(MXU_DTYPE)
        du_ref[...] = dub
        acc[...] += _dot_nt(dub, w1_ref[...])

        @pl.when(f == nf - 1)
        def _():
            dxm_ref[...] = _modulate_bwd(acc[...], xm_ref[...], vec_ref, R_GMLP, R_SC2, R_SH2, ps_ref, dxo_ref[...])

    return pl.pallas_call(
        body, name="mlp_bwd", grid=(T // tm, nf),
        in_specs=[_rows(tm, D), _rows(tm, D), pl.BlockSpec((tm, tf), lambda i, f: (i, f)), w1_spec, w2_spec,
                  _rows(tm, D), _full((8, D))],
        out_specs=[pl.BlockSpec((tm, tf), lambda i, f: (i, f)), _rows(tm, D), _rows(tm, D), _full((8, D))],
        out_shape=[jax.ShapeDtypeStruct((T, D_FF), MXU_DTYPE), jax.ShapeDtypeStruct((T, D), MXU_DTYPE),
                   jax.ShapeDtypeStruct((T, D), jnp.float32), jax.ShapeDtypeStruct((8, D), jnp.float32)],
        scratch_shapes=[pltpu.VMEM((tm, D), MXU_DTYPE), pltpu.VMEM((tm, D), jnp.float32)],
        compiler_params=_params("arbitrary", "arbitrary"),
    )(dxo, y2, a, w1, w2, xm, vec)


def _mm_tn(a, g, name, split=None, layers=1, layer=0, into=None, a_transposed=False):
    K, T = a.shape if a_transposed else a.shape[::-1]
    N = g.shape[1]
    kq = K // N_CHIPS if split == "rows" else K
    nq = N // N_CHIPS if split == "cols" else N
    bk, bn, bt = _tile(kq, 1024), _tile(nq, 1024), _tile(T, DW_TOKENS)
    if nq % bn or bn % LANES:
        bn = nq
    kper, nper = kq // bk, nq // bn

    def body(*refs):
        a_ref, g_ref, o_ref = refs[0], refs[1], refs[-1]

        @pl.when(pl.program_id(2) == 0)
        def _():
            o_ref[...] = jnp.zeros_like(o_ref)

        o_ref[...] += (_dot if a_transposed else _dot_tn)(a_ref[...], g_ref[...])

    a_spec = pl.BlockSpec((bk, bt), lambda k, n, t: (k, t)) if a_transposed else pl.BlockSpec((bt, bk), lambda k, n, t: (t, k))
    in_specs = [a_spec, pl.BlockSpec((bt, bn), lambda k, n, t: (t, n))]
    args = [a, g]
    aliases = {}
    if split is None:
        out_spec = pl.BlockSpec((bk, bn), lambda k, n, t: (k, n))
        out_shape = jax.ShapeDtypeStruct((K, N), jnp.float32)
    else:
        if split == "cols":
            idx = lambda k, n, t: (n // nper, layer, k, n % nper)
        else:
            idx = lambda k, n, t: (k // kper, layer, k % kper, n)
        out_spec = pl.BlockSpec((None, None, bk, bn), idx)
        out_shape = jax.ShapeDtypeStruct((N_CHIPS, layers, kq, nq), jnp.float32)
        if into is not None:
            in_specs.append(pl.BlockSpec(memory_space=pl.ANY))
            args.append(into)
            aliases = {2: 0}
    return pl.pallas_call(
        body, name=name, grid=(K // bk, N // bn, T // bt), in_specs=in_specs, out_specs=out_spec, out_shape=out_shape,
        input_output_aliases=aliases, compiler_params=_params("parallel", "parallel", "arbitrary"),
    )(*args)


def _attn_out_bwd(dxm, y1, o, w_o, vec, with_delta):
    T = dxm.shape[0]
    tm = _tile(T, 512)
    H = MLA_HEADS

    def body(dxm_ref, y_ref, w_ref, vec_ref, *refs):
        o_ref = refs[0] if with_delta else None
        dy_ref, do_ref, ps_ref, *delta_ref = refs[1:] if with_delta else refs

        @pl.when(pl.program_id(0) == 0)
        def _():
            ps_ref[...] = jnp.zeros_like(ps_ref)

        dxm_t = dxm_ref[...]
        dy = dxm_t * vec_ref[R_GT1:R_GT1 + 1, :]
        ps_ref[R_GT1:R_GT1 + 1, :] += _rowsum(dxm_t * y_ref[...].astype(jnp.float32))
        ps_ref[R_BO:R_BO + 1, :] += _rowsum(dy)
        dyb = dy.astype(MXU_DTYPE)
        dy_ref[...] = dyb
        if not with_delta:
            do_ref[...] = _dot_nt(w_ref[...], dyb).astype(do_ref.dtype)
        else:
            do = _dot_nt(dyb, w_ref[...])
            do_ref[...] = do.astype(do_ref.dtype)
            of = o_ref[...].astype(jnp.float32)
            ones = jnp.ones((8, V_DIM), jnp.float32)
            for hd in range(H):
                sl = slice(hd * V_DIM, (hd + 1) * V_DIM)
                d = lax.dot_general(ones, do[:, sl] * of[:, sl], (((1,), (1,)), ((), ())),
                                    precision=lax.Precision.HIGHEST, preferred_element_type=jnp.float32)
                delta_ref[0][hd] = d[0:1, :]

    out_specs = [_rows(tm, D), _rows(tm, D), _full((8, D))]
    out_shape = [jax.ShapeDtypeStruct((T, D), MXU_DTYPE), jax.ShapeDtypeStruct((T, D), MXU_DTYPE),
                 jax.ShapeDtypeStruct((8, D), jnp.float32)]
    if not with_delta:
        out_specs[1] = pl.BlockSpec((D, tm), lambda i: (0, i))
        out_shape[1] = jax.ShapeDtypeStruct((D, T), MXU_DTYPE)
    if with_delta:
        out_specs.append(pl.BlockSpec((H, 1, tm), lambda i: (0, 0, i)))
        out_shape.append(jax.ShapeDtypeStruct((H, 1, T), jnp.float32))
    return pl.pallas_call(
        body, name="attn_out_bwd_mla" if with_delta else "attn_out_bwd_swa", grid=(T // tm,),
        in_specs=[_rows(tm, D), _rows(tm, D), _full((D, D)), _full((8, D))] + ([_rows(tm, D)] if with_delta else []),
        out_specs=out_specs, out_shape=out_shape,
        compiler_params=_params("arbitrary"),
    )(dxm, y1, w_o, vec, *([o] if with_delta else []))


def _mla_attn_bwd(q, k, v, do, lse, delta):
    H, T, _ = q.shape
    tb = _tile(T, ATTN_BLOCK)
    sub = min(ATTN_SUB, tb)
    ns, nb = tb // sub, T // tb

    def body(q_ref, k_ref, v_ref, do_ref, lse_ref, dl_ref, dq_ref, dk_ref, dv_ref, dk_acc, dv_acc):
        j, i = pl.program_id(1), pl.program_id(2)

        @pl.when((j == 0) & (i == 0))
        def _():
            dq_ref[...] = jnp.zeros_like(dq_ref)

        def update(kk, r, masked):
            keys, rows = pl.ds(kk * sub, sub), pl.ds(r * sub, sub)
            kb, qb, dob = k_ref[0, keys, :], q_ref[0, rows, :], do_ref[rows, :]
            st = _dot_nt(kb, qb)
            if masked:
                row = lax.broadcasted_iota(jnp.int32, (sub, sub), 0)
                col = lax.broadcasted_iota(jnp.int32, (sub, sub), 1)
                st = jnp.where(row <= col, st, NEG)
            pt = jnp.exp2(st - lse_ref[0, :, rows])
            dv_acc[keys, :] += _dot(pt.astype(MXU_DTYPE), dob)
            dpt = _dot_nt(v_ref[0, keys, :], dob)
            dst = (pt * (dpt - dl_ref[0, :, rows])).astype(MXU_DTYPE)
            dk_acc[keys, :] += _dot(dst, qb)
            q_rows = pl.ds(pl.multiple_of(i * tb + r * sub, sub), sub)
            dq_ref[0, q_rows, :] += _dot_tn(dst, kb)

        @pl.when(i == j)
        def _():
            dk_acc[...] = jnp.zeros_like(dk_acc)
            dv_acc[...] = jnp.zeros_like(dv_acc)
            for r in range(ns):
                for kk in range(r + 1):
                    update(kk, r, kk == r)

        @pl.when(i > j)
        def _():
            for r in range(ns):
                for kk in range(ns):
                    update(kk, r, False)

        @pl.when(i == nb - 1)
        def _():
            dk_ref[0] = (dk_acc[...] * LN2).astype(dk_ref.dtype)
            dv_ref[0] = dv_acc[...].astype(dv_ref.dtype)

    q_idx = lambda h, j, i: (h, jnp.maximum(i, j), 0)
    kv_idx = lambda h, j, i: (h, j, 0)
    stat_idx = lambda h, j, i: (h, 0, jnp.maximum(i, j))
    return pl.pallas_call(
        body, name="mla_attn_bwd", grid=(H, nb, nb),
        in_specs=[pl.BlockSpec((1, tb, QK_EXT), q_idx), pl.BlockSpec((1, tb, QK_EXT), kv_idx),
                  pl.BlockSpec((1, tb, V_DIM), kv_idx),
                  pl.BlockSpec((tb, V_DIM), lambda h, j, i: (jnp.maximum(i, j), h)),
                  pl.BlockSpec((1, 1, tb), stat_idx), pl.BlockSpec((1, 1, tb), stat_idx)],
        out_specs=[pl.BlockSpec((1, T, QK_EXT), lambda h, j, i: (h, 0, 0)),
                   pl.BlockSpec((1, tb, QK_EXT), kv_idx), pl.BlockSpec((1, tb, V_DIM), kv_idx)],
        out_shape=[jax.ShapeDtypeStruct((H, T, QK_EXT), jnp.float32), jax.ShapeDtypeStruct((H, T, QK_EXT), MXU_DTYPE),
                   jax.ShapeDtypeStruct((H, T, V_DIM), MXU_DTYPE)],
        scratch_shapes=[pltpu.VMEM((tb, QK_EXT), jnp.float32), pltpu.VMEM((tb, V_DIM), jnp.float32)],
        compiler_params=_params("parallel", "arbitrary", "arbitrary"),
    )(q, k, v, do, lse, delta)


def _mla_pre_bwd(x, dxm, vec, hb, z, dq, dk, dv, cs, wcat, g_q, g_kv, wuq, wukv):
    T = x.shape[0]
    tm = _tile(T, 256)
    H = MLA_HEADS
    zc = wcat.shape[1]

    def body(x_ref, dxm_ref, vec_ref, h_ref, z_ref, dq_ref, dk_ref, dv_ref, cs_ref, wcat_ref, gq_ref, gkv_ref,
             wuq_ref, wukv_ref, dx_ref, ps_ref, dgq_ref, dgkv_ref, dwcat_ref, dwuq_ref, dwukv_ref):
        @pl.when(pl.program_id(0) == 0)
        def _():
            for ref in (ps_ref, dgq_ref, dgkv_ref, dwcat_ref, dwuq_ref, dwukv_ref):
                ref[...] = jnp.zeros_like(ref)

        z = z_ref[...]
        cs_t = cs_ref[...]
        cqhat, rq = _rms(z[:, :Q_LORA])
        ckhat, rk = _rms(z[:, Q_LORA:Q_LORA + KV_LORA])
        gq, gkv = gq_ref[...], gkv_ref[...]
        cq = (cqhat * gq).astype(MXU_DTYPE)
        ckv = (ckhat * gkv).astype(MXU_DTYPE)
        dcq = jnp.zeros((tm, Q_LORA), jnp.float32)
        dckv = jnp.zeros((tm, KV_LORA), jnp.float32)
        dkr = jnp.zeros((tm, LANES), jnp.float32)
        for hd in range(H):
            dqh = dq_ref[hd] * MLA_SCALE
            gqh = jnp.concatenate([dqh[:, :QK_NOPE], dqh[:, QK_NOPE:] * cs_t], axis=1).astype(MXU_DTYPE)
            dcq += _dot_nt(gqh, wuq_ref[hd])
            dwuq_ref[hd] += _dot_tn(cq, gqh)
            dkh = dk_ref[hd]
            gkvh = jnp.concatenate([dkh[:, :QK_NOPE], dv_ref[hd]], axis=1)
            dckv += _dot_nt(gkvh, wukv_ref[hd])
            dwukv_ref[hd] += _dot_tn(ckv, gkvh)
            dkr += dkh[:, QK_NOPE:].astype(jnp.float32)
        dgq_ref[...] += _rowsum(dcq * cqhat)
        dgkv_ref[...] += _rowsum(dckv * ckhat)
        dcq_pre = _rms_bwd(dcq * gq, cqhat, rq)
        dckv_pre = _rms_bwd(dckv * gkv, ckhat, rk)
        dkr2 = (dkr + pltpu.roll(dkr, QK_ROPE, axis=1)) * cs_t
        dz = jnp.concatenate([dcq_pre, dckv_pre, dkr2], axis=1).astype(MXU_DTYPE)
        dwcat_ref[...] += _dot_tn(h_ref[...], dz)
        dh = _dot_nt(dz, wcat_ref[...])
        dx_ref[...] = _modulate_bwd(dh, x_ref[...], vec_ref, R_GMIX, R_SC1, R_SH1, ps_ref, dxm_ref[...])

    hblk = lambda w: pl.BlockSpec((H, tm, w), lambda i: (0, i, 0))
    return pl.pallas_call(
        body, name="mla_pre_bwd", grid=(T // tm,),
        in_specs=[_rows(tm, D), _rows(tm, D), _full((8, D)), _rows(tm, D), _rows(tm, zc), hblk(QK_EXT), hblk(QK_EXT),
                  hblk(V_DIM), _rows(tm, LANES), _full(wcat.shape), _full(g_q.shape), _full(g_kv.shape),
                  _full(wuq.shape), _full(wukv.shape)],
        out_specs=[_rows(tm, D), _full((8, D)), _full(g_q.shape), _full(g_kv.shape), _full(wcat.shape),
                   _full(wuq.shape), _full(wukv.shape)],
        out_shape=[jax.ShapeDtypeStruct((T, D), jnp.float32), jax.ShapeDtypeStruct((8, D), jnp.float32),
                   jax.ShapeDtypeStruct(g_q.shape, jnp.float32), jax.ShapeDtypeStruct(g_kv.shape, jnp.float32),
                   jax.ShapeDtypeStruct(wcat.shape, jnp.float32), jax.ShapeDtypeStruct(wuq.shape, jnp.float32),
                   jax.ShapeDtypeStruct(wukv.shape, jnp.float32)],
        compiler_params=_params("arbitrary"),
    )(x, dxm, vec, hb, z, dq, dk, dv, cs, wcat, g_q, g_kv, wuq, wukv)


def _swa_attn_bwd(qt, k, v, dot_, bias, sink_rows):
    T = qt.shape[1]
    W, Dh, G, Hk = WINDOW, SWA_HEAD_DIM, SWA_GROUP, SWA_KV_HEADS
    nk = Hk * Dh
    nb = _swa_blocks(T)

    def body(qt_ref, kp_ref, kc_ref, vp_ref, vc_ref, dot_ref, bias_ref, sink_ref, dqt_ref, dk_ref, dv_ref, dsink_ref):
        n = pl.program_id(0)

        @pl.when(n == 0)
        def _():
            dk_ref[...] = jnp.zeros_like(dk_ref)
            dv_ref[...] = jnp.zeros_like(dv_ref)
            dsink_ref[...] = jnp.zeros_like(dsink_ref)

        def add_rows(first_row, dkb_part, dvb_part):
            rows = pl.ds(pl.multiple_of(first_row, W), W)
            dk_ref[rows, :] += dkb_part
            dv_ref[rows, :] += dvb_part

        for b in range(nb):
            q_b, kp_b, kc_b = _swa_views(b, qt_ref, kp_ref, kc_ref)
            do_b, vp_b, vc_b = _swa_views(b, dot_ref, vp_ref, vc_ref)
            dks, dvs = [], []
            for kh in range(Hk):
                qt, kb, pn, p_sink = _swa_probs(True if b else n > 0, kh, q_b, kp_b, kc_b, bias_ref, sink_ref)
                vb = jnp.concatenate([vp_b[:, kh * Dh:(kh + 1) * Dh], vc_b[:, kh * Dh:(kh + 1) * Dh]], axis=0)
                dot_h = jnp.concatenate([do_b[(kh * G + g) * Dh:(kh * G + g + 1) * Dh, :] for g in range(G)], axis=1)
                dp = _dot(vb, dot_h)
                delta = jnp.sum(pn * dp, axis=0, keepdims=True)
                dsb = (pn * (dp - delta)).astype(MXU_DTYPE)
                dsink_ref[kh] += -p_sink * delta
                dqt = _dot_tn(kb, dsb) * SWA_SCALE
                for g in range(G):
                    dqt_ref[pl.ds((kh * G + g) * Dh, Dh), pl.ds(b * W, W)] = dqt[:, g * W:(g + 1) * W]
                dks.append(_dot_nt(dsb, qt))
                dvs.append(_dot_nt(pn.astype(MXU_DTYPE), dot_h))
            dkb = jnp.concatenate(dks, axis=1)
            dvb = jnp.concatenate(dvs, axis=1)
            add_rows((n * nb + b) * W, dkb[W:], dvb[W:])
            if b:
                add_rows((n * nb + b - 1) * W, dkb[:W], dvb[:W])
            else:
                @pl.when(n > 0)
                def _():
                    add_rows((n * nb - 1) * W, dkb[:W], dvb[:W])

    prev = lambda n: (jnp.maximum(n * nb - 1, 0), 0)
    cur = lambda n: (n, 0)
    col = lambda n: (0, n)
    return pl.pallas_call(
        body, name="swa_attn_bwd", grid=(T // (nb * W),),
        in_specs=[pl.BlockSpec((D, nb * W), col), pl.BlockSpec((W, nk), prev), pl.BlockSpec((nb * W, nk), cur),
                  pl.BlockSpec((W, nk), prev), pl.BlockSpec((nb * W, nk), cur), pl.BlockSpec((D, nb * W), col),
                  _full(bias.shape), _full(sink_rows.shape)],
        out_specs=[pl.BlockSpec((D, nb * W), col), _full((T, nk)), _full((T, nk)), _full(sink_rows.shape)],
        out_shape=[jax.ShapeDtypeStruct((D, T), jnp.float32), jax.ShapeDtypeStruct((T, nk), jnp.float32),
                   jax.ShapeDtypeStruct((T, nk), jnp.float32), jax.ShapeDtypeStruct(sink_rows.shape, jnp.float32)],
        compiler_params=_params("arbitrary"),
    )(qt, k, k, v, v, dot_, bias, sink_rows)


def _swa_pre_bwd(x, dxm, vec, dq, dk, dv, w_qkv):
    T = x.shape[0]
    tm = _tile(T, 512)
    nq = SWA_HEADS * SWA_HEAD_DIM
    nk = SWA_KV_HEADS * SWA_HEAD_DIM
    nqkv = nq + 2 * nk

    def body(x_ref, dxm_ref, vec_ref, dq_ref, dk_ref, dv_ref, w_ref, dx_ref, dqkv_ref, ps_ref, db_ref):
        @pl.when(pl.program_id(0) == 0)
        def _():
            ps_ref[...] = jnp.zeros_like(ps_ref)
            db_ref[...] = jnp.zeros_like(db_ref)

        dqkv = jnp.concatenate([dq_ref[...], dk_ref[...], dv_ref[...]], axis=1)
        db_ref[...] += _rowsum(dqkv)
        dqkv_b = dqkv.astype(MXU_DTYPE)
        dqkv_ref[...] = dqkv_b
        dh = _dot_nt(dqkv_b, w_ref[...])
        dx_ref[...] = _modulate_bwd(dh, x_ref[...], vec_ref, R_GMIX, R_SC1, R_SH1, ps_ref, dxm_ref[...])

    return pl.pallas_call(
        body, name="swa_pre_bwd", grid=(T // tm,),
        in_specs=[_rows(tm, D), _rows(tm, D), _full((8, D)), _rows(tm, nq), _rows(tm, nk), _rows(tm, nk),
                  _full(w_qkv.shape)],
        out_specs=[_rows(tm, D), _rows(tm, nqkv), _full((8, D)), _full((1, nqkv))],
        out_shape=[jax.ShapeDtypeStruct((T, D), jnp.float32), jax.ShapeDtypeStruct((T, nqkv), MXU_DTYPE),
                   jax.ShapeDtypeStruct((8, D), jnp.float32), jax.ShapeDtypeStruct((1, nqkv), jnp.float32)],
        compiler_params=_params("arbitrary"),
    )(x, dxm, vec, dq, dk, dv, w_qkv)


def _rot_cols(w):
    half = QK_ROPE // 2
    return jnp.concatenate([-w[..., half:], w[..., :half]], axis=-1)


def _unrot_grad(d_rope, d_rot):
    half = QK_ROPE // 2
    return d_rope + jnp.concatenate([d_rot[..., half:], -d_rot[..., :half]], axis=-1)


def _rope_table(positions):
    half = QK_ROPE // 2
    inv_freq = ROPE_THETA ** (-jnp.arange(half, dtype=jnp.float32) / half)
    ang = positions.astype(jnp.float32)[:, None] * inv_freq
    cos, sin = jnp.cos(ang), jnp.sin(ang)
    return jnp.concatenate([cos, cos, sin, sin], axis=1)


def _sequence_step(x, tgt, positions, vecs, g_q, g_kv, sinks, g_final, wts, late_weights, on_late_grads, on_late_landed):
    H = MLA_HEADS
    cs = _rope_table(positions)
    w_dkv = wts["mla_w_dkv"]
    wcat = jnp.concatenate([wts["mla_w_dq"], w_dkv, _rot_cols(w_dkv[:, KV_LORA:])], axis=1)
    uq = wts["mla_w_uq"].reshape(Q_LORA, H, QK_NOPE + QK_ROPE)
    wuq = jnp.concatenate([uq, _rot_cols(uq[..., QK_NOPE:])], axis=-1).transpose(1, 0, 2)
    wukv = wts["mla_w_ukv"].reshape(KV_LORA, H, QK_NOPE + V_DIM).transpose(1, 0, 2)
    zero_bias = jnp.zeros((1, D), jnp.float32)
    bias = _swa_bias()
    sink_rows = jnp.broadcast_to(sinks.reshape(SWA_KV_HEADS, 1, SWA_GROUP, 1),
                                 (SWA_KV_HEADS, 1, SWA_GROUP, WINDOW)).reshape(SWA_KV_HEADS, 1, SWA_GROUP * WINDOW)

    h1a, z, q, k, v = _mla_pre(x, vecs[0], wcat, g_q, g_kv, wuq, wukv, cs)
    o_a, lse = _mla_attn_fwd(q, k, v)
    y1a, xm_a, h2a = _post_attn(o_a, x, wts["mla_w_o"], zero_bias, vecs[0])
    wts = {**wts, **late_weights(h2a)}
    a_a, r_a, y2a, x1 = _mlp_fwd(h2a, wts["w_ff1"][0], wts["w_ff2"][0], xm_a, vecs[0])

    h1b, qs_t, ks, vs = _swa_pre(x1, vecs[1], wts["swa_w_qkv"], wts["swa_b_qkv"])
    o_bt = _swa_attn_fwd(qs_t, ks, vs, bias, sink_rows)
    y1b, xm_b, h2b = _post_attn(o_bt, x1, wts["swa_w_o"], wts["swa_b_o"], vecs[1], o_transposed=True)
    a_b, r_b, y2b, x2 = _mlp_fwd(h2b, wts["w_ff1"][1], wts["w_ff2"][1], xm_b, vecs[1])

    loss8, dx2, dg_final = _final_loss(x2, tgt, g_final.reshape(1, D))

    du_b, dy2b, dxm_b, ps_mlp_b = _mlp_bwd(dx2, y2b, r_b, wts["w_ff1"][1], wts["w_ff2"][1], xm_b, vecs[1])
    g_ff2 = _mm_tn(a_b, dy2b, "dw_ff2_l1", "rows", DEPTH, 1)
    g_ff1 = _mm_tn(h2b, du_b, "dw_ff1_l1", "cols", DEPTH, 1)
    dy1b, do_bt, ps_out_b = _attn_out_bwd(dxm_b, y1b, None, wts["swa_w_o"], vecs[1], False)
    g_swa_o = _mm_tn(o_bt, dy1b, "dw_o_swa", a_transposed=True)
    dqs_t, dks, dvs, dsinks = _swa_attn_bwd(qs_t, ks, vs, do_bt, bias, sink_rows)
    dqs = dqs_t.T
    dx1, dqkv, ps_pre_b, g_swa_bqkv = _swa_pre_bwd(x1, dxm_b, vecs[1], dqs, dks, dvs, wts["swa_w_qkv"])
    g_swa_qkv = _mm_tn(h1b, dqkv, "dw_qkv", "cols")

    du_a, dy2a, dxm_a, ps_mlp_a = _mlp_bwd(dx1, y2a, r_a, wts["w_ff1"][0], wts["w_ff2"][0], xm_a, vecs[0])
    g_ff2 = _mm_tn(a_a, dy2a, "dw_ff2_l0", "rows", DEPTH, 0, g_ff2)
    g_ff1 = _mm_tn(h2a, du_a, "dw_ff1_l0", "cols", DEPTH, 0, g_ff1)
    rows4 = lambda g: g.reshape(N_CHIPS, g.shape[0] // N_CHIPS, g.shape[1])
    token = on_late_grads({
        "swa_w_qkv": g_swa_qkv.reshape(N_CHIPS, D, -1), "swa_w_o": rows4(g_swa_o),
        "w_ff1": g_ff1.reshape(N_CHIPS, DEPTH * D, -1), "w_ff2": g_ff2.reshape(N_CHIPS, -1, D)})
    dy1a, do_a, ps_out_a, delta = _attn_out_bwd(dxm_a, y1a, o_a, wts["mla_w_o"], vecs[0] + token[0, 0], True)
    g_mla_o = _mm_tn(o_a, dy1a, "dw_o_mla")
    token = on_late_landed(g_mla_o)
    dq, dk, dv = _mla_attn_bwd(q, k, v, do_a, lse, delta + token[0, 0])
    dx0, ps_pre_a, dg_q, dg_kv, dwcat, dwuq, dwukv = _mla_pre_bwd(
        x, dxm_a, vecs[0], h1a, z, dq, dk, dv, cs, wcat, g_q, g_kv, wuq, wukv)

    c0, c1, c2 = Q_LORA, Q_LORA + KV_LORA, Q_LORA + KV_LORA + QK_ROPE
    g_dq = dwcat[:, :c0]
    g_dkv = jnp.concatenate([dwcat[:, c0:c1], _unrot_grad(dwcat[:, c1:c2], dwcat[:, c2:])], axis=1)
    e0 = QK_NOPE + QK_ROPE
    g_uq = jnp.concatenate([dwuq[..., :QK_NOPE], _unrot_grad(dwuq[..., QK_NOPE:e0], dwuq[..., e0:])], axis=-1)
    per = H // N_CHIPS
    g_uq = g_uq.reshape(N_CHIPS, per, Q_LORA, e0).transpose(0, 2, 1, 3).reshape(N_CHIPS, Q_LORA, per * e0)
    g_ukv = dwukv.reshape(N_CHIPS, per, KV_LORA, QK_NOPE + V_DIM).transpose(0, 2, 1, 3)
    g_ukv = g_ukv.reshape(N_CHIPS, KV_LORA, per * (QK_NOPE + V_DIM))

    def dmod(ps_pre, ps_out, ps_mlp):
        return jnp.concatenate([ps_pre[R_SH1:R_SC1 + 1], ps_out[R_GT1:R_GT1 + 1], ps_mlp[R_SH2:R_GT2 + 1]], axis=0)

    grads = {"mla_w_dq": rows4(g_dq), "mla_w_uq": g_uq, "mla_w_dkv": rows4(g_dkv), "mla_w_ukv": g_ukv,
             "mla_w_o": rows4(g_mla_o)}
    small = {
        "dmod": jnp.stack([dmod(ps_pre_a, ps_out_a, ps_mlp_a), dmod(ps_pre_b, ps_out_b, ps_mlp_b)]).reshape(DEPTH, 6 * D),
        "g_mix": jnp.stack([ps_pre_a[R_GMIX], ps_pre_b[R_GMIX]]),
        "g_mlp": jnp.stack([ps_mlp_a[R_GMLP], ps_mlp_b[R_GMLP]]),
        "mla_g_q": dg_q, "mla_g_kv": dg_kv, "swa_sinks": jnp.sum(dsinks.reshape(SWA_HEADS, WINDOW), axis=1).reshape(1, SWA_HEADS),
        "swa_b_qkv": g_swa_bqkv, "swa_b_o": ps_out_b[R_BO:R_BO + 1],
        "g_final": dg_final.reshape(D), "loss": loss8[0, 0],
    }
    return dx0, grads, small


SHARDED = {
    "mla_w_dq": (1, D // N_CHIPS, Q_LORA),
    "mla_w_uq": (1, Q_LORA, MLA_HEADS * (QK_NOPE + QK_ROPE) // N_CHIPS),
    "mla_w_dkv": (1, D // N_CHIPS, KV_LORA + QK_ROPE),
    "mla_w_ukv": (1, KV_LORA, MLA_HEADS * (QK_NOPE + V_DIM) // N_CHIPS),
    "mla_w_o": (1, MLA_HEADS * V_DIM // N_CHIPS, D),
    "swa_w_qkv": (1, D, (SWA_HEADS + 2 * SWA_KV_HEADS) * SWA_HEAD_DIM // N_CHIPS),
    "swa_w_o": (1, SWA_HEADS * SWA_HEAD_DIM // N_CHIPS, D),
    "w_ff1": (DEPTH, D, D_FF // N_CHIPS),
    "w_ff2": (DEPTH, D_FF // N_CHIPS, D),
}
COL_SPLIT = ("mla_w_uq", "mla_w_ukv", "swa_w_qkv")
BIASES = {"swa_b_qkv": (SWA_HEADS + 2 * SWA_KV_HEADS) * SWA_HEAD_DIM, "swa_b_o": D}


def _view2d(name):
    shape = SHARDED[name]
    return math.prod(shape[:-1]), shape[-1]


SMALL = {"b_ada": (DEPTH, 6 * D), "g_mix": (DEPTH, D), "g_mlp": (DEPTH, D), "mla_g_q": (1, Q_LORA),
         "mla_g_kv": (1, KV_LORA), "swa_sinks": (1, SWA_HEADS), "g_final": (D,), "loss": (),
         "swa_b_qkv": (1, BIASES["swa_b_qkv"]), "swa_b_o": (1, BIASES["swa_b_o"])}
SMALL_ROWS = 168
DMA_ROWS = 256


def _small_slots():
    slots, off = {}, 0
    for name, shape in SMALL.items():
        n = max(math.prod(shape), 1)
        slots[name] = (off, n)
        off += -(-n // LANES) * LANES
    assert off <= SMALL_ROWS * LANES
    return slots


def _pack_small(vals):
    parts, end = [], 0
    for name, (off, n) in _small_slots().items():
        pad = -(-n // LANES) * LANES - n
        v = vals[name].astype(jnp.float32).reshape(-1) if name in vals else jnp.zeros((n,), jnp.float32)
        parts += [v, jnp.zeros((pad,), jnp.float32)]
        end = off + n + pad
    parts.append(jnp.zeros((SMALL_ROWS * LANES - end,), jnp.float32))
    return jnp.concatenate(parts).reshape(SMALL_ROWS, LANES)


def _unpack_small(buf):
    flat = buf.reshape(-1)
    return {name: flat[off:off + n].reshape(SMALL[name]) for name, (off, n) in _small_slots().items()}


def _pieces(rows):
    return [(off, min(DMA_ROWS, rows - off)) for off in range(0, rows, DMA_ROWS)]


HBM = pl.BlockSpec(memory_space=pltpu.HBM)
MESH = pl.DeviceIdType.MESH


def _place():
    x, y, c = lax.axis_index("x"), lax.axis_index("y"), lax.axis_index("c")
    chips = [(1 - x, y), (x, 1 - y), (1 - x, 1 - y)]
    return x, y, c, chips


def _all_gather(block):
    m_per, n = block.shape

    def body(x_ref, out_ref, send_sems, recv_sems, local_sem):
        x, y, c, chips = _place()
        me, sibling = (x, y, c), (x, y, 1 - c)

        def rows(px, py, pc):
            return out_ref.at[pl.ds((4 * px + 2 * py + pc) * m_per, m_per), :]

        def copy(k, blk, to, src=None):
            return pltpu.make_async_remote_copy(
                src_ref=rows(*blk) if src is None else src, dst_ref=rows(*blk),
                send_sem=send_sems.at[k], recv_sem=recv_sems.at[k], device_id=to, device_id_type=MESH)

        mine = pltpu.make_async_copy(x_ref, rows(*me), local_sem)
        mine.start()
        first = [copy(0, me, sibling, src=x_ref)]
        first += [copy(1 + j, me, (*chip, c), src=x_ref) for j, chip in enumerate(chips)]
        for cp in first:
            cp.start()
        passed = [copy(4 + j, (*chip, c), sibling) for j, chip in enumerate(chips)]
        for j, chip in enumerate(chips):
            copy(1 + j, (*chip, c), me).wait_recv()
            passed[j].start()
        copy(0, sibling, me).wait_recv()
        for j, chip in enumerate(chips):
            copy(4 + j, (*chip, 1 - c), me).wait_recv()
        for cp in first + passed:
            cp.wait_send()
        mine.wait()

    out = pl.pallas_call(
        body, name="all_gather_small",
        out_shape=jax.ShapeDtypeStruct((N_DEV * m_per, n), block.dtype),
        in_specs=[pl.BlockSpec(memory_space=pltpu.VMEM)],
        out_specs=pl.BlockSpec(memory_space=pltpu.VMEM),
        scratch_shapes=[pltpu.SemaphoreType.DMA((7,)), pltpu.SemaphoreType.DMA((7,)), pltpu.SemaphoreType.DMA],
    )(block)
    return out.reshape(N_DEV, m_per, n)


def _weight_gather(shards):
    nt = len(shards)

    def body(*refs):
        w_refs, out_refs = refs[:nt], refs[nt:2 * nt]
        send_sems, recv_sems = refs[2 * nt:]
        x, y, c, chips = _place()
        sibling = (x, y, 1 - c)

        def slab(t, px, py, half):
            rh = shards[t].shape[0] // 2
            return out_refs[t].at[2 * px + py, pl.ds(half * rh, rh), :]

        def copy(t, k, src, dst, to):
            return pltpu.make_async_remote_copy(src_ref=src, dst_ref=dst, send_sem=send_sems.at[6 * t + k],
                                                recv_sem=recv_sems.at[6 * t + k], device_id=to, device_id_type=MESH)

        first = []
        for t in range(nt):
            rh = shards[t].shape[0] // 2
            first += [copy(t, j, w_refs[t].at[pl.ds(c * rh, rh), :], slab(t, x, y, c), (*chip, c))
                      for j, chip in enumerate(chips)]
        for cp in first:
            cp.start()
        passed = []
        for t in range(nt):
            for j, chip in enumerate(chips):
                copy(t, j, slab(t, *chip, c), slab(t, *chip, c), (*chip, c)).wait_recv()
                rh = shards[t].shape[0] // 2
                for off, n in _pieces(rh):
                    piece = out_refs[t].at[2 * chip[0] + chip[1], pl.ds(c * rh + off, n), :]
                    copy(t, 3 + j, piece, piece, sibling).start()
                passed.append(copy(t, 3 + j, slab(t, *chip, c), slab(t, *chip, c), sibling))
        for t in range(nt):
            for j, chip in enumerate(chips):
                copy(t, 3 + j, slab(t, *chip, 1 - c), slab(t, *chip, 1 - c), sibling).wait_recv()
        for cp in first + passed:
            cp.wait_send()

    return pl.pallas_call(
        body, name="weight_gather",
        out_shape=[jax.ShapeDtypeStruct((N_CHIPS,) + s.shape, s.dtype) for s in shards],
        in_specs=[HBM] * nt, out_specs=[HBM] * nt,
        scratch_shapes=[pltpu.SemaphoreType.DMA((6 * nt,)), pltpu.SemaphoreType.DMA((6 * nt,))],
    )(*shards)


SEM = pl.BlockSpec(memory_space=pltpu.SEMAPHORE)
ANY = pl.BlockSpec(memory_space=pl.ANY)
SPLIT_COPY = pltpu.SideEffectType.DATAFLOW_SIDE_EFFECTING


def _late_copies(w_refs, land_refs, send_sems, recv_sems):
    x, y, c, chips = _place()
    return [pltpu.make_async_remote_copy(
        src_ref=w_refs[t], dst_ref=land_refs[t].at[2 * x + y], send_sem=send_sems.at[3 * t + j],
        recv_sem=recv_sems.at[3 * t + j], device_id=(cx, cy, c), device_id_type=MESH)
        for t in range(len(w_refs)) for j, (cx, cy) in enumerate(chips)], chips


def _late_gather_start(shards, after):
    nt, na = len(shards), len(after)

    def body(*refs):
        w_refs, land_refs = refs[:nt], refs[nt:2 * nt]
        send_sems, recv_sems, token = refs[2 * nt + na], refs[2 * nt + na + 1], refs[-1]
        copies, _ = _late_copies(w_refs, land_refs, send_sems, recv_sems)
        for cp in copies:
            cp.start()
        token[...] = jnp.zeros_like(token)

    hbm = lambda a: pltpu.with_memory_space_constraint(a, pltpu.HBM)
    lands = [lax.empty((N_CHIPS,) + s.shape, s.dtype) for s in shards]
    outs = pl.pallas_call(
        body, name="late_gather_start",
        out_shape=(pltpu.SemaphoreType.DMA((3 * nt,)), pltpu.SemaphoreType.DMA((3 * nt,)),
                   *[pltpu.HBM(s.shape, s.dtype) for s in shards], *[pltpu.HBM(l.shape, l.dtype) for l in lands],
                   jax.ShapeDtypeStruct((8, LANES), jnp.float32)),
        in_specs=[HBM] * (2 * nt) + [ANY] * na,
        out_specs=(SEM, SEM, *([HBM] * (2 * nt)), pl.BlockSpec(memory_space=pltpu.VMEM)),
        input_output_aliases={i: 2 + i for i in range(2 * nt)},
        compiler_params=pltpu.CompilerParams(has_side_effects=SPLIT_COPY),
    )(*[hbm(s) for s in shards], *[hbm(l) for l in lands], *after)
    return outs[0], outs[1], list(outs[2:2 + nt]), list(outs[2 + nt:2 + 2 * nt]), outs[-1]


def _late_gather_wait(send_sems, recv_sems, shards, lands, after):
    nt = len(shards)

    def body(*refs):
        w_refs, land_refs = refs[:nt], refs[nt:2 * nt]
        s_sems, r_sems = refs[2 * nt], refs[2 * nt + 1]
        x, y, c, chips = _place()
        for t in range(nt):
            for j, (cx, cy) in enumerate(chips):
                cp = pltpu.make_async_remote_copy(
                    src_ref=w_refs[t], dst_ref=land_refs[t].at[2 * cx + cy], send_sem=s_sems.at[3 * t + j],
                    recv_sem=r_sems.at[3 * t + j], device_id=(cx, cy, c), device_id_type=MESH)
                cp.wait_send()
                cp.wait_recv()

    outs = pl.pallas_call(
        body, name="late_gather_wait",
        out_shape=(*[pltpu.HBM(s.shape, s.dtype) for s in shards], *[pltpu.HBM(l.shape, l.dtype) for l in lands]),
        in_specs=[HBM] * (2 * nt) + [SEM, SEM, ANY], out_specs=tuple([HBM] * (2 * nt)),
        input_output_aliases={i: i for i in range(2 * nt)},
        compiler_params=pltpu.CompilerParams(has_side_effects=SPLIT_COPY),
    )(*shards, *lands, send_sems, recv_sems, after)
    return list(outs[nt:])


def _grad_pair_in(grads):
    nt = len(grads)

    def body(*refs):
        g_refs, got_refs = refs[:nt], refs[nt:2 * nt]
        send_sems, recv_sems = refs[2 * nt:]
        x, y, c, _ = _place()
        sibling = (x, y, 1 - c)

        def copy(t, src, dst):
            return pltpu.make_async_remote_copy(src_ref=src, dst_ref=dst, send_sem=send_sems.at[t],
                                                recv_sem=recv_sems.at[t], device_id=sibling, device_id_type=MESH)

        for t in range(nt):
            rh = grads[t].shape[1] // 2
            for p in range(N_CHIPS):
                for off, n in _pieces(rh):
                    copy(t, g_refs[t].at[p, pl.ds((1 - c) * rh + off, n), :], got_refs[t].at[p, pl.ds(off, n), :]).start()
        for t in range(nt):
            rh = grads[t].shape[1] // 2
            copy(t, g_refs[t].at[:, pl.ds((1 - c) * rh, rh), :], got_refs[t]).wait()

    return pl.pallas_call(
        body, name="grad_pair_in",
        out_shape=[jax.ShapeDtypeStruct((N_CHIPS, g.shape[1] // 2, g.shape[2]), g.dtype) for g in grads],
        in_specs=[HBM] * nt, out_specs=[HBM] * nt,
        scratch_shapes=[pltpu.SemaphoreType.DMA((nt,)), pltpu.SemaphoreType.DMA((nt,))],
    )(*grads)


def _pair_in_start(grads):
    nt = len(grads)

    def body(*refs):
        g_refs, land_refs = refs[:nt], refs[nt:2 * nt]
        send_sems, recv_sems, token = refs[2 * nt], refs[2 * nt + 1], refs[-1]
        x, y, c, _ = _place()
        for t in range(nt):
            rh = grads[t].shape[1] // 2
            for p in range(N_CHIPS):
                for off, n in _pieces(rh):
                    pltpu.make_async_remote_copy(
                        src_ref=g_refs[t].at[p, pl.ds((1 - c) * rh + off, n), :], dst_ref=land_refs[t].at[p, pl.ds(off, n), :],
                        send_sem=send_sems.at[t], recv_sem=recv_sems.at[t], device_id=(x, y, 1 - c),
                        device_id_type=MESH).start()
        token[...] = jnp.zeros_like(token)

    hbm = lambda a: pltpu.with_memory_space_constraint(a, pltpu.HBM)
    lands = [lax.empty((N_CHIPS, g.shape[1] // 2, g.shape[2]), g.dtype) for g in grads]
    outs = pl.pallas_call(
        body, name="grad_pair_in_start",
        out_shape=(pltpu.SemaphoreType.DMA((nt,)), pltpu.SemaphoreType.DMA((nt,)),
                   *[pltpu.HBM(g.shape, g.dtype) for g in grads], *[pltpu.HBM(l.shape, l.dtype) for l in lands],
                   jax.ShapeDtypeStruct((8, LANES), jnp.float32)),
        in_specs=[HBM] * (2 * nt),
        out_specs=(SEM, SEM, *([HBM] * (2 * nt)), pl.BlockSpec(memory_space=pltpu.VMEM)),
        input_output_aliases={i: 2 + i for i in range(2 * nt)},
        compiler_params=pltpu.CompilerParams(has_side_effects=SPLIT_COPY),
    )(*[hbm(g) for g in grads], *[hbm(l) for l in lands])
    return outs[0], outs[1], list(outs[2:2 + nt]), list(outs[2 + nt:2 + 2 * nt]), outs[-1]


def _pair_in_wait(send_sems, recv_sems, grads, lands, after):
    nt = len(grads)

    def body(*refs):
        g_refs, land_refs = refs[:nt], refs[nt:2 * nt]
        s_sems, r_sems = refs[2 * nt], refs[2 * nt + 1]
        x, y, c, _ = _place()
        for t in range(nt):
            rh = grads[t].shape[1] // 2
            cp = pltpu.make_async_remote_copy(
                src_ref=g_refs[t].at[:, pl.ds((1 - c) * rh, rh), :], dst_ref=land_refs[t], send_sem=s_sems.at[t],
                recv_sem=r_sems.at[t], device_id=(x, y, 1 - c), device_id_type=MESH)
            cp.wait_send()
            cp.wait_recv()

    outs = pl.pallas_call(
        body, name="grad_pair_in_wait",
        out_shape=(*[pltpu.HBM(g.shape, g.dtype) for g in grads], *[pltpu.HBM(l.shape, l.dtype) for l in lands]),
        in_specs=[HBM] * (2 * nt) + [SEM, SEM, ANY], out_specs=tuple([HBM] * (2 * nt)),
        input_output_aliases={i: i for i in range(2 * nt)},
        compiler_params=pltpu.CompilerParams(has_side_effects=SPLIT_COPY),
    )(*grads, *lands, send_sems, recv_sems, after)
    return list(outs[:nt]), list(outs[nt:])


def _pair_sum(g, got, core, name):
    _, rows, cols = g.shape
    rh = rows // 2
    tr = _tile(rh, 512)
    nb = rh // tr

    def body(c_ref, g_ref, got_ref, s32_ref, s16_ref):
        s = g_ref[...] + got_ref[...]
        s32_ref[...] = s
        s16_ref[...] = s.astype(s16_ref.dtype)

    blk = pl.BlockSpec((None, tr, cols), lambda p, i, c_ref: (p, i, 0))
    return pl.pallas_call(
        body, name=name,
        grid_spec=pltpu.PrefetchScalarGridSpec(
            num_scalar_prefetch=1, grid=(N_CHIPS, nb),
            in_specs=[pl.BlockSpec((None, tr, cols), lambda p, i, c_ref: (p, c_ref[0] * nb + i, 0)), blk],
            out_specs=[blk, blk]),
        out_shape=[jax.ShapeDtypeStruct((N_CHIPS, rh, cols), jnp.float32),
                   jax.ShapeDtypeStruct((N_CHIPS, rh, cols), jnp.bfloat16)],
        compiler_params=_params("parallel", "parallel"),
    )(core, g, got)


def _grad_chip_exchange(parts):
    nt = len(parts)

    def body(*refs):
        a_refs, got_refs = refs[:nt], refs[nt:2 * nt]
        send_sems, recv_sems = refs[2 * nt:]
        x, y, c, chips = _place()
        sends = [pltpu.make_async_remote_copy(
            src_ref=a_refs[t].at[2 * cx + cy], dst_ref=got_refs[t].at[j], send_sem=send_sems.at[3 * t + j],
            recv_sem=recv_sems.at[3 * t + j], device_id=(cx, cy, c), device_id_type=MESH)
            for t in range(nt) for j, (cx, cy) in enumerate(chips)]
        for cp in sends:
            cp.start()
        for cp in sends:
            cp.wait_recv()
        for cp in sends:
            cp.wait_send()

    return pl.pallas_call(
        body, name="grad_chip_exchange",
        out_shape=[jax.ShapeDtypeStruct((N_CHIPS - 1,) + a.shape[1:], a.dtype) for a in parts],
        in_specs=[HBM] * nt, out_specs=[HBM] * nt,
        scratch_shapes=[pltpu.SemaphoreType.DMA((3 * nt,)), pltpu.SemaphoreType.DMA((3 * nt,))],
    )(*parts)


def _exchange_start(parts):
    nt = len(parts)

    def body(*refs):
        a_refs, land_refs = refs[:nt], refs[nt:2 * nt]
        send_sems, recv_sems, token = refs[2 * nt], refs[2 * nt + 1], refs[-1]
        x, y, c, chips = _place()
        for t in range(nt):
            for j, (cx, cy) in enumerate(chips):
                pltpu.make_async_remote_copy(
                    src_ref=a_refs[t].at[2 * cx + cy], dst_ref=land_refs[t].at[j], send_sem=send_sems.at[3 * t + j],
                    recv_sem=recv_sems.at[3 * t + j], device_id=(cx, cy, c), device_id_type=MESH).start()
        token[...] = jnp.zeros_like(token)

    hbm = lambda a: pltpu.with_memory_space_constraint(a, pltpu.HBM)
    lands = [lax.empty((N_CHIPS - 1,) + a.shape[1:], a.dtype) for a in parts]
    outs = pl.pallas_call(
        body, name="grad_exchange_start",
        out_shape=(pltpu.SemaphoreType.DMA((3 * nt,)), pltpu.SemaphoreType.DMA((3 * nt,)),
                   *[pltpu.HBM(a.shape, a.dtype) for a in parts], *[pltpu.HBM(l.shape, l.dtype) for l in lands],
                   jax.ShapeDtypeStruct((8, LANES), jnp.float32)),
        in_specs=[HBM] * (2 * nt),
        out_specs=(SEM, SEM, *([HBM] * (2 * nt)), pl.BlockSpec(memory_space=pltpu.VMEM)),
        input_output_aliases={i: 2 + i for i in range(2 * nt)},
        compiler_params=pltpu.CompilerParams(has_side_effects=SPLIT_COPY),
    )(*[hbm(a) for a in parts], *[hbm(l) for l in lands])
    return outs[0], outs[1], list(outs[2:2 + nt]), list(outs[2 + nt:2 + 2 * nt]), outs[-1]


def _exchange_wait(send_sems, recv_sems, parts, lands, after):
    nt = len(parts)

    def body(*refs):
        a_refs, land_refs = refs[:nt], refs[nt:2 * nt]
        s_sems, r_sems = refs[2 * nt], refs[2 * nt + 1]
        x, y, c, chips = _place()
        for t in range(nt):
            for j, (cx, cy) in enumerate(chips):
                cp = pltpu.make_async_remote_copy(
                    src_ref=a_refs[t].at[2 * cx + cy], dst_ref=land_refs[t].at[j], send_sem=s_sems.at[3 * t + j],
                    recv_sem=r_sems.at[3 * t + j], device_id=(cx, cy, c), device_id_type=MESH)
                cp.wait_send()
                cp.wait_recv()

    outs = pl.pallas_call(
        body, name="grad_exchange_wait",
        out_shape=(*[pltpu.HBM(a.shape, a.dtype) for a in parts], *[pltpu.HBM(l.shape, l.dtype) for l in lands]),
        in_specs=[HBM] * (2 * nt) + [SEM, SEM, ANY], out_specs=tuple([HBM] * (2 * nt)),
        input_output_aliases={i: i for i in range(2 * nt)},
        compiler_params=pltpu.CompilerParams(has_side_effects=SPLIT_COPY),
    )(*parts, *lands, send_sems, recv_sems, after)
    return list(outs[nt:])


def _chip_sum(s32, got, chip, name):
    _, rh, cols = s32.shape
    tr = _tile(rh, 512)

    def body(p_ref, s_ref, got_ref, o_ref):
        acc = s_ref[...]
        for j in range(N_CHIPS - 1):
            acc = acc + got_ref[j].astype(jnp.float32)
        o_ref[...] = acc

    return pl.pallas_call(
        body, name=name,
        grid_spec=pltpu.PrefetchScalarGridSpec(
            num_scalar_prefetch=1, grid=(rh // tr,),
            in_specs=[pl.BlockSpec((None, tr, cols), lambda i, p_ref: (p_ref[0], i, 0)),
                      pl.BlockSpec((N_CHIPS - 1, tr, cols), lambda i, p_ref: (0, i, 0))],
            out_specs=pl.BlockSpec((tr, cols), lambda i, p_ref: (i, 0))),
        out_shape=jax.ShapeDtypeStruct((rh, cols), jnp.float32),
        compiler_params=_params("parallel"),
    )(chip, s32, got)


def _grad_pair_out(halves):
    nt = len(halves)

    def body(*refs):
        h_refs, got_refs = refs[:nt], refs[nt:2 * nt]
        send_sems, recv_sems = refs[2 * nt:]
        x, y, c, _ = _place()
        sibling = (x, y, 1 - c)

        def copy(t, src, dst):
            return pltpu.make_async_remote_copy(src_ref=src, dst_ref=dst, send_sem=send_sems.at[t],
                                                recv_sem=recv_sems.at[t], device_id=sibling, device_id_type=MESH)

        for t in range(nt):
            for off, n in _pieces(halves[t].shape[0]):
                copy(t, h_refs[t].at[pl.ds(off, n), :], got_refs[t].at[pl.ds(off, n), :]).start()
        for t in range(nt):
            copy(t, h_refs[t], got_refs[t]).wait()

    return pl.pallas_call(
        body, name="grad_pair_out",
        out_shape=[jax.ShapeDtypeStruct(h.shape, h.dtype) for h in halves],
        in_specs=[HBM] * nt, out_specs=[HBM] * nt,
        scratch_shapes=[pltpu.SemaphoreType.DMA((nt,)), pltpu.SemaphoreType.DMA((nt,))],
    )(*halves)


def _ada_part(c_all, w_ada):
    L, _, ncol = w_ada.shape
    tn = _tile(ncol, 512)

    def body(c_ref, w_ref, cond_ref, part_ref):
        cv = c_ref[...]
        cond = cv * jax.nn.sigmoid(cv)
        cond_ref[...] = cond
        part_ref[0] = jnp.dot(cond, w_ref[0], precision=lax.Precision.HIGHEST, preferred_element_type=jnp.float32)

    return pl.pallas_call(
        body, name="ada_part", grid=(L, ncol // tn),
        in_specs=[_full((N_DEV, D)), pl.BlockSpec((1, D, tn), lambda l, j: (l, 0, j))],
        out_specs=[_full((N_DEV, D)), pl.BlockSpec((1, N_DEV, tn), lambda l, j: (l, 0, j))],
        out_shape=[jax.ShapeDtypeStruct((N_DEV, D), jnp.float32), jax.ShapeDtypeStruct((L, N_DEV, ncol), jnp.float32)],
        compiler_params=_params("arbitrary", "arbitrary"),
    )(c_all, w_ada)


def _adamw_math(w, g, m, v):
    m = ADAM_B1 * m + (1.0 - ADAM_B1) * g
    v = ADAM_B2 * v + (1.0 - ADAM_B2) * jnp.square(g)
    m_hat = m / (1.0 - ADAM_B1 ** ADAM_STEP)
    v_hat = v / (1.0 - ADAM_B2 ** ADAM_STEP)
    delta = -ADAM_LR * (m_hat / (jnp.sqrt(v_hat) + ADAM_EPS) + ADAM_WD * w)
    return delta, m, v


def _adamw(w, g, m, v, name):
    shape = w.shape
    cols = shape[-1]
    rows = math.prod(shape[:-1])
    tr = _tile(rows, 512)
    two_d = lambda t: t.reshape(rows, cols)

    def body(w_ref, g_ref, m_ref, v_ref, d_ref, mo_ref, vo_ref):
        d_ref[...], mo_ref[...], vo_ref[...] = _adamw_math(w_ref[...], g_ref[...], m_ref[...], v_ref[...])

    out = jax.ShapeDtypeStruct((rows, cols), jnp.float32)
    outs = pl.pallas_call(
        body, name=name, grid=(rows // tr,), in_specs=[_rows(tr, cols)] * 4, out_specs=[_rows(tr, cols)] * 3,
        out_shape=[out, out, out], compiler_params=_params("parallel"),
    )(two_d(w), two_d(g), two_d(m), two_d(v))
    return [t.reshape(shape) for t in outs]


def _adamw_halves(w, mine, got, m, v, core, name):
    shape = w.shape
    cols = shape[-1]
    rows = math.prod(shape[:-1])
    rh = rows // 2
    tr = _tile(rh, 512)
    nbh = rh // tr
    two_d = lambda t: t.reshape(rows, cols)

    def body(c_ref, w_ref, a_ref, b_ref, m_ref, v_ref, g_ref, d_ref, mo_ref, vo_ref):
        g = jnp.where(pl.program_id(0) // nbh == c_ref[0], a_ref[...], b_ref[...])
        g_ref[...] = g
        d_ref[...], mo_ref[...], vo_ref[...] = _adamw_math(w_ref[...], g, m_ref[...], v_ref[...])

    row = pl.BlockSpec((tr, cols), lambda i, c_ref: (i, 0))

    def half(keep):
        return pl.BlockSpec((tr, cols), lambda i, c_ref: (jnp.where((i // nbh == c_ref[0]) == keep, i % nbh, 0), 0))

    out = jax.ShapeDtypeStruct((rows, cols), jnp.float32)
    outs = pl.pallas_call(
        body, name=name,
        grid_spec=pltpu.PrefetchScalarGridSpec(
            num_scalar_prefetch=1, grid=(rows // tr,),
            in_specs=[row, half(True), half(False), row, row], out_specs=[row] * 4),
        out_shape=[out] * 4, compiler_params=_params("arbitrary"),
    )(core, two_d(w), mine, got, two_d(m), two_d(v))
    return [t.reshape(shape) for t in outs]


def _ada_grad_adamw(cond_t, dm, w, m, v):
    L, _, ncol = w.shape
    tn = _tile(ncol, 512)

    def body(ct_ref, dm_ref, w_ref, m_ref, v_ref, g_ref, d_ref, mo_ref, vo_ref):
        g = ct_ref[:, 0:1] * dm_ref[0, 0:1, :]
        for b in range(1, N_DEV):
            g = g + ct_ref[:, b:b + 1] * dm_ref[0, b:b + 1, :]
        g_ref[0] = g
        d_ref[0], mo_ref[0], vo_ref[0] = _adamw_math(w_ref[0], g, m_ref[0], v_ref[0])

    wblk = pl.BlockSpec((1, D, tn), lambda l, j: (l, 0, j))
    out = jax.ShapeDtypeStruct(w.shape, jnp.float32)
    return pl.pallas_call(
        body, name="ada_grad_adamw", grid=(L, ncol // tn),
        in_specs=[_full((D, N_DEV)), pl.BlockSpec((1, N_DEV, tn), lambda l, j: (l, 0, j)), wblk, wblk, wblk],
        out_specs=[wblk] * 4, out_shape=[out] * 4, compiler_params=_params("parallel", "parallel"),
    )(cond_t, dm, w, m, v)


def _small_adamw(gathered, w, m, v):
    def body(ga_ref, w_ref, m_ref, v_ref, g_ref, d_ref, mo_ref, vo_ref):
        g = ga_ref[0]
        for dev in range(1, N_DEV):
            g = g + ga_ref[dev]
        g_ref[...] = g
        d_ref[...], mo_ref[...], vo_ref[...] = _adamw_math(w_ref[...], g, m_ref[...], v_ref[...])

    out = jax.ShapeDtypeStruct((SMALL_ROWS, LANES), jnp.float32)
    return pl.pallas_call(
        body, name="small_adamw", out_shape=[out] * 4,
        in_specs=[pl.BlockSpec(memory_space=pltpu.VMEM)] * 4, out_specs=[pl.BlockSpec(memory_space=pltpu.VMEM)] * 4,
    )(gathered, w, m, v)


def _one_hot_pick(arr, index, axis):
    n = arr.shape[axis]
    shape = [1] * arr.ndim
    shape[axis] = n
    hot = (jnp.arange(n) == index).astype(arr.dtype).reshape(shape)
    return jnp.sum(arr * hot, axis=axis)


def kernel(x, c, positions, w_ada, b_ada, g_mix, g_mlp, mla_w_dq, mla_g_q, mla_w_uq, mla_w_dkv, mla_g_kv, mla_w_ukv, mla_w_o, swa_w_qkv, swa_b_qkv, swa_sinks, swa_w_o, swa_b_o, w_ff1, w_ff2, g_final, loss_target, m_w_ada, m_b_ada, m_g_mix, m_g_mlp, m_mla_w_dq, m_mla_g_q, m_mla_w_uq, m_mla_w_dkv, m_mla_g_kv, m_mla_w_ukv, m_mla_w_o, m_swa_w_qkv, m_swa_b_qkv, m_swa_sinks, m_swa_w_o, m_swa_b_o, m_w_ff1, m_w_ff2, m_g_final, v_w_ada, v_b_ada, v_g_mix, v_g_mlp, v_mla_w_dq, v_mla_g_q, v_mla_w_uq, v_mla_w_dkv, v_mla_g_kv, v_mla_w_ukv, v_mla_w_o, v_swa_w_qkv, v_swa_b_qkv, v_swa_sinks, v_swa_w_o, v_swa_b_o, v_w_ff1, v_w_ff2, v_g_final):
    W = dict(w_ada=w_ada, b_ada=b_ada, g_mix=g_mix, g_mlp=g_mlp, mla_w_dq=mla_w_dq, mla_g_q=mla_g_q, mla_w_uq=mla_w_uq,
             mla_w_dkv=mla_w_dkv, mla_g_kv=mla_g_kv, mla_w_ukv=mla_w_ukv, mla_w_o=mla_w_o, swa_w_qkv=swa_w_qkv,
             swa_b_qkv=swa_b_qkv, swa_sinks=swa_sinks, swa_w_o=swa_w_o, swa_b_o=swa_b_o, w_ff1=w_ff1, w_ff2=w_ff2,
             g_final=g_final)
    M = dict(w_ada=m_w_ada, b_ada=m_b_ada, g_mix=m_g_mix, g_mlp=m_g_mlp, mla_w_dq=m_mla_w_dq, mla_g_q=m_mla_g_q,
             mla_w_uq=m_mla_w_uq, mla_w_dkv=m_mla_w_dkv, mla_g_kv=m_mla_g_kv, mla_w_ukv=m_mla_w_ukv, mla_w_o=m_mla_w_o,
             swa_w_qkv=m_swa_w_qkv, swa_b_qkv=m_swa_b_qkv, swa_sinks=m_swa_sinks, swa_w_o=m_swa_w_o, swa_b_o=m_swa_b_o,
             w_ff1=m_w_ff1, w_ff2=m_w_ff2, g_final=m_g_final)
    V = dict(w_ada=v_w_ada, b_ada=v_b_ada, g_mix=v_g_mix, g_mlp=v_g_mlp, mla_w_dq=v_mla_w_dq, mla_g_q=v_mla_g_q,
             mla_w_uq=v_mla_w_uq, mla_w_dkv=v_mla_w_dkv, mla_g_kv=v_mla_g_kv, mla_w_ukv=v_mla_w_ukv, mla_w_o=v_mla_w_o,
             swa_w_qkv=v_swa_w_qkv, swa_b_qkv=v_swa_b_qkv, swa_sinks=v_swa_sinks, swa_w_o=v_swa_w_o, swa_b_o=v_swa_b_o,
             w_ff1=v_w_ff1, w_ff2=v_w_ff2, g_final=v_g_final)
    order = list(W)
    names = list(SHARDED)
    core = lax.axis_index("c")
    chip = 2 * lax.axis_index("x") + lax.axis_index("y")
    dev = 2 * chip + core
    core_arr = core.astype(jnp.int32).reshape(1)
    chip_arr = chip.astype(jnp.int32).reshape(1)

    def whole(n, g, own):
        g = lax.dynamic_update_slice(g, own[None], (chip, 0, 0))
        if n in ("w_ff1", "w_ff2"):
            return g
        if n in COL_SPLIT:
            return g.transpose(1, 0, 2).reshape(g.shape[1], N_CHIPS * g.shape[2])
        return g.reshape(N_CHIPS * g.shape[1], g.shape[2])

    early = [n for n in names if n.startswith("mla_")]
    local = {n: W[n].astype(MXU_DTYPE).reshape(_view2d(n)) for n in early}
    wts = {n: whole(n, g, local[n]) for n, g in zip(early, _weight_gather([local[n] for n in early]))}

    nbq, nbo = BIASES["swa_b_qkv"] // N_CHIPS, BIASES["swa_b_o"] // N_CHIPS
    first = jnp.concatenate([c.reshape(-1), swa_b_qkv.reshape(-1), swa_b_o.reshape(-1),
                             jnp.zeros((16 * LANES - D - nbq - nbo,), jnp.float32)]).reshape(16, LANES)
    first_all = _all_gather(first).reshape(N_DEV, 16 * LANES)
    c_all = first_all[:, :D]
    south = first_all[0::2]
    wts["swa_b_qkv"] = south[:, D:D + nbq].reshape(1, N_CHIPS * nbq)
    wts["swa_b_o"] = south[:, D + nbq:D + nbq + nbo].reshape(1, N_CHIPS * nbo)
    cond_all, part = _ada_part(c_all, w_ada)
    ncol = w_ada.shape[2]
    part_all = _all_gather(part.reshape(-1, LANES)).reshape(N_DEV, DEPTH, N_DEV, ncol)
    mine = _one_hot_pick(part_all[0::2], dev, axis=2)
    mod = mine.transpose(1, 0, 2).reshape(DEPTH, N_CHIPS * ncol) + b_ada
    vecs = jnp.concatenate([mod.reshape(DEPTH, 6, D), g_mix[:, None, :], g_mlp[:, None, :]], axis=1)

    late = [("w_ff1", 0), ("w_ff2", 0), ("swa_w_qkv", None), ("swa_w_o", None), ("w_ff1", 1), ("w_ff2", 1)]
    late_local = [(W[n][0] if l is None else W[n][l]).astype(MXU_DTYPE) for n, l in late]
    send_sems, recv_sems, passed, lands, token = _late_gather_start(late_local, [vecs] + [wts[n] for n in early])

    def late_weights(after):
        got = _late_gather_wait(send_sems, recv_sems, passed, lands, after)
        out = {"w_ff1": [None] * DEPTH, "w_ff2": [None] * DEPTH}
        for (n, l), g, own in zip(late, got, late_local):
            if l is None:
                out[n] = whole(n, g, own)
            else:
                out[n][l] = whole(n, g, own)
        return out

    late_names = [n for n in names if n not in early]
    reduce_state = {}

    def on_late_grads(late_grads):
        s_sems, r_sems, passed_g, zones, tok = _pair_in_start([late_grads[n] for n in late_names])
        reduce_state.update(pair=(s_sems, r_sems, passed_g, zones))
        return tok

    def on_late_landed(after):
        gl, got = _pair_in_wait(*reduce_state["pair"], after)
        sums = [_pair_sum(g, s, core_arr, "pair_sum_" + n) for n, g, s in zip(late_names, gl, got)]
        s_sems, r_sems, parts, zones, tok = _exchange_start([s16 for _, s16 in sums])
        reduce_state.update(sums=sums, split=(s_sems, r_sems, parts, zones))
        return tok

    grad_x, grads, small = _sequence_step(
        x[0], loss_target[0], positions[0], vecs, mla_g_q + token[0, 0], mla_g_kv, swa_sinks, g_final, wts,
        late_weights, on_late_grads, on_late_landed)

    small["b_ada"] = small.pop("dmod")
    small_all = _all_gather(_pack_small(small))
    pk = lambda src: _pack_small({n: src[n] for n in SMALL if n != "loss" and n not in BIASES})
    g_small, d_small, m_small, v_small = [_unpack_small(t) for t in _small_adamw(small_all, pk(W), pk(M), pk(V))]
    off, n = _small_slots()["b_ada"]
    dmod_all = small_all.reshape(N_DEV, -1)[:, off:off + n].reshape(N_DEV, DEPTH, N_CHIPS, ncol)
    dm = _one_hot_pick(dmod_all, chip, axis=2).transpose(1, 0, 2)
    ada = _ada_grad_adamw(cond_all.T, dm, w_ada, m_w_ada, v_w_ada)

    gl = [grads[n] for n in early]
    got = _grad_pair_in(gl)
    sums = [_pair_sum(g, s, core_arr, "pair_sum_" + n) for n, g, s in zip(early, gl, got)]
    others = _grad_chip_exchange([s16 for _, s16 in sums])
    late_others = _exchange_wait(*reduce_state["split"], grad_x)
    sums, others = list(sums) + list(reduce_state["sums"]), list(others) + list(late_others)
    halves = [_chip_sum(s32, o, chip_arr, "chip_sum_" + n) for n, (s32, _), o in zip(names, sums, others)]
    sibling_halves = _grad_pair_out(halves)

    res = {"w_ada": ada}
    for n, mine_h, got_h in zip(names, halves, sibling_halves):
        res[n] = _adamw_halves(W[n], mine_h, got_h, M[n], V[n], core_arr, "adamw_" + n)
    for n, width in BIASES.items():
        g = _one_hot_pick(g_small[n].reshape(N_CHIPS, width // N_CHIPS), chip, axis=0).reshape(1, -1)
        res[n] = [g] + _adamw(W[n], g, M[n], V[n], "adamw_" + n)
    for name in order:
        if name not in res:
            res[name] = [t[name] for t in (g_small, d_small, m_small, v_small)]
    outs = [g_small["loss"], grad_x[None]]
    for k in range(4):
        outs += [res[name][k] for name in order]
    return tuple(outs)
```

```python
import functools
import math

import jax
import jax.numpy as jnp
import numpy as np
from jax import lax
from jax.experimental import pallas as pl
from jax.experimental.pallas import tpu as pltpu

D = 1024
DEPTH = 2
MLA_HEADS = 8
QK_NOPE = 128
QK_ROPE = 64
V_DIM = 128
Q_LORA = 384
KV_LORA = 256
ROPE_THETA = 10000.0
SWA_HEADS = 16
SWA_KV_HEADS = 4
SWA_HEAD_DIM = 64
SWA_GROUP = SWA_HEADS // SWA_KV_HEADS
WINDOW = 128
D_FF = 4 * D
EPS = 1e-6
ADAM_LR = 0.001
ADAM_B1 = 0.9
ADAM_B2 = 0.999
ADAM_EPS = 1e-08
ADAM_WD = 0.01
ADAM_STEP = 10

N_CHIPS = 4
N_DEV = 8
LANES = 128
QK_EXT = 256
MLA_SCALE = (QK_NOPE + QK_ROPE) ** -0.5
LOG2E = math.log2(math.e)
LN2 = math.log(2.0)
MLA_QSCALE = MLA_SCALE * LOG2E
ATTN_BLOCK = 1024
ATTN_SUB = 512
MLP_FWD_TILE = (1024, 512)
MLP_BWD_TILE = (512, 1024)
DW_TOKENS = 2048
SWA_SCALE = SWA_HEAD_DIM ** -0.5
NEG = -1e30
MXU_DTYPE = jnp.bfloat16
VMEM_LIMIT = 56 * 1024 * 1024

R_SH1, R_SC1, R_GT1, R_SH2, R_SC2, R_GT2, R_GMIX, R_GMLP = range(8)
R_BO = 6


def _tile(n, pref):
    if n <= pref:
        return n
    for t in range(pref, 7, -1):
        if n % t == 0 and t % 8 == 0:
            return t
    return n


def _dot(a, b):
    return jnp.dot(a, b, preferred_element_type=jnp.float32)


def _dot_nt(a, b):
    return lax.dot_general(a, b, (((1,), (1,)), ((), ())), preferred_element_type=jnp.float32)


def _dot_tn(a, b):
    return lax.dot_general(a, b, (((0,), (0,)), ((), ())), preferred_element_type=jnp.float32)


def _rms(x):
    r = lax.rsqrt(jnp.mean(x * x, axis=-1, keepdims=True) + EPS)
    return x * r, r


def _rms_bwd(dxhat, xhat, r):
    return r * (dxhat - xhat * jnp.mean(dxhat * xhat, axis=-1, keepdims=True))


def _rowsum(v):
    return jnp.sum(v, axis=0, keepdims=True)


def _params(*sem):
    return pltpu.CompilerParams(dimension_semantics=sem, vmem_limit_bytes=VMEM_LIMIT)


def _full(shape):
    nd = len(shape)
    return pl.BlockSpec(shape, lambda *_: (0,) * nd)


def _rows(tm, cols):
    return pl.BlockSpec((tm, cols), lambda i, *_: (i, 0))


def _modulate_bwd(dh, x, vec_ref, r_g, r_sc, r_sh, ps_ref, dres):
    xhat, r = _rms(x)
    g = vec_ref[r_g:r_g + 1, :]
    n = xhat * g
    ps_ref[r_sh:r_sh + 1, :] += _rowsum(dh)
    ps_ref[r_sc:r_sc + 1, :] += _rowsum(dh * n)
    dn = dh * (1.0 + vec_ref[r_sc:r_sc + 1, :])
    ps_ref[r_g:r_g + 1, :] += _rowsum(dn * xhat)
    return dres + _rms_bwd(dn * g, xhat, r)


def _mla_pre(x, vec, wcat, g_q, g_kv, wuq, wukv, cs):
    T = x.shape[0]
    tm = _tile(T, 512)
    H = MLA_HEADS

    def body(x_ref, vec_ref, wcat_ref, gq_ref, gkv_ref, wuq_ref, wukv_ref, cs_ref, h_ref, z_ref, q_ref, k_ref, v_ref):
        xhat, _ = _rms(x_ref[...])
        h = xhat * vec_ref[R_GMIX:R_GMIX + 1, :] * (1.0 + vec_ref[R_SC1:R_SC1 + 1, :]) + vec_ref[R_SH1:R_SH1 + 1, :]
        hb = h.astype(MXU_DTYPE)
        h_ref[...] = hb
        z = _dot(hb, wcat_ref[...])
        z_ref[...] = z
        cq = (_rms(z[:, :Q_LORA])[0] * gq_ref[...]).astype(MXU_DTYPE)
        ckv = (_rms(z[:, Q_LORA:Q_LORA + KV_LORA])[0] * gkv_ref[...]).astype(MXU_DTYPE)
        cs_t = cs_ref[...]
        t = z[:, Q_LORA + KV_LORA:] * cs_t
        k_rope = (t + pltpu.roll(t, QK_ROPE, axis=1)).astype(MXU_DTYPE)
        low = lax.broadcasted_iota(jnp.int32, (1, LANES), 1) < QK_ROPE
        for hd in range(H):
            qf = _dot(cq, wuq_ref[hd])
            tq = qf[:, QK_NOPE:] * cs_t
            tq = tq + pltpu.roll(tq, QK_ROPE, axis=1)
            q_ref[hd, :, :QK_NOPE] = (qf[:, :QK_NOPE] * MLA_QSCALE).astype(MXU_DTYPE)
            q_ref[hd, :, QK_NOPE:] = jnp.where(low, tq * MLA_QSCALE, 0.0).astype(MXU_DTYPE)
            kvf = _dot(ckv, wukv_ref[hd])
            k_ref[hd, :, :QK_NOPE] = kvf[:, :QK_NOPE].astype(MXU_DTYPE)
            k_ref[hd, :, QK_NOPE:] = k_rope
            v_ref[hd] = kvf[:, QK_NOPE:].astype(MXU_DTYPE)

    zc = wcat.shape[1]
    return pl.pallas_call(
        body, name="mla_pre", grid=(T // tm,),
        in_specs=[_rows(tm, D), _full((8, D)), _full(wcat.shape), _full(g_q.shape), _full(g_kv.shape),
                  _full(wuq.shape), _full(wukv.shape), _rows(tm, LANES)],
        out_specs=[_rows(tm, D), _rows(tm, zc),
                   pl.BlockSpec((H, tm, QK_EXT), lambda i: (0, i, 0)),
                   pl.BlockSpec((H, tm, QK_EXT), lambda i: (0, i, 0)),
                   pl.BlockSpec((H, tm, V_DIM), lambda i: (0, i, 0))],
        out_shape=[jax.ShapeDtypeStruct((T, D), MXU_DTYPE), jax.ShapeDtypeStruct((T, zc), jnp.float32),
                   jax.ShapeDtypeStruct((H, T, QK_EXT), MXU_DTYPE), jax.ShapeDtypeStruct((H, T, QK_EXT), MXU_DTYPE),
                   jax.ShapeDtypeStruct((H, T, V_DIM), MXU_DTYPE)],
        compiler_params=_params("parallel"),
    )(x, vec, wcat, g_q, g_kv, wuq, wukv, cs)


def _mla_attn_fwd(q, k, v):
    H, T, _ = q.shape
    tb = _tile(T, ATTN_BLOCK)
    sub = min(ATTN_SUB, tb)
    ns, nb = tb // sub, T // tb

    def body(q_ref, k_ref, v_ref, o_ref, lse_ref, m_sc, l_sc, acc_sc):
        qi, kj = pl.program_id(1), pl.program_id(2)

        @pl.when(kj == 0)
        def _():
            m_sc[...] = jnp.full_like(m_sc, NEG)
            l_sc[...] = jnp.zeros_like(l_sc)
            acc_sc[...] = jnp.zeros_like(acc_sc)

        def update(r, kk, masked):
            rows, keys = pl.ds(r * sub, sub), pl.ds(kk * sub, sub)
            s = _dot_nt(q_ref[0, rows, :], k_ref[0, keys, :])
            if masked:
                row = lax.broadcasted_iota(jnp.int32, (sub, sub), 0)
                col = lax.broadcasted_iota(jnp.int32, (sub, sub), 1)
                s = jnp.where(col <= row, s, NEG)
            m_prev = m_sc[rows, :]
            m_new = jnp.maximum(m_prev, jnp.max(s, axis=1, keepdims=True))
            alpha = jnp.exp2(m_prev - m_new)
            p = jnp.exp2(s - jnp.tile(m_new, (1, sub // LANES)))
            l_sc[rows, :] = alpha * l_sc[rows, :] + jnp.sum(p, axis=1, keepdims=True)
            acc_sc[rows, :] = alpha * acc_sc[rows, :] + _dot(p.astype(MXU_DTYPE), v_ref[0, keys, :])
            m_sc[rows, :] = m_new

        @pl.when(kj < qi)
        def _():
            for kk in range(ns):
                for r in range(ns):
                    update(r, kk, False)

        @pl.when(kj == qi)
        def _():
            for kk in range(ns):
                for r in range(kk, ns):
                    update(r, kk, r == kk)
            l = l_sc[...]
            o_ref[...] = (acc_sc[...] / l).astype(o_ref.dtype)
            lse = m_sc[...] + jnp.log2(l)
            pick = (lax.broadcasted_iota(jnp.int32, (8, LANES), 1) == 0).astype(jnp.float32)
            row = lax.dot_general(pick, lse, (((1,), (1,)), ((), ())), precision=lax.Precision.HIGHEST,
                                  preferred_element_type=jnp.float32)
            lse_ref[0] = row[0:1, :]

    kv_idx = lambda h, i, j: (h, jnp.minimum(i, j), 0)
    return pl.pallas_call(
        body, name="mla_attn_fwd", grid=(H, nb, nb),
        in_specs=[pl.BlockSpec((1, tb, QK_EXT), lambda h, i, j: (h, i, 0)),
                  pl.BlockSpec((1, tb, QK_EXT), kv_idx),
                  pl.BlockSpec((1, tb, V_DIM), kv_idx)],
        out_specs=[pl.BlockSpec((tb, V_DIM), lambda h, i, j: (i, h)),
                   pl.BlockSpec((1, 1, tb), lambda h, i, j: (h, 0, i))],
        out_shape=[jax.ShapeDtypeStruct((T, H * V_DIM), MXU_DTYPE), jax.ShapeDtypeStruct((H, 1, T), jnp.float32)],
        scratch_shapes=[pltpu.VMEM((tb, LANES), jnp.float32), pltpu.VMEM((tb, LANES), jnp.float32),
                        pltpu.VMEM((tb, V_DIM), jnp.float32)],
        compiler_params=_params("parallel", "parallel", "arbitrary"),
    )(q, k, v)


def _post_attn(o, x, w_o, bias, vec, o_transposed=False):
    T = x.shape[0]
    tm = _tile(T, 512)
    o_spec = pl.BlockSpec((D, tm), lambda i: (0, i)) if o_transposed else _rows(tm, D)

    def body(o_ref, x_ref, w_ref, b_ref, vec_ref, y_ref, xm_ref, h_ref):
        y = (_dot_tn if o_transposed else _dot)(o_ref[...], w_ref[...]) + b_ref[...]
        y_ref[...] = y.astype(y_ref.dtype)
        xm = x_ref[...] + vec_ref[R_GT1:R_GT1 + 1, :] * y
        xm_ref[...] = xm
        xhat, _ = _rms(xm)
        h = xhat * vec_ref[R_GMLP:R_GMLP + 1, :] * (1.0 + vec_ref[R_SC2:R_SC2 + 1, :]) + vec_ref[R_SH2:R_SH2 + 1, :]
        h_ref[...] = h.astype(h_ref.dtype)

    return pl.pallas_call(
        body, name="post_attn", grid=(T // tm,),
        in_specs=[o_spec, _rows(tm, D), _full((D, D)), _full((1, D)), _full((8, D))],
        out_specs=[_rows(tm, D), _rows(tm, D), _rows(tm, D)],
        out_shape=[jax.ShapeDtypeStruct((T, D), MXU_DTYPE), jax.ShapeDtypeStruct((T, D), jnp.float32),
                   jax.ShapeDtypeStruct((T, D), MXU_DTYPE)],
        compiler_params=_params("parallel"),
    )(o, x, w_o, bias, vec)


def _ff_specs(tf):
    per = D_FF // N_CHIPS // tf
    w1 = pl.BlockSpec((None, D, tf), lambda i, f: (f // per, 0, f % per))
    w2 = pl.BlockSpec((None, tf, D), lambda i, f: (f // per, f % per, 0))
    return w1, w2


def _mlp_fwd(h2, w1, w2, xm, vec):
    T = h2.shape[0]
    tm = _tile(T, MLP_FWD_TILE[0])
    tf = _tile(D_FF // N_CHIPS, MLP_FWD_TILE[1])
    nf = D_FF // tf
    w1_spec, w2_spec = _ff_specs(tf)

    def body(h_ref, w1_ref, w2_ref, xm_ref, vec_ref, a_ref, y_ref, xo_ref, acc):
        f = pl.program_id(1)

        @pl.when(f == 0)
        def _():
            acc[...] = jnp.zeros_like(acc)

        u = jnp.maximum(_dot(h_ref[...], w1_ref[...]), 0.0)
        ab = (u * u).astype(MXU_DTYPE)
        a_ref[...] = ab
        acc[...] += _dot(ab, w2_ref[...])

        @pl.when(f == nf - 1)
        def _():
            y = acc[...]
            y_ref[...] = y.astype(y_ref.dtype)
            xo_ref[...] = xm_ref[...] + vec_ref[R_GT2:R_GT2 + 1, :] * y

    return pl.pallas_call(
        body, name="mlp_fwd", grid=(T // tm, nf),
        in_specs=[_rows(tm, D), w1_spec, w2_spec, _rows(tm, D), _full((8, D))],
        out_specs=[pl.BlockSpec((tm, tf), lambda i, f: (i, f)), _rows(tm, D), _rows(tm, D)],
        out_shape=[jax.ShapeDtypeStruct((T, D_FF), MXU_DTYPE), jax.ShapeDtypeStruct((T, D), MXU_DTYPE),
                   jax.ShapeDtypeStruct((T, D), jnp.float32)],
        scratch_shapes=[pltpu.VMEM((tm, D), jnp.float32)],
        compiler_params=_params("parallel", "arbitrary"),
    )(h2, w1, w2, xm, vec)


def _swa_pre(x, vec, w_qkv, b_qkv):
    T = x.shape[0]
    tm = _tile(T, 512)
    nq = SWA_HEADS * SWA_HEAD_DIM
    nk = SWA_KV_HEADS * SWA_HEAD_DIM
    wq_t, w_kv = w_qkv[:, :nq].T, w_qkv[:, nq:]
    bq_col, b_kv = b_qkv[:, :nq].reshape(nq, 1), b_qkv[:, nq:]

    def body(x_ref, vec_ref, wq_ref, wkv_ref, bq_ref, bkv_ref, h_ref, qt_ref, k_ref, v_ref):
        xhat, _ = _rms(x_ref[...])
        h = xhat * vec_ref[R_GMIX:R_GMIX + 1, :] * (1.0 + vec_ref[R_SC1:R_SC1 + 1, :]) + vec_ref[R_SH1:R_SH1 + 1, :]
        hb = h.astype(MXU_DTYPE)
        h_ref[...] = hb
        qt_ref[...] = ((_dot_nt(wq_ref[...], hb) + bq_ref[...]) * SWA_SCALE).astype(MXU_DTYPE)
        kv = _dot(hb, wkv_ref[...]) + bkv_ref[...]
        k_ref[...] = kv[:, :nk].astype(MXU_DTYPE)
        v_ref[...] = kv[:, nk:].astype(MXU_DTYPE)

    return pl.pallas_call(
        body, name="swa_pre", grid=(T // tm,),
        in_specs=[_rows(tm, D), _full((8, D)), _full(wq_t.shape), _full(w_kv.shape), _full(bq_col.shape),
                  _full(b_kv.shape)],
        out_specs=[_rows(tm, D), pl.BlockSpec((nq, tm), lambda i: (0, i)), _rows(tm, nk), _rows(tm, nk)],
        out_shape=[jax.ShapeDtypeStruct((T, D), MXU_DTYPE), jax.ShapeDtypeStruct((nq, T), MXU_DTYPE),
                   jax.ShapeDtypeStruct((T, nk), MXU_DTYPE), jax.ShapeDtypeStruct((T, nk), MXU_DTYPE)],
        compiler_params=_params("parallel"),
    )(x, vec, wq_t, w_kv, bq_col, b_kv)


def _swa_bias():
    W = WINDOW
    slopes = 2.0 ** (-8.0 * np.arange(1, SWA_HEADS + 1) / SWA_HEADS)
    dist = W + np.arange(W)[None, :] - np.arange(2 * W)[:, None]
    inside = (dist >= 0) & (dist < W)
    bias = np.where(inside[None], -slopes[:, None, None] * dist[None].astype(np.float64), NEG)
    bias = bias.reshape(SWA_KV_HEADS, SWA_GROUP, 2 * W, W).transpose(0, 2, 1, 3)
    return jnp.asarray(bias.reshape(SWA_KV_HEADS, 2 * W, SWA_GROUP * W), jnp.float32)


SWA_STEP_BLOCKS = 4


def _swa_blocks(T):
    nb = T // WINDOW
    return next(b for b in (SWA_STEP_BLOCKS, 2, 1) if nb % b == 0)


def _swa_views(b, qt_ref, kp_ref, kc_ref):
    W = WINDOW
    prev = kp_ref if b == 0 else kc_ref.at[pl.ds((b - 1) * W, W), :]
    return qt_ref.at[:, pl.ds(b * W, W)], prev, kc_ref.at[pl.ds(b * W, W), :]


def _swa_probs(has_prev, kh, qt_ref, kp_ref, kc_ref, bias_ref, sink_ref):
    W, Dh, G = WINDOW, SWA_HEAD_DIM, SWA_GROUP
    qt = jnp.concatenate([qt_ref[(kh * G + g) * Dh:(kh * G + g + 1) * Dh, :] for g in range(G)], axis=1)
    kb = jnp.concatenate([kp_ref[:, kh * Dh:(kh + 1) * Dh], kc_ref[:, kh * Dh:(kh + 1) * Dh]], axis=0)
    s = _dot(kb, qt) + bias_ref[kh]
    if has_prev is not True:
        key = lax.broadcasted_iota(jnp.int32, (2 * W, 1), 0)
        s = jnp.where((key >= W) | has_prev, s, NEG)
    sink = sink_ref[kh]
    m = jnp.maximum(jnp.max(s, axis=0, keepdims=True), sink)
    p = jnp.exp(s - m)
    p_sink = jnp.exp(sink - m)
    inv = 1.0 / (jnp.sum(p, axis=0, keepdims=True) + p_sink)
    return qt, kb, p * inv, p_sink * inv


def _swa_attn_fwd(qt, k, v, bias, sink_rows):
    T = qt.shape[1]
    W, Dh, G, Hk = WINDOW, SWA_HEAD_DIM, SWA_GROUP, SWA_KV_HEADS
    nk = Hk * Dh

    nb = _swa_blocks(T)

    def body(qt_ref, kp_ref, kc_ref, vp_ref, vc_ref, bias_ref, sink_ref, ot_ref):
        n = pl.program_id(0)
        for b in range(nb):
            q_b, kp_b, kc_b = _swa_views(b, qt_ref, kp_ref, kc_ref)
            _, vp_b, vc_b = _swa_views(b, qt_ref, vp_ref, vc_ref)
            for kh in range(Hk):
                _, _, pn, _ = _swa_probs(True if b else n > 0, kh, q_b, kp_b, kc_b, bias_ref, sink_ref)
                vb = jnp.concatenate([vp_b[:, kh * Dh:(kh + 1) * Dh], vc_b[:, kh * Dh:(kh + 1) * Dh]], axis=0)
                ot = _dot_tn(vb, pn.astype(MXU_DTYPE))
                for g in range(G):
                    rows = pl.ds((kh * G + g) * Dh, Dh)
                    ot_ref[rows, pl.ds(b * W, W)] = ot[:, g * W:(g + 1) * W].astype(ot_ref.dtype)

    prev = lambda n: (jnp.maximum(n * nb - 1, 0), 0)
    cur = lambda n: (n, 0)
    col = lambda n: (0, n)
    return pl.pallas_call(
        body, name="swa_attn_fwd", grid=(T // (nb * W),),
        in_specs=[pl.BlockSpec((D, nb * W), col), pl.BlockSpec((W, nk), prev), pl.BlockSpec((nb * W, nk), cur),
                  pl.BlockSpec((W, nk), prev), pl.BlockSpec((nb * W, nk), cur), _full(bias.shape),
                  _full(sink_rows.shape)],
        out_specs=pl.BlockSpec((D, nb * W), col),
        out_shape=jax.ShapeDtypeStruct((D, T), MXU_DTYPE),
        compiler_params=_params("parallel"),
    )(qt, k, k, v, v, bias, sink_rows)


def _final_loss(x, tgt, g):
    T = x.shape[0]
    tm = _tile(T, 512)

    def body(x_ref, t_ref, g_ref, loss_ref, dx_ref, dg_ref):
        @pl.when(pl.program_id(0) == 0)
        def _():
            loss_ref[...] = jnp.zeros_like(loss_ref)
            dg_ref[...] = jnp.zeros_like(dg_ref)

        xhat, r = _rms(x_ref[...])
        gv = g_ref[...]
        e = xhat * gv - t_ref[...]
        loss_ref[...] += 0.5 * jnp.sum(jnp.mean(e * e, axis=-1, keepdims=True), axis=0, keepdims=True)
        dy = e * (1.0 / D)
        dg_ref[...] += _rowsum(dy * xhat)
        dx_ref[...] = _rms_bwd(dy * gv, xhat, r)

    return pl.pallas_call(
        body, name="final_loss", grid=(T // tm,),
        in_specs=[_rows(tm, D), _rows(tm, D), _full((1, D))],
        out_specs=[_full((8, LANES)), _rows(tm, D), _full((1, D))],
        out_shape=[jax.ShapeDtypeStruct((8, LANES), jnp.float32), jax.ShapeDtypeStruct((T, D), jnp.float32),
                   jax.ShapeDtypeStruct((1, D), jnp.float32)],
        compiler_params=_params("arbitrary"),
    )(x, tgt, g)


def _mlp_bwd(dxo, y2, a, w1, w2, xm, vec):
    T = dxo.shape[0]
    tm = _tile(T, MLP_BWD_TILE[0])
    tf = _tile(D_FF // N_CHIPS, MLP_BWD_TILE[1])
    nf = D_FF // tf
    w1_spec, w2_spec = _ff_specs(tf)

    def body(dxo_ref, y_ref, a_ref, w1_ref, w2_ref, xm_ref, vec_ref, du_ref, dy_ref, dxm_ref, ps_ref, dyb, acc):
        i, f = pl.program_id(0), pl.program_id(1)

        @pl.when((i == 0) & (f == 0))
        def _():
            ps_ref[...] = jnp.zeros_like(ps_ref)

        @pl.when(f == 0)
        def _():
            dxo_t = dxo_ref[...]
            d = (dxo_t * vec_ref[R_GT2:R_GT2 + 1, :]).astype(MXU_DTYPE)
            dyb[...] = d
            dy_ref[...] = d
            acc[...] = jnp.zeros_like(acc)
            ps_ref[R_GT2:R_GT2 + 1, :] += _rowsum(dxo_t * y_ref[...].astype(jnp.float32))

        da = _dot_nt(dyb[...], w2_ref[...])
        dub = (da * (2.0 * jnp.sqrt(a_ref[...].astype(jnp.float32)))).astype(MXU_DTYPE)
        du_ref[...] = dub
        acc[...] += _dot_nt(dub, w1_ref[...])

        @pl.when(f == nf - 1)
        def _():
            dxm_ref[...] = _modulate_bwd(acc[...], xm_ref[...], vec_ref, R_GMLP, R_SC2, R_SH2, ps_ref, dxo_ref[...])

    return pl.pallas_call(
        body, name="mlp_bwd", grid=(T // tm, nf),
        in_specs=[_rows(tm, D), _rows(tm, D), pl.BlockSpec((tm, tf), lambda i, f: (i, f)), w1_spec, w2_spec,
                  _rows(tm, D), _full((8, D))],
        out_specs=[pl.BlockSpec((tm, tf), lambda i, f: (i, f)), _rows(tm, D), _rows(tm, D), _full((8, D))],
        out_shape=[jax.ShapeDtypeStruct((T, D_FF), MXU_DTYPE), jax.ShapeDtypeStruct((T, D), MXU_DTYPE),
                   jax.ShapeDtypeStruct((T, D), jnp.float32), jax.ShapeDtypeStruct((8, D), jnp.float32)],
        scratch_shapes=[pltpu.VMEM((tm, D), MXU_DTYPE), pltpu.VMEM((tm, D), jnp.float32)],
        compiler_params=_params("arbitrary", "arbitrary"),
    )(dxo, y2, a, w1, w2, xm, vec)


def _mm_tn(a, g, name, split=None, layers=1, layer=0, into=None, a_transposed=False):
    K, T = a.shape if a_transposed else a.shape[::-1]
    N = g.shape[1]
    kq = K // N_CHIPS if split == "rows" else K
    nq = N // N_CHIPS if split == "cols" else N
    bk, bn, bt = _tile(kq, 1024), _tile(nq, 1024), _tile(T, DW_TOKENS)
    if nq % bn or bn % LANES:
        bn = nq
    kper, nper = kq // bk, nq // bn

    def body(*refs):
        a_ref, g_ref, o_ref = refs[0], refs[1], refs[-1]

        @pl.when(pl.program_id(2) == 0)
        def _():
            o_ref[...] = jnp.zeros_like(o_ref)

        o_ref[...] += (_dot if a_transposed else _dot_tn)(a_ref[...], g_ref[...])

    a_spec = pl.BlockSpec((bk, bt), lambda k, n, t: (k, t)) if a_transposed else pl.BlockSpec((bt, bk), lambda k, n, t: (t, k))
    in_specs = [a_spec, pl.BlockSpec((bt, bn), lambda k, n, t: (t, n))]
    args = [a, g]
    aliases = {}
    if split is None:
        out_spec = pl.BlockSpec((bk, bn), lambda k, n, t: (k, n))
        out_shape = jax.ShapeDtypeStruct((K, N), jnp.float32)
    else:
        if split == "cols":
            idx = lambda k, n, t: (n // nper, layer, k, n % nper)
        else:
            idx = lambda k, n, t: (k // kper, layer, k % kper, n)
        out_spec = pl.BlockSpec((None, None, bk, bn), idx)
        out_shape = jax.ShapeDtypeStruct((N_CHIPS, layers, kq, nq), jnp.float32)
        if into is not None:
            in_specs.append(pl.BlockSpec(memory_space=pl.ANY))
            args.append(into)
            aliases = {2: 0}
    return pl.pallas_call(
        body, name=name, grid=(K // bk, N // bn, T // bt), in_specs=in_specs, out_specs=out_spec, out_shape=out_shape,
        input_output_aliases=aliases, compiler_params=_params("parallel", "parallel", "arbitrary"),
    )(*args)


def _attn_out_bwd(dxm, y1, o, w_o, vec, with_delta):
    T = dxm.shape[0]
    tm = _tile(T, 512)
    H = MLA_HEADS

    def body(dxm_ref, y_ref, w_ref, vec_ref, *refs):
        o_ref = refs[0] if with_delta else None
        dy_ref, do_ref, ps_ref, *delta_ref = refs[1:] if with_delta else refs

        @pl.when(pl.program_id(0) == 0)
        def _():
            ps_ref[...] = jnp.zeros_like(ps_ref)

        dxm_t = dxm_ref[...]
        dy = dxm_t * vec_ref[R_GT1:R_GT1 + 1, :]
        ps_ref[R_GT1:R_GT1 + 1, :] += _rowsum(dxm_t * y_ref[...].astype(jnp.float32))
        ps_ref[R_BO:R_BO + 1, :] += _rowsum(dy)
        dyb = dy.astype(MXU_DTYPE)
        dy_ref[...] = dyb
        if not with_delta:
            do_ref[...] = _dot_nt(w_ref[...], dyb).astype(do_ref.dtype)
        else:
            do = _dot_nt(dyb, w_ref[...])
            do_ref[...] = do.astype(do_ref.dtype)
            of = o_ref[...].astype(jnp.float32)
            ones = jnp.ones((8, V_DIM), jnp.float32)
            for hd in range(H):
                sl = slice(hd * V_DIM, (hd + 1) * V_DIM)
                d = lax.dot_general(ones, do[:, sl] * of[:, sl], (((1,), (1,)), ((), ())),
                                    precision=lax.Precision.HIGHEST, preferred_element_type=jnp.float32)
                delta_ref[0][hd] = d[0:1, :]

    out_specs = [_rows(tm, D), _rows(tm, D), _full((8, D))]
    out_shape = [jax.ShapeDtypeStruct((T, D), MXU_DTYPE), jax.ShapeDtypeStruct((T, D), MXU_DTYPE),
                 jax.ShapeDtypeStruct((8, D), jnp.float32)]
    if not with_delta:
        out_specs[1] = pl.BlockSpec((D, tm), lambda i: (0, i))
        out_shape[1] = jax.ShapeDtypeStruct((D, T), MXU_DTYPE)
    if with_delta:
        out_specs.append(pl.BlockSpec((H, 1, tm), lambda i: (0, 0, i)))
        out_shape.append(jax.ShapeDtypeStruct((H, 1, T), jnp.float32))
    return pl.pallas_call(
        body, name="attn_out_bwd_mla" if with_delta else "attn_out_bwd_swa", grid=(T // tm,),
        in_specs=[_rows(tm, D), _rows(tm, D), _full((D, D)), _full((8, D))] + ([_rows(tm, D)] if with_delta else []),
        out_specs=out_specs, out_shape=out_shape,
        compiler_params=_params("arbitrary"),
    )(dxm, y1, w_o, vec, *([o] if with_delta else []))


def _mla_attn_bwd(q, k, v, do, lse, delta):
    H, T, _ = q.shape
    tb = _tile(T, ATTN_BLOCK)
    sub = min(ATTN_SUB, tb)
    ns, nb = tb // sub, T // tb

    def body(q_ref, k_ref, v_ref, do_ref, lse_ref, dl_ref, dq_ref, dk_ref, dv_ref, dk_acc, dv_acc):
        j, i = pl.program_id(1), pl.program_id(2)

        @pl.when((j == 0) & (i == 0))
        def _():
            dq_ref[...] = jnp.zeros_like(dq_ref)

        def update(kk, r, masked):
            keys, rows = pl.ds(kk * sub, sub), pl.ds(r * sub, sub)
            kb, qb, dob = k_ref[0, keys, :], q_ref[0, rows, :], do_ref[rows, :]
            st = _dot_nt(kb, qb)
            if masked:
                row = lax.broadcasted_iota(jnp.int32, (sub, sub), 0)
                col = lax.broadcasted_iota(jnp.int32, (sub, sub), 1)
                st = jnp.where(row <= col, st, NEG)
            pt = jnp.exp2(st - lse_ref[0, :, rows])
            dv_acc[keys, :] += _dot(pt.astype(MXU_DTYPE), dob)
            dpt = _dot_nt(v_ref[0, keys, :], dob)
            dst = (pt * (dpt - dl_ref[0, :, rows])).astype(MXU_DTYPE)
            dk_acc[keys, :] += _dot(dst, qb)
            q_rows = pl.ds(pl.multiple_of(i * tb + r * sub, sub), sub)
            dq_ref[0, q_rows, :] += _dot_tn(dst, kb)

        @pl.when(i == j)
        def _():
            dk_acc[...] = jnp.zeros_like(dk_acc)
            dv_acc[...] = jnp.zeros_like(dv_acc)
            for r in range(ns):
                for kk in range(r + 1):
                    update(kk, r, kk == r)

        @pl.when(i > j)
        def _():
            for r in range(ns):
                for kk in range(ns):
                    update(kk, r, False)

        @pl.when(i == nb - 1)
        def _():
            dk_ref[0] = (dk_acc[...] * LN2).astype(dk_ref.dtype)
            dv_ref[0] = dv_acc[...].astype(dv_ref.dtype)

    q_idx = lambda h, j, i: (h, jnp.maximum(i, j), 0)
    kv_idx = lambda h, j, i: (h, j, 0)
    stat_idx = lambda h, j, i: (h, 0, jnp.maximum(i, j))
    return pl.pallas_call(
        body, name="mla_attn_bwd", grid=(H, nb, nb),
        in_specs=[pl.BlockSpec((1, tb, QK_EXT), q_idx), pl.BlockSpec((1, tb, QK_EXT), kv_idx),
                  pl.BlockSpec((1, tb, V_DIM), kv_idx),
                  pl.BlockSpec((tb, V_DIM), lambda h, j, i: (jnp.maximum(i, j), h)),
                  pl.BlockSpec((1, 1, tb), stat_idx), pl.BlockSpec((1, 1, tb), stat_idx)],
        out_specs=[pl.BlockSpec((1, T, QK_EXT), lambda h, j, i: (h, 0, 0)),
                   pl.BlockSpec((1, tb, QK_EXT), kv_idx), pl.BlockSpec((1, tb, V_DIM), kv_idx)],
        out_shape=[jax.ShapeDtypeStruct((H, T, QK_EXT), jnp.float32), jax.ShapeDtypeStruct((H, T, QK_EXT), MXU_DTYPE),
                   jax.ShapeDtypeStruct((H, T, V_DIM), MXU_DTYPE)],
        scratch_shapes=[pltpu.VMEM((tb, QK_EXT), jnp.float32), pltpu.VMEM((tb, V_DIM), jnp.float32)],
        compiler_params=_params("parallel", "arbitrary", "arbitrary"),
    )(q, k, v, do, lse, delta)


def _mla_pre_bwd(x, dxm, vec, hb, z, dq, dk, dv, cs, wcat, g_q, g_kv, wuq, wukv):
    T = x.shape[0]
    tm = _tile(T, 256)
    H = MLA_HEADS
    zc = wcat.shape[1]

    def body(x_ref, dxm_ref, vec_ref, h_ref, z_ref, dq_ref, dk_ref, dv_ref, cs_ref, wcat_ref, gq_ref, gkv_ref,
             wuq_ref, wukv_ref, dx_ref, ps_ref, dgq_ref, dgkv_ref, dwcat_ref, dwuq_ref, dwukv_ref):
        @pl.when(pl.program_id(0) == 0)
        def _():
            for ref in (ps_ref, dgq_ref, dgkv_ref, dwcat_ref, dwuq_ref, dwukv_ref):
                ref[...] = jnp.zeros_like(ref)

        z = z_ref[...]
        cs_t = cs_ref[...]
        cqhat, rq = _rms(z[:, :Q_LORA])
        ckhat, rk = _rms(z[:, Q_LORA:Q_LORA + KV_LORA])
        gq, gkv = gq_ref[...], gkv_ref[...]
        cq = (cqhat * gq).astype(MXU_DTYPE)
        ckv = (ckhat * gkv).astype(MXU_DTYPE)
        dcq = jnp.zeros((tm, Q_LORA), jnp.float32)
        dckv = jnp.zeros((tm, KV_LORA), jnp.float32)
        dkr = jnp.zeros((tm, LANES), jnp.float32)
        for hd in range(H):
            dqh = dq_ref[hd] * MLA_SCALE
            gqh = jnp.concatenate([dqh[:, :QK_NOPE], dqh[:, QK_NOPE:] * cs_t], axis=1).astype(MXU_DTYPE)
            dcq += _dot_nt(gqh, wuq_ref[hd])
            dwuq_ref[hd] += _dot_tn(cq, gqh)
            dkh = dk_ref[hd]
            gkvh = jnp.concatenate([dkh[:, :QK_NOPE], dv_ref[hd]], axis=1)
            dckv += _dot_nt(gkvh, wukv_ref[hd])
            dwukv_ref[hd] += _dot_tn(ckv, gkvh)
            dkr += dkh[:, QK_NOPE:].astype(jnp.float32)
        dgq_ref[...] += _rowsum(dcq * cqhat)
        dgkv_ref[...] += _rowsum(dckv * ckhat)
        dcq_pre = _rms_bwd(dcq * gq, cqhat, rq)
        dckv_pre = _rms_bwd(dckv * gkv, ckhat, rk)
        dkr2 = (dkr + pltpu.roll(dkr, QK_ROPE, axis=1)) * cs_t
        dz = jnp.concatenate([dcq_pre, dckv_pre, dkr2], axis=1).astype(MXU_DTYPE)
        dwcat_ref[...] += _dot_tn(h_ref[...], dz)
        dh = _dot_nt(dz, wcat_ref[...])
        dx_ref[...] = _modulate_bwd(dh, x_ref[...], vec_ref, R_GMIX, R_SC1, R_SH1, ps_ref, dxm_ref[...])

    hblk = lambda w: pl.BlockSpec((H, tm, w), lambda i: (0, i, 0))
    return pl.pallas_call(
        body, name="mla_pre_bwd", grid=(T // tm,),
        in_specs=[_rows(tm, D), _rows(tm, D), _full((8, D)), _rows(tm, D), _rows(tm, zc), hblk(QK_EXT), hblk(QK_EXT),
                  hblk(V_DIM), _rows(tm, LANES), _full(wcat.shape), _full(g_q.shape), _full(g_kv.shape),
                  _full(wuq.shape), _full(wukv.shape)],
        out_specs=[_rows(tm, D), _full((8, D)), _full(g_q.shape), _full(g_kv.shape), _full(wcat.shape),
                   _full(wuq.shape), _full(wukv.shape)],
        out_shape=[jax.ShapeDtypeStruct((T, D), jnp.float32), jax.ShapeDtypeStruct((8, D), jnp.float32),
                   jax.ShapeDtypeStruct(g_q.shape, jnp.float32), jax.ShapeDtypeStruct(g_kv.shape, jnp.float32),
                   jax.ShapeDtypeStruct(wcat.shape, jnp.float32), jax.ShapeDtypeStruct(wuq.shape, jnp.float32),
                   jax.ShapeDtypeStruct(wukv.shape, jnp.float32)],
        compiler_params=_params("arbitrary"),
    )(x, dxm, vec, hb, z, dq, dk, dv, cs, wcat, g_q, g_kv, wuq, wukv)


def _swa_attn_bwd(qt, k, v, dot_, bias, sink_rows):
    T = qt.shape[1]
    W, Dh, G, Hk = WINDOW, SWA_HEAD_DIM, SWA_GROUP, SWA_KV_HEADS
    nk = Hk * Dh
    nb = _swa_blocks(T)

    def body(qt_ref, kp_ref, kc_ref, vp_ref, vc_ref, dot_ref, bias_ref, sink_ref, dqt_ref, dk_ref, dv_ref, dsink_ref):
        n = pl.program_id(0)

        @pl.when(n == 0)
        def _():
            dk_ref[...] = jnp.zeros_like(dk_ref)
            dv_ref[...] = jnp.zeros_like(dv_ref)
            dsink_ref[...] = jnp.zeros_like(dsink_ref)

        def add_rows(first_row, dkb_part, dvb_part):
            rows = pl.ds(pl.multiple_of(first_row, W), W)
            dk_ref[rows, :] += dkb_part
            dv_ref[rows, :] += dvb_part

        for b in range(nb):
            q_b, kp_b, kc_b = _swa_views(b, qt_ref, kp_ref, kc_ref)
            do_b, vp_b, vc_b = _swa_views(b, dot_ref, vp_ref, vc_ref)
            dks, dvs = [], []
            for kh in range(Hk):
                qt, kb, pn, p_sink = _swa_probs(True if b else n > 0, kh, q_b, kp_b, kc_b, bias_ref, sink_ref)
                vb = jnp.concatenate([vp_b[:, kh * Dh:(kh + 1) * Dh], vc_b[:, kh * Dh:(kh + 1) * Dh]], axis=0)
                dot_h = jnp.concatenate([do_b[(kh * G + g) * Dh:(kh * G + g + 1) * Dh, :] for g in range(G)], axis=1)
                dp = _dot(vb, dot_h)
                delta = jnp.sum(pn * dp, axis=0, keepdims=True)
                dsb = (pn * (dp - delta)).astype(MXU_DTYPE)
                dsink_ref[kh] += -p_sink * delta
                dqt = _dot_tn(kb, dsb) * SWA_SCALE
                for g in range(G):
                    dqt_ref[pl.ds((kh * G + g) * Dh, Dh), pl.ds(b * W, W)] = dqt[:, g * W:(g + 1) * W]
                dks.append(_dot_nt(dsb, qt))
                dvs.append(_dot_nt(pn.astype(MXU_DTYPE), dot_h))
            dkb = jnp.concatenate(dks, axis=1)
            dvb = jnp.concatenate(dvs, axis=1)
            add_rows((n * nb + b) * W, dkb[W:], dvb[W:])
            if b:
                add_rows((n * nb + b - 1) * W, dkb[:W], dvb[:W])
            else:
                @pl.when(n > 0)
                def _():
                    add_rows((n * nb - 1) * W, dkb[:W], dvb[:W])

    prev = lambda n: (jnp.maximum(n * nb - 1, 0), 0)
    cur = lambda n: (n, 0)
    col = lambda n: (0, n)
    return pl.pallas_call(
        body, name="swa_attn_bwd", grid=(T // (nb * W),),
        in_specs=[pl.BlockSpec((D, nb * W), col), pl.BlockSpec((W, nk), prev), pl.BlockSpec((nb * W, nk), cur),
                  pl.BlockSpec((W, nk), prev), pl.BlockSpec((nb * W, nk), cur), pl.BlockSpec((D, nb * W), col),
                  _full(bias.shape), _full(sink_rows.shape)],
        out_specs=[pl.BlockSpec((D, nb * W), col), _full((T, nk)), _full((T, nk)), _full(sink_rows.shape)],
        out_shape=[jax.ShapeDtypeStruct((D, T), jnp.float32), jax.ShapeDtypeStruct((T, nk), jnp.float32),
                   jax.ShapeDtypeStruct((T, nk), jnp.float32), jax.ShapeDtypeStruct(sink_rows.shape, jnp.float32)],
        compiler_params=_params("arbitrary"),
    )(qt, k, k, v, v, dot_, bias, sink_rows)


def _swa_pre_bwd(x, dxm, vec, dq, dk, dv, w_qkv):
    T = x.shape[0]
    tm = _tile(T, 512)
    nq = SWA_HEADS * SWA_HEAD_DIM
    nk = SWA_KV_HEADS * SWA_HEAD_DIM
    nqkv = nq + 2 * nk

    def body(x_ref, dxm_ref, vec_ref, dq_ref, dk_ref, dv_ref, w_ref, dx_ref, dqkv_ref, ps_ref, db_ref):
        @pl.when(pl.program_id(0) == 0)
        def _():
            ps_ref[...] = jnp.zeros_like(ps_ref)
            db_ref[...] = jnp.zeros_like(db_ref)

        dqkv = jnp.concatenate([dq_ref[...], dk_ref[...], dv_ref[...]], axis=1)
        db_ref[...] += _rowsum(dqkv)
        dqkv_b = dqkv.astype(MXU_DTYPE)
        dqkv_ref[...] = dqkv_b
        dh = _dot_nt(dqkv_b, w_ref[...])
        dx_ref[...] = _modulate_bwd(dh, x_ref[...], vec_ref, R_GMIX, R_SC1, R_SH1, ps_ref, dxm_ref[...])

    return pl.pallas_call(
        body, name="swa_pre_bwd", grid=(T // tm,),
        in_specs=[_rows(tm, D), _rows(tm, D), _full((8, D)), _rows(tm, nq), _rows(tm, nk), _rows(tm, nk),
                  _full(w_qkv.shape)],
        out_specs=[_rows(tm, D), _rows(tm, nqkv), _full((8, D)), _full((1, nqkv))],
        out_shape=[jax.ShapeDtypeStruct((T, D), jnp.float32), jax.ShapeDtypeStruct((T, nqkv), MXU_DTYPE),
                   jax.ShapeDtypeStruct((8, D), jnp.float32), jax.ShapeDtypeStruct((1, nqkv), jnp.float32)],
        compiler_params=_params("arbitrary"),
    )(x, dxm, vec, dq, dk, dv, w_qkv)


def _rot_cols(w):
    half = QK_ROPE // 2
    return jnp.concatenate([-w[..., half:], w[..., :half]], axis=-1)


def _unrot_grad(d_rope, d_rot):
    half = QK_ROPE // 2
    return d_rope + jnp.concatenate([d_rot[..., half:], -d_rot[..., :half]], axis=-1)


def _rope_table(positions):
    half = QK_ROPE // 2
    inv_freq = ROPE_THETA ** (-jnp.arange(half, dtype=jnp.float32) / half)
    ang = positions.astype(jnp.float32)[:, None] * inv_freq
    cos, sin = jnp.cos(ang), jnp.sin(ang)
    return jnp.concatenate([cos, cos, sin, sin], axis=1)


def _sequence_step(x, tgt, positions, vecs, g_q, g_kv, sinks, g_final, wts, late_weights, on_late_grads, on_late_landed):
    H = MLA_HEADS
    cs = _rope_table(positions)
    w_dkv = wts["mla_w_dkv"]
    wcat = jnp.concatenate([wts["mla_w_dq"], w_dkv, _rot_cols(w_dkv[:, KV_LORA:])], axis=1)
    uq = wts["mla_w_uq"].reshape(Q_LORA, H, QK_NOPE + QK_ROPE)
    wuq = jnp.concatenate([uq, _rot_cols(uq[..., QK_NOPE:])], axis=-1).transpose(1, 0, 2)
    wukv = wts["mla_w_ukv"].reshape(KV_LORA, H, QK_NOPE + V_DIM).transpose(1, 0, 2)
    zero_bias = jnp.zeros((1, D), jnp.float32)
    bias = _swa_bias()
    sink_rows = jnp.broadcast_to(sinks.reshape(SWA_KV_HEADS, 1, SWA_GROUP, 1),
                                 (SWA_KV_HEADS, 1, SWA_GROUP, WINDOW)).reshape(SWA_KV_HEADS, 1, SWA_GROUP * WINDOW)

    h1a, z, q, k, v = _mla_pre(x, vecs[0], wcat, g_q, g_kv, wuq, wukv, cs)
    o_a, lse = _mla_attn_fwd(q, k, v)
    y1a, xm_a, h2a = _post_attn(o_a, x, wts["mla_w_o"], zero_bias, vecs[0])
    wts = {**wts, **late_weights(h2a)}
    a_a, y2a, x1 = _mlp_fwd(h2a, wts["w_ff1"][0], wts["w_ff2"][0], xm_a, vecs[0])

    h1b, qs_t, ks, vs = _swa_pre(x1, vecs[1], wts["swa_w_qkv"], wts["swa_b_qkv"])
    o_bt = _swa_attn_fwd(qs_t, ks, vs, bias, sink_rows)
    y1b, xm_b, h2b = _post_attn(o_bt, x1, wts["swa_w_o"], wts["swa_b_o"], vecs[1], o_transposed=True)
    a_b, y2b, x2 = _mlp_fwd(h2b, wts["w_ff1"][1], wts["w_ff2"][1], xm_b, vecs[1])

    loss8, dx2, dg_final = _final_loss(x2, tgt, g_final.reshape(1, D))

    du_b, dy2b, dxm_b, ps_mlp_b = _mlp_bwd(dx2, y2b, a_b, wts["w_ff1"][1], wts["w_ff2"][1], xm_b, vecs[1])
    g_ff2 = _mm_tn(a_b, dy2b, "dw_ff2_l1", "rows", DEPTH, 1)
    g_ff1 = _mm_tn(h2b, du_b, "dw_ff1_l1", "cols", DEPTH, 1)
    dy1b, do_bt, ps_out_b = _attn_out_bwd(dxm_b, y1b, None, wts["swa_w_o"], vecs[1], False)
    g_swa_o = _mm_tn(o_bt, dy1b, "dw_o_swa", a_transposed=True)
    dqs_t, dks, dvs, dsinks = _swa_attn_bwd(qs_t, ks, vs, do_bt, bias, sink_rows)
    dqs = dqs_t.T
    dx1, dqkv, ps_pre_b, g_swa_bqkv = _swa_pre_bwd(x1, dxm_b, vecs[1], dqs, dks, dvs, wts["swa_w_qkv"])
    g_swa_qkv = _mm_tn(h1b, dqkv, "dw_qkv", "cols")

    du_a, dy2a, dxm_a, ps_mlp_a = _mlp_bwd(dx1, y2a, a_a, wts["w_ff1"][0], wts["w_ff2"][0], xm_a, vecs[0])
    g_ff2 = _mm_tn(a_a, dy2a, "dw_ff2_l0", "rows", DEPTH, 0, g_ff2)
    g_ff1 = _mm_tn(h2a, du_a, "dw_ff1_l0", "cols", DEPTH, 0, g_ff1)
    rows4 = lambda g: g.reshape(N_CHIPS, g.shape[0] // N_CHIPS, g.shape[1])
    token = on_late_grads({
        "swa_w_qkv": g_swa_qkv.reshape(N_CHIPS, D, -1), "swa_w_o": rows4(g_swa_o),
        "w_ff1": g_ff1.reshape(N_CHIPS, DEPTH * D, -1), "w_ff2": g_ff2.reshape(N_CHIPS, -1, D)})
    dy1a, do_a, ps_out_a, delta = _attn_out_bwd(dxm_a, y1a, o_a, wts["mla_w_o"], vecs[0] + token[0, 0], True)
    g_mla_o = _mm_tn(o_a, dy1a, "dw_o_mla")
    token = on_late_landed(g_mla_o)
    dq, dk, dv = _mla_attn_bwd(q, k, v, do_a, lse, delta + token[0, 0])
    dx0, ps_pre_a, dg_q, dg_kv, dwcat, dwuq, dwukv = _mla_pre_bwd(
        x, dxm_a, vecs[0], h1a, z, dq, dk, dv, cs, wcat, g_q, g_kv, wuq, wukv)

    c0, c1, c2 = Q_LORA, Q_LORA + KV_LORA, Q_LORA + KV_LORA + QK_ROPE
    g_dq = dwcat[:, :c0]
    g_dkv = jnp.concatenate([dwcat[:, c0:c1], _unrot_grad(dwcat[:, c1:c2], dwcat[:, c2:])], axis=1)
    e0 = QK_NOPE + QK_ROPE
    g_uq = jnp.concatenate([dwuq[..., :QK_NOPE], _unrot_grad(dwuq[..., QK_NOPE:e0], dwuq[..., e0:])], axis=-1)
    per = H // N_CHIPS
    g_uq = g_uq.reshape(N_CHIPS, per, Q_LORA, e0).transpose(0, 2, 1, 3).reshape(N_CHIPS, Q_LORA, per * e0)
    g_ukv = dwukv.reshape(N_CHIPS, per, KV_LORA, QK_NOPE + V_DIM).transpose(0, 2, 1, 3)
    g_ukv = g_ukv.reshape(N_CHIPS, KV_LORA, per * (QK_NOPE + V_DIM))

    def dmod(ps_pre, ps_out, ps_mlp):
        return jnp.concatenate([ps_pre[R_SH1:R_SC1 + 1], ps_out[R_GT1:R_GT1 + 1], ps_mlp[R_SH2:R_GT2 + 1]], axis=0)

    grads = {"mla_w_dq": rows4(g_dq), "mla_w_uq": g_uq, "mla_w_dkv": rows4(g_dkv), "mla_w_ukv": g_ukv,
             "mla_w_o": rows4(g_mla_o)}
    small = {
        "dmod": jnp.stack([dmod(ps_pre_a, ps_out_a, ps_mlp_a), dmod(ps_pre_b, ps_out_b, ps_mlp_b)]).reshape(DEPTH, 6 * D),
        "g_mix": jnp.stack([ps_pre_a[R_GMIX], ps_pre_b[R_GMIX]]),
        "g_mlp": jnp.stack([ps_mlp_a[R_GMLP], ps_mlp_b[R_GMLP]]),
        "mla_g_q": dg_q, "mla_g_kv": dg_kv, "swa_sinks": jnp.sum(dsinks.reshape(SWA_HEADS, WINDOW), axis=1).reshape(1, SWA_HEADS),
        "swa_b_qkv": g_swa_bqkv, "swa_b_o": ps_out_b[R_BO:R_BO + 1],
        "g_final": dg_final.reshape(D), "loss": loss8[0, 0],
    }
    return dx0, grads, small


SHARDED = {
    "mla_w_dq": (1, D // N_CHIPS, Q_LORA),
    "mla_w_uq": (1, Q_LORA, MLA_HEADS * (QK_NOPE + QK_ROPE) // N_CHIPS),
    "mla_w_dkv": (1, D // N_CHIPS, KV_LORA + QK_ROPE),
    "mla_w_ukv": (1, KV_LORA, MLA_HEADS * (QK_NOPE + V_DIM) // N_CHIPS),
    "mla_w_o": (1, MLA_HEADS * V_DIM // N_CHIPS, D),
    "swa_w_qkv": (1, D, (SWA_HEADS + 2 * SWA_KV_HEADS) * SWA_HEAD_DIM // N_CHIPS),
    "swa_w_o": (1, SWA_HEADS * SWA_HEAD_DIM // N_CHIPS, D),
    "w_ff1": (DEPTH, D, D_FF // N_CHIPS),
    "w_ff2": (DEPTH, D_FF // N_CHIPS, D),
}
COL_SPLIT = ("mla_w_uq", "mla_w_ukv", "swa_w_qkv")
BIASES = {"swa_b_qkv": (SWA_HEADS + 2 * SWA_KV_HEADS) * SWA_HEAD_DIM, "swa_b_o": D}


def _view2d(name):
    shape = SHARDED[name]
    return math.prod(shape[:-1]), shape[-1]


SMALL = {"b_ada": (DEPTH, 6 * D), "g_mix": (DEPTH, D), "g_mlp": (DEPTH, D), "mla_g_q": (1, Q_LORA),
         "mla_g_kv": (1, KV_LORA), "swa_sinks": (1, SWA_HEADS), "g_final": (D,), "loss": (),
         "swa_b_qkv": (1, BIASES["swa_b_qkv"]), "swa_b_o": (1, BIASES["swa_b_o"])}
SMALL_ROWS = 192
DMA_ROWS = 256


SLOT_ROWS = 8


def _small_slots():
    slots, off = {}, 0
    for name, shape in SMALL.items():
        n = max(math.prod(shape), 1)
        slots[name] = (off, n)
        off += -(-n // (SLOT_ROWS * LANES)) * SLOT_ROWS * LANES
    assert off <= SMALL_ROWS * LANES
    return slots


def _pack_small(vals):
    parts, end = [], 0
    for name, (off, n) in _small_slots().items():
        pad = -(-n // (SLOT_ROWS * LANES)) * SLOT_ROWS * LANES - n
        v = vals[name].astype(jnp.float32).reshape(-1) if name in vals else jnp.zeros((n,), jnp.float32)
        parts += [v, jnp.zeros((pad,), jnp.float32)]
        end = off + n + pad
    parts.append(jnp.zeros((SMALL_ROWS * LANES - end,), jnp.float32))
    return jnp.concatenate(parts).reshape(SMALL_ROWS, LANES)


def _from_slot(name, rows):
    n = max(math.prod(SMALL[name]), 1)
    return rows.reshape(-1)[:n].reshape(SMALL[name])


def _pieces(rows):
    return [(off, min(DMA_ROWS, rows - off)) for off in range(0, rows, DMA_ROWS)]


HBM = pl.BlockSpec(memory_space=pltpu.HBM)
MESH = pl.DeviceIdType.MESH


def _place():
    x, y, c = lax.axis_index("x"), lax.axis_index("y"), lax.axis_index("c")
    chips = [(1 - x, y), (x, 1 - y), (1 - x, 1 - y)]
    return x, y, c, chips


def _all_gather(block):
    m_per, n = block.shape

    def body(x_ref, out_ref, send_sems, recv_sems, local_sem):
        x, y, c, chips = _place()
        me, sibling = (x, y, c), (x, y, 1 - c)

        def rows(px, py, pc):
            return out_ref.at[pl.ds((4 * px + 2 * py + pc) * m_per, m_per), :]

        def copy(k, blk, to, src=None):
            return pltpu.make_async_remote_copy(
                src_ref=rows(*blk) if src is None else src, dst_ref=rows(*blk),
                send_sem=send_sems.at[k], recv_sem=recv_sems.at[k], device_id=to, device_id_type=MESH)

        mine = pltpu.make_async_copy(x_ref, rows(*me), local_sem)
        mine.start()
        first = [copy(0, me, sibling, src=x_ref)]
        first += [copy(1 + j, me, (*chip, c), src=x_ref) for j, chip in enumerate(chips)]
        for cp in first:
            cp.start()
        passed = [copy(4 + j, (*chip, c), sibling) for j, chip in enumerate(chips)]
        for j, chip in enumerate(chips):
            copy(1 + j, (*chip, c), me).wait_recv()
            passed[j].start()
        copy(0, sibling, me).wait_recv()
        for j, chip in enumerate(chips):
            copy(4 + j, (*chip, 1 - c), me).wait_recv()
        for cp in first + passed:
            cp.wait_send()
        mine.wait()

    out = pl.pallas_call(
        body, name="all_gather_small",
        out_shape=jax.ShapeDtypeStruct((N_DEV * m_per, n), block.dtype),
        in_specs=[pl.BlockSpec(memory_space=pltpu.VMEM)],
        out_specs=pl.BlockSpec(memory_space=pltpu.VMEM),
        scratch_shapes=[pltpu.SemaphoreType.DMA((7,)), pltpu.SemaphoreType.DMA((7,)), pltpu.SemaphoreType.DMA],
    )(block)
    return out.reshape(N_DEV, m_per, n)


def _weight_gather(shards):
    nt = len(shards)

    def body(*refs):
        w_refs, out_refs = refs[:nt], refs[nt:2 * nt]
        send_sems, recv_sems = refs[2 * nt:]
        x, y, c, chips = _place()
        sibling = (x, y, 1 - c)

        def slab(t, px, py, half):
            rh = shards[t].shape[0] // 2
            return out_refs[t].at[2 * px + py, pl.ds(half * rh, rh), :]

        def copy(t, k, src, dst, to):
            return pltpu.make_async_remote_copy(src_ref=src, dst_ref=dst, send_sem=send_sems.at[6 * t + k],
                                                recv_sem=recv_sems.at[6 * t + k], device_id=to, device_id_type=MESH)

        first = []
        for t in range(nt):
            rh = shards[t].shape[0] // 2
            first += [copy(t, j, w_refs[t].at[pl.ds(c * rh, rh), :], slab(t, x, y, c), (*chip, c))
                      for j, chip in enumerate(chips)]
        for cp in first:
            cp.start()
        passed = []
        for t in range(nt):
            for j, chip in enumerate(chips):
                copy(t, j, slab(t, *chip, c), slab(t, *chip, c), (*chip, c)).wait_recv()
                rh = shards[t].shape[0] // 2
                for off, n in _pieces(rh):
                    piece = out_refs[t].at[2 * chip[0] + chip[1], pl.ds(c * rh + off, n), :]
                    copy(t, 3 + j, piece, piece, sibling).start()
                passed.append(copy(t, 3 + j, slab(t, *chip, c), slab(t, *chip, c), sibling))
        for t in range(nt):
            for j, chip in enumerate(chips):
                copy(t, 3 + j, slab(t, *chip, 1 - c), slab(t, *chip, 1 - c), sibling).wait_recv()
        for cp in first + passed:
            cp.wait_send()

    return pl.pallas_call(
        body, name="weight_gather",
        out_shape=[jax.ShapeDtypeStruct((N_CHIPS,) + s.shape, s.dtype) for s in shards],
        in_specs=[HBM] * nt, out_specs=[HBM] * nt,
        scratch_shapes=[pltpu.SemaphoreType.DMA((6 * nt,)), pltpu.SemaphoreType.DMA((6 * nt,))],
    )(*shards)


SEM = pl.BlockSpec(memory_space=pltpu.SEMAPHORE)
ANY = pl.BlockSpec(memory_space=pl.ANY)
SPLIT_COPY = pltpu.SideEffectType.DATAFLOW_SIDE_EFFECTING


def _late_copies(w_refs, land_refs, send_sems, recv_sems):
    x, y, c, chips = _place()
    return [pltpu.make_async_remote_copy(
        src_ref=w_refs[t], dst_ref=land_refs[t].at[2 * x + y], send_sem=send_sems.at[3 * t + j],
        recv_sem=recv_sems.at[3 * t + j], device_id=(cx, cy, c), device_id_type=MESH)
        for t in range(len(w_refs)) for j, (cx, cy) in enumerate(chips)], chips


def _late_gather_start(shards, after):
    nt, na = len(shards), len(after)

    def body(*refs):
        w_refs, land_refs = refs[:nt], refs[nt:2 * nt]
        send_sems, recv_sems, token = refs[2 * nt + na], refs[2 * nt + na + 1], refs[-1]
        copies, _ = _late_copies(w_refs, land_refs, send_sems, recv_sems)
        for cp in copies:
            cp.start()
        token[...] = jnp.zeros_like(token)

    hbm = lambda a: pltpu.with_memory_space_constraint(a, pltpu.HBM)
    lands = [lax.empty((N_CHIPS,) + s.shape, s.dtype) for s in shards]
    outs = pl.pallas_call(
        body, name="late_gather_start",
        out_shape=(pltpu.SemaphoreType.DMA((3 * nt,)), pltpu.SemaphoreType.DMA((3 * nt,)),
                   *[pltpu.HBM(s.shape, s.dtype) for s in shards], *[pltpu.HBM(l.shape, l.dtype) for l in lands],
                   jax.ShapeDtypeStruct((8, LANES), jnp.float32)),
        in_specs=[HBM] * (2 * nt) + [ANY] * na,
        out_specs=(SEM, SEM, *([HBM] * (2 * nt)), pl.BlockSpec(memory_space=pltpu.VMEM)),
        input_output_aliases={i: 2 + i for i in range(2 * nt)},
        compiler_params=pltpu.CompilerParams(has_side_effects=SPLIT_COPY),
    )(*[hbm(s) for s in shards], *[hbm(l) for l in lands], *after)
    return outs[0], outs[1], list(outs[2:2 + nt]), list(outs[2 + nt:2 + 2 * nt]), outs[-1]


def _late_gather_wait(send_sems, recv_sems, shards, lands, after):
    nt = len(shards)

    def body(*refs):
        w_refs, land_refs = refs[:nt], refs[nt:2 * nt]
        s_sems, r_sems = refs[2 * nt], refs[2 * nt + 1]
        x, y, c, chips = _place()
        for t in range(nt):
            for j, (cx, cy) in enumerate(chips):
                cp = pltpu.make_async_remote_copy(
                    src_ref=w_refs[t], dst_ref=land_refs[t].at[2 * cx + cy], send_sem=s_sems.at[3 * t + j],
                    recv_sem=r_sems.at[3 * t + j], device_id=(cx, cy, c), device_id_type=MESH)
                cp.wait_send()
                cp.wait_recv()

    outs = pl.pallas_call(
        body, name="late_gather_wait",
        out_shape=(*[pltpu.HBM(s.shape, s.dtype) for s in shards], *[pltpu.HBM(l.shape, l.dtype) for l in lands]),
        in_specs=[HBM] * (2 * nt) + [SEM, SEM, ANY], out_specs=tuple([HBM] * (2 * nt)),
        input_output_aliases={i: i for i in range(2 * nt)},
        compiler_params=pltpu.CompilerParams(has_side_effects=SPLIT_COPY),
    )(*shards, *lands, send_sems, recv_sems, after)
    return list(outs[nt:])


def _grad_pair_in(grads):
    nt = len(grads)

    def body(*refs):
        g_refs, got_refs = refs[:nt], refs[nt:2 * nt]
        send_sems, recv_sems = refs[2 * nt:]
        x, y, c, _ = _place()
        sibling = (x, y, 1 - c)

        def copy(t, src, dst):
            return pltpu.make_async_remote_copy(src_ref=src, dst_ref=dst, send_sem=send_sems.at[t],
                                                recv_sem=recv_sems.at[t], device_id=sibling, device_id_type=MESH)

        for t in range(nt):
            rh = grads[t].shape[1] // 2
            for p in range(N_CHIPS):
                for off, n in _pieces(rh):
                    copy(t, g_refs[t].at[p, pl.ds((1 - c) * rh + off, n), :], got_refs[t].at[p, pl.ds(off, n), :]).start()
        for t in range(nt):
            rh = grads[t].shape[1] // 2
            copy(t, g_refs[t].at[:, pl.ds((1 - c) * rh, rh), :], got_refs[t]).wait()

    return pl.pallas_call(
        body, name="grad_pair_in",
        out_shape=[jax.ShapeDtypeStruct((N_CHIPS, g.shape[1] // 2, g.shape[2]), g.dtype) for g in grads],
        in_specs=[HBM] * nt, out_specs=[HBM] * nt,
        scratch_shapes=[pltpu.SemaphoreType.DMA((nt,)), pltpu.SemaphoreType.DMA((nt,))],
    )(*grads)


def _pair_in_start(grads):
    nt = len(grads)

    def body(*refs):
        g_refs, land_refs = refs[:nt], refs[nt:2 * nt]
        send_sems, recv_sems, token = refs[2 * nt], refs[2 * nt + 1], refs[-1]
        x, y, c, _ = _place()
        for t in range(nt):
            rh = grads[t].shape[1] // 2
            for p in range(N_CHIPS):
                for off, n in _pieces(rh):
                    pltpu.make_async_remote_copy(
                        src_ref=g_refs[t].at[p, pl.ds((1 - c) * rh + off, n), :], dst_ref=land_refs[t].at[p, pl.ds(off, n), :],
                        send_sem=send_sems.at[t], recv_sem=recv_sems.at[t], device_id=(x, y, 1 - c),
                        device_id_type=MESH).start()
        token[...] = jnp.zeros_like(token)

    hbm = lambda a: pltpu.with_memory_space_constraint(a, pltpu.HBM)
    lands = [lax.empty((N_CHIPS, g.shape[1] // 2, g.shape[2]), g.dtype) for g in grads]
    outs = pl.pallas_call(
        body, name="grad_pair_in_start",
        out_shape=(pltpu.SemaphoreType.DMA((nt,)), pltpu.SemaphoreType.DMA((nt,)),
                   *[pltpu.HBM(g.shape, g.dtype) for g in grads], *[pltpu.HBM(l.shape, l.dtype) for l in lands],
                   jax.ShapeDtypeStruct((8, LANES), jnp.float32)),
        in_specs=[HBM] * (2 * nt),
        out_specs=(SEM, SEM, *([HBM] * (2 * nt)), pl.BlockSpec(memory_space=pltpu.VMEM)),
        input_output_aliases={i: 2 + i for i in range(2 * nt)},
        compiler_params=pltpu.CompilerParams(has_side_effects=SPLIT_COPY),
    )(*[hbm(g) for g in grads], *[hbm(l) for l in lands])
    return outs[0], outs[1], list(outs[2:2 + nt]), list(outs[2 + nt:2 + 2 * nt]), outs[-1]


def _pair_in_wait(send_sems, recv_sems, grads, lands, after):
    nt = len(grads)

    def body(*refs):
        g_refs, land_refs = refs[:nt], refs[nt:2 * nt]
        s_sems, r_sems = refs[2 * nt], refs[2 * nt + 1]
        x, y, c, _ = _place()
        for t in range(nt):
            rh = grads[t].shape[1] // 2
            cp = pltpu.make_async_remote_copy(
                src_ref=g_refs[t].at[:, pl.ds((1 - c) * rh, rh), :], dst_ref=land_refs[t], send_sem=s_sems.at[t],
                recv_sem=r_sems.at[t], device_id=(x, y, 1 - c), device_id_type=MESH)
            cp.wait_send()
            cp.wait_recv()

    outs = pl.pallas_call(
        body, name="grad_pair_in_wait",
        out_shape=(*[pltpu.HBM(g.shape, g.dtype) for g in grads], *[pltpu.HBM(l.shape, l.dtype) for l in lands]),
        in_specs=[HBM] * (2 * nt) + [SEM, SEM, ANY], out_specs=tuple([HBM] * (2 * nt)),
        input_output_aliases={i: i for i in range(2 * nt)},
        compiler_params=pltpu.CompilerParams(has_side_effects=SPLIT_COPY),
    )(*grads, *lands, send_sems, recv_sems, after)
    return list(outs[:nt]), list(outs[nt:])


def _pair_sum(g, got, core, name):
    _, rows, cols = g.shape
    rh = rows // 2
    tr = _tile(rh, 512)
    nb = rh // tr

    def body(c_ref, g_ref, got_ref, s32_ref, s16_ref):
        s = g_ref[...] + got_ref[...]
        s32_ref[...] = s
        s16_ref[...] = s.astype(s16_ref.dtype)

    blk = pl.BlockSpec((None, tr, cols), lambda p, i, c_ref: (p, i, 0))
    return pl.pallas_call(
        body, name=name,
        grid_spec=pltpu.PrefetchScalarGridSpec(
            num_scalar_prefetch=1, grid=(N_CHIPS, nb),
            in_specs=[pl.BlockSpec((None, tr, cols), lambda p, i, c_ref: (p, c_ref[0] * nb + i, 0)), blk],
            out_specs=[blk, blk]),
        out_shape=[jax.ShapeDtypeStruct((N_CHIPS, rh, cols), jnp.float32),
                   jax.ShapeDtypeStruct((N_CHIPS, rh, cols), jnp.bfloat16)],
        compiler_params=_params("parallel", "parallel"),
    )(core, g, got)


def _exchange_start(parts, name):
    nt = len(parts)

    def body(*refs):
        a_refs, land_refs = refs[:nt], refs[nt:2 * nt]
        send_sems, recv_sems, token = refs[2 * nt], refs[2 * nt + 1], refs[-1]
        x, y, c, chips = _place()
        for t in range(nt):
            for j, (cx, cy) in enumerate(chips):
                pltpu.make_async_remote_copy(
                    src_ref=a_refs[t].at[2 * cx + cy], dst_ref=land_refs[t].at[j], send_sem=send_sems.at[3 * t + j],
                    recv_sem=recv_sems.at[3 * t + j], device_id=(cx, cy, c), device_id_type=MESH).start()
        token[...] = jnp.zeros_like(token)

    hbm = lambda a: pltpu.with_memory_space_constraint(a, pltpu.HBM)
    lands = [lax.empty((N_CHIPS - 1,) + a.shape[1:], a.dtype) for a in parts]
    outs = pl.pallas_call(
        body, name=name,
        out_shape=(pltpu.SemaphoreType.DMA((3 * nt,)), pltpu.SemaphoreType.DMA((3 * nt,)),
                   *[pltpu.HBM(a.shape, a.dtype) for a in parts], *[pltpu.HBM(l.shape, l.dtype) for l in lands],
                   jax.ShapeDtypeStruct((8, LANES), jnp.float32)),
        in_specs=[HBM] * (2 * nt),
        out_specs=(SEM, SEM, *([HBM] * (2 * nt)), pl.BlockSpec(memory_space=pltpu.VMEM)),
        input_output_aliases={i: 2 + i for i in range(2 * nt)},
        compiler_params=pltpu.CompilerParams(has_side_effects=SPLIT_COPY),
    )(*[hbm(a) for a in parts], *[hbm(l) for l in lands])
    return outs[0], outs[1], list(outs[2:2 + nt]), list(outs[2 + nt:2 + 2 * nt]), outs[-1]


def _exchange_wait(send_sems, recv_sems, parts, lands, after, name):
    nt = len(parts)

    def body(*refs):
        a_refs, land_refs = refs[:nt], refs[nt:2 * nt]
        s_sems, r_sems = refs[2 * nt], refs[2 * nt + 1]
        x, y, c, chips = _place()
        for t in range(nt):
            for j, (cx, cy) in enumerate(chips):
                cp = pltpu.make_async_remote_copy(
                    src_ref=a_refs[t].at[2 * cx + cy], dst_ref=land_refs[t].at[j], send_sem=s_sems.at[3 * t + j],
                    recv_sem=r_sems.at[3 * t + j], device_id=(cx, cy, c), device_id_type=MESH)
                cp.wait_send()
                cp.wait_recv()

    outs = pl.pallas_call(
        body, name=name,
        out_shape=(*[pltpu.HBM(a.shape, a.dtype) for a in parts], *[pltpu.HBM(l.shape, l.dtype) for l in lands]),
        in_specs=[HBM] * (2 * nt) + [SEM, SEM, ANY], out_specs=tuple([HBM] * (2 * nt)),
        input_output_aliases={i: i for i in range(2 * nt)},
        compiler_params=pltpu.CompilerParams(has_side_effects=SPLIT_COPY),
    )(*parts, *lands, send_sems, recv_sems, after)
    return list(outs[nt:])


def _chip_sum(s32, got, chip, name, behind=None):
    _, rh, cols = s32.shape
    tr = _tile(rh, 512)

    def body(p_ref, s_ref, got_ref, *refs):
        acc = s_ref[...]
        for j in range(N_CHIPS - 1):
            acc = acc + got_ref[j].astype(jnp.float32)
        refs[-1][...] = acc

    extra = [] if behind is None else [behind]
    return pl.pallas_call(
        body, name=name,
        grid_spec=pltpu.PrefetchScalarGridSpec(
            num_scalar_prefetch=1, grid=(rh // tr,),
            in_specs=[pl.BlockSpec((None, tr, cols), lambda i, p_ref: (p_ref[0], i, 0)),
                      pl.BlockSpec((N_CHIPS - 1, tr, cols), lambda i, p_ref: (0, i, 0))]
            + [pl.BlockSpec((8, LANES), lambda i, p_ref: (0, 0))] * len(extra),
            out_specs=pl.BlockSpec((tr, cols), lambda i, p_ref: (i, 0))),
        out_shape=jax.ShapeDtypeStruct((rh, cols), jnp.float32),
        compiler_params=_params("parallel"),
    )(chip, s32, got, *extra)


def _grad_pair_out(halves):
    nt = len(halves)

    def body(*refs):
        h_refs, got_refs = refs[:nt], refs[nt:2 * nt]
        send_sems, recv_sems = refs[2 * nt:]
        x, y, c, _ = _place()
        sibling = (x, y, 1 - c)

        def copy(t, src, dst):
            return pltpu.make_async_remote_copy(src_ref=src, dst_ref=dst, send_sem=send_sems.at[t],
                                                recv_sem=recv_sems.at[t], device_id=sibling, device_id_type=MESH)

        for t in range(nt):
            for off, n in _pieces(halves[t].shape[0]):
                copy(t, h_refs[t].at[pl.ds(off, n), :], got_refs[t].at[pl.ds(off, n), :]).start()
        for t in range(nt):
            copy(t, h_refs[t], got_refs[t]).wait()

    return pl.pallas_call(
        body, name="grad_pair_out",
        out_shape=[jax.ShapeDtypeStruct(h.shape, h.dtype) for h in halves],
        in_specs=[HBM] * nt, out_specs=[HBM] * nt,
        scratch_shapes=[pltpu.SemaphoreType.DMA((nt,)), pltpu.SemaphoreType.DMA((nt,))],
    )(*halves)


def _ada_part(c_all, w_ada):
    L, _, ncol = w_ada.shape
    tn = _tile(ncol, 512)

    def body(c_ref, w_ref, cond_ref, part_ref):
        cv = c_ref[...]
        cond = cv * jax.nn.sigmoid(cv)
        cond_ref[...] = cond
        part_ref[0] = jnp.dot(cond, w_ref[0], precision=lax.Precision.HIGHEST, preferred_element_type=jnp.float32)

    return pl.pallas_call(
        body, name="ada_part", grid=(L, ncol // tn),
        in_specs=[_full((N_DEV, D)), pl.BlockSpec((1, D, tn), lambda l, j: (l, 0, j))],
        out_specs=[_full((N_DEV, D)), pl.BlockSpec((1, N_DEV, tn), lambda l, j: (l, 0, j))],
        out_shape=[jax.ShapeDtypeStruct((N_DEV, D), jnp.float32), jax.ShapeDtypeStruct((L, N_DEV, ncol), jnp.float32)],
        compiler_params=_params("arbitrary", "arbitrary"),
    )(c_all, w_ada)


def _adamw_math(w, g, m, v):
    m = ADAM_B1 * m + (1.0 - ADAM_B1) * g
    v = ADAM_B2 * v + (1.0 - ADAM_B2) * jnp.square(g)
    m_hat = m / (1.0 - ADAM_B1 ** ADAM_STEP)
    v_hat = v / (1.0 - ADAM_B2 ** ADAM_STEP)
    delta = -ADAM_LR * (m_hat / (jnp.sqrt(v_hat) + ADAM_EPS) + ADAM_WD * w)
    return delta, m, v


def _adamw(w, g, m, v, name):
    shape = w.shape
    cols = shape[-1]
    rows = math.prod(shape[:-1])
    tr = _tile(rows, 512)
    two_d = lambda t: t.reshape(rows, cols)

    def body(w_ref, g_ref, m_ref, v_ref, d_ref, mo_ref, vo_ref):
        d_ref[...], mo_ref[...], vo_ref[...] = _adamw_math(w_ref[...], g_ref[...], m_ref[...], v_ref[...])

    out = jax.ShapeDtypeStruct((rows, cols), jnp.float32)
    outs = pl.pallas_call(
        body, name=name, grid=(rows // tr,), in_specs=[_rows(tr, cols)] * 4, out_specs=[_rows(tr, cols)] * 3,
        out_shape=[out, out, out], compiler_params=_params("parallel"),
    )(two_d(w), two_d(g), two_d(m), two_d(v))
    return [t.reshape(shape) for t in outs]


def _adamw_halves(w, mine, got, m, v, core, name):
    shape = w.shape
    cols = shape[-1]
    rows = math.prod(shape[:-1])
    rh = rows // 2
    tr = _tile(rh, 512)
    nbh = rh // tr
    two_d = lambda t: t.reshape(rows, cols)

    def body(c_ref, w_ref, a_ref, b_ref, m_ref, v_ref, g_ref, d_ref, mo_ref, vo_ref):
        g = jnp.where(pl.program_id(0) // nbh == c_ref[0], a_ref[...], b_ref[...])
        g_ref[...] = g
        d_ref[...], mo_ref[...], vo_ref[...] = _adamw_math(w_ref[...], g, m_ref[...], v_ref[...])

    row = pl.BlockSpec((tr, cols), lambda i, c_ref: (i, 0))

    def half(keep):
        return pl.BlockSpec((tr, cols), lambda i, c_ref: (jnp.where((i // nbh == c_ref[0]) == keep, i % nbh, 0), 0))

    out = jax.ShapeDtypeStruct((rows, cols), jnp.float32)
    outs = pl.pallas_call(
        body, name=name,
        grid_spec=pltpu.PrefetchScalarGridSpec(
            num_scalar_prefetch=1, grid=(rows // tr,),
            in_specs=[row, half(True), half(False), row, row], out_specs=[row] * 4),
        out_shape=[out] * 4, compiler_params=_params("arbitrary"),
    )(core, two_d(w), mine, got, two_d(m), two_d(v))
    return [t.reshape(shape) for t in outs]


def _ada_grad_adamw(cond_t, dm, w, m, v):
    L, _, ncol = w.shape
    tn = _tile(ncol, 512)

    def body(ct_ref, dm_ref, w_ref, m_ref, v_ref, g_ref, d_ref, mo_ref, vo_ref):
        g = ct_ref[:, 0:1] * dm_ref[0, 0:1, :]
        for b in range(1, N_DEV):
            g = g + ct_ref[:, b:b + 1] * dm_ref[0, b:b + 1, :]
        g_ref[0] = g
        d_ref[0], mo_ref[0], vo_ref[0] = _adamw_math(w_ref[0], g, m_ref[0], v_ref[0])

    wblk = pl.BlockSpec((1, D, tn), lambda l, j: (l, 0, j))
    out = jax.ShapeDtypeStruct(w.shape, jnp.float32)
    return pl.pallas_call(
        body, name="ada_grad_adamw", grid=(L, ncol // tn),
        in_specs=[_full((D, N_DEV)), pl.BlockSpec((1, N_DEV, tn), lambda l, j: (l, 0, j)), wblk, wblk, wblk],
        out_specs=[wblk] * 4, out_shape=[out] * 4, compiler_params=_params("parallel", "parallel"),
    )(cond_t, dm, w, m, v)


def _small_adamw(gathered, w, m, v):
    slots = _small_slots()
    rows = {name: (off // LANES, -(-n // LANES)) for name, (off, n) in slots.items()}
    kinds = {name: 1 if name == "loss" or name in BIASES else 4 for name in slots}

    def body(ga_ref, w_ref, m_ref, v_ref, *out_refs):
        g = ga_ref[0]
        for dev in range(1, N_DEV):
            g = g + ga_ref[dev]
        d, mo, vo = _adamw_math(w_ref[...], g, m_ref[...], v_ref[...])
        k = 0
        for name, (r0, nr) in rows.items():
            for src in (g, d, mo, vo)[:kinds[name]]:
                out_refs[k][...] = src[r0:r0 + nr, :]
                k += 1

    out_shape = [jax.ShapeDtypeStruct((rows[name][1], LANES), jnp.float32) for name in slots for _ in range(kinds[name])]
    flat = pl.pallas_call(
        body, name="small_adamw", out_shape=out_shape,
        in_specs=[pl.BlockSpec(memory_space=pltpu.VMEM)] * 4,
        out_specs=[pl.BlockSpec(memory_space=pltpu.VMEM)] * len(out_shape),
    )(gathered, w, m, v)
    out, k = {}, 0
    for name in slots:
        out[name] = [_from_slot(name, t) for t in flat[k:k + kinds[name]]]
        k += kinds[name]
    return out


def _one_hot_pick(arr, index, axis):
    n = arr.shape[axis]
    shape = [1] * arr.ndim
    shape[axis] = n
    hot = (jnp.arange(n) == index).astype(arr.dtype).reshape(shape)
    return jnp.sum(arr * hot, axis=axis)


def kernel(x, c, positions, w_ada, b_ada, g_mix, g_mlp, mla_w_dq, mla_g_q, mla_w_uq, mla_w_dkv, mla_g_kv, mla_w_ukv, mla_w_o, swa_w_qkv, swa_b_qkv, swa_sinks, swa_w_o, swa_b_o, w_ff1, w_ff2, g_final, loss_target, m_w_ada, m_b_ada, m_g_mix, m_g_mlp, m_mla_w_dq, m_mla_g_q, m_mla_w_uq, m_mla_w_dkv, m_mla_g_kv, m_mla_w_ukv, m_mla_w_o, m_swa_w_qkv, m_swa_b_qkv, m_swa_sinks, m_swa_w_o, m_swa_b_o, m_w_ff1, m_w_ff2, m_g_final, v_w_ada, v_b_ada, v_g_mix, v_g_mlp, v_mla_w_dq, v_mla_g_q, v_mla_w_uq, v_mla_w_dkv, v_mla_g_kv, v_mla_w_ukv, v_mla_w_o, v_swa_w_qkv, v_swa_b_qkv, v_swa_sinks, v_swa_w_o, v_swa_b_o, v_w_ff1, v_w_ff2, v_g_final):
    W = dict(w_ada=w_ada, b_ada=b_ada, g_mix=g_mix, g_mlp=g_mlp, mla_w_dq=mla_w_dq, mla_g_q=mla_g_q, mla_w_uq=mla_w_uq,
             mla_w_dkv=mla_w_dkv, mla_g_kv=mla_g_kv, mla_w_ukv=mla_w_ukv, mla_w_o=mla_w_o, swa_w_qkv=swa_w_qkv,
             swa_b_qkv=swa_b_qkv, swa_sinks=swa_sinks, swa_w_o=swa_w_o, swa_b_o=swa_b_o, w_ff1=w_ff1, w_ff2=w_ff2,
             g_final=g_final)
    M = dict(w_ada=m_w_ada, b_ada=m_b_ada, g_mix=m_g_mix, g_mlp=m_g_mlp, mla_w_dq=m_mla_w_dq, mla_g_q=m_mla_g_q,
             mla_w_uq=m_mla_w_uq, mla_w_dkv=m_mla_w_dkv, mla_g_kv=m_mla_g_kv, mla_w_ukv=m_mla_w_ukv, mla_w_o=m_mla_w_o,
             swa_w_qkv=m_swa_w_qkv, swa_b_qkv=m_swa_b_qkv, swa_sinks=m_swa_sinks, swa_w_o=m_swa_w_o, swa_b_o=m_swa_b_o,
             w_ff1=m_w_ff1, w_ff2=m_w_ff2, g_final=m_g_final)
    V = dict(w_ada=v_w_ada, b_ada=v_b_ada, g_mix=v_g_mix, g_mlp=v_g_mlp, mla_w_dq=v_mla_w_dq, mla_g_q=v_mla_g_q,
             mla_w_uq=v_mla_w_uq, mla_w_dkv=v_mla_w_dkv, mla_g_kv=v_mla_g_kv, mla_w_ukv=v_mla_w_ukv, mla_w_o=v_mla_w_o,
             swa_w_qkv=v_swa_w_qkv, swa_b_qkv=v_swa_b_qkv, swa_sinks=v_swa_sinks, swa_w_o=v_swa_w_o, swa_b_o=v_swa_b_o,
             w_ff1=v_w_ff1, w_ff2=v_w_ff2, g_final=v_g_final)
    order = list(W)
    names = list(SHARDED)
    core = lax.axis_index("c")
    chip = 2 * lax.axis_index("x") + lax.axis_index("y")
    dev = 2 * chip + core
    core_arr = core.astype(jnp.int32).reshape(1)
    chip_arr = chip.astype(jnp.int32).reshape(1)

    def whole(n, g, own):
        g = lax.dynamic_update_slice(g, own[None], (chip, 0, 0))
        if n in ("w_ff1", "w_ff2"):
            return g
        if n in COL_SPLIT:
            return g.transpose(1, 0, 2).reshape(g.shape[1], N_CHIPS * g.shape[2])
        return g.reshape(N_CHIPS * g.shape[1], g.shape[2])

    early = [n for n in names if n.startswith("mla_")]
    local = {n: W[n].astype(MXU_DTYPE).reshape(_view2d(n)) for n in early}
    wts = {n: whole(n, g, local[n]) for n, g in zip(early, _weight_gather([local[n] for n in early]))}

    nbq, nbo = BIASES["swa_b_qkv"] // N_CHIPS, BIASES["swa_b_o"] // N_CHIPS
    first = jnp.concatenate([c.reshape(-1), swa_b_qkv.reshape(-1), swa_b_o.reshape(-1),
                             jnp.zeros((16 * LANES - D - nbq - nbo,), jnp.float32)]).reshape(16, LANES)
    first_all = _all_gather(first).reshape(N_DEV, 16 * LANES)
    c_all = first_all[:, :D]
    south = first_all[0::2]
    wts["swa_b_qkv"] = south[:, D:D + nbq].reshape(1, N_CHIPS * nbq)
    wts["swa_b_o"] = south[:, D + nbq:D + nbq + nbo].reshape(1, N_CHIPS * nbo)
    cond_all, part = _ada_part(c_all, w_ada)
    ncol = w_ada.shape[2]
    part_all = _all_gather(part.reshape(-1, LANES)).reshape(N_DEV, DEPTH, N_DEV, ncol)
    mine = _one_hot_pick(part_all[0::2], dev, axis=2)
    mod = mine.transpose(1, 0, 2).reshape(DEPTH, N_CHIPS * ncol) + b_ada
    vecs = jnp.concatenate([mod.reshape(DEPTH, 6, D), g_mix[:, None, :], g_mlp[:, None, :]], axis=1)

    late = [("w_ff1", 0), ("w_ff2", 0), ("swa_w_qkv", None), ("swa_w_o", None), ("w_ff1", 1), ("w_ff2", 1)]
    late_local = [(W[n][0] if l is None else W[n][l]).astype(MXU_DTYPE) for n, l in late]
    send_sems, recv_sems, passed, lands, token = _late_gather_start(late_local, [vecs] + [wts[n] for n in early])

    def late_weights(after):
        got = _late_gather_wait(send_sems, recv_sems, passed, lands, after)
        out = {"w_ff1": [None] * DEPTH, "w_ff2": [None] * DEPTH}
        for (n, l), g, own in zip(late, got, late_local):
            if l is None:
                out[n] = whole(n, g, own)
            else:
                out[n][l] = whole(n, g, own)
        return out

    late_names = [n for n in names if n not in early]
    reduce_state = {}

    def on_late_grads(late_grads):
        s_sems, r_sems, passed_g, zones, tok = _pair_in_start([late_grads[n] for n in late_names])
        reduce_state.update(pair=(s_sems, r_sems, passed_g, zones))
        return tok

    def on_late_landed(after):
        gl, got = _pair_in_wait(*reduce_state["pair"], after)
        sums = [_pair_sum(g, s, core_arr, "pair_sum_" + n) for n, g, s in zip(late_names, gl, got)]
        s_sems, r_sems, parts, zones, tok = _exchange_start([s16 for _, s16 in sums], "grad_exchange_start")
        reduce_state.update(sums=sums, split=(s_sems, r_sems, parts, zones))
        return tok

    grad_x, grads, small = _sequence_step(
        x[0], loss_target[0], positions[0], vecs, mla_g_q + token[0, 0], mla_g_kv, swa_sinks, g_final, wts,
        late_weights, on_late_grads, on_late_landed)

    small["b_ada"] = small.pop("dmod")
    small_all = _all_gather(_pack_small(small))
    pk = lambda src: _pack_small({n: src[n] for n in SMALL if n != "loss" and n not in BIASES})
    off, n = _small_slots()["b_ada"]
    dmod_all = small_all.reshape(N_DEV, -1)[:, off:off + n].reshape(N_DEV, DEPTH, N_CHIPS, ncol)
    dm = _one_hot_pick(dmod_all, chip, axis=2).transpose(1, 0, 2)

    gl = [grads[n] for n in early]
    got = _grad_pair_in(gl)
    sums = [_pair_sum(g, s, core_arr, "pair_sum_" + n) for n, g, s in zip(early, gl, got)]
    e_sems, e_rems, e_parts, e_zones, e_tok = _exchange_start([s16 for _, s16 in sums], "mla_exchange_start")

    def finish(tensor_names, sums, others, behind):
        halves = [_chip_sum(s32, o, chip_arr, "chip_sum_" + n, behind) for n, (s32, _), o in zip(tensor_names, sums, others)]
        return {n: _adamw_halves(W[n], mine_h, got_h, M[n], V[n], core_arr, "adamw_" + n)
                for n, mine_h, got_h in zip(tensor_names, halves, _grad_pair_out(halves))}

    late_others = _exchange_wait(*reduce_state["split"], grad_x, "grad_exchange_wait")
    res = finish(late_names, reduce_state["sums"], late_others, e_tok)
    res["w_ada"] = _ada_grad_adamw(cond_all.T, dm, w_ada, m_w_ada, v_w_ada)
    small_res = _small_adamw(small_all, pk(W), pk(M), pk(V))
    early_others = _exchange_wait(e_sems, e_rems, e_parts, e_zones, res["w_ff2"][1], "mla_exchange_wait")
    res.update(finish(early, sums, early_others, None))

    for n, width in BIASES.items():
        g = _one_hot_pick(small_res[n][0].reshape(N_CHIPS, width // N_CHIPS), chip, axis=0).reshape(1, -1)
        res[n] = [g] + _adamw(W[n], g, M[n], V[n], "adamw_" + n)
    for name in order:
        if name not in res:
            res[name] = small_res[name]
    outs = [small_res["loss"][0], grad_x[None]]
    for k in range(4):
        outs += [res[name][k] for name in order]
    return tuple(outs)
```

```python
import functools
import math

import jax
import jax.numpy as jnp
import numpy as np
from jax import lax
from jax.experimental import pallas as pl
from jax.experimental.pallas import tpu as pltpu

D = 1024
DEPTH = 2
MLA_HEADS = 8
QK_NOPE = 128
QK_ROPE = 64
V_DIM = 128
Q_LORA = 384
KV_LORA = 256
ROPE_THETA = 10000.0
SWA_HEADS = 16
SWA_KV_HEADS = 4
SWA_HEAD_DIM = 64
SWA_GROUP = SWA_HEADS // SWA_KV_HEADS
WINDOW = 128
D_FF = 4 * D
EPS = 1e-6
ADAM_LR = 0.001
ADAM_B1 = 0.9
ADAM_B2 = 0.999
ADAM_EPS = 1e-08
ADAM_WD = 0.01
ADAM_STEP = 10

N_CHIPS = 4
N_DEV = 8
LANES = 128
QK_EXT = 256
MLA_SCALE = (QK_NOPE + QK_ROPE) ** -0.5
LOG2E = math.log2(math.e)
LN2 = math.log(2.0)
MLA_QSCALE = MLA_SCALE * LOG2E
ATTN_BLOCK = 2048
ATTN_SUB = 512
MLP_FWD_TILE = (1024, 512)
MLP_BWD_TILE = (512, 1024)
DW_TOKENS = 2048
SWA_SCALE = SWA_HEAD_DIM ** -0.5
NEG = -1e30
MXU_DTYPE = jnp.bfloat16
VMEM_LIMIT = 56 * 1024 * 1024

R_SH1, R_SC1, R_GT1, R_SH2, R_SC2, R_GT2, R_GMIX, R_GMLP = range(8)
R_BO = 6


def _tile(n, pref):
    if n <= pref:
        return n
    for t in range(pref, 7, -1):
        if n % t == 0 and t % 8 == 0:
            return t
    return n


def _dot(a, b):
    return jnp.dot(a, b, preferred_element_type=jnp.float32)


def _dot_nt(a, b):
    return lax.dot_general(a, b, (((1,), (1,)), ((), ())), preferred_element_type=jnp.float32)


def _dot_tn(a, b):
    return lax.dot_general(a, b, (((0,), (0,)), ((), ())), preferred_element_type=jnp.float32)


def _rms(x):
    r = lax.rsqrt(jnp.mean(x * x, axis=-1, keepdims=True) + EPS)
    return x * r, r


def _rms_bwd(dxhat, xhat, r):
    return r * (dxhat - xhat * jnp.mean(dxhat * xhat, axis=-1, keepdims=True))


def _rowsum(v):
    return jnp.sum(v, axis=0, keepdims=True)


def _params(*sem):
    return pltpu.CompilerParams(dimension_semantics=sem, vmem_limit_bytes=VMEM_LIMIT)


def _full(shape):
    nd = len(shape)
    return pl.BlockSpec(shape, lambda *_: (0,) * nd)


def _rows(tm, cols):
    return pl.BlockSpec((tm, cols), lambda i, *_: (i, 0))


def _modulate_bwd(dh, x, vec_ref, r_g, r_sc, r_sh, ps_ref, dres):
    xhat, r = _rms(x)
    g = vec_ref[r_g:r_g + 1, :]
    n = xhat * g
    ps_ref[r_sh:r_sh + 1, :] += _rowsum(dh)
    ps_ref[r_sc:r_sc + 1, :] += _rowsum(dh * n)
    dn = dh * (1.0 + vec_ref[r_sc:r_sc + 1, :])
    ps_ref[r_g:r_g + 1, :] += _rowsum(dn * xhat)
    return dres + _rms_bwd(dn * g, xhat, r)


def _mla_pre(x, vec, wcat, g_q, g_kv, wuq, wukv, cs):
    T = x.shape[0]
    tm = _tile(T, 512)
    H = MLA_HEADS

    def body(x_ref, vec_ref, wcat_ref, gq_ref, gkv_ref, wuq_ref, wukv_ref, cs_ref, h_ref, z_ref, q_ref, k_ref, v_ref):
        xhat, _ = _rms(x_ref[...])
        h = xhat * vec_ref[R_GMIX:R_GMIX + 1, :] * (1.0 + vec_ref[R_SC1:R_SC1 + 1, :]) + vec_ref[R_SH1:R_SH1 + 1, :]
        hb = h.astype(MXU_DTYPE)
        h_ref[...] = hb
        z = _dot(hb, wcat_ref[...])
        z_ref[...] = z
        cq = (_rms(z[:, :Q_LORA])[0] * gq_ref[...]).astype(MXU_DTYPE)
        ckv = (_rms(z[:, Q_LORA:Q_LORA + KV_LORA])[0] * gkv_ref[...]).astype(MXU_DTYPE)
        cs_t = cs_ref[...]
        t = z[:, Q_LORA + KV_LORA:] * cs_t
        k_rope = (t + pltpu.roll(t, QK_ROPE, axis=1)).astype(MXU_DTYPE)
        low = lax.broadcasted_iota(jnp.int32, (1, LANES), 1) < QK_ROPE
        for hd in range(H):
            qf = _dot(cq, wuq_ref[hd])
            tq = qf[:, QK_NOPE:] * cs_t
            tq = tq + pltpu.roll(tq, QK_ROPE, axis=1)
            q_ref[hd, :, :QK_NOPE] = (qf[:, :QK_NOPE] * MLA_QSCALE).astype(MXU_DTYPE)
            q_ref[hd, :, QK_NOPE:] = jnp.where(low, tq * MLA_QSCALE, 0.0).astype(MXU_DTYPE)
            kvf = _dot(ckv, wukv_ref[hd])
            k_ref[hd, :, :QK_NOPE] = kvf[:, :QK_NOPE].astype(MXU_DTYPE)
            k_ref[hd, :, QK_NOPE:] = k_rope
            v_ref[hd] = kvf[:, QK_NOPE:].astype(MXU_DTYPE)

    zc = wcat.shape[1]
    return pl.pallas_call(
        body, name="mla_pre", grid=(T // tm,),
        in_specs=[_rows(tm, D), _full((8, D)), _full(wcat.shape), _full(g_q.shape), _full(g_kv.shape),
                  _full(wuq.shape), _full(wukv.shape), _rows(tm, LANES)],
        out_specs=[_rows(tm, D), _rows(tm, zc),
                   pl.BlockSpec((H, tm, QK_EXT), lambda i: (0, i, 0)),
                   pl.BlockSpec((H, tm, QK_EXT), lambda i: (0, i, 0)),
                   pl.BlockSpec((H, tm, V_DIM), lambda i: (0, i, 0))],
        out_shape=[jax.ShapeDtypeStruct((T, D), MXU_DTYPE), jax.ShapeDtypeStruct((T, zc), jnp.float32),
                   jax.ShapeDtypeStruct((H, T, QK_EXT), MXU_DTYPE), jax.ShapeDtypeStruct((H, T, QK_EXT), MXU_DTYPE),
                   jax.ShapeDtypeStruct((H, T, V_DIM), MXU_DTYPE)],
        compiler_params=_params("parallel"),
    )(x, vec, wcat, g_q, g_kv, wuq, wukv, cs)


def _mla_attn_fwd(q, k, v):
    H, T, _ = q.shape
    tb = _tile(T, ATTN_BLOCK)
    sub = min(ATTN_SUB, tb)
    ns, nb = tb // sub, T // tb

    def body(q_ref, k_ref, v_ref, o_ref, lse_ref, m_sc, l_sc, acc_sc):
        qi, kj = pl.program_id(1), pl.program_id(2)

        @pl.when(kj == 0)
        def _():
            m_sc[...] = jnp.full_like(m_sc, NEG)
            l_sc[...] = jnp.zeros_like(l_sc)
            acc_sc[...] = jnp.zeros_like(acc_sc)

        def update(r, kk, masked):
            rows, keys = pl.ds(r * sub, sub), pl.ds(kk * sub, sub)
            s = _dot_nt(q_ref[0, rows, :], k_ref[0, keys, :])
            if masked:
                row = lax.broadcasted_iota(jnp.int32, (sub, sub), 0)
                col = lax.broadcasted_iota(jnp.int32, (sub, sub), 1)
                s = jnp.where(col <= row, s, NEG)
            m_prev = m_sc[rows, :]
            m_new = jnp.maximum(m_prev, jnp.max(s, axis=1, keepdims=True))
            alpha = jnp.exp2(m_prev - m_new)
            p = jnp.exp2(s - jnp.tile(m_new, (1, sub // LANES)))
            l_sc[rows, :] = alpha * l_sc[rows, :] + jnp.sum(p, axis=1, keepdims=True)
            acc_sc[rows, :] = alpha * acc_sc[rows, :] + _dot(p.astype(MXU_DTYPE), v_ref[0, keys, :])
            m_sc[rows, :] = m_new

        @pl.when(kj < qi)
        def _():
            for kk in range(ns):
                for r in range(ns):
                    update(r, kk, False)

        @pl.when(kj == qi)
        def _():
            for kk in range(ns):
                for r in range(kk, ns):
                    update(r, kk, r == kk)
            l = l_sc[...]
            o_ref[...] = (acc_sc[...] / l).astype(o_ref.dtype)
            lse = m_sc[...] + jnp.log2(l)
            pick = (lax.broadcasted_iota(jnp.int32, (8, LANES), 1) == 0).astype(jnp.float32)
            row = lax.dot_general(pick, lse, (((1,), (1,)), ((), ())), precision=lax.Precision.HIGHEST,
                                  preferred_element_type=jnp.float32)
            lse_ref[0] = row[0:1, :]

    kv_idx = lambda h, i, j: (h, jnp.minimum(i, j), 0)
    return pl.pallas_call(
        body, name="mla_attn_fwd", grid=(H, nb, nb),
        in_specs=[pl.BlockSpec((1, tb, QK_EXT), lambda h, i, j: (h, i, 0)),
                  pl.BlockSpec((1, tb, QK_EXT), kv_idx),
                  pl.BlockSpec((1, tb, V_DIM), kv_idx)],
        out_specs=[pl.BlockSpec((tb, V_DIM), lambda h, i, j: (i, h)),
                   pl.BlockSpec((1, 1, tb), lambda h, i, j: (h, 0, i))],
        out_shape=[jax.ShapeDtypeStruct((T, H * V_DIM), MXU_DTYPE), jax.ShapeDtypeStruct((H, 1, T), jnp.float32)],
        scratch_shapes=[pltpu.VMEM((tb, LANES), jnp.float32), pltpu.VMEM((tb, LANES), jnp.float32),
                        pltpu.VMEM((tb, V_DIM), jnp.float32)],
        compiler_params=_params("parallel", "parallel", "arbitrary"),
    )(q, k, v)


def _post_attn(o, x, w_o, bias, vec, o_transposed=False):
    T = x.shape[0]
    tm = _tile(T, 512)
    o_spec = pl.BlockSpec((D, tm), lambda i: (0, i)) if o_transposed else _rows(tm, D)

    def body(o_ref, x_ref, w_ref, b_ref, vec_ref, y_ref, xm_ref, h_ref):
        y = (_dot_tn if o_transposed else _dot)(o_ref[...], w_ref[...]) + b_ref[...]
        y_ref[...] = y.astype(y_ref.dtype)
        xm = x_ref[...] + vec_ref[R_GT1:R_GT1 + 1, :] * y
        xm_ref[...] = xm
        xhat, _ = _rms(xm)
        h = xhat * vec_ref[R_GMLP:R_GMLP + 1, :] * (1.0 + vec_ref[R_SC2:R_SC2 + 1, :]) + vec_ref[R_SH2:R_SH2 + 1, :]
        h_ref[...] = h.astype(h_ref.dtype)

    return pl.pallas_call(
        body, name="post_attn", grid=(T // tm,),
        in_specs=[o_spec, _rows(tm, D), _full((D, D)), _full((1, D)), _full((8, D))],
        out_specs=[_rows(tm, D), _rows(tm, D), _rows(tm, D)],
        out_shape=[jax.ShapeDtypeStruct((T, D), MXU_DTYPE), jax.ShapeDtypeStruct((T, D), jnp.float32),
                   jax.ShapeDtypeStruct((T, D), MXU_DTYPE)],
        compiler_params=_params("parallel"),
    )(o, x, w_o, bias, vec)


def _ff_specs(tf):
    per = D_FF // N_CHIPS // tf
    w1 = pl.BlockSpec((None, D, tf), lambda i, f: (f // per, 0, f % per))
    w2 = pl.BlockSpec((None, tf, D), lambda i, f: (f // per, f % per, 0))
    return w1, w2


def _mlp_fwd(h2, w1, w2, xm, vec):
    T = h2.shape[0]
    tm = _tile(T, MLP_FWD_TILE[0])
    tf = _tile(D_FF // N_CHIPS, MLP_FWD_TILE[1])
    nf = D_FF // tf
    w1_spec, w2_spec = _ff_specs(tf)

    def body(h_ref, w1_ref, w2_ref, xm_ref, vec_ref, a_ref, y_ref, xo_ref, acc):
        f = pl.program_id(1)

        @pl.when(f == 0)
        def _():
            acc[...] = jnp.zeros_like(acc)

        u = jnp.maximum(_dot(h_ref[...], w1_ref[...]), 0.0)
        ab = (u * u).astype(MXU_DTYPE)
        a_ref[...] = ab
        acc[...] += _dot(ab, w2_ref[...])

        @pl.when(f == nf - 1)
        def _():
            y = acc[...]
            y_ref[...] = y.astype(y_ref.dtype)
            xo_ref[...] = xm_ref[...] + vec_ref[R_GT2:R_GT2 + 1, :] * y

    return pl.pallas_call(
        body, name="mlp_fwd", grid=(T // tm, nf),
        in_specs=[_rows(tm, D), w1_spec, w2_spec, _rows(tm, D), _full((8, D))],
        out_specs=[pl.BlockSpec((tm, tf), lambda i, f: (i, f)), _rows(tm, D), _rows(tm, D)],
        out_shape=[jax.ShapeDtypeStruct((T, D_FF), MXU_DTYPE), jax.ShapeDtypeStruct((T, D), MXU_DTYPE),
                   jax.ShapeDtypeStruct((T, D), jnp.float32)],
        scratch_shapes=[pltpu.VMEM((tm, D), jnp.float32)],
        compiler_params=_params("parallel", "arbitrary"),
    )(h2, w1, w2, xm, vec)


def _swa_pre(x, vec, w_qkv, b_qkv):
    T = x.shape[0]
    tm = _tile(T, 512)
    nq = SWA_HEADS * SWA_HEAD_DIM
    nk = SWA_KV_HEADS * SWA_HEAD_DIM
    wq_t, w_kv = w_qkv[:, :nq].T, w_qkv[:, nq:]
    bq_col, b_kv = b_qkv[:, :nq].reshape(nq, 1), b_qkv[:, nq:]

    def body(x_ref, vec_ref, wq_ref, wkv_ref, bq_ref, bkv_ref, h_ref, qt_ref, k_ref, v_ref):
        xhat, _ = _rms(x_ref[...])
        h = xhat * vec_ref[R_GMIX:R_GMIX + 1, :] * (1.0 + vec_ref[R_SC1:R_SC1 + 1, :]) + vec_ref[R_SH1:R_SH1 + 1, :]
        hb = h.astype(MXU_DTYPE)
        h_ref[...] = hb
        qt_ref[...] = ((_dot_nt(wq_ref[...], hb) + bq_ref[...]) * SWA_SCALE).astype(MXU_DTYPE)
        kv = _dot(hb, wkv_ref[...]) + bkv_ref[...]
        k_ref[...] = kv[:, :nk].astype(MXU_DTYPE)
        v_ref[...] = kv[:, nk:].astype(MXU_DTYPE)

    return pl.pallas_call(
        body, name="swa_pre", grid=(T // tm,),
        in_specs=[_rows(tm, D), _full((8, D)), _full(wq_t.shape), _full(w_kv.shape), _full(bq_col.shape),
                  _full(b_kv.shape)],
        out_specs=[_rows(tm, D), pl.BlockSpec((nq, tm), lambda i: (0, i)), _rows(tm, nk), _rows(tm, nk)],
        out_shape=[jax.ShapeDtypeStruct((T, D), MXU_DTYPE), jax.ShapeDtypeStruct((nq, T), MXU_DTYPE),
                   jax.ShapeDtypeStruct((T, nk), MXU_DTYPE), jax.ShapeDtypeStruct((T, nk), MXU_DTYPE)],
        compiler_params=_params("parallel"),
    )(x, vec, wq_t, w_kv, bq_col, b_kv)


def _swa_bias():
    W = WINDOW
    slopes = 2.0 ** (-8.0 * np.arange(1, SWA_HEADS + 1) / SWA_HEADS)
    dist = W + np.arange(W)[None, :] - np.arange(2 * W)[:, None]
    inside = (dist >= 0) & (dist < W)
    bias = np.where(inside[None], -slopes[:, None, None] * dist[None].astype(np.float64), NEG)
    bias = bias.reshape(SWA_KV_HEADS, SWA_GROUP, 2 * W, W).transpose(0, 2, 1, 3)
    return jnp.asarray(bias.reshape(SWA_KV_HEADS, 2 * W, SWA_GROUP * W), jnp.float32)


SWA_STEP_BLOCKS = 4


def _swa_blocks(T):
    nb = T // WINDOW
    return next(b for b in (SWA_STEP_BLOCKS, 2, 1) if nb % b == 0)


def _swa_views(b, qt_ref, kp_ref, kc_ref):
    W = WINDOW
    prev = kp_ref if b == 0 else kc_ref.at[pl.ds((b - 1) * W, W), :]
    return qt_ref.at[:, pl.ds(b * W, W)], prev, kc_ref.at[pl.ds(b * W, W), :]


def _swa_probs(has_prev, kh, qt_ref, kp_ref, kc_ref, bias_ref, sink_ref):
    W, Dh, G = WINDOW, SWA_HEAD_DIM, SWA_GROUP
    qt = jnp.concatenate([qt_ref[(kh * G + g) * Dh:(kh * G + g + 1) * Dh, :] for g in range(G)], axis=1)
    kb = jnp.concatenate([kp_ref[:, kh * Dh:(kh + 1) * Dh], kc_ref[:, kh * Dh:(kh + 1) * Dh]], axis=0)
    s = _dot(kb, qt) + bias_ref[kh]
    if has_prev is not True:
        key = lax.broadcasted_iota(jnp.int32, (2 * W, 1), 0)
        s = jnp.where((key >= W) | has_prev, s, NEG)
    sink = sink_ref[kh]
    m = jnp.maximum(jnp.max(s, axis=0, keepdims=True), sink)
    p = jnp.exp(s - m)
    p_sink = jnp.exp(sink - m)
    inv = 1.0 / (jnp.sum(p, axis=0, keepdims=True) + p_sink)
    return qt, kb, p * inv, p_sink * inv


def _swa_attn_fwd(qt, k, v, bias, sink_rows):
    T = qt.shape[1]
    W, Dh, G, Hk = WINDOW, SWA_HEAD_DIM, SWA_GROUP, SWA_KV_HEADS
    nk = Hk * Dh

    nb = _swa_blocks(T)

    def body(qt_ref, kp_ref, kc_ref, vp_ref, vc_ref, bias_ref, sink_ref, ot_ref):
        n = pl.program_id(0)
        for b in range(nb):
            q_b, kp_b, kc_b = _swa_views(b, qt_ref, kp_ref, kc_ref)
            _, vp_b, vc_b = _swa_views(b, qt_ref, vp_ref, vc_ref)
            for kh in range(Hk):
                _, _, pn, _ = _swa_probs(True if b else n > 0, kh, q_b, kp_b, kc_b, bias_ref, sink_ref)
                vb = jnp.concatenate([vp_b[:, kh * Dh:(kh + 1) * Dh], vc_b[:, kh * Dh:(kh + 1) * Dh]], axis=0)
                ot = _dot_tn(vb, pn.astype(MXU_DTYPE))
                for g in range(G):
                    rows = pl.ds((kh * G + g) * Dh, Dh)
                    ot_ref[rows, pl.ds(b * W, W)] = ot[:, g * W:(g + 1) * W].astype(ot_ref.dtype)

    prev = lambda n: (jnp.maximum(n * nb - 1, 0), 0)
    cur = lambda n: (n, 0)
    col = lambda n: (0, n)
    return pl.pallas_call(
        body, name="swa_attn_fwd", grid=(T // (nb * W),),
        in_specs=[pl.BlockSpec((D, nb * W), col), pl.BlockSpec((W, nk), prev), pl.BlockSpec((nb * W, nk), cur),
                  pl.BlockSpec((W, nk), prev), pl.BlockSpec((nb * W, nk), cur), _full(bias.shape),
                  _full(sink_rows.shape)],
        out_specs=pl.BlockSpec((D, nb * W), col),
        out_shape=jax.ShapeDtypeStruct((D, T), MXU_DTYPE),
        compiler_params=_params("parallel"),
    )(qt, k, k, v, v, bias, sink_rows)


def _final_loss(x, tgt, g):
    T = x.shape[0]
    tm = _tile(T, 512)

    def body(x_ref, t_ref, g_ref, loss_ref, dx_ref, dg_ref):
        @pl.when(pl.program_id(0) == 0)
        def _():
            loss_ref[...] = jnp.zeros_like(loss_ref)
            dg_ref[...] = jnp.zeros_like(dg_ref)

        xhat, r = _rms(x_ref[...])
        gv = g_ref[...]
        e = xhat * gv - t_ref[...]
        loss_ref[...] += 0.5 * jnp.sum(jnp.mean(e * e, axis=-1, keepdims=True), axis=0, keepdims=True)
        dy = e * (1.0 / D)
        dg_ref[...] += _rowsum(dy * xhat)
        dx_ref[...] = _rms_bwd(dy * gv, xhat, r)

    return pl.pallas_call(
        body, name="final_loss", grid=(T // tm,),
        in_specs=[_rows(tm, D), _rows(tm, D), _full((1, D))],
        out_specs=[_full((8, LANES)), _rows(tm, D), _full((1, D))],
        out_shape=[jax.ShapeDtypeStruct((8, LANES), jnp.float32), jax.ShapeDtypeStruct((T, D), jnp.float32),
                   jax.ShapeDtypeStruct((1, D), jnp.float32)],
        compiler_params=_params("arbitrary"),
    )(x, tgt, g)


def _mlp_bwd(dxo, y2, a, w1, w2, xm, vec):
    T = dxo.shape[0]
    tm = _tile(T, MLP_BWD_TILE[0])
    tf = _tile(D_FF // N_CHIPS, MLP_BWD_TILE[1])
    nf = D_FF // tf
    w1_spec, w2_spec = _ff_specs(tf)

    def body(dxo_ref, y_ref, a_ref, w1_ref, w2_ref, xm_ref, vec_ref, du_ref, dy_ref, dxm_ref, ps_ref, dyb, acc):
        i, f = pl.program_id(0), pl.program_id(1)

        @pl.when((i == 0) & (f == 0))
        def _():
            ps_ref[...] = jnp.zeros_like(ps_ref)

        @pl.when(f == 0)
        def _():
            dxo_t = dxo_ref[...]
            d = (dxo_t * vec_ref[R_GT2:R_GT2 + 1, :]).astype(MXU_DTYPE)
            dyb[...] = d
            dy_ref[...] = d
            acc[...] = jnp.zeros_like(acc)
            ps_ref[R_GT2:R_GT2 + 1, :] += _rowsum(dxo_t * y_ref[...].astype(jnp.float32))

        da = _dot_nt(dyb[...], w2_ref[...])
        dub = (da * (2.0 * jnp.sqrt(a_ref[...].astype(jnp.float32)))).astype(MXU_DTYPE)
        du_ref[...] = dub
        acc[...] += _dot_nt(dub, w1_ref[...])

        @pl.when(f == nf - 1)
        def _():
            dxm_ref[...] = _modulate_bwd(acc[...], xm_ref[...], vec_ref, R_GMLP, R_SC2, R_SH2, ps_ref, dxo_ref[...])

    return pl.pallas_call(
        body, name="mlp_bwd", grid=(T // tm, nf),
        in_specs=[_rows(tm, D), _rows(tm, D), pl.BlockSpec((tm, tf), lambda i, f: (i, f)), w1_spec, w2_spec,
                  _rows(tm, D), _full((8, D))],
        out_specs=[pl.BlockSpec((tm, tf), lambda i, f: (i, f)), _rows(tm, D), _rows(tm, D), _full((8, D))],
        out_shape=[jax.ShapeDtypeStruct((T, D_FF), MXU_DTYPE), jax.ShapeDtypeStruct((T, D), MXU_DTYPE),
                   jax.ShapeDtypeStruct((T, D), jnp.float32), jax.ShapeDtypeStruct((8, D), jnp.float32)],
        scratch_shapes=[pltpu.VMEM((tm, D), MXU_DTYPE), pltpu.VMEM((tm, D), jnp.float32)],
        compiler_params=_params("arbitrary", "arbitrary"),
    )(dxo, y2, a, w1, w2, xm, vec)


def _mm_tn(a, g, name, split=None, layers=1, layer=0, into=None, a_transposed=False):
    K, T = a.shape if a_transposed else a.shape[::-1]
    N = g.shape[1]
    kq = K // N_CHIPS if split == "rows" else K
    nq = N // N_CHIPS if split == "cols" else N
    bk, bn, bt = _tile(kq, 1024), _tile(nq, 1024), _tile(T, DW_TOKENS)
    if nq % bn or bn % LANES:
        bn = nq
    kper, nper = kq // bk, nq // bn

    def body(*refs):
        a_ref, g_ref, o_ref = refs[0], refs[1], refs[-1]

        @pl.when(pl.program_id(2) == 0)
        def _():
            o_ref[...] = jnp.zeros_like(o_ref)

        o_ref[...] += (_dot if a_transposed else _dot_tn)(a_ref[...], g_ref[...])

    a_spec = pl.BlockSpec((bk, bt), lambda k, n, t: (k, t)) if a_transposed else pl.BlockSpec((bt, bk), lambda k, n, t: (t, k))
    in_specs = [a_spec, pl.BlockSpec((bt, bn), lambda k, n, t: (t, n))]
    args = [a, g]
    aliases = {}
    if split is None:
        out_spec = pl.BlockSpec((bk, bn), lambda k, n, t: (k, n))
        out_shape = jax.ShapeDtypeStruct((K, N), jnp.float32)
    else:
        if split == "cols":
            idx = lambda k, n, t: (n // nper, layer, k, n % nper)
        else:
            idx = lambda k, n, t: (k // kper, layer, k % kper, n)
        out_spec = pl.BlockSpec((None, None, bk, bn), idx)
        out_shape = jax.ShapeDtypeStruct((N_CHIPS, layers, kq, nq), jnp.float32)
        if into is not None:
            in_specs.append(pl.BlockSpec(memory_space=pl.ANY))
            args.append(into)
            aliases = {2: 0}
    return pl.pallas_call(
        body, name=name, grid=(K // bk, N // bn, T // bt), in_specs=in_specs, out_specs=out_spec, out_shape=out_shape,
        input_output_aliases=aliases, compiler_params=_params("parallel", "parallel", "arbitrary"),
    )(*args)


def _attn_out_bwd(dxm, y1, o, w_o, vec, with_delta):
    T = dxm.shape[0]
    tm = _tile(T, 512)
    H = MLA_HEADS

    def body(dxm_ref, y_ref, w_ref, vec_ref, *refs):
        o_ref = refs[0] if with_delta else None
        dy_ref, do_ref, ps_ref, *delta_ref = refs[1:] if with_delta else refs

        @pl.when(pl.program_id(0) == 0)
        def _():
            ps_ref[...] = jnp.zeros_like(ps_ref)

        dxm_t = dxm_ref[...]
        dy = dxm_t * vec_ref[R_GT1:R_GT1 + 1, :]
        ps_ref[R_GT1:R_GT1 + 1, :] += _rowsum(dxm_t * y_ref[...].astype(jnp.float32))
        ps_ref[R_BO:R_BO + 1, :] += _rowsum(dy)
        dyb = dy.astype(MXU_DTYPE)
        dy_ref[...] = dyb
        if not with_delta:
            do_ref[...] = _dot_nt(w_ref[...], dyb).astype(do_ref.dtype)
        else:
            do = _dot_nt(dyb, w_ref[...])
            do_ref[...] = do.astype(do_ref.dtype)
            of = o_ref[...].astype(jnp.float32)
            ones = jnp.ones((8, V_DIM), jnp.float32)
            for hd in range(H):
                sl = slice(hd * V_DIM, (hd + 1) * V_DIM)
                d = lax.dot_general(ones, do[:, sl] * of[:, sl], (((1,), (1,)), ((), ())),
                                    precision=lax.Precision.HIGHEST, preferred_element_type=jnp.float32)
                delta_ref[0][hd] = d[0:1, :]

    out_specs = [_rows(tm, D), _rows(tm, D), _full((8, D))]
    out_shape = [jax.ShapeDtypeStruct((T, D), MXU_DTYPE), jax.ShapeDtypeStruct((T, D), MXU_DTYPE),
                 jax.ShapeDtypeStruct((8, D), jnp.float32)]
    if not with_delta:
        out_specs[1] = pl.BlockSpec((D, tm), lambda i: (0, i))
        out_shape[1] = jax.ShapeDtypeStruct((D, T), MXU_DTYPE)
    if with_delta:
        out_specs.append(pl.BlockSpec((H, 1, tm), lambda i: (0, 0, i)))
        out_shape.append(jax.ShapeDtypeStruct((H, 1, T), jnp.float32))
    return pl.pallas_call(
        body, name="attn_out_bwd_mla" if with_delta else "attn_out_bwd_swa", grid=(T // tm,),
        in_specs=[_rows(tm, D), _rows(tm, D), _full((D, D)), _full((8, D))] + ([_rows(tm, D)] if with_delta else []),
        out_specs=out_specs, out_shape=out_shape,
        compiler_params=_params("arbitrary"),
    )(dxm, y1, w_o, vec, *([o] if with_delta else []))


def _mla_attn_bwd(q, k, v, do, lse, delta):
    H, T, _ = q.shape
    tb = _tile(T, ATTN_BLOCK)
    sub = min(ATTN_SUB, tb)
    ns, nb = tb // sub, T // tb

    def body(q_ref, k_ref, v_ref, do_ref, lse_ref, dl_ref, dq_ref, dk_ref, dv_ref, dk_acc, dv_acc):
        j, i = pl.program_id(1), pl.program_id(2)

        @pl.when((j == 0) & (i == 0))
        def _():
            dq_ref[...] = jnp.zeros_like(dq_ref)

        def update(kk, r, masked):
            keys, rows = pl.ds(kk * sub, sub), pl.ds(r * sub, sub)
            kb, qb, dob = k_ref[0, keys, :], q_ref[0, rows, :], do_ref[rows, :]
            st = _dot_nt(kb, qb)
            if masked:
                row = lax.broadcasted_iota(jnp.int32, (sub, sub), 0)
                col = lax.broadcasted_iota(jnp.int32, (sub, sub), 1)
                st = jnp.where(row <= col, st, NEG)
            pt = jnp.exp2(st - lse_ref[0, :, rows])
            dv_acc[keys, :] += _dot(pt.astype(MXU_DTYPE), dob)
            dpt = _dot_nt(v_ref[0, keys, :], dob)
            dst = (pt * (dpt - dl_ref[0, :, rows])).astype(MXU_DTYPE)
            dk_acc[keys, :] += _dot(dst, qb)
            q_rows = pl.ds(pl.multiple_of(i * tb + r * sub, sub), sub)
            dq_ref[0, q_rows, :] += _dot_tn(dst, kb)

        @pl.when(i == j)
        def _():
            dk_acc[...] = jnp.zeros_like(dk_acc)
            dv_acc[...] = jnp.zeros_like(dv_acc)
            for r in range(ns):
                for kk in range(r + 1):
                    update(kk, r, kk == r)

        @pl.when(i > j)
        def _():
            for r in range(ns):
                for kk in range(ns):
                    update(kk, r, False)

        @pl.when(i == nb - 1)
        def _():
            dk_ref[0] = (dk_acc[...] * LN2).astype(dk_ref.dtype)
            dv_ref[0] = dv_acc[...].astype(dv_ref.dtype)

    q_idx = lambda h, j, i: (h, jnp.maximum(i, j), 0)
    kv_idx = lambda h, j, i: (h, j, 0)
    stat_idx = lambda h, j, i: (h, 0, jnp.maximum(i, j))
    return pl.pallas_call(
        body, name="mla_attn_bwd", grid=(H, nb, nb),
        in_specs=[pl.BlockSpec((1, tb, QK_EXT), q_idx), pl.BlockSpec((1, tb, QK_EXT), kv_idx),
                  pl.BlockSpec((1, tb, V_DIM), kv_idx),
                  pl.BlockSpec((tb, V_DIM), lambda h, j, i: (jnp.maximum(i, j), h)),
                  pl.BlockSpec((1, 1, tb), stat_idx), pl.BlockSpec((1, 1, tb), stat_idx)],
        out_specs=[pl.BlockSpec((1, T, QK_EXT), lambda h, j, i: (h, 0, 0)),
                   pl.BlockSpec((1, tb, QK_EXT), kv_idx), pl.BlockSpec((1, tb, V_DIM), kv_idx)],
        out_shape=[jax.ShapeDtypeStruct((H, T, QK_EXT), jnp.float32), jax.ShapeDtypeStruct((H, T, QK_EXT), MXU_DTYPE),
                   jax.ShapeDtypeStruct((H, T, V_DIM), MXU_DTYPE)],
        scratch_shapes=[pltpu.VMEM((tb, QK_EXT), jnp.float32), pltpu.VMEM((tb, V_DIM), jnp.float32)],
        compiler_params=_params("parallel", "arbitrary", "arbitrary"),
    )(q, k, v, do, lse, delta)


def _mla_pre_bwd(x, dxm, vec, hb, z, dq, dk, dv, cs, wcat, g_q, g_kv, wuq, wukv):
    T = x.shape[0]
    tm = _tile(T, 256)
    H = MLA_HEADS
    zc = wcat.shape[1]

    def body(x_ref, dxm_ref, vec_ref, h_ref, z_ref, dq_ref, dk_ref, dv_ref, cs_ref, wcat_ref, gq_ref, gkv_ref,
             wuq_ref, wukv_ref, dx_ref, ps_ref, dgq_ref, dgkv_ref, dwcat_ref, dwuq_ref, dwukv_ref):
        @pl.when(pl.program_id(0) == 0)
        def _():
            for ref in (ps_ref, dgq_ref, dgkv_ref, dwcat_ref, dwuq_ref, dwukv_ref):
                ref[...] = jnp.zeros_like(ref)

        z = z_ref[...]
        cs_t = cs_ref[...]
        cqhat, rq = _rms(z[:, :Q_LORA])
        ckhat, rk = _rms(z[:, Q_LORA:Q_LORA + KV_LORA])
        gq, gkv = gq_ref[...], gkv_ref[...]
        cq = (cqhat * gq).astype(MXU_DTYPE)
        ckv = (ckhat * gkv).astype(MXU_DTYPE)
        dcq = jnp.zeros((tm, Q_LORA), jnp.float32)
        dckv = jnp.zeros((tm, KV_LORA), jnp.float32)
        dkr = jnp.zeros((tm, LANES), jnp.float32)
        for hd in range(H):
            dqh = dq_ref[hd] * MLA_SCALE
            gqh = jnp.concatenate([dqh[:, :QK_NOPE], dqh[:, QK_NOPE:] * cs_t], axis=1).astype(MXU_DTYPE)
            dcq += _dot_nt(gqh, wuq_ref[hd])
            dwuq_ref[hd] += _dot_tn(cq, gqh)
            dkh = dk_ref[hd]
            gkvh = jnp.concatenate([dkh[:, :QK_NOPE], dv_ref[hd]], axis=1)
            dckv += _dot_nt(gkvh, wukv_ref[hd])
            dwukv_ref[hd] += _dot_tn(ckv, gkvh)
            dkr += dkh[:, QK_NOPE:].astype(jnp.float32)
        dgq_ref[...] += _rowsum(dcq * cqhat)
        dgkv_ref[...] += _rowsum(dckv * ckhat)
        dcq_pre = _rms_bwd(dcq * gq, cqhat, rq)
        dckv_pre = _rms_bwd(dckv * gkv, ckhat, rk)
        dkr2 = (dkr + pltpu.roll(dkr, QK_ROPE, axis=1)) * cs_t
        dz = jnp.concatenate([dcq_pre, dckv_pre, dkr2], axis=1).astype(MXU_DTYPE)
        dwcat_ref[...] += _dot_tn(h_ref[...], dz)
        dh = _dot_nt(dz, wcat_ref[...])
        dx_ref[...] = _modulate_bwd(dh, x_ref[...], vec_ref, R_GMIX, R_SC1, R_SH1, ps_ref, dxm_ref[...])

    hblk = lambda w: pl.BlockSpec((H, tm, w), lambda i: (0, i, 0))
    return pl.pallas_call(
        body, name="mla_pre_bwd", grid=(T // tm,),
        in_specs=[_rows(tm, D), _rows(tm, D), _full((8, D)), _rows(tm, D), _rows(tm, zc), hblk(QK_EXT), hblk(QK_EXT),
                  hblk(V_DIM), _rows(tm, LANES), _full(wcat.shape), _full(g_q.shape), _full(g_kv.shape),
                  _full(wuq.shape), _full(wukv.shape)],
        out_specs=[_rows(tm, D), _full((8, D)), _full(g_q.shape), _full(g_kv.shape), _full(wcat.shape),
                   _full(wuq.shape), _full(wukv.shape)],
        out_shape=[jax.ShapeDtypeStruct((T, D), jnp.float32), jax.ShapeDtypeStruct((8, D), jnp.float32),
                   jax.ShapeDtypeStruct(g_q.shape, jnp.float32), jax.ShapeDtypeStruct(g_kv.shape, jnp.float32),
                   jax.ShapeDtypeStruct(wcat.shape, jnp.float32), jax.ShapeDtypeStruct(wuq.shape, jnp.float32),
                   jax.ShapeDtypeStruct(wukv.shape, jnp.float32)],
        compiler_params=_params("arbitrary"),
    )(x, dxm, vec, hb, z, dq, dk, dv, cs, wcat, g_q, g_kv, wuq, wukv)


def _swa_attn_bwd(qt, k, v, dot_, bias, sink_rows):
    T = qt.shape[1]
    W, Dh, G, Hk = WINDOW, SWA_HEAD_DIM, SWA_GROUP, SWA_KV_HEADS
    nk = Hk * Dh
    nb = _swa_blocks(T)

    def body(qt_ref, kp_ref, kc_ref, vp_ref, vc_ref, dot_ref, bias_ref, sink_ref, dqt_ref, dk_ref, dv_ref, dsink_ref):
        n = pl.program_id(0)

        @pl.when(n == 0)
        def _():
            dk_ref[...] = jnp.zeros_like(dk_ref)
            dv_ref[...] = jnp.zeros_like(dv_ref)
            dsink_ref[...] = jnp.zeros_like(dsink_ref)

        def add_rows(first_row, dkb_part, dvb_part):
            rows = pl.ds(pl.multiple_of(first_row, W), W)
            dk_ref[rows, :] += dkb_part
            dv_ref[rows, :] += dvb_part

        for b in range(nb):
            q_b, kp_b, kc_b = _swa_views(b, qt_ref, kp_ref, kc_ref)
            do_b, vp_b, vc_b = _swa_views(b, dot_ref, vp_ref, vc_ref)
            dks, dvs = [], []
            for kh in range(Hk):
                qt, kb, pn, p_sink = _swa_probs(True if b else n > 0, kh, q_b, kp_b, kc_b, bias_ref, sink_ref)
                vb = jnp.concatenate([vp_b[:, kh * Dh:(kh + 1) * Dh], vc_b[:, kh * Dh:(kh + 1) * Dh]], axis=0)
                dot_h = jnp.concatenate([do_b[(kh * G + g) * Dh:(kh * G + g + 1) * Dh, :] for g in range(G)], axis=1)
                dp = _dot(vb, dot_h)
                delta = jnp.sum(pn * dp, axis=0, keepdims=True)
                dsb = (pn * (dp - delta)).astype(MXU_DTYPE)
                dsink_ref[kh] += -p_sink * delta
                dqt = _dot_tn(kb, dsb) * SWA_SCALE
                for g in range(G):
                    dqt_ref[pl.ds((kh * G + g) * Dh, Dh), pl.ds(b * W, W)] = dqt[:, g * W:(g + 1) * W]
                dks.append(_dot_nt(dsb, qt))
                dvs.append(_dot_nt(pn.astype(MXU_DTYPE), dot_h))
            dkb = jnp.concatenate(dks, axis=1)
            dvb = jnp.concatenate(dvs, axis=1)
            add_rows((n * nb + b) * W, dkb[W:], dvb[W:])
            if b:
                add_rows((n * nb + b - 1) * W, dkb[:W], dvb[:W])
            else:
                @pl.when(n > 0)
                def _():
                    add_rows((n * nb - 1) * W, dkb[:W], dvb[:W])

    prev = lambda n: (jnp.maximum(n * nb - 1, 0), 0)
    cur = lambda n: (n, 0)
    col = lambda n: (0, n)
    return pl.pallas_call(
        body, name="swa_attn_bwd", grid=(T // (nb * W),),
        in_specs=[pl.BlockSpec((D, nb * W), col), pl.BlockSpec((W, nk), prev), pl.BlockSpec((nb * W, nk), cur),
                  pl.BlockSpec((W, nk), prev), pl.BlockSpec((nb * W, nk), cur), pl.BlockSpec((D, nb * W), col),
                  _full(bias.shape), _full(sink_rows.shape)],
        out_specs=[pl.BlockSpec((D, nb * W), col), _full((T, nk)), _full((T, nk)), _full(sink_rows.shape)],
        out_shape=[jax.ShapeDtypeStruct((D, T), jnp.float32), jax.ShapeDtypeStruct((T, nk), jnp.float32),
                   jax.ShapeDtypeStruct((T, nk), jnp.float32), jax.ShapeDtypeStruct(sink_rows.shape, jnp.float32)],
        compiler_params=_params("arbitrary"),
    )(qt, k, k, v, v, dot_, bias, sink_rows)


def _swa_pre_bwd(x, dxm, vec, dq, dk, dv, w_qkv):
    T = x.shape[0]
    tm = _tile(T, 512)
    nq = SWA_HEADS * SWA_HEAD_DIM
    nk = SWA_KV_HEADS * SWA_HEAD_DIM
    nqkv = nq + 2 * nk

    def body(x_ref, dxm_ref, vec_ref, dq_ref, dk_ref, dv_ref, w_ref, dx_ref, dqkv_ref, ps_ref, db_ref):
        @pl.when(pl.program_id(0) == 0)
        def _():
            ps_ref[...] = jnp.zeros_like(ps_ref)
            db_ref[...] = jnp.zeros_like(db_ref)

        dqkv = jnp.concatenate([dq_ref[...], dk_ref[...], dv_ref[...]], axis=1)
        db_ref[...] += _rowsum(dqkv)
        dqkv_b = dqkv.astype(MXU_DTYPE)
        dqkv_ref[...] = dqkv_b
        dh = _dot_nt(dqkv_b, w_ref[...])
        dx_ref[...] = _modulate_bwd(dh, x_ref[...], vec_ref, R_GMIX, R_SC1, R_SH1, ps_ref, dxm_ref[...])

    return pl.pallas_call(
        body, name="swa_pre_bwd", grid=(T // tm,),
        in_specs=[_rows(tm, D), _rows(tm, D), _full((8, D)), _rows(tm, nq), _rows(tm, nk), _rows(tm, nk),
                  _full(w_qkv.shape)],
        out_specs=[_rows(tm, D), _rows(tm, nqkv), _full((8, D)), _full((1, nqkv))],
        out_shape=[jax.ShapeDtypeStruct((T, D), jnp.float32), jax.ShapeDtypeStruct((T, nqkv), MXU_DTYPE),
                   jax.ShapeDtypeStruct((8, D), jnp.float32), jax.ShapeDtypeStruct((1, nqkv), jnp.float32)],
        compiler_params=_params("arbitrary"),
    )(x, dxm, vec, dq, dk, dv, w_qkv)


def _rot_cols(w):
    half = QK_ROPE // 2
    return jnp.concatenate([-w[..., half:], w[..., :half]], axis=-1)


def _unrot_grad(d_rope, d_rot):
    half = QK_ROPE // 2
    return d_rope + jnp.concatenate([d_rot[..., half:], -d_rot[..., :half]], axis=-1)


def _rope_table(positions):
    half = QK_ROPE // 2
    inv_freq = ROPE_THETA ** (-jnp.arange(half, dtype=jnp.float32) / half)
    ang = positions.astype(jnp.float32)[:, None] * inv_freq
    cos, sin = jnp.cos(ang), jnp.sin(ang)
    return jnp.concatenate([cos, cos, sin, sin], axis=1)


def _sequence_step(x, tgt, positions, vecs, g_q, g_kv, sinks, g_final, wts, late_weights, on_late_grads, on_late_landed):
    H = MLA_HEADS
    cs = _rope_table(positions)
    w_dkv = wts["mla_w_dkv"]
    wcat = jnp.concatenate([wts["mla_w_dq"], w_dkv, _rot_cols(w_dkv[:, KV_LORA:])], axis=1)
    uq = wts["mla_w_uq"].reshape(Q_LORA, H, QK_NOPE + QK_ROPE)
    wuq = jnp.concatenate([uq, _rot_cols(uq[..., QK_NOPE:])], axis=-1).transpose(1, 0, 2)
    wukv = wts["mla_w_ukv"].reshape(KV_LORA, H, QK_NOPE + V_DIM).transpose(1, 0, 2)
    zero_bias = jnp.zeros((1, D), jnp.float32)
    bias = _swa_bias()
    sink_rows = jnp.broadcast_to(sinks.reshape(SWA_KV_HEADS, 1, SWA_GROUP, 1),
                                 (SWA_KV_HEADS, 1, SWA_GROUP, WINDOW)).reshape(SWA_KV_HEADS, 1, SWA_GROUP * WINDOW)

    h1a, z, q, k, v = _mla_pre(x, vecs[0], wcat, g_q, g_kv, wuq, wukv, cs)
    o_a, lse = _mla_attn_fwd(q, k, v)
    y1a, xm_a, h2a = _post_attn(o_a, x, wts["mla_w_o"], zero_bias, vecs[0])
    wts = {**wts, **late_weights(h2a)}
    a_a, y2a, x1 = _mlp_fwd(h2a, wts["w_ff1"][0], wts["w_ff2"][0], xm_a, vecs[0])

    h1b, qs_t, ks, vs = _swa_pre(x1, vecs[1], wts["swa_w_qkv"], wts["swa_b_qkv"])
    o_bt = _swa_attn_fwd(qs_t, ks, vs, bias, sink_rows)
    y1b, xm_b, h2b = _post_attn(o_bt, x1, wts["swa_w_o"], wts["swa_b_o"], vecs[1], o_transposed=True)
    a_b, y2b, x2 = _mlp_fwd(h2b, wts["w_ff1"][1], wts["w_ff2"][1], xm_b, vecs[1])

    loss8, dx2, dg_final = _final_loss(x2, tgt, g_final.reshape(1, D))

    du_b, dy2b, dxm_b, ps_mlp_b = _mlp_bwd(dx2, y2b, a_b, wts["w_ff1"][1], wts["w_ff2"][1], xm_b, vecs[1])
    g_ff2 = _mm_tn(a_b, dy2b, "dw_ff2_l1", "rows", DEPTH, 1)
    g_ff1 = _mm_tn(h2b, du_b, "dw_ff1_l1", "cols", DEPTH, 1)
    dy1b, do_bt, ps_out_b = _attn_out_bwd(dxm_b, y1b, None, wts["swa_w_o"], vecs[1], False)
    g_swa_o = _mm_tn(o_bt, dy1b, "dw_o_swa", a_transposed=True)
    dqs_t, dks, dvs, dsinks = _swa_attn_bwd(qs_t, ks, vs, do_bt, bias, sink_rows)
    dqs = dqs_t.T
    dx1, dqkv, ps_pre_b, g_swa_bqkv = _swa_pre_bwd(x1, dxm_b, vecs[1], dqs, dks, dvs, wts["swa_w_qkv"])
    g_swa_qkv = _mm_tn(h1b, dqkv, "dw_qkv", "cols")

    du_a, dy2a, dxm_a, ps_mlp_a = _mlp_bwd(dx1, y2a, a_a, wts["w_ff1"][0], wts["w_ff2"][0], xm_a, vecs[0])
    g_ff2 = _mm_tn(a_a, dy2a, "dw_ff2_l0", "rows", DEPTH, 0, g_ff2)
    g_ff1 = _mm_tn(h2a, du_a, "dw_ff1_l0", "cols", DEPTH, 0, g_ff1)
    rows4 = lambda g: g.reshape(N_CHIPS, g.shape[0] // N_CHIPS, g.shape[1])
    token = on_late_grads({
        "swa_w_qkv": g_swa_qkv.reshape(N_CHIPS, D, -1), "swa_w_o": rows4(g_swa_o),
        "w_ff1": g_ff1.reshape(N_CHIPS, DEPTH * D, -1), "w_ff2": g_ff2.reshape(N_CHIPS, -1, D)})
    dy1a, do_a, ps_out_a, delta = _attn_out_bwd(dxm_a, y1a, o_a, wts["mla_w_o"], vecs[0] + token[0, 0], True)
    g_mla_o = _mm_tn(o_a, dy1a, "dw_o_mla")
    token = on_late_landed(g_mla_o)
    dq, dk, dv = _mla_attn_bwd(q, k, v, do_a, lse, delta + token[0, 0])
    dx0, ps_pre_a, dg_q, dg_kv, dwcat, dwuq, dwukv = _mla_pre_bwd(
        x, dxm_a, vecs[0], h1a, z, dq, dk, dv, cs, wcat, g_q, g_kv, wuq, wukv)

    c0, c1, c2 = Q_LORA, Q_LORA + KV_LORA, Q_LORA + KV_LORA + QK_ROPE
    g_dq = dwcat[:, :c0]
    g_dkv = jnp.concatenate([dwcat[:, c0:c1], _unrot_grad(dwcat[:, c1:c2], dwcat[:, c2:])], axis=1)
    e0 = QK_NOPE + QK_ROPE
    g_uq = jnp.concatenate([dwuq[..., :QK_NOPE], _unrot_grad(dwuq[..., QK_NOPE:e0], dwuq[..., e0:])], axis=-1)
    per = H // N_CHIPS
    g_uq = g_uq.reshape(N_CHIPS, per, Q_LORA, e0).transpose(0, 2, 1, 3).reshape(N_CHIPS, Q_LORA, per * e0)
    g_ukv = dwukv.reshape(N_CHIPS, per, KV_LORA, QK_NOPE + V_DIM).transpose(0, 2, 1, 3)
    g_ukv = g_ukv.reshape(N_CHIPS, KV_LORA, per * (QK_NOPE + V_DIM))

    def dmod(ps_pre, ps_out, ps_mlp):
        return jnp.concatenate([ps_pre[R_SH1:R_SC1 + 1], ps_out[R_GT1:R_GT1 + 1], ps_mlp[R_SH2:R_GT2 + 1]], axis=0)

    grads = {"mla_w_dq": rows4(g_dq), "mla_w_uq": g_uq, "mla_w_dkv": rows4(g_dkv), "mla_w_ukv": g_ukv,
             "mla_w_o": rows4(g_mla_o)}
    small = {
        "dmod": jnp.stack([dmod(ps_pre_a, ps_out_a, ps_mlp_a), dmod(ps_pre_b, ps_out_b, ps_mlp_b)]).reshape(DEPTH, 6 * D),
        "g_mix": jnp.stack([ps_pre_a[R_GMIX], ps_pre_b[R_GMIX]]),
        "g_mlp": jnp.stack([ps_mlp_a[R_GMLP], ps_mlp_b[R_GMLP]]),
        "mla_g_q": dg_q, "mla_g_kv": dg_kv, "swa_sinks": jnp.sum(dsinks.reshape(SWA_HEADS, WINDOW), axis=1).reshape(1, SWA_HEADS),
        "swa_b_qkv": g_swa_bqkv, "swa_b_o": ps_out_b[R_BO:R_BO + 1],
        "g_final": dg_final.reshape(D), "loss": loss8[0, 0],
    }
    return dx0, grads, small


SHARDED = {
    "mla_w_dq": (1, D // N_CHIPS, Q_LORA),
    "mla_w_uq": (1, Q_LORA, MLA_HEADS * (QK_NOPE + QK_ROPE) // N_CHIPS),
    "mla_w_dkv": (1, D // N_CHIPS, KV_LORA + QK_ROPE),
    "mla_w_ukv": (1, KV_LORA, MLA_HEADS * (QK_NOPE + V_DIM) // N_CHIPS),
    "mla_w_o": (1, MLA_HEADS * V_DIM // N_CHIPS, D),
    "swa_w_qkv": (1, D, (SWA_HEADS + 2 * SWA_KV_HEADS) * SWA_HEAD_DIM // N_CHIPS),
    "swa_w_o": (1, SWA_HEADS * SWA_HEAD_DIM // N_CHIPS, D),
    "w_ff1": (DEPTH, D, D_FF // N_CHIPS),
    "w_ff2": (DEPTH, D_FF // N_CHIPS, D),
}
COL_SPLIT = ("mla_w_uq", "mla_w_ukv", "swa_w_qkv")
BIASES = {"swa_b_qkv": (SWA_HEADS + 2 * SWA_KV_HEADS) * SWA_HEAD_DIM, "swa_b_o": D}


def _view2d(name):
    shape = SHARDED[name]
    return math.prod(shape[:-1]), shape[-1]


SMALL = {"b_ada": (DEPTH, 6 * D), "g_mix": (DEPTH, D), "g_mlp": (DEPTH, D), "mla_g_q": (1, Q_LORA),
         "mla_g_kv": (1, KV_LORA), "swa_sinks": (1, SWA_HEADS), "g_final": (D,), "loss": (),
         "swa_b_qkv": (1, BIASES["swa_b_qkv"]), "swa_b_o": (1, BIASES["swa_b_o"])}
SMALL_ROWS = 192
DMA_ROWS = 256


SLOT_ROWS = 8


def _small_slots():
    slots, off = {}, 0
    for name, shape in SMALL.items():
        n = max(math.prod(shape), 1)
        slots[name] = (off, n)
        off += -(-n // (SLOT_ROWS * LANES)) * SLOT_ROWS * LANES
    assert off <= SMALL_ROWS * LANES
    return slots


def _pack_small(vals):
    parts, end = [], 0
    for name, (off, n) in _small_slots().items():
        pad = -(-n // (SLOT_ROWS * LANES)) * SLOT_ROWS * LANES - n
        v = vals[name].astype(jnp.float32).reshape(-1) if name in vals else jnp.zeros((n,), jnp.float32)
        parts += [v, jnp.zeros((pad,), jnp.float32)]
        end = off + n + pad
    parts.append(jnp.zeros((SMALL_ROWS * LANES - end,), jnp.float32))
    return jnp.concatenate(parts).reshape(SMALL_ROWS, LANES)


def _from_slot(name, rows):
    n = max(math.prod(SMALL[name]), 1)
    return rows.reshape(-1)[:n].reshape(SMALL[name])


def _pieces(rows):
    return [(off, min(DMA_ROWS, rows - off)) for off in range(0, rows, DMA_ROWS)]


HBM = pl.BlockSpec(memory_space=pltpu.HBM)
MESH = pl.DeviceIdType.MESH


def _place():
    x, y, c = lax.axis_index("x"), lax.axis_index("y"), lax.axis_index("c")
    chips = [(1 - x, y), (x, 1 - y), (1 - x, 1 - y)]
    return x, y, c, chips


def _all_gather(block):
    m_per, n = block.shape

    def body(x_ref, out_ref, send_sems, recv_sems, local_sem):
        x, y, c, chips = _place()
        me, sibling = (x, y, c), (x, y, 1 - c)

        def rows(px, py, pc):
            return out_ref.at[pl.ds((4 * px + 2 * py + pc) * m_per, m_per), :]

        def copy(k, blk, to, src=None):
            return pltpu.make_async_remote_copy(
                src_ref=rows(*blk) if src is None else src, dst_ref=rows(*blk),
                send_sem=send_sems.at[k], recv_sem=recv_sems.at[k], device_id=to, device_id_type=MESH)

        mine = pltpu.make_async_copy(x_ref, rows(*me), local_sem)
        mine.start()
        first = [copy(0, me, sibling, src=x_ref)]
        first += [copy(1 + j, me, (*chip, c), src=x_ref) for j, chip in enumerate(chips)]
        for cp in first:
            cp.start()
        passed = [copy(4 + j, (*chip, c), sibling) for j, chip in enumerate(chips)]
        for j, chip in enumerate(chips):
            copy(1 + j, (*chip, c), me).wait_recv()
            passed[j].start()
        copy(0, sibling, me).wait_recv()
        for j, chip in enumerate(chips):
            copy(4 + j, (*chip, 1 - c), me).wait_recv()
        for cp in first + passed:
            cp.wait_send()
        mine.wait()

    out = pl.pallas_call(
        body, name="all_gather_small",
        out_shape=jax.ShapeDtypeStruct((N_DEV * m_per, n), block.dtype),
        in_specs=[pl.BlockSpec(memory_space=pltpu.VMEM)],
        out_specs=pl.BlockSpec(memory_space=pltpu.VMEM),
        scratch_shapes=[pltpu.SemaphoreType.DMA((7,)), pltpu.SemaphoreType.DMA((7,)), pltpu.SemaphoreType.DMA],
    )(block)
    return out.reshape(N_DEV, m_per, n)


def _weight_gather(shards):
    nt = len(shards)

    def body(*refs):
        w_refs, out_refs = refs[:nt], refs[nt:2 * nt]
        send_sems, recv_sems = refs[2 * nt:]
        x, y, c, chips = _place()
        sibling = (x, y, 1 - c)

        def slab(t, px, py, half):
            rh = shards[t].shape[0] // 2
            return out_refs[t].at[2 * px + py, pl.ds(half * rh, rh), :]

        def copy(t, k, src, dst, to):
            return pltpu.make_async_remote_copy(src_ref=src, dst_ref=dst, send_sem=send_sems.at[6 * t + k],
                                                recv_sem=recv_sems.at[6 * t + k], device_id=to, device_id_type=MESH)

        first = []
        for t in range(nt):
            rh = shards[t].shape[0] // 2
            first += [copy(t, j, w_refs[t].at[pl.ds(c * rh, rh), :], slab(t, x, y, c), (*chip, c))
                      for j, chip in enumerate(chips)]
        for cp in first:
            cp.start()
        passed = []
        for t in range(nt):
            for j, chip in enumerate(chips):
                copy(t, j, slab(t, *chip, c), slab(t, *chip, c), (*chip, c)).wait_recv()
                rh = shards[t].shape[0] // 2
                for off, n in _pieces(rh):
                    piece = out_refs[t].at[2 * chip[0] + chip[1], pl.ds(c * rh + off, n), :]
                    copy(t, 3 + j, piece, piece, sibling).start()
                passed.append(copy(t, 3 + j, slab(t, *chip, c), slab(t, *chip, c), sibling))
        for t in range(nt):
            for j, chip in enumerate(chips):
                copy(t, 3 + j, slab(t, *chip, 1 - c), slab(t, *chip, 1 - c), sibling).wait_recv()
        for cp in first + passed:
            cp.wait_send()

    return pl.pallas_call(
        body, name="weight_gather",
        out_shape=[jax.ShapeDtypeStruct((N_CHIPS,) + s.shape, s.dtype) for s in shards],
        in_specs=[HBM] * nt, out_specs=[HBM] * nt,
        scratch_shapes=[pltpu.SemaphoreType.DMA((6 * nt,)), pltpu.SemaphoreType.DMA((6 * nt,))],
    )(*shards)


SEM = pl.BlockSpec(memory_space=pltpu.SEMAPHORE)
ANY = pl.BlockSpec(memory_space=pl.ANY)
SPLIT_COPY = pltpu.SideEffectType.DATAFLOW_SIDE_EFFECTING


def _late_copies(w_refs, land_refs, send_sems, recv_sems):
    x, y, c, chips = _place()
    return [pltpu.make_async_remote_copy(
        src_ref=w_refs[t], dst_ref=land_refs[t].at[2 * x + y], send_sem=send_sems.at[3 * t + j],
        recv_sem=recv_sems.at[3 * t + j], device_id=(cx, cy, c), device_id_type=MESH)
        for t in range(len(w_refs)) for j, (cx, cy) in enumerate(chips)], chips


def _late_gather_start(shards, after):
    nt, na = len(shards), len(after)

    def body(*refs):
        w_refs, land_refs = refs[:nt], refs[nt:2 * nt]
        send_sems, recv_sems, token = refs[2 * nt + na], refs[2 * nt + na + 1], refs[-1]
        copies, _ = _late_copies(w_refs, land_refs, send_sems, recv_sems)
        for cp in copies:
            cp.start()
        token[...] = jnp.zeros_like(token)

    hbm = lambda a: pltpu.with_memory_space_constraint(a, pltpu.HBM)
    lands = [lax.empty((N_CHIPS,) + s.shape, s.dtype) for s in shards]
    outs = pl.pallas_call(
        body, name="late_gather_start",
        out_shape=(pltpu.SemaphoreType.DMA((3 * nt,)), pltpu.SemaphoreType.DMA((3 * nt,)),
                   *[pltpu.HBM(s.shape, s.dtype) for s in shards], *[pltpu.HBM(l.shape, l.dtype) for l in lands],
                   jax.ShapeDtypeStruct((8, LANES), jnp.float32)),
        in_specs=[HBM] * (2 * nt) + [ANY] * na,
        out_specs=(SEM, SEM, *([HBM] * (2 * nt)), pl.BlockSpec(memory_space=pltpu.VMEM)),
        input_output_aliases={i: 2 + i for i in range(2 * nt)},
        compiler_params=pltpu.CompilerParams(has_side_effects=SPLIT_COPY),
    )(*[hbm(s) for s in shards], *[hbm(l) for l in lands], *after)
    return outs[0], outs[1], list(outs[2:2 + nt]), list(outs[2 + nt:2 + 2 * nt]), outs[-1]


def _late_gather_wait(send_sems, recv_sems, shards, lands, after):
    nt = len(shards)

    def body(*refs):
        w_refs, land_refs = refs[:nt], refs[nt:2 * nt]
        s_sems, r_sems = refs[2 * nt], refs[2 * nt + 1]
        x, y, c, chips = _place()
        for t in range(nt):
            for j, (cx, cy) in enumerate(chips):
                cp = pltpu.make_async_remote_copy(
                    src_ref=w_refs[t], dst_ref=land_refs[t].at[2 * cx + cy], send_sem=s_sems.at[3 * t + j],
                    recv_sem=r_sems.at[3 * t + j], device_id=(cx, cy, c), device_id_type=MESH)
                cp.wait_send()
                cp.wait_recv()

    outs = pl.pallas_call(
        body, name="late_gather_wait",
        out_shape=(*[pltpu.HBM(s.shape, s.dtype) for s in shards], *[pltpu.HBM(l.shape, l.dtype) for l in lands]),
        in_specs=[HBM] * (2 * nt) + [SEM, SEM, ANY], out_specs=tuple([HBM] * (2 * nt)),
        input_output_aliases={i: i for i in range(2 * nt)},
        compiler_params=pltpu.CompilerParams(has_side_effects=SPLIT_COPY),
    )(*shards, *lands, send_sems, recv_sems, after)
    return list(outs[nt:])


def _grad_pair_in(grads):
    nt = len(grads)

    def body(*refs):
        g_refs, got_refs = refs[:nt], refs[nt:2 * nt]
        send_sems, recv_sems = refs[2 * nt:]
        x, y, c, _ = _place()
        sibling = (x, y, 1 - c)

        def copy(t, src, dst):
            return pltpu.make_async_remote_copy(src_ref=src, dst_ref=dst, send_sem=send_sems.at[t],
                                                recv_sem=recv_sems.at[t], device_id=sibling, device_id_type=MESH)

        for t in range(nt):
            rh = grads[t].shape[1] // 2
            for p in range(N_CHIPS):
                for off, n in _pieces(rh):
                    copy(t, g_refs[t].at[p, pl.ds((1 - c) * rh + off, n), :], got_refs[t].at[p, pl.ds(off, n), :]).start()
        for t in range(nt):
            rh = grads[t].shape[1] // 2
            copy(t, g_refs[t].at[:, pl.ds((1 - c) * rh, rh), :], got_refs[t]).wait()

    return pl.pallas_call(
        body, name="grad_pair_in",
        out_shape=[jax.ShapeDtypeStruct((N_CHIPS, g.shape[1] // 2, g.shape[2]), g.dtype) for g in grads],
        in_specs=[HBM] * nt, out_specs=[HBM] * nt,
        scratch_shapes=[pltpu.SemaphoreType.DMA((nt,)), pltpu.SemaphoreType.DMA((nt,))],
    )(*grads)


def _pair_in_start(grads):
    nt = len(grads)

    def body(*refs):
        g_refs, land_refs = refs[:nt], refs[nt:2 * nt]
        send_sems, recv_sems, token = refs[2 * nt], refs[2 * nt + 1], refs[-1]
        x, y, c, _ = _place()
        for t in range(nt):
            rh = grads[t].shape[1] // 2
            for p in range(N_CHIPS):
                for off, n in _pieces(rh):
                    pltpu.make_async_remote_copy(
                        src_ref=g_refs[t].at[p, pl.ds((1 - c) * rh + off, n), :], dst_ref=land_refs[t].at[p, pl.ds(off, n), :],
                        send_sem=send_sems.at[t], recv_sem=recv_sems.at[t], device_id=(x, y, 1 - c),
                        device_id_type=MESH).start()
        token[...] = jnp.zeros_like(token)

    hbm = lambda a: pltpu.with_memory_space_constraint(a, pltpu.HBM)
    lands = [lax.empty((N_CHIPS, g.shape[1] // 2, g.shape[2]), g.dtype) for g in grads]
    outs = pl.pallas_call(
        body, name="grad_pair_in_start",
        out_shape=(pltpu.SemaphoreType.DMA((nt,)), pltpu.SemaphoreType.DMA((nt,)),
                   *[pltpu.HBM(g.shape, g.dtype) for g in grads], *[pltpu.HBM(l.shape, l.dtype) for l in lands],
                   jax.ShapeDtypeStruct((8, LANES), jnp.float32)),
        in_specs=[HBM] * (2 * nt),
        out_specs=(SEM, SEM, *([HBM] * (2 * nt)), pl.BlockSpec(memory_space=pltpu.VMEM)),
        input_output_aliases={i: 2 + i for i in range(2 * nt)},
        compiler_params=pltpu.CompilerParams(has_side_effects=SPLIT_COPY),
    )(*[hbm(g) for g in grads], *[hbm(l) for l in lands])
    return outs[0], outs[1], list(outs[2:2 + nt]), list(outs[2 + nt:2 + 2 * nt]), outs[-1]


def _pair_in_wait(send_sems, recv_sems, grads, lands, after):
    nt = len(grads)

    def body(*refs):
        g_refs, land_refs = refs[:nt], refs[nt:2 * nt]
        s_sems, r_sems = refs[2 * nt], refs[2 * nt + 1]
        x, y, c, _ = _place()
        for t in range(nt):
            rh = grads[t].shape[1] // 2
            cp = pltpu.make_async_remote_copy(
                src_ref=g_refs[t].at[:, pl.ds((1 - c) * rh, rh), :], dst_ref=land_refs[t], send_sem=s_sems.at[t],
                recv_sem=r_sems.at[t], device_id=(x, y, 1 - c), device_id_type=MESH)
            cp.wait_send()
            cp.wait_recv()

    outs = pl.pallas_call(
        body, name="grad_pair_in_wait",
        out_shape=(*[pltpu.HBM(g.shape, g.dtype) for g in grads], *[pltpu.HBM(l.shape, l.dtype) for l in lands]),
        in_specs=[HBM] * (2 * nt) + [SEM, SEM, ANY], out_specs=tuple([HBM] * (2 * nt)),
        input_output_aliases={i: i for i in range(2 * nt)},
        compiler_params=pltpu.CompilerParams(has_side_effects=SPLIT_COPY),
    )(*grads, *lands, send_sems, recv_sems, after)
    return list(outs[:nt]), list(outs[nt:])


def _pair_sum(g, got, core, name):
    _, rows, cols = g.shape
    rh = rows // 2
    tr = _tile(rh, 512)
    nb = rh // tr

    def body(c_ref, g_ref, got_ref, s32_ref, s16_ref):
        s = g_ref[...] + got_ref[...]
        s32_ref[...] = s
        s16_ref[...] = s.astype(s16_ref.dtype)

    blk = pl.BlockSpec((None, tr, cols), lambda p, i, c_ref: (p, i, 0))
    return pl.pallas_call(
        body, name=name,
        grid_spec=pltpu.PrefetchScalarGridSpec(
            num_scalar_prefetch=1, grid=(N_CHIPS, nb),
            in_specs=[pl.BlockSpec((None, tr, cols), lambda p, i, c_ref: (p, c_ref[0] * nb + i, 0)), blk],
            out_specs=[blk, blk]),
        out_shape=[jax.ShapeDtypeStruct((N_CHIPS, rh, cols), jnp.float32),
                   jax.ShapeDtypeStruct((N_CHIPS, rh, cols), jnp.bfloat16)],
        compiler_params=_params("parallel", "parallel"),
    )(core, g, got)


def _exchange_start(parts, name):
    nt = len(parts)

    def body(*refs):
        a_refs, land_refs = refs[:nt], refs[nt:2 * nt]
        send_sems, recv_sems, token = refs[2 * nt], refs[2 * nt + 1], refs[-1]
        x, y, c, chips = _place()
        for t in range(nt):
            for j, (cx, cy) in enumerate(chips):
                pltpu.make_async_remote_copy(
                    src_ref=a_refs[t].at[2 * cx + cy], dst_ref=land_refs[t].at[j], send_sem=send_sems.at[3 * t + j],
                    recv_sem=recv_sems.at[3 * t + j], device_id=(cx, cy, c), device_id_type=MESH).start()
        token[...] = jnp.zeros_like(token)

    hbm = lambda a: pltpu.with_memory_space_constraint(a, pltpu.HBM)
    lands = [lax.empty((N_CHIPS - 1,) + a.shape[1:], a.dtype) for a in parts]
    outs = pl.pallas_call(
        body, name=name,
        out_shape=(pltpu.SemaphoreType.DMA((3 * nt,)), pltpu.SemaphoreType.DMA((3 * nt,)),
                   *[pltpu.HBM(a.shape, a.dtype) for a in parts], *[pltpu.HBM(l.shape, l.dtype) for l in lands],
                   jax.ShapeDtypeStruct((8, LANES), jnp.float32)),
        in_specs=[HBM] * (2 * nt),
        out_specs=(SEM, SEM, *([HBM] * (2 * nt)), pl.BlockSpec(memory_space=pltpu.VMEM)),
        input_output_aliases={i: 2 + i for i in range(2 * nt)},
        compiler_params=pltpu.CompilerParams(has_side_effects=SPLIT_COPY),
    )(*[hbm(a) for a in parts], *[hbm(l) for l in lands])
    return outs[0], outs[1], list(outs[2:2 + nt]), list(outs[2 + nt:2 + 2 * nt]), outs[-1]


def _exchange_wait(send_sems, recv_sems, parts, lands, after, name):
    nt = len(parts)

    def body(*refs):
        a_refs, land_refs = refs[:nt], refs[nt:2 * nt]
        s_sems, r_sems = refs[2 * nt], refs[2 * nt + 1]
        x, y, c, chips = _place()
        for t in range(nt):
            for j, (cx, cy) in enumerate(chips):
                cp = pltpu.make_async_remote_copy(
                    src_ref=a_refs[t].at[2 * cx + cy], dst_ref=land_refs[t].at[j], send_sem=s_sems.at[3 * t + j],
                    recv_sem=r_sems.at[3 * t + j], device_id=(cx, cy, c), device_id_type=MESH)
                cp.wait_send()
                cp.wait_recv()

    outs = pl.pallas_call(
        body, name=name,
        out_shape=(*[pltpu.HBM(a.shape, a.dtype) for a in parts], *[pltpu.HBM(l.shape, l.dtype) for l in lands]),
        in_specs=[HBM] * (2 * nt) + [SEM, SEM, ANY], out_specs=tuple([HBM] * (2 * nt)),
        input_output_aliases={i: i for i in range(2 * nt)},
        compiler_params=pltpu.CompilerParams(has_side_effects=SPLIT_COPY),
    )(*parts, *lands, send_sems, recv_sems, after)
    return list(outs[nt:])


def _chip_sum(s32, got, chip, name, behind=None):
    _, rh, cols = s32.shape
    tr = _tile(rh, 512)

    def body(p_ref, s_ref, got_ref, *refs):
        acc = s_ref[...]
        for j in range(N_CHIPS - 1):
            acc = acc + got_ref[j].astype(jnp.float32)
        refs[-1][...] = acc

    extra = [] if behind is None else [behind]
    return pl.pallas_call(
        body, name=name,
        grid_spec=pltpu.PrefetchScalarGridSpec(
            num_scalar_prefetch=1, grid=(rh // tr,),
            in_specs=[pl.BlockSpec((None, tr, cols), lambda i, p_ref: (p_ref[0], i, 0)),
                      pl.BlockSpec((N_CHIPS - 1, tr, cols), lambda i, p_ref: (0, i, 0))]
            + [pl.BlockSpec((8, LANES), lambda i, p_ref: (0, 0))] * len(extra),
            out_specs=pl.BlockSpec((tr, cols), lambda i, p_ref: (i, 0))),
        out_shape=jax.ShapeDtypeStruct((rh, cols), jnp.float32),
        compiler_params=_params("parallel"),
    )(chip, s32, got, *extra)


def _grad_pair_out(halves):
    nt = len(halves)

    def body(*refs):
        h_refs, got_refs = refs[:nt], refs[nt:2 * nt]
        send_sems, recv_sems = refs[2 * nt:]
        x, y, c, _ = _place()
        sibling = (x, y, 1 - c)

        def copy(t, src, dst):
            return pltpu.make_async_remote_copy(src_ref=src, dst_ref=dst, send_sem=send_sems.at[t],
                                                recv_sem=recv_sems.at[t], device_id=sibling, device_id_type=MESH)

        for t in range(nt):
            for off, n in _pieces(halves[t].shape[0]):
                copy(t, h_refs[t].at[pl.ds(off, n), :], got_refs[t].at[pl.ds(off, n), :]).start()
        for t in range(nt):
            copy(t, h_refs[t], got_refs[t]).wait()

    return pl.pallas_call(
        body, name="grad_pair_out",
        out_shape=[jax.ShapeDtypeStruct(h.shape, h.dtype) for h in halves],
        in_specs=[HBM] * nt, out_specs=[HBM] * nt,
        scratch_shapes=[pltpu.SemaphoreType.DMA((nt,)), pltpu.SemaphoreType.DMA((nt,))],
    )(*halves)


def _ada_part(c_all, w_ada):
    L, _, ncol = w_ada.shape
    tn = _tile(ncol, 512)

    def body(c_ref, w_ref, cond_ref, part_ref):
        cv = c_ref[...]
        cond = cv * jax.nn.sigmoid(cv)
        cond_ref[...] = cond
        part_ref[0] = jnp.dot(cond, w_ref[0], precision=lax.Precision.HIGHEST, preferred_element_type=jnp.float32)

    return pl.pallas_call(
        body, name="ada_part", grid=(L, ncol // tn),
        in_specs=[_full((N_DEV, D)), pl.BlockSpec((1, D, tn), lambda l, j: (l, 0, j))],
        out_specs=[_full((N_DEV, D)), pl.BlockSpec((1, N_DEV, tn), lambda l, j: (l, 0, j))],
        out_shape=[jax.ShapeDtypeStruct((N_DEV, D), jnp.float32), jax.ShapeDtypeStruct((L, N_DEV, ncol), jnp.float32)],
        compiler_params=_params("arbitrary", "arbitrary"),
    )(c_all, w_ada)


def _adamw_math(w, g, m, v):
    m = ADAM_B1 * m + (1.0 - ADAM_B1) * g
    v = ADAM_B2 * v + (1.0 - ADAM_B2) * jnp.square(g)
    m_hat = m / (1.0 - ADAM_B1 ** ADAM_STEP)
    v_hat = v / (1.0 - ADAM_B2 ** ADAM_STEP)
    delta = -ADAM_LR * (m_hat / (jnp.sqrt(v_hat) + ADAM_EPS) + ADAM_WD * w)
    return delta, m, v


def _adamw(w, g, m, v, name):
    shape = w.shape
    cols = shape[-1]
    rows = math.prod(shape[:-1])
    tr = _tile(rows, 512)
    two_d = lambda t: t.reshape(rows, cols)

    def body(w_ref, g_ref, m_ref, v_ref, d_ref, mo_ref, vo_ref):
        d_ref[...], mo_ref[...], vo_ref[...] = _adamw_math(w_ref[...], g_ref[...], m_ref[...], v_ref[...])

    out = jax.ShapeDtypeStruct((rows, cols), jnp.float32)
    outs = pl.pallas_call(
        body, name=name, grid=(rows // tr,), in_specs=[_rows(tr, cols)] * 4, out_specs=[_rows(tr, cols)] * 3,
        out_shape=[out, out, out], compiler_params=_params("parallel"),
    )(two_d(w), two_d(g), two_d(m), two_d(v))
    return [t.reshape(shape) for t in outs]


def _adamw_halves(w, mine, got, m, v, core, name):
    shape = w.shape
    cols = shape[-1]
    rows = math.prod(shape[:-1])
    rh = rows // 2
    tr = _tile(rh, 512)
    nbh = rh // tr
    two_d = lambda t: t.reshape(rows, cols)

    def body(c_ref, w_ref, a_ref, b_ref, m_ref, v_ref, g_ref, d_ref, mo_ref, vo_ref):
        g = jnp.where(pl.program_id(0) // nbh == c_ref[0], a_ref[...], b_ref[...])
        g_ref[...] = g
        d_ref[...], mo_ref[...], vo_ref[...] = _adamw_math(w_ref[...], g, m_ref[...], v_ref[...])

    row = pl.BlockSpec((tr, cols), lambda i, c_ref: (i, 0))

    def half(keep):
        return pl.BlockSpec((tr, cols), lambda i, c_ref: (jnp.where((i // nbh == c_ref[0]) == keep, i % nbh, 0), 0))

    out = jax.ShapeDtypeStruct((rows, cols), jnp.float32)
    outs = pl.pallas_call(
        body, name=name,
        grid_spec=pltpu.PrefetchScalarGridSpec(
            num_scalar_prefetch=1, grid=(rows // tr,),
            in_specs=[row, half(True), half(False), row, row], out_specs=[row] * 4),
        out_shape=[out] * 4, compiler_params=_params("arbitrary"),
    )(core, two_d(w), mine, got, two_d(m), two_d(v))
    return [t.reshape(shape) for t in outs]


def _ada_grad_adamw(cond_t, dm, w, m, v):
    L, _, ncol = w.shape
    tn = _tile(ncol, 512)

    def body(ct_ref, dm_ref, w_ref, m_ref, v_ref, g_ref, d_ref, mo_ref, vo_ref):
        g = ct_ref[:, 0:1] * dm_ref[0, 0:1, :]
        for b in range(1, N_DEV):
            g = g + ct_ref[:, b:b + 1] * dm_ref[0, b:b + 1, :]
        g_ref[0] = g
        d_ref[0], mo_ref[0], vo_ref[0] = _adamw_math(w_ref[0], g, m_ref[0], v_ref[0])

    wblk = pl.BlockSpec((1, D, tn), lambda l, j: (l, 0, j))
    out = jax.ShapeDtypeStruct(w.shape, jnp.float32)
    return pl.pallas_call(
        body, name="ada_grad_adamw", grid=(L, ncol // tn),
        in_specs=[_full((D, N_DEV)), pl.BlockSpec((1, N_DEV, tn), lambda l, j: (l, 0, j)), wblk, wblk, wblk],
        out_specs=[wblk] * 4, out_shape=[out] * 4, compiler_params=_params("parallel", "parallel"),
    )(cond_t, dm, w, m, v)


def _small_adamw(gathered, w, m, v):
    slots = _small_slots()
    rows = {name: (off // LANES, -(-n // LANES)) for name, (off, n) in slots.items()}
    kinds = {name: 1 if name == "loss" or name in BIASES else 4 for name in slots}

    def body(ga_ref, w_ref, m_ref, v_ref, *out_refs):
        g = ga_ref[0]
        for dev in range(1, N_DEV):
            g = g + ga_ref[dev]
        d, mo, vo = _adamw_math(w_ref[...], g, m_ref[...], v_ref[...])
        k = 0
        for name, (r0, nr) in rows.items():
            for src in (g, d, mo, vo)[:kinds[name]]:
                out_refs[k][...] = src[r0:r0 + nr, :]
                k += 1

    out_shape = [jax.ShapeDtypeStruct((rows[name][1], LANES), jnp.float32) for name in slots for _ in range(kinds[name])]
    flat = pl.pallas_call(
        body, name="small_adamw", out_shape=out_shape,
        in_specs=[pl.BlockSpec(memory_space=pltpu.VMEM)] * 4,
        out_specs=[pl.BlockSpec(memory_space=pltpu.VMEM)] * len(out_shape),
    )(gathered, w, m, v)
    out, k = {}, 0
    for name in slots:
        out[name] = [_from_slot(name, t) for t in flat[k:k + kinds[name]]]
        k += kinds[name]
    return out


def _one_hot_pick(arr, index, axis):
    n = arr.shape[axis]
    shape = [1] * arr.ndim
    shape[axis] = n
    hot = (jnp.arange(n) == index).astype(arr.dtype).reshape(shape)
    return jnp.sum(arr * hot, axis=axis)


def kernel(x, c, positions, w_ada, b_ada, g_mix, g_mlp, mla_w_dq, mla_g_q, mla_w_uq, mla_w_dkv, mla_g_kv, mla_w_ukv, mla_w_o, swa_w_qkv, swa_b_qkv, swa_sinks, swa_w_o, swa_b_o, w_ff1, w_ff2, g_final, loss_target, m_w_ada, m_b_ada, m_g_mix, m_g_mlp, m_mla_w_dq, m_mla_g_q, m_mla_w_uq, m_mla_w_dkv, m_mla_g_kv, m_mla_w_ukv, m_mla_w_o, m_swa_w_qkv, m_swa_b_qkv, m_swa_sinks, m_swa_w_o, m_swa_b_o, m_w_ff1, m_w_ff2, m_g_final, v_w_ada, v_b_ada, v_g_mix, v_g_mlp, v_mla_w_dq, v_mla_g_q, v_mla_w_uq, v_mla_w_dkv, v_mla_g_kv, v_mla_w_ukv, v_mla_w_o, v_swa_w_qkv, v_swa_b_qkv, v_swa_sinks, v_swa_w_o, v_swa_b_o, v_w_ff1, v_w_ff2, v_g_final):
    W = dict(w_ada=w_ada, b_ada=b_ada, g_mix=g_mix, g_mlp=g_mlp, mla_w_dq=mla_w_dq, mla_g_q=mla_g_q, mla_w_uq=mla_w_uq,
             mla_w_dkv=mla_w_dkv, mla_g_kv=mla_g_kv, mla_w_ukv=mla_w_ukv, mla_w_o=mla_w_o, swa_w_qkv=swa_w_qkv,
             swa_b_qkv=swa_b_qkv, swa_sinks=swa_sinks, swa_w_o=swa_w_o, swa_b_o=swa_b_o, w_ff1=w_ff1, w_ff2=w_ff2,
             g_final=g_final)
    M = dict(w_ada=m_w_ada, b_ada=m_b_ada, g_mix=m_g_mix, g_mlp=m_g_mlp, mla_w_dq=m_mla_w_dq, mla_g_q=m_mla_g_q,
             mla_w_uq=m_mla_w_uq, mla_w_dkv=m_mla_w_dkv, mla_g_kv=m_mla_g_kv, mla_w_ukv=m_mla_w_ukv, mla_w_o=m_mla_w_o,
             swa_w_qkv=m_swa_w_qkv, swa_b_qkv=m_swa_b_qkv, swa_sinks=m_swa_sinks, swa_w_o=m_swa_w_o, swa_b_o=m_swa_b_o,
             w_ff1=m_w_ff1, w_ff2=m_w_ff2, g_final=m_g_final)
    V = dict(w_ada=v_w_ada, b_ada=v_b_ada, g_mix=v_g_mix, g_mlp=v_g_mlp, mla_w_dq=v_mla_w_dq, mla_g_q=v_mla_g_q,
             mla_w_uq=v_mla_w_uq, mla_w_dkv=v_mla_w_dkv, mla_g_kv=v_mla_g_kv, mla_w_ukv=v_mla_w_ukv, mla_w_o=v_mla_w_o,
             swa_w_qkv=v_swa_w_qkv, swa_b_qkv=v_swa_b_qkv, swa_sinks=v_swa_sinks, swa_w_o=v_swa_w_o, swa_b_o=v_swa_b_o,
             w_ff1=v_w_ff1, w_ff2=v_w_ff2, g_final=v_g_final)
    order = list(W)
    names = list(SHARDED)
    core = lax.axis_index("c")
    chip = 2 * lax.axis_index("x") + lax.axis_index("y")
    dev = 2 * chip + core
    core_arr = core.astype(jnp.int32).reshape(1)
    chip_arr = chip.astype(jnp.int32).reshape(1)

    def whole(n, g, own):
        g = lax.dynamic_update_slice(g, own[None], (chip, 0, 0))
        if n in ("w_ff1", "w_ff2"):
            return g
        if n in COL_SPLIT:
            return g.transpose(1, 0, 2).reshape(g.shape[1], N_CHIPS * g.shape[2])
        return g.reshape(N_CHIPS * g.shape[1], g.shape[2])

    early = [n for n in names if n.startswith("mla_")]
    local = {n: W[n].astype(MXU_DTYPE).reshape(_view2d(n)) for n in early}
    wts = {n: whole(n, g, local[n]) for n, g in zip(early, _weight_gather([local[n] for n in early]))}

    nbq, nbo = BIASES["swa_b_qkv"] // N_CHIPS, BIASES["swa_b_o"] // N_CHIPS
    first = jnp.concatenate([c.reshape(-1), swa_b_qkv.reshape(-1), swa_b_o.reshape(-1),
                             jnp.zeros((16 * LANES - D - nbq - nbo,), jnp.float32)]).reshape(16, LANES)
    first_all = _all_gather(first).reshape(N_DEV, 16 * LANES)
    c_all = first_all[:, :D]
    south = first_all[0::2]
    wts["swa_b_qkv"] = south[:, D:D + nbq].reshape(1, N_CHIPS * nbq)
    wts["swa_b_o"] = south[:, D + nbq:D + nbq + nbo].reshape(1, N_CHIPS * nbo)
    cond_all, part = _ada_part(c_all, w_ada)
    ncol = w_ada.shape[2]
    part_all = _all_gather(part.reshape(-1, LANES)).reshape(N_DEV, DEPTH, N_DEV, ncol)
    mine = _one_hot_pick(part_all[0::2], dev, axis=2)
    mod = mine.transpose(1, 0, 2).reshape(DEPTH, N_CHIPS * ncol) + b_ada
    vecs = jnp.concatenate([mod.reshape(DEPTH, 6, D), g_mix[:, None, :], g_mlp[:, None, :]], axis=1)

    late = [("w_ff1", 0), ("w_ff2", 0), ("swa_w_qkv", None), ("swa_w_o", None), ("w_ff1", 1), ("w_ff2", 1)]
    late_local = [(W[n][0] if l is None else W[n][l]).astype(MXU_DTYPE) for n, l in late]
    send_sems, recv_sems, passed, lands, token = _late_gather_start(late_local, [vecs] + [wts[n] for n in early])

    def late_weights(after):
        got = _late_gather_wait(send_sems, recv_sems, passed, lands, after)
        out = {"w_ff1": [None] * DEPTH, "w_ff2": [None] * DEPTH}
        for (n, l), g, own in zip(late, got, late_local):
            if l is None:
                out[n] = whole(n, g, own)
            else:
                out[n][l] = whole(n, g, own)
        return out

    late_names = [n for n in names if n not in early]
    reduce_state = {}

    def on_late_grads(late_grads):
        s_sems, r_sems, passed_g, zones, tok = _pair_in_start([late_grads[n] for n in late_names])
        reduce_state.update(pair=(s_sems, r_sems, passed_g, zones))
        return tok

    def on_late_landed(after):
        gl, got = _pair_in_wait(*reduce_state["pair"], after)
        sums = [_pair_sum(g, s, core_arr, "pair_sum_" + n) for n, g, s in zip(late_names, gl, got)]
        s_sems, r_sems, parts, zones, tok = _exchange_start([s16 for _, s16 in sums], "grad_exchange_start")
        reduce_state.update(sums=sums, split=(s_sems, r_sems, parts, zones))
        return tok

    grad_x, grads, small = _sequence_step(
        x[0], loss_target[0], positions[0], vecs, mla_g_q + token[0, 0], mla_g_kv, swa_sinks, g_final, wts,
        late_weights, on_late_grads, on_late_landed)

    small["b_ada"] = small.pop("dmod")
    small_all = _all_gather(_pack_small(small))
    pk = lambda src: _pack_small({n: src[n] for n in SMALL if n != "loss" and n not in BIASES})
    off, n = _small_slots()["b_ada"]
    dmod_all = small_all.reshape(N_DEV, -1)[:, off:off + n].reshape(N_DEV, DEPTH, N_CHIPS, ncol)
    dm = _one_hot_pick(dmod_all, chip, axis=2).transpose(1, 0, 2)

    gl = [grads[n] for n in early]
    got = _grad_pair_in(gl)
    sums = [_pair_sum(g, s, core_arr, "pair_sum_" + n) for n, g, s in zip(early, gl, got)]
    e_sems, e_rems, e_parts, e_zones, e_tok = _exchange_start([s16 for _, s16 in sums], "mla_exchange_start")

    def finish(tensor_names, sums, others, behind):
        halves = [_chip_sum(s32, o, chip_arr, "chip_sum_" + n, behind) for n, (s32, _), o in zip(tensor_names, sums, others)]
        return {n: _adamw_halves(W[n], mine_h, got_h, M[n], V[n], core_arr, "adamw_" + n)
                for n, mine_h, got_h in zip(tensor_names, halves, _grad_pair_out(halves))}

    late_others = _exchange_wait(*reduce_state["split"], grad_x, "grad_exchange_wait")
    res = finish(late_names, reduce_state["sums"], late_others, e_tok)
    res["w_ada"] = _ada_grad_adamw(cond_all.T, dm, w_ada, m_w_ada, v_w_ada)
    small_res = _small_adamw(small_all, pk(W), pk(M), pk(V))
    early_others = _exchange_wait(e_sems, e_rems, e_parts, e_zones, res["w_ff2"][1], "mla_exchange_wait")
    res.update(finish(early, sums, early_others, None))

    for n, width in BIASES.items():
        g = _one_hot_pick(small_res[n][0].reshape(N_CHIPS, width // N_CHIPS), chip, axis=0).reshape(1, -1)
        res[n] = [g] + _adamw(W[n], g, M[n], V[n], "adamw_" + n)
    for name in order:
        if name not in res:
            res[name] = small_res[name]
    outs = [small_res["loss"][0], grad_x[None]]
    for k in range(4):
        outs += [res[name][k] for name in order]
    return tuple(outs)
```

```python
import functools
import math

import jax
import jax.numpy as jnp
import numpy as np
from jax import lax
from jax.experimental import pallas as pl
from jax.experimental.pallas import tpu as pltpu

D = 1024
DEPTH = 2
MLA_HEADS = 8
QK_NOPE = 128
QK_ROPE = 64
V_DIM = 128
Q_LORA = 384
KV_LORA = 256
ROPE_THETA = 10000.0
SWA_HEADS = 16
SWA_KV_HEADS = 4
SWA_HEAD_DIM = 64
SWA_GROUP = SWA_HEADS // SWA_KV_HEADS
WINDOW = 128
D_FF = 4 * D
EPS = 1e-6
ADAM_LR = 0.001
ADAM_B1 = 0.9
ADAM_B2 = 0.999
ADAM_EPS = 1e-08
ADAM_WD = 0.01
ADAM_STEP = 10

N_CHIPS = 4
N_DEV = 8
LANES = 128
QK_EXT = 256
MLA_SCALE = (QK_NOPE + QK_ROPE) ** -0.5
LOG2E = math.log2(math.e)
LN2 = math.log(2.0)
MLA_QSCALE = MLA_SCALE * LOG2E
ATTN_BLOCK = 2048
ATTN_SUB = 512
MLP_FWD_TILE = (1024, 1024)
MLP_BWD_TILE = (512, 1024)
DW_TOKENS = 4096
SWA_SCALE = SWA_HEAD_DIM ** -0.5
NEG = -1e30
MXU_DTYPE = jnp.bfloat16
VMEM_LIMIT = 56 * 1024 * 1024

R_SH1, R_SC1, R_GT1, R_SH2, R_SC2, R_GT2, R_GMIX, R_GMLP = range(8)
R_BO = 6


def _tile(n, pref):
    if n <= pref:
        return n
    for t in range(pref, 7, -1):
        if n % t == 0 and t % 8 == 0:
            return t
    return n


def _dot(a, b):
    return jnp.dot(a, b, preferred_element_type=jnp.float32)


def _dot_nt(a, b):
    return lax.dot_general(a, b, (((1,), (1,)), ((), ())), preferred_element_type=jnp.float32)


def _dot_tn(a, b):
    return lax.dot_general(a, b, (((0,), (0,)), ((), ())), preferred_element_type=jnp.float32)


def _rms(x):
    r = lax.rsqrt(jnp.mean(x * x, axis=-1, keepdims=True) + EPS)
    return x * r, r


def _rms_bwd(dxhat, xhat, r):
    return r * (dxhat - xhat * jnp.mean(dxhat * xhat, axis=-1, keepdims=True))


def _rowsum(v):
    return jnp.sum(v, axis=0, keepdims=True)


def _params(*sem):
    return pltpu.CompilerParams(dimension_semantics=sem, vmem_limit_bytes=VMEM_LIMIT)


def _full(shape):
    nd = len(shape)
    return pl.BlockSpec(shape, lambda *_: (0,) * nd)


def _rows(tm, cols):
    return pl.BlockSpec((tm, cols), lambda i, *_: (i, 0))


def _modulate_bwd(dh, x, vec_ref, r_g, r_sc, r_sh, ps_ref, dres):
    xhat, r = _rms(x)
    g = vec_ref[r_g:r_g + 1, :]
    n = xhat * g
    ps_ref[r_sh:r_sh + 1, :] += _rowsum(dh)
    ps_ref[r_sc:r_sc + 1, :] += _rowsum(dh * n)
    dn = dh * (1.0 + vec_ref[r_sc:r_sc + 1, :])
    ps_ref[r_g:r_g + 1, :] += _rowsum(dn * xhat)
    return dres + _rms_bwd(dn * g, xhat, r)


def _mla_pre(x, vec, wcat, g_q, g_kv, wuq, wukv, cs):
    T = x.shape[0]
    tm = _tile(T, 512)
    H = MLA_HEADS

    def body(x_ref, vec_ref, wcat_ref, gq_ref, gkv_ref, wuq_ref, wukv_ref, cs_ref, h_ref, z_ref, q_ref, k_ref, v_ref):
        xhat, _ = _rms(x_ref[...])
        h = xhat * vec_ref[R_GMIX:R_GMIX + 1, :] * (1.0 + vec_ref[R_SC1:R_SC1 + 1, :]) + vec_ref[R_SH1:R_SH1 + 1, :]
        hb = h.astype(MXU_DTYPE)
        h_ref[...] = hb
        z = _dot(hb, wcat_ref[...])
        z_ref[...] = z
        cq = (_rms(z[:, :Q_LORA])[0] * gq_ref[...]).astype(MXU_DTYPE)
        ckv = (_rms(z[:, Q_LORA:Q_LORA + KV_LORA])[0] * gkv_ref[...]).astype(MXU_DTYPE)
        cs_t = cs_ref[...]
        t = z[:, Q_LORA + KV_LORA:] * cs_t
        k_rope = (t + pltpu.roll(t, QK_ROPE, axis=1)).astype(MXU_DTYPE)
        low = lax.broadcasted_iota(jnp.int32, (1, LANES), 1) < QK_ROPE
        for hd in range(H):
            qf = _dot(cq, wuq_ref[hd])
            tq = qf[:, QK_NOPE:] * cs_t
            tq = tq + pltpu.roll(tq, QK_ROPE, axis=1)
            q_ref[hd, :, :QK_NOPE] = (qf[:, :QK_NOPE] * MLA_QSCALE).astype(MXU_DTYPE)
            q_ref[hd, :, QK_NOPE:] = jnp.where(low, tq * MLA_QSCALE, 0.0).astype(MXU_DTYPE)
            kvf = _dot(ckv, wukv_ref[hd])
            k_ref[hd, :, :QK_NOPE] = kvf[:, :QK_NOPE].astype(MXU_DTYPE)
            k_ref[hd, :, QK_NOPE:] = k_rope
            v_ref[hd] = kvf[:, QK_NOPE:].astype(MXU_DTYPE)

    zc = wcat.shape[1]
    return pl.pallas_call(
        body, name="mla_pre", grid=(T // tm,),
        in_specs=[_rows(tm, D), _full((8, D)), _full(wcat.shape), _full(g_q.shape), _full(g_kv.shape),
                  _full(wuq.shape), _full(wukv.shape), _rows(tm, LANES)],
        out_specs=[_rows(tm, D), _rows(tm, zc),
                   pl.BlockSpec((H, tm, QK_EXT), lambda i: (0, i, 0)),
                   pl.BlockSpec((H, tm, QK_EXT), lambda i: (0, i, 0)),
                   pl.BlockSpec((H, tm, V_DIM), lambda i: (0, i, 0))],
        out_shape=[jax.ShapeDtypeStruct((T, D), MXU_DTYPE), jax.ShapeDtypeStruct((T, zc), jnp.float32),
                   jax.ShapeDtypeStruct((H, T, QK_EXT), MXU_DTYPE), jax.ShapeDtypeStruct((H, T, QK_EXT), MXU_DTYPE),
                   jax.ShapeDtypeStruct((H, T, V_DIM), MXU_DTYPE)],
        compiler_params=_params("parallel"),
    )(x, vec, wcat, g_q, g_kv, wuq, wukv, cs)


def _mla_attn_fwd(q, k, v):
    H, T, _ = q.shape
    tb = _tile(T, ATTN_BLOCK)
    sub = min(ATTN_SUB, tb)
    ns, nb = tb // sub, T // tb

    def body(q_ref, k_ref, v_ref, o_ref, lse_ref, m_sc, l_sc, acc_sc):
        qi, kj = pl.program_id(1), pl.program_id(2)

        @pl.when(kj == 0)
        def _():
            m_sc[...] = jnp.full_like(m_sc, NEG)
            l_sc[...] = jnp.zeros_like(l_sc)
            acc_sc[...] = jnp.zeros_like(acc_sc)

        def update(r, kk, masked):
            rows, keys = pl.ds(r * sub, sub), pl.ds(kk * sub, sub)
            s = _dot_nt(q_ref[0, rows, :], k_ref[0, keys, :])
            if masked:
                row = lax.broadcasted_iota(jnp.int32, (sub, sub), 0)
                col = lax.broadcasted_iota(jnp.int32, (sub, sub), 1)
                s = jnp.where(col <= row, s, NEG)
            m_prev = m_sc[rows, :]
            m_new = jnp.maximum(m_prev, jnp.max(s, axis=1, keepdims=True))
            alpha = jnp.exp2(m_prev - m_new)
            p = jnp.exp2(s - jnp.tile(m_new, (1, sub // LANES)))
            l_sc[rows, :] = alpha * l_sc[rows, :] + jnp.sum(p, axis=1, keepdims=True)
            acc_sc[rows, :] = alpha * acc_sc[rows, :] + _dot(p.astype(MXU_DTYPE), v_ref[0, keys, :])
            m_sc[rows, :] = m_new

        @pl.when(kj < qi)
        def _():
            for kk in range(ns):
                for r in range(ns):
                    update(r, kk, False)

        @pl.when(kj == qi)
        def _():
            for kk in range(ns):
                for r in range(kk, ns):
                    update(r, kk, r == kk)
            l = l_sc[...]
            o_ref[...] = (acc_sc[...] / l).astype(o_ref.dtype)
            lse = m_sc[...] + jnp.log2(l)
            pick = (lax.broadcasted_iota(jnp.int32, (8, LANES), 1) == 0).astype(jnp.float32)
            row = lax.dot_general(pick, lse, (((1,), (1,)), ((), ())), precision=lax.Precision.HIGHEST,
                                  preferred_element_type=jnp.float32)
            lse_ref[0] = row[0:1, :]

    kv_idx = lambda h, i, j: (h, jnp.minimum(i, j), 0)
    return pl.pallas_call(
        body, name="mla_attn_fwd", grid=(H, nb, nb),
        in_specs=[pl.BlockSpec((1, tb, QK_EXT), lambda h, i, j: (h, i, 0)),
                  pl.BlockSpec((1, tb, QK_EXT), kv_idx),
                  pl.BlockSpec((1, tb, V_DIM), kv_idx)],
        out_specs=[pl.BlockSpec((tb, V_DIM), lambda h, i, j: (i, h)),
                   pl.BlockSpec((1, 1, tb), lambda h, i, j: (h, 0, i))],
        out_shape=[jax.ShapeDtypeStruct((T, H * V_DIM), MXU_DTYPE), jax.ShapeDtypeStruct((H, 1, T), jnp.float32)],
        scratch_shapes=[pltpu.VMEM((tb, LANES), jnp.float32), pltpu.VMEM((tb, LANES), jnp.float32),
                        pltpu.VMEM((tb, V_DIM), jnp.float32)],
        compiler_params=_params("parallel", "parallel", "arbitrary"),
    )(q, k, v)


def _post_attn(o, x, w_o, bias, vec, o_transposed=False):
    T = x.shape[0]
    tm = _tile(T, 512)
    o_spec = pl.BlockSpec((D, tm), lambda i: (0, i)) if o_transposed else _rows(tm, D)

    def body(o_ref, x_ref, w_ref, b_ref, vec_ref, y_ref, xm_ref, h_ref):
        y = (_dot_tn if o_transposed else _dot)(o_ref[...], w_ref[...]) + b_ref[...]
        y_ref[...] = y.astype(y_ref.dtype)
        xm = x_ref[...] + vec_ref[R_GT1:R_GT1 + 1, :] * y
        xm_ref[...] = xm
        xhat, _ = _rms(xm)
        h = xhat * vec_ref[R_GMLP:R_GMLP + 1, :] * (1.0 + vec_ref[R_SC2:R_SC2 + 1, :]) + vec_ref[R_SH2:R_SH2 + 1, :]
        h_ref[...] = h.astype(h_ref.dtype)

    return pl.pallas_call(
        body, name="post_attn", grid=(T // tm,),
        in_specs=[o_spec, _rows(tm, D), _full((D, D)), _full((1, D)), _full((8, D))],
        out_specs=[_rows(tm, D), _rows(tm, D), _rows(tm, D)],
        out_shape=[jax.ShapeDtypeStruct((T, D), MXU_DTYPE), jax.ShapeDtypeStruct((T, D), jnp.float32),
                   jax.ShapeDtypeStruct((T, D), MXU_DTYPE)],
        compiler_params=_params("parallel"),
    )(o, x, w_o, bias, vec)


def _ff_specs(tf):
    per = D_FF // N_CHIPS // tf
    w1 = pl.BlockSpec((None, D, tf), lambda i, f: (f // per, 0, f % per))
    w2 = pl.BlockSpec((None, tf, D), lambda i, f: (f // per, f % per, 0))
    return w1, w2


def _mlp_fwd(h2, w1, w2, xm, vec):
    T = h2.shape[0]
    tm = _tile(T, MLP_FWD_TILE[0])
    tf = _tile(D_FF // N_CHIPS, MLP_FWD_TILE[1])
    nf = D_FF // tf
    w1_spec, w2_spec = _ff_specs(tf)

    def body(h_ref, w1_ref, w2_ref, xm_ref, vec_ref, a_ref, y_ref, xo_ref, acc):
        f = pl.program_id(1)

        @pl.when(f == 0)
        def _():
            acc[...] = jnp.zeros_like(acc)

        u = jnp.maximum(_dot(h_ref[...], w1_ref[...]), 0.0)
        ab = (u * u).astype(MXU_DTYPE)
        a_ref[...] = ab
        acc[...] += _dot(ab, w2_ref[...])

        @pl.when(f == nf - 1)
        def _():
            y = acc[...]
            y_ref[...] = y.astype(y_ref.dtype)
            xo_ref[...] = xm_ref[...] + vec_ref[R_GT2:R_GT2 + 1, :] * y

    return pl.pallas_call(
        body, name="mlp_fwd", grid=(T // tm, nf),
        in_specs=[_rows(tm, D), w1_spec, w2_spec, _rows(tm, D), _full((8, D))],
        out_specs=[pl.BlockSpec((tm, tf), lambda i, f: (i, f)), _rows(tm, D), _rows(tm, D)],
        out_shape=[jax.ShapeDtypeStruct((T, D_FF), MXU_DTYPE), jax.ShapeDtypeStruct((T, D), MXU_DTYPE),
                   jax.ShapeDtypeStruct((T, D), jnp.float32)],
        scratch_shapes=[pltpu.VMEM((tm, D), jnp.float32)],
        compiler_params=_params("parallel", "arbitrary"),
    )(h2, w1, w2, xm, vec)


def _swa_pre(x, vec, w_qkv, b_qkv):
    T = x.shape[0]
    tm = _tile(T, 512)
    nq = SWA_HEADS * SWA_HEAD_DIM
    nk = SWA_KV_HEADS * SWA_HEAD_DIM
    wq_t, w_kv = w_qkv[:, :nq].T, w_qkv[:, nq:]
    bq_col, b_kv = b_qkv[:, :nq].reshape(nq, 1), b_qkv[:, nq:]

    def body(x_ref, vec_ref, wq_ref, wkv_ref, bq_ref, bkv_ref, h_ref, qt_ref, k_ref, v_ref):
        xhat, _ = _rms(x_ref[...])
        h = xhat * vec_ref[R_GMIX:R_GMIX + 1, :] * (1.0 + vec_ref[R_SC1:R_SC1 + 1, :]) + vec_ref[R_SH1:R_SH1 + 1, :]
        hb = h.astype(MXU_DTYPE)
        h_ref[...] = hb
        qt_ref[...] = ((_dot_nt(wq_ref[...], hb) + bq_ref[...]) * SWA_SCALE).astype(MXU_DTYPE)
        kv = _dot(hb, wkv_ref[...]) + bkv_ref[...]
        k_ref[...] = kv[:, :nk].astype(MXU_DTYPE)
        v_ref[...] = kv[:, nk:].astype(MXU_DTYPE)

    return pl.pallas_call(
        body, name="swa_pre", grid=(T // tm,),
        in_specs=[_rows(tm, D), _full((8, D)), _full(wq_t.shape), _full(w_kv.shape), _full(bq_col.shape),
                  _full(b_kv.shape)],
        out_specs=[_rows(tm, D), pl.BlockSpec((nq, tm), lambda i: (0, i)), _rows(tm, nk), _rows(tm, nk)],
        out_shape=[jax.ShapeDtypeStruct((T, D), MXU_DTYPE), jax.ShapeDtypeStruct((nq, T), MXU_DTYPE),
                   jax.ShapeDtypeStruct((T, nk), MXU_DTYPE), jax.ShapeDtypeStruct((T, nk), MXU_DTYPE)],
        compiler_params=_params("parallel"),
    )(x, vec, wq_t, w_kv, bq_col, b_kv)


def _swa_bias():
    W = WINDOW
    slopes = 2.0 ** (-8.0 * np.arange(1, SWA_HEADS + 1) / SWA_HEADS)
    dist = W + np.arange(W)[None, :] - np.arange(2 * W)[:, None]
    inside = (dist >= 0) & (dist < W)
    bias = np.where(inside[None], -slopes[:, None, None] * dist[None].astype(np.float64), NEG)
    bias = bias.reshape(SWA_KV_HEADS, SWA_GROUP, 2 * W, W).transpose(0, 2, 1, 3)
    return jnp.asarray(bias.reshape(SWA_KV_HEADS, 2 * W, SWA_GROUP * W), jnp.float32)


SWA_STEP_BLOCKS = 4


def _swa_blocks(T):
    nb = T // WINDOW
    return next(b for b in (SWA_STEP_BLOCKS, 2, 1) if nb % b == 0)


def _swa_views(b, qt_ref, kp_ref, kc_ref):
    W = WINDOW
    prev = kp_ref if b == 0 else kc_ref.at[pl.ds((b - 1) * W, W), :]
    return qt_ref.at[:, pl.ds(b * W, W)], prev, kc_ref.at[pl.ds(b * W, W), :]


def _swa_probs(has_prev, kh, qt_ref, kp_ref, kc_ref, bias_ref, sink_ref):
    W, Dh, G = WINDOW, SWA_HEAD_DIM, SWA_GROUP
    qt = jnp.concatenate([qt_ref[(kh * G + g) * Dh:(kh * G + g + 1) * Dh, :] for g in range(G)], axis=1)
    kb = jnp.concatenate([kp_ref[:, kh * Dh:(kh + 1) * Dh], kc_ref[:, kh * Dh:(kh + 1) * Dh]], axis=0)
    s = _dot(kb, qt) + bias_ref[kh]
    if has_prev is not True:
        key = lax.broadcasted_iota(jnp.int32, (2 * W, 1), 0)
        s = jnp.where((key >= W) | has_prev, s, NEG)
    sink = sink_ref[kh]
    m = jnp.maximum(jnp.max(s, axis=0, keepdims=True), sink)
    p = jnp.exp(s - m)
    p_sink = jnp.exp(sink - m)
    inv = 1.0 / (jnp.sum(p, axis=0, keepdims=True) + p_sink)
    return qt, kb, p * inv, p_sink * inv


def _swa_attn_fwd(qt, k, v, bias, sink_rows):
    T = qt.shape[1]
    W, Dh, G, Hk = WINDOW, SWA_HEAD_DIM, SWA_GROUP, SWA_KV_HEADS
    nk = Hk * Dh

    nb = _swa_blocks(T)

    def body(qt_ref, kp_ref, kc_ref, vp_ref, vc_ref, bias_ref, sink_ref, ot_ref):
        n = pl.program_id(0)
        for b in range(nb):
            q_b, kp_b, kc_b = _swa_views(b, qt_ref, kp_ref, kc_ref)
            _, vp_b, vc_b = _swa_views(b, qt_ref, vp_ref, vc_ref)
            for kh in range(Hk):
                _, _, pn, _ = _swa_probs(True if b else n > 0, kh, q_b, kp_b, kc_b, bias_ref, sink_ref)
                vb = jnp.concatenate([vp_b[:, kh * Dh:(kh + 1) * Dh], vc_b[:, kh * Dh:(kh + 1) * Dh]], axis=0)
                ot = _dot_tn(vb, pn.astype(MXU_DTYPE))
                for g in range(G):
                    rows = pl.ds((kh * G + g) * Dh, Dh)
                    ot_ref[rows, pl.ds(b * W, W)] = ot[:, g * W:(g + 1) * W].astype(ot_ref.dtype)

    prev = lambda n: (jnp.maximum(n * nb - 1, 0), 0)
    cur = lambda n: (n, 0)
    col = lambda n: (0, n)
    return pl.pallas_call(
        body, name="swa_attn_fwd", grid=(T // (nb * W),),
        in_specs=[pl.BlockSpec((D, nb * W), col), pl.BlockSpec((W, nk), prev), pl.BlockSpec((nb * W, nk), cur),
                  pl.BlockSpec((W, nk), prev), pl.BlockSpec((nb * W, nk), cur), _full(bias.shape),
                  _full(sink_rows.shape)],
        out_specs=pl.BlockSpec((D, nb * W), col),
        out_shape=jax.ShapeDtypeStruct((D, T), MXU_DTYPE),
        compiler_params=_params("parallel"),
    )(qt, k, k, v, v, bias, sink_rows)


def _final_loss(x, tgt, g):
    T = x.shape[0]
    tm = _tile(T, 512)

    def body(x_ref, t_ref, g_ref, loss_ref, dx_ref, dg_ref):
        @pl.when(pl.program_id(0) == 0)
        def _():
            loss_ref[...] = jnp.zeros_like(loss_ref)
            dg_ref[...] = jnp.zeros_like(dg_ref)

        xhat, r = _rms(x_ref[...])
        gv = g_ref[...]
        e = xhat * gv - t_ref[...]
        loss_ref[...] += 0.5 * jnp.sum(jnp.mean(e * e, axis=-1, keepdims=True), axis=0, keepdims=True)
        dy = e * (1.0 / D)
        dg_ref[...] += _rowsum(dy * xhat)
        dx_ref[...] = _rms_bwd(dy * gv, xhat, r)

    return pl.pallas_call(
        body, name="final_loss", grid=(T // tm,),
        in_specs=[_rows(tm, D), _rows(tm, D), _full((1, D))],
        out_specs=[_full((8, LANES)), _rows(tm, D), _full((1, D))],
        out_shape=[jax.ShapeDtypeStruct((8, LANES), jnp.float32), jax.ShapeDtypeStruct((T, D), jnp.float32),
                   jax.ShapeDtypeStruct((1, D), jnp.float32)],
        compiler_params=_params("arbitrary"),
    )(x, tgt, g)


def _mlp_bwd(dxo, y2, a, w1, w2, xm, vec):
    T = dxo.shape[0]
    tm = _tile(T, MLP_BWD_TILE[0])
    tf = _tile(D_FF // N_CHIPS, MLP_BWD_TILE[1])
    nf = D_FF // tf
    w1_spec, w2_spec = _ff_specs(tf)

    def body(dxo_ref, y_ref, a_ref, w1_ref, w2_ref, xm_ref, vec_ref, du_ref, dy_ref, dxm_ref, ps_ref, dyb, acc):
        i, f = pl.program_id(0), pl.program_id(1)

        @pl.when((i == 0) & (f == 0))
        def _():
            ps_ref[...] = jnp.zeros_like(ps_ref)

        @pl.when(f == 0)
        def _():
            dxo_t = dxo_ref[...]
            d = (dxo_t * vec_ref[R_GT2:R_GT2 + 1, :]).astype(MXU_DTYPE)
            dyb[...] = d
            dy_ref[...] = d
            acc[...] = jnp.zeros_like(acc)
            ps_ref[R_GT2:R_GT2 + 1, :] += _rowsum(dxo_t * y_ref[...].astype(jnp.float32))

        da = _dot_nt(dyb[...], w2_ref[...])
        dub = (da * (2.0 * jnp.sqrt(a_ref[...].astype(jnp.float32)))).astype(MXU_DTYPE)
        du_ref[...] = dub
        acc[...] += _dot_nt(dub, w1_ref[...])

        @pl.when(f == nf - 1)
        def _():
            dxm_ref[...] = _modulate_bwd(acc[...], xm_ref[...], vec_ref, R_GMLP, R_SC2, R_SH2, ps_ref, dxo_ref[...])

    return pl.pallas_call(
        body, name="mlp_bwd", grid=(T // tm, nf),
        in_specs=[_rows(tm, D), _rows(tm, D), pl.BlockSpec((tm, tf), lambda i, f: (i, f)), w1_spec, w2_spec,
                  _rows(tm, D), _full((8, D))],
        out_specs=[pl.BlockSpec((tm, tf), lambda i, f: (i, f)), _rows(tm, D), _rows(tm, D), _full((8, D))],
        out_shape=[jax.ShapeDtypeStruct((T, D_FF), MXU_DTYPE), jax.ShapeDtypeStruct((T, D), MXU_DTYPE),
                   jax.ShapeDtypeStruct((T, D), jnp.float32), jax.ShapeDtypeStruct((8, D), jnp.float32)],
        scratch_shapes=[pltpu.VMEM((tm, D), MXU_DTYPE), pltpu.VMEM((tm, D), jnp.float32)],
        compiler_params=_params("arbitrary", "arbitrary"),
    )(dxo, y2, a, w1, w2, xm, vec)


def _mm_tn(a, g, name, split=None, layers=1, layer=0, into=None, a_transposed=False):
    K, T = a.shape if a_transposed else a.shape[::-1]
    N = g.shape[1]
    kq = K // N_CHIPS if split == "rows" else K
    nq = N // N_CHIPS if split == "cols" else N
    bk, bn, bt = _tile(kq, 1024), _tile(nq, 1024), _tile(T, DW_TOKENS)
    if nq % bn or bn % LANES:
        bn = nq
    kper, nper = kq // bk, nq // bn

    def body(*refs):
        a_ref, g_ref, o_ref = refs[0], refs[1], refs[-1]

        @pl.when(pl.program_id(2) == 0)
        def _():
            o_ref[...] = jnp.zeros_like(o_ref)

        o_ref[...] += (_dot if a_transposed else _dot_tn)(a_ref[...], g_ref[...])

    a_spec = pl.BlockSpec((bk, bt), lambda k, n, t: (k, t)) if a_transposed else pl.BlockSpec((bt, bk), lambda k, n, t: (t, k))
    in_specs = [a_spec, pl.BlockSpec((bt, bn), lambda k, n, t: (t, n))]
    args = [a, g]
    aliases = {}
    if split is None:
        out_spec = pl.BlockSpec((bk, bn), lambda k, n, t: (k, n))
        out_shape = jax.ShapeDtypeStruct((K, N), jnp.float32)
    else:
        if split == "cols":
            idx = lambda k, n, t: (n // nper, layer, k, n % nper)
        else:
            idx = lambda k, n, t: (k // kper, layer, k % kper, n)
        out_spec = pl.BlockSpec((None, None, bk, bn), idx)
        out_shape = jax.ShapeDtypeStruct((N_CHIPS, layers, kq, nq), jnp.float32)
        if into is not None:
            in_specs.append(pl.BlockSpec(memory_space=pl.ANY))
            args.append(into)
            aliases = {2: 0}
    return pl.pallas_call(
        body, name=name, grid=(K // bk, N // bn, T // bt), in_specs=in_specs, out_specs=out_spec, out_shape=out_shape,
        input_output_aliases=aliases, compiler_params=_params("parallel", "parallel", "arbitrary"),
    )(*args)


def _attn_out_bwd(dxm, y1, o, w_o, vec, with_delta):
    T = dxm.shape[0]
    tm = _tile(T, 512)
    H = MLA_HEADS

    def body(dxm_ref, y_ref, w_ref, vec_ref, *refs):
        o_ref = refs[0] if with_delta else None
        dy_ref, do_ref, ps_ref, *delta_ref = refs[1:] if with_delta else refs

        @pl.when(pl.program_id(0) == 0)
        def _():
            ps_ref[...] = jnp.zeros_like(ps_ref)

        dxm_t = dxm_ref[...]
        dy = dxm_t * vec_ref[R_GT1:R_GT1 + 1, :]
        ps_ref[R_GT1:R_GT1 + 1, :] += _rowsum(dxm_t * y_ref[...].astype(jnp.float32))
        ps_ref[R_BO:R_BO + 1, :] += _rowsum(dy)
        dyb = dy.astype(MXU_DTYPE)
        dy_ref[...] = dyb
        if not with_delta:
            do_ref[...] = _dot_nt(w_ref[...], dyb).astype(do_ref.dtype)
        else:
            do = _dot_nt(dyb, w_ref[...])
            do_ref[...] = do.astype(do_ref.dtype)
            of = o_ref[...].astype(jnp.float32)
            ones = jnp.ones((8, V_DIM), jnp.float32)
            for hd in range(H):
                sl = slice(hd * V_DIM, (hd + 1) * V_DIM)
                d = lax.dot_general(ones, do[:, sl] * of[:, sl], (((1,), (1,)), ((), ())),
                                    precision=lax.Precision.HIGHEST, preferred_element_type=jnp.float32)
                delta_ref[0][hd] = d[0:1, :]

    out_specs = [_rows(tm, D), _rows(tm, D), _full((8, D))]
    out_shape = [jax.ShapeDtypeStruct((T, D), MXU_DTYPE), jax.ShapeDtypeStruct((T, D), MXU_DTYPE),
                 jax.ShapeDtypeStruct((8, D), jnp.float32)]
    if not with_delta:
        out_specs[1] = pl.BlockSpec((D, tm), lambda i: (0, i))
        out_shape[1] = jax.ShapeDtypeStruct((D, T), MXU_DTYPE)
    if with_delta:
        out_specs.append(pl.BlockSpec((H, 1, tm), lambda i: (0, 0, i)))
        out_shape.append(jax.ShapeDtypeStruct((H, 1, T), jnp.float32))
    return pl.pallas_call(
        body, name="attn_out_bwd_mla" if with_delta else "attn_out_bwd_swa", grid=(T // tm,),
        in_specs=[_rows(tm, D), _rows(tm, D), _full((D, D)), _full((8, D))] + ([_rows(tm, D)] if with_delta else []),
        out_specs=out_specs, out_shape=out_shape,
        compiler_params=_params("arbitrary"),
    )(dxm, y1, w_o, vec, *([o] if with_delta else []))


def _mla_attn_bwd(q, k, v, do, lse, delta):
    H, T, _ = q.shape
    tb = _tile(T, ATTN_BLOCK)
    sub = min(ATTN_SUB, tb)
    ns, nb = tb // sub, T // tb

    def body(q_ref, k_ref, v_ref, do_ref, lse_ref, dl_ref, dq_ref, dk_ref, dv_ref, dk_acc, dv_acc):
        j, i = pl.program_id(1), pl.program_id(2)

        @pl.when((j == 0) & (i == 0))
        def _():
            dq_ref[...] = jnp.zeros_like(dq_ref)

        def update(kk, r, masked):
            keys, rows = pl.ds(kk * sub, sub), pl.ds(r * sub, sub)
            kb, qb, dob = k_ref[0, keys, :], q_ref[0, rows, :], do_ref[rows, :]
            st = _dot_nt(kb, qb)
            if masked:
                row = lax.broadcasted_iota(jnp.int32, (sub, sub), 0)
                col = lax.broadcasted_iota(jnp.int32, (sub, sub), 1)
                st = jnp.where(row <= col, st, NEG)
            pt = jnp.exp2(st - lse_ref[0, :, rows])
            dv_acc[keys, :] += _dot(pt.astype(MXU_DTYPE), dob)
            dpt = _dot_nt(v_ref[0, keys, :], dob)
            dst = (pt * (dpt - dl_ref[0, :, rows])).astype(MXU_DTYPE)
            dk_acc[keys, :] += _dot(dst, qb)
            q_rows = pl.ds(pl.multiple_of(i * tb + r * sub, sub), sub)
            dq_ref[0, q_rows, :] += _dot_tn(dst, kb)

        @pl.when(i == j)
        def _():
            dk_acc[...] = jnp.zeros_like(dk_acc)
            dv_acc[...] = jnp.zeros_like(dv_acc)
            for r in range(ns):
                for kk in range(r + 1):
                    update(kk, r, kk == r)

        @pl.when(i > j)
        def _():
            for r in range(ns):
                for kk in range(ns):
                    update(kk, r, False)

        @pl.when(i == nb - 1)
        def _():
            dk_ref[0] = (dk_acc[...] * LN2).astype(dk_ref.dtype)
            dv_ref[0] = dv_acc[...].astype(dv_ref.dtype)

    q_idx = lambda h, j, i: (h, jnp.maximum(i, j), 0)
    kv_idx = lambda h, j, i: (h, j, 0)
    stat_idx = lambda h, j, i: (h, 0, jnp.maximum(i, j))
    return pl.pallas_call(
        body, name="mla_attn_bwd", grid=(H, nb, nb),
        in_specs=[pl.BlockSpec((1, tb, QK_EXT), q_idx), pl.BlockSpec((1, tb, QK_EXT), kv_idx),
                  pl.BlockSpec((1, tb, V_DIM), kv_idx),
                  pl.BlockSpec((tb, V_DIM), lambda h, j, i: (jnp.maximum(i, j), h)),
                  pl.BlockSpec((1, 1, tb), stat_idx), pl.BlockSpec((1, 1, tb), stat_idx)],
        out_specs=[pl.BlockSpec((1, T, QK_EXT), lambda h, j, i: (h, 0, 0)),
                   pl.BlockSpec((1, tb, QK_EXT), kv_idx), pl.BlockSpec((1, tb, V_DIM), kv_idx)],
        out_shape=[jax.ShapeDtypeStruct((H, T, QK_EXT), jnp.float32), jax.ShapeDtypeStruct((H, T, QK_EXT), MXU_DTYPE),
                   jax.ShapeDtypeStruct((H, T, V_DIM), MXU_DTYPE)],
        scratch_shapes=[pltpu.VMEM((tb, QK_EXT), jnp.float32), pltpu.VMEM((tb, V_DIM), jnp.float32)],
        compiler_params=_params("parallel", "arbitrary", "arbitrary"),
    )(q, k, v, do, lse, delta)


def _mla_pre_bwd(x, dxm, vec, hb, z, dq, dk, dv, cs, wcat, g_q, g_kv, wuq, wukv):
    T = x.shape[0]
    tm = _tile(T, 256)
    H = MLA_HEADS
    zc = wcat.shape[1]

    def body(x_ref, dxm_ref, vec_ref, h_ref, z_ref, dq_ref, dk_ref, dv_ref, cs_ref, wcat_ref, gq_ref, gkv_ref,
             wuq_ref, wukv_ref, dx_ref, ps_ref, dgq_ref, dgkv_ref, dwcat_ref, dwuq_ref, dwukv_ref):
        @pl.when(pl.program_id(0) == 0)
        def _():
            for ref in (ps_ref, dgq_ref, dgkv_ref, dwcat_ref, dwuq_ref, dwukv_ref):
                ref[...] = jnp.zeros_like(ref)

        z = z_ref[...]
        cs_t = cs_ref[...]
        cqhat, rq = _rms(z[:, :Q_LORA])
        ckhat, rk = _rms(z[:, Q_LORA:Q_LORA + KV_LORA])
        gq, gkv = gq_ref[...], gkv_ref[...]
        cq = (cqhat * gq).astype(MXU_DTYPE)
        ckv = (ckhat * gkv).astype(MXU_DTYPE)
        dcq = jnp.zeros((tm, Q_LORA), jnp.float32)
        dckv = jnp.zeros((tm, KV_LORA), jnp.float32)
        dkr = jnp.zeros((tm, LANES), jnp.float32)
        for hd in range(H):
            dqh = dq_ref[hd] * MLA_SCALE
            gqh = jnp.concatenate([dqh[:, :QK_NOPE], dqh[:, QK_NOPE:] * cs_t], axis=1).astype(MXU_DTYPE)
            dcq += _dot_nt(gqh, wuq_ref[hd])
            dwuq_ref[hd] += _dot_tn(cq, gqh)
            dkh = dk_ref[hd]
            gkvh = jnp.concatenate([dkh[:, :QK_NOPE], dv_ref[hd]], axis=1)
            dckv += _dot_nt(gkvh, wukv_ref[hd])
            dwukv_ref[hd] += _dot_tn(ckv, gkvh)
            dkr += dkh[:, QK_NOPE:].astype(jnp.float32)
        dgq_ref[...] += _rowsum(dcq * cqhat)
        dgkv_ref[...] += _rowsum(dckv * ckhat)
        dcq_pre = _rms_bwd(dcq * gq, cqhat, rq)
        dckv_pre = _rms_bwd(dckv * gkv, ckhat, rk)
        dkr2 = (dkr + pltpu.roll(dkr, QK_ROPE, axis=1)) * cs_t
        dz = jnp.concatenate([dcq_pre, dckv_pre, dkr2], axis=1).astype(MXU_DTYPE)
        dwcat_ref[...] += _dot_tn(h_ref[...], dz)
        dh = _dot_nt(dz, wcat_ref[...])
        dx_ref[...] = _modulate_bwd(dh, x_ref[...], vec_ref, R_GMIX, R_SC1, R_SH1, ps_ref, dxm_ref[...])

    hblk = lambda w: pl.BlockSpec((H, tm, w), lambda i: (0, i, 0))
    return pl.pallas_call(
        body, name="mla_pre_bwd", grid=(T // tm,),
        in_specs=[_rows(tm, D), _rows(tm, D), _full((8, D)), _rows(tm, D), _rows(tm, zc), hblk(QK_EXT), hblk(QK_EXT),
                  hblk(V_DIM), _rows(tm, LANES), _full(wcat.shape), _full(g_q.shape), _full(g_kv.shape),
                  _full(wuq.shape), _full(wukv.shape)],
        out_specs=[_rows(tm, D), _full((8, D)), _full(g_q.shape), _full(g_kv.shape), _full(wcat.shape),
                   _full(wuq.shape), _full(wukv.shape)],
        out_shape=[jax.ShapeDtypeStruct((T, D), jnp.float32), jax.ShapeDtypeStruct((8, D), jnp.float32),
                   jax.ShapeDtypeStruct(g_q.shape, jnp.float32), jax.ShapeDtypeStruct(g_kv.shape, jnp.float32),
                   jax.ShapeDtypeStruct(wcat.shape, jnp.float32), jax.ShapeDtypeStruct(wuq.shape, jnp.float32),
                   jax.ShapeDtypeStruct(wukv.shape, jnp.float32)],
        compiler_params=_params("arbitrary"),
    )(x, dxm, vec, hb, z, dq, dk, dv, cs, wcat, g_q, g_kv, wuq, wukv)


def _swa_attn_bwd(qt, k, v, dot_, bias, sink_rows):
    T = qt.shape[1]
    W, Dh, G, Hk = WINDOW, SWA_HEAD_DIM, SWA_GROUP, SWA_KV_HEADS
    nk = Hk * Dh
    nb = _swa_blocks(T)

    def body(qt_ref, kp_ref, kc_ref, vp_ref, vc_ref, dot_ref, bias_ref, sink_ref, dqt_ref, dk_ref, dv_ref, dsink_ref):
        n = pl.program_id(0)

        @pl.when(n == 0)
        def _():
            dk_ref[...] = jnp.zeros_like(dk_ref)
            dv_ref[...] = jnp.zeros_like(dv_ref)
            dsink_ref[...] = jnp.zeros_like(dsink_ref)

        def add_rows(first_row, dkb_part, dvb_part):
            rows = pl.ds(pl.multiple_of(first_row, W), W)
            dk_ref[rows, :] += dkb_part
            dv_ref[rows, :] += dvb_part

        for b in range(nb):
            q_b, kp_b, kc_b = _swa_views(b, qt_ref, kp_ref, kc_ref)
            do_b, vp_b, vc_b = _swa_views(b, dot_ref, vp_ref, vc_ref)
            dks, dvs = [], []
            for kh in range(Hk):
                qt, kb, pn, p_sink = _swa_probs(True if b else n > 0, kh, q_b, kp_b, kc_b, bias_ref, sink_ref)
                vb = jnp.concatenate([vp_b[:, kh * Dh:(kh + 1) * Dh], vc_b[:, kh * Dh:(kh + 1) * Dh]], axis=0)
                dot_h = jnp.concatenate([do_b[(kh * G + g) * Dh:(kh * G + g + 1) * Dh, :] for g in range(G)], axis=1)
                dp = _dot(vb, dot_h)
                delta = jnp.sum(pn * dp, axis=0, keepdims=True)
                dsb = (pn * (dp - delta)).astype(MXU_DTYPE)
                dsink_ref[kh] += -p_sink * delta
                dqt = _dot_tn(kb, dsb) * SWA_SCALE
                for g in range(G):
                    dqt_ref[pl.ds((kh * G + g) * Dh, Dh), pl.ds(b * W, W)] = dqt[:, g * W:(g + 1) * W]
                dks.append(_dot_nt(dsb, qt))
                dvs.append(_dot_nt(pn.astype(MXU_DTYPE), dot_h))
            dkb = jnp.concatenate(dks, axis=1)
            dvb = jnp.concatenate(dvs, axis=1)
            add_rows((n * nb + b) * W, dkb[W:], dvb[W:])
            if b:
                add_rows((n * nb + b - 1) * W, dkb[:W], dvb[:W])
            else:
                @pl.when(n > 0)
                def _():
                    add_rows((n * nb - 1) * W, dkb[:W], dvb[:W])

    prev = lambda n: (jnp.maximum(n * nb - 1, 0), 0)
    cur = lambda n: (n, 0)
    col = lambda n: (0, n)
    return pl.pallas_call(
        body, name="swa_attn_bwd", grid=(T // (nb * W),),
        in_specs=[pl.BlockSpec((D, nb * W), col), pl.BlockSpec((W, nk), prev), pl.BlockSpec((nb * W, nk), cur),
                  pl.BlockSpec((W, nk), prev), pl.BlockSpec((nb * W, nk), cur), pl.BlockSpec((D, nb * W), col),
                  _full(bias.shape), _full(sink_rows.shape)],
        out_specs=[pl.BlockSpec((D, nb * W), col), _full((T, nk)), _full((T, nk)), _full(sink_rows.shape)],
        out_shape=[jax.ShapeDtypeStruct((D, T), jnp.float32), jax.ShapeDtypeStruct((T, nk), jnp.float32),
                   jax.ShapeDtypeStruct((T, nk), jnp.float32), jax.ShapeDtypeStruct(sink_rows.shape, jnp.float32)],
        compiler_params=_params("arbitrary"),
    )(qt, k, k, v, v, dot_, bias, sink_rows)


def _swa_pre_bwd(x, dxm, vec, dq, dk, dv, w_qkv):
    T = x.shape[0]
    tm = _tile(T, 512)
    nq = SWA_HEADS * SWA_HEAD_DIM
    nk = SWA_KV_HEADS * SWA_HEAD_DIM
    nqkv = nq + 2 * nk

    def body(x_ref, dxm_ref, vec_ref, dq_ref, dk_ref, dv_ref, w_ref, dx_ref, dqkv_ref, ps_ref, db_ref):
        @pl.when(pl.program_id(0) == 0)
        def _():
            ps_ref[...] = jnp.zeros_like(ps_ref)
            db_ref[...] = jnp.zeros_like(db_ref)

        dqkv = jnp.concatenate([dq_ref[...], dk_ref[...], dv_ref[...]], axis=1)
        db_ref[...] += _rowsum(dqkv)
        dqkv_b = dqkv.astype(MXU_DTYPE)
        dqkv_ref[...] = dqkv_b
        dh = _dot_nt(dqkv_b, w_ref[...])
        dx_ref[...] = _modulate_bwd(dh, x_ref[...], vec_ref, R_GMIX, R_SC1, R_SH1, ps_ref, dxm_ref[...])

    return pl.pallas_call(
        body, name="swa_pre_bwd", grid=(T // tm,),
        in_specs=[_rows(tm, D), _rows(tm, D), _full((8, D)), _rows(tm, nq), _rows(tm, nk), _rows(tm, nk),
                  _full(w_qkv.shape)],
        out_specs=[_rows(tm, D), _rows(tm, nqkv), _full((8, D)), _full((1, nqkv))],
        out_shape=[jax.ShapeDtypeStruct((T, D), jnp.float32), jax.ShapeDtypeStruct((T, nqkv), MXU_DTYPE),
                   jax.ShapeDtypeStruct((8, D), jnp.float32), jax.ShapeDtypeStruct((1, nqkv), jnp.float32)],
        compiler_params=_params("arbitrary"),
    )(x, dxm, vec, dq, dk, dv, w_qkv)


def _rot_cols(w):
    half = QK_ROPE // 2
    return jnp.concatenate([-w[..., half:], w[..., :half]], axis=-1)


def _unrot_grad(d_rope, d_rot):
    half = QK_ROPE // 2
    return d_rope + jnp.concatenate([d_rot[..., half:], -d_rot[..., :half]], axis=-1)


def _rope_table(positions):
    half = QK_ROPE // 2
    inv_freq = ROPE_THETA ** (-jnp.arange(half, dtype=jnp.float32) / half)
    ang = positions.astype(jnp.float32)[:, None] * inv_freq
    cos, sin = jnp.cos(ang), jnp.sin(ang)
    return jnp.concatenate([cos, cos, sin, sin], axis=1)


def _sequence_step(x, tgt, positions, vecs, g_q, g_kv, sinks, g_final, wts, late_weights, on_late_grads, on_late_landed):
    H = MLA_HEADS
    cs = _rope_table(positions)
    w_dkv = wts["mla_w_dkv"]
    wcat = jnp.concatenate([wts["mla_w_dq"], w_dkv, _rot_cols(w_dkv[:, KV_LORA:])], axis=1)
    uq = wts["mla_w_uq"].reshape(Q_LORA, H, QK_NOPE + QK_ROPE)
    wuq = jnp.concatenate([uq, _rot_cols(uq[..., QK_NOPE:])], axis=-1).transpose(1, 0, 2)
    wukv = wts["mla_w_ukv"].reshape(KV_LORA, H, QK_NOPE + V_DIM).transpose(1, 0, 2)
    zero_bias = jnp.zeros((1, D), jnp.float32)
    bias = _swa_bias()
    sink_rows = jnp.broadcast_to(sinks.reshape(SWA_KV_HEADS, 1, SWA_GROUP, 1),
                                 (SWA_KV_HEADS, 1, SWA_GROUP, WINDOW)).reshape(SWA_KV_HEADS, 1, SWA_GROUP * WINDOW)

    h1a, z, q, k, v = _mla_pre(x, vecs[0], wcat, g_q, g_kv, wuq, wukv, cs)
    o_a, lse = _mla_attn_fwd(q, k, v)
    y1a, xm_a, h2a = _post_attn(o_a, x, wts["mla_w_o"], zero_bias, vecs[0])
    wts = {**wts, **late_weights(h2a)}
    a_a, y2a, x1 = _mlp_fwd(h2a, wts["w_ff1"][0], wts["w_ff2"][0], xm_a, vecs[0])

    h1b, qs_t, ks, vs = _swa_pre(x1, vecs[1], wts["swa_w_qkv"], wts["swa_b_qkv"])
    o_bt = _swa_attn_fwd(qs_t, ks, vs, bias, sink_rows)
    y1b, xm_b, h2b = _post_attn(o_bt, x1, wts["swa_w_o"], wts["swa_b_o"], vecs[1], o_transposed=True)
    a_b, y2b, x2 = _mlp_fwd(h2b, wts["w_ff1"][1], wts["w_ff2"][1], xm_b, vecs[1])

    loss8, dx2, dg_final = _final_loss(x2, tgt, g_final.reshape(1, D))

    du_b, dy2b, dxm_b, ps_mlp_b = _mlp_bwd(dx2, y2b, a_b, wts["w_ff1"][1], wts["w_ff2"][1], xm_b, vecs[1])
    g_ff2 = _mm_tn(a_b, dy2b, "dw_ff2_l1", "rows", DEPTH, 1)
    g_ff1 = _mm_tn(h2b, du_b, "dw_ff1_l1", "cols", DEPTH, 1)
    dy1b, do_bt, ps_out_b = _attn_out_bwd(dxm_b, y1b, None, wts["swa_w_o"], vecs[1], False)
    g_swa_o = _mm_tn(o_bt, dy1b, "dw_o_swa", a_transposed=True)
    dqs_t, dks, dvs, dsinks = _swa_attn_bwd(qs_t, ks, vs, do_bt, bias, sink_rows)
    dqs = dqs_t.T
    dx1, dqkv, ps_pre_b, g_swa_bqkv = _swa_pre_bwd(x1, dxm_b, vecs[1], dqs, dks, dvs, wts["swa_w_qkv"])
    g_swa_qkv = _mm_tn(h1b, dqkv, "dw_qkv", "cols")

    du_a, dy2a, dxm_a, ps_mlp_a = _mlp_bwd(dx1, y2a, a_a, wts["w_ff1"][0], wts["w_ff2"][0], xm_a, vecs[0])
    g_ff2 = _mm_tn(a_a, dy2a, "dw_ff2_l0", "rows", DEPTH, 0, g_ff2)
    g_ff1 = _mm_tn(h2a, du_a, "dw_ff1_l0", "cols", DEPTH, 0, g_ff1)
    rows4 = lambda g: g.reshape(N_CHIPS, g.shape[0] // N_CHIPS, g.shape[1])
    token = on_late_grads({
        "swa_w_qkv": g_swa_qkv.reshape(N_CHIPS, D, -1), "swa_w_o": rows4(g_swa_o),
        "w_ff1": g_ff1.reshape(N_CHIPS, DEPTH * D, -1), "w_ff2": g_ff2.reshape(N_CHIPS, -1, D)})
    dy1a, do_a, ps_out_a, delta = _attn_out_bwd(dxm_a, y1a, o_a, wts["mla_w_o"], vecs[0] + token[0, 0], True)
    g_mla_o = _mm_tn(o_a, dy1a, "dw_o_mla")
    token = on_late_landed(g_mla_o)
    dq, dk, dv = _mla_attn_bwd(q, k, v, do_a, lse, delta + token[0, 0])
    dx0, ps_pre_a, dg_q, dg_kv, dwcat, dwuq, dwukv = _mla_pre_bwd(
        x, dxm_a, vecs[0], h1a, z, dq, dk, dv, cs, wcat, g_q, g_kv, wuq, wukv)

    c0, c1, c2 = Q_LORA, Q_LORA + KV_LORA, Q_LORA + KV_LORA + QK_ROPE
    g_dq = dwcat[:, :c0]
    g_dkv = jnp.concatenate([dwcat[:, c0:c1], _unrot_grad(dwcat[:, c1:c2], dwcat[:, c2:])], axis=1)
    e0 = QK_NOPE + QK_ROPE
    g_uq = jnp.concatenate([dwuq[..., :QK_NOPE], _unrot_grad(dwuq[..., QK_NOPE:e0], dwuq[..., e0:])], axis=-1)
    per = H // N_CHIPS
    g_uq = g_uq.reshape(N_CHIPS, per, Q_LORA, e0).transpose(0, 2, 1, 3).reshape(N_CHIPS, Q_LORA, per * e0)
    g_ukv = dwukv.reshape(N_CHIPS, per, KV_LORA, QK_NOPE + V_DIM).transpose(0, 2, 1, 3)
    g_ukv = g_ukv.reshape(N_CHIPS, KV_LORA, per * (QK_NOPE + V_DIM))

    def dmod(ps_pre, ps_out, ps_mlp):
        return jnp.concatenate([ps_pre[R_SH1:R_SC1 + 1], ps_out[R_GT1:R_GT1 + 1], ps_mlp[R_SH2:R_GT2 + 1]], axis=0)

    grads = {"mla_w_dq": rows4(g_dq), "mla_w_uq": g_uq, "mla_w_dkv": rows4(g_dkv), "mla_w_ukv": g_ukv,
             "mla_w_o": rows4(g_mla_o)}
    small = {
        "dmod": jnp.stack([dmod(ps_pre_a, ps_out_a, ps_mlp_a), dmod(ps_pre_b, ps_out_b, ps_mlp_b)]).reshape(DEPTH, 6 * D),
        "g_mix": jnp.stack([ps_pre_a[R_GMIX], ps_pre_b[R_GMIX]]),
        "g_mlp": jnp.stack([ps_mlp_a[R_GMLP], ps_mlp_b[R_GMLP]]),
        "mla_g_q": dg_q, "mla_g_kv": dg_kv, "swa_sinks": jnp.sum(dsinks.reshape(SWA_HEADS, WINDOW), axis=1).reshape(1, SWA_HEADS),
        "swa_b_qkv": g_swa_bqkv, "swa_b_o": ps_out_b[R_BO:R_BO + 1],
        "g_final": dg_final.reshape(D), "loss": loss8[0, 0],
    }
    return dx0, grads, small


SHARDED = {
    "mla_w_dq": (1, D // N_CHIPS, Q_LORA),
    "mla_w_uq": (1, Q_LORA, MLA_HEADS * (QK_NOPE + QK_ROPE) // N_CHIPS),
    "mla_w_dkv": (1, D // N_CHIPS, KV_LORA + QK_ROPE),
    "mla_w_ukv": (1, KV_LORA, MLA_HEADS * (QK_NOPE + V_DIM) // N_CHIPS),
    "mla_w_o": (1, MLA_HEADS * V_DIM // N_CHIPS, D),
    "swa_w_qkv": (1, D, (SWA_HEADS + 2 * SWA_KV_HEADS) * SWA_HEAD_DIM // N_CHIPS),
    "swa_w_o": (1, SWA_HEADS * SWA_HEAD_DIM // N_CHIPS, D),
    "w_ff1": (DEPTH, D, D_FF // N_CHIPS),
    "w_ff2": (DEPTH, D_FF // N_CHIPS, D),
}
COL_SPLIT = ("mla_w_uq", "mla_w_ukv", "swa_w_qkv")
BIASES = {"swa_b_qkv": (SWA_HEADS + 2 * SWA_KV_HEADS) * SWA_HEAD_DIM, "swa_b_o": D}


def _view2d(name):
    shape = SHARDED[name]
    return math.prod(shape[:-1]), shape[-1]


SMALL = {"b_ada": (DEPTH, 6 * D), "g_mix": (DEPTH, D), "g_mlp": (DEPTH, D), "mla_g_q": (1, Q_LORA),
         "mla_g_kv": (1, KV_LORA), "swa_sinks": (1, SWA_HEADS), "g_final": (D,), "loss": (),
         "swa_b_qkv": (1, BIASES["swa_b_qkv"]), "swa_b_o": (1, BIASES["swa_b_o"])}
SMALL_ROWS = 192
DMA_ROWS = 256


SLOT_ROWS = 8


def _small_slots():
    slots, off = {}, 0
    for name, shape in SMALL.items():
        n = max(math.prod(shape), 1)
        slots[name] = (off, n)
        off += -(-n // (SLOT_ROWS * LANES)) * SLOT_ROWS * LANES
    assert off <= SMALL_ROWS * LANES
    return slots


def _pack_small(vals):
    parts, end = [], 0
    for name, (off, n) in _small_slots().items():
        pad = -(-n // (SLOT_ROWS * LANES)) * SLOT_ROWS * LANES - n
        v = vals[name].astype(jnp.float32).reshape(-1) if name in vals else jnp.zeros((n,), jnp.float32)
        parts += [v, jnp.zeros((pad,), jnp.float32)]
        end = off + n + pad
    parts.append(jnp.zeros((SMALL_ROWS * LANES - end,), jnp.float32))
    return jnp.concatenate(parts).reshape(SMALL_ROWS, LANES)


def _from_slot(name, rows):
    n = max(math.prod(SMALL[name]), 1)
    return rows.reshape(-1)[:n].reshape(SMALL[name])


def _pieces(rows):
    return [(off, min(DMA_ROWS, rows - off)) for off in range(0, rows, DMA_ROWS)]


HBM = pl.BlockSpec(memory_space=pltpu.HBM)
MESH = pl.DeviceIdType.MESH


def _place():
    x, y, c = lax.axis_index("x"), lax.axis_index("y"), lax.axis_index("c")
    chips = [(1 - x, y), (x, 1 - y), (1 - x, 1 - y)]
    return x, y, c, chips


def _all_gather(block):
    m_per, n = block.shape

    def body(x_ref, out_ref, send_sems, recv_sems, local_sem):
        x, y, c, chips = _place()
        me, sibling = (x, y, c), (x, y, 1 - c)

        def rows(px, py, pc):
            return out_ref.at[pl.ds((4 * px + 2 * py + pc) * m_per, m_per), :]

        def copy(k, blk, to, src=None):
            return pltpu.make_async_remote_copy(
                src_ref=rows(*blk) if src is None else src, dst_ref=rows(*blk),
                send_sem=send_sems.at[k], recv_sem=recv_sems.at[k], device_id=to, device_id_type=MESH)

        mine = pltpu.make_async_copy(x_ref, rows(*me), local_sem)
        mine.start()
        first = [copy(0, me, sibling, src=x_ref)]
        first += [copy(1 + j, me, (*chip, c), src=x_ref) for j, chip in enumerate(chips)]
        for cp in first:
            cp.start()
        passed = [copy(4 + j, (*chip, c), sibling) for j, chip in enumerate(chips)]
        for j, chip in enumerate(chips):
            copy(1 + j, (*chip, c), me).wait_recv()
            passed[j].start()
        copy(0, sibling, me).wait_recv()
        for j, chip in enumerate(chips):
            copy(4 + j, (*chip, 1 - c), me).wait_recv()
        for cp in first + passed:
            cp.wait_send()
        mine.wait()

    out = pl.pallas_call(
        body, name="all_gather_small",
        out_shape=jax.ShapeDtypeStruct((N_DEV * m_per, n), block.dtype),
        in_specs=[pl.BlockSpec(memory_space=pltpu.VMEM)],
        out_specs=pl.BlockSpec(memory_space=pltpu.VMEM),
        scratch_shapes=[pltpu.SemaphoreType.DMA((7,)), pltpu.SemaphoreType.DMA((7,)), pltpu.SemaphoreType.DMA],
    )(block)
    return out.reshape(N_DEV, m_per, n)


def _weight_gather(shards):
    nt = len(shards)

    def body(*refs):
        w_refs, out_refs = refs[:nt], refs[nt:2 * nt]
        send_sems, recv_sems = refs[2 * nt:]
        x, y, c, chips = _place()
        sibling = (x, y, 1 - c)

        def slab(t, px, py, half):
            rh = shards[t].shape[0] // 2
            return out_refs[t].at[2 * px + py, pl.ds(half * rh, rh), :]

        def copy(t, k, src, dst, to):
            return pltpu.make_async_remote_copy(src_ref=src, dst_ref=dst, send_sem=send_sems.at[6 * t + k],
                                                recv_sem=recv_sems.at[6 * t + k], device_id=to, device_id_type=MESH)

        first = []
        for t in range(nt):
            rh = shards[t].shape[0] // 2
            first += [copy(t, j, w_refs[t].at[pl.ds(c * rh, rh), :], slab(t, x, y, c), (*chip, c))
                      for j, chip in enumerate(chips)]
        for cp in first:
            cp.start()
        passed = []
        for t in range(nt):
            for j, chip in enumerate(chips):
                copy(t, j, slab(t, *chip, c), slab(t, *chip, c), (*chip, c)).wait_recv()
                rh = shards[t].shape[0] // 2
                for off, n in _pieces(rh):
                    piece = out_refs[t].at[2 * chip[0] + chip[1], pl.ds(c * rh + off, n), :]
                    copy(t, 3 + j, piece, piece, sibling).start()
                passed.append(copy(t, 3 + j, slab(t, *chip, c), slab(t, *chip, c), sibling))
        for t in range(nt):
            for j, chip in enumerate(chips):
                copy(t, 3 + j, slab(t, *chip, 1 - c), slab(t, *chip, 1 - c), sibling).wait_recv()
        for cp in first + passed:
            cp.wait_send()

    return pl.pallas_call(
        body, name="weight_gather",
        out_shape=[jax.ShapeDtypeStruct((N_CHIPS,) + s.shape, s.dtype) for s in shards],
        in_specs=[HBM] * nt, out_specs=[HBM] * nt,
        scratch_shapes=[pltpu.SemaphoreType.DMA((6 * nt,)), pltpu.SemaphoreType.DMA((6 * nt,))],
    )(*shards)


SEM = pl.BlockSpec(memory_space=pltpu.SEMAPHORE)
ANY = pl.BlockSpec(memory_space=pl.ANY)
SPLIT_COPY = pltpu.SideEffectType.DATAFLOW_SIDE_EFFECTING


def _late_copies(w_refs, land_refs, send_sems, recv_sems):
    x, y, c, chips = _place()
    return [pltpu.make_async_remote_copy(
        src_ref=w_refs[t], dst_ref=land_refs[t].at[2 * x + y], send_sem=send_sems.at[3 * t + j],
        recv_sem=recv_sems.at[3 * t + j], device_id=(cx, cy, c), device_id_type=MESH)
        for t in range(len(w_refs)) for j, (cx, cy) in enumerate(chips)], chips


def _late_gather_start(shards, after):
    nt, na = len(shards), len(after)

    def body(*refs):
        w_refs, land_refs = refs[:nt], refs[nt:2 * nt]
        send_sems, recv_sems, token = refs[2 * nt + na], refs[2 * nt + na + 1], refs[-1]
        copies, _ = _late_copies(w_refs, land_refs, send_sems, recv_sems)
        for cp in copies:
            cp.start()
        token[...] = jnp.zeros_like(token)

    hbm = lambda a: pltpu.with_memory_space_constraint(a, pltpu.HBM)
    lands = [lax.empty((N_CHIPS,) + s.shape, s.dtype) for s in shards]
    outs = pl.pallas_call(
        body, name="late_gather_start",
        out_shape=(pltpu.SemaphoreType.DMA((3 * nt,)), pltpu.SemaphoreType.DMA((3 * nt,)),
                   *[pltpu.HBM(s.shape, s.dtype) for s in shards], *[pltpu.HBM(l.shape, l.dtype) for l in lands],
                   jax.ShapeDtypeStruct((8, LANES), jnp.float32)),
        in_specs=[HBM] * (2 * nt) + [ANY] * na,
        out_specs=(SEM, SEM, *([HBM] * (2 * nt)), pl.BlockSpec(memory_space=pltpu.VMEM)),
        input_output_aliases={i: 2 + i for i in range(2 * nt)},
        compiler_params=pltpu.CompilerParams(has_side_effects=SPLIT_COPY),
    )(*[hbm(s) for s in shards], *[hbm(l) for l in lands], *after)
    return outs[0], outs[1], list(outs[2:2 + nt]), list(outs[2 + nt:2 + 2 * nt]), outs[-1]


def _late_gather_wait(send_sems, recv_sems, shards, lands, after):
    nt = len(shards)

    def body(*refs):
        w_refs, land_refs = refs[:nt], refs[nt:2 * nt]
        s_sems, r_sems = refs[2 * nt], refs[2 * nt + 1]
        x, y, c, chips = _place()
        for t in range(nt):
            for j, (cx, cy) in enumerate(chips):
                cp = pltpu.make_async_remote_copy(
                    src_ref=w_refs[t], dst_ref=land_refs[t].at[2 * cx + cy], send_sem=s_sems.at[3 * t + j],
                    recv_sem=r_sems.at[3 * t + j], device_id=(cx, cy, c), device_id_type=MESH)
                cp.wait_send()
                cp.wait_recv()

    outs = pl.pallas_call(
        body, name="late_gather_wait",
        out_shape=(*[pltpu.HBM(s.shape, s.dtype) for s in shards], *[pltpu.HBM(l.shape, l.dtype) for l in lands]),
        in_specs=[HBM] * (2 * nt) + [SEM, SEM, ANY], out_specs=tuple([HBM] * (2 * nt)),
        input_output_aliases={i: i for i in range(2 * nt)},
        compiler_params=pltpu.CompilerParams(has_side_effects=SPLIT_COPY),
    )(*shards, *lands, send_sems, recv_sems, after)
    return list(outs[nt:])


def _grad_pair_in(grads):
    nt = len(grads)

    def body(*refs):
        g_refs, got_refs = refs[:nt], refs[nt:2 * nt]
        send_sems, recv_sems = refs[2 * nt:]
        x, y, c, _ = _place()
        sibling = (x, y, 1 - c)

        def copy(t, src, dst):
            return pltpu.make_async_remote_copy(src_ref=src, dst_ref=dst, send_sem=send_sems.at[t],
                                                recv_sem=recv_sems.at[t], device_id=sibling, device_id_type=MESH)

        for t in range(nt):
            rh = grads[t].shape[1] // 2
            for p in range(N_CHIPS):
                for off, n in _pieces(rh):
                    copy(t, g_refs[t].at[p, pl.ds((1 - c) * rh + off, n), :], got_refs[t].at[p, pl.ds(off, n), :]).start()
        for t in range(nt):
            rh = grads[t].shape[1] // 2
            copy(t, g_refs[t].at[:, pl.ds((1 - c) * rh, rh), :], got_refs[t]).wait()

    return pl.pallas_call(
        body, name="grad_pair_in",
        out_shape=[jax.ShapeDtypeStruct((N_CHIPS, g.shape[1] // 2, g.shape[2]), g.dtype) for g in grads],
        in_specs=[HBM] * nt, out_specs=[HBM] * nt,
        scratch_shapes=[pltpu.SemaphoreType.DMA((nt,)), pltpu.SemaphoreType.DMA((nt,))],
    )(*grads)


def _pair_in_start(grads):
    nt = len(grads)

    def body(*refs):
        g_refs, land_refs = refs[:nt], refs[nt:2 * nt]
        send_sems, recv_sems, token = refs[2 * nt], refs[2 * nt + 1], refs[-1]
        x, y, c, _ = _place()
        for t in range(nt):
            rh = grads[t].shape[1] // 2
            for p in range(N_CHIPS):
                for off, n in _pieces(rh):
                    pltpu.make_async_remote_copy(
                        src_ref=g_refs[t].at[p, pl.ds((1 - c) * rh + off, n), :], dst_ref=land_refs[t].at[p, pl.ds(off, n), :],
                        send_sem=send_sems.at[t], recv_sem=recv_sems.at[t], device_id=(x, y, 1 - c),
                        device_id_type=MESH).start()
        token[...] = jnp.zeros_like(token)

    hbm = lambda a: pltpu.with_memory_space_constraint(a, pltpu.HBM)
    lands = [lax.empty((N_CHIPS, g.shape[1] // 2, g.shape[2]), g.dtype) for g in grads]
    outs = pl.pallas_call(
        body, name="grad_pair_in_start",
        out_shape=(pltpu.SemaphoreType.DMA((nt,)), pltpu.SemaphoreType.DMA((nt,)),
                   *[pltpu.HBM(g.shape, g.dtype) for g in grads], *[pltpu.HBM(l.shape, l.dtype) for l in lands],
                   jax.ShapeDtypeStruct((8, LANES), jnp.float32)),
        in_specs=[HBM] * (2 * nt),
        out_specs=(SEM, SEM, *([HBM] * (2 * nt)), pl.BlockSpec(memory_space=pltpu.VMEM)),
        input_output_aliases={i: 2 + i for i in range(2 * nt)},
        compiler_params=pltpu.CompilerParams(has_side_effects=SPLIT_COPY),
    )(*[hbm(g) for g in grads], *[hbm(l) for l in lands])
    return outs[0], outs[1], list(outs[2:2 + nt]), list(outs[2 + nt:2 + 2 * nt]), outs[-1]


def _pair_in_wait(send_sems, recv_sems, grads, lands, after):
    nt = len(grads)

    def body(*refs):
        g_refs, land_refs = refs[:nt], refs[nt:2 * nt]
        s_sems, r_sems = refs[2 * nt], refs[2 * nt + 1]
        x, y, c, _ = _place()
        for t in range(nt):
            rh = grads[t].shape[1] // 2
            cp = pltpu.make_async_remote_copy(
                src_ref=g_refs[t].at[:, pl.ds((1 - c) * rh, rh), :], dst_ref=land_refs[t], send_sem=s_sems.at[t],
                recv_sem=r_sems.at[t], device_id=(x, y, 1 - c), device_id_type=MESH)
            cp.wait_send()
            cp.wait_recv()

    outs = pl.pallas_call(
        body, name="grad_pair_in_wait",
        out_shape=(*[pltpu.HBM(g.shape, g.dtype) for g in grads], *[pltpu.HBM(l.shape, l.dtype) for l in lands]),
        in_specs=[HBM] * (2 * nt) + [SEM, SEM, ANY], out_specs=tuple([HBM] * (2 * nt)),
        input_output_aliases={i: i for i in range(2 * nt)},
        compiler_params=pltpu.CompilerParams(has_side_effects=SPLIT_COPY),
    )(*grads, *lands, send_sems, recv_sems, after)
    return list(outs[:nt]), list(outs[nt:])


def _pair_sum(g, got, core, name):
    _, rows, cols = g.shape
    rh = rows // 2
    tr = _tile(rh, 512)
    nb = rh // tr

    def body(c_ref, g_ref, got_ref, s32_ref, s16_ref):
        s = g_ref[...] + got_ref[...]
        s32_ref[...] = s
        s16_ref[...] = s.astype(s16_ref.dtype)

    blk = pl.BlockSpec((None, tr, cols), lambda p, i, c_ref: (p, i, 0))
    return pl.pallas_call(
        body, name=name,
        grid_spec=pltpu.PrefetchScalarGridSpec(
            num_scalar_prefetch=1, grid=(N_CHIPS, nb),
            in_specs=[pl.BlockSpec((None, tr, cols), lambda p, i, c_ref: (p, c_ref[0] * nb + i, 0)), blk],
            out_specs=[blk, blk]),
        out_shape=[jax.ShapeDtypeStruct((N_CHIPS, rh, cols), jnp.float32),
                   jax.ShapeDtypeStruct((N_CHIPS, rh, cols), jnp.bfloat16)],
        compiler_params=_params("parallel", "parallel"),
    )(core, g, got)


def _exchange_start(parts, name):
    nt = len(parts)

    def body(*refs):
        a_refs, land_refs = refs[:nt], refs[nt:2 * nt]
        send_sems, recv_sems, token = refs[2 * nt], refs[2 * nt + 1], refs[-1]
        x, y, c, chips = _place()
        for t in range(nt):
            for j, (cx, cy) in enumerate(chips):
                pltpu.make_async_remote_copy(
                    src_ref=a_refs[t].at[2 * cx + cy], dst_ref=land_refs[t].at[j], send_sem=send_sems.at[3 * t + j],
                    recv_sem=recv_sems.at[3 * t + j], device_id=(cx, cy, c), device_id_type=MESH).start()
        token[...] = jnp.zeros_like(token)

    hbm = lambda a: pltpu.with_memory_space_constraint(a, pltpu.HBM)
    lands = [lax.empty((N_CHIPS - 1,) + a.shape[1:], a.dtype) for a in parts]
    outs = pl.pallas_call(
        body, name=name,
        out_shape=(pltpu.SemaphoreType.DMA((3 * nt,)), pltpu.SemaphoreType.DMA((3 * nt,)),
                   *[pltpu.HBM(a.shape, a.dtype) for a in parts], *[pltpu.HBM(l.shape, l.dtype) for l in lands],
                   jax.ShapeDtypeStruct((8, LANES), jnp.float32)),
        in_specs=[HBM] * (2 * nt),
        out_specs=(SEM, SEM, *([HBM] * (2 * nt)), pl.BlockSpec(memory_space=pltpu.VMEM)),
        input_output_aliases={i: 2 + i for i in range(2 * nt)},
        compiler_params=pltpu.CompilerParams(has_side_effects=SPLIT_COPY),
    )(*[hbm(a) for a in parts], *[hbm(l) for l in lands])
    return outs[0], outs[1], list(outs[2:2 + nt]), list(outs[2 + nt:2 + 2 * nt]), outs[-1]


def _exchange_wait(send_sems, recv_sems, parts, lands, after, name):
    nt = len(parts)

    def body(*refs):
        a_refs, land_refs = refs[:nt], refs[nt:2 * nt]
        s_sems, r_sems = refs[2 * nt], refs[2 * nt + 1]
        x, y, c, chips = _place()
        for t in range(nt):
            for j, (cx, cy) in enumerate(chips):
                cp = pltpu.make_async_remote_copy(
                    src_ref=a_refs[t].at[2 * cx + cy], dst_ref=land_refs[t].at[j], send_sem=s_sems.at[3 * t + j],
                    recv_sem=r_sems.at[3 * t + j], device_id=(cx, cy, c), device_id_type=MESH)
                cp.wait_send()
                cp.wait_recv()

    outs = pl.pallas_call(
        body, name=name,
        out_shape=(*[pltpu.HBM(a.shape, a.dtype) for a in parts], *[pltpu.HBM(l.shape, l.dtype) for l in lands]),
        in_specs=[HBM] * (2 * nt) + [SEM, SEM, ANY], out_specs=tuple([HBM] * (2 * nt)),
        input_output_aliases={i: i for i in range(2 * nt)},
        compiler_params=pltpu.CompilerParams(has_side_effects=SPLIT_COPY),
    )(*parts, *lands, send_sems, recv_sems, after)
    return list(outs[nt:])


def _chip_sum(s32, got, chip, name, behind=None):
    _, rh, cols = s32.shape
    tr = _tile(rh, 512)

    def body(p_ref, s_ref, got_ref, *refs):
        acc = s_ref[...]
        for j in range(N_CHIPS - 1):
            acc = acc + got_ref[j].astype(jnp.float32)
        refs[-1][...] = acc

    extra = [] if behind is None else [behind]
    return pl.pallas_call(
        body, name=name,
        grid_spec=pltpu.PrefetchScalarGridSpec(
            num_scalar_prefetch=1, grid=(rh // tr,),
            in_specs=[pl.BlockSpec((None, tr, cols), lambda i, p_ref: (p_ref[0], i, 0)),
                      pl.BlockSpec((N_CHIPS - 1, tr, cols), lambda i, p_ref: (0, i, 0))]
            + [pl.BlockSpec((8, LANES), lambda i, p_ref: (0, 0))] * len(extra),
            out_specs=pl.BlockSpec((tr, cols), lambda i, p_ref: (i, 0))),
        out_shape=jax.ShapeDtypeStruct((rh, cols), jnp.float32),
        compiler_params=_params("parallel"),
    )(chip, s32, got, *extra)


def _grad_pair_out(halves):
    nt = len(halves)

    def body(*refs):
        h_refs, got_refs = refs[:nt], refs[nt:2 * nt]
        send_sems, recv_sems = refs[2 * nt:]
        x, y, c, _ = _place()
        sibling = (x, y, 1 - c)

        def copy(t, src, dst):
            return pltpu.make_async_remote_copy(src_ref=src, dst_ref=dst, send_sem=send_sems.at[t],
                                                recv_sem=recv_sems.at[t], device_id=sibling, device_id_type=MESH)

        for t in range(nt):
            for off, n in _pieces(halves[t].shape[0]):
                copy(t, h_refs[t].at[pl.ds(off, n), :], got_refs[t].at[pl.ds(off, n), :]).start()
        for t in range(nt):
            copy(t, h_refs[t], got_refs[t]).wait()

    return pl.pallas_call(
        body, name="grad_pair_out",
        out_shape=[jax.ShapeDtypeStruct(h.shape, h.dtype) for h in halves],
        in_specs=[HBM] * nt, out_specs=[HBM] * nt,
        scratch_shapes=[pltpu.SemaphoreType.DMA((nt,)), pltpu.SemaphoreType.DMA((nt,))],
    )(*halves)


def _ada_part(c_all, w_ada):
    L, _, ncol = w_ada.shape
    tn = _tile(ncol, 512)

    def body(c_ref, w_ref, cond_ref, part_ref):
        cv = c_ref[...]
        cond = cv * jax.nn.sigmoid(cv)
        cond_ref[...] = cond
        part_ref[0] = jnp.dot(cond, w_ref[0], precision=lax.Precision.HIGHEST, preferred_element_type=jnp.float32)

    return pl.pallas_call(
        body, name="ada_part", grid=(L, ncol // tn),
        in_specs=[_full((N_DEV, D)), pl.BlockSpec((1, D, tn), lambda l, j: (l, 0, j))],
        out_specs=[_full((N_DEV, D)), pl.BlockSpec((1, N_DEV, tn), lambda l, j: (l, 0, j))],
        out_shape=[jax.ShapeDtypeStruct((N_DEV, D), jnp.float32), jax.ShapeDtypeStruct((L, N_DEV, ncol), jnp.float32)],
        compiler_params=_params("arbitrary", "arbitrary"),
    )(c_all, w_ada)


def _adamw_math(w, g, m, v):
    m = ADAM_B1 * m + (1.0 - ADAM_B1) * g
    v = ADAM_B2 * v + (1.0 - ADAM_B2) * jnp.square(g)
    m_hat = m / (1.0 - ADAM_B1 ** ADAM_STEP)
    v_hat = v / (1.0 - ADAM_B2 ** ADAM_STEP)
    delta = -ADAM_LR * (m_hat / (jnp.sqrt(v_hat) + ADAM_EPS) + ADAM_WD * w)
    return delta, m, v


def _adamw(w, g, m, v, name):
    shape = w.shape
    cols = shape[-1]
    rows = math.prod(shape[:-1])
    tr = _tile(rows, 512)
    two_d = lambda t: t.reshape(rows, cols)

    def body(w_ref, g_ref, m_ref, v_ref, d_ref, mo_ref, vo_ref):
        d_ref[...], mo_ref[...], vo_ref[...] = _adamw_math(w_ref[...], g_ref[...], m_ref[...], v_ref[...])

    out = jax.ShapeDtypeStruct((rows, cols), jnp.float32)
    outs = pl.pallas_call(
        body, name=name, grid=(rows // tr,), in_specs=[_rows(tr, cols)] * 4, out_specs=[_rows(tr, cols)] * 3,
        out_shape=[out, out, out], compiler_params=_params("parallel"),
    )(two_d(w), two_d(g), two_d(m), two_d(v))
    return [t.reshape(shape) for t in outs]


def _adamw_halves(w, mine, got, m, v, core, name):
    shape = w.shape
    cols = shape[-1]
    rows = math.prod(shape[:-1])
    rh = rows // 2
    tr = _tile(rh, 512)
    nbh = rh // tr
    two_d = lambda t: t.reshape(rows, cols)

    def body(c_ref, w_ref, a_ref, b_ref, m_ref, v_ref, g_ref, d_ref, mo_ref, vo_ref):
        g = jnp.where(pl.program_id(0) // nbh == c_ref[0], a_ref[...], b_ref[...])
        g_ref[...] = g
        d_ref[...], mo_ref[...], vo_ref[...] = _adamw_math(w_ref[...], g, m_ref[...], v_ref[...])

    row = pl.BlockSpec((tr, cols), lambda i, c_ref: (i, 0))

    def half(keep):
        return pl.BlockSpec((tr, cols), lambda i, c_ref: (jnp.where((i // nbh == c_ref[0]) == keep, i % nbh, 0), 0))

    out = jax.ShapeDtypeStruct((rows, cols), jnp.float32)
    outs = pl.pallas_call(
        body, name=name,
        grid_spec=pltpu.PrefetchScalarGridSpec(
            num_scalar_prefetch=1, grid=(rows // tr,),
            in_specs=[row, half(True), half(False), row, row], out_specs=[row] * 4),
        out_shape=[out] * 4, compiler_params=_params("arbitrary"),
    )(core, two_d(w), mine, got, two_d(m), two_d(v))
    return [t.reshape(shape) for t in outs]


def _ada_grad_adamw(cond_t, dm, w, m, v):
    L, _, ncol = w.shape
    tn = _tile(ncol, 512)

    def body(ct_ref, dm_ref, w_ref, m_ref, v_ref, g_ref, d_ref, mo_ref, vo_ref):
        g = ct_ref[:, 0:1] * dm_ref[0, 0:1, :]
        for b in range(1, N_DEV):
            g = g + ct_ref[:, b:b + 1] * dm_ref[0, b:b + 1, :]
        g_ref[0] = g
        d_ref[0], mo_ref[0], vo_ref[0] = _adamw_math(w_ref[0], g, m_ref[0], v_ref[0])

    wblk = pl.BlockSpec((1, D, tn), lambda l, j: (l, 0, j))
    out = jax.ShapeDtypeStruct(w.shape, jnp.float32)
    return pl.pallas_call(
        body, name="ada_grad_adamw", grid=(L, ncol // tn),
        in_specs=[_full((D, N_DEV)), pl.BlockSpec((1, N_DEV, tn), lambda l, j: (l, 0, j)), wblk, wblk, wblk],
        out_specs=[wblk] * 4, out_shape=[out] * 4, compiler_params=_params("parallel", "parallel"),
    )(cond_t, dm, w, m, v)


def _small_adamw(gathered, w, m, v):
    slots = _small_slots()
    rows = {name: (off // LANES, -(-n // LANES)) for name, (off, n) in slots.items()}
    kinds = {name: 1 if name == "loss" or name in BIASES else 4 for name in slots}

    def body(ga_ref, w_ref, m_ref, v_ref, *out_refs):
        g = ga_ref[0]
        for dev in range(1, N_DEV):
            g = g + ga_ref[dev]
        d, mo, vo = _adamw_math(w_ref[...], g, m_ref[...], v_ref[...])
        k = 0
        for name, (r0, nr) in rows.items():
            for src in (g, d, mo, vo)[:kinds[name]]:
                out_refs[k][...] = src[r0:r0 + nr, :]
                k += 1

    out_shape = [jax.ShapeDtypeStruct((rows[name][1], LANES), jnp.float32) for name in slots for _ in range(kinds[name])]
    flat = pl.pallas_call(
        body, name="small_adamw", out_shape=out_shape,
        in_specs=[pl.BlockSpec(memory_space=pltpu.VMEM)] * 4,
        out_specs=[pl.BlockSpec(memory_space=pltpu.VMEM)] * len(out_shape),
    )(gathered, w, m, v)
    out, k = {}, 0
    for name in slots:
        out[name] = [_from_slot(name, t) for t in flat[k:k + kinds[name]]]
        k += kinds[name]
    return out


def _one_hot_pick(arr, index, axis):
    n = arr.shape[axis]
    shape = [1] * arr.ndim
    shape[axis] = n
    hot = (jnp.arange(n) == index).astype(arr.dtype).reshape(shape)
    return jnp.sum(arr * hot, axis=axis)


def kernel(x, c, positions, w_ada, b_ada, g_mix, g_mlp, mla_w_dq, mla_g_q, mla_w_uq, mla_w_dkv, mla_g_kv, mla_w_ukv, mla_w_o, swa_w_qkv, swa_b_qkv, swa_sinks, swa_w_o, swa_b_o, w_ff1, w_ff2, g_final, loss_target, m_w_ada, m_b_ada, m_g_mix, m_g_mlp, m_mla_w_dq, m_mla_g_q, m_mla_w_uq, m_mla_w_dkv, m_mla_g_kv, m_mla_w_ukv, m_mla_w_o, m_swa_w_qkv, m_swa_b_qkv, m_swa_sinks, m_swa_w_o, m_swa_b_o, m_w_ff1, m_w_ff2, m_g_final, v_w_ada, v_b_ada, v_g_mix, v_g_mlp, v_mla_w_dq, v_mla_g_q, v_mla_w_uq, v_mla_w_dkv, v_mla_g_kv, v_mla_w_ukv, v_mla_w_o, v_swa_w_qkv, v_swa_b_qkv, v_swa_sinks, v_swa_w_o, v_swa_b_o, v_w_ff1, v_w_ff2, v_g_final):
    W = dict(w_ada=w_ada, b_ada=b_ada, g_mix=g_mix, g_mlp=g_mlp, mla_w_dq=mla_w_dq, mla_g_q=mla_g_q, mla_w_uq=mla_w_uq,
             mla_w_dkv=mla_w_dkv, mla_g_kv=mla_g_kv, mla_w_ukv=mla_w_ukv, mla_w_o=mla_w_o, swa_w_qkv=swa_w_qkv,
             swa_b_qkv=swa_b_qkv, swa_sinks=swa_sinks, swa_w_o=swa_w_o, swa_b_o=swa_b_o, w_ff1=w_ff1, w_ff2=w_ff2,
             g_final=g_final)
    M = dict(w_ada=m_w_ada, b_ada=m_b_ada, g_mix=m_g_mix, g_mlp=m_g_mlp, mla_w_dq=m_mla_w_dq, mla_g_q=m_mla_g_q,
             mla_w_uq=m_mla_w_uq, mla_w_dkv=m_mla_w_dkv, mla_g_kv=m_mla_g_kv, mla_w_ukv=m_mla_w_ukv, mla_w_o=m_mla_w_o,
             swa_w_qkv=m_swa_w_qkv, swa_b_qkv=m_swa_b_qkv, swa_sinks=m_swa_sinks, swa_w_o=m_swa_w_o, swa_b_o=m_swa_b_o,
             w_ff1=m_w_ff1, w_ff2=m_w_ff2, g_final=m_g_final)
    V = dict(w_ada=v_w_ada, b_ada=v_b_ada, g_mix=v_g_mix, g_mlp=v_g_mlp, mla_w_dq=v_mla_w_dq, mla_g_q=v_mla_g_q,
             mla_w_uq=v_mla_w_uq, mla_w_dkv=v_mla_w_dkv, mla_g_kv=v_mla_g_kv, mla_w_ukv=v_mla_w_ukv, mla_w_o=v_mla_w_o,
             swa_w_qkv=v_swa_w_qkv, swa_b_qkv=v_swa_b_qkv, swa_sinks=v_swa_sinks, swa_w_o=v_swa_w_o, swa_b_o=v_swa_b_o,
             w_ff1=v_w_ff1, w_ff2=v_w_ff2, g_final=v_g_final)
    order = list(W)
    names = list(SHARDED)
    core = lax.axis_index("c")
    chip = 2 * lax.axis_index("x") + lax.axis_index("y")
    dev = 2 * chip + core
    core_arr = core.astype(jnp.int32).reshape(1)
    chip_arr = chip.astype(jnp.int32).reshape(1)

    def whole(n, g, own):
        g = lax.dynamic_update_slice(g, own[None], (chip, 0, 0))
        if n in ("w_ff1", "w_ff2"):
            return g
        if n in COL_SPLIT:
            return g.transpose(1, 0, 2).reshape(g.shape[1], N_CHIPS * g.shape[2])
        return g.reshape(N_CHIPS * g.shape[1], g.shape[2])

    early = [n for n in names if n.startswith("mla_")]
    local = {n: W[n].astype(MXU_DTYPE).reshape(_view2d(n)) for n in early}
    wts = {n: whole(n, g, local[n]) for n, g in zip(early, _weight_gather([local[n] for n in early]))}

    nbq, nbo = BIASES["swa_b_qkv"] // N_CHIPS, BIASES["swa_b_o"] // N_CHIPS
    first = jnp.concatenate([c.reshape(-1), swa_b_qkv.reshape(-1), swa_b_o.reshape(-1),
                             jnp.zeros((16 * LANES - D - nbq - nbo,), jnp.float32)]).reshape(16, LANES)
    first_all = _all_gather(first).reshape(N_DEV, 16 * LANES)
    c_all = first_all[:, :D]
    south = first_all[0::2]
    wts["swa_b_qkv"] = south[:, D:D + nbq].reshape(1, N_CHIPS * nbq)
    wts["swa_b_o"] = south[:, D + nbq:D + nbq + nbo].reshape(1, N_CHIPS * nbo)
    cond_all, part = _ada_part(c_all, w_ada)
    ncol = w_ada.shape[2]
    part_all = _all_gather(part.reshape(-1, LANES)).reshape(N_DEV, DEPTH, N_DEV, ncol)
    mine = _one_hot_pick(part_all[0::2], dev, axis=2)
    mod = mine.transpose(1, 0, 2).reshape(DEPTH, N_CHIPS * ncol) + b_ada
    vecs = jnp.concatenate([mod.reshape(DEPTH, 6, D), g_mix[:, None, :], g_mlp[:, None, :]], axis=1)

    late = [("w_ff1", 0), ("w_ff2", 0), ("swa_w_qkv", None), ("swa_w_o", None), ("w_ff1", 1), ("w_ff2", 1)]
    late_local = [(W[n][0] if l is None else W[n][l]).astype(MXU_DTYPE) for n, l in late]
    send_sems, recv_sems, passed, lands, token = _late_gather_start(late_local, [vecs] + [wts[n] for n in early])

    def late_weights(after):
        got = _late_gather_wait(send_sems, recv_sems, passed, lands, after)
        out = {"w_ff1": [None] * DEPTH, "w_ff2": [None] * DEPTH}
        for (n, l), g, own in zip(late, got, late_local):
            if l is None:
                out[n] = whole(n, g, own)
            else:
                out[n][l] = whole(n, g, own)
        return out

    late_names = [n for n in names if n not in early]
    reduce_state = {}

    def on_late_grads(late_grads):
        s_sems, r_sems, passed_g, zones, tok = _pair_in_start([late_grads[n] for n in late_names])
        reduce_state.update(pair=(s_sems, r_sems, passed_g, zones))
        return tok

    def on_late_landed(after):
        gl, got = _pair_in_wait(*reduce_state["pair"], after)
        sums = [_pair_sum(g, s, core_arr, "pair_sum_" + n) for n, g, s in zip(late_names, gl, got)]
        s_sems, r_sems, parts, zones, tok = _exchange_start([s16 for _, s16 in sums], "grad_exchange_start")
        reduce_state.update(sums=sums, split=(s_sems, r_sems, parts, zones))
        return tok

    grad_x, grads, small = _sequence_step(
        x[0], loss_target[0], positions[0], vecs, mla_g_q + token[0, 0], mla_g_kv, swa_sinks, g_final, wts,
        late_weights, on_late_grads, on_late_landed)

    small["b_ada"] = small.pop("dmod")
    small_all = _all_gather(_pack_small(small))
    pk = lambda src: _pack_small({n: src[n] for n in SMALL if n != "loss" and n not in BIASES})
    off, n = _small_slots()["b_ada"]
    dmod_all = small_all.reshape(N_DEV, -1)[:, off:off + n].reshape(N_DEV, DEPTH, N_CHIPS, ncol)
    dm = _one_hot_pick(dmod_all, chip, axis=2).transpose(1, 0, 2)

    gl = [grads[n] for n in early]
    got = _grad_pair_in(gl)
    sums = [_pair_sum(g, s, core_arr, "pair_sum_" + n) for n, g, s in zip(early, gl, got)]
    e_sems, e_rems, e_parts, e_zones, e_tok = _exchange_start([s16 for _, s16 in sums], "mla_exchange_start")

    def finish(tensor_names, sums, others, behind):
        halves = [_chip_sum(s32, o, chip_arr, "chip_sum_" + n, behind) for n, (s32, _), o in zip(tensor_names, sums, others)]
        return {n: _adamw_halves(W[n], mine_h, got_h, M[n], V[n], core_arr, "adamw_" + n)
                for n, mine_h, got_h in zip(tensor_names, halves, _grad_pair_out(halves))}

    late_others = _exchange_wait(*reduce_state["split"], grad_x, "grad_exchange_wait")
    res = finish(late_names, reduce_state["sums"], late_others, e_tok)
    res["w_ada"] = _ada_grad_adamw(cond_all.T, dm, w_ada, m_w_ada, v_w_ada)
    small_res = _small_adamw(small_all, pk(W), pk(M), pk(V))
    early_others = _exchange_wait(e_sems, e_rems, e_parts, e_zones, res["w_ff2"][1], "mla_exchange_wait")
    res.update(finish(early, sums, early_others, None))

    for n, width in BIASES.items():
        g = _one_hot_pick(small_res[n][0].reshape(N_CHIPS, width // N_CHIPS), chip, axis=0).reshape(1, -1)
        res[n] = [g] + _adamw(W[n], g, M[n], V[n], "adamw_" + n)
    for name in order:
        if name not in res:
            res[name] = small_res[name]
    outs = [small_res["loss"][0], grad_x[None]]
    for k in range(4):
        outs += [res[name][k] for name in order]
    return tuple(outs)
```

```python
import functools
import math

import jax
import jax.numpy as jnp
import numpy as np
from jax import lax
from jax.experimental import pallas as pl
from jax.experimental.pallas import tpu as pltpu

D = 1024
DEPTH = 2
MLA_HEADS = 8
QK_NOPE = 128
QK_ROPE = 64
V_DIM = 128
Q_LORA = 384
KV_LORA = 256
ROPE_THETA = 10000.0
SWA_HEADS = 16
SWA_KV_HEADS = 4
SWA_HEAD_DIM = 64
SWA_GROUP = SWA_HEADS // SWA_KV_HEADS
WINDOW = 128
D_FF = 4 * D
EPS = 1e-6
ADAM_LR = 0.001
ADAM_B1 = 0.9
ADAM_B2 = 0.999
ADAM_EPS = 1e-08
ADAM_WD = 0.01
ADAM_STEP = 10

N_CHIPS = 4
N_DEV = 8
LANES = 128
QK_EXT = 256
MLA_SCALE = (QK_NOPE + QK_ROPE) ** -0.5
LOG2E = math.log2(math.e)
LN2 = math.log(2.0)
MLA_QSCALE = MLA_SCALE * LOG2E
ATTN_BLOCK = 2048
ATTN_SUB = 512
MLP_FWD_TILE = (1024, 1024)
MLP_BWD_TILE = (1024, 1024)
DW_TOKENS = 4096
SWA_SCALE = SWA_HEAD_DIM ** -0.5
NEG = -1e30
MXU_DTYPE = jnp.bfloat16
VMEM_LIMIT = 56 * 1024 * 1024

R_SH1, R_SC1, R_GT1, R_SH2, R_SC2, R_GT2, R_GMIX, R_GMLP = range(8)
R_BO = 6


def _tile(n, pref):
    if n <= pref:
        return n
    for t in range(pref, 7, -1):
        if n % t == 0 and t % 8 == 0:
            return t
    return n


def _dot(a, b):
    return jnp.dot(a, b, preferred_element_type=jnp.float32)


def _dot_nt(a, b):
    return lax.dot_general(a, b, (((1,), (1,)), ((), ())), preferred_element_type=jnp.float32)


def _dot_tn(a, b):
    return lax.dot_general(a, b, (((0,), (0,)), ((), ())), preferred_element_type=jnp.float32)


def _rms(x):
    r = lax.rsqrt(jnp.mean(x * x, axis=-1, keepdims=True) + EPS)
    return x * r, r


def _rms_bwd(dxhat, xhat, r):
    return r * (dxhat - xhat * jnp.mean(dxhat * xhat, axis=-1, keepdims=True))


def _rowsum(v):
    return jnp.sum(v, axis=0, keepdims=True)


def _params(*sem):
    return pltpu.CompilerParams(dimension_semantics=sem, vmem_limit_bytes=VMEM_LIMIT)


def _full(shape):
    nd = len(shape)
    return pl.BlockSpec(shape, lambda *_: (0,) * nd)


def _rows(tm, cols):
    return pl.BlockSpec((tm, cols), lambda i, *_: (i, 0))


def _modulate_bwd(dh, x, vec_ref, r_g, r_sc, r_sh, ps_ref, dres):
    xhat, r = _rms(x)
    g = vec_ref[r_g:r_g + 1, :]
    n = xhat * g
    ps_ref[r_sh:r_sh + 1, :] += _rowsum(dh)
    ps_ref[r_sc:r_sc + 1, :] += _rowsum(dh * n)
    dn = dh * (1.0 + vec_ref[r_sc:r_sc + 1, :])
    ps_ref[r_g:r_g + 1, :] += _rowsum(dn * xhat)
    return dres + _rms_bwd(dn * g, xhat, r)


def _mla_pre(x, vec, wcat, g_q, g_kv, wuq, wukv, cs):
    T = x.shape[0]
    tm = _tile(T, 512)
    H = MLA_HEADS

    def body(x_ref, vec_ref, wcat_ref, gq_ref, gkv_ref, wuq_ref, wukv_ref, cs_ref, h_ref, z_ref, q_ref, k_ref, v_ref):
        xhat, _ = _rms(x_ref[...])
        h = xhat * vec_ref[R_GMIX:R_GMIX + 1, :] * (1.0 + vec_ref[R_SC1:R_SC1 + 1, :]) + vec_ref[R_SH1:R_SH1 + 1, :]
        hb = h.astype(MXU_DTYPE)
        h_ref[...] = hb
        z = _dot(hb, wcat_ref[...])
        z_ref[...] = z
        cq = (_rms(z[:, :Q_LORA])[0] * gq_ref[...]).astype(MXU_DTYPE)
        ckv = (_rms(z[:, Q_LORA:Q_LORA + KV_LORA])[0] * gkv_ref[...]).astype(MXU_DTYPE)
        cs_t = cs_ref[...]
        t = z[:, Q_LORA + KV_LORA:] * cs_t
        k_rope = (t + pltpu.roll(t, QK_ROPE, axis=1)).astype(MXU_DTYPE)
        low = lax.broadcasted_iota(jnp.int32, (1, LANES), 1) < QK_ROPE
        for hd in range(H):
            qf = _dot(cq, wuq_ref[hd])
            tq = qf[:, QK_NOPE:] * cs_t
            tq = tq + pltpu.roll(tq, QK_ROPE, axis=1)
            q_ref[hd, :, :QK_NOPE] = (qf[:, :QK_NOPE] * MLA_QSCALE).astype(MXU_DTYPE)
            q_ref[hd, :, QK_NOPE:] = jnp.where(low, tq * MLA_QSCALE, 0.0).astype(MXU_DTYPE)
            kvf = _dot(ckv, wukv_ref[hd])
            k_ref[hd, :, :QK_NOPE] = kvf[:, :QK_NOPE].astype(MXU_DTYPE)
            k_ref[hd, :, QK_NOPE:] = k_rope
            v_ref[hd] = kvf[:, QK_NOPE:].astype(MXU_DTYPE)

    zc = wcat.shape[1]
    return pl.pallas_call(
        body, name="mla_pre", grid=(T // tm,),
        in_specs=[_rows(tm, D), _full((8, D)), _full(wcat.shape), _full(g_q.shape), _full(g_kv.shape),
                  _full(wuq.shape), _full(wukv.shape), _rows(tm, LANES)],
        out_specs=[_rows(tm, D), _rows(tm, zc),
                   pl.BlockSpec((H, tm, QK_EXT), lambda i: (0, i, 0)),
                   pl.BlockSpec((H, tm, QK_EXT), lambda i: (0, i, 0)),
                   pl.BlockSpec((H, tm, V_DIM), lambda i: (0, i, 0))],
        out_shape=[jax.ShapeDtypeStruct((T, D), MXU_DTYPE), jax.ShapeDtypeStruct((T, zc), jnp.float32),
                   jax.ShapeDtypeStruct((H, T, QK_EXT), MXU_DTYPE), jax.ShapeDtypeStruct((H, T, QK_EXT), MXU_DTYPE),
                   jax.ShapeDtypeStruct((H, T, V_DIM), MXU_DTYPE)],
        compiler_params=_params("parallel"),
    )(x, vec, wcat, g_q, g_kv, wuq, wukv, cs)


def _mla_attn_fwd(q, k, v):
    H, T, _ = q.shape
    tb = _tile(T, ATTN_BLOCK)
    sub = min(ATTN_SUB, tb)
    ns, nb = tb // sub, T // tb

    def body(q_ref, k_ref, v_ref, o_ref, lse_ref, m_sc, l_sc, acc_sc):
        qi, kj = pl.program_id(1), pl.program_id(2)

        @pl.when(kj == 0)
        def _():
            m_sc[...] = jnp.full_like(m_sc, NEG)
            l_sc[...] = jnp.zeros_like(l_sc)
            acc_sc[...] = jnp.zeros_like(acc_sc)

        def update(r, kk, masked):
            rows, keys = pl.ds(r * sub, sub), pl.ds(kk * sub, sub)
            s = _dot_nt(q_ref[0, rows, :], k_ref[0, keys, :])
            if masked:
                row = lax.broadcasted_iota(jnp.int32, (sub, sub), 0)
                col = lax.broadcasted_iota(jnp.int32, (sub, sub), 1)
                s = jnp.where(col <= row, s, NEG)
            m_prev = m_sc[rows, :]
            m_new = jnp.maximum(m_prev, jnp.max(s, axis=1, keepdims=True))
            alpha = jnp.exp2(m_prev - m_new)
            p = jnp.exp2(s - jnp.tile(m_new, (1, sub // LANES)))
            l_sc[rows, :] = alpha * l_sc[rows, :] + jnp.sum(p, axis=1, keepdims=True)
            acc_sc[rows, :] = alpha * acc_sc[rows, :] + _dot(p.astype(MXU_DTYPE), v_ref[0, keys, :])
            m_sc[rows, :] = m_new

        @pl.when(kj < qi)
        def _():
            for kk in range(ns):
                for r in range(ns):
                    update(r, kk, False)

        @pl.when(kj == qi)
        def _():
            for kk in range(ns):
                for r in range(kk, ns):
                    update(r, kk, r == kk)
            l = l_sc[...]
            o_ref[...] = (acc_sc[...] / l).astype(o_ref.dtype)
            lse = m_sc[...] + jnp.log2(l)
            pick = (lax.broadcasted_iota(jnp.int32, (8, LANES), 1) == 0).astype(jnp.float32)
            row = lax.dot_general(pick, lse, (((1,), (1,)), ((), ())), precision=lax.Precision.HIGHEST,
                                  preferred_element_type=jnp.float32)
            lse_ref[0] = row[0:1, :]

    kv_idx = lambda h, i, j: (h, jnp.minimum(i, j), 0)
    return pl.pallas_call(
        body, name="mla_attn_fwd", grid=(H, nb, nb),
        in_specs=[pl.BlockSpec((1, tb, QK_EXT), lambda h, i, j: (h, i, 0)),
                  pl.BlockSpec((1, tb, QK_EXT), kv_idx),
                  pl.BlockSpec((1, tb, V_DIM), kv_idx)],
        out_specs=[pl.BlockSpec((tb, V_DIM), lambda h, i, j: (i, h)),
                   pl.BlockSpec((1, 1, tb), lambda h, i, j: (h, 0, i))],
        out_shape=[jax.ShapeDtypeStruct((T, H * V_DIM), MXU_DTYPE), jax.ShapeDtypeStruct((H, 1, T), jnp.float32)],
        scratch_shapes=[pltpu.VMEM((tb, LANES), jnp.float32), pltpu.VMEM((tb, LANES), jnp.float32),
                        pltpu.VMEM((tb, V_DIM), jnp.float32)],
        compiler_params=_params("parallel", "parallel", "arbitrary"),
    )(q, k, v)


def _post_attn(o, x, w_o, bias, vec, o_transposed=False):
    T = x.shape[0]
    tm = _tile(T, 512)
    o_spec = pl.BlockSpec((D, tm), lambda i: (0, i)) if o_transposed else _rows(tm, D)

    def body(o_ref, x_ref, w_ref, b_ref, vec_ref, y_ref, xm_ref, h_ref):
        y = (_dot_tn if o_transposed else _dot)(o_ref[...], w_ref[...]) + b_ref[...]
        y_ref[...] = y.astype(y_ref.dtype)
        xm = x_ref[...] + vec_ref[R_GT1:R_GT1 + 1, :] * y
        xm_ref[...] = xm
        xhat, _ = _rms(xm)
        h = xhat * vec_ref[R_GMLP:R_GMLP + 1, :] * (1.0 + vec_ref[R_SC2:R_SC2 + 1, :]) + vec_ref[R_SH2:R_SH2 + 1, :]
        h_ref[...] = h.astype(h_ref.dtype)

    return pl.pallas_call(
        body, name="post_attn", grid=(T // tm,),
        in_specs=[o_spec, _rows(tm, D), _full((D, D)), _full((1, D)), _full((8, D))],
        out_specs=[_rows(tm, D), _rows(tm, D), _rows(tm, D)],
        out_shape=[jax.ShapeDtypeStruct((T, D), MXU_DTYPE), jax.ShapeDtypeStruct((T, D), jnp.float32),
                   jax.ShapeDtypeStruct((T, D), MXU_DTYPE)],
        compiler_params=_params("parallel"),
    )(o, x, w_o, bias, vec)


def _ff_specs(tf):
    per = D_FF // N_CHIPS // tf
    w1 = pl.BlockSpec((None, D, tf), lambda i, f: (f // per, 0, f % per))
    w2 = pl.BlockSpec((None, tf, D), lambda i, f: (f // per, f % per, 0))
    return w1, w2


def _mlp_fwd(h2, w1, w2, xm, vec):
    T = h2.shape[0]
    tm = _tile(T, MLP_FWD_TILE[0])
    tf = _tile(D_FF // N_CHIPS, MLP_FWD_TILE[1])
    nf = D_FF // tf
    w1_spec, w2_spec = _ff_specs(tf)

    def body(h_ref, w1_ref, w2_ref, xm_ref, vec_ref, a_ref, y_ref, xo_ref, acc):
        f = pl.program_id(1)

        @pl.when(f == 0)
        def _():
            acc[...] = jnp.zeros_like(acc)

        u = jnp.maximum(_dot(h_ref[...], w1_ref[...]), 0.0)
        ab = (u * u).astype(MXU_DTYPE)
        a_ref[...] = ab
        acc[...] += _dot(ab, w2_ref[...])

        @pl.when(f == nf - 1)
        def _():
            y = acc[...]
            y_ref[...] = y.astype(y_ref.dtype)
            xo_ref[...] = xm_ref[...] + vec_ref[R_GT2:R_GT2 + 1, :] * y

    return pl.pallas_call(
        body, name="mlp_fwd", grid=(T // tm, nf),
        in_specs=[_rows(tm, D), w1_spec, w2_spec, _rows(tm, D), _full((8, D))],
        out_specs=[pl.BlockSpec((tm, tf), lambda i, f: (i, f)), _rows(tm, D), _rows(tm, D)],
        out_shape=[jax.ShapeDtypeStruct((T, D_FF), MXU_DTYPE), jax.ShapeDtypeStruct((T, D), MXU_DTYPE),
                   jax.ShapeDtypeStruct((T, D), jnp.float32)],
        scratch_shapes=[pltpu.VMEM((tm, D), jnp.float32)],
        compiler_params=_params("parallel", "arbitrary"),
    )(h2, w1, w2, xm, vec)


def _swa_pre(x, vec, w_qkv, b_qkv):
    T = x.shape[0]
    tm = _tile(T, 512)
    nq = SWA_HEADS * SWA_HEAD_DIM
    nk = SWA_KV_HEADS * SWA_HEAD_DIM
    wq_t, w_kv = w_qkv[:, :nq].T, w_qkv[:, nq:]
    bq_col, b_kv = b_qkv[:, :nq].reshape(nq, 1), b_qkv[:, nq:]

    def body(x_ref, vec_ref, wq_ref, wkv_ref, bq_ref, bkv_ref, h_ref, qt_ref, k_ref, v_ref):
        xhat, _ = _rms(x_ref[...])
        h = xhat * vec_ref[R_GMIX:R_GMIX + 1, :] * (1.0 + vec_ref[R_SC1:R_SC1 + 1, :]) + vec_ref[R_SH1:R_SH1 + 1, :]
        hb = h.astype(MXU_DTYPE)
        h_ref[...] = hb
        qt_ref[...] = ((_dot_nt(wq_ref[...], hb) + bq_ref[...]) * SWA_SCALE).astype(MXU_DTYPE)
        kv = _dot(hb, wkv_ref[...]) + bkv_ref[...]
        k_ref[...] = kv[:, :nk].astype(MXU_DTYPE)
        v_ref[...] = kv[:, nk:].astype(MXU_DTYPE)

    return pl.pallas_call(
        body, name="swa_pre", grid=(T // tm,),
        in_specs=[_rows(tm, D), _full((8, D)), _full(wq_t.shape), _full(w_kv.shape), _full(bq_col.shape),
                  _full(b_kv.shape)],
        out_specs=[_rows(tm, D), pl.BlockSpec((nq, tm), lambda i: (0, i)), _rows(tm, nk), _rows(tm, nk)],
        out_shape=[jax.ShapeDtypeStruct((T, D), MXU_DTYPE), jax.ShapeDtypeStruct((nq, T), MXU_DTYPE),
                   jax.ShapeDtypeStruct((T, nk), MXU_DTYPE), jax.ShapeDtypeStruct((T, nk), MXU_DTYPE)],
        compiler_params=_params("parallel"),
    )(x, vec, wq_t, w_kv, bq_col, b_kv)


def _swa_bias():
    W = WINDOW
    slopes = 2.0 ** (-8.0 * np.arange(1, SWA_HEADS + 1) / SWA_HEADS)
    dist = W + np.arange(W)[None, :] - np.arange(2 * W)[:, None]
    inside = (dist >= 0) & (dist < W)
    bias = np.where(inside[None], -slopes[:, None, None] * dist[None].astype(np.float64), NEG)
    bias = bias.reshape(SWA_KV_HEADS, SWA_GROUP, 2 * W, W).transpose(0, 2, 1, 3)
    return jnp.asarray(bias.reshape(SWA_KV_HEADS, 2 * W, SWA_GROUP * W), jnp.float32)


SWA_STEP_BLOCKS = 4


def _swa_blocks(T):
    nb = T // WINDOW
    return next(b for b in (SWA_STEP_BLOCKS, 2, 1) if nb % b == 0)


def _swa_views(b, qt_ref, kp_ref, kc_ref):
    W = WINDOW
    prev = kp_ref if b == 0 else kc_ref.at[pl.ds((b - 1) * W, W), :]
    return qt_ref.at[:, pl.ds(b * W, W)], prev, kc_ref.at[pl.ds(b * W, W), :]


def _swa_probs(has_prev, kh, qt_ref, kp_ref, kc_ref, bias_ref, sink_ref):
    W, Dh, G = WINDOW, SWA_HEAD_DIM, SWA_GROUP
    qt = jnp.concatenate([qt_ref[(kh * G + g) * Dh:(kh * G + g + 1) * Dh, :] for g in range(G)], axis=1)
    kb = jnp.concatenate([kp_ref[:, kh * Dh:(kh + 1) * Dh], kc_ref[:, kh * Dh:(kh + 1) * Dh]], axis=0)
    s = _dot(kb, qt) + bias_ref[kh]
    if has_prev is not True:
        key = lax.broadcasted_iota(jnp.int32, (2 * W, 1), 0)
        s = jnp.where((key >= W) | has_prev, s, NEG)
    sink = sink_ref[kh]
    m = jnp.maximum(jnp.max(s, axis=0, keepdims=True), sink)
    p = jnp.exp(s - m)
    p_sink = jnp.exp(sink - m)
    inv = 1.0 / (jnp.sum(p, axis=0, keepdims=True) + p_sink)
    return qt, kb, p * inv, p_sink * inv


def _swa_attn_fwd(qt, k, v, bias, sink_rows):
    T = qt.shape[1]
    W, Dh, G, Hk = WINDOW, SWA_HEAD_DIM, SWA_GROUP, SWA_KV_HEADS
    nk = Hk * Dh

    nb = _swa_blocks(T)

    def body(qt_ref, kp_ref, kc_ref, vp_ref, vc_ref, bias_ref, sink_ref, ot_ref):
        n = pl.program_id(0)
        for b in range(nb):
            q_b, kp_b, kc_b = _swa_views(b, qt_ref, kp_ref, kc_ref)
            _, vp_b, vc_b = _swa_views(b, qt_ref, vp_ref, vc_ref)
            for kh in range(Hk):
                _, _, pn, _ = _swa_probs(True if b else n > 0, kh, q_b, kp_b, kc_b, bias_ref, sink_ref)
                vb = jnp.concatenate([vp_b[:, kh * Dh:(kh + 1) * Dh], vc_b[:, kh * Dh:(kh + 1) * Dh]], axis=0)
                ot = _dot_tn(vb, pn.astype(MXU_DTYPE))
                for g in range(G):
                    rows = pl.ds((kh * G + g) * Dh, Dh)
                    ot_ref[rows, pl.ds(b * W, W)] = ot[:, g * W:(g + 1) * W].astype(ot_ref.dtype)

    prev = lambda n: (jnp.maximum(n * nb - 1, 0), 0)
    cur = lambda n: (n, 0)
    col = lambda n: (0, n)
    return pl.pallas_call(
        body, name="swa_attn_fwd", grid=(T // (nb * W),),
        in_specs=[pl.BlockSpec((D, nb * W), col), pl.BlockSpec((W, nk), prev), pl.BlockSpec((nb * W, nk), cur),
                  pl.BlockSpec((W, nk), prev), pl.BlockSpec((nb * W, nk), cur), _full(bias.shape),
                  _full(sink_rows.shape)],
        out_specs=pl.BlockSpec((D, nb * W), col),
        out_shape=jax.ShapeDtypeStruct((D, T), MXU_DTYPE),
        compiler_params=_params("parallel"),
    )(qt, k, k, v, v, bias, sink_rows)


def _final_loss(x, tgt, g):
    T = x.shape[0]
    tm = _tile(T, 512)

    def body(x_ref, t_ref, g_ref, loss_ref, dx_ref, dg_ref):
        @pl.when(pl.program_id(0) == 0)
        def _():
            loss_ref[...] = jnp.zeros_like(loss_ref)
            dg_ref[...] = jnp.zeros_like(dg_ref)

        xhat, r = _rms(x_ref[...])
        gv = g_ref[...]
        e = xhat * gv - t_ref[...]
        loss_ref[...] += 0.5 * jnp.sum(jnp.mean(e * e, axis=-1, keepdims=True), axis=0, keepdims=True)
        dy = e * (1.0 / D)
        dg_ref[...] += _rowsum(dy * xhat)
        dx_ref[...] = _rms_bwd(dy * gv, xhat, r)

    return pl.pallas_call(
        body, name="final_loss", grid=(T // tm,),
        in_specs=[_rows(tm, D), _rows(tm, D), _full((1, D))],
        out_specs=[_full((8, LANES)), _rows(tm, D), _full((1, D))],
        out_shape=[jax.ShapeDtypeStruct((8, LANES), jnp.float32), jax.ShapeDtypeStruct((T, D), jnp.float32),
                   jax.ShapeDtypeStruct((1, D), jnp.float32)],
        compiler_params=_params("arbitrary"),
    )(x, tgt, g)


def _mlp_bwd(dxo, y2, a, w1, w2, xm, vec):
    T = dxo.shape[0]
    tm, tf = _tile(T, MLP_BWD_TILE[0]), _tile(D_FF // N_CHIPS, MLP_BWD_TILE[1])
    nf = D_FF // tf
    w1_spec, w2_spec = _ff_specs(tf)
    blk = pl.BlockSpec((tm, tf), lambda i, f: (i, f))

    def up(dxo_ref, y_ref, a_ref, w2_ref, vec_ref, du_ref, dy_ref, ps_ref, dyb):
        i, f = pl.program_id(0), pl.program_id(1)

        @pl.when((i == 0) & (f == 0))
        def _():
            ps_ref[...] = jnp.zeros_like(ps_ref)

        @pl.when(f == 0)
        def _():
            dxo_t = dxo_ref[...]
            d = (dxo_t * vec_ref[R_GT2:R_GT2 + 1, :]).astype(MXU_DTYPE)
            dyb[...] = d
            dy_ref[...] = d
            ps_ref[R_GT2:R_GT2 + 1, :] += _rowsum(dxo_t * y_ref[...].astype(jnp.float32))

        da = _dot_nt(dyb[...], w2_ref[...])
        du_ref[...] = (da * (2.0 * jnp.sqrt(a_ref[...].astype(jnp.float32)))).astype(MXU_DTYPE)

    du, dy, ps_gate = pl.pallas_call(
        up, name="mlp_bwd_up", grid=(T // tm, nf),
        in_specs=[_rows(tm, D), _rows(tm, D), blk, w2_spec, _full((8, D))],
        out_specs=[blk, _rows(tm, D), _full((8, D))],
        out_shape=[jax.ShapeDtypeStruct((T, D_FF), MXU_DTYPE), jax.ShapeDtypeStruct((T, D), MXU_DTYPE),
                   jax.ShapeDtypeStruct((8, D), jnp.float32)],
        scratch_shapes=[pltpu.VMEM((tm, D), MXU_DTYPE)],
        compiler_params=_params("arbitrary", "arbitrary"),
    )(dxo, y2, a, w2, vec)

    def down(du_ref, w1_ref, dxo_ref, xm_ref, vec_ref, ps_in_ref, dxm_ref, ps_ref, acc):
        i, f = pl.program_id(0), pl.program_id(1)

        @pl.when((i == 0) & (f == 0))
        def _():
            ps_ref[...] = ps_in_ref[...]

        @pl.when(f == 0)
        def _():
            acc[...] = jnp.zeros_like(acc)

        acc[...] += _dot_nt(du_ref[...], w1_ref[...])

        @pl.when(f == nf - 1)
        def _():
            dxm_ref[...] = _modulate_bwd(acc[...], xm_ref[...], vec_ref, R_GMLP, R_SC2, R_SH2, ps_ref, dxo_ref[...])

    dxm, ps = pl.pallas_call(
        down, name="mlp_bwd_down", grid=(T // tm, nf),
        in_specs=[blk, w1_spec, _rows(tm, D), _rows(tm, D), _full((8, D)), _full((8, D))],
        out_specs=[_rows(tm, D), _full((8, D))],
        out_shape=[jax.ShapeDtypeStruct((T, D), jnp.float32), jax.ShapeDtypeStruct((8, D), jnp.float32)],
        scratch_shapes=[pltpu.VMEM((tm, D), jnp.float32)],
        compiler_params=_params("arbitrary", "arbitrary"),
    )(du, w1, dxo, xm, vec, ps_gate)
    return du, dy, dxm, ps


def _mm_tn(a, g, name, split=None, layers=1, layer=0, into=None, a_transposed=False):
    K, T = a.shape if a_transposed else a.shape[::-1]
    N = g.shape[1]
    kq = K // N_CHIPS if split == "rows" else K
    nq = N // N_CHIPS if split == "cols" else N
    bk, bn, bt = _tile(kq, 1024), _tile(nq, 1024), _tile(T, DW_TOKENS)
    if nq % bn or bn % LANES:
        bn = nq
    kper, nper = kq // bk, nq // bn

    def body(*refs):
        a_ref, g_ref, o_ref = refs[0], refs[1], refs[-1]

        @pl.when(pl.program_id(2) == 0)
        def _():
            o_ref[...] = jnp.zeros_like(o_ref)

        o_ref[...] += (_dot if a_transposed else _dot_tn)(a_ref[...], g_ref[...])

    a_spec = pl.BlockSpec((bk, bt), lambda k, n, t: (k, t)) if a_transposed else pl.BlockSpec((bt, bk), lambda k, n, t: (t, k))
    in_specs = [a_spec, pl.BlockSpec((bt, bn), lambda k, n, t: (t, n))]
    args = [a, g]
    aliases = {}
    if split is None:
        out_spec = pl.BlockSpec((bk, bn), lambda k, n, t: (k, n))
        out_shape = jax.ShapeDtypeStruct((K, N), jnp.float32)
    else:
        if split == "cols":
            idx = lambda k, n, t: (n // nper, layer, k, n % nper)
        else:
            idx = lambda k, n, t: (k // kper, layer, k % kper, n)
        out_spec = pl.BlockSpec((None, None, bk, bn), idx)
        out_shape = jax.ShapeDtypeStruct((N_CHIPS, layers, kq, nq), jnp.float32)
        if into is not None:
            in_specs.append(pl.BlockSpec(memory_space=pl.ANY))
            args.append(into)
            aliases = {2: 0}
    return pl.pallas_call(
        body, name=name, grid=(K // bk, N // bn, T // bt), in_specs=in_specs, out_specs=out_spec, out_shape=out_shape,
        input_output_aliases=aliases, compiler_params=_params("parallel", "parallel", "arbitrary"),
    )(*args)


def _attn_out_bwd(dxm, y1, o, w_o, vec, with_delta):
    T = dxm.shape[0]
    tm = _tile(T, 512)
    H = MLA_HEADS

    def body(dxm_ref, y_ref, w_ref, vec_ref, *refs):
        o_ref = refs[0] if with_delta else None
        dy_ref, do_ref, ps_ref, *delta_ref = refs[1:] if with_delta else refs

        @pl.when(pl.program_id(0) == 0)
        def _():
            ps_ref[...] = jnp.zeros_like(ps_ref)

        dxm_t = dxm_ref[...]
        dy = dxm_t * vec_ref[R_GT1:R_GT1 + 1, :]
        ps_ref[R_GT1:R_GT1 + 1, :] += _rowsum(dxm_t * y_ref[...].astype(jnp.float32))
        ps_ref[R_BO:R_BO + 1, :] += _rowsum(dy)
        dyb = dy.astype(MXU_DTYPE)
        dy_ref[...] = dyb
        if not with_delta:
            do_ref[...] = _dot_nt(w_ref[...], dyb).astype(do_ref.dtype)
        else:
            do = _dot_nt(dyb, w_ref[...])
            do_ref[...] = do.astype(do_ref.dtype)
            of = o_ref[...].astype(jnp.float32)
            ones = jnp.ones((8, V_DIM), jnp.float32)
            for hd in range(H):
                sl = slice(hd * V_DIM, (hd + 1) * V_DIM)
                d = lax.dot_general(ones, do[:, sl] * of[:, sl], (((1,), (1,)), ((), ())),
                                    precision=lax.Precision.HIGHEST, preferred_element_type=jnp.float32)
                delta_ref[0][hd] = d[0:1, :]

    out_specs = [_rows(tm, D), _rows(tm, D), _full((8, D))]
    out_shape = [jax.ShapeDtypeStruct((T, D), MXU_DTYPE), jax.ShapeDtypeStruct((T, D), MXU_DTYPE),
                 jax.ShapeDtypeStruct((8, D), jnp.float32)]
    if not with_delta:
        out_specs[1] = pl.BlockSpec((D, tm), lambda i: (0, i))
        out_shape[1] = jax.ShapeDtypeStruct((D, T), MXU_DTYPE)
    if with_delta:
        out_specs.append(pl.BlockSpec((H, 1, tm), lambda i: (0, 0, i)))
        out_shape.append(jax.ShapeDtypeStruct((H, 1, T), jnp.float32))
    return pl.pallas_call(
        body, name="attn_out_bwd_mla" if with_delta else "attn_out_bwd_swa", grid=(T // tm,),
        in_specs=[_rows(tm, D), _rows(tm, D), _full((D, D)), _full((8, D))] + ([_rows(tm, D)] if with_delta else []),
        out_specs=out_specs, out_shape=out_shape,
        compiler_params=_params("arbitrary"),
    )(dxm, y1, w_o, vec, *([o] if with_delta else []))


def _mla_attn_bwd(q, k, v, do, lse, delta):
    H, T, _ = q.shape
    tb = _tile(T, ATTN_BLOCK)
    sub = min(ATTN_SUB, tb)
    ns, nb = tb // sub, T // tb

    def body(q_ref, k_ref, v_ref, do_ref, lse_ref, dl_ref, dq_ref, dk_ref, dv_ref, dk_acc, dv_acc):
        j, i = pl.program_id(1), pl.program_id(2)

        @pl.when((j == 0) & (i == 0))
        def _():
            dq_ref[...] = jnp.zeros_like(dq_ref)

        def update(kk, r, masked):
            keys, rows = pl.ds(kk * sub, sub), pl.ds(r * sub, sub)
            kb, qb, dob = k_ref[0, keys, :], q_ref[0, rows, :], do_ref[rows, :]
            st = _dot_nt(kb, qb)
            if masked:
                row = lax.broadcasted_iota(jnp.int32, (sub, sub), 0)
                col = lax.broadcasted_iota(jnp.int32, (sub, sub), 1)
                st = jnp.where(row <= col, st, NEG)
            pt = jnp.exp2(st - lse_ref[0, :, rows])
            dv_acc[keys, :] += _dot(pt.astype(MXU_DTYPE), dob)
            dpt = _dot_nt(v_ref[0, keys, :], dob)
            dst = (pt * (dpt - dl_ref[0, :, rows])).astype(MXU_DTYPE)
            dk_acc[keys, :] += _dot(dst, qb)
            q_rows = pl.ds(pl.multiple_of(i * tb + r * sub, sub), sub)
            dq_ref[0, q_rows, :] += _dot_tn(dst, kb)

        @pl.when(i == j)
        def _():
            dk_acc[...] = jnp.zeros_like(dk_acc)
            dv_acc[...] = jnp.zeros_like(dv_acc)
            for r in range(ns):
                for kk in range(r + 1):
                    update(kk, r, kk == r)

        @pl.when(i > j)
        def _():
            for r in range(ns):
                for kk in range(ns):
                    update(kk, r, False)

        @pl.when(i == nb - 1)
        def _():
            dk_ref[0] = (dk_acc[...] * LN2).astype(dk_ref.dtype)
            dv_ref[0] = dv_acc[...].astype(dv_ref.dtype)

    q_idx = lambda h, j, i: (h, jnp.maximum(i, j), 0)
    kv_idx = lambda h, j, i: (h, j, 0)
    stat_idx = lambda h, j, i: (h, 0, jnp.maximum(i, j))
    return pl.pallas_call(
        body, name="mla_attn_bwd", grid=(H, nb, nb),
        in_specs=[pl.BlockSpec((1, tb, QK_EXT), q_idx), pl.BlockSpec((1, tb, QK_EXT), kv_idx),
                  pl.BlockSpec((1, tb, V_DIM), kv_idx),
                  pl.BlockSpec((tb, V_DIM), lambda h, j, i: (jnp.maximum(i, j), h)),
                  pl.BlockSpec((1, 1, tb), stat_idx), pl.BlockSpec((1, 1, tb), stat_idx)],
        out_specs=[pl.BlockSpec((1, T, QK_EXT), lambda h, j, i: (h, 0, 0)),
                   pl.BlockSpec((1, tb, QK_EXT), kv_idx), pl.BlockSpec((1, tb, V_DIM), kv_idx)],
        out_shape=[jax.ShapeDtypeStruct((H, T, QK_EXT), jnp.float32), jax.ShapeDtypeStruct((H, T, QK_EXT), MXU_DTYPE),
                   jax.ShapeDtypeStruct((H, T, V_DIM), MXU_DTYPE)],
        scratch_shapes=[pltpu.VMEM((tb, QK_EXT), jnp.float32), pltpu.VMEM((tb, V_DIM), jnp.float32)],
        compiler_params=_params("parallel", "arbitrary", "arbitrary"),
    )(q, k, v, do, lse, delta)


def _mla_pre_bwd(x, dxm, vec, hb, z, dq, dk, dv, cs, wcat, g_q, g_kv, wuq, wukv):
    T = x.shape[0]
    tm = _tile(T, 256)
    H = MLA_HEADS
    zc = wcat.shape[1]

    def body(x_ref, dxm_ref, vec_ref, h_ref, z_ref, dq_ref, dk_ref, dv_ref, cs_ref, wcat_ref, gq_ref, gkv_ref,
             wuq_ref, wukv_ref, dx_ref, ps_ref, dgq_ref, dgkv_ref, dwcat_ref, dwuq_ref, dwukv_ref):
        @pl.when(pl.program_id(0) == 0)
        def _():
            for ref in (ps_ref, dgq_ref, dgkv_ref, dwcat_ref, dwuq_ref, dwukv_ref):
                ref[...] = jnp.zeros_like(ref)

        z = z_ref[...]
        cs_t = cs_ref[...]
        cqhat, rq = _rms(z[:, :Q_LORA])
        ckhat, rk = _rms(z[:, Q_LORA:Q_LORA + KV_LORA])
        gq, gkv = gq_ref[...], gkv_ref[...]
        cq = (cqhat * gq).astype(MXU_DTYPE)
        ckv = (ckhat * gkv).astype(MXU_DTYPE)
        dcq = jnp.zeros((tm, Q_LORA), jnp.float32)
        dckv = jnp.zeros((tm, KV_LORA), jnp.float32)
        dkr = jnp.zeros((tm, LANES), jnp.float32)
        for hd in range(H):
            dqh = dq_ref[hd] * MLA_SCALE
            gqh = jnp.concatenate([dqh[:, :QK_NOPE], dqh[:, QK_NOPE:] * cs_t], axis=1).astype(MXU_DTYPE)
            dcq += _dot_nt(gqh, wuq_ref[hd])
            dwuq_ref[hd] += _dot_tn(cq, gqh)
            dkh = dk_ref[hd]
            gkvh = jnp.concatenate([dkh[:, :QK_NOPE], dv_ref[hd]], axis=1)
            dckv += _dot_nt(gkvh, wukv_ref[hd])
            dwukv_ref[hd] += _dot_tn(ckv, gkvh)
            dkr += dkh[:, QK_NOPE:].astype(jnp.float32)
        dgq_ref[...] += _rowsum(dcq * cqhat)
        dgkv_ref[...] += _rowsum(dckv * ckhat)
        dcq_pre = _rms_bwd(dcq * gq, cqhat, rq)
        dckv_pre = _rms_bwd(dckv * gkv, ckhat, rk)
        dkr2 = (dkr + pltpu.roll(dkr, QK_ROPE, axis=1)) * cs_t
        dz = jnp.concatenate([dcq_pre, dckv_pre, dkr2], axis=1).astype(MXU_DTYPE)
        dwcat_ref[...] += _dot_tn(h_ref[...], dz)
        dh = _dot_nt(dz, wcat_ref[...])
        dx_ref[...] = _modulate_bwd(dh, x_ref[...], vec_ref, R_GMIX, R_SC1, R_SH1, ps_ref, dxm_ref[...])

    hblk = lambda w: pl.BlockSpec((H, tm, w), lambda i: (0, i, 0))
    return pl.pallas_call(
        body, name="mla_pre_bwd", grid=(T // tm,),
        in_specs=[_rows(tm, D), _rows(tm, D), _full((8, D)), _rows(tm, D), _rows(tm, zc), hblk(QK_EXT), hblk(QK_EXT),
                  hblk(V_DIM), _rows(tm, LANES), _full(wcat.shape), _full(g_q.shape), _full(g_kv.shape),
                  _full(wuq.shape), _full(wukv.shape)],
        out_specs=[_rows(tm, D), _full((8, D)), _full(g_q.shape), _full(g_kv.shape), _full(wcat.shape),
                   _full(wuq.shape), _full(wukv.shape)],
        out_shape=[jax.ShapeDtypeStruct((T, D), jnp.float32), jax.ShapeDtypeStruct((8, D), jnp.float32),
                   jax.ShapeDtypeStruct(g_q.shape, jnp.float32), jax.ShapeDtypeStruct(g_kv.shape, jnp.float32),
                   jax.ShapeDtypeStruct(wcat.shape, jnp.float32), jax.ShapeDtypeStruct(wuq.shape, jnp.float32),
                   jax.ShapeDtypeStruct(wukv.shape, jnp.float32)],
        compiler_params=_params("arbitrary"),
    )(x, dxm, vec, hb, z, dq, dk, dv, cs, wcat, g_q, g_kv, wuq, wukv)


def _swa_attn_bwd(qt, k, v, dot_, bias, sink_rows):
    T = qt.shape[1]
    W, Dh, G, Hk = WINDOW, SWA_HEAD_DIM, SWA_GROUP, SWA_KV_HEADS
    nk = Hk * Dh
    nb = _swa_blocks(T)

    def body(qt_ref, kp_ref, kc_ref, vp_ref, vc_ref, dot_ref, bias_ref, sink_ref, dqt_ref, dk_ref, dv_ref, dsink_ref):
        n = pl.program_id(0)

        @pl.when(n == 0)
        def _():
            dk_ref[...] = jnp.zeros_like(dk_ref)
            dv_ref[...] = jnp.zeros_like(dv_ref)
            dsink_ref[...] = jnp.zeros_like(dsink_ref)

        def add_rows(first_row, dkb_part, dvb_part):
            rows = pl.ds(pl.multiple_of(first_row, W), W)
            dk_ref[rows, :] += dkb_part
            dv_ref[rows, :] += dvb_part

        for b in range(nb):
            q_b, kp_b, kc_b = _swa_views(b, qt_ref, kp_ref, kc_ref)
            do_b, vp_b, vc_b = _swa_views(b, dot_ref, vp_ref, vc_ref)
            dks, dvs = [], []
            for kh in range(Hk):
                qt, kb, pn, p_sink = _swa_probs(True if b else n > 0, kh, q_b, kp_b, kc_b, bias_ref, sink_ref)
                vb = jnp.concatenate([vp_b[:, kh * Dh:(kh + 1) * Dh], vc_b[:, kh * Dh:(kh + 1) * Dh]], axis=0)
                dot_h = jnp.concatenate([do_b[(kh * G + g) * Dh:(kh * G + g + 1) * Dh, :] for g in range(G)], axis=1)
                dp = _dot(vb, dot_h)
                delta = jnp.sum(pn * dp, axis=0, keepdims=True)
                dsb = (pn * (dp - delta)).astype(MXU_DTYPE)
                dsink_ref[kh] += -p_sink * delta
                dqt = _dot_tn(kb, dsb) * SWA_SCALE
                for g in range(G):
                    dqt_ref[pl.ds((kh * G + g) * Dh, Dh), pl.ds(b * W, W)] = dqt[:, g * W:(g + 1) * W]
                dks.append(_dot_nt(dsb, qt))
                dvs.append(_dot_nt(pn.astype(MXU_DTYPE), dot_h))
            dkb = jnp.concatenate(dks, axis=1)
            dvb = jnp.concatenate(dvs, axis=1)
            add_rows((n * nb + b) * W, dkb[W:], dvb[W:])
            if b:
                add_rows((n * nb + b - 1) * W, dkb[:W], dvb[:W])
            else:
                @pl.when(n > 0)
                def _():
                    add_rows((n * nb - 1) * W, dkb[:W], dvb[:W])

    prev = lambda n: (jnp.maximum(n * nb - 1, 0), 0)
    cur = lambda n: (n, 0)
    col = lambda n: (0, n)
    return pl.pallas_call(
        body, name="swa_attn_bwd", grid=(T // (nb * W),),
        in_specs=[pl.BlockSpec((D, nb * W), col), pl.BlockSpec((W, nk), prev), pl.BlockSpec((nb * W, nk), cur),
                  pl.BlockSpec((W, nk), prev), pl.BlockSpec((nb * W, nk), cur), pl.BlockSpec((D, nb * W), col),
                  _full(bias.shape), _full(sink_rows.shape)],
        out_specs=[pl.BlockSpec((D, nb * W), col), _full((T, nk)), _full((T, nk)), _full(sink_rows.shape)],
        out_shape=[jax.ShapeDtypeStruct((D, T), jnp.float32), jax.ShapeDtypeStruct((T, nk), jnp.float32),
                   jax.ShapeDtypeStruct((T, nk), jnp.float32), jax.ShapeDtypeStruct(sink_rows.shape, jnp.float32)],
        compiler_params=_params("arbitrary"),
    )(qt, k, k, v, v, dot_, bias, sink_rows)


def _swa_pre_bwd(x, dxm, vec, dq, dk, dv, w_qkv):
    T = x.shape[0]
    tm = _tile(T, 512)
    nq = SWA_HEADS * SWA_HEAD_DIM
    nk = SWA_KV_HEADS * SWA_HEAD_DIM
    nqkv = nq + 2 * nk

    def body(x_ref, dxm_ref, vec_ref, dq_ref, dk_ref, dv_ref, w_ref, dx_ref, dqkv_ref, ps_ref, db_ref):
        @pl.when(pl.program_id(0) == 0)
        def _():
            ps_ref[...] = jnp.zeros_like(ps_ref)
            db_ref[...] = jnp.zeros_like(db_ref)

        dqkv = jnp.concatenate([dq_ref[...], dk_ref[...], dv_ref[...]], axis=1)
        db_ref[...] += _rowsum(dqkv)
        dqkv_b = dqkv.astype(MXU_DTYPE)
        dqkv_ref[...] = dqkv_b
        dh = _dot_nt(dqkv_b, w_ref[...])
        dx_ref[...] = _modulate_bwd(dh, x_ref[...], vec_ref, R_GMIX, R_SC1, R_SH1, ps_ref, dxm_ref[...])

    return pl.pallas_call(
        body, name="swa_pre_bwd", grid=(T // tm,),
        in_specs=[_rows(tm, D), _rows(tm, D), _full((8, D)), _rows(tm, nq), _rows(tm, nk), _rows(tm, nk),
                  _full(w_qkv.shape)],
        out_specs=[_rows(tm, D), _rows(tm, nqkv), _full((8, D)), _full((1, nqkv))],
        out_shape=[jax.ShapeDtypeStruct((T, D), jnp.float32), jax.ShapeDtypeStruct((T, nqkv), MXU_DTYPE),
                   jax.ShapeDtypeStruct((8, D), jnp.float32), jax.ShapeDtypeStruct((1, nqkv), jnp.float32)],
        compiler_params=_params("arbitrary"),
    )(x, dxm, vec, dq, dk, dv, w_qkv)


def _rot_cols(w):
    half = QK_ROPE // 2
    return jnp.concatenate([-w[..., half:], w[..., :half]], axis=-1)


def _unrot_grad(d_rope, d_rot):
    half = QK_ROPE // 2
    return d_rope + jnp.concatenate([d_rot[..., half:], -d_rot[..., :half]], axis=-1)


def _rope_table(positions):
    half = QK_ROPE // 2
    inv_freq = ROPE_THETA ** (-jnp.arange(half, dtype=jnp.float32) / half)
    ang = positions.astype(jnp.float32)[:, None] * inv_freq
    cos, sin = jnp.cos(ang), jnp.sin(ang)
    return jnp.concatenate([cos, cos, sin, sin], axis=1)


def _sequence_step(x, tgt, positions, vecs, g_q, g_kv, sinks, g_final, wts, late_weights, on_late_grads, on_late_landed):
    H = MLA_HEADS
    cs = _rope_table(positions)
    w_dkv = wts["mla_w_dkv"]
    wcat = jnp.concatenate([wts["mla_w_dq"], w_dkv, _rot_cols(w_dkv[:, KV_LORA:])], axis=1)
    uq = wts["mla_w_uq"].reshape(Q_LORA, H, QK_NOPE + QK_ROPE)
    wuq = jnp.concatenate([uq, _rot_cols(uq[..., QK_NOPE:])], axis=-1).transpose(1, 0, 2)
    wukv = wts["mla_w_ukv"].reshape(KV_LORA, H, QK_NOPE + V_DIM).transpose(1, 0, 2)
    zero_bias = jnp.zeros((1, D), jnp.float32)
    bias = _swa_bias()
    sink_rows = jnp.broadcast_to(sinks.reshape(SWA_KV_HEADS, 1, SWA_GROUP, 1),
                                 (SWA_KV_HEADS, 1, SWA_GROUP, WINDOW)).reshape(SWA_KV_HEADS, 1, SWA_GROUP * WINDOW)

    h1a, z, q, k, v = _mla_pre(x, vecs[0], wcat, g_q, g_kv, wuq, wukv, cs)
    o_a, lse = _mla_attn_fwd(q, k, v)
    y1a, xm_a, h2a = _post_attn(o_a, x, wts["mla_w_o"], zero_bias, vecs[0])
    wts = {**wts, **late_weights(h2a)}
    a_a, y2a, x1 = _mlp_fwd(h2a, wts["w_ff1"][0], wts["w_ff2"][0], xm_a, vecs[0])

    h1b, qs_t, ks, vs = _swa_pre(x1, vecs[1], wts["swa_w_qkv"], wts["swa_b_qkv"])
    o_bt = _swa_attn_fwd(qs_t, ks, vs, bias, sink_rows)
    y1b, xm_b, h2b = _post_attn(o_bt, x1, wts["swa_w_o"], wts["swa_b_o"], vecs[1], o_transposed=True)
    a_b, y2b, x2 = _mlp_fwd(h2b, wts["w_ff1"][1], wts["w_ff2"][1], xm_b, vecs[1])

    loss8, dx2, dg_final = _final_loss(x2, tgt, g_final.reshape(1, D))

    du_b, dy2b, dxm_b, ps_mlp_b = _mlp_bwd(dx2, y2b, a_b, wts["w_ff1"][1], wts["w_ff2"][1], xm_b, vecs[1])
    g_ff2 = _mm_tn(a_b, dy2b, "dw_ff2_l1", "rows", DEPTH, 1)
    g_ff1 = _mm_tn(h2b, du_b, "dw_ff1_l1", "cols", DEPTH, 1)
    dy1b, do_bt, ps_out_b = _attn_out_bwd(dxm_b, y1b, None, wts["swa_w_o"], vecs[1], False)
    g_swa_o = _mm_tn(o_bt, dy1b, "dw_o_swa", a_transposed=True)
    dqs_t, dks, dvs, dsinks = _swa_attn_bwd(qs_t, ks, vs, do_bt, bias, sink_rows)
    dqs = dqs_t.T
    dx1, dqkv, ps_pre_b, g_swa_bqkv = _swa_pre_bwd(x1, dxm_b, vecs[1], dqs, dks, dvs, wts["swa_w_qkv"])
    g_swa_qkv = _mm_tn(h1b, dqkv, "dw_qkv", "cols")

    du_a, dy2a, dxm_a, ps_mlp_a = _mlp_bwd(dx1, y2a, a_a, wts["w_ff1"][0], wts["w_ff2"][0], xm_a, vecs[0])
    g_ff2 = _mm_tn(a_a, dy2a, "dw_ff2_l0", "rows", DEPTH, 0, g_ff2)
    g_ff1 = _mm_tn(h2a, du_a, "dw_ff1_l0", "cols", DEPTH, 0, g_ff1)
    rows4 = lambda g: g.reshape(N_CHIPS, g.shape[0] // N_CHIPS, g.shape[1])
    token = on_late_grads({
        "swa_w_qkv": g_swa_qkv.reshape(N_CHIPS, D, -1), "swa_w_o": rows4(g_swa_o),
        "w_ff1": g_ff1.reshape(N_CHIPS, DEPTH * D, -1), "w_ff2": g_ff2.reshape(N_CHIPS, -1, D)})
    dy1a, do_a, ps_out_a, delta = _attn_out_bwd(dxm_a, y1a, o_a, wts["mla_w_o"], vecs[0] + token[0, 0], True)
    g_mla_o = _mm_tn(o_a, dy1a, "dw_o_mla")
    token = on_late_landed(g_mla_o)
    dq, dk, dv = _mla_attn_bwd(q, k, v, do_a, lse, delta + token[0, 0])
    dx0, ps_pre_a, dg_q, dg_kv, dwcat, dwuq, dwukv = _mla_pre_bwd(
        x, dxm_a, vecs[0], h1a, z, dq, dk, dv, cs, wcat, g_q, g_kv, wuq, wukv)

    c0, c1, c2 = Q_LORA, Q_LORA + KV_LORA, Q_LORA + KV_LORA + QK_ROPE
    g_dq = dwcat[:, :c0]
    g_dkv = jnp.concatenate([dwcat[:, c0:c1], _unrot_grad(dwcat[:, c1:c2], dwcat[:, c2:])], axis=1)
    e0 = QK_NOPE + QK_ROPE
    g_uq = jnp.concatenate([dwuq[..., :QK_NOPE], _unrot_grad(dwuq[..., QK_NOPE:e0], dwuq[..., e0:])], axis=-1)
    per = H // N_CHIPS
    g_uq = g_uq.reshape(N_CHIPS, per, Q_LORA, e0).transpose(0, 2, 1, 3).reshape(N_CHIPS, Q_LORA, per * e0)
    g_ukv = dwukv.reshape(N_CHIPS, per, KV_LORA, QK_NOPE + V_DIM).transpose(0, 2, 1, 3)
    g_ukv = g_ukv.reshape(N_CHIPS, KV_LORA, per * (QK_NOPE + V_DIM))

    def dmod(ps_pre, ps_out, ps_mlp):
        return jnp.concatenate([ps_pre[R_SH1:R_SC1 + 1], ps_out[R_GT1:R_GT1 + 1], ps_mlp[R_SH2:R_GT2 + 1]], axis=0)

    grads = {"mla_w_dq": rows4(g_dq), "mla_w_uq": g_uq, "mla_w_dkv": rows4(g_dkv), "mla_w_ukv": g_ukv,
             "mla_w_o": rows4(g_mla_o)}
    small = {
        "dmod": jnp.stack([dmod(ps_pre_a, ps_out_a, ps_mlp_a), dmod(ps_pre_b, ps_out_b, ps_mlp_b)]).reshape(DEPTH, 6 * D),
        "g_mix": jnp.stack([ps_pre_a[R_GMIX], ps_pre_b[R_GMIX]]),
        "g_mlp": jnp.stack([ps_mlp_a[R_GMLP], ps_mlp_b[R_GMLP]]),
        "mla_g_q": dg_q, "mla_g_kv": dg_kv, "swa_sinks": jnp.sum(dsinks.reshape(SWA_HEADS, WINDOW), axis=1).reshape(1, SWA_HEADS),
        "swa_b_qkv": g_swa_bqkv, "swa_b_o": ps_out_b[R_BO:R_BO + 1],
        "g_final": dg_final.reshape(D), "loss": loss8[0, 0],
    }
    return dx0, grads, small


SHARDED = {
    "mla_w_dq": (1, D // N_CHIPS, Q_LORA),
    "mla_w_uq": (1, Q_LORA, MLA_HEADS * (QK_NOPE + QK_ROPE) // N_CHIPS),
    "mla_w_dkv": (1, D // N_CHIPS, KV_LORA + QK_ROPE),
    "mla_w_ukv": (1, KV_LORA, MLA_HEADS * (QK_NOPE + V_DIM) // N_CHIPS),
    "mla_w_o": (1, MLA_HEADS * V_DIM // N_CHIPS, D),
    "swa_w_qkv": (1, D, (SWA_HEADS + 2 * SWA_KV_HEADS) * SWA_HEAD_DIM // N_CHIPS),
    "swa_w_o": (1, SWA_HEADS * SWA_HEAD_DIM // N_CHIPS, D),
    "w_ff1": (DEPTH, D, D_FF // N_CHIPS),
    "w_ff2": (DEPTH, D_FF // N_CHIPS, D),
}
COL_SPLIT = ("mla_w_uq", "mla_w_ukv", "swa_w_qkv")
BIASES = {"swa_b_qkv": (SWA_HEADS + 2 * SWA_KV_HEADS) * SWA_HEAD_DIM, "swa_b_o": D}


def _view2d(name):
    shape = SHARDED[name]
    return math.prod(shape[:-1]), shape[-1]


SMALL = {"b_ada": (DEPTH, 6 * D), "g_mix": (DEPTH, D), "g_mlp": (DEPTH, D), "mla_g_q": (1, Q_LORA),
         "mla_g_kv": (1, KV_LORA), "swa_sinks": (1, SWA_HEADS), "g_final": (D,), "loss": (),
         "swa_b_qkv": (1, BIASES["swa_b_qkv"]), "swa_b_o": (1, BIASES["swa_b_o"])}
SMALL_ROWS = 192
DMA_ROWS = 256


SLOT_ROWS = 8


def _small_slots():
    slots, off = {}, 0
    for name, shape in SMALL.items():
        n = max(math.prod(shape), 1)
        slots[name] = (off, n)
        off += -(-n // (SLOT_ROWS * LANES)) * SLOT_ROWS * LANES
    assert off <= SMALL_ROWS * LANES
    return slots


def _pack_small(vals):
    parts, end = [], 0
    for name, (off, n) in _small_slots().items():
        pad = -(-n // (SLOT_ROWS * LANES)) * SLOT_ROWS * LANES - n
        v = vals[name].astype(jnp.float32).reshape(-1) if name in vals else jnp.zeros((n,), jnp.float32)
        parts += [v, jnp.zeros((pad,), jnp.float32)]
        end = off + n + pad
    parts.append(jnp.zeros((SMALL_ROWS * LANES - end,), jnp.float32))
    return jnp.concatenate(parts).reshape(SMALL_ROWS, LANES)


def _from_slot(name, rows):
    n = max(math.prod(SMALL[name]), 1)
    return rows.reshape(-1)[:n].reshape(SMALL[name])


def _pieces(rows):
    return [(off, min(DMA_ROWS, rows - off)) for off in range(0, rows, DMA_ROWS)]


HBM = pl.BlockSpec(memory_space=pltpu.HBM)
MESH = pl.DeviceIdType.MESH


def _place():
    x, y, c = lax.axis_index("x"), lax.axis_index("y"), lax.axis_index("c")
    chips = [(1 - x, y), (x, 1 - y), (1 - x, 1 - y)]
    return x, y, c, chips


def _all_gather(block):
    m_per, n = block.shape

    def body(x_ref, out_ref, send_sems, recv_sems, local_sem):
        x, y, c, chips = _place()
        me, sibling = (x, y, c), (x, y, 1 - c)

        def rows(px, py, pc):
            return out_ref.at[pl.ds((4 * px + 2 * py + pc) * m_per, m_per), :]

        def copy(k, blk, to, src=None):
            return pltpu.make_async_remote_copy(
                src_ref=rows(*blk) if src is None else src, dst_ref=rows(*blk),
                send_sem=send_sems.at[k], recv_sem=recv_sems.at[k], device_id=to, device_id_type=MESH)

        mine = pltpu.make_async_copy(x_ref, rows(*me), local_sem)
        mine.start()
        first = [copy(0, me, sibling, src=x_ref)]
        first += [copy(1 + j, me, (*chip, c), src=x_ref) for j, chip in enumerate(chips)]
        for cp in first:
            cp.start()
        passed = [copy(4 + j, (*chip, c), sibling) for j, chip in enumerate(chips)]
        for j, chip in enumerate(chips):
            copy(1 + j, (*chip, c), me).wait_recv()
            passed[j].start()
        copy(0, sibling, me).wait_recv()
        for j, chip in enumerate(chips):
            copy(4 + j, (*chip, 1 - c), me).wait_recv()
        for cp in first + passed:
            cp.wait_send()
        mine.wait()

    out = pl.pallas_call(
        body, name="all_gather_small",
        out_shape=jax.ShapeDtypeStruct((N_DEV * m_per, n), block.dtype),
        in_specs=[pl.BlockSpec(memory_space=pltpu.VMEM)],
        out_specs=pl.BlockSpec(memory_space=pltpu.VMEM),
        scratch_shapes=[pltpu.SemaphoreType.DMA((7,)), pltpu.SemaphoreType.DMA((7,)), pltpu.SemaphoreType.DMA],
    )(block)
    return out.reshape(N_DEV, m_per, n)


def _weight_gather(shards):
    nt = len(shards)

    def body(*refs):
        w_refs, out_refs = refs[:nt], refs[nt:2 * nt]
        send_sems, recv_sems = refs[2 * nt:]
        x, y, c, chips = _place()
        sibling = (x, y, 1 - c)

        def slab(t, px, py, half):
            rh = shards[t].shape[0] // 2
            return out_refs[t].at[2 * px + py, pl.ds(half * rh, rh), :]

        def copy(t, k, src, dst, to):
            return pltpu.make_async_remote_copy(src_ref=src, dst_ref=dst, send_sem=send_sems.at[6 * t + k],
                                                recv_sem=recv_sems.at[6 * t + k], device_id=to, device_id_type=MESH)

        first = []
        for t in range(nt):
            rh = shards[t].shape[0] // 2
            first += [copy(t, j, w_refs[t].at[pl.ds(c * rh, rh), :], slab(t, x, y, c), (*chip, c))
                      for j, chip in enumerate(chips)]
        for cp in first:
            cp.start()
        passed = []
        for t in range(nt):
            for j, chip in enumerate(chips):
                copy(t, j, slab(t, *chip, c), slab(t, *chip, c), (*chip, c)).wait_recv()
                rh = shards[t].shape[0] // 2
                for off, n in _pieces(rh):
                    piece = out_refs[t].at[2 * chip[0] + chip[1], pl.ds(c * rh + off, n), :]
                    copy(t, 3 + j, piece, piece, sibling).start()
                passed.append(copy(t, 3 + j, slab(t, *chip, c), slab(t, *chip, c), sibling))
        for t in range(nt):
            for j, chip in enumerate(chips):
                copy(t, 3 + j, slab(t, *chip, 1 - c), slab(t, *chip, 1 - c), sibling).wait_recv()
        for cp in first + passed:
            cp.wait_send()

    return pl.pallas_call(
        body, name="weight_gather",
        out_shape=[jax.ShapeDtypeStruct((N_CHIPS,) + s.shape, s.dtype) for s in shards],
        in_specs=[HBM] * nt, out_specs=[HBM] * nt,
        scratch_shapes=[pltpu.SemaphoreType.DMA((6 * nt,)), pltpu.SemaphoreType.DMA((6 * nt,))],
    )(*shards)


SEM = pl.BlockSpec(memory_space=pltpu.SEMAPHORE)
ANY = pl.BlockSpec(memory_space=pl.ANY)
SPLIT_COPY = pltpu.SideEffectType.DATAFLOW_SIDE_EFFECTING


def _late_copies(w_refs, land_refs, send_sems, recv_sems):
    x, y, c, chips = _place()
    return [pltpu.make_async_remote_copy(
        src_ref=w_refs[t], dst_ref=land_refs[t].at[2 * x + y], send_sem=send_sems.at[3 * t + j],
        recv_sem=recv_sems.at[3 * t + j], device_id=(cx, cy, c), device_id_type=MESH)
        for t in range(len(w_refs)) for j, (cx, cy) in enumerate(chips)], chips


def _late_gather_start(shards, after):
    nt, na = len(shards), len(after)

    def body(*refs):
        w_refs, land_refs = refs[:nt], refs[nt:2 * nt]
        send_sems, recv_sems, token = refs[2 * nt + na], refs[2 * nt + na + 1], refs[-1]
        copies, _ = _late_copies(w_refs, land_refs, send_sems, recv_sems)
        for cp in copies:
            cp.start()
        token[...] = jnp.zeros_like(token)

    hbm = lambda a: pltpu.with_memory_space_constraint(a, pltpu.HBM)
    lands = [lax.empty((N_CHIPS,) + s.shape, s.dtype) for s in shards]
    outs = pl.pallas_call(
        body, name="late_gather_start",
        out_shape=(pltpu.SemaphoreType.DMA((3 * nt,)), pltpu.SemaphoreType.DMA((3 * nt,)),
                   *[pltpu.HBM(s.shape, s.dtype) for s in shards], *[pltpu.HBM(l.shape, l.dtype) for l in lands],
                   jax.ShapeDtypeStruct((8, LANES), jnp.float32)),
        in_specs=[HBM] * (2 * nt) + [ANY] * na,
        out_specs=(SEM, SEM, *([HBM] * (2 * nt)), pl.BlockSpec(memory_space=pltpu.VMEM)),
        input_output_aliases={i: 2 + i for i in range(2 * nt)},
        compiler_params=pltpu.CompilerParams(has_side_effects=SPLIT_COPY),
    )(*[hbm(s) for s in shards], *[hbm(l) for l in lands], *after)
    return outs[0], outs[1], list(outs[2:2 + nt]), list(outs[2 + nt:2 + 2 * nt]), outs[-1]


def _late_gather_wait(send_sems, recv_sems, shards, lands, after):
    nt = len(shards)

    def body(*refs):
        w_refs, land_refs = refs[:nt], refs[nt:2 * nt]
        s_sems, r_sems = refs[2 * nt], refs[2 * nt + 1]
        x, y, c, chips = _place()
        for t in range(nt):
            for j, (cx, cy) in enumerate(chips):
                cp = pltpu.make_async_remote_copy(
                    src_ref=w_refs[t], dst_ref=land_refs[t].at[2 * cx + cy], send_sem=s_sems.at[3 * t + j],
                    recv_sem=r_sems.at[3 * t + j], device_id=(cx, cy, c), device_id_type=MESH)
                cp.wait_send()
                cp.wait_recv()

    outs = pl.pallas_call(
        body, name="late_gather_wait",
        out_shape=(*[pltpu.HBM(s.shape, s.dtype) for s in shards], *[pltpu.HBM(l.shape, l.dtype) for l in lands]),
        in_specs=[HBM] * (2 * nt) + [SEM, SEM, ANY], out_specs=tuple([HBM] * (2 * nt)),
        input_output_aliases={i: i for i in range(2 * nt)},
        compiler_params=pltpu.CompilerParams(has_side_effects=SPLIT_COPY),
    )(*shards, *lands, send_sems, recv_sems, after)
    return list(outs[nt:])


def _grad_pair_in(grads):
    nt = len(grads)

    def body(*refs):
        g_refs, got_refs = refs[:nt], refs[nt:2 * nt]
        send_sems, recv_sems = refs[2 * nt:]
        x, y, c, _ = _place()
        sibling = (x, y, 1 - c)

        def copy(t, src, dst):
            return pltpu.make_async_remote_copy(src_ref=src, dst_ref=dst, send_sem=send_sems.at[t],
                                                recv_sem=recv_sems.at[t], device_id=sibling, device_id_type=MESH)

        for t in range(nt):
            rh = grads[t].shape[1] // 2
            for p in range(N_CHIPS):
                for off, n in _pieces(rh):
                    copy(t, g_refs[t].at[p, pl.ds((1 - c) * rh + off, n), :], got_refs[t].at[p, pl.ds(off, n), :]).start()
        for t in range(nt):
            rh = grads[t].shape[1] // 2
            copy(t, g_refs[t].at[:, pl.ds((1 - c) * rh, rh), :], got_refs[t]).wait()

    return pl.pallas_call(
        body, name="grad_pair_in",
        out_shape=[jax.ShapeDtypeStruct((N_CHIPS, g.shape[1] // 2, g.shape[2]), g.dtype) for g in grads],
        in_specs=[HBM] * nt, out_specs=[HBM] * nt,
        scratch_shapes=[pltpu.SemaphoreType.DMA((nt,)), pltpu.SemaphoreType.DMA((nt,))],
    )(*grads)


def _pair_in_start(grads):
    nt = len(grads)

    def body(*refs):
        g_refs, land_refs = refs[:nt], refs[nt:2 * nt]
        send_sems, recv_sems, token = refs[2 * nt], refs[2 * nt + 1], refs[-1]
        x, y, c, _ = _place()
        for t in range(nt):
            rh = grads[t].shape[1] // 2
            for p in range(N_CHIPS):
                for off, n in _pieces(rh):
                    pltpu.make_async_remote_copy(
                        src_ref=g_refs[t].at[p, pl.ds((1 - c) * rh + off, n), :], dst_ref=land_refs[t].at[p, pl.ds(off, n), :],
                        send_sem=send_sems.at[t], recv_sem=recv_sems.at[t], device_id=(x, y, 1 - c),
                        device_id_type=MESH).start()
        token[...] = jnp.zeros_like(token)

    hbm = lambda a: pltpu.with_memory_space_constraint(a, pltpu.HBM)
    lands = [lax.empty((N_CHIPS, g.shape[1] // 2, g.shape[2]), g.dtype) for g in grads]
    outs = pl.pallas_call(
        body, name="grad_pair_in_start",
        out_shape=(pltpu.SemaphoreType.DMA((nt,)), pltpu.SemaphoreType.DMA((nt,)),
                   *[pltpu.HBM(g.shape, g.dtype) for g in grads], *[pltpu.HBM(l.shape, l.dtype) for l in lands],
                   jax.ShapeDtypeStruct((8, LANES), jnp.float32)),
        in_specs=[HBM] * (2 * nt),
        out_specs=(SEM, SEM, *([HBM] * (2 * nt)), pl.BlockSpec(memory_space=pltpu.VMEM)),
        input_output_aliases={i: 2 + i for i in range(2 * nt)},
        compiler_params=pltpu.CompilerParams(has_side_effects=SPLIT_COPY),
    )(*[hbm(g) for g in grads], *[hbm(l) for l in lands])
    return outs[0], outs[1], list(outs[2:2 + nt]), list(outs[2 + nt:2 + 2 * nt]), outs[-1]


def _pair_in_wait(send_sems, recv_sems, grads, lands, after):
    nt = len(grads)

    def body(*refs):
        g_refs, land_refs = refs[:nt], refs[nt:2 * nt]
        s_sems, r_sems = refs[2 * nt], refs[2 * nt + 1]
        x, y, c, _ = _place()
        for t in range(nt):
            rh = grads[t].shape[1] // 2
            cp = pltpu.make_async_remote_copy(
                src_ref=g_refs[t].at[:, pl.ds((1 - c) * rh, rh), :], dst_ref=land_refs[t], send_sem=s_sems.at[t],
                recv_sem=r_sems.at[t], device_id=(x, y, 1 - c), device_id_type=MESH)
            cp.wait_send()
            cp.wait_recv()

    outs = pl.pallas_call(
        body, name="grad_pair_in_wait",
        out_shape=(*[pltpu.HBM(g.shape, g.dtype) for g in grads], *[pltpu.HBM(l.shape, l.dtype) for l in lands]),
        in_specs=[HBM] * (2 * nt) + [SEM, SEM, ANY], out_specs=tuple([HBM] * (2 * nt)),
        input_output_aliases={i: i for i in range(2 * nt)},
        compiler_params=pltpu.CompilerParams(has_side_effects=SPLIT_COPY),
    )(*grads, *lands, send_sems, recv_sems, after)
    return list(outs[:nt]), list(outs[nt:])


def _pair_sum(g, got, core, name):
    _, rows, cols = g.shape
    rh = rows // 2
    tr = _tile(rh, 512)
    nb = rh // tr

    def body(c_ref, g_ref, got_ref, s32_ref, s16_ref):
        s = g_ref[...] + got_ref[...]
        s32_ref[...] = s
        s16_ref[...] = s.astype(s16_ref.dtype)

    blk = pl.BlockSpec((None, tr, cols), lambda p, i, c_ref: (p, i, 0))
    return pl.pallas_call(
        body, name=name,
        grid_spec=pltpu.PrefetchScalarGridSpec(
            num_scalar_prefetch=1, grid=(N_CHIPS, nb),
            in_specs=[pl.BlockSpec((None, tr, cols), lambda p, i, c_ref: (p, c_ref[0] * nb + i, 0)), blk],
            out_specs=[blk, blk]),
        out_shape=[jax.ShapeDtypeStruct((N_CHIPS, rh, cols), jnp.float32),
                   jax.ShapeDtypeStruct((N_CHIPS, rh, cols), jnp.bfloat16)],
        compiler_params=_params("parallel", "parallel"),
    )(core, g, got)


def _exchange_start(parts, name):
    nt = len(parts)

    def body(*refs):
        a_refs, land_refs = refs[:nt], refs[nt:2 * nt]
        send_sems, recv_sems, token = refs[2 * nt], refs[2 * nt + 1], refs[-1]
        x, y, c, chips = _place()
        for t in range(nt):
            for j, (cx, cy) in enumerate(chips):
                pltpu.make_async_remote_copy(
                    src_ref=a_refs[t].at[2 * cx + cy], dst_ref=land_refs[t].at[j], send_sem=send_sems.at[3 * t + j],
                    recv_sem=recv_sems.at[3 * t + j], device_id=(cx, cy, c), device_id_type=MESH).start()
        token[...] = jnp.zeros_like(token)

    hbm = lambda a: pltpu.with_memory_space_constraint(a, pltpu.HBM)
    lands = [lax.empty((N_CHIPS - 1,) + a.shape[1:], a.dtype) for a in parts]
    outs = pl.pallas_call(
        body, name=name,
        out_shape=(pltpu.SemaphoreType.DMA((3 * nt,)), pltpu.SemaphoreType.DMA((3 * nt,)),
                   *[pltpu.HBM(a.shape, a.dtype) for a in parts], *[pltpu.HBM(l.shape, l.dtype) for l in lands],
                   jax.ShapeDtypeStruct((8, LANES), jnp.float32)),
        in_specs=[HBM] * (2 * nt),
        out_specs=(SEM, SEM, *([HBM] * (2 * nt)), pl.BlockSpec(memory_space=pltpu.VMEM)),
        input_output_aliases={i: 2 + i for i in range(2 * nt)},
        compiler_params=pltpu.CompilerParams(has_side_effects=SPLIT_COPY),
    )(*[hbm(a) for a in parts], *[hbm(l) for l in lands])
    return outs[0], outs[1], list(outs[2:2 + nt]), list(outs[2 + nt:2 + 2 * nt]), outs[-1]


def _exchange_wait(send_sems, recv_sems, parts, lands, after, name):
    nt = len(parts)

    def body(*refs):
        a_refs, land_refs = refs[:nt], refs[nt:2 * nt]
        s_sems, r_sems = refs[2 * nt], refs[2 * nt + 1]
        x, y, c, chips = _place()
        for t in range(nt):
            for j, (cx, cy) in enumerate(chips):
                cp = pltpu.make_async_remote_copy(
                    src_ref=a_refs[t].at[2 * cx + cy], dst_ref=land_refs[t].at[j], send_sem=s_sems.at[3 * t + j],
                    recv_sem=r_sems.at[3 * t + j], device_id=(cx, cy, c), device_id_type=MESH)
                cp.wait_send()
                cp.wait_recv()

    outs = pl.pallas_call(
        body, name=name,
        out_shape=(*[pltpu.HBM(a.shape, a.dtype) for a in parts], *[pltpu.HBM(l.shape, l.dtype) for l in lands]),
        in_specs=[HBM] * (2 * nt) + [SEM, SEM, ANY], out_specs=tuple([HBM] * (2 * nt)),
        input_output_aliases={i: i for i in range(2 * nt)},
        compiler_params=pltpu.CompilerParams(has_side_effects=SPLIT_COPY),
    )(*parts, *lands, send_sems, recv_sems, after)
    return list(outs[nt:])


def _chip_sum(s32, got, chip, name, behind=None):
    _, rh, cols = s32.shape
    tr = _tile(rh, 512)

    def body(p_ref, s_ref, got_ref, *refs):
        acc = s_ref[...]
        for j in range(N_CHIPS - 1):
            acc = acc + got_ref[j].astype(jnp.float32)
        refs[-1][...] = acc

    extra = [] if behind is None else [behind]
    return pl.pallas_call(
        body, name=name,
        grid_spec=pltpu.PrefetchScalarGridSpec(
            num_scalar_prefetch=1, grid=(rh // tr,),
            in_specs=[pl.BlockSpec((None, tr, cols), lambda i, p_ref: (p_ref[0], i, 0)),
                      pl.BlockSpec((N_CHIPS - 1, tr, cols), lambda i, p_ref: (0, i, 0))]
            + [pl.BlockSpec((8, LANES), lambda i, p_ref: (0, 0))] * len(extra),
            out_specs=pl.BlockSpec((tr, cols), lambda i, p_ref: (i, 0))),
        out_shape=jax.ShapeDtypeStruct((rh, cols), jnp.float32),
        compiler_params=_params("parallel"),
    )(chip, s32, got, *extra)


def _grad_pair_out(halves):
    nt = len(halves)

    def body(*refs):
        h_refs, got_refs = refs[:nt], refs[nt:2 * nt]
        send_sems, recv_sems = refs[2 * nt:]
        x, y, c, _ = _place()
        sibling = (x, y, 1 - c)

        def copy(t, src, dst):
            return pltpu.make_async_remote_copy(src_ref=src, dst_ref=dst, send_sem=send_sems.at[t],
                                                recv_sem=recv_sems.at[t], device_id=sibling, device_id_type=MESH)

        for t in range(nt):
            for off, n in _pieces(halves[t].shape[0]):
                copy(t, h_refs[t].at[pl.ds(off, n), :], got_refs[t].at[pl.ds(off, n), :]).start()
        for t in range(nt):
            copy(t, h_refs[t], got_refs[t]).wait()

    return pl.pallas_call(
        body, name="grad_pair_out",
        out_shape=[jax.ShapeDtypeStruct(h.shape, h.dtype) for h in halves],
        in_specs=[HBM] * nt, out_specs=[HBM] * nt,
        scratch_shapes=[pltpu.SemaphoreType.DMA((nt,)), pltpu.SemaphoreType.DMA((nt,))],
    )(*halves)


def _ada_part(c_all, w_ada):
    L, _, ncol = w_ada.shape
    tn = _tile(ncol, 512)

    def body(c_ref, w_ref, cond_ref, part_ref):
        cv = c_ref[...]
        cond = cv * jax.nn.sigmoid(cv)
        cond_ref[...] = cond
        part_ref[0] = jnp.dot(cond, w_ref[0], precision=lax.Precision.HIGHEST, preferred_element_type=jnp.float32)

    return pl.pallas_call(
        body, name="ada_part", grid=(L, ncol // tn),
        in_specs=[_full((N_DEV, D)), pl.BlockSpec((1, D, tn), lambda l, j: (l, 0, j))],
        out_specs=[_full((N_DEV, D)), pl.BlockSpec((1, N_DEV, tn), lambda l, j: (l, 0, j))],
        out_shape=[jax.ShapeDtypeStruct((N_DEV, D), jnp.float32), jax.ShapeDtypeStruct((L, N_DEV, ncol), jnp.float32)],
        compiler_params=_params("arbitrary", "arbitrary"),
    )(c_all, w_ada)


def _adamw_math(w, g, m, v):
    m = ADAM_B1 * m + (1.0 - ADAM_B1) * g
    v = ADAM_B2 * v + (1.0 - ADAM_B2) * jnp.square(g)
    m_hat = m / (1.0 - ADAM_B1 ** ADAM_STEP)
    v_hat = v / (1.0 - ADAM_B2 ** ADAM_STEP)
    delta = -ADAM_LR * (m_hat / (jnp.sqrt(v_hat) + ADAM_EPS) + ADAM_WD * w)
    return delta, m, v


def _adamw(w, g, m, v, name):
    shape = w.shape
    cols = shape[-1]
    rows = math.prod(shape[:-1])
    tr = _tile(rows, 512)
    two_d = lambda t: t.reshape(rows, cols)

    def body(w_ref, g_ref, m_ref, v_ref, d_ref, mo_ref, vo_ref):
        d_ref[...], mo_ref[...], vo_ref[...] = _adamw_math(w_ref[...], g_ref[...], m_ref[...], v_ref[...])

    out = jax.ShapeDtypeStruct((rows, cols), jnp.float32)
    outs = pl.pallas_call(
        body, name=name, grid=(rows // tr,), in_specs=[_rows(tr, cols)] * 4, out_specs=[_rows(tr, cols)] * 3,
        out_shape=[out, out, out], compiler_params=_params("parallel"),
    )(two_d(w), two_d(g), two_d(m), two_d(v))
    return [t.reshape(shape) for t in outs]


def _adamw_halves(w, mine, got, m, v, core, name):
    shape = w.shape
    cols = shape[-1]
    rows = math.prod(shape[:-1])
    rh = rows // 2
    tr = _tile(rh, 512)
    nbh = rh // tr
    two_d = lambda t: t.reshape(rows, cols)

    def body(c_ref, w_ref, a_ref, b_ref, m_ref, v_ref, g_ref, d_ref, mo_ref, vo_ref):
        g = jnp.where(pl.program_id(0) // nbh == c_ref[0], a_ref[...], b_ref[...])
        g_ref[...] = g
        d_ref[...], mo_ref[...], vo_ref[...] = _adamw_math(w_ref[...], g, m_ref[...], v_ref[...])

    row = pl.BlockSpec((tr, cols), lambda i, c_ref: (i, 0))

    def half(keep):
        return pl.BlockSpec((tr, cols), lambda i, c_ref: (jnp.where((i // nbh == c_ref[0]) == keep, i % nbh, 0), 0))

    out = jax.ShapeDtypeStruct((rows, cols), jnp.float32)
    outs = pl.pallas_call(
        body, name=name,
        grid_spec=pltpu.PrefetchScalarGridSpec(
            num_scalar_prefetch=1, grid=(rows // tr,),
            in_specs=[row, half(True), half(False), row, row], out_specs=[row] * 4),
        out_shape=[out] * 4, compiler_params=_params("arbitrary"),
    )(core, two_d(w), mine, got, two_d(m), two_d(v))
    return [t.reshape(shape) for t in outs]


def _ada_grad_adamw(cond_t, dm, w, m, v):
    L, _, ncol = w.shape
    tn = _tile(ncol, 512)

    def body(ct_ref, dm_ref, w_ref, m_ref, v_ref, g_ref, d_ref, mo_ref, vo_ref):
        g = ct_ref[:, 0:1] * dm_ref[0, 0:1, :]
        for b in range(1, N_DEV):
            g = g + ct_ref[:, b:b + 1] * dm_ref[0, b:b + 1, :]
        g_ref[0] = g
        d_ref[0], mo_ref[0], vo_ref[0] = _adamw_math(w_ref[0], g, m_ref[0], v_ref[0])

    wblk = pl.BlockSpec((1, D, tn), lambda l, j: (l, 0, j))
    out = jax.ShapeDtypeStruct(w.shape, jnp.float32)
    return pl.pallas_call(
        body, name="ada_grad_adamw", grid=(L, ncol // tn),
        in_specs=[_full((D, N_DEV)), pl.BlockSpec((1, N_DEV, tn), lambda l, j: (l, 0, j)), wblk, wblk, wblk],
        out_specs=[wblk] * 4, out_shape=[out] * 4, compiler_params=_params("parallel", "parallel"),
    )(cond_t, dm, w, m, v)


def _small_adamw(gathered, w, m, v):
    slots = _small_slots()
    rows = {name: (off // LANES, -(-n // LANES)) for name, (off, n) in slots.items()}
    kinds = {name: 1 if name == "loss" or name in BIASES else 4 for name in slots}

    def body(ga_ref, w_ref, m_ref, v_ref, *out_refs):
        g = ga_ref[0]
        for dev in range(1, N_DEV):
            g = g + ga_ref[dev]
        d, mo, vo = _adamw_math(w_ref[...], g, m_ref[...], v_ref[...])
        k = 0
        for name, (r0, nr) in rows.items():
            for src in (g, d, mo, vo)[:kinds[name]]:
                out_refs[k][...] = src[r0:r0 + nr, :]
                k += 1

    out_shape = [jax.ShapeDtypeStruct((rows[name][1], LANES), jnp.float32) for name in slots for _ in range(kinds[name])]
    flat = pl.pallas_call(
        body, name="small_adamw", out_shape=out_shape,
        in_specs=[pl.BlockSpec(memory_space=pltpu.VMEM)] * 4,
        out_specs=[pl.BlockSpec(memory_space=pltpu.VMEM)] * len(out_shape),
    )(gathered, w, m, v)
    out, k = {}, 0
    for name in slots:
        out[name] = [_from_slot(name, t) for t in flat[k:k + kinds[name]]]
        k += kinds[name]
    return out


def _one_hot_pick(arr, index, axis):
    n = arr.shape[axis]
    shape = [1] * arr.ndim
    shape[axis] = n
    hot = (jnp.arange(n) == index).astype(arr.dtype).reshape(shape)
    return jnp.sum(arr * hot, axis=axis)


def kernel(x, c, positions, w_ada, b_ada, g_mix, g_mlp, mla_w_dq, mla_g_q, mla_w_uq, mla_w_dkv, mla_g_kv, mla_w_ukv, mla_w_o, swa_w_qkv, swa_b_qkv, swa_sinks, swa_w_o, swa_b_o, w_ff1, w_ff2, g_final, loss_target, m_w_ada, m_b_ada, m_g_mix, m_g_mlp, m_mla_w_dq, m_mla_g_q, m_mla_w_uq, m_mla_w_dkv, m_mla_g_kv, m_mla_w_ukv, m_mla_w_o, m_swa_w_qkv, m_swa_b_qkv, m_swa_sinks, m_swa_w_o, m_swa_b_o, m_w_ff1, m_w_ff2, m_g_final, v_w_ada, v_b_ada, v_g_mix, v_g_mlp, v_mla_w_dq, v_mla_g_q, v_mla_w_uq, v_mla_w_dkv, v_mla_g_kv, v_mla_w_ukv, v_mla_w_o, v_swa_w_qkv, v_swa_b_qkv, v_swa_sinks, v_swa_w_o, v_swa_b_o, v_w_ff1, v_w_ff2, v_g_final):
    W = dict(w_ada=w_ada, b_ada=b_ada, g_mix=g_mix, g_mlp=g_mlp, mla_w_dq=mla_w_dq, mla_g_q=mla_g_q, mla_w_uq=mla_w_uq,
             mla_w_dkv=mla_w_dkv, mla_g_kv=mla_g_kv, mla_w_ukv=mla_w_ukv, mla_w_o=mla_w_o, swa_w_qkv=swa_w_qkv,
             swa_b_qkv=swa_b_qkv, swa_sinks=swa_sinks, swa_w_o=swa_w_o, swa_b_o=swa_b_o, w_ff1=w_ff1, w_ff2=w_ff2,
             g_final=g_final)
    M = dict(w_ada=m_w_ada, b_ada=m_b_ada, g_mix=m_g_mix, g_mlp=m_g_mlp, mla_w_dq=m_mla_w_dq, mla_g_q=m_mla_g_q,
             mla_w_uq=m_mla_w_uq, mla_w_dkv=m_mla_w_dkv, mla_g_kv=m_mla_g_kv, mla_w_ukv=m_mla_w_ukv, mla_w_o=m_mla_w_o,
             swa_w_qkv=m_swa_w_qkv, swa_b_qkv=m_swa_b_qkv, swa_sinks=m_swa_sinks, swa_w_o=m_swa_w_o, swa_b_o=m_swa_b_o,
             w_ff1=m_w_ff1, w_ff2=m_w_ff2, g_final=m_g_final)
    V = dict(w_ada=v_w_ada, b_ada=v_b_ada, g_mix=v_g_mix, g_mlp=v_g_mlp, mla_w_dq=v_mla_w_dq, mla_g_q=v_mla_g_q,
             mla_w_uq=v_mla_w_uq, mla_w_dkv=v_mla_w_dkv, mla_g_kv=v_mla_g_kv, mla_w_ukv=v_mla_w_ukv, mla_w_o=v_mla_w_o,
             swa_w_qkv=v_swa_w_qkv, swa_b_qkv=v_swa_b_qkv, swa_sinks=v_swa_sinks, swa_w_o=v_swa_w_o, swa_b_o=v_swa_b_o,
             w_ff1=v_w_ff1, w_ff2=v_w_ff2, g_final=v_g_final)
    order = list(W)
    names = list(SHARDED)
    core = lax.axis_index("c")
    chip = 2 * lax.axis_index("x") + lax.axis_index("y")
    dev = 2 * chip + core
    core_arr = core.astype(jnp.int32).reshape(1)
    chip_arr = chip.astype(jnp.int32).reshape(1)

    def whole(n, g, own):
        g = lax.dynamic_update_slice(g, own[None], (chip, 0, 0))
        if n in ("w_ff1", "w_ff2"):
            return g
        if n in COL_SPLIT:
            return g.transpose(1, 0, 2).reshape(g.shape[1], N_CHIPS * g.shape[2])
        return g.reshape(N_CHIPS * g.shape[1], g.shape[2])

    early = [n for n in names if n.startswith("mla_")]
    local = {n: W[n].astype(MXU_DTYPE).reshape(_view2d(n)) for n in early}
    wts = {n: whole(n, g, local[n]) for n, g in zip(early, _weight_gather([local[n] for n in early]))}

    nbq, nbo = BIASES["swa_b_qkv"] // N_CHIPS, BIASES["swa_b_o"] // N_CHIPS
    first = jnp.concatenate([c.reshape(-1), swa_b_qkv.reshape(-1), swa_b_o.reshape(-1),
                             jnp.zeros((16 * LANES - D - nbq - nbo,), jnp.float32)]).reshape(16, LANES)
    first_all = _all_gather(first).reshape(N_DEV, 16 * LANES)
    c_all = first_all[:, :D]
    south = first_all[0::2]
    wts["swa_b_qkv"] = south[:, D:D + nbq].reshape(1, N_CHIPS * nbq)
    wts["swa_b_o"] = south[:, D + nbq:D + nbq + nbo].reshape(1, N_CHIPS * nbo)
    cond_all, part = _ada_part(c_all, w_ada)
    ncol = w_ada.shape[2]
    part_all = _all_gather(part.reshape(-1, LANES)).reshape(N_DEV, DEPTH, N_DEV, ncol)
    mine = _one_hot_pick(part_all[0::2], dev, axis=2)
    mod = mine.transpose(1, 0, 2).reshape(DEPTH, N_CHIPS * ncol) + b_ada
    vecs = jnp.concatenate([mod.reshape(DEPTH, 6, D), g_mix[:, None, :], g_mlp[:, None, :]], axis=1)

    late = [("w_ff1", 0), ("w_ff2", 0), ("swa_w_qkv", None), ("swa_w_o", None), ("w_ff1", 1), ("w_ff2", 1)]
    late_local = [(W[n][0] if l is None else W[n][l]).astype(MXU_DTYPE) for n, l in late]
    send_sems, recv_sems, passed, lands, token = _late_gather_start(late_local, [vecs] + [wts[n] for n in early])

    def late_weights(after):
        got = _late_gather_wait(send_sems, recv_sems, passed, lands, after)
        out = {"w_ff1": [None] * DEPTH, "w_ff2": [None] * DEPTH}
        for (n, l), g, own in zip(late, got, late_local):
            if l is None:
                out[n] = whole(n, g, own)
            else:
                out[n][l] = whole(n, g, own)
        return out

    late_names = [n for n in names if n not in early]
    reduce_state = {}

    def on_late_grads(late_grads):
        s_sems, r_sems, passed_g, zones, tok = _pair_in_start([late_grads[n] for n in late_names])
        reduce_state.update(pair=(s_sems, r_sems, passed_g, zones))
        return tok

    def on_late_landed(after):
        gl, got = _pair_in_wait(*reduce_state["pair"], after)
        sums = [_pair_sum(g, s, core_arr, "pair_sum_" + n) for n, g, s in zip(late_names, gl, got)]
        s_sems, r_sems, parts, zones, tok = _exchange_start([s16 for _, s16 in sums], "grad_exchange_start")
        reduce_state.update(sums=sums, split=(s_sems, r_sems, parts, zones))
        return tok

    grad_x, grads, small = _sequence_step(
        x[0], loss_target[0], positions[0], vecs, mla_g_q + token[0, 0], mla_g_kv, swa_sinks, g_final, wts,
        late_weights, on_late_grads, on_late_landed)

    small["b_ada"] = small.pop("dmod")
    small_all = _all_gather(_pack_small(small))
    pk = lambda src: _pack_small({n: src[n] for n in SMALL if n != "loss" and n not in BIASES})
    off, n = _small_slots()["b_ada"]
    dmod_all = small_all.reshape(N_DEV, -1)[:, off:off + n].reshape(N_DEV, DEPTH, N_CHIPS, ncol)
    dm = _one_hot_pick(dmod_all, chip, axis=2).transpose(1, 0, 2)

    gl = [grads[n] for n in early]
    got = _grad_pair_in(gl)
    sums = [_pair_sum(g, s, core_arr, "pair_sum_" + n) for n, g, s in zip(early, gl, got)]
    e_sems, e_rems, e_parts, e_zones, e_tok = _exchange_start([s16 for _, s16 in sums], "mla_exchange_start")

    def finish(tensor_names, sums, others, behind):
        halves = [_chip_sum(s32, o, chip_arr, "chip_sum_" + n, behind) for n, (s32, _), o in zip(tensor_names, sums, others)]
        return {n: _adamw_halves(W[n], mine_h, got_h, M[n], V[n], core_arr, "adamw_" + n)
                for n, mine_h, got_h in zip(tensor_names, halves, _grad_pair_out(halves))}

    late_others = _exchange_wait(*reduce_state["split"], grad_x, "grad_exchange_wait")
    res = finish(late_names, reduce_state["sums"], late_others, e_tok)
    res["w_ada"] = _ada_grad_adamw(cond_all.T, dm, w_ada, m_w_ada, v_w_ada)
    small_res = _small_adamw(small_all, pk(W), pk(M), pk(V))
    early_others = _exchange_wait(e_sems, e_rems, e_parts, e_zones, res["w_ff2"][1], "mla_exchange_wait")
    res.update(finish(early, sums, early_others, None))

    for n, width in BIASES.items():
        g = _one_hot_pick(small_res[n][0].reshape(N_CHIPS, width // N_CHIPS), chip, axis=0).reshape(1, -1)
        res[n] = [g] + _adamw(W[n], g, M[n], V[n], "adamw_" + n)
    for name in order:
        if name not in res:
            res[name] = small_res[name]
    outs = [small_res["loss"][0], grad_x[None]]
    for k in range(4):
        outs += [res[name][k] for name in order]
    return tuple(outs)
```

```python
import functools
import math

import jax
import jax.numpy as jnp
import numpy as np
from jax import lax
from jax.experimental import pallas as pl
from jax.experimental.pallas import tpu as pltpu

D = 1024
DEPTH = 2
MLA_HEADS = 8
QK_NOPE = 128
QK_ROPE = 64
V_DIM = 128
Q_LORA = 384
KV_LORA = 256
ROPE_THETA = 10000.0
SWA_HEADS = 16
SWA_KV_HEADS = 4
SWA_HEAD_DIM = 64
SWA_GROUP = SWA_HEADS // SWA_KV_HEADS
WINDOW = 128
D_FF = 4 * D
EPS = 1e-6
ADAM_LR = 0.001
ADAM_B1 = 0.9
ADAM_B2 = 0.999
ADAM_EPS = 1e-08
ADAM_WD = 0.01
ADAM_STEP = 10

N_CHIPS = 4
N_DEV = 8
LANES = 128
QK_EXT = 256
MLA_SCALE = (QK_NOPE + QK_ROPE) ** -0.5
LOG2E = math.log2(math.e)
LN2 = math.log(2.0)
MLA_QSCALE = MLA_SCALE * LOG2E
ATTN_BLOCK = 2048
ATTN_SUB = 512
MLP_FWD_TILE = (1024, 1024)
MLP_BWD_TILE = (512, 1024)
DW_TOKENS = 4096
SWA_SCALE = SWA_HEAD_DIM ** -0.5
NEG = -1e30
MXU_DTYPE = jnp.bfloat16
VMEM_LIMIT = 56 * 1024 * 1024

R_SH1, R_SC1, R_GT1, R_SH2, R_SC2, R_GT2, R_GMIX, R_GMLP = range(8)
R_BO = 6


def _tile(n, pref):
    if n <= pref:
        return n
    for t in range(pref, 7, -1):
        if n % t == 0 and t % 8 == 0:
            return t
    return n


def _dot(a, b):
    return jnp.dot(a, b, preferred_element_type=jnp.float32)


def _dot_nt(a, b):
    return lax.dot_general(a, b, (((1,), (1,)), ((), ())), preferred_element_type=jnp.float32)


def _dot_tn(a, b):
    return lax.dot_general(a, b, (((0,), (0,)), ((), ())), preferred_element_type=jnp.float32)


def _rms(x):
    r = lax.rsqrt(jnp.mean(x * x, axis=-1, keepdims=True) + EPS)
    return x * r, r


def _rms_bwd(dxhat, xhat, r):
    return r * (dxhat - xhat * jnp.mean(dxhat * xhat, axis=-1, keepdims=True))


def _rowsum(v):
    return jnp.sum(v, axis=0, keepdims=True)


def _params(*sem):
    return pltpu.CompilerParams(dimension_semantics=sem, vmem_limit_bytes=VMEM_LIMIT)


def _full(shape):
    nd = len(shape)
    return pl.BlockSpec(shape, lambda *_: (0,) * nd)


def _rows(tm, cols):
    return pl.BlockSpec((tm, cols), lambda i, *_: (i, 0))


def _modulate_bwd(dh, x, vec_ref, r_g, r_sc, r_sh, ps_ref, dres):
    xhat, r = _rms(x)
    g = vec_ref[r_g:r_g + 1, :]
    n = xhat * g
    ps_ref[r_sh:r_sh + 1, :] += _rowsum(dh)
    ps_ref[r_sc:r_sc + 1, :] += _rowsum(dh * n)
    dn = dh * (1.0 + vec_ref[r_sc:r_sc + 1, :])
    ps_ref[r_g:r_g + 1, :] += _rowsum(dn * xhat)
    return dres + _rms_bwd(dn * g, xhat, r)


def _mla_pre(x, vec, wcat, g_q, g_kv, wuq, wukv, cs):
    T = x.shape[0]
    tm = _tile(T, 512)
    H = MLA_HEADS

    def body(x_ref, vec_ref, wcat_ref, gq_ref, gkv_ref, wuq_ref, wukv_ref, cs_ref, h_ref, z_ref, q_ref, k_ref, v_ref):
        xhat, _ = _rms(x_ref[...])
        h = xhat * vec_ref[R_GMIX:R_GMIX + 1, :] * (1.0 + vec_ref[R_SC1:R_SC1 + 1, :]) + vec_ref[R_SH1:R_SH1 + 1, :]
        hb = h.astype(MXU_DTYPE)
        h_ref[...] = hb
        z = _dot(hb, wcat_ref[...])
        z_ref[...] = z
        cq = (_rms(z[:, :Q_LORA])[0] * gq_ref[...]).astype(MXU_DTYPE)
        ckv = (_rms(z[:, Q_LORA:Q_LORA + KV_LORA])[0] * gkv_ref[...]).astype(MXU_DTYPE)
        cs_t = cs_ref[...]
        t = z[:, Q_LORA + KV_LORA:] * cs_t
        k_rope = (t + pltpu.roll(t, QK_ROPE, axis=1)).astype(MXU_DTYPE)
        low = lax.broadcasted_iota(jnp.int32, (1, LANES), 1) < QK_ROPE
        for hd in range(H):
            qf = _dot(cq, wuq_ref[hd])
            tq = qf[:, QK_NOPE:] * cs_t
            tq = tq + pltpu.roll(tq, QK_ROPE, axis=1)
            q_ref[hd, :, :QK_NOPE] = (qf[:, :QK_NOPE] * MLA_QSCALE).astype(MXU_DTYPE)
            q_ref[hd, :, QK_NOPE:] = jnp.where(low, tq * MLA_QSCALE, 0.0).astype(MXU_DTYPE)
            kvf = _dot(ckv, wukv_ref[hd])
            k_ref[hd, :, :QK_NOPE] = kvf[:, :QK_NOPE].astype(MXU_DTYPE)
            k_ref[hd, :, QK_NOPE:] = k_rope
            v_ref[hd] = kvf[:, QK_NOPE:].astype(MXU_DTYPE)

    zc = wcat.shape[1]
    return pl.pallas_call(
        body, name="mla_pre", grid=(T // tm,),
        in_specs=[_rows(tm, D), _full((8, D)), _full(wcat.shape), _full(g_q.shape), _full(g_kv.shape),
                  _full(wuq.shape), _full(wukv.shape), _rows(tm, LANES)],
        out_specs=[_rows(tm, D), _rows(tm, zc),
                   pl.BlockSpec((H, tm, QK_EXT), lambda i: (0, i, 0)),
                   pl.BlockSpec((H, tm, QK_EXT), lambda i: (0, i, 0)),
                   pl.BlockSpec((H, tm, V_DIM), lambda i: (0, i, 0))],
        out_shape=[jax.ShapeDtypeStruct((T, D), MXU_DTYPE), jax.ShapeDtypeStruct((T, zc), jnp.float32),
                   jax.ShapeDtypeStruct((H, T, QK_EXT), MXU_DTYPE), jax.ShapeDtypeStruct((H, T, QK_EXT), MXU_DTYPE),
                   jax.ShapeDtypeStruct((H, T, V_DIM), MXU_DTYPE)],
        compiler_params=_params("parallel"),
    )(x, vec, wcat, g_q, g_kv, wuq, wukv, cs)


def _mla_attn_fwd(q, k, v):
    H, T, _ = q.shape
    tb = _tile(T, ATTN_BLOCK)
    sub = min(ATTN_SUB, tb)
    ns, nb = tb // sub, T // tb
    pairs = [(i, j) for i in range(nb) for j in range(i + 1)]
    qi_tab = jnp.asarray([i for i, _ in pairs], jnp.int32)
    kj_tab = jnp.asarray([j for _, j in pairs], jnp.int32)

    def body(qi_ref, kj_ref, q_ref, k_ref, v_ref, o_ref, lse_ref, m_sc, l_sc, acc_sc):
        qi, kj = qi_ref[pl.program_id(1)], kj_ref[pl.program_id(1)]

        @pl.when(kj == 0)
        def _():
            m_sc[...] = jnp.full_like(m_sc, NEG)
            l_sc[...] = jnp.zeros_like(l_sc)
            acc_sc[...] = jnp.zeros_like(acc_sc)

        def update(r, kk, masked):
            rows, keys = pl.ds(r * sub, sub), pl.ds(kk * sub, sub)
            s = _dot_nt(q_ref[0, rows, :], k_ref[0, keys, :])
            if masked:
                row = lax.broadcasted_iota(jnp.int32, (sub, sub), 0)
                col = lax.broadcasted_iota(jnp.int32, (sub, sub), 1)
                s = jnp.where(col <= row, s, NEG)
            m_prev = m_sc[rows, :]
            m_new = jnp.maximum(m_prev, jnp.max(s, axis=1, keepdims=True))
            alpha = jnp.exp2(m_prev - m_new)
            p = jnp.exp2(s - jnp.tile(m_new, (1, sub // LANES)))
            l_sc[rows, :] = alpha * l_sc[rows, :] + jnp.sum(p, axis=1, keepdims=True)
            acc_sc[rows, :] = alpha * acc_sc[rows, :] + _dot(p.astype(MXU_DTYPE), v_ref[0, keys, :])
            m_sc[rows, :] = m_new

        @pl.when(kj < qi)
        def _():
            for kk in range(ns):
                for r in range(ns):
                    update(r, kk, False)

        @pl.when(kj == qi)
        def _():
            for kk in range(ns):
                for r in range(kk, ns):
                    update(r, kk, r == kk)
            l = l_sc[...]
            o_ref[...] = (acc_sc[...] / l).astype(o_ref.dtype)
            lse = m_sc[...] + jnp.log2(l)
            pick = (lax.broadcasted_iota(jnp.int32, (8, LANES), 1) == 0).astype(jnp.float32)
            row = lax.dot_general(pick, lse, (((1,), (1,)), ((), ())), precision=lax.Precision.HIGHEST,
                                  preferred_element_type=jnp.float32)
            lse_ref[0] = row[0:1, :]

    q_idx = lambda h, p, qi_ref, kj_ref: (h, qi_ref[p], 0)
    kv_idx = lambda h, p, qi_ref, kj_ref: (h, kj_ref[p], 0)
    return pl.pallas_call(
        body, name="mla_attn_fwd",
        grid_spec=pltpu.PrefetchScalarGridSpec(
            num_scalar_prefetch=2, grid=(H, len(pairs)),
            in_specs=[pl.BlockSpec((1, tb, QK_EXT), q_idx), pl.BlockSpec((1, tb, QK_EXT), kv_idx),
                      pl.BlockSpec((1, tb, V_DIM), kv_idx)],
            out_specs=[pl.BlockSpec((tb, V_DIM), lambda h, p, qi_ref, kj_ref: (qi_ref[p], h)),
                       pl.BlockSpec((1, 1, tb), lambda h, p, qi_ref, kj_ref: (h, 0, qi_ref[p]))],
            scratch_shapes=[pltpu.VMEM((tb, LANES), jnp.float32), pltpu.VMEM((tb, LANES), jnp.float32),
                            pltpu.VMEM((tb, V_DIM), jnp.float32)]),
        out_shape=[jax.ShapeDtypeStruct((T, H * V_DIM), MXU_DTYPE), jax.ShapeDtypeStruct((H, 1, T), jnp.float32)],
        compiler_params=_params("parallel", "arbitrary"),
    )(qi_tab, kj_tab, q, k, v)


def _post_attn(o, x, w_o, bias, vec, o_transposed=False):
    T = x.shape[0]
    tm = _tile(T, 512)
    o_spec = pl.BlockSpec((D, tm), lambda i: (0, i)) if o_transposed else _rows(tm, D)

    def body(o_ref, x_ref, w_ref, b_ref, vec_ref, y_ref, xm_ref, h_ref):
        y = (_dot_tn if o_transposed else _dot)(o_ref[...], w_ref[...]) + b_ref[...]
        y_ref[...] = y.astype(y_ref.dtype)
        xm = x_ref[...] + vec_ref[R_GT1:R_GT1 + 1, :] * y
        xm_ref[...] = xm
        xhat, _ = _rms(xm)
        h = xhat * vec_ref[R_GMLP:R_GMLP + 1, :] * (1.0 + vec_ref[R_SC2:R_SC2 + 1, :]) + vec_ref[R_SH2:R_SH2 + 1, :]
        h_ref[...] = h.astype(h_ref.dtype)

    return pl.pallas_call(
        body, name="post_attn", grid=(T // tm,),
        in_specs=[o_spec, _rows(tm, D), _full((D, D)), _full((1, D)), _full((8, D))],
        out_specs=[_rows(tm, D), _rows(tm, D), _rows(tm, D)],
        out_shape=[jax.ShapeDtypeStruct((T, D), MXU_DTYPE), jax.ShapeDtypeStruct((T, D), jnp.float32),
                   jax.ShapeDtypeStruct((T, D), MXU_DTYPE)],
        compiler_params=_params("parallel"),
    )(o, x, w_o, bias, vec)


def _ff_specs(tf):
    per = D_FF // N_CHIPS // tf
    w1 = pl.BlockSpec((None, D, tf), lambda i, f: (f // per, 0, f % per))
    w2 = pl.BlockSpec((None, tf, D), lambda i, f: (f // per, f % per, 0))
    return w1, w2


def _mlp_fwd(h2, w1, w2, xm, vec):
    T = h2.shape[0]
    tm = _tile(T, MLP_FWD_TILE[0])
    tf = _tile(D_FF // N_CHIPS, MLP_FWD_TILE[1])
    nf = D_FF // tf
    w1_spec, w2_spec = _ff_specs(tf)

    def body(h_ref, w1_ref, w2_ref, xm_ref, vec_ref, a_ref, y_ref, xo_ref, acc):
        f = pl.program_id(1)

        @pl.when(f == 0)
        def _():
            acc[...] = jnp.zeros_like(acc)

        u = jnp.maximum(_dot(h_ref[...], w1_ref[...]), 0.0)
        ab = (u * u).astype(MXU_DTYPE)
        a_ref[...] = ab
        acc[...] += _dot(ab, w2_ref[...])

        @pl.when(f == nf - 1)
        def _():
            y = acc[...]
            y_ref[...] = y.astype(y_ref.dtype)
            xo_ref[...] = xm_ref[...] + vec_ref[R_GT2:R_GT2 + 1, :] * y

    return pl.pallas_call(
        body, name="mlp_fwd", grid=(T // tm, nf),
        in_specs=[_rows(tm, D), w1_spec, w2_spec, _rows(tm, D), _full((8, D))],
        out_specs=[pl.BlockSpec((tm, tf), lambda i, f: (i, f)), _rows(tm, D), _rows(tm, D)],
        out_shape=[jax.ShapeDtypeStruct((T, D_FF), MXU_DTYPE), jax.ShapeDtypeStruct((T, D), MXU_DTYPE),
                   jax.ShapeDtypeStruct((T, D), jnp.float32)],
        scratch_shapes=[pltpu.VMEM((tm, D), jnp.float32)],
        compiler_params=_params("parallel", "arbitrary"),
    )(h2, w1, w2, xm, vec)


def _swa_pre(x, vec, w_qkv, b_qkv):
    T = x.shape[0]
    tm = _tile(T, 512)
    nq = SWA_HEADS * SWA_HEAD_DIM
    nk = SWA_KV_HEADS * SWA_HEAD_DIM
    wq_t, w_kv = w_qkv[:, :nq].T, w_qkv[:, nq:]
    bq_col, b_kv = b_qkv[:, :nq].reshape(nq, 1), b_qkv[:, nq:]

    def body(x_ref, vec_ref, wq_ref, wkv_ref, bq_ref, bkv_ref, h_ref, qt_ref, k_ref, v_ref):
        xhat, _ = _rms(x_ref[...])
        h = xhat * vec_ref[R_GMIX:R_GMIX + 1, :] * (1.0 + vec_ref[R_SC1:R_SC1 + 1, :]) + vec_ref[R_SH1:R_SH1 + 1, :]
        hb = h.astype(MXU_DTYPE)
        h_ref[...] = hb
        qt_ref[...] = ((_dot_nt(wq_ref[...], hb) + bq_ref[...]) * SWA_SCALE).astype(MXU_DTYPE)
        kv = _dot(hb, wkv_ref[...]) + bkv_ref[...]
        k_ref[...] = kv[:, :nk].astype(MXU_DTYPE)
        v_ref[...] = kv[:, nk:].astype(MXU_DTYPE)

    return pl.pallas_call(
        body, name="swa_pre", grid=(T // tm,),
        in_specs=[_rows(tm, D), _full((8, D)), _full(wq_t.shape), _full(w_kv.shape), _full(bq_col.shape),
                  _full(b_kv.shape)],
        out_specs=[_rows(tm, D), pl.BlockSpec((nq, tm), lambda i: (0, i)), _rows(tm, nk), _rows(tm, nk)],
        out_shape=[jax.ShapeDtypeStruct((T, D), MXU_DTYPE), jax.ShapeDtypeStruct((nq, T), MXU_DTYPE),
                   jax.ShapeDtypeStruct((T, nk), MXU_DTYPE), jax.ShapeDtypeStruct((T, nk), MXU_DTYPE)],
        compiler_params=_params("parallel"),
    )(x, vec, wq_t, w_kv, bq_col, b_kv)


def _swa_bias():
    W = WINDOW
    slopes = 2.0 ** (-8.0 * np.arange(1, SWA_HEADS + 1) / SWA_HEADS)
    dist = W + np.arange(W)[None, :] - np.arange(2 * W)[:, None]
    inside = (dist >= 0) & (dist < W)
    bias = np.where(inside[None], -slopes[:, None, None] * dist[None].astype(np.float64), NEG)
    bias = bias.reshape(SWA_KV_HEADS, SWA_GROUP, 2 * W, W).transpose(0, 2, 1, 3)
    return jnp.asarray(bias.reshape(SWA_KV_HEADS, 2 * W, SWA_GROUP * W), jnp.float32)


SWA_STEP_BLOCKS = 4


def _swa_blocks(T):
    nb = T // WINDOW
    return next(b for b in (SWA_STEP_BLOCKS, 2, 1) if nb % b == 0)


def _swa_views(b, qt_ref, kp_ref, kc_ref):
    W = WINDOW
    prev = kp_ref if b == 0 else kc_ref.at[pl.ds((b - 1) * W, W), :]
    return qt_ref.at[:, pl.ds(b * W, W)], prev, kc_ref.at[pl.ds(b * W, W), :]


def _swa_probs(has_prev, kh, qt_ref, kp_ref, kc_ref, bias_ref, sink_ref):
    W, Dh, G = WINDOW, SWA_HEAD_DIM, SWA_GROUP
    qt = jnp.concatenate([qt_ref[(kh * G + g) * Dh:(kh * G + g + 1) * Dh, :] for g in range(G)], axis=1)
    kb = jnp.concatenate([kp_ref[:, kh * Dh:(kh + 1) * Dh], kc_ref[:, kh * Dh:(kh + 1) * Dh]], axis=0)
    s = _dot(kb, qt) + bias_ref[kh]
    if has_prev is not True:
        key = lax.broadcasted_iota(jnp.int32, (2 * W, 1), 0)
        s = jnp.where((key >= W) | has_prev, s, NEG)
    sink = sink_ref[kh]
    m = jnp.maximum(jnp.max(s, axis=0, keepdims=True), sink)
    p = jnp.exp(s - m)
    p_sink = jnp.exp(sink - m)
    inv = 1.0 / (jnp.sum(p, axis=0, keepdims=True) + p_sink)
    return qt, kb, p * inv, p_sink * inv


def _swa_attn_fwd(qt, k, v, bias, sink_rows):
    T = qt.shape[1]
    W, Dh, G, Hk = WINDOW, SWA_HEAD_DIM, SWA_GROUP, SWA_KV_HEADS
    nk = Hk * Dh

    nb = _swa_blocks(T)

    def body(qt_ref, kp_ref, kc_ref, vp_ref, vc_ref, bias_ref, sink_ref, ot_ref):
        n = pl.program_id(0)
        for b in range(nb):
            q_b, kp_b, kc_b = _swa_views(b, qt_ref, kp_ref, kc_ref)
            _, vp_b, vc_b = _swa_views(b, qt_ref, vp_ref, vc_ref)
            for kh in range(Hk):
                _, _, pn, _ = _swa_probs(True if b else n > 0, kh, q_b, kp_b, kc_b, bias_ref, sink_ref)
                vb = jnp.concatenate([vp_b[:, kh * Dh:(kh + 1) * Dh], vc_b[:, kh * Dh:(kh + 1) * Dh]], axis=0)
                ot = _dot_tn(vb, pn.astype(MXU_DTYPE))
                for g in range(G):
                    rows = pl.ds((kh * G + g) * Dh, Dh)
                    ot_ref[rows, pl.ds(b * W, W)] = ot[:, g * W:(g + 1) * W].astype(ot_ref.dtype)

    prev = lambda n: (jnp.maximum(n * nb - 1, 0), 0)
    cur = lambda n: (n, 0)
    col = lambda n: (0, n)
    return pl.pallas_call(
        body, name="swa_attn_fwd", grid=(T // (nb * W),),
        in_specs=[pl.BlockSpec((D, nb * W), col), pl.BlockSpec((W, nk), prev), pl.BlockSpec((nb * W, nk), cur),
                  pl.BlockSpec((W, nk), prev), pl.BlockSpec((nb * W, nk), cur), _full(bias.shape),
                  _full(sink_rows.shape)],
        out_specs=pl.BlockSpec((D, nb * W), col),
        out_shape=jax.ShapeDtypeStruct((D, T), MXU_DTYPE),
        compiler_params=_params("parallel"),
    )(qt, k, k, v, v, bias, sink_rows)


def _final_loss(x, tgt, g):
    T = x.shape[0]
    tm = _tile(T, 512)

    def body(x_ref, t_ref, g_ref, loss_ref, dx_ref, dg_ref):
        @pl.when(pl.program_id(0) == 0)
        def _():
            loss_ref[...] = jnp.zeros_like(loss_ref)
            dg_ref[...] = jnp.zeros_like(dg_ref)

        xhat, r = _rms(x_ref[...])
        gv = g_ref[...]
        e = xhat * gv - t_ref[...]
        loss_ref[...] += 0.5 * jnp.sum(jnp.mean(e * e, axis=-1, keepdims=True), axis=0, keepdims=True)
        dy = e * (1.0 / D)
        dg_ref[...] += _rowsum(dy * xhat)
        dx_ref[...] = _rms_bwd(dy * gv, xhat, r)

    return pl.pallas_call(
        body, name="final_loss", grid=(T // tm,),
        in_specs=[_rows(tm, D), _rows(tm, D), _full((1, D))],
        out_specs=[_full((8, LANES)), _rows(tm, D), _full((1, D))],
        out_shape=[jax.ShapeDtypeStruct((8, LANES), jnp.float32), jax.ShapeDtypeStruct((T, D), jnp.float32),
                   jax.ShapeDtypeStruct((1, D), jnp.float32)],
        compiler_params=_params("arbitrary"),
    )(x, tgt, g)


def _mlp_bwd(dxo, y2, a, w1, w2, xm, vec):
    T = dxo.shape[0]
    tm = _tile(T, MLP_BWD_TILE[0])
    tf = _tile(D_FF // N_CHIPS, MLP_BWD_TILE[1])
    nf = D_FF // tf
    w1_spec, w2_spec = _ff_specs(tf)

    def body(dxo_ref, y_ref, a_ref, w1_ref, w2_ref, xm_ref, vec_ref, du_ref, dy_ref, dxm_ref, ps_ref, dyb, acc):
        i, f = pl.program_id(0), pl.program_id(1)

        @pl.when((i == 0) & (f == 0))
        def _():
            ps_ref[...] = jnp.zeros_like(ps_ref)

        @pl.when(f == 0)
        def _():
            dxo_t = dxo_ref[...]
            d = (dxo_t * vec_ref[R_GT2:R_GT2 + 1, :]).astype(MXU_DTYPE)
            dyb[...] = d
            dy_ref[...] = d
            acc[...] = jnp.zeros_like(acc)
            ps_ref[R_GT2:R_GT2 + 1, :] += _rowsum(dxo_t * y_ref[...].astype(jnp.float32))

        da = _dot_nt(dyb[...], w2_ref[...])
        dub = (da * (2.0 * jnp.sqrt(a_ref[...].astype(jnp.float32)))).astype(MXU_DTYPE)
        du_ref[...] = dub
        acc[...] += _dot_nt(dub, w1_ref[...])

        @pl.when(f == nf - 1)
        def _():
            dxm_ref[...] = _modulate_bwd(acc[...], xm_ref[...], vec_ref, R_GMLP, R_SC2, R_SH2, ps_ref, dxo_ref[...])

    return pl.pallas_call(
        body, name="mlp_bwd", grid=(T // tm, nf),
        in_specs=[_rows(tm, D), _rows(tm, D), pl.BlockSpec((tm, tf), lambda i, f: (i, f)), w1_spec, w2_spec,
                  _rows(tm, D), _full((8, D))],
        out_specs=[pl.BlockSpec((tm, tf), lambda i, f: (i, f)), _rows(tm, D), _rows(tm, D), _full((8, D))],
        out_shape=[jax.ShapeDtypeStruct((T, D_FF), MXU_DTYPE), jax.ShapeDtypeStruct((T, D), MXU_DTYPE),
                   jax.ShapeDtypeStruct((T, D), jnp.float32), jax.ShapeDtypeStruct((8, D), jnp.float32)],
        scratch_shapes=[pltpu.VMEM((tm, D), MXU_DTYPE), pltpu.VMEM((tm, D), jnp.float32)],
        compiler_params=_params("arbitrary", "arbitrary"),
    )(dxo, y2, a, w1, w2, xm, vec)


def _mm_tn(a, g, name, split=None, layers=1, layer=0, into=None, a_transposed=False):
    K, T = a.shape if a_transposed else a.shape[::-1]
    N = g.shape[1]
    kq = K // N_CHIPS if split == "rows" else K
    nq = N // N_CHIPS if split == "cols" else N
    bk, bn, bt = _tile(kq, 1024), _tile(nq, 1024), _tile(T, DW_TOKENS)
    if nq % bn or bn % LANES:
        bn = nq
    kper, nper = kq // bk, nq // bn

    def body(*refs):
        a_ref, g_ref, o_ref = refs[0], refs[1], refs[-1]

        @pl.when(pl.program_id(2) == 0)
        def _():
            o_ref[...] = jnp.zeros_like(o_ref)

        o_ref[...] += (_dot if a_transposed else _dot_tn)(a_ref[...], g_ref[...])

    a_spec = pl.BlockSpec((bk, bt), lambda k, n, t: (k, t)) if a_transposed else pl.BlockSpec((bt, bk), lambda k, n, t: (t, k))
    in_specs = [a_spec, pl.BlockSpec((bt, bn), lambda k, n, t: (t, n))]
    args = [a, g]
    aliases = {}
    if split is None:
        out_spec = pl.BlockSpec((bk, bn), lambda k, n, t: (k, n))
        out_shape = jax.ShapeDtypeStruct((K, N), jnp.float32)
    else:
        if split == "cols":
            idx = lambda k, n, t: (n // nper, layer, k, n % nper)
        else:
            idx = lambda k, n, t: (k // kper, layer, k % kper, n)
        out_spec = pl.BlockSpec((None, None, bk, bn), idx)
        out_shape = jax.ShapeDtypeStruct((N_CHIPS, layers, kq, nq), jnp.float32)
        if into is not None:
            in_specs.append(pl.BlockSpec(memory_space=pl.ANY))
            args.append(into)
            aliases = {2: 0}
    return pl.pallas_call(
        body, name=name, grid=(K // bk, N // bn, T // bt), in_specs=in_specs, out_specs=out_spec, out_shape=out_shape,
        input_output_aliases=aliases, compiler_params=_params("parallel", "parallel", "arbitrary"),
    )(*args)


def _attn_out_bwd(dxm, y1, o, w_o, vec, with_delta):
    T = dxm.shape[0]
    tm = _tile(T, 512)
    H = MLA_HEADS

    def body(dxm_ref, y_ref, w_ref, vec_ref, *refs):
        o_ref = refs[0] if with_delta else None
        dy_ref, do_ref, ps_ref, *delta_ref = refs[1:] if with_delta else refs

        @pl.when(pl.program_id(0) == 0)
        def _():
            ps_ref[...] = jnp.zeros_like(ps_ref)

        dxm_t = dxm_ref[...]
        dy = dxm_t * vec_ref[R_GT1:R_GT1 + 1, :]
        ps_ref[R_GT1:R_GT1 + 1, :] += _rowsum(dxm_t * y_ref[...].astype(jnp.float32))
        ps_ref[R_BO:R_BO + 1, :] += _rowsum(dy)
        dyb = dy.astype(MXU_DTYPE)
        dy_ref[...] = dyb
        if not with_delta:
            do_ref[...] = _dot_nt(w_ref[...], dyb).astype(do_ref.dtype)
        else:
            do = _dot_nt(dyb, w_ref[...])
            do_ref[...] = do.astype(do_ref.dtype)
            of = o_ref[...].astype(jnp.float32)
            ones = jnp.ones((8, V_DIM), jnp.float32)
            for hd in range(H):
                sl = slice(hd * V_DIM, (hd + 1) * V_DIM)
                d = lax.dot_general(ones, do[:, sl] * of[:, sl], (((1,), (1,)), ((), ())),
                                    precision=lax.Precision.HIGHEST, preferred_element_type=jnp.float32)
                delta_ref[0][hd] = d[0:1, :]

    out_specs = [_rows(tm, D), _rows(tm, D), _full((8, D))]
    out_shape = [jax.ShapeDtypeStruct((T, D), MXU_DTYPE), jax.ShapeDtypeStruct((T, D), MXU_DTYPE),
                 jax.ShapeDtypeStruct((8, D), jnp.float32)]
    if not with_delta:
        out_specs[1] = pl.BlockSpec((D, tm), lambda i: (0, i))
        out_shape[1] = jax.ShapeDtypeStruct((D, T), MXU_DTYPE)
    if with_delta:
        out_specs.append(pl.BlockSpec((H, 1, tm), lambda i: (0, 0, i)))
        out_shape.append(jax.ShapeDtypeStruct((H, 1, T), jnp.float32))
    return pl.pallas_call(
        body, name="attn_out_bwd_mla" if with_delta else "attn_out_bwd_swa", grid=(T // tm,),
        in_specs=[_rows(tm, D), _rows(tm, D), _full((D, D)), _full((8, D))] + ([_rows(tm, D)] if with_delta else []),
        out_specs=out_specs, out_shape=out_shape,
        compiler_params=_params("arbitrary"),
    )(dxm, y1, w_o, vec, *([o] if with_delta else []))


def _mla_attn_bwd(q, k, v, do, lse, delta):
    H, T, _ = q.shape
    tb = _tile(T, ATTN_BLOCK)
    sub = min(ATTN_SUB, tb)
    ns, nb = tb // sub, T // tb

    pairs = [(j, i) for j in range(nb) for i in range(j, nb)]
    kj_tab = jnp.asarray([j for j, _ in pairs], jnp.int32)
    qi_tab = jnp.asarray([i for _, i in pairs], jnp.int32)

    def body(kj_ref, qi_ref, q_ref, k_ref, v_ref, do_ref, lse_ref, dl_ref, dq_ref, dk_ref, dv_ref, dk_acc, dv_acc):
        j, i = kj_ref[pl.program_id(1)], qi_ref[pl.program_id(1)]

        @pl.when((j == 0) & (i == 0))
        def _():
            dq_ref[...] = jnp.zeros_like(dq_ref)

        def update(kk, r, masked):
            keys, rows = pl.ds(kk * sub, sub), pl.ds(r * sub, sub)
            kb, qb, dob = k_ref[0, keys, :], q_ref[0, rows, :], do_ref[rows, :]
            st = _dot_nt(kb, qb)
            if masked:
                row = lax.broadcasted_iota(jnp.int32, (sub, sub), 0)
                col = lax.broadcasted_iota(jnp.int32, (sub, sub), 1)
                st = jnp.where(row <= col, st, NEG)
            pt = jnp.exp2(st - lse_ref[0, :, rows])
            dv_acc[keys, :] += _dot(pt.astype(MXU_DTYPE), dob)
            dpt = _dot_nt(v_ref[0, keys, :], dob)
            dst = (pt * (dpt - dl_ref[0, :, rows])).astype(MXU_DTYPE)
            dk_acc[keys, :] += _dot(dst, qb)
            q_rows = pl.ds(pl.multiple_of(i * tb + r * sub, sub), sub)
            dq_ref[0, q_rows, :] += _dot_tn(dst, kb)

        @pl.when(i == j)
        def _():
            dk_acc[...] = jnp.zeros_like(dk_acc)
            dv_acc[...] = jnp.zeros_like(dv_acc)
            for r in range(ns):
                for kk in range(r + 1):
                    update(kk, r, kk == r)

        @pl.when(i > j)
        def _():
            for r in range(ns):
                for kk in range(ns):
                    update(kk, r, False)

        @pl.when(i == nb - 1)
        def _():
            dk_ref[0] = (dk_acc[...] * LN2).astype(dk_ref.dtype)
            dv_ref[0] = dv_acc[...].astype(dv_ref.dtype)

    q_idx = lambda h, p, kj_ref, qi_ref: (h, qi_ref[p], 0)
    kv_idx = lambda h, p, kj_ref, qi_ref: (h, kj_ref[p], 0)
    stat_idx = lambda h, p, kj_ref, qi_ref: (h, 0, qi_ref[p])
    return pl.pallas_call(
        body, name="mla_attn_bwd",
        grid_spec=pltpu.PrefetchScalarGridSpec(
            num_scalar_prefetch=2, grid=(H, len(pairs)),
            in_specs=[pl.BlockSpec((1, tb, QK_EXT), q_idx), pl.BlockSpec((1, tb, QK_EXT), kv_idx),
                      pl.BlockSpec((1, tb, V_DIM), kv_idx),
                      pl.BlockSpec((tb, V_DIM), lambda h, p, kj_ref, qi_ref: (qi_ref[p], h)),
                      pl.BlockSpec((1, 1, tb), stat_idx), pl.BlockSpec((1, 1, tb), stat_idx)],
            out_specs=[pl.BlockSpec((1, T, QK_EXT), lambda h, p, kj_ref, qi_ref: (h, 0, 0)),
                       pl.BlockSpec((1, tb, QK_EXT), kv_idx), pl.BlockSpec((1, tb, V_DIM), kv_idx)],
            scratch_shapes=[pltpu.VMEM((tb, QK_EXT), jnp.float32), pltpu.VMEM((tb, V_DIM), jnp.float32)]),
        out_shape=[jax.ShapeDtypeStruct((H, T, QK_EXT), jnp.float32), jax.ShapeDtypeStruct((H, T, QK_EXT), MXU_DTYPE),
                   jax.ShapeDtypeStruct((H, T, V_DIM), MXU_DTYPE)],
        compiler_params=_params("parallel", "arbitrary"),
    )(kj_tab, qi_tab, q, k, v, do, lse, delta)


def _mla_pre_bwd(x, dxm, vec, hb, z, dq, dk, dv, cs, wcat, g_q, g_kv, wuq, wukv):
    T = x.shape[0]
    tm = _tile(T, 256)
    H = MLA_HEADS
    zc = wcat.shape[1]

    def body(x_ref, dxm_ref, vec_ref, h_ref, z_ref, dq_ref, dk_ref, dv_ref, cs_ref, wcat_ref, gq_ref, gkv_ref,
             wuq_ref, wukv_ref, dx_ref, ps_ref, dgq_ref, dgkv_ref, dwcat_ref, dwuq_ref, dwukv_ref):
        @pl.when(pl.program_id(0) == 0)
        def _():
            for ref in (ps_ref, dgq_ref, dgkv_ref, dwcat_ref, dwuq_ref, dwukv_ref):
                ref[...] = jnp.zeros_like(ref)

        z = z_ref[...]
        cs_t = cs_ref[...]
        cqhat, rq = _rms(z[:, :Q_LORA])
        ckhat, rk = _rms(z[:, Q_LORA:Q_LORA + KV_LORA])
        gq, gkv = gq_ref[...], gkv_ref[...]
        cq = (cqhat * gq).astype(MXU_DTYPE)
        ckv = (ckhat * gkv).astype(MXU_DTYPE)
        dcq = jnp.zeros((tm, Q_LORA), jnp.float32)
        dckv = jnp.zeros((tm, KV_LORA), jnp.float32)
        dkr = jnp.zeros((tm, LANES), jnp.float32)
        for hd in range(H):
            dqh = dq_ref[hd] * MLA_SCALE
            gqh = jnp.concatenate([dqh[:, :QK_NOPE], dqh[:, QK_NOPE:] * cs_t], axis=1).astype(MXU_DTYPE)
            dcq += _dot_nt(gqh, wuq_ref[hd])
            dwuq_ref[hd] += _dot_tn(cq, gqh)
            dkh = dk_ref[hd]
            gkvh = jnp.concatenate([dkh[:, :QK_NOPE], dv_ref[hd]], axis=1)
            dckv += _dot_nt(gkvh, wukv_ref[hd])
            dwukv_ref[hd] += _dot_tn(ckv, gkvh)
            dkr += dkh[:, QK_NOPE:].astype(jnp.float32)
        dgq_ref[...] += _rowsum(dcq * cqhat)
        dgkv_ref[...] += _rowsum(dckv * ckhat)
        dcq_pre = _rms_bwd(dcq * gq, cqhat, rq)
        dckv_pre = _rms_bwd(dckv * gkv, ckhat, rk)
        dkr2 = (dkr + pltpu.roll(dkr, QK_ROPE, axis=1)) * cs_t
        dz = jnp.concatenate([dcq_pre, dckv_pre, dkr2], axis=1).astype(MXU_DTYPE)
        dwcat_ref[...] += _dot_tn(h_ref[...], dz)
        dh = _dot_nt(dz, wcat_ref[...])
        dx_ref[...] = _modulate_bwd(dh, x_ref[...], vec_ref, R_GMIX, R_SC1, R_SH1, ps_ref, dxm_ref[...])

    hblk = lambda w: pl.BlockSpec((H, tm, w), lambda i: (0, i, 0))
    return pl.pallas_call(
        body, name="mla_pre_bwd", grid=(T // tm,),
        in_specs=[_rows(tm, D), _rows(tm, D), _full((8, D)), _rows(tm, D), _rows(tm, zc), hblk(QK_EXT), hblk(QK_EXT),
                  hblk(V_DIM), _rows(tm, LANES), _full(wcat.shape), _full(g_q.shape), _full(g_kv.shape),
                  _full(wuq.shape), _full(wukv.shape)],
        out_specs=[_rows(tm, D), _full((8, D)), _full(g_q.shape), _full(g_kv.shape), _full(wcat.shape),
                   _full(wuq.shape), _full(wukv.shape)],
        out_shape=[jax.ShapeDtypeStruct((T, D), jnp.float32), jax.ShapeDtypeStruct((8, D), jnp.float32),
                   jax.ShapeDtypeStruct(g_q.shape, jnp.float32), jax.ShapeDtypeStruct(g_kv.shape, jnp.float32),
                   jax.ShapeDtypeStruct(wcat.shape, jnp.float32), jax.ShapeDtypeStruct(wuq.shape, jnp.float32),
                   jax.ShapeDtypeStruct(wukv.shape, jnp.float32)],
        compiler_params=_params("arbitrary"),
    )(x, dxm, vec, hb, z, dq, dk, dv, cs, wcat, g_q, g_kv, wuq, wukv)


def _swa_attn_bwd(qt, k, v, dot_, bias, sink_rows):
    T = qt.shape[1]
    W, Dh, G, Hk = WINDOW, SWA_HEAD_DIM, SWA_GROUP, SWA_KV_HEADS
    nk = Hk * Dh
    nb = _swa_blocks(T)

    def body(qt_ref, kp_ref, kc_ref, vp_ref, vc_ref, dot_ref, bias_ref, sink_ref, dqt_ref, dk_ref, dv_ref, dsink_ref):
        n = pl.program_id(0)

        @pl.when(n == 0)
        def _():
            dk_ref[...] = jnp.zeros_like(dk_ref)
            dv_ref[...] = jnp.zeros_like(dv_ref)
            dsink_ref[...] = jnp.zeros_like(dsink_ref)

        def add_rows(first_row, dkb_part, dvb_part):
            rows = pl.ds(pl.multiple_of(first_row, W), W)
            dk_ref[rows, :] += dkb_part
            dv_ref[rows, :] += dvb_part

        for b in range(nb):
            q_b, kp_b, kc_b = _swa_views(b, qt_ref, kp_ref, kc_ref)
            do_b, vp_b, vc_b = _swa_views(b, dot_ref, vp_ref, vc_ref)
            dks, dvs = [], []
            for kh in range(Hk):
                qt, kb, pn, p_sink = _swa_probs(True if b else n > 0, kh, q_b, kp_b, kc_b, bias_ref, sink_ref)
                vb = jnp.concatenate([vp_b[:, kh * Dh:(kh + 1) * Dh], vc_b[:, kh * Dh:(kh + 1) * Dh]], axis=0)
                dot_h = jnp.concatenate([do_b[(kh * G + g) * Dh:(kh * G + g + 1) * Dh, :] for g in range(G)], axis=1)
                dp = _dot(vb, dot_h)
                delta = jnp.sum(pn * dp, axis=0, keepdims=True)
                dsb = (pn * (dp - delta)).astype(MXU_DTYPE)
                dsink_ref[kh] += -p_sink * delta
                dqt = _dot_tn(kb, dsb) * SWA_SCALE
                for g in range(G):
                    dqt_ref[pl.ds((kh * G + g) * Dh, Dh), pl.ds(b * W, W)] = dqt[:, g * W:(g + 1) * W]
                dks.append(_dot_nt(dsb, qt))
                dvs.append(_dot_nt(pn.astype(MXU_DTYPE), dot_h))
            dkb = jnp.concatenate(dks, axis=1)
            dvb = jnp.concatenate(dvs, axis=1)
            add_rows((n * nb + b) * W, dkb[W:], dvb[W:])
            if b:
                add_rows((n * nb + b - 1) * W, dkb[:W], dvb[:W])
            else:
                @pl.when(n > 0)
                def _():
                    add_rows((n * nb - 1) * W, dkb[:W], dvb[:W])

    prev = lambda n: (jnp.maximum(n * nb - 1, 0), 0)
    cur = lambda n: (n, 0)
    col = lambda n: (0, n)
    return pl.pallas_call(
        body, name="swa_attn_bwd", grid=(T // (nb * W),),
        in_specs=[pl.BlockSpec((D, nb * W), col), pl.BlockSpec((W, nk), prev), pl.BlockSpec((nb * W, nk), cur),
                  pl.BlockSpec((W, nk), prev), pl.BlockSpec((nb * W, nk), cur), pl.BlockSpec((D, nb * W), col),
                  _full(bias.shape), _full(sink_rows.shape)],
        out_specs=[pl.BlockSpec((D, nb * W), col), _full((T, nk)), _full((T, nk)), _full(sink_rows.shape)],
        out_shape=[jax.ShapeDtypeStruct((D, T), jnp.float32), jax.ShapeDtypeStruct((T, nk), jnp.float32),
                   jax.ShapeDtypeStruct((T, nk), jnp.float32), jax.ShapeDtypeStruct(sink_rows.shape, jnp.float32)],
        compiler_params=_params("arbitrary"),
    )(qt, k, k, v, v, dot_, bias, sink_rows)


def _swa_pre_bwd(x, dxm, vec, dq_t, dk, dv, w_qkv):
    T = x.shape[0]
    tm = _tile(T, 512)
    nq = SWA_HEADS * SWA_HEAD_DIM
    nk = SWA_KV_HEADS * SWA_HEAD_DIM
    nqkv = nq + 2 * nk

    def body(x_ref, dxm_ref, vec_ref, dq_ref, dk_ref, dv_ref, w_ref, dx_ref, dqkv_ref, ps_ref, db_ref):
        @pl.when(pl.program_id(0) == 0)
        def _():
            ps_ref[...] = jnp.zeros_like(ps_ref)
            db_ref[...] = jnp.zeros_like(db_ref)

        dqkv = jnp.concatenate([dq_ref[...].T, dk_ref[...], dv_ref[...]], axis=1)
        db_ref[...] += _rowsum(dqkv)
        dqkv_b = dqkv.astype(MXU_DTYPE)
        dqkv_ref[...] = dqkv_b
        dh = _dot_nt(dqkv_b, w_ref[...])
        dx_ref[...] = _modulate_bwd(dh, x_ref[...], vec_ref, R_GMIX, R_SC1, R_SH1, ps_ref, dxm_ref[...])

    return pl.pallas_call(
        body, name="swa_pre_bwd", grid=(T // tm,),
        in_specs=[_rows(tm, D), _rows(tm, D), _full((8, D)), pl.BlockSpec((nq, tm), lambda i: (0, i)), _rows(tm, nk),
                  _rows(tm, nk), _full(w_qkv.shape)],
        out_specs=[_rows(tm, D), _rows(tm, nqkv), _full((8, D)), _full((1, nqkv))],
        out_shape=[jax.ShapeDtypeStruct((T, D), jnp.float32), jax.ShapeDtypeStruct((T, nqkv), MXU_DTYPE),
                   jax.ShapeDtypeStruct((8, D), jnp.float32), jax.ShapeDtypeStruct((1, nqkv), jnp.float32)],
        compiler_params=_params("arbitrary"),
    )(x, dxm, vec, dq_t, dk, dv, w_qkv)


def _rot_cols(w):
    half = QK_ROPE // 2
    return jnp.concatenate([-w[..., half:], w[..., :half]], axis=-1)


def _unrot_grad(d_rope, d_rot):
    half = QK_ROPE // 2
    return d_rope + jnp.concatenate([d_rot[..., half:], -d_rot[..., :half]], axis=-1)


def _rope_table(positions):
    half = QK_ROPE // 2
    inv_freq = ROPE_THETA ** (-jnp.arange(half, dtype=jnp.float32) / half)
    ang = positions.astype(jnp.float32)[:, None] * inv_freq
    cos, sin = jnp.cos(ang), jnp.sin(ang)
    return jnp.concatenate([cos, cos, sin, sin], axis=1)


def _sequence_step(x, tgt, positions, vecs, g_q, g_kv, sinks, g_final, wts, late_weights, on_late_grads, on_late_landed):
    H = MLA_HEADS
    cs = _rope_table(positions)
    w_dkv = wts["mla_w_dkv"]
    wcat = jnp.concatenate([wts["mla_w_dq"], w_dkv, _rot_cols(w_dkv[:, KV_LORA:])], axis=1)
    uq = wts["mla_w_uq"].reshape(Q_LORA, H, QK_NOPE + QK_ROPE)
    wuq = jnp.concatenate([uq, _rot_cols(uq[..., QK_NOPE:])], axis=-1).transpose(1, 0, 2)
    wukv = wts["mla_w_ukv"].reshape(KV_LORA, H, QK_NOPE + V_DIM).transpose(1, 0, 2)
    zero_bias = jnp.zeros((1, D), jnp.float32)
    bias = _swa_bias()
    sink_rows = jnp.broadcast_to(sinks.reshape(SWA_KV_HEADS, 1, SWA_GROUP, 1),
                                 (SWA_KV_HEADS, 1, SWA_GROUP, WINDOW)).reshape(SWA_KV_HEADS, 1, SWA_GROUP * WINDOW)

    h1a, z, q, k, v = _mla_pre(x, vecs[0], wcat, g_q, g_kv, wuq, wukv, cs)
    o_a, lse = _mla_attn_fwd(q, k, v)
    y1a, xm_a, h2a = _post_attn(o_a, x, wts["mla_w_o"], zero_bias, vecs[0])
    wts = {**wts, **late_weights(h2a)}
    a_a, y2a, x1 = _mlp_fwd(h2a, wts["w_ff1"][0], wts["w_ff2"][0], xm_a, vecs[0])

    h1b, qs_t, ks, vs = _swa_pre(x1, vecs[1], wts["swa_w_qkv"], wts["swa_b_qkv"])
    o_bt = _swa_attn_fwd(qs_t, ks, vs, bias, sink_rows)
    y1b, xm_b, h2b = _post_attn(o_bt, x1, wts["swa_w_o"], wts["swa_b_o"], vecs[1], o_transposed=True)
    a_b, y2b, x2 = _mlp_fwd(h2b, wts["w_ff1"][1], wts["w_ff2"][1], xm_b, vecs[1])

    loss8, dx2, dg_final = _final_loss(x2, tgt, g_final.reshape(1, D))

    du_b, dy2b, dxm_b, ps_mlp_b = _mlp_bwd(dx2, y2b, a_b, wts["w_ff1"][1], wts["w_ff2"][1], xm_b, vecs[1])
    g_ff2 = _mm_tn(a_b, dy2b, "dw_ff2_l1", "rows", DEPTH, 1)
    g_ff1 = _mm_tn(h2b, du_b, "dw_ff1_l1", "cols", DEPTH, 1)
    dy1b, do_bt, ps_out_b = _attn_out_bwd(dxm_b, y1b, None, wts["swa_w_o"], vecs[1], False)
    g_swa_o = _mm_tn(o_bt, dy1b, "dw_o_swa", a_transposed=True)
    dqs_t, dks, dvs, dsinks = _swa_attn_bwd(qs_t, ks, vs, do_bt, bias, sink_rows)
    dx1, dqkv, ps_pre_b, g_swa_bqkv = _swa_pre_bwd(x1, dxm_b, vecs[1], dqs_t, dks, dvs, wts["swa_w_qkv"])
    g_swa_qkv = _mm_tn(h1b, dqkv, "dw_qkv", "cols")

    du_a, dy2a, dxm_a, ps_mlp_a = _mlp_bwd(dx1, y2a, a_a, wts["w_ff1"][0], wts["w_ff2"][0], xm_a, vecs[0])
    g_ff2 = _mm_tn(a_a, dy2a, "dw_ff2_l0", "rows", DEPTH, 0, g_ff2)
    g_ff1 = _mm_tn(h2a, du_a, "dw_ff1_l0", "cols", DEPTH, 0, g_ff1)
    rows4 = lambda g: g.reshape(N_CHIPS, g.shape[0] // N_CHIPS, g.shape[1])
    token = on_late_grads({
        "swa_w_qkv": g_swa_qkv.reshape(N_CHIPS, D, -1), "swa_w_o": rows4(g_swa_o),
        "w_ff1": g_ff1.reshape(N_CHIPS, DEPTH * D, -1), "w_ff2": g_ff2.reshape(N_CHIPS, -1, D)})
    dy1a, do_a, ps_out_a, delta = _attn_out_bwd(dxm_a, y1a, o_a, wts["mla_w_o"], vecs[0] + token[0, 0], True)
    g_mla_o = _mm_tn(o_a, dy1a, "dw_o_mla")
    token = on_late_landed(g_mla_o)
    dq, dk, dv = _mla_attn_bwd(q, k, v, do_a, lse, delta + token[0, 0])
    dx0, ps_pre_a, dg_q, dg_kv, dwcat, dwuq, dwukv = _mla_pre_bwd(
        x, dxm_a, vecs[0], h1a, z, dq, dk, dv, cs, wcat, g_q, g_kv, wuq, wukv)

    c0, c1, c2 = Q_LORA, Q_LORA + KV_LORA, Q_LORA + KV_LORA + QK_ROPE
    g_dq = dwcat[:, :c0]
    g_dkv = jnp.concatenate([dwcat[:, c0:c1], _unrot_grad(dwcat[:, c1:c2], dwcat[:, c2:])], axis=1)
    e0 = QK_NOPE + QK_ROPE
    g_uq = jnp.concatenate([dwuq[..., :QK_NOPE], _unrot_grad(dwuq[..., QK_NOPE:e0], dwuq[..., e0:])], axis=-1)
    per = H // N_CHIPS
    g_uq = g_uq.reshape(N_CHIPS, per, Q_LORA, e0).transpose(0, 2, 1, 3).reshape(N_CHIPS, Q_LORA, per * e0)
    g_ukv = dwukv.reshape(N_CHIPS, per, KV_LORA, QK_NOPE + V_DIM).transpose(0, 2, 1, 3)
    g_ukv = g_ukv.reshape(N_CHIPS, KV_LORA, per * (QK_NOPE + V_DIM))

    def dmod(ps_pre, ps_out, ps_mlp):
        return jnp.concatenate([ps_pre[R_SH1:R_SC1 + 1], ps_out[R_GT1:R_GT1 + 1], ps_mlp[R_SH2:R_GT2 + 1]], axis=0)

    grads = {"mla_w_dq": rows4(g_dq), "mla_w_uq": g_uq, "mla_w_dkv": rows4(g_dkv), "mla_w_ukv": g_ukv,
             "mla_w_o": rows4(g_mla_o)}
    small = {
        "dmod": jnp.stack([dmod(ps_pre_a, ps_out_a, ps_mlp_a), dmod(ps_pre_b, ps_out_b, ps_mlp_b)]).reshape(DEPTH, 6 * D),
        "g_mix": jnp.stack([ps_pre_a[R_GMIX], ps_pre_b[R_GMIX]]),
        "g_mlp": jnp.stack([ps_mlp_a[R_GMLP], ps_mlp_b[R_GMLP]]),
        "mla_g_q": dg_q, "mla_g_kv": dg_kv, "swa_sinks": jnp.sum(dsinks.reshape(SWA_HEADS, WINDOW), axis=1).reshape(1, SWA_HEADS),
        "swa_b_qkv": g_swa_bqkv, "swa_b_o": ps_out_b[R_BO:R_BO + 1],
        "g_final": dg_final.reshape(D), "loss": loss8[0, 0],
    }
    return dx0, grads, small


SHARDED = {
    "mla_w_dq": (1, D // N_CHIPS, Q_LORA),
    "mla_w_uq": (1, Q_LORA, MLA_HEADS * (QK_NOPE + QK_ROPE) // N_CHIPS),
    "mla_w_dkv": (1, D // N_CHIPS, KV_LORA + QK_ROPE),
    "mla_w_ukv": (1, KV_LORA, MLA_HEADS * (QK_NOPE + V_DIM) // N_CHIPS),
    "mla_w_o": (1, MLA_HEADS * V_DIM // N_CHIPS, D),
    "swa_w_qkv": (1, D, (SWA_HEADS + 2 * SWA_KV_HEADS) * SWA_HEAD_DIM // N_CHIPS),
    "swa_w_o": (1, SWA_HEADS * SWA_HEAD_DIM // N_CHIPS, D),
    "w_ff1": (DEPTH, D, D_FF // N_CHIPS),
    "w_ff2": (DEPTH, D_FF // N_CHIPS, D),
}
COL_SPLIT = ("mla_w_uq", "mla_w_ukv", "swa_w_qkv")
BIASES = {"swa_b_qkv": (SWA_HEADS + 2 * SWA_KV_HEADS) * SWA_HEAD_DIM, "swa_b_o": D}


def _view2d(name):
    shape = SHARDED[name]
    return math.prod(shape[:-1]), shape[-1]


SMALL = {"b_ada": (DEPTH, 6 * D), "g_mix": (DEPTH, D), "g_mlp": (DEPTH, D), "mla_g_q": (1, Q_LORA),
         "mla_g_kv": (1, KV_LORA), "swa_sinks": (1, SWA_HEADS), "g_final": (D,), "loss": (),
         "swa_b_qkv": (1, BIASES["swa_b_qkv"]), "swa_b_o": (1, BIASES["swa_b_o"])}
SMALL_ROWS = 192
DMA_ROWS = 256


SLOT_ROWS = 8


def _small_slots():
    slots, off = {}, 0
    for name, shape in SMALL.items():
        n = max(math.prod(shape), 1)
        slots[name] = (off, n)
        off += -(-n // (SLOT_ROWS * LANES)) * SLOT_ROWS * LANES
    assert off <= SMALL_ROWS * LANES
    return slots


def _pack_small(vals):
    parts, end = [], 0
    for name, (off, n) in _small_slots().items():
        pad = -(-n // (SLOT_ROWS * LANES)) * SLOT_ROWS * LANES - n
        v = vals[name].astype(jnp.float32).reshape(-1) if name in vals else jnp.zeros((n,), jnp.float32)
        parts += [v, jnp.zeros((pad,), jnp.float32)]
        end = off + n + pad
    parts.append(jnp.zeros((SMALL_ROWS * LANES - end,), jnp.float32))
    return jnp.concatenate(parts).reshape(SMALL_ROWS, LANES)


def _from_slot(name, rows):
    n = max(math.prod(SMALL[name]), 1)
    return rows.reshape(-1)[:n].reshape(SMALL[name])


def _pieces(rows):
    return [(off, min(DMA_ROWS, rows - off)) for off in range(0, rows, DMA_ROWS)]


HBM = pl.BlockSpec(memory_space=pltpu.HBM)
MESH = pl.DeviceIdType.MESH


def _place():
    x, y, c = lax.axis_index("x"), lax.axis_index("y"), lax.axis_index("c")
    chips = [(1 - x, y), (x, 1 - y), (1 - x, 1 - y)]
    return x, y, c, chips


def _all_gather(block):
    m_per, n = block.shape

    def body(x_ref, out_ref, send_sems, recv_sems, local_sem):
        x, y, c, chips = _place()
        me, sibling = (x, y, c), (x, y, 1 - c)

        def rows(px, py, pc):
            return out_ref.at[pl.ds((4 * px + 2 * py + pc) * m_per, m_per), :]

        def copy(k, blk, to, src=None):
            return pltpu.make_async_remote_copy(
                src_ref=rows(*blk) if src is None else src, dst_ref=rows(*blk),
                send_sem=send_sems.at[k], recv_sem=recv_sems.at[k], device_id=to, device_id_type=MESH)

        mine = pltpu.make_async_copy(x_ref, rows(*me), local_sem)
        mine.start()
        first = [copy(0, me, sibling, src=x_ref)]
        first += [copy(1 + j, me, (*chip, c), src=x_ref) for j, chip in enumerate(chips)]
        for cp in first:
            cp.start()
        passed = [copy(4 + j, (*chip, c), sibling) for j, chip in enumerate(chips)]
        for j, chip in enumerate(chips):
            copy(1 + j, (*chip, c), me).wait_recv()
            passed[j].start()
        copy(0, sibling, me).wait_recv()
        for j, chip in enumerate(chips):
            copy(4 + j, (*chip, 1 - c), me).wait_recv()
        for cp in first + passed:
            cp.wait_send()
        mine.wait()

    out = pl.pallas_call(
        body, name="all_gather_small",
        out_shape=jax.ShapeDtypeStruct((N_DEV * m_per, n), block.dtype),
        in_specs=[pl.BlockSpec(memory_space=pltpu.VMEM)],
        out_specs=pl.BlockSpec(memory_space=pltpu.VMEM),
        scratch_shapes=[pltpu.SemaphoreType.DMA((7,)), pltpu.SemaphoreType.DMA((7,)), pltpu.SemaphoreType.DMA],
    )(block)
    return out.reshape(N_DEV, m_per, n)


def _weight_gather(shards):
    nt = len(shards)

    def body(*refs):
        w_refs, out_refs = refs[:nt], refs[nt:2 * nt]
        send_sems, recv_sems = refs[2 * nt:]
        x, y, c, chips = _place()
        sibling = (x, y, 1 - c)

        def slab(t, px, py, half):
            rh = shards[t].shape[0] // 2
            return out_refs[t].at[2 * px + py, pl.ds(half * rh, rh), :]

        def copy(t, k, src, dst, to):
            return pltpu.make_async_remote_copy(src_ref=src, dst_ref=dst, send_sem=send_sems.at[6 * t + k],
                                                recv_sem=recv_sems.at[6 * t + k], device_id=to, device_id_type=MESH)

        first = []
        for t in range(nt):
            rh = shards[t].shape[0] // 2
            first += [copy(t, j, w_refs[t].at[pl.ds(c * rh, rh), :], slab(t, x, y, c), (*chip, c))
                      for j, chip in enumerate(chips)]
        for cp in first:
            cp.start()
        passed = []
        for t in range(nt):
            for j, chip in enumerate(chips):
                copy(t, j, slab(t, *chip, c), slab(t, *chip, c), (*chip, c)).wait_recv()
                rh = shards[t].shape[0] // 2
                for off, n in _pieces(rh):
                    piece = out_refs[t].at[2 * chip[0] + chip[1], pl.ds(c * rh + off, n), :]
                    copy(t, 3 + j, piece, piece, sibling).start()
                passed.append(copy(t, 3 + j, slab(t, *chip, c), slab(t, *chip, c), sibling))
        for t in range(nt):
            for j, chip in enumerate(chips):
                copy(t, 3 + j, slab(t, *chip, 1 - c), slab(t, *chip, 1 - c), sibling).wait_recv()
        for cp in first + passed:
            cp.wait_send()

    return pl.pallas_call(
        body, name="weight_gather",
        out_shape=[jax.ShapeDtypeStruct((N_CHIPS,) + s.shape, s.dtype) for s in shards],
        in_specs=[HBM] * nt, out_specs=[HBM] * nt,
        scratch_shapes=[pltpu.SemaphoreType.DMA((6 * nt,)), pltpu.SemaphoreType.DMA((6 * nt,))],
    )(*shards)


SEM = pl.BlockSpec(memory_space=pltpu.SEMAPHORE)
ANY = pl.BlockSpec(memory_space=pl.ANY)
SPLIT_COPY = pltpu.SideEffectType.DATAFLOW_SIDE_EFFECTING


def _late_copies(w_refs, land_refs, send_sems, recv_sems):
    x, y, c, chips = _place()
    return [pltpu.make_async_remote_copy(
        src_ref=w_refs[t], dst_ref=land_refs[t].at[2 * x + y], send_sem=send_sems.at[3 * t + j],
        recv_sem=recv_sems.at[3 * t + j], device_id=(cx, cy, c), device_id_type=MESH)
        for t in range(len(w_refs)) for j, (cx, cy) in enumerate(chips)], chips


def _late_gather_start(shards, after):
    nt, na = len(shards), len(after)

    def body(*refs):
        w_refs, land_refs = refs[:nt], refs[nt:2 * nt]
        send_sems, recv_sems, token = refs[2 * nt + na], refs[2 * nt + na + 1], refs[-1]
        copies, _ = _late_copies(w_refs, land_refs, send_sems, recv_sems)
        for cp in copies:
            cp.start()
        token[...] = jnp.zeros_like(token)

    hbm = lambda a: pltpu.with_memory_space_constraint(a, pltpu.HBM)
    lands = [lax.empty((N_CHIPS,) + s.shape, s.dtype) for s in shards]
    outs = pl.pallas_call(
        body, name="late_gather_start",
        out_shape=(pltpu.SemaphoreType.DMA((3 * nt,)), pltpu.SemaphoreType.DMA((3 * nt,)),
                   *[pltpu.HBM(s.shape, s.dtype) for s in shards], *[pltpu.HBM(l.shape, l.dtype) for l in lands],
                   jax.ShapeDtypeStruct((8, LANES), jnp.float32)),
        in_specs=[HBM] * (2 * nt) + [ANY] * na,
        out_specs=(SEM, SEM, *([HBM] * (2 * nt)), pl.BlockSpec(memory_space=pltpu.VMEM)),
        input_output_aliases={i: 2 + i for i in range(2 * nt)},
        compiler_params=pltpu.CompilerParams(has_side_effects=SPLIT_COPY),
    )(*[hbm(s) for s in shards], *[hbm(l) for l in lands], *after)
    return outs[0], outs[1], list(outs[2:2 + nt]), list(outs[2 + nt:2 + 2 * nt]), outs[-1]


def _late_gather_wait(send_sems, recv_sems, shards, lands, after):
    nt = len(shards)

    def body(*refs):
        w_refs, land_refs = refs[:nt], refs[nt:2 * nt]
        s_sems, r_sems = refs[2 * nt], refs[2 * nt + 1]
        x, y, c, chips = _place()
        for t in range(nt):
            for j, (cx, cy) in enumerate(chips):
                cp = pltpu.make_async_remote_copy(
                    src_ref=w_refs[t], dst_ref=land_refs[t].at[2 * cx + cy], send_sem=s_sems.at[3 * t + j],
                    recv_sem=r_sems.at[3 * t + j], device_id=(cx, cy, c), device_id_type=MESH)
                cp.wait_send()
                cp.wait_recv()

    outs = pl.pallas_call(
        body, name="late_gather_wait",
        out_shape=(*[pltpu.HBM(s.shape, s.dtype) for s in shards], *[pltpu.HBM(l.shape, l.dtype) for l in lands]),
        in_specs=[HBM] * (2 * nt) + [SEM, SEM, ANY], out_specs=tuple([HBM] * (2 * nt)),
        input_output_aliases={i: i for i in range(2 * nt)},
        compiler_params=pltpu.CompilerParams(has_side_effects=SPLIT_COPY),
    )(*shards, *lands, send_sems, recv_sems, after)
    return list(outs[nt:])


def _grad_pair_in(grads):
    nt = len(grads)

    def body(*refs):
        g_refs, got_refs = refs[:nt], refs[nt:2 * nt]
        send_sems, recv_sems = refs[2 * nt:]
        x, y, c, _ = _place()
        sibling = (x, y, 1 - c)

        def copy(t, src, dst):
            return pltpu.make_async_remote_copy(src_ref=src, dst_ref=dst, send_sem=send_sems.at[t],
                                                recv_sem=recv_sems.at[t], device_id=sibling, device_id_type=MESH)

        for t in range(nt):
            rh = grads[t].shape[1] // 2
            for p in range(N_CHIPS):
                for off, n in _pieces(rh):
                    copy(t, g_refs[t].at[p, pl.ds((1 - c) * rh + off, n), :], got_refs[t].at[p, pl.ds(off, n), :]).start()
        for t in range(nt):
            rh = grads[t].shape[1] // 2
            copy(t, g_refs[t].at[:, pl.ds((1 - c) * rh, rh), :], got_refs[t]).wait()

    return pl.pallas_call(
        body, name="grad_pair_in",
        out_shape=[jax.ShapeDtypeStruct((N_CHIPS, g.shape[1] // 2, g.shape[2]), g.dtype) for g in grads],
        in_specs=[HBM] * nt, out_specs=[HBM] * nt,
        scratch_shapes=[pltpu.SemaphoreType.DMA((nt,)), pltpu.SemaphoreType.DMA((nt,))],
    )(*grads)


def _pair_in_start(grads):
    nt = len(grads)

    def body(*refs):
        g_refs, land_refs = refs[:nt], refs[nt:2 * nt]
        send_sems, recv_sems, token = refs[2 * nt], refs[2 * nt + 1], refs[-1]
        x, y, c, _ = _place()
        for t in range(nt):
            rh = grads[t].shape[1] // 2
            for p in range(N_CHIPS):
                for off, n in _pieces(rh):
                    pltpu.make_async_remote_copy(
                        src_ref=g_refs[t].at[p, pl.ds((1 - c) * rh + off, n), :], dst_ref=land_refs[t].at[p, pl.ds(off, n), :],
                        send_sem=send_sems.at[t], recv_sem=recv_sems.at[t], device_id=(x, y, 1 - c),
                        device_id_type=MESH).start()
        token[...] = jnp.zeros_like(token)

    hbm = lambda a: pltpu.with_memory_space_constraint(a, pltpu.HBM)
    lands = [lax.empty((N_CHIPS, g.shape[1] // 2, g.shape[2]), g.dtype) for g in grads]
    outs = pl.pallas_call(
        body, name="grad_pair_in_start",
        out_shape=(pltpu.SemaphoreType.DMA((nt,)), pltpu.SemaphoreType.DMA((nt,)),
                   *[pltpu.HBM(g.shape, g.dtype) for g in grads], *[pltpu.HBM(l.shape, l.dtype) for l in lands],
                   jax.ShapeDtypeStruct((8, LANES), jnp.float32)),
        in_specs=[HBM] * (2 * nt),
        out_specs=(SEM, SEM, *([HBM] * (2 * nt)), pl.BlockSpec(memory_space=pltpu.VMEM)),
        input_output_aliases={i: 2 + i for i in range(2 * nt)},
        compiler_params=pltpu.CompilerParams(has_side_effects=SPLIT_COPY),
    )(*[hbm(g) for g in grads], *[hbm(l) for l in lands])
    return outs[0], outs[1], list(outs[2:2 + nt]), list(outs[2 + nt:2 + 2 * nt]), outs[-1]


def _pair_in_wait(send_sems, recv_sems, grads, lands, after):
    nt = len(grads)

    def body(*refs):
        g_refs, land_refs = refs[:nt], refs[nt:2 * nt]
        s_sems, r_sems = refs[2 * nt], refs[2 * nt + 1]
        x, y, c, _ = _place()
        for t in range(nt):
            rh = grads[t].shape[1] // 2
            cp = pltpu.make_async_remote_copy(
                src_ref=g_refs[t].at[:, pl.ds((1 - c) * rh, rh), :], dst_ref=land_refs[t], send_sem=s_sems.at[t],
                recv_sem=r_sems.at[t], device_id=(x, y, 1 - c), device_id_type=MESH)
            cp.wait_send()
            cp.wait_recv()

    outs = pl.pallas_call(
        body, name="grad_pair_in_wait",
        out_shape=(*[pltpu.HBM(g.shape, g.dtype) for g in grads], *[pltpu.HBM(l.shape, l.dtype) for l in lands]),
        in_specs=[HBM] * (2 * nt) + [SEM, SEM, ANY], out_specs=tuple([HBM] * (2 * nt)),
        input_output_aliases={i: i for i in range(2 * nt)},
        compiler_params=pltpu.CompilerParams(has_side_effects=SPLIT_COPY),
    )(*grads, *lands, send_sems, recv_sems, after)
    return list(outs[:nt]), list(outs[nt:])


def _pair_sum(g, got, core, name):
    _, rows, cols = g.shape
    rh = rows // 2
    tr = _tile(rh, 512)
    nb = rh // tr

    def body(c_ref, g_ref, got_ref, s32_ref, s16_ref):
        s = g_ref[...] + got_ref[...]
        s32_ref[...] = s
        s16_ref[...] = s.astype(s16_ref.dtype)

    blk = pl.BlockSpec((None, tr, cols), lambda p, i, c_ref: (p, i, 0))
    return pl.pallas_call(
        body, name=name,
        grid_spec=pltpu.PrefetchScalarGridSpec(
            num_scalar_prefetch=1, grid=(N_CHIPS, nb),
            in_specs=[pl.BlockSpec((None, tr, cols), lambda p, i, c_ref: (p, c_ref[0] * nb + i, 0)), blk],
            out_specs=[blk, blk]),
        out_shape=[jax.ShapeDtypeStruct((N_CHIPS, rh, cols), jnp.float32),
                   jax.ShapeDtypeStruct((N_CHIPS, rh, cols), jnp.bfloat16)],
        compiler_params=_params("parallel", "parallel"),
    )(core, g, got)


def _exchange_start(parts, name):
    nt = len(parts)

    def body(*refs):
        a_refs, land_refs = refs[:nt], refs[nt:2 * nt]
        send_sems, recv_sems, token = refs[2 * nt], refs[2 * nt + 1], refs[-1]
        x, y, c, chips = _place()
        for t in range(nt):
            for j, (cx, cy) in enumerate(chips):
                pltpu.make_async_remote_copy(
                    src_ref=a_refs[t].at[2 * cx + cy], dst_ref=land_refs[t].at[j], send_sem=send_sems.at[3 * t + j],
                    recv_sem=recv_sems.at[3 * t + j], device_id=(cx, cy, c), device_id_type=MESH).start()
        token[...] = jnp.zeros_like(token)

    hbm = lambda a: pltpu.with_memory_space_constraint(a, pltpu.HBM)
    lands = [lax.empty((N_CHIPS - 1,) + a.shape[1:], a.dtype) for a in parts]
    outs = pl.pallas_call(
        body, name=name,
        out_shape=(pltpu.SemaphoreType.DMA((3 * nt,)), pltpu.SemaphoreType.DMA((3 * nt,)),
                   *[pltpu.HBM(a.shape, a.dtype) for a in parts], *[pltpu.HBM(l.shape, l.dtype) for l in lands],
                   jax.ShapeDtypeStruct((8, LANES), jnp.float32)),
        in_specs=[HBM] * (2 * nt),
        out_specs=(SEM, SEM, *([HBM] * (2 * nt)), pl.BlockSpec(memory_space=pltpu.VMEM)),
        input_output_aliases={i: 2 + i for i in range(2 * nt)},
        compiler_params=pltpu.CompilerParams(has_side_effects=SPLIT_COPY),
    )(*[hbm(a) for a in parts], *[hbm(l) for l in lands])
    return outs[0], outs[1], list(outs[2:2 + nt]), list(outs[2 + nt:2 + 2 * nt]), outs[-1]


def _exchange_wait(send_sems, recv_sems, parts, lands, after, name):
    nt = len(parts)

    def body(*refs):
        a_refs, land_refs = refs[:nt], refs[nt:2 * nt]
        s_sems, r_sems = refs[2 * nt], refs[2 * nt + 1]
        x, y, c, chips = _place()
        for t in range(nt):
            for j, (cx, cy) in enumerate(chips):
                cp = pltpu.make_async_remote_copy(
                    src_ref=a_refs[t].at[2 * cx + cy], dst_ref=land_refs[t].at[j], send_sem=s_sems.at[3 * t + j],
                    recv_sem=r_sems.at[3 * t + j], device_id=(cx, cy, c), device_id_type=MESH)
                cp.wait_send()
                cp.wait_recv()

    outs = pl.pallas_call(
        body, name=name,
        out_shape=(*[pltpu.HBM(a.shape, a.dtype) for a in parts], *[pltpu.HBM(l.shape, l.dtype) for l in lands]),
        in_specs=[HBM] * (2 * nt) + [SEM, SEM, ANY], out_specs=tuple([HBM] * (2 * nt)),
        input_output_aliases={i: i for i in range(2 * nt)},
        compiler_params=pltpu.CompilerParams(has_side_effects=SPLIT_COPY),
    )(*parts, *lands, send_sems, recv_sems, after)
    return list(outs[nt:])


def _chip_sum(s32, got, chip, name, behind=None):
    _, rh, cols = s32.shape
    tr = _tile(rh, 512)

    def body(p_ref, s_ref, got_ref, *refs):
        acc = s_ref[...]
        for j in range(N_CHIPS - 1):
            acc = acc + got_ref[j].astype(jnp.float32)
        refs[-1][...] = acc

    extra = [] if behind is None else [behind]
    return pl.pallas_call(
        body, name=name,
        grid_spec=pltpu.PrefetchScalarGridSpec(
            num_scalar_prefetch=1, grid=(rh // tr,),
            in_specs=[pl.BlockSpec((None, tr, cols), lambda i, p_ref: (p_ref[0], i, 0)),
                      pl.BlockSpec((N_CHIPS - 1, tr, cols), lambda i, p_ref: (0, i, 0))]
            + [pl.BlockSpec((8, LANES), lambda i, p_ref: (0, 0))] * len(extra),
            out_specs=pl.BlockSpec((tr, cols), lambda i, p_ref: (i, 0))),
        out_shape=jax.ShapeDtypeStruct((rh, cols), jnp.float32),
        compiler_params=_params("parallel"),
    )(chip, s32, got, *extra)


def _grad_pair_out(halves):
    nt = len(halves)

    def body(*refs):
        h_refs, got_refs = refs[:nt], refs[nt:2 * nt]
        send_sems, recv_sems = refs[2 * nt:]
        x, y, c, _ = _place()
        sibling = (x, y, 1 - c)

        def copy(t, src, dst):
            return pltpu.make_async_remote_copy(src_ref=src, dst_ref=dst, send_sem=send_sems.at[t],
                                                recv_sem=recv_sems.at[t], device_id=sibling, device_id_type=MESH)

        for t in range(nt):
            for off, n in _pieces(halves[t].shape[0]):
                copy(t, h_refs[t].at[pl.ds(off, n), :], got_refs[t].at[pl.ds(off, n), :]).start()
        for t in range(nt):
            copy(t, h_refs[t], got_refs[t]).wait()

    return pl.pallas_call(
        body, name="grad_pair_out",
        out_shape=[jax.ShapeDtypeStruct(h.shape, h.dtype) for h in halves],
        in_specs=[HBM] * nt, out_specs=[HBM] * nt,
        scratch_shapes=[pltpu.SemaphoreType.DMA((nt,)), pltpu.SemaphoreType.DMA((nt,))],
    )(*halves)


def _ada_part(c_all, w_ada):
    L, _, ncol = w_ada.shape
    tn = _tile(ncol, 512)

    def body(c_ref, w_ref, cond_ref, part_ref):
        cv = c_ref[...]
        cond = cv * jax.nn.sigmoid(cv)
        cond_ref[...] = cond
        part_ref[0] = jnp.dot(cond, w_ref[0], precision=lax.Precision.HIGHEST, preferred_element_type=jnp.float32)

    return pl.pallas_call(
        body, name="ada_part", grid=(L, ncol // tn),
        in_specs=[_full((N_DEV, D)), pl.BlockSpec((1, D, tn), lambda l, j: (l, 0, j))],
        out_specs=[_full((N_DEV, D)), pl.BlockSpec((1, N_DEV, tn), lambda l, j: (l, 0, j))],
        out_shape=[jax.ShapeDtypeStruct((N_DEV, D), jnp.float32), jax.ShapeDtypeStruct((L, N_DEV, ncol), jnp.float32)],
        compiler_params=_params("arbitrary", "arbitrary"),
    )(c_all, w_ada)


def _adamw_math(w, g, m, v):
    m = ADAM_B1 * m + (1.0 - ADAM_B1) * g
    v = ADAM_B2 * v + (1.0 - ADAM_B2) * jnp.square(g)
    m_hat = m / (1.0 - ADAM_B1 ** ADAM_STEP)
    v_hat = v / (1.0 - ADAM_B2 ** ADAM_STEP)
    delta = -ADAM_LR * (m_hat / (jnp.sqrt(v_hat) + ADAM_EPS) + ADAM_WD * w)
    return delta, m, v


def _adamw(w, g, m, v, name):
    shape = w.shape
    cols = shape[-1]
    rows = math.prod(shape[:-1])
    tr = _tile(rows, 512)
    two_d = lambda t: t.reshape(rows, cols)

    def body(w_ref, g_ref, m_ref, v_ref, d_ref, mo_ref, vo_ref):
        d_ref[...], mo_ref[...], vo_ref[...] = _adamw_math(w_ref[...], g_ref[...], m_ref[...], v_ref[...])

    out = jax.ShapeDtypeStruct((rows, cols), jnp.float32)
    outs = pl.pallas_call(
        body, name=name, grid=(rows // tr,), in_specs=[_rows(tr, cols)] * 4, out_specs=[_rows(tr, cols)] * 3,
        out_shape=[out, out, out], compiler_params=_params("parallel"),
    )(two_d(w), two_d(g), two_d(m), two_d(v))
    return [t.reshape(shape) for t in outs]


def _adamw_halves(w, mine, got, m, v, core, name):
    shape = w.shape
    cols = shape[-1]
    rows = math.prod(shape[:-1])
    rh = rows // 2
    tr = _tile(rh, 512)
    nbh = rh // tr
    two_d = lambda t: t.reshape(rows, cols)

    def body(c_ref, w_ref, a_ref, b_ref, m_ref, v_ref, g_ref, d_ref, mo_ref, vo_ref):
        g = jnp.where(pl.program_id(0) // nbh == c_ref[0], a_ref[...], b_ref[...])
        g_ref[...] = g
        d_ref[...], mo_ref[...], vo_ref[...] = _adamw_math(w_ref[...], g, m_ref[...], v_ref[...])

    row = pl.BlockSpec((tr, cols), lambda i, c_ref: (i, 0))

    def half(keep):
        return pl.BlockSpec((tr, cols), lambda i, c_ref: (jnp.where((i // nbh == c_ref[0]) == keep, i % nbh, 0), 0))

    out = jax.ShapeDtypeStruct((rows, cols), jnp.float32)
    outs = pl.pallas_call(
        body, name=name,
        grid_spec=pltpu.PrefetchScalarGridSpec(
            num_scalar_prefetch=1, grid=(rows // tr,),
            in_specs=[row, half(True), half(False), row, row], out_specs=[row] * 4),
        out_shape=[out] * 4, compiler_params=_params("arbitrary"),
    )(core, two_d(w), mine, got, two_d(m), two_d(v))
    return [t.reshape(shape) for t in outs]


def _ada_grad_adamw(cond_t, dm, w, m, v):
    L, _, ncol = w.shape
    tn = _tile(ncol, 512)

    def body(ct_ref, dm_ref, w_ref, m_ref, v_ref, g_ref, d_ref, mo_ref, vo_ref):
        g = ct_ref[:, 0:1] * dm_ref[0, 0:1, :]
        for b in range(1, N_DEV):
            g = g + ct_ref[:, b:b + 1] * dm_ref[0, b:b + 1, :]
        g_ref[0] = g
        d_ref[0], mo_ref[0], vo_ref[0] = _adamw_math(w_ref[0], g, m_ref[0], v_ref[0])

    wblk = pl.BlockSpec((1, D, tn), lambda l, j: (l, 0, j))
    out = jax.ShapeDtypeStruct(w.shape, jnp.float32)
    return pl.pallas_call(
        body, name="ada_grad_adamw", grid=(L, ncol // tn),
        in_specs=[_full((D, N_DEV)), pl.BlockSpec((1, N_DEV, tn), lambda l, j: (l, 0, j)), wblk, wblk, wblk],
        out_specs=[wblk] * 4, out_shape=[out] * 4, compiler_params=_params("parallel", "parallel"),
    )(cond_t, dm, w, m, v)


def _small_adamw(gathered, w, m, v):
    slots = _small_slots()
    rows = {name: (off // LANES, -(-n // LANES)) for name, (off, n) in slots.items()}
    kinds = {name: 1 if name == "loss" or name in BIASES else 4 for name in slots}

    def body(ga_ref, w_ref, m_ref, v_ref, *out_refs):
        g = ga_ref[0]
        for dev in range(1, N_DEV):
            g = g + ga_ref[dev]
        d, mo, vo = _adamw_math(w_ref[...], g, m_ref[...], v_ref[...])
        k = 0
        for name, (r0, nr) in rows.items():
            for src in (g, d, mo, vo)[:kinds[name]]:
                out_refs[k][...] = src[r0:r0 + nr, :]
                k += 1

    out_shape = [jax.ShapeDtypeStruct((rows[name][1], LANES), jnp.float32) for name in slots for _ in range(kinds[name])]
    flat = pl.pallas_call(
        body, name="small_adamw", out_shape=out_shape,
        in_specs=[pl.BlockSpec(memory_space=pltpu.VMEM)] * 4,
        out_specs=[pl.BlockSpec(memory_space=pltpu.VMEM)] * len(out_shape),
    )(gathered, w, m, v)
    out, k = {}, 0
    for name in slots:
        out[name] = [_from_slot(name, t) for t in flat[k:k + kinds[name]]]
        k += kinds[name]
    return out


def _one_hot_pick(arr, index, axis):
    n = arr.shape[axis]
    shape = [1] * arr.ndim
    shape[axis] = n
    hot = (jnp.arange(n) == index).astype(arr.dtype).reshape(shape)
    return jnp.sum(arr * hot, axis=axis)


def kernel(x, c, positions, w_ada, b_ada, g_mix, g_mlp, mla_w_dq, mla_g_q, mla_w_uq, mla_w_dkv, mla_g_kv, mla_w_ukv, mla_w_o, swa_w_qkv, swa_b_qkv, swa_sinks, swa_w_o, swa_b_o, w_ff1, w_ff2, g_final, loss_target, m_w_ada, m_b_ada, m_g_mix, m_g_mlp, m_mla_w_dq, m_mla_g_q, m_mla_w_uq, m_mla_w_dkv, m_mla_g_kv, m_mla_w_ukv, m_mla_w_o, m_swa_w_qkv, m_swa_b_qkv, m_swa_sinks, m_swa_w_o, m_swa_b_o, m_w_ff1, m_w_ff2, m_g_final, v_w_ada, v_b_ada, v_g_mix, v_g_mlp, v_mla_w_dq, v_mla_g_q, v_mla_w_uq, v_mla_w_dkv, v_mla_g_kv, v_mla_w_ukv, v_mla_w_o, v_swa_w_qkv, v_swa_b_qkv, v_swa_sinks, v_swa_w_o, v_swa_b_o, v_w_ff1, v_w_ff2, v_g_final):
    W = dict(w_ada=w_ada, b_ada=b_ada, g_mix=g_mix, g_mlp=g_mlp, mla_w_dq=mla_w_dq, mla_g_q=mla_g_q, mla_w_uq=mla_w_uq,
             mla_w_dkv=mla_w_dkv, mla_g_kv=mla_g_kv, mla_w_ukv=mla_w_ukv, mla_w_o=mla_w_o, swa_w_qkv=swa_w_qkv,
             swa_b_qkv=swa_b_qkv, swa_sinks=swa_sinks, swa_w_o=swa_w_o, swa_b_o=swa_b_o, w_ff1=w_ff1, w_ff2=w_ff2,
             g_final=g_final)
    M = dict(w_ada=m_w_ada, b_ada=m_b_ada, g_mix=m_g_mix, g_mlp=m_g_mlp, mla_w_dq=m_mla_w_dq, mla_g_q=m_mla_g_q,
             mla_w_uq=m_mla_w_uq, mla_w_dkv=m_mla_w_dkv, mla_g_kv=m_mla_g_kv, mla_w_ukv=m_mla_w_ukv, mla_w_o=m_mla_w_o,
             swa_w_qkv=m_swa_w_qkv, swa_b_qkv=m_swa_b_qkv, swa_sinks=m_swa_sinks, swa_w_o=m_swa_w_o, swa_b_o=m_swa_b_o,
             w_ff1=m_w_ff1, w_ff2=m_w_ff2, g_final=m_g_final)
    V = dict(w_ada=v_w_ada, b_ada=v_b_ada, g_mix=v_g_mix, g_mlp=v_g_mlp, mla_w_dq=v_mla_w_dq, mla_g_q=v_mla_g_q,
             mla_w_uq=v_mla_w_uq, mla_w_dkv=v_mla_w_dkv, mla_g_kv=v_mla_g_kv, mla_w_ukv=v_mla_w_ukv, mla_w_o=v_mla_w_o,
             swa_w_qkv=v_swa_w_qkv, swa_b_qkv=v_swa_b_qkv, swa_sinks=v_swa_sinks, swa_w_o=v_swa_w_o, swa_b_o=v_swa_b_o,
             w_ff1=v_w_ff1, w_ff2=v_w_ff2, g_final=v_g_final)
    order = list(W)
    names = list(SHARDED)
    core = lax.axis_index("c")
    chip = 2 * lax.axis_index("x") + lax.axis_index("y")
    dev = 2 * chip + core
    core_arr = core.astype(jnp.int32).reshape(1)
    chip_arr = chip.astype(jnp.int32).reshape(1)

    def whole(n, g, own):
        g = lax.dynamic_update_slice(g, own[None], (chip, 0, 0))
        if n in ("w_ff1", "w_ff2"):
            return g
        if n in COL_SPLIT:
            return g.transpose(1, 0, 2).reshape(g.shape[1], N_CHIPS * g.shape[2])
        return g.reshape(N_CHIPS * g.shape[1], g.shape[2])

    early = [n for n in names if n.startswith("mla_")]
    local = {n: W[n].astype(MXU_DTYPE).reshape(_view2d(n)) for n in early}
    wts = {n: whole(n, g, local[n]) for n, g in zip(early, _weight_gather([local[n] for n in early]))}

    nbq, nbo = BIASES["swa_b_qkv"] // N_CHIPS, BIASES["swa_b_o"] // N_CHIPS
    first = jnp.concatenate([c.reshape(-1), swa_b_qkv.reshape(-1), swa_b_o.reshape(-1),
                             jnp.zeros((16 * LANES - D - nbq - nbo,), jnp.float32)]).reshape(16, LANES)
    first_all = _all_gather(first).reshape(N_DEV, 16 * LANES)
    c_all = first_all[:, :D]
    south = first_all[0::2]
    wts["swa_b_qkv"] = south[:, D:D + nbq].reshape(1, N_CHIPS * nbq)
    wts["swa_b_o"] = south[:, D + nbq:D + nbq + nbo].reshape(1, N_CHIPS * nbo)
    cond_all, part = _ada_part(c_all, w_ada)
    ncol = w_ada.shape[2]
    part_all = _all_gather(part.reshape(-1, LANES)).reshape(N_DEV, DEPTH, N_DEV, ncol)
    mine = _one_hot_pick(part_all[0::2], dev, axis=2)
    mod = mine.transpose(1, 0, 2).reshape(DEPTH, N_CHIPS * ncol) + b_ada
    vecs = jnp.concatenate([mod.reshape(DEPTH, 6, D), g_mix[:, None, :], g_mlp[:, None, :]], axis=1)

    late = [("w_ff1", 0), ("w_ff2", 0), ("swa_w_qkv", None), ("swa_w_o", None), ("w_ff1", 1), ("w_ff2", 1)]
    late_local = [(W[n][0] if l is None else W[n][l]).astype(MXU_DTYPE) for n, l in late]
    send_sems, recv_sems, passed, lands, token = _late_gather_start(late_local, [vecs] + [wts[n] for n in early])

    def late_weights(after):
        got = _late_gather_wait(send_sems, recv_sems, passed, lands, after)
        out = {"w_ff1": [None] * DEPTH, "w_ff2": [None] * DEPTH}
        for (n, l), g, own in zip(late, got, late_local):
            if l is None:
                out[n] = whole(n, g, own)
            else:
                out[n][l] = whole(n, g, own)
        return out

    late_names = [n for n in names if n not in early]
    reduce_state = {}

    def on_late_grads(late_grads):
        s_sems, r_sems, passed_g, zones, tok = _pair_in_start([late_grads[n] for n in late_names])
        reduce_state.update(pair=(s_sems, r_sems, passed_g, zones))
        return tok

    def on_late_landed(after):
        gl, got = _pair_in_wait(*reduce_state["pair"], after)
        sums = [_pair_sum(g, s, core_arr, "pair_sum_" + n) for n, g, s in zip(late_names, gl, got)]
        s_sems, r_sems, parts, zones, tok = _exchange_start([s16 for _, s16 in sums], "grad_exchange_start")
        reduce_state.update(sums=sums, split=(s_sems, r_sems, parts, zones))
        return tok

    grad_x, grads, small = _sequence_step(
        x[0], loss_target[0], positions[0], vecs, mla_g_q + token[0, 0], mla_g_kv, swa_sinks, g_final, wts,
        late_weights, on_late_grads, on_late_landed)

    small["b_ada"] = small.pop("dmod")
    small_all = _all_gather(_pack_small(small))
    pk = lambda src: _pack_small({n: src[n] for n in SMALL if n != "loss" and n not in BIASES})
    off, n = _small_slots()["b_ada"]
    dmod_all = small_all.reshape(N_DEV, -1)[:, off:off + n].reshape(N_DEV, DEPTH, N_CHIPS, ncol)
    dm = _one_hot_pick(dmod_all, chip, axis=2).transpose(1, 0, 2)

    gl = [grads[n] for n in early]
    got = _grad_pair_in(gl)
    sums = [_pair_sum(g, s, core_arr, "pair_sum_" + n) for n, g, s in zip(early, gl, got)]
    e_sems, e_rems, e_parts, e_zones, e_tok = _exchange_start([s16 for _, s16 in sums], "mla_exchange_start")

    def finish(tensor_names, sums, others, behind):
        halves = [_chip_sum(s32, o, chip_arr, "chip_sum_" + n, behind) for n, (s32, _), o in zip(tensor_names, sums, others)]
        return {n: _adamw_halves(W[n], mine_h, got_h, M[n], V[n], core_arr, "adamw_" + n)
                for n, mine_h, got_h in zip(tensor_names, halves, _grad_pair_out(halves))}

    late_others = _exchange_wait(*reduce_state["split"], grad_x, "grad_exchange_wait")
    res = finish(late_names, reduce_state["sums"], late_others, e_tok)
    res["w_ada"] = _ada_grad_adamw(cond_all.T, dm, w_ada, m_w_ada, v_w_ada)
    small_res = _small_adamw(small_all, pk(W), pk(M), pk(V))
    early_others = _exchange_wait(e_sems, e_rems, e_parts, e_zones, res["w_ff2"][1], "mla_exchange_wait")
    res.update(finish(early, sums, early_others, None))

    for n, width in BIASES.items():
        g = _one_hot_pick(small_res[n][0].reshape(N_CHIPS, width // N_CHIPS), chip, axis=0).reshape(1, -1)
        res[n] = [g] + _adamw(W[n], g, M[n], V[n], "adamw_" + n)
    for name in order:
        if name not in res:
            res[name] = small_res[name]
    outs = [small_res["loss"][0], grad_x[None]]
    for k in range(4):
        outs += [res[name][k] for name in order]
    return tuple(outs)
```

```python
import functools
import math

import jax
import jax.numpy as jnp
import numpy as np
from jax import lax
from jax.experimental import pallas as pl
from jax.experimental.pallas import tpu as pltpu

D = 1024
DEPTH = 2
MLA_HEADS = 8
QK_NOPE = 128
QK_ROPE = 64
V_DIM = 128
Q_LORA = 384
KV_LORA = 256
ROPE_THETA = 10000.0
SWA_HEADS = 16
SWA_KV_HEADS = 4
SWA_HEAD_DIM = 64
SWA_GROUP = SWA_HEADS // SWA_KV_HEADS
WINDOW = 128
D_FF = 4 * D
EPS = 1e-6
ADAM_LR = 0.001
ADAM_B1 = 0.9
ADAM_B2 = 0.999
ADAM_EPS = 1e-08
ADAM_WD = 0.01
ADAM_STEP = 10

N_CHIPS = 4
N_DEV = 8
LANES = 128
QK_EXT = 256
MLA_SCALE = (QK_NOPE + QK_ROPE) ** -0.5
LOG2E = math.log2(math.e)
LN2 = math.log(2.0)
MLA_QSCALE = MLA_SCALE * LOG2E
ATTN_BLOCK = 2048
ATTN_SUB = 512
MLP_FWD_TILE = (1024, 1024)
MLP_BWD_TILE = (512, 1024)
DW_TOKENS = 4096
SWA_SCALE = SWA_HEAD_DIM ** -0.5
NEG = -1e30
MXU_DTYPE = jnp.bfloat16
VMEM_LIMIT = 56 * 1024 * 1024

R_SH1, R_SC1, R_GT1, R_SH2, R_SC2, R_GT2, R_GMIX, R_GMLP = range(8)
R_BO = 6


def _tile(n, pref):
    if n <= pref:
        return n
    for t in range(pref, 7, -1):
        if n % t == 0 and t % 8 == 0:
            return t
    return n


def _dot(a, b):
    return jnp.dot(a, b, preferred_element_type=jnp.float32)


def _dot_nt(a, b):
    return lax.dot_general(a, b, (((1,), (1,)), ((), ())), preferred_element_type=jnp.float32)


def _dot_tn(a, b):
    return lax.dot_general(a, b, (((0,), (0,)), ((), ())), preferred_element_type=jnp.float32)


def _rms(x):
    r = lax.rsqrt(jnp.mean(x * x, axis=-1, keepdims=True) + EPS)
    return x * r, r


def _rms_bwd(dxhat, xhat, r):
    return r * (dxhat - xhat * jnp.mean(dxhat * xhat, axis=-1, keepdims=True))


def _rowsum(v):
    return jnp.sum(v, axis=0, keepdims=True)


def _params(*sem):
    return pltpu.CompilerParams(dimension_semantics=sem, vmem_limit_bytes=VMEM_LIMIT)


def _full(shape):
    nd = len(shape)
    return pl.BlockSpec(shape, lambda *_: (0,) * nd)


def _rows(tm, cols):
    return pl.BlockSpec((tm, cols), lambda i, *_: (i, 0))


def _modulate_bwd(dh, x, vec_ref, r_g, r_sc, r_sh, ps_ref, dres):
    xhat, r = _rms(x)
    g = vec_ref[r_g:r_g + 1, :]
    n = xhat * g
    ps_ref[r_sh:r_sh + 1, :] += _rowsum(dh)
    ps_ref[r_sc:r_sc + 1, :] += _rowsum(dh * n)
    dn = dh * (1.0 + vec_ref[r_sc:r_sc + 1, :])
    ps_ref[r_g:r_g + 1, :] += _rowsum(dn * xhat)
    return dres + _rms_bwd(dn * g, xhat, r)


def _mla_pre(x, vec, wcat, g_q, g_kv, wuq, wukv, cs):
    T = x.shape[0]
    tm = _tile(T, 512)
    H = MLA_HEADS

    def body(x_ref, vec_ref, wcat_ref, gq_ref, gkv_ref, wuq_ref, wukv_ref, cs_ref, h_ref, z_ref, q_ref, k_ref, v_ref):
        xhat, _ = _rms(x_ref[...])
        h = xhat * vec_ref[R_GMIX:R_GMIX + 1, :] * (1.0 + vec_ref[R_SC1:R_SC1 + 1, :]) + vec_ref[R_SH1:R_SH1 + 1, :]
        hb = h.astype(MXU_DTYPE)
        h_ref[...] = hb
        z = _dot(hb, wcat_ref[...])
        z_ref[...] = z
        cq = (_rms(z[:, :Q_LORA])[0] * gq_ref[...]).astype(MXU_DTYPE)
        ckv = (_rms(z[:, Q_LORA:Q_LORA + KV_LORA])[0] * gkv_ref[...]).astype(MXU_DTYPE)
        cs_t = cs_ref[...]
        t = z[:, Q_LORA + KV_LORA:] * cs_t
        k_rope = (t + pltpu.roll(t, QK_ROPE, axis=1)).astype(MXU_DTYPE)
        low = lax.broadcasted_iota(jnp.int32, (1, LANES), 1) < QK_ROPE
        for hd in range(H):
            qf = _dot(cq, wuq_ref[hd])
            tq = qf[:, QK_NOPE:] * cs_t
            tq = tq + pltpu.roll(tq, QK_ROPE, axis=1)
            q_ref[hd, :, :QK_NOPE] = (qf[:, :QK_NOPE] * MLA_QSCALE).astype(MXU_DTYPE)
            q_ref[hd, :, QK_NOPE:] = jnp.where(low, tq * MLA_QSCALE, 0.0).astype(MXU_DTYPE)
            kvf = _dot(ckv, wukv_ref[hd])
            k_ref[hd, :, :QK_NOPE] = kvf[:, :QK_NOPE].astype(MXU_DTYPE)
            k_ref[hd, :, QK_NOPE:] = k_rope
            v_ref[hd] = kvf[:, QK_NOPE:].astype(MXU_DTYPE)

    zc = wcat.shape[1]
    return pl.pallas_call(
        body, name="mla_pre", grid=(T // tm,),
        in_specs=[_rows(tm, D), _full((8, D)), _full(wcat.shape), _full(g_q.shape), _full(g_kv.shape),
                  _full(wuq.shape), _full(wukv.shape), _rows(tm, LANES)],
        out_specs=[_rows(tm, D), _rows(tm, zc),
                   pl.BlockSpec((H, tm, QK_EXT), lambda i: (0, i, 0)),
                   pl.BlockSpec((H, tm, QK_EXT), lambda i: (0, i, 0)),
                   pl.BlockSpec((H, tm, V_DIM), lambda i: (0, i, 0))],
        out_shape=[jax.ShapeDtypeStruct((T, D), MXU_DTYPE), jax.ShapeDtypeStruct((T, zc), jnp.float32),
                   jax.ShapeDtypeStruct((H, T, QK_EXT), MXU_DTYPE), jax.ShapeDtypeStruct((H, T, QK_EXT), MXU_DTYPE),
                   jax.ShapeDtypeStruct((H, T, V_DIM), MXU_DTYPE)],
        compiler_params=_params("parallel"),
    )(x, vec, wcat, g_q, g_kv, wuq, wukv, cs)


def _mla_attn_fwd(q, k, v):
    H, T, _ = q.shape
    tb = _tile(T, ATTN_BLOCK)
    sub = min(ATTN_SUB, tb)
    ns, nb = tb // sub, T // tb
    pairs = [(i, j) for i in range(nb) for j in range(i + 1)]
    qi_tab = jnp.asarray([i for i, _ in pairs], jnp.int32)
    kj_tab = jnp.asarray([j for _, j in pairs], jnp.int32)

    def body(qi_ref, kj_ref, q_ref, k_ref, v_ref, o_ref, lse_ref, m_sc, l_sc, acc_sc):
        qi, kj = qi_ref[pl.program_id(1)], kj_ref[pl.program_id(1)]

        @pl.when(kj == 0)
        def _():
            m_sc[...] = jnp.full_like(m_sc, NEG)
            l_sc[...] = jnp.zeros_like(l_sc)
            acc_sc[...] = jnp.zeros_like(acc_sc)

        def update(r, kk, masked):
            rows, keys = pl.ds(r * sub, sub), pl.ds(kk * sub, sub)
            s = _dot_nt(q_ref[0, rows, :], k_ref[0, keys, :])
            if masked:
                row = lax.broadcasted_iota(jnp.int32, (sub, sub), 0)
                col = lax.broadcasted_iota(jnp.int32, (sub, sub), 1)
                s = jnp.where(col <= row, s, NEG)
            m_prev = m_sc[rows, :]
            m_new = jnp.maximum(m_prev, jnp.max(s, axis=1, keepdims=True))
            alpha = jnp.exp2(m_prev - m_new)
            p = jnp.exp2(s - jnp.tile(m_new, (1, sub // LANES)))
            l_sc[rows, :] = alpha * l_sc[rows, :] + jnp.sum(p, axis=1, keepdims=True)
            acc_sc[rows, :] = alpha * acc_sc[rows, :] + _dot(p.astype(MXU_DTYPE), v_ref[0, keys, :])
            m_sc[rows, :] = m_new

        @pl.when(kj < qi)
        def _():
            for kk in range(ns):
                for r in range(ns):
                    update(r, kk, False)

        @pl.when(kj == qi)
        def _():
            for kk in range(ns):
                for r in range(kk, ns):
                    update(r, kk, r == kk)
            l = l_sc[...]
            o_ref[...] = (acc_sc[...] / l).astype(o_ref.dtype)
            lse = m_sc[...] + jnp.log2(l)
            pick = (lax.broadcasted_iota(jnp.int32, (8, LANES), 1) == 0).astype(jnp.float32)
            row = lax.dot_general(pick, lse, (((1,), (1,)), ((), ())), precision=lax.Precision.HIGHEST,
                                  preferred_element_type=jnp.float32)
            lse_ref[0] = row[0:1, :]

    q_idx = lambda h, p, qi_ref, kj_ref: (h, qi_ref[p], 0)
    kv_idx = lambda h, p, qi_ref, kj_ref: (h, kj_ref[p], 0)
    return pl.pallas_call(
        body, name="mla_attn_fwd",
        grid_spec=pltpu.PrefetchScalarGridSpec(
            num_scalar_prefetch=2, grid=(H, len(pairs)),
            in_specs=[pl.BlockSpec((1, tb, QK_EXT), q_idx), pl.BlockSpec((1, tb, QK_EXT), kv_idx),
                      pl.BlockSpec((1, tb, V_DIM), kv_idx)],
            out_specs=[pl.BlockSpec((tb, V_DIM), lambda h, p, qi_ref, kj_ref: (qi_ref[p], h)),
                       pl.BlockSpec((1, 1, tb), lambda h, p, qi_ref, kj_ref: (h, 0, qi_ref[p]))],
            scratch_shapes=[pltpu.VMEM((tb, LANES), jnp.float32), pltpu.VMEM((tb, LANES), jnp.float32),
                            pltpu.VMEM((tb, V_DIM), jnp.float32)]),
        out_shape=[jax.ShapeDtypeStruct((T, H * V_DIM), MXU_DTYPE), jax.ShapeDtypeStruct((H, 1, T), jnp.float32)],
        compiler_params=_params("parallel", "arbitrary"),
    )(qi_tab, kj_tab, q, k, v)


def _post_attn(o, x, w_o, bias, vec, o_transposed=False):
    T = x.shape[0]
    tm = _tile(T, 512)
    o_spec = pl.BlockSpec((D, tm), lambda i: (0, i)) if o_transposed else _rows(tm, D)

    def body(o_ref, x_ref, w_ref, b_ref, vec_ref, y_ref, xm_ref, h_ref):
        y = (_dot_tn if o_transposed else _dot)(o_ref[...], w_ref[...]) + b_ref[...]
        y_ref[...] = y.astype(y_ref.dtype)
        xm = x_ref[...] + vec_ref[R_GT1:R_GT1 + 1, :] * y
        xm_ref[...] = xm
        xhat, _ = _rms(xm)
        h = xhat * vec_ref[R_GMLP:R_GMLP + 1, :] * (1.0 + vec_ref[R_SC2:R_SC2 + 1, :]) + vec_ref[R_SH2:R_SH2 + 1, :]
        h_ref[...] = h.astype(h_ref.dtype)

    return pl.pallas_call(
        body, name="post_attn", grid=(T // tm,),
        in_specs=[o_spec, _rows(tm, D), _full((D, D)), _full((1, D)), _full((8, D))],
        out_specs=[_rows(tm, D), _rows(tm, D), _rows(tm, D)],
        out_shape=[jax.ShapeDtypeStruct((T, D), MXU_DTYPE), jax.ShapeDtypeStruct((T, D), jnp.float32),
                   jax.ShapeDtypeStruct((T, D), MXU_DTYPE)],
        compiler_params=_params("parallel"),
    )(o, x, w_o, bias, vec)


def _ff_specs(tf):
    per = D_FF // N_CHIPS // tf
    w1 = pl.BlockSpec((None, D, tf), lambda i, f: (f // per, 0, f % per))
    w2 = pl.BlockSpec((None, tf, D), lambda i, f: (f // per, f % per, 0))
    return w1, w2


def _mlp_fwd(h2, w1, w2, xm, vec):
    T = h2.shape[0]
    tm = _tile(T, MLP_FWD_TILE[0])
    tf = _tile(D_FF // N_CHIPS, MLP_FWD_TILE[1])
    nf = D_FF // tf
    w1_spec, w2_spec = _ff_specs(tf)

    def body(h_ref, w1_ref, w2_ref, xm_ref, vec_ref, a_ref, y_ref, xo_ref, acc):
        f = pl.program_id(1)

        @pl.when(f == 0)
        def _():
            acc[...] = jnp.zeros_like(acc)

        u = jnp.maximum(_dot(h_ref[...], w1_ref[...]), 0.0)
        ab = (u * u).astype(MXU_DTYPE)
        a_ref[...] = ab
        acc[...] += _dot(ab, w2_ref[...])

        @pl.when(f == nf - 1)
        def _():
            y = acc[...]
            y_ref[...] = y.astype(y_ref.dtype)
            xo_ref[...] = xm_ref[...] + vec_ref[R_GT2:R_GT2 + 1, :] * y

    return pl.pallas_call(
        body, name="mlp_fwd", grid=(T // tm, nf),
        in_specs=[_rows(tm, D), w1_spec, w2_spec, _rows(tm, D), _full((8, D))],
        out_specs=[pl.BlockSpec((tm, tf), lambda i, f: (i, f)), _rows(tm, D), _rows(tm, D)],
        out_shape=[jax.ShapeDtypeStruct((T, D_FF), MXU_DTYPE), jax.ShapeDtypeStruct((T, D), MXU_DTYPE),
                   jax.ShapeDtypeStruct((T, D), jnp.float32)],
        scratch_shapes=[pltpu.VMEM((tm, D), jnp.float32)],
        compiler_params=_params("parallel", "arbitrary"),
    )(h2, w1, w2, xm, vec)


def _swa_pre(x, vec, w_qkv, b_qkv):
    T = x.shape[0]
    tm = _tile(T, 512)
    nq = SWA_HEADS * SWA_HEAD_DIM
    nk = SWA_KV_HEADS * SWA_HEAD_DIM
    wq_t, w_kv = w_qkv[:, :nq].T, w_qkv[:, nq:]
    bq_col, b_kv = b_qkv[:, :nq].reshape(nq, 1), b_qkv[:, nq:]

    def body(x_ref, vec_ref, wq_ref, wkv_ref, bq_ref, bkv_ref, h_ref, qt_ref, k_ref, v_ref):
        xhat, _ = _rms(x_ref[...])
        h = xhat * vec_ref[R_GMIX:R_GMIX + 1, :] * (1.0 + vec_ref[R_SC1:R_SC1 + 1, :]) + vec_ref[R_SH1:R_SH1 + 1, :]
        hb = h.astype(MXU_DTYPE)
        h_ref[...] = hb
        qt_ref[...] = ((_dot_nt(wq_ref[...], hb) + bq_ref[...]) * SWA_SCALE).astype(MXU_DTYPE)
        kv = _dot(hb, wkv_ref[...]) + bkv_ref[...]
        k_ref[...] = kv[:, :nk].astype(MXU_DTYPE)
        v_ref[...] = kv[:, nk:].astype(MXU_DTYPE)

    return pl.pallas_call(
        body, name="swa_pre", grid=(T // tm,),
        in_specs=[_rows(tm, D), _full((8, D)), _full(wq_t.shape), _full(w_kv.shape), _full(bq_col.shape),
                  _full(b_kv.shape)],
        out_specs=[_rows(tm, D), pl.BlockSpec((nq, tm), lambda i: (0, i)), _rows(tm, nk), _rows(tm, nk)],
        out_shape=[jax.ShapeDtypeStruct((T, D), MXU_DTYPE), jax.ShapeDtypeStruct((nq, T), MXU_DTYPE),
                   jax.ShapeDtypeStruct((T, nk), MXU_DTYPE), jax.ShapeDtypeStruct((T, nk), MXU_DTYPE)],
        compiler_params=_params("parallel"),
    )(x, vec, wq_t, w_kv, bq_col, b_kv)


def _swa_bias():
    W = WINDOW
    slopes = 2.0 ** (-8.0 * np.arange(1, SWA_HEADS + 1) / SWA_HEADS)
    dist = W + np.arange(W)[None, :] - np.arange(2 * W)[:, None]
    inside = (dist >= 0) & (dist < W)
    bias = np.where(inside[None], -slopes[:, None, None] * dist[None].astype(np.float64), NEG)
    bias = bias.reshape(SWA_KV_HEADS, SWA_GROUP, 2 * W, W).transpose(0, 2, 1, 3)
    return jnp.asarray(bias.reshape(SWA_KV_HEADS, 2 * W, SWA_GROUP * W), jnp.float32)


SWA_STEP_BLOCKS = 4


def _swa_blocks(T):
    nb = T // WINDOW
    return next(b for b in (SWA_STEP_BLOCKS, 2, 1) if nb % b == 0)


def _swa_views(b, qt_ref, kp_ref, kc_ref):
    W = WINDOW
    prev = kp_ref if b == 0 else kc_ref.at[pl.ds((b - 1) * W, W), :]
    return qt_ref.at[:, pl.ds(b * W, W)], prev, kc_ref.at[pl.ds(b * W, W), :]


def _swa_probs(has_prev, kh, qt_ref, kp_ref, kc_ref, bias_ref, sink_ref):
    W, Dh, G = WINDOW, SWA_HEAD_DIM, SWA_GROUP
    qt = jnp.concatenate([qt_ref[(kh * G + g) * Dh:(kh * G + g + 1) * Dh, :] for g in range(G)], axis=1)
    kb = jnp.concatenate([kp_ref[:, kh * Dh:(kh + 1) * Dh], kc_ref[:, kh * Dh:(kh + 1) * Dh]], axis=0)
    s = _dot(kb, qt) + bias_ref[kh]
    if has_prev is not True:
        key = lax.broadcasted_iota(jnp.int32, (2 * W, 1), 0)
        s = jnp.where((key >= W) | has_prev, s, NEG)
    sink = sink_ref[kh]
    m = jnp.maximum(jnp.max(s, axis=0, keepdims=True), sink)
    p = jnp.exp(s - m)
    p_sink = jnp.exp(sink - m)
    inv = 1.0 / (jnp.sum(p, axis=0, keepdims=True) + p_sink)
    return qt, kb, p * inv, p_sink * inv


def _swa_attn_fwd(qt, k, v, bias, sink_rows):
    T = qt.shape[1]
    W, Dh, G, Hk = WINDOW, SWA_HEAD_DIM, SWA_GROUP, SWA_KV_HEADS
    nk = Hk * Dh

    nb = _swa_blocks(T)

    def body(qt_ref, kp_ref, kc_ref, vp_ref, vc_ref, bias_ref, sink_ref, ot_ref):
        n = pl.program_id(0)
        for b in range(nb):
            q_b, kp_b, kc_b = _swa_views(b, qt_ref, kp_ref, kc_ref)
            _, vp_b, vc_b = _swa_views(b, qt_ref, vp_ref, vc_ref)
            for kh in range(Hk):
                _, _, pn, _ = _swa_probs(True if b else n > 0, kh, q_b, kp_b, kc_b, bias_ref, sink_ref)
                vb = jnp.concatenate([vp_b[:, kh * Dh:(kh + 1) * Dh], vc_b[:, kh * Dh:(kh + 1) * Dh]], axis=0)
                ot = _dot_tn(vb, pn.astype(MXU_DTYPE))
                for g in range(G):
                    rows = pl.ds((kh * G + g) * Dh, Dh)
                    ot_ref[rows, pl.ds(b * W, W)] = ot[:, g * W:(g + 1) * W].astype(ot_ref.dtype)

    prev = lambda n: (jnp.maximum(n * nb - 1, 0), 0)
    cur = lambda n: (n, 0)
    col = lambda n: (0, n)
    return pl.pallas_call(
        body, name="swa_attn_fwd", grid=(T // (nb * W),),
        in_specs=[pl.BlockSpec((D, nb * W), col), pl.BlockSpec((W, nk), prev), pl.BlockSpec((nb * W, nk), cur),
                  pl.BlockSpec((W, nk), prev), pl.BlockSpec((nb * W, nk), cur), _full(bias.shape),
                  _full(sink_rows.shape)],
        out_specs=pl.BlockSpec((D, nb * W), col),
        out_shape=jax.ShapeDtypeStruct((D, T), MXU_DTYPE),
        compiler_params=_params("parallel"),
    )(qt, k, k, v, v, bias, sink_rows)


def _final_loss(x, tgt, g):
    T = x.shape[0]
    tm = _tile(T, 512)

    def body(x_ref, t_ref, g_ref, loss_ref, dx_ref, dg_ref):
        @pl.when(pl.program_id(0) == 0)
        def _():
            loss_ref[...] = jnp.zeros_like(loss_ref)
            dg_ref[...] = jnp.zeros_like(dg_ref)

        xhat, r = _rms(x_ref[...])
        gv = g_ref[...]
        e = xhat * gv - t_ref[...]
        loss_ref[...] += 0.5 * jnp.sum(jnp.mean(e * e, axis=-1, keepdims=True), axis=0, keepdims=True)
        dy = e * (1.0 / D)
        dg_ref[...] += _rowsum(dy * xhat)
        dx_ref[...] = _rms_bwd(dy * gv, xhat, r)

    return pl.pallas_call(
        body, name="final_loss", grid=(T // tm,),
        in_specs=[_rows(tm, D), _rows(tm, D), _full((1, D))],
        out_specs=[_full((8, LANES)), _rows(tm, D), _full((1, D))],
        out_shape=[jax.ShapeDtypeStruct((8, LANES), jnp.float32), jax.ShapeDtypeStruct((T, D), jnp.float32),
                   jax.ShapeDtypeStruct((1, D), jnp.float32)],
        compiler_params=_params("arbitrary"),
    )(x, tgt, g)


def _mlp_bwd(dxo, y2, a, w1, w2, xm, vec):
    T = dxo.shape[0]
    tm = _tile(T, MLP_BWD_TILE[0])
    tf = _tile(D_FF // N_CHIPS, MLP_BWD_TILE[1])
    nf = D_FF // tf
    w1_spec, w2_spec = _ff_specs(tf)

    def body(dxo_ref, y_ref, a_ref, w1_ref, w2_ref, xm_ref, vec_ref, du_ref, dy_ref, dxm_ref, ps_ref, dyb, acc):
        i, f = pl.program_id(0), pl.program_id(1)

        @pl.when((i == 0) & (f == 0))
        def _():
            ps_ref[...] = jnp.zeros_like(ps_ref)

        @pl.when(f == 0)
        def _():
            dxo_t = dxo_ref[...]
            d = (dxo_t * vec_ref[R_GT2:R_GT2 + 1, :]).astype(MXU_DTYPE)
            dyb[...] = d
            dy_ref[...] = d
            acc[...] = jnp.zeros_like(acc)
            ps_ref[R_GT2:R_GT2 + 1, :] += _rowsum(dxo_t * y_ref[...].astype(jnp.float32))

        da = _dot_nt(dyb[...], w2_ref[...])
        dub = (da * (2.0 * jnp.sqrt(a_ref[...].astype(jnp.float32)))).astype(MXU_DTYPE)
        du_ref[...] = dub
        acc[...] += _dot_nt(dub, w1_ref[...])

        @pl.when(f == nf - 1)
        def _():
            dxm_ref[...] = _modulate_bwd(acc[...], xm_ref[...], vec_ref, R_GMLP, R_SC2, R_SH2, ps_ref, dxo_ref[...])

    return pl.pallas_call(
        body, name="mlp_bwd", grid=(T // tm, nf),
        in_specs=[_rows(tm, D), _rows(tm, D), pl.BlockSpec((tm, tf), lambda i, f: (i, f)), w1_spec, w2_spec,
                  _rows(tm, D), _full((8, D))],
        out_specs=[pl.BlockSpec((tm, tf), lambda i, f: (i, f)), _rows(tm, D), _rows(tm, D), _full((8, D))],
        out_shape=[jax.ShapeDtypeStruct((T, D_FF), MXU_DTYPE), jax.ShapeDtypeStruct((T, D), MXU_DTYPE),
                   jax.ShapeDtypeStruct((T, D), jnp.float32), jax.ShapeDtypeStruct((8, D), jnp.float32)],
        scratch_shapes=[pltpu.VMEM((tm, D), MXU_DTYPE), pltpu.VMEM((tm, D), jnp.float32)],
        compiler_params=_params("arbitrary", "arbitrary"),
    )(dxo, y2, a, w1, w2, xm, vec)


def _mm_tn(a, g, name, split=None, layers=1, layer=0, into=None, a_transposed=False):
    K, T = a.shape if a_transposed else a.shape[::-1]
    N = g.shape[1]
    kq = K // N_CHIPS if split == "rows" else K
    nq = N // N_CHIPS if split == "cols" else N
    bk, bn, bt = _tile(kq, 1024), _tile(nq, 1024), _tile(T, DW_TOKENS)
    if nq % bn or bn % LANES:
        bn = nq
    kper, nper = kq // bk, nq // bn

    def body(*refs):
        a_ref, g_ref, o_ref = refs[0], refs[1], refs[-1]

        @pl.when(pl.program_id(2) == 0)
        def _():
            o_ref[...] = jnp.zeros_like(o_ref)

        o_ref[...] += (_dot if a_transposed else _dot_tn)(a_ref[...], g_ref[...])

    a_spec = pl.BlockSpec((bk, bt), lambda k, n, t: (k, t)) if a_transposed else pl.BlockSpec((bt, bk), lambda k, n, t: (t, k))
    in_specs = [a_spec, pl.BlockSpec((bt, bn), lambda k, n, t: (t, n))]
    args = [a, g]
    aliases = {}
    if split is None:
        out_spec = pl.BlockSpec((bk, bn), lambda k, n, t: (k, n))
        out_shape = jax.ShapeDtypeStruct((K, N), jnp.float32)
    else:
        if split == "cols":
            idx = lambda k, n, t: (n // nper, layer, k, n % nper)
        else:
            idx = lambda k, n, t: (k // kper, layer, k % kper, n)
        out_spec = pl.BlockSpec((None, None, bk, bn), idx)
        out_shape = jax.ShapeDtypeStruct((N_CHIPS, layers, kq, nq), jnp.float32)
        if into is not None:
            in_specs.append(pl.BlockSpec(memory_space=pl.ANY))
            args.append(into)
            aliases = {2: 0}
    return pl.pallas_call(
        body, name=name, grid=(K // bk, N // bn, T // bt), in_specs=in_specs, out_specs=out_spec, out_shape=out_shape,
        input_output_aliases=aliases, compiler_params=_params("parallel", "parallel", "arbitrary"),
    )(*args)


def _attn_out_bwd(dxm, y1, o, w_o, vec, with_delta):
    T = dxm.shape[0]
    tm = _tile(T, 512)
    H = MLA_HEADS

    def body(dxm_ref, y_ref, w_ref, vec_ref, *refs):
        o_ref = refs[0] if with_delta else None
        dy_ref, do_ref, ps_ref, *delta_ref = refs[1:] if with_delta else refs

        @pl.when(pl.program_id(0) == 0)
        def _():
            ps_ref[...] = jnp.zeros_like(ps_ref)

        dxm_t = dxm_ref[...]
        dy = dxm_t * vec_ref[R_GT1:R_GT1 + 1, :]
        ps_ref[R_GT1:R_GT1 + 1, :] += _rowsum(dxm_t * y_ref[...].astype(jnp.float32))
        ps_ref[R_BO:R_BO + 1, :] += _rowsum(dy)
        dyb = dy.astype(MXU_DTYPE)
        dy_ref[...] = dyb
        if not with_delta:
            do_ref[...] = _dot_nt(w_ref[...], dyb).astype(do_ref.dtype)
        else:
            do = _dot_nt(dyb, w_ref[...])
            do_ref[...] = do.astype(do_ref.dtype)
            of = o_ref[...].astype(jnp.float32)
            ones = jnp.ones((8, V_DIM), jnp.float32)
            for hd in range(H):
                sl = slice(hd * V_DIM, (hd + 1) * V_DIM)
                d = lax.dot_general(ones, do[:, sl] * of[:, sl], (((1,), (1,)), ((), ())),
                                    precision=lax.Precision.HIGHEST, preferred_element_type=jnp.float32)
                delta_ref[0][hd] = d[0:1, :]

    out_specs = [_rows(tm, D), _rows(tm, D), _full((8, D))]
    out_shape = [jax.ShapeDtypeStruct((T, D), MXU_DTYPE), jax.ShapeDtypeStruct((T, D), MXU_DTYPE),
                 jax.ShapeDtypeStruct((8, D), jnp.float32)]
    if not with_delta:
        out_specs[1] = pl.BlockSpec((D, tm), lambda i: (0, i))
        out_shape[1] = jax.ShapeDtypeStruct((D, T), MXU_DTYPE)
    if with_delta:
        out_specs.append(pl.BlockSpec((H, 1, tm), lambda i: (0, 0, i)))
        out_shape.append(jax.ShapeDtypeStruct((H, 1, T), jnp.float32))
    return pl.pallas_call(
        body, name="attn_out_bwd_mla" if with_delta else "attn_out_bwd_swa", grid=(T // tm,),
        in_specs=[_rows(tm, D), _rows(tm, D), _full((D, D)), _full((8, D))] + ([_rows(tm, D)] if with_delta else []),
        out_specs=out_specs, out_shape=out_shape,
        compiler_params=_params("arbitrary"),
    )(dxm, y1, w_o, vec, *([o] if with_delta else []))


def _mla_attn_bwd(q, k, v, do, lse, delta):
    H, T, _ = q.shape
    tb = _tile(T, ATTN_BLOCK)
    sub = min(ATTN_SUB, tb)
    ns, nb = tb // sub, T // tb

    pairs = [(j, i) for j in range(nb) for i in range(j, nb)]
    kj_tab = jnp.asarray([j for j, _ in pairs], jnp.int32)
    qi_tab = jnp.asarray([i for _, i in pairs], jnp.int32)

    def body(kj_ref, qi_ref, q_ref, k_ref, v_ref, do_ref, lse_ref, dl_ref, dq_ref, dk_ref, dv_ref, dk_acc, dv_acc):
        j, i = kj_ref[pl.program_id(1)], qi_ref[pl.program_id(1)]

        @pl.when((j == 0) & (i == 0))
        def _():
            dq_ref[...] = jnp.zeros_like(dq_ref)

        def update(kk, r, masked):
            keys, rows = pl.ds(kk * sub, sub), pl.ds(r * sub, sub)
            kb, qb, dob = k_ref[0, keys, :], q_ref[0, rows, :], do_ref[rows, :]
            st = _dot_nt(kb, qb)
            if masked:
                row = lax.broadcasted_iota(jnp.int32, (sub, sub), 0)
                col = lax.broadcasted_iota(jnp.int32, (sub, sub), 1)
                st = jnp.where(row <= col, st, NEG)
            pt = jnp.exp2(st - lse_ref[0, :, rows])
            dv_acc[keys, :] += _dot(pt.astype(MXU_DTYPE), dob)
            dpt = _dot_nt(v_ref[0, keys, :], dob)
            dst = (pt * (dpt - dl_ref[0, :, rows])).astype(MXU_DTYPE)
            dk_acc[keys, :] += _dot(dst, qb)
            q_rows = pl.ds(pl.multiple_of(i * tb + r * sub, sub), sub)
            dq_ref[0, q_rows, :] += _dot_tn(dst, kb)

        @pl.when(i == j)
        def _():
            dk_acc[...] = jnp.zeros_like(dk_acc)
            dv_acc[...] = jnp.zeros_like(dv_acc)
            for r in range(ns):
                for kk in range(r + 1):
                    update(kk, r, kk == r)

        @pl.when(i > j)
        def _():
            for r in range(ns):
                for kk in range(ns):
                    update(kk, r, False)

        @pl.when(i == nb - 1)
        def _():
            dk_ref[0] = (dk_acc[...] * LN2).astype(dk_ref.dtype)
            dv_ref[0] = dv_acc[...].astype(dv_ref.dtype)

    q_idx = lambda h, p, kj_ref, qi_ref: (h, qi_ref[p], 0)
    kv_idx = lambda h, p, kj_ref, qi_ref: (h, kj_ref[p], 0)
    stat_idx = lambda h, p, kj_ref, qi_ref: (h, 0, qi_ref[p])
    return pl.pallas_call(
        body, name="mla_attn_bwd",
        grid_spec=pltpu.PrefetchScalarGridSpec(
            num_scalar_prefetch=2, grid=(H, len(pairs)),
            in_specs=[pl.BlockSpec((1, tb, QK_EXT), q_idx), pl.BlockSpec((1, tb, QK_EXT), kv_idx),
                      pl.BlockSpec((1, tb, V_DIM), kv_idx),
                      pl.BlockSpec((tb, V_DIM), lambda h, p, kj_ref, qi_ref: (qi_ref[p], h)),
                      pl.BlockSpec((1, 1, tb), stat_idx), pl.BlockSpec((1, 1, tb), stat_idx)],
            out_specs=[pl.BlockSpec((1, T, QK_EXT), lambda h, p, kj_ref, qi_ref: (h, 0, 0)),
                       pl.BlockSpec((1, tb, QK_EXT), kv_idx), pl.BlockSpec((1, tb, V_DIM), kv_idx)],
            scratch_shapes=[pltpu.VMEM((tb, QK_EXT), jnp.float32), pltpu.VMEM((tb, V_DIM), jnp.float32)]),
        out_shape=[jax.ShapeDtypeStruct((H, T, QK_EXT), jnp.float32), jax.ShapeDtypeStruct((H, T, QK_EXT), MXU_DTYPE),
                   jax.ShapeDtypeStruct((H, T, V_DIM), MXU_DTYPE)],
        compiler_params=_params("parallel", "arbitrary"),
    )(kj_tab, qi_tab, q, k, v, do, lse, delta)


def _mla_pre_bwd(x, dxm, vec, hb, z, dq, dk, dv, cs, wcat, g_q, g_kv, wuq, wukv):
    T = x.shape[0]
    tm = _tile(T, 512)
    H = MLA_HEADS
    zc = wcat.shape[1]

    def body(x_ref, dxm_ref, vec_ref, h_ref, z_ref, dq_ref, dk_ref, dv_ref, cs_ref, wcat_ref, gq_ref, gkv_ref,
             wuq_ref, wukv_ref, dx_ref, ps_ref, dgq_ref, dgkv_ref, dwcat_ref, dwuq_ref, dwukv_ref):
        @pl.when(pl.program_id(0) == 0)
        def _():
            for ref in (ps_ref, dgq_ref, dgkv_ref, dwcat_ref, dwuq_ref, dwukv_ref):
                ref[...] = jnp.zeros_like(ref)

        z = z_ref[...]
        cs_t = cs_ref[...]
        cqhat, rq = _rms(z[:, :Q_LORA])
        ckhat, rk = _rms(z[:, Q_LORA:Q_LORA + KV_LORA])
        gq, gkv = gq_ref[...], gkv_ref[...]
        cq = (cqhat * gq).astype(MXU_DTYPE)
        ckv = (ckhat * gkv).astype(MXU_DTYPE)
        dcq = jnp.zeros((tm, Q_LORA), jnp.float32)
        dckv = jnp.zeros((tm, KV_LORA), jnp.float32)
        dkr = jnp.zeros((tm, LANES), jnp.float32)
        for hd in range(H):
            dqh = dq_ref[hd] * MLA_SCALE
            gqh = jnp.concatenate([dqh[:, :QK_NOPE], dqh[:, QK_NOPE:] * cs_t], axis=1).astype(MXU_DTYPE)
            dcq += _dot_nt(gqh, wuq_ref[hd])
            dwuq_ref[hd] += _dot_tn(cq, gqh)
            dkh = dk_ref[hd]
            gkvh = jnp.concatenate([dkh[:, :QK_NOPE], dv_ref[hd]], axis=1)
            dckv += _dot_nt(gkvh, wukv_ref[hd])
            dwukv_ref[hd] += _dot_tn(ckv, gkvh)
            dkr += dkh[:, QK_NOPE:].astype(jnp.float32)
        dgq_ref[...] += _rowsum(dcq * cqhat)
        dgkv_ref[...] += _rowsum(dckv * ckhat)
        dcq_pre = _rms_bwd(dcq * gq, cqhat, rq)
        dckv_pre = _rms_bwd(dckv * gkv, ckhat, rk)
        dkr2 = (dkr + pltpu.roll(dkr, QK_ROPE, axis=1)) * cs_t
        dz = jnp.concatenate([dcq_pre, dckv_pre, dkr2], axis=1).astype(MXU_DTYPE)
        dwcat_ref[...] += _dot_tn(h_ref[...], dz)
        dh = _dot_nt(dz, wcat_ref[...])
        dx_ref[...] = _modulate_bwd(dh, x_ref[...], vec_ref, R_GMIX, R_SC1, R_SH1, ps_ref, dxm_ref[...])

    hblk = lambda w: pl.BlockSpec((H, tm, w), lambda i: (0, i, 0))
    return pl.pallas_call(
        body, name="mla_pre_bwd", grid=(T // tm,),
        in_specs=[_rows(tm, D), _rows(tm, D), _full((8, D)), _rows(tm, D), _rows(tm, zc), hblk(QK_EXT), hblk(QK_EXT),
                  hblk(V_DIM), _rows(tm, LANES), _full(wcat.shape), _full(g_q.shape), _full(g_kv.shape),
                  _full(wuq.shape), _full(wukv.shape)],
        out_specs=[_rows(tm, D), _full((8, D)), _full(g_q.shape), _full(g_kv.shape), _full(wcat.shape),
                   _full(wuq.shape), _full(wukv.shape)],
        out_shape=[jax.ShapeDtypeStruct((T, D), jnp.float32), jax.ShapeDtypeStruct((8, D), jnp.float32),
                   jax.ShapeDtypeStruct(g_q.shape, jnp.float32), jax.ShapeDtypeStruct(g_kv.shape, jnp.float32),
                   jax.ShapeDtypeStruct(wcat.shape, jnp.float32), jax.ShapeDtypeStruct(wuq.shape, jnp.float32),
                   jax.ShapeDtypeStruct(wukv.shape, jnp.float32)],
        compiler_params=_params("arbitrary"),
    )(x, dxm, vec, hb, z, dq, dk, dv, cs, wcat, g_q, g_kv, wuq, wukv)


def _swa_attn_bwd(qt, k, v, dot_, bias, sink_rows):
    T = qt.shape[1]
    W, Dh, G, Hk = WINDOW, SWA_HEAD_DIM, SWA_GROUP, SWA_KV_HEADS
    nk = Hk * Dh
    nb = _swa_blocks(T)

    def body(qt_ref, kp_ref, kc_ref, vp_ref, vc_ref, dot_ref, bias_ref, sink_ref, dqt_ref, dk_ref, dv_ref, dsink_ref):
        n = pl.program_id(0)

        @pl.when(n == 0)
        def _():
            dk_ref[...] = jnp.zeros_like(dk_ref)
            dv_ref[...] = jnp.zeros_like(dv_ref)
            dsink_ref[...] = jnp.zeros_like(dsink_ref)

        def add_rows(first_row, dkb_part, dvb_part):
            rows = pl.ds(pl.multiple_of(first_row, W), W)
            dk_ref[rows, :] += dkb_part
            dv_ref[rows, :] += dvb_part

        for b in range(nb):
            q_b, kp_b, kc_b = _swa_views(b, qt_ref, kp_ref, kc_ref)
            do_b, vp_b, vc_b = _swa_views(b, dot_ref, vp_ref, vc_ref)
            dks, dvs = [], []
            for kh in range(Hk):
                qt, kb, pn, p_sink = _swa_probs(True if b else n > 0, kh, q_b, kp_b, kc_b, bias_ref, sink_ref)
                vb = jnp.concatenate([vp_b[:, kh * Dh:(kh + 1) * Dh], vc_b[:, kh * Dh:(kh + 1) * Dh]], axis=0)
                dot_h = jnp.concatenate([do_b[(kh * G + g) * Dh:(kh * G + g + 1) * Dh, :] for g in range(G)], axis=1)
                dp = _dot(vb, dot_h)
                delta = jnp.sum(pn * dp, axis=0, keepdims=True)
                dsb = (pn * (dp - delta)).astype(MXU_DTYPE)
                dsink_ref[kh] += -p_sink * delta
                dqt = _dot_tn(kb, dsb) * SWA_SCALE
                for g in range(G):
                    dqt_ref[pl.ds((kh * G + g) * Dh, Dh), pl.ds(b * W, W)] = dqt[:, g * W:(g + 1) * W]
                dks.append(_dot_nt(dsb, qt))
                dvs.append(_dot_nt(pn.astype(MXU_DTYPE), dot_h))
            dkb = jnp.concatenate(dks, axis=1)
            dvb = jnp.concatenate(dvs, axis=1)
            add_rows((n * nb + b) * W, dkb[W:], dvb[W:])
            if b:
                add_rows((n * nb + b - 1) * W, dkb[:W], dvb[:W])
            else:
                @pl.when(n > 0)
                def _():
                    add_rows((n * nb - 1) * W, dkb[:W], dvb[:W])

    prev = lambda n: (jnp.maximum(n * nb - 1, 0), 0)
    cur = lambda n: (n, 0)
    col = lambda n: (0, n)
    return pl.pallas_call(
        body, name="swa_attn_bwd", grid=(T // (nb * W),),
        in_specs=[pl.BlockSpec((D, nb * W), col), pl.BlockSpec((W, nk), prev), pl.BlockSpec((nb * W, nk), cur),
                  pl.BlockSpec((W, nk), prev), pl.BlockSpec((nb * W, nk), cur), pl.BlockSpec((D, nb * W), col),
                  _full(bias.shape), _full(sink_rows.shape)],
        out_specs=[pl.BlockSpec((D, nb * W), col), _full((T, nk)), _full((T, nk)), _full(sink_rows.shape)],
        out_shape=[jax.ShapeDtypeStruct((D, T), jnp.float32), jax.ShapeDtypeStruct((T, nk), jnp.float32),
                   jax.ShapeDtypeStruct((T, nk), jnp.float32), jax.ShapeDtypeStruct(sink_rows.shape, jnp.float32)],
        compiler_params=_params("arbitrary"),
    )(qt, k, k, v, v, dot_, bias, sink_rows)


def _swa_pre_bwd(x, dxm, vec, dq_t, dk, dv, w_qkv):
    T = x.shape[0]
    tm = _tile(T, 512)
    nq = SWA_HEADS * SWA_HEAD_DIM
    nk = SWA_KV_HEADS * SWA_HEAD_DIM
    nqkv = nq + 2 * nk

    def body(x_ref, dxm_ref, vec_ref, dq_ref, dk_ref, dv_ref, w_ref, dx_ref, dqkv_ref, ps_ref, db_ref):
        @pl.when(pl.program_id(0) == 0)
        def _():
            ps_ref[...] = jnp.zeros_like(ps_ref)
            db_ref[...] = jnp.zeros_like(db_ref)

        dqkv = jnp.concatenate([dq_ref[...].T, dk_ref[...], dv_ref[...]], axis=1)
        db_ref[...] += _rowsum(dqkv)
        dqkv_b = dqkv.astype(MXU_DTYPE)
        dqkv_ref[...] = dqkv_b
        dh = _dot_nt(dqkv_b, w_ref[...])
        dx_ref[...] = _modulate_bwd(dh, x_ref[...], vec_ref, R_GMIX, R_SC1, R_SH1, ps_ref, dxm_ref[...])

    return pl.pallas_call(
        body, name="swa_pre_bwd", grid=(T // tm,),
        in_specs=[_rows(tm, D), _rows(tm, D), _full((8, D)), pl.BlockSpec((nq, tm), lambda i: (0, i)), _rows(tm, nk),
                  _rows(tm, nk), _full(w_qkv.shape)],
        out_specs=[_rows(tm, D), _rows(tm, nqkv), _full((8, D)), _full((1, nqkv))],
        out_shape=[jax.ShapeDtypeStruct((T, D), jnp.float32), jax.ShapeDtypeStruct((T, nqkv), MXU_DTYPE),
                   jax.ShapeDtypeStruct((8, D), jnp.float32), jax.ShapeDtypeStruct((1, nqkv), jnp.float32)],
        compiler_params=_params("arbitrary"),
    )(x, dxm, vec, dq_t, dk, dv, w_qkv)


def _rot_cols(w):
    half = QK_ROPE // 2
    return jnp.concatenate([-w[..., half:], w[..., :half]], axis=-1)


def _unrot_grad(d_rope, d_rot):
    half = QK_ROPE // 2
    return d_rope + jnp.concatenate([d_rot[..., half:], -d_rot[..., :half]], axis=-1)


def _rope_table(positions):
    half = QK_ROPE // 2
    inv_freq = ROPE_THETA ** (-jnp.arange(half, dtype=jnp.float32) / half)
    ang = positions.astype(jnp.float32)[:, None] * inv_freq
    cos, sin = jnp.cos(ang), jnp.sin(ang)
    return jnp.concatenate([cos, cos, sin, sin], axis=1)


def _sequence_step(x, tgt, positions, vecs, g_q, g_kv, sinks, g_final, wts, late_weights, on_late_grads, on_late_landed):
    H = MLA_HEADS
    cs = _rope_table(positions)
    w_dkv = wts["mla_w_dkv"]
    wcat = jnp.concatenate([wts["mla_w_dq"], w_dkv, _rot_cols(w_dkv[:, KV_LORA:])], axis=1)
    uq = wts["mla_w_uq"].reshape(Q_LORA, H, QK_NOPE + QK_ROPE)
    wuq = jnp.concatenate([uq, _rot_cols(uq[..., QK_NOPE:])], axis=-1).transpose(1, 0, 2)
    wukv = wts["mla_w_ukv"].reshape(KV_LORA, H, QK_NOPE + V_DIM).transpose(1, 0, 2)
    zero_bias = jnp.zeros((1, D), jnp.float32)
    bias = _swa_bias()
    sink_rows = jnp.broadcast_to(sinks.reshape(SWA_KV_HEADS, 1, SWA_GROUP, 1),
                                 (SWA_KV_HEADS, 1, SWA_GROUP, WINDOW)).reshape(SWA_KV_HEADS, 1, SWA_GROUP * WINDOW)

    h1a, z, q, k, v = _mla_pre(x, vecs[0], wcat, g_q, g_kv, wuq, wukv, cs)
    o_a, lse = _mla_attn_fwd(q, k, v)
    y1a, xm_a, h2a = _post_attn(o_a, x, wts["mla_w_o"], zero_bias, vecs[0])
    wts = {**wts, **late_weights(h2a)}
    a_a, y2a, x1 = _mlp_fwd(h2a, wts["w_ff1"][0], wts["w_ff2"][0], xm_a, vecs[0])

    h1b, qs_t, ks, vs = _swa_pre(x1, vecs[1], wts["swa_w_qkv"], wts["swa_b_qkv"])
    o_bt = _swa_attn_fwd(qs_t, ks, vs, bias, sink_rows)
    y1b, xm_b, h2b = _post_attn(o_bt, x1, wts["swa_w_o"], wts["swa_b_o"], vecs[1], o_transposed=True)
    a_b, y2b, x2 = _mlp_fwd(h2b, wts["w_ff1"][1], wts["w_ff2"][1], xm_b, vecs[1])

    loss8, dx2, dg_final = _final_loss(x2, tgt, g_final.reshape(1, D))

    du_b, dy2b, dxm_b, ps_mlp_b = _mlp_bwd(dx2, y2b, a_b, wts["w_ff1"][1], wts["w_ff2"][1], xm_b, vecs[1])
    g_ff2 = _mm_tn(a_b, dy2b, "dw_ff2_l1", "rows", DEPTH, 1)
    g_ff1 = _mm_tn(h2b, du_b, "dw_ff1_l1", "cols", DEPTH, 1)
    dy1b, do_bt, ps_out_b = _attn_out_bwd(dxm_b, y1b, None, wts["swa_w_o"], vecs[1], False)
    g_swa_o = _mm_tn(o_bt, dy1b, "dw_o_swa", a_transposed=True)
    dqs_t, dks, dvs, dsinks = _swa_attn_bwd(qs_t, ks, vs, do_bt, bias, sink_rows)
    dx1, dqkv, ps_pre_b, g_swa_bqkv = _swa_pre_bwd(x1, dxm_b, vecs[1], dqs_t, dks, dvs, wts["swa_w_qkv"])
    g_swa_qkv = _mm_tn(h1b, dqkv, "dw_qkv", "cols")

    du_a, dy2a, dxm_a, ps_mlp_a = _mlp_bwd(dx1, y2a, a_a, wts["w_ff1"][0], wts["w_ff2"][0], xm_a, vecs[0])
    g_ff2 = _mm_tn(a_a, dy2a, "dw_ff2_l0", "rows", DEPTH, 0, g_ff2)
    g_ff1 = _mm_tn(h2a, du_a, "dw_ff1_l0", "cols", DEPTH, 0, g_ff1)
    rows4 = lambda g: g.reshape(N_CHIPS, g.shape[0] // N_CHIPS, g.shape[1])
    token = on_late_grads({
        "swa_w_qkv": g_swa_qkv.reshape(N_CHIPS, D, -1), "swa_w_o": rows4(g_swa_o),
        "w_ff1": g_ff1.reshape(N_CHIPS, DEPTH * D, -1), "w_ff2": g_ff2.reshape(N_CHIPS, -1, D)})
    dy1a, do_a, ps_out_a, delta = _attn_out_bwd(dxm_a, y1a, o_a, wts["mla_w_o"], vecs[0] + token[0, 0], True)
    g_mla_o = _mm_tn(o_a, dy1a, "dw_o_mla")
    token = on_late_landed(g_mla_o)
    dq, dk, dv = _mla_attn_bwd(q, k, v, do_a, lse, delta + token[0, 0])
    dx0, ps_pre_a, dg_q, dg_kv, dwcat, dwuq, dwukv = _mla_pre_bwd(
        x, dxm_a, vecs[0], h1a, z, dq, dk, dv, cs, wcat, g_q, g_kv, wuq, wukv)

    c0, c1, c2 = Q_LORA, Q_LORA + KV_LORA, Q_LORA + KV_LORA + QK_ROPE
    g_dq = dwcat[:, :c0]
    g_dkv = jnp.concatenate([dwcat[:, c0:c1], _unrot_grad(dwcat[:, c1:c2], dwcat[:, c2:])], axis=1)
    e0 = QK_NOPE + QK_ROPE
    g_uq = jnp.concatenate([dwuq[..., :QK_NOPE], _unrot_grad(dwuq[..., QK_NOPE:e0], dwuq[..., e0:])], axis=-1)
    per = H // N_CHIPS
    g_uq = g_uq.reshape(N_CHIPS, per, Q_LORA, e0).transpose(0, 2, 1, 3).reshape(N_CHIPS, Q_LORA, per * e0)
    g_ukv = dwukv.reshape(N_CHIPS, per, KV_LORA, QK_NOPE + V_DIM).transpose(0, 2, 1, 3)
    g_ukv = g_ukv.reshape(N_CHIPS, KV_LORA, per * (QK_NOPE + V_DIM))

    def dmod(ps_pre, ps_out, ps_mlp):
        return jnp.concatenate([ps_pre[R_SH1:R_SC1 + 1], ps_out[R_GT1:R_GT1 + 1], ps_mlp[R_SH2:R_GT2 + 1]], axis=0)

    grads = {"mla_w_dq": rows4(g_dq), "mla_w_uq": g_uq, "mla_w_dkv": rows4(g_dkv), "mla_w_ukv": g_ukv,
             "mla_w_o": rows4(g_mla_o)}
    small = {
        "dmod": jnp.stack([dmod(ps_pre_a, ps_out_a, ps_mlp_a), dmod(ps_pre_b, ps_out_b, ps_mlp_b)]).reshape(DEPTH, 6 * D),
        "g_mix": jnp.stack([ps_pre_a[R_GMIX], ps_pre_b[R_GMIX]]),
        "g_mlp": jnp.stack([ps_mlp_a[R_GMLP], ps_mlp_b[R_GMLP]]),
        "mla_g_q": dg_q, "mla_g_kv": dg_kv, "swa_sinks": jnp.sum(dsinks.reshape(SWA_HEADS, WINDOW), axis=1).reshape(1, SWA_HEADS),
        "swa_b_qkv": g_swa_bqkv, "swa_b_o": ps_out_b[R_BO:R_BO + 1],
        "g_final": dg_final.reshape(D), "loss": loss8[0, 0],
    }
    return dx0, grads, small


SHARDED = {
    "mla_w_dq": (1, D // N_CHIPS, Q_LORA),
    "mla_w_uq": (1, Q_LORA, MLA_HEADS * (QK_NOPE + QK_ROPE) // N_CHIPS),
    "mla_w_dkv": (1, D // N_CHIPS, KV_LORA + QK_ROPE),
    "mla_w_ukv": (1, KV_LORA, MLA_HEADS * (QK_NOPE + V_DIM) // N_CHIPS),
    "mla_w_o": (1, MLA_HEADS * V_DIM // N_CHIPS, D),
    "swa_w_qkv": (1, D, (SWA_HEADS + 2 * SWA_KV_HEADS) * SWA_HEAD_DIM // N_CHIPS),
    "swa_w_o": (1, SWA_HEADS * SWA_HEAD_DIM // N_CHIPS, D),
    "w_ff1": (DEPTH, D, D_FF // N_CHIPS),
    "w_ff2": (DEPTH, D_FF // N_CHIPS, D),
}
COL_SPLIT = ("mla_w_uq", "mla_w_ukv", "swa_w_qkv")
BIASES = {"swa_b_qkv": (SWA_HEADS + 2 * SWA_KV_HEADS) * SWA_HEAD_DIM, "swa_b_o": D}


def _view2d(name):
    shape = SHARDED[name]
    return math.prod(shape[:-1]), shape[-1]


SMALL = {"b_ada": (DEPTH, 6 * D), "g_mix": (DEPTH, D), "g_mlp": (DEPTH, D), "mla_g_q": (1, Q_LORA),
         "mla_g_kv": (1, KV_LORA), "swa_sinks": (1, SWA_HEADS), "g_final": (D,), "loss": (),
         "swa_b_qkv": (1, BIASES["swa_b_qkv"]), "swa_b_o": (1, BIASES["swa_b_o"])}
SMALL_ROWS = 192
DMA_ROWS = 256


SLOT_ROWS = 8


def _small_slots():
    slots, off = {}, 0
    for name, shape in SMALL.items():
        n = max(math.prod(shape), 1)
        slots[name] = (off, n)
        off += -(-n // (SLOT_ROWS * LANES)) * SLOT_ROWS * LANES
    assert off <= SMALL_ROWS * LANES
    return slots


def _pack_small(vals):
    parts, end = [], 0
    for name, (off, n) in _small_slots().items():
        pad = -(-n // (SLOT_ROWS * LANES)) * SLOT_ROWS * LANES - n
        v = vals[name].astype(jnp.float32).reshape(-1) if name in vals else jnp.zeros((n,), jnp.float32)
        parts += [v, jnp.zeros((pad,), jnp.float32)]
        end = off + n + pad
    parts.append(jnp.zeros((SMALL_ROWS * LANES - end,), jnp.float32))
    return jnp.concatenate(parts).reshape(SMALL_ROWS, LANES)


def _from_slot(name, rows):
    n = max(math.prod(SMALL[name]), 1)
    return rows.reshape(-1)[:n].reshape(SMALL[name])


def _pieces(rows):
    return [(off, min(DMA_ROWS, rows - off)) for off in range(0, rows, DMA_ROWS)]


HBM = pl.BlockSpec(memory_space=pltpu.HBM)
MESH = pl.DeviceIdType.MESH


def _place():
    x, y, c = lax.axis_index("x"), lax.axis_index("y"), lax.axis_index("c")
    chips = [(1 - x, y), (x, 1 - y), (1 - x, 1 - y)]
    return x, y, c, chips


def _all_gather(block):
    m_per, n = block.shape

    def body(x_ref, out_ref, send_sems, recv_sems, local_sem):
        x, y, c, chips = _place()
        me, sibling = (x, y, c), (x, y, 1 - c)

        def rows(px, py, pc):
            return out_ref.at[pl.ds((4 * px + 2 * py + pc) * m_per, m_per), :]

        def copy(k, blk, to, src=None):
            return pltpu.make_async_remote_copy(
                src_ref=rows(*blk) if src is None else src, dst_ref=rows(*blk),
                send_sem=send_sems.at[k], recv_sem=recv_sems.at[k], device_id=to, device_id_type=MESH)

        mine = pltpu.make_async_copy(x_ref, rows(*me), local_sem)
        mine.start()
        first = [copy(0, me, sibling, src=x_ref)]
        first += [copy(1 + j, me, (*chip, c), src=x_ref) for j, chip in enumerate(chips)]
        for cp in first:
            cp.start()
        passed = [copy(4 + j, (*chip, c), sibling) for j, chip in enumerate(chips)]
        for j, chip in enumerate(chips):
            copy(1 + j, (*chip, c), me).wait_recv()
            passed[j].start()
        copy(0, sibling, me).wait_recv()
        for j, chip in enumerate(chips):
            copy(4 + j, (*chip, 1 - c), me).wait_recv()
        for cp in first + passed:
            cp.wait_send()
        mine.wait()

    out = pl.pallas_call(
        body, name="all_gather_small",
        out_shape=jax.ShapeDtypeStruct((N_DEV * m_per, n), block.dtype),
        in_specs=[pl.BlockSpec(memory_space=pltpu.VMEM)],
        out_specs=pl.BlockSpec(memory_space=pltpu.VMEM),
        scratch_shapes=[pltpu.SemaphoreType.DMA((7,)), pltpu.SemaphoreType.DMA((7,)), pltpu.SemaphoreType.DMA],
    )(block)
    return out.reshape(N_DEV, m_per, n)


def _weight_gather(shards):
    nt = len(shards)

    def body(*refs):
        w_refs, out_refs = refs[:nt], refs[nt:2 * nt]
        send_sems, recv_sems = refs[2 * nt:]
        x, y, c, chips = _place()
        sibling = (x, y, 1 - c)

        def slab(t, px, py, half):
            rh = shards[t].shape[0] // 2
            return out_refs[t].at[2 * px + py, pl.ds(half * rh, rh), :]

        def copy(t, k, src, dst, to):
            return pltpu.make_async_remote_copy(src_ref=src, dst_ref=dst, send_sem=send_sems.at[6 * t + k],
                                                recv_sem=recv_sems.at[6 * t + k], device_id=to, device_id_type=MESH)

        first = []
        for t in range(nt):
            rh = shards[t].shape[0] // 2
            first += [copy(t, j, w_refs[t].at[pl.ds(c * rh, rh), :], slab(t, x, y, c), (*chip, c))
                      for j, chip in enumerate(chips)]
        for cp in first:
            cp.start()
        passed = []
        for t in range(nt):
            for j, chip in enumerate(chips):
                copy(t, j, slab(t, *chip, c), slab(t, *chip, c), (*chip, c)).wait_recv()
                rh = shards[t].shape[0] // 2
                for off, n in _pieces(rh):
                    piece = out_refs[t].at[2 * chip[0] + chip[1], pl.ds(c * rh + off, n), :]
                    copy(t, 3 + j, piece, piece, sibling).start()
                passed.append(copy(t, 3 + j, slab(t, *chip, c), slab(t, *chip, c), sibling))
        for t in range(nt):
            for j, chip in enumerate(chips):
                copy(t, 3 + j, slab(t, *chip, 1 - c), slab(t, *chip, 1 - c), sibling).wait_recv()
        for cp in first + passed:
            cp.wait_send()

    return pl.pallas_call(
        body, name="weight_gather",
        out_shape=[jax.ShapeDtypeStruct((N_CHIPS,) + s.shape, s.dtype) for s in shards],
        in_specs=[HBM] * nt, out_specs=[HBM] * nt,
        scratch_shapes=[pltpu.SemaphoreType.DMA((6 * nt,)), pltpu.SemaphoreType.DMA((6 * nt,))],
    )(*shards)


SEM = pl.BlockSpec(memory_space=pltpu.SEMAPHORE)
ANY = pl.BlockSpec(memory_space=pl.ANY)
SPLIT_COPY = pltpu.SideEffectType.DATAFLOW_SIDE_EFFECTING


def _late_copies(w_refs, land_refs, send_sems, recv_sems):
    x, y, c, chips = _place()
    return [pltpu.make_async_remote_copy(
        src_ref=w_refs[t], dst_ref=land_refs[t].at[2 * x + y], send_sem=send_sems.at[3 * t + j],
        recv_sem=recv_sems.at[3 * t + j], device_id=(cx, cy, c), device_id_type=MESH)
        for t in range(len(w_refs)) for j, (cx, cy) in enumerate(chips)], chips


def _late_gather_start(shards, after):
    nt, na = len(shards), len(after)

    def body(*refs):
        w_refs, land_refs = refs[:nt], refs[nt:2 * nt]
        send_sems, recv_sems, token = refs[2 * nt + na], refs[2 * nt + na + 1], refs[-1]
        copies, _ = _late_copies(w_refs, land_refs, send_sems, recv_sems)
        for cp in copies:
            cp.start()
        token[...] = jnp.zeros_like(token)

    hbm = lambda a: pltpu.with_memory_space_constraint(a, pltpu.HBM)
    lands = [lax.empty((N_CHIPS,) + s.shape, s.dtype) for s in shards]
    outs = pl.pallas_call(
        body, name="late_gather_start",
        out_shape=(pltpu.SemaphoreType.DMA((3 * nt,)), pltpu.SemaphoreType.DMA((3 * nt,)),
                   *[pltpu.HBM(s.shape, s.dtype) for s in shards], *[pltpu.HBM(l.shape, l.dtype) for l in lands],
                   jax.ShapeDtypeStruct((8, LANES), jnp.float32)),
        in_specs=[HBM] * (2 * nt) + [ANY] * na,
        out_specs=(SEM, SEM, *([HBM] * (2 * nt)), pl.BlockSpec(memory_space=pltpu.VMEM)),
        input_output_aliases={i: 2 + i for i in range(2 * nt)},
        compiler_params=pltpu.CompilerParams(has_side_effects=SPLIT_COPY),
    )(*[hbm(s) for s in shards], *[hbm(l) for l in lands], *after)
    return outs[0], outs[1], list(outs[2:2 + nt]), list(outs[2 + nt:2 + 2 * nt]), outs[-1]


def _late_gather_wait(send_sems, recv_sems, shards, lands, after):
    nt = len(shards)

    def body(*refs):
        w_refs, land_refs = refs[:nt], refs[nt:2 * nt]
        s_sems, r_sems = refs[2 * nt], refs[2 * nt + 1]
        x, y, c, chips = _place()
        for t in range(nt):
            for j, (cx, cy) in enumerate(chips):
                cp = pltpu.make_async_remote_copy(
                    src_ref=w_refs[t], dst_ref=land_refs[t].at[2 * cx + cy], send_sem=s_sems.at[3 * t + j],
                    recv_sem=r_sems.at[3 * t + j], device_id=(cx, cy, c), device_id_type=MESH)
                cp.wait_send()
                cp.wait_recv()

    outs = pl.pallas_call(
        body, name="late_gather_wait",
        out_shape=(*[pltpu.HBM(s.shape, s.dtype) for s in shards], *[pltpu.HBM(l.shape, l.dtype) for l in lands]),
        in_specs=[HBM] * (2 * nt) + [SEM, SEM, ANY], out_specs=tuple([HBM] * (2 * nt)),
        input_output_aliases={i: i for i in range(2 * nt)},
        compiler_params=pltpu.CompilerParams(has_side_effects=SPLIT_COPY),
    )(*shards, *lands, send_sems, recv_sems, after)
    return list(outs[nt:])


def _grad_pair_in(grads):
    nt = len(grads)

    def body(*refs):
        g_refs, got_refs = refs[:nt], refs[nt:2 * nt]
        send_sems, recv_sems = refs[2 * nt:]
        x, y, c, _ = _place()
        sibling = (x, y, 1 - c)

        def copy(t, src, dst):
            return pltpu.make_async_remote_copy(src_ref=src, dst_ref=dst, send_sem=send_sems.at[t],
                                                recv_sem=recv_sems.at[t], device_id=sibling, device_id_type=MESH)

        for t in range(nt):
            rh = grads[t].shape[1] // 2
            for p in range(N_CHIPS):
                for off, n in _pieces(rh):
                    copy(t, g_refs[t].at[p, pl.ds((1 - c) * rh + off, n), :], got_refs[t].at[p, pl.ds(off, n), :]).start()
        for t in range(nt):
            rh = grads[t].shape[1] // 2
            copy(t, g_refs[t].at[:, pl.ds((1 - c) * rh, rh), :], got_refs[t]).wait()

    return pl.pallas_call(
        body, name="grad_pair_in",
        out_shape=[jax.ShapeDtypeStruct((N_CHIPS, g.shape[1] // 2, g.shape[2]), g.dtype) for g in grads],
        in_specs=[HBM] * nt, out_specs=[HBM] * nt,
        scratch_shapes=[pltpu.SemaphoreType.DMA((nt,)), pltpu.SemaphoreType.DMA((nt,))],
    )(*grads)


def _pair_in_start(grads):
    nt = len(grads)

    def body(*refs):
        g_refs, land_refs = refs[:nt], refs[nt:2 * nt]
        send_sems, recv_sems, token = refs[2 * nt], refs[2 * nt + 1], refs[-1]
        x, y, c, _ = _place()
        for t in range(nt):
            rh = grads[t].shape[1] // 2
            for p in range(N_CHIPS):
                for off, n in _pieces(rh):
                    pltpu.make_async_remote_copy(
                        src_ref=g_refs[t].at[p, pl.ds((1 - c) * rh + off, n), :], dst_ref=land_refs[t].at[p, pl.ds(off, n), :],
                        send_sem=send_sems.at[t], recv_sem=recv_sems.at[t], device_id=(x, y, 1 - c),
                        device_id_type=MESH).start()
        token[...] = jnp.zeros_like(token)

    hbm = lambda a: pltpu.with_memory_space_constraint(a, pltpu.HBM)
    lands = [lax.empty((N_CHIPS, g.shape[1] // 2, g.shape[2]), g.dtype) for g in grads]
    outs = pl.pallas_call(
        body, name="grad_pair_in_start",
        out_shape=(pltpu.SemaphoreType.DMA((nt,)), pltpu.SemaphoreType.DMA((nt,)),
                   *[pltpu.HBM(g.shape, g.dtype) for g in grads], *[pltpu.HBM(l.shape, l.dtype) for l in lands],
                   jax.ShapeDtypeStruct((8, LANES), jnp.float32)),
        in_specs=[HBM] * (2 * nt),
        out_specs=(SEM, SEM, *([HBM] * (2 * nt)), pl.BlockSpec(memory_space=pltpu.VMEM)),
        input_output_aliases={i: 2 + i for i in range(2 * nt)},
        compiler_params=pltpu.CompilerParams(has_side_effects=SPLIT_COPY),
    )(*[hbm(g) for g in grads], *[hbm(l) for l in lands])
    return outs[0], outs[1], list(outs[2:2 + nt]), list(outs[2 + nt:2 + 2 * nt]), outs[-1]


def _pair_in_wait(send_sems, recv_sems, grads, lands, after):
    nt = len(grads)

    def body(*refs):
        g_refs, land_refs = refs[:nt], refs[nt:2 * nt]
        s_sems, r_sems = refs[2 * nt], refs[2 * nt + 1]
        x, y, c, _ = _place()
        for t in range(nt):
            rh = grads[t].shape[1] // 2
            cp = pltpu.make_async_remote_copy(
                src_ref=g_refs[t].at[:, pl.ds((1 - c) * rh, rh), :], dst_ref=land_refs[t], send_sem=s_sems.at[t],
                recv_sem=r_sems.at[t], device_id=(x, y, 1 - c), device_id_type=MESH)
            cp.wait_send()
            cp.wait_recv()

    outs = pl.pallas_call(
        body, name="grad_pair_in_wait",
        out_shape=(*[pltpu.HBM(g.shape, g.dtype) for g in grads], *[pltpu.HBM(l.shape, l.dtype) for l in lands]),
        in_specs=[HBM] * (2 * nt) + [SEM, SEM, ANY], out_specs=tuple([HBM] * (2 * nt)),
        input_output_aliases={i: i for i in range(2 * nt)},
        compiler_params=pltpu.CompilerParams(has_side_effects=SPLIT_COPY),
    )(*grads, *lands, send_sems, recv_sems, after)
    return list(outs[:nt]), list(outs[nt:])


def _pair_sum(g, got, core, name):
    _, rows, cols = g.shape
    rh = rows // 2
    tr = _tile(rh, 512)
    nb = rh // tr

    def body(c_ref, g_ref, got_ref, s32_ref, s16_ref):
        s = g_ref[...] + got_ref[...]
        s32_ref[...] = s
        s16_ref[...] = s.astype(s16_ref.dtype)

    blk = pl.BlockSpec((None, tr, cols), lambda p, i, c_ref: (p, i, 0))
    return pl.pallas_call(
        body, name=name,
        grid_spec=pltpu.PrefetchScalarGridSpec(
            num_scalar_prefetch=1, grid=(N_CHIPS, nb),
            in_specs=[pl.BlockSpec((None, tr, cols), lambda p, i, c_ref: (p, c_ref[0] * nb + i, 0)), blk],
            out_specs=[blk, blk]),
        out_shape=[jax.ShapeDtypeStruct((N_CHIPS, rh, cols), jnp.float32),
                   jax.ShapeDtypeStruct((N_CHIPS, rh, cols), jnp.bfloat16)],
        compiler_params=_params("parallel", "parallel"),
    )(core, g, got)


def _exchange_start(parts, name):
    nt = len(parts)

    def body(*refs):
        a_refs, land_refs = refs[:nt], refs[nt:2 * nt]
        send_sems, recv_sems, token = refs[2 * nt], refs[2 * nt + 1], refs[-1]
        x, y, c, chips = _place()
        for t in range(nt):
            for j, (cx, cy) in enumerate(chips):
                pltpu.make_async_remote_copy(
                    src_ref=a_refs[t].at[2 * cx + cy], dst_ref=land_refs[t].at[j], send_sem=send_sems.at[3 * t + j],
                    recv_sem=recv_sems.at[3 * t + j], device_id=(cx, cy, c), device_id_type=MESH).start()
        token[...] = jnp.zeros_like(token)

    hbm = lambda a: pltpu.with_memory_space_constraint(a, pltpu.HBM)
    lands = [lax.empty((N_CHIPS - 1,) + a.shape[1:], a.dtype) for a in parts]
    outs = pl.pallas_call(
        body, name=name,
        out_shape=(pltpu.SemaphoreType.DMA((3 * nt,)), pltpu.SemaphoreType.DMA((3 * nt,)),
                   *[pltpu.HBM(a.shape, a.dtype) for a in parts], *[pltpu.HBM(l.shape, l.dtype) for l in lands],
                   jax.ShapeDtypeStruct((8, LANES), jnp.float32)),
        in_specs=[HBM] * (2 * nt),
        out_specs=(SEM, SEM, *([HBM] * (2 * nt)), pl.BlockSpec(memory_space=pltpu.VMEM)),
        input_output_aliases={i: 2 + i for i in range(2 * nt)},
        compiler_params=pltpu.CompilerParams(has_side_effects=SPLIT_COPY),
    )(*[hbm(a) for a in parts], *[hbm(l) for l in lands])
    return outs[0], outs[1], list(outs[2:2 + nt]), list(outs[2 + nt:2 + 2 * nt]), outs[-1]


def _exchange_wait(send_sems, recv_sems, parts, lands, after, name):
    nt = len(parts)

    def body(*refs):
        a_refs, land_refs = refs[:nt], refs[nt:2 * nt]
        s_sems, r_sems = refs[2 * nt], refs[2 * nt + 1]
        x, y, c, chips = _place()
        for t in range(nt):
            for j, (cx, cy) in enumerate(chips):
                cp = pltpu.make_async_remote_copy(
                    src_ref=a_refs[t].at[2 * cx + cy], dst_ref=land_refs[t].at[j], send_sem=s_sems.at[3 * t + j],
                    recv_sem=r_sems.at[3 * t + j], device_id=(cx, cy, c), device_id_type=MESH)
                cp.wait_send()
                cp.wait_recv()

    outs = pl.pallas_call(
        body, name=name,
        out_shape=(*[pltpu.HBM(a.shape, a.dtype) for a in parts], *[pltpu.HBM(l.shape, l.dtype) for l in lands]),
        in_specs=[HBM] * (2 * nt) + [SEM, SEM, ANY], out_specs=tuple([HBM] * (2 * nt)),
        input_output_aliases={i: i for i in range(2 * nt)},
        compiler_params=pltpu.CompilerParams(has_side_effects=SPLIT_COPY),
    )(*parts, *lands, send_sems, recv_sems, after)
    return list(outs[nt:])


def _chip_sum(s32, got, chip, name, behind=None):
    _, rh, cols = s32.shape
    tr = _tile(rh, 512)

    def body(p_ref, s_ref, got_ref, *refs):
        acc = s_ref[...]
        for j in range(N_CHIPS - 1):
            acc = acc + got_ref[j].astype(jnp.float32)
        refs[-1][...] = acc

    extra = [] if behind is None else [behind]
    return pl.pallas_call(
        body, name=name,
        grid_spec=pltpu.PrefetchScalarGridSpec(
            num_scalar_prefetch=1, grid=(rh // tr,),
            in_specs=[pl.BlockSpec((None, tr, cols), lambda i, p_ref: (p_ref[0], i, 0)),
                      pl.BlockSpec((N_CHIPS - 1, tr, cols), lambda i, p_ref: (0, i, 0))]
            + [pl.BlockSpec((8, LANES), lambda i, p_ref: (0, 0))] * len(extra),
            out_specs=pl.BlockSpec((tr, cols), lambda i, p_ref: (i, 0))),
        out_shape=jax.ShapeDtypeStruct((rh, cols), jnp.float32),
        compiler_params=_params("parallel"),
    )(chip, s32, got, *extra)


def _grad_pair_out(halves):
    nt = len(halves)

    def body(*refs):
        h_refs, got_refs = refs[:nt], refs[nt:2 * nt]
        send_sems, recv_sems = refs[2 * nt:]
        x, y, c, _ = _place()
        sibling = (x, y, 1 - c)

        def copy(t, src, dst):
            return pltpu.make_async_remote_copy(src_ref=src, dst_ref=dst, send_sem=send_sems.at[t],
                                                recv_sem=recv_sems.at[t], device_id=sibling, device_id_type=MESH)

        for t in range(nt):
            for off, n in _pieces(halves[t].shape[0]):
                copy(t, h_refs[t].at[pl.ds(off, n), :], got_refs[t].at[pl.ds(off, n), :]).start()
        for t in range(nt):
            copy(t, h_refs[t], got_refs[t]).wait()

    return pl.pallas_call(
        body, name="grad_pair_out",
        out_shape=[jax.ShapeDtypeStruct(h.shape, h.dtype) for h in halves],
        in_specs=[HBM] * nt, out_specs=[HBM] * nt,
        scratch_shapes=[pltpu.SemaphoreType.DMA((nt,)), pltpu.SemaphoreType.DMA((nt,))],
    )(*halves)


def _ada_part(c_all, w_ada):
    L, _, ncol = w_ada.shape
    tn = _tile(ncol, 512)

    def body(c_ref, w_ref, cond_ref, part_ref):
        cv = c_ref[...]
        cond = cv * jax.nn.sigmoid(cv)
        cond_ref[...] = cond
        part_ref[0] = jnp.dot(cond, w_ref[0], precision=lax.Precision.HIGHEST, preferred_element_type=jnp.float32)

    return pl.pallas_call(
        body, name="ada_part", grid=(L, ncol // tn),
        in_specs=[_full((N_DEV, D)), pl.BlockSpec((1, D, tn), lambda l, j: (l, 0, j))],
        out_specs=[_full((N_DEV, D)), pl.BlockSpec((1, N_DEV, tn), lambda l, j: (l, 0, j))],
        out_shape=[jax.ShapeDtypeStruct((N_DEV, D), jnp.float32), jax.ShapeDtypeStruct((L, N_DEV, ncol), jnp.float32)],
        compiler_params=_params("arbitrary", "arbitrary"),
    )(c_all, w_ada)


def _adamw_math(w, g, m, v):
    m = ADAM_B1 * m + (1.0 - ADAM_B1) * g
    v = ADAM_B2 * v + (1.0 - ADAM_B2) * jnp.square(g)
    m_hat = m / (1.0 - ADAM_B1 ** ADAM_STEP)
    v_hat = v / (1.0 - ADAM_B2 ** ADAM_STEP)
    delta = -ADAM_LR * (m_hat / (jnp.sqrt(v_hat) + ADAM_EPS) + ADAM_WD * w)
    return delta, m, v


def _adamw(w, g, m, v, name):
    shape = w.shape
    cols = shape[-1]
    rows = math.prod(shape[:-1])
    tr = _tile(rows, 512)
    two_d = lambda t: t.reshape(rows, cols)

    def body(w_ref, g_ref, m_ref, v_ref, d_ref, mo_ref, vo_ref):
        d_ref[...], mo_ref[...], vo_ref[...] = _adamw_math(w_ref[...], g_ref[...], m_ref[...], v_ref[...])

    out = jax.ShapeDtypeStruct((rows, cols), jnp.float32)
    outs = pl.pallas_call(
        body, name=name, grid=(rows // tr,), in_specs=[_rows(tr, cols)] * 4, out_specs=[_rows(tr, cols)] * 3,
        out_shape=[out, out, out], compiler_params=_params("parallel"),
    )(two_d(w), two_d(g), two_d(m), two_d(v))
    return [t.reshape(shape) for t in outs]


def _adamw_halves(w, mine, got, m, v, core, name):
    shape = w.shape
    cols = shape[-1]
    rows = math.prod(shape[:-1])
    rh = rows // 2
    tr = _tile(rh, 512)
    nbh = rh // tr
    two_d = lambda t: t.reshape(rows, cols)

    def body(c_ref, w_ref, a_ref, b_ref, m_ref, v_ref, g_ref, d_ref, mo_ref, vo_ref):
        g = jnp.where(pl.program_id(0) // nbh == c_ref[0], a_ref[...], b_ref[...])
        g_ref[...] = g
        d_ref[...], mo_ref[...], vo_ref[...] = _adamw_math(w_ref[...], g, m_ref[...], v_ref[...])

    row = pl.BlockSpec((tr, cols), lambda i, c_ref: (i, 0))

    def half(keep):
        return pl.BlockSpec((tr, cols), lambda i, c_ref: (jnp.where((i // nbh == c_ref[0]) == keep, i % nbh, 0), 0))

    out = jax.ShapeDtypeStruct((rows, cols), jnp.float32)
    outs = pl.pallas_call(
        body, name=name,
        grid_spec=pltpu.PrefetchScalarGridSpec(
            num_scalar_prefetch=1, grid=(rows // tr,),
            in_specs=[row, half(True), half(False), row, row], out_specs=[row] * 4),
        out_shape=[out] * 4, compiler_params=_params("arbitrary"),
    )(core, two_d(w), mine, got, two_d(m), two_d(v))
    return [t.reshape(shape) for t in outs]


def _ada_grad_adamw(cond_t, dm, w, m, v):
    L, _, ncol = w.shape
    tn = _tile(ncol, 512)

    def body(ct_ref, dm_ref, w_ref, m_ref, v_ref, g_ref, d_ref, mo_ref, vo_ref):
        g = ct_ref[:, 0:1] * dm_ref[0, 0:1, :]
        for b in range(1, N_DEV):
            g = g + ct_ref[:, b:b + 1] * dm_ref[0, b:b + 1, :]
        g_ref[0] = g
        d_ref[0], mo_ref[0], vo_ref[0] = _adamw_math(w_ref[0], g, m_ref[0], v_ref[0])

    wblk = pl.BlockSpec((1, D, tn), lambda l, j: (l, 0, j))
    out = jax.ShapeDtypeStruct(w.shape, jnp.float32)
    return pl.pallas_call(
        body, name="ada_grad_adamw", grid=(L, ncol // tn),
        in_specs=[_full((D, N_DEV)), pl.BlockSpec((1, N_DEV, tn), lambda l, j: (l, 0, j)), wblk, wblk, wblk],
        out_specs=[wblk] * 4, out_shape=[out] * 4, compiler_params=_params("parallel", "parallel"),
    )(cond_t, dm, w, m, v)


def _small_adamw(gathered, w, m, v):
    slots = _small_slots()
    rows = {name: (off // LANES, -(-n // LANES)) for name, (off, n) in slots.items()}
    kinds = {name: 1 if name == "loss" or name in BIASES else 4 for name in slots}

    def body(ga_ref, w_ref, m_ref, v_ref, *out_refs):
        g = ga_ref[0]
        for dev in range(1, N_DEV):
            g = g + ga_ref[dev]
        d, mo, vo = _adamw_math(w_ref[...], g, m_ref[...], v_ref[...])
        k = 0
        for name, (r0, nr) in rows.items():
            for src in (g, d, mo, vo)[:kinds[name]]:
                out_refs[k][...] = src[r0:r0 + nr, :]
                k += 1

    out_shape = [jax.ShapeDtypeStruct((rows[name][1], LANES), jnp.float32) for name in slots for _ in range(kinds[name])]
    flat = pl.pallas_call(
        body, name="small_adamw", out_shape=out_shape,
        in_specs=[pl.BlockSpec(memory_space=pltpu.VMEM)] * 4,
        out_specs=[pl.BlockSpec(memory_space=pltpu.VMEM)] * len(out_shape),
    )(gathered, w, m, v)
    out, k = {}, 0
    for name in slots:
        out[name] = [_from_slot(name, t) for t in flat[k:k + kinds[name]]]
        k += kinds[name]
    return out


def _one_hot_pick(arr, index, axis):
    n = arr.shape[axis]
    shape = [1] * arr.ndim
    shape[axis] = n
    hot = (jnp.arange(n) == index).astype(arr.dtype).reshape(shape)
    return jnp.sum(arr * hot, axis=axis)


def kernel(x, c, positions, w_ada, b_ada, g_mix, g_mlp, mla_w_dq, mla_g_q, mla_w_uq, mla_w_dkv, mla_g_kv, mla_w_ukv, mla_w_o, swa_w_qkv, swa_b_qkv, swa_sinks, swa_w_o, swa_b_o, w_ff1, w_ff2, g_final, loss_target, m_w_ada, m_b_ada, m_g_mix, m_g_mlp, m_mla_w_dq, m_mla_g_q, m_mla_w_uq, m_mla_w_dkv, m_mla_g_kv, m_mla_w_ukv, m_mla_w_o, m_swa_w_qkv, m_swa_b_qkv, m_swa_sinks, m_swa_w_o, m_swa_b_o, m_w_ff1, m_w_ff2, m_g_final, v_w_ada, v_b_ada, v_g_mix, v_g_mlp, v_mla_w_dq, v_mla_g_q, v_mla_w_uq, v_mla_w_dkv, v_mla_g_kv, v_mla_w_ukv, v_mla_w_o, v_swa_w_qkv, v_swa_b_qkv, v_swa_sinks, v_swa_w_o, v_swa_b_o, v_w_ff1, v_w_ff2, v_g_final):
    W = dict(w_ada=w_ada, b_ada=b_ada, g_mix=g_mix, g_mlp=g_mlp, mla_w_dq=mla_w_dq, mla_g_q=mla_g_q, mla_w_uq=mla_w_uq,
             mla_w_dkv=mla_w_dkv, mla_g_kv=mla_g_kv, mla_w_ukv=mla_w_ukv, mla_w_o=mla_w_o, swa_w_qkv=swa_w_qkv,
             swa_b_qkv=swa_b_qkv, swa_sinks=swa_sinks, swa_w_o=swa_w_o, swa_b_o=swa_b_o, w_ff1=w_ff1, w_ff2=w_ff2,
             g_final=g_final)
    M = dict(w_ada=m_w_ada, b_ada=m_b_ada, g_mix=m_g_mix, g_mlp=m_g_mlp, mla_w_dq=m_mla_w_dq, mla_g_q=m_mla_g_q,
             mla_w_uq=m_mla_w_uq, mla_w_dkv=m_mla_w_dkv, mla_g_kv=m_mla_g_kv, mla_w_ukv=m_mla_w_ukv, mla_w_o=m_mla_w_o,
             swa_w_qkv=m_swa_w_qkv, swa_b_qkv=m_swa_b_qkv, swa_sinks=m_swa_sinks, swa_w_o=m_swa_w_o, swa_b_o=m_swa_b_o,
             w_ff1=m_w_ff1, w_ff2=m_w_ff2, g_final=m_g_final)
    V = dict(w_ada=v_w_ada, b_ada=v_b_ada, g_mix=v_g_mix, g_mlp=v_g_mlp, mla_w_dq=v_mla_w_dq, mla_g_q=v_mla_g_q,
             mla_w_uq=v_mla_w_uq, mla_w_dkv=v_mla_w_dkv, mla_g_kv=v_mla_g_kv, mla_w_ukv=v_mla_w_ukv, mla_w_o=v_mla_w_o,
             swa_w_qkv=v_swa_w_qkv, swa_b_qkv=v_swa_b_qkv, swa_sinks=v_swa_sinks, swa_w_o=v_swa_w_o, swa_b_o=v_swa_b_o,
             w_ff1=v_w_ff1, w_ff2=v_w_ff2, g_final=v_g_final)
    order = list(W)
    names = list(SHARDED)
    core = lax.axis_index("c")
    chip = 2 * lax.axis_index("x") + lax.axis_index("y")
    dev = 2 * chip + core
    core_arr = core.astype(jnp.int32).reshape(1)
    chip_arr = chip.astype(jnp.int32).reshape(1)

    def whole(n, g, own):
        g = lax.dynamic_update_slice(g, own[None], (chip, 0, 0))
        if n in ("w_ff1", "w_ff2"):
            return g
        if n in COL_SPLIT:
            return g.transpose(1, 0, 2).reshape(g.shape[1], N_CHIPS * g.shape[2])
        return g.reshape(N_CHIPS * g.shape[1], g.shape[2])

    early = [n for n in names if n.startswith("mla_")]
    local = {n: W[n].astype(MXU_DTYPE).reshape(_view2d(n)) for n in early}
    wts = {n: whole(n, g, local[n]) for n, g in zip(early, _weight_gather([local[n] for n in early]))}

    nbq, nbo = BIASES["swa_b_qkv"] // N_CHIPS, BIASES["swa_b_o"] // N_CHIPS
    first = jnp.concatenate([c.reshape(-1), swa_b_qkv.reshape(-1), swa_b_o.reshape(-1),
                             jnp.zeros((16 * LANES - D - nbq - nbo,), jnp.float32)]).reshape(16, LANES)
    first_all = _all_gather(first).reshape(N_DEV, 16 * LANES)
    c_all = first_all[:, :D]
    south = first_all[0::2]
    wts["swa_b_qkv"] = south[:, D:D + nbq].reshape(1, N_CHIPS * nbq)
    wts["swa_b_o"] = south[:, D + nbq:D + nbq + nbo].reshape(1, N_CHIPS * nbo)
    cond_all, part = _ada_part(c_all, w_ada)
    ncol = w_ada.shape[2]
    part_all = _all_gather(part.reshape(-1, LANES)).reshape(N_DEV, DEPTH, N_DEV, ncol)
    mine = _one_hot_pick(part_all[0::2], dev, axis=2)
    mod = mine.transpose(1, 0, 2).reshape(DEPTH, N_CHIPS * ncol) + b_ada
    vecs = jnp.concatenate([mod.reshape(DEPTH, 6, D), g_mix[:, None, :], g_mlp[:, None, :]], axis=1)

    late = [("w_ff1", 0), ("w_ff2", 0), ("swa_w_qkv", None), ("swa_w_o", None), ("w_ff1", 1), ("w_ff2", 1)]
    late_local = [(W[n][0] if l is None else W[n][l]).astype(MXU_DTYPE) for n, l in late]
    send_sems, recv_sems, passed, lands, token = _late_gather_start(late_local, [vecs] + [wts[n] for n in early])

    def late_weights(after):
        got = _late_gather_wait(send_sems, recv_sems, passed, lands, after)
        out = {"w_ff1": [None] * DEPTH, "w_ff2": [None] * DEPTH}
        for (n, l), g, own in zip(late, got, late_local):
            if l is None:
                out[n] = whole(n, g, own)
            else:
                out[n][l] = whole(n, g, own)
        return out

    late_names = [n for n in names if n not in early]
    reduce_state = {}

    def on_late_grads(late_grads):
        s_sems, r_sems, passed_g, zones, tok = _pair_in_start([late_grads[n] for n in late_names])
        reduce_state.update(pair=(s_sems, r_sems, passed_g, zones))
        return tok

    def on_late_landed(after):
        gl, got = _pair_in_wait(*reduce_state["pair"], after)
        sums = [_pair_sum(g, s, core_arr, "pair_sum_" + n) for n, g, s in zip(late_names, gl, got)]
        s_sems, r_sems, parts, zones, tok = _exchange_start([s16 for _, s16 in sums], "grad_exchange_start")
        reduce_state.update(sums=sums, split=(s_sems, r_sems, parts, zones))
        return tok

    grad_x, grads, small = _sequence_step(
        x[0], loss_target[0], positions[0], vecs, mla_g_q + token[0, 0], mla_g_kv, swa_sinks, g_final, wts,
        late_weights, on_late_grads, on_late_landed)

    small["b_ada"] = small.pop("dmod")
    small_all = _all_gather(_pack_small(small))
    pk = lambda src: _pack_small({n: src[n] for n in SMALL if n != "loss" and n not in BIASES})
    off, n = _small_slots()["b_ada"]
    dmod_all = small_all.reshape(N_DEV, -1)[:, off:off + n].reshape(N_DEV, DEPTH, N_CHIPS, ncol)
    dm = _one_hot_pick(dmod_all, chip, axis=2).transpose(1, 0, 2)

    gl = [grads[n] for n in early]
    got = _grad_pair_in(gl)
    sums = [_pair_sum(g, s, core_arr, "pair_sum_" + n) for n, g, s in zip(early, gl, got)]
    e_sems, e_rems, e_parts, e_zones, e_tok = _exchange_start([s16 for _, s16 in sums], "mla_exchange_start")

    def finish(tensor_names, sums, others, behind):
        halves = [_chip_sum(s32, o, chip_arr, "chip_sum_" + n, behind) for n, (s32, _), o in zip(tensor_names, sums, others)]
        return {n: _adamw_halves(W[n], mine_h, got_h, M[n], V[n], core_arr, "adamw_" + n)
                for n, mine_h, got_h in zip(tensor_names, halves, _grad_pair_out(halves))}

    late_others = _exchange_wait(*reduce_state["split"], grad_x, "grad_exchange_wait")
    res = finish(late_names, reduce_state["sums"], late_others, e_tok)
    res["w_ada"] = _ada_grad_adamw(cond_all.T, dm, w_ada, m_w_ada, v_w_ada)
    small_res = _small_adamw(small_all, pk(W), pk(M), pk(V))
    early_others = _exchange_wait(e_sems, e_rems, e_parts, e_zones, res["w_ff2"][1], "mla_exchange_wait")
    res.update(finish(early, sums, early_others, None))

    for n, width in BIASES.items():
        g = _one_hot_pick(small_res[n][0].reshape(N_CHIPS, width // N_CHIPS), chip, axis=0).reshape(1, -1)
        res[n] = [g] + _adamw(W[n], g, M[n], V[n], "adamw_" + n)
    for name in order:
        if name not in res:
            res[name] = small_res[name]
    outs = [small_res["loss"][0], grad_x[None]]
    for k in range(4):
        outs += [res[name][k] for name in order]
    return tuple(outs)
```

```python
import functools
import math

import jax
import jax.numpy as jnp
import numpy as np
from jax import lax
from jax.experimental import pallas as pl
from jax.experimental.pallas import tpu as pltpu

D = 1024
DEPTH = 2
MLA_HEADS = 8
QK_NOPE = 128
QK_ROPE = 64
V_DIM = 128
Q_LORA = 384
KV_LORA = 256
ROPE_THETA = 10000.0
SWA_HEADS = 16
SWA_KV_HEADS = 4
SWA_HEAD_DIM = 64
SWA_GROUP = SWA_HEADS // SWA_KV_HEADS
WINDOW = 128
D_FF = 4 * D
EPS = 1e-6
ADAM_LR = 0.001
ADAM_B1 = 0.9
ADAM_B2 = 0.999
ADAM_EPS = 1e-08
ADAM_WD = 0.01
ADAM_STEP = 10

N_CHIPS = 4
N_DEV = 8
LANES = 128
QK_EXT = 256
MLA_SCALE = (QK_NOPE + QK_ROPE) ** -0.5
LOG2E = math.log2(math.e)
LN2 = math.log(2.0)
MLA_QSCALE = MLA_SCALE * LOG2E
ATTN_BLOCK = 2048
ATTN_SUB = 512
MLP_FWD_TILE = (1024, 1024)
MLP_BWD_TILE = (512, 1024)
DW_TOKENS = 4096
SWA_SCALE = SWA_HEAD_DIM ** -0.5
NEG = -1e30
MXU_DTYPE = jnp.bfloat16
VMEM_LIMIT = 56 * 1024 * 1024

R_SH1, R_SC1, R_GT1, R_SH2, R_SC2, R_GT2, R_GMIX, R_GMLP = range(8)
R_BO = 6


def _tile(n, pref):
    if n <= pref:
        return n
    for t in range(pref, 7, -1):
        if n % t == 0 and t % 8 == 0:
            return t
    return n


def _dot(a, b):
    return jnp.dot(a, b, preferred_element_type=jnp.float32)


def _dot_nt(a, b):
    return lax.dot_general(a, b, (((1,), (1,)), ((), ())), preferred_element_type=jnp.float32)


def _dot_tn(a, b):
    return lax.dot_general(a, b, (((0,), (0,)), ((), ())), preferred_element_type=jnp.float32)


def _rms(x):
    r = lax.rsqrt(jnp.mean(x * x, axis=-1, keepdims=True) + EPS)
    return x * r, r


def _rms_bwd(dxhat, xhat, r):
    return r * (dxhat - xhat * jnp.mean(dxhat * xhat, axis=-1, keepdims=True))


def _rowsum(v):
    return jnp.sum(v, axis=0, keepdims=True)


def _params(*sem):
    return pltpu.CompilerParams(dimension_semantics=sem, vmem_limit_bytes=VMEM_LIMIT)


def _full(shape):
    nd = len(shape)
    return pl.BlockSpec(shape, lambda *_: (0,) * nd)


def _rows(tm, cols):
    return pl.BlockSpec((tm, cols), lambda i, *_: (i, 0))


def _modulate_bwd(dh, x, vec_ref, r_g, r_sc, r_sh, ps_ref, dres):
    xhat, r = _rms(x)
    g = vec_ref[r_g:r_g + 1, :]
    n = xhat * g
    ps_ref[r_sh:r_sh + 1, :] += _rowsum(dh)
    ps_ref[r_sc:r_sc + 1, :] += _rowsum(dh * n)
    dn = dh * (1.0 + vec_ref[r_sc:r_sc + 1, :])
    ps_ref[r_g:r_g + 1, :] += _rowsum(dn * xhat)
    return dres + _rms_bwd(dn * g, xhat, r)


def _mla_pre(x, vec, wcat, g_q, g_kv, wuq, wukv, cs):
    T = x.shape[0]
    tm = _tile(T, 512)
    H = MLA_HEADS

    def body(x_ref, vec_ref, wcat_ref, gq_ref, gkv_ref, wuq_ref, wukv_ref, cs_ref, h_ref, z_ref, q_ref, k_ref, v_ref):
        xhat, _ = _rms(x_ref[...])
        h = xhat * vec_ref[R_GMIX:R_GMIX + 1, :] * (1.0 + vec_ref[R_SC1:R_SC1 + 1, :]) + vec_ref[R_SH1:R_SH1 + 1, :]
        hb = h.astype(MXU_DTYPE)
        h_ref[...] = hb
        z = _dot(hb, wcat_ref[...])
        z_ref[...] = z
        cq = (_rms(z[:, :Q_LORA])[0] * gq_ref[...]).astype(MXU_DTYPE)
        ckv = (_rms(z[:, Q_LORA:Q_LORA + KV_LORA])[0] * gkv_ref[...]).astype(MXU_DTYPE)
        cs_t = cs_ref[...]
        t = z[:, Q_LORA + KV_LORA:] * cs_t
        k_rope = (t + pltpu.roll(t, QK_ROPE, axis=1)).astype(MXU_DTYPE)
        low = lax.broadcasted_iota(jnp.int32, (1, LANES), 1) < QK_ROPE
        for hd in range(H):
            qf = _dot(cq, wuq_ref[hd])
            tq = qf[:, QK_NOPE:] * cs_t
            tq = tq + pltpu.roll(tq, QK_ROPE, axis=1)
            q_ref[hd, :, :QK_NOPE] = (qf[:, :QK_NOPE] * MLA_QSCALE).astype(MXU_DTYPE)
            q_ref[hd, :, QK_NOPE:] = jnp.where(low, tq * MLA_QSCALE, 0.0).astype(MXU_DTYPE)
            kvf = _dot(ckv, wukv_ref[hd])
            k_ref[hd, :, :QK_NOPE] = kvf[:, :QK_NOPE].astype(MXU_DTYPE)
            k_ref[hd, :, QK_NOPE:] = k_rope
            v_ref[hd] = kvf[:, QK_NOPE:].astype(MXU_DTYPE)

    zc = wcat.shape[1]
    return pl.pallas_call(
        body, name="mla_pre", grid=(T // tm,),
        in_specs=[_rows(tm, D), _full((8, D)), _full(wcat.shape), _full(g_q.shape), _full(g_kv.shape),
                  _full(wuq.shape), _full(wukv.shape), _rows(tm, LANES)],
        out_specs=[_rows(tm, D), _rows(tm, zc),
                   pl.BlockSpec((H, tm, QK_EXT), lambda i: (0, i, 0)),
                   pl.BlockSpec((H, tm, QK_EXT), lambda i: (0, i, 0)),
                   pl.BlockSpec((H, tm, V_DIM), lambda i: (0, i, 0))],
        out_shape=[jax.ShapeDtypeStruct((T, D), MXU_DTYPE), jax.ShapeDtypeStruct((T, zc), jnp.float32),
                   jax.ShapeDtypeStruct((H, T, QK_EXT), MXU_DTYPE), jax.ShapeDtypeStruct((H, T, QK_EXT), MXU_DTYPE),
                   jax.ShapeDtypeStruct((H, T, V_DIM), MXU_DTYPE)],
        compiler_params=_params("parallel"),
    )(x, vec, wcat, g_q, g_kv, wuq, wukv, cs)


def _mla_attn_fwd(q, k, v):
    H, T, _ = q.shape
    tb = _tile(T, ATTN_BLOCK)
    sub = min(ATTN_SUB, tb)
    ns, nb = tb // sub, T // tb
    pairs = [(i, j) for i in range(nb) for j in range(i + 1)]
    qi_tab = jnp.asarray([i for i, _ in pairs], jnp.int32)
    kj_tab = jnp.asarray([j for _, j in pairs], jnp.int32)

    def body(qi_ref, kj_ref, q_ref, k_ref, v_ref, o_ref, lse_ref, m_sc, l_sc, acc_sc):
        qi, kj = qi_ref[pl.program_id(1)], kj_ref[pl.program_id(1)]

        @pl.when(kj == 0)
        def _():
            m_sc[...] = jnp.full_like(m_sc, NEG)
            l_sc[...] = jnp.zeros_like(l_sc)
            acc_sc[...] = jnp.zeros_like(acc_sc)

        def update(r, kk, masked):
            rows, keys = pl.ds(r * sub, sub), pl.ds(kk * sub, sub)
            s = _dot_nt(q_ref[0, rows, :], k_ref[0, keys, :])
            if masked:
                row = lax.broadcasted_iota(jnp.int32, (sub, sub), 0)
                col = lax.broadcasted_iota(jnp.int32, (sub, sub), 1)
                s = jnp.where(col <= row, s, NEG)
            m_prev = m_sc[rows, :]
            m_new = jnp.maximum(m_prev, jnp.max(s, axis=1, keepdims=True))
            alpha = jnp.exp2(m_prev - m_new)
            p = jnp.exp2(s - jnp.tile(m_new, (1, sub // LANES)))
            l_sc[rows, :] = alpha * l_sc[rows, :] + jnp.sum(p, axis=1, keepdims=True)
            acc_sc[rows, :] = alpha * acc_sc[rows, :] + _dot(p.astype(MXU_DTYPE), v_ref[0, keys, :])
            m_sc[rows, :] = m_new

        @pl.when(kj < qi)
        def _():
            for kk in range(ns):
                for r in range(ns):
                    update(r, kk, False)

        @pl.when(kj == qi)
        def _():
            for kk in range(ns):
                for r in range(kk, ns):
                    update(r, kk, r == kk)
            l = l_sc[...]
            o_ref[...] = (acc_sc[...] / l).astype(o_ref.dtype)
            lse = m_sc[...] + jnp.log2(l)
            pick = (lax.broadcasted_iota(jnp.int32, (8, LANES), 1) == 0).astype(jnp.float32)
            row = lax.dot_general(pick, lse, (((1,), (1,)), ((), ())), precision=lax.Precision.HIGHEST,
                                  preferred_element_type=jnp.float32)
            lse_ref[0] = row[0:1, :]

    q_idx = lambda h, p, qi_ref, kj_ref: (h, qi_ref[p], 0)
    kv_idx = lambda h, p, qi_ref, kj_ref: (h, kj_ref[p], 0)
    return pl.pallas_call(
        body, name="mla_attn_fwd",
        grid_spec=pltpu.PrefetchScalarGridSpec(
            num_scalar_prefetch=2, grid=(H, len(pairs)),
            in_specs=[pl.BlockSpec((1, tb, QK_EXT), q_idx), pl.BlockSpec((1, tb, QK_EXT), kv_idx),
                      pl.BlockSpec((1, tb, V_DIM), kv_idx)],
            out_specs=[pl.BlockSpec((tb, V_DIM), lambda h, p, qi_ref, kj_ref: (qi_ref[p], h)),
                       pl.BlockSpec((1, 1, tb), lambda h, p, qi_ref, kj_ref: (h, 0, qi_ref[p]))],
            scratch_shapes=[pltpu.VMEM((tb, LANES), jnp.float32), pltpu.VMEM((tb, LANES), jnp.float32),
                            pltpu.VMEM((tb, V_DIM), jnp.float32)]),
        out_shape=[jax.ShapeDtypeStruct((T, H * V_DIM), MXU_DTYPE), jax.ShapeDtypeStruct((H, 1, T), jnp.float32)],
        compiler_params=_params("parallel", "arbitrary"),
    )(qi_tab, kj_tab, q, k, v)


def _post_attn(o, x, w_o, bias, vec, o_transposed=False):
    T = x.shape[0]
    tm = _tile(T, 512)
    o_spec = pl.BlockSpec((D, tm), lambda i: (0, i)) if o_transposed else _rows(tm, D)

    def body(o_ref, x_ref, w_ref, b_ref, vec_ref, y_ref, xm_ref, h_ref):
        y = (_dot_tn if o_transposed else _dot)(o_ref[...], w_ref[...]) + b_ref[...]
        y_ref[...] = y.astype(y_ref.dtype)
        xm = x_ref[...] + vec_ref[R_GT1:R_GT1 + 1, :] * y
        xm_ref[...] = xm
        xhat, _ = _rms(xm)
        h = xhat * vec_ref[R_GMLP:R_GMLP + 1, :] * (1.0 + vec_ref[R_SC2:R_SC2 + 1, :]) + vec_ref[R_SH2:R_SH2 + 1, :]
        h_ref[...] = h.astype(h_ref.dtype)

    return pl.pallas_call(
        body, name="post_attn", grid=(T // tm,),
        in_specs=[o_spec, _rows(tm, D), _full((D, D)), _full((1, D)), _full((8, D))],
        out_specs=[_rows(tm, D), _rows(tm, D), _rows(tm, D)],
        out_shape=[jax.ShapeDtypeStruct((T, D), MXU_DTYPE), jax.ShapeDtypeStruct((T, D), jnp.float32),
                   jax.ShapeDtypeStruct((T, D), MXU_DTYPE)],
        compiler_params=_params("parallel"),
    )(o, x, w_o, bias, vec)


def _ff_specs(tf):
    per = D_FF // N_CHIPS // tf
    w1 = pl.BlockSpec((None, D, tf), lambda i, f: (f // per, 0, f % per))
    w2 = pl.BlockSpec((None, tf, D), lambda i, f: (f // per, f % per, 0))
    return w1, w2


def _mlp_fwd(h2, w1, w2, xm, vec):
    T = h2.shape[0]
    tm = _tile(T, MLP_FWD_TILE[0])
    tf = _tile(D_FF // N_CHIPS, MLP_FWD_TILE[1])
    nf = D_FF // tf
    w1_spec, w2_spec = _ff_specs(tf)

    def body(h_ref, w1_ref, w2_ref, xm_ref, vec_ref, a_ref, y_ref, xo_ref, acc):
        f = pl.program_id(1)

        @pl.when(f == 0)
        def _():
            acc[...] = jnp.zeros_like(acc)

        u = jnp.maximum(_dot(h_ref[...], w1_ref[...]), 0.0)
        ab = (u * u).astype(MXU_DTYPE)
        a_ref[...] = ab
        acc[...] += _dot(ab, w2_ref[...])

        @pl.when(f == nf - 1)
        def _():
            y = acc[...]
            y_ref[...] = y.astype(y_ref.dtype)
            xo_ref[...] = xm_ref[...] + vec_ref[R_GT2:R_GT2 + 1, :] * y

    return pl.pallas_call(
        body, name="mlp_fwd", grid=(T // tm, nf),
        in_specs=[_rows(tm, D), w1_spec, w2_spec, _rows(tm, D), _full((8, D))],
        out_specs=[pl.BlockSpec((tm, tf), lambda i, f: (i, f)), _rows(tm, D), _rows(tm, D)],
        out_shape=[jax.ShapeDtypeStruct((T, D_FF), MXU_DTYPE), jax.ShapeDtypeStruct((T, D), MXU_DTYPE),
                   jax.ShapeDtypeStruct((T, D), jnp.float32)],
        scratch_shapes=[pltpu.VMEM((tm, D), jnp.float32)],
        compiler_params=_params("parallel", "arbitrary"),
    )(h2, w1, w2, xm, vec)


def _swa_pre(x, vec, w_qkv, b_qkv):
    T = x.shape[0]
    tm = _tile(T, 512)
    nq = SWA_HEADS * SWA_HEAD_DIM
    nk = SWA_KV_HEADS * SWA_HEAD_DIM
    wq_t, w_kv = w_qkv[:, :nq].T, w_qkv[:, nq:]
    bq_col, b_kv = b_qkv[:, :nq].reshape(nq, 1), b_qkv[:, nq:]

    def body(x_ref, vec_ref, wq_ref, wkv_ref, bq_ref, bkv_ref, h_ref, qt_ref, k_ref, v_ref):
        xhat, _ = _rms(x_ref[...])
        h = xhat * vec_ref[R_GMIX:R_GMIX + 1, :] * (1.0 + vec_ref[R_SC1:R_SC1 + 1, :]) + vec_ref[R_SH1:R_SH1 + 1, :]
        hb = h.astype(MXU_DTYPE)
        h_ref[...] = hb
        qt_ref[...] = ((_dot_nt(wq_ref[...], hb) + bq_ref[...]) * SWA_SCALE).astype(MXU_DTYPE)
        kv = _dot(hb, wkv_ref[...]) + bkv_ref[...]
        k_ref[...] = kv[:, :nk].astype(MXU_DTYPE)
        v_ref[...] = kv[:, nk:].astype(MXU_DTYPE)

    return pl.pallas_call(
        body, name="swa_pre", grid=(T // tm,),
        in_specs=[_rows(tm, D), _full((8, D)), _full(wq_t.shape), _full(w_kv.shape), _full(bq_col.shape),
                  _full(b_kv.shape)],
        out_specs=[_rows(tm, D), pl.BlockSpec((nq, tm), lambda i: (0, i)), _rows(tm, nk), _rows(tm, nk)],
        out_shape=[jax.ShapeDtypeStruct((T, D), MXU_DTYPE), jax.ShapeDtypeStruct((nq, T), MXU_DTYPE),
                   jax.ShapeDtypeStruct((T, nk), MXU_DTYPE), jax.ShapeDtypeStruct((T, nk), MXU_DTYPE)],
        compiler_params=_params("parallel"),
    )(x, vec, wq_t, w_kv, bq_col, b_kv)


def _swa_bias():
    W = WINDOW
    slopes = 2.0 ** (-8.0 * np.arange(1, SWA_HEADS + 1) / SWA_HEADS)
    j, i = np.arange(W)[:, None], np.arange(W)[None, :]
    dist = np.where(j > i, W + i - j, i - j)
    bias = -slopes[:, None, None] * dist[None].astype(np.float64)
    bias = bias.reshape(SWA_KV_HEADS, SWA_GROUP, W, W).transpose(0, 2, 1, 3)
    return jnp.asarray(bias.reshape(SWA_KV_HEADS, W, SWA_GROUP * W), jnp.float32)


def _swa_fold_mask():
    W, G = WINDOW, SWA_GROUP
    j = lax.broadcasted_iota(jnp.int32, (W, G * W), 0)
    i = lax.broadcasted_iota(jnp.int32, (W, G * W), 1) & (W - 1)
    return j > i


def _swa_fold(band, up):
    return jnp.where(up, band[:WINDOW], band[WINDOW:])


def _swa_unfold(folded, up):
    zero = jnp.zeros_like(folded)
    return jnp.concatenate([jnp.where(up, folded, zero), jnp.where(up, zero, folded)], axis=0)


SWA_STEP_BLOCKS = 4


def _swa_blocks(T):
    nb = T // WINDOW
    return next(b for b in (SWA_STEP_BLOCKS, 2, 1) if nb % b == 0)


def _swa_views(b, qt_ref, kp_ref, kc_ref):
    W = WINDOW
    prev = kp_ref if b == 0 else kc_ref.at[pl.ds((b - 1) * W, W), :]
    return qt_ref.at[:, pl.ds(b * W, W)], prev, kc_ref.at[pl.ds(b * W, W), :]


def _swa_probs(has_prev, up, kh, qt_ref, kp_ref, kc_ref, bias_ref, sink_ref):
    W, Dh, G = WINDOW, SWA_HEAD_DIM, SWA_GROUP
    qt = jnp.concatenate([qt_ref[(kh * G + g) * Dh:(kh * G + g + 1) * Dh, :] for g in range(G)], axis=1)
    kb = jnp.concatenate([kp_ref[:, kh * Dh:(kh + 1) * Dh], kc_ref[:, kh * Dh:(kh + 1) * Dh]], axis=0)
    s = _swa_fold(_dot(kb, qt), up) + bias_ref[kh]
    if has_prev is not True:
        s = jnp.where(up & jnp.logical_not(has_prev), NEG, s)
    sink = sink_ref[kh]
    m = jnp.maximum(jnp.max(s, axis=0, keepdims=True), sink)
    p = jnp.exp(s - m)
    p_sink = jnp.exp(sink - m)
    inv = 1.0 / (jnp.sum(p, axis=0, keepdims=True) + p_sink)
    return qt, kb, p * inv, p_sink * inv


def _swa_attn_fwd(qt, k, v, bias, sink_rows):
    T = qt.shape[1]
    W, Dh, G, Hk = WINDOW, SWA_HEAD_DIM, SWA_GROUP, SWA_KV_HEADS
    nk = Hk * Dh

    nb = _swa_blocks(T)

    def body(qt_ref, kp_ref, kc_ref, vp_ref, vc_ref, bias_ref, sink_ref, ot_ref):
        n = pl.program_id(0)
        up = _swa_fold_mask()
        for b in range(nb):
            q_b, kp_b, kc_b = _swa_views(b, qt_ref, kp_ref, kc_ref)
            _, vp_b, vc_b = _swa_views(b, qt_ref, vp_ref, vc_ref)
            for kh in range(Hk):
                _, _, pn, _ = _swa_probs(True if b else n > 0, up, kh, q_b, kp_b, kc_b, bias_ref, sink_ref)
                vb = jnp.concatenate([vp_b[:, kh * Dh:(kh + 1) * Dh], vc_b[:, kh * Dh:(kh + 1) * Dh]], axis=0)
                ot = _dot_tn(vb, _swa_unfold(pn, up).astype(MXU_DTYPE))
                for g in range(G):
                    rows = pl.ds((kh * G + g) * Dh, Dh)
                    ot_ref[rows, pl.ds(b * W, W)] = ot[:, g * W:(g + 1) * W].astype(ot_ref.dtype)

    prev = lambda n: (jnp.maximum(n * nb - 1, 0), 0)
    cur = lambda n: (n, 0)
    col = lambda n: (0, n)
    return pl.pallas_call(
        body, name="swa_attn_fwd", grid=(T // (nb * W),),
        in_specs=[pl.BlockSpec((D, nb * W), col), pl.BlockSpec((W, nk), prev), pl.BlockSpec((nb * W, nk), cur),
                  pl.BlockSpec((W, nk), prev), pl.BlockSpec((nb * W, nk), cur), _full(bias.shape),
                  _full(sink_rows.shape)],
        out_specs=pl.BlockSpec((D, nb * W), col),
        out_shape=jax.ShapeDtypeStruct((D, T), MXU_DTYPE),
        compiler_params=_params("parallel"),
    )(qt, k, k, v, v, bias, sink_rows)


def _final_loss(x, tgt, g):
    T = x.shape[0]
    tm = _tile(T, 512)

    def body(x_ref, t_ref, g_ref, loss_ref, dx_ref, dg_ref):
        @pl.when(pl.program_id(0) == 0)
        def _():
            loss_ref[...] = jnp.zeros_like(loss_ref)
            dg_ref[...] = jnp.zeros_like(dg_ref)

        xhat, r = _rms(x_ref[...])
        gv = g_ref[...]
        e = xhat * gv - t_ref[...]
        loss_ref[...] += 0.5 * jnp.sum(jnp.mean(e * e, axis=-1, keepdims=True), axis=0, keepdims=True)
        dy = e * (1.0 / D)
        dg_ref[...] += _rowsum(dy * xhat)
        dx_ref[...] = _rms_bwd(dy * gv, xhat, r)

    return pl.pallas_call(
        body, name="final_loss", grid=(T // tm,),
        in_specs=[_rows(tm, D), _rows(tm, D), _full((1, D))],
        out_specs=[_full((8, LANES)), _rows(tm, D), _full((1, D))],
        out_shape=[jax.ShapeDtypeStruct((8, LANES), jnp.float32), jax.ShapeDtypeStruct((T, D), jnp.float32),
                   jax.ShapeDtypeStruct((1, D), jnp.float32)],
        compiler_params=_params("arbitrary"),
    )(x, tgt, g)


def _mlp_bwd(dxo, y2, a, w1, w2, xm, vec):
    T = dxo.shape[0]
    tm = _tile(T, MLP_BWD_TILE[0])
    tf = _tile(D_FF // N_CHIPS, MLP_BWD_TILE[1])
    nf = D_FF // tf
    w1_spec, w2_spec = _ff_specs(tf)

    def body(dxo_ref, y_ref, a_ref, w1_ref, w2_ref, xm_ref, vec_ref, du_ref, dy_ref, dxm_ref, ps_ref, dyb, acc):
        i, f = pl.program_id(0), pl.program_id(1)

        @pl.when((i == 0) & (f == 0))
        def _():
            ps_ref[...] = jnp.zeros_like(ps_ref)

        @pl.when(f == 0)
        def _():
            dxo_t = dxo_ref[...]
            d = (dxo_t * vec_ref[R_GT2:R_GT2 + 1, :]).astype(MXU_DTYPE)
            dyb[...] = d
            dy_ref[...] = d
            acc[...] = jnp.zeros_like(acc)
            ps_ref[R_GT2:R_GT2 + 1, :] += _rowsum(dxo_t * y_ref[...].astype(jnp.float32))

        da = _dot_nt(dyb[...], w2_ref[...])
        dub = (da * (2.0 * jnp.sqrt(a_ref[...].astype(jnp.float32)))).astype(MXU_DTYPE)
        du_ref[...] = dub
        acc[...] += _dot_nt(dub, w1_ref[...])

        @pl.when(f == nf - 1)
        def _():
            dxm_ref[...] = _modulate_bwd(acc[...], xm_ref[...], vec_ref, R_GMLP, R_SC2, R_SH2, ps_ref, dxo_ref[...])

    return pl.pallas_call(
        body, name="mlp_bwd", grid=(T // tm, nf),
        in_specs=[_rows(tm, D), _rows(tm, D), pl.BlockSpec((tm, tf), lambda i, f: (i, f)), w1_spec, w2_spec,
                  _rows(tm, D), _full((8, D))],
        out_specs=[pl.BlockSpec((tm, tf), lambda i, f: (i, f)), _rows(tm, D), _rows(tm, D), _full((8, D))],
        out_shape=[jax.ShapeDtypeStruct((T, D_FF), MXU_DTYPE), jax.ShapeDtypeStruct((T, D), MXU_DTYPE),
                   jax.ShapeDtypeStruct((T, D), jnp.float32), jax.ShapeDtypeStruct((8, D), jnp.float32)],
        scratch_shapes=[pltpu.VMEM((tm, D), MXU_DTYPE), pltpu.VMEM((tm, D), jnp.float32)],
        compiler_params=_params("arbitrary", "arbitrary"),
    )(dxo, y2, a, w1, w2, xm, vec)


def _mm_tn(a, g, name, split=None, layers=1, layer=0, into=None, a_transposed=False):
    K, T = a.shape if a_transposed else a.shape[::-1]
    N = g.shape[1]
    kq = K // N_CHIPS if split == "rows" else K
    nq = N // N_CHIPS if split == "cols" else N
    bk, bn, bt = _tile(kq, 1024), _tile(nq, 1024), _tile(T, DW_TOKENS)
    if nq % bn or bn % LANES:
        bn = nq
    kper, nper = kq // bk, nq // bn

    def body(*refs):
        a_ref, g_ref, o_ref = refs[0], refs[1], refs[-1]

        @pl.when(pl.program_id(2) == 0)
        def _():
            o_ref[...] = jnp.zeros_like(o_ref)

        o_ref[...] += (_dot if a_transposed else _dot_tn)(a_ref[...], g_ref[...])

    a_spec = pl.BlockSpec((bk, bt), lambda k, n, t: (k, t)) if a_transposed else pl.BlockSpec((bt, bk), lambda k, n, t: (t, k))
    in_specs = [a_spec, pl.BlockSpec((bt, bn), lambda k, n, t: (t, n))]
    args = [a, g]
    aliases = {}
    if split is None:
        out_spec = pl.BlockSpec((bk, bn), lambda k, n, t: (k, n))
        out_shape = jax.ShapeDtypeStruct((K, N), jnp.float32)
    else:
        if split == "cols":
            idx = lambda k, n, t: (n // nper, layer, k, n % nper)
        else:
            idx = lambda k, n, t: (k // kper, layer, k % kper, n)
        out_spec = pl.BlockSpec((None, None, bk, bn), idx)
        out_shape = jax.ShapeDtypeStruct((N_CHIPS, layers, kq, nq), jnp.float32)
        if into is not None:
            in_specs.append(pl.BlockSpec(memory_space=pl.ANY))
            args.append(into)
            aliases = {2: 0}
    return pl.pallas_call(
        body, name=name, grid=(K // bk, N // bn, T // bt), in_specs=in_specs, out_specs=out_spec, out_shape=out_shape,
        input_output_aliases=aliases, compiler_params=_params("parallel", "parallel", "arbitrary"),
    )(*args)


def _attn_out_bwd(dxm, y1, o, w_o, vec, with_delta):
    T = dxm.shape[0]
    tm = _tile(T, 512)
    H = MLA_HEADS

    def body(dxm_ref, y_ref, w_ref, vec_ref, *refs):
        o_ref = refs[0] if with_delta else None
        dy_ref, do_ref, ps_ref, *delta_ref = refs[1:] if with_delta else refs

        @pl.when(pl.program_id(0) == 0)
        def _():
            ps_ref[...] = jnp.zeros_like(ps_ref)

        dxm_t = dxm_ref[...]
        dy = dxm_t * vec_ref[R_GT1:R_GT1 + 1, :]
        ps_ref[R_GT1:R_GT1 + 1, :] += _rowsum(dxm_t * y_ref[...].astype(jnp.float32))
        ps_ref[R_BO:R_BO + 1, :] += _rowsum(dy)
        dyb = dy.astype(MXU_DTYPE)
        dy_ref[...] = dyb
        if not with_delta:
            do_ref[...] = _dot_nt(w_ref[...], dyb).astype(do_ref.dtype)
        else:
            do = _dot_nt(dyb, w_ref[...])
            do_ref[...] = do.astype(do_ref.dtype)
            of = o_ref[...].astype(jnp.float32)
            ones = jnp.ones((8, V_DIM), jnp.float32)
            for hd in range(H):
                sl = slice(hd * V_DIM, (hd + 1) * V_DIM)
                d = lax.dot_general(ones, do[:, sl] * of[:, sl], (((1,), (1,)), ((), ())),
                                    precision=lax.Precision.HIGHEST, preferred_element_type=jnp.float32)
                delta_ref[0][hd] = d[0:1, :]

    out_specs = [_rows(tm, D), _rows(tm, D), _full((8, D))]
    out_shape = [jax.ShapeDtypeStruct((T, D), MXU_DTYPE), jax.ShapeDtypeStruct((T, D), MXU_DTYPE),
                 jax.ShapeDtypeStruct((8, D), jnp.float32)]
    if not with_delta:
        out_specs[1] = pl.BlockSpec((D, tm), lambda i: (0, i))
        out_shape[1] = jax.ShapeDtypeStruct((D, T), MXU_DTYPE)
    if with_delta:
        out_specs.append(pl.BlockSpec((H, 1, tm), lambda i: (0, 0, i)))
        out_shape.append(jax.ShapeDtypeStruct((H, 1, T), jnp.float32))
    return pl.pallas_call(
        body, name="attn_out_bwd_mla" if with_delta else "attn_out_bwd_swa", grid=(T // tm,),
        in_specs=[_rows(tm, D), _rows(tm, D), _full((D, D)), _full((8, D))] + ([_rows(tm, D)] if with_delta else []),
        out_specs=out_specs, out_shape=out_shape,
        compiler_params=_params("arbitrary"),
    )(dxm, y1, w_o, vec, *([o] if with_delta else []))


def _mla_attn_bwd(q, k, v, do, lse, delta):
    H, T, _ = q.shape
    tb = _tile(T, ATTN_BLOCK)
    sub = min(ATTN_SUB, tb)
    ns, nb = tb // sub, T // tb

    pairs = [(j, i) for j in range(nb) for i in range(j, nb)]
    kj_tab = jnp.asarray([j for j, _ in pairs], jnp.int32)
    qi_tab = jnp.asarray([i for _, i in pairs], jnp.int32)

    def body(kj_ref, qi_ref, q_ref, k_ref, v_ref, do_ref, lse_ref, dl_ref, dq_ref, dk_ref, dv_ref, dk_acc, dv_acc):
        j, i = kj_ref[pl.program_id(1)], qi_ref[pl.program_id(1)]

        @pl.when((j == 0) & (i == 0))
        def _():
            dq_ref[...] = jnp.zeros_like(dq_ref)

        def update(kk, r, masked):
            keys, rows = pl.ds(kk * sub, sub), pl.ds(r * sub, sub)
            kb, qb, dob = k_ref[0, keys, :], q_ref[0, rows, :], do_ref[rows, :]
            st = _dot_nt(kb, qb)
            if masked:
                row = lax.broadcasted_iota(jnp.int32, (sub, sub), 0)
                col = lax.broadcasted_iota(jnp.int32, (sub, sub), 1)
                st = jnp.where(row <= col, st, NEG)
            pt = jnp.exp2(st - lse_ref[0, :, rows])
            dv_acc[keys, :] += _dot(pt.astype(MXU_DTYPE), dob)
            dpt = _dot_nt(v_ref[0, keys, :], dob)
            dst = (pt * (dpt - dl_ref[0, :, rows])).astype(MXU_DTYPE)
            dk_acc[keys, :] += _dot(dst, qb)
            q_rows = pl.ds(pl.multiple_of(i * tb + r * sub, sub), sub)
            dq_ref[0, q_rows, :] += _dot_tn(dst, kb)

        @pl.when(i == j)
        def _():
            dk_acc[...] = jnp.zeros_like(dk_acc)
            dv_acc[...] = jnp.zeros_like(dv_acc)
            for r in range(ns):
                for kk in range(r + 1):
                    update(kk, r, kk == r)

        @pl.when(i > j)
        def _():
            for r in range(ns):
                for kk in range(ns):
                    update(kk, r, False)

        @pl.when(i == nb - 1)
        def _():
            dk_ref[0] = (dk_acc[...] * LN2).astype(dk_ref.dtype)
            dv_ref[0] = dv_acc[...].astype(dv_ref.dtype)

    q_idx = lambda h, p, kj_ref, qi_ref: (h, qi_ref[p], 0)
    kv_idx = lambda h, p, kj_ref, qi_ref: (h, kj_ref[p], 0)
    stat_idx = lambda h, p, kj_ref, qi_ref: (h, 0, qi_ref[p])
    return pl.pallas_call(
        body, name="mla_attn_bwd",
        grid_spec=pltpu.PrefetchScalarGridSpec(
            num_scalar_prefetch=2, grid=(H, len(pairs)),
            in_specs=[pl.BlockSpec((1, tb, QK_EXT), q_idx), pl.BlockSpec((1, tb, QK_EXT), kv_idx),
                      pl.BlockSpec((1, tb, V_DIM), kv_idx),
                      pl.BlockSpec((tb, V_DIM), lambda h, p, kj_ref, qi_ref: (qi_ref[p], h)),
                      pl.BlockSpec((1, 1, tb), stat_idx), pl.BlockSpec((1, 1, tb), stat_idx)],
            out_specs=[pl.BlockSpec((1, T, QK_EXT), lambda h, p, kj_ref, qi_ref: (h, 0, 0)),
                       pl.BlockSpec((1, tb, QK_EXT), kv_idx), pl.BlockSpec((1, tb, V_DIM), kv_idx)],
            scratch_shapes=[pltpu.VMEM((tb, QK_EXT), jnp.float32), pltpu.VMEM((tb, V_DIM), jnp.float32)]),
        out_shape=[jax.ShapeDtypeStruct((H, T, QK_EXT), jnp.float32), jax.ShapeDtypeStruct((H, T, QK_EXT), MXU_DTYPE),
                   jax.ShapeDtypeStruct((H, T, V_DIM), MXU_DTYPE)],
        compiler_params=_params("parallel", "arbitrary"),
    )(kj_tab, qi_tab, q, k, v, do, lse, delta)


def _mla_pre_bwd(x, dxm, vec, hb, z, dq, dk, dv, cs, wcat, g_q, g_kv, wuq, wukv):
    T = x.shape[0]
    tm = _tile(T, 512)
    H = MLA_HEADS
    zc = wcat.shape[1]

    def body(x_ref, dxm_ref, vec_ref, h_ref, z_ref, dq_ref, dk_ref, dv_ref, cs_ref, wcat_ref, gq_ref, gkv_ref,
             wuq_ref, wukv_ref, dx_ref, ps_ref, dgq_ref, dgkv_ref, dwcat_ref, dwuq_ref, dwukv_ref):
        @pl.when(pl.program_id(0) == 0)
        def _():
            for ref in (ps_ref, dgq_ref, dgkv_ref, dwcat_ref, dwuq_ref, dwukv_ref):
                ref[...] = jnp.zeros_like(ref)

        z = z_ref[...]
        cs_t = cs_ref[...]
        cqhat, rq = _rms(z[:, :Q_LORA])
        ckhat, rk = _rms(z[:, Q_LORA:Q_LORA + KV_LORA])
        gq, gkv = gq_ref[...], gkv_ref[...]
        cq = (cqhat * gq).astype(MXU_DTYPE)
        ckv = (ckhat * gkv).astype(MXU_DTYPE)
        dcq = jnp.zeros((tm, Q_LORA), jnp.float32)
        dckv = jnp.zeros((tm, KV_LORA), jnp.float32)
        dkr = jnp.zeros((tm, LANES), jnp.float32)
        for hd in range(H):
            dqh = dq_ref[hd] * MLA_SCALE
            gqh = jnp.concatenate([dqh[:, :QK_NOPE], dqh[:, QK_NOPE:] * cs_t], axis=1).astype(MXU_DTYPE)
            dcq += _dot_nt(gqh, wuq_ref[hd])
            dwuq_ref[hd] += _dot_tn(cq, gqh)
            dkh = dk_ref[hd]
            gkvh = jnp.concatenate([dkh[:, :QK_NOPE], dv_ref[hd]], axis=1)
            dckv += _dot_nt(gkvh, wukv_ref[hd])
            dwukv_ref[hd] += _dot_tn(ckv, gkvh)
            dkr += dkh[:, QK_NOPE:].astype(jnp.float32)
        dgq_ref[...] += _rowsum(dcq * cqhat)
        dgkv_ref[...] += _rowsum(dckv * ckhat)
        dcq_pre = _rms_bwd(dcq * gq, cqhat, rq)
        dckv_pre = _rms_bwd(dckv * gkv, ckhat, rk)
        dkr2 = (dkr + pltpu.roll(dkr, QK_ROPE, axis=1)) * cs_t
        dz = jnp.concatenate([dcq_pre, dckv_pre, dkr2], axis=1).astype(MXU_DTYPE)
        dwcat_ref[...] += _dot_tn(h_ref[...], dz)
        dh = _dot_nt(dz, wcat_ref[...])
        dx_ref[...] = _modulate_bwd(dh, x_ref[...], vec_ref, R_GMIX, R_SC1, R_SH1, ps_ref, dxm_ref[...])

    hblk = lambda w: pl.BlockSpec((H, tm, w), lambda i: (0, i, 0))
    return pl.pallas_call(
        body, name="mla_pre_bwd", grid=(T // tm,),
        in_specs=[_rows(tm, D), _rows(tm, D), _full((8, D)), _rows(tm, D), _rows(tm, zc), hblk(QK_EXT), hblk(QK_EXT),
                  hblk(V_DIM), _rows(tm, LANES), _full(wcat.shape), _full(g_q.shape), _full(g_kv.shape),
                  _full(wuq.shape), _full(wukv.shape)],
        out_specs=[_rows(tm, D), _full((8, D)), _full(g_q.shape), _full(g_kv.shape), _full(wcat.shape),
                   _full(wuq.shape), _full(wukv.shape)],
        out_shape=[jax.ShapeDtypeStruct((T, D), jnp.float32), jax.ShapeDtypeStruct((8, D), jnp.float32),
                   jax.ShapeDtypeStruct(g_q.shape, jnp.float32), jax.ShapeDtypeStruct(g_kv.shape, jnp.float32),
                   jax.ShapeDtypeStruct(wcat.shape, jnp.float32), jax.ShapeDtypeStruct(wuq.shape, jnp.float32),
                   jax.ShapeDtypeStruct(wukv.shape, jnp.float32)],
        compiler_params=_params("arbitrary"),
    )(x, dxm, vec, hb, z, dq, dk, dv, cs, wcat, g_q, g_kv, wuq, wukv)


def _swa_attn_bwd(qt, k, v, dot_, bias, sink_rows):
    T = qt.shape[1]
    W, Dh, G, Hk = WINDOW, SWA_HEAD_DIM, SWA_GROUP, SWA_KV_HEADS
    nk = Hk * Dh
    nb = _swa_blocks(T)

    def body(qt_ref, kp_ref, kc_ref, vp_ref, vc_ref, dot_ref, bias_ref, sink_ref, dqt_ref, dk_ref, dv_ref, dsink_ref):
        n = pl.program_id(0)

        @pl.when(n == 0)
        def _():
            dk_ref[...] = jnp.zeros_like(dk_ref)
            dv_ref[...] = jnp.zeros_like(dv_ref)
            dsink_ref[...] = jnp.zeros_like(dsink_ref)

        def add_rows(first_row, dkb_part, dvb_part):
            rows = pl.ds(pl.multiple_of(first_row, W), W)
            dk_ref[rows, :] += dkb_part
            dv_ref[rows, :] += dvb_part

        up = _swa_fold_mask()
        for b in range(nb):
            q_b, kp_b, kc_b = _swa_views(b, qt_ref, kp_ref, kc_ref)
            do_b, vp_b, vc_b = _swa_views(b, dot_ref, vp_ref, vc_ref)
            dks, dvs = [], []
            for kh in range(Hk):
                qt, kb, pn, p_sink = _swa_probs(True if b else n > 0, up, kh, q_b, kp_b, kc_b, bias_ref, sink_ref)
                vb = jnp.concatenate([vp_b[:, kh * Dh:(kh + 1) * Dh], vc_b[:, kh * Dh:(kh + 1) * Dh]], axis=0)
                dot_h = jnp.concatenate([do_b[(kh * G + g) * Dh:(kh * G + g + 1) * Dh, :] for g in range(G)], axis=1)
                dp = _swa_fold(_dot(vb, dot_h), up)
                delta = jnp.sum(pn * dp, axis=0, keepdims=True)
                dsb = _swa_unfold(pn * (dp - delta), up).astype(MXU_DTYPE)
                dsink_ref[kh] += -p_sink * delta
                dqt = _dot_tn(kb, dsb) * SWA_SCALE
                for g in range(G):
                    dqt_ref[pl.ds((kh * G + g) * Dh, Dh), pl.ds(b * W, W)] = dqt[:, g * W:(g + 1) * W]
                dks.append(_dot_nt(dsb, qt))
                dvs.append(_dot_nt(_swa_unfold(pn, up).astype(MXU_DTYPE), dot_h))
            dkb = jnp.concatenate(dks, axis=1)
            dvb = jnp.concatenate(dvs, axis=1)
            add_rows((n * nb + b) * W, dkb[W:], dvb[W:])
            if b:
                add_rows((n * nb + b - 1) * W, dkb[:W], dvb[:W])
            else:
                @pl.when(n > 0)
                def _():
                    add_rows((n * nb - 1) * W, dkb[:W], dvb[:W])

    prev = lambda n: (jnp.maximum(n * nb - 1, 0), 0)
    cur = lambda n: (n, 0)
    col = lambda n: (0, n)
    return pl.pallas_call(
        body, name="swa_attn_bwd", grid=(T // (nb * W),),
        in_specs=[pl.BlockSpec((D, nb * W), col), pl.BlockSpec((W, nk), prev), pl.BlockSpec((nb * W, nk), cur),
                  pl.BlockSpec((W, nk), prev), pl.BlockSpec((nb * W, nk), cur), pl.BlockSpec((D, nb * W), col),
                  _full(bias.shape), _full(sink_rows.shape)],
        out_specs=[pl.BlockSpec((D, nb * W), col), _full((T, nk)), _full((T, nk)), _full(sink_rows.shape)],
        out_shape=[jax.ShapeDtypeStruct((D, T), jnp.float32), jax.ShapeDtypeStruct((T, nk), jnp.float32),
                   jax.ShapeDtypeStruct((T, nk), jnp.float32), jax.ShapeDtypeStruct(sink_rows.shape, jnp.float32)],
        compiler_params=_params("arbitrary"),
    )(qt, k, k, v, v, dot_, bias, sink_rows)


def _swa_pre_bwd(x, dxm, vec, dq_t, dk, dv, w_qkv):
    T = x.shape[0]
    tm = _tile(T, 512)
    nq = SWA_HEADS * SWA_HEAD_DIM
    nk = SWA_KV_HEADS * SWA_HEAD_DIM
    nqkv = nq + 2 * nk

    def body(x_ref, dxm_ref, vec_ref, dq_ref, dk_ref, dv_ref, w_ref, dx_ref, dqkv_ref, ps_ref, db_ref):
        @pl.when(pl.program_id(0) == 0)
        def _():
            ps_ref[...] = jnp.zeros_like(ps_ref)
            db_ref[...] = jnp.zeros_like(db_ref)

        dqkv = jnp.concatenate([dq_ref[...].T, dk_ref[...], dv_ref[...]], axis=1)
        db_ref[...] += _rowsum(dqkv)
        dqkv_b = dqkv.astype(MXU_DTYPE)
        dqkv_ref[...] = dqkv_b
        dh = _dot_nt(dqkv_b, w_ref[...])
        dx_ref[...] = _modulate_bwd(dh, x_ref[...], vec_ref, R_GMIX, R_SC1, R_SH1, ps_ref, dxm_ref[...])

    return pl.pallas_call(
        body, name="swa_pre_bwd", grid=(T // tm,),
        in_specs=[_rows(tm, D), _rows(tm, D), _full((8, D)), pl.BlockSpec((nq, tm), lambda i: (0, i)), _rows(tm, nk),
                  _rows(tm, nk), _full(w_qkv.shape)],
        out_specs=[_rows(tm, D), _rows(tm, nqkv), _full((8, D)), _full((1, nqkv))],
        out_shape=[jax.ShapeDtypeStruct((T, D), jnp.float32), jax.ShapeDtypeStruct((T, nqkv), MXU_DTYPE),
                   jax.ShapeDtypeStruct((8, D), jnp.float32), jax.ShapeDtypeStruct((1, nqkv), jnp.float32)],
        compiler_params=_params("arbitrary"),
    )(x, dxm, vec, dq_t, dk, dv, w_qkv)


def _rot_cols(w):
    half = QK_ROPE // 2
    return jnp.concatenate([-w[..., half:], w[..., :half]], axis=-1)


def _unrot_grad(d_rope, d_rot):
    half = QK_ROPE // 2
    return d_rope + jnp.concatenate([d_rot[..., half:], -d_rot[..., :half]], axis=-1)


def _rope_table(positions):
    half = QK_ROPE // 2
    inv_freq = ROPE_THETA ** (-jnp.arange(half, dtype=jnp.float32) / half)
    ang = positions.astype(jnp.float32)[:, None] * inv_freq
    cos, sin = jnp.cos(ang), jnp.sin(ang)
    return jnp.concatenate([cos, cos, sin, sin], axis=1)


def _sequence_step(x, tgt, positions, vecs, g_q, g_kv, sinks, g_final, wts, late_weights, on_late_grads, on_late_landed):
    H = MLA_HEADS
    cs = _rope_table(positions)
    w_dkv = wts["mla_w_dkv"]
    wcat = jnp.concatenate([wts["mla_w_dq"], w_dkv, _rot_cols(w_dkv[:, KV_LORA:])], axis=1)
    uq = wts["mla_w_uq"].reshape(Q_LORA, H, QK_NOPE + QK_ROPE)
    wuq = jnp.concatenate([uq, _rot_cols(uq[..., QK_NOPE:])], axis=-1).transpose(1, 0, 2)
    wukv = wts["mla_w_ukv"].reshape(KV_LORA, H, QK_NOPE + V_DIM).transpose(1, 0, 2)
    zero_bias = jnp.zeros((1, D), jnp.float32)
    bias = _swa_bias()
    sink_rows = jnp.broadcast_to(sinks.reshape(SWA_KV_HEADS, 1, SWA_GROUP, 1),
                                 (SWA_KV_HEADS, 1, SWA_GROUP, WINDOW)).reshape(SWA_KV_HEADS, 1, SWA_GROUP * WINDOW)

    h1a, z, q, k, v = _mla_pre(x, vecs[0], wcat, g_q, g_kv, wuq, wukv, cs)
    o_a, lse = _mla_attn_fwd(q, k, v)
    y1a, xm_a, h2a = _post_attn(o_a, x, wts["mla_w_o"], zero_bias, vecs[0])
    wts = {**wts, **late_weights(h2a)}
    a_a, y2a, x1 = _mlp_fwd(h2a, wts["w_ff1"][0], wts["w_ff2"][0], xm_a, vecs[0])

    h1b, qs_t, ks, vs = _swa_pre(x1, vecs[1], wts["swa_w_qkv"], wts["swa_b_qkv"])
    o_bt = _swa_attn_fwd(qs_t, ks, vs, bias, sink_rows)
    y1b, xm_b, h2b = _post_attn(o_bt, x1, wts["swa_w_o"], wts["swa_b_o"], vecs[1], o_transposed=True)
    a_b, y2b, x2 = _mlp_fwd(h2b, wts["w_ff1"][1], wts["w_ff2"][1], xm_b, vecs[1])

    loss8, dx2, dg_final = _final_loss(x2, tgt, g_final.reshape(1, D))

    du_b, dy2b, dxm_b, ps_mlp_b = _mlp_bwd(dx2, y2b, a_b, wts["w_ff1"][1], wts["w_ff2"][1], xm_b, vecs[1])
    g_ff2 = _mm_tn(a_b, dy2b, "dw_ff2_l1", "rows", DEPTH, 1)
    g_ff1 = _mm_tn(h2b, du_b, "dw_ff1_l1", "cols", DEPTH, 1)
    dy1b, do_bt, ps_out_b = _attn_out_bwd(dxm_b, y1b, None, wts["swa_w_o"], vecs[1], False)
    g_swa_o = _mm_tn(o_bt, dy1b, "dw_o_swa", a_transposed=True)
    dqs_t, dks, dvs, dsinks = _swa_attn_bwd(qs_t, ks, vs, do_bt, bias, sink_rows)
    dx1, dqkv, ps_pre_b, g_swa_bqkv = _swa_pre_bwd(x1, dxm_b, vecs[1], dqs_t, dks, dvs, wts["swa_w_qkv"])
    g_swa_qkv = _mm_tn(h1b, dqkv, "dw_qkv", "cols")

    du_a, dy2a, dxm_a, ps_mlp_a = _mlp_bwd(dx1, y2a, a_a, wts["w_ff1"][0], wts["w_ff2"][0], xm_a, vecs[0])
    g_ff2 = _mm_tn(a_a, dy2a, "dw_ff2_l0", "rows", DEPTH, 0, g_ff2)
    g_ff1 = _mm_tn(h2a, du_a, "dw_ff1_l0", "cols", DEPTH, 0, g_ff1)
    rows4 = lambda g: g.reshape(N_CHIPS, g.shape[0] // N_CHIPS, g.shape[1])
    token = on_late_grads({
        "swa_w_qkv": g_swa_qkv.reshape(N_CHIPS, D, -1), "swa_w_o": rows4(g_swa_o),
        "w_ff1": g_ff1.reshape(N_CHIPS, DEPTH * D, -1), "w_ff2": g_ff2.reshape(N_CHIPS, -1, D)})
    dy1a, do_a, ps_out_a, delta = _attn_out_bwd(dxm_a, y1a, o_a, wts["mla_w_o"], vecs[0] + token[0, 0], True)
    g_mla_o = _mm_tn(o_a, dy1a, "dw_o_mla")
    token = on_late_landed(g_mla_o)
    dq, dk, dv = _mla_attn_bwd(q, k, v, do_a, lse, delta + token[0, 0])
    dx0, ps_pre_a, dg_q, dg_kv, dwcat, dwuq, dwukv = _mla_pre_bwd(
        x, dxm_a, vecs[0], h1a, z, dq, dk, dv, cs, wcat, g_q, g_kv, wuq, wukv)

    c0, c1, c2 = Q_LORA, Q_LORA + KV_LORA, Q_LORA + KV_LORA + QK_ROPE
    g_dq = dwcat[:, :c0]
    g_dkv = jnp.concatenate([dwcat[:, c0:c1], _unrot_grad(dwcat[:, c1:c2], dwcat[:, c2:])], axis=1)
    e0 = QK_NOPE + QK_ROPE
    g_uq = jnp.concatenate([dwuq[..., :QK_NOPE], _unrot_grad(dwuq[..., QK_NOPE:e0], dwuq[..., e0:])], axis=-1)
    per = H // N_CHIPS
    g_uq = g_uq.reshape(N_CHIPS, per, Q_LORA, e0).transpose(0, 2, 1, 3).reshape(N_CHIPS, Q_LORA, per * e0)
    g_ukv = dwukv.reshape(N_CHIPS, per, KV_LORA, QK_NOPE + V_DIM).transpose(0, 2, 1, 3)
    g_ukv = g_ukv.reshape(N_CHIPS, KV_LORA, per * (QK_NOPE + V_DIM))

    def dmod(ps_pre, ps_out, ps_mlp):
        return jnp.concatenate([ps_pre[R_SH1:R_SC1 + 1], ps_out[R_GT1:R_GT1 + 1], ps_mlp[R_SH2:R_GT2 + 1]], axis=0)

    grads = {"mla_w_dq": rows4(g_dq), "mla_w_uq": g_uq, "mla_w_dkv": rows4(g_dkv), "mla_w_ukv": g_ukv,
             "mla_w_o": rows4(g_mla_o)}
    small = {
        "dmod": jnp.stack([dmod(ps_pre_a, ps_out_a, ps_mlp_a), dmod(ps_pre_b, ps_out_b, ps_mlp_b)]).reshape(DEPTH, 6 * D),
        "g_mix": jnp.stack([ps_pre_a[R_GMIX], ps_pre_b[R_GMIX]]),
        "g_mlp": jnp.stack([ps_mlp_a[R_GMLP], ps_mlp_b[R_GMLP]]),
        "mla_g_q": dg_q, "mla_g_kv": dg_kv, "swa_sinks": jnp.sum(dsinks.reshape(SWA_HEADS, WINDOW), axis=1).reshape(1, SWA_HEADS),
        "swa_b_qkv": g_swa_bqkv, "swa_b_o": ps_out_b[R_BO:R_BO + 1],
        "g_final": dg_final.reshape(D), "loss": loss8[0, 0],
    }
    return dx0, grads, small


SHARDED = {
    "mla_w_dq": (1, D // N_CHIPS, Q_LORA),
    "mla_w_uq": (1, Q_LORA, MLA_HEADS * (QK_NOPE + QK_ROPE) // N_CHIPS),
    "mla_w_dkv": (1, D // N_CHIPS, KV_LORA + QK_ROPE),
    "mla_w_ukv": (1, KV_LORA, MLA_HEADS * (QK_NOPE + V_DIM) // N_CHIPS),
    "mla_w_o": (1, MLA_HEADS * V_DIM // N_CHIPS, D),
    "swa_w_qkv": (1, D, (SWA_HEADS + 2 * SWA_KV_HEADS) * SWA_HEAD_DIM // N_CHIPS),
    "swa_w_o": (1, SWA_HEADS * SWA_HEAD_DIM // N_CHIPS, D),
    "w_ff1": (DEPTH, D, D_FF // N_CHIPS),
    "w_ff2": (DEPTH, D_FF // N_CHIPS, D),
}
COL_SPLIT = ("mla_w_uq", "mla_w_ukv", "swa_w_qkv")
BIASES = {"swa_b_qkv": (SWA_HEADS + 2 * SWA_KV_HEADS) * SWA_HEAD_DIM, "swa_b_o": D}


def _view2d(name):
    shape = SHARDED[name]
    return math.prod(shape[:-1]), shape[-1]


SMALL = {"b_ada": (DEPTH, 6 * D), "g_mix": (DEPTH, D), "g_mlp": (DEPTH, D), "mla_g_q": (1, Q_LORA),
         "mla_g_kv": (1, KV_LORA), "swa_sinks": (1, SWA_HEADS), "g_final": (D,), "loss": (),
         "swa_b_qkv": (1, BIASES["swa_b_qkv"]), "swa_b_o": (1, BIASES["swa_b_o"])}
SMALL_ROWS = 192
DMA_ROWS = 256


SLOT_ROWS = 8


def _small_slots():
    slots, off = {}, 0
    for name, shape in SMALL.items():
        n = max(math.prod(shape), 1)
        slots[name] = (off, n)
        off += -(-n // (SLOT_ROWS * LANES)) * SLOT_ROWS * LANES
    assert off <= SMALL_ROWS * LANES
    return slots


def _pack_small(vals):
    parts, end = [], 0
    for name, (off, n) in _small_slots().items():
        pad = -(-n // (SLOT_ROWS * LANES)) * SLOT_ROWS * LANES - n
        v = vals[name].astype(jnp.float32).reshape(-1) if name in vals else jnp.zeros((n,), jnp.float32)
        parts += [v, jnp.zeros((pad,), jnp.float32)]
        end = off + n + pad
    parts.append(jnp.zeros((SMALL_ROWS * LANES - end,), jnp.float32))
    return jnp.concatenate(parts).reshape(SMALL_ROWS, LANES)


def _from_slot(name, rows):
    n = max(math.prod(SMALL[name]), 1)
    return rows.reshape(-1)[:n].reshape(SMALL[name])


def _pieces(rows):
    return [(off, min(DMA_ROWS, rows - off)) for off in range(0, rows, DMA_ROWS)]


HBM = pl.BlockSpec(memory_space=pltpu.HBM)
MESH = pl.DeviceIdType.MESH


def _place():
    x, y, c = lax.axis_index("x"), lax.axis_index("y"), lax.axis_index("c")
    chips = [(1 - x, y), (x, 1 - y), (1 - x, 1 - y)]
    return x, y, c, chips


def _all_gather(block):
    m_per, n = block.shape

    def body(x_ref, out_ref, send_sems, recv_sems, local_sem):
        x, y, c, chips = _place()
        me, sibling = (x, y, c), (x, y, 1 - c)

        def rows(px, py, pc):
            return out_ref.at[pl.ds((4 * px + 2 * py + pc) * m_per, m_per), :]

        def copy(k, blk, to, src=None):
            return pltpu.make_async_remote_copy(
                src_ref=rows(*blk) if src is None else src, dst_ref=rows(*blk),
                send_sem=send_sems.at[k], recv_sem=recv_sems.at[k], device_id=to, device_id_type=MESH)

        mine = pltpu.make_async_copy(x_ref, rows(*me), local_sem)
        mine.start()
        first = [copy(0, me, sibling, src=x_ref)]
        first += [copy(1 + j, me, (*chip, c), src=x_ref) for j, chip in enumerate(chips)]
        for cp in first:
            cp.start()
        passed = [copy(4 + j, (*chip, c), sibling) for j, chip in enumerate(chips)]
        for j, chip in enumerate(chips):
            copy(1 + j, (*chip, c), me).wait_recv()
            passed[j].start()
        copy(0, sibling, me).wait_recv()
        for j, chip in enumerate(chips):
            copy(4 + j, (*chip, 1 - c), me).wait_recv()
        for cp in first + passed:
            cp.wait_send()
        mine.wait()

    out = pl.pallas_call(
        body, name="all_gather_small",
        out_shape=jax.ShapeDtypeStruct((N_DEV * m_per, n), block.dtype),
        in_specs=[pl.BlockSpec(memory_space=pltpu.VMEM)],
        out_specs=pl.BlockSpec(memory_space=pltpu.VMEM),
        scratch_shapes=[pltpu.SemaphoreType.DMA((7,)), pltpu.SemaphoreType.DMA((7,)), pltpu.SemaphoreType.DMA],
    )(block)
    return out.reshape(N_DEV, m_per, n)


def _weight_gather(shards):
    nt = len(shards)

    def body(*refs):
        w_refs, out_refs = refs[:nt], refs[nt:2 * nt]
        send_sems, recv_sems = refs[2 * nt:]
        x, y, c, chips = _place()
        sibling = (x, y, 1 - c)

        def slab(t, px, py, half):
            rh = shards[t].shape[0] // 2
            return out_refs[t].at[2 * px + py, pl.ds(half * rh, rh), :]

        def copy(t, k, src, dst, to):
            return pltpu.make_async_remote_copy(src_ref=src, dst_ref=dst, send_sem=send_sems.at[6 * t + k],
                                                recv_sem=recv_sems.at[6 * t + k], device_id=to, device_id_type=MESH)

        first = []
        for t in range(nt):
            rh = shards[t].shape[0] // 2
            first += [copy(t, j, w_refs[t].at[pl.ds(c * rh, rh), :], slab(t, x, y, c), (*chip, c))
                      for j, chip in enumerate(chips)]
        for cp in first:
            cp.start()
        passed = []
        for t in range(nt):
            for j, chip in enumerate(chips):
                copy(t, j, slab(t, *chip, c), slab(t, *chip, c), (*chip, c)).wait_recv()
                rh = shards[t].shape[0] // 2
                for off, n in _pieces(rh):
                    piece = out_refs[t].at[2 * chip[0] + chip[1], pl.ds(c * rh + off, n), :]
                    copy(t, 3 + j, piece, piece, sibling).start()
                passed.append(copy(t, 3 + j, slab(t, *chip, c), slab(t, *chip, c), sibling))
        for t in range(nt):
            for j, chip in enumerate(chips):
                copy(t, 3 + j, slab(t, *chip, 1 - c), slab(t, *chip, 1 - c), sibling).wait_recv()
        for cp in first + passed:
            cp.wait_send()

    return pl.pallas_call(
        body, name="weight_gather",
        out_shape=[jax.ShapeDtypeStruct((N_CHIPS,) + s.shape, s.dtype) for s in shards],
        in_specs=[HBM] * nt, out_specs=[HBM] * nt,
        scratch_shapes=[pltpu.SemaphoreType.DMA((6 * nt,)), pltpu.SemaphoreType.DMA((6 * nt,))],
    )(*shards)


SEM = pl.BlockSpec(memory_space=pltpu.SEMAPHORE)
ANY = pl.BlockSpec(memory_space=pl.ANY)
SPLIT_COPY = pltpu.SideEffectType.DATAFLOW_SIDE_EFFECTING


def _late_copies(w_refs, land_refs, send_sems, recv_sems):
    x, y, c, chips = _place()
    return [pltpu.make_async_remote_copy(
        src_ref=w_refs[t], dst_ref=land_refs[t].at[2 * x + y], send_sem=send_sems.at[3 * t + j],
        recv_sem=recv_sems.at[3 * t + j], device_id=(cx, cy, c), device_id_type=MESH)
        for t in range(len(w_refs)) for j, (cx, cy) in enumerate(chips)], chips


def _late_gather_start(shards, after):
    nt, na = len(shards), len(after)

    def body(*refs):
        w_refs, land_refs = refs[:nt], refs[nt:2 * nt]
        send_sems, recv_sems, token = refs[2 * nt + na], refs[2 * nt + na + 1], refs[-1]
        copies, _ = _late_copies(w_refs, land_refs, send_sems, recv_sems)
        for cp in copies:
            cp.start()
        token[...] = jnp.zeros_like(token)

    hbm = lambda a: pltpu.with_memory_space_constraint(a, pltpu.HBM)
    lands = [lax.empty((N_CHIPS,) + s.shape, s.dtype) for s in shards]
    outs = pl.pallas_call(
        body, name="late_gather_start",
        out_shape=(pltpu.SemaphoreType.DMA((3 * nt,)), pltpu.SemaphoreType.DMA((3 * nt,)),
                   *[pltpu.HBM(s.shape, s.dtype) for s in shards], *[pltpu.HBM(l.shape, l.dtype) for l in lands],
                   jax.ShapeDtypeStruct((8, LANES), jnp.float32)),
        in_specs=[HBM] * (2 * nt) + [ANY] * na,
        out_specs=(SEM, SEM, *([HBM] * (2 * nt)), pl.BlockSpec(memory_space=pltpu.VMEM)),
        input_output_aliases={i: 2 + i for i in range(2 * nt)},
        compiler_params=pltpu.CompilerParams(has_side_effects=SPLIT_COPY),
    )(*[hbm(s) for s in shards], *[hbm(l) for l in lands], *after)
    return outs[0], outs[1], list(outs[2:2 + nt]), list(outs[2 + nt:2 + 2 * nt]), outs[-1]


def _late_gather_wait(send_sems, recv_sems, shards, lands, after):
    nt = len(shards)

    def body(*refs):
        w_refs, land_refs = refs[:nt], refs[nt:2 * nt]
        s_sems, r_sems = refs[2 * nt], refs[2 * nt + 1]
        x, y, c, chips = _place()
        for t in range(nt):
            for j, (cx, cy) in enumerate(chips):
                cp = pltpu.make_async_remote_copy(
                    src_ref=w_refs[t], dst_ref=land_refs[t].at[2 * cx + cy], send_sem=s_sems.at[3 * t + j],
                    recv_sem=r_sems.at[3 * t + j], device_id=(cx, cy, c), device_id_type=MESH)
                cp.wait_send()
                cp.wait_recv()

    outs = pl.pallas_call(
        body, name="late_gather_wait",
        out_shape=(*[pltpu.HBM(s.shape, s.dtype) for s in shards], *[pltpu.HBM(l.shape, l.dtype) for l in lands]),
        in_specs=[HBM] * (2 * nt) + [SEM, SEM, ANY], out_specs=tuple([HBM] * (2 * nt)),
        input_output_aliases={i: i for i in range(2 * nt)},
        compiler_params=pltpu.CompilerParams(has_side_effects=SPLIT_COPY),
    )(*shards, *lands, send_sems, recv_sems, after)
    return list(outs[nt:])


def _grad_pair_in(grads):
    nt = len(grads)

    def body(*refs):
        g_refs, got_refs = refs[:nt], refs[nt:2 * nt]
        send_sems, recv_sems = refs[2 * nt:]
        x, y, c, _ = _place()
        sibling = (x, y, 1 - c)

        def copy(t, src, dst):
            return pltpu.make_async_remote_copy(src_ref=src, dst_ref=dst, send_sem=send_sems.at[t],
                                                recv_sem=recv_sems.at[t], device_id=sibling, device_id_type=MESH)

        for t in range(nt):
            rh = grads[t].shape[1] // 2
            for p in range(N_CHIPS):
                for off, n in _pieces(rh):
                    copy(t, g_refs[t].at[p, pl.ds((1 - c) * rh + off, n), :], got_refs[t].at[p, pl.ds(off, n), :]).start()
        for t in range(nt):
            rh = grads[t].shape[1] // 2
            copy(t, g_refs[t].at[:, pl.ds((1 - c) * rh, rh), :], got_refs[t]).wait()

    return pl.pallas_call(
        body, name="grad_pair_in",
        out_shape=[jax.ShapeDtypeStruct((N_CHIPS, g.shape[1] // 2, g.shape[2]), g.dtype) for g in grads],
        in_specs=[HBM] * nt, out_specs=[HBM] * nt,
        scratch_shapes=[pltpu.SemaphoreType.DMA((nt,)), pltpu.SemaphoreType.DMA((nt,))],
    )(*grads)


def _pair_in_start(grads):
    nt = len(grads)

    def body(*refs):
        g_refs, land_refs = refs[:nt], refs[nt:2 * nt]
        send_sems, recv_sems, token = refs[2 * nt], refs[2 * nt + 1], refs[-1]
        x, y, c, _ = _place()
        for t in range(nt):
            rh = grads[t].shape[1] // 2
            for p in range(N_CHIPS):
                for off, n in _pieces(rh):
                    pltpu.make_async_remote_copy(
                        src_ref=g_refs[t].at[p, pl.ds((1 - c) * rh + off, n), :], dst_ref=land_refs[t].at[p, pl.ds(off, n), :],
                        send_sem=send_sems.at[t], recv_sem=recv_sems.at[t], device_id=(x, y, 1 - c),
                        device_id_type=MESH).start()
        token[...] = jnp.zeros_like(token)

    hbm = lambda a: pltpu.with_memory_space_constraint(a, pltpu.HBM)
    lands = [lax.empty((N_CHIPS, g.shape[1] // 2, g.shape[2]), g.dtype) for g in grads]
    outs = pl.pallas_call(
        body, name="grad_pair_in_start",
        out_shape=(pltpu.SemaphoreType.DMA((nt,)), pltpu.SemaphoreType.DMA((nt,)),
                   *[pltpu.HBM(g.shape, g.dtype) for g in grads], *[pltpu.HBM(l.shape, l.dtype) for l in lands],
                   jax.ShapeDtypeStruct((8, LANES), jnp.float32)),
        in_specs=[HBM] * (2 * nt),
        out_specs=(SEM, SEM, *([HBM] * (2 * nt)), pl.BlockSpec(memory_space=pltpu.VMEM)),
        input_output_aliases={i: 2 + i for i in range(2 * nt)},
        compiler_params=pltpu.CompilerParams(has_side_effects=SPLIT_COPY),
    )(*[hbm(g) for g in grads], *[hbm(l) for l in lands])
    return outs[0], outs[1], list(outs[2:2 + nt]), list(outs[2 + nt:2 + 2 * nt]), outs[-1]


def _pair_in_wait(send_sems, recv_sems, grads, lands, after):
    nt = len(grads)

    def body(*refs):
        g_refs, land_refs = refs[:nt], refs[nt:2 * nt]
        s_sems, r_sems = refs[2 * nt], refs[2 * nt + 1]
        x, y, c, _ = _place()
        for t in range(nt):
            rh = grads[t].shape[1] // 2
            cp = pltpu.make_async_remote_copy(
                src_ref=g_refs[t].at[:, pl.ds((1 - c) * rh, rh), :], dst_ref=land_refs[t], send_sem=s_sems.at[t],
                recv_sem=r_sems.at[t], device_id=(x, y, 1 - c), device_id_type=MESH)
            cp.wait_send()
            cp.wait_recv()

    outs = pl.pallas_call(
        body, name="grad_pair_in_wait",
        out_shape=(*[pltpu.HBM(g.shape, g.dtype) for g in grads], *[pltpu.HBM(l.shape, l.dtype) for l in lands]),
        in_specs=[HBM] * (2 * nt) + [SEM, SEM, ANY], out_specs=tuple([HBM] * (2 * nt)),
        input_output_aliases={i: i for i in range(2 * nt)},
        compiler_params=pltpu.CompilerParams(has_side_effects=SPLIT_COPY),
    )(*grads, *lands, send_sems, recv_sems, after)
    return list(outs[:nt]), list(outs[nt:])


def _pair_sum(g, got, core, name):
    _, rows, cols = g.shape
    rh = rows // 2
    tr = _tile(rh, 512)
    nb = rh // tr

    def body(c_ref, g_ref, got_ref, s32_ref, s16_ref):
        s = g_ref[...] + got_ref[...]
        s32_ref[...] = s
        s16_ref[...] = s.astype(s16_ref.dtype)

    blk = pl.BlockSpec((None, tr, cols), lambda p, i, c_ref: (p, i, 0))
    return pl.pallas_call(
        body, name=name,
        grid_spec=pltpu.PrefetchScalarGridSpec(
            num_scalar_prefetch=1, grid=(N_CHIPS, nb),
            in_specs=[pl.BlockSpec((None, tr, cols), lambda p, i, c_ref: (p, c_ref[0] * nb + i, 0)), blk],
            out_specs=[blk, blk]),
        out_shape=[jax.ShapeDtypeStruct((N_CHIPS, rh, cols), jnp.float32),
                   jax.ShapeDtypeStruct((N_CHIPS, rh, cols), jnp.bfloat16)],
        compiler_params=_params("parallel", "parallel"),
    )(core, g, got)


def _exchange_start(parts, name):
    nt = len(parts)

    def body(*refs):
        a_refs, land_refs = refs[:nt], refs[nt:2 * nt]
        send_sems, recv_sems, token = refs[2 * nt], refs[2 * nt + 1], refs[-1]
        x, y, c, chips = _place()
        for t in range(nt):
            for j, (cx, cy) in enumerate(chips):
                pltpu.make_async_remote_copy(
                    src_ref=a_refs[t].at[2 * cx + cy], dst_ref=land_refs[t].at[j], send_sem=send_sems.at[3 * t + j],
                    recv_sem=recv_sems.at[3 * t + j], device_id=(cx, cy, c), device_id_type=MESH).start()
        token[...] = jnp.zeros_like(token)

    hbm = lambda a: pltpu.with_memory_space_constraint(a, pltpu.HBM)
    lands = [lax.empty((N_CHIPS - 1,) + a.shape[1:], a.dtype) for a in parts]
    outs = pl.pallas_call(
        body, name=name,
        out_shape=(pltpu.SemaphoreType.DMA((3 * nt,)), pltpu.SemaphoreType.DMA((3 * nt,)),
                   *[pltpu.HBM(a.shape, a.dtype) for a in parts], *[pltpu.HBM(l.shape, l.dtype) for l in lands],
                   jax.ShapeDtypeStruct((8, LANES), jnp.float32)),
        in_specs=[HBM] * (2 * nt),
        out_specs=(SEM, SEM, *([HBM] * (2 * nt)), pl.BlockSpec(memory_space=pltpu.VMEM)),
        input_output_aliases={i: 2 + i for i in range(2 * nt)},
        compiler_params=pltpu.CompilerParams(has_side_effects=SPLIT_COPY),
    )(*[hbm(a) for a in parts], *[hbm(l) for l in lands])
    return outs[0], outs[1], list(outs[2:2 + nt]), list(outs[2 + nt:2 + 2 * nt]), outs[-1]


def _exchange_wait(send_sems, recv_sems, parts, lands, after, name):
    nt = len(parts)

    def body(*refs):
        a_refs, land_refs = refs[:nt], refs[nt:2 * nt]
        s_sems, r_sems = refs[2 * nt], refs[2 * nt + 1]
        x, y, c, chips = _place()
        for t in range(nt):
            for j, (cx, cy) in enumerate(chips):
                cp = pltpu.make_async_remote_copy(
                    src_ref=a_refs[t].at[2 * cx + cy], dst_ref=land_refs[t].at[j], send_sem=s_sems.at[3 * t + j],
                    recv_sem=r_sems.at[3 * t + j], device_id=(cx, cy, c), device_id_type=MESH)
                cp.wait_send()
                cp.wait_recv()

    outs = pl.pallas_call(
        body, name=name,
        out_shape=(*[pltpu.HBM(a.shape, a.dtype) for a in parts], *[pltpu.HBM(l.shape, l.dtype) for l in lands]),
        in_specs=[HBM] * (2 * nt) + [SEM, SEM, ANY], out_specs=tuple([HBM] * (2 * nt)),
        input_output_aliases={i: i for i in range(2 * nt)},
        compiler_params=pltpu.CompilerParams(has_side_effects=SPLIT_COPY),
    )(*parts, *lands, send_sems, recv_sems, after)
    return list(outs[nt:])


def _chip_sum(s32, got, chip, name, behind=None):
    _, rh, cols = s32.shape
    tr = _tile(rh, 512)

    def body(p_ref, s_ref, got_ref, *refs):
        acc = s_ref[...]
        for j in range(N_CHIPS - 1):
            acc = acc + got_ref[j].astype(jnp.float32)
        refs[-1][...] = acc

    extra = [] if behind is None else [behind]
    return pl.pallas_call(
        body, name=name,
        grid_spec=pltpu.PrefetchScalarGridSpec(
            num_scalar_prefetch=1, grid=(rh // tr,),
            in_specs=[pl.BlockSpec((None, tr, cols), lambda i, p_ref: (p_ref[0], i, 0)),
                      pl.BlockSpec((N_CHIPS - 1, tr, cols), lambda i, p_ref: (0, i, 0))]
            + [pl.BlockSpec((8, LANES), lambda i, p_ref: (0, 0))] * len(extra),
            out_specs=pl.BlockSpec((tr, cols), lambda i, p_ref: (i, 0))),
        out_shape=jax.ShapeDtypeStruct((rh, cols), jnp.float32),
        compiler_params=_params("parallel"),
    )(chip, s32, got, *extra)


def _grad_pair_out(halves):
    nt = len(halves)

    def body(*refs):
        h_refs, got_refs = refs[:nt], refs[nt:2 * nt]
        send_sems, recv_sems = refs[2 * nt:]
        x, y, c, _ = _place()
        sibling = (x, y, 1 - c)

        def copy(t, src, dst):
            return pltpu.make_async_remote_copy(src_ref=src, dst_ref=dst, send_sem=send_sems.at[t],
                                                recv_sem=recv_sems.at[t], device_id=sibling, device_id_type=MESH)

        for t in range(nt):
            for off, n in _pieces(halves[t].shape[0]):
                copy(t, h_refs[t].at[pl.ds(off, n), :], got_refs[t].at[pl.ds(off, n), :]).start()
        for t in range(nt):
            copy(t, h_refs[t], got_refs[t]).wait()

    return pl.pallas_call(
        body, name="grad_pair_out",
        out_shape=[jax.ShapeDtypeStruct(h.shape, h.dtype) for h in halves],
        in_specs=[HBM] * nt, out_specs=[HBM] * nt,
        scratch_shapes=[pltpu.SemaphoreType.DMA((nt,)), pltpu.SemaphoreType.DMA((nt,))],
    )(*halves)


def _ada_part(c_all, w_ada):
    L, _, ncol = w_ada.shape
    tn = _tile(ncol, 512)

    def body(c_ref, w_ref, cond_ref, part_ref):
        cv = c_ref[...]
        cond = cv * jax.nn.sigmoid(cv)
        cond_ref[...] = cond
        part_ref[0] = jnp.dot(cond, w_ref[0], precision=lax.Precision.HIGHEST, preferred_element_type=jnp.float32)

    return pl.pallas_call(
        body, name="ada_part", grid=(L, ncol // tn),
        in_specs=[_full((N_DEV, D)), pl.BlockSpec((1, D, tn), lambda l, j: (l, 0, j))],
        out_specs=[_full((N_DEV, D)), pl.BlockSpec((1, N_DEV, tn), lambda l, j: (l, 0, j))],
        out_shape=[jax.ShapeDtypeStruct((N_DEV, D), jnp.float32), jax.ShapeDtypeStruct((L, N_DEV, ncol), jnp.float32)],
        compiler_params=_params("arbitrary", "arbitrary"),
    )(c_all, w_ada)


def _adamw_math(w, g, m, v):
    m = ADAM_B1 * m + (1.0 - ADAM_B1) * g
    v = ADAM_B2 * v + (1.0 - ADAM_B2) * jnp.square(g)
    m_hat = m / (1.0 - ADAM_B1 ** ADAM_STEP)
    v_hat = v / (1.0 - ADAM_B2 ** ADAM_STEP)
    delta = -ADAM_LR * (m_hat / (jnp.sqrt(v_hat) + ADAM_EPS) + ADAM_WD * w)
    return delta, m, v


def _adamw(w, g, m, v, name):
    shape = w.shape
    cols = shape[-1]
    rows = math.prod(shape[:-1])
    tr = _tile(rows, 512)
    two_d = lambda t: t.reshape(rows, cols)

    def body(w_ref, g_ref, m_ref, v_ref, d_ref, mo_ref, vo_ref):
        d_ref[...], mo_ref[...], vo_ref[...] = _adamw_math(w_ref[...], g_ref[...], m_ref[...], v_ref[...])

    out = jax.ShapeDtypeStruct((rows, cols), jnp.float32)
    outs = pl.pallas_call(
        body, name=name, grid=(rows // tr,), in_specs=[_rows(tr, cols)] * 4, out_specs=[_rows(tr, cols)] * 3,
        out_shape=[out, out, out], compiler_params=_params("parallel"),
    )(two_d(w), two_d(g), two_d(m), two_d(v))
    return [t.reshape(shape) for t in outs]


def _adamw_halves(w, mine, got, m, v, core, name):
    shape = w.shape
    cols = shape[-1]
    rows = math.prod(shape[:-1])
    rh = rows // 2
    tr = _tile(rh, 512)
    nbh = rh // tr
    two_d = lambda t: t.reshape(rows, cols)

    def body(c_ref, w_ref, a_ref, b_ref, m_ref, v_ref, g_ref, d_ref, mo_ref, vo_ref):
        g = jnp.where(pl.program_id(0) // nbh == c_ref[0], a_ref[...], b_ref[...])
        g_ref[...] = g
        d_ref[...], mo_ref[...], vo_ref[...] = _adamw_math(w_ref[...], g, m_ref[...], v_ref[...])

    row = pl.BlockSpec((tr, cols), lambda i, c_ref: (i, 0))

    def half(keep):
        return pl.BlockSpec((tr, cols), lambda i, c_ref: (jnp.where((i // nbh == c_ref[0]) == keep, i % nbh, 0), 0))

    out = jax.ShapeDtypeStruct((rows, cols), jnp.float32)
    outs = pl.pallas_call(
        body, name=name,
        grid_spec=pltpu.PrefetchScalarGridSpec(
            num_scalar_prefetch=1, grid=(rows // tr,),
            in_specs=[row, half(True), half(False), row, row], out_specs=[row] * 4),
        out_shape=[out] * 4, compiler_params=_params("arbitrary"),
    )(core, two_d(w), mine, got, two_d(m), two_d(v))
    return [t.reshape(shape) for t in outs]


def _ada_grad_adamw(cond_t, dm, w, m, v):
    L, _, ncol = w.shape
    tn = _tile(ncol, 512)

    def body(ct_ref, dm_ref, w_ref, m_ref, v_ref, g_ref, d_ref, mo_ref, vo_ref):
        g = ct_ref[:, 0:1] * dm_ref[0, 0:1, :]
        for b in range(1, N_DEV):
            g = g + ct_ref[:, b:b + 1] * dm_ref[0, b:b + 1, :]
        g_ref[0] = g
        d_ref[0], mo_ref[0], vo_ref[0] = _adamw_math(w_ref[0], g, m_ref[0], v_ref[0])

    wblk = pl.BlockSpec((1, D, tn), lambda l, j: (l, 0, j))
    out = jax.ShapeDtypeStruct(w.shape, jnp.float32)
    return pl.pallas_call(
        body, name="ada_grad_adamw", grid=(L, ncol // tn),
        in_specs=[_full((D, N_DEV)), pl.BlockSpec((1, N_DEV, tn), lambda l, j: (l, 0, j)), wblk, wblk, wblk],
        out_specs=[wblk] * 4, out_shape=[out] * 4, compiler_params=_params("parallel", "parallel"),
    )(cond_t, dm, w, m, v)


def _small_adamw(gathered, w, m, v):
    slots = _small_slots()
    rows = {name: (off // LANES, -(-n // LANES)) for name, (off, n) in slots.items()}
    kinds = {name: 1 if name == "loss" or name in BIASES else 4 for name in slots}

    def body(ga_ref, w_ref, m_ref, v_ref, *out_refs):
        g = ga_ref[0]
        for dev in range(1, N_DEV):
            g = g + ga_ref[dev]
        d, mo, vo = _adamw_math(w_ref[...], g, m_ref[...], v_ref[...])
        k = 0
        for name, (r0, nr) in rows.items():
            for src in (g, d, mo, vo)[:kinds[name]]:
                out_refs[k][...] = src[r0:r0 + nr, :]
                k += 1

    out_shape = [jax.ShapeDtypeStruct((rows[name][1], LANES), jnp.float32) for name in slots for _ in range(kinds[name])]
    flat = pl.pallas_call(
        body, name="small_adamw", out_shape=out_shape,
        in_specs=[pl.BlockSpec(memory_space=pltpu.VMEM)] * 4,
        out_specs=[pl.BlockSpec(memory_space=pltpu.VMEM)] * len(out_shape),
    )(gathered, w, m, v)
    out, k = {}, 0
    for name in slots:
        out[name] = [_from_slot(name, t) for t in flat[k:k + kinds[name]]]
        k += kinds[name]
    return out


def _one_hot_pick(arr, index, axis):
    n = arr.shape[axis]
    shape = [1] * arr.ndim
    shape[axis] = n
    hot = (jnp.arange(n) == index).astype(arr.dtype).reshape(shape)
    return jnp.sum(arr * hot, axis=axis)


def kernel(x, c, positions, w_ada, b_ada, g_mix, g_mlp, mla_w_dq, mla_g_q, mla_w_uq, mla_w_dkv, mla_g_kv, mla_w_ukv, mla_w_o, swa_w_qkv, swa_b_qkv, swa_sinks, swa_w_o, swa_b_o, w_ff1, w_ff2, g_final, loss_target, m_w_ada, m_b_ada, m_g_mix, m_g_mlp, m_mla_w_dq, m_mla_g_q, m_mla_w_uq, m_mla_w_dkv, m_mla_g_kv, m_mla_w_ukv, m_mla_w_o, m_swa_w_qkv, m_swa_b_qkv, m_swa_sinks, m_swa_w_o, m_swa_b_o, m_w_ff1, m_w_ff2, m_g_final, v_w_ada, v_b_ada, v_g_mix, v_g_mlp, v_mla_w_dq, v_mla_g_q, v_mla_w_uq, v_mla_w_dkv, v_mla_g_kv, v_mla_w_ukv, v_mla_w_o, v_swa_w_qkv, v_swa_b_qkv, v_swa_sinks, v_swa_w_o, v_swa_b_o, v_w_ff1, v_w_ff2, v_g_final):
    W = dict(w_ada=w_ada, b_ada=b_ada, g_mix=g_mix, g_mlp=g_mlp, mla_w_dq=mla_w_dq, mla_g_q=mla_g_q, mla_w_uq=mla_w_uq,
             mla_w_dkv=mla_w_dkv, mla_g_kv=mla_g_kv, mla_w_ukv=mla_w_ukv, mla_w_o=mla_w_o, swa_w_qkv=swa_w_qkv,
             swa_b_qkv=swa_b_qkv, swa_sinks=swa_sinks, swa_w_o=swa_w_o, swa_b_o=swa_b_o, w_ff1=w_ff1, w_ff2=w_ff2,
             g_final=g_final)
    M = dict(w_ada=m_w_ada, b_ada=m_b_ada, g_mix=m_g_mix, g_mlp=m_g_mlp, mla_w_dq=m_mla_w_dq, mla_g_q=m_mla_g_q,
             mla_w_uq=m_mla_w_uq, mla_w_dkv=m_mla_w_dkv, mla_g_kv=m_mla_g_kv, mla_w_ukv=m_mla_w_ukv, mla_w_o=m_mla_w_o,
             swa_w_qkv=m_swa_w_qkv, swa_b_qkv=m_swa_b_qkv, swa_sinks=m_swa_sinks, swa_w_o=m_swa_w_o, swa_b_o=m_swa_b_o,
             w_ff1=m_w_ff1, w_ff2=m_w_ff2, g_final=m_g_final)
    V = dict(w_ada=v_w_ada, b_ada=v_b_ada, g_mix=v_g_mix, g_mlp=v_g_mlp, mla_w_dq=v_mla_w_dq, mla_g_q=v_mla_g_q,
             mla_w_uq=v_mla_w_uq, mla_w_dkv=v_mla_w_dkv, mla_g_kv=v_mla_g_kv, mla_w_ukv=v_mla_w_ukv, mla_w_o=v_mla_w_o,
             swa_w_qkv=v_swa_w_qkv, swa_b_qkv=v_swa_b_qkv, swa_sinks=v_swa_sinks, swa_w_o=v_swa_w_o, swa_b_o=v_swa_b_o,
             w_ff1=v_w_ff1, w_ff2=v_w_ff2, g_final=v_g_final)
    order = list(W)
    names = list(SHARDED)
    core = lax.axis_index("c")
    chip = 2 * lax.axis_index("x") + lax.axis_index("y")
    dev = 2 * chip + core
    core_arr = core.astype(jnp.int32).reshape(1)
    chip_arr = chip.astype(jnp.int32).reshape(1)

    def whole(n, g, own):
        g = lax.dynamic_update_slice(g, own[None], (chip, 0, 0))
        if n in ("w_ff1", "w_ff2"):
            return g
        if n in COL_SPLIT:
            return g.transpose(1, 0, 2).reshape(g.shape[1], N_CHIPS * g.shape[2])
        return g.reshape(N_CHIPS * g.shape[1], g.shape[2])

    early = [n for n in names if n.startswith("mla_")]
    local = {n: W[n].astype(MXU_DTYPE).reshape(_view2d(n)) for n in early}
    wts = {n: whole(n, g, local[n]) for n, g in zip(early, _weight_gather([local[n] for n in early]))}

    nbq, nbo = BIASES["swa_b_qkv"] // N_CHIPS, BIASES["swa_b_o"] // N_CHIPS
    first = jnp.concatenate([c.reshape(-1), swa_b_qkv.reshape(-1), swa_b_o.reshape(-1),
                             jnp.zeros((16 * LANES - D - nbq - nbo,), jnp.float32)]).reshape(16, LANES)
    first_all = _all_gather(first).reshape(N_DEV, 16 * LANES)
    c_all = first_all[:, :D]
    south = first_all[0::2]
    wts["swa_b_qkv"] = south[:, D:D + nbq].reshape(1, N_CHIPS * nbq)
    wts["swa_b_o"] = south[:, D + nbq:D + nbq + nbo].reshape(1, N_CHIPS * nbo)
    cond_all, part = _ada_part(c_all, w_ada)
    ncol = w_ada.shape[2]
    part_all = _all_gather(part.reshape(-1, LANES)).reshape(N_DEV, DEPTH, N_DEV, ncol)
    mine = _one_hot_pick(part_all[0::2], dev, axis=2)
    mod = mine.transpose(1, 0, 2).reshape(DEPTH, N_CHIPS * ncol) + b_ada
    vecs = jnp.concatenate([mod.reshape(DEPTH, 6, D), g_mix[:, None, :], g_mlp[:, None, :]], axis=1)

    late = [("w_ff1", 0), ("w_ff2", 0), ("swa_w_qkv", None), ("swa_w_o", None), ("w_ff1", 1), ("w_ff2", 1)]
    late_local = [(W[n][0] if l is None else W[n][l]).astype(MXU_DTYPE) for n, l in late]
    send_sems, recv_sems, passed, lands, token = _late_gather_start(late_local, [vecs] + [wts[n] for n in early])

    def late_weights(after):
        got = _late_gather_wait(send_sems, recv_sems, passed, lands, after)
        out = {"w_ff1": [None] * DEPTH, "w_ff2": [None] * DEPTH}
        for (n, l), g, own in zip(late, got, late_local):
            if l is None:
                out[n] = whole(n, g, own)
            else:
                out[n][l] = whole(n, g, own)
        return out

    late_names = [n for n in names if n not in early]
    reduce_state = {}

    def on_late_grads(late_grads):
        s_sems, r_sems, passed_g, zones, tok = _pair_in_start([late_grads[n] for n in late_names])
        reduce_state.update(pair=(s_sems, r_sems, passed_g, zones))
        return tok

    def on_late_landed(after):
        gl, got = _pair_in_wait(*reduce_state["pair"], after)
        sums = [_pair_sum(g, s, core_arr, "pair_sum_" + n) for n, g, s in zip(late_names, gl, got)]
        s_sems, r_sems, parts, zones, tok = _exchange_start([s16 for _, s16 in sums], "grad_exchange_start")
        reduce_state.update(sums=sums, split=(s_sems, r_sems, parts, zones))
        return tok

    grad_x, grads, small = _sequence_step(
        x[0], loss_target[0], positions[0], vecs, mla_g_q + token[0, 0], mla_g_kv, swa_sinks, g_final, wts,
        late_weights, on_late_grads, on_late_landed)

    small["b_ada"] = small.pop("dmod")
    small_all = _all_gather(_pack_small(small))
    pk = lambda src: _pack_small({n: src[n] for n in SMALL if n != "loss" and n not in BIASES})
    off, n = _small_slots()["b_ada"]
    dmod_all = small_all.reshape(N_DEV, -1)[:, off:off + n].reshape(N_DEV, DEPTH, N_CHIPS, ncol)
    dm = _one_hot_pick(dmod_all, chip, axis=2).transpose(1, 0, 2)

    gl = [grads[n] for n in early]
    got = _grad_pair_in(gl)
    sums = [_pair_sum(g, s, core_arr, "pair_sum_" + n) for n, g, s in zip(early, gl, got)]
    e_sems, e_rems, e_parts, e_zones, e_tok = _exchange_start([s16 for _, s16 in sums], "mla_exchange_start")

    def finish(tensor_names, sums, others, behind):
        halves = [_chip_sum(s32, o, chip_arr, "chip_sum_" + n, behind) for n, (s32, _), o in zip(tensor_names, sums, others)]
        return {n: _adamw_halves(W[n], mine_h, got_h, M[n], V[n], core_arr, "adamw_" + n)
                for n, mine_h, got_h in zip(tensor_names, halves, _grad_pair_out(halves))}

    late_others = _exchange_wait(*reduce_state["split"], grad_x, "grad_exchange_wait")
    res = finish(late_names, reduce_state["sums"], late_others, e_tok)
    res["w_ada"] = _ada_grad_adamw(cond_all.T, dm, w_ada, m_w_ada, v_w_ada)
    small_res = _small_adamw(small_all, pk(W), pk(M), pk(V))
    early_others = _exchange_wait(e_sems, e_rems, e_parts, e_zones, res["w_ff2"][1], "mla_exchange_wait")
    res.update(finish(early, sums, early_others, None))

    for n, width in BIASES.items():
        g = _one_hot_pick(small_res[n][0].reshape(N_CHIPS, width // N_CHIPS), chip, axis=0).reshape(1, -1)
        res[n] = [g] + _adamw(W[n], g, M[n], V[n], "adamw_" + n)
    for name in order:
        if name not in res:
            res[name] = small_res[name]
    outs = [small_res["loss"][0], grad_x[None]]
    for k in range(4):
        outs += [res[name][k] for name in order]
    return tuple(outs)
```

```python
import functools
import math

import jax
import jax.numpy as jnp
import numpy as np
from jax import lax
from jax.experimental import pallas as pl
from jax.experimental.pallas import tpu as pltpu

D = 1024
DEPTH = 2
MLA_HEADS = 8
QK_NOPE = 128
QK_ROPE = 64
V_DIM = 128
Q_LORA = 384
KV_LORA = 256
ROPE_THETA = 10000.0
SWA_HEADS = 16
SWA_KV_HEADS = 4
SWA_HEAD_DIM = 64
SWA_GROUP = SWA_HEADS // SWA_KV_HEADS
WINDOW = 128
D_FF = 4 * D
EPS = 1e-6
ADAM_LR = 0.001
ADAM_B1 = 0.9
ADAM_B2 = 0.999
ADAM_EPS = 1e-08
ADAM_WD = 0.01
ADAM_STEP = 10

N_CHIPS = 4
N_DEV = 8
LANES = 128
QK_EXT = 256
MLA_SCALE = (QK_NOPE + QK_ROPE) ** -0.5
LOG2E = math.log2(math.e)
LN2 = math.log(2.0)
MLA_QSCALE = MLA_SCALE * LOG2E
ATTN_BLOCK = 2048
ATTN_SUB = 512
MLP_FWD_TILE = (1024, 1024)
MLP_BWD_TILE = (512, 1024)
DW_TOKENS = 4096
ROW_TILE = 1024
SWA_SCALE = SWA_HEAD_DIM ** -0.5
NEG = -1e30
MXU_DTYPE = jnp.bfloat16
VMEM_LIMIT = 56 * 1024 * 1024

R_SH1, R_SC1, R_GT1, R_SH2, R_SC2, R_GT2, R_GMIX, R_GMLP = range(8)
R_BO = 6


def _tile(n, pref):
    if n <= pref:
        return n
    for t in range(pref, 7, -1):
        if n % t == 0 and t % 8 == 0:
            return t
    return n


def _dot(a, b):
    return jnp.dot(a, b, preferred_element_type=jnp.float32)


def _dot_nt(a, b):
    return lax.dot_general(a, b, (((1,), (1,)), ((), ())), preferred_element_type=jnp.float32)


def _dot_tn(a, b):
    return lax.dot_general(a, b, (((0,), (0,)), ((), ())), preferred_element_type=jnp.float32)


def _rms(x):
    r = lax.rsqrt(jnp.mean(x * x, axis=-1, keepdims=True) + EPS)
    return x * r, r


def _rms_bwd(dxhat, xhat, r):
    return r * (dxhat - xhat * jnp.mean(dxhat * xhat, axis=-1, keepdims=True))


def _rowsum(v):
    return jnp.sum(v, axis=0, keepdims=True)


def _params(*sem):
    return pltpu.CompilerParams(dimension_semantics=sem, vmem_limit_bytes=VMEM_LIMIT)


def _full(shape):
    nd = len(shape)
    return pl.BlockSpec(shape, lambda *_: (0,) * nd)


def _rows(tm, cols):
    return pl.BlockSpec((tm, cols), lambda i, *_: (i, 0))


def _modulate_bwd(dh, x, vec_ref, r_g, r_sc, r_sh, ps_ref, dres):
    xhat, r = _rms(x)
    g = vec_ref[r_g:r_g + 1, :]
    n = xhat * g
    ps_ref[r_sh:r_sh + 1, :] += _rowsum(dh)
    ps_ref[r_sc:r_sc + 1, :] += _rowsum(dh * n)
    dn = dh * (1.0 + vec_ref[r_sc:r_sc + 1, :])
    ps_ref[r_g:r_g + 1, :] += _rowsum(dn * xhat)
    return dres + _rms_bwd(dn * g, xhat, r)


def _mla_pre(x, vec, wcat, g_q, g_kv, wuq, wukv, cs):
    T = x.shape[0]
    tm = _tile(T, 512)
    H = MLA_HEADS

    def body(x_ref, vec_ref, wcat_ref, gq_ref, gkv_ref, wuq_ref, wukv_ref, cs_ref, h_ref, z_ref, q_ref, k_ref, v_ref):
        xhat, _ = _rms(x_ref[...])
        h = xhat * vec_ref[R_GMIX:R_GMIX + 1, :] * (1.0 + vec_ref[R_SC1:R_SC1 + 1, :]) + vec_ref[R_SH1:R_SH1 + 1, :]
        hb = h.astype(MXU_DTYPE)
        h_ref[...] = hb
        z = _dot(hb, wcat_ref[...])
        z_ref[...] = z
        cq = (_rms(z[:, :Q_LORA])[0] * gq_ref[...]).astype(MXU_DTYPE)
        ckv = (_rms(z[:, Q_LORA:Q_LORA + KV_LORA])[0] * gkv_ref[...]).astype(MXU_DTYPE)
        cs_t = cs_ref[...]
        t = z[:, Q_LORA + KV_LORA:] * cs_t
        k_rope = (t + pltpu.roll(t, QK_ROPE, axis=1)).astype(MXU_DTYPE)
        low = lax.broadcasted_iota(jnp.int32, (1, LANES), 1) < QK_ROPE
        for hd in range(H):
            qf = _dot(cq, wuq_ref[hd])
            tq = qf[:, QK_NOPE:] * cs_t
            tq = tq + pltpu.roll(tq, QK_ROPE, axis=1)
            q_ref[hd, :, :QK_NOPE] = (qf[:, :QK_NOPE] * MLA_QSCALE).astype(MXU_DTYPE)
            q_ref[hd, :, QK_NOPE:] = jnp.where(low, tq * MLA_QSCALE, 0.0).astype(MXU_DTYPE)
            kvf = _dot(ckv, wukv_ref[hd])
            k_ref[hd, :, :QK_NOPE] = kvf[:, :QK_NOPE].astype(MXU_DTYPE)
            k_ref[hd, :, QK_NOPE:] = k_rope
            v_ref[hd] = kvf[:, QK_NOPE:].astype(MXU_DTYPE)

    zc = wcat.shape[1]
    return pl.pallas_call(
        body, name="mla_pre", grid=(T // tm,),
        in_specs=[_rows(tm, D), _full((8, D)), _full(wcat.shape), _full(g_q.shape), _full(g_kv.shape),
                  _full(wuq.shape), _full(wukv.shape), _rows(tm, LANES)],
        out_specs=[_rows(tm, D), _rows(tm, zc),
                   pl.BlockSpec((H, tm, QK_EXT), lambda i: (0, i, 0)),
                   pl.BlockSpec((H, tm, QK_EXT), lambda i: (0, i, 0)),
                   pl.BlockSpec((H, tm, V_DIM), lambda i: (0, i, 0))],
        out_shape=[jax.ShapeDtypeStruct((T, D), MXU_DTYPE), jax.ShapeDtypeStruct((T, zc), jnp.float32),
                   jax.ShapeDtypeStruct((H, T, QK_EXT), MXU_DTYPE), jax.ShapeDtypeStruct((H, T, QK_EXT), MXU_DTYPE),
                   jax.ShapeDtypeStruct((H, T, V_DIM), MXU_DTYPE)],
        compiler_params=_params("parallel"),
    )(x, vec, wcat, g_q, g_kv, wuq, wukv, cs)


def _mla_attn_fwd(q, k, v):
    H, T, _ = q.shape
    tb = _tile(T, ATTN_BLOCK)
    sub = min(ATTN_SUB, tb)
    ns, nb = tb // sub, T // tb
    pairs = [(i, j) for i in range(nb) for j in range(i + 1)]
    qi_tab = jnp.asarray([i for i, _ in pairs], jnp.int32)
    kj_tab = jnp.asarray([j for _, j in pairs], jnp.int32)

    def body(qi_ref, kj_ref, q_ref, k_ref, v_ref, o_ref, lse_ref, m_sc, l_sc, acc_sc):
        qi, kj = qi_ref[pl.program_id(1)], kj_ref[pl.program_id(1)]

        @pl.when(kj == 0)
        def _():
            m_sc[...] = jnp.full_like(m_sc, NEG)
            l_sc[...] = jnp.zeros_like(l_sc)
            acc_sc[...] = jnp.zeros_like(acc_sc)

        def update(r, kk, masked):
            rows, keys = pl.ds(r * sub, sub), pl.ds(kk * sub, sub)
            s = _dot_nt(q_ref[0, rows, :], k_ref[0, keys, :])
            if masked:
                row = lax.broadcasted_iota(jnp.int32, (sub, sub), 0)
                col = lax.broadcasted_iota(jnp.int32, (sub, sub), 1)
                s = jnp.where(col <= row, s, NEG)
            m_prev = m_sc[rows, :]
            m_new = jnp.maximum(m_prev, jnp.max(s, axis=1, keepdims=True))
            alpha = jnp.exp2(m_prev - m_new)
            p = jnp.exp2(s - jnp.tile(m_new, (1, sub // LANES)))
            l_sc[rows, :] = alpha * l_sc[rows, :] + jnp.sum(p, axis=1, keepdims=True)
            acc_sc[rows, :] = alpha * acc_sc[rows, :] + _dot(p.astype(MXU_DTYPE), v_ref[0, keys, :])
            m_sc[rows, :] = m_new

        @pl.when(kj < qi)
        def _():
            for kk in range(ns):
                for r in range(ns):
                    update(r, kk, False)

        @pl.when(kj == qi)
        def _():
            for kk in range(ns):
                for r in range(kk, ns):
                    update(r, kk, r == kk)
            l = l_sc[...]
            o_ref[...] = (acc_sc[...] / l).astype(o_ref.dtype)
            lse = m_sc[...] + jnp.log2(l)
            pick = (lax.broadcasted_iota(jnp.int32, (8, LANES), 1) == 0).astype(jnp.float32)
            row = lax.dot_general(pick, lse, (((1,), (1,)), ((), ())), precision=lax.Precision.HIGHEST,
                                  preferred_element_type=jnp.float32)
            lse_ref[0] = row[0:1, :]

    q_idx = lambda h, p, qi_ref, kj_ref: (h, qi_ref[p], 0)
    kv_idx = lambda h, p, qi_ref, kj_ref: (h, kj_ref[p], 0)
    return pl.pallas_call(
        body, name="mla_attn_fwd",
        grid_spec=pltpu.PrefetchScalarGridSpec(
            num_scalar_prefetch=2, grid=(H, len(pairs)),
            in_specs=[pl.BlockSpec((1, tb, QK_EXT), q_idx), pl.BlockSpec((1, tb, QK_EXT), kv_idx),
                      pl.BlockSpec((1, tb, V_DIM), kv_idx)],
            out_specs=[pl.BlockSpec((tb, V_DIM), lambda h, p, qi_ref, kj_ref: (qi_ref[p], h)),
                       pl.BlockSpec((1, 1, tb), lambda h, p, qi_ref, kj_ref: (h, 0, qi_ref[p]))],
            scratch_shapes=[pltpu.VMEM((tb, LANES), jnp.float32), pltpu.VMEM((tb, LANES), jnp.float32),
                            pltpu.VMEM((tb, V_DIM), jnp.float32)]),
        out_shape=[jax.ShapeDtypeStruct((T, H * V_DIM), MXU_DTYPE), jax.ShapeDtypeStruct((H, 1, T), jnp.float32)],
        compiler_params=_params("parallel", "arbitrary"),
    )(qi_tab, kj_tab, q, k, v)


def _post_attn(o, x, w_o, bias, vec, o_transposed=False):
    T = x.shape[0]
    tm = _tile(T, ROW_TILE)
    o_spec = pl.BlockSpec((D, tm), lambda i: (0, i)) if o_transposed else _rows(tm, D)

    def body(o_ref, x_ref, w_ref, b_ref, vec_ref, y_ref, xm_ref, h_ref):
        y = (_dot_tn if o_transposed else _dot)(o_ref[...], w_ref[...]) + b_ref[...]
        y_ref[...] = y.astype(y_ref.dtype)
        xm = x_ref[...] + vec_ref[R_GT1:R_GT1 + 1, :] * y
        xm_ref[...] = xm
        xhat, _ = _rms(xm)
        h = xhat * vec_ref[R_GMLP:R_GMLP + 1, :] * (1.0 + vec_ref[R_SC2:R_SC2 + 1, :]) + vec_ref[R_SH2:R_SH2 + 1, :]
        h_ref[...] = h.astype(h_ref.dtype)

    return pl.pallas_call(
        body, name="post_attn", grid=(T // tm,),
        in_specs=[o_spec, _rows(tm, D), _full((D, D)), _full((1, D)), _full((8, D))],
        out_specs=[_rows(tm, D), _rows(tm, D), _rows(tm, D)],
        out_shape=[jax.ShapeDtypeStruct((T, D), MXU_DTYPE), jax.ShapeDtypeStruct((T, D), jnp.float32),
                   jax.ShapeDtypeStruct((T, D), MXU_DTYPE)],
        compiler_params=_params("parallel"),
    )(o, x, w_o, bias, vec)


def _ff_specs(tf):
    per = D_FF // N_CHIPS // tf
    w1 = pl.BlockSpec((None, D, tf), lambda i, f: (f // per, 0, f % per))
    w2 = pl.BlockSpec((None, tf, D), lambda i, f: (f // per, f % per, 0))
    return w1, w2


def _mlp_fwd(h2, w1, w2, xm, vec):
    T = h2.shape[0]
    tm = _tile(T, MLP_FWD_TILE[0])
    tf = _tile(D_FF // N_CHIPS, MLP_FWD_TILE[1])
    nf = D_FF // tf
    w1_spec, w2_spec = _ff_specs(tf)

    def body(h_ref, w1_ref, w2_ref, xm_ref, vec_ref, a_ref, y_ref, xo_ref, acc):
        f = pl.program_id(1)

        @pl.when(f == 0)
        def _():
            acc[...] = jnp.zeros_like(acc)

        u = jnp.maximum(_dot(h_ref[...], w1_ref[...]), 0.0)
        ab = (u * u).astype(MXU_DTYPE)
        a_ref[...] = ab
        acc[...] += _dot(ab, w2_ref[...])

        @pl.when(f == nf - 1)
        def _():
            y = acc[...]
            y_ref[...] = y.astype(y_ref.dtype)
            xo_ref[...] = xm_ref[...] + vec_ref[R_GT2:R_GT2 + 1, :] * y

    return pl.pallas_call(
        body, name="mlp_fwd", grid=(T // tm, nf),
        in_specs=[_rows(tm, D), w1_spec, w2_spec, _rows(tm, D), _full((8, D))],
        out_specs=[pl.BlockSpec((tm, tf), lambda i, f: (i, f)), _rows(tm, D), _rows(tm, D)],
        out_shape=[jax.ShapeDtypeStruct((T, D_FF), MXU_DTYPE), jax.ShapeDtypeStruct((T, D), MXU_DTYPE),
                   jax.ShapeDtypeStruct((T, D), jnp.float32)],
        scratch_shapes=[pltpu.VMEM((tm, D), jnp.float32)],
        compiler_params=_params("parallel", "arbitrary"),
    )(h2, w1, w2, xm, vec)


def _swa_pre(x, vec, w_qkv, b_qkv):
    T = x.shape[0]
    tm = _tile(T, 512)
    nq = SWA_HEADS * SWA_HEAD_DIM
    nk = SWA_KV_HEADS * SWA_HEAD_DIM
    wq_t, w_kv = w_qkv[:, :nq].T, w_qkv[:, nq:]
    bq_col, b_kv = b_qkv[:, :nq].reshape(nq, 1), b_qkv[:, nq:]

    def body(x_ref, vec_ref, wq_ref, wkv_ref, bq_ref, bkv_ref, h_ref, qt_ref, k_ref, v_ref):
        xhat, _ = _rms(x_ref[...])
        h = xhat * vec_ref[R_GMIX:R_GMIX + 1, :] * (1.0 + vec_ref[R_SC1:R_SC1 + 1, :]) + vec_ref[R_SH1:R_SH1 + 1, :]
        hb = h.astype(MXU_DTYPE)
        h_ref[...] = hb
        qt_ref[...] = ((_dot_nt(wq_ref[...], hb) + bq_ref[...]) * SWA_SCALE).astype(MXU_DTYPE)
        kv = _dot(hb, wkv_ref[...]) + bkv_ref[...]
        k_ref[...] = kv[:, :nk].astype(MXU_DTYPE)
        v_ref[...] = kv[:, nk:].astype(MXU_DTYPE)

    return pl.pallas_call(
        body, name="swa_pre", grid=(T // tm,),
        in_specs=[_rows(tm, D), _full((8, D)), _full(wq_t.shape), _full(w_kv.shape), _full(bq_col.shape),
                  _full(b_kv.shape)],
        out_specs=[_rows(tm, D), pl.BlockSpec((nq, tm), lambda i: (0, i)), _rows(tm, nk), _rows(tm, nk)],
        out_shape=[jax.ShapeDtypeStruct((T, D), MXU_DTYPE), jax.ShapeDtypeStruct((nq, T), MXU_DTYPE),
                   jax.ShapeDtypeStruct((T, nk), MXU_DTYPE), jax.ShapeDtypeStruct((T, nk), MXU_DTYPE)],
        compiler_params=_params("parallel"),
    )(x, vec, wq_t, w_kv, bq_col, b_kv)


def _swa_bias():
    W = WINDOW
    slopes = 2.0 ** (-8.0 * np.arange(1, SWA_HEADS + 1) / SWA_HEADS)
    j, i = np.arange(W)[:, None], np.arange(W)[None, :]
    dist = np.where(j > i, W + i - j, i - j)
    bias = -slopes[:, None, None] * dist[None].astype(np.float64)
    bias = bias.reshape(SWA_KV_HEADS, SWA_GROUP, W, W).transpose(0, 2, 1, 3)
    return jnp.asarray(bias.reshape(SWA_KV_HEADS, W, SWA_GROUP * W), jnp.float32)


def _swa_fold_mask():
    W, G = WINDOW, SWA_GROUP
    j = lax.broadcasted_iota(jnp.int32, (W, G * W), 0)
    i = lax.broadcasted_iota(jnp.int32, (W, G * W), 1) & (W - 1)
    return j > i


def _swa_fold(band, up):
    return jnp.where(up, band[:WINDOW], band[WINDOW:])


def _swa_unfold(folded, up):
    zero = jnp.zeros_like(folded)
    return jnp.concatenate([jnp.where(up, folded, zero), jnp.where(up, zero, folded)], axis=0)


SWA_STEP_BLOCKS = 4


def _swa_blocks(T):
    nb = T // WINDOW
    return next(b for b in (SWA_STEP_BLOCKS, 2, 1) if nb % b == 0)


def _swa_views(b, qt_ref, kp_ref, kc_ref):
    W = WINDOW
    prev = kp_ref if b == 0 else kc_ref.at[pl.ds((b - 1) * W, W), :]
    return qt_ref.at[:, pl.ds(b * W, W)], prev, kc_ref.at[pl.ds(b * W, W), :]


def _swa_probs(has_prev, up, kh, qt_ref, kp_ref, kc_ref, bias_ref, sink_ref):
    W, Dh, G = WINDOW, SWA_HEAD_DIM, SWA_GROUP
    qt = jnp.concatenate([qt_ref[(kh * G + g) * Dh:(kh * G + g + 1) * Dh, :] for g in range(G)], axis=1)
    kb = jnp.concatenate([kp_ref[:, kh * Dh:(kh + 1) * Dh], kc_ref[:, kh * Dh:(kh + 1) * Dh]], axis=0)
    s = _swa_fold(_dot(kb, qt), up) + bias_ref[kh]
    if has_prev is not True:
        s = jnp.where(up & jnp.logical_not(has_prev), NEG, s)
    sink = sink_ref[kh]
    m = jnp.maximum(jnp.max(s, axis=0, keepdims=True), sink)
    p = jnp.exp(s - m)
    p_sink = jnp.exp(sink - m)
    inv = 1.0 / (jnp.sum(p, axis=0, keepdims=True) + p_sink)
    return qt, kb, p * inv, p_sink * inv


def _swa_attn_fwd(qt, k, v, bias, sink_rows):
    T = qt.shape[1]
    W, Dh, G, Hk = WINDOW, SWA_HEAD_DIM, SWA_GROUP, SWA_KV_HEADS
    nk = Hk * Dh

    nb = _swa_blocks(T)

    def body(qt_ref, kp_ref, kc_ref, vp_ref, vc_ref, bias_ref, sink_ref, ot_ref):
        n = pl.program_id(0)
        up = _swa_fold_mask()
        for b in range(nb):
            q_b, kp_b, kc_b = _swa_views(b, qt_ref, kp_ref, kc_ref)
            _, vp_b, vc_b = _swa_views(b, qt_ref, vp_ref, vc_ref)
            for kh in range(Hk):
                _, _, pn, _ = _swa_probs(True if b else n > 0, up, kh, q_b, kp_b, kc_b, bias_ref, sink_ref)
                vb = jnp.concatenate([vp_b[:, kh * Dh:(kh + 1) * Dh], vc_b[:, kh * Dh:(kh + 1) * Dh]], axis=0)
                ot = _dot_tn(vb, _swa_unfold(pn, up).astype(MXU_DTYPE))
                for g in range(G):
                    rows = pl.ds((kh * G + g) * Dh, Dh)
                    ot_ref[rows, pl.ds(b * W, W)] = ot[:, g * W:(g + 1) * W].astype(ot_ref.dtype)

    prev = lambda n: (jnp.maximum(n * nb - 1, 0), 0)
    cur = lambda n: (n, 0)
    col = lambda n: (0, n)
    return pl.pallas_call(
        body, name="swa_attn_fwd", grid=(T // (nb * W),),
        in_specs=[pl.BlockSpec((D, nb * W), col), pl.BlockSpec((W, nk), prev), pl.BlockSpec((nb * W, nk), cur),
                  pl.BlockSpec((W, nk), prev), pl.BlockSpec((nb * W, nk), cur), _full(bias.shape),
                  _full(sink_rows.shape)],
        out_specs=pl.BlockSpec((D, nb * W), col),
        out_shape=jax.ShapeDtypeStruct((D, T), MXU_DTYPE),
        compiler_params=_params("parallel"),
    )(qt, k, k, v, v, bias, sink_rows)


def _final_loss(x, tgt, g):
    T = x.shape[0]
    tm = _tile(T, ROW_TILE)

    def body(x_ref, t_ref, g_ref, loss_ref, dx_ref, dg_ref):
        @pl.when(pl.program_id(0) == 0)
        def _():
            loss_ref[...] = jnp.zeros_like(loss_ref)
            dg_ref[...] = jnp.zeros_like(dg_ref)

        xhat, r = _rms(x_ref[...])
        gv = g_ref[...]
        e = xhat * gv - t_ref[...]
        loss_ref[...] += 0.5 * jnp.sum(jnp.mean(e * e, axis=-1, keepdims=True), axis=0, keepdims=True)
        dy = e * (1.0 / D)
        dg_ref[...] += _rowsum(dy * xhat)
        dx_ref[...] = _rms_bwd(dy * gv, xhat, r)

    return pl.pallas_call(
        body, name="final_loss", grid=(T // tm,),
        in_specs=[_rows(tm, D), _rows(tm, D), _full((1, D))],
        out_specs=[_full((8, LANES)), _rows(tm, D), _full((1, D))],
        out_shape=[jax.ShapeDtypeStruct((8, LANES), jnp.float32), jax.ShapeDtypeStruct((T, D), jnp.float32),
                   jax.ShapeDtypeStruct((1, D), jnp.float32)],
        compiler_params=_params("arbitrary"),
    )(x, tgt, g)


def _mlp_bwd(dxo, y2, a, w1, w2, xm, vec):
    T = dxo.shape[0]
    tm = _tile(T, MLP_BWD_TILE[0])
    tf = _tile(D_FF // N_CHIPS, MLP_BWD_TILE[1])
    nf = D_FF // tf
    w1_spec, w2_spec = _ff_specs(tf)

    def body(dxo_ref, y_ref, a_ref, w1_ref, w2_ref, xm_ref, vec_ref, du_ref, dy_ref, dxm_ref, ps_ref, dyb, acc):
        i, f = pl.program_id(0), pl.program_id(1)

        @pl.when((i == 0) & (f == 0))
        def _():
            ps_ref[...] = jnp.zeros_like(ps_ref)

        @pl.when(f == 0)
        def _():
            dxo_t = dxo_ref[...]
            d = (dxo_t * vec_ref[R_GT2:R_GT2 + 1, :]).astype(MXU_DTYPE)
            dyb[...] = d
            dy_ref[...] = d
            acc[...] = jnp.zeros_like(acc)
            ps_ref[R_GT2:R_GT2 + 1, :] += _rowsum(dxo_t * y_ref[...].astype(jnp.float32))

        da = _dot_nt(dyb[...], w2_ref[...])
        dub = (da * (2.0 * jnp.sqrt(a_ref[...].astype(jnp.float32)))).astype(MXU_DTYPE)
        du_ref[...] = dub
        acc[...] += _dot_nt(dub, w1_ref[...])

        @pl.when(f == nf - 1)
        def _():
            dxm_ref[...] = _modulate_bwd(acc[...], xm_ref[...], vec_ref, R_GMLP, R_SC2, R_SH2, ps_ref, dxo_ref[...])

    return pl.pallas_call(
        body, name="mlp_bwd", grid=(T // tm, nf),
        in_specs=[_rows(tm, D), _rows(tm, D), pl.BlockSpec((tm, tf), lambda i, f: (i, f)), w1_spec, w2_spec,
                  _rows(tm, D), _full((8, D))],
        out_specs=[pl.BlockSpec((tm, tf), lambda i, f: (i, f)), _rows(tm, D), _rows(tm, D), _full((8, D))],
        out_shape=[jax.ShapeDtypeStruct((T, D_FF), MXU_DTYPE), jax.ShapeDtypeStruct((T, D), MXU_DTYPE),
                   jax.ShapeDtypeStruct((T, D), jnp.float32), jax.ShapeDtypeStruct((8, D), jnp.float32)],
        scratch_shapes=[pltpu.VMEM((tm, D), MXU_DTYPE), pltpu.VMEM((tm, D), jnp.float32)],
        compiler_params=_params("arbitrary", "arbitrary"),
    )(dxo, y2, a, w1, w2, xm, vec)


def _mm_tn(a, g, name, split=None, layers=1, layer=0, into=None, a_transposed=False):
    K, T = a.shape if a_transposed else a.shape[::-1]
    N = g.shape[1]
    kq = K // N_CHIPS if split == "rows" else K
    nq = N // N_CHIPS if split == "cols" else N
    bk, bn, bt = _tile(kq, 1024), _tile(nq, 1024), _tile(T, DW_TOKENS)
    if nq % bn or bn % LANES:
        bn = nq
    kper, nper = kq // bk, nq // bn

    def body(*refs):
        a_ref, g_ref, o_ref = refs[0], refs[1], refs[-1]

        @pl.when(pl.program_id(2) == 0)
        def _():
            o_ref[...] = jnp.zeros_like(o_ref)

        o_ref[...] += (_dot if a_transposed else _dot_tn)(a_ref[...], g_ref[...])

    a_spec = pl.BlockSpec((bk, bt), lambda k, n, t: (k, t)) if a_transposed else pl.BlockSpec((bt, bk), lambda k, n, t: (t, k))
    in_specs = [a_spec, pl.BlockSpec((bt, bn), lambda k, n, t: (t, n))]
    args = [a, g]
    aliases = {}
    if split is None:
        out_spec = pl.BlockSpec((bk, bn), lambda k, n, t: (k, n))
        out_shape = jax.ShapeDtypeStruct((K, N), jnp.float32)
    else:
        if split == "cols":
            idx = lambda k, n, t: (n // nper, layer, k, n % nper)
        else:
            idx = lambda k, n, t: (k // kper, layer, k % kper, n)
        out_spec = pl.BlockSpec((None, None, bk, bn), idx)
        out_shape = jax.ShapeDtypeStruct((N_CHIPS, layers, kq, nq), jnp.float32)
        if into is not None:
            in_specs.append(pl.BlockSpec(memory_space=pl.ANY))
            args.append(into)
            aliases = {2: 0}
    return pl.pallas_call(
        body, name=name, grid=(K // bk, N // bn, T // bt), in_specs=in_specs, out_specs=out_spec, out_shape=out_shape,
        input_output_aliases=aliases, compiler_params=_params("parallel", "parallel", "arbitrary"),
    )(*args)


def _attn_out_bwd(dxm, y1, o, w_o, vec, with_delta):
    T = dxm.shape[0]
    tm = _tile(T, ROW_TILE)
    H = MLA_HEADS

    def body(dxm_ref, y_ref, w_ref, vec_ref, *refs):
        o_ref = refs[0] if with_delta else None
        dy_ref, do_ref, ps_ref, *delta_ref = refs[1:] if with_delta else refs

        @pl.when(pl.program_id(0) == 0)
        def _():
            ps_ref[...] = jnp.zeros_like(ps_ref)

        dxm_t = dxm_ref[...]
        dy = dxm_t * vec_ref[R_GT1:R_GT1 + 1, :]
        ps_ref[R_GT1:R_GT1 + 1, :] += _rowsum(dxm_t * y_ref[...].astype(jnp.float32))
        ps_ref[R_BO:R_BO + 1, :] += _rowsum(dy)
        dyb = dy.astype(MXU_DTYPE)
        dy_ref[...] = dyb
        if not with_delta:
            do_ref[...] = _dot_nt(w_ref[...], dyb).astype(do_ref.dtype)
        else:
            do = _dot_nt(dyb, w_ref[...])
            do_ref[...] = do.astype(do_ref.dtype)
            of = o_ref[...].astype(jnp.float32)
            ones = jnp.ones((8, V_DIM), jnp.float32)
            for hd in range(H):
                sl = slice(hd * V_DIM, (hd + 1) * V_DIM)
                d = lax.dot_general(ones, do[:, sl] * of[:, sl], (((1,), (1,)), ((), ())),
                                    precision=lax.Precision.HIGHEST, preferred_element_type=jnp.float32)
                delta_ref[0][hd] = d[0:1, :]

    out_specs = [_rows(tm, D), _rows(tm, D), _full((8, D))]
    out_shape = [jax.ShapeDtypeStruct((T, D), MXU_DTYPE), jax.ShapeDtypeStruct((T, D), MXU_DTYPE),
                 jax.ShapeDtypeStruct((8, D), jnp.float32)]
    if not with_delta:
        out_specs[1] = pl.BlockSpec((D, tm), lambda i: (0, i))
        out_shape[1] = jax.ShapeDtypeStruct((D, T), MXU_DTYPE)
    if with_delta:
        out_specs.append(pl.BlockSpec((H, 1, tm), lambda i: (0, 0, i)))
        out_shape.append(jax.ShapeDtypeStruct((H, 1, T), jnp.float32))
    return pl.pallas_call(
        body, name="attn_out_bwd_mla" if with_delta else "attn_out_bwd_swa", grid=(T // tm,),
        in_specs=[_rows(tm, D), _rows(tm, D), _full((D, D)), _full((8, D))] + ([_rows(tm, D)] if with_delta else []),
        out_specs=out_specs, out_shape=out_shape,
        compiler_params=_params("arbitrary"),
    )(dxm, y1, w_o, vec, *([o] if with_delta else []))


def _mla_attn_bwd(q, k, v, do, lse, delta):
    H, T, _ = q.shape
    tb = _tile(T, ATTN_BLOCK)
    sub = min(ATTN_SUB, tb)
    ns, nb = tb // sub, T // tb

    pairs = [(j, i) for j in range(nb) for i in range(j, nb)]
    kj_tab = jnp.asarray([j for j, _ in pairs], jnp.int32)
    qi_tab = jnp.asarray([i for _, i in pairs], jnp.int32)

    def body(kj_ref, qi_ref, q_ref, k_ref, v_ref, do_ref, lse_ref, dl_ref, dq_ref, dk_ref, dv_ref, dk_acc, dv_acc):
        j, i = kj_ref[pl.program_id(1)], qi_ref[pl.program_id(1)]

        @pl.when((j == 0) & (i == 0))
        def _():
            dq_ref[...] = jnp.zeros_like(dq_ref)

        def update(kk, r, masked):
            keys, rows = pl.ds(kk * sub, sub), pl.ds(r * sub, sub)
            kb, qb, dob = k_ref[0, keys, :], q_ref[0, rows, :], do_ref[rows, :]
            st = _dot_nt(kb, qb)
            if masked:
                row = lax.broadcasted_iota(jnp.int32, (sub, sub), 0)
                col = lax.broadcasted_iota(jnp.int32, (sub, sub), 1)
                st = jnp.where(row <= col, st, NEG)
            pt = jnp.exp2(st - lse_ref[0, :, rows])
            dv_acc[keys, :] += _dot(pt.astype(MXU_DTYPE), dob)
            dpt = _dot_nt(v_ref[0, keys, :], dob)
            dst = (pt * (dpt - dl_ref[0, :, rows])).astype(MXU_DTYPE)
            dk_acc[keys, :] += _dot(dst, qb)
            q_rows = pl.ds(pl.multiple_of(i * tb + r * sub, sub), sub)
            dq_ref[0, q_rows, :] += _dot_tn(dst, kb)

        @pl.when(i == j)
        def _():
            dk_acc[...] = jnp.zeros_like(dk_acc)
            dv_acc[...] = jnp.zeros_like(dv_acc)
            for r in range(ns):
                for kk in range(r + 1):
                    update(kk, r, kk == r)

        @pl.when(i > j)
        def _():
            for r in range(ns):
                for kk in range(ns):
                    update(kk, r, False)

        @pl.when(i == nb - 1)
        def _():
            dk_ref[0] = (dk_acc[...] * LN2).astype(dk_ref.dtype)
            dv_ref[0] = dv_acc[...].astype(dv_ref.dtype)

    q_idx = lambda h, p, kj_ref, qi_ref: (h, qi_ref[p], 0)
    kv_idx = lambda h, p, kj_ref, qi_ref: (h, kj_ref[p], 0)
    stat_idx = lambda h, p, kj_ref, qi_ref: (h, 0, qi_ref[p])
    return pl.pallas_call(
        body, name="mla_attn_bwd",
        grid_spec=pltpu.PrefetchScalarGridSpec(
            num_scalar_prefetch=2, grid=(H, len(pairs)),
            in_specs=[pl.BlockSpec((1, tb, QK_EXT), q_idx), pl.BlockSpec((1, tb, QK_EXT), kv_idx),
                      pl.BlockSpec((1, tb, V_DIM), kv_idx),
                      pl.BlockSpec((tb, V_DIM), lambda h, p, kj_ref, qi_ref: (qi_ref[p], h)),
                      pl.BlockSpec((1, 1, tb), stat_idx), pl.BlockSpec((1, 1, tb), stat_idx)],
            out_specs=[pl.BlockSpec((1, T, QK_EXT), lambda h, p, kj_ref, qi_ref: (h, 0, 0)),
                       pl.BlockSpec((1, tb, QK_EXT), kv_idx), pl.BlockSpec((1, tb, V_DIM), kv_idx)],
            scratch_shapes=[pltpu.VMEM((tb, QK_EXT), jnp.float32), pltpu.VMEM((tb, V_DIM), jnp.float32)]),
        out_shape=[jax.ShapeDtypeStruct((H, T, QK_EXT), jnp.float32), jax.ShapeDtypeStruct((H, T, QK_EXT), MXU_DTYPE),
                   jax.ShapeDtypeStruct((H, T, V_DIM), MXU_DTYPE)],
        compiler_params=_params("parallel", "arbitrary"),
    )(kj_tab, qi_tab, q, k, v, do, lse, delta)


def _mla_pre_bwd(x, dxm, vec, hb, z, dq, dk, dv, cs, wcat, g_q, g_kv, wuq, wukv):
    T = x.shape[0]
    tm = _tile(T, 512)
    H = MLA_HEADS
    zc = wcat.shape[1]

    def body(x_ref, dxm_ref, vec_ref, h_ref, z_ref, dq_ref, dk_ref, dv_ref, cs_ref, wcat_ref, gq_ref, gkv_ref,
             wuq_ref, wukv_ref, dx_ref, ps_ref, dgq_ref, dgkv_ref, dwcat_ref, dwuq_ref, dwukv_ref):
        @pl.when(pl.program_id(0) == 0)
        def _():
            for ref in (ps_ref, dgq_ref, dgkv_ref, dwcat_ref, dwuq_ref, dwukv_ref):
                ref[...] = jnp.zeros_like(ref)

        z = z_ref[...]
        cs_t = cs_ref[...]
        cqhat, rq = _rms(z[:, :Q_LORA])
        ckhat, rk = _rms(z[:, Q_LORA:Q_LORA + KV_LORA])
        gq, gkv = gq_ref[...], gkv_ref[...]
        cq = (cqhat * gq).astype(MXU_DTYPE)
        ckv = (ckhat * gkv).astype(MXU_DTYPE)
        dcq = jnp.zeros((tm, Q_LORA), jnp.float32)
        dckv = jnp.zeros((tm, KV_LORA), jnp.float32)
        dkr = jnp.zeros((tm, LANES), jnp.float32)
        for hd in range(H):
            dqh = dq_ref[hd] * MLA_SCALE
            gqh = jnp.concatenate([dqh[:, :QK_NOPE], dqh[:, QK_NOPE:] * cs_t], axis=1).astype(MXU_DTYPE)
            dcq += _dot_nt(gqh, wuq_ref[hd])
            dwuq_ref[hd] += _dot_tn(cq, gqh)
            dkh = dk_ref[hd]
            gkvh = jnp.concatenate([dkh[:, :QK_NOPE], dv_ref[hd]], axis=1)
            dckv += _dot_nt(gkvh, wukv_ref[hd])
            dwukv_ref[hd] += _dot_tn(ckv, gkvh)
            dkr += dkh[:, QK_NOPE:].astype(jnp.float32)
        dgq_ref[...] += _rowsum(dcq * cqhat)
        dgkv_ref[...] += _rowsum(dckv * ckhat)
        dcq_pre = _rms_bwd(dcq * gq, cqhat, rq)
        dckv_pre = _rms_bwd(dckv * gkv, ckhat, rk)
        dkr2 = (dkr + pltpu.roll(dkr, QK_ROPE, axis=1)) * cs_t
        dz = jnp.concatenate([dcq_pre, dckv_pre, dkr2], axis=1).astype(MXU_DTYPE)
        dwcat_ref[...] += _dot_tn(h_ref[...], dz)
        dh = _dot_nt(dz, wcat_ref[...])
        dx_ref[...] = _modulate_bwd(dh, x_ref[...], vec_ref, R_GMIX, R_SC1, R_SH1, ps_ref, dxm_ref[...])

    hblk = lambda w: pl.BlockSpec((H, tm, w), lambda i: (0, i, 0))
    return pl.pallas_call(
        body, name="mla_pre_bwd", grid=(T // tm,),
        in_specs=[_rows(tm, D), _rows(tm, D), _full((8, D)), _rows(tm, D), _rows(tm, zc), hblk(QK_EXT), hblk(QK_EXT),
                  hblk(V_DIM), _rows(tm, LANES), _full(wcat.shape), _full(g_q.shape), _full(g_kv.shape),
                  _full(wuq.shape), _full(wukv.shape)],
        out_specs=[_rows(tm, D), _full((8, D)), _full(g_q.shape), _full(g_kv.shape), _full(wcat.shape),
                   _full(wuq.shape), _full(wukv.shape)],
        out_shape=[jax.ShapeDtypeStruct((T, D), jnp.float32), jax.ShapeDtypeStruct((8, D), jnp.float32),
                   jax.ShapeDtypeStruct(g_q.shape, jnp.float32), jax.ShapeDtypeStruct(g_kv.shape, jnp.float32),
                   jax.ShapeDtypeStruct(wcat.shape, jnp.float32), jax.ShapeDtypeStruct(wuq.shape, jnp.float32),
                   jax.ShapeDtypeStruct(wukv.shape, jnp.float32)],
        compiler_params=_params("arbitrary"),
    )(x, dxm, vec, hb, z, dq, dk, dv, cs, wcat, g_q, g_kv, wuq, wukv)


def _swa_attn_bwd(qt, k, v, dot_, bias, sink_rows):
    T = qt.shape[1]
    W, Dh, G, Hk = WINDOW, SWA_HEAD_DIM, SWA_GROUP, SWA_KV_HEADS
    nk = Hk * Dh
    nb = _swa_blocks(T)

    def body(qt_ref, kp_ref, kc_ref, vp_ref, vc_ref, dot_ref, bias_ref, sink_ref, dqt_ref, dk_ref, dv_ref, dsink_ref):
        n = pl.program_id(0)

        @pl.when(n == 0)
        def _():
            dk_ref[...] = jnp.zeros_like(dk_ref)
            dv_ref[...] = jnp.zeros_like(dv_ref)
            dsink_ref[...] = jnp.zeros_like(dsink_ref)

        def add_rows(first_row, dkb_part, dvb_part):
            rows = pl.ds(pl.multiple_of(first_row, W), W)
            dk_ref[rows, :] += dkb_part
            dv_ref[rows, :] += dvb_part

        up = _swa_fold_mask()
        for b in range(nb):
            q_b, kp_b, kc_b = _swa_views(b, qt_ref, kp_ref, kc_ref)
            do_b, vp_b, vc_b = _swa_views(b, dot_ref, vp_ref, vc_ref)
            dks, dvs = [], []
            for kh in range(Hk):
                qt, kb, pn, p_sink = _swa_probs(True if b else n > 0, up, kh, q_b, kp_b, kc_b, bias_ref, sink_ref)
                vb = jnp.concatenate([vp_b[:, kh * Dh:(kh + 1) * Dh], vc_b[:, kh * Dh:(kh + 1) * Dh]], axis=0)
                dot_h = jnp.concatenate([do_b[(kh * G + g) * Dh:(kh * G + g + 1) * Dh, :] for g in range(G)], axis=1)
                dp = _swa_fold(_dot(vb, dot_h), up)
                delta = jnp.sum(pn * dp, axis=0, keepdims=True)
                dsb = _swa_unfold(pn * (dp - delta), up).astype(MXU_DTYPE)
                dsink_ref[kh] += -p_sink * delta
                dqt = _dot_tn(kb, dsb) * SWA_SCALE
                for g in range(G):
                    dqt_ref[pl.ds((kh * G + g) * Dh, Dh), pl.ds(b * W, W)] = dqt[:, g * W:(g + 1) * W]
                dks.append(_dot_nt(dsb, qt))
                dvs.append(_dot_nt(_swa_unfold(pn, up).astype(MXU_DTYPE), dot_h))
            dkb = jnp.concatenate(dks, axis=1)
            dvb = jnp.concatenate(dvs, axis=1)
            add_rows((n * nb + b) * W, dkb[W:], dvb[W:])
            if b:
                add_rows((n * nb + b - 1) * W, dkb[:W], dvb[:W])
            else:
                @pl.when(n > 0)
                def _():
                    add_rows((n * nb - 1) * W, dkb[:W], dvb[:W])

    prev = lambda n: (jnp.maximum(n * nb - 1, 0), 0)
    cur = lambda n: (n, 0)
    col = lambda n: (0, n)
    return pl.pallas_call(
        body, name="swa_attn_bwd", grid=(T // (nb * W),),
        in_specs=[pl.BlockSpec((D, nb * W), col), pl.BlockSpec((W, nk), prev), pl.BlockSpec((nb * W, nk), cur),
                  pl.BlockSpec((W, nk), prev), pl.BlockSpec((nb * W, nk), cur), pl.BlockSpec((D, nb * W), col),
                  _full(bias.shape), _full(sink_rows.shape)],
        out_specs=[pl.BlockSpec((D, nb * W), col), _full((T, nk)), _full((T, nk)), _full(sink_rows.shape)],
        out_shape=[jax.ShapeDtypeStruct((D, T), jnp.float32), jax.ShapeDtypeStruct((T, nk), jnp.float32),
                   jax.ShapeDtypeStruct((T, nk), jnp.float32), jax.ShapeDtypeStruct(sink_rows.shape, jnp.float32)],
        compiler_params=_params("arbitrary"),
    )(qt, k, k, v, v, dot_, bias, sink_rows)


def _swa_pre_bwd(x, dxm, vec, dq_t, dk, dv, w_qkv):
    T = x.shape[0]
    tm = _tile(T, 512)
    nq = SWA_HEADS * SWA_HEAD_DIM
    nk = SWA_KV_HEADS * SWA_HEAD_DIM
    nqkv = nq + 2 * nk

    def body(x_ref, dxm_ref, vec_ref, dq_ref, dk_ref, dv_ref, w_ref, dx_ref, dqkv_ref, ps_ref, db_ref):
        @pl.when(pl.program_id(0) == 0)
        def _():
            ps_ref[...] = jnp.zeros_like(ps_ref)
            db_ref[...] = jnp.zeros_like(db_ref)

        dqkv = jnp.concatenate([dq_ref[...].T, dk_ref[...], dv_ref[...]], axis=1)
        db_ref[...] += _rowsum(dqkv)
        dqkv_b = dqkv.astype(MXU_DTYPE)
        dqkv_ref[...] = dqkv_b
        dh = _dot_nt(dqkv_b, w_ref[...])
        dx_ref[...] = _modulate_bwd(dh, x_ref[...], vec_ref, R_GMIX, R_SC1, R_SH1, ps_ref, dxm_ref[...])

    return pl.pallas_call(
        body, name="swa_pre_bwd", grid=(T // tm,),
        in_specs=[_rows(tm, D), _rows(tm, D), _full((8, D)), pl.BlockSpec((nq, tm), lambda i: (0, i)), _rows(tm, nk),
                  _rows(tm, nk), _full(w_qkv.shape)],
        out_specs=[_rows(tm, D), _rows(tm, nqkv), _full((8, D)), _full((1, nqkv))],
        out_shape=[jax.ShapeDtypeStruct((T, D), jnp.float32), jax.ShapeDtypeStruct((T, nqkv), MXU_DTYPE),
                   jax.ShapeDtypeStruct((8, D), jnp.float32), jax.ShapeDtypeStruct((1, nqkv), jnp.float32)],
        compiler_params=_params("arbitrary"),
    )(x, dxm, vec, dq_t, dk, dv, w_qkv)


def _rot_cols(w):
    half = QK_ROPE // 2
    return jnp.concatenate([-w[..., half:], w[..., :half]], axis=-1)


def _unrot_grad(d_rope, d_rot):
    half = QK_ROPE // 2
    return d_rope + jnp.concatenate([d_rot[..., half:], -d_rot[..., :half]], axis=-1)


def _rope_table(positions):
    half = QK_ROPE // 2
    inv_freq = ROPE_THETA ** (-jnp.arange(half, dtype=jnp.float32) / half)
    ang = positions.astype(jnp.float32)[:, None] * inv_freq
    cos, sin = jnp.cos(ang), jnp.sin(ang)
    return jnp.concatenate([cos, cos, sin, sin], axis=1)


def _sequence_step(x, tgt, positions, vecs, g_q, g_kv, sinks, g_final, wts, late_weights, on_late_grads, on_late_landed):
    H = MLA_HEADS
    cs = _rope_table(positions)
    w_dkv = wts["mla_w_dkv"]
    wcat = jnp.concatenate([wts["mla_w_dq"], w_dkv, _rot_cols(w_dkv[:, KV_LORA:])], axis=1)
    uq = wts["mla_w_uq"].reshape(Q_LORA, H, QK_NOPE + QK_ROPE)
    wuq = jnp.concatenate([uq, _rot_cols(uq[..., QK_NOPE:])], axis=-1).transpose(1, 0, 2)
    wukv = wts["mla_w_ukv"].reshape(KV_LORA, H, QK_NOPE + V_DIM).transpose(1, 0, 2)
    zero_bias = jnp.zeros((1, D), jnp.float32)
    bias = _swa_bias()
    sink_rows = jnp.broadcast_to(sinks.reshape(SWA_KV_HEADS, 1, SWA_GROUP, 1),
                                 (SWA_KV_HEADS, 1, SWA_GROUP, WINDOW)).reshape(SWA_KV_HEADS, 1, SWA_GROUP * WINDOW)

    h1a, z, q, k, v = _mla_pre(x, vecs[0], wcat, g_q, g_kv, wuq, wukv, cs)
    o_a, lse = _mla_attn_fwd(q, k, v)
    y1a, xm_a, h2a = _post_attn(o_a, x, wts["mla_w_o"], zero_bias, vecs[0])
    wts = {**wts, **late_weights(h2a)}
    a_a, y2a, x1 = _mlp_fwd(h2a, wts["w_ff1"][0], wts["w_ff2"][0], xm_a, vecs[0])

    h1b, qs_t, ks, vs = _swa_pre(x1, vecs[1], wts["swa_w_qkv"], wts["swa_b_qkv"])
    o_bt = _swa_attn_fwd(qs_t, ks, vs, bias, sink_rows)
    y1b, xm_b, h2b = _post_attn(o_bt, x1, wts["swa_w_o"], wts["swa_b_o"], vecs[1], o_transposed=True)
    a_b, y2b, x2 = _mlp_fwd(h2b, wts["w_ff1"][1], wts["w_ff2"][1], xm_b, vecs[1])

    loss8, dx2, dg_final = _final_loss(x2, tgt, g_final.reshape(1, D))

    du_b, dy2b, dxm_b, ps_mlp_b = _mlp_bwd(dx2, y2b, a_b, wts["w_ff1"][1], wts["w_ff2"][1], xm_b, vecs[1])
    g_ff2 = _mm_tn(a_b, dy2b, "dw_ff2_l1", "rows", DEPTH, 1)
    g_ff1 = _mm_tn(h2b, du_b, "dw_ff1_l1", "cols", DEPTH, 1)
    dy1b, do_bt, ps_out_b = _attn_out_bwd(dxm_b, y1b, None, wts["swa_w_o"], vecs[1], False)
    g_swa_o = _mm_tn(o_bt, dy1b, "dw_o_swa", a_transposed=True)
    dqs_t, dks, dvs, dsinks = _swa_attn_bwd(qs_t, ks, vs, do_bt, bias, sink_rows)
    dx1, dqkv, ps_pre_b, g_swa_bqkv = _swa_pre_bwd(x1, dxm_b, vecs[1], dqs_t, dks, dvs, wts["swa_w_qkv"])
    g_swa_qkv = _mm_tn(h1b, dqkv, "dw_qkv", "cols")

    du_a, dy2a, dxm_a, ps_mlp_a = _mlp_bwd(dx1, y2a, a_a, wts["w_ff1"][0], wts["w_ff2"][0], xm_a, vecs[0])
    g_ff2 = _mm_tn(a_a, dy2a, "dw_ff2_l0", "rows", DEPTH, 0, g_ff2)
    g_ff1 = _mm_tn(h2a, du_a, "dw_ff1_l0", "cols", DEPTH, 0, g_ff1)
    rows4 = lambda g: g.reshape(N_CHIPS, g.shape[0] // N_CHIPS, g.shape[1])
    token = on_late_grads({
        "swa_w_qkv": g_swa_qkv.reshape(N_CHIPS, D, -1), "swa_w_o": rows4(g_swa_o),
        "w_ff1": g_ff1.reshape(N_CHIPS, DEPTH * D, -1), "w_ff2": g_ff2.reshape(N_CHIPS, -1, D)})
    dy1a, do_a, ps_out_a, delta = _attn_out_bwd(dxm_a, y1a, o_a, wts["mla_w_o"], vecs[0] + token[0, 0], True)
    g_mla_o = _mm_tn(o_a, dy1a, "dw_o_mla")
    token = on_late_landed(g_mla_o)
    dq, dk, dv = _mla_attn_bwd(q, k, v, do_a, lse, delta + token[0, 0])
    dx0, ps_pre_a, dg_q, dg_kv, dwcat, dwuq, dwukv = _mla_pre_bwd(
        x, dxm_a, vecs[0], h1a, z, dq, dk, dv, cs, wcat, g_q, g_kv, wuq, wukv)

    c0, c1, c2 = Q_LORA, Q_LORA + KV_LORA, Q_LORA + KV_LORA + QK_ROPE
    g_dq = dwcat[:, :c0]
    g_dkv = jnp.concatenate([dwcat[:, c0:c1], _unrot_grad(dwcat[:, c1:c2], dwcat[:, c2:])], axis=1)
    e0 = QK_NOPE + QK_ROPE
    g_uq = jnp.concatenate([dwuq[..., :QK_NOPE], _unrot_grad(dwuq[..., QK_NOPE:e0], dwuq[..., e0:])], axis=-1)
    per = H // N_CHIPS
    g_uq = g_uq.reshape(N_CHIPS, per, Q_LORA, e0).transpose(0, 2, 1, 3).reshape(N_CHIPS, Q_LORA, per * e0)
    g_ukv = dwukv.reshape(N_CHIPS, per, KV_LORA, QK_NOPE + V_DIM).transpose(0, 2, 1, 3)
    g_ukv = g_ukv.reshape(N_CHIPS, KV_LORA, per * (QK_NOPE + V_DIM))

    def dmod(ps_pre, ps_out, ps_mlp):
        return jnp.concatenate([ps_pre[R_SH1:R_SC1 + 1], ps_out[R_GT1:R_GT1 + 1], ps_mlp[R_SH2:R_GT2 + 1]], axis=0)

    grads = {"mla_w_dq": rows4(g_dq), "mla_w_uq": g_uq, "mla_w_dkv": rows4(g_dkv), "mla_w_ukv": g_ukv,
             "mla_w_o": rows4(g_mla_o)}
    small = {
        "dmod": jnp.stack([dmod(ps_pre_a, ps_out_a, ps_mlp_a), dmod(ps_pre_b, ps_out_b, ps_mlp_b)]).reshape(DEPTH, 6 * D),
        "g_mix": jnp.stack([ps_pre_a[R_GMIX], ps_pre_b[R_GMIX]]),
        "g_mlp": jnp.stack([ps_mlp_a[R_GMLP], ps_mlp_b[R_GMLP]]),
        "mla_g_q": dg_q, "mla_g_kv": dg_kv, "swa_sinks": jnp.sum(dsinks.reshape(SWA_HEADS, WINDOW), axis=1).reshape(1, SWA_HEADS),
        "swa_b_qkv": g_swa_bqkv, "swa_b_o": ps_out_b[R_BO:R_BO + 1],
        "g_final": dg_final.reshape(D), "loss": loss8[0, 0],
    }
    return dx0, grads, small


SHARDED = {
    "mla_w_dq": (1, D // N_CHIPS, Q_LORA),
    "mla_w_uq": (1, Q_LORA, MLA_HEADS * (QK_NOPE + QK_ROPE) // N_CHIPS),
    "mla_w_dkv": (1, D // N_CHIPS, KV_LORA + QK_ROPE),
    "mla_w_ukv": (1, KV_LORA, MLA_HEADS * (QK_NOPE + V_DIM) // N_CHIPS),
    "mla_w_o": (1, MLA_HEADS * V_DIM // N_CHIPS, D),
    "swa_w_qkv": (1, D, (SWA_HEADS + 2 * SWA_KV_HEADS) * SWA_HEAD_DIM // N_CHIPS),
    "swa_w_o": (1, SWA_HEADS * SWA_HEAD_DIM // N_CHIPS, D),
    "w_ff1": (DEPTH, D, D_FF // N_CHIPS),
    "w_ff2": (DEPTH, D_FF // N_CHIPS, D),
}
COL_SPLIT = ("mla_w_uq", "mla_w_ukv", "swa_w_qkv")
BIASES = {"swa_b_qkv": (SWA_HEADS + 2 * SWA_KV_HEADS) * SWA_HEAD_DIM, "swa_b_o": D}


def _view2d(name):
    shape = SHARDED[name]
    return math.prod(shape[:-1]), shape[-1]


SMALL = {"b_ada": (DEPTH, 6 * D), "g_mix": (DEPTH, D), "g_mlp": (DEPTH, D), "mla_g_q": (1, Q_LORA),
         "mla_g_kv": (1, KV_LORA), "swa_sinks": (1, SWA_HEADS), "g_final": (D,), "loss": (),
         "swa_b_qkv": (1, BIASES["swa_b_qkv"]), "swa_b_o": (1, BIASES["swa_b_o"])}
SMALL_ROWS = 192
DMA_ROWS = 256


SLOT_ROWS = 8


def _small_slots():
    slots, off = {}, 0
    for name, shape in SMALL.items():
        n = max(math.prod(shape), 1)
        slots[name] = (off, n)
        off += -(-n // (SLOT_ROWS * LANES)) * SLOT_ROWS * LANES
    assert off <= SMALL_ROWS * LANES
    return slots


def _pack_small(vals):
    parts, end = [], 0
    for name, (off, n) in _small_slots().items():
        pad = -(-n // (SLOT_ROWS * LANES)) * SLOT_ROWS * LANES - n
        v = vals[name].astype(jnp.float32).reshape(-1) if name in vals else jnp.zeros((n,), jnp.float32)
        parts += [v, jnp.zeros((pad,), jnp.float32)]
        end = off + n + pad
    parts.append(jnp.zeros((SMALL_ROWS * LANES - end,), jnp.float32))
    return jnp.concatenate(parts).reshape(SMALL_ROWS, LANES)


def _from_slot(name, rows):
    n = max(math.prod(SMALL[name]), 1)
    return rows.reshape(-1)[:n].reshape(SMALL[name])


def _pieces(rows):
    return [(off, min(DMA_ROWS, rows - off)) for off in range(0, rows, DMA_ROWS)]


HBM = pl.BlockSpec(memory_space=pltpu.HBM)
MESH = pl.DeviceIdType.MESH


def _place():
    x, y, c = lax.axis_index("x"), lax.axis_index("y"), lax.axis_index("c")
    chips = [(1 - x, y), (x, 1 - y), (1 - x, 1 - y)]
    return x, y, c, chips


def _all_gather(block):
    m_per, n = block.shape

    def body(x_ref, out_ref, send_sems, recv_sems, local_sem):
        x, y, c, chips = _place()
        me, sibling = (x, y, c), (x, y, 1 - c)

        def rows(px, py, pc):
            return out_ref.at[pl.ds((4 * px + 2 * py + pc) * m_per, m_per), :]

        def copy(k, blk, to, src=None):
            return pltpu.make_async_remote_copy(
                src_ref=rows(*blk) if src is None else src, dst_ref=rows(*blk),
                send_sem=send_sems.at[k], recv_sem=recv_sems.at[k], device_id=to, device_id_type=MESH)

        mine = pltpu.make_async_copy(x_ref, rows(*me), local_sem)
        mine.start()
        first = [copy(0, me, sibling, src=x_ref)]
        first += [copy(1 + j, me, (*chip, c), src=x_ref) for j, chip in enumerate(chips)]
        for cp in first:
            cp.start()
        passed = [copy(4 + j, (*chip, c), sibling) for j, chip in enumerate(chips)]
        for j, chip in enumerate(chips):
            copy(1 + j, (*chip, c), me).wait_recv()
            passed[j].start()
        copy(0, sibling, me).wait_recv()
        for j, chip in enumerate(chips):
            copy(4 + j, (*chip, 1 - c), me).wait_recv()
        for cp in first + passed:
            cp.wait_send()
        mine.wait()

    out = pl.pallas_call(
        body, name="all_gather_small",
        out_shape=jax.ShapeDtypeStruct((N_DEV * m_per, n), block.dtype),
        in_specs=[pl.BlockSpec(memory_space=pltpu.VMEM)],
        out_specs=pl.BlockSpec(memory_space=pltpu.VMEM),
        scratch_shapes=[pltpu.SemaphoreType.DMA((7,)), pltpu.SemaphoreType.DMA((7,)), pltpu.SemaphoreType.DMA],
    )(block)
    return out.reshape(N_DEV, m_per, n)


def _weight_gather(shards):
    nt = len(shards)

    def body(*refs):
        w_refs, out_refs = refs[:nt], refs[nt:2 * nt]
        send_sems, recv_sems = refs[2 * nt:]
        x, y, c, chips = _place()
        sibling = (x, y, 1 - c)

        def slab(t, px, py, half):
            rh = shards[t].shape[0] // 2
            return out_refs[t].at[2 * px + py, pl.ds(half * rh, rh), :]

        def copy(t, k, src, dst, to):
            return pltpu.make_async_remote_copy(src_ref=src, dst_ref=dst, send_sem=send_sems.at[6 * t + k],
                                                recv_sem=recv_sems.at[6 * t + k], device_id=to, device_id_type=MESH)

        first = []
        for t in range(nt):
            rh = shards[t].shape[0] // 2
            first += [copy(t, j, w_refs[t].at[pl.ds(c * rh, rh), :], slab(t, x, y, c), (*chip, c))
                      for j, chip in enumerate(chips)]
        for cp in first:
            cp.start()
        passed = []
        for t in range(nt):
            for j, chip in enumerate(chips):
                copy(t, j, slab(t, *chip, c), slab(t, *chip, c), (*chip, c)).wait_recv()
                rh = shards[t].shape[0] // 2
                for off, n in _pieces(rh):
                    piece = out_refs[t].at[2 * chip[0] + chip[1], pl.ds(c * rh + off, n), :]
                    copy(t, 3 + j, piece, piece, sibling).start()
                passed.append(copy(t, 3 + j, slab(t, *chip, c), slab(t, *chip, c), sibling))
        for t in range(nt):
            for j, chip in enumerate(chips):
                copy(t, 3 + j, slab(t, *chip, 1 - c), slab(t, *chip, 1 - c), sibling).wait_recv()
        for cp in first + passed:
            cp.wait_send()

    return pl.pallas_call(
        body, name="weight_gather",
        out_shape=[jax.ShapeDtypeStruct((N_CHIPS,) + s.shape, s.dtype) for s in shards],
        in_specs=[HBM] * nt, out_specs=[HBM] * nt,
        scratch_shapes=[pltpu.SemaphoreType.DMA((6 * nt,)), pltpu.SemaphoreType.DMA((6 * nt,))],
    )(*shards)


SEM = pl.BlockSpec(memory_space=pltpu.SEMAPHORE)
ANY = pl.BlockSpec(memory_space=pl.ANY)
SPLIT_COPY = pltpu.SideEffectType.DATAFLOW_SIDE_EFFECTING


def _late_copies(w_refs, land_refs, send_sems, recv_sems):
    x, y, c, chips = _place()
    return [pltpu.make_async_remote_copy(
        src_ref=w_refs[t], dst_ref=land_refs[t].at[2 * x + y], send_sem=send_sems.at[3 * t + j],
        recv_sem=recv_sems.at[3 * t + j], device_id=(cx, cy, c), device_id_type=MESH)
        for t in range(len(w_refs)) for j, (cx, cy) in enumerate(chips)], chips


def _late_gather_start(shards, after):
    nt, na = len(shards), len(after)

    def body(*refs):
        w_refs, land_refs = refs[:nt], refs[nt:2 * nt]
        send_sems, recv_sems, token = refs[2 * nt + na], refs[2 * nt + na + 1], refs[-1]
        copies, _ = _late_copies(w_refs, land_refs, send_sems, recv_sems)
        for cp in copies:
            cp.start()
        token[...] = jnp.zeros_like(token)

    hbm = lambda a: pltpu.with_memory_space_constraint(a, pltpu.HBM)
    lands = [lax.empty((N_CHIPS,) + s.shape, s.dtype) for s in shards]
    outs = pl.pallas_call(
        body, name="late_gather_start",
        out_shape=(pltpu.SemaphoreType.DMA((3 * nt,)), pltpu.SemaphoreType.DMA((3 * nt,)),
                   *[pltpu.HBM(s.shape, s.dtype) for s in shards], *[pltpu.HBM(l.shape, l.dtype) for l in lands],
                   jax.ShapeDtypeStruct((8, LANES), jnp.float32)),
        in_specs=[HBM] * (2 * nt) + [ANY] * na,
        out_specs=(SEM, SEM, *([HBM] * (2 * nt)), pl.BlockSpec(memory_space=pltpu.VMEM)),
        input_output_aliases={i: 2 + i for i in range(2 * nt)},
        compiler_params=pltpu.CompilerParams(has_side_effects=SPLIT_COPY),
    )(*[hbm(s) for s in shards], *[hbm(l) for l in lands], *after)
    return outs[0], outs[1], list(outs[2:2 + nt]), list(outs[2 + nt:2 + 2 * nt]), outs[-1]


def _late_gather_wait(send_sems, recv_sems, shards, lands, after):
    nt = len(shards)

    def body(*refs):
        w_refs, land_refs = refs[:nt], refs[nt:2 * nt]
        s_sems, r_sems = refs[2 * nt], refs[2 * nt + 1]
        x, y, c, chips = _place()
        for t in range(nt):
            for j, (cx, cy) in enumerate(chips):
                cp = pltpu.make_async_remote_copy(
                    src_ref=w_refs[t], dst_ref=land_refs[t].at[2 * cx + cy], send_sem=s_sems.at[3 * t + j],
                    recv_sem=r_sems.at[3 * t + j], device_id=(cx, cy, c), device_id_type=MESH)
                cp.wait_send()
                cp.wait_recv()

    outs = pl.pallas_call(
        body, name="late_gather_wait",
        out_shape=(*[pltpu.HBM(s.shape, s.dtype) for s in shards], *[pltpu.HBM(l.shape, l.dtype) for l in lands]),
        in_specs=[HBM] * (2 * nt) + [SEM, SEM, ANY], out_specs=tuple([HBM] * (2 * nt)),
        input_output_aliases={i: i for i in range(2 * nt)},
        compiler_params=pltpu.CompilerParams(has_side_effects=SPLIT_COPY),
    )(*shards, *lands, send_sems, recv_sems, after)
    return list(outs[nt:])


def _grad_pair_in(grads):
    nt = len(grads)

    def body(*refs):
        g_refs, got_refs = refs[:nt], refs[nt:2 * nt]
        send_sems, recv_sems = refs[2 * nt:]
        x, y, c, _ = _place()
        sibling = (x, y, 1 - c)

        def copy(t, src, dst):
            return pltpu.make_async_remote_copy(src_ref=src, dst_ref=dst, send_sem=send_sems.at[t],
                                                recv_sem=recv_sems.at[t], device_id=sibling, device_id_type=MESH)

        for t in range(nt):
            rh = grads[t].shape[1] // 2
            for p in range(N_CHIPS):
                for off, n in _pieces(rh):
                    copy(t, g_refs[t].at[p, pl.ds((1 - c) * rh + off, n), :], got_refs[t].at[p, pl.ds(off, n), :]).start()
        for t in range(nt):
            rh = grads[t].shape[1] // 2
            copy(t, g_refs[t].at[:, pl.ds((1 - c) * rh, rh), :], got_refs[t]).wait()

    return pl.pallas_call(
        body, name="grad_pair_in",
        out_shape=[jax.ShapeDtypeStruct((N_CHIPS, g.shape[1] // 2, g.shape[2]), g.dtype) for g in grads],
        in_specs=[HBM] * nt, out_specs=[HBM] * nt,
        scratch_shapes=[pltpu.SemaphoreType.DMA((nt,)), pltpu.SemaphoreType.DMA((nt,))],
    )(*grads)


def _pair_in_start(grads):
    nt = len(grads)

    def body(*refs):
        g_refs, land_refs = refs[:nt], refs[nt:2 * nt]
        send_sems, recv_sems, token = refs[2 * nt], refs[2 * nt + 1], refs[-1]
        x, y, c, _ = _place()
        for t in range(nt):
            rh = grads[t].shape[1] // 2
            for p in range(N_CHIPS):
                for off, n in _pieces(rh):
                    pltpu.make_async_remote_copy(
                        src_ref=g_refs[t].at[p, pl.ds((1 - c) * rh + off, n), :], dst_ref=land_refs[t].at[p, pl.ds(off, n), :],
                        send_sem=send_sems.at[t], recv_sem=recv_sems.at[t], device_id=(x, y, 1 - c),
                        device_id_type=MESH).start()
        token[...] = jnp.zeros_like(token)

    hbm = lambda a: pltpu.with_memory_space_constraint(a, pltpu.HBM)
    lands = [lax.empty((N_CHIPS, g.shape[1] // 2, g.shape[2]), g.dtype) for g in grads]
    outs = pl.pallas_call(
        body, name="grad_pair_in_start",
        out_shape=(pltpu.SemaphoreType.DMA((nt,)), pltpu.SemaphoreType.DMA((nt,)),
                   *[pltpu.HBM(g.shape, g.dtype) for g in grads], *[pltpu.HBM(l.shape, l.dtype) for l in lands],
                   jax.ShapeDtypeStruct((8, LANES), jnp.float32)),
        in_specs=[HBM] * (2 * nt),
        out_specs=(SEM, SEM, *([HBM] * (2 * nt)), pl.BlockSpec(memory_space=pltpu.VMEM)),
        input_output_aliases={i: 2 + i for i in range(2 * nt)},
        compiler_params=pltpu.CompilerParams(has_side_effects=SPLIT_COPY),
    )(*[hbm(g) for g in grads], *[hbm(l) for l in lands])
    return outs[0], outs[1], list(outs[2:2 + nt]), list(outs[2 + nt:2 + 2 * nt]), outs[-1]


def _pair_in_wait(send_sems, recv_sems, grads, lands, after):
    nt = len(grads)

    def body(*refs):
        g_refs, land_refs = refs[:nt], refs[nt:2 * nt]
        s_sems, r_sems = refs[2 * nt], refs[2 * nt + 1]
        x, y, c, _ = _place()
        for t in range(nt):
            rh = grads[t].shape[1] // 2
            cp = pltpu.make_async_remote_copy(
                src_ref=g_refs[t].at[:, pl.ds((1 - c) * rh, rh), :], dst_ref=land_refs[t], send_sem=s_sems.at[t],
                recv_sem=r_sems.at[t], device_id=(x, y, 1 - c), device_id_type=MESH)
            cp.wait_send()
            cp.wait_recv()

    outs = pl.pallas_call(
        body, name="grad_pair_in_wait",
        out_shape=(*[pltpu.HBM(g.shape, g.dtype) for g in grads], *[pltpu.HBM(l.shape, l.dtype) for l in lands]),
        in_specs=[HBM] * (2 * nt) + [SEM, SEM, ANY], out_specs=tuple([HBM] * (2 * nt)),
        input_output_aliases={i: i for i in range(2 * nt)},
        compiler_params=pltpu.CompilerParams(has_side_effects=SPLIT_COPY),
    )(*grads, *lands, send_sems, recv_sems, after)
    return list(outs[:nt]), list(outs[nt:])


def _pair_sum(g, got, core, name):
    _, rows, cols = g.shape
    rh = rows // 2
    tr = _tile(rh, 512)
    nb = rh // tr

    def body(c_ref, g_ref, got_ref, s32_ref, s16_ref):
        s = g_ref[...] + got_ref[...]
        s32_ref[...] = s
        s16_ref[...] = s.astype(s16_ref.dtype)

    blk = pl.BlockSpec((None, tr, cols), lambda p, i, c_ref: (p, i, 0))
    return pl.pallas_call(
        body, name=name,
        grid_spec=pltpu.PrefetchScalarGridSpec(
            num_scalar_prefetch=1, grid=(N_CHIPS, nb),
            in_specs=[pl.BlockSpec((None, tr, cols), lambda p, i, c_ref: (p, c_ref[0] * nb + i, 0)), blk],
            out_specs=[blk, blk]),
        out_shape=[jax.ShapeDtypeStruct((N_CHIPS, rh, cols), jnp.float32),
                   jax.ShapeDtypeStruct((N_CHIPS, rh, cols), jnp.bfloat16)],
        compiler_params=_params("parallel", "parallel"),
    )(core, g, got)


def _exchange_start(parts, name):
    nt = len(parts)

    def body(*refs):
        a_refs, land_refs = refs[:nt], refs[nt:2 * nt]
        send_sems, recv_sems, token = refs[2 * nt], refs[2 * nt + 1], refs[-1]
        x, y, c, chips = _place()
        for t in range(nt):
            for j, (cx, cy) in enumerate(chips):
                pltpu.make_async_remote_copy(
                    src_ref=a_refs[t].at[2 * cx + cy], dst_ref=land_refs[t].at[j], send_sem=send_sems.at[3 * t + j],
                    recv_sem=recv_sems.at[3 * t + j], device_id=(cx, cy, c), device_id_type=MESH).start()
        token[...] = jnp.zeros_like(token)

    hbm = lambda a: pltpu.with_memory_space_constraint(a, pltpu.HBM)
    lands = [lax.empty((N_CHIPS - 1,) + a.shape[1:], a.dtype) for a in parts]
    outs = pl.pallas_call(
        body, name=name,
        out_shape=(pltpu.SemaphoreType.DMA((3 * nt,)), pltpu.SemaphoreType.DMA((3 * nt,)),
                   *[pltpu.HBM(a.shape, a.dtype) for a in parts], *[pltpu.HBM(l.shape, l.dtype) for l in lands],
                   jax.ShapeDtypeStruct((8, LANES), jnp.float32)),
        in_specs=[HBM] * (2 * nt),
        out_specs=(SEM, SEM, *([HBM] * (2 * nt)), pl.BlockSpec(memory_space=pltpu.VMEM)),
        input_output_aliases={i: 2 + i for i in range(2 * nt)},
        compiler_params=pltpu.CompilerParams(has_side_effects=SPLIT_COPY),
    )(*[hbm(a) for a in parts], *[hbm(l) for l in lands])
    return outs[0], outs[1], list(outs[2:2 + nt]), list(outs[2 + nt:2 + 2 * nt]), outs[-1]


def _exchange_wait(send_sems, recv_sems, parts, lands, after, name):
    nt = len(parts)

    def body(*refs):
        a_refs, land_refs = refs[:nt], refs[nt:2 * nt]
        s_sems, r_sems = refs[2 * nt], refs[2 * nt + 1]
        x, y, c, chips = _place()
        for t in range(nt):
            for j, (cx, cy) in enumerate(chips):
                cp = pltpu.make_async_remote_copy(
                    src_ref=a_refs[t].at[2 * cx + cy], dst_ref=land_refs[t].at[j], send_sem=s_sems.at[3 * t + j],
                    recv_sem=r_sems.at[3 * t + j], device_id=(cx, cy, c), device_id_type=MESH)
                cp.wait_send()
                cp.wait_recv()

    outs = pl.pallas_call(
        body, name=name,
        out_shape=(*[pltpu.HBM(a.shape, a.dtype) for a in parts], *[pltpu.HBM(l.shape, l.dtype) for l in lands]),
        in_specs=[HBM] * (2 * nt) + [SEM, SEM, ANY], out_specs=tuple([HBM] * (2 * nt)),
        input_output_aliases={i: i for i in range(2 * nt)},
        compiler_params=pltpu.CompilerParams(has_side_effects=SPLIT_COPY),
    )(*parts, *lands, send_sems, recv_sems, after)
    return list(outs[nt:])


def _chip_sum(s32, got, chip, name, behind=None):
    _, rh, cols = s32.shape
    tr = _tile(rh, 512)

    def body(p_ref, s_ref, got_ref, *refs):
        acc = s_ref[...]
        for j in range(N_CHIPS - 1):
            acc = acc + got_ref[j].astype(jnp.float32)
        refs[-1][...] = acc

    extra = [] if behind is None else [behind]
    return pl.pallas_call(
        body, name=name,
        grid_spec=pltpu.PrefetchScalarGridSpec(
            num_scalar_prefetch=1, grid=(rh // tr,),
            in_specs=[pl.BlockSpec((None, tr, cols), lambda i, p_ref: (p_ref[0], i, 0)),
                      pl.BlockSpec((N_CHIPS - 1, tr, cols), lambda i, p_ref: (0, i, 0))]
            + [pl.BlockSpec((8, LANES), lambda i, p_ref: (0, 0))] * len(extra),
            out_specs=pl.BlockSpec((tr, cols), lambda i, p_ref: (i, 0))),
        out_shape=jax.ShapeDtypeStruct((rh, cols), jnp.float32),
        compiler_params=_params("parallel"),
    )(chip, s32, got, *extra)


def _grad_pair_out(halves):
    nt = len(halves)

    def body(*refs):
        h_refs, got_refs = refs[:nt], refs[nt:2 * nt]
        send_sems, recv_sems = refs[2 * nt:]
        x, y, c, _ = _place()
        sibling = (x, y, 1 - c)

        def copy(t, src, dst):
            return pltpu.make_async_remote_copy(src_ref=src, dst_ref=dst, send_sem=send_sems.at[t],
                                                recv_sem=recv_sems.at[t], device_id=sibling, device_id_type=MESH)

        for t in range(nt):
            for off, n in _pieces(halves[t].shape[0]):
                copy(t, h_refs[t].at[pl.ds(off, n), :], got_refs[t].at[pl.ds(off, n), :]).start()
        for t in range(nt):
            copy(t, h_refs[t], got_refs[t]).wait()

    return pl.pallas_call(
        body, name="grad_pair_out",
        out_shape=[jax.ShapeDtypeStruct(h.shape, h.dtype) for h in halves],
        in_specs=[HBM] * nt, out_specs=[HBM] * nt,
        scratch_shapes=[pltpu.SemaphoreType.DMA((nt,)), pltpu.SemaphoreType.DMA((nt,))],
    )(*halves)


def _ada_part(c_all, w_ada):
    L, _, ncol = w_ada.shape
    tn = _tile(ncol, 512)

    def body(c_ref, w_ref, cond_ref, part_ref):
        cv = c_ref[...]
        cond = cv * jax.nn.sigmoid(cv)
        cond_ref[...] = cond
        part_ref[0] = jnp.dot(cond, w_ref[0], precision=lax.Precision.HIGHEST, preferred_element_type=jnp.float32)

    return pl.pallas_call(
        body, name="ada_part", grid=(L, ncol // tn),
        in_specs=[_full((N_DEV, D)), pl.BlockSpec((1, D, tn), lambda l, j: (l, 0, j))],
        out_specs=[_full((N_DEV, D)), pl.BlockSpec((1, N_DEV, tn), lambda l, j: (l, 0, j))],
        out_shape=[jax.ShapeDtypeStruct((N_DEV, D), jnp.float32), jax.ShapeDtypeStruct((L, N_DEV, ncol), jnp.float32)],
        compiler_params=_params("arbitrary", "arbitrary"),
    )(c_all, w_ada)


def _adamw_math(w, g, m, v):
    m = ADAM_B1 * m + (1.0 - ADAM_B1) * g
    v = ADAM_B2 * v + (1.0 - ADAM_B2) * jnp.square(g)
    m_hat = m / (1.0 - ADAM_B1 ** ADAM_STEP)
    v_hat = v / (1.0 - ADAM_B2 ** ADAM_STEP)
    delta = -ADAM_LR * (m_hat / (jnp.sqrt(v_hat) + ADAM_EPS) + ADAM_WD * w)
    return delta, m, v


def _adamw(w, g, m, v, name):
    shape = w.shape
    cols = shape[-1]
    rows = math.prod(shape[:-1])
    tr = _tile(rows, 512)
    two_d = lambda t: t.reshape(rows, cols)

    def body(w_ref, g_ref, m_ref, v_ref, d_ref, mo_ref, vo_ref):
        d_ref[...], mo_ref[...], vo_ref[...] = _adamw_math(w_ref[...], g_ref[...], m_ref[...], v_ref[...])

    out = jax.ShapeDtypeStruct((rows, cols), jnp.float32)
    outs = pl.pallas_call(
        body, name=name, grid=(rows // tr,), in_specs=[_rows(tr, cols)] * 4, out_specs=[_rows(tr, cols)] * 3,
        out_shape=[out, out, out], compiler_params=_params("parallel"),
    )(two_d(w), two_d(g), two_d(m), two_d(v))
    return [t.reshape(shape) for t in outs]


def _adamw_halves(w, mine, got, m, v, core, name):
    shape = w.shape
    cols = shape[-1]
    rows = math.prod(shape[:-1])
    rh = rows // 2
    tr = _tile(rh, 512)
    nbh = rh // tr
    two_d = lambda t: t.reshape(rows, cols)

    def body(c_ref, w_ref, a_ref, b_ref, m_ref, v_ref, g_ref, d_ref, mo_ref, vo_ref):
        g = jnp.where(pl.program_id(0) // nbh == c_ref[0], a_ref[...], b_ref[...])
        g_ref[...] = g
        d_ref[...], mo_ref[...], vo_ref[...] = _adamw_math(w_ref[...], g, m_ref[...], v_ref[...])

    row = pl.BlockSpec((tr, cols), lambda i, c_ref: (i, 0))

    def half(keep):
        return pl.BlockSpec((tr, cols), lambda i, c_ref: (jnp.where((i // nbh == c_ref[0]) == keep, i % nbh, 0), 0))

    out = jax.ShapeDtypeStruct((rows, cols), jnp.float32)
    outs = pl.pallas_call(
        body, name=name,
        grid_spec=pltpu.PrefetchScalarGridSpec(
            num_scalar_prefetch=1, grid=(rows // tr,),
            in_specs=[row, half(True), half(False), row, row], out_specs=[row] * 4),
        out_shape=[out] * 4, compiler_params=_params("arbitrary"),
    )(core, two_d(w), mine, got, two_d(m), two_d(v))
    return [t.reshape(shape) for t in outs]


def _ada_grad_adamw(cond_t, dm, w, m, v):
    L, _, ncol = w.shape
    tn = _tile(ncol, 512)

    def body(ct_ref, dm_ref, w_ref, m_ref, v_ref, g_ref, d_ref, mo_ref, vo_ref):
        g = ct_ref[:, 0:1] * dm_ref[0, 0:1, :]
        for b in range(1, N_DEV):
            g = g + ct_ref[:, b:b + 1] * dm_ref[0, b:b + 1, :]
        g_ref[0] = g
        d_ref[0], mo_ref[0], vo_ref[0] = _adamw_math(w_ref[0], g, m_ref[0], v_ref[0])

    wblk = pl.BlockSpec((1, D, tn), lambda l, j: (l, 0, j))
    out = jax.ShapeDtypeStruct(w.shape, jnp.float32)
    return pl.pallas_call(
        body, name="ada_grad_adamw", grid=(L, ncol // tn),
        in_specs=[_full((D, N_DEV)), pl.BlockSpec((1, N_DEV, tn), lambda l, j: (l, 0, j)), wblk, wblk, wblk],
        out_specs=[wblk] * 4, out_shape=[out] * 4, compiler_params=_params("parallel", "parallel"),
    )(cond_t, dm, w, m, v)


def _small_adamw(gathered, w, m, v):
    slots = _small_slots()
    rows = {name: (off // LANES, -(-n // LANES)) for name, (off, n) in slots.items()}
    kinds = {name: 1 if name == "loss" or name in BIASES else 4 for name in slots}

    def body(ga_ref, w_ref, m_ref, v_ref, *out_refs):
        g = ga_ref[0]
        for dev in range(1, N_DEV):
            g = g + ga_ref[dev]
        d, mo, vo = _adamw_math(w_ref[...], g, m_ref[...], v_ref[...])
        k = 0
        for name, (r0, nr) in rows.items():
            for src in (g, d, mo, vo)[:kinds[name]]:
                out_refs[k][...] = src[r0:r0 + nr, :]
                k += 1

    out_shape = [jax.ShapeDtypeStruct((rows[name][1], LANES), jnp.float32) for name in slots for _ in range(kinds[name])]
    flat = pl.pallas_call(
        body, name="small_adamw", out_shape=out_shape,
        in_specs=[pl.BlockSpec(memory_space=pltpu.VMEM)] * 4,
        out_specs=[pl.BlockSpec(memory_space=pltpu.VMEM)] * len(out_shape),
    )(gathered, w, m, v)
    out, k = {}, 0
    for name in slots:
        out[name] = [_from_slot(name, t) for t in flat[k:k + kinds[name]]]
        k += kinds[name]
    return out


def _one_hot_pick(arr, index, axis):
    n = arr.shape[axis]
    shape = [1] * arr.ndim
    shape[axis] = n
    hot = (jnp.arange(n) == index).astype(arr.dtype).reshape(shape)
    return jnp.sum(arr * hot, axis=axis)


def kernel(x, c, positions, w_ada, b_ada, g_mix, g_mlp, mla_w_dq, mla_g_q, mla_w_uq, mla_w_dkv, mla_g_kv, mla_w_ukv, mla_w_o, swa_w_qkv, swa_b_qkv, swa_sinks, swa_w_o, swa_b_o, w_ff1, w_ff2, g_final, loss_target, m_w_ada, m_b_ada, m_g_mix, m_g_mlp, m_mla_w_dq, m_mla_g_q, m_mla_w_uq, m_mla_w_dkv, m_mla_g_kv, m_mla_w_ukv, m_mla_w_o, m_swa_w_qkv, m_swa_b_qkv, m_swa_sinks, m_swa_w_o, m_swa_b_o, m_w_ff1, m_w_ff2, m_g_final, v_w_ada, v_b_ada, v_g_mix, v_g_mlp, v_mla_w_dq, v_mla_g_q, v_mla_w_uq, v_mla_w_dkv, v_mla_g_kv, v_mla_w_ukv, v_mla_w_o, v_swa_w_qkv, v_swa_b_qkv, v_swa_sinks, v_swa_w_o, v_swa_b_o, v_w_ff1, v_w_ff2, v_g_final):
    W = dict(w_ada=w_ada, b_ada=b_ada, g_mix=g_mix, g_mlp=g_mlp, mla_w_dq=mla_w_dq, mla_g_q=mla_g_q, mla_w_uq=mla_w_uq,
             mla_w_dkv=mla_w_dkv, mla_g_kv=mla_g_kv, mla_w_ukv=mla_w_ukv, mla_w_o=mla_w_o, swa_w_qkv=swa_w_qkv,
             swa_b_qkv=swa_b_qkv, swa_sinks=swa_sinks, swa_w_o=swa_w_o, swa_b_o=swa_b_o, w_ff1=w_ff1, w_ff2=w_ff2,
             g_final=g_final)
    M = dict(w_ada=m_w_ada, b_ada=m_b_ada, g_mix=m_g_mix, g_mlp=m_g_mlp, mla_w_dq=m_mla_w_dq, mla_g_q=m_mla_g_q,
             mla_w_uq=m_mla_w_uq, mla_w_dkv=m_mla_w_dkv, mla_g_kv=m_mla_g_kv, mla_w_ukv=m_mla_w_ukv, mla_w_o=m_mla_w_o,
             swa_w_qkv=m_swa_w_qkv, swa_b_qkv=m_swa_b_qkv, swa_sinks=m_swa_sinks, swa_w_o=m_swa_w_o, swa_b_o=m_swa_b_o,
             w_ff1=m_w_ff1, w_ff2=m_w_ff2, g_final=m_g_final)
    V = dict(w_ada=v_w_ada, b_ada=v_b_ada, g_mix=v_g_mix, g_mlp=v_g_mlp, mla_w_dq=v_mla_w_dq, mla_g_q=v_mla_g_q,
             mla_w_uq=v_mla_w_uq, mla_w_dkv=v_mla_w_dkv, mla_g_kv=v_mla_g_kv, mla_w_ukv=v_mla_w_ukv, mla_w_o=v_mla_w_o,
             swa_w_qkv=v_swa_w_qkv, swa_b_qkv=v_swa_b_qkv, swa_sinks=v_swa_sinks, swa_w_o=v_swa_w_o, swa_b_o=v_swa_b_o,
             w_ff1=v_w_ff1, w_ff2=v_w_ff2, g_final=v_g_final)
    order = list(W)
    names = list(SHARDED)
    core = lax.axis_index("c")
    chip = 2 * lax.axis_index("x") + lax.axis_index("y")
    dev = 2 * chip + core
    core_arr = core.astype(jnp.int32).reshape(1)
    chip_arr = chip.astype(jnp.int32).reshape(1)

    def whole(n, g, own):
        g = lax.dynamic_update_slice(g, own[None], (chip, 0, 0))
        if n in ("w_ff1", "w_ff2"):
            return g
        if n in COL_SPLIT:
            return g.transpose(1, 0, 2).reshape(g.shape[1], N_CHIPS * g.shape[2])
        return g.reshape(N_CHIPS * g.shape[1], g.shape[2])

    early = [n for n in names if n.startswith("mla_")]
    local = {n: W[n].astype(MXU_DTYPE).reshape(_view2d(n)) for n in early}
    wts = {n: whole(n, g, local[n]) for n, g in zip(early, _weight_gather([local[n] for n in early]))}

    nbq, nbo = BIASES["swa_b_qkv"] // N_CHIPS, BIASES["swa_b_o"] // N_CHIPS
    first = jnp.concatenate([c.reshape(-1), swa_b_qkv.reshape(-1), swa_b_o.reshape(-1),
                             jnp.zeros((16 * LANES - D - nbq - nbo,), jnp.float32)]).reshape(16, LANES)
    first_all = _all_gather(first).reshape(N_DEV, 16 * LANES)
    c_all = first_all[:, :D]
    south = first_all[0::2]
    wts["swa_b_qkv"] = south[:, D:D + nbq].reshape(1, N_CHIPS * nbq)
    wts["swa_b_o"] = south[:, D + nbq:D + nbq + nbo].reshape(1, N_CHIPS * nbo)
    cond_all, part = _ada_part(c_all, w_ada)
    ncol = w_ada.shape[2]
    part_all = _all_gather(part.reshape(-1, LANES)).reshape(N_DEV, DEPTH, N_DEV, ncol)
    mine = _one_hot_pick(part_all[0::2], dev, axis=2)
    mod = mine.transpose(1, 0, 2).reshape(DEPTH, N_CHIPS * ncol) + b_ada
    vecs = jnp.concatenate([mod.reshape(DEPTH, 6, D), g_mix[:, None, :], g_mlp[:, None, :]], axis=1)

    late = [("w_ff1", 0), ("w_ff2", 0), ("swa_w_qkv", None), ("swa_w_o", None), ("w_ff1", 1), ("w_ff2", 1)]
    late_local = [(W[n][0] if l is None else W[n][l]).astype(MXU_DTYPE) for n, l in late]
    send_sems, recv_sems, passed, lands, token = _late_gather_start(late_local, [vecs] + [wts[n] for n in early])

    def late_weights(after):
        got = _late_gather_wait(send_sems, recv_sems, passed, lands, after)
        out = {"w_ff1": [None] * DEPTH, "w_ff2": [None] * DEPTH}
        for (n, l), g, own in zip(late, got, late_local):
            if l is None:
                out[n] = whole(n, g, own)
            else:
                out[n][l] = whole(n, g, own)
        return out

    late_names = [n for n in names if n not in early]
    reduce_state = {}

    def on_late_grads(late_grads):
        s_sems, r_sems, passed_g, zones, tok = _pair_in_start([late_grads[n] for n in late_names])
        reduce_state.update(pair=(s_sems, r_sems, passed_g, zones))
        return tok

    def on_late_landed(after):
        gl, got = _pair_in_wait(*reduce_state["pair"], after)
        sums = [_pair_sum(g, s, core_arr, "pair_sum_" + n) for n, g, s in zip(late_names, gl, got)]
        s_sems, r_sems, parts, zones, tok = _exchange_start([s16 for _, s16 in sums], "grad_exchange_start")
        reduce_state.update(sums=sums, split=(s_sems, r_sems, parts, zones))
        return tok

    grad_x, grads, small = _sequence_step(
        x[0], loss_target[0], positions[0], vecs, mla_g_q + token[0, 0], mla_g_kv, swa_sinks, g_final, wts,
        late_weights, on_late_grads, on_late_landed)

    small["b_ada"] = small.pop("dmod")
    small_all = _all_gather(_pack_small(small))
    pk = lambda src: _pack_small({n: src[n] for n in SMALL if n != "loss" and n not in BIASES})
    off, n = _small_slots()["b_ada"]
    dmod_all = small_all.reshape(N_DEV, -1)[:, off:off + n].reshape(N_DEV, DEPTH, N_CHIPS, ncol)
    dm = _one_hot_pick(dmod_all, chip, axis=2).transpose(1, 0, 2)

    gl = [grads[n] for n in early]
    got = _grad_pair_in(gl)
    sums = [_pair_sum(g, s, core_arr, "pair_sum_" + n) for n, g, s in zip(early, gl, got)]
    e_sems, e_rems, e_parts, e_zones, e_tok = _exchange_start([s16 for _, s16 in sums], "mla_exchange_start")

    def finish(tensor_names, sums, others, behind):
        halves = [_chip_sum(s32, o, chip_arr, "chip_sum_" + n, behind) for n, (s32, _), o in zip(tensor_names, sums, others)]
        return {n: _adamw_halves(W[n], mine_h, got_h, M[n], V[n], core_arr, "adamw_" + n)
                for n, mine_h, got_h in zip(tensor_names, halves, _grad_pair_out(halves))}

    late_others = _exchange_wait(*reduce_state["split"], grad_x, "grad_exchange_wait")
    res = finish(late_names, reduce_state["sums"], late_others, e_tok)
    res["w_ada"] = _ada_grad_adamw(cond_all.T, dm, w_ada, m_w_ada, v_w_ada)
    small_res = _small_adamw(small_all, pk(W), pk(M), pk(V))
    early_others = _exchange_wait(e_sems, e_rems, e_parts, e_zones, res["w_ff2"][1], "mla_exchange_wait")
    res.update(finish(early, sums, early_others, None))

    for n, width in BIASES.items():
        g = _one_hot_pick(small_res[n][0].reshape(N_CHIPS, width // N_CHIPS), chip, axis=0).reshape(1, -1)
        res[n] = [g] + _adamw(W[n], g, M[n], V[n], "adamw_" + n)
    for name in order:
        if name not in res:
            res[name] = small_res[name]
    outs = [small_res["loss"][0], grad_x[None]]
    for k in range(4):
        outs += [res[name][k] for name in order]
    return tuple(outs)
```

```python
import functools
import math

import jax
import jax.numpy as jnp
import numpy as np
from jax import lax
from jax.experimental import pallas as pl
from jax.experimental.pallas import tpu as pltpu

D = 1024
DEPTH = 2
MLA_HEADS = 8
QK_NOPE = 128
QK_ROPE = 64
V_DIM = 128
Q_LORA = 384
KV_LORA = 256
ROPE_THETA = 10000.0
SWA_HEADS = 16
SWA_KV_HEADS = 4
SWA_HEAD_DIM = 64
SWA_GROUP = SWA_HEADS // SWA_KV_HEADS
WINDOW = 128
D_FF = 4 * D
EPS = 1e-6
ADAM_LR = 0.001
ADAM_B1 = 0.9
ADAM_B2 = 0.999
ADAM_EPS = 1e-08
ADAM_WD = 0.01
ADAM_STEP = 10

N_CHIPS = 4
N_DEV = 8
LANES = 128
QK_EXT = 256
MLA_SCALE = (QK_NOPE + QK_ROPE) ** -0.5
LOG2E = math.log2(math.e)
LN2 = math.log(2.0)
MLA_QSCALE = MLA_SCALE * LOG2E
ATTN_BLOCK = 2048
ATTN_SUB = 512
MLP_FWD_TILE = (1024, 1024)
MLP_BWD_TILE = (1024, 1024)
DW_TOKENS = 4096
ROW_TILE = 1024
SWA_SCALE = SWA_HEAD_DIM ** -0.5
NEG = -1e30
MXU_DTYPE = jnp.bfloat16
VMEM_LIMIT = 56 * 1024 * 1024

R_SH1, R_SC1, R_GT1, R_SH2, R_SC2, R_GT2, R_GMIX, R_GMLP = range(8)
R_BO = 6


def _tile(n, pref):
    if n <= pref:
        return n
    for t in range(pref, 7, -1):
        if n % t == 0 and t % 8 == 0:
            return t
    return n


def _dot(a, b):
    return jnp.dot(a, b, preferred_element_type=jnp.float32)


def _dot_nt(a, b):
    return lax.dot_general(a, b, (((1,), (1,)), ((), ())), preferred_element_type=jnp.float32)


def _dot_tn(a, b):
    return lax.dot_general(a, b, (((0,), (0,)), ((), ())), preferred_element_type=jnp.float32)


def _rms(x):
    r = lax.rsqrt(jnp.mean(x * x, axis=-1, keepdims=True) + EPS)
    return x * r, r


def _rms_bwd(dxhat, xhat, r):
    return r * (dxhat - xhat * jnp.mean(dxhat * xhat, axis=-1, keepdims=True))


def _rowsum(v):
    return jnp.sum(v, axis=0, keepdims=True)


def _params(*sem):
    return pltpu.CompilerParams(dimension_semantics=sem, vmem_limit_bytes=VMEM_LIMIT)


def _full(shape):
    nd = len(shape)
    return pl.BlockSpec(shape, lambda *_: (0,) * nd)


def _rows(tm, cols):
    return pl.BlockSpec((tm, cols), lambda i, *_: (i, 0))


def _modulate_bwd(dh, x, vec_ref, r_g, r_sc, r_sh, ps_ref, dres):
    xhat, r = _rms(x)
    g = vec_ref[r_g:r_g + 1, :]
    n = xhat * g
    ps_ref[r_sh:r_sh + 1, :] += _rowsum(dh)
    ps_ref[r_sc:r_sc + 1, :] += _rowsum(dh * n)
    dn = dh * (1.0 + vec_ref[r_sc:r_sc + 1, :])
    ps_ref[r_g:r_g + 1, :] += _rowsum(dn * xhat)
    return dres + _rms_bwd(dn * g, xhat, r)


def _mla_pre(x, vec, wcat, g_q, g_kv, wuq, wukv, cs):
    T = x.shape[0]
    tm = _tile(T, 512)
    H = MLA_HEADS

    def body(x_ref, vec_ref, wcat_ref, gq_ref, gkv_ref, wuq_ref, wukv_ref, cs_ref, h_ref, z_ref, q_ref, k_ref, v_ref):
        xhat, _ = _rms(x_ref[...])
        h = xhat * vec_ref[R_GMIX:R_GMIX + 1, :] * (1.0 + vec_ref[R_SC1:R_SC1 + 1, :]) + vec_ref[R_SH1:R_SH1 + 1, :]
        hb = h.astype(MXU_DTYPE)
        h_ref[...] = hb
        z = _dot(hb, wcat_ref[...])
        z_ref[...] = z
        cq = (_rms(z[:, :Q_LORA])[0] * gq_ref[...]).astype(MXU_DTYPE)
        ckv = (_rms(z[:, Q_LORA:Q_LORA + KV_LORA])[0] * gkv_ref[...]).astype(MXU_DTYPE)
        cs_t = cs_ref[...]
        t = z[:, Q_LORA + KV_LORA:] * cs_t
        k_rope = (t + pltpu.roll(t, QK_ROPE, axis=1)).astype(MXU_DTYPE)
        low = lax.broadcasted_iota(jnp.int32, (1, LANES), 1) < QK_ROPE
        for hd in range(H):
            qf = _dot(cq, wuq_ref[hd])
            tq = qf[:, QK_NOPE:] * cs_t
            tq = tq + pltpu.roll(tq, QK_ROPE, axis=1)
            q_ref[hd, :, :QK_NOPE] = (qf[:, :QK_NOPE] * MLA_QSCALE).astype(MXU_DTYPE)
            q_ref[hd, :, QK_NOPE:] = jnp.where(low, tq * MLA_QSCALE, 0.0).astype(MXU_DTYPE)
            kvf = _dot(ckv, wukv_ref[hd])
            k_ref[hd, :, :QK_NOPE] = kvf[:, :QK_NOPE].astype(MXU_DTYPE)
            k_ref[hd, :, QK_NOPE:] = k_rope
            v_ref[hd] = kvf[:, QK_NOPE:].astype(MXU_DTYPE)

    zc = wcat.shape[1]
    return pl.pallas_call(
        body, name="mla_pre", grid=(T // tm,),
        in_specs=[_rows(tm, D), _full((8, D)), _full(wcat.shape), _full(g_q.shape), _full(g_kv.shape),
                  _full(wuq.shape), _full(wukv.shape), _rows(tm, LANES)],
        out_specs=[_rows(tm, D), _rows(tm, zc),
                   pl.BlockSpec((H, tm, QK_EXT), lambda i: (0, i, 0)),
                   pl.BlockSpec((H, tm, QK_EXT), lambda i: (0, i, 0)),
                   pl.BlockSpec((H, tm, V_DIM), lambda i: (0, i, 0))],
        out_shape=[jax.ShapeDtypeStruct((T, D), MXU_DTYPE), jax.ShapeDtypeStruct((T, zc), jnp.float32),
                   jax.ShapeDtypeStruct((H, T, QK_EXT), MXU_DTYPE), jax.ShapeDtypeStruct((H, T, QK_EXT), MXU_DTYPE),
                   jax.ShapeDtypeStruct((H, T, V_DIM), MXU_DTYPE)],
        compiler_params=_params("parallel"),
    )(x, vec, wcat, g_q, g_kv, wuq, wukv, cs)


def _mla_attn_fwd(q, k, v):
    H, T, _ = q.shape
    tb = _tile(T, ATTN_BLOCK)
    sub = min(ATTN_SUB, tb)
    ns, nb = tb // sub, T // tb
    pairs = [(i, j) for i in range(nb) for j in range(i + 1)]
    qi_tab = jnp.asarray([i for i, _ in pairs], jnp.int32)
    kj_tab = jnp.asarray([j for _, j in pairs], jnp.int32)

    def body(qi_ref, kj_ref, q_ref, k_ref, v_ref, o_ref, lse_ref, m_sc, l_sc, acc_sc):
        qi, kj = qi_ref[pl.program_id(1)], kj_ref[pl.program_id(1)]

        @pl.when(kj == 0)
        def _():
            m_sc[...] = jnp.full_like(m_sc, NEG)
            l_sc[...] = jnp.zeros_like(l_sc)
            acc_sc[...] = jnp.zeros_like(acc_sc)

        def update(r, kk, masked):
            rows, keys = pl.ds(r * sub, sub), pl.ds(kk * sub, sub)
            s = _dot_nt(q_ref[0, rows, :], k_ref[0, keys, :])
            if masked:
                row = lax.broadcasted_iota(jnp.int32, (sub, sub), 0)
                col = lax.broadcasted_iota(jnp.int32, (sub, sub), 1)
                s = jnp.where(col <= row, s, NEG)
            m_prev = m_sc[rows, :]
            m_new = jnp.maximum(m_prev, jnp.max(s, axis=1, keepdims=True))
            alpha = jnp.exp2(m_prev - m_new)
            p = jnp.exp2(s - jnp.tile(m_new, (1, sub // LANES)))
            l_sc[rows, :] = alpha * l_sc[rows, :] + jnp.sum(p, axis=1, keepdims=True)
            acc_sc[rows, :] = alpha * acc_sc[rows, :] + _dot(p.astype(MXU_DTYPE), v_ref[0, keys, :])
            m_sc[rows, :] = m_new

        @pl.when(kj < qi)
        def _():
            for kk in range(ns):
                for r in range(ns):
                    update(r, kk, False)

        @pl.when(kj == qi)
        def _():
            for kk in range(ns):
                for r in range(kk, ns):
                    update(r, kk, r == kk)
            l = l_sc[...]
            o_ref[...] = (acc_sc[...] / l).astype(o_ref.dtype)
            lse = m_sc[...] + jnp.log2(l)
            pick = (lax.broadcasted_iota(jnp.int32, (8, LANES), 1) == 0).astype(jnp.float32)
            row = lax.dot_general(pick, lse, (((1,), (1,)), ((), ())), precision=lax.Precision.HIGHEST,
                                  preferred_element_type=jnp.float32)
            lse_ref[0] = row[0:1, :]

    q_idx = lambda h, p, qi_ref, kj_ref: (h, qi_ref[p], 0)
    kv_idx = lambda h, p, qi_ref, kj_ref: (h, kj_ref[p], 0)
    return pl.pallas_call(
        body, name="mla_attn_fwd",
        grid_spec=pltpu.PrefetchScalarGridSpec(
            num_scalar_prefetch=2, grid=(H, len(pairs)),
            in_specs=[pl.BlockSpec((1, tb, QK_EXT), q_idx), pl.BlockSpec((1, tb, QK_EXT), kv_idx),
                      pl.BlockSpec((1, tb, V_DIM), kv_idx)],
            out_specs=[pl.BlockSpec((tb, V_DIM), lambda h, p, qi_ref, kj_ref: (qi_ref[p], h)),
                       pl.BlockSpec((1, 1, tb), lambda h, p, qi_ref, kj_ref: (h, 0, qi_ref[p]))],
            scratch_shapes=[pltpu.VMEM((tb, LANES), jnp.float32), pltpu.VMEM((tb, LANES), jnp.float32),
                            pltpu.VMEM((tb, V_DIM), jnp.float32)]),
        out_shape=[jax.ShapeDtypeStruct((T, H * V_DIM), MXU_DTYPE), jax.ShapeDtypeStruct((H, 1, T), jnp.float32)],
        compiler_params=_params("parallel", "arbitrary"),
    )(qi_tab, kj_tab, q, k, v)


def _post_attn(o, x, w_o, bias, vec, o_transposed=False):
    T = x.shape[0]
    tm = _tile(T, ROW_TILE)
    o_spec = pl.BlockSpec((D, tm), lambda i: (0, i)) if o_transposed else _rows(tm, D)

    def body(o_ref, x_ref, w_ref, b_ref, vec_ref, y_ref, xm_ref, h_ref):
        y = (_dot_tn if o_transposed else _dot)(o_ref[...], w_ref[...]) + b_ref[...]
        y_ref[...] = y.astype(y_ref.dtype)
        xm = x_ref[...] + vec_ref[R_GT1:R_GT1 + 1, :] * y
        xm_ref[...] = xm
        xhat, _ = _rms(xm)
        h = xhat * vec_ref[R_GMLP:R_GMLP + 1, :] * (1.0 + vec_ref[R_SC2:R_SC2 + 1, :]) + vec_ref[R_SH2:R_SH2 + 1, :]
        h_ref[...] = h.astype(h_ref.dtype)

    return pl.pallas_call(
        body, name="post_attn", grid=(T // tm,),
        in_specs=[o_spec, _rows(tm, D), _full((D, D)), _full((1, D)), _full((8, D))],
        out_specs=[_rows(tm, D), _rows(tm, D), _rows(tm, D)],
        out_shape=[jax.ShapeDtypeStruct((T, D), MXU_DTYPE), jax.ShapeDtypeStruct((T, D), jnp.float32),
                   jax.ShapeDtypeStruct((T, D), MXU_DTYPE)],
        compiler_params=_params("parallel"),
    )(o, x, w_o, bias, vec)


def _ff_specs(tf):
    per = D_FF // N_CHIPS // tf
    w1 = pl.BlockSpec((None, D, tf), lambda i, f: (f // per, 0, f % per))
    w2 = pl.BlockSpec((None, tf, D), lambda i, f: (f // per, f % per, 0))
    return w1, w2


def _mlp_fwd(h2, w1, w2, xm, vec):
    T = h2.shape[0]
    tm = _tile(T, MLP_FWD_TILE[0])
    tf = _tile(D_FF // N_CHIPS, MLP_FWD_TILE[1])
    nf = D_FF // tf
    w1_spec, w2_spec = _ff_specs(tf)

    def body(h_ref, w1_ref, w2_ref, xm_ref, vec_ref, a_ref, y_ref, xo_ref, acc):
        f = pl.program_id(1)

        @pl.when(f == 0)
        def _():
            acc[...] = jnp.zeros_like(acc)

        u = jnp.maximum(_dot(h_ref[...], w1_ref[...]), 0.0)
        ab = (u * u).astype(MXU_DTYPE)
        a_ref[...] = ab
        acc[...] += _dot(ab, w2_ref[...])

        @pl.when(f == nf - 1)
        def _():
            y = acc[...]
            y_ref[...] = y.astype(y_ref.dtype)
            xo_ref[...] = xm_ref[...] + vec_ref[R_GT2:R_GT2 + 1, :] * y

    return pl.pallas_call(
        body, name="mlp_fwd", grid=(T // tm, nf),
        in_specs=[_rows(tm, D), w1_spec, w2_spec, _rows(tm, D), _full((8, D))],
        out_specs=[pl.BlockSpec((tm, tf), lambda i, f: (i, f)), _rows(tm, D), _rows(tm, D)],
        out_shape=[jax.ShapeDtypeStruct((T, D_FF), MXU_DTYPE), jax.ShapeDtypeStruct((T, D), MXU_DTYPE),
                   jax.ShapeDtypeStruct((T, D), jnp.float32)],
        scratch_shapes=[pltpu.VMEM((tm, D), jnp.float32)],
        compiler_params=_params("parallel", "arbitrary"),
    )(h2, w1, w2, xm, vec)


def _swa_pre(x, vec, w_qkv, b_qkv):
    T = x.shape[0]
    tm = _tile(T, 512)
    nq = SWA_HEADS * SWA_HEAD_DIM
    nk = SWA_KV_HEADS * SWA_HEAD_DIM
    wq_t, w_kv = w_qkv[:, :nq].T, w_qkv[:, nq:]
    bq_col, b_kv = b_qkv[:, :nq].reshape(nq, 1), b_qkv[:, nq:]

    def body(x_ref, vec_ref, wq_ref, wkv_ref, bq_ref, bkv_ref, h_ref, qt_ref, k_ref, v_ref):
        xhat, _ = _rms(x_ref[...])
        h = xhat * vec_ref[R_GMIX:R_GMIX + 1, :] * (1.0 + vec_ref[R_SC1:R_SC1 + 1, :]) + vec_ref[R_SH1:R_SH1 + 1, :]
        hb = h.astype(MXU_DTYPE)
        h_ref[...] = hb
        qt_ref[...] = ((_dot_nt(wq_ref[...], hb) + bq_ref[...]) * SWA_SCALE).astype(MXU_DTYPE)
        kv = _dot(hb, wkv_ref[...]) + bkv_ref[...]
        k_ref[...] = kv[:, :nk].astype(MXU_DTYPE)
        v_ref[...] = kv[:, nk:].astype(MXU_DTYPE)

    return pl.pallas_call(
        body, name="swa_pre", grid=(T // tm,),
        in_specs=[_rows(tm, D), _full((8, D)), _full(wq_t.shape), _full(w_kv.shape), _full(bq_col.shape),
                  _full(b_kv.shape)],
        out_specs=[_rows(tm, D), pl.BlockSpec((nq, tm), lambda i: (0, i)), _rows(tm, nk), _rows(tm, nk)],
        out_shape=[jax.ShapeDtypeStruct((T, D), MXU_DTYPE), jax.ShapeDtypeStruct((nq, T), MXU_DTYPE),
                   jax.ShapeDtypeStruct((T, nk), MXU_DTYPE), jax.ShapeDtypeStruct((T, nk), MXU_DTYPE)],
        compiler_params=_params("parallel"),
    )(x, vec, wq_t, w_kv, bq_col, b_kv)


def _swa_bias():
    W = WINDOW
    slopes = 2.0 ** (-8.0 * np.arange(1, SWA_HEADS + 1) / SWA_HEADS)
    j, i = np.arange(W)[:, None], np.arange(W)[None, :]
    dist = np.where(j > i, W + i - j, i - j)
    bias = -slopes[:, None, None] * dist[None].astype(np.float64)
    bias = bias.reshape(SWA_KV_HEADS, SWA_GROUP, W, W).transpose(0, 2, 1, 3)
    return jnp.asarray(bias.reshape(SWA_KV_HEADS, W, SWA_GROUP * W), jnp.float32)


def _swa_fold_mask():
    W, G = WINDOW, SWA_GROUP
    j = lax.broadcasted_iota(jnp.int32, (W, G * W), 0)
    i = lax.broadcasted_iota(jnp.int32, (W, G * W), 1) & (W - 1)
    return j > i


def _swa_fold(band, up):
    return jnp.where(up, band[:WINDOW], band[WINDOW:])


def _swa_unfold(folded, up):
    zero = jnp.zeros_like(folded)
    return jnp.concatenate([jnp.where(up, folded, zero), jnp.where(up, zero, folded)], axis=0)


SWA_STEP_BLOCKS = 4


def _swa_blocks(T):
    nb = T // WINDOW
    return next(b for b in (SWA_STEP_BLOCKS, 2, 1) if nb % b == 0)


def _swa_views(b, qt_ref, kp_ref, kc_ref):
    W = WINDOW
    prev = kp_ref if b == 0 else kc_ref.at[pl.ds((b - 1) * W, W), :]
    return qt_ref.at[:, pl.ds(b * W, W)], prev, kc_ref.at[pl.ds(b * W, W), :]


def _swa_probs(has_prev, up, kh, qt_ref, kp_ref, kc_ref, bias_ref, sink_ref):
    W, Dh, G = WINDOW, SWA_HEAD_DIM, SWA_GROUP
    qt = jnp.concatenate([qt_ref[(kh * G + g) * Dh:(kh * G + g + 1) * Dh, :] for g in range(G)], axis=1)
    kb = jnp.concatenate([kp_ref[:, kh * Dh:(kh + 1) * Dh], kc_ref[:, kh * Dh:(kh + 1) * Dh]], axis=0)
    s = _swa_fold(_dot(kb, qt), up) + bias_ref[kh]
    if has_prev is not True:
        s = jnp.where(up & jnp.logical_not(has_prev), NEG, s)
    sink = sink_ref[kh]
    m = jnp.maximum(jnp.max(s, axis=0, keepdims=True), sink)
    p = jnp.exp(s - m)
    p_sink = jnp.exp(sink - m)
    inv = 1.0 / (jnp.sum(p, axis=0, keepdims=True) + p_sink)
    return qt, kb, p * inv, p_sink * inv


def _swa_attn_fwd(qt, k, v, bias, sink_rows):
    T = qt.shape[1]
    W, Dh, G, Hk = WINDOW, SWA_HEAD_DIM, SWA_GROUP, SWA_KV_HEADS
    nk = Hk * Dh

    nb = _swa_blocks(T)

    def body(qt_ref, kp_ref, kc_ref, vp_ref, vc_ref, bias_ref, sink_ref, ot_ref):
        n = pl.program_id(0)
        up = _swa_fold_mask()
        for b in range(nb):
            q_b, kp_b, kc_b = _swa_views(b, qt_ref, kp_ref, kc_ref)
            _, vp_b, vc_b = _swa_views(b, qt_ref, vp_ref, vc_ref)
            for kh in range(Hk):
                _, _, pn, _ = _swa_probs(True if b else n > 0, up, kh, q_b, kp_b, kc_b, bias_ref, sink_ref)
                vb = jnp.concatenate([vp_b[:, kh * Dh:(kh + 1) * Dh], vc_b[:, kh * Dh:(kh + 1) * Dh]], axis=0)
                ot = _dot_tn(vb, _swa_unfold(pn, up).astype(MXU_DTYPE))
                for g in range(G):
                    rows = pl.ds((kh * G + g) * Dh, Dh)
                    ot_ref[rows, pl.ds(b * W, W)] = ot[:, g * W:(g + 1) * W].astype(ot_ref.dtype)

    prev = lambda n: (jnp.maximum(n * nb - 1, 0), 0)
    cur = lambda n: (n, 0)
    col = lambda n: (0, n)
    return pl.pallas_call(
        body, name="swa_attn_fwd", grid=(T // (nb * W),),
        in_specs=[pl.BlockSpec((D, nb * W), col), pl.BlockSpec((W, nk), prev), pl.BlockSpec((nb * W, nk), cur),
                  pl.BlockSpec((W, nk), prev), pl.BlockSpec((nb * W, nk), cur), _full(bias.shape),
                  _full(sink_rows.shape)],
        out_specs=pl.BlockSpec((D, nb * W), col),
        out_shape=jax.ShapeDtypeStruct((D, T), MXU_DTYPE),
        compiler_params=_params("parallel"),
    )(qt, k, k, v, v, bias, sink_rows)


def _final_loss(x, tgt, g):
    T = x.shape[0]
    tm = _tile(T, ROW_TILE)

    def body(x_ref, t_ref, g_ref, loss_ref, dx_ref, dg_ref):
        @pl.when(pl.program_id(0) == 0)
        def _():
            loss_ref[...] = jnp.zeros_like(loss_ref)
            dg_ref[...] = jnp.zeros_like(dg_ref)

        xhat, r = _rms(x_ref[...])
        gv = g_ref[...]
        e = xhat * gv - t_ref[...]
        loss_ref[...] += 0.5 * jnp.sum(jnp.mean(e * e, axis=-1, keepdims=True), axis=0, keepdims=True)
        dy = e * (1.0 / D)
        dg_ref[...] += _rowsum(dy * xhat)
        dx_ref[...] = _rms_bwd(dy * gv, xhat, r)

    return pl.pallas_call(
        body, name="final_loss", grid=(T // tm,),
        in_specs=[_rows(tm, D), _rows(tm, D), _full((1, D))],
        out_specs=[_full((8, LANES)), _rows(tm, D), _full((1, D))],
        out_shape=[jax.ShapeDtypeStruct((8, LANES), jnp.float32), jax.ShapeDtypeStruct((T, D), jnp.float32),
                   jax.ShapeDtypeStruct((1, D), jnp.float32)],
        compiler_params=_params("arbitrary"),
    )(x, tgt, g)


def _mlp_bwd(dxo, y2, a, w1, w2, xm, vec):
    T = dxo.shape[0]
    tm = _tile(T, MLP_BWD_TILE[0])
    tf = _tile(D_FF // N_CHIPS, MLP_BWD_TILE[1])
    ni, nf = T // tm, D_FF // tf
    w1_spec, w2_spec = _ff_specs(tf)

    def body(dxo_hbm, y_ref, a_ref, w1_ref, w2_ref, xm_hbm, vec_ref, du_ref, dy_ref, dxm_hbm, ps_ref,
             buf, dyb, acc, sems):
        i, f = pl.program_id(0), pl.program_id(1)
        rows = pl.ds(pl.multiple_of(i * tm, tm), tm)
        load_dxo = pltpu.make_async_copy(dxo_hbm.at[rows, :], buf.at[0], sems.at[0])
        load_xm = pltpu.make_async_copy(xm_hbm.at[rows, :], buf.at[1], sems.at[1])
        store_dxm = pltpu.make_async_copy(buf.at[1], dxm_hbm.at[rows, :], sems.at[2])

        @pl.when((i == 0) & (f == 0))
        def _():
            ps_ref[...] = jnp.zeros_like(ps_ref)

        @pl.when(f == 0)
        def _():
            load_dxo.start()

            @pl.when(i > 0)
            def _():
                store_dxm.wait()

            load_xm.start()
            load_dxo.wait()
            dxo_t = buf[0]
            d = (dxo_t * vec_ref[R_GT2:R_GT2 + 1, :]).astype(MXU_DTYPE)
            dyb[...] = d
            dy_ref[...] = d
            acc[...] = jnp.zeros_like(acc)
            ps_ref[R_GT2:R_GT2 + 1, :] += _rowsum(dxo_t * y_ref[...].astype(jnp.float32))

        da = _dot_nt(dyb[...], w2_ref[...])
        dub = (da * (2.0 * jnp.sqrt(a_ref[...].astype(jnp.float32)))).astype(MXU_DTYPE)
        du_ref[...] = dub
        acc[...] += _dot_nt(dub, w1_ref[...])

        @pl.when(f == nf - 1)
        def _():
            load_xm.wait()
            buf[1] = _modulate_bwd(acc[...], buf[1], vec_ref, R_GMLP, R_SC2, R_SH2, ps_ref, buf[0])
            store_dxm.start()

            @pl.when(i == ni - 1)
            def _():
                store_dxm.wait()

    return pl.pallas_call(
        body, name="mlp_bwd", grid=(ni, nf),
        in_specs=[ANY, _rows(tm, D), pl.BlockSpec((tm, tf), lambda i, f: (i, f)), w1_spec, w2_spec, ANY, _full((8, D))],
        out_specs=[pl.BlockSpec((tm, tf), lambda i, f: (i, f)), _rows(tm, D), ANY, _full((8, D))],
        out_shape=[jax.ShapeDtypeStruct((T, D_FF), MXU_DTYPE), jax.ShapeDtypeStruct((T, D), MXU_DTYPE),
                   jax.ShapeDtypeStruct((T, D), jnp.float32), jax.ShapeDtypeStruct((8, D), jnp.float32)],
        scratch_shapes=[pltpu.VMEM((2, tm, D), jnp.float32), pltpu.VMEM((tm, D), MXU_DTYPE),
                        pltpu.VMEM((tm, D), jnp.float32), pltpu.SemaphoreType.DMA((3,))],
        compiler_params=_params("arbitrary", "arbitrary"),
    )(dxo, y2, a, w1, w2, xm, vec)


def _mm_tn(a, g, name, split=None, layers=1, layer=0, into=None, a_transposed=False):
    K, T = a.shape if a_transposed else a.shape[::-1]
    N = g.shape[1]
    kq = K // N_CHIPS if split == "rows" else K
    nq = N // N_CHIPS if split == "cols" else N
    bk, bn, bt = _tile(kq, 1024), _tile(nq, 1024), _tile(T, DW_TOKENS)
    if nq % bn or bn % LANES:
        bn = nq
    kper, nper = kq // bk, nq // bn

    def body(*refs):
        a_ref, g_ref, o_ref = refs[0], refs[1], refs[-1]

        @pl.when(pl.program_id(2) == 0)
        def _():
            o_ref[...] = jnp.zeros_like(o_ref)

        o_ref[...] += (_dot if a_transposed else _dot_tn)(a_ref[...], g_ref[...])

    a_spec = pl.BlockSpec((bk, bt), lambda k, n, t: (k, t)) if a_transposed else pl.BlockSpec((bt, bk), lambda k, n, t: (t, k))
    in_specs = [a_spec, pl.BlockSpec((bt, bn), lambda k, n, t: (t, n))]
    args = [a, g]
    aliases = {}
    if split is None:
        out_spec = pl.BlockSpec((bk, bn), lambda k, n, t: (k, n))
        out_shape = jax.ShapeDtypeStruct((K, N), jnp.float32)
    else:
        if split == "cols":
            idx = lambda k, n, t: (n // nper, layer, k, n % nper)
        else:
            idx = lambda k, n, t: (k // kper, layer, k % kper, n)
        out_spec = pl.BlockSpec((None, None, bk, bn), idx)
        out_shape = jax.ShapeDtypeStruct((N_CHIPS, layers, kq, nq), jnp.float32)
        if into is not None:
            in_specs.append(pl.BlockSpec(memory_space=pl.ANY))
            args.append(into)
            aliases = {2: 0}
    return pl.pallas_call(
        body, name=name, grid=(K // bk, N // bn, T // bt), in_specs=in_specs, out_specs=out_spec, out_shape=out_shape,
        input_output_aliases=aliases, compiler_params=_params("parallel", "parallel", "arbitrary"),
    )(*args)


def _attn_out_bwd(dxm, y1, o, w_o, vec, with_delta):
    T = dxm.shape[0]
    tm = _tile(T, ROW_TILE)
    H = MLA_HEADS

    def body(dxm_ref, y_ref, w_ref, vec_ref, *refs):
        o_ref = refs[0] if with_delta else None
        dy_ref, do_ref, ps_ref, *delta_ref = refs[1:] if with_delta else refs

        @pl.when(pl.program_id(0) == 0)
        def _():
            ps_ref[...] = jnp.zeros_like(ps_ref)

        dxm_t = dxm_ref[...]
        dy = dxm_t * vec_ref[R_GT1:R_GT1 + 1, :]
        ps_ref[R_GT1:R_GT1 + 1, :] += _rowsum(dxm_t * y_ref[...].astype(jnp.float32))
        ps_ref[R_BO:R_BO + 1, :] += _rowsum(dy)
        dyb = dy.astype(MXU_DTYPE)
        dy_ref[...] = dyb
        if not with_delta:
            do_ref[...] = _dot_nt(w_ref[...], dyb).astype(do_ref.dtype)
        else:
            do = _dot_nt(dyb, w_ref[...])
            do_ref[...] = do.astype(do_ref.dtype)
            of = o_ref[...].astype(jnp.float32)
            ones = jnp.ones((8, V_DIM), jnp.float32)
            for hd in range(H):
                sl = slice(hd * V_DIM, (hd + 1) * V_DIM)
                d = lax.dot_general(ones, do[:, sl] * of[:, sl], (((1,), (1,)), ((), ())),
                                    precision=lax.Precision.HIGHEST, preferred_element_type=jnp.float32)
                delta_ref[0][hd] = d[0:1, :]

    out_specs = [_rows(tm, D), _rows(tm, D), _full((8, D))]
    out_shape = [jax.ShapeDtypeStruct((T, D), MXU_DTYPE), jax.ShapeDtypeStruct((T, D), MXU_DTYPE),
                 jax.ShapeDtypeStruct((8, D), jnp.float32)]
    if not with_delta:
        out_specs[1] = pl.BlockSpec((D, tm), lambda i: (0, i))
        out_shape[1] = jax.ShapeDtypeStruct((D, T), MXU_DTYPE)
    if with_delta:
        out_specs.append(pl.BlockSpec((H, 1, tm), lambda i: (0, 0, i)))
        out_shape.append(jax.ShapeDtypeStruct((H, 1, T), jnp.float32))
    return pl.pallas_call(
        body, name="attn_out_bwd_mla" if with_delta else "attn_out_bwd_swa", grid=(T // tm,),
        in_specs=[_rows(tm, D), _rows(tm, D), _full((D, D)), _full((8, D))] + ([_rows(tm, D)] if with_delta else []),
        out_specs=out_specs, out_shape=out_shape,
        compiler_params=_params("arbitrary"),
    )(dxm, y1, w_o, vec, *([o] if with_delta else []))


def _mla_attn_bwd(q, k, v, do, lse, delta):
    H, T, _ = q.shape
    tb = _tile(T, ATTN_BLOCK)
    sub = min(ATTN_SUB, tb)
    ns, nb = tb // sub, T // tb

    pairs = [(j, i) for j in range(nb) for i in range(j, nb)]
    kj_tab = jnp.asarray([j for j, _ in pairs], jnp.int32)
    qi_tab = jnp.asarray([i for _, i in pairs], jnp.int32)

    def body(kj_ref, qi_ref, q_ref, k_ref, v_ref, do_ref, lse_ref, dl_ref, dq_ref, dk_ref, dv_ref, dk_acc, dv_acc):
        j, i = kj_ref[pl.program_id(1)], qi_ref[pl.program_id(1)]

        @pl.when((j == 0) & (i == 0))
        def _():
            dq_ref[...] = jnp.zeros_like(dq_ref)

        def update(kk, r, masked):
            keys, rows = pl.ds(kk * sub, sub), pl.ds(r * sub, sub)
            kb, qb, dob = k_ref[0, keys, :], q_ref[0, rows, :], do_ref[rows, :]
            st = _dot_nt(kb, qb)
            if masked:
                row = lax.broadcasted_iota(jnp.int32, (sub, sub), 0)
                col = lax.broadcasted_iota(jnp.int32, (sub, sub), 1)
                st = jnp.where(row <= col, st, NEG)
            pt = jnp.exp2(st - lse_ref[0, :, rows])
            dv_acc[keys, :] += _dot(pt.astype(MXU_DTYPE), dob)
            dpt = _dot_nt(v_ref[0, keys, :], dob)
            dst = (pt * (dpt - dl_ref[0, :, rows])).astype(MXU_DTYPE)
            dk_acc[keys, :] += _dot(dst, qb)
            q_rows = pl.ds(pl.multiple_of(i * tb + r * sub, sub), sub)
            dq_ref[0, q_rows, :] += _dot_tn(dst, kb)

        @pl.when(i == j)
        def _():
            dk_acc[...] = jnp.zeros_like(dk_acc)
            dv_acc[...] = jnp.zeros_like(dv_acc)
            for r in range(ns):
                for kk in range(r + 1):
                    update(kk, r, kk == r)

        @pl.when(i > j)
        def _():
            for r in range(ns):
                for kk in range(ns):
                    update(kk, r, False)

        @pl.when(i == nb - 1)
        def _():
            dk_ref[0] = (dk_acc[...] * LN2).astype(dk_ref.dtype)
            dv_ref[0] = dv_acc[...].astype(dv_ref.dtype)

    q_idx = lambda h, p, kj_ref, qi_ref: (h, qi_ref[p], 0)
    kv_idx = lambda h, p, kj_ref, qi_ref: (h, kj_ref[p], 0)
    stat_idx = lambda h, p, kj_ref, qi_ref: (h, 0, qi_ref[p])
    return pl.pallas_call(
        body, name="mla_attn_bwd",
        grid_spec=pltpu.PrefetchScalarGridSpec(
            num_scalar_prefetch=2, grid=(H, len(pairs)),
            in_specs=[pl.BlockSpec((1, tb, QK_EXT), q_idx), pl.BlockSpec((1, tb, QK_EXT), kv_idx),
                      pl.BlockSpec((1, tb, V_DIM), kv_idx),
                      pl.BlockSpec((tb, V_DIM), lambda h, p, kj_ref, qi_ref: (qi_ref[p], h)),
                      pl.BlockSpec((1, 1, tb), stat_idx), pl.BlockSpec((1, 1, tb), stat_idx)],
            out_specs=[pl.BlockSpec((1, T, QK_EXT), lambda h, p, kj_ref, qi_ref: (h, 0, 0)),
                       pl.BlockSpec((1, tb, QK_EXT), kv_idx), pl.BlockSpec((1, tb, V_DIM), kv_idx)],
            scratch_shapes=[pltpu.VMEM((tb, QK_EXT), jnp.float32), pltpu.VMEM((tb, V_DIM), jnp.float32)]),
        out_shape=[jax.ShapeDtypeStruct((H, T, QK_EXT), jnp.float32), jax.ShapeDtypeStruct((H, T, QK_EXT), MXU_DTYPE),
                   jax.ShapeDtypeStruct((H, T, V_DIM), MXU_DTYPE)],
        compiler_params=_params("parallel", "arbitrary"),
    )(kj_tab, qi_tab, q, k, v, do, lse, delta)


def _mla_pre_bwd(x, dxm, vec, hb, z, dq, dk, dv, cs, wcat, g_q, g_kv, wuq, wukv):
    T = x.shape[0]
    tm = _tile(T, 512)
    H = MLA_HEADS
    zc = wcat.shape[1]

    def body(x_ref, dxm_ref, vec_ref, h_ref, z_ref, dq_ref, dk_ref, dv_ref, cs_ref, wcat_ref, gq_ref, gkv_ref,
             wuq_ref, wukv_ref, dx_ref, ps_ref, dgq_ref, dgkv_ref, dwcat_ref, dwuq_ref, dwukv_ref):
        @pl.when(pl.program_id(0) == 0)
        def _():
            for ref in (ps_ref, dgq_ref, dgkv_ref, dwcat_ref, dwuq_ref, dwukv_ref):
                ref[...] = jnp.zeros_like(ref)

        z = z_ref[...]
        cs_t = cs_ref[...]
        cqhat, rq = _rms(z[:, :Q_LORA])
        ckhat, rk = _rms(z[:, Q_LORA:Q_LORA + KV_LORA])
        gq, gkv = gq_ref[...], gkv_ref[...]
        cq = (cqhat * gq).astype(MXU_DTYPE)
        ckv = (ckhat * gkv).astype(MXU_DTYPE)
        dcq = jnp.zeros((tm, Q_LORA), jnp.float32)
        dckv = jnp.zeros((tm, KV_LORA), jnp.float32)
        dkr = jnp.zeros((tm, LANES), jnp.float32)
        for hd in range(H):
            dqh = dq_ref[hd] * MLA_SCALE
            gqh = jnp.concatenate([dqh[:, :QK_NOPE], dqh[:, QK_NOPE:] * cs_t], axis=1).astype(MXU_DTYPE)
            dcq += _dot_nt(gqh, wuq_ref[hd])
            dwuq_ref[hd] += _dot_tn(cq, gqh)
            dkh = dk_ref[hd]
            gkvh = jnp.concatenate([dkh[:, :QK_NOPE], dv_ref[hd]], axis=1)
            dckv += _dot_nt(gkvh, wukv_ref[hd])
            dwukv_ref[hd] += _dot_tn(ckv, gkvh)
            dkr += dkh[:, QK_NOPE:].astype(jnp.float32)
        dgq_ref[...] += _rowsum(dcq * cqhat)
        dgkv_ref[...] += _rowsum(dckv * ckhat)
        dcq_pre = _rms_bwd(dcq * gq, cqhat, rq)
        dckv_pre = _rms_bwd(dckv * gkv, ckhat, rk)
        dkr2 = (dkr + pltpu.roll(dkr, QK_ROPE, axis=1)) * cs_t
        dz = jnp.concatenate([dcq_pre, dckv_pre, dkr2], axis=1).astype(MXU_DTYPE)
        dwcat_ref[...] += _dot_tn(h_ref[...], dz)
        dh = _dot_nt(dz, wcat_ref[...])
        dx_ref[...] = _modulate_bwd(dh, x_ref[...], vec_ref, R_GMIX, R_SC1, R_SH1, ps_ref, dxm_ref[...])

    hblk = lambda w: pl.BlockSpec((H, tm, w), lambda i: (0, i, 0))
    return pl.pallas_call(
        body, name="mla_pre_bwd", grid=(T // tm,),
        in_specs=[_rows(tm, D), _rows(tm, D), _full((8, D)), _rows(tm, D), _rows(tm, zc), hblk(QK_EXT), hblk(QK_EXT),
                  hblk(V_DIM), _rows(tm, LANES), _full(wcat.shape), _full(g_q.shape), _full(g_kv.shape),
                  _full(wuq.shape), _full(wukv.shape)],
        out_specs=[_rows(tm, D), _full((8, D)), _full(g_q.shape), _full(g_kv.shape), _full(wcat.shape),
                   _full(wuq.shape), _full(wukv.shape)],
        out_shape=[jax.ShapeDtypeStruct((T, D), jnp.float32), jax.ShapeDtypeStruct((8, D), jnp.float32),
                   jax.ShapeDtypeStruct(g_q.shape, jnp.float32), jax.ShapeDtypeStruct(g_kv.shape, jnp.float32),
                   jax.ShapeDtypeStruct(wcat.shape, jnp.float32), jax.ShapeDtypeStruct(wuq.shape, jnp.float32),
                   jax.ShapeDtypeStruct(wukv.shape, jnp.float32)],
        compiler_params=_params("arbitrary"),
    )(x, dxm, vec, hb, z, dq, dk, dv, cs, wcat, g_q, g_kv, wuq, wukv)


def _swa_attn_bwd(qt, k, v, dot_, bias, sink_rows):
    T = qt.shape[1]
    W, Dh, G, Hk = WINDOW, SWA_HEAD_DIM, SWA_GROUP, SWA_KV_HEADS
    nk = Hk * Dh
    nb = _swa_blocks(T)

    def body(qt_ref, kp_ref, kc_ref, vp_ref, vc_ref, dot_ref, bias_ref, sink_ref, dqt_ref, dk_ref, dv_ref, dsink_ref):
        n = pl.program_id(0)

        @pl.when(n == 0)
        def _():
            dk_ref[...] = jnp.zeros_like(dk_ref)
            dv_ref[...] = jnp.zeros_like(dv_ref)
            dsink_ref[...] = jnp.zeros_like(dsink_ref)

        def add_rows(first_row, dkb_part, dvb_part):
            rows = pl.ds(pl.multiple_of(first_row, W), W)
            dk_ref[rows, :] += dkb_part
            dv_ref[rows, :] += dvb_part

        up = _swa_fold_mask()
        for b in range(nb):
            q_b, kp_b, kc_b = _swa_views(b, qt_ref, kp_ref, kc_ref)
            do_b, vp_b, vc_b = _swa_views(b, dot_ref, vp_ref, vc_ref)
            dks, dvs = [], []
            for kh in range(Hk):
                qt, kb, pn, p_sink = _swa_probs(True if b else n > 0, up, kh, q_b, kp_b, kc_b, bias_ref, sink_ref)
                vb = jnp.concatenate([vp_b[:, kh * Dh:(kh + 1) * Dh], vc_b[:, kh * Dh:(kh + 1) * Dh]], axis=0)
                dot_h = jnp.concatenate([do_b[(kh * G + g) * Dh:(kh * G + g + 1) * Dh, :] for g in range(G)], axis=1)
                dp = _swa_fold(_dot(vb, dot_h), up)
                delta = jnp.sum(pn * dp, axis=0, keepdims=True)
                dsb = _swa_unfold(pn * (dp - delta), up).astype(MXU_DTYPE)
                dsink_ref[kh] += -p_sink * delta
                dqt = _dot_tn(kb, dsb) * SWA_SCALE
                for g in range(G):
                    dqt_ref[pl.ds((kh * G + g) * Dh, Dh), pl.ds(b * W, W)] = dqt[:, g * W:(g + 1) * W]
                dks.append(_dot_nt(dsb, qt))
                dvs.append(_dot_nt(_swa_unfold(pn, up).astype(MXU_DTYPE), dot_h))
            dkb = jnp.concatenate(dks, axis=1)
            dvb = jnp.concatenate(dvs, axis=1)
            add_rows((n * nb + b) * W, dkb[W:], dvb[W:])
            if b:
                add_rows((n * nb + b - 1) * W, dkb[:W], dvb[:W])
            else:
                @pl.when(n > 0)
                def _():
                    add_rows((n * nb - 1) * W, dkb[:W], dvb[:W])

    prev = lambda n: (jnp.maximum(n * nb - 1, 0), 0)
    cur = lambda n: (n, 0)
    col = lambda n: (0, n)
    return pl.pallas_call(
        body, name="swa_attn_bwd", grid=(T // (nb * W),),
        in_specs=[pl.BlockSpec((D, nb * W), col), pl.BlockSpec((W, nk), prev), pl.BlockSpec((nb * W, nk), cur),
                  pl.BlockSpec((W, nk), prev), pl.BlockSpec((nb * W, nk), cur), pl.BlockSpec((D, nb * W), col),
                  _full(bias.shape), _full(sink_rows.shape)],
        out_specs=[pl.BlockSpec((D, nb * W), col), _full((T, nk)), _full((T, nk)), _full(sink_rows.shape)],
        out_shape=[jax.ShapeDtypeStruct((D, T), jnp.float32), jax.ShapeDtypeStruct((T, nk), jnp.float32),
                   jax.ShapeDtypeStruct((T, nk), jnp.float32), jax.ShapeDtypeStruct(sink_rows.shape, jnp.float32)],
        compiler_params=_params("arbitrary"),
    )(qt, k, k, v, v, dot_, bias, sink_rows)


def _swa_pre_bwd(x, dxm, vec, dq_t, dk, dv, w_qkv):
    T = x.shape[0]
    tm = _tile(T, 512)
    nq = SWA_HEADS * SWA_HEAD_DIM
    nk = SWA_KV_HEADS * SWA_HEAD_DIM
    nqkv = nq + 2 * nk

    def body(x_ref, dxm_ref, vec_ref, dq_ref, dk_ref, dv_ref, w_ref, dx_ref, dqkv_ref, ps_ref, db_ref):
        @pl.when(pl.program_id(0) == 0)
        def _():
            ps_ref[...] = jnp.zeros_like(ps_ref)
            db_ref[...] = jnp.zeros_like(db_ref)

        dqkv = jnp.concatenate([dq_ref[...].T, dk_ref[...], dv_ref[...]], axis=1)
        db_ref[...] += _rowsum(dqkv)
        dqkv_b = dqkv.astype(MXU_DTYPE)
        dqkv_ref[...] = dqkv_b
        dh = _dot_nt(dqkv_b, w_ref[...])
        dx_ref[...] = _modulate_bwd(dh, x_ref[...], vec_ref, R_GMIX, R_SC1, R_SH1, ps_ref, dxm_ref[...])

    return pl.pallas_call(
        body, name="swa_pre_bwd", grid=(T // tm,),
        in_specs=[_rows(tm, D), _rows(tm, D), _full((8, D)), pl.BlockSpec((nq, tm), lambda i: (0, i)), _rows(tm, nk),
                  _rows(tm, nk), _full(w_qkv.shape)],
        out_specs=[_rows(tm, D), _rows(tm, nqkv), _full((8, D)), _full((1, nqkv))],
        out_shape=[jax.ShapeDtypeStruct((T, D), jnp.float32), jax.ShapeDtypeStruct((T, nqkv), MXU_DTYPE),
                   jax.ShapeDtypeStruct((8, D), jnp.float32), jax.ShapeDtypeStruct((1, nqkv), jnp.float32)],
        compiler_params=_params("arbitrary"),
    )(x, dxm, vec, dq_t, dk, dv, w_qkv)


def _rot_cols(w):
    half = QK_ROPE // 2
    return jnp.concatenate([-w[..., half:], w[..., :half]], axis=-1)


def _unrot_grad(d_rope, d_rot):
    half = QK_ROPE // 2
    return d_rope + jnp.concatenate([d_rot[..., half:], -d_rot[..., :half]], axis=-1)


def _rope_table(positions):
    half = QK_ROPE // 2
    inv_freq = ROPE_THETA ** (-jnp.arange(half, dtype=jnp.float32) / half)
    ang = positions.astype(jnp.float32)[:, None] * inv_freq
    cos, sin = jnp.cos(ang), jnp.sin(ang)
    return jnp.concatenate([cos, cos, sin, sin], axis=1)


def _sequence_step(x, tgt, positions, vecs, g_q, g_kv, sinks, g_final, wts, late_weights, on_late_grads, on_late_landed):
    H = MLA_HEADS
    cs = _rope_table(positions)
    w_dkv = wts["mla_w_dkv"]
    wcat = jnp.concatenate([wts["mla_w_dq"], w_dkv, _rot_cols(w_dkv[:, KV_LORA:])], axis=1)
    uq = wts["mla_w_uq"].reshape(Q_LORA, H, QK_NOPE + QK_ROPE)
    wuq = jnp.concatenate([uq, _rot_cols(uq[..., QK_NOPE:])], axis=-1).transpose(1, 0, 2)
    wukv = wts["mla_w_ukv"].reshape(KV_LORA, H, QK_NOPE + V_DIM).transpose(1, 0, 2)
    zero_bias = jnp.zeros((1, D), jnp.float32)
    bias = _swa_bias()
    sink_rows = jnp.broadcast_to(sinks.reshape(SWA_KV_HEADS, 1, SWA_GROUP, 1),
                                 (SWA_KV_HEADS, 1, SWA_GROUP, WINDOW)).reshape(SWA_KV_HEADS, 1, SWA_GROUP * WINDOW)

    h1a, z, q, k, v = _mla_pre(x, vecs[0], wcat, g_q, g_kv, wuq, wukv, cs)
    o_a, lse = _mla_attn_fwd(q, k, v)
    y1a, xm_a, h2a = _post_attn(o_a, x, wts["mla_w_o"], zero_bias, vecs[0])
    wts = {**wts, **late_weights(h2a)}
    a_a, y2a, x1 = _mlp_fwd(h2a, wts["w_ff1"][0], wts["w_ff2"][0], xm_a, vecs[0])

    h1b, qs_t, ks, vs = _swa_pre(x1, vecs[1], wts["swa_w_qkv"], wts["swa_b_qkv"])
    o_bt = _swa_attn_fwd(qs_t, ks, vs, bias, sink_rows)
    y1b, xm_b, h2b = _post_attn(o_bt, x1, wts["swa_w_o"], wts["swa_b_o"], vecs[1], o_transposed=True)
    a_b, y2b, x2 = _mlp_fwd(h2b, wts["w_ff1"][1], wts["w_ff2"][1], xm_b, vecs[1])

    loss8, dx2, dg_final = _final_loss(x2, tgt, g_final.reshape(1, D))

    du_b, dy2b, dxm_b, ps_mlp_b = _mlp_bwd(dx2, y2b, a_b, wts["w_ff1"][1], wts["w_ff2"][1], xm_b, vecs[1])
    g_ff2 = _mm_tn(a_b, dy2b, "dw_ff2_l1", "rows", DEPTH, 1)
    g_ff1 = _mm_tn(h2b, du_b, "dw_ff1_l1", "cols", DEPTH, 1)
    dy1b, do_bt, ps_out_b = _attn_out_bwd(dxm_b, y1b, None, wts["swa_w_o"], vecs[1], False)
    g_swa_o = _mm_tn(o_bt, dy1b, "dw_o_swa", a_transposed=True)
    dqs_t, dks, dvs, dsinks = _swa_attn_bwd(qs_t, ks, vs, do_bt, bias, sink_rows)
    dx1, dqkv, ps_pre_b, g_swa_bqkv = _swa_pre_bwd(x1, dxm_b, vecs[1], dqs_t, dks, dvs, wts["swa_w_qkv"])
    g_swa_qkv = _mm_tn(h1b, dqkv, "dw_qkv", "cols")

    du_a, dy2a, dxm_a, ps_mlp_a = _mlp_bwd(dx1, y2a, a_a, wts["w_ff1"][0], wts["w_ff2"][0], xm_a, vecs[0])
    g_ff2 = _mm_tn(a_a, dy2a, "dw_ff2_l0", "rows", DEPTH, 0, g_ff2)
    g_ff1 = _mm_tn(h2a, du_a, "dw_ff1_l0", "cols", DEPTH, 0, g_ff1)
    rows4 = lambda g: g.reshape(N_CHIPS, g.shape[0] // N_CHIPS, g.shape[1])
    token = on_late_grads({
        "swa_w_qkv": g_swa_qkv.reshape(N_CHIPS, D, -1), "swa_w_o": rows4(g_swa_o),
        "w_ff1": g_ff1.reshape(N_CHIPS, DEPTH * D, -1), "w_ff2": g_ff2.reshape(N_CHIPS, -1, D)})
    dy1a, do_a, ps_out_a, delta = _attn_out_bwd(dxm_a, y1a, o_a, wts["mla_w_o"], vecs[0] + token[0, 0], True)
    g_mla_o = _mm_tn(o_a, dy1a, "dw_o_mla")
    token = on_late_landed(g_mla_o)
    dq, dk, dv = _mla_attn_bwd(q, k, v, do_a, lse, delta + token[0, 0])
    dx0, ps_pre_a, dg_q, dg_kv, dwcat, dwuq, dwukv = _mla_pre_bwd(
        x, dxm_a, vecs[0], h1a, z, dq, dk, dv, cs, wcat, g_q, g_kv, wuq, wukv)

    c0, c1, c2 = Q_LORA, Q_LORA + KV_LORA, Q_LORA + KV_LORA + QK_ROPE
    g_dq = dwcat[:, :c0]
    g_dkv = jnp.concatenate([dwcat[:, c0:c1], _unrot_grad(dwcat[:, c1:c2], dwcat[:, c2:])], axis=1)
    e0 = QK_NOPE + QK_ROPE
    g_uq = jnp.concatenate([dwuq[..., :QK_NOPE], _unrot_grad(dwuq[..., QK_NOPE:e0], dwuq[..., e0:])], axis=-1)
    per = H // N_CHIPS
    g_uq = g_uq.reshape(N_CHIPS, per, Q_LORA, e0).transpose(0, 2, 1, 3).reshape(N_CHIPS, Q_LORA, per * e0)
    g_ukv = dwukv.reshape(N_CHIPS, per, KV_LORA, QK_NOPE + V_DIM).transpose(0, 2, 1, 3)
    g_ukv = g_ukv.reshape(N_CHIPS, KV_LORA, per * (QK_NOPE + V_DIM))

    def dmod(ps_pre, ps_out, ps_mlp):
        return jnp.concatenate([ps_pre[R_SH1:R_SC1 + 1], ps_out[R_GT1:R_GT1 + 1], ps_mlp[R_SH2:R_GT2 + 1]], axis=0)

    grads = {"mla_w_dq": rows4(g_dq), "mla_w_uq": g_uq, "mla_w_dkv": rows4(g_dkv), "mla_w_ukv": g_ukv,
             "mla_w_o": rows4(g_mla_o)}
    small = {
        "dmod": jnp.stack([dmod(ps_pre_a, ps_out_a, ps_mlp_a), dmod(ps_pre_b, ps_out_b, ps_mlp_b)]).reshape(DEPTH, 6 * D),
        "g_mix": jnp.stack([ps_pre_a[R_GMIX], ps_pre_b[R_GMIX]]),
        "g_mlp": jnp.stack([ps_mlp_a[R_GMLP], ps_mlp_b[R_GMLP]]),
        "mla_g_q": dg_q, "mla_g_kv": dg_kv, "swa_sinks": jnp.sum(dsinks.reshape(SWA_HEADS, WINDOW), axis=1).reshape(1, SWA_HEADS),
        "swa_b_qkv": g_swa_bqkv, "swa_b_o": ps_out_b[R_BO:R_BO + 1],
        "g_final": dg_final.reshape(D), "loss": loss8[0, 0],
    }
    return dx0, grads, small


SHARDED = {
    "mla_w_dq": (1, D // N_CHIPS, Q_LORA),
    "mla_w_uq": (1, Q_LORA, MLA_HEADS * (QK_NOPE + QK_ROPE) // N_CHIPS),
    "mla_w_dkv": (1, D // N_CHIPS, KV_LORA + QK_ROPE),
    "mla_w_ukv": (1, KV_LORA, MLA_HEADS * (QK_NOPE + V_DIM) // N_CHIPS),
    "mla_w_o": (1, MLA_HEADS * V_DIM // N_CHIPS, D),
    "swa_w_qkv": (1, D, (SWA_HEADS + 2 * SWA_KV_HEADS) * SWA_HEAD_DIM // N_CHIPS),
    "swa_w_o": (1, SWA_HEADS * SWA_HEAD_DIM // N_CHIPS, D),
    "w_ff1": (DEPTH, D, D_FF // N_CHIPS),
    "w_ff2": (DEPTH, D_FF // N_CHIPS, D),
}
COL_SPLIT = ("mla_w_uq", "mla_w_ukv", "swa_w_qkv")
BIASES = {"swa_b_qkv": (SWA_HEADS + 2 * SWA_KV_HEADS) * SWA_HEAD_DIM, "swa_b_o": D}


def _view2d(name):
    shape = SHARDED[name]
    return math.prod(shape[:-1]), shape[-1]


SMALL = {"b_ada": (DEPTH, 6 * D), "g_mix": (DEPTH, D), "g_mlp": (DEPTH, D), "mla_g_q": (1, Q_LORA),
         "mla_g_kv": (1, KV_LORA), "swa_sinks": (1, SWA_HEADS), "g_final": (D,), "loss": (),
         "swa_b_qkv": (1, BIASES["swa_b_qkv"]), "swa_b_o": (1, BIASES["swa_b_o"])}
SMALL_ROWS = 192
DMA_ROWS = 256


SLOT_ROWS = 8


def _small_slots():
    slots, off = {}, 0
    for name, shape in SMALL.items():
        n = max(math.prod(shape), 1)
        slots[name] = (off, n)
        off += -(-n // (SLOT_ROWS * LANES)) * SLOT_ROWS * LANES
    assert off <= SMALL_ROWS * LANES
    return slots


def _pack_small(vals):
    parts, end = [], 0
    for name, (off, n) in _small_slots().items():
        pad = -(-n // (SLOT_ROWS * LANES)) * SLOT_ROWS * LANES - n
        v = vals[name].astype(jnp.float32).reshape(-1) if name in vals else jnp.zeros((n,), jnp.float32)
        parts += [v, jnp.zeros((pad,), jnp.float32)]
        end = off + n + pad
    parts.append(jnp.zeros((SMALL_ROWS * LANES - end,), jnp.float32))
    return jnp.concatenate(parts).reshape(SMALL_ROWS, LANES)


def _from_slot(name, rows):
    n = max(math.prod(SMALL[name]), 1)
    return rows.reshape(-1)[:n].reshape(SMALL[name])


def _pieces(rows):
    return [(off, min(DMA_ROWS, rows - off)) for off in range(0, rows, DMA_ROWS)]


HBM = pl.BlockSpec(memory_space=pltpu.HBM)
MESH = pl.DeviceIdType.MESH


def _place():
    x, y, c = lax.axis_index("x"), lax.axis_index("y"), lax.axis_index("c")
    chips = [(1 - x, y), (x, 1 - y), (1 - x, 1 - y)]
    return x, y, c, chips


def _all_gather(block):
    m_per, n = block.shape

    def body(x_ref, out_ref, send_sems, recv_sems, local_sem):
        x, y, c, chips = _place()
        me, sibling = (x, y, c), (x, y, 1 - c)

        def rows(px, py, pc):
            return out_ref.at[pl.ds((4 * px + 2 * py + pc) * m_per, m_per), :]

        def copy(k, blk, to, src=None):
            return pltpu.make_async_remote_copy(
                src_ref=rows(*blk) if src is None else src, dst_ref=rows(*blk),
                send_sem=send_sems.at[k], recv_sem=recv_sems.at[k], device_id=to, device_id_type=MESH)

        mine = pltpu.make_async_copy(x_ref, rows(*me), local_sem)
        mine.start()
        first = [copy(0, me, sibling, src=x_ref)]
        first += [copy(1 + j, me, (*chip, c), src=x_ref) for j, chip in enumerate(chips)]
        for cp in first:
            cp.start()
        passed = [copy(4 + j, (*chip, c), sibling) for j, chip in enumerate(chips)]
        for j, chip in enumerate(chips):
            copy(1 + j, (*chip, c), me).wait_recv()
            passed[j].start()
        copy(0, sibling, me).wait_recv()
        for j, chip in enumerate(chips):
            copy(4 + j, (*chip, 1 - c), me).wait_recv()
        for cp in first + passed:
            cp.wait_send()
        mine.wait()

    out = pl.pallas_call(
        body, name="all_gather_small",
        out_shape=jax.ShapeDtypeStruct((N_DEV * m_per, n), block.dtype),
        in_specs=[pl.BlockSpec(memory_space=pltpu.VMEM)],
        out_specs=pl.BlockSpec(memory_space=pltpu.VMEM),
        scratch_shapes=[pltpu.SemaphoreType.DMA((7,)), pltpu.SemaphoreType.DMA((7,)), pltpu.SemaphoreType.DMA],
    )(block)
    return out.reshape(N_DEV, m_per, n)


def _weight_gather(shards):
    nt = len(shards)

    def body(*refs):
        w_refs, out_refs = refs[:nt], refs[nt:2 * nt]
        send_sems, recv_sems = refs[2 * nt:]
        x, y, c, chips = _place()
        sibling = (x, y, 1 - c)

        def slab(t, px, py, half):
            rh = shards[t].shape[0] // 2
            return out_refs[t].at[2 * px + py, pl.ds(half * rh, rh), :]

        def copy(t, k, src, dst, to):
            return pltpu.make_async_remote_copy(src_ref=src, dst_ref=dst, send_sem=send_sems.at[6 * t + k],
                                                recv_sem=recv_sems.at[6 * t + k], device_id=to, device_id_type=MESH)

        first = []
        for t in range(nt):
            rh = shards[t].shape[0] // 2
            first += [copy(t, j, w_refs[t].at[pl.ds(c * rh, rh), :], slab(t, x, y, c), (*chip, c))
                      for j, chip in enumerate(chips)]
        for cp in first:
            cp.start()
        passed = []
        for t in range(nt):
            for j, chip in enumerate(chips):
                copy(t, j, slab(t, *chip, c), slab(t, *chip, c), (*chip, c)).wait_recv()
                rh = shards[t].shape[0] // 2
                for off, n in _pieces(rh):
                    piece = out_refs[t].at[2 * chip[0] + chip[1], pl.ds(c * rh + off, n), :]
                    copy(t, 3 + j, piece, piece, sibling).start()
                passed.append(copy(t, 3 + j, slab(t, *chip, c), slab(t, *chip, c), sibling))
        for t in range(nt):
            for j, chip in enumerate(chips):
                copy(t, 3 + j, slab(t, *chip, 1 - c), slab(t, *chip, 1 - c), sibling).wait_recv()
        for cp in first + passed:
            cp.wait_send()

    return pl.pallas_call(
        body, name="weight_gather",
        out_shape=[jax.ShapeDtypeStruct((N_CHIPS,) + s.shape, s.dtype) for s in shards],
        in_specs=[HBM] * nt, out_specs=[HBM] * nt,
        scratch_shapes=[pltpu.SemaphoreType.DMA((6 * nt,)), pltpu.SemaphoreType.DMA((6 * nt,))],
    )(*shards)


SEM = pl.BlockSpec(memory_space=pltpu.SEMAPHORE)
ANY = pl.BlockSpec(memory_space=pl.ANY)
SPLIT_COPY = pltpu.SideEffectType.DATAFLOW_SIDE_EFFECTING


def _late_copies(w_refs, land_refs, send_sems, recv_sems):
    x, y, c, chips = _place()
    return [pltpu.make_async_remote_copy(
        src_ref=w_refs[t], dst_ref=land_refs[t].at[2 * x + y], send_sem=send_sems.at[3 * t + j],
        recv_sem=recv_sems.at[3 * t + j], device_id=(cx, cy, c), device_id_type=MESH)
        for t in range(len(w_refs)) for j, (cx, cy) in enumerate(chips)], chips


def _late_gather_start(shards, after):
    nt, na = len(shards), len(after)

    def body(*refs):
        w_refs, land_refs = refs[:nt], refs[nt:2 * nt]
        send_sems, recv_sems, token = refs[2 * nt + na], refs[2 * nt + na + 1], refs[-1]
        copies, _ = _late_copies(w_refs, land_refs, send_sems, recv_sems)
        for cp in copies:
            cp.start()
        token[...] = jnp.zeros_like(token)

    hbm = lambda a: pltpu.with_memory_space_constraint(a, pltpu.HBM)
    lands = [lax.empty((N_CHIPS,) + s.shape, s.dtype) for s in shards]
    outs = pl.pallas_call(
        body, name="late_gather_start",
        out_shape=(pltpu.SemaphoreType.DMA((3 * nt,)), pltpu.SemaphoreType.DMA((3 * nt,)),
                   *[pltpu.HBM(s.shape, s.dtype) for s in shards], *[pltpu.HBM(l.shape, l.dtype) for l in lands],
                   jax.ShapeDtypeStruct((8, LANES), jnp.float32)),
        in_specs=[HBM] * (2 * nt) + [ANY] * na,
        out_specs=(SEM, SEM, *([HBM] * (2 * nt)), pl.BlockSpec(memory_space=pltpu.VMEM)),
        input_output_aliases={i: 2 + i for i in range(2 * nt)},
        compiler_params=pltpu.CompilerParams(has_side_effects=SPLIT_COPY),
    )(*[hbm(s) for s in shards], *[hbm(l) for l in lands], *after)
    return outs[0], outs[1], list(outs[2:2 + nt]), list(outs[2 + nt:2 + 2 * nt]), outs[-1]


def _late_gather_wait(send_sems, recv_sems, shards, lands, after):
    nt = len(shards)

    def body(*refs):
        w_refs, land_refs = refs[:nt], refs[nt:2 * nt]
        s_sems, r_sems = refs[2 * nt], refs[2 * nt + 1]
        x, y, c, chips = _place()
        for t in range(nt):
            for j, (cx, cy) in enumerate(chips):
                cp = pltpu.make_async_remote_copy(
                    src_ref=w_refs[t], dst_ref=land_refs[t].at[2 * cx + cy], send_sem=s_sems.at[3 * t + j],
                    recv_sem=r_sems.at[3 * t + j], device_id=(cx, cy, c), device_id_type=MESH)
                cp.wait_send()
                cp.wait_recv()

    outs = pl.pallas_call(
        body, name="late_gather_wait",
        out_shape=(*[pltpu.HBM(s.shape, s.dtype) for s in shards], *[pltpu.HBM(l.shape, l.dtype) for l in lands]),
        in_specs=[HBM] * (2 * nt) + [SEM, SEM, ANY], out_specs=tuple([HBM] * (2 * nt)),
        input_output_aliases={i: i for i in range(2 * nt)},
        compiler_params=pltpu.CompilerParams(has_side_effects=SPLIT_COPY),
    )(*shards, *lands, send_sems, recv_sems, after)
    return list(outs[nt:])


def _grad_pair_in(grads):
    nt = len(grads)

    def body(*refs):
        g_refs, got_refs = refs[:nt], refs[nt:2 * nt]
        send_sems, recv_sems = refs[2 * nt:]
        x, y, c, _ = _place()
        sibling = (x, y, 1 - c)

        def copy(t, src, dst):
            return pltpu.make_async_remote_copy(src_ref=src, dst_ref=dst, send_sem=send_sems.at[t],
                                                recv_sem=recv_sems.at[t], device_id=sibling, device_id_type=MESH)

        for t in range(nt):
            rh = grads[t].shape[1] // 2
            for p in range(N_CHIPS):
                for off, n in _pieces(rh):
                    copy(t, g_refs[t].at[p, pl.ds((1 - c) * rh + off, n), :], got_refs[t].at[p, pl.ds(off, n), :]).start()
        for t in range(nt):
            rh = grads[t].shape[1] // 2
            copy(t, g_refs[t].at[:, pl.ds((1 - c) * rh, rh), :], got_refs[t]).wait()

    return pl.pallas_call(
        body, name="grad_pair_in",
        out_shape=[jax.ShapeDtypeStruct((N_CHIPS, g.shape[1] // 2, g.shape[2]), g.dtype) for g in grads],
        in_specs=[HBM] * nt, out_specs=[HBM] * nt,
        scratch_shapes=[pltpu.SemaphoreType.DMA((nt,)), pltpu.SemaphoreType.DMA((nt,))],
    )(*grads)


def _pair_in_start(grads):
    nt = len(grads)

    def body(*refs):
        g_refs, land_refs = refs[:nt], refs[nt:2 * nt]
        send_sems, recv_sems, token = refs[2 * nt], refs[2 * nt + 1], refs[-1]
        x, y, c, _ = _place()
        for t in range(nt):
            rh = grads[t].shape[1] // 2
            for p in range(N_CHIPS):
                for off, n in _pieces(rh):
                    pltpu.make_async_remote_copy(
                        src_ref=g_refs[t].at[p, pl.ds((1 - c) * rh + off, n), :], dst_ref=land_refs[t].at[p, pl.ds(off, n), :],
                        send_sem=send_sems.at[t], recv_sem=recv_sems.at[t], device_id=(x, y, 1 - c),
                        device_id_type=MESH).start()
        token[...] = jnp.zeros_like(token)

    hbm = lambda a: pltpu.with_memory_space_constraint(a, pltpu.HBM)
    lands = [lax.empty((N_CHIPS, g.shape[1] // 2, g.shape[2]), g.dtype) for g in grads]
    outs = pl.pallas_call(
        body, name="grad_pair_in_start",
        out_shape=(pltpu.SemaphoreType.DMA((nt,)), pltpu.SemaphoreType.DMA((nt,)),
                   *[pltpu.HBM(g.shape, g.dtype) for g in grads], *[pltpu.HBM(l.shape, l.dtype) for l in lands],
                   jax.ShapeDtypeStruct((8, LANES), jnp.float32)),
        in_specs=[HBM] * (2 * nt),
        out_specs=(SEM, SEM, *([HBM] * (2 * nt)), pl.BlockSpec(memory_space=pltpu.VMEM)),
        input_output_aliases={i: 2 + i for i in range(2 * nt)},
        compiler_params=pltpu.CompilerParams(has_side_effects=SPLIT_COPY),
    )(*[hbm(g) for g in grads], *[hbm(l) for l in lands])
    return outs[0], outs[1], list(outs[2:2 + nt]), list(outs[2 + nt:2 + 2 * nt]), outs[-1]


def _pair_in_wait(send_sems, recv_sems, grads, lands, after):
    nt = len(grads)

    def body(*refs):
        g_refs, land_refs = refs[:nt], refs[nt:2 * nt]
        s_sems, r_sems = refs[2 * nt], refs[2 * nt + 1]
        x, y, c, _ = _place()
        for t in range(nt):
            rh = grads[t].shape[1] // 2
            cp = pltpu.make_async_remote_copy(
                src_ref=g_refs[t].at[:, pl.ds((1 - c) * rh, rh), :], dst_ref=land_refs[t], send_sem=s_sems.at[t],
                recv_sem=r_sems.at[t], device_id=(x, y, 1 - c), device_id_type=MESH)
            cp.wait_send()
            cp.wait_recv()

    outs = pl.pallas_call(
        body, name="grad_pair_in_wait",
        out_shape=(*[pltpu.HBM(g.shape, g.dtype) for g in grads], *[pltpu.HBM(l.shape, l.dtype) for l in lands]),
        in_specs=[HBM] * (2 * nt) + [SEM, SEM, ANY], out_specs=tuple([HBM] * (2 * nt)),
        input_output_aliases={i: i for i in range(2 * nt)},
        compiler_params=pltpu.CompilerParams(has_side_effects=SPLIT_COPY),
    )(*grads, *lands, send_sems, recv_sems, after)
    return list(outs[:nt]), list(outs[nt:])


def _pair_sum(g, got, core, name):
    _, rows, cols = g.shape
    rh = rows // 2
    tr = _tile(rh, 512)
    nb = rh // tr

    def body(c_ref, g_ref, got_ref, s32_ref, s16_ref):
        s = g_ref[...] + got_ref[...]
        s32_ref[...] = s
        s16_ref[...] = s.astype(s16_ref.dtype)

    blk = pl.BlockSpec((None, tr, cols), lambda p, i, c_ref: (p, i, 0))
    return pl.pallas_call(
        body, name=name,
        grid_spec=pltpu.PrefetchScalarGridSpec(
            num_scalar_prefetch=1, grid=(N_CHIPS, nb),
            in_specs=[pl.BlockSpec((None, tr, cols), lambda p, i, c_ref: (p, c_ref[0] * nb + i, 0)), blk],
            out_specs=[blk, blk]),
        out_shape=[jax.ShapeDtypeStruct((N_CHIPS, rh, cols), jnp.float32),
                   jax.ShapeDtypeStruct((N_CHIPS, rh, cols), jnp.bfloat16)],
        compiler_params=_params("parallel", "parallel"),
    )(core, g, got)


def _exchange_start(parts, name):
    nt = len(parts)

    def body(*refs):
        a_refs, land_refs = refs[:nt], refs[nt:2 * nt]
        send_sems, recv_sems, token = refs[2 * nt], refs[2 * nt + 1], refs[-1]
        x, y, c, chips = _place()
        for t in range(nt):
            for j, (cx, cy) in enumerate(chips):
                pltpu.make_async_remote_copy(
                    src_ref=a_refs[t].at[2 * cx + cy], dst_ref=land_refs[t].at[j], send_sem=send_sems.at[3 * t + j],
                    recv_sem=recv_sems.at[3 * t + j], device_id=(cx, cy, c), device_id_type=MESH).start()
        token[...] = jnp.zeros_like(token)

    hbm = lambda a: pltpu.with_memory_space_constraint(a, pltpu.HBM)
    lands = [lax.empty((N_CHIPS - 1,) + a.shape[1:], a.dtype) for a in parts]
    outs = pl.pallas_call(
        body, name=name,
        out_shape=(pltpu.SemaphoreType.DMA((3 * nt,)), pltpu.SemaphoreType.DMA((3 * nt,)),
                   *[pltpu.HBM(a.shape, a.dtype) for a in parts], *[pltpu.HBM(l.shape, l.dtype) for l in lands],
                   jax.ShapeDtypeStruct((8, LANES), jnp.float32)),
        in_specs=[HBM] * (2 * nt),
        out_specs=(SEM, SEM, *([HBM] * (2 * nt)), pl.BlockSpec(memory_space=pltpu.VMEM)),
        input_output_aliases={i: 2 + i for i in range(2 * nt)},
        compiler_params=pltpu.CompilerParams(has_side_effects=SPLIT_COPY),
    )(*[hbm(a) for a in parts], *[hbm(l) for l in lands])
    return outs[0], outs[1], list(outs[2:2 + nt]), list(outs[2 + nt:2 + 2 * nt]), outs[-1]


def _exchange_wait(send_sems, recv_sems, parts, lands, after, name):
    nt = len(parts)

    def body(*refs):
        a_refs, land_refs = refs[:nt], refs[nt:2 * nt]
        s_sems, r_sems = refs[2 * nt], refs[2 * nt + 1]
        x, y, c, chips = _place()
        for t in range(nt):
            for j, (cx, cy) in enumerate(chips):
                cp = pltpu.make_async_remote_copy(
                    src_ref=a_refs[t].at[2 * cx + cy], dst_ref=land_refs[t].at[j], send_sem=s_sems.at[3 * t + j],
                    recv_sem=r_sems.at[3 * t + j], device_id=(cx, cy, c), device_id_type=MESH)
                cp.wait_send()
                cp.wait_recv()

    outs = pl.pallas_call(
        body, name=name,
        out_shape=(*[pltpu.HBM(a.shape, a.dtype) for a in parts], *[pltpu.HBM(l.shape, l.dtype) for l in lands]),
        in_specs=[HBM] * (2 * nt) + [SEM, SEM, ANY], out_specs=tuple([HBM] * (2 * nt)),
        input_output_aliases={i: i for i in range(2 * nt)},
        compiler_params=pltpu.CompilerParams(has_side_effects=SPLIT_COPY),
    )(*parts, *lands, send_sems, recv_sems, after)
    return list(outs[nt:])


def _chip_sum(s32, got, chip, name, behind=None):
    _, rh, cols = s32.shape
    tr = _tile(rh, 512)

    def body(p_ref, s_ref, got_ref, *refs):
        acc = s_ref[...]
        for j in range(N_CHIPS - 1):
            acc = acc + got_ref[j].astype(jnp.float32)
        refs[-1][...] = acc

    extra = [] if behind is None else [behind]
    return pl.pallas_call(
        body, name=name,
        grid_spec=pltpu.PrefetchScalarGridSpec(
            num_scalar_prefetch=1, grid=(rh // tr,),
            in_specs=[pl.BlockSpec((None, tr, cols), lambda i, p_ref: (p_ref[0], i, 0)),
                      pl.BlockSpec((N_CHIPS - 1, tr, cols), lambda i, p_ref: (0, i, 0))]
            + [pl.BlockSpec((8, LANES), lambda i, p_ref: (0, 0))] * len(extra),
            out_specs=pl.BlockSpec((tr, cols), lambda i, p_ref: (i, 0))),
        out_shape=jax.ShapeDtypeStruct((rh, cols), jnp.float32),
        compiler_params=_params("parallel"),
    )(chip, s32, got, *extra)


def _grad_pair_out(halves):
    nt = len(halves)

    def body(*refs):
        h_refs, got_refs = refs[:nt], refs[nt:2 * nt]
        send_sems, recv_sems = refs[2 * nt:]
        x, y, c, _ = _place()
        sibling = (x, y, 1 - c)

        def copy(t, src, dst):
            return pltpu.make_async_remote_copy(src_ref=src, dst_ref=dst, send_sem=send_sems.at[t],
                                                recv_sem=recv_sems.at[t], device_id=sibling, device_id_type=MESH)

        for t in range(nt):
            for off, n in _pieces(halves[t].shape[0]):
                copy(t, h_refs[t].at[pl.ds(off, n), :], got_refs[t].at[pl.ds(off, n), :]).start()
        for t in range(nt):
            copy(t, h_refs[t], got_refs[t]).wait()

    return pl.pallas_call(
        body, name="grad_pair_out",
        out_shape=[jax.ShapeDtypeStruct(h.shape, h.dtype) for h in halves],
        in_specs=[HBM] * nt, out_specs=[HBM] * nt,
        scratch_shapes=[pltpu.SemaphoreType.DMA((nt,)), pltpu.SemaphoreType.DMA((nt,))],
    )(*halves)


def _ada_part(c_all, w_ada):
    L, _, ncol = w_ada.shape
    tn = _tile(ncol, 512)

    def body(c_ref, w_ref, cond_ref, part_ref):
        cv = c_ref[...]
        cond = cv * jax.nn.sigmoid(cv)
        cond_ref[...] = cond
        part_ref[0] = jnp.dot(cond, w_ref[0], precision=lax.Precision.HIGHEST, preferred_element_type=jnp.float32)

    return pl.pallas_call(
        body, name="ada_part", grid=(L, ncol // tn),
        in_specs=[_full((N_DEV, D)), pl.BlockSpec((1, D, tn), lambda l, j: (l, 0, j))],
        out_specs=[_full((N_DEV, D)), pl.BlockSpec((1, N_DEV, tn), lambda l, j: (l, 0, j))],
        out_shape=[jax.ShapeDtypeStruct((N_DEV, D), jnp.float32), jax.ShapeDtypeStruct((L, N_DEV, ncol), jnp.float32)],
        compiler_params=_params("arbitrary", "arbitrary"),
    )(c_all, w_ada)


def _adamw_math(w, g, m, v):
    m = ADAM_B1 * m + (1.0 - ADAM_B1) * g
    v = ADAM_B2 * v + (1.0 - ADAM_B2) * jnp.square(g)
    m_hat = m / (1.0 - ADAM_B1 ** ADAM_STEP)
    v_hat = v / (1.0 - ADAM_B2 ** ADAM_STEP)
    delta = -ADAM_LR * (m_hat / (jnp.sqrt(v_hat) + ADAM_EPS) + ADAM_WD * w)
    return delta, m, v


def _adamw(w, g, m, v, name):
    shape = w.shape
    cols = shape[-1]
    rows = math.prod(shape[:-1])
    tr = _tile(rows, 512)
    two_d = lambda t: t.reshape(rows, cols)

    def body(w_ref, g_ref, m_ref, v_ref, d_ref, mo_ref, vo_ref):
        d_ref[...], mo_ref[...], vo_ref[...] = _adamw_math(w_ref[...], g_ref[...], m_ref[...], v_ref[...])

    out = jax.ShapeDtypeStruct((rows, cols), jnp.float32)
    outs = pl.pallas_call(
        body, name=name, grid=(rows // tr,), in_specs=[_rows(tr, cols)] * 4, out_specs=[_rows(tr, cols)] * 3,
        out_shape=[out, out, out], compiler_params=_params("parallel"),
    )(two_d(w), two_d(g), two_d(m), two_d(v))
    return [t.reshape(shape) for t in outs]


def _adamw_halves(w, mine, got, m, v, core, name):
    shape = w.shape
    cols = shape[-1]
    rows = math.prod(shape[:-1])
    rh = rows // 2
    tr = _tile(rh, 512)
    nbh = rh // tr
    two_d = lambda t: t.reshape(rows, cols)

    def body(c_ref, w_ref, a_ref, b_ref, m_ref, v_ref, g_ref, d_ref, mo_ref, vo_ref):
        g = jnp.where(pl.program_id(0) // nbh == c_ref[0], a_ref[...], b_ref[...])
        g_ref[...] = g
        d_ref[...], mo_ref[...], vo_ref[...] = _adamw_math(w_ref[...], g, m_ref[...], v_ref[...])

    row = pl.BlockSpec((tr, cols), lambda i, c_ref: (i, 0))

    def half(keep):
        return pl.BlockSpec((tr, cols), lambda i, c_ref: (jnp.where((i // nbh == c_ref[0]) == keep, i % nbh, 0), 0))

    out = jax.ShapeDtypeStruct((rows, cols), jnp.float32)
    outs = pl.pallas_call(
        body, name=name,
        grid_spec=pltpu.PrefetchScalarGridSpec(
            num_scalar_prefetch=1, grid=(rows // tr,),
            in_specs=[row, half(True), half(False), row, row], out_specs=[row] * 4),
        out_shape=[out] * 4, compiler_params=_params("arbitrary"),
    )(core, two_d(w), mine, got, two_d(m), two_d(v))
    return [t.reshape(shape) for t in outs]


def _ada_grad_adamw(cond_t, dm, w, m, v):
    L, _, ncol = w.shape
    tn = _tile(ncol, 512)

    def body(ct_ref, dm_ref, w_ref, m_ref, v_ref, g_ref, d_ref, mo_ref, vo_ref):
        g = ct_ref[:, 0:1] * dm_ref[0, 0:1, :]
        for b in range(1, N_DEV):
            g = g + ct_ref[:, b:b + 1] * dm_ref[0, b:b + 1, :]
        g_ref[0] = g
        d_ref[0], mo_ref[0], vo_ref[0] = _adamw_math(w_ref[0], g, m_ref[0], v_ref[0])

    wblk = pl.BlockSpec((1, D, tn), lambda l, j: (l, 0, j))
    out = jax.ShapeDtypeStruct(w.shape, jnp.float32)
    return pl.pallas_call(
        body, name="ada_grad_adamw", grid=(L, ncol // tn),
        in_specs=[_full((D, N_DEV)), pl.BlockSpec((1, N_DEV, tn), lambda l, j: (l, 0, j)), wblk, wblk, wblk],
        out_specs=[wblk] * 4, out_shape=[out] * 4, compiler_params=_params("parallel", "parallel"),
    )(cond_t, dm, w, m, v)


def _small_adamw(gathered, w, m, v):
    slots = _small_slots()
    rows = {name: (off // LANES, -(-n // LANES)) for name, (off, n) in slots.items()}
    kinds = {name: 1 if name == "loss" or name in BIASES else 4 for name in slots}

    def body(ga_ref, w_ref, m_ref, v_ref, *out_refs):
        g = ga_ref[0]
        for dev in range(1, N_DEV):
            g = g + ga_ref[dev]
        d, mo, vo = _adamw_math(w_ref[...], g, m_ref[...], v_ref[...])
        k = 0
        for name, (r0, nr) in rows.items():
            for src in (g, d, mo, vo)[:kinds[name]]:
                out_refs[k][...] = src[r0:r0 + nr, :]
                k += 1

    out_shape = [jax.ShapeDtypeStruct((rows[name][1], LANES), jnp.float32) for name in slots for _ in range(kinds[name])]
    flat = pl.pallas_call(
        body, name="small_adamw", out_shape=out_shape,
        in_specs=[pl.BlockSpec(memory_space=pltpu.VMEM)] * 4,
        out_specs=[pl.BlockSpec(memory_space=pltpu.VMEM)] * len(out_shape),
    )(gathered, w, m, v)
    out, k = {}, 0
    for name in slots:
        out[name] = [_from_slot(name, t) for t in flat[k:k + kinds[name]]]
        k += kinds[name]
    return out


def _one_hot_pick(arr, index, axis):
    n = arr.shape[axis]
    shape = [1] * arr.ndim
    shape[axis] = n
    hot = (jnp.arange(n) == index).astype(arr.dtype).reshape(shape)
    return jnp.sum(arr * hot, axis=axis)


def kernel(x, c, positions, w_ada, b_ada, g_mix, g_mlp, mla_w_dq, mla_g_q, mla_w_uq, mla_w_dkv, mla_g_kv, mla_w_ukv, mla_w_o, swa_w_qkv, swa_b_qkv, swa_sinks, swa_w_o, swa_b_o, w_ff1, w_ff2, g_final, loss_target, m_w_ada, m_b_ada, m_g_mix, m_g_mlp, m_mla_w_dq, m_mla_g_q, m_mla_w_uq, m_mla_w_dkv, m_mla_g_kv, m_mla_w_ukv, m_mla_w_o, m_swa_w_qkv, m_swa_b_qkv, m_swa_sinks, m_swa_w_o, m_swa_b_o, m_w_ff1, m_w_ff2, m_g_final, v_w_ada, v_b_ada, v_g_mix, v_g_mlp, v_mla_w_dq, v_mla_g_q, v_mla_w_uq, v_mla_w_dkv, v_mla_g_kv, v_mla_w_ukv, v_mla_w_o, v_swa_w_qkv, v_swa_b_qkv, v_swa_sinks, v_swa_w_o, v_swa_b_o, v_w_ff1, v_w_ff2, v_g_final):
    W = dict(w_ada=w_ada, b_ada=b_ada, g_mix=g_mix, g_mlp=g_mlp, mla_w_dq=mla_w_dq, mla_g_q=mla_g_q, mla_w_uq=mla_w_uq,
             mla_w_dkv=mla_w_dkv, mla_g_kv=mla_g_kv, mla_w_ukv=mla_w_ukv, mla_w_o=mla_w_o, swa_w_qkv=swa_w_qkv,
             swa_b_qkv=swa_b_qkv, swa_sinks=swa_sinks, swa_w_o=swa_w_o, swa_b_o=swa_b_o, w_ff1=w_ff1, w_ff2=w_ff2,
             g_final=g_final)
    M = dict(w_ada=m_w_ada, b_ada=m_b_ada, g_mix=m_g_mix, g_mlp=m_g_mlp, mla_w_dq=m_mla_w_dq, mla_g_q=m_mla_g_q,
             mla_w_uq=m_mla_w_uq, mla_w_dkv=m_mla_w_dkv, mla_g_kv=m_mla_g_kv, mla_w_ukv=m_mla_w_ukv, mla_w_o=m_mla_w_o,
             swa_w_qkv=m_swa_w_qkv, swa_b_qkv=m_swa_b_qkv, swa_sinks=m_swa_sinks, swa_w_o=m_swa_w_o, swa_b_o=m_swa_b_o,
             w_ff1=m_w_ff1, w_ff2=m_w_ff2, g_final=m_g_final)
    V = dict(w_ada=v_w_ada, b_ada=v_b_ada, g_mix=v_g_mix, g_mlp=v_g_mlp, mla_w_dq=v_mla_w_dq, mla_g_q=v_mla_g_q,
             mla_w_uq=v_mla_w_uq, mla_w_dkv=v_mla_w_dkv, mla_g_kv=v_mla_g_kv, mla_w_ukv=v_mla_w_ukv, mla_w_o=v_mla_w_o,
             swa_w_qkv=v_swa_w_qkv, swa_b_qkv=v_swa_b_qkv, swa_sinks=v_swa_sinks, swa_w_o=v_swa_w_o, swa_b_o=v_swa_b_o,
             w_ff1=v_w_ff1, w_ff2=v_w_ff2, g_final=v_g_final)
    order = list(W)
    names = list(SHARDED)
    core = lax.axis_index("c")
    chip = 2 * lax.axis_index("x") + lax.axis_index("y")
    dev = 2 * chip + core
    core_arr = core.astype(jnp.int32).reshape(1)
    chip_arr = chip.astype(jnp.int32).reshape(1)

    def whole(n, g, own):
        g = lax.dynamic_update_slice(g, own[None], (chip, 0, 0))
        if n in ("w_ff1", "w_ff2"):
            return g
        if n in COL_SPLIT:
            return g.transpose(1, 0, 2).reshape(g.shape[1], N_CHIPS * g.shape[2])
        return g.reshape(N_CHIPS * g.shape[1], g.shape[2])

    early = [n for n in names if n.startswith("mla_")]
    local = {n: W[n].astype(MXU_DTYPE).reshape(_view2d(n)) for n in early}
    wts = {n: whole(n, g, local[n]) for n, g in zip(early, _weight_gather([local[n] for n in early]))}

    nbq, nbo = BIASES["swa_b_qkv"] // N_CHIPS, BIASES["swa_b_o"] // N_CHIPS
    first = jnp.concatenate([c.reshape(-1), swa_b_qkv.reshape(-1), swa_b_o.reshape(-1),
                             jnp.zeros((16 * LANES - D - nbq - nbo,), jnp.float32)]).reshape(16, LANES)
    first_all = _all_gather(first).reshape(N_DEV, 16 * LANES)
    c_all = first_all[:, :D]
    south = first_all[0::2]
    wts["swa_b_qkv"] = south[:, D:D + nbq].reshape(1, N_CHIPS * nbq)
    wts["swa_b_o"] = south[:, D + nbq:D + nbq + nbo].reshape(1, N_CHIPS * nbo)
    cond_all, part = _ada_part(c_all, w_ada)
    ncol = w_ada.shape[2]
    part_all = _all_gather(part.reshape(-1, LANES)).reshape(N_DEV, DEPTH, N_DEV, ncol)
    mine = _one_hot_pick(part_all[0::2], dev, axis=2)
    mod = mine.transpose(1, 0, 2).reshape(DEPTH, N_CHIPS * ncol) + b_ada
    vecs = jnp.concatenate([mod.reshape(DEPTH, 6, D), g_mix[:, None, :], g_mlp[:, None, :]], axis=1)

    late = [("w_ff1", 0), ("w_ff2", 0), ("swa_w_qkv", None), ("swa_w_o", None), ("w_ff1", 1), ("w_ff2", 1)]
    late_local = [(W[n][0] if l is None else W[n][l]).astype(MXU_DTYPE) for n, l in late]
    send_sems, recv_sems, passed, lands, token = _late_gather_start(late_local, [vecs] + [wts[n] for n in early])

    def late_weights(after):
        got = _late_gather_wait(send_sems, recv_sems, passed, lands, after)
        out = {"w_ff1": [None] * DEPTH, "w_ff2": [None] * DEPTH}
        for (n, l), g, own in zip(late, got, late_local):
            if l is None:
                out[n] = whole(n, g, own)
            else:
                out[n][l] = whole(n, g, own)
        return out

    late_names = [n for n in names if n not in early]
    reduce_state = {}

    def on_late_grads(late_grads):
        s_sems, r_sems, passed_g, zones, tok = _pair_in_start([late_grads[n] for n in late_names])
        reduce_state.update(pair=(s_sems, r_sems, passed_g, zones))
        return tok

    def on_late_landed(after):
        gl, got = _pair_in_wait(*reduce_state["pair"], after)
        sums = [_pair_sum(g, s, core_arr, "pair_sum_" + n) for n, g, s in zip(late_names, gl, got)]
        s_sems, r_sems, parts, zones, tok = _exchange_start([s16 for _, s16 in sums], "grad_exchange_start")
        reduce_state.update(sums=sums, split=(s_sems, r_sems, parts, zones))
        return tok

    grad_x, grads, small = _sequence_step(
        x[0], loss_target[0], positions[0], vecs, mla_g_q + token[0, 0], mla_g_kv, swa_sinks, g_final, wts,
        late_weights, on_late_grads, on_late_landed)

    small["b_ada"] = small.pop("dmod")
    small_all = _all_gather(_pack_small(small))
    pk = lambda src: _pack_small({n: src[n] for n in SMALL if n != "loss" and n not in BIASES})
    off, n = _small_slots()["b_ada"]
    dmod_all = small_all.reshape(N_DEV, -1)[:, off:off + n].reshape(N_DEV, DEPTH, N_CHIPS, ncol)
    dm = _one_hot_pick(dmod_all, chip, axis=2).transpose(1, 0, 2)

    gl = [grads[n] for n in early]
    got = _grad_pair_in(gl)
    sums = [_pair_sum(g, s, core_arr, "pair_sum_" + n) for n, g, s in zip(early, gl, got)]
    e_sems, e_rems, e_parts, e_zones, e_tok = _exchange_start([s16 for _, s16 in sums], "mla_exchange_start")

    def finish(tensor_names, sums, others, behind):
        halves = [_chip_sum(s32, o, chip_arr, "chip_sum_" + n, behind) for n, (s32, _), o in zip(tensor_names, sums, others)]
        return {n: _adamw_halves(W[n], mine_h, got_h, M[n], V[n], core_arr, "adamw_" + n)
                for n, mine_h, got_h in zip(tensor_names, halves, _grad_pair_out(halves))}

    late_others = _exchange_wait(*reduce_state["split"], grad_x, "grad_exchange_wait")
    res = finish(late_names, reduce_state["sums"], late_others, e_tok)
    res["w_ada"] = _ada_grad_adamw(cond_all.T, dm, w_ada, m_w_ada, v_w_ada)
    small_res = _small_adamw(small_all, pk(W), pk(M), pk(V))
    early_others = _exchange_wait(e_sems, e_rems, e_parts, e_zones, res["w_ff2"][1], "mla_exchange_wait")
    res.update(finish(early, sums, early_others, None))

    for n, width in BIASES.items():
        g = _one_hot_pick(small_res[n][0].reshape(N_CHIPS, width // N_CHIPS), chip, axis=0).reshape(1, -1)
        res[n] = [g] + _adamw(W[n], g, M[n], V[n], "adamw_" + n)
    for name in order:
        if name not in res:
            res[name] = small_res[name]
    outs = [small_res["loss"][0], grad_x[None]]
    for k in range(4):
        outs += [res[name][k] for name in order]
    return tuple(outs)
```

```python
import math

import jax
import jax.numpy as jnp
import numpy as np
from jax import lax
from jax.experimental import pallas as pl
from jax.experimental.pallas import tpu as pltpu

D = 1024
DEPTH = 2
MLA_HEADS = 8
QK_NOPE = 128
QK_ROPE = 64
V_DIM = 128
Q_LORA = 384
KV_LORA = 256
ROPE_THETA = 10000.0
SWA_HEADS = 16
SWA_KV_HEADS = 4
SWA_HEAD_DIM = 64
SWA_GROUP = SWA_HEADS // SWA_KV_HEADS
WINDOW = 128
D_FF = 4 * D
EPS = 1e-6
ADAM_LR = 0.001
ADAM_B1 = 0.9
ADAM_B2 = 0.999
ADAM_EPS = 1e-08
ADAM_WD = 0.01
ADAM_STEP = 10

N_CHIPS = 4
N_DEV = 8
LANES = 128
QK_EXT = 256
MLA_SCALE = (QK_NOPE + QK_ROPE) ** -0.5
LOG2E = math.log2(math.e)
LN2 = math.log(2.0)
MLA_QSCALE = MLA_SCALE * LOG2E
ATTN_BLOCK = 2048
ATTN_SUB = 512
MLP_FWD_TILE = (1024, 1024)
MLP_BWD_TILE = (512, 1024)
DW_TOKENS = 4096
DW_TILE = 1024
ROW_TILE = 1024
PROJ_ROWS = 512
FIRST_ROWS = 16
STREAM_ROWS = 256
SWA_SCALE = SWA_HEAD_DIM ** -0.5
NEG = -1e30
MXU_DTYPE = jnp.bfloat16
VMEM_LIMIT = 56 * 1024 * 1024

R_SH1, R_SC1, R_GT1, R_SH2, R_SC2, R_GT2, R_GMIX, R_GMLP = range(8)
R_BO = 6


def _tile(n, pref):
    if n <= pref:
        return n
    for t in range(pref, 7, -1):
        if n % t == 0 and t % 8 == 0:
            return t
    return n


def _dot(a, b):
    return jnp.dot(a, b, preferred_element_type=jnp.float32)


def _dot_nt(a, b):
    return lax.dot_general(a, b, (((1,), (1,)), ((), ())), preferred_element_type=jnp.float32)


def _dot_tn(a, b):
    return lax.dot_general(a, b, (((0,), (0,)), ((), ())), preferred_element_type=jnp.float32)


def _rms(x):
    r = lax.rsqrt(jnp.mean(x * x, axis=-1, keepdims=True) + EPS)
    return x * r, r


def _rms_bwd(dxhat, xhat, r):
    return r * (dxhat - xhat * jnp.mean(dxhat * xhat, axis=-1, keepdims=True))


def _rowsum(v):
    return jnp.sum(v, axis=0, keepdims=True)


def _params(*sem):
    return pltpu.CompilerParams(dimension_semantics=sem, vmem_limit_bytes=VMEM_LIMIT)


def _full(shape):
    nd = len(shape)
    return pl.BlockSpec(shape, lambda *_: (0,) * nd)


def _rows(tm, cols):
    return pl.BlockSpec((tm, cols), lambda i, *_: (i, 0))


def _modulate_bwd(dh, x, vec_ref, r_g, r_sc, r_sh, ps_ref, dres):
    xhat, r = _rms(x)
    g = vec_ref[r_g:r_g + 1, :]
    n = xhat * g
    ps_ref[r_sh:r_sh + 1, :] += _rowsum(dh)
    ps_ref[r_sc:r_sc + 1, :] += _rowsum(dh * n)
    dn = dh * (1.0 + vec_ref[r_sc:r_sc + 1, :])
    ps_ref[r_g:r_g + 1, :] += _rowsum(dn * xhat)
    return dres + _rms_bwd(dn * g, xhat, r)


def _mla_pre(x, vec, wcat, g_q, g_kv, wuq, wukv, cs):
    T = x.shape[0]
    tm = _tile(T, PROJ_ROWS)
    H = MLA_HEADS

    def body(x_ref, vec_ref, wcat_ref, gq_ref, gkv_ref, wuq_ref, wukv_ref, cs_ref, h_ref, z_ref, q_ref, k_ref, v_ref):
        xhat, _ = _rms(x_ref[...])
        h = xhat * vec_ref[R_GMIX:R_GMIX + 1, :] * (1.0 + vec_ref[R_SC1:R_SC1 + 1, :]) + vec_ref[R_SH1:R_SH1 + 1, :]
        hb = h.astype(MXU_DTYPE)
        h_ref[...] = hb
        z = _dot(hb, wcat_ref[...])
        z_ref[...] = z
        cq = (_rms(z[:, :Q_LORA])[0] * gq_ref[...]).astype(MXU_DTYPE)
        ckv = (_rms(z[:, Q_LORA:Q_LORA + KV_LORA])[0] * gkv_ref[...]).astype(MXU_DTYPE)
        cs_t = cs_ref[...]
        t = z[:, Q_LORA + KV_LORA:] * cs_t
        k_rope = (t + pltpu.roll(t, QK_ROPE, axis=1)).astype(MXU_DTYPE)
        low = lax.broadcasted_iota(jnp.int32, (1, LANES), 1) < QK_ROPE
        for hd in range(H):
            qf = _dot(cq, wuq_ref[hd])
            tq = qf[:, QK_NOPE:] * cs_t
            tq = tq + pltpu.roll(tq, QK_ROPE, axis=1)
            q_ref[hd, :, :QK_NOPE] = (qf[:, :QK_NOPE] * MLA_QSCALE).astype(MXU_DTYPE)
            q_ref[hd, :, QK_NOPE:] = jnp.where(low, tq * MLA_QSCALE, 0.0).astype(MXU_DTYPE)
            kvf = _dot(ckv, wukv_ref[hd])
            k_ref[hd, :, :QK_NOPE] = kvf[:, :QK_NOPE].astype(MXU_DTYPE)
            k_ref[hd, :, QK_NOPE:] = k_rope
            v_ref[hd] = kvf[:, QK_NOPE:].astype(MXU_DTYPE)

    zc = wcat.shape[1]
    return pl.pallas_call(
        body, name="mla_pre", grid=(T // tm,),
        in_specs=[_rows(tm, D), _full((8, D)), _full(wcat.shape), _full(g_q.shape), _full(g_kv.shape),
                  _full(wuq.shape), _full(wukv.shape), _rows(tm, LANES)],
        out_specs=[_rows(tm, D), _rows(tm, zc),
                   pl.BlockSpec((H, tm, QK_EXT), lambda i: (0, i, 0)),
                   pl.BlockSpec((H, tm, QK_EXT), lambda i: (0, i, 0)),
                   pl.BlockSpec((H, tm, V_DIM), lambda i: (0, i, 0))],
        out_shape=[jax.ShapeDtypeStruct((T, D), MXU_DTYPE), jax.ShapeDtypeStruct((T, zc), jnp.float32),
                   jax.ShapeDtypeStruct((H, T, QK_EXT), MXU_DTYPE), jax.ShapeDtypeStruct((H, T, QK_EXT), MXU_DTYPE),
                   jax.ShapeDtypeStruct((H, T, V_DIM), MXU_DTYPE)],
        compiler_params=_params("parallel"),
    )(x, vec, wcat, g_q, g_kv, wuq, wukv, cs)


def _mla_attn_fwd(q, k, v):
    H, T, _ = q.shape
    tb = _tile(T, ATTN_BLOCK)
    sub = min(ATTN_SUB, tb)
    ns, nb = tb // sub, T // tb
    pairs = [(i, j) for i in range(nb) for j in range(i + 1)]
    qi_tab = jnp.asarray([i for i, _ in pairs], jnp.int32)
    kj_tab = jnp.asarray([j for _, j in pairs], jnp.int32)

    def body(qi_ref, kj_ref, q_ref, k_ref, v_ref, o_ref, lse_ref, m_sc, l_sc, acc_sc):
        qi, kj = qi_ref[pl.program_id(1)], kj_ref[pl.program_id(1)]

        @pl.when(kj == 0)
        def _():
            m_sc[...] = jnp.full_like(m_sc, NEG)
            l_sc[...] = jnp.zeros_like(l_sc)
            acc_sc[...] = jnp.zeros_like(acc_sc)

        def update(r, kk, masked):
            rows, keys = pl.ds(r * sub, sub), pl.ds(kk * sub, sub)
            s = _dot_nt(q_ref[0, rows, :], k_ref[0, keys, :])
            if masked:
                row = lax.broadcasted_iota(jnp.int32, (sub, sub), 0)
                col = lax.broadcasted_iota(jnp.int32, (sub, sub), 1)
                s = jnp.where(col <= row, s, NEG)
            m_prev = m_sc[rows, :]
            m_new = jnp.maximum(m_prev, jnp.max(s, axis=1, keepdims=True))
            alpha = jnp.exp2(m_prev - m_new)
            p = jnp.exp2(s - jnp.tile(m_new, (1, sub // LANES)))
            l_sc[rows, :] = alpha * l_sc[rows, :] + jnp.sum(p, axis=1, keepdims=True)
            acc_sc[rows, :] = alpha * acc_sc[rows, :] + _dot(p.astype(MXU_DTYPE), v_ref[0, keys, :])
            m_sc[rows, :] = m_new

        @pl.when(kj < qi)
        def _():
            for kk in range(ns):
                for r in range(ns):
                    update(r, kk, False)

        @pl.when(kj == qi)
        def _():
            for kk in range(ns):
                for r in range(kk, ns):
                    update(r, kk, r == kk)
            l = l_sc[...]
            o_ref[...] = (acc_sc[...] / l).astype(o_ref.dtype)
            lse = m_sc[...] + jnp.log2(l)
            pick = (lax.broadcasted_iota(jnp.int32, (8, LANES), 1) == 0).astype(jnp.float32)
            row = lax.dot_general(pick, lse, (((1,), (1,)), ((), ())), precision=lax.Precision.HIGHEST,
                                  preferred_element_type=jnp.float32)
            lse_ref[0] = row[0:1, :]

    q_idx = lambda h, p, qi_ref, kj_ref: (h, qi_ref[p], 0)
    kv_idx = lambda h, p, qi_ref, kj_ref: (h, kj_ref[p], 0)
    return pl.pallas_call(
        body, name="mla_attn_fwd",
        grid_spec=pltpu.PrefetchScalarGridSpec(
            num_scalar_prefetch=2, grid=(H, len(pairs)),
            in_specs=[pl.BlockSpec((1, tb, QK_EXT), q_idx), pl.BlockSpec((1, tb, QK_EXT), kv_idx),
                      pl.BlockSpec((1, tb, V_DIM), kv_idx)],
            out_specs=[pl.BlockSpec((tb, V_DIM), lambda h, p, qi_ref, kj_ref: (qi_ref[p], h)),
                       pl.BlockSpec((1, 1, tb), lambda h, p, qi_ref, kj_ref: (h, 0, qi_ref[p]))],
            scratch_shapes=[pltpu.VMEM((tb, LANES), jnp.float32), pltpu.VMEM((tb, LANES), jnp.float32),
                            pltpu.VMEM((tb, V_DIM), jnp.float32)]),
        out_shape=[jax.ShapeDtypeStruct((T, H * V_DIM), MXU_DTYPE), jax.ShapeDtypeStruct((H, 1, T), jnp.float32)],
        compiler_params=_params("parallel", "arbitrary"),
    )(qi_tab, kj_tab, q, k, v)


def _post_attn(o, x, w_o, bias, vec, o_transposed=False):
    T = x.shape[0]
    tm = _tile(T, ROW_TILE)
    o_spec = pl.BlockSpec((D, tm), lambda i: (0, i)) if o_transposed else _rows(tm, D)

    def body(o_ref, x_ref, w_ref, b_ref, vec_ref, y_ref, xm_ref, h_ref):
        y = (_dot_tn if o_transposed else _dot)(o_ref[...], w_ref[...]) + b_ref[...]
        y_ref[...] = y.astype(y_ref.dtype)
        xm = x_ref[...] + vec_ref[R_GT1:R_GT1 + 1, :] * y
        xm_ref[...] = xm
        xhat, _ = _rms(xm)
        h = xhat * vec_ref[R_GMLP:R_GMLP + 1, :] * (1.0 + vec_ref[R_SC2:R_SC2 + 1, :]) + vec_ref[R_SH2:R_SH2 + 1, :]
        h_ref[...] = h.astype(h_ref.dtype)

    return pl.pallas_call(
        body, name="post_attn", grid=(T // tm,),
        in_specs=[o_spec, _rows(tm, D), _full((D, D)), _full((1, D)), _full((8, D))],
        out_specs=[_rows(tm, D), _rows(tm, D), _rows(tm, D)],
        out_shape=[jax.ShapeDtypeStruct((T, D), MXU_DTYPE), jax.ShapeDtypeStruct((T, D), jnp.float32),
                   jax.ShapeDtypeStruct((T, D), MXU_DTYPE)],
        compiler_params=_params("parallel"),
    )(o, x, w_o, bias, vec)


def _ff_specs(tf):
    per = D_FF // N_CHIPS // tf
    w1 = pl.BlockSpec((None, D, tf), lambda i, f: (f // per, 0, f % per))
    w2 = pl.BlockSpec((None, tf, D), lambda i, f: (f // per, f % per, 0))
    return w1, w2


def _mlp_fwd(h2, w1, w2, xm, vec):
    T = h2.shape[0]
    tm = _tile(T, MLP_FWD_TILE[0])
    tf = _tile(D_FF // N_CHIPS, MLP_FWD_TILE[1])
    nf = D_FF // tf
    w1_spec, w2_spec = _ff_specs(tf)

    def body(h_ref, w1_ref, w2_ref, xm_ref, vec_ref, a_ref, y_ref, xo_ref, acc):
        f = pl.program_id(1)

        @pl.when(f == 0)
        def _():
            acc[...] = jnp.zeros_like(acc)

        u = jnp.maximum(_dot(h_ref[...], w1_ref[...]), 0.0)
        ab = (u * u).astype(MXU_DTYPE)
        a_ref[...] = ab
        acc[...] += _dot(ab, w2_ref[...])

        @pl.when(f == nf - 1)
        def _():
            y = acc[...]
            y_ref[...] = y.astype(y_ref.dtype)
            xo_ref[...] = xm_ref[...] + vec_ref[R_GT2:R_GT2 + 1, :] * y

    return pl.pallas_call(
        body, name="mlp_fwd", grid=(T // tm, nf),
        in_specs=[_rows(tm, D), w1_spec, w2_spec, _rows(tm, D), _full((8, D))],
        out_specs=[pl.BlockSpec((tm, tf), lambda i, f: (i, f)), _rows(tm, D), _rows(tm, D)],
        out_shape=[jax.ShapeDtypeStruct((T, D_FF), MXU_DTYPE), jax.ShapeDtypeStruct((T, D), MXU_DTYPE),
                   jax.ShapeDtypeStruct((T, D), jnp.float32)],
        scratch_shapes=[pltpu.VMEM((tm, D), jnp.float32)],
        compiler_params=_params("parallel", "arbitrary"),
    )(h2, w1, w2, xm, vec)


def _swa_pre(x, vec, w_qkv, b_qkv):
    T = x.shape[0]
    tm = _tile(T, PROJ_ROWS)
    nq = SWA_HEADS * SWA_HEAD_DIM
    nk = SWA_KV_HEADS * SWA_HEAD_DIM
    wq_t, w_kv = w_qkv[:, :nq].T, w_qkv[:, nq:]
    bq_col, b_kv = b_qkv[:, :nq].reshape(nq, 1), b_qkv[:, nq:]

    def body(x_ref, vec_ref, wq_ref, wkv_ref, bq_ref, bkv_ref, h_ref, qt_ref, k_ref, v_ref):
        xhat, _ = _rms(x_ref[...])
        h = xhat * vec_ref[R_GMIX:R_GMIX + 1, :] * (1.0 + vec_ref[R_SC1:R_SC1 + 1, :]) + vec_ref[R_SH1:R_SH1 + 1, :]
        hb = h.astype(MXU_DTYPE)
        h_ref[...] = hb
        qt_ref[...] = ((_dot_nt(wq_ref[...], hb) + bq_ref[...]) * SWA_SCALE).astype(MXU_DTYPE)
        kv = _dot(hb, wkv_ref[...]) + bkv_ref[...]
        k_ref[...] = kv[:, :nk].astype(MXU_DTYPE)
        v_ref[...] = kv[:, nk:].astype(MXU_DTYPE)

    return pl.pallas_call(
        body, name="swa_pre", grid=(T // tm,),
        in_specs=[_rows(tm, D), _full((8, D)), _full(wq_t.shape), _full(w_kv.shape), _full(bq_col.shape),
                  _full(b_kv.shape)],
        out_specs=[_rows(tm, D), pl.BlockSpec((nq, tm), lambda i: (0, i)), _rows(tm, nk), _rows(tm, nk)],
        out_shape=[jax.ShapeDtypeStruct((T, D), MXU_DTYPE), jax.ShapeDtypeStruct((nq, T), MXU_DTYPE),
                   jax.ShapeDtypeStruct((T, nk), MXU_DTYPE), jax.ShapeDtypeStruct((T, nk), MXU_DTYPE)],
        compiler_params=_params("parallel"),
    )(x, vec, wq_t, w_kv, bq_col, b_kv)


def _swa_bias():
    W = WINDOW
    slopes = 2.0 ** (-8.0 * np.arange(1, SWA_HEADS + 1) / SWA_HEADS)
    j, i = np.arange(W)[:, None], np.arange(W)[None, :]
    dist = np.where(j > i, W + i - j, i - j)
    bias = -slopes[:, None, None] * dist[None].astype(np.float64)
    bias = bias.reshape(SWA_KV_HEADS, SWA_GROUP, W, W).transpose(0, 2, 1, 3)
    return jnp.asarray(bias.reshape(SWA_KV_HEADS, W, SWA_GROUP * W), jnp.float32)


def _swa_fold_mask():
    W, G = WINDOW, SWA_GROUP
    j = lax.broadcasted_iota(jnp.int32, (W, G * W), 0)
    i = lax.broadcasted_iota(jnp.int32, (W, G * W), 1) & (W - 1)
    return j > i


def _swa_fold(band, up):
    return jnp.where(up, band[:WINDOW], band[WINDOW:])


def _swa_unfold(folded, up):
    zero = jnp.zeros_like(folded)
    return jnp.concatenate([jnp.where(up, folded, zero), jnp.where(up, zero, folded)], axis=0)


SWA_STEP_BLOCKS = 4


def _swa_blocks(T):
    nb = T // WINDOW
    return next(b for b in (SWA_STEP_BLOCKS, 2, 1) if nb % b == 0)


def _swa_views(b, qt_ref, kp_ref, kc_ref):
    W = WINDOW
    prev = kp_ref if b == 0 else kc_ref.at[pl.ds((b - 1) * W, W), :]
    return qt_ref.at[:, pl.ds(b * W, W)], prev, kc_ref.at[pl.ds(b * W, W), :]


def _swa_probs(has_prev, up, kh, qt_ref, kp_ref, kc_ref, bias_ref, sink_ref):
    W, Dh, G = WINDOW, SWA_HEAD_DIM, SWA_GROUP
    qt = jnp.concatenate([qt_ref[(kh * G + g) * Dh:(kh * G + g + 1) * Dh, :] for g in range(G)], axis=1)
    kb = jnp.concatenate([kp_ref[:, kh * Dh:(kh + 1) * Dh], kc_ref[:, kh * Dh:(kh + 1) * Dh]], axis=0)
    s = _swa_fold(_dot(kb, qt), up) + bias_ref[kh]
    if has_prev is not True:
        s = jnp.where(up & jnp.logical_not(has_prev), NEG, s)
    sink = sink_ref[kh]
    m = jnp.maximum(jnp.max(s, axis=0, keepdims=True), sink)
    p = jnp.exp(s - m)
    p_sink = jnp.exp(sink - m)
    inv = 1.0 / (jnp.sum(p, axis=0, keepdims=True) + p_sink)
    return qt, kb, p * inv, p_sink * inv


def _swa_attn_fwd(qt, k, v, bias, sink_rows):
    T = qt.shape[1]
    W, Dh, G, Hk = WINDOW, SWA_HEAD_DIM, SWA_GROUP, SWA_KV_HEADS
    nk = Hk * Dh

    nb = _swa_blocks(T)

    def body(qt_ref, kp_ref, kc_ref, vp_ref, vc_ref, bias_ref, sink_ref, ot_ref):
        n = pl.program_id(0)
        up = _swa_fold_mask()
        for b in range(nb):
            q_b, kp_b, kc_b = _swa_views(b, qt_ref, kp_ref, kc_ref)
            _, vp_b, vc_b = _swa_views(b, qt_ref, vp_ref, vc_ref)
            for kh in range(Hk):
                _, _, pn, _ = _swa_probs(True if b else n > 0, up, kh, q_b, kp_b, kc_b, bias_ref, sink_ref)
                vb = jnp.concatenate([vp_b[:, kh * Dh:(kh + 1) * Dh], vc_b[:, kh * Dh:(kh + 1) * Dh]], axis=0)
                ot = _dot_tn(vb, _swa_unfold(pn, up).astype(MXU_DTYPE))
                for g in range(G):
                    rows = pl.ds((kh * G + g) * Dh, Dh)
                    ot_ref[rows, pl.ds(b * W, W)] = ot[:, g * W:(g + 1) * W].astype(ot_ref.dtype)

    prev = lambda n: (jnp.maximum(n * nb - 1, 0), 0)
    cur = lambda n: (n, 0)
    col = lambda n: (0, n)
    return pl.pallas_call(
        body, name="swa_attn_fwd", grid=(T // (nb * W),),
        in_specs=[pl.BlockSpec((D, nb * W), col), pl.BlockSpec((W, nk), prev), pl.BlockSpec((nb * W, nk), cur),
                  pl.BlockSpec((W, nk), prev), pl.BlockSpec((nb * W, nk), cur), _full(bias.shape),
                  _full(sink_rows.shape)],
        out_specs=pl.BlockSpec((D, nb * W), col),
        out_shape=jax.ShapeDtypeStruct((D, T), MXU_DTYPE),
        compiler_params=_params("parallel"),
    )(qt, k, k, v, v, bias, sink_rows)


def _final_loss(x, tgt, g):
    T = x.shape[0]
    tm = _tile(T, ROW_TILE)

    def body(x_ref, t_ref, g_ref, loss_ref, dx_ref, dg_ref):
        @pl.when(pl.program_id(0) == 0)
        def _():
            loss_ref[...] = jnp.zeros_like(loss_ref)
            dg_ref[...] = jnp.zeros_like(dg_ref)

        xhat, r = _rms(x_ref[...])
        gv = g_ref[...]
        e = xhat * gv - t_ref[...]
        loss_ref[...] += 0.5 * jnp.sum(jnp.mean(e * e, axis=-1, keepdims=True), axis=0, keepdims=True)
        dy = e * (1.0 / D)
        dg_ref[...] += _rowsum(dy * xhat)
        dx_ref[...] = _rms_bwd(dy * gv, xhat, r)

    return pl.pallas_call(
        body, name="final_loss", grid=(T // tm,),
        in_specs=[_rows(tm, D), _rows(tm, D), _full((1, D))],
        out_specs=[_full((8, LANES)), _rows(tm, D), _full((1, D))],
        out_shape=[jax.ShapeDtypeStruct((8, LANES), jnp.float32), jax.ShapeDtypeStruct((T, D), jnp.float32),
                   jax.ShapeDtypeStruct((1, D), jnp.float32)],
        compiler_params=_params("arbitrary"),
    )(x, tgt, g)


def _mlp_bwd(dxo, y2, a, w1, w2, xm, vec):
    T = dxo.shape[0]
    tm = _tile(T, MLP_BWD_TILE[0])
    tf = _tile(D_FF // N_CHIPS, MLP_BWD_TILE[1])
    nf = D_FF // tf
    w1_spec, w2_spec = _ff_specs(tf)

    def body(dxo_ref, y_ref, a_ref, w1_ref, w2_ref, xm_ref, vec_ref, du_ref, dy_ref, dxm_ref, ps_ref, dyb, acc):
        i, f = pl.program_id(0), pl.program_id(1)

        @pl.when((i == 0) & (f == 0))
        def _():
            ps_ref[...] = jnp.zeros_like(ps_ref)

        @pl.when(f == 0)
        def _():
            dxo_t = dxo_ref[...]
            d = (dxo_t * vec_ref[R_GT2:R_GT2 + 1, :]).astype(MXU_DTYPE)
            dyb[...] = d
            dy_ref[...] = d
            acc[...] = jnp.zeros_like(acc)
            ps_ref[R_GT2:R_GT2 + 1, :] += _rowsum(dxo_t * y_ref[...].astype(jnp.float32))

        da = _dot_nt(dyb[...], w2_ref[...])
        dub = (da * (2.0 * jnp.sqrt(a_ref[...].astype(jnp.float32)))).astype(MXU_DTYPE)
        du_ref[...] = dub
        acc[...] += _dot_nt(dub, w1_ref[...])

        @pl.when(f == nf - 1)
        def _():
            dxm_ref[...] = _modulate_bwd(acc[...], xm_ref[...], vec_ref, R_GMLP, R_SC2, R_SH2, ps_ref, dxo_ref[...])

    return pl.pallas_call(
        body, name="mlp_bwd", grid=(T // tm, nf),
        in_specs=[_rows(tm, D), _rows(tm, D), pl.BlockSpec((tm, tf), lambda i, f: (i, f)), w1_spec, w2_spec,
                  _rows(tm, D), _full((8, D))],
        out_specs=[pl.BlockSpec((tm, tf), lambda i, f: (i, f)), _rows(tm, D), _rows(tm, D), _full((8, D))],
        out_shape=[jax.ShapeDtypeStruct((T, D_FF), MXU_DTYPE), jax.ShapeDtypeStruct((T, D), MXU_DTYPE),
                   jax.ShapeDtypeStruct((T, D), jnp.float32), jax.ShapeDtypeStruct((8, D), jnp.float32)],
        scratch_shapes=[pltpu.VMEM((tm, D), MXU_DTYPE), pltpu.VMEM((tm, D), jnp.float32)],
        compiler_params=_params("arbitrary", "arbitrary"),
    )(dxo, y2, a, w1, w2, xm, vec)


def _mm_tn(a, g, name, split=None, layers=1, layer=0, into=None, a_transposed=False):
    K, T = a.shape if a_transposed else a.shape[::-1]
    N = g.shape[1]
    kq = K // N_CHIPS if split == "rows" else K
    nq = N // N_CHIPS if split == "cols" else N
    bk, bn, bt = _tile(kq, DW_TILE), _tile(nq, DW_TILE), _tile(T, DW_TOKENS)
    if nq % bn or bn % LANES:
        bn = nq
    kper, nper = kq // bk, nq // bn

    def body(*refs):
        a_ref, g_ref, o_ref = refs[0], refs[1], refs[-1]

        @pl.when(pl.program_id(2) == 0)
        def _():
            o_ref[...] = jnp.zeros_like(o_ref)

        o_ref[...] += (_dot if a_transposed else _dot_tn)(a_ref[...], g_ref[...])

    a_spec = pl.BlockSpec((bk, bt), lambda k, n, t: (k, t)) if a_transposed else pl.BlockSpec((bt, bk), lambda k, n, t: (t, k))
    in_specs = [a_spec, pl.BlockSpec((bt, bn), lambda k, n, t: (t, n))]
    args = [a, g]
    aliases = {}
    if split is None:
        out_spec = pl.BlockSpec((bk, bn), lambda k, n, t: (k, n))
        out_shape = jax.ShapeDtypeStruct((K, N), jnp.float32)
    else:
        if split == "cols":
            idx = lambda k, n, t: (n // nper, layer, k, n % nper)
        else:
            idx = lambda k, n, t: (k // kper, layer, k % kper, n)
        out_spec = pl.BlockSpec((None, None, bk, bn), idx)
        out_shape = jax.ShapeDtypeStruct((N_CHIPS, layers, kq, nq), jnp.float32)
        if into is not None:
            in_specs.append(pl.BlockSpec(memory_space=pl.ANY))
            args.append(into)
            aliases = {2: 0}
    return pl.pallas_call(
        body, name=name, grid=(K // bk, N // bn, T // bt), in_specs=in_specs, out_specs=out_spec, out_shape=out_shape,
        input_output_aliases=aliases, compiler_params=_params("parallel", "parallel", "arbitrary"),
    )(*args)


def _attn_out_bwd(dxm, y1, o, w_o, vec, with_delta):
    T = dxm.shape[0]
    tm = _tile(T, ROW_TILE)
    H = MLA_HEADS

    def body(dxm_ref, y_ref, w_ref, vec_ref, *refs):
        o_ref = refs[0] if with_delta else None
        dy_ref, do_ref, ps_ref, *delta_ref = refs[1:] if with_delta else refs

        @pl.when(pl.program_id(0) == 0)
        def _():
            ps_ref[...] = jnp.zeros_like(ps_ref)

        dxm_t = dxm_ref[...]
        dy = dxm_t * vec_ref[R_GT1:R_GT1 + 1, :]
        ps_ref[R_GT1:R_GT1 + 1, :] += _rowsum(dxm_t * y_ref[...].astype(jnp.float32))
        ps_ref[R_BO:R_BO + 1, :] += _rowsum(dy)
        dyb = dy.astype(MXU_DTYPE)
        dy_ref[...] = dyb
        if not with_delta:
            do_ref[...] = _dot_nt(w_ref[...], dyb).astype(do_ref.dtype)
        else:
            do = _dot_nt(dyb, w_ref[...])
            do_ref[...] = do.astype(do_ref.dtype)
            of = o_ref[...].astype(jnp.float32)
            ones = jnp.ones((8, V_DIM), jnp.float32)
            for hd in range(H):
                sl = slice(hd * V_DIM, (hd + 1) * V_DIM)
                d = lax.dot_general(ones, do[:, sl] * of[:, sl], (((1,), (1,)), ((), ())),
                                    precision=lax.Precision.HIGHEST, preferred_element_type=jnp.float32)
                delta_ref[0][hd] = d[0:1, :]

    out_specs = [_rows(tm, D), _rows(tm, D), _full((8, D))]
    out_shape = [jax.ShapeDtypeStruct((T, D), MXU_DTYPE), jax.ShapeDtypeStruct((T, D), MXU_DTYPE),
                 jax.ShapeDtypeStruct((8, D), jnp.float32)]
    if not with_delta:
        out_specs[1] = pl.BlockSpec((D, tm), lambda i: (0, i))
        out_shape[1] = jax.ShapeDtypeStruct((D, T), MXU_DTYPE)
    if with_delta:
        out_specs.append(pl.BlockSpec((H, 1, tm), lambda i: (0, 0, i)))
        out_shape.append(jax.ShapeDtypeStruct((H, 1, T), jnp.float32))
    return pl.pallas_call(
        body, name="attn_out_bwd_mla" if with_delta else "attn_out_bwd_swa", grid=(T // tm,),
        in_specs=[_rows(tm, D), _rows(tm, D), _full((D, D)), _full((8, D))] + ([_rows(tm, D)] if with_delta else []),
        out_specs=out_specs, out_shape=out_shape,
        compiler_params=_params("arbitrary"),
    )(dxm, y1, w_o, vec, *([o] if with_delta else []))


def _mla_attn_bwd(q, k, v, do, lse, delta):
    H, T, _ = q.shape
    tb = _tile(T, ATTN_BLOCK)
    sub = min(ATTN_SUB, tb)
    ns, nb = tb // sub, T // tb

    pairs = [(j, i) for j in range(nb) for i in range(j, nb)]
    kj_tab = jnp.asarray([j for j, _ in pairs], jnp.int32)
    qi_tab = jnp.asarray([i for _, i in pairs], jnp.int32)

    def body(kj_ref, qi_ref, q_ref, k_ref, v_ref, do_ref, lse_ref, dl_ref, dq_ref, dk_ref, dv_ref, dk_acc, dv_acc):
        j, i = kj_ref[pl.program_id(1)], qi_ref[pl.program_id(1)]

        @pl.when((j == 0) & (i == 0))
        def _():
            dq_ref[...] = jnp.zeros_like(dq_ref)

        def update(kk, r, masked):
            keys, rows = pl.ds(kk * sub, sub), pl.ds(r * sub, sub)
            kb, qb, dob = k_ref[0, keys, :], q_ref[0, rows, :], do_ref[rows, :]
            st = _dot_nt(kb, qb)
            if masked:
                row = lax.broadcasted_iota(jnp.int32, (sub, sub), 0)
                col = lax.broadcasted_iota(jnp.int32, (sub, sub), 1)
                st = jnp.where(row <= col, st, NEG)
            pt = jnp.exp2(st - lse_ref[0, :, rows])
            dv_acc[keys, :] += _dot(pt.astype(MXU_DTYPE), dob)
            dpt = _dot_nt(v_ref[0, keys, :], dob)
            dst = (pt * (dpt - dl_ref[0, :, rows])).astype(MXU_DTYPE)
            dk_acc[keys, :] += _dot(dst, qb)
            q_rows = pl.ds(pl.multiple_of(i * tb + r * sub, sub), sub)
            dq_ref[0, q_rows, :] += _dot_tn(dst, kb)

        @pl.when(i == j)
        def _():
            dk_acc[...] = jnp.zeros_like(dk_acc)
            dv_acc[...] = jnp.zeros_like(dv_acc)
            for r in range(ns):
                for kk in range(r + 1):
                    update(kk, r, kk == r)

        @pl.when(i > j)
        def _():
            for r in range(ns):
                for kk in range(ns):
                    update(kk, r, False)

        @pl.when(i == nb - 1)
        def _():
            dk_ref[0] = (dk_acc[...] * LN2).astype(dk_ref.dtype)
            dv_ref[0] = dv_acc[...].astype(dv_ref.dtype)

    q_idx = lambda h, p, kj_ref, qi_ref: (h, qi_ref[p], 0)
    kv_idx = lambda h, p, kj_ref, qi_ref: (h, kj_ref[p], 0)
    stat_idx = lambda h, p, kj_ref, qi_ref: (h, 0, qi_ref[p])
    return pl.pallas_call(
        body, name="mla_attn_bwd",
        grid_spec=pltpu.PrefetchScalarGridSpec(
            num_scalar_prefetch=2, grid=(H, len(pairs)),
            in_specs=[pl.BlockSpec((1, tb, QK_EXT), q_idx), pl.BlockSpec((1, tb, QK_EXT), kv_idx),
                      pl.BlockSpec((1, tb, V_DIM), kv_idx),
                      pl.BlockSpec((tb, V_DIM), lambda h, p, kj_ref, qi_ref: (qi_ref[p], h)),
                      pl.BlockSpec((1, 1, tb), stat_idx), pl.BlockSpec((1, 1, tb), stat_idx)],
            out_specs=[pl.BlockSpec((1, T, QK_EXT), lambda h, p, kj_ref, qi_ref: (h, 0, 0)),
                       pl.BlockSpec((1, tb, QK_EXT), kv_idx), pl.BlockSpec((1, tb, V_DIM), kv_idx)],
            scratch_shapes=[pltpu.VMEM((tb, QK_EXT), jnp.float32), pltpu.VMEM((tb, V_DIM), jnp.float32)]),
        out_shape=[jax.ShapeDtypeStruct((H, T, QK_EXT), jnp.float32), jax.ShapeDtypeStruct((H, T, QK_EXT), MXU_DTYPE),
                   jax.ShapeDtypeStruct((H, T, V_DIM), MXU_DTYPE)],
        compiler_params=_params("parallel", "arbitrary"),
    )(kj_tab, qi_tab, q, k, v, do, lse, delta)


def _mla_pre_bwd(x, dxm, vec, hb, z, dq, dk, dv, cs, wcat, g_q, g_kv, wuq, wukv):
    T = x.shape[0]
    tm = _tile(T, PROJ_ROWS)
    H = MLA_HEADS
    zc = wcat.shape[1]

    def body(x_ref, dxm_ref, vec_ref, h_ref, z_ref, dq_ref, dk_ref, dv_ref, cs_ref, wcat_ref, gq_ref, gkv_ref,
             wuq_ref, wukv_ref, dx_ref, ps_ref, dgq_ref, dgkv_ref, dwcat_ref, dwuq_ref, dwukv_ref):
        @pl.when(pl.program_id(0) == 0)
        def _():
            for ref in (ps_ref, dgq_ref, dgkv_ref, dwcat_ref, dwuq_ref, dwukv_ref):
                ref[...] = jnp.zeros_like(ref)

        z = z_ref[...]
        cs_t = cs_ref[...]
        cqhat, rq = _rms(z[:, :Q_LORA])
        ckhat, rk = _rms(z[:, Q_LORA:Q_LORA + KV_LORA])
        gq, gkv = gq_ref[...], gkv_ref[...]
        cq = (cqhat * gq).astype(MXU_DTYPE)
        ckv = (ckhat * gkv).astype(MXU_DTYPE)
        dcq = jnp.zeros((tm, Q_LORA), jnp.float32)
        dckv = jnp.zeros((tm, KV_LORA), jnp.float32)
        dkr = jnp.zeros((tm, LANES), jnp.float32)
        for hd in range(H):
            dqh = dq_ref[hd] * MLA_SCALE
            gqh = jnp.concatenate([dqh[:, :QK_NOPE], dqh[:, QK_NOPE:] * cs_t], axis=1).astype(MXU_DTYPE)
            dcq += _dot_nt(gqh, wuq_ref[hd])
            dwuq_ref[hd] += _dot_tn(cq, gqh)
            dkh = dk_ref[hd]
            gkvh = jnp.concatenate([dkh[:, :QK_NOPE], dv_ref[hd]], axis=1)
            dckv += _dot_nt(gkvh, wukv_ref[hd])
            dwukv_ref[hd] += _dot_tn(ckv, gkvh)
            dkr += dkh[:, QK_NOPE:].astype(jnp.float32)
        dgq_ref[...] += _rowsum(dcq * cqhat)
        dgkv_ref[...] += _rowsum(dckv * ckhat)
        dcq_pre = _rms_bwd(dcq * gq, cqhat, rq)
        dckv_pre = _rms_bwd(dckv * gkv, ckhat, rk)
        dkr2 = (dkr + pltpu.roll(dkr, QK_ROPE, axis=1)) * cs_t
        dz = jnp.concatenate([dcq_pre, dckv_pre, dkr2], axis=1).astype(MXU_DTYPE)
        dwcat_ref[...] += _dot_tn(h_ref[...], dz)
        dh = _dot_nt(dz, wcat_ref[...])
        dx_ref[...] = _modulate_bwd(dh, x_ref[...], vec_ref, R_GMIX, R_SC1, R_SH1, ps_ref, dxm_ref[...])

    hblk = lambda w: pl.BlockSpec((H, tm, w), lambda i: (0, i, 0))
    return pl.pallas_call(
        body, name="mla_pre_bwd", grid=(T // tm,),
        in_specs=[_rows(tm, D), _rows(tm, D), _full((8, D)), _rows(tm, D), _rows(tm, zc), hblk(QK_EXT), hblk(QK_EXT),
                  hblk(V_DIM), _rows(tm, LANES), _full(wcat.shape), _full(g_q.shape), _full(g_kv.shape),
                  _full(wuq.shape), _full(wukv.shape)],
        out_specs=[_rows(tm, D), _full((8, D)), _full(g_q.shape), _full(g_kv.shape), _full(wcat.shape),
                   _full(wuq.shape), _full(wukv.shape)],
        out_shape=[jax.ShapeDtypeStruct((T, D), jnp.float32), jax.ShapeDtypeStruct((8, D), jnp.float32),
                   jax.ShapeDtypeStruct(g_q.shape, jnp.float32), jax.ShapeDtypeStruct(g_kv.shape, jnp.float32),
                   jax.ShapeDtypeStruct(wcat.shape, jnp.float32), jax.ShapeDtypeStruct(wuq.shape, jnp.float32),
                   jax.ShapeDtypeStruct(wukv.shape, jnp.float32)],
        compiler_params=_params("arbitrary"),
    )(x, dxm, vec, hb, z, dq, dk, dv, cs, wcat, g_q, g_kv, wuq, wukv)


def _swa_attn_bwd(qt, k, v, dot_, bias, sink_rows):
    T = qt.shape[1]
    W, Dh, G, Hk = WINDOW, SWA_HEAD_DIM, SWA_GROUP, SWA_KV_HEADS
    nk = Hk * Dh
    nb = _swa_blocks(T)

    def body(qt_ref, kp_ref, kc_ref, vp_ref, vc_ref, dot_ref, bias_ref, sink_ref, dqt_ref, dk_ref, dv_ref, dsink_ref):
        n = pl.program_id(0)

        @pl.when(n == 0)
        def _():
            dk_ref[...] = jnp.zeros_like(dk_ref)
            dv_ref[...] = jnp.zeros_like(dv_ref)
            dsink_ref[...] = jnp.zeros_like(dsink_ref)

        def add_rows(first_row, dkb_part, dvb_part):
            rows = pl.ds(pl.multiple_of(first_row, W), W)
            dk_ref[rows, :] += dkb_part
            dv_ref[rows, :] += dvb_part

        up = _swa_fold_mask()
        for b in range(nb):
            q_b, kp_b, kc_b = _swa_views(b, qt_ref, kp_ref, kc_ref)
            do_b, vp_b, vc_b = _swa_views(b, dot_ref, vp_ref, vc_ref)
            dks, dvs = [], []
            for kh in range(Hk):
                qt, kb, pn, p_sink = _swa_probs(True if b else n > 0, up, kh, q_b, kp_b, kc_b, bias_ref, sink_ref)
                vb = jnp.concatenate([vp_b[:, kh * Dh:(kh + 1) * Dh], vc_b[:, kh * Dh:(kh + 1) * Dh]], axis=0)
                dot_h = jnp.concatenate([do_b[(kh * G + g) * Dh:(kh * G + g + 1) * Dh, :] for g in range(G)], axis=1)
                dp = _swa_fold(_dot(vb, dot_h), up)
                delta = jnp.sum(pn * dp, axis=0, keepdims=True)
                dsb = _swa_unfold(pn * (dp - delta), up).astype(MXU_DTYPE)
                dsink_ref[kh] += -p_sink * delta
                dqt = _dot_tn(kb, dsb) * SWA_SCALE
                for g in range(G):
                    dqt_ref[pl.ds((kh * G + g) * Dh, Dh), pl.ds(b * W, W)] = dqt[:, g * W:(g + 1) * W]
                dks.append(_dot_nt(dsb, qt))
                dvs.append(_dot_nt(_swa_unfold(pn, up).astype(MXU_DTYPE), dot_h))
            dkb = jnp.concatenate(dks, axis=1)
            dvb = jnp.concatenate(dvs, axis=1)
            add_rows((n * nb + b) * W, dkb[W:], dvb[W:])
            if b:
                add_rows((n * nb + b - 1) * W, dkb[:W], dvb[:W])
            else:
                @pl.when(n > 0)
                def _():
                    add_rows((n * nb - 1) * W, dkb[:W], dvb[:W])

    prev = lambda n: (jnp.maximum(n * nb - 1, 0), 0)
    cur = lambda n: (n, 0)
    col = lambda n: (0, n)
    return pl.pallas_call(
        body, name="swa_attn_bwd", grid=(T // (nb * W),),
        in_specs=[pl.BlockSpec((D, nb * W), col), pl.BlockSpec((W, nk), prev), pl.BlockSpec((nb * W, nk), cur),
                  pl.BlockSpec((W, nk), prev), pl.BlockSpec((nb * W, nk), cur), pl.BlockSpec((D, nb * W), col),
                  _full(bias.shape), _full(sink_rows.shape)],
        out_specs=[pl.BlockSpec((D, nb * W), col), _full((T, nk)), _full((T, nk)), _full(sink_rows.shape)],
        out_shape=[jax.ShapeDtypeStruct((D, T), jnp.float32), jax.ShapeDtypeStruct((T, nk), jnp.float32),
                   jax.ShapeDtypeStruct((T, nk), jnp.float32), jax.ShapeDtypeStruct(sink_rows.shape, jnp.float32)],
        compiler_params=_params("arbitrary"),
    )(qt, k, k, v, v, dot_, bias, sink_rows)


def _swa_pre_bwd(x, dxm, vec, dq_t, dk, dv, w_qkv):
    T = x.shape[0]
    tm = _tile(T, PROJ_ROWS)
    nq = SWA_HEADS * SWA_HEAD_DIM
    nk = SWA_KV_HEADS * SWA_HEAD_DIM
    nqkv = nq + 2 * nk

    def body(x_ref, dxm_ref, vec_ref, dq_ref, dk_ref, dv_ref, w_ref, dx_ref, dqkv_ref, ps_ref, db_ref):
        @pl.when(pl.program_id(0) == 0)
        def _():
            ps_ref[...] = jnp.zeros_like(ps_ref)
            db_ref[...] = jnp.zeros_like(db_ref)

        dqkv = jnp.concatenate([dq_ref[...].T, dk_ref[...], dv_ref[...]], axis=1)
        db_ref[...] += _rowsum(dqkv)
        dqkv_b = dqkv.astype(MXU_DTYPE)
        dqkv_ref[...] = dqkv_b
        dh = _dot_nt(dqkv_b, w_ref[...])
        dx_ref[...] = _modulate_bwd(dh, x_ref[...], vec_ref, R_GMIX, R_SC1, R_SH1, ps_ref, dxm_ref[...])

    return pl.pallas_call(
        body, name="swa_pre_bwd", grid=(T // tm,),
        in_specs=[_rows(tm, D), _rows(tm, D), _full((8, D)), pl.BlockSpec((nq, tm), lambda i: (0, i)), _rows(tm, nk),
                  _rows(tm, nk), _full(w_qkv.shape)],
        out_specs=[_rows(tm, D), _rows(tm, nqkv), _full((8, D)), _full((1, nqkv))],
        out_shape=[jax.ShapeDtypeStruct((T, D), jnp.float32), jax.ShapeDtypeStruct((T, nqkv), MXU_DTYPE),
                   jax.ShapeDtypeStruct((8, D), jnp.float32), jax.ShapeDtypeStruct((1, nqkv), jnp.float32)],
        compiler_params=_params("arbitrary"),
    )(x, dxm, vec, dq_t, dk, dv, w_qkv)


def _rot_cols(w):
    half = QK_ROPE // 2
    return jnp.concatenate([-w[..., half:], w[..., :half]], axis=-1)


def _unrot_grad(d_rope, d_rot):
    half = QK_ROPE // 2
    return d_rope + jnp.concatenate([d_rot[..., half:], -d_rot[..., :half]], axis=-1)


def _rope_table(positions):
    half = QK_ROPE // 2
    inv_freq = ROPE_THETA ** (-jnp.arange(half, dtype=jnp.float32) / half)
    ang = positions.astype(jnp.float32)[:, None] * inv_freq
    cos, sin = jnp.cos(ang), jnp.sin(ang)
    return jnp.concatenate([cos, cos, sin, sin], axis=1)


def _sequence_step(x, tgt, positions, vecs, g_q, g_kv, sinks, g_final, wts, late_weights, on_late_grads, on_late_landed):
    H = MLA_HEADS
    cs = _rope_table(positions)
    w_dkv = wts["mla_w_dkv"]
    wcat = jnp.concatenate([wts["mla_w_dq"], w_dkv, _rot_cols(w_dkv[:, KV_LORA:])], axis=1)
    uq = wts["mla_w_uq"].reshape(Q_LORA, H, QK_NOPE + QK_ROPE)
    wuq = jnp.concatenate([uq, _rot_cols(uq[..., QK_NOPE:])], axis=-1).transpose(1, 0, 2)
    wukv = wts["mla_w_ukv"].reshape(KV_LORA, H, QK_NOPE + V_DIM).transpose(1, 0, 2)
    zero_bias = jnp.zeros((1, D), jnp.float32)
    bias = _swa_bias()
    sink_rows = jnp.broadcast_to(sinks.reshape(SWA_KV_HEADS, 1, SWA_GROUP, 1),
                                 (SWA_KV_HEADS, 1, SWA_GROUP, WINDOW)).reshape(SWA_KV_HEADS, 1, SWA_GROUP * WINDOW)

    h1a, z, q, k, v = _mla_pre(x, vecs[0], wcat, g_q, g_kv, wuq, wukv, cs)
    o_a, lse = _mla_attn_fwd(q, k, v)
    y1a, xm_a, h2a = _post_attn(o_a, x, wts["mla_w_o"], zero_bias, vecs[0])
    wts = {**wts, **late_weights(h2a)}
    a_a, y2a, x1 = _mlp_fwd(h2a, wts["w_ff1"][0], wts["w_ff2"][0], xm_a, vecs[0])

    h1b, qs_t, ks, vs = _swa_pre(x1, vecs[1], wts["swa_w_qkv"], wts["swa_b_qkv"])
    o_bt = _swa_attn_fwd(qs_t, ks, vs, bias, sink_rows)
    y1b, xm_b, h2b = _post_attn(o_bt, x1, wts["swa_w_o"], wts["swa_b_o"], vecs[1], o_transposed=True)
    a_b, y2b, x2 = _mlp_fwd(h2b, wts["w_ff1"][1], wts["w_ff2"][1], xm_b, vecs[1])

    loss8, dx2, dg_final = _final_loss(x2, tgt, g_final.reshape(1, D))

    du_b, dy2b, dxm_b, ps_mlp_b = _mlp_bwd(dx2, y2b, a_b, wts["w_ff1"][1], wts["w_ff2"][1], xm_b, vecs[1])
    g_ff2 = _mm_tn(a_b, dy2b, "dw_ff2_l1", "rows", DEPTH, 1)
    g_ff1 = _mm_tn(h2b, du_b, "dw_ff1_l1", "cols", DEPTH, 1)
    dy1b, do_bt, ps_out_b = _attn_out_bwd(dxm_b, y1b, None, wts["swa_w_o"], vecs[1], False)
    g_swa_o = _mm_tn(o_bt, dy1b, "dw_o_swa", a_transposed=True)
    dqs_t, dks, dvs, dsinks = _swa_attn_bwd(qs_t, ks, vs, do_bt, bias, sink_rows)
    dx1, dqkv, ps_pre_b, g_swa_bqkv = _swa_pre_bwd(x1, dxm_b, vecs[1], dqs_t, dks, dvs, wts["swa_w_qkv"])
    g_swa_qkv = _mm_tn(h1b, dqkv, "dw_qkv", "cols")

    du_a, dy2a, dxm_a, ps_mlp_a = _mlp_bwd(dx1, y2a, a_a, wts["w_ff1"][0], wts["w_ff2"][0], xm_a, vecs[0])
    g_ff2 = _mm_tn(a_a, dy2a, "dw_ff2_l0", "rows", DEPTH, 0, g_ff2)
    g_ff1 = _mm_tn(h2a, du_a, "dw_ff1_l0", "cols", DEPTH, 0, g_ff1)
    rows4 = lambda g: g.reshape(N_CHIPS, g.shape[0] // N_CHIPS, g.shape[1])
    token = on_late_grads({
        "swa_w_qkv": g_swa_qkv.reshape(N_CHIPS, D, -1), "swa_w_o": rows4(g_swa_o),
        "w_ff1": g_ff1.reshape(N_CHIPS, DEPTH * D, -1), "w_ff2": g_ff2.reshape(N_CHIPS, -1, D)})
    dy1a, do_a, ps_out_a, delta = _attn_out_bwd(dxm_a, y1a, o_a, wts["mla_w_o"], vecs[0] + token[0, 0], True)
    g_mla_o = _mm_tn(o_a, dy1a, "dw_o_mla")
    token = on_late_landed(g_mla_o)
    dq, dk, dv = _mla_attn_bwd(q, k, v, do_a, lse, delta + token[0, 0])
    dx0, ps_pre_a, dg_q, dg_kv, dwcat, dwuq, dwukv = _mla_pre_bwd(
        x, dxm_a, vecs[0], h1a, z, dq, dk, dv, cs, wcat, g_q, g_kv, wuq, wukv)

    c0, c1, c2 = Q_LORA, Q_LORA + KV_LORA, Q_LORA + KV_LORA + QK_ROPE
    g_dq = dwcat[:, :c0]
    g_dkv = jnp.concatenate([dwcat[:, c0:c1], _unrot_grad(dwcat[:, c1:c2], dwcat[:, c2:])], axis=1)
    e0 = QK_NOPE + QK_ROPE
    g_uq = jnp.concatenate([dwuq[..., :QK_NOPE], _unrot_grad(dwuq[..., QK_NOPE:e0], dwuq[..., e0:])], axis=-1)
    per = H // N_CHIPS
    g_uq = g_uq.reshape(N_CHIPS, per, Q_LORA, e0).transpose(0, 2, 1, 3).reshape(N_CHIPS, Q_LORA, per * e0)
    g_ukv = dwukv.reshape(N_CHIPS, per, KV_LORA, QK_NOPE + V_DIM).transpose(0, 2, 1, 3)
    g_ukv = g_ukv.reshape(N_CHIPS, KV_LORA, per * (QK_NOPE + V_DIM))

    def dmod(ps_pre, ps_out, ps_mlp):
        return jnp.concatenate([ps_pre[R_SH1:R_SC1 + 1], ps_out[R_GT1:R_GT1 + 1], ps_mlp[R_SH2:R_GT2 + 1]], axis=0)

    grads = {"mla_w_dq": rows4(g_dq), "mla_w_uq": g_uq, "mla_w_dkv": rows4(g_dkv), "mla_w_ukv": g_ukv,
             "mla_w_o": rows4(g_mla_o)}
    small = {
        "dmod": jnp.stack([dmod(ps_pre_a, ps_out_a, ps_mlp_a), dmod(ps_pre_b, ps_out_b, ps_mlp_b)]).reshape(DEPTH, 6 * D),
        "g_mix": jnp.stack([ps_pre_a[R_GMIX], ps_pre_b[R_GMIX]]),
        "g_mlp": jnp.stack([ps_mlp_a[R_GMLP], ps_mlp_b[R_GMLP]]),
        "mla_g_q": dg_q, "mla_g_kv": dg_kv, "swa_sinks": jnp.sum(dsinks.reshape(SWA_HEADS, WINDOW), axis=1).reshape(1, SWA_HEADS),
        "swa_b_qkv": g_swa_bqkv, "swa_b_o": ps_out_b[R_BO:R_BO + 1],
        "g_final": dg_final.reshape(D), "loss": loss8[0, 0],
    }
    return dx0, grads, small


SHARDED = {
    "mla_w_dq": (1, D // N_CHIPS, Q_LORA),
    "mla_w_uq": (1, Q_LORA, MLA_HEADS * (QK_NOPE + QK_ROPE) // N_CHIPS),
    "mla_w_dkv": (1, D // N_CHIPS, KV_LORA + QK_ROPE),
    "mla_w_ukv": (1, KV_LORA, MLA_HEADS * (QK_NOPE + V_DIM) // N_CHIPS),
    "mla_w_o": (1, MLA_HEADS * V_DIM // N_CHIPS, D),
    "swa_w_qkv": (1, D, (SWA_HEADS + 2 * SWA_KV_HEADS) * SWA_HEAD_DIM // N_CHIPS),
    "swa_w_o": (1, SWA_HEADS * SWA_HEAD_DIM // N_CHIPS, D),
    "w_ff1": (DEPTH, D, D_FF // N_CHIPS),
    "w_ff2": (DEPTH, D_FF // N_CHIPS, D),
}
COL_SPLIT = ("mla_w_uq", "mla_w_ukv", "swa_w_qkv")
BIASES = {"swa_b_qkv": (SWA_HEADS + 2 * SWA_KV_HEADS) * SWA_HEAD_DIM, "swa_b_o": D}


def _view2d(name):
    shape = SHARDED[name]
    return math.prod(shape[:-1]), shape[-1]


SMALL = {"b_ada": (DEPTH, 6 * D), "g_mix": (DEPTH, D), "g_mlp": (DEPTH, D), "mla_g_q": (1, Q_LORA),
         "mla_g_kv": (1, KV_LORA), "swa_sinks": (1, SWA_HEADS), "g_final": (D,), "loss": (),
         "swa_b_qkv": (1, BIASES["swa_b_qkv"]), "swa_b_o": (1, BIASES["swa_b_o"])}
SMALL_ROWS = 192
DMA_ROWS = 256


SLOT_ROWS = 8


def _small_slots():
    slots, off = {}, 0
    for name, shape in SMALL.items():
        n = max(math.prod(shape), 1)
        slots[name] = (off, n)
        off += -(-n // (SLOT_ROWS * LANES)) * SLOT_ROWS * LANES
    assert off <= SMALL_ROWS * LANES
    return slots


def _pack_small(vals):
    parts, end = [], 0
    for name, (off, n) in _small_slots().items():
        pad = -(-n // (SLOT_ROWS * LANES)) * SLOT_ROWS * LANES - n
        v = vals[name].astype(jnp.float32).reshape(-1) if name in vals else jnp.zeros((n,), jnp.float32)
        parts += [v, jnp.zeros((pad,), jnp.float32)]
        end = off + n + pad
    parts.append(jnp.zeros((SMALL_ROWS * LANES - end,), jnp.float32))
    return jnp.concatenate(parts).reshape(SMALL_ROWS, LANES)


def _from_slot(name, rows):
    n = max(math.prod(SMALL[name]), 1)
    return rows.reshape(-1)[:n].reshape(SMALL[name])


def _pieces(rows):
    return [(off, min(DMA_ROWS, rows - off)) for off in range(0, rows, DMA_ROWS)]


HBM = pl.BlockSpec(memory_space=pltpu.HBM)
MESH = pl.DeviceIdType.MESH


def _place():
    x, y, c = lax.axis_index("x"), lax.axis_index("y"), lax.axis_index("c")
    chips = [(1 - x, y), (x, 1 - y), (1 - x, 1 - y)]
    return x, y, c, chips


def _all_gather(block):
    m_per, n = block.shape

    def body(x_ref, out_ref, send_sems, recv_sems, local_sem):
        x, y, c, chips = _place()
        me, sibling = (x, y, c), (x, y, 1 - c)

        def rows(px, py, pc):
            return out_ref.at[pl.ds((4 * px + 2 * py + pc) * m_per, m_per), :]

        def copy(k, blk, to, src=None):
            return pltpu.make_async_remote_copy(
                src_ref=rows(*blk) if src is None else src, dst_ref=rows(*blk),
                send_sem=send_sems.at[k], recv_sem=recv_sems.at[k], device_id=to, device_id_type=MESH)

        mine = pltpu.make_async_copy(x_ref, rows(*me), local_sem)
        mine.start()
        first = [copy(0, me, sibling, src=x_ref)]
        first += [copy(1 + j, me, (*chip, c), src=x_ref) for j, chip in enumerate(chips)]
        for cp in first:
            cp.start()
        passed = [copy(4 + j, (*chip, c), sibling) for j, chip in enumerate(chips)]
        for j, chip in enumerate(chips):
            copy(1 + j, (*chip, c), me).wait_recv()
            passed[j].start()
        copy(0, sibling, me).wait_recv()
        for j, chip in enumerate(chips):
            copy(4 + j, (*chip, 1 - c), me).wait_recv()
        for cp in first + passed:
            cp.wait_send()
        mine.wait()

    out = pl.pallas_call(
        body, name="all_gather_small",
        out_shape=jax.ShapeDtypeStruct((N_DEV * m_per, n), block.dtype),
        in_specs=[pl.BlockSpec(memory_space=pltpu.VMEM)],
        out_specs=pl.BlockSpec(memory_space=pltpu.VMEM),
        scratch_shapes=[pltpu.SemaphoreType.DMA((7,)), pltpu.SemaphoreType.DMA((7,)), pltpu.SemaphoreType.DMA],
    )(block)
    return out.reshape(N_DEV, m_per, n)


def _weight_gather(shards):
    nt = len(shards)

    def body(*refs):
        w_refs, out_refs = refs[:nt], refs[nt:2 * nt]
        send_sems, recv_sems = refs[2 * nt:]
        x, y, c, chips = _place()
        sibling = (x, y, 1 - c)

        def slab(t, px, py, half):
            rh = shards[t].shape[0] // 2
            return out_refs[t].at[2 * px + py, pl.ds(half * rh, rh), :]

        def copy(t, k, src, dst, to):
            return pltpu.make_async_remote_copy(src_ref=src, dst_ref=dst, send_sem=send_sems.at[6 * t + k],
                                                recv_sem=recv_sems.at[6 * t + k], device_id=to, device_id_type=MESH)

        first = []
        for t in range(nt):
            rh = shards[t].shape[0] // 2
            first += [copy(t, j, w_refs[t].at[pl.ds(c * rh, rh), :], slab(t, x, y, c), (*chip, c))
                      for j, chip in enumerate(chips)]
        for cp in first:
            cp.start()
        passed = []
        for t in range(nt):
            for j, chip in enumerate(chips):
                copy(t, j, slab(t, *chip, c), slab(t, *chip, c), (*chip, c)).wait_recv()
                rh = shards[t].shape[0] // 2
                for off, n in _pieces(rh):
                    piece = out_refs[t].at[2 * chip[0] + chip[1], pl.ds(c * rh + off, n), :]
                    copy(t, 3 + j, piece, piece, sibling).start()
                passed.append(copy(t, 3 + j, slab(t, *chip, c), slab(t, *chip, c), sibling))
        for t in range(nt):
            for j, chip in enumerate(chips):
                copy(t, 3 + j, slab(t, *chip, 1 - c), slab(t, *chip, 1 - c), sibling).wait_recv()
        for cp in first + passed:
            cp.wait_send()

    return pl.pallas_call(
        body, name="weight_gather",
        out_shape=[jax.ShapeDtypeStruct((N_CHIPS,) + s.shape, s.dtype) for s in shards],
        in_specs=[HBM] * nt, out_specs=[HBM] * nt,
        scratch_shapes=[pltpu.SemaphoreType.DMA((6 * nt,)), pltpu.SemaphoreType.DMA((6 * nt,))],
    )(*shards)


SEM = pl.BlockSpec(memory_space=pltpu.SEMAPHORE)
ANY = pl.BlockSpec(memory_space=pl.ANY)
SPLIT_COPY = pltpu.SideEffectType.DATAFLOW_SIDE_EFFECTING


def _late_copies(w_refs, land_refs, send_sems, recv_sems):
    x, y, c, chips = _place()
    return [pltpu.make_async_remote_copy(
        src_ref=w_refs[t], dst_ref=land_refs[t].at[2 * x + y], send_sem=send_sems.at[3 * t + j],
        recv_sem=recv_sems.at[3 * t + j], device_id=(cx, cy, c), device_id_type=MESH)
        for t in range(len(w_refs)) for j, (cx, cy) in enumerate(chips)], chips


def _late_gather_start(shards, after):
    nt, na = len(shards), len(after)

    def body(*refs):
        w_refs, land_refs = refs[:nt], refs[nt:2 * nt]
        send_sems, recv_sems, token = refs[2 * nt + na], refs[2 * nt + na + 1], refs[-1]
        copies, _ = _late_copies(w_refs, land_refs, send_sems, recv_sems)
        for cp in copies:
            cp.start()
        token[...] = jnp.zeros_like(token)

    hbm = lambda a: pltpu.with_memory_space_constraint(a, pltpu.HBM)
    lands = [lax.empty((N_CHIPS,) + s.shape, s.dtype) for s in shards]
    outs = pl.pallas_call(
        body, name="late_gather_start",
        out_shape=(pltpu.SemaphoreType.DMA((3 * nt,)), pltpu.SemaphoreType.DMA((3 * nt,)),
                   *[pltpu.HBM(s.shape, s.dtype) for s in shards], *[pltpu.HBM(l.shape, l.dtype) for l in lands],
                   jax.ShapeDtypeStruct((8, LANES), jnp.float32)),
        in_specs=[HBM] * (2 * nt) + [ANY] * na,
        out_specs=(SEM, SEM, *([HBM] * (2 * nt)), pl.BlockSpec(memory_space=pltpu.VMEM)),
        input_output_aliases={i: 2 + i for i in range(2 * nt)},
        compiler_params=pltpu.CompilerParams(has_side_effects=SPLIT_COPY),
    )(*[hbm(s) for s in shards], *[hbm(l) for l in lands], *after)
    return outs[0], outs[1], list(outs[2:2 + nt]), list(outs[2 + nt:2 + 2 * nt]), outs[-1]


def _late_gather_wait(send_sems, recv_sems, shards, lands, after):
    nt = len(shards)

    def body(*refs):
        w_refs, land_refs = refs[:nt], refs[nt:2 * nt]
        s_sems, r_sems = refs[2 * nt], refs[2 * nt + 1]
        x, y, c, chips = _place()
        for t in range(nt):
            for j, (cx, cy) in enumerate(chips):
                cp = pltpu.make_async_remote_copy(
                    src_ref=w_refs[t], dst_ref=land_refs[t].at[2 * cx + cy], send_sem=s_sems.at[3 * t + j],
                    recv_sem=r_sems.at[3 * t + j], device_id=(cx, cy, c), device_id_type=MESH)
                cp.wait_send()
                cp.wait_recv()

    outs = pl.pallas_call(
        body, name="late_gather_wait",
        out_shape=(*[pltpu.HBM(s.shape, s.dtype) for s in shards], *[pltpu.HBM(l.shape, l.dtype) for l in lands]),
        in_specs=[HBM] * (2 * nt) + [SEM, SEM, ANY], out_specs=tuple([HBM] * (2 * nt)),
        input_output_aliases={i: i for i in range(2 * nt)},
        compiler_params=pltpu.CompilerParams(has_side_effects=SPLIT_COPY),
    )(*shards, *lands, send_sems, recv_sems, after)
    return list(outs[nt:])


def _grad_pair_in(grads, behind):
    nt = len(grads)

    def body(*refs):
        g_refs, got_refs = refs[:nt], refs[nt + 1:2 * nt + 1]
        send_sems, recv_sems = refs[2 * nt + 1:]
        x, y, c, _ = _place()
        sibling = (x, y, 1 - c)

        def copy(t, src, dst):
            return pltpu.make_async_remote_copy(src_ref=src, dst_ref=dst, send_sem=send_sems.at[t],
                                                recv_sem=recv_sems.at[t], device_id=sibling, device_id_type=MESH)

        for t in range(nt):
            rh = grads[t].shape[1] // 2
            for p in range(N_CHIPS):
                for off, n in _pieces(rh):
                    copy(t, g_refs[t].at[p, pl.ds((1 - c) * rh + off, n), :], got_refs[t].at[p, pl.ds(off, n), :]).start()
        for t in range(nt):
            rh = grads[t].shape[1] // 2
            copy(t, g_refs[t].at[:, pl.ds((1 - c) * rh, rh), :], got_refs[t]).wait()

    return pl.pallas_call(
        body, name="grad_pair_in",
        out_shape=[jax.ShapeDtypeStruct((N_CHIPS, g.shape[1] // 2, g.shape[2]), g.dtype) for g in grads],
        in_specs=[HBM] * nt + [ANY], out_specs=[HBM] * nt,
        scratch_shapes=[pltpu.SemaphoreType.DMA((nt,)), pltpu.SemaphoreType.DMA((nt,))],
    )(*grads, behind)


def _pair_in_start(grads):
    nt = len(grads)

    def body(*refs):
        g_refs, land_refs = refs[:nt], refs[nt:2 * nt]
        send_sems, recv_sems, token = refs[2 * nt], refs[2 * nt + 1], refs[-1]
        x, y, c, _ = _place()
        for t in range(nt):
            rh = grads[t].shape[1] // 2
            for p in range(N_CHIPS):
                for off, n in _pieces(rh):
                    pltpu.make_async_remote_copy(
                        src_ref=g_refs[t].at[p, pl.ds((1 - c) * rh + off, n), :], dst_ref=land_refs[t].at[p, pl.ds(off, n), :],
                        send_sem=send_sems.at[t], recv_sem=recv_sems.at[t], device_id=(x, y, 1 - c),
                        device_id_type=MESH).start()
        token[...] = jnp.zeros_like(token)

    hbm = lambda a: pltpu.with_memory_space_constraint(a, pltpu.HBM)
    lands = [lax.empty((N_CHIPS, g.shape[1] // 2, g.shape[2]), g.dtype) for g in grads]
    outs = pl.pallas_call(
        body, name="grad_pair_in_start",
        out_shape=(pltpu.SemaphoreType.DMA((nt,)), pltpu.SemaphoreType.DMA((nt,)),
                   *[pltpu.HBM(g.shape, g.dtype) for g in grads], *[pltpu.HBM(l.shape, l.dtype) for l in lands],
                   jax.ShapeDtypeStruct((8, LANES), jnp.float32)),
        in_specs=[HBM] * (2 * nt),
        out_specs=(SEM, SEM, *([HBM] * (2 * nt)), pl.BlockSpec(memory_space=pltpu.VMEM)),
        input_output_aliases={i: 2 + i for i in range(2 * nt)},
        compiler_params=pltpu.CompilerParams(has_side_effects=SPLIT_COPY),
    )(*[hbm(g) for g in grads], *[hbm(l) for l in lands])
    return outs[0], outs[1], list(outs[2:2 + nt]), list(outs[2 + nt:2 + 2 * nt]), outs[-1]


def _pair_in_wait(send_sems, recv_sems, grads, lands, after):
    nt = len(grads)

    def body(*refs):
        g_refs, land_refs = refs[:nt], refs[nt:2 * nt]
        s_sems, r_sems = refs[2 * nt], refs[2 * nt + 1]
        x, y, c, _ = _place()
        for t in range(nt):
            rh = grads[t].shape[1] // 2
            cp = pltpu.make_async_remote_copy(
                src_ref=g_refs[t].at[:, pl.ds((1 - c) * rh, rh), :], dst_ref=land_refs[t], send_sem=s_sems.at[t],
                recv_sem=r_sems.at[t], device_id=(x, y, 1 - c), device_id_type=MESH)
            cp.wait_send()
            cp.wait_recv()

    outs = pl.pallas_call(
        body, name="grad_pair_in_wait",
        out_shape=(*[pltpu.HBM(g.shape, g.dtype) for g in grads], *[pltpu.HBM(l.shape, l.dtype) for l in lands]),
        in_specs=[HBM] * (2 * nt) + [SEM, SEM, ANY], out_specs=tuple([HBM] * (2 * nt)),
        input_output_aliases={i: i for i in range(2 * nt)},
        compiler_params=pltpu.CompilerParams(has_side_effects=SPLIT_COPY),
    )(*grads, *lands, send_sems, recv_sems, after)
    return list(outs[:nt]), list(outs[nt:])


def _pair_sum(g, got, core, name):
    _, rows, cols = g.shape
    rh = rows // 2
    tr = _tile(rh, 512)
    nb = rh // tr

    def body(c_ref, g_ref, got_ref, s32_ref, s16_ref):
        s = g_ref[...] + got_ref[...]
        s32_ref[...] = s
        s16_ref[...] = s.astype(s16_ref.dtype)

    blk = pl.BlockSpec((None, tr, cols), lambda p, i, c_ref: (p, i, 0))
    return pl.pallas_call(
        body, name=name,
        grid_spec=pltpu.PrefetchScalarGridSpec(
            num_scalar_prefetch=1, grid=(N_CHIPS, nb),
            in_specs=[pl.BlockSpec((None, tr, cols), lambda p, i, c_ref: (p, c_ref[0] * nb + i, 0)), blk],
            out_specs=[blk, blk]),
        out_shape=[jax.ShapeDtypeStruct((N_CHIPS, rh, cols), jnp.float32),
                   jax.ShapeDtypeStruct((N_CHIPS, rh, cols), jnp.bfloat16)],
        compiler_params=_params("parallel", "parallel"),
    )(core, g, got)


def _exchange_start(parts, name):
    nt = len(parts)

    def body(*refs):
        a_refs, land_refs = refs[:nt], refs[nt:2 * nt]
        send_sems, recv_sems, token = refs[2 * nt], refs[2 * nt + 1], refs[-1]
        x, y, c, chips = _place()
        for t in range(nt):
            for j, (cx, cy) in enumerate(chips):
                pltpu.make_async_remote_copy(
                    src_ref=a_refs[t].at[2 * cx + cy], dst_ref=land_refs[t].at[j], send_sem=send_sems.at[3 * t + j],
                    recv_sem=recv_sems.at[3 * t + j], device_id=(cx, cy, c), device_id_type=MESH).start()
        token[...] = jnp.zeros_like(token)

    hbm = lambda a: pltpu.with_memory_space_constraint(a, pltpu.HBM)
    lands = [lax.empty((N_CHIPS - 1,) + a.shape[1:], a.dtype) for a in parts]
    outs = pl.pallas_call(
        body, name=name,
        out_shape=(pltpu.SemaphoreType.DMA((3 * nt,)), pltpu.SemaphoreType.DMA((3 * nt,)),
                   *[pltpu.HBM(a.shape, a.dtype) for a in parts], *[pltpu.HBM(l.shape, l.dtype) for l in lands],
                   jax.ShapeDtypeStruct((8, LANES), jnp.float32)),
        in_specs=[HBM] * (2 * nt),
        out_specs=(SEM, SEM, *([HBM] * (2 * nt)), pl.BlockSpec(memory_space=pltpu.VMEM)),
        input_output_aliases={i: 2 + i for i in range(2 * nt)},
        compiler_params=pltpu.CompilerParams(has_side_effects=SPLIT_COPY),
    )(*[hbm(a) for a in parts], *[hbm(l) for l in lands])
    return outs[0], outs[1], list(outs[2:2 + nt]), list(outs[2 + nt:2 + 2 * nt]), outs[-1]


def _exchange_wait(send_sems, recv_sems, parts, lands, after, name):
    nt = len(parts)

    def body(*refs):
        a_refs, land_refs = refs[:nt], refs[nt:2 * nt]
        s_sems, r_sems = refs[2 * nt], refs[2 * nt + 1]
        x, y, c, chips = _place()
        for t in range(nt):
            for j, (cx, cy) in enumerate(chips):
                cp = pltpu.make_async_remote_copy(
                    src_ref=a_refs[t].at[2 * cx + cy], dst_ref=land_refs[t].at[j], send_sem=s_sems.at[3 * t + j],
                    recv_sem=r_sems.at[3 * t + j], device_id=(cx, cy, c), device_id_type=MESH)
                cp.wait_send()
                cp.wait_recv()

    outs = pl.pallas_call(
        body, name=name,
        out_shape=(*[pltpu.HBM(a.shape, a.dtype) for a in parts], *[pltpu.HBM(l.shape, l.dtype) for l in lands]),
        in_specs=[HBM] * (2 * nt) + [SEM, SEM, ANY], out_specs=tuple([HBM] * (2 * nt)),
        input_output_aliases={i: i for i in range(2 * nt)},
        compiler_params=pltpu.CompilerParams(has_side_effects=SPLIT_COPY),
    )(*parts, *lands, send_sems, recv_sems, after)
    return list(outs[nt:])


def _chip_sum(s32, got, chip, name, behind=None):
    _, rh, cols = s32.shape
    tr = _tile(rh, STREAM_ROWS)

    def body(p_ref, s_ref, got_ref, *refs):
        acc = s_ref[...]
        for j in range(N_CHIPS - 1):
            acc = acc + got_ref[j].astype(jnp.float32)
        refs[-1][...] = acc

    extra = [] if behind is None else [behind]
    return pl.pallas_call(
        body, name=name,
        grid_spec=pltpu.PrefetchScalarGridSpec(
            num_scalar_prefetch=1, grid=(rh // tr,),
            in_specs=[pl.BlockSpec((None, tr, cols), lambda i, p_ref: (p_ref[0], i, 0)),
                      pl.BlockSpec((N_CHIPS - 1, tr, cols), lambda i, p_ref: (0, i, 0))]
            + [pl.BlockSpec((8, LANES), lambda i, p_ref: (0, 0))] * len(extra),
            out_specs=pl.BlockSpec((tr, cols), lambda i, p_ref: (i, 0))),
        out_shape=jax.ShapeDtypeStruct((rh, cols), jnp.float32),
        compiler_params=_params("parallel"),
    )(chip, s32, got, *extra)


def _pair_out_start(halves):
    nt = len(halves)

    def body(*refs):
        h_refs, land_refs = refs[:nt], refs[nt:2 * nt]
        send_sems, recv_sems, token = refs[2 * nt], refs[2 * nt + 1], refs[-1]
        x, y, c, _ = _place()
        for t in range(nt):
            for off, n in _pieces(halves[t].shape[0]):
                pltpu.make_async_remote_copy(
                    src_ref=h_refs[t].at[pl.ds(off, n), :], dst_ref=land_refs[t].at[pl.ds(off, n), :],
                    send_sem=send_sems.at[t], recv_sem=recv_sems.at[t], device_id=(x, y, 1 - c),
                    device_id_type=MESH).start()
        token[...] = jnp.zeros_like(token)

    hbm = lambda a: pltpu.with_memory_space_constraint(a, pltpu.HBM)
    lands = [lax.empty(h.shape, h.dtype) for h in halves]
    outs = pl.pallas_call(
        body, name="grad_pair_out_start",
        out_shape=(pltpu.SemaphoreType.DMA((nt,)), pltpu.SemaphoreType.DMA((nt,)),
                   *[pltpu.HBM(h.shape, h.dtype) for h in halves], *[pltpu.HBM(l.shape, l.dtype) for l in lands],
                   jax.ShapeDtypeStruct((8, LANES), jnp.float32)),
        in_specs=[HBM] * (2 * nt),
        out_specs=(SEM, SEM, *([HBM] * (2 * nt)), pl.BlockSpec(memory_space=pltpu.VMEM)),
        input_output_aliases={i: 2 + i for i in range(2 * nt)},
        compiler_params=pltpu.CompilerParams(has_side_effects=SPLIT_COPY),
    )(*[hbm(h) for h in halves], *[hbm(l) for l in lands])
    return outs[0], outs[1], list(outs[2:2 + nt]), list(outs[2 + nt:2 + 2 * nt]), outs[-1]


def _pair_out_wait(send_sems, recv_sems, halves, lands, after):
    nt = len(halves)

    def body(*refs):
        h_refs, land_refs = refs[:nt], refs[nt:2 * nt]
        s_sems, r_sems = refs[2 * nt], refs[2 * nt + 1]
        x, y, c, _ = _place()
        for t in range(nt):
            cp = pltpu.make_async_remote_copy(
                src_ref=h_refs[t], dst_ref=land_refs[t], send_sem=s_sems.at[t], recv_sem=r_sems.at[t],
                device_id=(x, y, 1 - c), device_id_type=MESH)
            cp.wait_send()
            cp.wait_recv()

    outs = pl.pallas_call(
        body, name="grad_pair_out_wait",
        out_shape=(*[pltpu.HBM(h.shape, h.dtype) for h in halves], *[pltpu.HBM(l.shape, l.dtype) for l in lands]),
        in_specs=[HBM] * (2 * nt) + [SEM, SEM, ANY], out_specs=tuple([HBM] * (2 * nt)),
        input_output_aliases={i: i for i in range(2 * nt)},
        compiler_params=pltpu.CompilerParams(has_side_effects=SPLIT_COPY),
    )(*halves, *lands, send_sems, recv_sems, after)
    return list(outs[:nt]), list(outs[nt:])


def _grad_pair_out(halves):
    nt = len(halves)

    def body(*refs):
        h_refs, got_refs = refs[:nt], refs[nt:2 * nt]
        send_sems, recv_sems = refs[2 * nt:]
        x, y, c, _ = _place()
        sibling = (x, y, 1 - c)

        def copy(t, src, dst):
            return pltpu.make_async_remote_copy(src_ref=src, dst_ref=dst, send_sem=send_sems.at[t],
                                                recv_sem=recv_sems.at[t], device_id=sibling, device_id_type=MESH)

        for t in range(nt):
            for off, n in _pieces(halves[t].shape[0]):
                copy(t, h_refs[t].at[pl.ds(off, n), :], got_refs[t].at[pl.ds(off, n), :]).start()
        for t in range(nt):
            copy(t, h_refs[t], got_refs[t]).wait()

    return pl.pallas_call(
        body, name="grad_pair_out",
        out_shape=[jax.ShapeDtypeStruct(h.shape, h.dtype) for h in halves],
        in_specs=[HBM] * nt, out_specs=[HBM] * nt,
        scratch_shapes=[pltpu.SemaphoreType.DMA((nt,)), pltpu.SemaphoreType.DMA((nt,))],
    )(*halves)


def _ada_part(c_all, w_ada):
    L, _, ncol = w_ada.shape
    tn = _tile(ncol, 512)

    def body(c_ref, w_ref, cond_ref, part_ref):
        cv = c_ref[...]
        cond = cv * jax.nn.sigmoid(cv)
        cond_ref[...] = cond
        part_ref[0] = jnp.dot(cond, w_ref[0], precision=lax.Precision.HIGHEST, preferred_element_type=jnp.float32)

    return pl.pallas_call(
        body, name="ada_part", grid=(L, ncol // tn),
        in_specs=[_full((N_DEV, D)), pl.BlockSpec((1, D, tn), lambda l, j: (l, 0, j))],
        out_specs=[_full((N_DEV, D)), pl.BlockSpec((1, N_DEV, tn), lambda l, j: (l, 0, j))],
        out_shape=[jax.ShapeDtypeStruct((N_DEV, D), jnp.float32), jax.ShapeDtypeStruct((L, N_DEV, ncol), jnp.float32)],
        compiler_params=_params("arbitrary", "arbitrary"),
    )(c_all, w_ada)


def _adamw_math(w, g, m, v):
    m = ADAM_B1 * m + (1.0 - ADAM_B1) * g
    v = ADAM_B2 * v + (1.0 - ADAM_B2) * jnp.square(g)
    m_hat = m / (1.0 - ADAM_B1 ** ADAM_STEP)
    v_hat = v / (1.0 - ADAM_B2 ** ADAM_STEP)
    delta = -ADAM_LR * (m_hat / (jnp.sqrt(v_hat) + ADAM_EPS) + ADAM_WD * w)
    return delta, m, v


def _adamw(w, g, m, v, name):
    shape = w.shape
    cols = shape[-1]
    rows = math.prod(shape[:-1])
    tr = _tile(rows, 512)
    two_d = lambda t: t.reshape(rows, cols)

    def body(w_ref, g_ref, m_ref, v_ref, d_ref, mo_ref, vo_ref):
        d_ref[...], mo_ref[...], vo_ref[...] = _adamw_math(w_ref[...], g_ref[...], m_ref[...], v_ref[...])

    out = jax.ShapeDtypeStruct((rows, cols), jnp.float32)
    outs = pl.pallas_call(
        body, name=name, grid=(rows // tr,), in_specs=[_rows(tr, cols)] * 4, out_specs=[_rows(tr, cols)] * 3,
        out_shape=[out, out, out], compiler_params=_params("parallel"),
    )(two_d(w), two_d(g), two_d(m), two_d(v))
    return [t.reshape(shape) for t in outs]


def _adamw_halves(w, mine, got, m, v, core, name):
    shape = w.shape
    cols = shape[-1]
    rows = math.prod(shape[:-1])
    rh = rows // 2
    tr = _tile(rh, STREAM_ROWS)
    nbh = rh // tr
    two_d = lambda t: t.reshape(rows, cols)

    def body(c_ref, w_ref, a_ref, b_ref, m_ref, v_ref, g_ref, d_ref, mo_ref, vo_ref):
        g = jnp.where(pl.program_id(0) // nbh == c_ref[0], a_ref[...], b_ref[...])
        g_ref[...] = g
        d_ref[...], mo_ref[...], vo_ref[...] = _adamw_math(w_ref[...], g, m_ref[...], v_ref[...])

    row = pl.BlockSpec((tr, cols), lambda i, c_ref: (i, 0))

    def half(keep):
        return pl.BlockSpec((tr, cols), lambda i, c_ref: (jnp.where((i // nbh == c_ref[0]) == keep, i % nbh, 0), 0))

    out = jax.ShapeDtypeStruct((rows, cols), jnp.float32)
    outs = pl.pallas_call(
        body, name=name,
        grid_spec=pltpu.PrefetchScalarGridSpec(
            num_scalar_prefetch=1, grid=(rows // tr,),
            in_specs=[row, half(True), half(False), row, row], out_specs=[row] * 4),
        out_shape=[out] * 4, compiler_params=_params("arbitrary"),
    )(core, two_d(w), mine, got, two_d(m), two_d(v))
    return [t.reshape(shape) for t in outs]


def _ada_grad_adamw(cond_t, dm, w, m, v):
    L, _, ncol = w.shape
    tn = _tile(ncol, STREAM_ROWS)

    def body(ct_ref, dm_ref, w_ref, m_ref, v_ref, g_ref, d_ref, mo_ref, vo_ref):
        g = ct_ref[:, 0:1] * dm_ref[0, 0:1, :]
        for b in range(1, N_DEV):
            g = g + ct_ref[:, b:b + 1] * dm_ref[0, b:b + 1, :]
        g_ref[0] = g
        d_ref[0], mo_ref[0], vo_ref[0] = _adamw_math(w_ref[0], g, m_ref[0], v_ref[0])

    wblk = pl.BlockSpec((1, D, tn), lambda l, j: (l, 0, j))
    out = jax.ShapeDtypeStruct(w.shape, jnp.float32)
    return pl.pallas_call(
        body, name="ada_grad_adamw", grid=(L, ncol // tn),
        in_specs=[_full((D, N_DEV)), pl.BlockSpec((1, N_DEV, tn), lambda l, j: (l, 0, j)), wblk, wblk, wblk],
        out_specs=[wblk] * 4, out_shape=[out] * 4, compiler_params=_params("parallel", "parallel"),
    )(cond_t, dm, w, m, v)


def _small_adamw(gathered, w, m, v):
    slots = _small_slots()
    rows = {name: (off // LANES, -(-n // LANES)) for name, (off, n) in slots.items()}
    kinds = {name: 1 if name == "loss" or name in BIASES else 4 for name in slots}

    def body(ga_ref, w_ref, m_ref, v_ref, *out_refs):
        g = ga_ref[0]
        for dev in range(1, N_DEV):
            g = g + ga_ref[dev]
        d, mo, vo = _adamw_math(w_ref[...], g, m_ref[...], v_ref[...])
        k = 0
        for name, (r0, nr) in rows.items():
            for src in (g, d, mo, vo)[:kinds[name]]:
                out_refs[k][...] = src[r0:r0 + nr, :]
                k += 1

    out_shape = [jax.ShapeDtypeStruct((rows[name][1], LANES), jnp.float32) for name in slots for _ in range(kinds[name])]
    flat = pl.pallas_call(
        body, name="small_adamw", out_shape=out_shape,
        in_specs=[pl.BlockSpec(memory_space=pltpu.VMEM)] * 4,
        out_specs=[pl.BlockSpec(memory_space=pltpu.VMEM)] * len(out_shape),
    )(gathered, w, m, v)
    out, k = {}, 0
    for name in slots:
        out[name] = [_from_slot(name, t) for t in flat[k:k + kinds[name]]]
        k += kinds[name]
    return out


def _one_hot_pick(arr, index, axis):
    n = arr.shape[axis]
    shape = [1] * arr.ndim
    shape[axis] = n
    hot = (jnp.arange(n) == index).astype(arr.dtype).reshape(shape)
    return jnp.sum(arr * hot, axis=axis)


def kernel(x, c, positions, w_ada, b_ada, g_mix, g_mlp, mla_w_dq, mla_g_q, mla_w_uq, mla_w_dkv, mla_g_kv, mla_w_ukv, mla_w_o, swa_w_qkv, swa_b_qkv, swa_sinks, swa_w_o, swa_b_o, w_ff1, w_ff2, g_final, loss_target, m_w_ada, m_b_ada, m_g_mix, m_g_mlp, m_mla_w_dq, m_mla_g_q, m_mla_w_uq, m_mla_w_dkv, m_mla_g_kv, m_mla_w_ukv, m_mla_w_o, m_swa_w_qkv, m_swa_b_qkv, m_swa_sinks, m_swa_w_o, m_swa_b_o, m_w_ff1, m_w_ff2, m_g_final, v_w_ada, v_b_ada, v_g_mix, v_g_mlp, v_mla_w_dq, v_mla_g_q, v_mla_w_uq, v_mla_w_dkv, v_mla_g_kv, v_mla_w_ukv, v_mla_w_o, v_swa_w_qkv, v_swa_b_qkv, v_swa_sinks, v_swa_w_o, v_swa_b_o, v_w_ff1, v_w_ff2, v_g_final):
    W = dict(w_ada=w_ada, b_ada=b_ada, g_mix=g_mix, g_mlp=g_mlp, mla_w_dq=mla_w_dq, mla_g_q=mla_g_q, mla_w_uq=mla_w_uq,
             mla_w_dkv=mla_w_dkv, mla_g_kv=mla_g_kv, mla_w_ukv=mla_w_ukv, mla_w_o=mla_w_o, swa_w_qkv=swa_w_qkv,
             swa_b_qkv=swa_b_qkv, swa_sinks=swa_sinks, swa_w_o=swa_w_o, swa_b_o=swa_b_o, w_ff1=w_ff1, w_ff2=w_ff2,
             g_final=g_final)
    M = dict(w_ada=m_w_ada, b_ada=m_b_ada, g_mix=m_g_mix, g_mlp=m_g_mlp, mla_w_dq=m_mla_w_dq, mla_g_q=m_mla_g_q,
             mla_w_uq=m_mla_w_uq, mla_w_dkv=m_mla_w_dkv, mla_g_kv=m_mla_g_kv, mla_w_ukv=m_mla_w_ukv, mla_w_o=m_mla_w_o,
             swa_w_qkv=m_swa_w_qkv, swa_b_qkv=m_swa_b_qkv, swa_sinks=m_swa_sinks, swa_w_o=m_swa_w_o, swa_b_o=m_swa_b_o,
             w_ff1=m_w_ff1, w_ff2=m_w_ff2, g_final=m_g_final)
    V = dict(w_ada=v_w_ada, b_ada=v_b_ada, g_mix=v_g_mix, g_mlp=v_g_mlp, mla_w_dq=v_mla_w_dq, mla_g_q=v_mla_g_q,
             mla_w_uq=v_mla_w_uq, mla_w_dkv=v_mla_w_dkv, mla_g_kv=v_mla_g_kv, mla_w_ukv=v_mla_w_ukv, mla_w_o=v_mla_w_o,
             swa_w_qkv=v_swa_w_qkv, swa_b_qkv=v_swa_b_qkv, swa_sinks=v_swa_sinks, swa_w_o=v_swa_w_o, swa_b_o=v_swa_b_o,
             w_ff1=v_w_ff1, w_ff2=v_w_ff2, g_final=v_g_final)
    order = list(W)
    names = list(SHARDED)
    core = lax.axis_index("c")
    chip = 2 * lax.axis_index("x") + lax.axis_index("y")
    dev = 2 * chip + core
    core_arr = core.astype(jnp.int32).reshape(1)
    chip_arr = chip.astype(jnp.int32).reshape(1)

    def whole(n, g, own):
        g = lax.dynamic_update_slice(g, own[None], (chip, 0, 0))
        if n in ("w_ff1", "w_ff2"):
            return g
        if n in COL_SPLIT:
            return g.transpose(1, 0, 2).reshape(g.shape[1], N_CHIPS * g.shape[2])
        return g.reshape(N_CHIPS * g.shape[1], g.shape[2])

    early = [n for n in names if n.startswith("mla_")]
    local = {n: W[n].astype(MXU_DTYPE).reshape(_view2d(n)) for n in early}
    wts = {n: whole(n, g, local[n]) for n, g in zip(early, _weight_gather([local[n] for n in early]))}

    nbq, nbo = BIASES["swa_b_qkv"] // N_CHIPS, BIASES["swa_b_o"] // N_CHIPS
    first = jnp.concatenate([c.reshape(-1), swa_b_qkv.reshape(-1), swa_b_o.reshape(-1),
                             jnp.zeros((FIRST_ROWS * LANES - D - nbq - nbo,), jnp.float32)]).reshape(FIRST_ROWS, LANES)
    first_all = _all_gather(first).reshape(N_DEV, FIRST_ROWS * LANES)
    c_all = first_all[:, :D]
    south = first_all[0::2]
    wts["swa_b_qkv"] = south[:, D:D + nbq].reshape(1, N_CHIPS * nbq)
    wts["swa_b_o"] = south[:, D + nbq:D + nbq + nbo].reshape(1, N_CHIPS * nbo)
    cond_all, part = _ada_part(c_all, w_ada)
    ncol = w_ada.shape[2]
    part_all = _all_gather(part.reshape(-1, LANES)).reshape(N_DEV, DEPTH, N_DEV, ncol)
    mine = _one_hot_pick(part_all[0::2], dev, axis=2)
    mod = mine.transpose(1, 0, 2).reshape(DEPTH, N_CHIPS * ncol) + b_ada
    vecs = jnp.concatenate([mod.reshape(DEPTH, 6, D), g_mix[:, None, :], g_mlp[:, None, :]], axis=1)

    late = [("w_ff1", 0), ("w_ff2", 0), ("swa_w_qkv", None), ("swa_w_o", None), ("w_ff1", 1), ("w_ff2", 1)]
    late_local = [(W[n][0] if l is None else W[n][l]).astype(MXU_DTYPE) for n, l in late]
    send_sems, recv_sems, passed, lands, token = _late_gather_start(late_local, [vecs] + [wts[n] for n in early])

    def late_weights(after):
        got = _late_gather_wait(send_sems, recv_sems, passed, lands, after)
        out = {"w_ff1": [None] * DEPTH, "w_ff2": [None] * DEPTH}
        for (n, l), g, own in zip(late, got, late_local):
            if l is None:
                out[n] = whole(n, g, own)
            else:
                out[n][l] = whole(n, g, own)
        return out

    late_names = [n for n in names if n not in early]
    reduce_state = {}

    def on_late_grads(late_grads):
        s_sems, r_sems, passed_g, zones, tok = _pair_in_start([late_grads[n] for n in late_names])
        reduce_state.update(pair=(s_sems, r_sems, passed_g, zones))
        return tok

    def on_late_landed(after):
        gl, got = _pair_in_wait(*reduce_state["pair"], after)
        sums = [_pair_sum(g, s, core_arr, "pair_sum_" + n) for n, g, s in zip(late_names, gl, got)]
        s_sems, r_sems, parts, zones, tok = _exchange_start([s16 for _, s16 in sums], "grad_exchange_start")
        reduce_state.update(sums=sums, split=(s_sems, r_sems, parts, zones))
        return tok

    grad_x, grads, small = _sequence_step(
        x[0], loss_target[0], positions[0], vecs, mla_g_q + token[0, 0], mla_g_kv, swa_sinks, g_final, wts,
        late_weights, on_late_grads, on_late_landed)

    small["b_ada"] = small.pop("dmod")
    small_all = _all_gather(_pack_small(small))
    pk = lambda src: _pack_small({n: src[n] for n in SMALL if n != "loss" and n not in BIASES})
    off, n = _small_slots()["b_ada"]
    dmod_all = small_all.reshape(N_DEV, -1)[:, off:off + n].reshape(N_DEV, DEPTH, N_CHIPS, ncol)
    dm = _one_hot_pick(dmod_all, chip, axis=2).transpose(1, 0, 2)

    def chip_sums(tensor_names, sums, others):
        return [_chip_sum(s32, o, chip_arr, "chip_sum_" + n) for n, (s32, _), o in zip(tensor_names, sums, others)]

    def adamw(tensor_names, mine, sibling):
        return {n: _adamw_halves(W[n], a, b, M[n], V[n], core_arr, "adamw_" + n)
                for n, a, b in zip(tensor_names, mine, sibling)}

    late_others = _exchange_wait(*reduce_state["split"], grad_x, "grad_exchange_wait")
    p_sems, p_rems, p_halves, p_zones, p_tok = _pair_out_start(chip_sums(late_names, reduce_state["sums"], late_others))
    gl = [grads[n] for n in early]
    got = _grad_pair_in(gl, p_tok)
    sums = [_pair_sum(g, s, core_arr, "pair_sum_" + n) for n, g, s in zip(early, gl, got)]
    e_sems, e_rems, e_parts, e_zones, e_tok = _exchange_start([s16 for _, s16 in sums], "mla_exchange_start")
    res = adamw(late_names, *_pair_out_wait(p_sems, p_rems, p_halves, p_zones, e_tok))
    res["w_ada"] = _ada_grad_adamw(cond_all.T, dm, w_ada, m_w_ada, v_w_ada)
    small_res = _small_adamw(small_all, pk(W), pk(M), pk(V))
    early_others = _exchange_wait(e_sems, e_rems, e_parts, e_zones, res["w_ff2"][1], "mla_exchange_wait")
    early_halves = chip_sums(early, sums, early_others)
    res.update(adamw(early, early_halves, _grad_pair_out(early_halves)))

    for n, width in BIASES.items():
        g = _one_hot_pick(small_res[n][0].reshape(N_CHIPS, width // N_CHIPS), chip, axis=0).reshape(1, -1)
        res[n] = [g] + _adamw(W[n], g, M[n], V[n], "adamw_" + n)
    for name in order:
        if name not in res:
            res[name] = small_res[name]
    outs = [small_res["loss"][0], grad_x[None]]
    for k in range(4):
        outs += [res[name][k] for name in order]
    return tuple(outs)
```

```python
import math

import jax
import jax.numpy as jnp
import numpy as np
from jax import lax
from jax.experimental import pallas as pl
from jax.experimental.pallas import tpu as pltpu

D = 1024
DEPTH = 2
MLA_HEADS = 8
QK_NOPE = 128
QK_ROPE = 64
V_DIM = 128
Q_LORA = 384
KV_LORA = 256
ROPE_THETA = 10000.0
SWA_HEADS = 16
SWA_KV_HEADS = 4
SWA_HEAD_DIM = 64
SWA_GROUP = SWA_HEADS // SWA_KV_HEADS
WINDOW = 128
D_FF = 4 * D
EPS = 1e-6
ADAM_LR = 0.001
ADAM_B1 = 0.9
ADAM_B2 = 0.999
ADAM_EPS = 1e-08
ADAM_WD = 0.01
ADAM_STEP = 10

N_CHIPS = 4
N_DEV = 8
LANES = 128
QK_EXT = 256
MLA_SCALE = (QK_NOPE + QK_ROPE) ** -0.5
LOG2E = math.log2(math.e)
LN2 = math.log(2.0)
MLA_QSCALE = MLA_SCALE * LOG2E
ATTN_BLOCK = 2048
ATTN_SUB = 512
MLP_FWD_TILE = (1024, 1024)
MLP_BWD_TILE = (512, 1024)
DW_TOKENS = 4096
DW_TILE = 1024
ROW_TILE = 1024
PROJ_ROWS = 512
FIRST_ROWS = 16
STREAM_ROWS = 512
SWA_SCALE = SWA_HEAD_DIM ** -0.5
NEG = -1e30
MXU_DTYPE = jnp.bfloat16
VMEM_LIMIT = 56 * 1024 * 1024

R_SH1, R_SC1, R_GT1, R_SH2, R_SC2, R_GT2, R_GMIX, R_GMLP = range(8)
R_BO = 6


def _tile(n, pref):
    if n <= pref:
        return n
    for t in range(pref, 7, -1):
        if n % t == 0 and t % 8 == 0:
            return t
    return n


def _dot(a, b):
    return jnp.dot(a, b, preferred_element_type=jnp.float32)


def _dot_nt(a, b):
    return lax.dot_general(a, b, (((1,), (1,)), ((), ())), preferred_element_type=jnp.float32)


def _dot_tn(a, b):
    return lax.dot_general(a, b, (((0,), (0,)), ((), ())), preferred_element_type=jnp.float32)


def _rms(x):
    r = lax.rsqrt(jnp.mean(x * x, axis=-1, keepdims=True) + EPS)
    return x * r, r


def _rms_bwd(dxhat, xhat, r):
    return r * (dxhat - xhat * jnp.mean(dxhat * xhat, axis=-1, keepdims=True))


def _rowsum(v):
    return jnp.sum(v, axis=0, keepdims=True)


def _params(*sem):
    return pltpu.CompilerParams(dimension_semantics=sem, vmem_limit_bytes=VMEM_LIMIT)


def _full(shape):
    nd = len(shape)
    return pl.BlockSpec(shape, lambda *_: (0,) * nd)


def _rows(tm, cols):
    return pl.BlockSpec((tm, cols), lambda i, *_: (i, 0))


def _modulate_bwd(dh, x, vec_ref, r_g, r_sc, r_sh, ps_ref, dres):
    xhat, r = _rms(x)
    g = vec_ref[r_g:r_g + 1, :]
    n = xhat * g
    ps_ref[r_sh:r_sh + 1, :] += _rowsum(dh)
    ps_ref[r_sc:r_sc + 1, :] += _rowsum(dh * n)
    dn = dh * (1.0 + vec_ref[r_sc:r_sc + 1, :])
    ps_ref[r_g:r_g + 1, :] += _rowsum(dn * xhat)
    return dres + _rms_bwd(dn * g, xhat, r)


def _mla_pre(x, vec, wcat, g_q, g_kv, wuq, wukv, cs):
    T = x.shape[0]
    tm = _tile(T, PROJ_ROWS)
    H = MLA_HEADS

    def body(x_ref, vec_ref, wcat_ref, gq_ref, gkv_ref, wuq_ref, wukv_ref, cs_ref, h_ref, z_ref, q_ref, k_ref, v_ref):
        xhat, _ = _rms(x_ref[...])
        h = xhat * vec_ref[R_GMIX:R_GMIX + 1, :] * (1.0 + vec_ref[R_SC1:R_SC1 + 1, :]) + vec_ref[R_SH1:R_SH1 + 1, :]
        hb = h.astype(MXU_DTYPE)
        h_ref[...] = hb
        z = _dot(hb, wcat_ref[...])
        z_ref[...] = z
        cq = (_rms(z[:, :Q_LORA])[0] * gq_ref[...]).astype(MXU_DTYPE)
        ckv = (_rms(z[:, Q_LORA:Q_LORA + KV_LORA])[0] * gkv_ref[...]).astype(MXU_DTYPE)
        cs_t = cs_ref[...]
        t = z[:, Q_LORA + KV_LORA:] * cs_t
        k_rope = (t + pltpu.roll(t, QK_ROPE, axis=1)).astype(MXU_DTYPE)
        low = lax.broadcasted_iota(jnp.int32, (1, LANES), 1) < QK_ROPE
        for hd in range(H):
            qf = _dot(cq, wuq_ref[hd])
            tq = qf[:, QK_NOPE:] * cs_t
            tq = tq + pltpu.roll(tq, QK_ROPE, axis=1)
            q_ref[hd, :, :QK_NOPE] = (qf[:, :QK_NOPE] * MLA_QSCALE).astype(MXU_DTYPE)
            q_ref[hd, :, QK_NOPE:] = jnp.where(low, tq * MLA_QSCALE, 0.0).astype(MXU_DTYPE)
            kvf = _dot(ckv, wukv_ref[hd])
            k_ref[hd, :, :QK_NOPE] = kvf[:, :QK_NOPE].astype(MXU_DTYPE)
            k_ref[hd, :, QK_NOPE:] = k_rope
            v_ref[hd] = kvf[:, QK_NOPE:].astype(MXU_DTYPE)

    zc = wcat.shape[1]
    return pl.pallas_call(
        body, name="mla_pre", grid=(T // tm,),
        in_specs=[_rows(tm, D), _full((8, D)), _full(wcat.shape), _full(g_q.shape), _full(g_kv.shape),
                  _full(wuq.shape), _full(wukv.shape), _rows(tm, LANES)],
        out_specs=[_rows(tm, D), _rows(tm, zc),
                   pl.BlockSpec((H, tm, QK_EXT), lambda i: (0, i, 0)),
                   pl.BlockSpec((H, tm, QK_EXT), lambda i: (0, i, 0)),
                   pl.BlockSpec((H, tm, V_DIM), lambda i: (0, i, 0))],
        out_shape=[jax.ShapeDtypeStruct((T, D), MXU_DTYPE), jax.ShapeDtypeStruct((T, zc), jnp.float32),
                   jax.ShapeDtypeStruct((H, T, QK_EXT), MXU_DTYPE), jax.ShapeDtypeStruct((H, T, QK_EXT), MXU_DTYPE),
                   jax.ShapeDtypeStruct((H, T, V_DIM), MXU_DTYPE)],
        compiler_params=_params("parallel"),
    )(x, vec, wcat, g_q, g_kv, wuq, wukv, cs)


def _mla_attn_fwd(q, k, v):
    H, T, _ = q.shape
    tb = _tile(T, ATTN_BLOCK)
    sub = min(ATTN_SUB, tb)
    ns, nb = tb // sub, T // tb
    pairs = [(i, j) for i in range(nb) for j in range(i + 1)]
    qi_tab = jnp.asarray([i for i, _ in pairs], jnp.int32)
    kj_tab = jnp.asarray([j for _, j in pairs], jnp.int32)

    def body(qi_ref, kj_ref, q_ref, k_ref, v_ref, o_ref, lse_ref, m_sc, l_sc, acc_sc):
        qi, kj = qi_ref[pl.program_id(1)], kj_ref[pl.program_id(1)]

        @pl.when(kj == 0)
        def _():
            m_sc[...] = jnp.full_like(m_sc, NEG)
            l_sc[...] = jnp.zeros_like(l_sc)
            acc_sc[...] = jnp.zeros_like(acc_sc)

        def update(r, kk, masked):
            rows, keys = pl.ds(r * sub, sub), pl.ds(kk * sub, sub)
            s = _dot_nt(q_ref[0, rows, :], k_ref[0, keys, :])
            if masked:
                row = lax.broadcasted_iota(jnp.int32, (sub, sub), 0)
                col = lax.broadcasted_iota(jnp.int32, (sub, sub), 1)
                s = jnp.where(col <= row, s, NEG)
            m_prev = m_sc[rows, :]
            m_new = jnp.maximum(m_prev, jnp.max(s, axis=1, keepdims=True))
            alpha = jnp.exp2(m_prev - m_new)
            p = jnp.exp2(s - jnp.tile(m_new, (1, sub // LANES)))
            l_sc[rows, :] = alpha * l_sc[rows, :] + jnp.sum(p, axis=1, keepdims=True)
            acc_sc[rows, :] = alpha * acc_sc[rows, :] + _dot(p.astype(MXU_DTYPE), v_ref[0, keys, :])
            m_sc[rows, :] = m_new

        @pl.when(kj < qi)
        def _():
            for kk in range(ns):
                for r in range(ns):
                    update(r, kk, False)

        @pl.when(kj == qi)
        def _():
            for kk in range(ns):
                for r in range(kk, ns):
                    update(r, kk, r == kk)
            l = l_sc[...]
            o_ref[...] = (acc_sc[...] / l).astype(o_ref.dtype)
            lse = m_sc[...] + jnp.log2(l)
            pick = (lax.broadcasted_iota(jnp.int32, (8, LANES), 1) == 0).astype(jnp.float32)
            row = lax.dot_general(pick, lse, (((1,), (1,)), ((), ())), precision=lax.Precision.HIGHEST,
                                  preferred_element_type=jnp.float32)
            lse_ref[0] = row[0:1, :]

    q_idx = lambda h, p, qi_ref, kj_ref: (h, qi_ref[p], 0)
    kv_idx = lambda h, p, qi_ref, kj_ref: (h, kj_ref[p], 0)
    return pl.pallas_call(
        body, name="mla_attn_fwd",
        grid_spec=pltpu.PrefetchScalarGridSpec(
            num_scalar_prefetch=2, grid=(H, len(pairs)),
            in_specs=[pl.BlockSpec((1, tb, QK_EXT), q_idx), pl.BlockSpec((1, tb, QK_EXT), kv_idx),
                      pl.BlockSpec((1, tb, V_DIM), kv_idx)],
            out_specs=[pl.BlockSpec((tb, V_DIM), lambda h, p, qi_ref, kj_ref: (qi_ref[p], h)),
                       pl.BlockSpec((1, 1, tb), lambda h, p, qi_ref, kj_ref: (h, 0, qi_ref[p]))],
            scratch_shapes=[pltpu.VMEM((tb, LANES), jnp.float32), pltpu.VMEM((tb, LANES), jnp.float32),
                            pltpu.VMEM((tb, V_DIM), jnp.float32)]),
        out_shape=[jax.ShapeDtypeStruct((T, H * V_DIM), MXU_DTYPE), jax.ShapeDtypeStruct((H, 1, T), jnp.float32)],
        compiler_params=_params("parallel", "arbitrary"),
    )(qi_tab, kj_tab, q, k, v)


def _post_attn(o, x, w_o, bias, vec, o_transposed=False):
    T = x.shape[0]
    tm = _tile(T, ROW_TILE)
    o_spec = pl.BlockSpec((D, tm), lambda i: (0, i)) if o_transposed else _rows(tm, D)

    def body(o_ref, x_ref, w_ref, b_ref, vec_ref, y_ref, xm_ref, h_ref):
        y = (_dot_tn if o_transposed else _dot)(o_ref[...], w_ref[...]) + b_ref[...]
        y_ref[...] = y.astype(y_ref.dtype)
        xm = x_ref[...] + vec_ref[R_GT1:R_GT1 + 1, :] * y
        xm_ref[...] = xm
        xhat, _ = _rms(xm)
        h = xhat * vec_ref[R_GMLP:R_GMLP + 1, :] * (1.0 + vec_ref[R_SC2:R_SC2 + 1, :]) + vec_ref[R_SH2:R_SH2 + 1, :]
        h_ref[...] = h.astype(h_ref.dtype)

    return pl.pallas_call(
        body, name="post_attn", grid=(T // tm,),
        in_specs=[o_spec, _rows(tm, D), _full((D, D)), _full((1, D)), _full((8, D))],
        out_specs=[_rows(tm, D), _rows(tm, D), _rows(tm, D)],
        out_shape=[jax.ShapeDtypeStruct((T, D), MXU_DTYPE), jax.ShapeDtypeStruct((T, D), jnp.float32),
                   jax.ShapeDtypeStruct((T, D), MXU_DTYPE)],
        compiler_params=_params("parallel"),
    )(o, x, w_o, bias, vec)


def _ff_specs(tf):
    per = D_FF // N_CHIPS // tf
    w1 = pl.BlockSpec((None, D, tf), lambda i, f: (f // per, 0, f % per))
    w2 = pl.BlockSpec((None, tf, D), lambda i, f: (f // per, f % per, 0))
    return w1, w2


def _mlp_fwd(h2, w1, w2, xm, vec):
    T = h2.shape[0]
    tm = _tile(T, MLP_FWD_TILE[0])
    tf = _tile(D_FF // N_CHIPS, MLP_FWD_TILE[1])
    nf = D_FF // tf
    w1_spec, w2_spec = _ff_specs(tf)

    def body(h_ref, w1_ref, w2_ref, xm_ref, vec_ref, a_ref, y_ref, xo_ref, acc):
        f = pl.program_id(1)

        @pl.when(f == 0)
        def _():
            acc[...] = jnp.zeros_like(acc)

        u = jnp.maximum(_dot(h_ref[...], w1_ref[...]), 0.0)
        ab = (u * u).astype(MXU_DTYPE)
        a_ref[...] = ab
        acc[...] += _dot(ab, w2_ref[...])

        @pl.when(f == nf - 1)
        def _():
            y = acc[...]
            y_ref[...] = y.astype(y_ref.dtype)
            xo_ref[...] = xm_ref[...] + vec_ref[R_GT2:R_GT2 + 1, :] * y

    return pl.pallas_call(
        body, name="mlp_fwd", grid=(T // tm, nf),
        in_specs=[_rows(tm, D), w1_spec, w2_spec, _rows(tm, D), _full((8, D))],
        out_specs=[pl.BlockSpec((tm, tf), lambda i, f: (i, f)), _rows(tm, D), _rows(tm, D)],
        out_shape=[jax.ShapeDtypeStruct((T, D_FF), MXU_DTYPE), jax.ShapeDtypeStruct((T, D), MXU_DTYPE),
                   jax.ShapeDtypeStruct((T, D), jnp.float32)],
        scratch_shapes=[pltpu.VMEM((tm, D), jnp.float32)],
        compiler_params=_params("parallel", "arbitrary"),
    )(h2, w1, w2, xm, vec)


def _swa_pre(x, vec, w_qkv, b_qkv):
    T = x.shape[0]
    tm = _tile(T, PROJ_ROWS)
    nq = SWA_HEADS * SWA_HEAD_DIM
    nk = SWA_KV_HEADS * SWA_HEAD_DIM
    wq_t, w_kv = w_qkv[:, :nq].T, w_qkv[:, nq:]
    bq_col, b_kv = b_qkv[:, :nq].reshape(nq, 1), b_qkv[:, nq:]

    def body(x_ref, vec_ref, wq_ref, wkv_ref, bq_ref, bkv_ref, h_ref, qt_ref, k_ref, v_ref):
        xhat, _ = _rms(x_ref[...])
        h = xhat * vec_ref[R_GMIX:R_GMIX + 1, :] * (1.0 + vec_ref[R_SC1:R_SC1 + 1, :]) + vec_ref[R_SH1:R_SH1 + 1, :]
        hb = h.astype(MXU_DTYPE)
        h_ref[...] = hb
        qt_ref[...] = ((_dot_nt(wq_ref[...], hb) + bq_ref[...]) * SWA_SCALE).astype(MXU_DTYPE)
        kv = _dot(hb, wkv_ref[...]) + bkv_ref[...]
        k_ref[...] = kv[:, :nk].astype(MXU_DTYPE)
        v_ref[...] = kv[:, nk:].astype(MXU_DTYPE)

    return pl.pallas_call(
        body, name="swa_pre", grid=(T // tm,),
        in_specs=[_rows(tm, D), _full((8, D)), _full(wq_t.shape), _full(w_kv.shape), _full(bq_col.shape),
                  _full(b_kv.shape)],
        out_specs=[_rows(tm, D), pl.BlockSpec((nq, tm), lambda i: (0, i)), _rows(tm, nk), _rows(tm, nk)],
        out_shape=[jax.ShapeDtypeStruct((T, D), MXU_DTYPE), jax.ShapeDtypeStruct((nq, T), MXU_DTYPE),
                   jax.ShapeDtypeStruct((T, nk), MXU_DTYPE), jax.ShapeDtypeStruct((T, nk), MXU_DTYPE)],
        compiler_params=_params("parallel"),
    )(x, vec, wq_t, w_kv, bq_col, b_kv)


def _swa_bias():
    W = WINDOW
    slopes = 2.0 ** (-8.0 * np.arange(1, SWA_HEADS + 1) / SWA_HEADS)
    j, i = np.arange(W)[:, None], np.arange(W)[None, :]
    dist = np.where(j > i, W + i - j, i - j)
    bias = -slopes[:, None, None] * dist[None].astype(np.float64)
    bias = bias.reshape(SWA_KV_HEADS, SWA_GROUP, W, W).transpose(0, 2, 1, 3)
    return jnp.asarray(bias.reshape(SWA_KV_HEADS, W, SWA_GROUP * W), jnp.float32)


def _swa_fold_mask():
    W, G = WINDOW, SWA_GROUP
    j = lax.broadcasted_iota(jnp.int32, (W, G * W), 0)
    i = lax.broadcasted_iota(jnp.int32, (W, G * W), 1) & (W - 1)
    return j > i


def _swa_fold(band, up):
    return jnp.where(up, band[:WINDOW], band[WINDOW:])


def _swa_unfold(folded, up):
    zero = jnp.zeros_like(folded)
    return jnp.concatenate([jnp.where(up, folded, zero), jnp.where(up, zero, folded)], axis=0)


SWA_STEP_BLOCKS = 4


def _swa_blocks(T):
    nb = T // WINDOW
    return next(b for b in (SWA_STEP_BLOCKS, 2, 1) if nb % b == 0)


def _swa_views(b, qt_ref, kp_ref, kc_ref):
    W = WINDOW
    prev = kp_ref if b == 0 else kc_ref.at[pl.ds((b - 1) * W, W), :]
    return qt_ref.at[:, pl.ds(b * W, W)], prev, kc_ref.at[pl.ds(b * W, W), :]


def _swa_probs(has_prev, up, kh, qt_ref, kp_ref, kc_ref, bias_ref, sink_ref):
    W, Dh, G = WINDOW, SWA_HEAD_DIM, SWA_GROUP
    qt = jnp.concatenate([qt_ref[(kh * G + g) * Dh:(kh * G + g + 1) * Dh, :] for g in range(G)], axis=1)
    kb = jnp.concatenate([kp_ref[:, kh * Dh:(kh + 1) * Dh], kc_ref[:, kh * Dh:(kh + 1) * Dh]], axis=0)
    s = _swa_fold(_dot(kb, qt), up) + bias_ref[kh]
    if has_prev is not True:
        s = jnp.where(up & jnp.logical_not(has_prev), NEG, s)
    sink = sink_ref[kh]
    m = jnp.maximum(jnp.max(s, axis=0, keepdims=True), sink)
    p = jnp.exp(s - m)
    p_sink = jnp.exp(sink - m)
    inv = 1.0 / (jnp.sum(p, axis=0, keepdims=True) + p_sink)
    return qt, kb, p * inv, p_sink * inv


def _swa_attn_fwd(qt, k, v, bias, sink_rows):
    T = qt.shape[1]
    W, Dh, G, Hk = WINDOW, SWA_HEAD_DIM, SWA_GROUP, SWA_KV_HEADS
    nk = Hk * Dh

    nb = _swa_blocks(T)

    def body(qt_ref, kp_ref, kc_ref, vp_ref, vc_ref, bias_ref, sink_ref, ot_ref):
        n = pl.program_id(0)
        up = _swa_fold_mask()
        for b in range(nb):
            q_b, kp_b, kc_b = _swa_views(b, qt_ref, kp_ref, kc_ref)
            _, vp_b, vc_b = _swa_views(b, qt_ref, vp_ref, vc_ref)
            for kh in range(Hk):
                _, _, pn, _ = _swa_probs(True if b else n > 0, up, kh, q_b, kp_b, kc_b, bias_ref, sink_ref)
                vb = jnp.concatenate([vp_b[:, kh * Dh:(kh + 1) * Dh], vc_b[:, kh * Dh:(kh + 1) * Dh]], axis=0)
                ot = _dot_tn(vb, _swa_unfold(pn, up).astype(MXU_DTYPE))
                for g in range(G):
                    rows = pl.ds((kh * G + g) * Dh, Dh)
                    ot_ref[rows, pl.ds(b * W, W)] = ot[:, g * W:(g + 1) * W].astype(ot_ref.dtype)

    prev = lambda n: (jnp.maximum(n * nb - 1, 0), 0)
    cur = lambda n: (n, 0)
    col = lambda n: (0, n)
    return pl.pallas_call(
        body, name="swa_attn_fwd", grid=(T // (nb * W),),
        in_specs=[pl.BlockSpec((D, nb * W), col), pl.BlockSpec((W, nk), prev), pl.BlockSpec((nb * W, nk), cur),
                  pl.BlockSpec((W, nk), prev), pl.BlockSpec((nb * W, nk), cur), _full(bias.shape),
                  _full(sink_rows.shape)],
        out_specs=pl.BlockSpec((D, nb * W), col),
        out_shape=jax.ShapeDtypeStruct((D, T), MXU_DTYPE),
        compiler_params=_params("parallel"),
    )(qt, k, k, v, v, bias, sink_rows)


def _final_loss(x, tgt, g):
    T = x.shape[0]
    tm = _tile(T, ROW_TILE)

    def body(x_ref, t_ref, g_ref, loss_ref, dx_ref, dg_ref):
        @pl.when(pl.program_id(0) == 0)
        def _():
            loss_ref[...] = jnp.zeros_like(loss_ref)
            dg_ref[...] = jnp.zeros_like(dg_ref)

        xhat, r = _rms(x_ref[...])
        gv = g_ref[...]
        e = xhat * gv - t_ref[...]
        loss_ref[...] += 0.5 * jnp.sum(jnp.mean(e * e, axis=-1, keepdims=True), axis=0, keepdims=True)
        dy = e * (1.0 / D)
        dg_ref[...] += _rowsum(dy * xhat)
        dx_ref[...] = _rms_bwd(dy * gv, xhat, r)

    return pl.pallas_call(
        body, name="final_loss", grid=(T // tm,),
        in_specs=[_rows(tm, D), _rows(tm, D), _full((1, D))],
        out_specs=[_full((8, LANES)), _rows(tm, D), _full((1, D))],
        out_shape=[jax.ShapeDtypeStruct((8, LANES), jnp.float32), jax.ShapeDtypeStruct((T, D), jnp.float32),
                   jax.ShapeDtypeStruct((1, D), jnp.float32)],
        compiler_params=_params("arbitrary"),
    )(x, tgt, g)


def _mlp_bwd(dxo, y2, a, w1, w2, xm, vec):
    T = dxo.shape[0]
    tm = _tile(T, MLP_BWD_TILE[0])
    tf = _tile(D_FF // N_CHIPS, MLP_BWD_TILE[1])
    nf = D_FF // tf
    w1_spec, w2_spec = _ff_specs(tf)

    def body(dxo_ref, y_ref, a_ref, w1_ref, w2_ref, xm_ref, vec_ref, du_ref, dy_ref, dxm_ref, ps_ref, dyb, acc):
        i, f = pl.program_id(0), pl.program_id(1)

        @pl.when((i == 0) & (f == 0))
        def _():
            ps_ref[...] = jnp.zeros_like(ps_ref)

        @pl.when(f == 0)
        def _():
            dxo_t = dxo_ref[...]
            d = (dxo_t * vec_ref[R_GT2:R_GT2 + 1, :]).astype(MXU_DTYPE)
            dyb[...] = d
            dy_ref[...] = d
            acc[...] = jnp.zeros_like(acc)
            ps_ref[R_GT2:R_GT2 + 1, :] += _rowsum(dxo_t * y_ref[...].astype(jnp.float32))

        da = _dot_nt(dyb[...], w2_ref[...])
        dub = (da * (2.0 * jnp.sqrt(a_ref[...].astype(jnp.float32)))).astype(MXU_DTYPE)
        du_ref[...] = dub
        acc[...] += _dot_nt(dub, w1_ref[...])

        @pl.when(f == nf - 1)
        def _():
            dxm_ref[...] = _modulate_bwd(acc[...], xm_ref[...], vec_ref, R_GMLP, R_SC2, R_SH2, ps_ref, dxo_ref[...])

    return pl.pallas_call(
        body, name="mlp_bwd", grid=(T // tm, nf),
        in_specs=[_rows(tm, D), _rows(tm, D), pl.BlockSpec((tm, tf), lambda i, f: (i, f)), w1_spec, w2_spec,
                  _rows(tm, D), _full((8, D))],
        out_specs=[pl.BlockSpec((tm, tf), lambda i, f: (i, f)), _rows(tm, D), _rows(tm, D), _full((8, D))],
        out_shape=[jax.ShapeDtypeStruct((T, D_FF), MXU_DTYPE), jax.ShapeDtypeStruct((T, D), MXU_DTYPE),
                   jax.ShapeDtypeStruct((T, D), jnp.float32), jax.ShapeDtypeStruct((8, D), jnp.float32)],
        scratch_shapes=[pltpu.VMEM((tm, D), MXU_DTYPE), pltpu.VMEM((tm, D), jnp.float32)],
        compiler_params=_params("arbitrary", "arbitrary"),
    )(dxo, y2, a, w1, w2, xm, vec)


def _mm_tn(a, g, name, split=None, layers=1, layer=0, into=None, a_transposed=False):
    K, T = a.shape if a_transposed else a.shape[::-1]
    N = g.shape[1]
    kq = K // N_CHIPS if split == "rows" else K
    nq = N // N_CHIPS if split == "cols" else N
    bk, bn, bt = _tile(kq, DW_TILE), _tile(nq, DW_TILE), _tile(T, DW_TOKENS)
    if nq % bn or bn % LANES:
        bn = nq
    kper, nper = kq // bk, nq // bn

    def body(*refs):
        a_ref, g_ref, o_ref = refs[0], refs[1], refs[-1]

        @pl.when(pl.program_id(2) == 0)
        def _():
            o_ref[...] = jnp.zeros_like(o_ref)

        o_ref[...] += (_dot if a_transposed else _dot_tn)(a_ref[...], g_ref[...])

    a_spec = pl.BlockSpec((bk, bt), lambda k, n, t: (k, t)) if a_transposed else pl.BlockSpec((bt, bk), lambda k, n, t: (t, k))
    in_specs = [a_spec, pl.BlockSpec((bt, bn), lambda k, n, t: (t, n))]
    args = [a, g]
    aliases = {}
    if split is None:
        out_spec = pl.BlockSpec((bk, bn), lambda k, n, t: (k, n))
        out_shape = jax.ShapeDtypeStruct((K, N), jnp.float32)
    else:
        if split == "cols":
            idx = lambda k, n, t: (n // nper, layer, k, n % nper)
        else:
            idx = lambda k, n, t: (k // kper, layer, k % kper, n)
        out_spec = pl.BlockSpec((None, None, bk, bn), idx)
        out_shape = jax.ShapeDtypeStruct((N_CHIPS, layers, kq, nq), jnp.float32)
        if into is not None:
            in_specs.append(pl.BlockSpec(memory_space=pl.ANY))
            args.append(into)
            aliases = {2: 0}
    return pl.pallas_call(
        body, name=name, grid=(K // bk, N // bn, T // bt), in_specs=in_specs, out_specs=out_spec, out_shape=out_shape,
        input_output_aliases=aliases, compiler_params=_params("parallel", "parallel", "arbitrary"),
    )(*args)


def _attn_out_bwd(dxm, y1, o, w_o, vec, with_delta):
    T = dxm.shape[0]
    tm = _tile(T, ROW_TILE)
    H = MLA_HEADS

    def body(dxm_ref, y_ref, w_ref, vec_ref, *refs):
        o_ref = refs[0] if with_delta else None
        dy_ref, do_ref, ps_ref, *delta_ref = refs[1:] if with_delta else refs

        @pl.when(pl.program_id(0) == 0)
        def _():
            ps_ref[...] = jnp.zeros_like(ps_ref)

        dxm_t = dxm_ref[...]
        dy = dxm_t * vec_ref[R_GT1:R_GT1 + 1, :]
        ps_ref[R_GT1:R_GT1 + 1, :] += _rowsum(dxm_t * y_ref[...].astype(jnp.float32))
        ps_ref[R_BO:R_BO + 1, :] += _rowsum(dy)
        dyb = dy.astype(MXU_DTYPE)
        dy_ref[...] = dyb
        if not with_delta:
            do_ref[...] = _dot_nt(w_ref[...], dyb).astype(do_ref.dtype)
        else:
            do = _dot_nt(dyb, w_ref[...])
            do_ref[...] = do.astype(do_ref.dtype)
            of = o_ref[...].astype(jnp.float32)
            ones = jnp.ones((8, V_DIM), jnp.float32)
            for hd in range(H):
                sl = slice(hd * V_DIM, (hd + 1) * V_DIM)
                d = lax.dot_general(ones, do[:, sl] * of[:, sl], (((1,), (1,)), ((), ())),
                                    precision=lax.Precision.HIGHEST, preferred_element_type=jnp.float32)
                delta_ref[0][hd] = d[0:1, :]

    out_specs = [_rows(tm, D), _rows(tm, D), _full((8, D))]
    out_shape = [jax.ShapeDtypeStruct((T, D), MXU_DTYPE), jax.ShapeDtypeStruct((T, D), MXU_DTYPE),
                 jax.ShapeDtypeStruct((8, D), jnp.float32)]
    if not with_delta:
        out_specs[1] = pl.BlockSpec((D, tm), lambda i: (0, i))
        out_shape[1] = jax.ShapeDtypeStruct((D, T), MXU_DTYPE)
    if with_delta:
        out_specs.append(pl.BlockSpec((H, 1, tm), lambda i: (0, 0, i)))
        out_shape.append(jax.ShapeDtypeStruct((H, 1, T), jnp.float32))
    return pl.pallas_call(
        body, name="attn_out_bwd_mla" if with_delta else "attn_out_bwd_swa", grid=(T // tm,),
        in_specs=[_rows(tm, D), _rows(tm, D), _full((D, D)), _full((8, D))] + ([_rows(tm, D)] if with_delta else []),
        out_specs=out_specs, out_shape=out_shape,
        compiler_params=_params("arbitrary"),
    )(dxm, y1, w_o, vec, *([o] if with_delta else []))


def _mla_attn_bwd(q, k, v, do, lse, delta):
    H, T, _ = q.shape
    tb = _tile(T, ATTN_BLOCK)
    sub = min(ATTN_SUB, tb)
    ns, nb = tb // sub, T // tb

    pairs = [(j, i) for j in range(nb) for i in range(j, nb)]
    kj_tab = jnp.asarray([j for j, _ in pairs], jnp.int32)
    qi_tab = jnp.asarray([i for _, i in pairs], jnp.int32)

    def body(kj_ref, qi_ref, q_ref, k_ref, v_ref, do_ref, lse_ref, dl_ref, dq_ref, dk_ref, dv_ref, dk_acc, dv_acc):
        j, i = kj_ref[pl.program_id(1)], qi_ref[pl.program_id(1)]

        @pl.when((j == 0) & (i == 0))
        def _():
            dq_ref[...] = jnp.zeros_like(dq_ref)

        def update(kk, r, masked):
            keys, rows = pl.ds(kk * sub, sub), pl.ds(r * sub, sub)
            kb, qb, dob = k_ref[0, keys, :], q_ref[0, rows, :], do_ref[rows, :]
            st = _dot_nt(kb, qb)
            if masked:
                row = lax.broadcasted_iota(jnp.int32, (sub, sub), 0)
                col = lax.broadcasted_iota(jnp.int32, (sub, sub), 1)
                st = jnp.where(row <= col, st, NEG)
            pt = jnp.exp2(st - lse_ref[0, :, rows])
            dv_acc[keys, :] += _dot(pt.astype(MXU_DTYPE), dob)
            dpt = _dot_nt(v_ref[0, keys, :], dob)
            dst = (pt * (dpt - dl_ref[0, :, rows])).astype(MXU_DTYPE)
            dk_acc[keys, :] += _dot(dst, qb)
            q_rows = pl.ds(pl.multiple_of(i * tb + r * sub, sub), sub)
            dq_ref[0, q_rows, :] += _dot_tn(dst, kb)

        @pl.when(i == j)
        def _():
            dk_acc[...] = jnp.zeros_like(dk_acc)
            dv_acc[...] = jnp.zeros_like(dv_acc)
            for r in range(ns):
                for kk in range(r + 1):
                    update(kk, r, kk == r)

        @pl.when(i > j)
        def _():
            for r in range(ns):
                for kk in range(ns):
                    update(kk, r, False)

        @pl.when(i == nb - 1)
        def _():
            dk_ref[0] = (dk_acc[...] * LN2).astype(dk_ref.dtype)
            dv_ref[0] = dv_acc[...].astype(dv_ref.dtype)

    q_idx = lambda h, p, kj_ref, qi_ref: (h, qi_ref[p], 0)
    kv_idx = lambda h, p, kj_ref, qi_ref: (h, kj_ref[p], 0)
    stat_idx = lambda h, p, kj_ref, qi_ref: (h, 0, qi_ref[p])
    return pl.pallas_call(
        body, name="mla_attn_bwd",
        grid_spec=pltpu.PrefetchScalarGridSpec(
            num_scalar_prefetch=2, grid=(H, len(pairs)),
            in_specs=[pl.BlockSpec((1, tb, QK_EXT), q_idx), pl.BlockSpec((1, tb, QK_EXT), kv_idx),
                      pl.BlockSpec((1, tb, V_DIM), kv_idx),
                      pl.BlockSpec((tb, V_DIM), lambda h, p, kj_ref, qi_ref: (qi_ref[p], h)),
                      pl.BlockSpec((1, 1, tb), stat_idx), pl.BlockSpec((1, 1, tb), stat_idx)],
            out_specs=[pl.BlockSpec((1, T, QK_EXT), lambda h, p, kj_ref, qi_ref: (h, 0, 0)),
                       pl.BlockSpec((1, tb, QK_EXT), kv_idx), pl.BlockSpec((1, tb, V_DIM), kv_idx)],
            scratch_shapes=[pltpu.VMEM((tb, QK_EXT), jnp.float32), pltpu.VMEM((tb, V_DIM), jnp.float32)]),
        out_shape=[jax.ShapeDtypeStruct((H, T, QK_EXT), jnp.float32), jax.ShapeDtypeStruct((H, T, QK_EXT), MXU_DTYPE),
                   jax.ShapeDtypeStruct((H, T, V_DIM), MXU_DTYPE)],
        compiler_params=_params("parallel", "arbitrary"),
    )(kj_tab, qi_tab, q, k, v, do, lse, delta)


def _mla_pre_bwd(x, dxm, vec, hb, z, dq, dk, dv, cs, wcat, g_q, g_kv, wuq, wukv):
    T = x.shape[0]
    tm = _tile(T, PROJ_ROWS)
    H = MLA_HEADS
    zc = wcat.shape[1]

    def body(x_ref, dxm_ref, vec_ref, h_ref, z_ref, dq_ref, dk_ref, dv_ref, cs_ref, wcat_ref, gq_ref, gkv_ref,
             wuq_ref, wukv_ref, dx_ref, ps_ref, dgq_ref, dgkv_ref, dwcat_ref, dwuq_ref, dwukv_ref):
        @pl.when(pl.program_id(0) == 0)
        def _():
            for ref in (ps_ref, dgq_ref, dgkv_ref, dwcat_ref, dwuq_ref, dwukv_ref):
                ref[...] = jnp.zeros_like(ref)

        z = z_ref[...]
        cs_t = cs_ref[...]
        cqhat, rq = _rms(z[:, :Q_LORA])
        ckhat, rk = _rms(z[:, Q_LORA:Q_LORA + KV_LORA])
        gq, gkv = gq_ref[...], gkv_ref[...]
        cq = (cqhat * gq).astype(MXU_DTYPE)
        ckv = (ckhat * gkv).astype(MXU_DTYPE)
        dcq = jnp.zeros((tm, Q_LORA), jnp.float32)
        dckv = jnp.zeros((tm, KV_LORA), jnp.float32)
        dkr = jnp.zeros((tm, LANES), jnp.float32)
        for hd in range(H):
            dqh = dq_ref[hd] * MLA_SCALE
            gqh = jnp.concatenate([dqh[:, :QK_NOPE], dqh[:, QK_NOPE:] * cs_t], axis=1).astype(MXU_DTYPE)
            dcq += _dot_nt(gqh, wuq_ref[hd])
            dwuq_ref[hd] += _dot_tn(cq, gqh)
            dkh = dk_ref[hd]
            gkvh = jnp.concatenate([dkh[:, :QK_NOPE], dv_ref[hd]], axis=1)
            dckv += _dot_nt(gkvh, wukv_ref[hd])
            dwukv_ref[hd] += _dot_tn(ckv, gkvh)
            dkr += dkh[:, QK_NOPE:].astype(jnp.float32)
        dgq_ref[...] += _rowsum(dcq * cqhat)
        dgkv_ref[...] += _rowsum(dckv * ckhat)
        dcq_pre = _rms_bwd(dcq * gq, cqhat, rq)
        dckv_pre = _rms_bwd(dckv * gkv, ckhat, rk)
        dkr2 = (dkr + pltpu.roll(dkr, QK_ROPE, axis=1)) * cs_t
        dz = jnp.concatenate([dcq_pre, dckv_pre, dkr2], axis=1).astype(MXU_DTYPE)
        dwcat_ref[...] += _dot_tn(h_ref[...], dz)
        dh = _dot_nt(dz, wcat_ref[...])
        dx_ref[...] = _modulate_bwd(dh, x_ref[...], vec_ref, R_GMIX, R_SC1, R_SH1, ps_ref, dxm_ref[...])

    hblk = lambda w: pl.BlockSpec((H, tm, w), lambda i: (0, i, 0))
    return pl.pallas_call(
        body, name="mla_pre_bwd", grid=(T // tm,),
        in_specs=[_rows(tm, D), _rows(tm, D), _full((8, D)), _rows(tm, D), _rows(tm, zc), hblk(QK_EXT), hblk(QK_EXT),
                  hblk(V_DIM), _rows(tm, LANES), _full(wcat.shape), _full(g_q.shape), _full(g_kv.shape),
                  _full(wuq.shape), _full(wukv.shape)],
        out_specs=[_rows(tm, D), _full((8, D)), _full(g_q.shape), _full(g_kv.shape), _full(wcat.shape),
                   _full(wuq.shape), _full(wukv.shape)],
        out_shape=[jax.ShapeDtypeStruct((T, D), jnp.float32), jax.ShapeDtypeStruct((8, D), jnp.float32),
                   jax.ShapeDtypeStruct(g_q.shape, jnp.float32), jax.ShapeDtypeStruct(g_kv.shape, jnp.float32),
                   jax.ShapeDtypeStruct(wcat.shape, jnp.float32), jax.ShapeDtypeStruct(wuq.shape, jnp.float32),
                   jax.ShapeDtypeStruct(wukv.shape, jnp.float32)],
        compiler_params=_params("arbitrary"),
    )(x, dxm, vec, hb, z, dq, dk, dv, cs, wcat, g_q, g_kv, wuq, wukv)


def _swa_attn_bwd(qt, k, v, dot_, bias, sink_rows):
    T = qt.shape[1]
    W, Dh, G, Hk = WINDOW, SWA_HEAD_DIM, SWA_GROUP, SWA_KV_HEADS
    nk = Hk * Dh
    nb = _swa_blocks(T)

    def body(qt_ref, kp_ref, kc_ref, vp_ref, vc_ref, dot_ref, bias_ref, sink_ref, dqt_ref, dk_ref, dv_ref, dsink_ref):
        n = pl.program_id(0)

        @pl.when(n == 0)
        def _():
            dk_ref[...] = jnp.zeros_like(dk_ref)
            dv_ref[...] = jnp.zeros_like(dv_ref)
            dsink_ref[...] = jnp.zeros_like(dsink_ref)

        def add_rows(first_row, dkb_part, dvb_part):
            rows = pl.ds(pl.multiple_of(first_row, W), W)
            dk_ref[rows, :] += dkb_part
            dv_ref[rows, :] += dvb_part

        up = _swa_fold_mask()
        for b in range(nb):
            q_b, kp_b, kc_b = _swa_views(b, qt_ref, kp_ref, kc_ref)
            do_b, vp_b, vc_b = _swa_views(b, dot_ref, vp_ref, vc_ref)
            dks, dvs = [], []
            for kh in range(Hk):
                qt, kb, pn, p_sink = _swa_probs(True if b else n > 0, up, kh, q_b, kp_b, kc_b, bias_ref, sink_ref)
                vb = jnp.concatenate([vp_b[:, kh * Dh:(kh + 1) * Dh], vc_b[:, kh * Dh:(kh + 1) * Dh]], axis=0)
                dot_h = jnp.concatenate([do_b[(kh * G + g) * Dh:(kh * G + g + 1) * Dh, :] for g in range(G)], axis=1)
                dp = _swa_fold(_dot(vb, dot_h), up)
                delta = jnp.sum(pn * dp, axis=0, keepdims=True)
                dsb = _swa_unfold(pn * (dp - delta), up).astype(MXU_DTYPE)
                dsink_ref[kh] += -p_sink * delta
                dqt = _dot_tn(kb, dsb) * SWA_SCALE
                for g in range(G):
                    dqt_ref[pl.ds((kh * G + g) * Dh, Dh), pl.ds(b * W, W)] = dqt[:, g * W:(g + 1) * W]
                dks.append(_dot_nt(dsb, qt))
                dvs.append(_dot_nt(_swa_unfold(pn, up).astype(MXU_DTYPE), dot_h))
            dkb = jnp.concatenate(dks, axis=1)
            dvb = jnp.concatenate(dvs, axis=1)
            add_rows((n * nb + b) * W, dkb[W:], dvb[W:])
            if b:
                add_rows((n * nb + b - 1) * W, dkb[:W], dvb[:W])
            else:
                @pl.when(n > 0)
                def _():
                    add_rows((n * nb - 1) * W, dkb[:W], dvb[:W])

    prev = lambda n: (jnp.maximum(n * nb - 1, 0), 0)
    cur = lambda n: (n, 0)
    col = lambda n: (0, n)
    return pl.pallas_call(
        body, name="swa_attn_bwd", grid=(T // (nb * W),),
        in_specs=[pl.BlockSpec((D, nb * W), col), pl.BlockSpec((W, nk), prev), pl.BlockSpec((nb * W, nk), cur),
                  pl.BlockSpec((W, nk), prev), pl.BlockSpec((nb * W, nk), cur), pl.BlockSpec((D, nb * W), col),
                  _full(bias.shape), _full(sink_rows.shape)],
        out_specs=[pl.BlockSpec((D, nb * W), col), _full((T, nk)), _full((T, nk)), _full(sink_rows.shape)],
        out_shape=[jax.ShapeDtypeStruct((D, T), jnp.float32), jax.ShapeDtypeStruct((T, nk), jnp.float32),
                   jax.ShapeDtypeStruct((T, nk), jnp.float32), jax.ShapeDtypeStruct(sink_rows.shape, jnp.float32)],
        compiler_params=_params("arbitrary"),
    )(qt, k, k, v, v, dot_, bias, sink_rows)


def _swa_pre_bwd(x, dxm, vec, dq_t, dk, dv, w_qkv):
    T = x.shape[0]
    tm = _tile(T, PROJ_ROWS)
    nq = SWA_HEADS * SWA_HEAD_DIM
    nk = SWA_KV_HEADS * SWA_HEAD_DIM
    nqkv = nq + 2 * nk

    def body(x_ref, dxm_ref, vec_ref, dq_ref, dk_ref, dv_ref, w_ref, dx_ref, dqkv_ref, ps_ref, db_ref):
        @pl.when(pl.program_id(0) == 0)
        def _():
            ps_ref[...] = jnp.zeros_like(ps_ref)
            db_ref[...] = jnp.zeros_like(db_ref)

        dqkv = jnp.concatenate([dq_ref[...].T, dk_ref[...], dv_ref[...]], axis=1)
        db_ref[...] += _rowsum(dqkv)
        dqkv_b = dqkv.astype(MXU_DTYPE)
        dqkv_ref[...] = dqkv_b
        dh = _dot_nt(dqkv_b, w_ref[...])
        dx_ref[...] = _modulate_bwd(dh, x_ref[...], vec_ref, R_GMIX, R_SC1, R_SH1, ps_ref, dxm_ref[...])

    return pl.pallas_call(
        body, name="swa_pre_bwd", grid=(T // tm,),
        in_specs=[_rows(tm, D), _rows(tm, D), _full((8, D)), pl.BlockSpec((nq, tm), lambda i: (0, i)), _rows(tm, nk),
                  _rows(tm, nk), _full(w_qkv.shape)],
        out_specs=[_rows(tm, D), _rows(tm, nqkv), _full((8, D)), _full((1, nqkv))],
        out_shape=[jax.ShapeDtypeStruct((T, D), jnp.float32), jax.ShapeDtypeStruct((T, nqkv), MXU_DTYPE),
                   jax.ShapeDtypeStruct((8, D), jnp.float32), jax.ShapeDtypeStruct((1, nqkv), jnp.float32)],
        compiler_params=_params("arbitrary"),
    )(x, dxm, vec, dq_t, dk, dv, w_qkv)


def _rot_cols(w):
    half = QK_ROPE // 2
    return jnp.concatenate([-w[..., half:], w[..., :half]], axis=-1)


def _unrot_grad(d_rope, d_rot):
    half = QK_ROPE // 2
    return d_rope + jnp.concatenate([d_rot[..., half:], -d_rot[..., :half]], axis=-1)


def _rope_table(positions):
    half = QK_ROPE // 2
    inv_freq = ROPE_THETA ** (-jnp.arange(half, dtype=jnp.float32) / half)
    ang = positions.astype(jnp.float32)[:, None] * inv_freq
    cos, sin = jnp.cos(ang), jnp.sin(ang)
    return jnp.concatenate([cos, cos, sin, sin], axis=1)


def _sequence_step(x, tgt, positions, vecs, g_q, g_kv, sinks, g_final, wts, late_weights, on_late_grads, on_late_landed):
    H = MLA_HEADS
    cs = _rope_table(positions)
    w_dkv = wts["mla_w_dkv"]
    wcat = jnp.concatenate([wts["mla_w_dq"], w_dkv, _rot_cols(w_dkv[:, KV_LORA:])], axis=1)
    uq = wts["mla_w_uq"].reshape(Q_LORA, H, QK_NOPE + QK_ROPE)
    wuq = jnp.concatenate([uq, _rot_cols(uq[..., QK_NOPE:])], axis=-1).transpose(1, 0, 2)
    wukv = wts["mla_w_ukv"].reshape(KV_LORA, H, QK_NOPE + V_DIM).transpose(1, 0, 2)
    zero_bias = jnp.zeros((1, D), jnp.float32)
    bias = _swa_bias()
    sink_rows = jnp.broadcast_to(sinks.reshape(SWA_KV_HEADS, 1, SWA_GROUP, 1),
                                 (SWA_KV_HEADS, 1, SWA_GROUP, WINDOW)).reshape(SWA_KV_HEADS, 1, SWA_GROUP * WINDOW)

    h1a, z, q, k, v = _mla_pre(x, vecs[0], wcat, g_q, g_kv, wuq, wukv, cs)
    o_a, lse = _mla_attn_fwd(q, k, v)
    y1a, xm_a, h2a = _post_attn(o_a, x, wts["mla_w_o"], zero_bias, vecs[0])
    wts = {**wts, **late_weights(h2a)}
    a_a, y2a, x1 = _mlp_fwd(h2a, wts["w_ff1"][0], wts["w_ff2"][0], xm_a, vecs[0])

    h1b, qs_t, ks, vs = _swa_pre(x1, vecs[1], wts["swa_w_qkv"], wts["swa_b_qkv"])
    o_bt = _swa_attn_fwd(qs_t, ks, vs, bias, sink_rows)
    y1b, xm_b, h2b = _post_attn(o_bt, x1, wts["swa_w_o"], wts["swa_b_o"], vecs[1], o_transposed=True)
    a_b, y2b, x2 = _mlp_fwd(h2b, wts["w_ff1"][1], wts["w_ff2"][1], xm_b, vecs[1])

    loss8, dx2, dg_final = _final_loss(x2, tgt, g_final.reshape(1, D))

    du_b, dy2b, dxm_b, ps_mlp_b = _mlp_bwd(dx2, y2b, a_b, wts["w_ff1"][1], wts["w_ff2"][1], xm_b, vecs[1])
    g_ff2 = _mm_tn(a_b, dy2b, "dw_ff2_l1", "rows", DEPTH, 1)
    g_ff1 = _mm_tn(h2b, du_b, "dw_ff1_l1", "cols", DEPTH, 1)
    dy1b, do_bt, ps_out_b = _attn_out_bwd(dxm_b, y1b, None, wts["swa_w_o"], vecs[1], False)
    g_swa_o = _mm_tn(o_bt, dy1b, "dw_o_swa", a_transposed=True)
    dqs_t, dks, dvs, dsinks = _swa_attn_bwd(qs_t, ks, vs, do_bt, bias, sink_rows)
    dx1, dqkv, ps_pre_b, g_swa_bqkv = _swa_pre_bwd(x1, dxm_b, vecs[1], dqs_t, dks, dvs, wts["swa_w_qkv"])
    g_swa_qkv = _mm_tn(h1b, dqkv, "dw_qkv", "cols")

    du_a, dy2a, dxm_a, ps_mlp_a = _mlp_bwd(dx1, y2a, a_a, wts["w_ff1"][0], wts["w_ff2"][0], xm_a, vecs[0])
    g_ff2 = _mm_tn(a_a, dy2a, "dw_ff2_l0", "rows", DEPTH, 0, g_ff2)
    g_ff1 = _mm_tn(h2a, du_a, "dw_ff1_l0", "cols", DEPTH, 0, g_ff1)
    rows4 = lambda g: g.reshape(N_CHIPS, g.shape[0] // N_CHIPS, g.shape[1])
    token = on_late_grads({
        "swa_w_qkv": g_swa_qkv.reshape(N_CHIPS, D, -1), "swa_w_o": rows4(g_swa_o),
        "w_ff1": g_ff1.reshape(N_CHIPS, DEPTH * D, -1), "w_ff2": g_ff2.reshape(N_CHIPS, -1, D)})
    dy1a, do_a, ps_out_a, delta = _attn_out_bwd(dxm_a, y1a, o_a, wts["mla_w_o"], vecs[0] + token[0, 0], True)
    g_mla_o = _mm_tn(o_a, dy1a, "dw_o_mla")
    token = on_late_landed(g_mla_o)
    dq, dk, dv = _mla_attn_bwd(q, k, v, do_a, lse, delta + token[0, 0])
    dx0, ps_pre_a, dg_q, dg_kv, dwcat, dwuq, dwukv = _mla_pre_bwd(
        x, dxm_a, vecs[0], h1a, z, dq, dk, dv, cs, wcat, g_q, g_kv, wuq, wukv)

    c0, c1, c2 = Q_LORA, Q_LORA + KV_LORA, Q_LORA + KV_LORA + QK_ROPE
    g_dq = dwcat[:, :c0]
    g_dkv = jnp.concatenate([dwcat[:, c0:c1], _unrot_grad(dwcat[:, c1:c2], dwcat[:, c2:])], axis=1)
    e0 = QK_NOPE + QK_ROPE
    g_uq = jnp.concatenate([dwuq[..., :QK_NOPE], _unrot_grad(dwuq[..., QK_NOPE:e0], dwuq[..., e0:])], axis=-1)
    per = H // N_CHIPS
    g_uq = g_uq.reshape(N_CHIPS, per, Q_LORA, e0).transpose(0, 2, 1, 3).reshape(N_CHIPS, Q_LORA, per * e0)
    g_ukv = dwukv.reshape(N_CHIPS, per, KV_LORA, QK_NOPE + V_DIM).transpose(0, 2, 1, 3)
    g_ukv = g_ukv.reshape(N_CHIPS, KV_LORA, per * (QK_NOPE + V_DIM))

    def dmod(ps_pre, ps_out, ps_mlp):
        return jnp.concatenate([ps_pre[R_SH1:R_SC1 + 1], ps_out[R_GT1:R_GT1 + 1], ps_mlp[R_SH2:R_GT2 + 1]], axis=0)

    grads = {"mla_w_dq": rows4(g_dq), "mla_w_uq": g_uq, "mla_w_dkv": rows4(g_dkv), "mla_w_ukv": g_ukv,
             "mla_w_o": rows4(g_mla_o)}
    small = {
        "dmod": jnp.stack([dmod(ps_pre_a, ps_out_a, ps_mlp_a), dmod(ps_pre_b, ps_out_b, ps_mlp_b)]).reshape(DEPTH, 6 * D),
        "g_mix": jnp.stack([ps_pre_a[R_GMIX], ps_pre_b[R_GMIX]]),
        "g_mlp": jnp.stack([ps_mlp_a[R_GMLP], ps_mlp_b[R_GMLP]]),
        "mla_g_q": dg_q, "mla_g_kv": dg_kv, "swa_sinks": jnp.sum(dsinks.reshape(SWA_HEADS, WINDOW), axis=1).reshape(1, SWA_HEADS),
        "swa_b_qkv": g_swa_bqkv, "swa_b_o": ps_out_b[R_BO:R_BO + 1],
        "g_final": dg_final.reshape(D), "loss": loss8[0, 0],
    }
    return dx0, grads, small


SHARDED = {
    "mla_w_dq": (1, D // N_CHIPS, Q_LORA),
    "mla_w_uq": (1, Q_LORA, MLA_HEADS * (QK_NOPE + QK_ROPE) // N_CHIPS),
    "mla_w_dkv": (1, D // N_CHIPS, KV_LORA + QK_ROPE),
    "mla_w_ukv": (1, KV_LORA, MLA_HEADS * (QK_NOPE + V_DIM) // N_CHIPS),
    "mla_w_o": (1, MLA_HEADS * V_DIM // N_CHIPS, D),
    "swa_w_qkv": (1, D, (SWA_HEADS + 2 * SWA_KV_HEADS) * SWA_HEAD_DIM // N_CHIPS),
    "swa_w_o": (1, SWA_HEADS * SWA_HEAD_DIM // N_CHIPS, D),
    "w_ff1": (DEPTH, D, D_FF // N_CHIPS),
    "w_ff2": (DEPTH, D_FF // N_CHIPS, D),
}
COL_SPLIT = ("mla_w_uq", "mla_w_ukv", "swa_w_qkv")
BIASES = {"swa_b_qkv": (SWA_HEADS + 2 * SWA_KV_HEADS) * SWA_HEAD_DIM, "swa_b_o": D}


def _view2d(name):
    shape = SHARDED[name]
    return math.prod(shape[:-1]), shape[-1]


SMALL = {"b_ada": (DEPTH, 6 * D), "g_mix": (DEPTH, D), "g_mlp": (DEPTH, D), "mla_g_q": (1, Q_LORA),
         "mla_g_kv": (1, KV_LORA), "swa_sinks": (1, SWA_HEADS), "g_final": (D,), "loss": (),
         "swa_b_qkv": (1, BIASES["swa_b_qkv"]), "swa_b_o": (1, BIASES["swa_b_o"])}
SMALL_ROWS = 192
DMA_ROWS = 256


SLOT_ROWS = 8


def _small_slots():
    slots, off = {}, 0
    for name, shape in SMALL.items():
        n = max(math.prod(shape), 1)
        slots[name] = (off, n)
        off += -(-n // (SLOT_ROWS * LANES)) * SLOT_ROWS * LANES
    assert off <= SMALL_ROWS * LANES
    return slots


def _pack_small(vals):
    parts, end = [], 0
    for name, (off, n) in _small_slots().items():
        pad = -(-n // (SLOT_ROWS * LANES)) * SLOT_ROWS * LANES - n
        v = vals[name].astype(jnp.float32).reshape(-1) if name in vals else jnp.zeros((n,), jnp.float32)
        parts += [v, jnp.zeros((pad,), jnp.float32)]
        end = off + n + pad
    parts.append(jnp.zeros((SMALL_ROWS * LANES - end,), jnp.float32))
    return jnp.concatenate(parts).reshape(SMALL_ROWS, LANES)


def _from_slot(name, rows):
    n = max(math.prod(SMALL[name]), 1)
    return rows.reshape(-1)[:n].reshape(SMALL[name])


def _pieces(rows):
    return [(off, min(DMA_ROWS, rows - off)) for off in range(0, rows, DMA_ROWS)]


HBM = pl.BlockSpec(memory_space=pltpu.HBM)
MESH = pl.DeviceIdType.MESH


def _place():
    x, y, c = lax.axis_index("x"), lax.axis_index("y"), lax.axis_index("c")
    chips = [(1 - x, y), (x, 1 - y), (1 - x, 1 - y)]
    return x, y, c, chips


def _all_gather(block):
    m_per, n = block.shape

    def body(x_ref, out_ref, send_sems, recv_sems, local_sem):
        x, y, c, chips = _place()
        me, sibling = (x, y, c), (x, y, 1 - c)

        def rows(px, py, pc):
            return out_ref.at[pl.ds((4 * px + 2 * py + pc) * m_per, m_per), :]

        def copy(k, blk, to, src=None):
            return pltpu.make_async_remote_copy(
                src_ref=rows(*blk) if src is None else src, dst_ref=rows(*blk),
                send_sem=send_sems.at[k], recv_sem=recv_sems.at[k], device_id=to, device_id_type=MESH)

        mine = pltpu.make_async_copy(x_ref, rows(*me), local_sem)
        mine.start()
        first = [copy(0, me, sibling, src=x_ref)]
        first += [copy(1 + j, me, (*chip, c), src=x_ref) for j, chip in enumerate(chips)]
        for cp in first:
            cp.start()
        passed = [copy(4 + j, (*chip, c), sibling) for j, chip in enumerate(chips)]
        for j, chip in enumerate(chips):
            copy(1 + j, (*chip, c), me).wait_recv()
            passed[j].start()
        copy(0, sibling, me).wait_recv()
        for j, chip in enumerate(chips):
            copy(4 + j, (*chip, 1 - c), me).wait_recv()
        for cp in first + passed:
            cp.wait_send()
        mine.wait()

    out = pl.pallas_call(
        body, name="all_gather_small",
        out_shape=jax.ShapeDtypeStruct((N_DEV * m_per, n), block.dtype),
        in_specs=[pl.BlockSpec(memory_space=pltpu.VMEM)],
        out_specs=pl.BlockSpec(memory_space=pltpu.VMEM),
        scratch_shapes=[pltpu.SemaphoreType.DMA((7,)), pltpu.SemaphoreType.DMA((7,)), pltpu.SemaphoreType.DMA],
    )(block)
    return out.reshape(N_DEV, m_per, n)


def _weight_gather(shards):
    nt = len(shards)

    def body(*refs):
        w_refs, out_refs = refs[:nt], refs[nt:2 * nt]
        send_sems, recv_sems = refs[2 * nt:]
        x, y, c, chips = _place()
        sibling = (x, y, 1 - c)

        def slab(t, px, py, half):
            rh = shards[t].shape[0] // 2
            return out_refs[t].at[2 * px + py, pl.ds(half * rh, rh), :]

        def copy(t, k, src, dst, to):
            return pltpu.make_async_remote_copy(src_ref=src, dst_ref=dst, send_sem=send_sems.at[6 * t + k],
                                                recv_sem=recv_sems.at[6 * t + k], device_id=to, device_id_type=MESH)

        first = []
        for t in range(nt):
            rh = shards[t].shape[0] // 2
            first += [copy(t, j, w_refs[t].at[pl.ds(c * rh, rh), :], slab(t, x, y, c), (*chip, c))
                      for j, chip in enumerate(chips)]
        for cp in first:
            cp.start()
        passed = []
        for t in range(nt):
            for j, chip in enumerate(chips):
                copy(t, j, slab(t, *chip, c), slab(t, *chip, c), (*chip, c)).wait_recv()
                rh = shards[t].shape[0] // 2
                for off, n in _pieces(rh):
                    piece = out_refs[t].at[2 * chip[0] + chip[1], pl.ds(c * rh + off, n), :]
                    copy(t, 3 + j, piece, piece, sibling).start()
                passed.append(copy(t, 3 + j, slab(t, *chip, c), slab(t, *chip, c), sibling))
        for t in range(nt):
            for j, chip in enumerate(chips):
                copy(t, 3 + j, slab(t, *chip, 1 - c), slab(t, *chip, 1 - c), sibling).wait_recv()
        for cp in first + passed:
            cp.wait_send()

    return pl.pallas_call(
        body, name="weight_gather",
        out_shape=[jax.ShapeDtypeStruct((N_CHIPS,) + s.shape, s.dtype) for s in shards],
        in_specs=[HBM] * nt, out_specs=[HBM] * nt,
        scratch_shapes=[pltpu.SemaphoreType.DMA((6 * nt,)), pltpu.SemaphoreType.DMA((6 * nt,))],
    )(*shards)


SEM = pl.BlockSpec(memory_space=pltpu.SEMAPHORE)
ANY = pl.BlockSpec(memory_space=pl.ANY)
SPLIT_COPY = pltpu.SideEffectType.DATAFLOW_SIDE_EFFECTING


def _late_copies(w_refs, land_refs, send_sems, recv_sems):
    x, y, c, chips = _place()
    return [pltpu.make_async_remote_copy(
        src_ref=w_refs[t], dst_ref=land_refs[t].at[2 * x + y], send_sem=send_sems.at[3 * t + j],
        recv_sem=recv_sems.at[3 * t + j], device_id=(cx, cy, c), device_id_type=MESH)
        for t in range(len(w_refs)) for j, (cx, cy) in enumerate(chips)], chips


def _late_gather_start(shards, after):
    nt, na = len(shards), len(after)

    def body(*refs):
        w_refs, land_refs = refs[:nt], refs[nt:2 * nt]
        send_sems, recv_sems, token = refs[2 * nt + na], refs[2 * nt + na + 1], refs[-1]
        copies, _ = _late_copies(w_refs, land_refs, send_sems, recv_sems)
        for cp in copies:
            cp.start()
        token[...] = jnp.zeros_like(token)

    hbm = lambda a: pltpu.with_memory_space_constraint(a, pltpu.HBM)
    lands = [lax.empty((N_CHIPS,) + s.shape, s.dtype) for s in shards]
    outs = pl.pallas_call(
        body, name="late_gather_start",
        out_shape=(pltpu.SemaphoreType.DMA((3 * nt,)), pltpu.SemaphoreType.DMA((3 * nt,)),
                   *[pltpu.HBM(s.shape, s.dtype) for s in shards], *[pltpu.HBM(l.shape, l.dtype) for l in lands],
                   jax.ShapeDtypeStruct((8, LANES), jnp.float32)),
        in_specs=[HBM] * (2 * nt) + [ANY] * na,
        out_specs=(SEM, SEM, *([HBM] * (2 * nt)), pl.BlockSpec(memory_space=pltpu.VMEM)),
        input_output_aliases={i: 2 + i for i in range(2 * nt)},
        compiler_params=pltpu.CompilerParams(has_side_effects=SPLIT_COPY),
    )(*[hbm(s) for s in shards], *[hbm(l) for l in lands], *after)
    return outs[0], outs[1], list(outs[2:2 + nt]), list(outs[2 + nt:2 + 2 * nt]), outs[-1]


def _late_gather_wait(send_sems, recv_sems, shards, lands, after):
    nt = len(shards)

    def body(*refs):
        w_refs, land_refs = refs[:nt], refs[nt:2 * nt]
        s_sems, r_sems = refs[2 * nt], refs[2 * nt + 1]
        x, y, c, chips = _place()
        for t in range(nt):
            for j, (cx, cy) in enumerate(chips):
                cp = pltpu.make_async_remote_copy(
                    src_ref=w_refs[t], dst_ref=land_refs[t].at[2 * cx + cy], send_sem=s_sems.at[3 * t + j],
                    recv_sem=r_sems.at[3 * t + j], device_id=(cx, cy, c), device_id_type=MESH)
                cp.wait_send()
                cp.wait_recv()

    outs = pl.pallas_call(
        body, name="late_gather_wait",
        out_shape=(*[pltpu.HBM(s.shape, s.dtype) for s in shards], *[pltpu.HBM(l.shape, l.dtype) for l in lands]),
        in_specs=[HBM] * (2 * nt) + [SEM, SEM, ANY], out_specs=tuple([HBM] * (2 * nt)),
        input_output_aliases={i: i for i in range(2 * nt)},
        compiler_params=pltpu.CompilerParams(has_side_effects=SPLIT_COPY),
    )(*shards, *lands, send_sems, recv_sems, after)
    return list(outs[nt:])


def _grad_pair_in(grads, behind):
    nt = len(grads)

    def body(*refs):
        g_refs, got_refs = refs[:nt], refs[nt + 1:2 * nt + 1]
        send_sems, recv_sems = refs[2 * nt + 1:]
        x, y, c, _ = _place()
        sibling = (x, y, 1 - c)

        def copy(t, src, dst):
            return pltpu.make_async_remote_copy(src_ref=src, dst_ref=dst, send_sem=send_sems.at[t],
                                                recv_sem=recv_sems.at[t], device_id=sibling, device_id_type=MESH)

        for t in range(nt):
            rh = grads[t].shape[1] // 2
            for p in range(N_CHIPS):
                for off, n in _pieces(rh):
                    copy(t, g_refs[t].at[p, pl.ds((1 - c) * rh + off, n), :], got_refs[t].at[p, pl.ds(off, n), :]).start()
        for t in range(nt):
            rh = grads[t].shape[1] // 2
            copy(t, g_refs[t].at[:, pl.ds((1 - c) * rh, rh), :], got_refs[t]).wait()

    return pl.pallas_call(
        body, name="grad_pair_in",
        out_shape=[jax.ShapeDtypeStruct((N_CHIPS, g.shape[1] // 2, g.shape[2]), g.dtype) for g in grads],
        in_specs=[HBM] * nt + [ANY], out_specs=[HBM] * nt,
        scratch_shapes=[pltpu.SemaphoreType.DMA((nt,)), pltpu.SemaphoreType.DMA((nt,))],
    )(*grads, behind)


def _pair_in_start(grads):
    nt = len(grads)

    def body(*refs):
        g_refs, land_refs = refs[:nt], refs[nt:2 * nt]
        send_sems, recv_sems, token = refs[2 * nt], refs[2 * nt + 1], refs[-1]
        x, y, c, _ = _place()
        for t in range(nt):
            rh = grads[t].shape[1] // 2
            for p in range(N_CHIPS):
                for off, n in _pieces(rh):
                    pltpu.make_async_remote_copy(
                        src_ref=g_refs[t].at[p, pl.ds((1 - c) * rh + off, n), :], dst_ref=land_refs[t].at[p, pl.ds(off, n), :],
                        send_sem=send_sems.at[t], recv_sem=recv_sems.at[t], device_id=(x, y, 1 - c),
                        device_id_type=MESH).start()
        token[...] = jnp.zeros_like(token)

    hbm = lambda a: pltpu.with_memory_space_constraint(a, pltpu.HBM)
    lands = [lax.empty((N_CHIPS, g.shape[1] // 2, g.shape[2]), g.dtype) for g in grads]
    outs = pl.pallas_call(
        body, name="grad_pair_in_start",
        out_shape=(pltpu.SemaphoreType.DMA((nt,)), pltpu.SemaphoreType.DMA((nt,)),
                   *[pltpu.HBM(g.shape, g.dtype) for g in grads], *[pltpu.HBM(l.shape, l.dtype) for l in lands],
                   jax.ShapeDtypeStruct((8, LANES), jnp.float32)),
        in_specs=[HBM] * (2 * nt),
        out_specs=(SEM, SEM, *([HBM] * (2 * nt)), pl.BlockSpec(memory_space=pltpu.VMEM)),
        input_output_aliases={i: 2 + i for i in range(2 * nt)},
        compiler_params=pltpu.CompilerParams(has_side_effects=SPLIT_COPY),
    )(*[hbm(g) for g in grads], *[hbm(l) for l in lands])
    return outs[0], outs[1], list(outs[2:2 + nt]), list(outs[2 + nt:2 + 2 * nt]), outs[-1]


def _pair_in_wait(send_sems, recv_sems, grads, lands, after):
    nt = len(grads)

    def body(*refs):
        g_refs, land_refs = refs[:nt], refs[nt:2 * nt]
        s_sems, r_sems = refs[2 * nt], refs[2 * nt + 1]
        x, y, c, _ = _place()
        for t in range(nt):
            rh = grads[t].shape[1] // 2
            cp = pltpu.make_async_remote_copy(
                src_ref=g_refs[t].at[:, pl.ds((1 - c) * rh, rh), :], dst_ref=land_refs[t], send_sem=s_sems.at[t],
                recv_sem=r_sems.at[t], device_id=(x, y, 1 - c), device_id_type=MESH)
            cp.wait_send()
            cp.wait_recv()

    outs = pl.pallas_call(
        body, name="grad_pair_in_wait",
        out_shape=(*[pltpu.HBM(g.shape, g.dtype) for g in grads], *[pltpu.HBM(l.shape, l.dtype) for l in lands]),
        in_specs=[HBM] * (2 * nt) + [SEM, SEM, ANY], out_specs=tuple([HBM] * (2 * nt)),
        input_output_aliases={i: i for i in range(2 * nt)},
        compiler_params=pltpu.CompilerParams(has_side_effects=SPLIT_COPY),
    )(*grads, *lands, send_sems, recv_sems, after)
    return list(outs[:nt]), list(outs[nt:])


def _pair_sum(g, got, core, name):
    _, rows, cols = g.shape
    rh = rows // 2
    tr = _tile(rh, 512)
    nb = rh // tr

    def body(c_ref, g_ref, got_ref, s16_ref):
        s16_ref[...] = (g_ref[...] + got_ref[...]).astype(s16_ref.dtype)

    blk = pl.BlockSpec((None, tr, cols), lambda p, i, c_ref: (p, i, 0))
    return pl.pallas_call(
        body, name=name,
        grid_spec=pltpu.PrefetchScalarGridSpec(
            num_scalar_prefetch=1, grid=(N_CHIPS, nb),
            in_specs=[pl.BlockSpec((None, tr, cols), lambda p, i, c_ref: (p, c_ref[0] * nb + i, 0)), blk],
            out_specs=blk),
        out_shape=jax.ShapeDtypeStruct((N_CHIPS, rh, cols), jnp.bfloat16),
        compiler_params=_params("parallel", "parallel"),
    )(core, g, got)


def _exchange_start(parts, name):
    nt = len(parts)

    def body(*refs):
        a_refs, land_refs = refs[:nt], refs[nt:2 * nt]
        send_sems, recv_sems, token = refs[2 * nt], refs[2 * nt + 1], refs[-1]
        x, y, c, chips = _place()
        for t in range(nt):
            for j, (cx, cy) in enumerate(chips):
                pltpu.make_async_remote_copy(
                    src_ref=a_refs[t].at[2 * cx + cy], dst_ref=land_refs[t].at[j], send_sem=send_sems.at[3 * t + j],
                    recv_sem=recv_sems.at[3 * t + j], device_id=(cx, cy, c), device_id_type=MESH).start()
        token[...] = jnp.zeros_like(token)

    hbm = lambda a: pltpu.with_memory_space_constraint(a, pltpu.HBM)
    lands = [lax.empty((N_CHIPS - 1,) + a.shape[1:], a.dtype) for a in parts]
    outs = pl.pallas_call(
        body, name=name,
        out_shape=(pltpu.SemaphoreType.DMA((3 * nt,)), pltpu.SemaphoreType.DMA((3 * nt,)),
                   *[pltpu.HBM(a.shape, a.dtype) for a in parts], *[pltpu.HBM(l.shape, l.dtype) for l in lands],
                   jax.ShapeDtypeStruct((8, LANES), jnp.float32)),
        in_specs=[HBM] * (2 * nt),
        out_specs=(SEM, SEM, *([HBM] * (2 * nt)), pl.BlockSpec(memory_space=pltpu.VMEM)),
        input_output_aliases={i: 2 + i for i in range(2 * nt)},
        compiler_params=pltpu.CompilerParams(has_side_effects=SPLIT_COPY),
    )(*[hbm(a) for a in parts], *[hbm(l) for l in lands])
    return outs[0], outs[1], list(outs[2:2 + nt]), list(outs[2 + nt:2 + 2 * nt]), outs[-1]


def _exchange_wait(send_sems, recv_sems, parts, lands, after, name):
    nt = len(parts)

    def body(*refs):
        a_refs, land_refs = refs[:nt], refs[nt:2 * nt]
        s_sems, r_sems = refs[2 * nt], refs[2 * nt + 1]
        x, y, c, chips = _place()
        for t in range(nt):
            for j, (cx, cy) in enumerate(chips):
                cp = pltpu.make_async_remote_copy(
                    src_ref=a_refs[t].at[2 * cx + cy], dst_ref=land_refs[t].at[j], send_sem=s_sems.at[3 * t + j],
                    recv_sem=r_sems.at[3 * t + j], device_id=(cx, cy, c), device_id_type=MESH)
                cp.wait_send()
                cp.wait_recv()

    outs = pl.pallas_call(
        body, name=name,
        out_shape=(*[pltpu.HBM(a.shape, a.dtype) for a in parts], *[pltpu.HBM(l.shape, l.dtype) for l in lands]),
        in_specs=[HBM] * (2 * nt) + [SEM, SEM, ANY], out_specs=tuple([HBM] * (2 * nt)),
        input_output_aliases={i: i for i in range(2 * nt)},
        compiler_params=pltpu.CompilerParams(has_side_effects=SPLIT_COPY),
    )(*parts, *lands, send_sems, recv_sems, after)
    return list(outs[nt:])


def _chip_sum(g, got_pair, got_chips, core, chip, name):
    _, rh, cols = got_pair.shape
    tr = _tile(rh, STREAM_ROWS)
    nb = rh // tr

    def body(c_ref, p_ref, g_ref, pair_ref, chips_ref, o_ref):
        acc = g_ref[...] + pair_ref[...]
        for j in range(N_CHIPS - 1):
            acc = acc + chips_ref[j].astype(jnp.float32)
        o_ref[...] = acc

    return pl.pallas_call(
        body, name=name,
        grid_spec=pltpu.PrefetchScalarGridSpec(
            num_scalar_prefetch=2, grid=(nb,),
            in_specs=[pl.BlockSpec((None, tr, cols), lambda i, c_ref, p_ref: (p_ref[0], c_ref[0] * nb + i, 0)),
                      pl.BlockSpec((None, tr, cols), lambda i, c_ref, p_ref: (p_ref[0], i, 0)),
                      pl.BlockSpec((N_CHIPS - 1, tr, cols), lambda i, c_ref, p_ref: (0, i, 0))],
            out_specs=pl.BlockSpec((tr, cols), lambda i, c_ref, p_ref: (i, 0))),
        out_shape=jax.ShapeDtypeStruct((rh, cols), jnp.float32),
        compiler_params=_params("parallel"),
    )(core, chip, g, got_pair, got_chips)


def _pair_out_start(halves):
    nt = len(halves)

    def body(*refs):
        h_refs, land_refs = refs[:nt], refs[nt:2 * nt]
        send_sems, recv_sems, token = refs[2 * nt], refs[2 * nt + 1], refs[-1]
        x, y, c, _ = _place()
        for t in range(nt):
            for off, n in _pieces(halves[t].shape[0]):
                pltpu.make_async_remote_copy(
                    src_ref=h_refs[t].at[pl.ds(off, n), :], dst_ref=land_refs[t].at[pl.ds(off, n), :],
                    send_sem=send_sems.at[t], recv_sem=recv_sems.at[t], device_id=(x, y, 1 - c),
                    device_id_type=MESH).start()
        token[...] = jnp.zeros_like(token)

    hbm = lambda a: pltpu.with_memory_space_constraint(a, pltpu.HBM)
    lands = [lax.empty(h.shape, h.dtype) for h in halves]
    outs = pl.pallas_call(
        body, name="grad_pair_out_start",
        out_shape=(pltpu.SemaphoreType.DMA((nt,)), pltpu.SemaphoreType.DMA((nt,)),
                   *[pltpu.HBM(h.shape, h.dtype) for h in halves], *[pltpu.HBM(l.shape, l.dtype) for l in lands],
                   jax.ShapeDtypeStruct((8, LANES), jnp.float32)),
        in_specs=[HBM] * (2 * nt),
        out_specs=(SEM, SEM, *([HBM] * (2 * nt)), pl.BlockSpec(memory_space=pltpu.VMEM)),
        input_output_aliases={i: 2 + i for i in range(2 * nt)},
        compiler_params=pltpu.CompilerParams(has_side_effects=SPLIT_COPY),
    )(*[hbm(h) for h in halves], *[hbm(l) for l in lands])
    return outs[0], outs[1], list(outs[2:2 + nt]), list(outs[2 + nt:2 + 2 * nt]), outs[-1]


def _pair_out_wait(send_sems, recv_sems, halves, lands, after):
    nt = len(halves)

    def body(*refs):
        h_refs, land_refs = refs[:nt], refs[nt:2 * nt]
        s_sems, r_sems = refs[2 * nt], refs[2 * nt + 1]
        x, y, c, _ = _place()
        for t in range(nt):
            cp = pltpu.make_async_remote_copy(
                src_ref=h_refs[t], dst_ref=land_refs[t], send_sem=s_sems.at[t], recv_sem=r_sems.at[t],
                device_id=(x, y, 1 - c), device_id_type=MESH)
            cp.wait_send()
            cp.wait_recv()

    outs = pl.pallas_call(
        body, name="grad_pair_out_wait",
        out_shape=(*[pltpu.HBM(h.shape, h.dtype) for h in halves], *[pltpu.HBM(l.shape, l.dtype) for l in lands]),
        in_specs=[HBM] * (2 * nt) + [SEM, SEM, ANY], out_specs=tuple([HBM] * (2 * nt)),
        input_output_aliases={i: i for i in range(2 * nt)},
        compiler_params=pltpu.CompilerParams(has_side_effects=SPLIT_COPY),
    )(*halves, *lands, send_sems, recv_sems, after)
    return list(outs[:nt]), list(outs[nt:])


def _grad_pair_out(halves):
    nt = len(halves)

    def body(*refs):
        h_refs, got_refs = refs[:nt], refs[nt:2 * nt]
        send_sems, recv_sems = refs[2 * nt:]
        x, y, c, _ = _place()
        sibling = (x, y, 1 - c)

        def copy(t, src, dst):
            return pltpu.make_async_remote_copy(src_ref=src, dst_ref=dst, send_sem=send_sems.at[t],
                                                recv_sem=recv_sems.at[t], device_id=sibling, device_id_type=MESH)

        for t in range(nt):
            for off, n in _pieces(halves[t].shape[0]):
                copy(t, h_refs[t].at[pl.ds(off, n), :], got_refs[t].at[pl.ds(off, n), :]).start()
        for t in range(nt):
            copy(t, h_refs[t], got_refs[t]).wait()

    return pl.pallas_call(
        body, name="grad_pair_out",
        out_shape=[jax.ShapeDtypeStruct(h.shape, h.dtype) for h in halves],
        in_specs=[HBM] * nt, out_specs=[HBM] * nt,
        scratch_shapes=[pltpu.SemaphoreType.DMA((nt,)), pltpu.SemaphoreType.DMA((nt,))],
    )(*halves)


def _ada_part(c_all, w_ada):
    L, _, ncol = w_ada.shape
    tn = _tile(ncol, 512)

    def body(c_ref, w_ref, cond_ref, part_ref):
        cv = c_ref[...]
        cond = cv * jax.nn.sigmoid(cv)
        cond_ref[...] = cond
        part_ref[0] = jnp.dot(cond, w_ref[0], precision=lax.Precision.HIGHEST, preferred_element_type=jnp.float32)

    return pl.pallas_call(
        body, name="ada_part", grid=(L, ncol // tn),
        in_specs=[_full((N_DEV, D)), pl.BlockSpec((1, D, tn), lambda l, j: (l, 0, j))],
        out_specs=[_full((N_DEV, D)), pl.BlockSpec((1, N_DEV, tn), lambda l, j: (l, 0, j))],
        out_shape=[jax.ShapeDtypeStruct((N_DEV, D), jnp.float32), jax.ShapeDtypeStruct((L, N_DEV, ncol), jnp.float32)],
        compiler_params=_params("arbitrary", "arbitrary"),
    )(c_all, w_ada)


def _adamw_math(w, g, m, v):
    m = ADAM_B1 * m + (1.0 - ADAM_B1) * g
    v = ADAM_B2 * v + (1.0 - ADAM_B2) * jnp.square(g)
    m_hat = m / (1.0 - ADAM_B1 ** ADAM_STEP)
    v_hat = v / (1.0 - ADAM_B2 ** ADAM_STEP)
    delta = -ADAM_LR * (m_hat / (jnp.sqrt(v_hat) + ADAM_EPS) + ADAM_WD * w)
    return delta, m, v


def _adamw(w, g, m, v, name):
    shape = w.shape
    cols = shape[-1]
    rows = math.prod(shape[:-1])
    tr = _tile(rows, 512)
    two_d = lambda t: t.reshape(rows, cols)

    def body(w_ref, g_ref, m_ref, v_ref, d_ref, mo_ref, vo_ref):
        d_ref[...], mo_ref[...], vo_ref[...] = _adamw_math(w_ref[...], g_ref[...], m_ref[...], v_ref[...])

    out = jax.ShapeDtypeStruct((rows, cols), jnp.float32)
    outs = pl.pallas_call(
        body, name=name, grid=(rows // tr,), in_specs=[_rows(tr, cols)] * 4, out_specs=[_rows(tr, cols)] * 3,
        out_shape=[out, out, out], compiler_params=_params("parallel"),
    )(two_d(w), two_d(g), two_d(m), two_d(v))
    return [t.reshape(shape) for t in outs]


def _adamw_halves(w, mine, got, m, v, core, name):
    shape = w.shape
    cols = shape[-1]
    rows = math.prod(shape[:-1])
    rh = rows // 2
    tr = _tile(rh, STREAM_ROWS)
    nbh = rh // tr
    two_d = lambda t: t.reshape(rows, cols)

    def body(c_ref, w_ref, a_ref, b_ref, m_ref, v_ref, g_ref, d_ref, mo_ref, vo_ref):
        g = jnp.where(pl.program_id(0) // nbh == c_ref[0], a_ref[...], b_ref[...])
        g_ref[...] = g
        d_ref[...], mo_ref[...], vo_ref[...] = _adamw_math(w_ref[...], g, m_ref[...], v_ref[...])

    row = pl.BlockSpec((tr, cols), lambda i, c_ref: (i, 0))

    def half(keep):
        return pl.BlockSpec((tr, cols), lambda i, c_ref: (jnp.where((i // nbh == c_ref[0]) == keep, i % nbh, 0), 0))

    out = jax.ShapeDtypeStruct((rows, cols), jnp.float32)
    outs = pl.pallas_call(
        body, name=name,
        grid_spec=pltpu.PrefetchScalarGridSpec(
            num_scalar_prefetch=1, grid=(rows // tr,),
            in_specs=[row, half(True), half(False), row, row], out_specs=[row] * 4),
        out_shape=[out] * 4, compiler_params=_params("arbitrary"),
    )(core, two_d(w), mine, got, two_d(m), two_d(v))
    return [t.reshape(shape) for t in outs]


def _ada_grad_adamw(cond_t, dm, w, m, v):
    L, _, ncol = w.shape
    tn = _tile(ncol, STREAM_ROWS)

    def body(ct_ref, dm_ref, w_ref, m_ref, v_ref, g_ref, d_ref, mo_ref, vo_ref):
        g = ct_ref[:, 0:1] * dm_ref[0, 0:1, :]
        for b in range(1, N_DEV):
            g = g + ct_ref[:, b:b + 1] * dm_ref[0, b:b + 1, :]
        g_ref[0] = g
        d_ref[0], mo_ref[0], vo_ref[0] = _adamw_math(w_ref[0], g, m_ref[0], v_ref[0])

    wblk = pl.BlockSpec((1, D, tn), lambda l, j: (l, 0, j))
    out = jax.ShapeDtypeStruct(w.shape, jnp.float32)
    return pl.pallas_call(
        body, name="ada_grad_adamw", grid=(L, ncol // tn),
        in_specs=[_full((D, N_DEV)), pl.BlockSpec((1, N_DEV, tn), lambda l, j: (l, 0, j)), wblk, wblk, wblk],
        out_specs=[wblk] * 4, out_shape=[out] * 4, compiler_params=_params("parallel", "parallel"),
    )(cond_t, dm, w, m, v)


def _small_adamw(gathered, w, m, v):
    slots = _small_slots()
    rows = {name: (off // LANES, -(-n // LANES)) for name, (off, n) in slots.items()}
    kinds = {name: 1 if name == "loss" or name in BIASES else 4 for name in slots}

    def body(ga_ref, w_ref, m_ref, v_ref, *out_refs):
        g = ga_ref[0]
        for dev in range(1, N_DEV):
            g = g + ga_ref[dev]
        d, mo, vo = _adamw_math(w_ref[...], g, m_ref[...], v_ref[...])
        k = 0
        for name, (r0, nr) in rows.items():
            for src in (g, d, mo, vo)[:kinds[name]]:
                out_refs[k][...] = src[r0:r0 + nr, :]
                k += 1

    out_shape = [jax.ShapeDtypeStruct((rows[name][1], LANES), jnp.float32) for name in slots for _ in range(kinds[name])]
    flat = pl.pallas_call(
        body, name="small_adamw", out_shape=out_shape,
        in_specs=[pl.BlockSpec(memory_space=pltpu.VMEM)] * 4,
        out_specs=[pl.BlockSpec(memory_space=pltpu.VMEM)] * len(out_shape),
    )(gathered, w, m, v)
    out, k = {}, 0
    for name in slots:
        out[name] = [_from_slot(name, t) for t in flat[k:k + kinds[name]]]
        k += kinds[name]
    return out


def _one_hot_pick(arr, index, axis):
    n = arr.shape[axis]
    shape = [1] * arr.ndim
    shape[axis] = n
    hot = (jnp.arange(n) == index).astype(arr.dtype).reshape(shape)
    return jnp.sum(arr * hot, axis=axis)


def kernel(x, c, positions, w_ada, b_ada, g_mix, g_mlp, mla_w_dq, mla_g_q, mla_w_uq, mla_w_dkv, mla_g_kv, mla_w_ukv, mla_w_o, swa_w_qkv, swa_b_qkv, swa_sinks, swa_w_o, swa_b_o, w_ff1, w_ff2, g_final, loss_target, m_w_ada, m_b_ada, m_g_mix, m_g_mlp, m_mla_w_dq, m_mla_g_q, m_mla_w_uq, m_mla_w_dkv, m_mla_g_kv, m_mla_w_ukv, m_mla_w_o, m_swa_w_qkv, m_swa_b_qkv, m_swa_sinks, m_swa_w_o, m_swa_b_o, m_w_ff1, m_w_ff2, m_g_final, v_w_ada, v_b_ada, v_g_mix, v_g_mlp, v_mla_w_dq, v_mla_g_q, v_mla_w_uq, v_mla_w_dkv, v_mla_g_kv, v_mla_w_ukv, v_mla_w_o, v_swa_w_qkv, v_swa_b_qkv, v_swa_sinks, v_swa_w_o, v_swa_b_o, v_w_ff1, v_w_ff2, v_g_final):
    W = dict(w_ada=w_ada, b_ada=b_ada, g_mix=g_mix, g_mlp=g_mlp, mla_w_dq=mla_w_dq, mla_g_q=mla_g_q, mla_w_uq=mla_w_uq,
             mla_w_dkv=mla_w_dkv, mla_g_kv=mla_g_kv, mla_w_ukv=mla_w_ukv, mla_w_o=mla_w_o, swa_w_qkv=swa_w_qkv,
             swa_b_qkv=swa_b_qkv, swa_sinks=swa_sinks, swa_w_o=swa_w_o, swa_b_o=swa_b_o, w_ff1=w_ff1, w_ff2=w_ff2,
             g_final=g_final)
    M = dict(w_ada=m_w_ada, b_ada=m_b_ada, g_mix=m_g_mix, g_mlp=m_g_mlp, mla_w_dq=m_mla_w_dq, mla_g_q=m_mla_g_q,
             mla_w_uq=m_mla_w_uq, mla_w_dkv=m_mla_w_dkv, mla_g_kv=m_mla_g_kv, mla_w_ukv=m_mla_w_ukv, mla_w_o=m_mla_w_o,
             swa_w_qkv=m_swa_w_qkv, swa_b_qkv=m_swa_b_qkv, swa_sinks=m_swa_sinks, swa_w_o=m_swa_w_o, swa_b_o=m_swa_b_o,
             w_ff1=m_w_ff1, w_ff2=m_w_ff2, g_final=m_g_final)
    V = dict(w_ada=v_w_ada, b_ada=v_b_ada, g_mix=v_g_mix, g_mlp=v_g_mlp, mla_w_dq=v_mla_w_dq, mla_g_q=v_mla_g_q,
             mla_w_uq=v_mla_w_uq, mla_w_dkv=v_mla_w_dkv, mla_g_kv=v_mla_g_kv, mla_w_ukv=v_mla_w_ukv, mla_w_o=v_mla_w_o,
             swa_w_qkv=v_swa_w_qkv, swa_b_qkv=v_swa_b_qkv, swa_sinks=v_swa_sinks, swa_w_o=v_swa_w_o, swa_b_o=v_swa_b_o,
             w_ff1=v_w_ff1, w_ff2=v_w_ff2, g_final=v_g_final)
    order = list(W)
    names = list(SHARDED)
    core = lax.axis_index("c")
    chip = 2 * lax.axis_index("x") + lax.axis_index("y")
    dev = 2 * chip + core
    core_arr = core.astype(jnp.int32).reshape(1)
    chip_arr = chip.astype(jnp.int32).reshape(1)

    def whole(n, g, own):
        g = lax.dynamic_update_slice(g, own[None], (chip, 0, 0))
        if n in ("w_ff1", "w_ff2"):
            return g
        if n in COL_SPLIT:
            return g.transpose(1, 0, 2).reshape(g.shape[1], N_CHIPS * g.shape[2])
        return g.reshape(N_CHIPS * g.shape[1], g.shape[2])

    early = [n for n in names if n.startswith("mla_")]
    local = {n: W[n].astype(MXU_DTYPE).reshape(_view2d(n)) for n in early}
    wts = {n: whole(n, g, local[n]) for n, g in zip(early, _weight_gather([local[n] for n in early]))}

    nbq, nbo = BIASES["swa_b_qkv"] // N_CHIPS, BIASES["swa_b_o"] // N_CHIPS
    first = jnp.concatenate([c.reshape(-1), swa_b_qkv.reshape(-1), swa_b_o.reshape(-1),
                             jnp.zeros((FIRST_ROWS * LANES - D - nbq - nbo,), jnp.float32)]).reshape(FIRST_ROWS, LANES)
    first_all = _all_gather(first).reshape(N_DEV, FIRST_ROWS * LANES)
    c_all = first_all[:, :D]
    south = first_all[0::2]
    wts["swa_b_qkv"] = south[:, D:D + nbq].reshape(1, N_CHIPS * nbq)
    wts["swa_b_o"] = south[:, D + nbq:D + nbq + nbo].reshape(1, N_CHIPS * nbo)
    cond_all, part = _ada_part(c_all, w_ada)
    ncol = w_ada.shape[2]
    part_all = _all_gather(part.reshape(-1, LANES)).reshape(N_DEV, DEPTH, N_DEV, ncol)
    mine = _one_hot_pick(part_all[0::2], dev, axis=2)
    mod = mine.transpose(1, 0, 2).reshape(DEPTH, N_CHIPS * ncol) + b_ada
    vecs = jnp.concatenate([mod.reshape(DEPTH, 6, D), g_mix[:, None, :], g_mlp[:, None, :]], axis=1)

    late = [("w_ff1", 0), ("w_ff2", 0), ("swa_w_qkv", None), ("swa_w_o", None), ("w_ff1", 1), ("w_ff2", 1)]
    late_local = [(W[n][0] if l is None else W[n][l]).astype(MXU_DTYPE) for n, l in late]
    send_sems, recv_sems, passed, lands, token = _late_gather_start(late_local, [vecs] + [wts[n] for n in early])

    def late_weights(after):
        got = _late_gather_wait(send_sems, recv_sems, passed, lands, after)
        out = {"w_ff1": [None] * DEPTH, "w_ff2": [None] * DEPTH}
        for (n, l), g, own in zip(late, got, late_local):
            if l is None:
                out[n] = whole(n, g, own)
            else:
                out[n][l] = whole(n, g, own)
        return out

    late_names = [n for n in names if n not in early]
    reduce_state = {}

    def on_late_grads(late_grads):
        s_sems, r_sems, passed_g, zones, tok = _pair_in_start([late_grads[n] for n in late_names])
        reduce_state.update(pair=(s_sems, r_sems, passed_g, zones))
        return tok

    def on_late_landed(after):
        gl, got = _pair_in_wait(*reduce_state["pair"], after)
        sums = [_pair_sum(g, s, core_arr, "pair_sum_" + n) for n, g, s in zip(late_names, gl, got)]
        s_sems, r_sems, parts, zones, tok = _exchange_start(sums, "grad_exchange_start")
        reduce_state.update(pairs=(gl, got), split=(s_sems, r_sems, parts, zones))
        return tok

    grad_x, grads, small = _sequence_step(
        x[0], loss_target[0], positions[0], vecs, mla_g_q + token[0, 0], mla_g_kv, swa_sinks, g_final, wts,
        late_weights, on_late_grads, on_late_landed)

    small["b_ada"] = small.pop("dmod")
    small_all = _all_gather(_pack_small(small))
    pk = lambda src: _pack_small({n: src[n] for n in SMALL if n != "loss" and n not in BIASES})
    off, n = _small_slots()["b_ada"]
    dmod_all = small_all.reshape(N_DEV, -1)[:, off:off + n].reshape(N_DEV, DEPTH, N_CHIPS, ncol)
    dm = _one_hot_pick(dmod_all, chip, axis=2).transpose(1, 0, 2)

    def chip_sums(tensor_names, gl, got, others):
        return [_chip_sum(g, s, o, core_arr, chip_arr, "chip_sum_" + n) for n, g, s, o in zip(tensor_names, gl, got, others)]

    def adamw(tensor_names, mine, sibling):
        return {n: _adamw_halves(W[n], a, b, M[n], V[n], core_arr, "adamw_" + n)
                for n, a, b in zip(tensor_names, mine, sibling)}

    late_others = _exchange_wait(*reduce_state["split"], grad_x, "grad_exchange_wait")
    p_sems, p_rems, p_halves, p_zones, p_tok = _pair_out_start(chip_sums(late_names, *reduce_state["pairs"], late_others))
    gl = [grads[n] for n in early]
    got = _grad_pair_in(gl, p_tok)
    sums = [_pair_sum(g, s, core_arr, "pair_sum_" + n) for n, g, s in zip(early, gl, got)]
    e_sems, e_rems, e_parts, e_zones, e_tok = _exchange_start(sums, "mla_exchange_start")
    res = adamw(late_names, *_pair_out_wait(p_sems, p_rems, p_halves, p_zones, e_tok))
    res["w_ada"] = _ada_grad_adamw(cond_all.T, dm, w_ada, m_w_ada, v_w_ada)
    small_res = _small_adamw(small_all, pk(W), pk(M), pk(V))
    early_others = _exchange_wait(e_sems, e_rems, e_parts, e_zones, res["w_ff2"][1], "mla_exchange_wait")
    early_halves = chip_sums(early, gl, got, early_others)
    res.update(adamw(early, early_halves, _grad_pair_out(early_halves)))

    for n, width in BIASES.items():
        g = _one_hot_pick(small_res[n][0].reshape(N_CHIPS, width // N_CHIPS), chip, axis=0).reshape(1, -1)
        res[n] = [g] + _adamw(W[n], g, M[n], V[n], "adamw_" + n)
    for name in order:
        if name not in res:
            res[name] = small_res[name]
    outs = [small_res["loss"][0], grad_x[None]]
    for k in range(4):
        outs += [res[name][k] for name in order]
    return tuple(outs)
```

```python
import math

import jax
import jax.numpy as jnp
import numpy as np
from jax import lax
from jax.experimental import pallas as pl
from jax.experimental.pallas import tpu as pltpu

D = 1024
DEPTH = 2
MLA_HEADS = 8
QK_NOPE = 128
QK_ROPE = 64
V_DIM = 128
Q_LORA = 384
KV_LORA = 256
ROPE_THETA = 10000.0
SWA_HEADS = 16
SWA_KV_HEADS = 4
SWA_HEAD_DIM = 64
SWA_GROUP = SWA_HEADS // SWA_KV_HEADS
WINDOW = 128
D_FF = 4 * D
EPS = 1e-6
ADAM_LR = 0.001
ADAM_B1 = 0.9
ADAM_B2 = 0.999
ADAM_EPS = 1e-08
ADAM_WD = 0.01
ADAM_STEP = 10

N_CHIPS = 4
N_DEV = 8
LANES = 128
QK_EXT = 256
MLA_SCALE = (QK_NOPE + QK_ROPE) ** -0.5
LOG2E = math.log2(math.e)
LN2 = math.log(2.0)
MLA_QSCALE = MLA_SCALE * LOG2E
ATTN_BLOCK = 2048
ATTN_SUB = 512
MLP_FWD_TILE = (1024, 1024)
MLP_BWD_TILE = (512, 1024)
DW_TOKENS = 4096
DW_TILE = 1024
ROW_TILE = 1024
PROJ_ROWS = 512
FIRST_ROWS = 16
STREAM_ROWS = 512
SWA_SCALE = SWA_HEAD_DIM ** -0.5
NEG = -1e30
MXU_DTYPE = jnp.bfloat16
VMEM_LIMIT = 56 * 1024 * 1024

R_SH1, R_SC1, R_GT1, R_SH2, R_SC2, R_GT2, R_GMIX, R_GMLP = range(8)
R_BO = 6


def _tile(n, pref):
    if n <= pref:
        return n
    for t in range(pref, 7, -1):
        if n % t == 0 and t % 8 == 0:
            return t
    return n


def _dot(a, b):
    return jnp.dot(a, b, preferred_element_type=jnp.float32)


def _dot_nt(a, b):
    return lax.dot_general(a, b, (((1,), (1,)), ((), ())), preferred_element_type=jnp.float32)


def _dot_tn(a, b):
    return lax.dot_general(a, b, (((0,), (0,)), ((), ())), preferred_element_type=jnp.float32)


def _rms(x):
    r = lax.rsqrt(jnp.mean(x * x, axis=-1, keepdims=True) + EPS)
    return x * r, r


def _rms_bwd(dxhat, xhat, r):
    return r * (dxhat - xhat * jnp.mean(dxhat * xhat, axis=-1, keepdims=True))


def _rowsum(v):
    return jnp.sum(v, axis=0, keepdims=True)


def _params(*sem):
    return pltpu.CompilerParams(dimension_semantics=sem, vmem_limit_bytes=VMEM_LIMIT)


def _full(shape):
    nd = len(shape)
    return pl.BlockSpec(shape, lambda *_: (0,) * nd)


def _rows(tm, cols):
    return pl.BlockSpec((tm, cols), lambda i, *_: (i, 0))


def _modulate_bwd(dh, x, vec_ref, r_g, r_sc, r_sh, ps_ref, dres):
    xhat, r = _rms(x)
    g = vec_ref[r_g:r_g + 1, :]
    n = xhat * g
    ps_ref[r_sh:r_sh + 1, :] += _rowsum(dh)
    ps_ref[r_sc:r_sc + 1, :] += _rowsum(dh * n)
    dn = dh * (1.0 + vec_ref[r_sc:r_sc + 1, :])
    ps_ref[r_g:r_g + 1, :] += _rowsum(dn * xhat)
    return dres + _rms_bwd(dn * g, xhat, r)


def _mla_pre(x, vec, wcat, g_q, g_kv, wuq, wukv, cs):
    T = x.shape[0]
    tm = _tile(T, PROJ_ROWS)
    H = MLA_HEADS

    def body(x_ref, vec_ref, wcat_ref, gq_ref, gkv_ref, wuq_ref, wukv_ref, cs_ref, h_ref, z_ref, q_ref, k_ref, v_ref):
        xhat, _ = _rms(x_ref[...])
        h = xhat * vec_ref[R_GMIX:R_GMIX + 1, :] * (1.0 + vec_ref[R_SC1:R_SC1 + 1, :]) + vec_ref[R_SH1:R_SH1 + 1, :]
        hb = h.astype(MXU_DTYPE)
        h_ref[...] = hb
        z = _dot(hb, wcat_ref[...])
        z_ref[...] = z
        cq = (_rms(z[:, :Q_LORA])[0] * gq_ref[...]).astype(MXU_DTYPE)
        ckv = (_rms(z[:, Q_LORA:Q_LORA + KV_LORA])[0] * gkv_ref[...]).astype(MXU_DTYPE)
        cs_t = cs_ref[...]
        t = z[:, Q_LORA + KV_LORA:] * cs_t
        k_rope = (t + pltpu.roll(t, QK_ROPE, axis=1)).astype(MXU_DTYPE)
        low = lax.broadcasted_iota(jnp.int32, (1, LANES), 1) < QK_ROPE
        for hd in range(H):
            qf = _dot(cq, wuq_ref[hd])
            tq = qf[:, QK_NOPE:] * cs_t
            tq = tq + pltpu.roll(tq, QK_ROPE, axis=1)
            q_ref[hd, :, :QK_NOPE] = (qf[:, :QK_NOPE] * MLA_QSCALE).astype(MXU_DTYPE)
            q_ref[hd, :, QK_NOPE:] = jnp.where(low, tq * MLA_QSCALE, 0.0).astype(MXU_DTYPE)
            kvf = _dot(ckv, wukv_ref[hd])
            k_ref[hd, :, :QK_NOPE] = kvf[:, :QK_NOPE].astype(MXU_DTYPE)
            k_ref[hd, :, QK_NOPE:] = k_rope
            v_ref[hd] = kvf[:, QK_NOPE:].astype(MXU_DTYPE)

    zc = wcat.shape[1]
    return pl.pallas_call(
        body, name="mla_pre", grid=(T // tm,),
        in_specs=[_rows(tm, D), _full((8, D)), _full(wcat.shape), _full(g_q.shape), _full(g_kv.shape),
                  _full(wuq.shape), _full(wukv.shape), _rows(tm, LANES)],
        out_specs=[_rows(tm, D), _rows(tm, zc),
                   pl.BlockSpec((H, tm, QK_EXT), lambda i: (0, i, 0)),
                   pl.BlockSpec((H, tm, QK_EXT), lambda i: (0, i, 0)),
                   pl.BlockSpec((H, tm, V_DIM), lambda i: (0, i, 0))],
        out_shape=[jax.ShapeDtypeStruct((T, D), MXU_DTYPE), jax.ShapeDtypeStruct((T, zc), jnp.float32),
                   jax.ShapeDtypeStruct((H, T, QK_EXT), MXU_DTYPE), jax.ShapeDtypeStruct((H, T, QK_EXT), MXU_DTYPE),
                   jax.ShapeDtypeStruct((H, T, V_DIM), MXU_DTYPE)],
        compiler_params=_params("parallel"),
    )(x, vec, wcat, g_q, g_kv, wuq, wukv, cs)


def _mla_attn_fwd(q, k, v):
    H, T, _ = q.shape
    tb = _tile(T, ATTN_BLOCK)
    sub = min(ATTN_SUB, tb)
    ns, nb = tb // sub, T // tb
    pairs = [(i, j) for i in range(nb) for j in range(i + 1)]
    qi_tab = jnp.asarray([i for i, _ in pairs], jnp.int32)
    kj_tab = jnp.asarray([j for _, j in pairs], jnp.int32)

    def body(qi_ref, kj_ref, q_ref, k_ref, v_ref, o_ref, lse_ref, m_sc, l_sc, acc_sc):
        qi, kj = qi_ref[pl.program_id(1)], kj_ref[pl.program_id(1)]

        @pl.when(kj == 0)
        def _():
            m_sc[...] = jnp.full_like(m_sc, NEG)
            l_sc[...] = jnp.zeros_like(l_sc)
            acc_sc[...] = jnp.zeros_like(acc_sc)

        def update(r, kk, masked):
            rows, keys = pl.ds(r * sub, sub), pl.ds(kk * sub, sub)
            s = _dot_nt(q_ref[0, rows, :], k_ref[0, keys, :])
            if masked:
                row = lax.broadcasted_iota(jnp.int32, (sub, sub), 0)
                col = lax.broadcasted_iota(jnp.int32, (sub, sub), 1)
                s = jnp.where(col <= row, s, NEG)
            m_prev = m_sc[rows, :]
            m_new = jnp.maximum(m_prev, jnp.max(s, axis=1, keepdims=True))
            alpha = jnp.exp2(m_prev - m_new)
            p = jnp.exp2(s - jnp.tile(m_new, (1, sub // LANES)))
            l_sc[rows, :] = alpha * l_sc[rows, :] + jnp.sum(p, axis=1, keepdims=True)
            acc_sc[rows, :] = alpha * acc_sc[rows, :] + _dot(p.astype(MXU_DTYPE), v_ref[0, keys, :])
            m_sc[rows, :] = m_new

        @pl.when(kj < qi)
        def _():
            for kk in range(ns):
                for r in range(ns):
                    update(r, kk, False)

        @pl.when(kj == qi)
        def _():
            for kk in range(ns):
                for r in range(kk, ns):
                    update(r, kk, r == kk)
            l = l_sc[...]
            o_ref[...] = (acc_sc[...] / l).astype(o_ref.dtype)
            lse = m_sc[...] + jnp.log2(l)
            pick = (lax.broadcasted_iota(jnp.int32, (8, LANES), 1) == 0).astype(jnp.float32)
            row = lax.dot_general(pick, lse, (((1,), (1,)), ((), ())), precision=lax.Precision.HIGHEST,
                                  preferred_element_type=jnp.float32)
            lse_ref[0] = row[0:1, :]

    q_idx = lambda h, p, qi_ref, kj_ref: (h, qi_ref[p], 0)
    kv_idx = lambda h, p, qi_ref, kj_ref: (h, kj_ref[p], 0)
    return pl.pallas_call(
        body, name="mla_attn_fwd",
        grid_spec=pltpu.PrefetchScalarGridSpec(
            num_scalar_prefetch=2, grid=(H, len(pairs)),
            in_specs=[pl.BlockSpec((1, tb, QK_EXT), q_idx), pl.BlockSpec((1, tb, QK_EXT), kv_idx),
                      pl.BlockSpec((1, tb, V_DIM), kv_idx)],
            out_specs=[pl.BlockSpec((tb, V_DIM), lambda h, p, qi_ref, kj_ref: (qi_ref[p], h)),
                       pl.BlockSpec((1, 1, tb), lambda h, p, qi_ref, kj_ref: (h, 0, qi_ref[p]))],
            scratch_shapes=[pltpu.VMEM((tb, LANES), jnp.float32), pltpu.VMEM((tb, LANES), jnp.float32),
                            pltpu.VMEM((tb, V_DIM), jnp.float32)]),
        out_shape=[jax.ShapeDtypeStruct((T, H * V_DIM), MXU_DTYPE), jax.ShapeDtypeStruct((H, 1, T), jnp.float32)],
        compiler_params=_params("parallel", "arbitrary"),
    )(qi_tab, kj_tab, q, k, v)


def _post_attn(o, x, w_o, bias, vec, o_transposed=False):
    T = x.shape[0]
    tm = _tile(T, ROW_TILE)
    o_spec = pl.BlockSpec((D, tm), lambda i: (0, i)) if o_transposed else _rows(tm, D)

    def body(o_ref, x_ref, w_ref, b_ref, vec_ref, y_ref, xm_ref, h_ref):
        y = (_dot_tn if o_transposed else _dot)(o_ref[...], w_ref[...]) + b_ref[...]
        y_ref[...] = y.astype(y_ref.dtype)
        xm = x_ref[...] + vec_ref[R_GT1:R_GT1 + 1, :] * y
        xm_ref[...] = xm
        xhat, _ = _rms(xm)
        h = xhat * vec_ref[R_GMLP:R_GMLP + 1, :] * (1.0 + vec_ref[R_SC2:R_SC2 + 1, :]) + vec_ref[R_SH2:R_SH2 + 1, :]
        h_ref[...] = h.astype(h_ref.dtype)

    return pl.pallas_call(
        body, name="post_attn", grid=(T // tm,),
        in_specs=[o_spec, _rows(tm, D), _full((D, D)), _full((1, D)), _full((8, D))],
        out_specs=[_rows(tm, D), _rows(tm, D), _rows(tm, D)],
        out_shape=[jax.ShapeDtypeStruct((T, D), MXU_DTYPE), jax.ShapeDtypeStruct((T, D), jnp.float32),
                   jax.ShapeDtypeStruct((T, D), MXU_DTYPE)],
        compiler_params=_params("parallel"),
    )(o, x, w_o, bias, vec)


def _ff_specs(tf):
    per = D_FF // N_CHIPS // tf
    w1 = pl.BlockSpec((None, D, tf), lambda i, f: (f // per, 0, f % per))
    w2 = pl.BlockSpec((None, tf, D), lambda i, f: (f // per, f % per, 0))
    return w1, w2


def _mlp_fwd(h2, w1, w2, xm, vec):
    T = h2.shape[0]
    tm = _tile(T, MLP_FWD_TILE[0])
    tf = _tile(D_FF // N_CHIPS, MLP_FWD_TILE[1])
    nf = D_FF // tf
    w1_spec, w2_spec = _ff_specs(tf)

    def body(h_ref, w1_ref, w2_ref, xm_ref, vec_ref, a_ref, y_ref, xo_ref, acc):
        f = pl.program_id(1)

        @pl.when(f == 0)
        def _():
            acc[...] = jnp.zeros_like(acc)

        u = jnp.maximum(_dot(h_ref[...], w1_ref[...]), 0.0)
        ab = (u * u).astype(MXU_DTYPE)
        a_ref[...] = ab
        acc[...] += _dot(ab, w2_ref[...])

        @pl.when(f == nf - 1)
        def _():
            y = acc[...]
            y_ref[...] = y.astype(y_ref.dtype)
            xo_ref[...] = xm_ref[...] + vec_ref[R_GT2:R_GT2 + 1, :] * y

    return pl.pallas_call(
        body, name="mlp_fwd", grid=(T // tm, nf),
        in_specs=[_rows(tm, D), w1_spec, w2_spec, _rows(tm, D), _full((8, D))],
        out_specs=[pl.BlockSpec((tm, tf), lambda i, f: (i, f)), _rows(tm, D), _rows(tm, D)],
        out_shape=[jax.ShapeDtypeStruct((T, D_FF), MXU_DTYPE), jax.ShapeDtypeStruct((T, D), MXU_DTYPE),
                   jax.ShapeDtypeStruct((T, D), jnp.float32)],
        scratch_shapes=[pltpu.VMEM((tm, D), jnp.float32)],
        compiler_params=_params("parallel", "arbitrary"),
    )(h2, w1, w2, xm, vec)


def _swa_pre(x, vec, w_qkv, b_qkv):
    T = x.shape[0]
    tm = _tile(T, PROJ_ROWS)
    nq = SWA_HEADS * SWA_HEAD_DIM
    nk = SWA_KV_HEADS * SWA_HEAD_DIM
    wq_t, w_kv = w_qkv[:, :nq].T, w_qkv[:, nq:]
    bq_col, b_kv = b_qkv[:, :nq].reshape(nq, 1), b_qkv[:, nq:]

    def body(x_ref, vec_ref, wq_ref, wkv_ref, bq_ref, bkv_ref, h_ref, qt_ref, k_ref, v_ref):
        xhat, _ = _rms(x_ref[...])
        h = xhat * vec_ref[R_GMIX:R_GMIX + 1, :] * (1.0 + vec_ref[R_SC1:R_SC1 + 1, :]) + vec_ref[R_SH1:R_SH1 + 1, :]
        hb = h.astype(MXU_DTYPE)
        h_ref[...] = hb
        qt_ref[...] = ((_dot_nt(wq_ref[...], hb) + bq_ref[...]) * SWA_SCALE).astype(MXU_DTYPE)
        kv = _dot(hb, wkv_ref[...]) + bkv_ref[...]
        k_ref[...] = kv[:, :nk].astype(MXU_DTYPE)
        v_ref[...] = kv[:, nk:].astype(MXU_DTYPE)

    return pl.pallas_call(
        body, name="swa_pre", grid=(T // tm,),
        in_specs=[_rows(tm, D), _full((8, D)), _full(wq_t.shape), _full(w_kv.shape), _full(bq_col.shape),
                  _full(b_kv.shape)],
        out_specs=[_rows(tm, D), pl.BlockSpec((nq, tm), lambda i: (0, i)), _rows(tm, nk), _rows(tm, nk)],
        out_shape=[jax.ShapeDtypeStruct((T, D), MXU_DTYPE), jax.ShapeDtypeStruct((nq, T), MXU_DTYPE),
                   jax.ShapeDtypeStruct((T, nk), MXU_DTYPE), jax.ShapeDtypeStruct((T, nk), MXU_DTYPE)],
        compiler_params=_params("parallel"),
    )(x, vec, wq_t, w_kv, bq_col, b_kv)


def _swa_bias():
    W = WINDOW
    slopes = 2.0 ** (-8.0 * np.arange(1, SWA_HEADS + 1) / SWA_HEADS)
    j, i = np.arange(W)[:, None], np.arange(W)[None, :]
    dist = np.where(j > i, W + i - j, i - j)
    bias = -slopes[:, None, None] * dist[None].astype(np.float64)
    bias = bias.reshape(SWA_KV_HEADS, SWA_GROUP, W, W).transpose(0, 2, 1, 3)
    return jnp.asarray(bias.reshape(SWA_KV_HEADS, W, SWA_GROUP * W), jnp.float32)


def _swa_fold_mask():
    W, G = WINDOW, SWA_GROUP
    j = lax.broadcasted_iota(jnp.int32, (W, G * W), 0)
    i = lax.broadcasted_iota(jnp.int32, (W, G * W), 1) & (W - 1)
    return j > i


def _swa_fold(band, up):
    return jnp.where(up, band[:WINDOW], band[WINDOW:])


def _swa_unfold(folded, up):
    zero = jnp.zeros_like(folded)
    return jnp.concatenate([jnp.where(up, folded, zero), jnp.where(up, zero, folded)], axis=0)


SWA_STEP_BLOCKS = 4


def _swa_blocks(T):
    nb = T // WINDOW
    return next(b for b in (SWA_STEP_BLOCKS, 2, 1) if nb % b == 0)


def _swa_views(b, qt_ref, kp_ref, kc_ref):
    W = WINDOW
    prev = kp_ref if b == 0 else kc_ref.at[pl.ds((b - 1) * W, W), :]
    return qt_ref.at[:, pl.ds(b * W, W)], prev, kc_ref.at[pl.ds(b * W, W), :]


def _swa_probs(has_prev, up, kh, qt_ref, kp_ref, kc_ref, bias_ref, sink_ref):
    W, Dh, G = WINDOW, SWA_HEAD_DIM, SWA_GROUP
    qt = jnp.concatenate([qt_ref[(kh * G + g) * Dh:(kh * G + g + 1) * Dh, :] for g in range(G)], axis=1)
    kb = jnp.concatenate([kp_ref[:, kh * Dh:(kh + 1) * Dh], kc_ref[:, kh * Dh:(kh + 1) * Dh]], axis=0)
    s = _swa_fold(_dot(kb, qt), up) + bias_ref[kh]
    if has_prev is not True:
        s = jnp.where(up & jnp.logical_not(has_prev), NEG, s)
    sink = sink_ref[kh]
    m = jnp.maximum(jnp.max(s, axis=0, keepdims=True), sink)
    p = jnp.exp(s - m)
    p_sink = jnp.exp(sink - m)
    inv = 1.0 / (jnp.sum(p, axis=0, keepdims=True) + p_sink)
    return qt, kb, p * inv, p_sink * inv


def _swa_attn_fwd(qt, k, v, bias, sink_rows):
    T = qt.shape[1]
    W, Dh, G, Hk = WINDOW, SWA_HEAD_DIM, SWA_GROUP, SWA_KV_HEADS
    nk = Hk * Dh

    nb = _swa_blocks(T)

    def body(qt_ref, kp_ref, kc_ref, vp_ref, vc_ref, bias_ref, sink_ref, ot_ref):
        n = pl.program_id(0)
        up = _swa_fold_mask()
        for b in range(nb):
            q_b, kp_b, kc_b = _swa_views(b, qt_ref, kp_ref, kc_ref)
            _, vp_b, vc_b = _swa_views(b, qt_ref, vp_ref, vc_ref)
            for kh in range(Hk):
                _, _, pn, _ = _swa_probs(True if b else n > 0, up, kh, q_b, kp_b, kc_b, bias_ref, sink_ref)
                vb = jnp.concatenate([vp_b[:, kh * Dh:(kh + 1) * Dh], vc_b[:, kh * Dh:(kh + 1) * Dh]], axis=0)
                ot = _dot_tn(vb, _swa_unfold(pn, up).astype(MXU_DTYPE))
                for g in range(G):
                    rows = pl.ds((kh * G + g) * Dh, Dh)
                    ot_ref[rows, pl.ds(b * W, W)] = ot[:, g * W:(g + 1) * W].astype(ot_ref.dtype)

    prev = lambda n: (jnp.maximum(n * nb - 1, 0), 0)
    cur = lambda n: (n, 0)
    col = lambda n: (0, n)
    return pl.pallas_call(
        body, name="swa_attn_fwd", grid=(T // (nb * W),),
        in_specs=[pl.BlockSpec((D, nb * W), col), pl.BlockSpec((W, nk), prev), pl.BlockSpec((nb * W, nk), cur),
                  pl.BlockSpec((W, nk), prev), pl.BlockSpec((nb * W, nk), cur), _full(bias.shape),
                  _full(sink_rows.shape)],
        out_specs=pl.BlockSpec((D, nb * W), col),
        out_shape=jax.ShapeDtypeStruct((D, T), MXU_DTYPE),
        compiler_params=_params("parallel"),
    )(qt, k, k, v, v, bias, sink_rows)


def _final_loss(x, tgt, g):
    T = x.shape[0]
    tm = _tile(T, ROW_TILE)

    def body(x_ref, t_ref, g_ref, loss_ref, dx_ref, dg_ref):
        @pl.when(pl.program_id(0) == 0)
        def _():
            loss_ref[...] = jnp.zeros_like(loss_ref)
            dg_ref[...] = jnp.zeros_like(dg_ref)

        xhat, r = _rms(x_ref[...])
        gv = g_ref[...]
        e = xhat * gv - t_ref[...]
        loss_ref[...] += 0.5 * jnp.sum(jnp.mean(e * e, axis=-1, keepdims=True), axis=0, keepdims=True)
        dy = e * (1.0 / D)
        dg_ref[...] += _rowsum(dy * xhat)
        dx_ref[...] = _rms_bwd(dy * gv, xhat, r)

    return pl.pallas_call(
        body, name="final_loss", grid=(T // tm,),
        in_specs=[_rows(tm, D), _rows(tm, D), _full((1, D))],
        out_specs=[_full((8, LANES)), _rows(tm, D), _full((1, D))],
        out_shape=[jax.ShapeDtypeStruct((8, LANES), jnp.float32), jax.ShapeDtypeStruct((T, D), jnp.float32),
                   jax.ShapeDtypeStruct((1, D), jnp.float32)],
        compiler_params=_params("arbitrary"),
    )(x, tgt, g)


def _mlp_bwd(dxo, y2, a, w1, w2, xm, vec):
    T = dxo.shape[0]
    tm = _tile(T, MLP_BWD_TILE[0])
    tf = _tile(D_FF // N_CHIPS, MLP_BWD_TILE[1])
    nf = D_FF // tf
    w1_spec, w2_spec = _ff_specs(tf)

    def body(dxo_ref, y_ref, a_ref, w1_ref, w2_ref, xm_ref, vec_ref, du_ref, dy_ref, dxm_ref, ps_ref, dyb, acc):
        i, f = pl.program_id(0), pl.program_id(1)

        @pl.when((i == 0) & (f == 0))
        def _():
            ps_ref[...] = jnp.zeros_like(ps_ref)

        @pl.when(f == 0)
        def _():
            dxo_t = dxo_ref[...]
            d = (dxo_t * vec_ref[R_GT2:R_GT2 + 1, :]).astype(MXU_DTYPE)
            dyb[...] = d
            dy_ref[...] = d
            acc[...] = jnp.zeros_like(acc)
            ps_ref[R_GT2:R_GT2 + 1, :] += _rowsum(dxo_t * y_ref[...].astype(jnp.float32))

        da = _dot_nt(dyb[...], w2_ref[...])
        dub = (da * (2.0 * jnp.sqrt(a_ref[...].astype(jnp.float32)))).astype(MXU_DTYPE)
        du_ref[...] = dub
        acc[...] += _dot_nt(dub, w1_ref[...])

        @pl.when(f == nf - 1)
        def _():
            dxm_ref[...] = _modulate_bwd(acc[...], xm_ref[...], vec_ref, R_GMLP, R_SC2, R_SH2, ps_ref, dxo_ref[...])

    return pl.pallas_call(
        body, name="mlp_bwd", grid=(T // tm, nf),
        in_specs=[_rows(tm, D), _rows(tm, D), pl.BlockSpec((tm, tf), lambda i, f: (i, f)), w1_spec, w2_spec,
                  _rows(tm, D), _full((8, D))],
        out_specs=[pl.BlockSpec((tm, tf), lambda i, f: (i, f)), _rows(tm, D), _rows(tm, D), _full((8, D))],
        out_shape=[jax.ShapeDtypeStruct((T, D_FF), MXU_DTYPE), jax.ShapeDtypeStruct((T, D), MXU_DTYPE),
                   jax.ShapeDtypeStruct((T, D), jnp.float32), jax.ShapeDtypeStruct((8, D), jnp.float32)],
        scratch_shapes=[pltpu.VMEM((tm, D), MXU_DTYPE), pltpu.VMEM((tm, D), jnp.float32)],
        compiler_params=_params("arbitrary", "arbitrary"),
    )(dxo, y2, a, w1, w2, xm, vec)


def _mm_tn(a, g, name, split=None, layers=1, layer=0, into=None, a_transposed=False):
    K, T = a.shape if a_transposed else a.shape[::-1]
    N = g.shape[1]
    kq = K // N_CHIPS if split == "rows" else K
    nq = N // N_CHIPS if split == "cols" else N
    bk, bn, bt = _tile(kq, DW_TILE), _tile(nq, DW_TILE), _tile(T, DW_TOKENS)
    if nq % bn or bn % LANES:
        bn = nq
    kper, nper = kq // bk, nq // bn

    def body(*refs):
        a_ref, g_ref, o_ref = refs[0], refs[1], refs[-1]

        @pl.when(pl.program_id(2) == 0)
        def _():
            o_ref[...] = jnp.zeros_like(o_ref)

        o_ref[...] += (_dot if a_transposed else _dot_tn)(a_ref[...], g_ref[...])

    a_spec = pl.BlockSpec((bk, bt), lambda k, n, t: (k, t)) if a_transposed else pl.BlockSpec((bt, bk), lambda k, n, t: (t, k))
    in_specs = [a_spec, pl.BlockSpec((bt, bn), lambda k, n, t: (t, n))]
    args = [a, g]
    aliases = {}
    if split is None:
        out_spec = pl.BlockSpec((bk, bn), lambda k, n, t: (k, n))
        out_shape = jax.ShapeDtypeStruct((K, N), jnp.float32)
    else:
        if split == "cols":
            idx = lambda k, n, t: (n // nper, layer, k, n % nper)
        else:
            idx = lambda k, n, t: (k // kper, layer, k % kper, n)
        out_spec = pl.BlockSpec((None, None, bk, bn), idx)
        out_shape = jax.ShapeDtypeStruct((N_CHIPS, layers, kq, nq), jnp.float32)
        if into is not None:
            in_specs.append(pl.BlockSpec(memory_space=pl.ANY))
            args.append(into)
            aliases = {2: 0}
    return pl.pallas_call(
        body, name=name, grid=(K // bk, N // bn, T // bt), in_specs=in_specs, out_specs=out_spec, out_shape=out_shape,
        input_output_aliases=aliases, compiler_params=_params("parallel", "parallel", "arbitrary"),
    )(*args)


def _attn_out_bwd(dxm, y1, o, w_o, vec, with_delta):
    T = dxm.shape[0]
    tm = _tile(T, ROW_TILE)
    H = MLA_HEADS

    def body(dxm_ref, y_ref, w_ref, vec_ref, *refs):
        o_ref = refs[0] if with_delta else None
        dy_ref, do_ref, ps_ref, *delta_ref = refs[1:] if with_delta else refs

        @pl.when(pl.program_id(0) == 0)
        def _():
            ps_ref[...] = jnp.zeros_like(ps_ref)

        dxm_t = dxm_ref[...]
        dy = dxm_t * vec_ref[R_GT1:R_GT1 + 1, :]
        ps_ref[R_GT1:R_GT1 + 1, :] += _rowsum(dxm_t * y_ref[...].astype(jnp.float32))
        ps_ref[R_BO:R_BO + 1, :] += _rowsum(dy)
        dyb = dy.astype(MXU_DTYPE)
        dy_ref[...] = dyb
        if not with_delta:
            do_ref[...] = _dot_nt(w_ref[...], dyb).astype(do_ref.dtype)
        else:
            do = _dot_nt(dyb, w_ref[...])
            do_ref[...] = do.astype(do_ref.dtype)
            of = o_ref[...].astype(jnp.float32)
            ones = jnp.ones((8, V_DIM), jnp.float32)
            for hd in range(H):
                sl = slice(hd * V_DIM, (hd + 1) * V_DIM)
                d = lax.dot_general(ones, do[:, sl] * of[:, sl], (((1,), (1,)), ((), ())),
                                    precision=lax.Precision.HIGHEST, preferred_element_type=jnp.float32)
                delta_ref[0][hd] = d[0:1, :]

    out_specs = [_rows(tm, D), _rows(tm, D), _full((8, D))]
    out_shape = [jax.ShapeDtypeStruct((T, D), MXU_DTYPE), jax.ShapeDtypeStruct((T, D), MXU_DTYPE),
                 jax.ShapeDtypeStruct((8, D), jnp.float32)]
    if not with_delta:
        out_specs[1] = pl.BlockSpec((D, tm), lambda i: (0, i))
        out_shape[1] = jax.ShapeDtypeStruct((D, T), MXU_DTYPE)
    if with_delta:
        out_specs.append(pl.BlockSpec((H, 1, tm), lambda i: (0, 0, i)))
        out_shape.append(jax.ShapeDtypeStruct((H, 1, T), jnp.float32))
    return pl.pallas_call(
        body, name="attn_out_bwd_mla" if with_delta else "attn_out_bwd_swa", grid=(T // tm,),
        in_specs=[_rows(tm, D), _rows(tm, D), _full((D, D)), _full((8, D))] + ([_rows(tm, D)] if with_delta else []),
        out_specs=out_specs, out_shape=out_shape,
        compiler_params=_params("arbitrary"),
    )(dxm, y1, w_o, vec, *([o] if with_delta else []))


def _mla_attn_bwd(q, k, v, do, lse, delta):
    H, T, _ = q.shape
    tb = _tile(T, ATTN_BLOCK)
    sub = min(ATTN_SUB, tb)
    ns, nb = tb // sub, T // tb

    pairs = [(j, i) for j in range(nb) for i in range(j, nb)]
    kj_tab = jnp.asarray([j for j, _ in pairs], jnp.int32)
    qi_tab = jnp.asarray([i for _, i in pairs], jnp.int32)

    def body(kj_ref, qi_ref, q_ref, k_ref, v_ref, do_ref, lse_ref, dl_ref, dq_ref, dk_ref, dv_ref, dk_acc, dv_acc):
        j, i = kj_ref[pl.program_id(1)], qi_ref[pl.program_id(1)]

        @pl.when((j == 0) & (i == 0))
        def _():
            dq_ref[...] = jnp.zeros_like(dq_ref)

        def update(kk, r, masked):
            keys, rows = pl.ds(kk * sub, sub), pl.ds(r * sub, sub)
            kb, qb, dob = k_ref[0, keys, :], q_ref[0, rows, :], do_ref[rows, :]
            st = _dot_nt(kb, qb)
            if masked:
                row = lax.broadcasted_iota(jnp.int32, (sub, sub), 0)
                col = lax.broadcasted_iota(jnp.int32, (sub, sub), 1)
                st = jnp.where(row <= col, st, NEG)
            pt = jnp.exp2(st - lse_ref[0, :, rows])
            dv_acc[keys, :] += _dot(pt.astype(MXU_DTYPE), dob)
            dpt = _dot_nt(v_ref[0, keys, :], dob)
            dst = (pt * (dpt - dl_ref[0, :, rows])).astype(MXU_DTYPE)
            dk_acc[keys, :] += _dot(dst, qb)
            q_rows = pl.ds(pl.multiple_of(i * tb + r * sub, sub), sub)
            dq_ref[0, q_rows, :] += _dot_tn(dst, kb)

        @pl.when(i == j)
        def _():
            dk_acc[...] = jnp.zeros_like(dk_acc)
            dv_acc[...] = jnp.zeros_like(dv_acc)
            for r in range(ns):
                for kk in range(r + 1):
                    update(kk, r, kk == r)

        @pl.when(i > j)
        def _():
            for r in range(ns):
                for kk in range(ns):
                    update(kk, r, False)

        @pl.when(i == nb - 1)
        def _():
            dk_ref[0] = (dk_acc[...] * LN2).astype(dk_ref.dtype)
            dv_ref[0] = dv_acc[...].astype(dv_ref.dtype)

    q_idx = lambda h, p, kj_ref, qi_ref: (h, qi_ref[p], 0)
    kv_idx = lambda h, p, kj_ref, qi_ref: (h, kj_ref[p], 0)
    stat_idx = lambda h, p, kj_ref, qi_ref: (h, 0, qi_ref[p])
    return pl.pallas_call(
        body, name="mla_attn_bwd",
        grid_spec=pltpu.PrefetchScalarGridSpec(
            num_scalar_prefetch=2, grid=(H, len(pairs)),
            in_specs=[pl.BlockSpec((1, tb, QK_EXT), q_idx), pl.BlockSpec((1, tb, QK_EXT), kv_idx),
                      pl.BlockSpec((1, tb, V_DIM), kv_idx),
                      pl.BlockSpec((tb, V_DIM), lambda h, p, kj_ref, qi_ref: (qi_ref[p], h)),
                      pl.BlockSpec((1, 1, tb), stat_idx), pl.BlockSpec((1, 1, tb), stat_idx)],
            out_specs=[pl.BlockSpec((1, T, QK_EXT), lambda h, p, kj_ref, qi_ref: (h, 0, 0)),
                       pl.BlockSpec((1, tb, QK_EXT), kv_idx), pl.BlockSpec((1, tb, V_DIM), kv_idx)],
            scratch_shapes=[pltpu.VMEM((tb, QK_EXT), jnp.float32), pltpu.VMEM((tb, V_DIM), jnp.float32)]),
        out_shape=[jax.ShapeDtypeStruct((H, T, QK_EXT), jnp.float32), jax.ShapeDtypeStruct((H, T, QK_EXT), MXU_DTYPE),
                   jax.ShapeDtypeStruct((H, T, V_DIM), MXU_DTYPE)],
        compiler_params=_params("parallel", "arbitrary"),
    )(kj_tab, qi_tab, q, k, v, do, lse, delta)


def _mla_pre_bwd(x, dxm, vec, hb, z, dq, dk, dv, cs, wcat, g_q, g_kv, wuq, wukv):
    T = x.shape[0]
    tm = _tile(T, PROJ_ROWS)
    H = MLA_HEADS
    zc = wcat.shape[1]

    def body(x_ref, dxm_ref, vec_ref, h_ref, z_ref, dq_ref, dk_ref, dv_ref, cs_ref, wcat_ref, gq_ref, gkv_ref,
             wuq_ref, wukv_ref, dx_ref, ps_ref, dgq_ref, dgkv_ref, dwcat_ref, dwuq_ref, dwukv_ref):
        @pl.when(pl.program_id(0) == 0)
        def _():
            for ref in (ps_ref, dgq_ref, dgkv_ref, dwcat_ref, dwuq_ref, dwukv_ref):
                ref[...] = jnp.zeros_like(ref)

        z = z_ref[...]
        cs_t = cs_ref[...]
        cqhat, rq = _rms(z[:, :Q_LORA])
        ckhat, rk = _rms(z[:, Q_LORA:Q_LORA + KV_LORA])
        gq, gkv = gq_ref[...], gkv_ref[...]
        cq = (cqhat * gq).astype(MXU_DTYPE)
        ckv = (ckhat * gkv).astype(MXU_DTYPE)
        dcq = jnp.zeros((tm, Q_LORA), jnp.float32)
        dckv = jnp.zeros((tm, KV_LORA), jnp.float32)
        dkr = jnp.zeros((tm, LANES), jnp.float32)
        for hd in range(H):
            dqh = dq_ref[hd] * MLA_SCALE
            gqh = jnp.concatenate([dqh[:, :QK_NOPE], dqh[:, QK_NOPE:] * cs_t], axis=1).astype(MXU_DTYPE)
            dcq += _dot_nt(gqh, wuq_ref[hd])
            dwuq_ref[hd] += _dot_tn(cq, gqh)
            dkh = dk_ref[hd]
            gkvh = jnp.concatenate([dkh[:, :QK_NOPE], dv_ref[hd]], axis=1)
            dckv += _dot_nt(gkvh, wukv_ref[hd])
            dwukv_ref[hd] += _dot_tn(ckv, gkvh)
            dkr += dkh[:, QK_NOPE:].astype(jnp.float32)
        dgq_ref[...] += _rowsum(dcq * cqhat)
        dgkv_ref[...] += _rowsum(dckv * ckhat)
        dcq_pre = _rms_bwd(dcq * gq, cqhat, rq)
        dckv_pre = _rms_bwd(dckv * gkv, ckhat, rk)
        dkr2 = (dkr + pltpu.roll(dkr, QK_ROPE, axis=1)) * cs_t
        dz = jnp.concatenate([dcq_pre, dckv_pre, dkr2], axis=1).astype(MXU_DTYPE)
        dwcat_ref[...] += _dot_tn(h_ref[...], dz)
        dh = _dot_nt(dz, wcat_ref[...])
        dx_ref[...] = _modulate_bwd(dh, x_ref[...], vec_ref, R_GMIX, R_SC1, R_SH1, ps_ref, dxm_ref[...])

    hblk = lambda w: pl.BlockSpec((H, tm, w), lambda i: (0, i, 0))
    return pl.pallas_call(
        body, name="mla_pre_bwd", grid=(T // tm,),
        in_specs=[_rows(tm, D), _rows(tm, D), _full((8, D)), _rows(tm, D), _rows(tm, zc), hblk(QK_EXT), hblk(QK_EXT),
                  hblk(V_DIM), _rows(tm, LANES), _full(wcat.shape), _full(g_q.shape), _full(g_kv.shape),
                  _full(wuq.shape), _full(wukv.shape)],
        out_specs=[_rows(tm, D), _full((8, D)), _full(g_q.shape), _full(g_kv.shape), _full(wcat.shape),
                   _full(wuq.shape), _full(wukv.shape)],
        out_shape=[jax.ShapeDtypeStruct((T, D), jnp.float32), jax.ShapeDtypeStruct((8, D), jnp.float32),
                   jax.ShapeDtypeStruct(g_q.shape, jnp.float32), jax.ShapeDtypeStruct(g_kv.shape, jnp.float32),
                   jax.ShapeDtypeStruct(wcat.shape, jnp.float32), jax.ShapeDtypeStruct(wuq.shape, jnp.float32),
                   jax.ShapeDtypeStruct(wukv.shape, jnp.float32)],
        compiler_params=_params("arbitrary"),
    )(x, dxm, vec, hb, z, dq, dk, dv, cs, wcat, g_q, g_kv, wuq, wukv)


def _swa_attn_bwd(qt, k, v, dot_, bias, sink_rows):
    T = qt.shape[1]
    W, Dh, G, Hk = WINDOW, SWA_HEAD_DIM, SWA_GROUP, SWA_KV_HEADS
    nk = Hk * Dh
    nb = _swa_blocks(T)

    def body(qt_ref, kp_ref, kc_ref, vp_ref, vc_ref, dot_ref, bias_ref, sink_ref, dqt_ref, dk_ref, dv_ref, dsink_ref):
        n = pl.program_id(0)

        @pl.when(n == 0)
        def _():
            dk_ref[...] = jnp.zeros_like(dk_ref)
            dv_ref[...] = jnp.zeros_like(dv_ref)
            dsink_ref[...] = jnp.zeros_like(dsink_ref)

        def add_rows(first_row, dkb_part, dvb_part):
            rows = pl.ds(pl.multiple_of(first_row, W), W)
            dk_ref[rows, :] += dkb_part
            dv_ref[rows, :] += dvb_part

        up = _swa_fold_mask()
        for b in range(nb):
            q_b, kp_b, kc_b = _swa_views(b, qt_ref, kp_ref, kc_ref)
            do_b, vp_b, vc_b = _swa_views(b, dot_ref, vp_ref, vc_ref)
            dks, dvs = [], []
            for kh in range(Hk):
                qt, kb, pn, p_sink = _swa_probs(True if b else n > 0, up, kh, q_b, kp_b, kc_b, bias_ref, sink_ref)
                vb = jnp.concatenate([vp_b[:, kh * Dh:(kh + 1) * Dh], vc_b[:, kh * Dh:(kh + 1) * Dh]], axis=0)
                dot_h = jnp.concatenate([do_b[(kh * G + g) * Dh:(kh * G + g + 1) * Dh, :] for g in range(G)], axis=1)
                dp = _swa_fold(_dot(vb, dot_h), up)
                delta = jnp.sum(pn * dp, axis=0, keepdims=True)
                dsb = _swa_unfold(pn * (dp - delta), up).astype(MXU_DTYPE)
                dsink_ref[kh] += -p_sink * delta
                dqt = _dot_tn(kb, dsb) * SWA_SCALE
                for g in range(G):
                    dqt_ref[pl.ds((kh * G + g) * Dh, Dh), pl.ds(b * W, W)] = dqt[:, g * W:(g + 1) * W]
                dks.append(_dot_nt(dsb, qt))
                dvs.append(_dot_nt(_swa_unfold(pn, up).astype(MXU_DTYPE), dot_h))
            dkb = jnp.concatenate(dks, axis=1)
            dvb = jnp.concatenate(dvs, axis=1)
            add_rows((n * nb + b) * W, dkb[W:], dvb[W:])
            if b:
                add_rows((n * nb + b - 1) * W, dkb[:W], dvb[:W])
            else:
                @pl.when(n > 0)
                def _():
                    add_rows((n * nb - 1) * W, dkb[:W], dvb[:W])

    prev = lambda n: (jnp.maximum(n * nb - 1, 0), 0)
    cur = lambda n: (n, 0)
    col = lambda n: (0, n)
    return pl.pallas_call(
        body, name="swa_attn_bwd", grid=(T // (nb * W),),
        in_specs=[pl.BlockSpec((D, nb * W), col), pl.BlockSpec((W, nk), prev), pl.BlockSpec((nb * W, nk), cur),
                  pl.BlockSpec((W, nk), prev), pl.BlockSpec((nb * W, nk), cur), pl.BlockSpec((D, nb * W), col),
                  _full(bias.shape), _full(sink_rows.shape)],
        out_specs=[pl.BlockSpec((D, nb * W), col), _full((T, nk)), _full((T, nk)), _full(sink_rows.shape)],
        out_shape=[jax.ShapeDtypeStruct((D, T), jnp.float32), jax.ShapeDtypeStruct((T, nk), jnp.float32),
                   jax.ShapeDtypeStruct((T, nk), jnp.float32), jax.ShapeDtypeStruct(sink_rows.shape, jnp.float32)],
        compiler_params=_params("arbitrary"),
    )(qt, k, k, v, v, dot_, bias, sink_rows)


def _swa_pre_bwd(x, dxm, vec, dq_t, dk, dv, w_qkv):
    T = x.shape[0]
    tm = _tile(T, PROJ_ROWS)
    nq = SWA_HEADS * SWA_HEAD_DIM
    nk = SWA_KV_HEADS * SWA_HEAD_DIM
    nqkv = nq + 2 * nk

    def body(x_ref, dxm_ref, vec_ref, dq_ref, dk_ref, dv_ref, w_ref, dx_ref, dqkv_ref, ps_ref, db_ref):
        @pl.when(pl.program_id(0) == 0)
        def _():
            ps_ref[...] = jnp.zeros_like(ps_ref)
            db_ref[...] = jnp.zeros_like(db_ref)

        dqkv = jnp.concatenate([dq_ref[...].T, dk_ref[...], dv_ref[...]], axis=1)
        db_ref[...] += _rowsum(dqkv)
        dqkv_b = dqkv.astype(MXU_DTYPE)
        dqkv_ref[...] = dqkv_b
        dh = _dot_nt(dqkv_b, w_ref[...])
        dx_ref[...] = _modulate_bwd(dh, x_ref[...], vec_ref, R_GMIX, R_SC1, R_SH1, ps_ref, dxm_ref[...])

    return pl.pallas_call(
        body, name="swa_pre_bwd", grid=(T // tm,),
        in_specs=[_rows(tm, D), _rows(tm, D), _full((8, D)), pl.BlockSpec((nq, tm), lambda i: (0, i)), _rows(tm, nk),
                  _rows(tm, nk), _full(w_qkv.shape)],
        out_specs=[_rows(tm, D), _rows(tm, nqkv), _full((8, D)), _full((1, nqkv))],
        out_shape=[jax.ShapeDtypeStruct((T, D), jnp.float32), jax.ShapeDtypeStruct((T, nqkv), MXU_DTYPE),
                   jax.ShapeDtypeStruct((8, D), jnp.float32), jax.ShapeDtypeStruct((1, nqkv), jnp.float32)],
        compiler_params=_params("arbitrary"),
    )(x, dxm, vec, dq_t, dk, dv, w_qkv)


def _rot_cols(w):
    half = QK_ROPE // 2
    return jnp.concatenate([-w[..., half:], w[..., :half]], axis=-1)


def _unrot_grad(d_rope, d_rot):
    half = QK_ROPE // 2
    return d_rope + jnp.concatenate([d_rot[..., half:], -d_rot[..., :half]], axis=-1)


def _rope_table(positions):
    half = QK_ROPE // 2
    inv_freq = ROPE_THETA ** (-jnp.arange(half, dtype=jnp.float32) / half)
    ang = positions.astype(jnp.float32)[:, None] * inv_freq
    cos, sin = jnp.cos(ang), jnp.sin(ang)
    return jnp.concatenate([cos, cos, sin, sin], axis=1)


def _sequence_step(x, tgt, positions, vecs, g_q, g_kv, sinks, g_final, wts, late_weights, on_late_grads, on_late_landed):
    H = MLA_HEADS
    cs = _rope_table(positions)
    w_dkv = wts["mla_w_dkv"]
    wcat = jnp.concatenate([wts["mla_w_dq"], w_dkv, _rot_cols(w_dkv[:, KV_LORA:])], axis=1)
    uq = wts["mla_w_uq"].reshape(Q_LORA, H, QK_NOPE + QK_ROPE)
    wuq = jnp.concatenate([uq, _rot_cols(uq[..., QK_NOPE:])], axis=-1).transpose(1, 0, 2)
    wukv = wts["mla_w_ukv"].reshape(KV_LORA, H, QK_NOPE + V_DIM).transpose(1, 0, 2)
    zero_bias = jnp.zeros((1, D), jnp.float32)
    bias = _swa_bias()
    sink_rows = jnp.broadcast_to(sinks.reshape(SWA_KV_HEADS, 1, SWA_GROUP, 1),
                                 (SWA_KV_HEADS, 1, SWA_GROUP, WINDOW)).reshape(SWA_KV_HEADS, 1, SWA_GROUP * WINDOW)

    h1a, z, q, k, v = _mla_pre(x, vecs[0], wcat, g_q, g_kv, wuq, wukv, cs)
    o_a, lse = _mla_attn_fwd(q, k, v)
    wts = {**wts, **late_weights(o_a)}
    y1a, xm_a, h2a = _post_attn(o_a, x, wts["mla_w_o"], zero_bias, vecs[0])
    a_a, y2a, x1 = _mlp_fwd(h2a, wts["w_ff1"][0], wts["w_ff2"][0], xm_a, vecs[0])

    h1b, qs_t, ks, vs = _swa_pre(x1, vecs[1], wts["swa_w_qkv"], wts["swa_b_qkv"])
    o_bt = _swa_attn_fwd(qs_t, ks, vs, bias, sink_rows)
    y1b, xm_b, h2b = _post_attn(o_bt, x1, wts["swa_w_o"], wts["swa_b_o"], vecs[1], o_transposed=True)
    a_b, y2b, x2 = _mlp_fwd(h2b, wts["w_ff1"][1], wts["w_ff2"][1], xm_b, vecs[1])

    loss8, dx2, dg_final = _final_loss(x2, tgt, g_final.reshape(1, D))

    du_b, dy2b, dxm_b, ps_mlp_b = _mlp_bwd(dx2, y2b, a_b, wts["w_ff1"][1], wts["w_ff2"][1], xm_b, vecs[1])
    g_ff2 = _mm_tn(a_b, dy2b, "dw_ff2_l1", "rows", DEPTH, 1)
    g_ff1 = _mm_tn(h2b, du_b, "dw_ff1_l1", "cols", DEPTH, 1)
    dy1b, do_bt, ps_out_b = _attn_out_bwd(dxm_b, y1b, None, wts["swa_w_o"], vecs[1], False)
    g_swa_o = _mm_tn(o_bt, dy1b, "dw_o_swa", a_transposed=True)
    dqs_t, dks, dvs, dsinks = _swa_attn_bwd(qs_t, ks, vs, do_bt, bias, sink_rows)
    dx1, dqkv, ps_pre_b, g_swa_bqkv = _swa_pre_bwd(x1, dxm_b, vecs[1], dqs_t, dks, dvs, wts["swa_w_qkv"])
    g_swa_qkv = _mm_tn(h1b, dqkv, "dw_qkv", "cols")

    du_a, dy2a, dxm_a, ps_mlp_a = _mlp_bwd(dx1, y2a, a_a, wts["w_ff1"][0], wts["w_ff2"][0], xm_a, vecs[0])
    g_ff2 = _mm_tn(a_a, dy2a, "dw_ff2_l0", "rows", DEPTH, 0, g_ff2)
    g_ff1 = _mm_tn(h2a, du_a, "dw_ff1_l0", "cols", DEPTH, 0, g_ff1)
    rows4 = lambda g: g.reshape(N_CHIPS, g.shape[0] // N_CHIPS, g.shape[1])
    token = on_late_grads({
        "swa_w_qkv": g_swa_qkv.reshape(N_CHIPS, D, -1), "swa_w_o": rows4(g_swa_o),
        "w_ff1": g_ff1.reshape(N_CHIPS, DEPTH * D, -1), "w_ff2": g_ff2.reshape(N_CHIPS, -1, D)})
    dy1a, do_a, ps_out_a, delta = _attn_out_bwd(dxm_a, y1a, o_a, wts["mla_w_o"], vecs[0] + token[0, 0], True)
    g_mla_o = _mm_tn(o_a, dy1a, "dw_o_mla")
    token = on_late_landed(g_mla_o)
    dq, dk, dv = _mla_attn_bwd(q, k, v, do_a, lse, delta + token[0, 0])
    dx0, ps_pre_a, dg_q, dg_kv, dwcat, dwuq, dwukv = _mla_pre_bwd(
        x, dxm_a, vecs[0], h1a, z, dq, dk, dv, cs, wcat, g_q, g_kv, wuq, wukv)

    c0, c1, c2 = Q_LORA, Q_LORA + KV_LORA, Q_LORA + KV_LORA + QK_ROPE
    g_dq = dwcat[:, :c0]
    g_dkv = jnp.concatenate([dwcat[:, c0:c1], _unrot_grad(dwcat[:, c1:c2], dwcat[:, c2:])], axis=1)
    e0 = QK_NOPE + QK_ROPE
    g_uq = jnp.concatenate([dwuq[..., :QK_NOPE], _unrot_grad(dwuq[..., QK_NOPE:e0], dwuq[..., e0:])], axis=-1)
    per = H // N_CHIPS
    g_uq = g_uq.reshape(N_CHIPS, per, Q_LORA, e0).transpose(0, 2, 1, 3).reshape(N_CHIPS, Q_LORA, per * e0)
    g_ukv = dwukv.reshape(N_CHIPS, per, KV_LORA, QK_NOPE + V_DIM).transpose(0, 2, 1, 3)
    g_ukv = g_ukv.reshape(N_CHIPS, KV_LORA, per * (QK_NOPE + V_DIM))

    def dmod(ps_pre, ps_out, ps_mlp):
        return jnp.concatenate([ps_pre[R_SH1:R_SC1 + 1], ps_out[R_GT1:R_GT1 + 1], ps_mlp[R_SH2:R_GT2 + 1]], axis=0)

    grads = {"mla_w_dq": rows4(g_dq), "mla_w_uq": g_uq, "mla_w_dkv": rows4(g_dkv), "mla_w_ukv": g_ukv,
             "mla_w_o": rows4(g_mla_o)}
    small = {
        "dmod": jnp.stack([dmod(ps_pre_a, ps_out_a, ps_mlp_a), dmod(ps_pre_b, ps_out_b, ps_mlp_b)]).reshape(DEPTH, 6 * D),
        "g_mix": jnp.stack([ps_pre_a[R_GMIX], ps_pre_b[R_GMIX]]),
        "g_mlp": jnp.stack([ps_mlp_a[R_GMLP], ps_mlp_b[R_GMLP]]),
        "mla_g_q": dg_q, "mla_g_kv": dg_kv, "swa_sinks": jnp.sum(dsinks.reshape(SWA_HEADS, WINDOW), axis=1).reshape(1, SWA_HEADS),
        "swa_b_qkv": g_swa_bqkv, "swa_b_o": ps_out_b[R_BO:R_BO + 1],
        "g_final": dg_final.reshape(D), "loss": loss8[0, 0],
    }
    return dx0, grads, small


SHARDED = {
    "mla_w_dq": (1, D // N_CHIPS, Q_LORA),
    "mla_w_uq": (1, Q_LORA, MLA_HEADS * (QK_NOPE + QK_ROPE) // N_CHIPS),
    "mla_w_dkv": (1, D // N_CHIPS, KV_LORA + QK_ROPE),
    "mla_w_ukv": (1, KV_LORA, MLA_HEADS * (QK_NOPE + V_DIM) // N_CHIPS),
    "mla_w_o": (1, MLA_HEADS * V_DIM // N_CHIPS, D),
    "swa_w_qkv": (1, D, (SWA_HEADS + 2 * SWA_KV_HEADS) * SWA_HEAD_DIM // N_CHIPS),
    "swa_w_o": (1, SWA_HEADS * SWA_HEAD_DIM // N_CHIPS, D),
    "w_ff1": (DEPTH, D, D_FF // N_CHIPS),
    "w_ff2": (DEPTH, D_FF // N_CHIPS, D),
}
COL_SPLIT = ("mla_w_uq", "mla_w_ukv", "swa_w_qkv")
BIASES = {"swa_b_qkv": (SWA_HEADS + 2 * SWA_KV_HEADS) * SWA_HEAD_DIM, "swa_b_o": D}


def _view2d(name):
    shape = SHARDED[name]
    return math.prod(shape[:-1]), shape[-1]


SMALL = {"b_ada": (DEPTH, 6 * D), "g_mix": (DEPTH, D), "g_mlp": (DEPTH, D), "mla_g_q": (1, Q_LORA),
         "mla_g_kv": (1, KV_LORA), "swa_sinks": (1, SWA_HEADS), "g_final": (D,), "loss": (),
         "swa_b_qkv": (1, BIASES["swa_b_qkv"]), "swa_b_o": (1, BIASES["swa_b_o"])}
SMALL_ROWS = 192
DMA_ROWS = 256


SLOT_ROWS = 8


def _small_slots():
    slots, off = {}, 0
    for name, shape in SMALL.items():
        n = max(math.prod(shape), 1)
        slots[name] = (off, n)
        off += -(-n // (SLOT_ROWS * LANES)) * SLOT_ROWS * LANES
    assert off <= SMALL_ROWS * LANES
    return slots


def _pack_small(vals):
    parts, end = [], 0
    for name, (off, n) in _small_slots().items():
        pad = -(-n // (SLOT_ROWS * LANES)) * SLOT_ROWS * LANES - n
        v = vals[name].astype(jnp.float32).reshape(-1) if name in vals else jnp.zeros((n,), jnp.float32)
        parts += [v, jnp.zeros((pad,), jnp.float32)]
        end = off + n + pad
    parts.append(jnp.zeros((SMALL_ROWS * LANES - end,), jnp.float32))
    return jnp.concatenate(parts).reshape(SMALL_ROWS, LANES)


def _from_slot(name, rows):
    n = max(math.prod(SMALL[name]), 1)
    return rows.reshape(-1)[:n].reshape(SMALL[name])


def _pieces(rows):
    return [(off, min(DMA_ROWS, rows - off)) for off in range(0, rows, DMA_ROWS)]


HBM = pl.BlockSpec(memory_space=pltpu.HBM)
MESH = pl.DeviceIdType.MESH


def _place():
    x, y, c = lax.axis_index("x"), lax.axis_index("y"), lax.axis_index("c")
    chips = [(1 - x, y), (x, 1 - y), (1 - x, 1 - y)]
    return x, y, c, chips


def _all_gather(block):
    m_per, n = block.shape

    def body(x_ref, out_ref, send_sems, recv_sems, local_sem):
        x, y, c, chips = _place()
        me, sibling = (x, y, c), (x, y, 1 - c)

        def rows(px, py, pc):
            return out_ref.at[pl.ds((4 * px + 2 * py + pc) * m_per, m_per), :]

        def copy(k, blk, to, src=None):
            return pltpu.make_async_remote_copy(
                src_ref=rows(*blk) if src is None else src, dst_ref=rows(*blk),
                send_sem=send_sems.at[k], recv_sem=recv_sems.at[k], device_id=to, device_id_type=MESH)

        mine = pltpu.make_async_copy(x_ref, rows(*me), local_sem)
        mine.start()
        first = [copy(0, me, sibling, src=x_ref)]
        first += [copy(1 + j, me, (*chip, c), src=x_ref) for j, chip in enumerate(chips)]
        for cp in first:
            cp.start()
        passed = [copy(4 + j, (*chip, c), sibling) for j, chip in enumerate(chips)]
        for j, chip in enumerate(chips):
            copy(1 + j, (*chip, c), me).wait_recv()
            passed[j].start()
        copy(0, sibling, me).wait_recv()
        for j, chip in enumerate(chips):
            copy(4 + j, (*chip, 1 - c), me).wait_recv()
        for cp in first + passed:
            cp.wait_send()
        mine.wait()

    out = pl.pallas_call(
        body, name="all_gather_small",
        out_shape=jax.ShapeDtypeStruct((N_DEV * m_per, n), block.dtype),
        in_specs=[pl.BlockSpec(memory_space=pltpu.VMEM)],
        out_specs=pl.BlockSpec(memory_space=pltpu.VMEM),
        scratch_shapes=[pltpu.SemaphoreType.DMA((7,)), pltpu.SemaphoreType.DMA((7,)), pltpu.SemaphoreType.DMA],
    )(block)
    return out.reshape(N_DEV, m_per, n)


def _weight_gather(shards):
    nt = len(shards)

    def body(*refs):
        w_refs, out_refs = refs[:nt], refs[nt:2 * nt]
        send_sems, recv_sems = refs[2 * nt:]
        x, y, c, chips = _place()
        sibling = (x, y, 1 - c)

        def slab(t, px, py, half):
            rh = shards[t].shape[0] // 2
            return out_refs[t].at[2 * px + py, pl.ds(half * rh, rh), :]

        def copy(t, k, src, dst, to):
            return pltpu.make_async_remote_copy(src_ref=src, dst_ref=dst, send_sem=send_sems.at[6 * t + k],
                                                recv_sem=recv_sems.at[6 * t + k], device_id=to, device_id_type=MESH)

        first = []
        for t in range(nt):
            rh = shards[t].shape[0] // 2
            first += [copy(t, j, w_refs[t].at[pl.ds(c * rh, rh), :], slab(t, x, y, c), (*chip, c))
                      for j, chip in enumerate(chips)]
        for cp in first:
            cp.start()
        passed = []
        for t in range(nt):
            for j, chip in enumerate(chips):
                copy(t, j, slab(t, *chip, c), slab(t, *chip, c), (*chip, c)).wait_recv()
                rh = shards[t].shape[0] // 2
                for off, n in _pieces(rh):
                    piece = out_refs[t].at[2 * chip[0] + chip[1], pl.ds(c * rh + off, n), :]
                    copy(t, 3 + j, piece, piece, sibling).start()
                passed.append(copy(t, 3 + j, slab(t, *chip, c), slab(t, *chip, c), sibling))
        for t in range(nt):
            for j, chip in enumerate(chips):
                copy(t, 3 + j, slab(t, *chip, 1 - c), slab(t, *chip, 1 - c), sibling).wait_recv()
        for cp in first + passed:
            cp.wait_send()

    return pl.pallas_call(
        body, name="weight_gather",
        out_shape=[jax.ShapeDtypeStruct((N_CHIPS,) + s.shape, s.dtype) for s in shards],
        in_specs=[HBM] * nt, out_specs=[HBM] * nt,
        scratch_shapes=[pltpu.SemaphoreType.DMA((6 * nt,)), pltpu.SemaphoreType.DMA((6 * nt,))],
    )(*shards)


SEM = pl.BlockSpec(memory_space=pltpu.SEMAPHORE)
ANY = pl.BlockSpec(memory_space=pl.ANY)
SPLIT_COPY = pltpu.SideEffectType.DATAFLOW_SIDE_EFFECTING


def _late_copies(w_refs, land_refs, send_sems, recv_sems):
    x, y, c, chips = _place()
    return [pltpu.make_async_remote_copy(
        src_ref=w_refs[t], dst_ref=land_refs[t].at[2 * x + y], send_sem=send_sems.at[3 * t + j],
        recv_sem=recv_sems.at[3 * t + j], device_id=(cx, cy, c), device_id_type=MESH)
        for t in range(len(w_refs)) for j, (cx, cy) in enumerate(chips)], chips


def _late_gather_start(shards, after):
    nt, na = len(shards), len(after)

    def body(*refs):
        w_refs, land_refs = refs[:nt], refs[nt:2 * nt]
        send_sems, recv_sems, token = refs[2 * nt + na], refs[2 * nt + na + 1], refs[-1]
        copies, _ = _late_copies(w_refs, land_refs, send_sems, recv_sems)
        for cp in copies:
            cp.start()
        token[...] = jnp.zeros_like(token)

    hbm = lambda a: pltpu.with_memory_space_constraint(a, pltpu.HBM)
    lands = [lax.empty((N_CHIPS,) + s.shape, s.dtype) for s in shards]
    outs = pl.pallas_call(
        body, name="late_gather_start",
        out_shape=(pltpu.SemaphoreType.DMA((3 * nt,)), pltpu.SemaphoreType.DMA((3 * nt,)),
                   *[pltpu.HBM(s.shape, s.dtype) for s in shards], *[pltpu.HBM(l.shape, l.dtype) for l in lands],
                   jax.ShapeDtypeStruct((8, LANES), jnp.float32)),
        in_specs=[HBM] * (2 * nt) + [ANY] * na,
        out_specs=(SEM, SEM, *([HBM] * (2 * nt)), pl.BlockSpec(memory_space=pltpu.VMEM)),
        input_output_aliases={i: 2 + i for i in range(2 * nt)},
        compiler_params=pltpu.CompilerParams(has_side_effects=SPLIT_COPY),
    )(*[hbm(s) for s in shards], *[hbm(l) for l in lands], *after)
    return outs[0], outs[1], list(outs[2:2 + nt]), list(outs[2 + nt:2 + 2 * nt]), outs[-1]


def _late_gather_wait(send_sems, recv_sems, shards, lands, after):
    nt = len(shards)

    def body(*refs):
        w_refs, land_refs = refs[:nt], refs[nt:2 * nt]
        s_sems, r_sems = refs[2 * nt], refs[2 * nt + 1]
        x, y, c, chips = _place()
        for t in range(nt):
            for j, (cx, cy) in enumerate(chips):
                cp = pltpu.make_async_remote_copy(
                    src_ref=w_refs[t], dst_ref=land_refs[t].at[2 * cx + cy], send_sem=s_sems.at[3 * t + j],
                    recv_sem=r_sems.at[3 * t + j], device_id=(cx, cy, c), device_id_type=MESH)
                cp.wait_send()
                cp.wait_recv()

    outs = pl.pallas_call(
        body, name="late_gather_wait",
        out_shape=(*[pltpu.HBM(s.shape, s.dtype) for s in shards], *[pltpu.HBM(l.shape, l.dtype) for l in lands]),
        in_specs=[HBM] * (2 * nt) + [SEM, SEM, ANY], out_specs=tuple([HBM] * (2 * nt)),
        input_output_aliases={i: i for i in range(2 * nt)},
        compiler_params=pltpu.CompilerParams(has_side_effects=SPLIT_COPY),
    )(*shards, *lands, send_sems, recv_sems, after)
    return list(outs[nt:])


def _grad_pair_in(grads, behind):
    nt = len(grads)

    def body(*refs):
        g_refs, got_refs = refs[:nt], refs[nt + 1:2 * nt + 1]
        send_sems, recv_sems = refs[2 * nt + 1:]
        x, y, c, _ = _place()
        sibling = (x, y, 1 - c)

        def copy(t, src, dst):
            return pltpu.make_async_remote_copy(src_ref=src, dst_ref=dst, send_sem=send_sems.at[t],
                                                recv_sem=recv_sems.at[t], device_id=sibling, device_id_type=MESH)

        for t in range(nt):
            rh = grads[t].shape[1] // 2
            for p in range(N_CHIPS):
                for off, n in _pieces(rh):
                    copy(t, g_refs[t].at[p, pl.ds((1 - c) * rh + off, n), :], got_refs[t].at[p, pl.ds(off, n), :]).start()
        for t in range(nt):
            rh = grads[t].shape[1] // 2
            copy(t, g_refs[t].at[:, pl.ds((1 - c) * rh, rh), :], got_refs[t]).wait()

    return pl.pallas_call(
        body, name="grad_pair_in",
        out_shape=[jax.ShapeDtypeStruct((N_CHIPS, g.shape[1] // 2, g.shape[2]), g.dtype) for g in grads],
        in_specs=[HBM] * nt + [ANY], out_specs=[HBM] * nt,
        scratch_shapes=[pltpu.SemaphoreType.DMA((nt,)), pltpu.SemaphoreType.DMA((nt,))],
    )(*grads, behind)


def _pair_in_start(grads):
    nt = len(grads)

    def body(*refs):
        g_refs, land_refs = refs[:nt], refs[nt:2 * nt]
        send_sems, recv_sems, token = refs[2 * nt], refs[2 * nt + 1], refs[-1]
        x, y, c, _ = _place()
        for t in range(nt):
            rh = grads[t].shape[1] // 2
            for p in range(N_CHIPS):
                for off, n in _pieces(rh):
                    pltpu.make_async_remote_copy(
                        src_ref=g_refs[t].at[p, pl.ds((1 - c) * rh + off, n), :], dst_ref=land_refs[t].at[p, pl.ds(off, n), :],
                        send_sem=send_sems.at[t], recv_sem=recv_sems.at[t], device_id=(x, y, 1 - c),
                        device_id_type=MESH).start()
        token[...] = jnp.zeros_like(token)

    hbm = lambda a: pltpu.with_memory_space_constraint(a, pltpu.HBM)
    lands = [lax.empty((N_CHIPS, g.shape[1] // 2, g.shape[2]), g.dtype) for g in grads]
    outs = pl.pallas_call(
        body, name="grad_pair_in_start",
        out_shape=(pltpu.SemaphoreType.DMA((nt,)), pltpu.SemaphoreType.DMA((nt,)),
                   *[pltpu.HBM(g.shape, g.dtype) for g in grads], *[pltpu.HBM(l.shape, l.dtype) for l in lands],
                   jax.ShapeDtypeStruct((8, LANES), jnp.float32)),
        in_specs=[HBM] * (2 * nt),
        out_specs=(SEM, SEM, *([HBM] * (2 * nt)), pl.BlockSpec(memory_space=pltpu.VMEM)),
        input_output_aliases={i: 2 + i for i in range(2 * nt)},
        compiler_params=pltpu.CompilerParams(has_side_effects=SPLIT_COPY),
    )(*[hbm(g) for g in grads], *[hbm(l) for l in lands])
    return outs[0], outs[1], list(outs[2:2 + nt]), list(outs[2 + nt:2 + 2 * nt]), outs[-1]


def _pair_in_wait(send_sems, recv_sems, grads, lands, after):
    nt = len(grads)

    def body(*refs):
        g_refs, land_refs = refs[:nt], refs[nt:2 * nt]
        s_sems, r_sems = refs[2 * nt], refs[2 * nt + 1]
        x, y, c, _ = _place()
        for t in range(nt):
            rh = grads[t].shape[1] // 2
            cp = pltpu.make_async_remote_copy(
                src_ref=g_refs[t].at[:, pl.ds((1 - c) * rh, rh), :], dst_ref=land_refs[t], send_sem=s_sems.at[t],
                recv_sem=r_sems.at[t], device_id=(x, y, 1 - c), device_id_type=MESH)
            cp.wait_send()
            cp.wait_recv()

    outs = pl.pallas_call(
        body, name="grad_pair_in_wait",
        out_shape=(*[pltpu.HBM(g.shape, g.dtype) for g in grads], *[pltpu.HBM(l.shape, l.dtype) for l in lands]),
        in_specs=[HBM] * (2 * nt) + [SEM, SEM, ANY], out_specs=tuple([HBM] * (2 * nt)),
        input_output_aliases={i: i for i in range(2 * nt)},
        compiler_params=pltpu.CompilerParams(has_side_effects=SPLIT_COPY),
    )(*grads, *lands, send_sems, recv_sems, after)
    return list(outs[:nt]), list(outs[nt:])


def _pair_sum(g, got, core, name):
    _, rows, cols = g.shape
    rh = rows // 2
    tr = _tile(rh, 512)
    nb = rh // tr

    def body(c_ref, g_ref, got_ref, s16_ref):
        s16_ref[...] = (g_ref[...] + got_ref[...]).astype(s16_ref.dtype)

    blk = pl.BlockSpec((None, tr, cols), lambda p, i, c_ref: (p, i, 0))
    return pl.pallas_call(
        body, name=name,
        grid_spec=pltpu.PrefetchScalarGridSpec(
            num_scalar_prefetch=1, grid=(N_CHIPS, nb),
            in_specs=[pl.BlockSpec((None, tr, cols), lambda p, i, c_ref: (p, c_ref[0] * nb + i, 0)), blk],
            out_specs=blk),
        out_shape=jax.ShapeDtypeStruct((N_CHIPS, rh, cols), jnp.bfloat16),
        compiler_params=_params("parallel", "parallel"),
    )(core, g, got)


def _exchange_start(parts, name):
    nt = len(parts)

    def body(*refs):
        a_refs, land_refs = refs[:nt], refs[nt:2 * nt]
        send_sems, recv_sems, token = refs[2 * nt], refs[2 * nt + 1], refs[-1]
        x, y, c, chips = _place()
        for t in range(nt):
            for j, (cx, cy) in enumerate(chips):
                pltpu.make_async_remote_copy(
                    src_ref=a_refs[t].at[2 * cx + cy], dst_ref=land_refs[t].at[j], send_sem=send_sems.at[3 * t + j],
                    recv_sem=recv_sems.at[3 * t + j], device_id=(cx, cy, c), device_id_type=MESH).start()
        token[...] = jnp.zeros_like(token)

    hbm = lambda a: pltpu.with_memory_space_constraint(a, pltpu.HBM)
    lands = [lax.empty((N_CHIPS - 1,) + a.shape[1:], a.dtype) for a in parts]
    outs = pl.pallas_call(
        body, name=name,
        out_shape=(pltpu.SemaphoreType.DMA((3 * nt,)), pltpu.SemaphoreType.DMA((3 * nt,)),
                   *[pltpu.HBM(a.shape, a.dtype) for a in parts], *[pltpu.HBM(l.shape, l.dtype) for l in lands],
                   jax.ShapeDtypeStruct((8, LANES), jnp.float32)),
        in_specs=[HBM] * (2 * nt),
        out_specs=(SEM, SEM, *([HBM] * (2 * nt)), pl.BlockSpec(memory_space=pltpu.VMEM)),
        input_output_aliases={i: 2 + i for i in range(2 * nt)},
        compiler_params=pltpu.CompilerParams(has_side_effects=SPLIT_COPY),
    )(*[hbm(a) for a in parts], *[hbm(l) for l in lands])
    return outs[0], outs[1], list(outs[2:2 + nt]), list(outs[2 + nt:2 + 2 * nt]), outs[-1]


def _exchange_wait(send_sems, recv_sems, parts, lands, after, name):
    nt = len(parts)

    def body(*refs):
        a_refs, land_refs = refs[:nt], refs[nt:2 * nt]
        s_sems, r_sems = refs[2 * nt], refs[2 * nt + 1]
        x, y, c, chips = _place()
        for t in range(nt):
            for j, (cx, cy) in enumerate(chips):
                cp = pltpu.make_async_remote_copy(
                    src_ref=a_refs[t].at[2 * cx + cy], dst_ref=land_refs[t].at[j], send_sem=s_sems.at[3 * t + j],
                    recv_sem=r_sems.at[3 * t + j], device_id=(cx, cy, c), device_id_type=MESH)
                cp.wait_send()
                cp.wait_recv()

    outs = pl.pallas_call(
        body, name=name,
        out_shape=(*[pltpu.HBM(a.shape, a.dtype) for a in parts], *[pltpu.HBM(l.shape, l.dtype) for l in lands]),
        in_specs=[HBM] * (2 * nt) + [SEM, SEM, ANY], out_specs=tuple([HBM] * (2 * nt)),
        input_output_aliases={i: i for i in range(2 * nt)},
        compiler_params=pltpu.CompilerParams(has_side_effects=SPLIT_COPY),
    )(*parts, *lands, send_sems, recv_sems, after)
    return list(outs[nt:])


def _chip_sum(g, got_pair, got_chips, core, chip, name):
    _, rh, cols = got_pair.shape
    tr = _tile(rh, STREAM_ROWS)
    nb = rh // tr

    def body(c_ref, p_ref, g_ref, pair_ref, chips_ref, o_ref):
        acc = g_ref[...] + pair_ref[...]
        for j in range(N_CHIPS - 1):
            acc = acc + chips_ref[j].astype(jnp.float32)
        o_ref[...] = acc

    return pl.pallas_call(
        body, name=name,
        grid_spec=pltpu.PrefetchScalarGridSpec(
            num_scalar_prefetch=2, grid=(nb,),
            in_specs=[pl.BlockSpec((None, tr, cols), lambda i, c_ref, p_ref: (p_ref[0], c_ref[0] * nb + i, 0)),
                      pl.BlockSpec((None, tr, cols), lambda i, c_ref, p_ref: (p_ref[0], i, 0)),
                      pl.BlockSpec((N_CHIPS - 1, tr, cols), lambda i, c_ref, p_ref: (0, i, 0))],
            out_specs=pl.BlockSpec((tr, cols), lambda i, c_ref, p_ref: (i, 0))),
        out_shape=jax.ShapeDtypeStruct((rh, cols), jnp.float32),
        compiler_params=_params("parallel"),
    )(core, chip, g, got_pair, got_chips)


def _pair_out_start(halves):
    nt = len(halves)

    def body(*refs):
        h_refs, land_refs = refs[:nt], refs[nt:2 * nt]
        send_sems, recv_sems, token = refs[2 * nt], refs[2 * nt + 1], refs[-1]
        x, y, c, _ = _place()
        for t in range(nt):
            for off, n in _pieces(halves[t].shape[0]):
                pltpu.make_async_remote_copy(
                    src_ref=h_refs[t].at[pl.ds(off, n), :], dst_ref=land_refs[t].at[pl.ds(off, n), :],
                    send_sem=send_sems.at[t], recv_sem=recv_sems.at[t], device_id=(x, y, 1 - c),
                    device_id_type=MESH).start()
        token[...] = jnp.zeros_like(token)

    hbm = lambda a: pltpu.with_memory_space_constraint(a, pltpu.HBM)
    lands = [lax.empty(h.shape, h.dtype) for h in halves]
    outs = pl.pallas_call(
        body, name="grad_pair_out_start",
        out_shape=(pltpu.SemaphoreType.DMA((nt,)), pltpu.SemaphoreType.DMA((nt,)),
                   *[pltpu.HBM(h.shape, h.dtype) for h in halves], *[pltpu.HBM(l.shape, l.dtype) for l in lands],
                   jax.ShapeDtypeStruct((8, LANES), jnp.float32)),
        in_specs=[HBM] * (2 * nt),
        out_specs=(SEM, SEM, *([HBM] * (2 * nt)), pl.BlockSpec(memory_space=pltpu.VMEM)),
        input_output_aliases={i: 2 + i for i in range(2 * nt)},
        compiler_params=pltpu.CompilerParams(has_side_effects=SPLIT_COPY),
    )(*[hbm(h) for h in halves], *[hbm(l) for l in lands])
    return outs[0], outs[1], list(outs[2:2 + nt]), list(outs[2 + nt:2 + 2 * nt]), outs[-1]


def _pair_out_wait(send_sems, recv_sems, halves, lands, after):
    nt = len(halves)

    def body(*refs):
        h_refs, land_refs = refs[:nt], refs[nt:2 * nt]
        s_sems, r_sems = refs[2 * nt], refs[2 * nt + 1]
        x, y, c, _ = _place()
        for t in range(nt):
            cp = pltpu.make_async_remote_copy(
                src_ref=h_refs[t], dst_ref=land_refs[t], send_sem=s_sems.at[t], recv_sem=r_sems.at[t],
                device_id=(x, y, 1 - c), device_id_type=MESH)
            cp.wait_send()
            cp.wait_recv()

    outs = pl.pallas_call(
        body, name="grad_pair_out_wait",
        out_shape=(*[pltpu.HBM(h.shape, h.dtype) for h in halves], *[pltpu.HBM(l.shape, l.dtype) for l in lands]),
        in_specs=[HBM] * (2 * nt) + [SEM, SEM, ANY], out_specs=tuple([HBM] * (2 * nt)),
        input_output_aliases={i: i for i in range(2 * nt)},
        compiler_params=pltpu.CompilerParams(has_side_effects=SPLIT_COPY),
    )(*halves, *lands, send_sems, recv_sems, after)
    return list(outs[:nt]), list(outs[nt:])


def _grad_pair_out(halves):
    nt = len(halves)

    def body(*refs):
        h_refs, got_refs = refs[:nt], refs[nt:2 * nt]
        send_sems, recv_sems = refs[2 * nt:]
        x, y, c, _ = _place()
        sibling = (x, y, 1 - c)

        def copy(t, src, dst):
            return pltpu.make_async_remote_copy(src_ref=src, dst_ref=dst, send_sem=send_sems.at[t],
                                                recv_sem=recv_sems.at[t], device_id=sibling, device_id_type=MESH)

        for t in range(nt):
            for off, n in _pieces(halves[t].shape[0]):
                copy(t, h_refs[t].at[pl.ds(off, n), :], got_refs[t].at[pl.ds(off, n), :]).start()
        for t in range(nt):
            copy(t, h_refs[t], got_refs[t]).wait()

    return pl.pallas_call(
        body, name="grad_pair_out",
        out_shape=[jax.ShapeDtypeStruct(h.shape, h.dtype) for h in halves],
        in_specs=[HBM] * nt, out_specs=[HBM] * nt,
        scratch_shapes=[pltpu.SemaphoreType.DMA((nt,)), pltpu.SemaphoreType.DMA((nt,))],
    )(*halves)


def _ada_part(c_all, w_ada):
    L, _, ncol = w_ada.shape
    tn = _tile(ncol, 512)

    def body(c_ref, w_ref, cond_ref, part_ref):
        cv = c_ref[...]
        cond = cv * jax.nn.sigmoid(cv)
        cond_ref[...] = cond
        part_ref[0] = jnp.dot(cond, w_ref[0], precision=lax.Precision.HIGHEST, preferred_element_type=jnp.float32)

    return pl.pallas_call(
        body, name="ada_part", grid=(L, ncol // tn),
        in_specs=[_full((N_DEV, D)), pl.BlockSpec((1, D, tn), lambda l, j: (l, 0, j))],
        out_specs=[_full((N_DEV, D)), pl.BlockSpec((1, N_DEV, tn), lambda l, j: (l, 0, j))],
        out_shape=[jax.ShapeDtypeStruct((N_DEV, D), jnp.float32), jax.ShapeDtypeStruct((L, N_DEV, ncol), jnp.float32)],
        compiler_params=_params("arbitrary", "arbitrary"),
    )(c_all, w_ada)


def _adamw_math(w, g, m, v):
    m = ADAM_B1 * m + (1.0 - ADAM_B1) * g
    v = ADAM_B2 * v + (1.0 - ADAM_B2) * jnp.square(g)
    m_hat = m / (1.0 - ADAM_B1 ** ADAM_STEP)
    v_hat = v / (1.0 - ADAM_B2 ** ADAM_STEP)
    delta = -ADAM_LR * (m_hat / (jnp.sqrt(v_hat) + ADAM_EPS) + ADAM_WD * w)
    return delta, m, v


def _adamw(w, g, m, v, name):
    shape = w.shape
    cols = shape[-1]
    rows = math.prod(shape[:-1])
    tr = _tile(rows, 512)
    two_d = lambda t: t.reshape(rows, cols)

    def body(w_ref, g_ref, m_ref, v_ref, d_ref, mo_ref, vo_ref):
        d_ref[...], mo_ref[...], vo_ref[...] = _adamw_math(w_ref[...], g_ref[...], m_ref[...], v_ref[...])

    out = jax.ShapeDtypeStruct((rows, cols), jnp.float32)
    outs = pl.pallas_call(
        body, name=name, grid=(rows // tr,), in_specs=[_rows(tr, cols)] * 4, out_specs=[_rows(tr, cols)] * 3,
        out_shape=[out, out, out], compiler_params=_params("parallel"),
    )(two_d(w), two_d(g), two_d(m), two_d(v))
    return [t.reshape(shape) for t in outs]


def _adamw_halves(w, mine, got, m, v, core, name):
    shape = w.shape
    cols = shape[-1]
    rows = math.prod(shape[:-1])
    rh = rows // 2
    tr = _tile(rh, STREAM_ROWS)
    nbh = rh // tr
    two_d = lambda t: t.reshape(rows, cols)

    def body(c_ref, w_ref, a_ref, b_ref, m_ref, v_ref, g_ref, d_ref, mo_ref, vo_ref):
        g = jnp.where(pl.program_id(0) // nbh == c_ref[0], a_ref[...], b_ref[...])
        g_ref[...] = g
        d_ref[...], mo_ref[...], vo_ref[...] = _adamw_math(w_ref[...], g, m_ref[...], v_ref[...])

    row = pl.BlockSpec((tr, cols), lambda i, c_ref: (i, 0))

    def half(keep):
        return pl.BlockSpec((tr, cols), lambda i, c_ref: (jnp.where((i // nbh == c_ref[0]) == keep, i % nbh, 0), 0))

    out = jax.ShapeDtypeStruct((rows, cols), jnp.float32)
    outs = pl.pallas_call(
        body, name=name,
        grid_spec=pltpu.PrefetchScalarGridSpec(
            num_scalar_prefetch=1, grid=(rows // tr,),
            in_specs=[row, half(True), half(False), row, row], out_specs=[row] * 4),
        out_shape=[out] * 4, compiler_params=_params("arbitrary"),
    )(core, two_d(w), mine, got, two_d(m), two_d(v))
    return [t.reshape(shape) for t in outs]


def _ada_grad_adamw(cond_t, dm, w, m, v):
    L, _, ncol = w.shape
    tn = _tile(ncol, STREAM_ROWS)

    def body(ct_ref, dm_ref, w_ref, m_ref, v_ref, g_ref, d_ref, mo_ref, vo_ref):
        g = ct_ref[:, 0:1] * dm_ref[0, 0:1, :]
        for b in range(1, N_DEV):
            g = g + ct_ref[:, b:b + 1] * dm_ref[0, b:b + 1, :]
        g_ref[0] = g
        d_ref[0], mo_ref[0], vo_ref[0] = _adamw_math(w_ref[0], g, m_ref[0], v_ref[0])

    wblk = pl.BlockSpec((1, D, tn), lambda l, j: (l, 0, j))
    out = jax.ShapeDtypeStruct(w.shape, jnp.float32)
    return pl.pallas_call(
        body, name="ada_grad_adamw", grid=(L, ncol // tn),
        in_specs=[_full((D, N_DEV)), pl.BlockSpec((1, N_DEV, tn), lambda l, j: (l, 0, j)), wblk, wblk, wblk],
        out_specs=[wblk] * 4, out_shape=[out] * 4, compiler_params=_params("parallel", "parallel"),
    )(cond_t, dm, w, m, v)


def _small_adamw(gathered, w, m, v):
    slots = _small_slots()
    rows = {name: (off // LANES, -(-n // LANES)) for name, (off, n) in slots.items()}
    kinds = {name: 1 if name == "loss" or name in BIASES else 4 for name in slots}

    def body(ga_ref, w_ref, m_ref, v_ref, *out_refs):
        g = ga_ref[0]
        for dev in range(1, N_DEV):
            g = g + ga_ref[dev]
        d, mo, vo = _adamw_math(w_ref[...], g, m_ref[...], v_ref[...])
        k = 0
        for name, (r0, nr) in rows.items():
            for src in (g, d, mo, vo)[:kinds[name]]:
                out_refs[k][...] = src[r0:r0 + nr, :]
                k += 1

    out_shape = [jax.ShapeDtypeStruct((rows[name][1], LANES), jnp.float32) for name in slots for _ in range(kinds[name])]
    flat = pl.pallas_call(
        body, name="small_adamw", out_shape=out_shape,
        in_specs=[pl.BlockSpec(memory_space=pltpu.VMEM)] * 4,
        out_specs=[pl.BlockSpec(memory_space=pltpu.VMEM)] * len(out_shape),
    )(gathered, w, m, v)
    out, k = {}, 0
    for name in slots:
        out[name] = [_from_slot(name, t) for t in flat[k:k + kinds[name]]]
        k += kinds[name]
    return out


def _one_hot_pick(arr, index, axis):
    n = arr.shape[axis]
    shape = [1] * arr.ndim
    shape[axis] = n
    hot = (jnp.arange(n) == index).astype(arr.dtype).reshape(shape)
    return jnp.sum(arr * hot, axis=axis)


def kernel(x, c, positions, w_ada, b_ada, g_mix, g_mlp, mla_w_dq, mla_g_q, mla_w_uq, mla_w_dkv, mla_g_kv, mla_w_ukv, mla_w_o, swa_w_qkv, swa_b_qkv, swa_sinks, swa_w_o, swa_b_o, w_ff1, w_ff2, g_final, loss_target, m_w_ada, m_b_ada, m_g_mix, m_g_mlp, m_mla_w_dq, m_mla_g_q, m_mla_w_uq, m_mla_w_dkv, m_mla_g_kv, m_mla_w_ukv, m_mla_w_o, m_swa_w_qkv, m_swa_b_qkv, m_swa_sinks, m_swa_w_o, m_swa_b_o, m_w_ff1, m_w_ff2, m_g_final, v_w_ada, v_b_ada, v_g_mix, v_g_mlp, v_mla_w_dq, v_mla_g_q, v_mla_w_uq, v_mla_w_dkv, v_mla_g_kv, v_mla_w_ukv, v_mla_w_o, v_swa_w_qkv, v_swa_b_qkv, v_swa_sinks, v_swa_w_o, v_swa_b_o, v_w_ff1, v_w_ff2, v_g_final):
    W = dict(w_ada=w_ada, b_ada=b_ada, g_mix=g_mix, g_mlp=g_mlp, mla_w_dq=mla_w_dq, mla_g_q=mla_g_q, mla_w_uq=mla_w_uq,
             mla_w_dkv=mla_w_dkv, mla_g_kv=mla_g_kv, mla_w_ukv=mla_w_ukv, mla_w_o=mla_w_o, swa_w_qkv=swa_w_qkv,
             swa_b_qkv=swa_b_qkv, swa_sinks=swa_sinks, swa_w_o=swa_w_o, swa_b_o=swa_b_o, w_ff1=w_ff1, w_ff2=w_ff2,
             g_final=g_final)
    M = dict(w_ada=m_w_ada, b_ada=m_b_ada, g_mix=m_g_mix, g_mlp=m_g_mlp, mla_w_dq=m_mla_w_dq, mla_g_q=m_mla_g_q,
             mla_w_uq=m_mla_w_uq, mla_w_dkv=m_mla_w_dkv, mla_g_kv=m_mla_g_kv, mla_w_ukv=m_mla_w_ukv, mla_w_o=m_mla_w_o,
             swa_w_qkv=m_swa_w_qkv, swa_b_qkv=m_swa_b_qkv, swa_sinks=m_swa_sinks, swa_w_o=m_swa_w_o, swa_b_o=m_swa_b_o,
             w_ff1=m_w_ff1, w_ff2=m_w_ff2, g_final=m_g_final)
    V = dict(w_ada=v_w_ada, b_ada=v_b_ada, g_mix=v_g_mix, g_mlp=v_g_mlp, mla_w_dq=v_mla_w_dq, mla_g_q=v_mla_g_q,
             mla_w_uq=v_mla_w_uq, mla_w_dkv=v_mla_w_dkv, mla_g_kv=v_mla_g_kv, mla_w_ukv=v_mla_w_ukv, mla_w_o=v_mla_w_o,
             swa_w_qkv=v_swa_w_qkv, swa_b_qkv=v_swa_b_qkv, swa_sinks=v_swa_sinks, swa_w_o=v_swa_w_o, swa_b_o=v_swa_b_o,
             w_ff1=v_w_ff1, w_ff2=v_w_ff2, g_final=v_g_final)
    order = list(W)
    names = list(SHARDED)
    core = lax.axis_index("c")
    chip = 2 * lax.axis_index("x") + lax.axis_index("y")
    dev = 2 * chip + core
    core_arr = core.astype(jnp.int32).reshape(1)
    chip_arr = chip.astype(jnp.int32).reshape(1)

    def whole(n, g, own):
        g = lax.dynamic_update_slice(g, own[None], (chip, 0, 0))
        if n in ("w_ff1", "w_ff2"):
            return g
        if n in COL_SPLIT:
            return g.transpose(1, 0, 2).reshape(g.shape[1], N_CHIPS * g.shape[2])
        return g.reshape(N_CHIPS * g.shape[1], g.shape[2])

    early = [n for n in names if n.startswith("mla_")]
    at_once = [n for n in early if n != "mla_w_o"]
    local = {n: W[n].astype(MXU_DTYPE).reshape(_view2d(n)) for n in at_once}
    wts = {n: whole(n, g, local[n]) for n, g in zip(at_once, _weight_gather([local[n] for n in at_once]))}

    nbq, nbo = BIASES["swa_b_qkv"] // N_CHIPS, BIASES["swa_b_o"] // N_CHIPS
    first = jnp.concatenate([c.reshape(-1), swa_b_qkv.reshape(-1), swa_b_o.reshape(-1),
                             jnp.zeros((FIRST_ROWS * LANES - D - nbq - nbo,), jnp.float32)]).reshape(FIRST_ROWS, LANES)
    first_all = _all_gather(first).reshape(N_DEV, FIRST_ROWS * LANES)
    c_all = first_all[:, :D]
    south = first_all[0::2]
    wts["swa_b_qkv"] = south[:, D:D + nbq].reshape(1, N_CHIPS * nbq)
    wts["swa_b_o"] = south[:, D + nbq:D + nbq + nbo].reshape(1, N_CHIPS * nbo)
    cond_all, part = _ada_part(c_all, w_ada)
    ncol = w_ada.shape[2]
    part_all = _all_gather(part.reshape(-1, LANES)).reshape(N_DEV, DEPTH, N_DEV, ncol)
    mine = _one_hot_pick(part_all[0::2], dev, axis=2)
    mod = mine.transpose(1, 0, 2).reshape(DEPTH, N_CHIPS * ncol) + b_ada
    vecs = jnp.concatenate([mod.reshape(DEPTH, 6, D), g_mix[:, None, :], g_mlp[:, None, :]], axis=1)

    late = [("mla_w_o", None), ("w_ff1", 0), ("w_ff2", 0), ("swa_w_qkv", None), ("swa_w_o", None), ("w_ff1", 1),
            ("w_ff2", 1)]
    late_local = [(W[n][0] if l is None else W[n][l]).astype(MXU_DTYPE) for n, l in late]
    send_sems, recv_sems, passed, lands, token = _late_gather_start(late_local, [vecs] + [wts[n] for n in at_once])

    def late_weights(after):
        got = _late_gather_wait(send_sems, recv_sems, passed, lands, after)
        out = {"w_ff1": [None] * DEPTH, "w_ff2": [None] * DEPTH}
        for (n, l), g, own in zip(late, got, late_local):
            if l is None:
                out[n] = whole(n, g, own)
            else:
                out[n][l] = whole(n, g, own)
        return out

    late_names = [n for n in names if n not in early]
    reduce_state = {}

    def on_late_grads(late_grads):
        s_sems, r_sems, passed_g, zones, tok = _pair_in_start([late_grads[n] for n in late_names])
        reduce_state.update(pair=(s_sems, r_sems, passed_g, zones))
        return tok

    def on_late_landed(after):
        gl, got = _pair_in_wait(*reduce_state["pair"], after)
        sums = [_pair_sum(g, s, core_arr, "pair_sum_" + n) for n, g, s in zip(late_names, gl, got)]
        s_sems, r_sems, parts, zones, tok = _exchange_start(sums, "grad_exchange_start")
        reduce_state.update(pairs=(gl, got), split=(s_sems, r_sems, parts, zones))
        return tok

    grad_x, grads, small = _sequence_step(
        x[0], loss_target[0], positions[0], vecs, mla_g_q + token[0, 0], mla_g_kv, swa_sinks, g_final, wts,
        late_weights, on_late_grads, on_late_landed)

    small["b_ada"] = small.pop("dmod")
    small_all = _all_gather(_pack_small(small))
    pk = lambda src: _pack_small({n: src[n] for n in SMALL if n != "loss" and n not in BIASES})
    off, n = _small_slots()["b_ada"]
    dmod_all = small_all.reshape(N_DEV, -1)[:, off:off + n].reshape(N_DEV, DEPTH, N_CHIPS, ncol)
    dm = _one_hot_pick(dmod_all, chip, axis=2).transpose(1, 0, 2)

    def chip_sums(tensor_names, gl, got, others):
        return [_chip_sum(g, s, o, core_arr, chip_arr, "chip_sum_" + n) for n, g, s, o in zip(tensor_names, gl, got, others)]

    def adamw(tensor_names, mine, sibling):
        return {n: _adamw_halves(W[n], a, b, M[n], V[n], core_arr, "adamw_" + n)
                for n, a, b in zip(tensor_names, mine, sibling)}

    late_others = _exchange_wait(*reduce_state["split"], grad_x, "grad_exchange_wait")
    p_sems, p_rems, p_halves, p_zones, p_tok = _pair_out_start(chip_sums(late_names, *reduce_state["pairs"], late_others))
    gl = [grads[n] for n in early]
    got = _grad_pair_in(gl, p_tok)
    sums = [_pair_sum(g, s, core_arr, "pair_sum_" + n) for n, g, s in zip(early, gl, got)]
    e_sems, e_rems, e_parts, e_zones, e_tok = _exchange_start(sums, "mla_exchange_start")
    res = adamw(late_names, *_pair_out_wait(p_sems, p_rems, p_halves, p_zones, e_tok))
    res["w_ada"] = _ada_grad_adamw(cond_all.T, dm, w_ada, m_w_ada, v_w_ada)
    small_res = _small_adamw(small_all, pk(W), pk(M), pk(V))
    early_others = _exchange_wait(e_sems, e_rems, e_parts, e_zones, res["w_ff2"][1], "mla_exchange_wait")
    early_halves = chip_sums(early, gl, got, early_others)
    res.update(adamw(early, early_halves, _grad_pair_out(early_halves)))

    for n, width in BIASES.items():
        g = _one_hot_pick(small_res[n][0].reshape(N_CHIPS, width // N_CHIPS), chip, axis=0).reshape(1, -1)
        res[n] = [g] + _adamw(W[n], g, M[n], V[n], "adamw_" + n)
    for name in order:
        if name not in res:
            res[name] = small_res[name]
    outs = [small_res["loss"][0], grad_x[None]]
    for k in range(4):
        outs += [res[name][k] for name in order]
    return tuple(outs)
```

```python
import math

import jax
import jax.numpy as jnp
import numpy as np
from jax import lax
from jax.experimental import pallas as pl
from jax.experimental.pallas import tpu as pltpu

D = 1024
DEPTH = 2
MLA_HEADS = 8
QK_NOPE = 128
QK_ROPE = 64
V_DIM = 128
Q_LORA = 384
KV_LORA = 256
ROPE_THETA = 10000.0
SWA_HEADS = 16
SWA_KV_HEADS = 4
SWA_HEAD_DIM = 64
SWA_GROUP = SWA_HEADS // SWA_KV_HEADS
WINDOW = 128
D_FF = 4 * D
EPS = 1e-6
ADAM_LR = 0.001
ADAM_B1 = 0.9
ADAM_B2 = 0.999
ADAM_EPS = 1e-08
ADAM_WD = 0.01
ADAM_STEP = 10

N_CHIPS = 4
N_DEV = 8
LANES = 128
QK_EXT = 256
V_EXT = 256
MLA_SCALE = (QK_NOPE + QK_ROPE) ** -0.5
LOG2E = math.log2(math.e)
LN2 = math.log(2.0)
MLA_QSCALE = MLA_SCALE * LOG2E
ATTN_BLOCK = 2048
ATTN_SUB = 512
MLP_FWD_TILE = (1024, 1024)
MLP_BWD_TILE = (512, 1024)
DW_TOKENS = 4096
DW_TILE = 1024
ROW_TILE = 1024
PROJ_ROWS = 512
FIRST_ROWS = 16
STREAM_ROWS = 512
SWA_SCALE = SWA_HEAD_DIM ** -0.5
NEG = -1e30
MXU_DTYPE = jnp.bfloat16
VMEM_LIMIT = 56 * 1024 * 1024

R_SH1, R_SC1, R_GT1, R_SH2, R_SC2, R_GT2, R_GMIX, R_GMLP = range(8)
R_BO = 6


def _tile(n, pref):
    if n <= pref:
        return n
    for t in range(pref, 7, -1):
        if n % t == 0 and t % 8 == 0:
            return t
    return n


def _dot(a, b):
    return jnp.dot(a, b, preferred_element_type=jnp.float32)


def _dot_nt(a, b):
    return lax.dot_general(a, b, (((1,), (1,)), ((), ())), preferred_element_type=jnp.float32)


def _dot_tn(a, b):
    return lax.dot_general(a, b, (((0,), (0,)), ((), ())), preferred_element_type=jnp.float32)


def _rms(x):
    r = lax.rsqrt(jnp.mean(x * x, axis=-1, keepdims=True) + EPS)
    return x * r, r


def _rms_bwd(dxhat, xhat, r):
    return r * (dxhat - xhat * jnp.mean(dxhat * xhat, axis=-1, keepdims=True))


def _rowsum(v):
    return jnp.sum(v, axis=0, keepdims=True)


def _params(*sem):
    return pltpu.CompilerParams(dimension_semantics=sem, vmem_limit_bytes=VMEM_LIMIT)


def _full(shape):
    nd = len(shape)
    return pl.BlockSpec(shape, lambda *_: (0,) * nd)


def _rows(tm, cols):
    return pl.BlockSpec((tm, cols), lambda i, *_: (i, 0))


def _modulate_bwd(dh, x, vec_ref, r_g, r_sc, r_sh, ps_ref, dres):
    xhat, r = _rms(x)
    g = vec_ref[r_g:r_g + 1, :]
    n = xhat * g
    ps_ref[r_sh:r_sh + 1, :] += _rowsum(dh)
    ps_ref[r_sc:r_sc + 1, :] += _rowsum(dh * n)
    dn = dh * (1.0 + vec_ref[r_sc:r_sc + 1, :])
    ps_ref[r_g:r_g + 1, :] += _rowsum(dn * xhat)
    return dres + _rms_bwd(dn * g, xhat, r)


def _mla_pre(x, vec, wcat, g_q, g_kv, wuq, wukv, cs):
    T = x.shape[0]
    tm = _tile(T, PROJ_ROWS)
    H = MLA_HEADS

    def body(x_ref, vec_ref, wcat_ref, gq_ref, gkv_ref, wuq_ref, wukv_ref, cs_ref, h_ref, z_ref, q_ref, k_ref, v_ref):
        xhat, _ = _rms(x_ref[...])
        h = xhat * vec_ref[R_GMIX:R_GMIX + 1, :] * (1.0 + vec_ref[R_SC1:R_SC1 + 1, :]) + vec_ref[R_SH1:R_SH1 + 1, :]
        hb = h.astype(MXU_DTYPE)
        h_ref[...] = hb
        z = _dot(hb, wcat_ref[...])
        z_ref[...] = z
        cq = (_rms(z[:, :Q_LORA])[0] * gq_ref[...]).astype(MXU_DTYPE)
        ckv = (_rms(z[:, Q_LORA:Q_LORA + KV_LORA])[0] * gkv_ref[...]).astype(MXU_DTYPE)
        cs_t = cs_ref[...]
        t = z[:, Q_LORA + KV_LORA:] * cs_t
        k_rope = (t + pltpu.roll(t, QK_ROPE, axis=1)).astype(MXU_DTYPE)
        low = lax.broadcasted_iota(jnp.int32, (1, LANES), 1) < QK_ROPE
        one_col = (lax.broadcasted_iota(jnp.int32, (1, LANES), 1) == 0).astype(MXU_DTYPE)
        for hd in range(H):
            qf = _dot(cq, wuq_ref[hd])
            tq = qf[:, QK_NOPE:] * cs_t
            tq = tq + pltpu.roll(tq, QK_ROPE, axis=1)
            q_ref[hd, :, :QK_NOPE] = (qf[:, :QK_NOPE] * MLA_QSCALE).astype(MXU_DTYPE)
            q_ref[hd, :, QK_NOPE:] = jnp.where(low, tq * MLA_QSCALE, 0.0).astype(MXU_DTYPE)
            kvf = _dot(ckv, wukv_ref[hd])
            k_ref[hd, :, :QK_NOPE] = kvf[:, :QK_NOPE].astype(MXU_DTYPE)
            k_ref[hd, :, QK_NOPE:] = k_rope
            v_ref[hd, :, :V_DIM] = kvf[:, QK_NOPE:].astype(MXU_DTYPE)
            v_ref[hd, :, V_DIM:] = jnp.broadcast_to(one_col, (tm, LANES))

    zc = wcat.shape[1]
    return pl.pallas_call(
        body, name="mla_pre", grid=(T // tm,),
        in_specs=[_rows(tm, D), _full((8, D)), _full(wcat.shape), _full(g_q.shape), _full(g_kv.shape),
                  _full(wuq.shape), _full(wukv.shape), _rows(tm, LANES)],
        out_specs=[_rows(tm, D), _rows(tm, zc),
                   pl.BlockSpec((H, tm, QK_EXT), lambda i: (0, i, 0)),
                   pl.BlockSpec((H, tm, QK_EXT), lambda i: (0, i, 0)),
                   pl.BlockSpec((H, tm, V_EXT), lambda i: (0, i, 0))],
        out_shape=[jax.ShapeDtypeStruct((T, D), MXU_DTYPE), jax.ShapeDtypeStruct((T, zc), jnp.float32),
                   jax.ShapeDtypeStruct((H, T, QK_EXT), MXU_DTYPE), jax.ShapeDtypeStruct((H, T, QK_EXT), MXU_DTYPE),
                   jax.ShapeDtypeStruct((H, T, V_EXT), MXU_DTYPE)],
        compiler_params=_params("parallel"),
    )(x, vec, wcat, g_q, g_kv, wuq, wukv, cs)


def _mla_attn_fwd(q, k, v):
    H, T, _ = q.shape
    tb = _tile(T, ATTN_BLOCK)
    sub = min(ATTN_SUB, tb)
    ns, nb = tb // sub, T // tb
    pairs = [(i, j) for i in range(nb) for j in range(i + 1)]
    qi_tab = jnp.asarray([i for i, _ in pairs], jnp.int32)
    kj_tab = jnp.asarray([j for _, j in pairs], jnp.int32)

    def body(qi_ref, kj_ref, q_ref, k_ref, v_ref, o_ref, lse_ref, m_sc, acc_sc):
        qi, kj = qi_ref[pl.program_id(1)], kj_ref[pl.program_id(1)]

        @pl.when(kj == 0)
        def _():
            m_sc[...] = jnp.full_like(m_sc, NEG)
            acc_sc[...] = jnp.zeros_like(acc_sc)

        def update(r, kk, masked):
            rows, keys = pl.ds(r * sub, sub), pl.ds(kk * sub, sub)
            s = _dot_nt(q_ref[0, rows, :], k_ref[0, keys, :])
            if masked:
                row = lax.broadcasted_iota(jnp.int32, (sub, sub), 0)
                col = lax.broadcasted_iota(jnp.int32, (sub, sub), 1)
                s = jnp.where(col <= row, s, NEG)
            m_prev = m_sc[rows, :]
            m_new = jnp.maximum(m_prev, jnp.max(s, axis=1, keepdims=True))
            alpha = jnp.exp2(m_prev - m_new)
            p = jnp.exp2(s - jnp.tile(m_new, (1, sub // LANES)))
            pv = _dot(p.astype(MXU_DTYPE), v_ref[0, keys, :])
            acc_sc[rows, :] = jnp.tile(alpha, (1, V_EXT // LANES)) * acc_sc[rows, :] + pv
            m_sc[rows, :] = m_new

        @pl.when(kj < qi)
        def _():
            for kk in range(ns):
                for r in range(ns):
                    update(r, kk, False)

        @pl.when(kj == qi)
        def _():
            for kk in range(ns):
                for r in range(kk, ns):
                    update(r, kk, r == kk)
            l = acc_sc[:, V_DIM:V_DIM + 1]
            o_ref[...] = (acc_sc[:, :V_DIM] / l).astype(o_ref.dtype)
            lse = m_sc[...] + jnp.log2(l)
            pick = (lax.broadcasted_iota(jnp.int32, (8, LANES), 1) == 0).astype(jnp.float32)
            row = lax.dot_general(pick, lse, (((1,), (1,)), ((), ())), precision=lax.Precision.HIGHEST,
                                  preferred_element_type=jnp.float32)
            lse_ref[0] = row[0:1, :]

    q_idx = lambda h, p, qi_ref, kj_ref: (h, qi_ref[p], 0)
    kv_idx = lambda h, p, qi_ref, kj_ref: (h, kj_ref[p], 0)
    return pl.pallas_call(
        body, name="mla_attn_fwd",
        grid_spec=pltpu.PrefetchScalarGridSpec(
            num_scalar_prefetch=2, grid=(H, len(pairs)),
            in_specs=[pl.BlockSpec((1, tb, QK_EXT), q_idx), pl.BlockSpec((1, tb, QK_EXT), kv_idx),
                      pl.BlockSpec((1, tb, V_EXT), kv_idx)],
            out_specs=[pl.BlockSpec((tb, V_DIM), lambda h, p, qi_ref, kj_ref: (qi_ref[p], h)),
                       pl.BlockSpec((1, 1, tb), lambda h, p, qi_ref, kj_ref: (h, 0, qi_ref[p]))],
            scratch_shapes=[pltpu.VMEM((tb, LANES), jnp.float32), pltpu.VMEM((tb, V_EXT), jnp.float32)]),
        out_shape=[jax.ShapeDtypeStruct((T, H * V_DIM), MXU_DTYPE), jax.ShapeDtypeStruct((H, 1, T), jnp.float32)],
        compiler_params=_params("parallel", "arbitrary"),
    )(qi_tab, kj_tab, q, k, v)


def _post_attn(o, x, w_o, bias, vec, o_transposed=False):
    T = x.shape[0]
    tm = _tile(T, ROW_TILE)
    o_spec = pl.BlockSpec((D, tm), lambda i: (0, i)) if o_transposed else _rows(tm, D)

    def body(o_ref, x_ref, w_ref, b_ref, vec_ref, y_ref, xm_ref, h_ref):
        y = (_dot_tn if o_transposed else _dot)(o_ref[...], w_ref[...]) + b_ref[...]
        y_ref[...] = y.astype(y_ref.dtype)
        xm = x_ref[...] + vec_ref[R_GT1:R_GT1 + 1, :] * y
        xm_ref[...] = xm
        xhat, _ = _rms(xm)
        h = xhat * vec_ref[R_GMLP:R_GMLP + 1, :] * (1.0 + vec_ref[R_SC2:R_SC2 + 1, :]) + vec_ref[R_SH2:R_SH2 + 1, :]
        h_ref[...] = h.astype(h_ref.dtype)

    return pl.pallas_call(
        body, name="post_attn", grid=(T // tm,),
        in_specs=[o_spec, _rows(tm, D), _full((D, D)), _full((1, D)), _full((8, D))],
        out_specs=[_rows(tm, D), _rows(tm, D), _rows(tm, D)],
        out_shape=[jax.ShapeDtypeStruct((T, D), MXU_DTYPE), jax.ShapeDtypeStruct((T, D), jnp.float32),
                   jax.ShapeDtypeStruct((T, D), MXU_DTYPE)],
        compiler_params=_params("parallel"),
    )(o, x, w_o, bias, vec)


def _ff_specs(tf):
    per = D_FF // N_CHIPS // tf
    w1 = pl.BlockSpec((None, D, tf), lambda i, f: (f // per, 0, f % per))
    w2 = pl.BlockSpec((None, tf, D), lambda i, f: (f // per, f % per, 0))
    return w1, w2


def _mlp_fwd(h2, w1, w2, xm, vec):
    T = h2.shape[0]
    tm = _tile(T, MLP_FWD_TILE[0])
    tf = _tile(D_FF // N_CHIPS, MLP_FWD_TILE[1])
    nf = D_FF // tf
    w1_spec, w2_spec = _ff_specs(tf)

    def body(h_ref, w1_ref, w2_ref, xm_ref, vec_ref, a_ref, y_ref, xo_ref, acc):
        f = pl.program_id(1)

        @pl.when(f == 0)
        def _():
            acc[...] = jnp.zeros_like(acc)

        u = jnp.maximum(_dot(h_ref[...], w1_ref[...]), 0.0)
        ab = (u * u).astype(MXU_DTYPE)
        a_ref[...] = ab
        acc[...] += _dot(ab, w2_ref[...])

        @pl.when(f == nf - 1)
        def _():
            y = acc[...]
            y_ref[...] = y.astype(y_ref.dtype)
            xo_ref[...] = xm_ref[...] + vec_ref[R_GT2:R_GT2 + 1, :] * y

    return pl.pallas_call(
        body, name="mlp_fwd", grid=(T // tm, nf),
        in_specs=[_rows(tm, D), w1_spec, w2_spec, _rows(tm, D), _full((8, D))],
        out_specs=[pl.BlockSpec((tm, tf), lambda i, f: (i, f)), _rows(tm, D), _rows(tm, D)],
        out_shape=[jax.ShapeDtypeStruct((T, D_FF), MXU_DTYPE), jax.ShapeDtypeStruct((T, D), MXU_DTYPE),
                   jax.ShapeDtypeStruct((T, D), jnp.float32)],
        scratch_shapes=[pltpu.VMEM((tm, D), jnp.float32)],
        compiler_params=_params("parallel", "arbitrary"),
    )(h2, w1, w2, xm, vec)


def _swa_pre(x, vec, w_qkv, b_qkv):
    T = x.shape[0]
    tm = _tile(T, PROJ_ROWS)
    nq = SWA_HEADS * SWA_HEAD_DIM
    nk = SWA_KV_HEADS * SWA_HEAD_DIM
    wq_t, w_kv = w_qkv[:, :nq].T, w_qkv[:, nq:]
    bq_col, b_kv = b_qkv[:, :nq].reshape(nq, 1), b_qkv[:, nq:]

    def body(x_ref, vec_ref, wq_ref, wkv_ref, bq_ref, bkv_ref, h_ref, qt_ref, k_ref, v_ref):
        xhat, _ = _rms(x_ref[...])
        h = xhat * vec_ref[R_GMIX:R_GMIX + 1, :] * (1.0 + vec_ref[R_SC1:R_SC1 + 1, :]) + vec_ref[R_SH1:R_SH1 + 1, :]
        hb = h.astype(MXU_DTYPE)
        h_ref[...] = hb
        qt_ref[...] = ((_dot_nt(wq_ref[...], hb) + bq_ref[...]) * SWA_SCALE).astype(MXU_DTYPE)
        kv = _dot(hb, wkv_ref[...]) + bkv_ref[...]
        k_ref[...] = kv[:, :nk].astype(MXU_DTYPE)
        v_ref[...] = kv[:, nk:].astype(MXU_DTYPE)

    return pl.pallas_call(
        body, name="swa_pre", grid=(T // tm,),
        in_specs=[_rows(tm, D), _full((8, D)), _full(wq_t.shape), _full(w_kv.shape), _full(bq_col.shape),
                  _full(b_kv.shape)],
        out_specs=[_rows(tm, D), pl.BlockSpec((nq, tm), lambda i: (0, i)), _rows(tm, nk), _rows(tm, nk)],
        out_shape=[jax.ShapeDtypeStruct((T, D), MXU_DTYPE), jax.ShapeDtypeStruct((nq, T), MXU_DTYPE),
                   jax.ShapeDtypeStruct((T, nk), MXU_DTYPE), jax.ShapeDtypeStruct((T, nk), MXU_DTYPE)],
        compiler_params=_params("parallel"),
    )(x, vec, wq_t, w_kv, bq_col, b_kv)


def _swa_bias():
    W = WINDOW
    slopes = 2.0 ** (-8.0 * np.arange(1, SWA_HEADS + 1) / SWA_HEADS)
    j, i = np.arange(W)[:, None], np.arange(W)[None, :]
    dist = np.where(j > i, W + i - j, i - j)
    bias = -slopes[:, None, None] * dist[None].astype(np.float64)
    bias = bias.reshape(SWA_KV_HEADS, SWA_GROUP, W, W).transpose(0, 2, 1, 3)
    return jnp.asarray(bias.reshape(SWA_KV_HEADS, W, SWA_GROUP * W), jnp.float32)


def _swa_fold_mask():
    W, G = WINDOW, SWA_GROUP
    j = lax.broadcasted_iota(jnp.int32, (W, G * W), 0)
    i = lax.broadcasted_iota(jnp.int32, (W, G * W), 1) & (W - 1)
    return j > i


def _swa_fold(band, up):
    return jnp.where(up, band[:WINDOW], band[WINDOW:])


def _swa_unfold(folded, up):
    zero = jnp.zeros_like(folded)
    return jnp.concatenate([jnp.where(up, folded, zero), jnp.where(up, zero, folded)], axis=0)


SWA_STEP_BLOCKS = 4


def _swa_blocks(T):
    nb = T // WINDOW
    return next(b for b in (SWA_STEP_BLOCKS, 2, 1) if nb % b == 0)


def _swa_views(b, qt_ref, kp_ref, kc_ref):
    W = WINDOW
    prev = kp_ref if b == 0 else kc_ref.at[pl.ds((b - 1) * W, W), :]
    return qt_ref.at[:, pl.ds(b * W, W)], prev, kc_ref.at[pl.ds(b * W, W), :]


def _swa_probs(has_prev, up, kh, qt_ref, kp_ref, kc_ref, bias_ref, sink_ref):
    W, Dh, G = WINDOW, SWA_HEAD_DIM, SWA_GROUP
    qt = jnp.concatenate([qt_ref[(kh * G + g) * Dh:(kh * G + g + 1) * Dh, :] for g in range(G)], axis=1)
    kb = jnp.concatenate([kp_ref[:, kh * Dh:(kh + 1) * Dh], kc_ref[:, kh * Dh:(kh + 1) * Dh]], axis=0)
    s = _swa_fold(_dot(kb, qt), up) + bias_ref[kh]
    if has_prev is not True:
        s = jnp.where(up & jnp.logical_not(has_prev), NEG, s)
    sink = sink_ref[kh]
    m = jnp.maximum(jnp.max(s, axis=0, keepdims=True), sink)
    p = jnp.exp(s - m)
    p_sink = jnp.exp(sink - m)
    inv = 1.0 / (jnp.sum(p, axis=0, keepdims=True) + p_sink)
    return qt, kb, p * inv, p_sink * inv


def _swa_attn_fwd(qt, k, v, bias, sink_rows):
    T = qt.shape[1]
    W, Dh, G, Hk = WINDOW, SWA_HEAD_DIM, SWA_GROUP, SWA_KV_HEADS
    nk = Hk * Dh

    nb = _swa_blocks(T)

    def body(qt_ref, kp_ref, kc_ref, vp_ref, vc_ref, bias_ref, sink_ref, ot_ref):
        n = pl.program_id(0)
        up = _swa_fold_mask()
        for b in range(nb):
            q_b, kp_b, kc_b = _swa_views(b, qt_ref, kp_ref, kc_ref)
            _, vp_b, vc_b = _swa_views(b, qt_ref, vp_ref, vc_ref)
            for kh in range(Hk):
                _, _, pn, _ = _swa_probs(True if b else n > 0, up, kh, q_b, kp_b, kc_b, bias_ref, sink_ref)
                vb = jnp.concatenate([vp_b[:, kh * Dh:(kh + 1) * Dh], vc_b[:, kh * Dh:(kh + 1) * Dh]], axis=0)
                ot = _dot_tn(vb, _swa_unfold(pn, up).astype(MXU_DTYPE))
                for g in range(G):
                    rows = pl.ds((kh * G + g) * Dh, Dh)
                    ot_ref[rows, pl.ds(b * W, W)] = ot[:, g * W:(g + 1) * W].astype(ot_ref.dtype)

    prev = lambda n: (jnp.maximum(n * nb - 1, 0), 0)
    cur = lambda n: (n, 0)
    col = lambda n: (0, n)
    return pl.pallas_call(
        body, name="swa_attn_fwd", grid=(T // (nb * W),),
        in_specs=[pl.BlockSpec((D, nb * W), col), pl.BlockSpec((W, nk), prev), pl.BlockSpec((nb * W, nk), cur),
                  pl.BlockSpec((W, nk), prev), pl.BlockSpec((nb * W, nk), cur), _full(bias.shape),
                  _full(sink_rows.shape)],
        out_specs=pl.BlockSpec((D, nb * W), col),
        out_shape=jax.ShapeDtypeStruct((D, T), MXU_DTYPE),
        compiler_params=_params("parallel"),
    )(qt, k, k, v, v, bias, sink_rows)


def _final_loss(x, tgt, g):
    T = x.shape[0]
    tm = _tile(T, ROW_TILE)

    def body(x_ref, t_ref, g_ref, loss_ref, dx_ref, dg_ref):
        @pl.when(pl.program_id(0) == 0)
        def _():
            loss_ref[...] = jnp.zeros_like(loss_ref)
            dg_ref[...] = jnp.zeros_like(dg_ref)

        xhat, r = _rms(x_ref[...])
        gv = g_ref[...]
        e = xhat * gv - t_ref[...]
        loss_ref[...] += 0.5 * jnp.sum(jnp.mean(e * e, axis=-1, keepdims=True), axis=0, keepdims=True)
        dy = e * (1.0 / D)
        dg_ref[...] += _rowsum(dy * xhat)
        dx_ref[...] = _rms_bwd(dy * gv, xhat, r)

    return pl.pallas_call(
        body, name="final_loss", grid=(T // tm,),
        in_specs=[_rows(tm, D), _rows(tm, D), _full((1, D))],
        out_specs=[_full((8, LANES)), _rows(tm, D), _full((1, D))],
        out_shape=[jax.ShapeDtypeStruct((8, LANES), jnp.float32), jax.ShapeDtypeStruct((T, D), jnp.float32),
                   jax.ShapeDtypeStruct((1, D), jnp.float32)],
        compiler_params=_params("arbitrary"),
    )(x, tgt, g)


def _mlp_bwd(dxo, y2, a, w1, w2, xm, vec):
    T = dxo.shape[0]
    tm = _tile(T, MLP_BWD_TILE[0])
    tf = _tile(D_FF // N_CHIPS, MLP_BWD_TILE[1])
    nf = D_FF // tf
    w1_spec, w2_spec = _ff_specs(tf)

    def body(dxo_ref, y_ref, a_ref, w1_ref, w2_ref, xm_ref, vec_ref, du_ref, dy_ref, dxm_ref, ps_ref, dyb, acc):
        i, f = pl.program_id(0), pl.program_id(1)

        @pl.when((i == 0) & (f == 0))
        def _():
            ps_ref[...] = jnp.zeros_like(ps_ref)

        @pl.when(f == 0)
        def _():
            dxo_t = dxo_ref[...]
            d = (dxo_t * vec_ref[R_GT2:R_GT2 + 1, :]).astype(MXU_DTYPE)
            dyb[...] = d
            dy_ref[...] = d
            acc[...] = jnp.zeros_like(acc)
            ps_ref[R_GT2:R_GT2 + 1, :] += _rowsum(dxo_t * y_ref[...].astype(jnp.float32))

        da = _dot_nt(dyb[...], w2_ref[...])
        dub = (da * (2.0 * jnp.sqrt(a_ref[...].astype(jnp.float32)))).astype(MXU_DTYPE)
        du_ref[...] = dub
        acc[...] += _dot_nt(dub, w1_ref[...])

        @pl.when(f == nf - 1)
        def _():
            dxm_ref[...] = _modulate_bwd(acc[...], xm_ref[...], vec_ref, R_GMLP, R_SC2, R_SH2, ps_ref, dxo_ref[...])

    return pl.pallas_call(
        body, name="mlp_bwd", grid=(T // tm, nf),
        in_specs=[_rows(tm, D), _rows(tm, D), pl.BlockSpec((tm, tf), lambda i, f: (i, f)), w1_spec, w2_spec,
                  _rows(tm, D), _full((8, D))],
        out_specs=[pl.BlockSpec((tm, tf), lambda i, f: (i, f)), _rows(tm, D), _rows(tm, D), _full((8, D))],
        out_shape=[jax.ShapeDtypeStruct((T, D_FF), MXU_DTYPE), jax.ShapeDtypeStruct((T, D), MXU_DTYPE),
                   jax.ShapeDtypeStruct((T, D), jnp.float32), jax.ShapeDtypeStruct((8, D), jnp.float32)],
        scratch_shapes=[pltpu.VMEM((tm, D), MXU_DTYPE), pltpu.VMEM((tm, D), jnp.float32)],
        compiler_params=_params("arbitrary", "arbitrary"),
    )(dxo, y2, a, w1, w2, xm, vec)


def _mm_tn(a, g, name, split=None, layers=1, layer=0, into=None, a_transposed=False):
    K, T = a.shape if a_transposed else a.shape[::-1]
    N = g.shape[1]
    kq = K // N_CHIPS if split == "rows" else K
    nq = N // N_CHIPS if split == "cols" else N
    bk, bn, bt = _tile(kq, DW_TILE), _tile(nq, DW_TILE), _tile(T, DW_TOKENS)
    if nq % bn or bn % LANES:
        bn = nq
    kper, nper = kq // bk, nq // bn

    def body(*refs):
        a_ref, g_ref, o_ref = refs[0], refs[1], refs[-1]

        @pl.when(pl.program_id(2) == 0)
        def _():
            o_ref[...] = jnp.zeros_like(o_ref)

        o_ref[...] += (_dot if a_transposed else _dot_tn)(a_ref[...], g_ref[...])

    a_spec = pl.BlockSpec((bk, bt), lambda k, n, t: (k, t)) if a_transposed else pl.BlockSpec((bt, bk), lambda k, n, t: (t, k))
    in_specs = [a_spec, pl.BlockSpec((bt, bn), lambda k, n, t: (t, n))]
    args = [a, g]
    aliases = {}
    if split is None:
        out_spec = pl.BlockSpec((bk, bn), lambda k, n, t: (k, n))
        out_shape = jax.ShapeDtypeStruct((K, N), jnp.float32)
    else:
        if split == "cols":
            idx = lambda k, n, t: (n // nper, layer, k, n % nper)
        else:
            idx = lambda k, n, t: (k // kper, layer, k % kper, n)
        out_spec = pl.BlockSpec((None, None, bk, bn), idx)
        out_shape = jax.ShapeDtypeStruct((N_CHIPS, layers, kq, nq), jnp.float32)
        if into is not None:
            in_specs.append(pl.BlockSpec(memory_space=pl.ANY))
            args.append(into)
            aliases = {2: 0}
    return pl.pallas_call(
        body, name=name, grid=(K // bk, N // bn, T // bt), in_specs=in_specs, out_specs=out_spec, out_shape=out_shape,
        input_output_aliases=aliases, compiler_params=_params("parallel", "parallel", "arbitrary"),
    )(*args)


def _attn_out_bwd(dxm, y1, o, w_o, vec, with_delta):
    T = dxm.shape[0]
    tm = _tile(T, ROW_TILE)
    H = MLA_HEADS

    def body(dxm_ref, y_ref, w_ref, vec_ref, *refs):
        o_ref = refs[0] if with_delta else None
        dy_ref, do_ref, ps_ref, *delta_ref = refs[1:] if with_delta else refs

        @pl.when(pl.program_id(0) == 0)
        def _():
            ps_ref[...] = jnp.zeros_like(ps_ref)

        dxm_t = dxm_ref[...]
        dy = dxm_t * vec_ref[R_GT1:R_GT1 + 1, :]
        ps_ref[R_GT1:R_GT1 + 1, :] += _rowsum(dxm_t * y_ref[...].astype(jnp.float32))
        ps_ref[R_BO:R_BO + 1, :] += _rowsum(dy)
        dyb = dy.astype(MXU_DTYPE)
        dy_ref[...] = dyb
        if not with_delta:
            do_ref[...] = _dot_nt(w_ref[...], dyb).astype(do_ref.dtype)
        else:
            do = _dot_nt(dyb, w_ref[...])
            do_ref[...] = do.astype(do_ref.dtype)
            of = o_ref[...].astype(jnp.float32)
            ones = jnp.ones((8, V_DIM), jnp.float32)
            for hd in range(H):
                sl = slice(hd * V_DIM, (hd + 1) * V_DIM)
                d = lax.dot_general(ones, do[:, sl] * of[:, sl], (((1,), (1,)), ((), ())),
                                    precision=lax.Precision.HIGHEST, preferred_element_type=jnp.float32)
                delta_ref[0][hd] = d[0:1, :]

    out_specs = [_rows(tm, D), _rows(tm, D), _full((8, D))]
    out_shape = [jax.ShapeDtypeStruct((T, D), MXU_DTYPE), jax.ShapeDtypeStruct((T, D), MXU_DTYPE),
                 jax.ShapeDtypeStruct((8, D), jnp.float32)]
    if not with_delta:
        out_specs[1] = pl.BlockSpec((D, tm), lambda i: (0, i))
        out_shape[1] = jax.ShapeDtypeStruct((D, T), MXU_DTYPE)
    if with_delta:
        out_specs.append(pl.BlockSpec((H, 1, tm), lambda i: (0, 0, i)))
        out_shape.append(jax.ShapeDtypeStruct((H, 1, T), jnp.float32))
    return pl.pallas_call(
        body, name="attn_out_bwd_mla" if with_delta else "attn_out_bwd_swa", grid=(T // tm,),
        in_specs=[_rows(tm, D), _rows(tm, D), _full((D, D)), _full((8, D))] + ([_rows(tm, D)] if with_delta else []),
        out_specs=out_specs, out_shape=out_shape,
        compiler_params=_params("arbitrary"),
    )(dxm, y1, w_o, vec, *([o] if with_delta else []))


def _mla_attn_bwd(q, k, v, do, lse, delta):
    H, T, _ = q.shape
    tb = _tile(T, ATTN_BLOCK)
    sub = min(ATTN_SUB, tb)
    ns, nb = tb // sub, T // tb

    pairs = [(j, i) for j in range(nb) for i in range(j, nb)]
    kj_tab = jnp.asarray([j for j, _ in pairs], jnp.int32)
    qi_tab = jnp.asarray([i for _, i in pairs], jnp.int32)

    def body(kj_ref, qi_ref, q_ref, k_ref, v_ref, do_ref, lse_ref, dl_ref, dq_ref, dk_ref, dv_ref, dk_acc, dv_acc):
        j, i = kj_ref[pl.program_id(1)], qi_ref[pl.program_id(1)]

        @pl.when((j == 0) & (i == 0))
        def _():
            dq_ref[...] = jnp.zeros_like(dq_ref)

        def update(kk, r, masked):
            keys, rows = pl.ds(kk * sub, sub), pl.ds(r * sub, sub)
            kb, qb, dob = k_ref[0, keys, :], q_ref[0, rows, :], do_ref[rows, :]
            st = _dot_nt(kb, qb)
            if masked:
                row = lax.broadcasted_iota(jnp.int32, (sub, sub), 0)
                col = lax.broadcasted_iota(jnp.int32, (sub, sub), 1)
                st = jnp.where(row <= col, st, NEG)
            pt = jnp.exp2(st - lse_ref[0, :, rows])
            dv_acc[keys, :] += _dot(pt.astype(MXU_DTYPE), dob)
            dpt = _dot_nt(v_ref[0, keys, :], dob)
            dst = (pt * (dpt - dl_ref[0, :, rows])).astype(MXU_DTYPE)
            dk_acc[keys, :] += _dot(dst, qb)
            q_rows = pl.ds(pl.multiple_of(i * tb + r * sub, sub), sub)
            dq_ref[0, q_rows, :] += _dot_tn(dst, kb)

        @pl.when(i == j)
        def _():
            dk_acc[...] = jnp.zeros_like(dk_acc)
            dv_acc[...] = jnp.zeros_like(dv_acc)
            for r in range(ns):
                for kk in range(r + 1):
                    update(kk, r, kk == r)

        @pl.when(i > j)
        def _():
            for r in range(ns):
                for kk in range(ns):
                    update(kk, r, False)

        @pl.when(i == nb - 1)
        def _():
            dk_ref[0] = (dk_acc[...] * LN2).astype(dk_ref.dtype)
            dv_ref[0] = dv_acc[...].astype(dv_ref.dtype)

    q_idx = lambda h, p, kj_ref, qi_ref: (h, qi_ref[p], 0)
    kv_idx = lambda h, p, kj_ref, qi_ref: (h, kj_ref[p], 0)
    stat_idx = lambda h, p, kj_ref, qi_ref: (h, 0, qi_ref[p])
    return pl.pallas_call(
        body, name="mla_attn_bwd",
        grid_spec=pltpu.PrefetchScalarGridSpec(
            num_scalar_prefetch=2, grid=(H, len(pairs)),
            in_specs=[pl.BlockSpec((1, tb, QK_EXT), q_idx), pl.BlockSpec((1, tb, QK_EXT), kv_idx),
                      pl.BlockSpec((1, tb, V_DIM), kv_idx),
                      pl.BlockSpec((tb, V_DIM), lambda h, p, kj_ref, qi_ref: (qi_ref[p], h)),
                      pl.BlockSpec((1, 1, tb), stat_idx), pl.BlockSpec((1, 1, tb), stat_idx)],
            out_specs=[pl.BlockSpec((1, T, QK_EXT), lambda h, p, kj_ref, qi_ref: (h, 0, 0)),
                       pl.BlockSpec((1, tb, QK_EXT), kv_idx), pl.BlockSpec((1, tb, V_DIM), kv_idx)],
            scratch_shapes=[pltpu.VMEM((tb, QK_EXT), jnp.float32), pltpu.VMEM((tb, V_DIM), jnp.float32)]),
        out_shape=[jax.ShapeDtypeStruct((H, T, QK_EXT), jnp.float32), jax.ShapeDtypeStruct((H, T, QK_EXT), MXU_DTYPE),
                   jax.ShapeDtypeStruct((H, T, V_DIM), MXU_DTYPE)],
        compiler_params=_params("parallel", "arbitrary"),
    )(kj_tab, qi_tab, q, k, v, do, lse, delta)


def _mla_pre_bwd(x, dxm, vec, hb, z, dq, dk, dv, cs, wcat, g_q, g_kv, wuq, wukv):
    T = x.shape[0]
    tm = _tile(T, PROJ_ROWS)
    H = MLA_HEADS
    zc = wcat.shape[1]

    def body(x_ref, dxm_ref, vec_ref, h_ref, z_ref, dq_ref, dk_ref, dv_ref, cs_ref, wcat_ref, gq_ref, gkv_ref,
             wuq_ref, wukv_ref, dx_ref, ps_ref, dgq_ref, dgkv_ref, dwcat_ref, dwuq_ref, dwukv_ref):
        @pl.when(pl.program_id(0) == 0)
        def _():
            for ref in (ps_ref, dgq_ref, dgkv_ref, dwcat_ref, dwuq_ref, dwukv_ref):
                ref[...] = jnp.zeros_like(ref)

        z = z_ref[...]
        cs_t = cs_ref[...]
        cqhat, rq = _rms(z[:, :Q_LORA])
        ckhat, rk = _rms(z[:, Q_LORA:Q_LORA + KV_LORA])
        gq, gkv = gq_ref[...], gkv_ref[...]
        cq = (cqhat * gq).astype(MXU_DTYPE)
        ckv = (ckhat * gkv).astype(MXU_DTYPE)
        dcq = jnp.zeros((tm, Q_LORA), jnp.float32)
        dckv = jnp.zeros((tm, KV_LORA), jnp.float32)
        dkr = jnp.zeros((tm, LANES), jnp.float32)
        for hd in range(H):
            dqh = dq_ref[hd] * MLA_SCALE
            gqh = jnp.concatenate([dqh[:, :QK_NOPE], dqh[:, QK_NOPE:] * cs_t], axis=1).astype(MXU_DTYPE)
            dcq += _dot_nt(gqh, wuq_ref[hd])
            dwuq_ref[hd] += _dot_tn(cq, gqh)
            dkh = dk_ref[hd]
            gkvh = jnp.concatenate([dkh[:, :QK_NOPE], dv_ref[hd]], axis=1)
            dckv += _dot_nt(gkvh, wukv_ref[hd])
            dwukv_ref[hd] += _dot_tn(ckv, gkvh)
            dkr += dkh[:, QK_NOPE:].astype(jnp.float32)
        dgq_ref[...] += _rowsum(dcq * cqhat)
        dgkv_ref[...] += _rowsum(dckv * ckhat)
        dcq_pre = _rms_bwd(dcq * gq, cqhat, rq)
        dckv_pre = _rms_bwd(dckv * gkv, ckhat, rk)
        dkr2 = (dkr + pltpu.roll(dkr, QK_ROPE, axis=1)) * cs_t
        dz = jnp.concatenate([dcq_pre, dckv_pre, dkr2], axis=1).astype(MXU_DTYPE)
        dwcat_ref[...] += _dot_tn(h_ref[...], dz)
        dh = _dot_nt(dz, wcat_ref[...])
        dx_ref[...] = _modulate_bwd(dh, x_ref[...], vec_ref, R_GMIX, R_SC1, R_SH1, ps_ref, dxm_ref[...])

    hblk = lambda w: pl.BlockSpec((H, tm, w), lambda i: (0, i, 0))
    return pl.pallas_call(
        body, name="mla_pre_bwd", grid=(T // tm,),
        in_specs=[_rows(tm, D), _rows(tm, D), _full((8, D)), _rows(tm, D), _rows(tm, zc), hblk(QK_EXT), hblk(QK_EXT),
                  hblk(V_DIM), _rows(tm, LANES), _full(wcat.shape), _full(g_q.shape), _full(g_kv.shape),
                  _full(wuq.shape), _full(wukv.shape)],
        out_specs=[_rows(tm, D), _full((8, D)), _full(g_q.shape), _full(g_kv.shape), _full(wcat.shape),
                   _full(wuq.shape), _full(wukv.shape)],
        out_shape=[jax.ShapeDtypeStruct((T, D), jnp.float32), jax.ShapeDtypeStruct((8, D), jnp.float32),
                   jax.ShapeDtypeStruct(g_q.shape, jnp.float32), jax.ShapeDtypeStruct(g_kv.shape, jnp.float32),
                   jax.ShapeDtypeStruct(wcat.shape, jnp.float32), jax.ShapeDtypeStruct(wuq.shape, jnp.float32),
                   jax.ShapeDtypeStruct(wukv.shape, jnp.float32)],
        compiler_params=_params("arbitrary"),
    )(x, dxm, vec, hb, z, dq, dk, dv, cs, wcat, g_q, g_kv, wuq, wukv)


def _swa_attn_bwd(qt, k, v, dot_, bias, sink_rows):
    T = qt.shape[1]
    W, Dh, G, Hk = WINDOW, SWA_HEAD_DIM, SWA_GROUP, SWA_KV_HEADS
    nk = Hk * Dh
    nb = _swa_blocks(T)

    def body(qt_ref, kp_ref, kc_ref, vp_ref, vc_ref, dot_ref, bias_ref, sink_ref, dqt_ref, dk_ref, dv_ref, dsink_ref):
        n = pl.program_id(0)

        @pl.when(n == 0)
        def _():
            dk_ref[...] = jnp.zeros_like(dk_ref)
            dv_ref[...] = jnp.zeros_like(dv_ref)
            dsink_ref[...] = jnp.zeros_like(dsink_ref)

        def add_rows(first_row, dkb_part, dvb_part):
            rows = pl.ds(pl.multiple_of(first_row, W), W)
            dk_ref[rows, :] += dkb_part
            dv_ref[rows, :] += dvb_part

        up = _swa_fold_mask()
        for b in range(nb):
            q_b, kp_b, kc_b = _swa_views(b, qt_ref, kp_ref, kc_ref)
            do_b, vp_b, vc_b = _swa_views(b, dot_ref, vp_ref, vc_ref)
            dks, dvs = [], []
            for kh in range(Hk):
                qt, kb, pn, p_sink = _swa_probs(True if b else n > 0, up, kh, q_b, kp_b, kc_b, bias_ref, sink_ref)
                vb = jnp.concatenate([vp_b[:, kh * Dh:(kh + 1) * Dh], vc_b[:, kh * Dh:(kh + 1) * Dh]], axis=0)
                dot_h = jnp.concatenate([do_b[(kh * G + g) * Dh:(kh * G + g + 1) * Dh, :] for g in range(G)], axis=1)
                dp = _swa_fold(_dot(vb, dot_h), up)
                delta = jnp.sum(pn * dp, axis=0, keepdims=True)
                dsb = _swa_unfold(pn * (dp - delta), up).astype(MXU_DTYPE)
                dsink_ref[kh] += -p_sink * delta
                dqt = _dot_tn(kb, dsb) * SWA_SCALE
                for g in range(G):
                    dqt_ref[pl.ds((kh * G + g) * Dh, Dh), pl.ds(b * W, W)] = dqt[:, g * W:(g + 1) * W]
                dks.append(_dot_nt(dsb, qt))
                dvs.append(_dot_nt(_swa_unfold(pn, up).astype(MXU_DTYPE), dot_h))
            dkb = jnp.concatenate(dks, axis=1)
            dvb = jnp.concatenate(dvs, axis=1)
            add_rows((n * nb + b) * W, dkb[W:], dvb[W:])
            if b:
                add_rows((n * nb + b - 1) * W, dkb[:W], dvb[:W])
            else:
                @pl.when(n > 0)
                def _():
                    add_rows((n * nb - 1) * W, dkb[:W], dvb[:W])

    prev = lambda n: (jnp.maximum(n * nb - 1, 0), 0)
    cur = lambda n: (n, 0)
    col = lambda n: (0, n)
    return pl.pallas_call(
        body, name="swa_attn_bwd", grid=(T // (nb * W),),
        in_specs=[pl.BlockSpec((D, nb * W), col), pl.BlockSpec((W, nk), prev), pl.BlockSpec((nb * W, nk), cur),
                  pl.BlockSpec((W, nk), prev), pl.BlockSpec((nb * W, nk), cur), pl.BlockSpec((D, nb * W), col),
                  _full(bias.shape), _full(sink_rows.shape)],
        out_specs=[pl.BlockSpec((D, nb * W), col), _full((T, nk)), _full((T, nk)), _full(sink_rows.shape)],
        out_shape=[jax.ShapeDtypeStruct((D, T), jnp.float32), jax.ShapeDtypeStruct((T, nk), jnp.float32),
                   jax.ShapeDtypeStruct((T, nk), jnp.float32), jax.ShapeDtypeStruct(sink_rows.shape, jnp.float32)],
        compiler_params=_params("arbitrary"),
    )(qt, k, k, v, v, dot_, bias, sink_rows)


def _swa_pre_bwd(x, dxm, vec, dq_t, dk, dv, w_qkv):
    T = x.shape[0]
    tm = _tile(T, PROJ_ROWS)
    nq = SWA_HEADS * SWA_HEAD_DIM
    nk = SWA_KV_HEADS * SWA_HEAD_DIM
    nqkv = nq + 2 * nk

    def body(x_ref, dxm_ref, vec_ref, dq_ref, dk_ref, dv_ref, w_ref, dx_ref, dqkv_ref, ps_ref, db_ref):
        @pl.when(pl.program_id(0) == 0)
        def _():
            ps_ref[...] = jnp.zeros_like(ps_ref)
            db_ref[...] = jnp.zeros_like(db_ref)

        dqkv = jnp.concatenate([dq_ref[...].T, dk_ref[...], dv_ref[...]], axis=1)
        db_ref[...] += _rowsum(dqkv)
        dqkv_b = dqkv.astype(MXU_DTYPE)
        dqkv_ref[...] = dqkv_b
        dh = _dot_nt(dqkv_b, w_ref[...])
        dx_ref[...] = _modulate_bwd(dh, x_ref[...], vec_ref, R_GMIX, R_SC1, R_SH1, ps_ref, dxm_ref[...])

    return pl.pallas_call(
        body, name="swa_pre_bwd", grid=(T // tm,),
        in_specs=[_rows(tm, D), _rows(tm, D), _full((8, D)), pl.BlockSpec((nq, tm), lambda i: (0, i)), _rows(tm, nk),
                  _rows(tm, nk), _full(w_qkv.shape)],
        out_specs=[_rows(tm, D), _rows(tm, nqkv), _full((8, D)), _full((1, nqkv))],
        out_shape=[jax.ShapeDtypeStruct((T, D), jnp.float32), jax.ShapeDtypeStruct((T, nqkv), MXU_DTYPE),
                   jax.ShapeDtypeStruct((8, D), jnp.float32), jax.ShapeDtypeStruct((1, nqkv), jnp.float32)],
        compiler_params=_params("arbitrary"),
    )(x, dxm, vec, dq_t, dk, dv, w_qkv)


def _rot_cols(w):
    half = QK_ROPE // 2
    return jnp.concatenate([-w[..., half:], w[..., :half]], axis=-1)


def _unrot_grad(d_rope, d_rot):
    half = QK_ROPE // 2
    return d_rope + jnp.concatenate([d_rot[..., half:], -d_rot[..., :half]], axis=-1)


def _rope_table(positions):
    half = QK_ROPE // 2
    inv_freq = ROPE_THETA ** (-jnp.arange(half, dtype=jnp.float32) / half)
    ang = positions.astype(jnp.float32)[:, None] * inv_freq
    cos, sin = jnp.cos(ang), jnp.sin(ang)
    return jnp.concatenate([cos, cos, sin, sin], axis=1)


def _sequence_step(x, tgt, positions, vecs, g_q, g_kv, sinks, g_final, wts, late_weights, on_late_grads, on_late_landed):
    H = MLA_HEADS
    cs = _rope_table(positions)
    w_dkv = wts["mla_w_dkv"]
    wcat = jnp.concatenate([wts["mla_w_dq"], w_dkv, _rot_cols(w_dkv[:, KV_LORA:])], axis=1)
    uq = wts["mla_w_uq"].reshape(Q_LORA, H, QK_NOPE + QK_ROPE)
    wuq = jnp.concatenate([uq, _rot_cols(uq[..., QK_NOPE:])], axis=-1).transpose(1, 0, 2)
    wukv = wts["mla_w_ukv"].reshape(KV_LORA, H, QK_NOPE + V_DIM).transpose(1, 0, 2)
    zero_bias = jnp.zeros((1, D), jnp.float32)
    bias = _swa_bias()
    sink_rows = jnp.broadcast_to(sinks.reshape(SWA_KV_HEADS, 1, SWA_GROUP, 1),
                                 (SWA_KV_HEADS, 1, SWA_GROUP, WINDOW)).reshape(SWA_KV_HEADS, 1, SWA_GROUP * WINDOW)

    h1a, z, q, k, v = _mla_pre(x, vecs[0], wcat, g_q, g_kv, wuq, wukv, cs)
    o_a, lse = _mla_attn_fwd(q, k, v)
    y1a, xm_a, h2a = _post_attn(o_a, x, wts["mla_w_o"], zero_bias, vecs[0])
    wts = {**wts, **late_weights(h2a)}
    a_a, y2a, x1 = _mlp_fwd(h2a, wts["w_ff1"][0], wts["w_ff2"][0], xm_a, vecs[0])

    h1b, qs_t, ks, vs = _swa_pre(x1, vecs[1], wts["swa_w_qkv"], wts["swa_b_qkv"])
    o_bt = _swa_attn_fwd(qs_t, ks, vs, bias, sink_rows)
    y1b, xm_b, h2b = _post_attn(o_bt, x1, wts["swa_w_o"], wts["swa_b_o"], vecs[1], o_transposed=True)
    a_b, y2b, x2 = _mlp_fwd(h2b, wts["w_ff1"][1], wts["w_ff2"][1], xm_b, vecs[1])

    loss8, dx2, dg_final = _final_loss(x2, tgt, g_final.reshape(1, D))

    du_b, dy2b, dxm_b, ps_mlp_b = _mlp_bwd(dx2, y2b, a_b, wts["w_ff1"][1], wts["w_ff2"][1], xm_b, vecs[1])
    g_ff2 = _mm_tn(a_b, dy2b, "dw_ff2_l1", "rows", DEPTH, 1)
    g_ff1 = _mm_tn(h2b, du_b, "dw_ff1_l1", "cols", DEPTH, 1)
    dy1b, do_bt, ps_out_b = _attn_out_bwd(dxm_b, y1b, None, wts["swa_w_o"], vecs[1], False)
    g_swa_o = _mm_tn(o_bt, dy1b, "dw_o_swa", a_transposed=True)
    dqs_t, dks, dvs, dsinks = _swa_attn_bwd(qs_t, ks, vs, do_bt, bias, sink_rows)
    dx1, dqkv, ps_pre_b, g_swa_bqkv = _swa_pre_bwd(x1, dxm_b, vecs[1], dqs_t, dks, dvs, wts["swa_w_qkv"])
    g_swa_qkv = _mm_tn(h1b, dqkv, "dw_qkv", "cols")

    du_a, dy2a, dxm_a, ps_mlp_a = _mlp_bwd(dx1, y2a, a_a, wts["w_ff1"][0], wts["w_ff2"][0], xm_a, vecs[0])
    g_ff2 = _mm_tn(a_a, dy2a, "dw_ff2_l0", "rows", DEPTH, 0, g_ff2)
    g_ff1 = _mm_tn(h2a, du_a, "dw_ff1_l0", "cols", DEPTH, 0, g_ff1)
    rows4 = lambda g: g.reshape(N_CHIPS, g.shape[0] // N_CHIPS, g.shape[1])
    token = on_late_grads({
        "swa_w_qkv": g_swa_qkv.reshape(N_CHIPS, D, -1), "swa_w_o": rows4(g_swa_o),
        "w_ff1": g_ff1.reshape(N_CHIPS, DEPTH * D, -1), "w_ff2": g_ff2.reshape(N_CHIPS, -1, D)})
    dy1a, do_a, ps_out_a, delta = _attn_out_bwd(dxm_a, y1a, o_a, wts["mla_w_o"], vecs[0] + token[0, 0], True)
    g_mla_o = _mm_tn(o_a, dy1a, "dw_o_mla")
    token = on_late_landed(g_mla_o)
    dq, dk, dv = _mla_attn_bwd(q, k, v, do_a, lse, delta + token[0, 0])
    dx0, ps_pre_a, dg_q, dg_kv, dwcat, dwuq, dwukv = _mla_pre_bwd(
        x, dxm_a, vecs[0], h1a, z, dq, dk, dv, cs, wcat, g_q, g_kv, wuq, wukv)

    c0, c1, c2 = Q_LORA, Q_LORA + KV_LORA, Q_LORA + KV_LORA + QK_ROPE
    g_dq = dwcat[:, :c0]
    g_dkv = jnp.concatenate([dwcat[:, c0:c1], _unrot_grad(dwcat[:, c1:c2], dwcat[:, c2:])], axis=1)
    e0 = QK_NOPE + QK_ROPE
    g_uq = jnp.concatenate([dwuq[..., :QK_NOPE], _unrot_grad(dwuq[..., QK_NOPE:e0], dwuq[..., e0:])], axis=-1)
    per = H // N_CHIPS
    g_uq = g_uq.reshape(N_CHIPS, per, Q_LORA, e0).transpose(0, 2, 1, 3).reshape(N_CHIPS, Q_LORA, per * e0)
    g_ukv = dwukv.reshape(N_CHIPS, per, KV_LORA, QK_NOPE + V_DIM).transpose(0, 2, 1, 3)
    g_ukv = g_ukv.reshape(N_CHIPS, KV_LORA, per * (QK_NOPE + V_DIM))

    def dmod(ps_pre, ps_out, ps_mlp):
        return jnp.concatenate([ps_pre[R_SH1:R_SC1 + 1], ps_out[R_GT1:R_GT1 + 1], ps_mlp[R_SH2:R_GT2 + 1]], axis=0)

    grads = {"mla_w_dq": rows4(g_dq), "mla_w_uq": g_uq, "mla_w_dkv": rows4(g_dkv), "mla_w_ukv": g_ukv,
             "mla_w_o": rows4(g_mla_o)}
    small = {
        "dmod": jnp.stack([dmod(ps_pre_a, ps_out_a, ps_mlp_a), dmod(ps_pre_b, ps_out_b, ps_mlp_b)]).reshape(DEPTH, 6 * D),
        "g_mix": jnp.stack([ps_pre_a[R_GMIX], ps_pre_b[R_GMIX]]),
        "g_mlp": jnp.stack([ps_mlp_a[R_GMLP], ps_mlp_b[R_GMLP]]),
        "mla_g_q": dg_q, "mla_g_kv": dg_kv, "swa_sinks": jnp.sum(dsinks.reshape(SWA_HEADS, WINDOW), axis=1).reshape(1, SWA_HEADS),
        "swa_b_qkv": g_swa_bqkv, "swa_b_o": ps_out_b[R_BO:R_BO + 1],
        "g_final": dg_final.reshape(D), "loss": loss8[0, 0],
    }
    return dx0, grads, small


SHARDED = {
    "mla_w_dq": (1, D // N_CHIPS, Q_LORA),
    "mla_w_uq": (1, Q_LORA, MLA_HEADS * (QK_NOPE + QK_ROPE) // N_CHIPS),
    "mla_w_dkv": (1, D // N_CHIPS, KV_LORA + QK_ROPE),
    "mla_w_ukv": (1, KV_LORA, MLA_HEADS * (QK_NOPE + V_DIM) // N_CHIPS),
    "mla_w_o": (1, MLA_HEADS * V_DIM // N_CHIPS, D),
    "swa_w_qkv": (1, D, (SWA_HEADS + 2 * SWA_KV_HEADS) * SWA_HEAD_DIM // N_CHIPS),
    "swa_w_o": (1, SWA_HEADS * SWA_HEAD_DIM // N_CHIPS, D),
    "w_ff1": (DEPTH, D, D_FF // N_CHIPS),
    "w_ff2": (DEPTH, D_FF // N_CHIPS, D),
}
COL_SPLIT = ("mla_w_uq", "mla_w_ukv", "swa_w_qkv")
BIASES = {"swa_b_qkv": (SWA_HEADS + 2 * SWA_KV_HEADS) * SWA_HEAD_DIM, "swa_b_o": D}


def _view2d(name):
    shape = SHARDED[name]
    return math.prod(shape[:-1]), shape[-1]


SMALL = {"b_ada": (DEPTH, 6 * D), "g_mix": (DEPTH, D), "g_mlp": (DEPTH, D), "mla_g_q": (1, Q_LORA),
         "mla_g_kv": (1, KV_LORA), "swa_sinks": (1, SWA_HEADS), "g_final": (D,), "loss": (),
         "swa_b_qkv": (1, BIASES["swa_b_qkv"]), "swa_b_o": (1, BIASES["swa_b_o"])}
SMALL_ROWS = 192
DMA_ROWS = 256


SLOT_ROWS = 8


def _small_slots():
    slots, off = {}, 0
    for name, shape in SMALL.items():
        n = max(math.prod(shape), 1)
        slots[name] = (off, n)
        off += -(-n // (SLOT_ROWS * LANES)) * SLOT_ROWS * LANES
    assert off <= SMALL_ROWS * LANES
    return slots


def _pack_small(vals):
    parts, end = [], 0
    for name, (off, n) in _small_slots().items():
        pad = -(-n // (SLOT_ROWS * LANES)) * SLOT_ROWS * LANES - n
        v = vals[name].astype(jnp.float32).reshape(-1) if name in vals else jnp.zeros((n,), jnp.float32)
        parts += [v, jnp.zeros((pad,), jnp.float32)]
        end = off + n + pad
    parts.append(jnp.zeros((SMALL_ROWS * LANES - end,), jnp.float32))
    return jnp.concatenate(parts).reshape(SMALL_ROWS, LANES)


def _from_slot(name, rows):
    n = max(math.prod(SMALL[name]), 1)
    return rows.reshape(-1)[:n].reshape(SMALL[name])


def _pieces(rows):
    return [(off, min(DMA_ROWS, rows - off)) for off in range(0, rows, DMA_ROWS)]


HBM = pl.BlockSpec(memory_space=pltpu.HBM)
MESH = pl.DeviceIdType.MESH


def _place():
    x, y, c = lax.axis_index("x"), lax.axis_index("y"), lax.axis_index("c")
    chips = [(1 - x, y), (x, 1 - y), (1 - x, 1 - y)]
    return x, y, c, chips


def _all_gather(block):
    m_per, n = block.shape

    def body(x_ref, out_ref, send_sems, recv_sems, local_sem):
        x, y, c, chips = _place()
        me, sibling = (x, y, c), (x, y, 1 - c)

        def rows(px, py, pc):
            return out_ref.at[pl.ds((4 * px + 2 * py + pc) * m_per, m_per), :]

        def copy(k, blk, to, src=None):
            return pltpu.make_async_remote_copy(
                src_ref=rows(*blk) if src is None else src, dst_ref=rows(*blk),
                send_sem=send_sems.at[k], recv_sem=recv_sems.at[k], device_id=to, device_id_type=MESH)

        mine = pltpu.make_async_copy(x_ref, rows(*me), local_sem)
        mine.start()
        first = [copy(0, me, sibling, src=x_ref)]
        first += [copy(1 + j, me, (*chip, c), src=x_ref) for j, chip in enumerate(chips)]
        for cp in first:
            cp.start()
        passed = [copy(4 + j, (*chip, c), sibling) for j, chip in enumerate(chips)]
        for j, chip in enumerate(chips):
            copy(1 + j, (*chip, c), me).wait_recv()
            passed[j].start()
        copy(0, sibling, me).wait_recv()
        for j, chip in enumerate(chips):
            copy(4 + j, (*chip, 1 - c), me).wait_recv()
        for cp in first + passed:
            cp.wait_send()
        mine.wait()

    out = pl.pallas_call(
        body, name="all_gather_small",
        out_shape=jax.ShapeDtypeStruct((N_DEV * m_per, n), block.dtype),
        in_specs=[pl.BlockSpec(memory_space=pltpu.VMEM)],
        out_specs=pl.BlockSpec(memory_space=pltpu.VMEM),
        scratch_shapes=[pltpu.SemaphoreType.DMA((7,)), pltpu.SemaphoreType.DMA((7,)), pltpu.SemaphoreType.DMA],
    )(block)
    return out.reshape(N_DEV, m_per, n)


def _weight_gather(shards):
    nt = len(shards)

    def body(*refs):
        w_refs, out_refs = refs[:nt], refs[nt:2 * nt]
        send_sems, recv_sems = refs[2 * nt:]
        x, y, c, chips = _place()
        sibling = (x, y, 1 - c)

        def slab(t, px, py, half):
            rh = shards[t].shape[0] // 2
            return out_refs[t].at[2 * px + py, pl.ds(half * rh, rh), :]

        def copy(t, k, src, dst, to):
            return pltpu.make_async_remote_copy(src_ref=src, dst_ref=dst, send_sem=send_sems.at[6 * t + k],
                                                recv_sem=recv_sems.at[6 * t + k], device_id=to, device_id_type=MESH)

        first = []
        for t in range(nt):
            rh = shards[t].shape[0] // 2
            first += [copy(t, j, w_refs[t].at[pl.ds(c * rh, rh), :], slab(t, x, y, c), (*chip, c))
                      for j, chip in enumerate(chips)]
        for cp in first:
            cp.start()
        passed = []
        for t in range(nt):
            for j, chip in enumerate(chips):
                copy(t, j, slab(t, *chip, c), slab(t, *chip, c), (*chip, c)).wait_recv()
                rh = shards[t].shape[0] // 2
                for off, n in _pieces(rh):
                    piece = out_refs[t].at[2 * chip[0] + chip[1], pl.ds(c * rh + off, n), :]
                    copy(t, 3 + j, piece, piece, sibling).start()
                passed.append(copy(t, 3 + j, slab(t, *chip, c), slab(t, *chip, c), sibling))
        for t in range(nt):
            for j, chip in enumerate(chips):
                copy(t, 3 + j, slab(t, *chip, 1 - c), slab(t, *chip, 1 - c), sibling).wait_recv()
        for cp in first + passed:
            cp.wait_send()

    return pl.pallas_call(
        body, name="weight_gather",
        out_shape=[jax.ShapeDtypeStruct((N_CHIPS,) + s.shape, s.dtype) for s in shards],
        in_specs=[HBM] * nt, out_specs=[HBM] * nt,
        scratch_shapes=[pltpu.SemaphoreType.DMA((6 * nt,)), pltpu.SemaphoreType.DMA((6 * nt,))],
    )(*shards)


SEM = pl.BlockSpec(memory_space=pltpu.SEMAPHORE)
ANY = pl.BlockSpec(memory_space=pl.ANY)
SPLIT_COPY = pltpu.SideEffectType.DATAFLOW_SIDE_EFFECTING


def _late_copies(w_refs, land_refs, send_sems, recv_sems):
    x, y, c, chips = _place()
    return [pltpu.make_async_remote_copy(
        src_ref=w_refs[t], dst_ref=land_refs[t].at[2 * x + y], send_sem=send_sems.at[3 * t + j],
        recv_sem=recv_sems.at[3 * t + j], device_id=(cx, cy, c), device_id_type=MESH)
        for t in range(len(w_refs)) for j, (cx, cy) in enumerate(chips)], chips


def _late_gather_start(shards, after):
    nt, na = len(shards), len(after)

    def body(*refs):
        w_refs, land_refs = refs[:nt], refs[nt:2 * nt]
        send_sems, recv_sems, token = refs[2 * nt + na], refs[2 * nt + na + 1], refs[-1]
        copies, _ = _late_copies(w_refs, land_refs, send_sems, recv_sems)
        for cp in copies:
            cp.start()
        token[...] = jnp.zeros_like(token)

    hbm = lambda a: pltpu.with_memory_space_constraint(a, pltpu.HBM)
    lands = [lax.empty((N_CHIPS,) + s.shape, s.dtype) for s in shards]
    outs = pl.pallas_call(
        body, name="late_gather_start",
        out_shape=(pltpu.SemaphoreType.DMA((3 * nt,)), pltpu.SemaphoreType.DMA((3 * nt,)),
                   *[pltpu.HBM(s.shape, s.dtype) for s in shards], *[pltpu.HBM(l.shape, l.dtype) for l in lands],
                   jax.ShapeDtypeStruct((8, LANES), jnp.float32)),
        in_specs=[HBM] * (2 * nt) + [ANY] * na,
        out_specs=(SEM, SEM, *([HBM] * (2 * nt)), pl.BlockSpec(memory_space=pltpu.VMEM)),
        input_output_aliases={i: 2 + i for i in range(2 * nt)},
        compiler_params=pltpu.CompilerParams(has_side_effects=SPLIT_COPY),
    )(*[hbm(s) for s in shards], *[hbm(l) for l in lands], *after)
    return outs[0], outs[1], list(outs[2:2 + nt]), list(outs[2 + nt:2 + 2 * nt]), outs[-1]


def _late_gather_wait(send_sems, recv_sems, shards, lands, after):
    nt = len(shards)

    def body(*refs):
        w_refs, land_refs = refs[:nt], refs[nt:2 * nt]
        s_sems, r_sems = refs[2 * nt], refs[2 * nt + 1]
        x, y, c, chips = _place()
        for t in range(nt):
            for j, (cx, cy) in enumerate(chips):
                cp = pltpu.make_async_remote_copy(
                    src_ref=w_refs[t], dst_ref=land_refs[t].at[2 * cx + cy], send_sem=s_sems.at[3 * t + j],
                    recv_sem=r_sems.at[3 * t + j], device_id=(cx, cy, c), device_id_type=MESH)
                cp.wait_send()
                cp.wait_recv()

    outs = pl.pallas_call(
        body, name="late_gather_wait",
        out_shape=(*[pltpu.HBM(s.shape, s.dtype) for s in shards], *[pltpu.HBM(l.shape, l.dtype) for l in lands]),
        in_specs=[HBM] * (2 * nt) + [SEM, SEM, ANY], out_specs=tuple([HBM] * (2 * nt)),
        input_output_aliases={i: i for i in range(2 * nt)},
        compiler_params=pltpu.CompilerParams(has_side_effects=SPLIT_COPY),
    )(*shards, *lands, send_sems, recv_sems, after)
    return list(outs[nt:])


def _grad_pair_in(grads, behind):
    nt = len(grads)

    def body(*refs):
        g_refs, got_refs = refs[:nt], refs[nt + 1:2 * nt + 1]
        send_sems, recv_sems = refs[2 * nt + 1:]
        x, y, c, _ = _place()
        sibling = (x, y, 1 - c)

        def copy(t, src, dst):
            return pltpu.make_async_remote_copy(src_ref=src, dst_ref=dst, send_sem=send_sems.at[t],
                                                recv_sem=recv_sems.at[t], device_id=sibling, device_id_type=MESH)

        for t in range(nt):
            rh = grads[t].shape[1] // 2
            for p in range(N_CHIPS):
                for off, n in _pieces(rh):
                    copy(t, g_refs[t].at[p, pl.ds((1 - c) * rh + off, n), :], got_refs[t].at[p, pl.ds(off, n), :]).start()
        for t in range(nt):
            rh = grads[t].shape[1] // 2
            copy(t, g_refs[t].at[:, pl.ds((1 - c) * rh, rh), :], got_refs[t]).wait()

    return pl.pallas_call(
        body, name="grad_pair_in",
        out_shape=[jax.ShapeDtypeStruct((N_CHIPS, g.shape[1] // 2, g.shape[2]), g.dtype) for g in grads],
        in_specs=[HBM] * nt + [ANY], out_specs=[HBM] * nt,
        scratch_shapes=[pltpu.SemaphoreType.DMA((nt,)), pltpu.SemaphoreType.DMA((nt,))],
    )(*grads, behind)


def _pair_in_start(grads):
    nt = len(grads)

    def body(*refs):
        g_refs, land_refs = refs[:nt], refs[nt:2 * nt]
        send_sems, recv_sems, token = refs[2 * nt], refs[2 * nt + 1], refs[-1]
        x, y, c, _ = _place()
        for t in range(nt):
            rh = grads[t].shape[1] // 2
            for p in range(N_CHIPS):
                for off, n in _pieces(rh):
                    pltpu.make_async_remote_copy(
                        src_ref=g_refs[t].at[p, pl.ds((1 - c) * rh + off, n), :], dst_ref=land_refs[t].at[p, pl.ds(off, n), :],
                        send_sem=send_sems.at[t], recv_sem=recv_sems.at[t], device_id=(x, y, 1 - c),
                        device_id_type=MESH).start()
        token[...] = jnp.zeros_like(token)

    hbm = lambda a: pltpu.with_memory_space_constraint(a, pltpu.HBM)
    lands = [lax.empty((N_CHIPS, g.shape[1] // 2, g.shape[2]), g.dtype) for g in grads]
    outs = pl.pallas_call(
        body, name="grad_pair_in_start",
        out_shape=(pltpu.SemaphoreType.DMA((nt,)), pltpu.SemaphoreType.DMA((nt,)),
                   *[pltpu.HBM(g.shape, g.dtype) for g in grads], *[pltpu.HBM(l.shape, l.dtype) for l in lands],
                   jax.ShapeDtypeStruct((8, LANES), jnp.float32)),
        in_specs=[HBM] * (2 * nt),
        out_specs=(SEM, SEM, *([HBM] * (2 * nt)), pl.BlockSpec(memory_space=pltpu.VMEM)),
        input_output_aliases={i: 2 + i for i in range(2 * nt)},
        compiler_params=pltpu.CompilerParams(has_side_effects=SPLIT_COPY),
    )(*[hbm(g) for g in grads], *[hbm(l) for l in lands])
    return outs[0], outs[1], list(outs[2:2 + nt]), list(outs[2 + nt:2 + 2 * nt]), outs[-1]


def _pair_in_wait(send_sems, recv_sems, grads, lands, after):
    nt = len(grads)

    def body(*refs):
        g_refs, land_refs = refs[:nt], refs[nt:2 * nt]
        s_sems, r_sems = refs[2 * nt], refs[2 * nt + 1]
        x, y, c, _ = _place()
        for t in range(nt):
            rh = grads[t].shape[1] // 2
            cp = pltpu.make_async_remote_copy(
                src_ref=g_refs[t].at[:, pl.ds((1 - c) * rh, rh), :], dst_ref=land_refs[t], send_sem=s_sems.at[t],
                recv_sem=r_sems.at[t], device_id=(x, y, 1 - c), device_id_type=MESH)
            cp.wait_send()
            cp.wait_recv()

    outs = pl.pallas_call(
        body, name="grad_pair_in_wait",
        out_shape=(*[pltpu.HBM(g.shape, g.dtype) for g in grads], *[pltpu.HBM(l.shape, l.dtype) for l in lands]),
        in_specs=[HBM] * (2 * nt) + [SEM, SEM, ANY], out_specs=tuple([HBM] * (2 * nt)),
        input_output_aliases={i: i for i in range(2 * nt)},
        compiler_params=pltpu.CompilerParams(has_side_effects=SPLIT_COPY),
    )(*grads, *lands, send_sems, recv_sems, after)
    return list(outs[:nt]), list(outs[nt:])


def _pair_sum(g, got, core, name):
    _, rows, cols = g.shape
    rh = rows // 2
    tr = _tile(rh, 512)
    nb = rh // tr

    def body(c_ref, g_ref, got_ref, s16_ref):
        s16_ref[...] = (g_ref[...] + got_ref[...]).astype(s16_ref.dtype)

    blk = pl.BlockSpec((None, tr, cols), lambda p, i, c_ref: (p, i, 0))
    return pl.pallas_call(
        body, name=name,
        grid_spec=pltpu.PrefetchScalarGridSpec(
            num_scalar_prefetch=1, grid=(N_CHIPS, nb),
            in_specs=[pl.BlockSpec((None, tr, cols), lambda p, i, c_ref: (p, c_ref[0] * nb + i, 0)), blk],
            out_specs=blk),
        out_shape=jax.ShapeDtypeStruct((N_CHIPS, rh, cols), jnp.bfloat16),
        compiler_params=_params("parallel", "parallel"),
    )(core, g, got)


def _exchange_start(parts, name):
    nt = len(parts)

    def body(*refs):
        a_refs, land_refs = refs[:nt], refs[nt:2 * nt]
        send_sems, recv_sems, token = refs[2 * nt], refs[2 * nt + 1], refs[-1]
        x, y, c, chips = _place()
        for t in range(nt):
            for j, (cx, cy) in enumerate(chips):
                pltpu.make_async_remote_copy(
                    src_ref=a_refs[t].at[2 * cx + cy], dst_ref=land_refs[t].at[j], send_sem=send_sems.at[3 * t + j],
                    recv_sem=recv_sems.at[3 * t + j], device_id=(cx, cy, c), device_id_type=MESH).start()
        token[...] = jnp.zeros_like(token)

    hbm = lambda a: pltpu.with_memory_space_constraint(a, pltpu.HBM)
    lands = [lax.empty((N_CHIPS - 1,) + a.shape[1:], a.dtype) for a in parts]
    outs = pl.pallas_call(
        body, name=name,
        out_shape=(pltpu.SemaphoreType.DMA((3 * nt,)), pltpu.SemaphoreType.DMA((3 * nt,)),
                   *[pltpu.HBM(a.shape, a.dtype) for a in parts], *[pltpu.HBM(l.shape, l.dtype) for l in lands],
                   jax.ShapeDtypeStruct((8, LANES), jnp.float32)),
        in_specs=[HBM] * (2 * nt),
        out_specs=(SEM, SEM, *([HBM] * (2 * nt)), pl.BlockSpec(memory_space=pltpu.VMEM)),
        input_output_aliases={i: 2 + i for i in range(2 * nt)},
        compiler_params=pltpu.CompilerParams(has_side_effects=SPLIT_COPY),
    )(*[hbm(a) for a in parts], *[hbm(l) for l in lands])
    return outs[0], outs[1], list(outs[2:2 + nt]), list(outs[2 + nt:2 + 2 * nt]), outs[-1]


def _exchange_wait(send_sems, recv_sems, parts, lands, after, name):
    nt = len(parts)

    def body(*refs):
        a_refs, land_refs = refs[:nt], refs[nt:2 * nt]
        s_sems, r_sems = refs[2 * nt], refs[2 * nt + 1]
        x, y, c, chips = _place()
        for t in range(nt):
            for j, (cx, cy) in enumerate(chips):
                cp = pltpu.make_async_remote_copy(
                    src_ref=a_refs[t].at[2 * cx + cy], dst_ref=land_refs[t].at[j], send_sem=s_sems.at[3 * t + j],
                    recv_sem=r_sems.at[3 * t + j], device_id=(cx, cy, c), device_id_type=MESH)
                cp.wait_send()
                cp.wait_recv()

    outs = pl.pallas_call(
        body, name=name,
        out_shape=(*[pltpu.HBM(a.shape, a.dtype) for a in parts], *[pltpu.HBM(l.shape, l.dtype) for l in lands]),
        in_specs=[HBM] * (2 * nt) + [SEM, SEM, ANY], out_specs=tuple([HBM] * (2 * nt)),
        input_output_aliases={i: i for i in range(2 * nt)},
        compiler_params=pltpu.CompilerParams(has_side_effects=SPLIT_COPY),
    )(*parts, *lands, send_sems, recv_sems, after)
    return list(outs[nt:])


def _chip_sum(g, got_pair, got_chips, core, chip, name):
    _, rh, cols = got_pair.shape
    tr = _tile(rh, STREAM_ROWS)
    nb = rh // tr

    def body(c_ref, p_ref, g_ref, pair_ref, chips_ref, o_ref):
        acc = g_ref[...] + pair_ref[...]
        for j in range(N_CHIPS - 1):
            acc = acc + chips_ref[j].astype(jnp.float32)
        o_ref[...] = acc

    return pl.pallas_call(
        body, name=name,
        grid_spec=pltpu.PrefetchScalarGridSpec(
            num_scalar_prefetch=2, grid=(nb,),
            in_specs=[pl.BlockSpec((None, tr, cols), lambda i, c_ref, p_ref: (p_ref[0], c_ref[0] * nb + i, 0)),
                      pl.BlockSpec((None, tr, cols), lambda i, c_ref, p_ref: (p_ref[0], i, 0)),
                      pl.BlockSpec((N_CHIPS - 1, tr, cols), lambda i, c_ref, p_ref: (0, i, 0))],
            out_specs=pl.BlockSpec((tr, cols), lambda i, c_ref, p_ref: (i, 0))),
        out_shape=jax.ShapeDtypeStruct((rh, cols), jnp.float32),
        compiler_params=_params("parallel"),
    )(core, chip, g, got_pair, got_chips)


def _pair_out_start(halves):
    nt = len(halves)

    def body(*refs):
        h_refs, land_refs = refs[:nt], refs[nt:2 * nt]
        send_sems, recv_sems, token = refs[2 * nt], refs[2 * nt + 1], refs[-1]
        x, y, c, _ = _place()
        for t in range(nt):
            for off, n in _pieces(halves[t].shape[0]):
                pltpu.make_async_remote_copy(
                    src_ref=h_refs[t].at[pl.ds(off, n), :], dst_ref=land_refs[t].at[pl.ds(off, n), :],
                    send_sem=send_sems.at[t], recv_sem=recv_sems.at[t], device_id=(x, y, 1 - c),
                    device_id_type=MESH).start()
        token[...] = jnp.zeros_like(token)

    hbm = lambda a: pltpu.with_memory_space_constraint(a, pltpu.HBM)
    lands = [lax.empty(h.shape, h.dtype) for h in halves]
    outs = pl.pallas_call(
        body, name="grad_pair_out_start",
        out_shape=(pltpu.SemaphoreType.DMA((nt,)), pltpu.SemaphoreType.DMA((nt,)),
                   *[pltpu.HBM(h.shape, h.dtype) for h in halves], *[pltpu.HBM(l.shape, l.dtype) for l in lands],
                   jax.ShapeDtypeStruct((8, LANES), jnp.float32)),
        in_specs=[HBM] * (2 * nt),
        out_specs=(SEM, SEM, *([HBM] * (2 * nt)), pl.BlockSpec(memory_space=pltpu.VMEM)),
        input_output_aliases={i: 2 + i for i in range(2 * nt)},
        compiler_params=pltpu.CompilerParams(has_side_effects=SPLIT_COPY),
    )(*[hbm(h) for h in halves], *[hbm(l) for l in lands])
    return outs[0], outs[1], list(outs[2:2 + nt]), list(outs[2 + nt:2 + 2 * nt]), outs[-1]


def _pair_out_wait(send_sems, recv_sems, halves, lands, after):
    nt = len(halves)

    def body(*refs):
        h_refs, land_refs = refs[:nt], refs[nt:2 * nt]
        s_sems, r_sems = refs[2 * nt], refs[2 * nt + 1]
        x, y, c, _ = _place()
        for t in range(nt):
            cp = pltpu.make_async_remote_copy(
                src_ref=h_refs[t], dst_ref=land_refs[t], send_sem=s_sems.at[t], recv_sem=r_sems.at[t],
                device_id=(x, y, 1 - c), device_id_type=MESH)
            cp.wait_send()
            cp.wait_recv()

    outs = pl.pallas_call(
        body, name="grad_pair_out_wait",
        out_shape=(*[pltpu.HBM(h.shape, h.dtype) for h in halves], *[pltpu.HBM(l.shape, l.dtype) for l in lands]),
        in_specs=[HBM] * (2 * nt) + [SEM, SEM, ANY], out_specs=tuple([HBM] * (2 * nt)),
        input_output_aliases={i: i for i in range(2 * nt)},
        compiler_params=pltpu.CompilerParams(has_side_effects=SPLIT_COPY),
    )(*halves, *lands, send_sems, recv_sems, after)
    return list(outs[:nt]), list(outs[nt:])


def _grad_pair_out(halves):
    nt = len(halves)

    def body(*refs):
        h_refs, got_refs = refs[:nt], refs[nt:2 * nt]
        send_sems, recv_sems = refs[2 * nt:]
        x, y, c, _ = _place()
        sibling = (x, y, 1 - c)

        def copy(t, src, dst):
            return pltpu.make_async_remote_copy(src_ref=src, dst_ref=dst, send_sem=send_sems.at[t],
                                                recv_sem=recv_sems.at[t], device_id=sibling, device_id_type=MESH)

        for t in range(nt):
            for off, n in _pieces(halves[t].shape[0]):
                copy(t, h_refs[t].at[pl.ds(off, n), :], got_refs[t].at[pl.ds(off, n), :]).start()
        for t in range(nt):
            copy(t, h_refs[t], got_refs[t]).wait()

    return pl.pallas_call(
        body, name="grad_pair_out",
        out_shape=[jax.ShapeDtypeStruct(h.shape, h.dtype) for h in halves],
        in_specs=[HBM] * nt, out_specs=[HBM] * nt,
        scratch_shapes=[pltpu.SemaphoreType.DMA((nt,)), pltpu.SemaphoreType.DMA((nt,))],
    )(*halves)


def _ada_part(c_all, w_ada):
    L, _, ncol = w_ada.shape
    tn = _tile(ncol, 512)

    def body(c_ref, w_ref, cond_ref, part_ref):
        cv = c_ref[...]
        cond = cv * jax.nn.sigmoid(cv)
        cond_ref[...] = cond
        part_ref[0] = jnp.dot(cond, w_ref[0], precision=lax.Precision.HIGHEST, preferred_element_type=jnp.float32)

    return pl.pallas_call(
        body, name="ada_part", grid=(L, ncol // tn),
        in_specs=[_full((N_DEV, D)), pl.BlockSpec((1, D, tn), lambda l, j: (l, 0, j))],
        out_specs=[_full((N_DEV, D)), pl.BlockSpec((1, N_DEV, tn), lambda l, j: (l, 0, j))],
        out_shape=[jax.ShapeDtypeStruct((N_DEV, D), jnp.float32), jax.ShapeDtypeStruct((L, N_DEV, ncol), jnp.float32)],
        compiler_params=_params("arbitrary", "arbitrary"),
    )(c_all, w_ada)


def _adamw_math(w, g, m, v):
    m = ADAM_B1 * m + (1.0 - ADAM_B1) * g
    v = ADAM_B2 * v + (1.0 - ADAM_B2) * jnp.square(g)
    m_hat = m / (1.0 - ADAM_B1 ** ADAM_STEP)
    v_hat = v / (1.0 - ADAM_B2 ** ADAM_STEP)
    delta = -ADAM_LR * (m_hat / (jnp.sqrt(v_hat) + ADAM_EPS) + ADAM_WD * w)
    return delta, m, v


def _adamw(w, g, m, v, name):
    shape = w.shape
    cols = shape[-1]
    rows = math.prod(shape[:-1])
    tr = _tile(rows, 512)
    two_d = lambda t: t.reshape(rows, cols)

    def body(w_ref, g_ref, m_ref, v_ref, d_ref, mo_ref, vo_ref):
        d_ref[...], mo_ref[...], vo_ref[...] = _adamw_math(w_ref[...], g_ref[...], m_ref[...], v_ref[...])

    out = jax.ShapeDtypeStruct((rows, cols), jnp.float32)
    outs = pl.pallas_call(
        body, name=name, grid=(rows // tr,), in_specs=[_rows(tr, cols)] * 4, out_specs=[_rows(tr, cols)] * 3,
        out_shape=[out, out, out], compiler_params=_params("parallel"),
    )(two_d(w), two_d(g), two_d(m), two_d(v))
    return [t.reshape(shape) for t in outs]


def _adamw_halves(w, mine, got, m, v, core, name):
    shape = w.shape
    cols = shape[-1]
    rows = math.prod(shape[:-1])
    rh = rows // 2
    tr = _tile(rh, STREAM_ROWS)
    nbh = rh // tr
    two_d = lambda t: t.reshape(rows, cols)

    def body(c_ref, w_ref, a_ref, b_ref, m_ref, v_ref, g_ref, d_ref, mo_ref, vo_ref):
        g = jnp.where(pl.program_id(0) // nbh == c_ref[0], a_ref[...], b_ref[...])
        g_ref[...] = g
        d_ref[...], mo_ref[...], vo_ref[...] = _adamw_math(w_ref[...], g, m_ref[...], v_ref[...])

    row = pl.BlockSpec((tr, cols), lambda i, c_ref: (i, 0))

    def half(keep):
        return pl.BlockSpec((tr, cols), lambda i, c_ref: (jnp.where((i // nbh == c_ref[0]) == keep, i % nbh, 0), 0))

    out = jax.ShapeDtypeStruct((rows, cols), jnp.float32)
    outs = pl.pallas_call(
        body, name=name,
        grid_spec=pltpu.PrefetchScalarGridSpec(
            num_scalar_prefetch=1, grid=(rows // tr,),
            in_specs=[row, half(True), half(False), row, row], out_specs=[row] * 4),
        out_shape=[out] * 4, compiler_params=_params("arbitrary"),
    )(core, two_d(w), mine, got, two_d(m), two_d(v))
    return [t.reshape(shape) for t in outs]


def _ada_grad_adamw(cond_t, dm, w, m, v):
    L, _, ncol = w.shape
    tn = _tile(ncol, STREAM_ROWS)

    def body(ct_ref, dm_ref, w_ref, m_ref, v_ref, g_ref, d_ref, mo_ref, vo_ref):
        g = ct_ref[:, 0:1] * dm_ref[0, 0:1, :]
        for b in range(1, N_DEV):
            g = g + ct_ref[:, b:b + 1] * dm_ref[0, b:b + 1, :]
        g_ref[0] = g
        d_ref[0], mo_ref[0], vo_ref[0] = _adamw_math(w_ref[0], g, m_ref[0], v_ref[0])

    wblk = pl.BlockSpec((1, D, tn), lambda l, j: (l, 0, j))
    out = jax.ShapeDtypeStruct(w.shape, jnp.float32)
    return pl.pallas_call(
        body, name="ada_grad_adamw", grid=(L, ncol // tn),
        in_specs=[_full((D, N_DEV)), pl.BlockSpec((1, N_DEV, tn), lambda l, j: (l, 0, j)), wblk, wblk, wblk],
        out_specs=[wblk] * 4, out_shape=[out] * 4, compiler_params=_params("parallel", "parallel"),
    )(cond_t, dm, w, m, v)


def _small_adamw(gathered, w, m, v):
    slots = _small_slots()
    rows = {name: (off // LANES, -(-n // LANES)) for name, (off, n) in slots.items()}
    kinds = {name: 1 if name == "loss" or name in BIASES else 4 for name in slots}

    def body(ga_ref, w_ref, m_ref, v_ref, *out_refs):
        g = ga_ref[0]
        for dev in range(1, N_DEV):
            g = g + ga_ref[dev]
        d, mo, vo = _adamw_math(w_ref[...], g, m_ref[...], v_ref[...])
        k = 0
        for name, (r0, nr) in rows.items():
            for src in (g, d, mo, vo)[:kinds[name]]:
                out_refs[k][...] = src[r0:r0 + nr, :]
                k += 1

    out_shape = [jax.ShapeDtypeStruct((rows[name][1], LANES), jnp.float32) for name in slots for _ in range(kinds[name])]
    flat = pl.pallas_call(
        body, name="small_adamw", out_shape=out_shape,
        in_specs=[pl.BlockSpec(memory_space=pltpu.VMEM)] * 4,
        out_specs=[pl.BlockSpec(memory_space=pltpu.VMEM)] * len(out_shape),
    )(gathered, w, m, v)
    out, k = {}, 0
    for name in slots:
        out[name] = [_from_slot(name, t) for t in flat[k:k + kinds[name]]]
        k += kinds[name]
    return out


def _one_hot_pick(arr, index, axis):
    n = arr.shape[axis]
    shape = [1] * arr.ndim
    shape[axis] = n
    hot = (jnp.arange(n) == index).astype(arr.dtype).reshape(shape)
    return jnp.sum(arr * hot, axis=axis)


def kernel(x, c, positions, w_ada, b_ada, g_mix, g_mlp, mla_w_dq, mla_g_q, mla_w_uq, mla_w_dkv, mla_g_kv, mla_w_ukv, mla_w_o, swa_w_qkv, swa_b_qkv, swa_sinks, swa_w_o, swa_b_o, w_ff1, w_ff2, g_final, loss_target, m_w_ada, m_b_ada, m_g_mix, m_g_mlp, m_mla_w_dq, m_mla_g_q, m_mla_w_uq, m_mla_w_dkv, m_mla_g_kv, m_mla_w_ukv, m_mla_w_o, m_swa_w_qkv, m_swa_b_qkv, m_swa_sinks, m_swa_w_o, m_swa_b_o, m_w_ff1, m_w_ff2, m_g_final, v_w_ada, v_b_ada, v_g_mix, v_g_mlp, v_mla_w_dq, v_mla_g_q, v_mla_w_uq, v_mla_w_dkv, v_mla_g_kv, v_mla_w_ukv, v_mla_w_o, v_swa_w_qkv, v_swa_b_qkv, v_swa_sinks, v_swa_w_o, v_swa_b_o, v_w_ff1, v_w_ff2, v_g_final):
    W = dict(w_ada=w_ada, b_ada=b_ada, g_mix=g_mix, g_mlp=g_mlp, mla_w_dq=mla_w_dq, mla_g_q=mla_g_q, mla_w_uq=mla_w_uq,
             mla_w_dkv=mla_w_dkv, mla_g_kv=mla_g_kv, mla_w_ukv=mla_w_ukv, mla_w_o=mla_w_o, swa_w_qkv=swa_w_qkv,
             swa_b_qkv=swa_b_qkv, swa_sinks=swa_sinks, swa_w_o=swa_w_o, swa_b_o=swa_b_o, w_ff1=w_ff1, w_ff2=w_ff2,
             g_final=g_final)
    M = dict(w_ada=m_w_ada, b_ada=m_b_ada, g_mix=m_g_mix, g_mlp=m_g_mlp, mla_w_dq=m_mla_w_dq, mla_g_q=m_mla_g_q,
             mla_w_uq=m_mla_w_uq, mla_w_dkv=m_mla_w_dkv, mla_g_kv=m_mla_g_kv, mla_w_ukv=m_mla_w_ukv, mla_w_o=m_mla_w_o,
             swa_w_qkv=m_swa_w_qkv, swa_b_qkv=m_swa_b_qkv, swa_sinks=m_swa_sinks, swa_w_o=m_swa_w_o, swa_b_o=m_swa_b_o,
             w_ff1=m_w_ff1, w_ff2=m_w_ff2, g_final=m_g_final)
    V = dict(w_ada=v_w_ada, b_ada=v_b_ada, g_mix=v_g_mix, g_mlp=v_g_mlp, mla_w_dq=v_mla_w_dq, mla_g_q=v_mla_g_q,
             mla_w_uq=v_mla_w_uq, mla_w_dkv=v_mla_w_dkv, mla_g_kv=v_mla_g_kv, mla_w_ukv=v_mla_w_ukv, mla_w_o=v_mla_w_o,
             swa_w_qkv=v_swa_w_qkv, swa_b_qkv=v_swa_b_qkv, swa_sinks=v_swa_sinks, swa_w_o=v_swa_w_o, swa_b_o=v_swa_b_o,
             w_ff1=v_w_ff1, w_ff2=v_w_ff2, g_final=v_g_final)
    order = list(W)
    names = list(SHARDED)
    core = lax.axis_index("c")
    chip = 2 * lax.axis_index("x") + lax.axis_index("y")
    dev = 2 * chip + core
    core_arr = core.astype(jnp.int32).reshape(1)
    chip_arr = chip.astype(jnp.int32).reshape(1)

    def whole(n, g, own):
        g = lax.dynamic_update_slice(g, own[None], (chip, 0, 0))
        if n in ("w_ff1", "w_ff2"):
            return g
        if n in COL_SPLIT:
            return g.transpose(1, 0, 2).reshape(g.shape[1], N_CHIPS * g.shape[2])
        return g.reshape(N_CHIPS * g.shape[1], g.shape[2])

    early = [n for n in names if n.startswith("mla_")]
    local = {n: W[n].astype(MXU_DTYPE).reshape(_view2d(n)) for n in early}
    wts = {n: whole(n, g, local[n]) for n, g in zip(early, _weight_gather([local[n] for n in early]))}

    nbq, nbo = BIASES["swa_b_qkv"] // N_CHIPS, BIASES["swa_b_o"] // N_CHIPS
    first = jnp.concatenate([c.reshape(-1), swa_b_qkv.reshape(-1), swa_b_o.reshape(-1),
                             jnp.zeros((FIRST_ROWS * LANES - D - nbq - nbo,), jnp.float32)]).reshape(FIRST_ROWS, LANES)
    first_all = _all_gather(first).reshape(N_DEV, FIRST_ROWS * LANES)
    c_all = first_all[:, :D]
    south = first_all[0::2]
    wts["swa_b_qkv"] = south[:, D:D + nbq].reshape(1, N_CHIPS * nbq)
    wts["swa_b_o"] = south[:, D + nbq:D + nbq + nbo].reshape(1, N_CHIPS * nbo)
    cond_all, part = _ada_part(c_all, w_ada)
    ncol = w_ada.shape[2]
    part_all = _all_gather(part.reshape(-1, LANES)).reshape(N_DEV, DEPTH, N_DEV, ncol)
    mine = _one_hot_pick(part_all[0::2], dev, axis=2)
    mod = mine.transpose(1, 0, 2).reshape(DEPTH, N_CHIPS * ncol) + b_ada
    vecs = jnp.concatenate([mod.reshape(DEPTH, 6, D), g_mix[:, None, :], g_mlp[:, None, :]], axis=1)

    late = [("w_ff1", 0), ("w_ff2", 0), ("swa_w_qkv", None), ("swa_w_o", None), ("w_ff1", 1), ("w_ff2", 1)]
    late_local = [(W[n][0] if l is None else W[n][l]).astype(MXU_DTYPE) for n, l in late]
    send_sems, recv_sems, passed, lands, token = _late_gather_start(late_local, [vecs] + [wts[n] for n in early])

    def late_weights(after):
        got = _late_gather_wait(send_sems, recv_sems, passed, lands, after)
        out = {"w_ff1": [None] * DEPTH, "w_ff2": [None] * DEPTH}
        for (n, l), g, own in zip(late, got, late_local):
            if l is None:
                out[n] = whole(n, g, own)
            else:
                out[n][l] = whole(n, g, own)
        return out

    late_names = [n for n in names if n not in early]
    reduce_state = {}

    def on_late_grads(late_grads):
        s_sems, r_sems, passed_g, zones, tok = _pair_in_start([late_grads[n] for n in late_names])
        reduce_state.update(pair=(s_sems, r_sems, passed_g, zones))
        return tok

    def on_late_landed(after):
        gl, got = _pair_in_wait(*reduce_state["pair"], after)
        sums = [_pair_sum(g, s, core_arr, "pair_sum_" + n) for n, g, s in zip(late_names, gl, got)]
        s_sems, r_sems, parts, zones, tok = _exchange_start(sums, "grad_exchange_start")
        reduce_state.update(pairs=(gl, got), split=(s_sems, r_sems, parts, zones))
        return tok

    grad_x, grads, small = _sequence_step(
        x[0], loss_target[0], positions[0], vecs, mla_g_q + token[0, 0], mla_g_kv, swa_sinks, g_final, wts,
        late_weights, on_late_grads, on_late_landed)

    small["b_ada"] = small.pop("dmod")
    small_all = _all_gather(_pack_small(small))
    pk = lambda src: _pack_small({n: src[n] for n in SMALL if n != "loss" and n not in BIASES})
    off, n = _small_slots()["b_ada"]
    dmod_all = small_all.reshape(N_DEV, -1)[:, off:off + n].reshape(N_DEV, DEPTH, N_CHIPS, ncol)
    dm = _one_hot_pick(dmod_all, chip, axis=2).transpose(1, 0, 2)

    def chip_sums(tensor_names, gl, got, others):
        return [_chip_sum(g, s, o, core_arr, chip_arr, "chip_sum_" + n) for n, g, s, o in zip(tensor_names, gl, got, others)]

    def adamw(tensor_names, mine, sibling):
        return {n: _adamw_halves(W[n], a, b, M[n], V[n], core_arr, "adamw_" + n)
                for n, a, b in zip(tensor_names, mine, sibling)}

    late_others = _exchange_wait(*reduce_state["split"], grad_x, "grad_exchange_wait")
    p_sems, p_rems, p_halves, p_zones, p_tok = _pair_out_start(chip_sums(late_names, *reduce_state["pairs"], late_others))
    gl = [grads[n] for n in early]
    got = _grad_pair_in(gl, p_tok)
    sums = [_pair_sum(g, s, core_arr, "pair_sum_" + n) for n, g, s in zip(early, gl, got)]
    e_sems, e_rems, e_parts, e_zones, e_tok = _exchange_start(sums, "mla_exchange_start")
    res = adamw(late_names, *_pair_out_wait(p_sems, p_rems, p_halves, p_zones, e_tok))
    res["w_ada"] = _ada_grad_adamw(cond_all.T, dm, w_ada, m_w_ada, v_w_ada)
    small_res = _small_adamw(small_all, pk(W), pk(M), pk(V))
    early_others = _exchange_wait(e_sems, e_rems, e_parts, e_zones, res["w_ff2"][1], "mla_exchange_wait")
    early_halves = chip_sums(early, gl, got, early_others)
    res.update(adamw(early, early_halves, _grad_pair_out(early_halves)))

    for n, width in BIASES.items():
        g = _one_hot_pick(small_res[n][0].reshape(N_CHIPS, width // N_CHIPS), chip, axis=0).reshape(1, -1)
        res[n] = [g] + _adamw(W[n], g, M[n], V[n], "adamw_" + n)
    for name in order:
        if name not in res:
            res[name] = small_res[name]
    outs = [small_res["loss"][0], grad_x[None]]
    for k in range(4):
        outs += [res[name][k] for name in order]
    return tuple(outs)
```
